```python
import math
import jax, jax.numpy as jnp
from jax import lax
import numpy as np

D_MODEL = 1024
BATCH = 8
SEQ = 16384
DEPTH = 2

HEAD_DIM = 64
CONV_CH = 256
CONV_GROUPS = CONV_CH // HEAD_DIM
CONV_WIDTH = 31
ATTN_HEADS = 8
ATTN_CH = ATTN_HEADS * HEAD_DIM
DILATED_PATTERNS = ((128, 1), (512, 4), (2048, 16))
ATTN_BLOCK = 128
N_BUCKETS = 32
MAX_DISTANCE = 2048
GMLP_CH = 256
GMLP_GROUPS = 4
GMLP_GROUP_DIM = GMLP_CH // GMLP_GROUPS
CHUNK = 128
MIX_CH = CONV_CH + ATTN_CH + GMLP_CH
IN_CH = 2 * CONV_CH + 3 * ATTN_CH + 2 * GMLP_CH
D_FF = 2816
FFN_CONV_WIDTH = 3
LN_EPS = 1e-5
ALPHA = (2.0 * DEPTH) ** 0.25
BETA = (8.0 * DEPTH) ** -0.25

kernel_name = "hymba_style_conv_dilattn_gmlp_convffn_deepnorm"


def _layernorm(x, g, b):
    xf = x.astype(jnp.float32)
    mu = jnp.mean(xf, axis=-1, keepdims=True)
    var = jnp.mean(jnp.square(xf - mu), axis=-1, keepdims=True)
    y = (xf - mu) * lax.rsqrt(var + LN_EPS)
    return (y * g.astype(jnp.float32) + b.astype(jnp.float32)).astype(x.dtype)


def _causal_dwconv(x, w, b):
    K, C = w.shape
    y = lax.conv_general_dilated(x, w[:, None, :].astype(x.dtype), window_strides=(1,),
                                 padding=((K - 1, 0),), dimension_numbers=("NWC", "WIO", "NWC"),
                                 feature_group_count=C)
    return y + b.astype(x.dtype)


def _t5_bucket(dist):
    max_exact = N_BUCKETS // 2
    d = np.maximum(dist, 1).astype(np.float64)
    large = max_exact + (np.log(d / max_exact) / math.log(MAX_DISTANCE / max_exact)
                         * (N_BUCKETS - max_exact)).astype(np.int32)
    large = np.minimum(large, N_BUCKETS - 1)
    return np.where(dist < max_exact, dist, large).astype(np.int32)


def _conv_module(a_in, dw_w, dw_b, ln_g, ln_b):
    a, gate = jnp.split(a_in, 2, axis=-1)
    h = a * jax.nn.sigmoid(gate)
    h = _causal_dwconv(h, dw_w, dw_b)
    h = _layernorm(h, ln_g, ln_b)
    return jax.nn.silu(h)


def _dilated_branch(q, k, v, rel_table, window, dilation):
    B, S, H, Dh = q.shape
    L = S // dilation
    n_win = window // dilation
    nb = -(-L // ATTN_BLOCK)
    Lp = nb * ATTN_BLOCK

    def to_blocks(t):
        t = t.reshape(B, L, dilation, H, Dh).transpose(0, 2, 1, 3, 4)
        t = jnp.pad(t, ((0, 0), (0, 0), (0, Lp - L), (0, 0), (0, 0)))
        return t.reshape(B, dilation, nb, ATTN_BLOCK, H, Dh)

    def with_prev(t):
        prev = jnp.pad(t, ((0, 0), (0, 0), (1, 0), (0, 0), (0, 0), (0, 0)))[:, :, :-1]
        return jnp.concatenate([prev, t], axis=3)

    qb = to_blocks(q)
    kk = with_prev(to_blocks(k))
    vv = with_prev(to_blocks(v))

    qi = np.arange(ATTN_BLOCK)[:, None]
    kj = np.arange(2 * ATTN_BLOCK)[None, :]
    dist_sub = qi + ATTN_BLOCK - kj
    kpos = np.arange(nb)[:, None, None] * ATTN_BLOCK + kj[None] - ATTN_BLOCK
    valid = (dist_sub >= 0)[None] & (dist_sub <= n_win)[None] & (kpos >= 0)
    bucket = _t5_bucket(np.clip(dist_sub, 0, None) * dilation)
    bias = jnp.transpose(rel_table[bucket].astype(jnp.float32), (2, 0, 1))

    logits = jnp.einsum("brnqhd,brnkhd->brnhqk", qb, kk).astype(jnp.float32) + bias
    logits = jnp.where(jnp.asarray(valid)[None, None, :, None], logits, -jnp.inf)
    m = jnp.max(logits, axis=-1)
    p = jnp.exp(logits - m[..., None])
    s = jnp.sum(p, axis=-1)
    o = jnp.einsum("brnhqk,brnkhd->brnqhd", p, vv.astype(jnp.float32))

    def from_blocks(t):
        rest = t.shape[4:]
        t = t.reshape((B, dilation, Lp) + rest)[:, :, :L]
        t = jnp.moveaxis(t, 1, 2)
        return t.reshape((B, S) + rest)

    m = from_blocks(jnp.swapaxes(m, 3, 4))
    s = from_blocks(jnp.swapaxes(s, 3, 4))
    o = from_blocks(o)
    return m, s, o


def _dilated_attention(q, k, v, rel_table):
    q = q * (HEAD_DIM ** -0.5)
    outs = [_dilated_branch(q, k, v, rel_table, w, d) for (w, d) in DILATED_PATTERNS]
    ms = jnp.stack([r[0] for r in outs])
    big_m = jnp.max(ms, axis=0)
    wts = jnp.exp(ms - big_m)
    den = sum(wts[i] * outs[i][1] for i in range(len(outs)))
    num = sum(wts[i][..., None] * outs[i][2] for i in range(len(outs)))
    return (num / den[..., None]).astype(q.dtype)


def _spatial_gating(c_in, ln_g, ln_b, w_s, b_s):
    B, S, _ = c_in.shape
    u, v = jnp.split(c_in, 2, axis=-1)
    v = _layernorm(v, ln_g, ln_b)
    vc = v.reshape(B, S // CHUNK, CHUNK, GMLP_GROUPS, GMLP_GROUP_DIM)
    w = jnp.tril(w_s).astype(v.dtype)
    mixed = jnp.einsum("gts,bcsgd->bctgd", w, vc) + b_s.T.astype(v.dtype)[None, None, :, :, None]
    return u * mixed.reshape(B, S, GMLP_CH)


def _conv_ffn(x, w_up, b_up, conv_w, conv_b, w_down, b_down):
    h = x @ w_up + b_up
    h = _causal_dwconv(h, conv_w, conv_b)
    g, val = jnp.split(h, 2, axis=-1)
    return (jax.nn.silu(g) * val) @ w_down + b_down


def _fwd_setup_inputs(seed: int = 0) -> dict:
    key = jax.random.key(seed)
    ks = jax.random.split(key, 24)
    f32 = jnp.float32

    def nrm(k, shape, scale):
        return jax.random.normal(k, shape, f32) * scale

    x = nrm(ks[0], (BATCH, SEQ, D_MODEL), 1.0)
    w_in = nrm(ks[1], (DEPTH, D_MODEL, IN_CH), D_MODEL ** -0.5)
    v_lo = 2 * CONV_CH + 2 * ATTN_CH
    v_scale = jnp.ones((IN_CH,), f32).at[v_lo:v_lo + ATTN_CH].set(BETA)
    w_in = w_in * v_scale
    b_in = nrm(ks[2], (DEPTH, IN_CH), 0.02)
    conv_dw_w = nrm(ks[3], (DEPTH, CONV_WIDTH, CONV_CH), CONV_WIDTH ** -0.5)
    conv_dw_b = nrm(ks[4], (DEPTH, CONV_CH), 0.02)
    conv_ln_g = 1.0 + nrm(ks[5], (DEPTH, CONV_CH), 0.02)
    conv_ln_b = nrm(ks[6], (DEPTH, CONV_CH), 0.02)
    rel_bias_table = nrm(ks[7], (N_BUCKETS, ATTN_HEADS), 0.5)
    gmlp_ln_g = 1.0 + nrm(ks[8], (DEPTH, GMLP_CH), 0.02)
    gmlp_ln_b = nrm(ks[9], (DEPTH, GMLP_CH), 0.02)
    gmlp_w_s = nrm(ks[10], (DEPTH, GMLP_GROUPS, CHUNK, CHUNK), CHUNK ** -0.5)
    gmlp_b_s = 1.0 + nrm(ks[11], (DEPTH, GMLP_GROUPS, CHUNK), 0.02)
    w_out = nrm(ks[12], (DEPTH, MIX_CH, D_MODEL), MIX_CH ** -0.5 * BETA)
    b_out = nrm(ks[13], (DEPTH, D_MODEL), 0.02)
    ln1_g = 1.0 + nrm(ks[14], (DEPTH, D_MODEL), 0.02)
    ln1_b = nrm(ks[15], (DEPTH, D_MODEL), 0.02)
    ffn_w_up = nrm(ks[16], (DEPTH, D_MODEL, 2 * D_FF), D_MODEL ** -0.5)
    ffn_b_up = nrm(ks[17], (DEPTH, 2 * D_FF), 0.02)
    ffn_conv_w = nrm(ks[18], (DEPTH, FFN_CONV_WIDTH, 2 * D_FF), FFN_CONV_WIDTH ** -0.5)
    ffn_conv_b = nrm(ks[19], (DEPTH, 2 * D_FF), 0.02)
    ffn_w_down = nrm(ks[20], (DEPTH, D_FF, D_MODEL), D_FF ** -0.5 * BETA)
    ffn_b_down = nrm(ks[21], (DEPTH, D_MODEL), 0.02)
    ln2_g = 1.0 + nrm(ks[22], (DEPTH, D_MODEL), 0.02)
    ln2_b = nrm(ks[23], (DEPTH, D_MODEL), 0.02)
    return {"x": x, "w_in": w_in, "b_in": b_in, "conv_dw_w": conv_dw_w, "conv_dw_b": conv_dw_b,
            "conv_ln_g": conv_ln_g, "conv_ln_b": conv_ln_b, "rel_bias_table": rel_bias_table,
            "gmlp_ln_g": gmlp_ln_g, "gmlp_ln_b": gmlp_ln_b, "gmlp_w_s": gmlp_w_s, "gmlp_b_s": gmlp_b_s,
            "w_out": w_out, "b_out": b_out, "ln1_g": ln1_g, "ln1_b": ln1_b,
            "ffn_w_up": ffn_w_up, "ffn_b_up": ffn_b_up, "ffn_conv_w": ffn_conv_w, "ffn_conv_b": ffn_conv_b,
            "ffn_w_down": ffn_w_down, "ffn_b_down": ffn_b_down, "ln2_g": ln2_g, "ln2_b": ln2_b}


def _fwd_reference(x, w_in, b_in, conv_dw_w, conv_dw_b, conv_ln_g, conv_ln_b, rel_bias_table,
              gmlp_ln_g, gmlp_ln_b, gmlp_w_s, gmlp_b_s, w_out, b_out, ln1_g, ln1_b,
              ffn_w_up, ffn_b_up, ffn_conv_w, ffn_conv_b, ffn_w_down, ffn_b_down, ln2_g, ln2_b):
    B, S, _ = x.shape
    split_pts = [2 * CONV_CH, 2 * CONV_CH + ATTN_CH, 2 * CONV_CH + 2 * ATTN_CH,
                 2 * CONV_CH + 3 * ATTN_CH]
    for l in range(DEPTH):
        h = x @ w_in[l] + b_in[l]
        a_in, q, k, v, c_in = jnp.split(h, split_pts, axis=-1)
        conv_out = _conv_module(a_in, conv_dw_w[l], conv_dw_b[l], conv_ln_g[l], conv_ln_b[l])
        attn_out = _dilated_attention(q.reshape(B, S, ATTN_HEADS, HEAD_DIM),
                                      k.reshape(B, S, ATTN_HEADS, HEAD_DIM),
                                      v.reshape(B, S, ATTN_HEADS, HEAD_DIM),
                                      rel_bias_table).reshape(B, S, ATTN_CH)
        gm_out = _spatial_gating(c_in, gmlp_ln_g[l], gmlp_ln_b[l], gmlp_w_s[l], gmlp_b_s[l])
        mix = jnp.concatenate([conv_out, attn_out, gm_out], axis=-1) @ w_out[l] + b_out[l]
        x = _layernorm(ALPHA * x + mix, ln1_g[l], ln1_b[l])
        ffn = _conv_ffn(x, ffn_w_up[l], ffn_b_up[l], ffn_conv_w[l], ffn_conv_b[l],
                        ffn_w_down[l], ffn_b_down[l])
        x = _layernorm(ALPHA * x + ffn, ln2_g[l], ln2_b[l])
    return x


import jax as _jax
import jax.numpy as _jnp

TWIN_FORMAT = 'train_step'
FWD_PARAMS = ['x', 'w_in', 'b_in', 'conv_dw_w', 'conv_dw_b', 'conv_ln_g', 'conv_ln_b', 'rel_bias_table', 'gmlp_ln_g', 'gmlp_ln_b', 'gmlp_w_s', 'gmlp_b_s', 'w_out', 'b_out', 'ln1_g', 'ln1_b', 'ffn_w_up', 'ffn_b_up', 'ffn_conv_w', 'ffn_conv_b', 'ffn_w_down', 'ffn_b_down', 'ln2_g', 'ln2_b']
TWIN_WEIGHTS = ['w_in', 'b_in', 'conv_dw_w', 'conv_dw_b', 'conv_ln_g', 'conv_ln_b', 'rel_bias_table', 'gmlp_ln_g', 'gmlp_ln_b', 'gmlp_w_s', 'gmlp_b_s', 'w_out', 'b_out', 'ln1_g', 'ln1_b', 'ffn_w_up', 'ffn_b_up', 'ffn_conv_w', 'ffn_conv_b', 'ffn_w_down', 'ffn_b_down', 'ln2_g', 'ln2_b']
TWIN_DIFF_INPUT = 'x'
TWIN_INPUTS = ['x', 'w_in', 'b_in', 'conv_dw_w', 'conv_dw_b', 'conv_ln_g', 'conv_ln_b', 'rel_bias_table', 'gmlp_ln_g', 'gmlp_ln_b', 'gmlp_w_s', 'gmlp_b_s', 'w_out', 'b_out', 'ln1_g', 'ln1_b', 'ffn_w_up', 'ffn_b_up', 'ffn_conv_w', 'ffn_conv_b', 'ffn_w_down', 'ffn_b_down', 'ln2_g', 'ln2_b', 'loss_target', 'm_w_in', 'm_b_in', 'm_conv_dw_w', 'm_conv_dw_b', 'm_conv_ln_g', 'm_conv_ln_b', 'm_rel_bias_table', 'm_gmlp_ln_g', 'm_gmlp_ln_b', 'm_gmlp_w_s', 'm_gmlp_b_s', 'm_w_out', 'm_b_out', 'm_ln1_g', 'm_ln1_b', 'm_ffn_w_up', 'm_ffn_b_up', 'm_ffn_conv_w', 'm_ffn_conv_b', 'm_ffn_w_down', 'm_ffn_b_down', 'm_ln2_g', 'm_ln2_b', 'v_w_in', 'v_b_in', 'v_conv_dw_w', 'v_conv_dw_b', 'v_conv_ln_g', 'v_conv_ln_b', 'v_rel_bias_table', 'v_gmlp_ln_g', 'v_gmlp_ln_b', 'v_gmlp_w_s', 'v_gmlp_b_s', 'v_w_out', 'v_b_out', 'v_ln1_g', 'v_ln1_b', 'v_ffn_w_up', 'v_ffn_b_up', 'v_ffn_conv_w', 'v_ffn_conv_b', 'v_ffn_w_down', 'v_ffn_b_down', 'v_ln2_g', 'v_ln2_b']
TWIN_OUTPUTS = ['loss', 'grad_x', 'grad_w_in', 'grad_b_in', 'grad_conv_dw_w', 'grad_conv_dw_b', 'grad_conv_ln_g', 'grad_conv_ln_b', 'grad_rel_bias_table', 'grad_gmlp_ln_g', 'grad_gmlp_ln_b', 'grad_gmlp_w_s', 'grad_gmlp_b_s', 'grad_w_out', 'grad_b_out', 'grad_ln1_g', 'grad_ln1_b', 'grad_ffn_w_up', 'grad_ffn_b_up', 'grad_ffn_conv_w', 'grad_ffn_conv_b', 'grad_ffn_w_down', 'grad_ffn_b_down', 'grad_ln2_g', 'grad_ln2_b', 'delta_w_in', 'delta_b_in', 'delta_conv_dw_w', 'delta_conv_dw_b', 'delta_conv_ln_g', 'delta_conv_ln_b', 'delta_rel_bias_table', 'delta_gmlp_ln_g', 'delta_gmlp_ln_b', 'delta_gmlp_w_s', 'delta_gmlp_b_s', 'delta_w_out', 'delta_b_out', 'delta_ln1_g', 'delta_ln1_b', 'delta_ffn_w_up', 'delta_ffn_b_up', 'delta_ffn_conv_w', 'delta_ffn_conv_b', 'delta_ffn_w_down', 'delta_ffn_b_down', 'delta_ln2_g', 'delta_ln2_b', 'new_m_w_in', 'new_m_b_in', 'new_m_conv_dw_w', 'new_m_conv_dw_b', 'new_m_conv_ln_g', 'new_m_conv_ln_b', 'new_m_rel_bias_table', 'new_m_gmlp_ln_g', 'new_m_gmlp_ln_b', 'new_m_gmlp_w_s', 'new_m_gmlp_b_s', 'new_m_w_out', 'new_m_b_out', 'new_m_ln1_g', 'new_m_ln1_b', 'new_m_ffn_w_up', 'new_m_ffn_b_up', 'new_m_ffn_conv_w', 'new_m_ffn_conv_b', 'new_m_ffn_w_down', 'new_m_ffn_b_down', 'new_m_ln2_g', 'new_m_ln2_b', 'new_v_w_in', 'new_v_b_in', 'new_v_conv_dw_w', 'new_v_conv_dw_b', 'new_v_conv_ln_g', 'new_v_conv_ln_b', 'new_v_rel_bias_table', 'new_v_gmlp_ln_g', 'new_v_gmlp_ln_b', 'new_v_gmlp_w_s', 'new_v_gmlp_b_s', 'new_v_w_out', 'new_v_b_out', 'new_v_ln1_g', 'new_v_ln1_b', 'new_v_ffn_w_up', 'new_v_ffn_b_up', 'new_v_ffn_conv_w', 'new_v_ffn_conv_b', 'new_v_ffn_w_down', 'new_v_ffn_b_down', 'new_v_ln2_g', 'new_v_ln2_b']
TWIN_LEAF_KINDS = {'loss': 'loss', 'grad_x': 'grad_x', 'grad_w_in': 'grad_w', 'grad_b_in': 'grad_w', 'grad_conv_dw_w': 'grad_w', 'grad_conv_dw_b': 'grad_w', 'grad_conv_ln_g': 'grad_w', 'grad_conv_ln_b': 'grad_w', 'grad_rel_bias_table': 'grad_w', 'grad_gmlp_ln_g': 'grad_w', 'grad_gmlp_ln_b': 'grad_w', 'grad_gmlp_w_s': 'grad_w', 'grad_gmlp_b_s': 'grad_w', 'grad_w_out': 'grad_w', 'grad_b_out': 'grad_w', 'grad_ln1_g': 'grad_w', 'grad_ln1_b': 'grad_w', 'grad_ffn_w_up': 'grad_w', 'grad_ffn_b_up': 'grad_w', 'grad_ffn_conv_w': 'grad_w', 'grad_ffn_conv_b': 'grad_w', 'grad_ffn_w_down': 'grad_w', 'grad_ffn_b_down': 'grad_w', 'grad_ln2_g': 'grad_w', 'grad_ln2_b': 'grad_w', 'delta_w_in': 'delta_w', 'delta_b_in': 'delta_w', 'delta_conv_dw_w': 'delta_w', 'delta_conv_dw_b': 'delta_w', 'delta_conv_ln_g': 'delta_w', 'delta_conv_ln_b': 'delta_w', 'delta_rel_bias_table': 'delta_w', 'delta_gmlp_ln_g': 'delta_w', 'delta_gmlp_ln_b': 'delta_w', 'delta_gmlp_w_s': 'delta_w', 'delta_gmlp_b_s': 'delta_w', 'delta_w_out': 'delta_w', 'delta_b_out': 'delta_w', 'delta_ln1_g': 'delta_w', 'delta_ln1_b': 'delta_w', 'delta_ffn_w_up': 'delta_w', 'delta_ffn_b_up': 'delta_w', 'delta_ffn_conv_w': 'delta_w', 'delta_ffn_conv_b': 'delta_w', 'delta_ffn_w_down': 'delta_w', 'delta_ffn_b_down': 'delta_w', 'delta_ln2_g': 'delta_w', 'delta_ln2_b': 'delta_w', 'new_m_w_in': 'new_m', 'new_m_b_in': 'new_m', 'new_m_conv_dw_w': 'new_m', 'new_m_conv_dw_b': 'new_m', 'new_m_conv_ln_g': 'new_m', 'new_m_conv_ln_b': 'new_m', 'new_m_rel_bias_table': 'new_m', 'new_m_gmlp_ln_g': 'new_m', 'new_m_gmlp_ln_b': 'new_m', 'new_m_gmlp_w_s': 'new_m', 'new_m_gmlp_b_s': 'new_m', 'new_m_w_out': 'new_m', 'new_m_b_out': 'new_m', 'new_m_ln1_g': 'new_m', 'new_m_ln1_b': 'new_m', 'new_m_ffn_w_up': 'new_m', 'new_m_ffn_b_up': 'new_m', 'new_m_ffn_conv_w': 'new_m', 'new_m_ffn_conv_b': 'new_m', 'new_m_ffn_w_down': 'new_m', 'new_m_ffn_b_down': 'new_m', 'new_m_ln2_g': 'new_m', 'new_m_ln2_b': 'new_m', 'new_v_w_in': 'new_v', 'new_v_b_in': 'new_v', 'new_v_conv_dw_w': 'new_v', 'new_v_conv_dw_b': 'new_v', 'new_v_conv_ln_g': 'new_v', 'new_v_conv_ln_b': 'new_v', 'new_v_rel_bias_table': 'new_v', 'new_v_gmlp_ln_g': 'new_v', 'new_v_gmlp_ln_b': 'new_v', 'new_v_gmlp_w_s': 'new_v', 'new_v_gmlp_b_s': 'new_v', 'new_v_w_out': 'new_v', 'new_v_b_out': 'new_v', 'new_v_ln1_g': 'new_v', 'new_v_ln1_b': 'new_v', 'new_v_ffn_w_up': 'new_v', 'new_v_ffn_b_up': 'new_v', 'new_v_ffn_conv_w': 'new_v', 'new_v_ffn_conv_b': 'new_v', 'new_v_ffn_w_down': 'new_v', 'new_v_ffn_b_down': 'new_v', 'new_v_ln2_g': 'new_v', 'new_v_ln2_b': 'new_v'}


def _forward(args):
    return _fwd_reference(*[args[k] for k in FWD_PARAMS])


def _output_shape():
    def fwd():
        inp = _fwd_setup_inputs(0)
        return _fwd_reference(*[inp[k] for k in FWD_PARAMS])
    out = _jax.eval_shape(fwd)
    return out.shape, out.dtype

N_MICROBATCH = 1
ADAM_LR = 0.001
ADAM_B1 = 0.9
ADAM_B2 = 0.999
ADAM_EPS = 1e-08
ADAM_WD = 0.01
ADAM_STEP = 10
PER_EXAMPLE_BATCH_AXIS = {'x': 0, 'loss_target': 0}
SHARED_INPUTS = []
_WEIGHT_DTYPES = {'w_in': _jnp.float32, 'b_in': _jnp.float32, 'conv_dw_w': _jnp.float32, 'conv_dw_b': _jnp.float32, 'conv_ln_g': _jnp.float32, 'conv_ln_b': _jnp.float32, 'rel_bias_table': _jnp.float32, 'gmlp_ln_g': _jnp.float32, 'gmlp_ln_b': _jnp.float32, 'gmlp_w_s': _jnp.float32, 'gmlp_b_s': _jnp.float32, 'w_out': _jnp.float32, 'b_out': _jnp.float32, 'ln1_g': _jnp.float32, 'ln1_b': _jnp.float32, 'ffn_w_up': _jnp.float32, 'ffn_b_up': _jnp.float32, 'ffn_conv_w': _jnp.float32, 'ffn_conv_b': _jnp.float32, 'ffn_w_down': _jnp.float32, 'ffn_b_down': _jnp.float32, 'ln2_g': _jnp.float32, 'ln2_b': _jnp.float32}
MOMENT_SCALE = {'w_in': 6.453704e-02, 'b_in': 2.735689e-01, 'conv_dw_w': 8.144414e-02, 'conv_dw_b': 3.642641e-01, 'conv_ln_g': 1.378856e-01, 'conv_ln_b': 2.227286e-01, 'rel_bias_table': 2.498980e-02, 'gmlp_ln_g': 9.144904e-02, 'gmlp_ln_b': 8.925271e-02, 'gmlp_w_s': 6.317937e-02, 'gmlp_b_s': 9.015095e-02, 'w_out': 1.890685e-01, 'b_out': 1.104501e+00, 'ln1_g': 3.877277e+00, 'ln1_b': 1.606630e+00, 'ffn_w_up': 4.618630e-02, 'ffn_b_up': 5.724284e-02, 'ffn_conv_w': 4.788944e-02, 'ffn_conv_b': 5.795732e-02, 'ffn_w_down': 1.514325e-01, 'ffn_b_down': 1.149361e+00, 'ln2_g': 9.067664e+01, 'ln2_b': 5.680153e+00}


def _to_microbatches(a, axis):
    t = _jnp.moveaxis(a, axis, 0)
    t = t.reshape((N_MICROBATCH, t.shape[0] // N_MICROBATCH) + t.shape[1:])
    return _jnp.moveaxis(t, 1, axis + 1)


def setup_inputs(seed: int = 0) -> dict:
    inp = _fwd_setup_inputs(seed)
    key = _jax.random.fold_in(_jax.random.key(seed), 7919)
    shape, _ = _output_shape()
    out = dict(inp)
    out["loss_target"] = _jax.random.normal(_jax.random.fold_in(key, 0), shape, _jnp.float32)
    for i, name in enumerate(TWIN_WEIGHTS):
        w = inp[name].astype(_jnp.float32)
        if MOMENT_SCALE is None:
            s = _jnp.sqrt(_jnp.mean(_jnp.square(w)) + 1e-30)
        else:
            s = MOMENT_SCALE[name]
        km, kv = _jax.random.split(_jax.random.fold_in(key, i + 1))
        out[name] = w
        out["m_" + name] = s * _jax.random.normal(km, w.shape, _jnp.float32)
        out["v_" + name] = (s * s) * _jax.random.uniform(kv, w.shape, _jnp.float32, 0.5, 1.5)
    if N_MICROBATCH > 1:
        for name, axis in PER_EXAMPLE_BATCH_AXIS.items():
            out[name] = _to_microbatches(out[name], axis)
    return {'x': out['x'], 'w_in': out['w_in'], 'b_in': out['b_in'], 'conv_dw_w': out['conv_dw_w'], 'conv_dw_b': out['conv_dw_b'], 'conv_ln_g': out['conv_ln_g'], 'conv_ln_b': out['conv_ln_b'], 'rel_bias_table': out['rel_bias_table'], 'gmlp_ln_g': out['gmlp_ln_g'], 'gmlp_ln_b': out['gmlp_ln_b'], 'gmlp_w_s': out['gmlp_w_s'], 'gmlp_b_s': out['gmlp_b_s'], 'w_out': out['w_out'], 'b_out': out['b_out'], 'ln1_g': out['ln1_g'], 'ln1_b': out['ln1_b'], 'ffn_w_up': out['ffn_w_up'], 'ffn_b_up': out['ffn_b_up'], 'ffn_conv_w': out['ffn_conv_w'], 'ffn_conv_b': out['ffn_conv_b'], 'ffn_w_down': out['ffn_w_down'], 'ffn_b_down': out['ffn_b_down'], 'ln2_g': out['ln2_g'], 'ln2_b': out['ln2_b'], 'loss_target': out['loss_target'], 'm_w_in': out['m_w_in'], 'm_b_in': out['m_b_in'], 'm_conv_dw_w': out['m_conv_dw_w'], 'm_conv_dw_b': out['m_conv_dw_b'], 'm_conv_ln_g': out['m_conv_ln_g'], 'm_conv_ln_b': out['m_conv_ln_b'], 'm_rel_bias_table': out['m_rel_bias_table'], 'm_gmlp_ln_g': out['m_gmlp_ln_g'], 'm_gmlp_ln_b': out['m_gmlp_ln_b'], 'm_gmlp_w_s': out['m_gmlp_w_s'], 'm_gmlp_b_s': out['m_gmlp_b_s'], 'm_w_out': out['m_w_out'], 'm_b_out': out['m_b_out'], 'm_ln1_g': out['m_ln1_g'], 'm_ln1_b': out['m_ln1_b'], 'm_ffn_w_up': out['m_ffn_w_up'], 'm_ffn_b_up': out['m_ffn_b_up'], 'm_ffn_conv_w': out['m_ffn_conv_w'], 'm_ffn_conv_b': out['m_ffn_conv_b'], 'm_ffn_w_down': out['m_ffn_w_down'], 'm_ffn_b_down': out['m_ffn_b_down'], 'm_ln2_g': out['m_ln2_g'], 'm_ln2_b': out['m_ln2_b'], 'v_w_in': out['v_w_in'], 'v_b_in': out['v_b_in'], 'v_conv_dw_w': out['v_conv_dw_w'], 'v_conv_dw_b': out['v_conv_dw_b'], 'v_conv_ln_g': out['v_conv_ln_g'], 'v_conv_ln_b': out['v_conv_ln_b'], 'v_rel_bias_table': out['v_rel_bias_table'], 'v_gmlp_ln_g': out['v_gmlp_ln_g'], 'v_gmlp_ln_b': out['v_gmlp_ln_b'], 'v_gmlp_w_s': out['v_gmlp_w_s'], 'v_gmlp_b_s': out['v_gmlp_b_s'], 'v_w_out': out['v_w_out'], 'v_b_out': out['v_b_out'], 'v_ln1_g': out['v_ln1_g'], 'v_ln1_b': out['v_ln1_b'], 'v_ffn_w_up': out['v_ffn_w_up'], 'v_ffn_b_up': out['v_ffn_b_up'], 'v_ffn_conv_w': out['v_ffn_conv_w'], 'v_ffn_conv_b': out['v_ffn_conv_b'], 'v_ffn_w_down': out['v_ffn_w_down'], 'v_ffn_b_down': out['v_ffn_b_down'], 'v_ln2_g': out['v_ln2_g'], 'v_ln2_b': out['v_ln2_b']}


def _loss(weights, diff, rest, loss_target):
    with _jax.named_scope("forward"):
        args = {**rest, TWIN_DIFF_INPUT: diff, **{k: w.astype(_WEIGHT_DTYPES[k]) for k, w in weights.items()}}
        y = _forward(args)
    with _jax.named_scope("loss_head"):
        err = _jnp.square(y.astype(_jnp.float32) - loss_target)
        return 0.5 * _jnp.sum(_jnp.mean(err, axis=-1)) if err.ndim else 0.5 * err


def _adamw(w, g, m, v):
    m = ADAM_B1 * m + (1.0 - ADAM_B1) * g
    v = ADAM_B2 * v + (1.0 - ADAM_B2) * _jnp.square(g)
    m_hat = m / (1.0 - ADAM_B1 ** ADAM_STEP)
    v_hat = v / (1.0 - ADAM_B2 ** ADAM_STEP)
    delta = -ADAM_LR * (m_hat / (_jnp.sqrt(v_hat) + ADAM_EPS) + ADAM_WD * w)
    return delta, m, v


def reference(x, w_in, b_in, conv_dw_w, conv_dw_b, conv_ln_g, conv_ln_b, rel_bias_table, gmlp_ln_g, gmlp_ln_b, gmlp_w_s, gmlp_b_s, w_out, b_out, ln1_g, ln1_b, ffn_w_up, ffn_b_up, ffn_conv_w, ffn_conv_b, ffn_w_down, ffn_b_down, ln2_g, ln2_b, loss_target, m_w_in, m_b_in, m_conv_dw_w, m_conv_dw_b, m_conv_ln_g, m_conv_ln_b, m_rel_bias_table, m_gmlp_ln_g, m_gmlp_ln_b, m_gmlp_w_s, m_gmlp_b_s, m_w_out, m_b_out, m_ln1_g, m_ln1_b, m_ffn_w_up, m_ffn_b_up, m_ffn_conv_w, m_ffn_conv_b, m_ffn_w_down, m_ffn_b_down, m_ln2_g, m_ln2_b, v_w_in, v_b_in, v_conv_dw_w, v_conv_dw_b, v_conv_ln_g, v_conv_ln_b, v_rel_bias_table, v_gmlp_ln_g, v_gmlp_ln_b, v_gmlp_w_s, v_gmlp_b_s, v_w_out, v_b_out, v_ln1_g, v_ln1_b, v_ffn_w_up, v_ffn_b_up, v_ffn_conv_w, v_ffn_conv_b, v_ffn_w_down, v_ffn_b_down, v_ln2_g, v_ln2_b):
    given = dict(x=x, w_in=w_in, b_in=b_in, conv_dw_w=conv_dw_w, conv_dw_b=conv_dw_b, conv_ln_g=conv_ln_g, conv_ln_b=conv_ln_b, rel_bias_table=rel_bias_table, gmlp_ln_g=gmlp_ln_g, gmlp_ln_b=gmlp_ln_b, gmlp_w_s=gmlp_w_s, gmlp_b_s=gmlp_b_s, w_out=w_out, b_out=b_out, ln1_g=ln1_g, ln1_b=ln1_b, ffn_w_up=ffn_w_up, ffn_b_up=ffn_b_up, ffn_conv_w=ffn_conv_w, ffn_conv_b=ffn_conv_b, ffn_w_down=ffn_w_down, ffn_b_down=ffn_b_down, ln2_g=ln2_g, ln2_b=ln2_b, loss_target=loss_target, m_w_in=m_w_in, m_b_in=m_b_in, m_conv_dw_w=m_conv_dw_w, m_conv_dw_b=m_conv_dw_b, m_conv_ln_g=m_conv_ln_g, m_conv_ln_b=m_conv_ln_b, m_rel_bias_table=m_rel_bias_table, m_gmlp_ln_g=m_gmlp_ln_g, m_gmlp_ln_b=m_gmlp_ln_b, m_gmlp_w_s=m_gmlp_w_s, m_gmlp_b_s=m_gmlp_b_s, m_w_out=m_w_out, m_b_out=m_b_out, m_ln1_g=m_ln1_g, m_ln1_b=m_ln1_b, m_ffn_w_up=m_ffn_w_up, m_ffn_b_up=m_ffn_b_up, m_ffn_conv_w=m_ffn_conv_w, m_ffn_conv_b=m_ffn_conv_b, m_ffn_w_down=m_ffn_w_down, m_ffn_b_down=m_ffn_b_down, m_ln2_g=m_ln2_g, m_ln2_b=m_ln2_b, v_w_in=v_w_in, v_b_in=v_b_in, v_conv_dw_w=v_conv_dw_w, v_conv_dw_b=v_conv_dw_b, v_conv_ln_g=v_conv_ln_g, v_conv_ln_b=v_conv_ln_b, v_rel_bias_table=v_rel_bias_table, v_gmlp_ln_g=v_gmlp_ln_g, v_gmlp_ln_b=v_gmlp_ln_b, v_gmlp_w_s=v_gmlp_w_s, v_gmlp_b_s=v_gmlp_b_s, v_w_out=v_w_out, v_b_out=v_b_out, v_ln1_g=v_ln1_g, v_ln1_b=v_ln1_b, v_ffn_w_up=v_ffn_w_up, v_ffn_b_up=v_ffn_b_up, v_ffn_conv_w=v_ffn_conv_w, v_ffn_conv_b=v_ffn_conv_b, v_ffn_w_down=v_ffn_w_down, v_ffn_b_down=v_ffn_b_down, v_ln2_g=v_ln2_g, v_ln2_b=v_ln2_b)
    weights = {n: given[n] for n in TWIN_WEIGHTS}
    shared = {n: given[n] for n in SHARED_INPUTS}
    per_example = {n: given[n] for n in ['x']}
    grad_fn = _jax.value_and_grad(_loss, argnums=(0, 1))

    def one_microbatch(ex, loss_target):
        ex = dict(ex)
        diff = ex.pop(TWIN_DIFF_INPUT)
        return grad_fn(weights, diff, {**shared, **ex}, loss_target)

    if N_MICROBATCH == 1:
        loss, (grad_w, grad_x) = one_microbatch(per_example, given["loss_target"])
    else:
        def body(carry, xs):
            loss_sum, grad_sum = carry
            l_k, (gw_k, gx_k) = one_microbatch(xs[0], xs[1])
            with _jax.named_scope("update"):
                return (loss_sum + l_k, _jax.tree.map(_jnp.add, grad_sum, gw_k)), gx_k

        init = (_jnp.zeros((), _jnp.float32), _jax.tree.map(_jnp.zeros_like, weights))
        (loss, grad_w), grad_x = _jax.lax.scan(body, init, (per_example, given["loss_target"]))
    with _jax.named_scope("update"):
        delta_w, new_m, new_v = {}, {}, {}
        for n in TWIN_WEIGHTS:
            delta_w[n], new_m[n], new_v[n] = _adamw(weights[n], grad_w[n], given["m_" + n], given["v_" + n])
    return (loss, grad_x, *[grad_w[n] for n in TWIN_WEIGHTS], *[delta_w[n] for n in TWIN_WEIGHTS],
            *[new_m[n] for n in TWIN_WEIGHTS], *[new_v[n] for n in TWIN_WEIGHTS])
```

```python
import functools
import math

import numpy as np
import jax
import jax.numpy as jnp
from jax import lax
from jax.experimental import pallas as pl
from jax.experimental.pallas import tpu as pltpu

f32 = jnp.float32
bf16 = jnp.bfloat16

D_MODEL = 1024
DEPTH = 2
HEAD_DIM = 64
CONV_CH = 256
CONV_WIDTH = 31
ATTN_HEADS = 8
ATTN_CH = ATTN_HEADS * HEAD_DIM
DILATIONS = (1, 4, 16)
ATTN_BLOCK = 128
N_BUCKETS = 32
MAX_DISTANCE = 2048
GMLP_CH = 256
GMLP_GROUPS = 4
GMLP_GROUP_DIM = GMLP_CH // GMLP_GROUPS
CHUNK = 128
IN_CH = 2 * CONV_CH + 3 * ATTN_CH + 2 * GMLP_CH
D_FF = 2816
FFN_CONV_WIDTH = 3
LN_EPS = 1e-5
ALPHA = (2.0 * DEPTH) ** 0.25
ADAM_LR = 0.001
ADAM_B1 = 0.9
ADAM_B2 = 0.999
ADAM_EPS = 1e-08
ADAM_WD = 0.01
ADAM_STEP = 10

CONV_HALO = 32
FFN_HALO = 8
NEG = -1e30
MIB = 2 ** 20
NT_DIMS = (((1,), (1,)), ((), ()))
TN_DIMS = (((0,), (0,)), ((), ()))
MESH_ID = pl.DeviceIdType.MESH


def _cp(sem, vmem_mib):
    return pltpu.CompilerParams(dimension_semantics=sem, vmem_limit_bytes=vmem_mib * MIB)


def _resident(shape):
    nd = len(shape)
    return pl.BlockSpec(shape, lambda *_: (0,) * nd, pipeline_mode=pl.Buffered(1))


def _acc(shape):
    nd = len(shape)
    return pl.BlockSpec(shape, lambda *_: (0,) * nd)


def _sig(x):
    return 1.0 / (1.0 + jnp.exp(-x))


def _ln_stats(z):
    mu = jnp.mean(z, axis=-1, keepdims=True)
    zc = z - mu
    var = jnp.mean(zc * zc, axis=-1, keepdims=True)
    rstd = lax.rsqrt(var + LN_EPS)
    return zc * rstd, rstd


def _ln_bwd(dy, xhat, rstd, g):
    dxh = dy * g
    m1 = jnp.mean(dxh, axis=-1, keepdims=True)
    m2 = jnp.mean(dxh * xhat, axis=-1, keepdims=True)
    return rstd * (dxh - m1 - xhat * m2)


def _colsum(x):
    return jnp.sum(x, axis=0, keepdims=True)


def _t5_bucket_np(dist):
    max_exact = N_BUCKETS // 2
    dd = np.maximum(dist, 1).astype(np.float64)
    large = max_exact + (np.log(dd / max_exact) / math.log(MAX_DISTANCE / max_exact)
                         * (N_BUCKETS - max_exact)).astype(np.int32)
    large = np.minimum(large, N_BUCKETS - 1)
    return np.where(dist < max_exact, dist, large).astype(np.int32)


def _bucket_ids():
    qi = np.arange(ATTN_BLOCK)[:, None]
    kj = np.arange(2 * ATTN_BLOCK)[None, :]
    dist = np.clip(qi + ATTN_BLOCK - kj, 0, None)
    return np.stack([_t5_bucket_np(dist * d) for d in DILATIONS]).astype(np.int32)


def _inproj_fwd(x, w, b):
    S = x.shape[0]
    T = 512

    def body(x_ref, w_ref, b_ref, a_ref, qkv_ref, c_ref):
        h = jnp.dot(x_ref[...].astype(bf16), w_ref[...], preferred_element_type=f32) + b_ref[...]
        a_ref[...] = h[:, :2 * CONV_CH]
        q0 = 2 * CONV_CH
        qkv_ref[:, :ATTN_CH] = (h[:, q0:q0 + ATTN_CH] * (HEAD_DIM ** -0.5)).astype(bf16)
        qkv_ref[:, ATTN_CH:] = h[:, q0 + ATTN_CH:q0 + 3 * ATTN_CH].astype(bf16)
        c_ref[...] = h[:, q0 + 3 * ATTN_CH:]

    return pl.pallas_call(
        body, grid=(S // T,), name="inproj_fwd",
        out_shape=(jax.ShapeDtypeStruct((S, 2 * CONV_CH), f32), jax.ShapeDtypeStruct((S, 3 * ATTN_CH), bf16),
                   jax.ShapeDtypeStruct((S, 2 * GMLP_CH), f32)),
        in_specs=[pl.BlockSpec((T, D_MODEL), lambda i: (i, 0)), _resident((D_MODEL, IN_CH)), _resident((1, IN_CH))],
        out_specs=(pl.BlockSpec((T, 2 * CONV_CH), lambda i: (i, 0)), pl.BlockSpec((T, 3 * ATTN_CH), lambda i: (i, 0)),
                   pl.BlockSpec((T, 2 * GMLP_CH), lambda i: (i, 0))),
        compiler_params=_cp(("parallel",), 48),
    )(x, w, b)


def _conv_fwd(a_in, dw_w, dw_b, ln_g, ln_b):
    S = a_in.shape[0]
    T = 512
    hb = T // CONV_HALO

    def body(a_ref, halo_ref, w_ref, b_ref, g_ref, be_ref, out_ref, hc_ref, buf):
        i = pl.program_id(0)
        am = a_ref[...]
        ah = halo_ref[...]
        hgh = ah[:, :CONV_CH] * _sig(ah[:, CONV_CH:])
        buf[0:CONV_HALO, :] = jnp.where(i > 0, hgh, 0.0)
        buf[CONV_HALO:, :] = am[:, :CONV_CH] * _sig(am[:, CONV_CH:])
        acc = jnp.zeros((T, CONV_CH), f32) + b_ref[...]
        for k in range(CONV_WIDTH):
            acc = acc + w_ref[k:k + 1, :] * buf[pl.ds(CONV_HALO - (CONV_WIDTH - 1) + k, T), :]
        hc_ref[...] = acc
        xhat, _ = _ln_stats(acc)
        y = xhat * g_ref[...] + be_ref[...]
        out_ref[...] = (y * _sig(y)).astype(bf16)

    return pl.pallas_call(
        body, grid=(S // T,), name="conv_fwd",
        out_shape=(jax.ShapeDtypeStruct((S, CONV_CH), bf16), jax.ShapeDtypeStruct((S, CONV_CH), f32)),
        in_specs=[pl.BlockSpec((T, 2 * CONV_CH), lambda i: (i, 0)),
                  pl.BlockSpec((CONV_HALO, 2 * CONV_CH), lambda i: (jnp.maximum(i * hb - 1, 0), 0)),
                  _acc((32, CONV_CH)), _acc((1, CONV_CH)), _acc((1, CONV_CH)), _acc((1, CONV_CH))],
        out_specs=(pl.BlockSpec((T, CONV_CH), lambda i: (i, 0)), pl.BlockSpec((T, CONV_CH), lambda i: (i, 0))),
        scratch_shapes=[pltpu.VMEM((T + CONV_HALO, CONV_CH), f32)],
        compiler_params=_cp(("parallel",), 32),
    )(a_in, a_in, dw_w, dw_b, ln_g, ln_b)


def _bias_build(table, buckets):
    def body(t_ref, bk_ref, o_ref):
        h = pl.program_id(1)
        ids = bk_ref[0]
        acc = jnp.zeros((ATTN_BLOCK, 2 * ATTN_BLOCK), f32)
        for b in range(N_BUCKETS):
            acc = jnp.where(ids == b, t_ref[b, h], acc)
        o_ref[0, 0] = acc

    return pl.pallas_call(
        body, grid=(len(DILATIONS), ATTN_HEADS), name="bias_build",
        out_shape=jax.ShapeDtypeStruct((len(DILATIONS), ATTN_HEADS, ATTN_BLOCK, 2 * ATTN_BLOCK), f32),
        in_specs=[pl.BlockSpec(memory_space=pltpu.SMEM),
                  pl.BlockSpec((1, ATTN_BLOCK, 2 * ATTN_BLOCK), lambda p, h: (p, 0, 0))],
        out_specs=pl.BlockSpec((1, 1, ATTN_BLOCK, 2 * ATTN_BLOCK), lambda p, h: (p, h, 0, 0)),
        compiler_params=_cp(("arbitrary", "arbitrary"), 16),
    )(table, buckets)


def _head_tile(tile, h, col):
    lane_head = lax.broadcasted_iota(jnp.int32, tile.shape, 1) // 16
    return jnp.where(lane_head == h, col, tile)


def _attn_fwd_pattern(qkv, bias, d, run, last):
    S = qkv.shape[0]
    L = S // d
    nb = L // ATTN_BLOCK
    B = ATTN_BLOCK
    first = run is None
    qv = qkv.reshape(L, d * 3 * ATTN_CH)

    def body(*refs):
        q_ref, kc_ref, vc_ref, kp_ref, vp_ref, b_ref = refs[:6]
        if first:
            outs = refs[6:]
        else:
            m_in, s_in, o_in = refs[6:9]
            outs = refs[9:]
        n = pl.program_id(1)
        row = lax.broadcasted_iota(jnp.int32, (B, 2 * B), 0)
        col = lax.broadcasted_iota(jnp.int32, (B, 2 * B), 1)
        valid = (col >= row) & (col <= row + B) & ((col >= B) | (n > 0))
        q = q_ref[...]
        kk = jnp.concatenate([kp_ref[...], kc_ref[...]], axis=0)
        vv = jnp.concatenate([vp_ref[...], vc_ref[...]], axis=0)
        m_tile = jnp.zeros((B, B), f32)
        s_tile = jnp.zeros((B, B), f32)
        for h in range(ATTN_HEADS):
            sl = slice(HEAD_DIM * h, HEAD_DIM * (h + 1))
            logits = lax.dot_general(q[:, sl], kk[:, sl], NT_DIMS, preferred_element_type=f32) + b_ref[h]
            logits = jnp.where(valid, logits, NEG)
            m_h = jnp.max(logits, axis=1, keepdims=True)
            p = jnp.exp(logits - m_h)
            s_h = jnp.sum(p, axis=1, keepdims=True)
            o_h = jnp.dot(p.astype(bf16), vv[:, sl], preferred_element_type=f32)
            if not first:
                m_r = m_in[:, 16 * h:16 * h + 1]
                s_r = s_in[:, 16 * h:16 * h + 1]
                big = jnp.maximum(m_r, m_h)
                wa = jnp.exp(m_r - big)
                wb = jnp.exp(m_h - big)
                s_h = wa * s_r + wb * s_h
                o_h = wa * o_in[:, sl] + wb * o_h
                m_h = big
            if last:
                outs[0][:, sl] = (o_h / s_h).astype(bf16)
                m_tile = _head_tile(m_tile, h, m_h + jnp.log(s_h))
            else:
                outs[2][:, sl] = o_h
                m_tile = _head_tile(m_tile, h, m_h)
                s_tile = _head_tile(s_tile, h, s_h)
        if last:
            outs[1][...] = m_tile
        else:
            outs[0][...] = m_tile
            outs[1][...] = s_tile

    def qspec(part, prev):
        if prev:
            return pl.BlockSpec((B, ATTN_CH), lambda r, n: (jnp.maximum(n - 1, 0), 3 * r + part))
        return pl.BlockSpec((B, ATTN_CH), lambda r, n: (n, 3 * r + part))

    stat_spec = pl.BlockSpec((B, B), lambda r, n: (n, r))
    o_spec = pl.BlockSpec((B, ATTN_CH), lambda r, n: (n, r))
    in_specs = [qspec(0, False), qspec(1, False), qspec(2, False), qspec(1, True), qspec(2, True),
                _resident((ATTN_HEADS, B, 2 * B))]
    args = [qv, qv, qv, qv, qv, bias]
    if not first:
        in_specs += [stat_spec, stat_spec, o_spec]
        args += [run[0].reshape(L, d * B), run[1].reshape(L, d * B), run[2].reshape(L, d * ATTN_CH)]
    if last:
        out_shape = (jax.ShapeDtypeStruct((L, d * ATTN_CH), bf16), jax.ShapeDtypeStruct((L, d * B), f32))
        out_specs = (o_spec, stat_spec)
    else:
        out_shape = (jax.ShapeDtypeStruct((L, d * B), f32), jax.ShapeDtypeStruct((L, d * B), f32),
                     jax.ShapeDtypeStruct((L, d * ATTN_CH), f32))
        out_specs = (stat_spec, stat_spec, o_spec)
    outs = pl.pallas_call(
        body, grid=(d, nb), name=f"attn_fwd_d{d}", out_shape=out_shape, in_specs=in_specs, out_specs=out_specs,
        compiler_params=_cp(("parallel", "parallel"), 32),
    )(*args)
    if last:
        return outs[0].reshape(S, ATTN_CH), outs[1].reshape(S, B)
    return outs[0].reshape(S, B), outs[1].reshape(S, B), outs[2].reshape(S, ATTN_CH)


def _attn_fwd(qkv, bias):
    run = None
    for p, d in enumerate(DILATIONS):
        run = _attn_fwd_pattern(qkv, bias[p], d, run, last=(p == len(DILATIONS) - 1))
    return run


def _tril_bf16(w):
    row = lax.broadcasted_iota(jnp.int32, (CHUNK, CHUNK), 0)
    col = lax.broadcasted_iota(jnp.int32, (CHUNK, CHUNK), 1)
    return jnp.where(col <= row, w, 0.0).astype(bf16)


def _gmlp_fwd(c_in, ln_g, ln_b, w_s, b_s_t):
    S = c_in.shape[0]
    T = 512

    def body(c_ref, g_ref, be_ref, w_ref, bs_ref, out_ref, mix):
        c = c_ref[...]
        xhat, _ = _ln_stats(c[:, GMLP_CH:])
        vb = (xhat * g_ref[...] + be_ref[...]).astype(bf16)
        for g in range(GMLP_GROUPS):
            wt = _tril_bf16(w_ref[g])
            cs = slice(GMLP_GROUP_DIM * g, GMLP_GROUP_DIM * (g + 1))
            for ci in range(T // CHUNK):
                rs = slice(CHUNK * ci, CHUNK * (ci + 1))
                mix[rs, cs] = jnp.dot(wt, vb[rs, cs], preferred_element_type=f32) + bs_ref[:, g:g + 1]
        out_ref[...] = (c[:, :GMLP_CH] * mix[...]).astype(bf16)

    return pl.pallas_call(
        body, grid=(S // T,), name="gmlp_fwd",
        out_shape=jax.ShapeDtypeStruct((S, GMLP_CH), bf16),
        in_specs=[pl.BlockSpec((T, 2 * GMLP_CH), lambda i: (i, 0)), _acc((1, GMLP_CH)), _acc((1, GMLP_CH)),
                  _acc((GMLP_GROUPS, CHUNK, CHUNK)), _acc((CHUNK, GMLP_GROUPS))],
        out_specs=pl.BlockSpec((T, GMLP_CH), lambda i: (i, 0)),
        scratch_shapes=[pltpu.VMEM((T, GMLP_CH), f32)],
        compiler_params=_cp(("parallel",), 32),
    )(c_in, ln_g, ln_b, w_s, b_s_t)


def _outproj_ln_fwd(conv_out, attn_out, gm_out, w, b, x, ln_g, ln_b):
    S = x.shape[0]
    T = 512

    def body(co_ref, ao_ref, go_ref, w_ref, b_ref, x_ref, g_ref, be_ref, cat_ref, z_ref, y_ref, yb_ref):
        cat = jnp.concatenate([co_ref[...], ao_ref[...], go_ref[...]], axis=1)
        cat_ref[...] = cat
        z = jnp.dot(cat, w_ref[...], preferred_element_type=f32) + b_ref[...] + ALPHA * x_ref[...]
        z_ref[...] = z
        xhat, _ = _ln_stats(z)
        y = xhat * g_ref[...] + be_ref[...]
        y_ref[...] = y
        yb_ref[...] = y.astype(bf16)

    row = lambda c: pl.BlockSpec((T, c), lambda i: (i, 0))
    return pl.pallas_call(
        body, grid=(S // T,), name="outproj_ln_fwd",
        out_shape=(jax.ShapeDtypeStruct((S, D_MODEL), bf16), jax.ShapeDtypeStruct((S, D_MODEL), f32),
                   jax.ShapeDtypeStruct((S, D_MODEL), f32), jax.ShapeDtypeStruct((S, D_MODEL), bf16)),
        in_specs=[row(CONV_CH), row(ATTN_CH), row(GMLP_CH), _resident((D_MODEL, D_MODEL)), _acc((1, D_MODEL)),
                  row(D_MODEL), _acc((1, D_MODEL)), _acc((1, D_MODEL))],
        out_specs=(row(D_MODEL), row(D_MODEL), row(D_MODEL), row(D_MODEL)),
        compiler_params=_cp(("parallel",), 40),
    )(conv_out, attn_out, gm_out, w, b, x, ln_g, ln_b)


def _mm_nn(a, w, b, tn, name):
    S, K = a.shape
    N = w.shape[1]
    T = 512

    def body(a_ref, w_ref, b_ref, o_ref):
        o_ref[...] = jnp.dot(a_ref[...], w_ref[...], preferred_element_type=f32) + b_ref[...]

    return pl.pallas_call(
        body, grid=(N // tn, S // T), name=name,
        out_shape=jax.ShapeDtypeStruct((S, N), f32),
        in_specs=[pl.BlockSpec((T, K), lambda j, i: (i, 0)), pl.BlockSpec((K, tn), lambda j, i: (0, j)),
                  pl.BlockSpec((1, tn), lambda j, i: (0, j))],
        out_specs=pl.BlockSpec((T, tn), lambda j, i: (i, j)),
        compiler_params=_cp(("parallel", "parallel"), 40),
    )(a, w, b)


FFN_COLS = 704


def _ffn_gate_fwd(hf, conv_w, conv_b):
    S = hf.shape[0]
    T = 256
    hb = T // FFN_HALO
    o = FFN_HALO - (FFN_CONV_WIDTH - 1)

    def body(h_ref, halo_ref, w_ref, b_ref, out_ref, buf):
        i = pl.program_id(0)
        buf[0:FFN_HALO, :] = jnp.where(i > 0, halo_ref[...], 0.0)
        buf[FFN_HALO:, :] = h_ref[...]
        for c in range(D_FF // FFN_COLS):
            def conv(c0):
                cs = slice(c0, c0 + FFN_COLS)
                acc = b_ref[:, cs] + w_ref[0:1, cs] * buf[pl.ds(o, T), cs]
                for k in range(1, FFN_CONV_WIDTH):
                    acc = acc + w_ref[k:k + 1, cs] * buf[pl.ds(o + k, T), cs]
                return acc
            g = conv(c * FFN_COLS)
            v = conv(D_FF + c * FFN_COLS)
            out_ref[:, c * FFN_COLS:(c + 1) * FFN_COLS] = (g * _sig(g) * v).astype(bf16)

    return pl.pallas_call(
        body, grid=(S // T,), name="ffn_gate_fwd",
        out_shape=jax.ShapeDtypeStruct((S, D_FF), bf16),
        in_specs=[pl.BlockSpec((T, 2 * D_FF), lambda i: (i, 0)),
                  pl.BlockSpec((FFN_HALO, 2 * D_FF), lambda i: (jnp.maximum(i * hb - 1, 0), 0)),
                  _acc((8, 2 * D_FF)), _acc((1, 2 * D_FF))],
        out_specs=pl.BlockSpec((T, D_FF), lambda i: (i, 0)),
        scratch_shapes=[pltpu.VMEM((T + FFN_HALO, 2 * D_FF), f32)],
        compiler_params=_cp(("parallel",), 48),
    )(hf, hf, conv_w, conv_b)


def _ffn_down_ln_fwd(act, w, b, x1, ln_g, ln_b):
    S = act.shape[0]
    T = 512

    def body(a_ref, w_ref, b_ref, x_ref, g_ref, be_ref, z_ref, y_ref):
        z = jnp.dot(a_ref[...], w_ref[...], preferred_element_type=f32) + b_ref[...] + ALPHA * x_ref[...]
        z_ref[...] = z
        xhat, _ = _ln_stats(z)
        y_ref[...] = xhat * g_ref[...] + be_ref[...]

    row = lambda c: pl.BlockSpec((T, c), lambda i: (i, 0))
    return pl.pallas_call(
        body, grid=(S // T,), name="ffn_down_ln_fwd",
        out_shape=(jax.ShapeDtypeStruct((S, D_MODEL), f32), jax.ShapeDtypeStruct((S, D_MODEL), f32)),
        in_specs=[row(D_FF), _resident((D_FF, D_MODEL)), _acc((1, D_MODEL)), row(D_MODEL), _acc((1, D_MODEL)),
                  _acc((1, D_MODEL))],
        out_specs=(row(D_MODEL), row(D_MODEL)),
        compiler_params=_cp(("parallel",), 40),
    )(act, w, b, x1, ln_g, ln_b)


def _loss_ln_bwd(y, target, z, ln_g):
    S = y.shape[0]
    T = 512

    def body(y_ref, t_ref, z_ref, g_ref, dz_ref, dzb_ref, loss_ref, dg_ref, db_ref):
        @pl.when(pl.program_id(0) == 0)
        def _():
            loss_ref[...] = jnp.zeros_like(loss_ref)
            dg_ref[...] = jnp.zeros_like(dg_ref)
            db_ref[...] = jnp.zeros_like(db_ref)
        err = y_ref[...] - t_ref[...]
        loss_ref[...] += _colsum(err * err) * (0.5 / D_MODEL)
        dy = err * (1.0 / D_MODEL)
        xhat, rstd = _ln_stats(z_ref[...])
        dz = _ln_bwd(dy, xhat, rstd, g_ref[...])
        dz_ref[...] = dz
        dzb_ref[...] = dz.astype(bf16)
        dg_ref[...] += _colsum(dy * xhat)
        db_ref[...] += _colsum(dy)

    row = pl.BlockSpec((T, D_MODEL), lambda i: (i, 0))
    vec = jax.ShapeDtypeStruct((1, D_MODEL), f32)
    return pl.pallas_call(
        body, grid=(S // T,), name="loss_ln_bwd",
        out_shape=(jax.ShapeDtypeStruct((S, D_MODEL), f32), jax.ShapeDtypeStruct((S, D_MODEL), bf16), vec, vec, vec),
        in_specs=[row, row, row, _acc((1, D_MODEL))],
        out_specs=(row, row, _acc((1, D_MODEL)), _acc((1, D_MODEL)), _acc((1, D_MODEL))),
        compiler_params=_cp(("arbitrary",), 40),
    )(y, target, z, ln_g)


def _dgrad_ln_bwd(g, w, dz_res, z, ln_g, name):
    S, K = g.shape
    T = 256
    with_ln = z is not None

    def body(*refs):
        if with_ln:
            g_ref, w_ref, r_ref, z_ref, lg_ref, dz_ref, dzb_ref, dg_ref, db_ref = refs
        else:
            g_ref, w_ref, r_ref, dx_ref = refs
        dx = lax.dot_general(g_ref[...], w_ref[...], NT_DIMS, preferred_element_type=f32) + ALPHA * r_ref[...]
        if not with_ln:
            dx_ref[...] = dx
            return

        @pl.when(pl.program_id(0) == 0)
        def _():
            dg_ref[...] = jnp.zeros_like(dg_ref)
            db_ref[...] = jnp.zeros_like(db_ref)
        xhat, rstd = _ln_stats(z_ref[...])
        dz = _ln_bwd(dx, xhat, rstd, lg_ref[...])
        dz_ref[...] = dz
        dzb_ref[...] = dz.astype(bf16)
        dg_ref[...] += _colsum(dx * xhat)
        db_ref[...] += _colsum(dx)

    row = pl.BlockSpec((T, D_MODEL), lambda i: (i, 0))
    vec = jax.ShapeDtypeStruct((1, D_MODEL), f32)
    in_specs = [pl.BlockSpec((T, K), lambda i: (i, 0)), _resident((D_MODEL, K)), row]
    args = [g, w, dz_res]
    if with_ln:
        in_specs += [row, _acc((1, D_MODEL))]
        args += [z, ln_g]
        out_shape = (jax.ShapeDtypeStruct((S, D_MODEL), f32), jax.ShapeDtypeStruct((S, D_MODEL), bf16), vec, vec)
        out_specs = (row, row, _acc((1, D_MODEL)), _acc((1, D_MODEL)))
    else:
        out_shape = jax.ShapeDtypeStruct((S, D_MODEL), f32)
        out_specs = row
    return pl.pallas_call(
        body, grid=(S // T,), name=name, out_shape=out_shape, in_specs=in_specs, out_specs=out_specs,
        compiler_params=_cp(("arbitrary",), 48),
    )(*args)


def _mm_nt(g, w, name):
    S, K = g.shape
    N = w.shape[0]
    T = 512

    def body(g_ref, w_ref, o_ref):
        o_ref[...] = lax.dot_general(g_ref[...], w_ref[...], NT_DIMS, preferred_element_type=f32)

    return pl.pallas_call(
        body, grid=(S // T,), name=name, out_shape=jax.ShapeDtypeStruct((S, N), f32),
        in_specs=[pl.BlockSpec((T, K), lambda i: (i, 0)), _resident((N, K))],
        out_specs=pl.BlockSpec((T, N), lambda i: (i, 0)),
        compiler_params=_cp(("parallel",), 48),
    )(g, w)


def _ffn_gate_bwd(hf, dact, conv_w, conv_b):
    S = hf.shape[0]
    T = 128
    hb = T // FFN_HALO
    nsteps = S // T
    R = T + FFN_HALO
    o = FFN_HALO - (FFN_CONV_WIDTH - 1)

    def body(hp_ref, h_ref, hn_ref, da_ref, dan_ref, w_ref, b_ref, dh_ref, dw_ref, dcb_ref, buf, dabuf, dbuf):
        i = pl.program_id(0)

        @pl.when(i == 0)
        def _():
            dw_ref[...] = jnp.zeros_like(dw_ref)
            dcb_ref[...] = jnp.zeros_like(dcb_ref)
        buf[0:FFN_HALO, :] = jnp.where(i > 0, hp_ref[...], 0.0)
        buf[FFN_HALO:FFN_HALO + T, :] = h_ref[...]
        buf[FFN_HALO + T:, :] = hn_ref[...]
        dabuf[0:T, :] = da_ref[...]
        dabuf[T:, :] = jnp.where(i < nsteps - 1, dan_ref[...], 0.0)
        for c in range(D_FF // FFN_COLS):
            gs = slice(c * FFN_COLS, (c + 1) * FFN_COLS)
            vs = slice(D_FF + c * FFN_COLS, D_FF + (c + 1) * FFN_COLS)

            def conv(cs):
                acc = b_ref[:, cs] + w_ref[0:1, cs] * buf[pl.ds(o, R), cs]
                for k in range(1, FFN_CONV_WIDTH):
                    acc = acc + w_ref[k:k + 1, cs] * buf[pl.ds(o + k, R), cs]
                return acc
            g = conv(gs)
            v = conv(vs)
            da = dabuf[:, gs]
            sg = _sig(g)
            dbuf[:, gs] = da * v * (sg * (1.0 + g * (1.0 - sg)))
            dbuf[:, vs] = da * (g * sg)
            for cs in (gs, vs):
                dm = dbuf[pl.ds(0, T), cs]
                dh = w_ref[2:3, cs] * dm + w_ref[1:2, cs] * dbuf[pl.ds(1, T), cs] + w_ref[0:1, cs] * dbuf[pl.ds(2, T), cs]
                dh_ref[:, cs] = dh.astype(bf16)
                dcb_ref[:, cs] += _colsum(dm)
                for k in range(FFN_CONV_WIDTH):
                    dw_ref[k:k + 1, cs] += _colsum(dm * buf[pl.ds(o + k, T), cs])

    return pl.pallas_call(
        body, grid=(nsteps,), name="ffn_gate_bwd",
        out_shape=(jax.ShapeDtypeStruct((S, 2 * D_FF), bf16), jax.ShapeDtypeStruct((8, 2 * D_FF), f32),
                   jax.ShapeDtypeStruct((1, 2 * D_FF), f32)),
        in_specs=[pl.BlockSpec((FFN_HALO, 2 * D_FF), lambda i: (jnp.maximum(i * hb - 1, 0), 0)),
                  pl.BlockSpec((T, 2 * D_FF), lambda i: (i, 0)),
                  pl.BlockSpec((FFN_HALO, 2 * D_FF), lambda i: (jnp.minimum((i + 1) * hb, nsteps * hb - 1), 0)),
                  pl.BlockSpec((T, D_FF), lambda i: (i, 0)),
                  pl.BlockSpec((FFN_HALO, D_FF), lambda i: (jnp.minimum((i + 1) * hb, nsteps * hb - 1), 0)),
                  _acc((8, 2 * D_FF)), _acc((1, 2 * D_FF))],
        out_specs=(pl.BlockSpec((T, 2 * D_FF), lambda i: (i, 0)), _acc((8, 2 * D_FF)), _acc((1, 2 * D_FF))),
        scratch_shapes=[pltpu.VMEM((T + 2 * FFN_HALO, 2 * D_FF), f32), pltpu.VMEM((R, D_FF), f32),
                        pltpu.VMEM((R, 2 * D_FF), f32)],
        compiler_params=_cp(("arbitrary",), 48),
    )(hf, hf, hf, dact, dact, conv_w, conv_b)


def _wgrad(a, g, tn, name):
    S, K = a.shape
    N = g.shape[1]
    T = 1024 if S % 1024 == 0 else S

    def body(a_ref, g_ref, dw_ref, db_ref):
        @pl.when(pl.program_id(1) == 0)
        def _():
            dw_ref[...] = jnp.zeros_like(dw_ref)
            db_ref[...] = jnp.zeros_like(db_ref)
        gt = g_ref[...]
        dw_ref[...] += lax.dot_general(a_ref[...].astype(bf16), gt, TN_DIMS, preferred_element_type=f32)
        db_ref[...] += _colsum(gt.astype(f32))

    return pl.pallas_call(
        body, grid=(N // tn, S // T), name=name,
        out_shape=(jax.ShapeDtypeStruct((K, N), f32), jax.ShapeDtypeStruct((1, N), f32)),
        in_specs=[pl.BlockSpec((T, K), lambda j, i: (i, 0)), pl.BlockSpec((T, tn), lambda j, i: (i, j))],
        out_specs=(pl.BlockSpec((K, tn), lambda j, i: (0, j)), pl.BlockSpec((1, tn), lambda j, i: (0, j))),
        compiler_params=_cp(("parallel", "arbitrary"), 48),
    )(a, g)


def _outproj_dgrad(dzb, w):
    S = dzb.shape[0]
    T = 512

    def body(g_ref, w_ref, dco_ref, dao_ref, dgo_ref):
        dcat = lax.dot_general(g_ref[...], w_ref[...], NT_DIMS, preferred_element_type=f32)
        dco_ref[...] = dcat[:, :CONV_CH]
        dao_ref[...] = dcat[:, CONV_CH:CONV_CH + ATTN_CH].astype(bf16)
        dgo_ref[...] = dcat[:, CONV_CH + ATTN_CH:]

    row = lambda c: pl.BlockSpec((T, c), lambda i: (i, 0))
    return pl.pallas_call(
        body, grid=(S // T,), name="outproj_dgrad",
        out_shape=(jax.ShapeDtypeStruct((S, CONV_CH), f32), jax.ShapeDtypeStruct((S, ATTN_CH), bf16),
                   jax.ShapeDtypeStruct((S, GMLP_CH), f32)),
        in_specs=[row(D_MODEL), _resident((D_MODEL, D_MODEL))],
        out_specs=(row(CONV_CH), row(ATTN_CH), row(GMLP_CH)),
        compiler_params=_cp(("parallel",), 32),
    )(dzb, w)


def _gmlp_bwd(c_in, dgm, ln_g, ln_b, w_s, b_s_t):
    S = c_in.shape[0]
    T = 512
    nsteps = S // T

    def body(c_ref, dg_ref, g_ref, be_ref, w_ref, bs_ref, dc_ref, dlg_ref, dlb_ref, dw_ref, dbs_ref,
             du_buf, dv_buf, dm_acc):
        i = pl.program_id(0)

        @pl.when(i == 0)
        def _():
            dlg_ref[...] = jnp.zeros_like(dlg_ref)
            dlb_ref[...] = jnp.zeros_like(dlb_ref)
            dw_ref[...] = jnp.zeros_like(dw_ref)
            dm_acc[...] = jnp.zeros_like(dm_acc)
        c = c_ref[...]
        u = c[:, :GMLP_CH]
        xhat, rstd = _ln_stats(c[:, GMLP_CH:])
        vb = (xhat * g_ref[...] + be_ref[...]).astype(bf16)
        dgm_t = dg_ref[...]
        dm_all = dgm_t * u
        for g in range(GMLP_GROUPS):
            wt = _tril_bf16(w_ref[g])
            cs = slice(GMLP_GROUP_DIM * g, GMLP_GROUP_DIM * (g + 1))
            dw_g = jnp.zeros((CHUNK, CHUNK), f32)
            for ci in range(T // CHUNK):
                rs = slice(CHUNK * ci, CHUNK * (ci + 1))
                v_c = vb[rs, cs]
                mixed = jnp.dot(wt, v_c, preferred_element_type=f32) + bs_ref[:, g:g + 1]
                dm = dm_all[rs, cs]
                dmb = dm.astype(bf16)
                du_buf[rs, cs] = dgm_t[rs, cs] * mixed
                dv_buf[rs, cs] = lax.dot_general(wt, dmb, TN_DIMS, preferred_element_type=f32)
                dw_g = dw_g + lax.dot_general(dmb, v_c, NT_DIMS, preferred_element_type=f32)
                dm_acc[:, cs] += dm
            dw_ref[g] += dw_g
        dv = dv_buf[...]
        dvr = _ln_bwd(dv, xhat, rstd, g_ref[...])
        dlg_ref[...] += _colsum(dv * xhat)
        dlb_ref[...] += _colsum(dv)
        dc_ref[:, :GMLP_CH] = du_buf[...].astype(bf16)
        dc_ref[:, GMLP_CH:] = dvr.astype(bf16)

        @pl.when(i == nsteps - 1)
        def _():
            row = lax.broadcasted_iota(jnp.int32, (CHUNK, CHUNK), 0)
            col = lax.broadcasted_iota(jnp.int32, (CHUNK, CHUNK), 1)
            tile = jnp.zeros((CHUNK, CHUNK), f32)
            for g in range(GMLP_GROUPS):
                dw_ref[g] = jnp.where(col <= row, dw_ref[g], 0.0)
                gsum = jnp.sum(dm_acc[:, GMLP_GROUP_DIM * g:GMLP_GROUP_DIM * (g + 1)], axis=1, keepdims=True)
                tile = jnp.where(col == g, gsum, tile)
            dbs_ref[...] = tile

    vec = jax.ShapeDtypeStruct((1, GMLP_CH), f32)
    return pl.pallas_call(
        body, grid=(nsteps,), name="gmlp_bwd",
        out_shape=(jax.ShapeDtypeStruct((S, 2 * GMLP_CH), bf16), vec, vec,
                   jax.ShapeDtypeStruct((GMLP_GROUPS, CHUNK, CHUNK), f32), jax.ShapeDtypeStruct((CHUNK, CHUNK), f32)),
        in_specs=[pl.BlockSpec((T, 2 * GMLP_CH), lambda i: (i, 0)), pl.BlockSpec((T, GMLP_CH), lambda i: (i, 0)),
                  _acc((1, GMLP_CH)), _acc((1, GMLP_CH)), _acc((GMLP_GROUPS, CHUNK, CHUNK)), _acc((CHUNK, GMLP_GROUPS))],
        out_specs=(pl.BlockSpec((T, 2 * GMLP_CH), lambda i: (i, 0)), _acc((1, GMLP_CH)), _acc((1, GMLP_CH)),
                   _acc((GMLP_GROUPS, CHUNK, CHUNK)), _acc((CHUNK, CHUNK))),
        scratch_shapes=[pltpu.VMEM((T, GMLP_CH), f32), pltpu.VMEM((T, GMLP_CH), f32), pltpu.VMEM((CHUNK, GMLP_CH), f32)],
        compiler_params=_cp(("arbitrary",), 32),
    )(c_in, dgm, ln_g, ln_b, w_s, b_s_t)


def _attn_bwd_pattern(qkv, attn_out, d_out, lse, bias, d, run, last):
    S = qkv.shape[0]
    L = S // d
    nb = L // ATTN_BLOCK
    B = ATTN_BLOCK
    first = run is None
    qv = qkv.reshape(L, d * 3 * ATTN_CH)
    ov = attn_out.reshape(L, d * ATTN_CH)
    dov = d_out.reshape(L, d * ATTN_CH)
    lv = lse.reshape(L, d * B)
    odt = bf16 if last else f32

    def body(*refs):
        (qa_ref, qb_ref, k_ref, v_ref, oa_ref, ob_ref, doa_ref, dob_ref, la_ref, lb_ref, b_ref) = refs[:11]
        if first:
            rest = refs[11:]
        else:
            dq_in, dk_in, dv_in = refs[11:14]
            rest = refs[14:]
        dq_ref, dk_ref, dv_ref, dbias_ref, carry = rest
        r = pl.program_id(0)
        j = pl.program_id(1)

        @pl.when((r == 0) & (j == 0))
        def _():
            dbias_ref[...] = jnp.zeros_like(dbias_ref)

        @pl.when(j == 0)
        def _():
            carry[...] = jnp.zeros_like(carry)
        row = lax.broadcasted_iota(jnp.int32, (B, B), 0)
        col = lax.broadcasted_iota(jnp.int32, (B, B), 1)
        valid_a = col <= row
        valid_b = (col >= row) & (j + 1 < nb)
        qa = qa_ref[...]
        qb = qb_ref[...]
        kj = k_ref[...]
        vj = v_ref[...]
        doa = doa_ref[...]
        dob = dob_ref[...]
        prod_a = doa.astype(f32) * oa_ref[...].astype(f32)
        prod_b = dob.astype(f32) * ob_ref[...].astype(f32)
        for h in range(ATTN_HEADS):
            sl = slice(HEAD_DIM * h, HEAD_DIM * (h + 1))
            k_h = kj[:, sl]
            v_h = vj[:, sl]

            def part(q_h, do_h, prod, l_ref, bias_h, valid):
                delta = jnp.sum(prod[:, sl], axis=1, keepdims=True)
                logits = lax.dot_general(q_h, k_h, NT_DIMS, preferred_element_type=f32) + bias_h
                p = jnp.where(valid, jnp.exp(logits - l_ref[:, 16 * h:16 * h + 1]), 0.0)
                dp = lax.dot_general(do_h, v_h, NT_DIMS, preferred_element_type=f32)
                ds = p * (dp - delta)
                return p.astype(bf16), ds

            pa, dsa = part(qa[:, sl], doa[:, sl], prod_a, la_ref, b_ref[h, :, B:], valid_a)
            pb, dsb = part(qb[:, sl], dob[:, sl], prod_b, lb_ref, b_ref[h, :, :B], valid_b)
            dbias_ref[h, :, B:] += dsa
            dbias_ref[h, :, :B] += dsb
            dsa_b = dsa.astype(bf16)
            dsb_b = dsb.astype(bf16)
            dv_h = (lax.dot_general(pa, doa[:, sl], TN_DIMS, preferred_element_type=f32)
                    + lax.dot_general(pb, dob[:, sl], TN_DIMS, preferred_element_type=f32))
            dk_h = (lax.dot_general(dsa_b, qa[:, sl], TN_DIMS, preferred_element_type=f32)
                    + lax.dot_general(dsb_b, qb[:, sl], TN_DIMS, preferred_element_type=f32))
            dq_h = carry[:, sl] + jnp.dot(dsa_b, k_h, preferred_element_type=f32)
            carry[:, sl] = jnp.dot(dsb_b, k_h, preferred_element_type=f32)
            if not first:
                dq_h = dq_h + dq_in[:, sl]
                dk_h = dk_h + dk_in[:, sl]
                dv_h = dv_h + dv_in[:, sl]
            if last:
                dq_h = dq_h * (HEAD_DIM ** -0.5)
            dq_ref[:, sl] = dq_h.astype(odt)
            dk_ref[:, sl] = dk_h.astype(odt)
            dv_ref[:, sl] = dv_h.astype(odt)

    def cur(part_idx, width_blocks):
        return pl.BlockSpec((B, ATTN_CH), lambda r, j: (j, width_blocks * r + part_idx))

    def nxt(part_idx, width_blocks):
        return pl.BlockSpec((B, ATTN_CH), lambda r, j: (jnp.minimum(j + 1, nb - 1), width_blocks * r + part_idx))

    l_cur = pl.BlockSpec((B, B), lambda r, j: (j, r))
    l_nxt = pl.BlockSpec((B, B), lambda r, j: (jnp.minimum(j + 1, nb - 1), r))
    in_specs = [cur(0, 3), nxt(0, 3), cur(1, 3), cur(2, 3), cur(0, 1), nxt(0, 1), cur(0, 1), nxt(0, 1), l_cur, l_nxt,
                _resident((ATTN_HEADS, B, 2 * B))]
    args = [qv, qv, qv, qv, ov, ov, dov, dov, lv, lv, bias]
    if not first:
        in_specs += [cur(0, 1)] * 3
        args += [t.reshape(L, d * ATTN_CH) for t in run]
    blk = jax.ShapeDtypeStruct((L, d * ATTN_CH), odt)
    outs = pl.pallas_call(
        body, grid=(d, nb), name=f"attn_bwd_d{d}",
        out_shape=(blk, blk, blk, jax.ShapeDtypeStruct((ATTN_HEADS, B, 2 * B), f32)),
        in_specs=in_specs,
        out_specs=(cur(0, 1), cur(0, 1), cur(0, 1), _acc((ATTN_HEADS, B, 2 * B))),
        scratch_shapes=[pltpu.VMEM((B, ATTN_CH), f32)],
        compiler_params=_cp(("arbitrary", "arbitrary"), 32),
    )(*args)
    return tuple(t.reshape(S, ATTN_CH) for t in outs[:3]), outs[3]


def _attn_bwd(qkv, attn_out, d_out, lse, bias):
    run = None
    dbias = []
    for p, d in enumerate(DILATIONS):
        run, db = _attn_bwd_pattern(qkv, attn_out, d_out, lse, bias[p], d, run, last=(p == len(DILATIONS) - 1))
        dbias.append(db)
    return run, jnp.stack(dbias)


def _bias_table_grad(dbias, buckets):
    n = dbias.shape[0]

    def body(db_ref, bk_ref, o_ref):
        p = pl.program_id(0)
        h = pl.program_id(1)

        @pl.when((p == 0) & (h == 0))
        def _():
            o_ref[...] = jnp.zeros_like(o_ref)
        ids = bk_ref[0]
        db = db_ref[0, 0]
        row = lax.broadcasted_iota(jnp.int32, (N_BUCKETS, 128), 0)
        lane = lax.broadcasted_iota(jnp.int32, (N_BUCKETS, 128), 1)
        upd = jnp.zeros((N_BUCKETS, 128), f32)
        for b in range(N_BUCKETS):
            s = jnp.sum(jnp.sum(jnp.where(ids == b, db, 0.0), axis=1, keepdims=True), axis=0, keepdims=True)
            upd = jnp.where((row == b) & (lane == h), s, upd)
        o_ref[...] += upd

    return pl.pallas_call(
        body, grid=(n, ATTN_HEADS), name="bias_table_grad",
        out_shape=jax.ShapeDtypeStruct((N_BUCKETS, 128), f32),
        in_specs=[pl.BlockSpec((1, 1, ATTN_BLOCK, 2 * ATTN_BLOCK), lambda p, h: (p, h, 0, 0)),
                  pl.BlockSpec((1, ATTN_BLOCK, 2 * ATTN_BLOCK), lambda p, h: (p, 0, 0))],
        out_specs=_acc((N_BUCKETS, 128)),
        compiler_params=_cp(("arbitrary", "arbitrary"), 16),
    )(dbias, buckets)


def _conv_bwd(a_in, hc, dco, dw_w, ln_g, ln_b):
    S = a_in.shape[0]
    T = 512
    hb = T // CONV_HALO
    nsteps = S // T
    R = T + CONV_HALO
    K = CONV_WIDTH

    def body(ap_ref, a_ref, hc_ref, hcn_ref, d_ref, dn_ref, w_ref, g_ref, be_ref,
             da_ref, dw_ref, dcb_ref, dlg_ref, dlb_ref, hg_buf, ext, dbuf):
        i = pl.program_id(0)

        @pl.when(i == 0)
        def _():
            dw_ref[...] = jnp.zeros_like(dw_ref)
            dcb_ref[...] = jnp.zeros_like(dcb_ref)
            dlg_ref[...] = jnp.zeros_like(dlg_ref)
            dlb_ref[...] = jnp.zeros_like(dlb_ref)
        am = a_ref[...]
        ah = ap_ref[...]
        a = am[:, :CONV_CH]
        sg = _sig(am[:, CONV_CH:])
        hg_buf[0:CONV_HALO, :] = jnp.where(i > 0, ah[:, :CONV_CH] * _sig(ah[:, CONV_CH:]), 0.0)
        hg_buf[CONV_HALO:, :] = a * sg
        ext[0:T, :] = hc_ref[...]
        ext[T:, :] = hcn_ref[...]
        xhat, rstd = _ln_stats(ext[...])
        hl = xhat * g_ref[...] + be_ref[...]
        ext[0:T, :] = d_ref[...]
        ext[T:, :] = dn_ref[...]
        sl_ = _sig(hl)
        dhl = ext[...] * (sl_ * (1.0 + hl * (1.0 - sl_)))
        dhc = _ln_bwd(dhl, xhat, rstd, g_ref[...])
        rowi = lax.broadcasted_iota(jnp.int32, (R, CONV_CH), 0)
        dbuf[...] = jnp.where((rowi < T) | (i < nsteps - 1), dhc, 0.0)
        dlg_ref[...] += _colsum(dhl[:T] * xhat[:T])
        dlb_ref[...] += _colsum(dhl[:T])
        dm = dbuf[pl.ds(0, T), :]
        dcb_ref[...] += _colsum(dm)
        dhg = jnp.zeros((T, CONV_CH), f32)
        for k in range(K):
            dw_ref[k:k + 1, :] += _colsum(dm * hg_buf[pl.ds(CONV_HALO - (K - 1) + k, T), :])
            dhg = dhg + w_ref[k:k + 1, :] * dbuf[pl.ds(K - 1 - k, T), :]
        da_ref[:, :CONV_CH] = (dhg * sg).astype(bf16)
        da_ref[:, CONV_CH:] = (dhg * a * sg * (1.0 - sg)).astype(bf16)

    vec = jax.ShapeDtypeStruct((1, CONV_CH), f32)
    nxt = lambda i: (jnp.minimum((i + 1) * hb, nsteps * hb - 1), 0)
    return pl.pallas_call(
        body, grid=(nsteps,), name="conv_bwd",
        out_shape=(jax.ShapeDtypeStruct((S, 2 * CONV_CH), bf16), jax.ShapeDtypeStruct((32, CONV_CH), f32), vec, vec, vec),
        in_specs=[pl.BlockSpec((CONV_HALO, 2 * CONV_CH), lambda i: (jnp.maximum(i * hb - 1, 0), 0)),
                  pl.BlockSpec((T, 2 * CONV_CH), lambda i: (i, 0)),
                  pl.BlockSpec((T, CONV_CH), lambda i: (i, 0)), pl.BlockSpec((CONV_HALO, CONV_CH), nxt),
                  pl.BlockSpec((T, CONV_CH), lambda i: (i, 0)), pl.BlockSpec((CONV_HALO, CONV_CH), nxt),
                  _acc((32, CONV_CH)), _acc((1, CONV_CH)), _acc((1, CONV_CH))],
        out_specs=(pl.BlockSpec((T, 2 * CONV_CH), lambda i: (i, 0)), _acc((32, CONV_CH)), _acc((1, CONV_CH)),
                   _acc((1, CONV_CH)), _acc((1, CONV_CH))),
        scratch_shapes=[pltpu.VMEM((T + CONV_HALO, CONV_CH), f32), pltpu.VMEM((R, CONV_CH), f32),
                        pltpu.VMEM((R, CONV_CH), f32)],
        compiler_params=_cp(("arbitrary",), 32),
    )(a_in, a_in, hc, hc, dco, dco, dw_w, ln_g, ln_b)


def _adamw(g, w, m, v, name):
    R, C = g.shape
    T = R
    for cand in (512, 256, 128, 64, 32, 16, 8):
        if R % cand == 0 and cand * C * 4 <= MIB:
            T = cand
            break
    c1 = 1.0 / (1.0 - ADAM_B1 ** ADAM_STEP)
    c2 = 1.0 / (1.0 - ADAM_B2 ** ADAM_STEP)

    def body(g_ref, w_ref, m_ref, v_ref, d_ref, nm_ref, nv_ref):
        gg = g_ref[...]
        nm = ADAM_B1 * m_ref[...] + (1.0 - ADAM_B1) * gg
        nv = ADAM_B2 * v_ref[...] + (1.0 - ADAM_B2) * (gg * gg)
        nm_ref[...] = nm
        nv_ref[...] = nv
        d_ref[...] = -ADAM_LR * ((nm * c1) / (jnp.sqrt(nv * c2) + ADAM_EPS) + ADAM_WD * w_ref[...])

    blk = pl.BlockSpec((T, C), lambda i: (i, 0))
    sd = jax.ShapeDtypeStruct((R, C), f32)
    return pl.pallas_call(
        body, grid=(R // T,), name=name, out_shape=(sd, sd, sd), in_specs=[blk] * 4, out_specs=(blk, blk, blk),
        compiler_params=_cp(("parallel",), 48),
    )(g, w, m, v)


def _pad_rows(a, rows):
    return jnp.pad(a, ((0, rows - a.shape[0]), (0, 0)))


def _local_step(x, target, wb, sp):
    buckets = jnp.asarray(_bucket_ids())
    bias = _bias_build(sp["rel_bias_table"], buckets)
    saved = []
    xl = x
    for l in range(DEPTH):
        vec = lambda name: sp[name][l][None, :]
        a_in, qkv, c_in = _inproj_fwd(xl, wb["w_in"][l], vec("b_in"))
        conv_w = _pad_rows(sp["conv_dw_w"][l], 32)
        conv_out, hc = _conv_fwd(a_in, conv_w, vec("conv_dw_b"), vec("conv_ln_g"), vec("conv_ln_b"))
        attn_out, lse = _attn_fwd(qkv, bias)
        bs_t = sp["gmlp_b_s"][l].T
        gm_out = _gmlp_fwd(c_in, vec("gmlp_ln_g"), vec("gmlp_ln_b"), sp["gmlp_w_s"][l], bs_t)
        cat, z1, x1, x1b = _outproj_ln_fwd(conv_out, attn_out, gm_out, wb["w_out"][l], vec("b_out"), xl,
                                           vec("ln1_g"), vec("ln1_b"))
        hf = _mm_nn(x1b, wb["ffn_w_up"][l], vec("ffn_b_up"), 1408, "ffn_up_fwd")
        fconv_w = _pad_rows(sp["ffn_conv_w"][l], 8)
        act = _ffn_gate_fwd(hf, fconv_w, vec("ffn_conv_b"))
        z2, x2 = _ffn_down_ln_fwd(act, wb["ffn_w_down"][l], vec("ffn_b_down"), x1, vec("ln2_g"), vec("ln2_b"))
        saved.append(dict(x=xl, a_in=a_in, qkv=qkv, c_in=c_in, hc=hc, attn_out=attn_out, lse=lse, cat=cat, z1=z1,
                          x1b=x1b, hf=hf, act=act, z2=z2, conv_w=conv_w, fconv_w=fconv_w, bs_t=bs_t))
        xl = x2

    grads = {}
    per_layer = {k: [None] * DEPTH for k in (
        "w_in", "b_in", "conv_dw_w", "conv_dw_b", "conv_ln_g", "conv_ln_b", "gmlp_ln_g", "gmlp_ln_b", "gmlp_w_s",
        "gmlp_b_s", "w_out", "b_out", "ln1_g", "ln1_b", "ffn_w_up", "ffn_b_up", "ffn_conv_w", "ffn_conv_b",
        "ffn_w_down", "ffn_b_down", "ln2_g", "ln2_b")}
    dbias_all = []
    l = DEPTH - 1
    vec = lambda name: sp[name][l][None, :]
    dz2, dz2b, loss_part, dg2, db2 = _loss_ln_bwd(xl, target, saved[l]["z2"], vec("ln2_g"))
    loss = jnp.sum(loss_part)
    grad_x = None
    for l in reversed(range(DEPTH)):
        sv = saved[l]
        vec = lambda name: sp[name][l][None, :]
        per_layer["ln2_g"][l] = dg2[0]
        per_layer["ln2_b"][l] = db2[0]
        dact = _mm_nt(dz2b, wb["ffn_w_down"][l], "ffn_down_dgrad")
        dw_down, db_down = _wgrad(sv["act"], dz2b, 512, "ffn_down_wgrad")
        per_layer["ffn_w_down"][l] = dw_down
        per_layer["ffn_b_down"][l] = db_down[0]
        dhf, dfcw, dfcb = _ffn_gate_bwd(sv["hf"], dact, sv["fconv_w"], vec("ffn_conv_b"))
        per_layer["ffn_conv_w"][l] = dfcw[:FFN_CONV_WIDTH]
        per_layer["ffn_conv_b"][l] = dfcb[0]
        dw_up, db_up = _wgrad(sv["x1b"], dhf, 1408, "ffn_up_wgrad")
        per_layer["ffn_w_up"][l] = dw_up
        per_layer["ffn_b_up"][l] = db_up[0]
        dz1, dz1b, dg1, db1 = _dgrad_ln_bwd(dhf, wb["ffn_w_up"][l], dz2, sv["z1"], vec("ln1_g"), "ffn_up_dgrad_ln")
        per_layer["ln1_g"][l] = dg1[0]
        per_layer["ln1_b"][l] = db1[0]
        dw_out, db_out = _wgrad(sv["cat"], dz1b, 512, "outproj_wgrad")
        per_layer["w_out"][l] = dw_out
        per_layer["b_out"][l] = db_out[0]
        dco, dao, dgo = _outproj_dgrad(dz1b, wb["w_out"][l])
        d_c, dglg, dglb, dws, dbs = _gmlp_bwd(sv["c_in"], dgo, vec("gmlp_ln_g"), vec("gmlp_ln_b"), sp["gmlp_w_s"][l],
                                              sv["bs_t"])
        per_layer["gmlp_ln_g"][l] = dglg[0]
        per_layer["gmlp_ln_b"][l] = dglb[0]
        per_layer["gmlp_w_s"][l] = dws
        per_layer["gmlp_b_s"][l] = dbs[:, :GMLP_GROUPS].T
        (dq, dk, dv), dbias = _attn_bwd(sv["qkv"], sv["attn_out"], dao, sv["lse"], bias)
        dbias_all.append(dbias)
        d_a, dcw, dcb, dclg, dclb = _conv_bwd(sv["a_in"], sv["hc"], dco, sv["conv_w"], vec("conv_ln_g"),
                                              vec("conv_ln_b"))
        per_layer["conv_dw_w"][l] = dcw[:CONV_WIDTH]
        per_layer["conv_dw_b"][l] = dcb[0]
        per_layer["conv_ln_g"][l] = dclg[0]
        per_layer["conv_ln_b"][l] = dclb[0]
        dh = jnp.concatenate([d_a, dq, dk, dv, d_c], axis=1)
        dw_in, db_in = _wgrad(sv["x"], dh, 640, "inproj_wgrad")
        per_layer["w_in"][l] = dw_in
        per_layer["b_in"][l] = db_in[0]
        if l > 0:
            pv = saved[l - 1]
            dz2, dz2b, dg2, db2 = _dgrad_ln_bwd(dh, wb["w_in"][l], dz1, pv["z2"], sp["ln2_g"][l - 1][None, :],
                                                "inproj_dgrad_ln")
        else:
            grad_x = _dgrad_ln_bwd(dh, wb["w_in"][l], dz1, None, None, "inproj_dgrad")
    for k, v in per_layer.items():
        grads[k] = v if k in BIG else jnp.stack(v)
    dbias_cat = jnp.concatenate(dbias_all, axis=0)
    bk_cat = jnp.concatenate([buckets] * DEPTH, axis=0)
    grads["rel_bias_table"] = _bias_table_grad(dbias_cat, bk_cat)[:, :ATTN_HEADS]
    return loss, grad_x, grads


N_CHIPS = 4
BIG = {"w_in": (D_MODEL, IN_CH, 1), "w_out": (D_MODEL, D_MODEL, 0),
       "ffn_w_up": (D_MODEL, 2 * D_FF, 1), "ffn_w_down": (D_FF, D_MODEL, 0)}
SMALL = ("b_in", "conv_dw_w", "conv_dw_b", "conv_ln_g", "conv_ln_b", "rel_bias_table", "gmlp_ln_g", "gmlp_ln_b",
         "gmlp_w_s", "gmlp_b_s", "b_out", "ln1_g", "ln1_b", "ffn_b_up", "ffn_conv_w", "ffn_conv_b", "ffn_b_down",
         "ln2_g", "ln2_b")
SMALL_SHARDED = ("conv_dw_w", "ffn_conv_w")
WEIGHTS = ("w_in", "b_in", "conv_dw_w", "conv_dw_b", "conv_ln_g", "conv_ln_b", "rel_bias_table", "gmlp_ln_g",
           "gmlp_ln_b", "gmlp_w_s", "gmlp_b_s", "w_out", "b_out", "ln1_g", "ln1_b", "ffn_w_up", "ffn_b_up",
           "ffn_conv_w", "ffn_conv_b", "ffn_w_down", "ffn_b_down", "ln2_g", "ln2_b")
ANY = pl.BlockSpec(memory_space=pl.ANY)


def _position():
    return lax.axis_index("x"), lax.axis_index("y"), lax.axis_index("c")


def _other_chips(x, y):
    return [(1 - x, y), (x, 1 - y), (1 - x, 1 - y)]


def _cast_bf16(a):
    R, C = a.shape
    T = 128

    def body(a_ref, o_ref):
        o_ref[...] = a_ref[...].astype(bf16)

    return pl.pallas_call(
        body, grid=(R // T,), name="cast_bf16", out_shape=jax.ShapeDtypeStruct((R, C), bf16),
        in_specs=[pl.BlockSpec((T, C), lambda i: (i, 0))], out_specs=pl.BlockSpec((T, C), lambda i: (i, 0)),
        compiler_params=_cp(("parallel",), 16),
    )(a)


def _chip_slot(ref, name, p):
    K, N, ax = BIG[name]
    if ax == 1:
        sz = N // N_CHIPS
        return ref.at[:, :, pl.ds(pl.multiple_of(p * sz, 128), sz)]
    sz = K // N_CHIPS
    return ref.at[:, pl.ds(pl.multiple_of(p * sz, 16), sz), :]


def _gather_weights(shards, conv_w, fconv_w):
    names = list(BIG)
    n_t = len(names) + 2

    def body(*refs):
        ins = refs[:n_t]
        outs = refs[n_t:2 * n_t]
        send_sems, recv_sems, local_sems = refs[2 * n_t:]
        x, y, c = _position()
        me = 2 * x + y
        chips = _other_chips(x, y)

        def slot(t, p):
            if t < len(names):
                return _chip_slot(outs[t], names[t], p)
            return outs[t].at[p]

        locs, cps = [], []
        for t in range(n_t):
            loc = pltpu.make_async_copy(ins[t], slot(t, me), local_sems.at[t])
            loc.start()
            locs.append(loc)
            for k, (px, py) in enumerate(chips):
                cp = pltpu.make_async_remote_copy(
                    src_ref=ins[t], dst_ref=slot(t, me), send_sem=send_sems.at[3 * t + k],
                    recv_sem=recv_sems.at[3 * t + k], device_id=(px, py, c), device_id_type=MESH_ID)
                cp.start()
                cps.append(cp)
        for t in range(n_t):
            for k, (px, py) in enumerate(chips):
                pltpu.make_async_remote_copy(
                    src_ref=ins[t], dst_ref=slot(t, 2 * px + py), send_sem=send_sems.at[3 * t + k],
                    recv_sem=recv_sems.at[3 * t + k], device_id=(px, py, c), device_id_type=MESH_ID).wait_recv()
        for cp in cps:
            cp.wait_send()
        for loc in locs:
            loc.wait()

    ins = [shards[n] for n in names] + [conv_w, fconv_w]
    out_shape = [jax.ShapeDtypeStruct((DEPTH, BIG[n][0], BIG[n][1]), bf16) for n in names]
    out_shape += [jax.ShapeDtypeStruct((N_CHIPS,) + conv_w.shape, f32), jax.ShapeDtypeStruct((N_CHIPS,) + fconv_w.shape, f32)]
    outs = pl.pallas_call(
        body, name="gather_weights", out_shape=tuple(out_shape), in_specs=[ANY] * n_t, out_specs=tuple([ANY] * n_t),
        scratch_shapes=[pltpu.SemaphoreType.DMA((3 * n_t,)), pltpu.SemaphoreType.DMA((3 * n_t,)),
                        pltpu.SemaphoreType.DMA((n_t,))],
    )(*ins)
    return dict(zip(names, outs[:len(names)])), outs[-2], outs[-1]


def _half(ref, name, c):
    K, N, ax = BIG[name]
    if ax == 1:
        return ref.at[pl.ds(pl.multiple_of(c * (K // 2), 8), K // 2), :]
    return ref.at[:, pl.ds(pl.multiple_of(c * (N // 2), 128), N // 2)]


def _half_shape(name):
    K, N, ax = BIG[name]
    return (K // 2, N) if ax == 1 else (K, N // 2)


def _shard_of_half(ref, name, q):
    K, N, ax = BIG[name]
    if ax == 1:
        sz = N // N_CHIPS
        return ref.at[:, pl.ds(pl.multiple_of(q * sz, 128), sz)]
    sz = K // N_CHIPS
    return ref.at[pl.ds(pl.multiple_of(q * sz, 8), sz), :]


def _shard_half_shape(name):
    K, N, ax = BIG[name]
    return (K // 2, N // N_CHIPS) if ax == 1 else (K // N_CHIPS, N // 2)


def _shard_shape(name):
    K, N, ax = BIG[name]
    return (K, N // N_CHIPS) if ax == 1 else (K // N_CHIPS, N)


def _place_in_shard(ref, name, l, c):
    K, N, ax = BIG[name]
    if ax == 1:
        return ref.at[l, pl.ds(pl.multiple_of(c * (K // 2), 8), K // 2), :]
    return ref.at[l, :, pl.ds(pl.multiple_of(c * (N // 2), 128), N // 2)]


def _pair_exchange(tensors):
    n_t = len(tensors)

    def body(*refs):
        ins = refs[:n_t]
        outs = refs[n_t:2 * n_t]
        send_sems, recv_sems = refs[2 * n_t:]
        x, y, c = _position()
        cps = []
        for t, (name, _) in enumerate(tensors):
            cp = pltpu.make_async_remote_copy(
                src_ref=_half(ins[t], name, 1 - c), dst_ref=outs[t], send_sem=send_sems.at[t],
                recv_sem=recv_sems.at[t], device_id=(x, y, 1 - c), device_id_type=MESH_ID)
            cp.start()
            cps.append(cp)
        for cp in cps:
            cp.wait()

    return pl.pallas_call(
        body, name="grad_pair_exchange",
        out_shape=tuple(jax.ShapeDtypeStruct(_half_shape(n), f32) for n, _ in tensors),
        in_specs=[ANY] * n_t, out_specs=tuple([ANY] * n_t),
        scratch_shapes=[pltpu.SemaphoreType.DMA((n_t,)), pltpu.SemaphoreType.DMA((n_t,))],
    )(*[g for _, g in tensors])


def _pair_add(g, rcv, name, c_arr):
    K, N, ax = BIG[name]
    hr, hc = _half_shape(name)
    T = 128
    nrt = hr // T

    def body(c_ref, g_ref, r_ref, o_ref):
        o_ref[...] = g_ref[...] + r_ref[...]

    if ax == 1:
        g_spec = pl.BlockSpec((T, hc), lambda i, c: (c[0] * nrt + i, 0))
    else:
        g_spec = pl.BlockSpec((T, hc), lambda i, c: (i, c[0]))
    plain = pl.BlockSpec((T, hc), lambda i, c: (i, 0))
    return pl.pallas_call(
        body, name="grad_pair_add", out_shape=jax.ShapeDtypeStruct((hr, hc), f32),
        grid_spec=pltpu.PrefetchScalarGridSpec(num_scalar_prefetch=1, grid=(nrt,), in_specs=[g_spec, plain],
                                               out_specs=plain),
        compiler_params=_cp(("parallel",), 32),
    )(c_arr, g, rcv)


def _chip_exchange(tensors):
    n_t = len(tensors)

    def body(*refs):
        ins = refs[:n_t]
        outs = refs[n_t:2 * n_t]
        send_sems, recv_sems, local_sems = refs[2 * n_t:]
        x, y, c = _position()
        me = 2 * x + y
        chips = _other_chips(x, y)
        locs, cps = [], []
        for t, (name, _) in enumerate(tensors):
            loc = pltpu.make_async_copy(_shard_of_half(ins[t], name, me), outs[t].at[me], local_sems.at[t])
            loc.start()
            locs.append(loc)
            for k, (px, py) in enumerate(chips):
                cp = pltpu.make_async_remote_copy(
                    src_ref=_shard_of_half(ins[t], name, 2 * px + py), dst_ref=outs[t].at[me],
                    send_sem=send_sems.at[3 * t + k], recv_sem=recv_sems.at[3 * t + k],
                    device_id=(px, py, c), device_id_type=MESH_ID)
                cp.start()
                cps.append(cp)
        for t, (name, _) in enumerate(tensors):
            for k, (px, py) in enumerate(chips):
                pltpu.make_async_remote_copy(
                    src_ref=_shard_of_half(ins[t], name, 2 * px + py), dst_ref=outs[t].at[2 * px + py],
                    send_sem=send_sems.at[3 * t + k], recv_sem=recv_sems.at[3 * t + k],
                    device_id=(px, py, c), device_id_type=MESH_ID).wait_recv()
        for cp in cps:
            cp.wait_send()
        for loc in locs:
            loc.wait()

    return pl.pallas_call(
        body, name="grad_chip_exchange",
        out_shape=tuple(jax.ShapeDtypeStruct((N_CHIPS,) + _shard_half_shape(n), f32) for n, _ in tensors),
        in_specs=[ANY] * n_t, out_specs=tuple([ANY] * n_t),
        scratch_shapes=[pltpu.SemaphoreType.DMA((3 * n_t,)), pltpu.SemaphoreType.DMA((3 * n_t,)),
                        pltpu.SemaphoreType.DMA((n_t,))],
    )(*[g for _, g in tensors])


def _sum_chips(parts):
    _, R, C = parts.shape
    T = 64

    def body(p_ref, o_ref):
        o_ref[...] = ((p_ref[0] + p_ref[1]) + p_ref[2]) + p_ref[3]

    return pl.pallas_call(
        body, grid=(R // T,), name="grad_sum_chips", out_shape=jax.ShapeDtypeStruct((R, C), f32),
        in_specs=[pl.BlockSpec((N_CHIPS, T, C), lambda i: (0, i, 0))], out_specs=pl.BlockSpec((T, C), lambda i: (i, 0)),
        compiler_params=_cp(("parallel",), 32),
    )(parts)


def _pair_gather(tensors):
    n_t = len(tensors)
    names = list(BIG)

    def body(*refs):
        ins = refs[:n_t]
        outs = dict(zip(names, refs[n_t:n_t + len(names)]))
        send_sems, recv_sems, local_sems = refs[n_t + len(names):]
        x, y, c = _position()
        locs, cps = [], []
        for t, (name, l, _) in enumerate(tensors):
            loc = pltpu.make_async_copy(ins[t], _place_in_shard(outs[name], name, l, c), local_sems.at[t])
            loc.start()
            locs.append(loc)
            cp = pltpu.make_async_remote_copy(
                src_ref=ins[t], dst_ref=_place_in_shard(outs[name], name, l, c), send_sem=send_sems.at[t],
                recv_sem=recv_sems.at[t], device_id=(x, y, 1 - c), device_id_type=MESH_ID)
            cp.start()
            cps.append(cp)
        for t, (name, l, _) in enumerate(tensors):
            pltpu.make_async_remote_copy(
                src_ref=ins[t], dst_ref=_place_in_shard(outs[name], name, l, 1 - c), send_sem=send_sems.at[t],
                recv_sem=recv_sems.at[t], device_id=(x, y, 1 - c), device_id_type=MESH_ID).wait_recv()
        for cp in cps:
            cp.wait_send()
        for loc in locs:
            loc.wait()

    outs = pl.pallas_call(
        body, name="grad_pair_gather",
        out_shape=tuple(jax.ShapeDtypeStruct((DEPTH,) + _shard_shape(n), f32) for n in names),
        in_specs=[ANY] * n_t, out_specs=tuple([ANY] * len(names)),
        scratch_shapes=[pltpu.SemaphoreType.DMA((n_t,)), pltpu.SemaphoreType.DMA((n_t,)),
                        pltpu.SemaphoreType.DMA((n_t,))],
    )(*[g for _, _, g in tensors])
    return dict(zip(names, outs))


def _reduce_big_grads(grads):
    c_arr = jnp.reshape(lax.axis_index("c"), (1,)).astype(jnp.int32)
    tensors = [(n, grads[n][l]) for n in BIG for l in range(DEPTH)]
    layers = [l for n in BIG for l in range(DEPTH)]
    received = _pair_exchange(tensors)
    pair = [(n, _pair_add(g, r, n, c_arr)) for (n, g), r in zip(tensors, received)]
    parts = _chip_exchange(pair)
    reduced = [(n, l, _sum_chips(p)) for (n, _), l, p in zip(pair, layers, parts)]
    return _pair_gather(reduced)


def _small_allreduce(buf):
    R = buf.shape[0]
    n_dev = 8

    def body(in_ref, out_ref, slots, send_sems, recv_sems):
        x, y, c = _position()
        me = 4 * x + 2 * y + c
        slots[me] = in_ref[...]
        peers = []
        for k in range(1, n_dev):
            px = 1 - x if k & 4 else x
            py = 1 - y if k & 2 else y
            pc = 1 - c if k & 1 else c
            peers.append((px, py, pc))
        cps = []
        for k, peer in enumerate(peers):
            cp = pltpu.make_async_remote_copy(
                src_ref=in_ref, dst_ref=slots.at[me], send_sem=send_sems.at[k], recv_sem=recv_sems.at[k],
                device_id=peer, device_id_type=MESH_ID)
            cp.start()
            cps.append(cp)
        for k, (px, py, pc) in enumerate(peers):
            pltpu.make_async_remote_copy(
                src_ref=in_ref, dst_ref=slots.at[4 * px + 2 * py + pc], send_sem=send_sems.at[k],
                recv_sem=recv_sems.at[k], device_id=(px, py, pc), device_id_type=MESH_ID).wait_recv()
        for cp in cps:
            cp.wait_send()
        acc = slots[0]
        for dv in range(1, n_dev):
            acc = acc + slots[dv]
        out_ref[...] = acc

    vm = pl.BlockSpec(memory_space=pltpu.VMEM)
    return pl.pallas_call(
        body, name="small_allreduce", out_shape=jax.ShapeDtypeStruct((R, 128), f32), in_specs=[vm], out_specs=vm,
        scratch_shapes=[pltpu.VMEM((n_dev, R, 128), f32), pltpu.SemaphoreType.DMA((n_dev - 1,)),
                        pltpu.SemaphoreType.DMA((n_dev - 1,))],
        compiler_params=pltpu.CompilerParams(vmem_limit_bytes=40 * MIB),
    )(buf)


PACK_UNIT = 1024


def _pack(arrs):
    parts = []
    for a in arrs:
        flat = a.reshape(-1)
        n = -(-flat.shape[0] // PACK_UNIT) * PACK_UNIT
        parts.append(jnp.pad(flat, (0, n - flat.shape[0])))
    return jnp.concatenate(parts).reshape(-1, 128)


def _unpack(buf, shapes):
    flat = buf.reshape(-1)
    out, off = [], 0
    for shp in shapes:
        n = int(np.prod(shp))
        out.append(flat[off:off + n].reshape(shp))
        off += -(-n // PACK_UNIT) * PACK_UNIT
    return out


def _adamw_rows(g, w, m, v, name):
    shp = g.shape
    C = shp[-1]
    outs = _adamw(g.reshape(-1, C), w.reshape(-1, C), m.reshape(-1, C), v.reshape(-1, C), name)
    return [o.reshape(shp) for o in outs]


def kernel(x, w_in, b_in, conv_dw_w, conv_dw_b, conv_ln_g, conv_ln_b, rel_bias_table, gmlp_ln_g, gmlp_ln_b, gmlp_w_s, gmlp_b_s, w_out, b_out, ln1_g, ln1_b, ffn_w_up, ffn_b_up, ffn_conv_w, ffn_conv_b, ffn_w_down, ffn_b_down, ln2_g, ln2_b, loss_target, m_w_in, m_b_in, m_conv_dw_w, m_conv_dw_b, m_conv_ln_g, m_conv_ln_b, m_rel_bias_table, m_gmlp_ln_g, m_gmlp_ln_b, m_gmlp_w_s, m_gmlp_b_s, m_w_out, m_b_out, m_ln1_g, m_ln1_b, m_ffn_w_up, m_ffn_b_up, m_ffn_conv_w, m_ffn_conv_b, m_ffn_w_down, m_ffn_b_down, m_ln2_g, m_ln2_b, v_w_in, v_b_in, v_conv_dw_w, v_conv_dw_b, v_conv_ln_g, v_conv_ln_b, v_rel_bias_table, v_gmlp_ln_g, v_gmlp_ln_b, v_gmlp_w_s, v_gmlp_b_s, v_w_out, v_b_out, v_ln1_g, v_ln1_b, v_ffn_w_up, v_ffn_b_up, v_ffn_conv_w, v_ffn_conv_b, v_ffn_w_down, v_ffn_b_down, v_ln2_g, v_ln2_b):
    w = dict(w_in=w_in, b_in=b_in, conv_dw_w=conv_dw_w, conv_dw_b=conv_dw_b, conv_ln_g=conv_ln_g, conv_ln_b=conv_ln_b,
             rel_bias_table=rel_bias_table, gmlp_ln_g=gmlp_ln_g, gmlp_ln_b=gmlp_ln_b, gmlp_w_s=gmlp_w_s,
             gmlp_b_s=gmlp_b_s, w_out=w_out, b_out=b_out, ln1_g=ln1_g, ln1_b=ln1_b, ffn_w_up=ffn_w_up,
             ffn_b_up=ffn_b_up, ffn_conv_w=ffn_conv_w, ffn_conv_b=ffn_conv_b, ffn_w_down=ffn_w_down,
             ffn_b_down=ffn_b_down, ln2_g=ln2_g, ln2_b=ln2_b)
    m = dict(w_in=m_w_in, b_in=m_b_in, conv_dw_w=m_conv_dw_w, conv_dw_b=m_conv_dw_b, conv_ln_g=m_conv_ln_g,
             conv_ln_b=m_conv_ln_b, rel_bias_table=m_rel_bias_table, gmlp_ln_g=m_gmlp_ln_g, gmlp_ln_b=m_gmlp_ln_b,
             gmlp_w_s=m_gmlp_w_s, gmlp_b_s=m_gmlp_b_s, w_out=m_w_out, b_out=m_b_out, ln1_g=m_ln1_g, ln1_b=m_ln1_b,
             ffn_w_up=m_ffn_w_up, ffn_b_up=m_ffn_b_up, ffn_conv_w=m_ffn_conv_w, ffn_conv_b=m_ffn_conv_b,
             ffn_w_down=m_ffn_w_down, ffn_b_down=m_ffn_b_down, ln2_g=m_ln2_g, ln2_b=m_ln2_b)
    v = dict(w_in=v_w_in, b_in=v_b_in, conv_dw_w=v_conv_dw_w, conv_dw_b=v_conv_dw_b, conv_ln_g=v_conv_ln_g,
             conv_ln_b=v_conv_ln_b, rel_bias_table=v_rel_bias_table, gmlp_ln_g=v_gmlp_ln_g, gmlp_ln_b=v_gmlp_ln_b,
             gmlp_w_s=v_gmlp_w_s, gmlp_b_s=v_gmlp_b_s, w_out=v_w_out, b_out=v_b_out, ln1_g=v_ln1_g, ln1_b=v_ln1_b,
             ffn_w_up=v_ffn_w_up, ffn_b_up=v_ffn_b_up, ffn_conv_w=v_ffn_conv_w, ffn_conv_b=v_ffn_conv_b,
             ffn_w_down=v_ffn_w_down, ffn_b_down=v_ffn_b_down, ln2_g=v_ln2_g, ln2_b=v_ln2_b)

    shards = {n: _cast_bf16(w[n].reshape(-1, w[n].shape[-1])).reshape(w[n].shape) for n in BIG}
    wb, conv_stack, fconv_stack = _gather_weights(shards, conv_dw_w, ffn_conv_w)
    sp = {n: w[n] for n in SMALL}
    sp["conv_dw_w"] = jnp.moveaxis(conv_stack, 0, 2).reshape(DEPTH, CONV_WIDTH, CONV_CH)
    sp["ffn_conv_w"] = jnp.moveaxis(fconv_stack, 0, 2).reshape(DEPTH, FFN_CONV_WIDTH, 2 * D_FF)

    loss_local, grad_x, grads = _local_step(x[0], loss_target[0], wb, sp)
    loss = lax.psum(loss_local, ("x", "y", "c"))

    big = _reduce_big_grads(grads)
    small_shapes = [grads[n].shape for n in SMALL]
    small = dict(zip(SMALL, _unpack(_small_allreduce(_pack([grads[n] for n in SMALL])), small_shapes)))
    chip = 2 * lax.axis_index("x") + lax.axis_index("y")
    for n in SMALL_SHARDED:
        width = w[n].shape[-1]
        small[n] = lax.dynamic_slice_in_dim(small[n], chip * width, width, axis=2)

    g_out, d_out, m_out, v_out = {}, {}, {}, {}
    for n in BIG:
        g_out[n] = big[n]
        d_out[n], m_out[n], v_out[n] = _adamw_rows(big[n], w[n], m[n], v[n], "adamw_" + n)
    shapes = [small[n].shape for n in SMALL]
    packed = [_pack([src[n] for n in SMALL]) for src in (small, w, m, v)]
    upd = _adamw(*packed, "adamw_small")
    for dst, buf in zip((d_out, m_out, v_out), upd):
        dst.update(zip(SMALL, _unpack(buf, shapes)))
    g_out.update(small)

    return (loss, grad_x[None], *[g_out[n] for n in WEIGHTS], *[d_out[n] for n in WEIGHTS],
            *[m_out[n] for n in WEIGHTS], *[v_out[n] for n in WEIGHTS])
```

```python
import functools
import math

import numpy as np
import jax
import jax.numpy as jnp
from jax import lax
from jax.experimental import pallas as pl
from jax.experimental.pallas import tpu as pltpu

f32 = jnp.float32
bf16 = jnp.bfloat16

D_MODEL = 1024
DEPTH = 2
HEAD_DIM = 64
CONV_CH = 256
CONV_WIDTH = 31
ATTN_HEADS = 8
ATTN_CH = ATTN_HEADS * HEAD_DIM
DILATIONS = (1, 4, 16)
ATTN_BLOCK = 128
N_BUCKETS = 32
MAX_DISTANCE = 2048
GMLP_CH = 256
GMLP_GROUPS = 4
GMLP_GROUP_DIM = GMLP_CH // GMLP_GROUPS
CHUNK = 128
IN_CH = 2 * CONV_CH + 3 * ATTN_CH + 2 * GMLP_CH
D_FF = 2816
FFN_CONV_WIDTH = 3
LN_EPS = 1e-5
ALPHA = (2.0 * DEPTH) ** 0.25
ADAM_LR = 0.001
ADAM_B1 = 0.9
ADAM_B2 = 0.999
ADAM_EPS = 1e-08
ADAM_WD = 0.01
ADAM_STEP = 10

CONV_HALO = 32
FFN_HALO = 8
NEG = -1e30
MIB = 2 ** 20
NT_DIMS = (((1,), (1,)), ((), ()))
TN_DIMS = (((0,), (0,)), ((), ()))
MESH_ID = pl.DeviceIdType.MESH


def _cp(sem, vmem_mib):
    return pltpu.CompilerParams(dimension_semantics=sem, vmem_limit_bytes=vmem_mib * MIB)


def _resident(shape):
    nd = len(shape)
    return pl.BlockSpec(shape, lambda *_: (0,) * nd, pipeline_mode=pl.Buffered(1))


def _acc(shape):
    nd = len(shape)
    return pl.BlockSpec(shape, lambda *_: (0,) * nd)


def _sig(x):
    return 1.0 / (1.0 + jnp.exp(-x))


def _ln_stats(z):
    mu = jnp.mean(z, axis=-1, keepdims=True)
    zc = z - mu
    var = jnp.mean(zc * zc, axis=-1, keepdims=True)
    rstd = lax.rsqrt(var + LN_EPS)
    return zc * rstd, rstd


def _ln_bwd(dy, xhat, rstd, g):
    dxh = dy * g
    m1 = jnp.mean(dxh, axis=-1, keepdims=True)
    m2 = jnp.mean(dxh * xhat, axis=-1, keepdims=True)
    return rstd * (dxh - m1 - xhat * m2)


def _colsum(x):
    return jnp.sum(x, axis=0, keepdims=True)


def _t5_bucket_np(dist):
    max_exact = N_BUCKETS // 2
    dd = np.maximum(dist, 1).astype(np.float64)
    large = max_exact + (np.log(dd / max_exact) / math.log(MAX_DISTANCE / max_exact)
                         * (N_BUCKETS - max_exact)).astype(np.int32)
    large = np.minimum(large, N_BUCKETS - 1)
    return np.where(dist < max_exact, dist, large).astype(np.int32)


def _bucket_ids():
    qi = np.arange(ATTN_BLOCK)[:, None]
    kj = np.arange(2 * ATTN_BLOCK)[None, :]
    dist = np.clip(qi + ATTN_BLOCK - kj, 0, None)
    return np.stack([_t5_bucket_np(dist * d) for d in DILATIONS]).astype(np.int32)


LANES = 128
QKV_CH = 3 * ATTN_CH
PERM_TILE = 512


def _slabs(n, rows):
    return [pltpu.VMEM((rows, LANES), f32)] * n


def _rows_of(slab, r, n, d):
    return slab[...] if d == 1 else slab[pl.ds(r, n, stride=d), :]


def _set_rows_of(slab, r, n, d, val):
    if d == 1:
        slab[...] = val
    else:
        slab[pl.ds(r, n, stride=d), :] = val


def _perm_spec(d, ch):
    return pl.BlockSpec((d, PERM_TILE // d, ch), lambda i: (0, i, 0))


def _perm_shape(S, d, ch, dtype):
    return jax.ShapeDtypeStruct((d, S // d, ch), dtype)


def _inproj_fwd(x, w, b):
    S = x.shape[0]
    T = PERM_TILE
    nsl = QKV_CH // LANES

    def body(x_ref, w_ref, b_ref, a_ref, c_ref, *rest):
        q_refs = rest[:len(DILATIONS)]
        slabs = rest[len(DILATIONS):]
        h = jnp.dot(x_ref[...].astype(bf16), w_ref[...], preferred_element_type=f32) + b_ref[...]
        a_ref[...] = h[:, :2 * CONV_CH]
        q0 = 2 * CONV_CH
        c_ref[...] = h[:, q0 + QKV_CH:]
        for j in range(nsl):
            piece = h[:, q0 + LANES * j:q0 + LANES * (j + 1)]
            if LANES * j < ATTN_CH:
                piece = piece * (HEAD_DIM ** -0.5)
            slabs[j][...] = piece
        for d, q_ref in zip(DILATIONS, q_refs):
            for r in range(d):
                for j in range(nsl):
                    q_ref[r, :, LANES * j:LANES * (j + 1)] = _rows_of(slabs[j], r, T // d, d).astype(bf16)

    row = lambda c: pl.BlockSpec((T, c), lambda i: (i, 0))
    return pl.pallas_call(
        body, grid=(S // T,), name="inproj_fwd",
        out_shape=(jax.ShapeDtypeStruct((S, 2 * CONV_CH), f32), jax.ShapeDtypeStruct((S, 2 * GMLP_CH), f32))
        + tuple(_perm_shape(S, d, QKV_CH, bf16) for d in DILATIONS),
        in_specs=[row(D_MODEL), _resident((D_MODEL, IN_CH)), _resident((1, IN_CH))],
        out_specs=(row(2 * CONV_CH), row(2 * GMLP_CH)) + tuple(_perm_spec(d, QKV_CH) for d in DILATIONS),
        scratch_shapes=_slabs(nsl, T),
        compiler_params=_cp(("parallel",), 48),
    )(x, w, b)


def _conv_fwd(a_in, dw_w, dw_b, ln_g, ln_b):
    S = a_in.shape[0]
    T = 512
    hb = T // CONV_HALO

    def body(a_ref, halo_ref, w_ref, b_ref, g_ref, be_ref, out_ref, hc_ref, buf):
        i = pl.program_id(0)
        am = a_ref[...]
        ah = halo_ref[...]
        hgh = ah[:, :CONV_CH] * _sig(ah[:, CONV_CH:])
        buf[0:CONV_HALO, :] = jnp.where(i > 0, hgh, 0.0)
        buf[CONV_HALO:, :] = am[:, :CONV_CH] * _sig(am[:, CONV_CH:])
        acc = jnp.zeros((T, CONV_CH), f32) + b_ref[...]
        for k in range(CONV_WIDTH):
            acc = acc + w_ref[k:k + 1, :] * buf[pl.ds(CONV_HALO - (CONV_WIDTH - 1) + k, T), :]
        hc_ref[...] = acc
        xhat, _ = _ln_stats(acc)
        y = xhat * g_ref[...] + be_ref[...]
        out_ref[...] = (y * _sig(y)).astype(bf16)

    return pl.pallas_call(
        body, grid=(S // T,), name="conv_fwd",
        out_shape=(jax.ShapeDtypeStruct((S, CONV_CH), bf16), jax.ShapeDtypeStruct((S, CONV_CH), f32)),
        in_specs=[pl.BlockSpec((T, 2 * CONV_CH), lambda i: (i, 0)),
                  pl.BlockSpec((CONV_HALO, 2 * CONV_CH), lambda i: (jnp.maximum(i * hb - 1, 0), 0)),
                  _acc((32, CONV_CH)), _acc((1, CONV_CH)), _acc((1, CONV_CH)), _acc((1, CONV_CH))],
        out_specs=(pl.BlockSpec((T, CONV_CH), lambda i: (i, 0)), pl.BlockSpec((T, CONV_CH), lambda i: (i, 0))),
        scratch_shapes=[pltpu.VMEM((T + CONV_HALO, CONV_CH), f32)],
        compiler_params=_cp(("parallel",), 32),
    )(a_in, a_in, dw_w, dw_b, ln_g, ln_b)


def _bias_build(table, buckets):
    def body(t_ref, bk_ref, o_ref):
        h = pl.program_id(1)
        ids = bk_ref[0]
        acc = jnp.zeros((ATTN_BLOCK, 2 * ATTN_BLOCK), f32)
        for b in range(N_BUCKETS):
            acc = jnp.where(ids == b, t_ref[b, h], acc)
        o_ref[0, 0] = acc

    return pl.pallas_call(
        body, grid=(len(DILATIONS), ATTN_HEADS), name="bias_build",
        out_shape=jax.ShapeDtypeStruct((len(DILATIONS), ATTN_HEADS, ATTN_BLOCK, 2 * ATTN_BLOCK), f32),
        in_specs=[pl.BlockSpec(memory_space=pltpu.SMEM),
                  pl.BlockSpec((1, ATTN_BLOCK, 2 * ATTN_BLOCK), lambda p, h: (p, 0, 0))],
        out_specs=pl.BlockSpec((1, 1, ATTN_BLOCK, 2 * ATTN_BLOCK), lambda p, h: (p, h, 0, 0)),
        compiler_params=_cp(("arbitrary", "arbitrary"), 16),
    )(table, buckets)


def _head_tile(tile, h, col):
    lane_head = lax.broadcasted_iota(jnp.int32, tile.shape, 1) // 16
    return jnp.where(lane_head == h, col, tile)


HEAD_PAIRS = ATTN_HEADS // 2
UNITS_PER_BLOCK = ATTN_HEADS


def _attn_tile(L):
    return min(512, L)


def _band_mask(first_block, n):
    B = ATTN_BLOCK
    row = lax.broadcasted_iota(jnp.int32, (B, 2 * B), 0)
    col = lax.broadcasted_iota(jnp.int32, (B, 2 * B), 1)
    valid = (col >= row) & (col <= row + B)
    if first_block:
        valid = valid & ((col >= B) | (n > 0))
    return valid


def _head_lanes(a):
    lane = lax.broadcasted_iota(jnp.int32, (ATTN_BLOCK, LANES), 1)
    return (lane < HEAD_DIM) if a == 0 else (lane >= HEAD_DIM)


def _pair_keys(cur_ref, halo_ref, part, b, j):
    B = ATTN_BLOCK
    c0 = part * ATTN_CH + LANES * j
    own = cur_ref[B * b:B * (b + 1), c0:c0 + LANES]
    prev = halo_ref[:, LANES * j:LANES * (j + 1)] if b == 0 else cur_ref[B * (b - 1):B * b, c0:c0 + LANES]
    return jnp.concatenate([prev, own], axis=0)


def _attn_fwd_pattern(qkv, bias, d):
    _, L, _ = qkv.shape
    B = ATTN_BLOCK
    QB = _attn_tile(L)
    nsb = QB // B
    U = nsb * UNITS_PER_BLOCK

    def body(cur_ref, hk_ref, hv_ref, b_ref, o_ref, lse_ref, lg, pb):
        n = pl.program_id(1)
        for b in range(nsb):
            valid = _band_mask(b == 0, n)
            for j in range(HEAD_PAIRS):
                q2 = cur_ref[B * b:B * (b + 1), LANES * j:LANES * (j + 1)]
                k2 = _pair_keys(cur_ref, hk_ref, 1, b, j)
                for a in range(2):
                    u = (b * HEAD_PAIRS + j) * 2 + a
                    qm = jnp.where(_head_lanes(a), q2, jnp.zeros_like(q2))
                    logits = lax.dot_general(qm, k2, NT_DIMS, preferred_element_type=f32) + b_ref[2 * j + a]
                    lg[B * u:B * (u + 1), :] = jnp.where(valid, logits, NEG)
        m = jnp.max(lg[...], axis=1, keepdims=True)
        p = jnp.exp(lg[...] - m)
        s = jnp.sum(p, axis=1, keepdims=True)
        pb[...] = p.astype(bf16)
        lse = m + jnp.log(s)
        inv = 1.0 / s
        for b in range(nsb):
            tile = jnp.zeros((B, B), f32)
            for j in range(HEAD_PAIRS):
                v2 = _pair_keys(cur_ref, hv_ref, 2, b, j)
                outs = []
                for a in range(2):
                    u = (b * HEAD_PAIRS + j) * 2 + a
                    rows = slice(B * u, B * (u + 1))
                    outs.append(jnp.dot(pb[rows, :], v2, preferred_element_type=f32) * inv[rows])
                    tile = _head_tile(tile, 2 * j + a, lse[rows])
                o_ref[B * b:B * (b + 1), LANES * j:LANES * (j + 1)] = jnp.where(_head_lanes(0), outs[0], outs[1])
            lse_ref[B * b:B * (b + 1), :] = tile

    halo = lambda part: pl.BlockSpec((None, B, ATTN_CH), lambda r, n: (r, jnp.maximum(n * nsb - 1, 0), part))
    tile_spec = lambda c: pl.BlockSpec((None, QB, c), lambda r, n: (r, n, 0))
    return pl.pallas_call(
        body, grid=(d, L // QB), name=f"attn_fwd_d{d}",
        out_shape=(jax.ShapeDtypeStruct((d, L, ATTN_CH), f32), jax.ShapeDtypeStruct((d, L, B), f32)),
        in_specs=[tile_spec(QKV_CH), halo(1), halo(2), _resident((ATTN_HEADS, B, 2 * B))],
        out_specs=(tile_spec(ATTN_CH), tile_spec(B)),
        scratch_shapes=[pltpu.VMEM((U * B, 2 * B), f32), pltpu.VMEM((U * B, 2 * B), bf16)],
        compiler_params=_cp(("parallel", "parallel"), 40),
    )(qkv, qkv, qkv, bias)


def _attn_merge(parts):
    S = parts[0][0].shape[0] * parts[0][0].shape[1]
    T = PERM_TILE
    nsl = ATTN_CH // LANES
    n_p = len(DILATIONS)

    def body(*refs):
        ins = refs[:2 * n_p]
        out_ref, lse_ref = refs[2 * n_p:2 * n_p + 2]
        slabs = refs[2 * n_p + 2:]
        lses = []
        for p, d in enumerate(DILATIONS):
            o_ref, l_ref = ins[2 * p], ins[2 * p + 1]
            osl = slabs[p * (nsl + 1):p * (nsl + 1) + nsl]
            lsl = slabs[p * (nsl + 1) + nsl]
            for r in range(d):
                for j in range(nsl):
                    _set_rows_of(osl[j], r, T // d, d, o_ref[r, :, LANES * j:LANES * (j + 1)])
                _set_rows_of(lsl, r, T // d, d, l_ref[r])
            lses.append(lsl[...])
        big = functools.reduce(jnp.maximum, lses)
        ws = [jnp.exp(l - big) for l in lses]
        tot = functools.reduce(lambda a_, b_: a_ + b_, ws)
        lse_ref[...] = big + jnp.log(tot)
        ws = [w / tot for w in ws]
        for j in range(nsl):
            acc = jnp.zeros((T, LANES), f32)
            for p in range(n_p):
                wa = ws[p][:, 32 * j:32 * j + 1]
                wb = ws[p][:, 32 * j + 16:32 * j + 17]
                lane = lax.broadcasted_iota(jnp.int32, (T, LANES), 1)
                acc = acc + jnp.where(lane < HEAD_DIM, wa, wb) * slabs[p * (nsl + 1) + j][...]
            out_ref[:, LANES * j:LANES * (j + 1)] = acc.astype(bf16)

    in_specs, args = [], []
    for (o, l), d in zip(parts, DILATIONS):
        in_specs += [_perm_spec(d, ATTN_CH), _perm_spec(d, ATTN_BLOCK)]
        args += [o, l]
    row = lambda c: pl.BlockSpec((T, c), lambda i: (i, 0))
    return pl.pallas_call(
        body, grid=(S // T,), name="attn_merge",
        out_shape=(jax.ShapeDtypeStruct((S, ATTN_CH), bf16), jax.ShapeDtypeStruct((S, ATTN_BLOCK), f32)),
        in_specs=in_specs, out_specs=(row(ATTN_CH), row(ATTN_BLOCK)),
        scratch_shapes=_slabs(n_p * (nsl + 1), T),
        compiler_params=_cp(("parallel",), 40),
    )(*args)


def _attn_fwd(qkvs, bias):
    parts = [_attn_fwd_pattern(q, bias[p], d) for p, (q, d) in enumerate(zip(qkvs, DILATIONS))]
    return _attn_merge(parts)


def _tril_bf16(w):
    row = lax.broadcasted_iota(jnp.int32, (CHUNK, CHUNK), 0)
    col = lax.broadcasted_iota(jnp.int32, (CHUNK, CHUNK), 1)
    return jnp.where(col <= row, w, 0.0).astype(bf16)


def _gmlp_fwd(c_in, ln_g, ln_b, w_s, b_s_t):
    S = c_in.shape[0]
    T = 512

    def body(c_ref, g_ref, be_ref, w_ref, bs_ref, out_ref, mix):
        c = c_ref[...]
        xhat, _ = _ln_stats(c[:, GMLP_CH:])
        vb = (xhat * g_ref[...] + be_ref[...]).astype(bf16)
        for g in range(GMLP_GROUPS):
            wt = _tril_bf16(w_ref[g])
            cs = slice(GMLP_GROUP_DIM * g, GMLP_GROUP_DIM * (g + 1))
            for ci in range(T // CHUNK):
                rs = slice(CHUNK * ci, CHUNK * (ci + 1))
                mix[rs, cs] = jnp.dot(wt, vb[rs, cs], preferred_element_type=f32) + bs_ref[:, g:g + 1]
        out_ref[...] = (c[:, :GMLP_CH] * mix[...]).astype(bf16)

    return pl.pallas_call(
        body, grid=(S // T,), name="gmlp_fwd",
        out_shape=jax.ShapeDtypeStruct((S, GMLP_CH), bf16),
        in_specs=[pl.BlockSpec((T, 2 * GMLP_CH), lambda i: (i, 0)), _acc((1, GMLP_CH)), _acc((1, GMLP_CH)),
                  _acc((GMLP_GROUPS, CHUNK, CHUNK)), _acc((CHUNK, GMLP_GROUPS))],
        out_specs=pl.BlockSpec((T, GMLP_CH), lambda i: (i, 0)),
        scratch_shapes=[pltpu.VMEM((T, GMLP_CH), f32)],
        compiler_params=_cp(("parallel",), 32),
    )(c_in, ln_g, ln_b, w_s, b_s_t)


def _outproj_ln_fwd(conv_out, attn_out, gm_out, w, b, x, ln_g, ln_b):
    S = x.shape[0]
    T = 512

    def body(co_ref, ao_ref, go_ref, w_ref, b_ref, x_ref, g_ref, be_ref, cat_ref, z_ref, y_ref, yb_ref):
        cat = jnp.concatenate([co_ref[...], ao_ref[...], go_ref[...]], axis=1)
        cat_ref[...] = cat
        z = jnp.dot(cat, w_ref[...], preferred_element_type=f32) + b_ref[...] + ALPHA * x_ref[...]
        z_ref[...] = z
        xhat, _ = _ln_stats(z)
        y = xhat * g_ref[...] + be_ref[...]
        y_ref[...] = y
        yb_ref[...] = y.astype(bf16)

    row = lambda c: pl.BlockSpec((T, c), lambda i: (i, 0))
    return pl.pallas_call(
        body, grid=(S // T,), name="outproj_ln_fwd",
        out_shape=(jax.ShapeDtypeStruct((S, D_MODEL), bf16), jax.ShapeDtypeStruct((S, D_MODEL), f32),
                   jax.ShapeDtypeStruct((S, D_MODEL), f32), jax.ShapeDtypeStruct((S, D_MODEL), bf16)),
        in_specs=[row(CONV_CH), row(ATTN_CH), row(GMLP_CH), _resident((D_MODEL, D_MODEL)), _acc((1, D_MODEL)),
                  row(D_MODEL), _acc((1, D_MODEL)), _acc((1, D_MODEL))],
        out_specs=(row(D_MODEL), row(D_MODEL), row(D_MODEL), row(D_MODEL)),
        compiler_params=_cp(("parallel",), 40),
    )(conv_out, attn_out, gm_out, w, b, x, ln_g, ln_b)


def _mm_nn(a, w, b, tn, name):
    S, K = a.shape
    N = w.shape[1]
    T = 512

    def body(a_ref, w_ref, b_ref, o_ref):
        o_ref[...] = jnp.dot(a_ref[...], w_ref[...], preferred_element_type=f32) + b_ref[...]

    return pl.pallas_call(
        body, grid=(N // tn, S // T), name=name,
        out_shape=jax.ShapeDtypeStruct((S, N), f32),
        in_specs=[pl.BlockSpec((T, K), lambda j, i: (i, 0)), pl.BlockSpec((K, tn), lambda j, i: (0, j)),
                  pl.BlockSpec((1, tn), lambda j, i: (0, j))],
        out_specs=pl.BlockSpec((T, tn), lambda j, i: (i, j)),
        compiler_params=_cp(("parallel", "parallel"), 40),
    )(a, w, b)


FFN_COLS = 704


def _ffn_gate_fwd(hf, conv_w, conv_b):
    S = hf.shape[0]
    T = 256
    hb = T // FFN_HALO
    o = FFN_HALO - (FFN_CONV_WIDTH - 1)

    def body(h_ref, halo_ref, w_ref, b_ref, out_ref, buf):
        i = pl.program_id(0)
        buf[0:FFN_HALO, :] = jnp.where(i > 0, halo_ref[...], 0.0)
        buf[FFN_HALO:, :] = h_ref[...]
        for c in range(D_FF // FFN_COLS):
            def conv(c0):
                cs = slice(c0, c0 + FFN_COLS)
                acc = b_ref[:, cs] + w_ref[0:1, cs] * buf[pl.ds(o, T), cs]
                for k in range(1, FFN_CONV_WIDTH):
                    acc = acc + w_ref[k:k + 1, cs] * buf[pl.ds(o + k, T), cs]
                return acc
            g = conv(c * FFN_COLS)
            v = conv(D_FF + c * FFN_COLS)
            out_ref[:, c * FFN_COLS:(c + 1) * FFN_COLS] = (g * _sig(g) * v).astype(bf16)

    return pl.pallas_call(
        body, grid=(S // T,), name="ffn_gate_fwd",
        out_shape=jax.ShapeDtypeStruct((S, D_FF), bf16),
        in_specs=[pl.BlockSpec((T, 2 * D_FF), lambda i: (i, 0)),
                  pl.BlockSpec((FFN_HALO, 2 * D_FF), lambda i: (jnp.maximum(i * hb - 1, 0), 0)),
                  _acc((8, 2 * D_FF)), _acc((1, 2 * D_FF))],
        out_specs=pl.BlockSpec((T, D_FF), lambda i: (i, 0)),
        scratch_shapes=[pltpu.VMEM((T + FFN_HALO, 2 * D_FF), f32)],
        compiler_params=_cp(("parallel",), 48),
    )(hf, hf, conv_w, conv_b)


def _ffn_down_ln_fwd(act, w, b, x1, ln_g, ln_b):
    S = act.shape[0]
    T = 512

    def body(a_ref, w_ref, b_ref, x_ref, g_ref, be_ref, z_ref, y_ref):
        z = jnp.dot(a_ref[...], w_ref[...], preferred_element_type=f32) + b_ref[...] + ALPHA * x_ref[...]
        z_ref[...] = z
        xhat, _ = _ln_stats(z)
        y_ref[...] = xhat * g_ref[...] + be_ref[...]

    row = lambda c: pl.BlockSpec((T, c), lambda i: (i, 0))
    return pl.pallas_call(
        body, grid=(S // T,), name="ffn_down_ln_fwd",
        out_shape=(jax.ShapeDtypeStruct((S, D_MODEL), f32), jax.ShapeDtypeStruct((S, D_MODEL), f32)),
        in_specs=[row(D_FF), _resident((D_FF, D_MODEL)), _acc((1, D_MODEL)), row(D_MODEL), _acc((1, D_MODEL)),
                  _acc((1, D_MODEL))],
        out_specs=(row(D_MODEL), row(D_MODEL)),
        compiler_params=_cp(("parallel",), 40),
    )(act, w, b, x1, ln_g, ln_b)


def _loss_ln_bwd(y, target, z, ln_g):
    S = y.shape[0]
    T = 512

    def body(y_ref, t_ref, z_ref, g_ref, dz_ref, dzb_ref, loss_ref, dg_ref, db_ref):
        @pl.when(pl.program_id(0) == 0)
        def _():
            loss_ref[...] = jnp.zeros_like(loss_ref)
            dg_ref[...] = jnp.zeros_like(dg_ref)
            db_ref[...] = jnp.zeros_like(db_ref)
        err = y_ref[...] - t_ref[...]
        loss_ref[...] += _colsum(err * err) * (0.5 / D_MODEL)
        dy = err * (1.0 / D_MODEL)
        xhat, rstd = _ln_stats(z_ref[...])
        dz = _ln_bwd(dy, xhat, rstd, g_ref[...])
        dz_ref[...] = dz
        dzb_ref[...] = dz.astype(bf16)
        dg_ref[...] += _colsum(dy * xhat)
        db_ref[...] += _colsum(dy)

    row = pl.BlockSpec((T, D_MODEL), lambda i: (i, 0))
    vec = jax.ShapeDtypeStruct((1, D_MODEL), f32)
    return pl.pallas_call(
        body, grid=(S // T,), name="loss_ln_bwd",
        out_shape=(jax.ShapeDtypeStruct((S, D_MODEL), f32), jax.ShapeDtypeStruct((S, D_MODEL), bf16), vec, vec, vec),
        in_specs=[row, row, row, _acc((1, D_MODEL))],
        out_specs=(row, row, _acc((1, D_MODEL)), _acc((1, D_MODEL)), _acc((1, D_MODEL))),
        compiler_params=_cp(("arbitrary",), 40),
    )(y, target, z, ln_g)


def _dgrad_ln_bwd(g, w, dz_res, z, ln_g, name):
    S, K = g.shape
    T = 256
    with_ln = z is not None

    def body(*refs):
        if with_ln:
            g_ref, w_ref, r_ref, z_ref, lg_ref, dz_ref, dzb_ref, dg_ref, db_ref = refs
        else:
            g_ref, w_ref, r_ref, dx_ref = refs
        dx = lax.dot_general(g_ref[...], w_ref[...], NT_DIMS, preferred_element_type=f32) + ALPHA * r_ref[...]
        if not with_ln:
            dx_ref[...] = dx
            return

        @pl.when(pl.program_id(0) == 0)
        def _():
            dg_ref[...] = jnp.zeros_like(dg_ref)
            db_ref[...] = jnp.zeros_like(db_ref)
        xhat, rstd = _ln_stats(z_ref[...])
        dz = _ln_bwd(dx, xhat, rstd, lg_ref[...])
        dz_ref[...] = dz
        dzb_ref[...] = dz.astype(bf16)
        dg_ref[...] += _colsum(dx * xhat)
        db_ref[...] += _colsum(dx)

    row = pl.BlockSpec((T, D_MODEL), lambda i: (i, 0))
    vec = jax.ShapeDtypeStruct((1, D_MODEL), f32)
    in_specs = [pl.BlockSpec((T, K), lambda i: (i, 0)), _resident((D_MODEL, K)), row]
    args = [g, w, dz_res]
    if with_ln:
        in_specs += [row, _acc((1, D_MODEL))]
        args += [z, ln_g]
        out_shape = (jax.ShapeDtypeStruct((S, D_MODEL), f32), jax.ShapeDtypeStruct((S, D_MODEL), bf16), vec, vec)
        out_specs = (row, row, _acc((1, D_MODEL)), _acc((1, D_MODEL)))
    else:
        out_shape = jax.ShapeDtypeStruct((S, D_MODEL), f32)
        out_specs = row
    return pl.pallas_call(
        body, grid=(S // T,), name=name, out_shape=out_shape, in_specs=in_specs, out_specs=out_specs,
        compiler_params=_cp(("arbitrary",), 48),
    )(*args)


def _mm_nt(g, w, name):
    S, K = g.shape
    N = w.shape[0]
    T = 512

    def body(g_ref, w_ref, o_ref):
        o_ref[...] = lax.dot_general(g_ref[...], w_ref[...], NT_DIMS, preferred_element_type=f32)

    return pl.pallas_call(
        body, grid=(S // T,), name=name, out_shape=jax.ShapeDtypeStruct((S, N), f32),
        in_specs=[pl.BlockSpec((T, K), lambda i: (i, 0)), _resident((N, K))],
        out_specs=pl.BlockSpec((T, N), lambda i: (i, 0)),
        compiler_params=_cp(("parallel",), 48),
    )(g, w)


def _ffn_gate_bwd(hf, dact, conv_w, conv_b):
    S = hf.shape[0]
    T = 128
    hb = T // FFN_HALO
    nsteps = S // T
    R = T + FFN_HALO
    o = FFN_HALO - (FFN_CONV_WIDTH - 1)

    def body(hp_ref, h_ref, hn_ref, da_ref, dan_ref, w_ref, b_ref, dh_ref, dw_ref, dcb_ref, buf, dabuf, dbuf):
        i = pl.program_id(0)

        @pl.when(i == 0)
        def _():
            dw_ref[...] = jnp.zeros_like(dw_ref)
            dcb_ref[...] = jnp.zeros_like(dcb_ref)
        buf[0:FFN_HALO, :] = jnp.where(i > 0, hp_ref[...], 0.0)
        buf[FFN_HALO:FFN_HALO + T, :] = h_ref[...]
        buf[FFN_HALO + T:, :] = hn_ref[...]
        dabuf[0:T, :] = da_ref[...]
        dabuf[T:, :] = jnp.where(i < nsteps - 1, dan_ref[...], 0.0)
        for c in range(D_FF // FFN_COLS):
            gs = slice(c * FFN_COLS, (c + 1) * FFN_COLS)
            vs = slice(D_FF + c * FFN_COLS, D_FF + (c + 1) * FFN_COLS)

            def conv(cs):
                acc = b_ref[:, cs] + w_ref[0:1, cs] * buf[pl.ds(o, R), cs]
                for k in range(1, FFN_CONV_WIDTH):
                    acc = acc + w_ref[k:k + 1, cs] * buf[pl.ds(o + k, R), cs]
                return acc
            g = conv(gs)
            v = conv(vs)
            da = dabuf[:, gs]
            sg = _sig(g)
            dbuf[:, gs] = da * v * (sg * (1.0 + g * (1.0 - sg)))
            dbuf[:, vs] = da * (g * sg)
            for cs in (gs, vs):
                dm = dbuf[pl.ds(0, T), cs]
                dh = w_ref[2:3, cs] * dm + w_ref[1:2, cs] * dbuf[pl.ds(1, T), cs] + w_ref[0:1, cs] * dbuf[pl.ds(2, T), cs]
                dh_ref[:, cs] = dh.astype(bf16)
                dcb_ref[:, cs] += _colsum(dm)
                for k in range(FFN_CONV_WIDTH):
                    dw_ref[k:k + 1, cs] += _colsum(dm * buf[pl.ds(o + k, T), cs])

    return pl.pallas_call(
        body, grid=(nsteps,), name="ffn_gate_bwd",
        out_shape=(jax.ShapeDtypeStruct((S, 2 * D_FF), bf16), jax.ShapeDtypeStruct((8, 2 * D_FF), f32),
                   jax.ShapeDtypeStruct((1, 2 * D_FF), f32)),
        in_specs=[pl.BlockSpec((FFN_HALO, 2 * D_FF), lambda i: (jnp.maximum(i * hb - 1, 0), 0)),
                  pl.BlockSpec((T, 2 * D_FF), lambda i: (i, 0)),
                  pl.BlockSpec((FFN_HALO, 2 * D_FF), lambda i: (jnp.minimum((i + 1) * hb, nsteps * hb - 1), 0)),
                  pl.BlockSpec((T, D_FF), lambda i: (i, 0)),
                  pl.BlockSpec((FFN_HALO, D_FF), lambda i: (jnp.minimum((i + 1) * hb, nsteps * hb - 1), 0)),
                  _acc((8, 2 * D_FF)), _acc((1, 2 * D_FF))],
        out_specs=(pl.BlockSpec((T, 2 * D_FF), lambda i: (i, 0)), _acc((8, 2 * D_FF)), _acc((1, 2 * D_FF))),
        scratch_shapes=[pltpu.VMEM((T + 2 * FFN_HALO, 2 * D_FF), f32), pltpu.VMEM((R, D_FF), f32),
                        pltpu.VMEM((R, 2 * D_FF), f32)],
        compiler_params=_cp(("arbitrary",), 48),
    )(hf, hf, hf, dact, dact, conv_w, conv_b)


def _wgrad(a, g, tn, name):
    S, K = a.shape
    N = g.shape[1]
    T = 1024 if S % 1024 == 0 else S

    def body(a_ref, g_ref, dw_ref, db_ref):
        @pl.when(pl.program_id(1) == 0)
        def _():
            dw_ref[...] = jnp.zeros_like(dw_ref)
            db_ref[...] = jnp.zeros_like(db_ref)
        gt = g_ref[...]
        dw_ref[...] += lax.dot_general(a_ref[...].astype(bf16), gt, TN_DIMS, preferred_element_type=f32)
        db_ref[...] += _colsum(gt.astype(f32))

    return pl.pallas_call(
        body, grid=(N // tn, S // T), name=name,
        out_shape=(jax.ShapeDtypeStruct((K, N), f32), jax.ShapeDtypeStruct((1, N), f32)),
        in_specs=[pl.BlockSpec((T, K), lambda j, i: (i, 0)), pl.BlockSpec((T, tn), lambda j, i: (i, j))],
        out_specs=(pl.BlockSpec((K, tn), lambda j, i: (0, j)), pl.BlockSpec((1, tn), lambda j, i: (0, j))),
        compiler_params=_cp(("parallel", "arbitrary"), 48),
    )(a, g)


def _outproj_dgrad(dzb, w, attn_out, lse):
    S = dzb.shape[0]
    T = PERM_TILE
    nsl = ATTN_CH // LANES
    n_p = len(DILATIONS)

    def body(g_ref, w_ref, ao_ref, lse_ref, dco_ref, dgo_ref, *rest):
        do_refs = rest[:n_p]
        st_refs = rest[n_p:2 * n_p]
        slabs = rest[2 * n_p:]
        dcat = lax.dot_general(g_ref[...], w_ref[...], NT_DIMS, preferred_element_type=f32)
        dco_ref[...] = dcat[:, :CONV_CH]
        dgo_ref[...] = dcat[:, CONV_CH + ATTN_CH:]
        lane = lax.broadcasted_iota(jnp.int32, (T, LANES), 1)
        st = lse_ref[...]
        for j in range(nsl):
            dO = dcat[:, CONV_CH + LANES * j:CONV_CH + LANES * (j + 1)]
            prod = dO * ao_ref[:, LANES * j:LANES * (j + 1)].astype(f32)
            for a in range(2):
                in_head = (lane < HEAD_DIM) if a == 0 else (lane >= HEAD_DIM)
                delta = jnp.sum(jnp.where(in_head, prod, 0.0), axis=1, keepdims=True)
                st = jnp.where((lane // 16 == 2 * j + a) & (lane % 16 >= 8), delta, st)
            slabs[j][...] = dO
        slabs[nsl][...] = st
        for d, do_ref, st_ref in zip(DILATIONS, do_refs, st_refs):
            for r in range(d):
                for j in range(nsl):
                    do_ref[r, :, LANES * j:LANES * (j + 1)] = _rows_of(slabs[j], r, T // d, d).astype(bf16)
                st_ref[r] = _rows_of(slabs[nsl], r, T // d, d)

    row = lambda c: pl.BlockSpec((T, c), lambda i: (i, 0))
    return pl.pallas_call(
        body, grid=(S // T,), name="outproj_dgrad",
        out_shape=(jax.ShapeDtypeStruct((S, CONV_CH), f32), jax.ShapeDtypeStruct((S, GMLP_CH), f32))
        + tuple(_perm_shape(S, d, ATTN_CH, bf16) for d in DILATIONS)
        + tuple(_perm_shape(S, d, ATTN_BLOCK, f32) for d in DILATIONS),
        in_specs=[row(D_MODEL), _resident((D_MODEL, D_MODEL)), row(ATTN_CH), row(ATTN_BLOCK)],
        out_specs=(row(CONV_CH), row(GMLP_CH)) + tuple(_perm_spec(d, ATTN_CH) for d in DILATIONS)
        + tuple(_perm_spec(d, ATTN_BLOCK) for d in DILATIONS),
        scratch_shapes=_slabs(nsl + 1, T),
        compiler_params=_cp(("parallel",), 40),
    )(dzb, w, attn_out, lse)


def _gmlp_bwd(c_in, dgm, ln_g, ln_b, w_s, b_s_t):
    S = c_in.shape[0]
    T = 512
    nsteps = S // T

    def body(c_ref, dg_ref, g_ref, be_ref, w_ref, bs_ref, dc_ref, dlg_ref, dlb_ref, dw_ref, dbs_ref,
             du_buf, dv_buf, dm_acc):
        i = pl.program_id(0)

        @pl.when(i == 0)
        def _():
            dlg_ref[...] = jnp.zeros_like(dlg_ref)
            dlb_ref[...] = jnp.zeros_like(dlb_ref)
            dw_ref[...] = jnp.zeros_like(dw_ref)
            dm_acc[...] = jnp.zeros_like(dm_acc)
        c = c_ref[...]
        u = c[:, :GMLP_CH]
        xhat, rstd = _ln_stats(c[:, GMLP_CH:])
        vb = (xhat * g_ref[...] + be_ref[...]).astype(bf16)
        dgm_t = dg_ref[...]
        dm_all = dgm_t * u
        for g in range(GMLP_GROUPS):
            wt = _tril_bf16(w_ref[g])
            cs = slice(GMLP_GROUP_DIM * g, GMLP_GROUP_DIM * (g + 1))
            dw_g = jnp.zeros((CHUNK, CHUNK), f32)
            for ci in range(T // CHUNK):
                rs = slice(CHUNK * ci, CHUNK * (ci + 1))
                v_c = vb[rs, cs]
                mixed = jnp.dot(wt, v_c, preferred_element_type=f32) + bs_ref[:, g:g + 1]
                dm = dm_all[rs, cs]
                dmb = dm.astype(bf16)
                du_buf[rs, cs] = dgm_t[rs, cs] * mixed
                dv_buf[rs, cs] = lax.dot_general(wt, dmb, TN_DIMS, preferred_element_type=f32)
                dw_g = dw_g + lax.dot_general(dmb, v_c, NT_DIMS, preferred_element_type=f32)
                dm_acc[:, cs] += dm
            dw_ref[g] += dw_g
        dv = dv_buf[...]
        dvr = _ln_bwd(dv, xhat, rstd, g_ref[...])
        dlg_ref[...] += _colsum(dv * xhat)
        dlb_ref[...] += _colsum(dv)
        dc_ref[:, :GMLP_CH] = du_buf[...].astype(bf16)
        dc_ref[:, GMLP_CH:] = dvr.astype(bf16)

        @pl.when(i == nsteps - 1)
        def _():
            row = lax.broadcasted_iota(jnp.int32, (CHUNK, CHUNK), 0)
            col = lax.broadcasted_iota(jnp.int32, (CHUNK, CHUNK), 1)
            tile = jnp.zeros((CHUNK, CHUNK), f32)
            for g in range(GMLP_GROUPS):
                dw_ref[g] = jnp.where(col <= row, dw_ref[g], 0.0)
                gsum = jnp.sum(dm_acc[:, GMLP_GROUP_DIM * g:GMLP_GROUP_DIM * (g + 1)], axis=1, keepdims=True)
                tile = jnp.where(col == g, gsum, tile)
            dbs_ref[...] = tile

    vec = jax.ShapeDtypeStruct((1, GMLP_CH), f32)
    return pl.pallas_call(
        body, grid=(nsteps,), name="gmlp_bwd",
        out_shape=(jax.ShapeDtypeStruct((S, 2 * GMLP_CH), bf16), vec, vec,
                   jax.ShapeDtypeStruct((GMLP_GROUPS, CHUNK, CHUNK), f32), jax.ShapeDtypeStruct((CHUNK, CHUNK), f32)),
        in_specs=[pl.BlockSpec((T, 2 * GMLP_CH), lambda i: (i, 0)), pl.BlockSpec((T, GMLP_CH), lambda i: (i, 0)),
                  _acc((1, GMLP_CH)), _acc((1, GMLP_CH)), _acc((GMLP_GROUPS, CHUNK, CHUNK)), _acc((CHUNK, GMLP_GROUPS))],
        out_specs=(pl.BlockSpec((T, 2 * GMLP_CH), lambda i: (i, 0)), _acc((1, GMLP_CH)), _acc((1, GMLP_CH)),
                   _acc((GMLP_GROUPS, CHUNK, CHUNK)), _acc((CHUNK, CHUNK))),
        scratch_shapes=[pltpu.VMEM((T, GMLP_CH), f32), pltpu.VMEM((T, GMLP_CH), f32), pltpu.VMEM((CHUNK, GMLP_CH), f32)],
        compiler_params=_cp(("arbitrary",), 32),
    )(c_in, dgm, ln_g, ln_b, w_s, b_s_t)


def _attn_bwd_pattern(qkv, d_out, stats, bias, d):
    _, L, _ = qkv.shape
    B = ATTN_BLOCK
    QB = _attn_tile(L)
    nsb = QB // B
    nt = L // QB
    U = nsb * UNITS_PER_BLOCK
    KV = 2 * ATTN_CH

    def body(cur_ref, hk_ref, hv_ref, do_ref, st_ref, b_ref, dqkv_ref, dbias_ref, lg, dp, pb, dsb, dkv, carry):
        r = pl.program_id(0)
        i = pl.program_id(1)
        n = nt - 1 - i

        @pl.when((r == 0) & (i == 0))
        def _():
            dbias_ref[...] = jnp.zeros_like(dbias_ref)

        @pl.when(i == 0)
        def _():
            carry[...] = jnp.zeros_like(carry)

        def operands(b, j, a):
            rows = slice(B * b, B * (b + 1))
            q2 = cur_ref[rows, LANES * j:LANES * (j + 1)]
            do2 = do_ref[rows, LANES * j:LANES * (j + 1)]
            keep = _head_lanes(a)
            return jnp.where(keep, q2, jnp.zeros_like(q2)), jnp.where(keep, do2, jnp.zeros_like(do2))

        for b in range(nsb):
            valid = _band_mask(b == 0, n)
            for j in range(HEAD_PAIRS):
                k2 = _pair_keys(cur_ref, hk_ref, 1, b, j)
                v2 = _pair_keys(cur_ref, hv_ref, 2, b, j)
                for a in range(2):
                    u = (b * HEAD_PAIRS + j) * 2 + a
                    qm, dom = operands(b, j, a)
                    logits = lax.dot_general(qm, k2, NT_DIMS, preferred_element_type=f32) + b_ref[2 * j + a]
                    lg[B * u:B * (u + 1), :] = jnp.where(valid, logits, NEG)
                    dp[B * u:B * (u + 1), :] = lax.dot_general(dom, v2, NT_DIMS, preferred_element_type=f32)
        for b in range(nsb):
            for j in range(HEAD_PAIRS):
                for a in range(2):
                    u = (b * HEAD_PAIRS + j) * 2 + a
                    rows = slice(B * u, B * (u + 1))
                    lane0 = 32 * j + 16 * a
                    lse = st_ref[B * b:B * (b + 1), lane0:lane0 + 1]
                    delta = st_ref[B * b:B * (b + 1), lane0 + 8:lane0 + 9]
                    p = jnp.exp(lg[rows, :] - lse)
                    ds = p * (dp[rows, :] - delta)
                    pb[rows, :] = p.astype(bf16)
                    dsb[rows, :] = ds.astype(bf16)
                    dbias_ref[2 * j + a] += ds
        dkv[...] = jnp.zeros_like(dkv)
        for b in range(nsb):
            for j in range(HEAD_PAIRS):
                k2 = _pair_keys(cur_ref, hk_ref, 1, b, j)
                dq, dk2, dv2 = [], None, None
                for a in range(2):
                    u = (b * HEAD_PAIRS + j) * 2 + a
                    rows = slice(B * u, B * (u + 1))
                    qm, dom = operands(b, j, a)
                    ds_u = dsb[rows, :]
                    dq.append(jnp.dot(ds_u, k2, preferred_element_type=f32))
                    dk_u = lax.dot_general(ds_u, qm, TN_DIMS, preferred_element_type=f32)
                    dv_u = lax.dot_general(pb[rows, :], dom, TN_DIMS, preferred_element_type=f32)
                    dk2 = dk_u if dk2 is None else dk2 + dk_u
                    dv2 = dv_u if dv2 is None else dv2 + dv_u
                dq2 = jnp.where(_head_lanes(0), dq[0], dq[1]) * (HEAD_DIM ** -0.5)
                dqkv_ref[B * b:B * (b + 1), LANES * j:LANES * (j + 1)] = dq2.astype(bf16)
                dkv[B * b:B * (b + 2), LANES * j:LANES * (j + 1)] += dk2
                dkv[B * b:B * (b + 2), ATTN_CH + LANES * j:ATTN_CH + LANES * (j + 1)] += dv2
        dkv[QB:, :] += carry[...]
        dqkv_ref[:, ATTN_CH:] = dkv[B:, :].astype(bf16)
        carry[...] = dkv[0:B, :]

    halo = lambda part: pl.BlockSpec((None, B, ATTN_CH),
                                     lambda r, i: (r, jnp.maximum((nt - 1 - i) * nsb - 1, 0), part))
    tile_spec = lambda c: pl.BlockSpec((None, QB, c), lambda r, i: (r, nt - 1 - i, 0))
    return pl.pallas_call(
        body, grid=(d, nt), name=f"attn_bwd_d{d}",
        out_shape=(jax.ShapeDtypeStruct((d, L, QKV_CH), bf16), jax.ShapeDtypeStruct((ATTN_HEADS, B, 2 * B), f32)),
        in_specs=[tile_spec(QKV_CH), halo(1), halo(2), tile_spec(ATTN_CH), tile_spec(B),
                  _resident((ATTN_HEADS, B, 2 * B))],
        out_specs=(tile_spec(QKV_CH), _acc((ATTN_HEADS, B, 2 * B))),
        scratch_shapes=[pltpu.VMEM((U * B, 2 * B), f32), pltpu.VMEM((U * B, 2 * B), f32),
                        pltpu.VMEM((U * B, 2 * B), bf16), pltpu.VMEM((U * B, 2 * B), bf16),
                        pltpu.VMEM((B + QB, KV), f32), pltpu.VMEM((B, KV), f32)],
        compiler_params=_cp(("arbitrary", "arbitrary"), 48),
    )(qkv, qkv, qkv, d_out, stats, bias)


def _attn_bwd_merge(d_a, dqkvs, d_c):
    S = d_a.shape[0]
    T = PERM_TILE
    nsl = QKV_CH // LANES
    n_p = len(DILATIONS)

    def body(da_ref, *rest):
        g_refs = rest[:n_p]
        dc_ref, dh_ref = rest[n_p:n_p + 2]
        slabs = rest[n_p + 2:]
        q0 = 2 * CONV_CH
        dh_ref[:, :q0] = da_ref[...]
        dh_ref[:, q0 + QKV_CH:] = dc_ref[...]
        for p, (d, g_ref) in enumerate(zip(DILATIONS, g_refs)):
            for r in range(d):
                for j in range(nsl):
                    _set_rows_of(slabs[p * nsl + j], r, T // d, d, g_ref[r, :, LANES * j:LANES * (j + 1)].astype(f32))
        for j in range(nsl):
            acc = slabs[j][...]
            for p in range(1, n_p):
                acc = acc + slabs[p * nsl + j][...]
            dh_ref[:, q0 + LANES * j:q0 + LANES * (j + 1)] = acc.astype(bf16)

    row = lambda c: pl.BlockSpec((T, c), lambda i: (i, 0))
    return pl.pallas_call(
        body, grid=(S // T,), name="attn_bwd_merge", out_shape=jax.ShapeDtypeStruct((S, IN_CH), bf16),
        in_specs=[row(2 * CONV_CH)] + [_perm_spec(d, QKV_CH) for d in DILATIONS] + [row(2 * GMLP_CH)],
        out_specs=row(IN_CH), scratch_shapes=_slabs(n_p * nsl, T),
        compiler_params=_cp(("parallel",), 48),
    )(d_a, *dqkvs, d_c)


def _bias_table_grad(dbias, buckets):
    n = dbias.shape[0]

    def body(db_ref, bk_ref, o_ref):
        p = pl.program_id(0)
        h = pl.program_id(1)

        @pl.when((p == 0) & (h == 0))
        def _():
            o_ref[...] = jnp.zeros_like(o_ref)
        ids = bk_ref[0]
        db = db_ref[0, 0]
        row = lax.broadcasted_iota(jnp.int32, (N_BUCKETS, 128), 0)
        lane = lax.broadcasted_iota(jnp.int32, (N_BUCKETS, 128), 1)
        upd = jnp.zeros((N_BUCKETS, 128), f32)
        for b in range(N_BUCKETS):
            s = jnp.sum(jnp.sum(jnp.where(ids == b, db, 0.0), axis=1, keepdims=True), axis=0, keepdims=True)
            upd = jnp.where((row == b) & (lane == h), s, upd)
        o_ref[...] += upd

    return pl.pallas_call(
        body, grid=(n, ATTN_HEADS), name="bias_table_grad",
        out_shape=jax.ShapeDtypeStruct((N_BUCKETS, 128), f32),
        in_specs=[pl.BlockSpec((1, 1, ATTN_BLOCK, 2 * ATTN_BLOCK), lambda p, h: (p, h, 0, 0)),
                  pl.BlockSpec((1, ATTN_BLOCK, 2 * ATTN_BLOCK), lambda p, h: (p, 0, 0))],
        out_specs=_acc((N_BUCKETS, 128)),
        compiler_params=_cp(("arbitrary", "arbitrary"), 16),
    )(dbias, buckets)


def _conv_bwd(a_in, hc, dco, dw_w, ln_g, ln_b):
    S = a_in.shape[0]
    T = 512
    hb = T // CONV_HALO
    nsteps = S // T
    R = T + CONV_HALO
    K = CONV_WIDTH

    def body(ap_ref, a_ref, hc_ref, hcn_ref, d_ref, dn_ref, w_ref, g_ref, be_ref,
             da_ref, dw_ref, dcb_ref, dlg_ref, dlb_ref, hg_buf, ext, dbuf):
        i = pl.program_id(0)

        @pl.when(i == 0)
        def _():
            dw_ref[...] = jnp.zeros_like(dw_ref)
            dcb_ref[...] = jnp.zeros_like(dcb_ref)
            dlg_ref[...] = jnp.zeros_like(dlg_ref)
            dlb_ref[...] = jnp.zeros_like(dlb_ref)
        am = a_ref[...]
        ah = ap_ref[...]
        a = am[:, :CONV_CH]
        sg = _sig(am[:, CONV_CH:])
        hg_buf[0:CONV_HALO, :] = jnp.where(i > 0, ah[:, :CONV_CH] * _sig(ah[:, CONV_CH:]), 0.0)
        hg_buf[CONV_HALO:, :] = a * sg
        ext[0:T, :] = hc_ref[...]
        ext[T:, :] = hcn_ref[...]
        xhat, rstd = _ln_stats(ext[...])
        hl = xhat * g_ref[...] + be_ref[...]
        ext[0:T, :] = d_ref[...]
        ext[T:, :] = dn_ref[...]
        sl_ = _sig(hl)
        dhl = ext[...] * (sl_ * (1.0 + hl * (1.0 - sl_)))
        dhc = _ln_bwd(dhl, xhat, rstd, g_ref[...])
        rowi = lax.broadcasted_iota(jnp.int32, (R, CONV_CH), 0)
        dbuf[...] = jnp.where((rowi < T) | (i < nsteps - 1), dhc, 0.0)
        dlg_ref[...] += _colsum(dhl[:T] * xhat[:T])
        dlb_ref[...] += _colsum(dhl[:T])
        dm = dbuf[pl.ds(0, T), :]
        dcb_ref[...] += _colsum(dm)
        dhg = jnp.zeros((T, CONV_CH), f32)
        for k in range(K):
            dw_ref[k:k + 1, :] += _colsum(dm * hg_buf[pl.ds(CONV_HALO - (K - 1) + k, T), :])
            dhg = dhg + w_ref[k:k + 1, :] * dbuf[pl.ds(K - 1 - k, T), :]
        da_ref[:, :CONV_CH] = (dhg * sg).astype(bf16)
        da_ref[:, CONV_CH:] = (dhg * a * sg * (1.0 - sg)).astype(bf16)

    vec = jax.ShapeDtypeStruct((1, CONV_CH), f32)
    nxt = lambda i: (jnp.minimum((i + 1) * hb, nsteps * hb - 1), 0)
    return pl.pallas_call(
        body, grid=(nsteps,), name="conv_bwd",
        out_shape=(jax.ShapeDtypeStruct((S, 2 * CONV_CH), bf16), jax.ShapeDtypeStruct((32, CONV_CH), f32), vec, vec, vec),
        in_specs=[pl.BlockSpec((CONV_HALO, 2 * CONV_CH), lambda i: (jnp.maximum(i * hb - 1, 0), 0)),
                  pl.BlockSpec((T, 2 * CONV_CH), lambda i: (i, 0)),
                  pl.BlockSpec((T, CONV_CH), lambda i: (i, 0)), pl.BlockSpec((CONV_HALO, CONV_CH), nxt),
                  pl.BlockSpec((T, CONV_CH), lambda i: (i, 0)), pl.BlockSpec((CONV_HALO, CONV_CH), nxt),
                  _acc((32, CONV_CH)), _acc((1, CONV_CH)), _acc((1, CONV_CH))],
        out_specs=(pl.BlockSpec((T, 2 * CONV_CH), lambda i: (i, 0)), _acc((32, CONV_CH)), _acc((1, CONV_CH)),
                   _acc((1, CONV_CH)), _acc((1, CONV_CH))),
        scratch_shapes=[pltpu.VMEM((T + CONV_HALO, CONV_CH), f32), pltpu.VMEM((R, CONV_CH), f32),
                        pltpu.VMEM((R, CONV_CH), f32)],
        compiler_params=_cp(("arbitrary",), 32),
    )(a_in, a_in, hc, hc, dco, dco, dw_w, ln_g, ln_b)


def _adamw(g, w, m, v, name):
    R, C = g.shape
    T = R
    for cand in (512, 256, 128, 64, 32, 16, 8):
        if R % cand == 0 and cand * C * 4 <= MIB:
            T = cand
            break
    c1 = 1.0 / (1.0 - ADAM_B1 ** ADAM_STEP)
    c2 = 1.0 / (1.0 - ADAM_B2 ** ADAM_STEP)

    def body(g_ref, w_ref, m_ref, v_ref, d_ref, nm_ref, nv_ref):
        gg = g_ref[...]
        nm = ADAM_B1 * m_ref[...] + (1.0 - ADAM_B1) * gg
        nv = ADAM_B2 * v_ref[...] + (1.0 - ADAM_B2) * (gg * gg)
        nm_ref[...] = nm
        nv_ref[...] = nv
        d_ref[...] = -ADAM_LR * ((nm * c1) / (jnp.sqrt(nv * c2) + ADAM_EPS) + ADAM_WD * w_ref[...])

    blk = pl.BlockSpec((T, C), lambda i: (i, 0))
    sd = jax.ShapeDtypeStruct((R, C), f32)
    return pl.pallas_call(
        body, grid=(R // T,), name=name, out_shape=(sd, sd, sd), in_specs=[blk] * 4, out_specs=(blk, blk, blk),
        compiler_params=_cp(("parallel",), 48),
    )(g, w, m, v)


def _pad_rows(a, rows):
    return jnp.pad(a, ((0, rows - a.shape[0]), (0, 0)))


def _local_step(x, target, wb, sp):
    buckets = jnp.asarray(_bucket_ids())
    bias = _bias_build(sp["rel_bias_table"], buckets)
    saved = []
    xl = x
    for l in range(DEPTH):
        vec = lambda name: sp[name][l][None, :]
        a_in, c_in, *qkv = _inproj_fwd(xl, wb["w_in"][l], vec("b_in"))
        conv_w = _pad_rows(sp["conv_dw_w"][l], 32)
        conv_out, hc = _conv_fwd(a_in, conv_w, vec("conv_dw_b"), vec("conv_ln_g"), vec("conv_ln_b"))
        attn_out, lse = _attn_fwd(qkv, bias)
        bs_t = sp["gmlp_b_s"][l].T
        gm_out = _gmlp_fwd(c_in, vec("gmlp_ln_g"), vec("gmlp_ln_b"), sp["gmlp_w_s"][l], bs_t)
        cat, z1, x1, x1b = _outproj_ln_fwd(conv_out, attn_out, gm_out, wb["w_out"][l], vec("b_out"), xl,
                                           vec("ln1_g"), vec("ln1_b"))
        hf = _mm_nn(x1b, wb["ffn_w_up"][l], vec("ffn_b_up"), 1408, "ffn_up_fwd")
        fconv_w = _pad_rows(sp["ffn_conv_w"][l], 8)
        act = _ffn_gate_fwd(hf, fconv_w, vec("ffn_conv_b"))
        z2, x2 = _ffn_down_ln_fwd(act, wb["ffn_w_down"][l], vec("ffn_b_down"), x1, vec("ln2_g"), vec("ln2_b"))
        saved.append(dict(x=xl, a_in=a_in, qkv=qkv, c_in=c_in, hc=hc, attn_out=attn_out, lse=lse, cat=cat, z1=z1,
                          x1b=x1b, hf=hf, act=act, z2=z2, conv_w=conv_w, fconv_w=fconv_w, bs_t=bs_t))
        xl = x2

    grads = {}
    per_layer = {k: [None] * DEPTH for k in (
        "w_in", "b_in", "conv_dw_w", "conv_dw_b", "conv_ln_g", "conv_ln_b", "gmlp_ln_g", "gmlp_ln_b", "gmlp_w_s",
        "gmlp_b_s", "w_out", "b_out", "ln1_g", "ln1_b", "ffn_w_up", "ffn_b_up", "ffn_conv_w", "ffn_conv_b",
        "ffn_w_down", "ffn_b_down", "ln2_g", "ln2_b")}
    dbias_all = []
    l = DEPTH - 1
    vec = lambda name: sp[name][l][None, :]
    dz2, dz2b, loss_part, dg2, db2 = _loss_ln_bwd(xl, target, saved[l]["z2"], vec("ln2_g"))
    loss = jnp.sum(loss_part)
    grad_x = None
    for l in reversed(range(DEPTH)):
        sv = saved[l]
        vec = lambda name: sp[name][l][None, :]
        per_layer["ln2_g"][l] = dg2[0]
        per_layer["ln2_b"][l] = db2[0]
        dact = _mm_nt(dz2b, wb["ffn_w_down"][l], "ffn_down_dgrad")
        dw_down, db_down = _wgrad(sv["act"], dz2b, 512, "ffn_down_wgrad")
        per_layer["ffn_w_down"][l] = dw_down
        per_layer["ffn_b_down"][l] = db_down[0]
        dhf, dfcw, dfcb = _ffn_gate_bwd(sv["hf"], dact, sv["fconv_w"], vec("ffn_conv_b"))
        per_layer["ffn_conv_w"][l] = dfcw[:FFN_CONV_WIDTH]
        per_layer["ffn_conv_b"][l] = dfcb[0]
        dw_up, db_up = _wgrad(sv["x1b"], dhf, 1408, "ffn_up_wgrad")
        per_layer["ffn_w_up"][l] = dw_up
        per_layer["ffn_b_up"][l] = db_up[0]
        dz1, dz1b, dg1, db1 = _dgrad_ln_bwd(dhf, wb["ffn_w_up"][l], dz2, sv["z1"], vec("ln1_g"), "ffn_up_dgrad_ln")
        per_layer["ln1_g"][l] = dg1[0]
        per_layer["ln1_b"][l] = db1[0]
        dw_out, db_out = _wgrad(sv["cat"], dz1b, 512, "outproj_wgrad")
        per_layer["w_out"][l] = dw_out
        per_layer["b_out"][l] = db_out[0]
        dco, dgo, *perm = _outproj_dgrad(dz1b, wb["w_out"][l], sv["attn_out"], sv["lse"])
        d_outs, stats = perm[:len(DILATIONS)], perm[len(DILATIONS):]
        d_c, dglg, dglb, dws, dbs = _gmlp_bwd(sv["c_in"], dgo, vec("gmlp_ln_g"), vec("gmlp_ln_b"), sp["gmlp_w_s"][l],
                                              sv["bs_t"])
        per_layer["gmlp_ln_g"][l] = dglg[0]
        per_layer["gmlp_ln_b"][l] = dglb[0]
        per_layer["gmlp_w_s"][l] = dws
        per_layer["gmlp_b_s"][l] = dbs[:, :GMLP_GROUPS].T
        dqkvs = []
        for p, d in enumerate(DILATIONS):
            dqkv, dbias = _attn_bwd_pattern(sv["qkv"][p], d_outs[p], stats[p], bias[p], d)
            dqkvs.append(dqkv)
            dbias_all.append(dbias)
        d_a, dcw, dcb, dclg, dclb = _conv_bwd(sv["a_in"], sv["hc"], dco, sv["conv_w"], vec("conv_ln_g"),
                                              vec("conv_ln_b"))
        per_layer["conv_dw_w"][l] = dcw[:CONV_WIDTH]
        per_layer["conv_dw_b"][l] = dcb[0]
        per_layer["conv_ln_g"][l] = dclg[0]
        per_layer["conv_ln_b"][l] = dclb[0]
        dh = _attn_bwd_merge(d_a, dqkvs, d_c)
        dw_in, db_in = _wgrad(sv["x"], dh, 640, "inproj_wgrad")
        per_layer["w_in"][l] = dw_in
        per_layer["b_in"][l] = db_in[0]
        if l > 0:
            pv = saved[l - 1]
            dz2, dz2b, dg2, db2 = _dgrad_ln_bwd(dh, wb["w_in"][l], dz1, pv["z2"], sp["ln2_g"][l - 1][None, :],
                                                "inproj_dgrad_ln")
        else:
            grad_x = _dgrad_ln_bwd(dh, wb["w_in"][l], dz1, None, None, "inproj_dgrad")
    for k, v in per_layer.items():
        grads[k] = v if k in BIG else jnp.stack(v)
    dbias_cat = jnp.stack(dbias_all)
    bk_cat = jnp.concatenate([buckets] * DEPTH, axis=0)
    grads["rel_bias_table"] = _bias_table_grad(dbias_cat, bk_cat)[:, :ATTN_HEADS]
    return loss, grad_x, grads


N_CHIPS = 4
BIG = {"w_in": (D_MODEL, IN_CH, 1), "w_out": (D_MODEL, D_MODEL, 0),
       "ffn_w_up": (D_MODEL, 2 * D_FF, 1), "ffn_w_down": (D_FF, D_MODEL, 0)}
SMALL = ("b_in", "conv_dw_w", "conv_dw_b", "conv_ln_g", "conv_ln_b", "rel_bias_table", "gmlp_ln_g", "gmlp_ln_b",
         "gmlp_w_s", "gmlp_b_s", "b_out", "ln1_g", "ln1_b", "ffn_b_up", "ffn_conv_w", "ffn_conv_b", "ffn_b_down",
         "ln2_g", "ln2_b")
SMALL_SHARDED = ("conv_dw_w", "ffn_conv_w")
WEIGHTS = ("w_in", "b_in", "conv_dw_w", "conv_dw_b", "conv_ln_g", "conv_ln_b", "rel_bias_table", "gmlp_ln_g",
           "gmlp_ln_b", "gmlp_w_s", "gmlp_b_s", "w_out", "b_out", "ln1_g", "ln1_b", "ffn_w_up", "ffn_b_up",
           "ffn_conv_w", "ffn_conv_b", "ffn_w_down", "ffn_b_down", "ln2_g", "ln2_b")
ANY = pl.BlockSpec(memory_space=pl.ANY)


def _position():
    return lax.axis_index("x"), lax.axis_index("y"), lax.axis_index("c")


def _other_chips(x, y):
    return [(1 - x, y), (x, 1 - y), (1 - x, 1 - y)]


def _cast_bf16(a):
    R, C = a.shape
    T = 128

    def body(a_ref, o_ref):
        o_ref[...] = a_ref[...].astype(bf16)

    return pl.pallas_call(
        body, grid=(R // T,), name="cast_bf16", out_shape=jax.ShapeDtypeStruct((R, C), bf16),
        in_specs=[pl.BlockSpec((T, C), lambda i: (i, 0))], out_specs=pl.BlockSpec((T, C), lambda i: (i, 0)),
        compiler_params=_cp(("parallel",), 16),
    )(a)


def _chip_slot(ref, name, p):
    K, N, ax = BIG[name]
    if ax == 1:
        sz = N // N_CHIPS
        return ref.at[:, :, pl.ds(pl.multiple_of(p * sz, 128), sz)]
    sz = K // N_CHIPS
    return ref.at[:, pl.ds(pl.multiple_of(p * sz, 16), sz), :]


def _gather_weights(shards, conv_w, fconv_w):
    names = list(BIG)
    n_t = len(names) + 2

    def body(*refs):
        ins = refs[:n_t]
        outs = refs[n_t:2 * n_t]
        send_sems, recv_sems, local_sems = refs[2 * n_t:]
        x, y, c = _position()
        me = 2 * x + y
        chips = _other_chips(x, y)

        def slot(t, p):
            if t < len(names):
                return _chip_slot(outs[t], names[t], p)
            return outs[t].at[p]

        locs, cps = [], []
        for t in range(n_t):
            loc = pltpu.make_async_copy(ins[t], slot(t, me), local_sems.at[t])
            loc.start()
            locs.append(loc)
            for k, (px, py) in enumerate(chips):
                cp = pltpu.make_async_remote_copy(
                    src_ref=ins[t], dst_ref=slot(t, me), send_sem=send_sems.at[3 * t + k],
                    recv_sem=recv_sems.at[3 * t + k], device_id=(px, py, c), device_id_type=MESH_ID)
                cp.start()
                cps.append(cp)
        for t in range(n_t):
            for k, (px, py) in enumerate(chips):
                pltpu.make_async_remote_copy(
                    src_ref=ins[t], dst_ref=slot(t, 2 * px + py), send_sem=send_sems.at[3 * t + k],
                    recv_sem=recv_sems.at[3 * t + k], device_id=(px, py, c), device_id_type=MESH_ID).wait_recv()
        for cp in cps:
            cp.wait_send()
        for loc in locs:
            loc.wait()

    ins = [shards[n] for n in names] + [conv_w, fconv_w]
    out_shape = [jax.ShapeDtypeStruct((DEPTH, BIG[n][0], BIG[n][1]), bf16) for n in names]
    out_shape += [jax.ShapeDtypeStruct((N_CHIPS,) + conv_w.shape, f32), jax.ShapeDtypeStruct((N_CHIPS,) + fconv_w.shape, f32)]
    outs = pl.pallas_call(
        body, name="gather_weights", out_shape=tuple(out_shape), in_specs=[ANY] * n_t, out_specs=tuple([ANY] * n_t),
        scratch_shapes=[pltpu.SemaphoreType.DMA((3 * n_t,)), pltpu.SemaphoreType.DMA((3 * n_t,)),
                        pltpu.SemaphoreType.DMA((n_t,))],
    )(*ins)
    return dict(zip(names, outs[:len(names)])), outs[-2], outs[-1]


def _half(ref, name, c):
    K, N, ax = BIG[name]
    if ax == 1:
        return ref.at[pl.ds(pl.multiple_of(c * (K // 2), 8), K // 2), :]
    return ref.at[:, pl.ds(pl.multiple_of(c * (N // 2), 128), N // 2)]


def _half_shape(name):
    K, N, ax = BIG[name]
    return (K // 2, N) if ax == 1 else (K, N // 2)


def _shard_of_half(ref, name, q):
    K, N, ax = BIG[name]
    if ax == 1:
        sz = N // N_CHIPS
        return ref.at[:, pl.ds(pl.multiple_of(q * sz, 128), sz)]
    sz = K // N_CHIPS
    return ref.at[pl.ds(pl.multiple_of(q * sz, 8), sz), :]


def _shard_half_shape(name):
    K, N, ax = BIG[name]
    return (K // 2, N // N_CHIPS) if ax == 1 else (K // N_CHIPS, N // 2)


def _shard_shape(name):
    K, N, ax = BIG[name]
    return (K, N // N_CHIPS) if ax == 1 else (K // N_CHIPS, N)


def _place_in_shard(ref, name, l, c):
    K, N, ax = BIG[name]
    if ax == 1:
        return ref.at[l, pl.ds(pl.multiple_of(c * (K // 2), 8), K // 2), :]
    return ref.at[l, :, pl.ds(pl.multiple_of(c * (N // 2), 128), N // 2)]


def _pair_exchange(tensors):
    n_t = len(tensors)

    def body(*refs):
        ins = refs[:n_t]
        outs = refs[n_t:2 * n_t]
        send_sems, recv_sems = refs[2 * n_t:]
        x, y, c = _position()
        cps = []
        for t, (name, _) in enumerate(tensors):
            cp = pltpu.make_async_remote_copy(
                src_ref=_half(ins[t], name, 1 - c), dst_ref=outs[t], send_sem=send_sems.at[t],
                recv_sem=recv_sems.at[t], device_id=(x, y, 1 - c), device_id_type=MESH_ID)
            cp.start()
            cps.append(cp)
        for cp in cps:
            cp.wait()

    return pl.pallas_call(
        body, name="grad_pair_exchange",
        out_shape=tuple(jax.ShapeDtypeStruct(_half_shape(n), f32) for n, _ in tensors),
        in_specs=[ANY] * n_t, out_specs=tuple([ANY] * n_t),
        scratch_shapes=[pltpu.SemaphoreType.DMA((n_t,)), pltpu.SemaphoreType.DMA((n_t,))],
    )(*[g for _, g in tensors])


def _pair_add(g, rcv, name, c_arr):
    K, N, ax = BIG[name]
    hr, hc = _half_shape(name)
    T = 128
    nrt = hr // T

    def body(c_ref, g_ref, r_ref, o_ref):
        o_ref[...] = g_ref[...] + r_ref[...]

    if ax == 1:
        g_spec = pl.BlockSpec((T, hc), lambda i, c: (c[0] * nrt + i, 0))
    else:
        g_spec = pl.BlockSpec((T, hc), lambda i, c: (i, c[0]))
    plain = pl.BlockSpec((T, hc), lambda i, c: (i, 0))
    return pl.pallas_call(
        body, name="grad_pair_add", out_shape=jax.ShapeDtypeStruct((hr, hc), f32),
        grid_spec=pltpu.PrefetchScalarGridSpec(num_scalar_prefetch=1, grid=(nrt,), in_specs=[g_spec, plain],
                                               out_specs=plain),
        compiler_params=_cp(("parallel",), 32),
    )(c_arr, g, rcv)


def _chip_exchange(tensors):
    n_t = len(tensors)

    def body(*refs):
        ins = refs[:n_t]
        outs = refs[n_t:2 * n_t]
        send_sems, recv_sems, local_sems = refs[2 * n_t:]
        x, y, c = _position()
        me = 2 * x + y
        chips = _other_chips(x, y)
        locs, cps = [], []
        for t, (name, _) in enumerate(tensors):
            loc = pltpu.make_async_copy(_shard_of_half(ins[t], name, me), outs[t].at[me], local_sems.at[t])
            loc.start()
            locs.append(loc)
            for k, (px, py) in enumerate(chips):
                cp = pltpu.make_async_remote_copy(
                    src_ref=_shard_of_half(ins[t], name, 2 * px + py), dst_ref=outs[t].at[me],
                    send_sem=send_sems.at[3 * t + k], recv_sem=recv_sems.at[3 * t + k],
                    device_id=(px, py, c), device_id_type=MESH_ID)
                cp.start()
                cps.append(cp)
        for t, (name, _) in enumerate(tensors):
            for k, (px, py) in enumerate(chips):
                pltpu.make_async_remote_copy(
                    src_ref=_shard_of_half(ins[t], name, 2 * px + py), dst_ref=outs[t].at[2 * px + py],
                    send_sem=send_sems.at[3 * t + k], recv_sem=recv_sems.at[3 * t + k],
                    device_id=(px, py, c), device_id_type=MESH_ID).wait_recv()
        for cp in cps:
            cp.wait_send()
        for loc in locs:
            loc.wait()

    return pl.pallas_call(
        body, name="grad_chip_exchange",
        out_shape=tuple(jax.ShapeDtypeStruct((N_CHIPS,) + _shard_half_shape(n), f32) for n, _ in tensors),
        in_specs=[ANY] * n_t, out_specs=tuple([ANY] * n_t),
        scratch_shapes=[pltpu.SemaphoreType.DMA((3 * n_t,)), pltpu.SemaphoreType.DMA((3 * n_t,)),
                        pltpu.SemaphoreType.DMA((n_t,))],
    )(*[g for _, g in tensors])


def _sum_chips(parts):
    _, R, C = parts.shape
    T = 64

    def body(p_ref, o_ref):
        o_ref[...] = ((p_ref[0] + p_ref[1]) + p_ref[2]) + p_ref[3]

    return pl.pallas_call(
        body, grid=(R // T,), name="grad_sum_chips", out_shape=jax.ShapeDtypeStruct((R, C), f32),
        in_specs=[pl.BlockSpec((N_CHIPS, T, C), lambda i: (0, i, 0))], out_specs=pl.BlockSpec((T, C), lambda i: (i, 0)),
        compiler_params=_cp(("parallel",), 32),
    )(parts)


def _pair_gather(tensors):
    n_t = len(tensors)
    names = list(BIG)

    def body(*refs):
        ins = refs[:n_t]
        outs = dict(zip(names, refs[n_t:n_t + len(names)]))
        send_sems, recv_sems, local_sems = refs[n_t + len(names):]
        x, y, c = _position()
        locs, cps = [], []
        for t, (name, l, _) in enumerate(tensors):
            loc = pltpu.make_async_copy(ins[t], _place_in_shard(outs[name], name, l, c), local_sems.at[t])
            loc.start()
            locs.append(loc)
            cp = pltpu.make_async_remote_copy(
                src_ref=ins[t], dst_ref=_place_in_shard(outs[name], name, l, c), send_sem=send_sems.at[t],
                recv_sem=recv_sems.at[t], device_id=(x, y, 1 - c), device_id_type=MESH_ID)
            cp.start()
            cps.append(cp)
        for t, (name, l, _) in enumerate(tensors):
            pltpu.make_async_remote_copy(
                src_ref=ins[t], dst_ref=_place_in_shard(outs[name], name, l, 1 - c), send_sem=send_sems.at[t],
                recv_sem=recv_sems.at[t], device_id=(x, y, 1 - c), device_id_type=MESH_ID).wait_recv()
        for cp in cps:
            cp.wait_send()
        for loc in locs:
            loc.wait()

    outs = pl.pallas_call(
        body, name="grad_pair_gather",
        out_shape=tuple(jax.ShapeDtypeStruct((DEPTH,) + _shard_shape(n), f32) for n in names),
        in_specs=[ANY] * n_t, out_specs=tuple([ANY] * len(names)),
        scratch_shapes=[pltpu.SemaphoreType.DMA((n_t,)), pltpu.SemaphoreType.DMA((n_t,)),
                        pltpu.SemaphoreType.DMA((n_t,))],
    )(*[g for _, _, g in tensors])
    return dict(zip(names, outs))


def _reduce_big_grads(grads):
    c_arr = jnp.reshape(lax.axis_index("c"), (1,)).astype(jnp.int32)
    tensors = [(n, grads[n][l]) for n in BIG for l in range(DEPTH)]
    layers = [l for n in BIG for l in range(DEPTH)]
    received = _pair_exchange(tensors)
    pair = [(n, _pair_add(g, r, n, c_arr)) for (n, g), r in zip(tensors, received)]
    parts = _chip_exchange(pair)
    reduced = [(n, l, _sum_chips(p)) for (n, _), l, p in zip(pair, layers, parts)]
    return _pair_gather(reduced)


def _small_allreduce(buf):
    R = buf.shape[0]
    n_dev = 8

    def body(in_ref, out_ref, slots, send_sems, recv_sems):
        x, y, c = _position()
        me = 4 * x + 2 * y + c
        slots[me] = in_ref[...]
        peers = []
        for k in range(1, n_dev):
            px = 1 - x if k & 4 else x
            py = 1 - y if k & 2 else y
            pc = 1 - c if k & 1 else c
            peers.append((px, py, pc))
        cps = []
        for k, peer in enumerate(peers):
            cp = pltpu.make_async_remote_copy(
                src_ref=in_ref, dst_ref=slots.at[me], send_sem=send_sems.at[k], recv_sem=recv_sems.at[k],
                device_id=peer, device_id_type=MESH_ID)
            cp.start()
            cps.append(cp)
        for k, (px, py, pc) in enumerate(peers):
            pltpu.make_async_remote_copy(
                src_ref=in_ref, dst_ref=slots.at[4 * px + 2 * py + pc], send_sem=send_sems.at[k],
                recv_sem=recv_sems.at[k], device_id=(px, py, pc), device_id_type=MESH_ID).wait_recv()
        for cp in cps:
            cp.wait_send()
        acc = slots[0]
        for dv in range(1, n_dev):
            acc = acc + slots[dv]
        out_ref[...] = acc

    vm = pl.BlockSpec(memory_space=pltpu.VMEM)
    return pl.pallas_call(
        body, name="small_allreduce", out_shape=jax.ShapeDtypeStruct((R, 128), f32), in_specs=[vm], out_specs=vm,
        scratch_shapes=[pltpu.VMEM((n_dev, R, 128), f32), pltpu.SemaphoreType.DMA((n_dev - 1,)),
                        pltpu.SemaphoreType.DMA((n_dev - 1,))],
        compiler_params=pltpu.CompilerParams(vmem_limit_bytes=40 * MIB),
    )(buf)


PACK_UNIT = 1024


def _pack(arrs):
    parts = []
    for a in arrs:
        flat = a.reshape(-1)
        n = -(-flat.shape[0] // PACK_UNIT) * PACK_UNIT
        parts.append(jnp.pad(flat, (0, n - flat.shape[0])))
    return jnp.concatenate(parts).reshape(-1, 128)


def _unpack(buf, shapes):
    flat = buf.reshape(-1)
    out, off = [], 0
    for shp in shapes:
        n = int(np.prod(shp))
        out.append(flat[off:off + n].reshape(shp))
        off += -(-n // PACK_UNIT) * PACK_UNIT
    return out


def _adamw_rows(g, w, m, v, name):
    shp = g.shape
    C = shp[-1]
    outs = _adamw(g.reshape(-1, C), w.reshape(-1, C), m.reshape(-1, C), v.reshape(-1, C), name)
    return [o.reshape(shp) for o in outs]


def kernel(x, w_in, b_in, conv_dw_w, conv_dw_b, conv_ln_g, conv_ln_b, rel_bias_table, gmlp_ln_g, gmlp_ln_b, gmlp_w_s, gmlp_b_s, w_out, b_out, ln1_g, ln1_b, ffn_w_up, ffn_b_up, ffn_conv_w, ffn_conv_b, ffn_w_down, ffn_b_down, ln2_g, ln2_b, loss_target, m_w_in, m_b_in, m_conv_dw_w, m_conv_dw_b, m_conv_ln_g, m_conv_ln_b, m_rel_bias_table, m_gmlp_ln_g, m_gmlp_ln_b, m_gmlp_w_s, m_gmlp_b_s, m_w_out, m_b_out, m_ln1_g, m_ln1_b, m_ffn_w_up, m_ffn_b_up, m_ffn_conv_w, m_ffn_conv_b, m_ffn_w_down, m_ffn_b_down, m_ln2_g, m_ln2_b, v_w_in, v_b_in, v_conv_dw_w, v_conv_dw_b, v_conv_ln_g, v_conv_ln_b, v_rel_bias_table, v_gmlp_ln_g, v_gmlp_ln_b, v_gmlp_w_s, v_gmlp_b_s, v_w_out, v_b_out, v_ln1_g, v_ln1_b, v_ffn_w_up, v_ffn_b_up, v_ffn_conv_w, v_ffn_conv_b, v_ffn_w_down, v_ffn_b_down, v_ln2_g, v_ln2_b):
    w = dict(w_in=w_in, b_in=b_in, conv_dw_w=conv_dw_w, conv_dw_b=conv_dw_b, conv_ln_g=conv_ln_g, conv_ln_b=conv_ln_b,
             rel_bias_table=rel_bias_table, gmlp_ln_g=gmlp_ln_g, gmlp_ln_b=gmlp_ln_b, gmlp_w_s=gmlp_w_s,
             gmlp_b_s=gmlp_b_s, w_out=w_out, b_out=b_out, ln1_g=ln1_g, ln1_b=ln1_b, ffn_w_up=ffn_w_up,
             ffn_b_up=ffn_b_up, ffn_conv_w=ffn_conv_w, ffn_conv_b=ffn_conv_b, ffn_w_down=ffn_w_down,
             ffn_b_down=ffn_b_down, ln2_g=ln2_g, ln2_b=ln2_b)
    m = dict(w_in=m_w_in, b_in=m_b_in, conv_dw_w=m_conv_dw_w, conv_dw_b=m_conv_dw_b, conv_ln_g=m_conv_ln_g,
             conv_ln_b=m_conv_ln_b, rel_bias_table=m_rel_bias_table, gmlp_ln_g=m_gmlp_ln_g, gmlp_ln_b=m_gmlp_ln_b,
             gmlp_w_s=m_gmlp_w_s, gmlp_b_s=m_gmlp_b_s, w_out=m_w_out, b_out=m_b_out, ln1_g=m_ln1_g, ln1_b=m_ln1_b,
             ffn_w_up=m_ffn_w_up, ffn_b_up=m_ffn_b_up, ffn_conv_w=m_ffn_conv_w, ffn_conv_b=m_ffn_conv_b,
             ffn_w_down=m_ffn_w_down, ffn_b_down=m_ffn_b_down, ln2_g=m_ln2_g, ln2_b=m_ln2_b)
    v = dict(w_in=v_w_in, b_in=v_b_in, conv_dw_w=v_conv_dw_w, conv_dw_b=v_conv_dw_b, conv_ln_g=v_conv_ln_g,
             conv_ln_b=v_conv_ln_b, rel_bias_table=v_rel_bias_table, gmlp_ln_g=v_gmlp_ln_g, gmlp_ln_b=v_gmlp_ln_b,
             gmlp_w_s=v_gmlp_w_s, gmlp_b_s=v_gmlp_b_s, w_out=v_w_out, b_out=v_b_out, ln1_g=v_ln1_g, ln1_b=v_ln1_b,
             ffn_w_up=v_ffn_w_up, ffn_b_up=v_ffn_b_up, ffn_conv_w=v_ffn_conv_w, ffn_conv_b=v_ffn_conv_b,
             ffn_w_down=v_ffn_w_down, ffn_b_down=v_ffn_b_down, ln2_g=v_ln2_g, ln2_b=v_ln2_b)

    shards = {n: _cast_bf16(w[n].reshape(-1, w[n].shape[-1])).reshape(w[n].shape) for n in BIG}
    wb, conv_stack, fconv_stack = _gather_weights(shards, conv_dw_w, ffn_conv_w)
    sp = {n: w[n] for n in SMALL}
    sp["conv_dw_w"] = jnp.moveaxis(conv_stack, 0, 2).reshape(DEPTH, CONV_WIDTH, CONV_CH)
    sp["ffn_conv_w"] = jnp.moveaxis(fconv_stack, 0, 2).reshape(DEPTH, FFN_CONV_WIDTH, 2 * D_FF)

    loss_local, grad_x, grads = _local_step(x[0], loss_target[0], wb, sp)
    loss = lax.psum(loss_local, ("x", "y", "c"))

    big = _reduce_big_grads(grads)
    small_shapes = [grads[n].shape for n in SMALL]
    small = dict(zip(SMALL, _unpack(_small_allreduce(_pack([grads[n] for n in SMALL])), small_shapes)))
    chip = 2 * lax.axis_index("x") + lax.axis_index("y")
    for n in SMALL_SHARDED:
        width = w[n].shape[-1]
        small[n] = lax.dynamic_slice_in_dim(small[n], chip * width, width, axis=2)

    g_out, d_out, m_out, v_out = {}, {}, {}, {}
    for n in BIG:
        g_out[n] = big[n]
        d_out[n], m_out[n], v_out[n] = _adamw_rows(big[n], w[n], m[n], v[n], "adamw_" + n)
    shapes = [small[n].shape for n in SMALL]
    packed = [_pack([src[n] for n in SMALL]) for src in (small, w, m, v)]
    upd = _adamw(*packed, "adamw_small")
    for dst, buf in zip((d_out, m_out, v_out), upd):
        dst.update(zip(SMALL, _unpack(buf, shapes)))
    g_out.update(small)

    return (loss, grad_x[None], *[g_out[n] for n in WEIGHTS], *[d_out[n] for n in WEIGHTS],
            *[m_out[n] for n in WEIGHTS], *[v_out[n] for n in WEIGHTS])
```

```python
import functools
import math

import numpy as np
import jax
import jax.numpy as jnp
from jax import lax
from jax.experimental import pallas as pl
from jax.experimental.pallas import tpu as pltpu

f32 = jnp.float32
bf16 = jnp.bfloat16

D_MODEL = 1024
DEPTH = 2
HEAD_DIM = 64
CONV_CH = 256
CONV_WIDTH = 31
ATTN_HEADS = 8
ATTN_CH = ATTN_HEADS * HEAD_DIM
DILATIONS = (1, 4, 16)
ATTN_BLOCK = 128
N_BUCKETS = 32
MAX_DISTANCE = 2048
GMLP_CH = 256
GMLP_GROUPS = 4
GMLP_GROUP_DIM = GMLP_CH // GMLP_GROUPS
CHUNK = 128
IN_CH = 2 * CONV_CH + 3 * ATTN_CH + 2 * GMLP_CH
D_FF = 2816
FFN_CONV_WIDTH = 3
LN_EPS = 1e-5
ALPHA = (2.0 * DEPTH) ** 0.25
ADAM_LR = 0.001
ADAM_B1 = 0.9
ADAM_B2 = 0.999
ADAM_EPS = 1e-08
ADAM_WD = 0.01
ADAM_STEP = 10

CONV_HALO = 32
FFN_HALO = 8
NEG = -1e30
MIB = 2 ** 20
NT_DIMS = (((1,), (1,)), ((), ()))
TN_DIMS = (((0,), (0,)), ((), ()))
MESH_ID = pl.DeviceIdType.MESH


def _cp(sem, vmem_mib):
    return pltpu.CompilerParams(dimension_semantics=sem, vmem_limit_bytes=vmem_mib * MIB)


def _resident(shape):
    nd = len(shape)
    return pl.BlockSpec(shape, lambda *_: (0,) * nd, pipeline_mode=pl.Buffered(1))


def _acc(shape):
    nd = len(shape)
    return pl.BlockSpec(shape, lambda *_: (0,) * nd)


def _sig(x):
    return 1.0 / (1.0 + jnp.exp(-x))


def _ln_stats(z):
    mu = jnp.mean(z, axis=-1, keepdims=True)
    zc = z - mu
    var = jnp.mean(zc * zc, axis=-1, keepdims=True)
    rstd = lax.rsqrt(var + LN_EPS)
    return zc * rstd, rstd


def _ln_bwd(dy, xhat, rstd, g):
    dxh = dy * g
    m1 = jnp.mean(dxh, axis=-1, keepdims=True)
    m2 = jnp.mean(dxh * xhat, axis=-1, keepdims=True)
    return rstd * (dxh - m1 - xhat * m2)


def _colsum(x):
    return jnp.sum(x, axis=0, keepdims=True)


def _t5_bucket_np(dist):
    max_exact = N_BUCKETS // 2
    dd = np.maximum(dist, 1).astype(np.float64)
    large = max_exact + (np.log(dd / max_exact) / math.log(MAX_DISTANCE / max_exact)
                         * (N_BUCKETS - max_exact)).astype(np.int32)
    large = np.minimum(large, N_BUCKETS - 1)
    return np.where(dist < max_exact, dist, large).astype(np.int32)


def _bucket_ids():
    qi = np.arange(ATTN_BLOCK)[:, None]
    kj = np.arange(2 * ATTN_BLOCK)[None, :]
    dist = np.clip(qi + ATTN_BLOCK - kj, 0, None)
    return np.stack([_t5_bucket_np(dist * d) for d in DILATIONS]).astype(np.int32)


LANES = 128
QKV_CH = 3 * ATTN_CH
PERM_TILE = 512


def _slabs(n, rows):
    return [pltpu.VMEM((rows, LANES), f32)] * n


def _rows_of(slab, r, n, d):
    return slab[...] if d == 1 else slab[pl.ds(r, n, stride=d), :]


def _set_rows_of(slab, r, n, d, val):
    if d == 1:
        slab[...] = val
    else:
        slab[pl.ds(r, n, stride=d), :] = val


def _perm_spec(d, ch):
    return pl.BlockSpec((d, PERM_TILE // d, ch), lambda i: (0, i, 0))


def _perm_shape(S, d, ch, dtype):
    return jax.ShapeDtypeStruct((d, S // d, ch), dtype)


def _inproj_fwd(x, w, b):
    S = x.shape[0]
    T = PERM_TILE
    nsl = QKV_CH // LANES

    def body(x_ref, w_ref, b_ref, a_ref, c_ref, *rest):
        q_refs = rest[:len(DILATIONS)]
        slabs = rest[len(DILATIONS):]
        h = jnp.dot(x_ref[...].astype(bf16), w_ref[...], preferred_element_type=f32) + b_ref[...]
        a_ref[...] = h[:, :2 * CONV_CH]
        q0 = 2 * CONV_CH
        c_ref[...] = h[:, q0 + QKV_CH:]
        for j in range(nsl):
            piece = h[:, q0 + LANES * j:q0 + LANES * (j + 1)]
            if LANES * j < ATTN_CH:
                piece = piece * (HEAD_DIM ** -0.5)
            slabs[j][...] = piece
        for d, q_ref in zip(DILATIONS, q_refs):
            for r in range(d):
                for j in range(nsl):
                    q_ref[r, :, LANES * j:LANES * (j + 1)] = _rows_of(slabs[j], r, T // d, d).astype(bf16)

    row = lambda c: pl.BlockSpec((T, c), lambda i: (i, 0))
    return pl.pallas_call(
        body, grid=(S // T,), name="inproj_fwd",
        out_shape=(jax.ShapeDtypeStruct((S, 2 * CONV_CH), f32), jax.ShapeDtypeStruct((S, 2 * GMLP_CH), f32))
        + tuple(_perm_shape(S, d, QKV_CH, bf16) for d in DILATIONS),
        in_specs=[row(D_MODEL), _resident((D_MODEL, IN_CH)), _resident((1, IN_CH))],
        out_specs=(row(2 * CONV_CH), row(2 * GMLP_CH)) + tuple(_perm_spec(d, QKV_CH) for d in DILATIONS),
        scratch_shapes=_slabs(nsl, T),
        compiler_params=_cp(("parallel",), 48),
    )(x, w, b)


def _conv_fwd(a_in, dw_w, dw_b, ln_g, ln_b):
    S = a_in.shape[0]
    T = 512
    hb = T // CONV_HALO

    def body(a_ref, halo_ref, w_ref, b_ref, g_ref, be_ref, out_ref, hc_ref, buf):
        i = pl.program_id(0)
        am = a_ref[...]
        ah = halo_ref[...]
        hgh = ah[:, :CONV_CH] * _sig(ah[:, CONV_CH:])
        buf[0:CONV_HALO, :] = jnp.where(i > 0, hgh, 0.0)
        buf[CONV_HALO:, :] = am[:, :CONV_CH] * _sig(am[:, CONV_CH:])
        acc = jnp.zeros((T, CONV_CH), f32) + b_ref[...]
        for k in range(CONV_WIDTH):
            acc = acc + w_ref[k:k + 1, :] * buf[pl.ds(CONV_HALO - (CONV_WIDTH - 1) + k, T), :]
        hc_ref[...] = acc
        xhat, _ = _ln_stats(acc)
        y = xhat * g_ref[...] + be_ref[...]
        out_ref[...] = (y * _sig(y)).astype(bf16)

    return pl.pallas_call(
        body, grid=(S // T,), name="conv_fwd",
        out_shape=(jax.ShapeDtypeStruct((S, CONV_CH), bf16), jax.ShapeDtypeStruct((S, CONV_CH), f32)),
        in_specs=[pl.BlockSpec((T, 2 * CONV_CH), lambda i: (i, 0)),
                  pl.BlockSpec((CONV_HALO, 2 * CONV_CH), lambda i: (jnp.maximum(i * hb - 1, 0), 0)),
                  _acc((32, CONV_CH)), _acc((1, CONV_CH)), _acc((1, CONV_CH)), _acc((1, CONV_CH))],
        out_specs=(pl.BlockSpec((T, CONV_CH), lambda i: (i, 0)), pl.BlockSpec((T, CONV_CH), lambda i: (i, 0))),
        scratch_shapes=[pltpu.VMEM((T + CONV_HALO, CONV_CH), f32)],
        compiler_params=_cp(("parallel",), 32),
    )(a_in, a_in, dw_w, dw_b, ln_g, ln_b)


def _bias_build(table, buckets):
    def body(t_ref, bk_ref, o_ref):
        h = pl.program_id(1)
        ids = bk_ref[0]
        acc = jnp.zeros((ATTN_BLOCK, 2 * ATTN_BLOCK), f32)
        for b in range(N_BUCKETS):
            acc = jnp.where(ids == b, t_ref[b, h], acc)
        o_ref[0, 0] = acc

    return pl.pallas_call(
        body, grid=(len(DILATIONS), ATTN_HEADS), name="bias_build",
        out_shape=jax.ShapeDtypeStruct((len(DILATIONS), ATTN_HEADS, ATTN_BLOCK, 2 * ATTN_BLOCK), f32),
        in_specs=[pl.BlockSpec(memory_space=pltpu.SMEM),
                  pl.BlockSpec((1, ATTN_BLOCK, 2 * ATTN_BLOCK), lambda p, h: (p, 0, 0))],
        out_specs=pl.BlockSpec((1, 1, ATTN_BLOCK, 2 * ATTN_BLOCK), lambda p, h: (p, h, 0, 0)),
        compiler_params=_cp(("arbitrary", "arbitrary"), 16),
    )(table, buckets)


def _head_tile(tile, h, col):
    lane_head = lax.broadcasted_iota(jnp.int32, tile.shape, 1) // 16
    return jnp.where(lane_head == h, col, tile)


HEAD_PAIRS = ATTN_HEADS // 2
UNITS_PER_BLOCK = ATTN_HEADS


def _attn_tile(L):
    return min(512, L)


def _band_mask(first_block, n):
    B = ATTN_BLOCK
    row = lax.broadcasted_iota(jnp.int32, (B, 2 * B), 0)
    col = lax.broadcasted_iota(jnp.int32, (B, 2 * B), 1)
    valid = (col >= row) & (col <= row + B)
    if first_block:
        valid = valid & ((col >= B) | (n > 0))
    return valid


def _head_lanes(a):
    lane = lax.broadcasted_iota(jnp.int32, (ATTN_BLOCK, LANES), 1)
    return (lane < HEAD_DIM) if a == 0 else (lane >= HEAD_DIM)


def _pair_keys(cur_ref, halo_ref, part, b, j):
    B = ATTN_BLOCK
    c0 = part * ATTN_CH + LANES * j
    own = cur_ref[B * b:B * (b + 1), c0:c0 + LANES]
    prev = halo_ref[:, LANES * j:LANES * (j + 1)] if b == 0 else cur_ref[B * (b - 1):B * b, c0:c0 + LANES]
    return jnp.concatenate([prev, own], axis=0)


def _attn_fwd_pattern(qkv, bias, d):
    _, L, _ = qkv.shape
    B = ATTN_BLOCK
    QB = _attn_tile(L)
    nsb = QB // B
    U = nsb * UNITS_PER_BLOCK

    def body(cur_ref, hk_ref, hv_ref, b_ref, o_ref, lse_ref, lg, pb):
        n = pl.program_id(1)
        for b in range(nsb):
            valid = _band_mask(b == 0, n)
            for j in range(HEAD_PAIRS):
                q2 = cur_ref[B * b:B * (b + 1), LANES * j:LANES * (j + 1)]
                k2 = _pair_keys(cur_ref, hk_ref, 1, b, j)
                for a in range(2):
                    u = (b * HEAD_PAIRS + j) * 2 + a
                    qm = jnp.where(_head_lanes(a), q2, jnp.zeros_like(q2))
                    logits = lax.dot_general(qm, k2, NT_DIMS, preferred_element_type=f32) + b_ref[2 * j + a]
                    lg[B * u:B * (u + 1), :] = jnp.where(valid, logits, NEG)
        m = jnp.max(lg[...], axis=1, keepdims=True)
        p = jnp.exp(lg[...] - m)
        s = jnp.sum(p, axis=1, keepdims=True)
        pb[...] = p.astype(bf16)
        lse = m + jnp.log(s)
        inv = 1.0 / s
        for b in range(nsb):
            tile = jnp.zeros((B, B), f32)
            for j in range(HEAD_PAIRS):
                v2 = _pair_keys(cur_ref, hv_ref, 2, b, j)
                outs = []
                for a in range(2):
                    u = (b * HEAD_PAIRS + j) * 2 + a
                    rows = slice(B * u, B * (u + 1))
                    outs.append(jnp.dot(pb[rows, :], v2, preferred_element_type=f32) * inv[rows])
                    tile = _head_tile(tile, 2 * j + a, lse[rows])
                o_ref[B * b:B * (b + 1), LANES * j:LANES * (j + 1)] = jnp.where(_head_lanes(0), outs[0], outs[1])
            lse_ref[B * b:B * (b + 1), :] = tile

    halo = lambda part: pl.BlockSpec((None, B, ATTN_CH), lambda r, n: (r, jnp.maximum(n * nsb - 1, 0), part))
    tile_spec = lambda c: pl.BlockSpec((None, QB, c), lambda r, n: (r, n, 0))
    return pl.pallas_call(
        body, grid=(d, L // QB), name=f"attn_fwd_d{d}",
        out_shape=(jax.ShapeDtypeStruct((d, L, ATTN_CH), f32), jax.ShapeDtypeStruct((d, L, B), f32)),
        in_specs=[tile_spec(QKV_CH), halo(1), halo(2), _resident((ATTN_HEADS, B, 2 * B))],
        out_specs=(tile_spec(ATTN_CH), tile_spec(B)),
        scratch_shapes=[pltpu.VMEM((U * B, 2 * B), f32), pltpu.VMEM((U * B, 2 * B), bf16)],
        compiler_params=_cp(("parallel", "parallel"), 40),
    )(qkv, qkv, qkv, bias)


def _attn_merge(parts):
    S = parts[0][0].shape[0] * parts[0][0].shape[1]
    T = PERM_TILE
    nsl = ATTN_CH // LANES
    n_p = len(DILATIONS)

    def body(*refs):
        ins = refs[:2 * n_p]
        out_ref, lse_ref = refs[2 * n_p:2 * n_p + 2]
        slabs = refs[2 * n_p + 2:]
        lses = []
        for p, d in enumerate(DILATIONS):
            o_ref, l_ref = ins[2 * p], ins[2 * p + 1]
            osl = slabs[p * (nsl + 1):p * (nsl + 1) + nsl]
            lsl = slabs[p * (nsl + 1) + nsl]
            for r in range(d):
                for j in range(nsl):
                    _set_rows_of(osl[j], r, T // d, d, o_ref[r, :, LANES * j:LANES * (j + 1)])
                _set_rows_of(lsl, r, T // d, d, l_ref[r])
            lses.append(lsl[...])
        big = functools.reduce(jnp.maximum, lses)
        ws = [jnp.exp(l - big) for l in lses]
        tot = functools.reduce(lambda a_, b_: a_ + b_, ws)
        lse_ref[...] = big + jnp.log(tot)
        ws = [w / tot for w in ws]
        for j in range(nsl):
            acc = jnp.zeros((T, LANES), f32)
            for p in range(n_p):
                wa = ws[p][:, 32 * j:32 * j + 1]
                wb = ws[p][:, 32 * j + 16:32 * j + 17]
                lane = lax.broadcasted_iota(jnp.int32, (T, LANES), 1)
                acc = acc + jnp.where(lane < HEAD_DIM, wa, wb) * slabs[p * (nsl + 1) + j][...]
            out_ref[:, LANES * j:LANES * (j + 1)] = acc.astype(bf16)

    in_specs, args = [], []
    for (o, l), d in zip(parts, DILATIONS):
        in_specs += [_perm_spec(d, ATTN_CH), _perm_spec(d, ATTN_BLOCK)]
        args += [o, l]
    row = lambda c: pl.BlockSpec((T, c), lambda i: (i, 0))
    return pl.pallas_call(
        body, grid=(S // T,), name="attn_merge",
        out_shape=(jax.ShapeDtypeStruct((S, ATTN_CH), bf16), jax.ShapeDtypeStruct((S, ATTN_BLOCK), f32)),
        in_specs=in_specs, out_specs=(row(ATTN_CH), row(ATTN_BLOCK)),
        scratch_shapes=_slabs(n_p * (nsl + 1), T),
        compiler_params=_cp(("parallel",), 40),
    )(*args)


def _attn_fwd(qkvs, bias):
    parts = [_attn_fwd_pattern(q, bias[p], d) for p, (q, d) in enumerate(zip(qkvs, DILATIONS))]
    return _attn_merge(parts)


def _tril_bf16(w):
    row = lax.broadcasted_iota(jnp.int32, (CHUNK, CHUNK), 0)
    col = lax.broadcasted_iota(jnp.int32, (CHUNK, CHUNK), 1)
    return jnp.where(col <= row, w, 0.0).astype(bf16)


def _gmlp_fwd(c_in, ln_g, ln_b, w_s, b_s_t):
    S = c_in.shape[0]
    T = 512

    def body(c_ref, g_ref, be_ref, w_ref, bs_ref, out_ref, mix):
        c = c_ref[...]
        xhat, _ = _ln_stats(c[:, GMLP_CH:])
        vb = (xhat * g_ref[...] + be_ref[...]).astype(bf16)
        for g in range(GMLP_GROUPS):
            wt = _tril_bf16(w_ref[g])
            cs = slice(GMLP_GROUP_DIM * g, GMLP_GROUP_DIM * (g + 1))
            for ci in range(T // CHUNK):
                rs = slice(CHUNK * ci, CHUNK * (ci + 1))
                mix[rs, cs] = jnp.dot(wt, vb[rs, cs], preferred_element_type=f32) + bs_ref[:, g:g + 1]
        out_ref[...] = (c[:, :GMLP_CH] * mix[...]).astype(bf16)

    return pl.pallas_call(
        body, grid=(S // T,), name="gmlp_fwd",
        out_shape=jax.ShapeDtypeStruct((S, GMLP_CH), bf16),
        in_specs=[pl.BlockSpec((T, 2 * GMLP_CH), lambda i: (i, 0)), _acc((1, GMLP_CH)), _acc((1, GMLP_CH)),
                  _acc((GMLP_GROUPS, CHUNK, CHUNK)), _acc((CHUNK, GMLP_GROUPS))],
        out_specs=pl.BlockSpec((T, GMLP_CH), lambda i: (i, 0)),
        scratch_shapes=[pltpu.VMEM((T, GMLP_CH), f32)],
        compiler_params=_cp(("parallel",), 32),
    )(c_in, ln_g, ln_b, w_s, b_s_t)


def _outproj_ln_fwd(conv_out, attn_out, gm_out, w, b, x, ln_g, ln_b):
    S = x.shape[0]
    T = 512

    def body(co_ref, ao_ref, go_ref, w_ref, b_ref, x_ref, g_ref, be_ref, cat_ref, z_ref, y_ref, yb_ref):
        cat = jnp.concatenate([co_ref[...], ao_ref[...], go_ref[...]], axis=1)
        cat_ref[...] = cat
        z = jnp.dot(cat, w_ref[...], preferred_element_type=f32) + b_ref[...] + ALPHA * x_ref[...]
        z_ref[...] = z
        xhat, _ = _ln_stats(z)
        y = xhat * g_ref[...] + be_ref[...]
        y_ref[...] = y
        yb_ref[...] = y.astype(bf16)

    row = lambda c: pl.BlockSpec((T, c), lambda i: (i, 0))
    return pl.pallas_call(
        body, grid=(S // T,), name="outproj_ln_fwd",
        out_shape=(jax.ShapeDtypeStruct((S, D_MODEL), bf16), jax.ShapeDtypeStruct((S, D_MODEL), f32),
                   jax.ShapeDtypeStruct((S, D_MODEL), f32), jax.ShapeDtypeStruct((S, D_MODEL), bf16)),
        in_specs=[row(CONV_CH), row(ATTN_CH), row(GMLP_CH), _resident((D_MODEL, D_MODEL)), _acc((1, D_MODEL)),
                  row(D_MODEL), _acc((1, D_MODEL)), _acc((1, D_MODEL))],
        out_specs=(row(D_MODEL), row(D_MODEL), row(D_MODEL), row(D_MODEL)),
        compiler_params=_cp(("parallel",), 40),
    )(conv_out, attn_out, gm_out, w, b, x, ln_g, ln_b)


GATE_ROWS = 32
GATE_COLS = 128
GATE_MM_COLS = 256
SUBLANES = 8


def _gate_cols(c0):
    return slice(c0, c0 + GATE_COLS), slice(D_FF + c0, D_FF + c0 + GATE_COLS)


def _bcast_rows(ref, k, cs):
    return jnp.broadcast_to(ref[k:k + 1, cs], (GATE_ROWS, GATE_COLS))


def _fold_rows(z):
    acc = z[0:SUBLANES]
    for r in range(SUBLANES, GATE_ROWS, SUBLANES):
        acc = acc + z[r:r + SUBLANES]
    return acc


def _ffn_up_gate_fwd(x1b, w, b, conv_w, conv_b):
    S = x1b.shape[0]
    T = 256
    H = FFN_HALO
    K = FFN_CONV_WIDTH

    def body(x_ref, w_ref, b_ref, cw_ref, cb_ref, hfb_ref, hc_ref, act_ref, hbuf, carry):
        @pl.when(pl.program_id(0) == 0)
        def _():
            carry[...] = jnp.zeros_like(carry)
        x = x_ref[...]
        for m0 in range(0, D_FF, GATE_MM_COLS):
            for cm in (slice(m0, m0 + GATE_MM_COLS), slice(D_FF + m0, D_FF + m0 + GATE_MM_COLS)):
                h = jnp.dot(x, w_ref[:, cm], preferred_element_type=f32) + b_ref[:, cm]
                hbuf[:, cm] = h
                hfb_ref[:, cm] = h.astype(bf16)
            for c0 in range(m0, m0 + GATE_MM_COLS, GATE_COLS):
                cols = _gate_cols(c0)
                wts = [[_bcast_rows(cw_ref, k, cs) for k in range(K)] + [_bcast_rows(cb_ref, 0, cs)] for cs in cols]

                def step(rg, tails, cols=cols, wts=wts):
                    rows = pl.ds(pl.multiple_of(rg * GATE_ROWS, GATE_ROWS), GATE_ROWS)
                    hc, new_tails = [], []
                    for cs, wt, tail in zip(cols, wts, tails):
                        h = hbuf[rows, cs]
                        ext = jnp.concatenate([tail, h], axis=0)
                        acc = wt[K] + wt[K - 1] * h
                        for back in range(1, K):
                            acc = acc + wt[K - 1 - back] * pltpu.roll(ext, back, 0)[H:]
                        hc_ref[rows, cs] = acc
                        hc.append(acc)
                        new_tails.append(h[GATE_ROWS - H:])
                    act_ref[rows, cols[0]] = (hc[0] * _sig(hc[0]) * hc[1]).astype(bf16)
                    return tuple(new_tails)

                tails = lax.fori_loop(0, T // GATE_ROWS, step, tuple(carry[:, cs] for cs in cols))
                for cs, tail in zip(cols, tails):
                    carry[:, cs] = tail

    row = lambda c: pl.BlockSpec((T, c), lambda i: (i, 0))
    return pl.pallas_call(
        body, grid=(S // T,), name="ffn_up_gate_fwd",
        out_shape=(jax.ShapeDtypeStruct((S, 2 * D_FF), bf16), jax.ShapeDtypeStruct((S, 2 * D_FF), f32),
                   jax.ShapeDtypeStruct((S, D_FF), bf16)),
        in_specs=[row(D_MODEL), _resident((D_MODEL, 2 * D_FF)), _acc((1, 2 * D_FF)), _acc((8, 2 * D_FF)),
                  _acc((1, 2 * D_FF))],
        out_specs=(row(2 * D_FF), row(2 * D_FF), row(D_FF)),
        scratch_shapes=[pltpu.VMEM((T, 2 * D_FF), f32), pltpu.VMEM((H, 2 * D_FF), f32)],
        compiler_params=_cp(("arbitrary",), 56),
    )(x1b, w, b, conv_w, conv_b)


def _ffn_down_ln_fwd(act, w, b, x1, ln_g, ln_b):
    S = act.shape[0]
    T = 512

    def body(a_ref, w_ref, b_ref, x_ref, g_ref, be_ref, z_ref, y_ref):
        z = jnp.dot(a_ref[...], w_ref[...], preferred_element_type=f32) + b_ref[...] + ALPHA * x_ref[...]
        z_ref[...] = z
        xhat, _ = _ln_stats(z)
        y_ref[...] = xhat * g_ref[...] + be_ref[...]

    row = lambda c: pl.BlockSpec((T, c), lambda i: (i, 0))
    return pl.pallas_call(
        body, grid=(S // T,), name="ffn_down_ln_fwd",
        out_shape=(jax.ShapeDtypeStruct((S, D_MODEL), f32), jax.ShapeDtypeStruct((S, D_MODEL), f32)),
        in_specs=[row(D_FF), _resident((D_FF, D_MODEL)), _acc((1, D_MODEL)), row(D_MODEL), _acc((1, D_MODEL)),
                  _acc((1, D_MODEL))],
        out_specs=(row(D_MODEL), row(D_MODEL)),
        compiler_params=_cp(("parallel",), 40),
    )(act, w, b, x1, ln_g, ln_b)


def _loss_ln_bwd(y, target, z, ln_g):
    S = y.shape[0]
    T = 512

    def body(y_ref, t_ref, z_ref, g_ref, dz_ref, dzb_ref, loss_ref, dg_ref, db_ref):
        @pl.when(pl.program_id(0) == 0)
        def _():
            loss_ref[...] = jnp.zeros_like(loss_ref)
            dg_ref[...] = jnp.zeros_like(dg_ref)
            db_ref[...] = jnp.zeros_like(db_ref)
        err = y_ref[...] - t_ref[...]
        loss_ref[...] += _colsum(err * err) * (0.5 / D_MODEL)
        dy = err * (1.0 / D_MODEL)
        xhat, rstd = _ln_stats(z_ref[...])
        dz = _ln_bwd(dy, xhat, rstd, g_ref[...])
        dz_ref[...] = dz
        dzb_ref[...] = dz.astype(bf16)
        dg_ref[...] += _colsum(dy * xhat)
        db_ref[...] += _colsum(dy)

    row = pl.BlockSpec((T, D_MODEL), lambda i: (i, 0))
    vec = jax.ShapeDtypeStruct((1, D_MODEL), f32)
    return pl.pallas_call(
        body, grid=(S // T,), name="loss_ln_bwd",
        out_shape=(jax.ShapeDtypeStruct((S, D_MODEL), f32), jax.ShapeDtypeStruct((S, D_MODEL), bf16), vec, vec, vec),
        in_specs=[row, row, row, _acc((1, D_MODEL))],
        out_specs=(row, row, _acc((1, D_MODEL)), _acc((1, D_MODEL)), _acc((1, D_MODEL))),
        compiler_params=_cp(("arbitrary",), 40),
    )(y, target, z, ln_g)


def _dgrad_ln_bwd(g, w, dz_res, z, ln_g, name):
    S, K = g.shape
    T = 256
    with_ln = z is not None

    def body(*refs):
        if with_ln:
            g_ref, w_ref, r_ref, z_ref, lg_ref, dz_ref, dzb_ref, dg_ref, db_ref = refs
        else:
            g_ref, w_ref, r_ref, dx_ref = refs
        dx = lax.dot_general(g_ref[...], w_ref[...], NT_DIMS, preferred_element_type=f32) + ALPHA * r_ref[...]
        if not with_ln:
            dx_ref[...] = dx
            return

        @pl.when(pl.program_id(0) == 0)
        def _():
            dg_ref[...] = jnp.zeros_like(dg_ref)
            db_ref[...] = jnp.zeros_like(db_ref)
        xhat, rstd = _ln_stats(z_ref[...])
        dz = _ln_bwd(dx, xhat, rstd, lg_ref[...])
        dz_ref[...] = dz
        dzb_ref[...] = dz.astype(bf16)
        dg_ref[...] += _colsum(dx * xhat)
        db_ref[...] += _colsum(dx)

    row = pl.BlockSpec((T, D_MODEL), lambda i: (i, 0))
    vec = jax.ShapeDtypeStruct((1, D_MODEL), f32)
    in_specs = [pl.BlockSpec((T, K), lambda i: (i, 0)), _resident((D_MODEL, K)), row]
    args = [g, w, dz_res]
    if with_ln:
        in_specs += [row, _acc((1, D_MODEL))]
        args += [z, ln_g]
        out_shape = (jax.ShapeDtypeStruct((S, D_MODEL), f32), jax.ShapeDtypeStruct((S, D_MODEL), bf16), vec, vec)
        out_specs = (row, row, _acc((1, D_MODEL)), _acc((1, D_MODEL)))
    else:
        out_shape = jax.ShapeDtypeStruct((S, D_MODEL), f32)
        out_specs = row
    return pl.pallas_call(
        body, grid=(S // T,), name=name, out_shape=out_shape, in_specs=in_specs, out_specs=out_specs,
        compiler_params=_cp(("arbitrary",), 48),
    )(*args)


def _ffn_down_gate_bwd(dzb, w_down, hfb, hc, conv_w):
    S = hc.shape[0]
    T = 256
    H = FFN_HALO
    nt = S // T
    K = FFN_CONV_WIDTH

    def body(dz_ref, w_ref, h_ref, hc_ref, cw_ref, dh_ref, dw_ref, dcb_ref, da_buf, carry):
        @pl.when(pl.program_id(0) == 0)
        def _():
            dw_ref[...] = jnp.zeros_like(dw_ref)
            dcb_ref[...] = jnp.zeros_like(dcb_ref)
            carry[...] = jnp.zeros_like(carry)
        da_buf[...] = lax.dot_general(dz_ref[...], w_ref[...], NT_DIMS, preferred_element_type=f32)
        ngroups = T // GATE_ROWS
        for c0 in range(0, D_FF, GATE_COLS):
            cols = _gate_cols(c0)
            wts = [[_bcast_rows(cw_ref, k, cs) for k in range(K)] for cs in cols]

            def step(it, state, cols=cols, wts=wts):
                heads, accs = state
                rows = pl.ds(pl.multiple_of((ngroups - 1 - it) * GATE_ROWS, GATE_ROWS), GATE_ROWS)
                g = hc_ref[rows, cols[0]]
                v = hc_ref[rows, cols[1]]
                da = da_buf[rows, cols[0]]
                sg = _sig(g)
                dms = (da * v * (sg * (1.0 + g * (1.0 - sg))), da * (g * sg))
                new_heads, new_accs = [], []
                for cs, wt, dm, head, acc in zip(cols, wts, dms, heads, accs):
                    h0 = h_ref[rows, cs].astype(f32)
                    ext = jnp.concatenate([dm, head], axis=0)
                    dh = wt[K - 1] * dm
                    acc_k = [None] * K + [acc[K] + _fold_rows(dm)]
                    acc_k[K - 1] = acc[K - 1] + _fold_rows(dm * h0)
                    for ahead in range(1, K):
                        dk = pltpu.roll(ext, GATE_ROWS + H - ahead, 0)[:GATE_ROWS]
                        dh = dh + wt[K - 1 - ahead] * dk
                        acc_k[K - 1 - ahead] = acc[K - 1 - ahead] + _fold_rows(dk * h0)
                    dh_ref[rows, cs] = dh.astype(bf16)
                    new_heads.append(dm[:H])
                    new_accs.append(tuple(acc_k))
                return tuple(new_heads), tuple(new_accs)

            zero = jnp.zeros((SUBLANES, GATE_COLS), f32)
            init = (tuple(carry[:, cs] for cs in cols), tuple(tuple(zero for _ in range(K + 1)) for _ in cols))
            heads, accs = lax.fori_loop(0, ngroups, step, init)
            for cs, head, acc in zip(cols, heads, accs):
                carry[:, cs] = head
                dcb_ref[:, cs] += _colsum(acc[K])
                for k in range(K):
                    dw_ref[k:k + 1, cs] += _colsum(acc[k])

    tile = lambda c: pl.BlockSpec((T, c), lambda i: (nt - 1 - i, 0))
    return pl.pallas_call(
        body, grid=(nt,), name="ffn_down_gate_bwd",
        out_shape=(jax.ShapeDtypeStruct((S, 2 * D_FF), bf16), jax.ShapeDtypeStruct((8, 2 * D_FF), f32),
                   jax.ShapeDtypeStruct((1, 2 * D_FF), f32)),
        in_specs=[tile(D_MODEL), _resident((D_FF, D_MODEL)), tile(2 * D_FF), tile(2 * D_FF), _acc((8, 2 * D_FF))],
        out_specs=(tile(2 * D_FF), _acc((8, 2 * D_FF)), _acc((1, 2 * D_FF))),
        scratch_shapes=[pltpu.VMEM((T, D_FF), f32), pltpu.VMEM((H, 2 * D_FF), f32)],
        compiler_params=_cp(("arbitrary",), 48),
    )(dzb, w_down, hfb, hc, conv_w)


def _wgrad(a, g, tn, name):
    S, K = a.shape
    N = g.shape[1]
    T = 1024 if S % 1024 == 0 else S

    def body(a_ref, g_ref, dw_ref, db_ref):
        @pl.when(pl.program_id(1) == 0)
        def _():
            dw_ref[...] = jnp.zeros_like(dw_ref)
            db_ref[...] = jnp.zeros_like(db_ref)
        gt = g_ref[...]
        dw_ref[...] += lax.dot_general(a_ref[...].astype(bf16), gt, TN_DIMS, preferred_element_type=f32)
        db_ref[...] += _colsum(gt.astype(f32))

    return pl.pallas_call(
        body, grid=(N // tn, S // T), name=name,
        out_shape=(jax.ShapeDtypeStruct((K, N), f32), jax.ShapeDtypeStruct((1, N), f32)),
        in_specs=[pl.BlockSpec((T, K), lambda j, i: (i, 0)), pl.BlockSpec((T, tn), lambda j, i: (i, j))],
        out_specs=(pl.BlockSpec((K, tn), lambda j, i: (0, j)), pl.BlockSpec((1, tn), lambda j, i: (0, j))),
        compiler_params=_cp(("parallel", "arbitrary"), 48),
    )(a, g)


def _outproj_dgrad(dzb, w, attn_out, lse):
    S = dzb.shape[0]
    T = PERM_TILE
    nsl = ATTN_CH // LANES
    n_p = len(DILATIONS)

    def body(g_ref, w_ref, ao_ref, lse_ref, dco_ref, dgo_ref, *rest):
        do_refs = rest[:n_p]
        st_refs = rest[n_p:2 * n_p]
        slabs = rest[2 * n_p:]
        dcat = lax.dot_general(g_ref[...], w_ref[...], NT_DIMS, preferred_element_type=f32)
        dco_ref[...] = dcat[:, :CONV_CH]
        dgo_ref[...] = dcat[:, CONV_CH + ATTN_CH:]
        lane = lax.broadcasted_iota(jnp.int32, (T, LANES), 1)
        st = lse_ref[...]
        for j in range(nsl):
            dO = dcat[:, CONV_CH + LANES * j:CONV_CH + LANES * (j + 1)]
            prod = dO * ao_ref[:, LANES * j:LANES * (j + 1)].astype(f32)
            for a in range(2):
                in_head = (lane < HEAD_DIM) if a == 0 else (lane >= HEAD_DIM)
                delta = jnp.sum(jnp.where(in_head, prod, 0.0), axis=1, keepdims=True)
                st = jnp.where((lane // 16 == 2 * j + a) & (lane % 16 >= 8), delta, st)
            slabs[j][...] = dO
        slabs[nsl][...] = st
        for d, do_ref, st_ref in zip(DILATIONS, do_refs, st_refs):
            for r in range(d):
                for j in range(nsl):
                    do_ref[r, :, LANES * j:LANES * (j + 1)] = _rows_of(slabs[j], r, T // d, d).astype(bf16)
                st_ref[r] = _rows_of(slabs[nsl], r, T // d, d)

    row = lambda c: pl.BlockSpec((T, c), lambda i: (i, 0))
    return pl.pallas_call(
        body, grid=(S // T,), name="outproj_dgrad",
        out_shape=(jax.ShapeDtypeStruct((S, CONV_CH), f32), jax.ShapeDtypeStruct((S, GMLP_CH), f32))
        + tuple(_perm_shape(S, d, ATTN_CH, bf16) for d in DILATIONS)
        + tuple(_perm_shape(S, d, ATTN_BLOCK, f32) for d in DILATIONS),
        in_specs=[row(D_MODEL), _resident((D_MODEL, D_MODEL)), row(ATTN_CH), row(ATTN_BLOCK)],
        out_specs=(row(CONV_CH), row(GMLP_CH)) + tuple(_perm_spec(d, ATTN_CH) for d in DILATIONS)
        + tuple(_perm_spec(d, ATTN_BLOCK) for d in DILATIONS),
        scratch_shapes=_slabs(nsl + 1, T),
        compiler_params=_cp(("parallel",), 40),
    )(dzb, w, attn_out, lse)


def _gmlp_bwd(c_in, dgm, ln_g, ln_b, w_s, b_s_t):
    S = c_in.shape[0]
    T = 512
    nsteps = S // T

    def body(c_ref, dg_ref, g_ref, be_ref, w_ref, bs_ref, dc_ref, dlg_ref, dlb_ref, dw_ref, dbs_ref,
             du_buf, dv_buf, dm_acc):
        i = pl.program_id(0)

        @pl.when(i == 0)
        def _():
            dlg_ref[...] = jnp.zeros_like(dlg_ref)
            dlb_ref[...] = jnp.zeros_like(dlb_ref)
            dw_ref[...] = jnp.zeros_like(dw_ref)
            dm_acc[...] = jnp.zeros_like(dm_acc)
        c = c_ref[...]
        u = c[:, :GMLP_CH]
        xhat, rstd = _ln_stats(c[:, GMLP_CH:])
        vb = (xhat * g_ref[...] + be_ref[...]).astype(bf16)
        dgm_t = dg_ref[...]
        dm_all = dgm_t * u
        for g in range(GMLP_GROUPS):
            wt = _tril_bf16(w_ref[g])
            cs = slice(GMLP_GROUP_DIM * g, GMLP_GROUP_DIM * (g + 1))
            dw_g = jnp.zeros((CHUNK, CHUNK), f32)
            for ci in range(T // CHUNK):
                rs = slice(CHUNK * ci, CHUNK * (ci + 1))
                v_c = vb[rs, cs]
                mixed = jnp.dot(wt, v_c, preferred_element_type=f32) + bs_ref[:, g:g + 1]
                dm = dm_all[rs, cs]
                dmb = dm.astype(bf16)
                du_buf[rs, cs] = dgm_t[rs, cs] * mixed
                dv_buf[rs, cs] = lax.dot_general(wt, dmb, TN_DIMS, preferred_element_type=f32)
                dw_g = dw_g + lax.dot_general(dmb, v_c, NT_DIMS, preferred_element_type=f32)
                dm_acc[:, cs] += dm
            dw_ref[g] += dw_g
        dv = dv_buf[...]
        dvr = _ln_bwd(dv, xhat, rstd, g_ref[...])
        dlg_ref[...] += _colsum(dv * xhat)
        dlb_ref[...] += _colsum(dv)
        dc_ref[:, :GMLP_CH] = du_buf[...].astype(bf16)
        dc_ref[:, GMLP_CH:] = dvr.astype(bf16)

        @pl.when(i == nsteps - 1)
        def _():
            row = lax.broadcasted_iota(jnp.int32, (CHUNK, CHUNK), 0)
            col = lax.broadcasted_iota(jnp.int32, (CHUNK, CHUNK), 1)
            tile = jnp.zeros((CHUNK, CHUNK), f32)
            for g in range(GMLP_GROUPS):
                dw_ref[g] = jnp.where(col <= row, dw_ref[g], 0.0)
                gsum = jnp.sum(dm_acc[:, GMLP_GROUP_DIM * g:GMLP_GROUP_DIM * (g + 1)], axis=1, keepdims=True)
                tile = jnp.where(col == g, gsum, tile)
            dbs_ref[...] = tile

    vec = jax.ShapeDtypeStruct((1, GMLP_CH), f32)
    return pl.pallas_call(
        body, grid=(nsteps,), name="gmlp_bwd",
        out_shape=(jax.ShapeDtypeStruct((S, 2 * GMLP_CH), bf16), vec, vec,
                   jax.ShapeDtypeStruct((GMLP_GROUPS, CHUNK, CHUNK), f32), jax.ShapeDtypeStruct((CHUNK, CHUNK), f32)),
        in_specs=[pl.BlockSpec((T, 2 * GMLP_CH), lambda i: (i, 0)), pl.BlockSpec((T, GMLP_CH), lambda i: (i, 0)),
                  _acc((1, GMLP_CH)), _acc((1, GMLP_CH)), _acc((GMLP_GROUPS, CHUNK, CHUNK)), _acc((CHUNK, GMLP_GROUPS))],
        out_specs=(pl.BlockSpec((T, 2 * GMLP_CH), lambda i: (i, 0)), _acc((1, GMLP_CH)), _acc((1, GMLP_CH)),
                   _acc((GMLP_GROUPS, CHUNK, CHUNK)), _acc((CHUNK, CHUNK))),
        scratch_shapes=[pltpu.VMEM((T, GMLP_CH), f32), pltpu.VMEM((T, GMLP_CH), f32), pltpu.VMEM((CHUNK, GMLP_CH), f32)],
        compiler_params=_cp(("arbitrary",), 32),
    )(c_in, dgm, ln_g, ln_b, w_s, b_s_t)


def _attn_bwd_pattern(qkv, d_out, stats, bias, d):
    _, L, _ = qkv.shape
    B = ATTN_BLOCK
    QB = _attn_tile(L)
    nsb = QB // B
    nt = L // QB
    U = nsb * UNITS_PER_BLOCK
    KV = 2 * ATTN_CH

    def body(cur_ref, hk_ref, hv_ref, do_ref, st_ref, b_ref, dqkv_ref, dbias_ref, lg, dp, pb, dsb, dkv, carry):
        r = pl.program_id(0)
        i = pl.program_id(1)
        n = nt - 1 - i

        @pl.when((r == 0) & (i == 0))
        def _():
            dbias_ref[...] = jnp.zeros_like(dbias_ref)

        @pl.when(i == 0)
        def _():
            carry[...] = jnp.zeros_like(carry)

        def operands(b, j, a):
            rows = slice(B * b, B * (b + 1))
            q2 = cur_ref[rows, LANES * j:LANES * (j + 1)]
            do2 = do_ref[rows, LANES * j:LANES * (j + 1)]
            keep = _head_lanes(a)
            return jnp.where(keep, q2, jnp.zeros_like(q2)), jnp.where(keep, do2, jnp.zeros_like(do2))

        for b in range(nsb):
            valid = _band_mask(b == 0, n)
            for j in range(HEAD_PAIRS):
                k2 = _pair_keys(cur_ref, hk_ref, 1, b, j)
                v2 = _pair_keys(cur_ref, hv_ref, 2, b, j)
                for a in range(2):
                    u = (b * HEAD_PAIRS + j) * 2 + a
                    qm, dom = operands(b, j, a)
                    logits = lax.dot_general(qm, k2, NT_DIMS, preferred_element_type=f32) + b_ref[2 * j + a]
                    lg[B * u:B * (u + 1), :] = jnp.where(valid, logits, NEG)
                    dp[B * u:B * (u + 1), :] = lax.dot_general(dom, v2, NT_DIMS, preferred_element_type=f32)
        for b in range(nsb):
            for j in range(HEAD_PAIRS):
                for a in range(2):
                    u = (b * HEAD_PAIRS + j) * 2 + a
                    rows = slice(B * u, B * (u + 1))
                    lane0 = 32 * j + 16 * a
                    lse = st_ref[B * b:B * (b + 1), lane0:lane0 + 1]
                    delta = st_ref[B * b:B * (b + 1), lane0 + 8:lane0 + 9]
                    p = jnp.exp(lg[rows, :] - lse)
                    ds = p * (dp[rows, :] - delta)
                    pb[rows, :] = p.astype(bf16)
                    dsb[rows, :] = ds.astype(bf16)
                    dbias_ref[2 * j + a] += ds
        dkv[...] = jnp.zeros_like(dkv)
        for b in range(nsb):
            for j in range(HEAD_PAIRS):
                k2 = _pair_keys(cur_ref, hk_ref, 1, b, j)
                dq, dk2, dv2 = [], None, None
                for a in range(2):
                    u = (b * HEAD_PAIRS + j) * 2 + a
                    rows = slice(B * u, B * (u + 1))
                    qm, dom = operands(b, j, a)
                    ds_u = dsb[rows, :]
                    dq.append(jnp.dot(ds_u, k2, preferred_element_type=f32))
                    dk_u = lax.dot_general(ds_u, qm, TN_DIMS, preferred_element_type=f32)
                    dv_u = lax.dot_general(pb[rows, :], dom, TN_DIMS, preferred_element_type=f32)
                    dk2 = dk_u if dk2 is None else dk2 + dk_u
                    dv2 = dv_u if dv2 is None else dv2 + dv_u
                dq2 = jnp.where(_head_lanes(0), dq[0], dq[1]) * (HEAD_DIM ** -0.5)
                dqkv_ref[B * b:B * (b + 1), LANES * j:LANES * (j + 1)] = dq2.astype(bf16)
                dkv[B * b:B * (b + 2), LANES * j:LANES * (j + 1)] += dk2
                dkv[B * b:B * (b + 2), ATTN_CH + LANES * j:ATTN_CH + LANES * (j + 1)] += dv2
        dkv[QB:, :] += carry[...]
        dqkv_ref[:, ATTN_CH:] = dkv[B:, :].astype(bf16)
        carry[...] = dkv[0:B, :]

    halo = lambda part: pl.BlockSpec((None, B, ATTN_CH),
                                     lambda r, i: (r, jnp.maximum((nt - 1 - i) * nsb - 1, 0), part))
    tile_spec = lambda c: pl.BlockSpec((None, QB, c), lambda r, i: (r, nt - 1 - i, 0))
    return pl.pallas_call(
        body, grid=(d, nt), name=f"attn_bwd_d{d}",
        out_shape=(jax.ShapeDtypeStruct((d, L, QKV_CH), bf16), jax.ShapeDtypeStruct((ATTN_HEADS, B, 2 * B), f32)),
        in_specs=[tile_spec(QKV_CH), halo(1), halo(2), tile_spec(ATTN_CH), tile_spec(B),
                  _resident((ATTN_HEADS, B, 2 * B))],
        out_specs=(tile_spec(QKV_CH), _acc((ATTN_HEADS, B, 2 * B))),
        scratch_shapes=[pltpu.VMEM((U * B, 2 * B), f32), pltpu.VMEM((U * B, 2 * B), f32),
                        pltpu.VMEM((U * B, 2 * B), bf16), pltpu.VMEM((U * B, 2 * B), bf16),
                        pltpu.VMEM((B + QB, KV), f32), pltpu.VMEM((B, KV), f32)],
        compiler_params=_cp(("arbitrary", "arbitrary"), 48),
    )(qkv, qkv, qkv, d_out, stats, bias)


def _attn_bwd_merge(d_a, dqkvs, d_c):
    S = d_a.shape[0]
    T = PERM_TILE
    nsl = QKV_CH // LANES
    n_p = len(DILATIONS)

    def body(da_ref, *rest):
        g_refs = rest[:n_p]
        dc_ref, dh_ref = rest[n_p:n_p + 2]
        slabs = rest[n_p + 2:]
        q0 = 2 * CONV_CH
        dh_ref[:, :q0] = da_ref[...]
        dh_ref[:, q0 + QKV_CH:] = dc_ref[...]
        for p, (d, g_ref) in enumerate(zip(DILATIONS, g_refs)):
            for r in range(d):
                for j in range(nsl):
                    _set_rows_of(slabs[p * nsl + j], r, T // d, d, g_ref[r, :, LANES * j:LANES * (j + 1)].astype(f32))
        for j in range(nsl):
            acc = slabs[j][...]
            for p in range(1, n_p):
                acc = acc + slabs[p * nsl + j][...]
            dh_ref[:, q0 + LANES * j:q0 + LANES * (j + 1)] = acc.astype(bf16)

    row = lambda c: pl.BlockSpec((T, c), lambda i: (i, 0))
    return pl.pallas_call(
        body, grid=(S // T,), name="attn_bwd_merge", out_shape=jax.ShapeDtypeStruct((S, IN_CH), bf16),
        in_specs=[row(2 * CONV_CH)] + [_perm_spec(d, QKV_CH) for d in DILATIONS] + [row(2 * GMLP_CH)],
        out_specs=row(IN_CH), scratch_shapes=_slabs(n_p * nsl, T),
        compiler_params=_cp(("parallel",), 48),
    )(d_a, *dqkvs, d_c)


def _bias_table_grad(dbias, buckets):
    n = dbias.shape[0]

    def body(db_ref, bk_ref, o_ref):
        p = pl.program_id(0)
        h = pl.program_id(1)

        @pl.when((p == 0) & (h == 0))
        def _():
            o_ref[...] = jnp.zeros_like(o_ref)
        ids = bk_ref[0]
        db = db_ref[0, 0]
        row = lax.broadcasted_iota(jnp.int32, (N_BUCKETS, 128), 0)
        lane = lax.broadcasted_iota(jnp.int32, (N_BUCKETS, 128), 1)
        upd = jnp.zeros((N_BUCKETS, 128), f32)
        for b in range(N_BUCKETS):
            s = jnp.sum(jnp.sum(jnp.where(ids == b, db, 0.0), axis=1, keepdims=True), axis=0, keepdims=True)
            upd = jnp.where((row == b) & (lane == h), s, upd)
        o_ref[...] += upd

    return pl.pallas_call(
        body, grid=(n, ATTN_HEADS), name="bias_table_grad",
        out_shape=jax.ShapeDtypeStruct((N_BUCKETS, 128), f32),
        in_specs=[pl.BlockSpec((1, 1, ATTN_BLOCK, 2 * ATTN_BLOCK), lambda p, h: (p, h, 0, 0)),
                  pl.BlockSpec((1, ATTN_BLOCK, 2 * ATTN_BLOCK), lambda p, h: (p, 0, 0))],
        out_specs=_acc((N_BUCKETS, 128)),
        compiler_params=_cp(("arbitrary", "arbitrary"), 16),
    )(dbias, buckets)


def _conv_bwd(a_in, hc, dco, dw_w, ln_g, ln_b):
    S = a_in.shape[0]
    T = 512
    hb = T // CONV_HALO
    nsteps = S // T
    R = T + CONV_HALO
    K = CONV_WIDTH

    def body(ap_ref, a_ref, hc_ref, hcn_ref, d_ref, dn_ref, w_ref, g_ref, be_ref,
             da_ref, dw_ref, dcb_ref, dlg_ref, dlb_ref, hg_buf, ext, dbuf):
        i = pl.program_id(0)

        @pl.when(i == 0)
        def _():
            dw_ref[...] = jnp.zeros_like(dw_ref)
            dcb_ref[...] = jnp.zeros_like(dcb_ref)
            dlg_ref[...] = jnp.zeros_like(dlg_ref)
            dlb_ref[...] = jnp.zeros_like(dlb_ref)
        am = a_ref[...]
        ah = ap_ref[...]
        a = am[:, :CONV_CH]
        sg = _sig(am[:, CONV_CH:])
        hg_buf[0:CONV_HALO, :] = jnp.where(i > 0, ah[:, :CONV_CH] * _sig(ah[:, CONV_CH:]), 0.0)
        hg_buf[CONV_HALO:, :] = a * sg
        ext[0:T, :] = hc_ref[...]
        ext[T:, :] = hcn_ref[...]
        xhat, rstd = _ln_stats(ext[...])
        hl = xhat * g_ref[...] + be_ref[...]
        ext[0:T, :] = d_ref[...]
        ext[T:, :] = dn_ref[...]
        sl_ = _sig(hl)
        dhl = ext[...] * (sl_ * (1.0 + hl * (1.0 - sl_)))
        dhc = _ln_bwd(dhl, xhat, rstd, g_ref[...])
        rowi = lax.broadcasted_iota(jnp.int32, (R, CONV_CH), 0)
        dbuf[...] = jnp.where((rowi < T) | (i < nsteps - 1), dhc, 0.0)
        dlg_ref[...] += _colsum(dhl[:T] * xhat[:T])
        dlb_ref[...] += _colsum(dhl[:T])
        dm = dbuf[pl.ds(0, T), :]
        dcb_ref[...] += _colsum(dm)
        dhg = jnp.zeros((T, CONV_CH), f32)
        for k in range(K):
            dw_ref[k:k + 1, :] += _colsum(dm * hg_buf[pl.ds(CONV_HALO - (K - 1) + k, T), :])
            dhg = dhg + w_ref[k:k + 1, :] * dbuf[pl.ds(K - 1 - k, T), :]
        da_ref[:, :CONV_CH] = (dhg * sg).astype(bf16)
        da_ref[:, CONV_CH:] = (dhg * a * sg * (1.0 - sg)).astype(bf16)

    vec = jax.ShapeDtypeStruct((1, CONV_CH), f32)
    nxt = lambda i: (jnp.minimum((i + 1) * hb, nsteps * hb - 1), 0)
    return pl.pallas_call(
        body, grid=(nsteps,), name="conv_bwd",
        out_shape=(jax.ShapeDtypeStruct((S, 2 * CONV_CH), bf16), jax.ShapeDtypeStruct((32, CONV_CH), f32), vec, vec, vec),
        in_specs=[pl.BlockSpec((CONV_HALO, 2 * CONV_CH), lambda i: (jnp.maximum(i * hb - 1, 0), 0)),
                  pl.BlockSpec((T, 2 * CONV_CH), lambda i: (i, 0)),
                  pl.BlockSpec((T, CONV_CH), lambda i: (i, 0)), pl.BlockSpec((CONV_HALO, CONV_CH), nxt),
                  pl.BlockSpec((T, CONV_CH), lambda i: (i, 0)), pl.BlockSpec((CONV_HALO, CONV_CH), nxt),
                  _acc((32, CONV_CH)), _acc((1, CONV_CH)), _acc((1, CONV_CH))],
        out_specs=(pl.BlockSpec((T, 2 * CONV_CH), lambda i: (i, 0)), _acc((32, CONV_CH)), _acc((1, CONV_CH)),
                   _acc((1, CONV_CH)), _acc((1, CONV_CH))),
        scratch_shapes=[pltpu.VMEM((T + CONV_HALO, CONV_CH), f32), pltpu.VMEM((R, CONV_CH), f32),
                        pltpu.VMEM((R, CONV_CH), f32)],
        compiler_params=_cp(("arbitrary",), 32),
    )(a_in, a_in, hc, hc, dco, dco, dw_w, ln_g, ln_b)


def _adamw(g, w, m, v, name):
    R, C = g.shape
    T = R
    for cand in (512, 256, 128, 64, 32, 16, 8):
        if R % cand == 0 and cand * C * 4 <= MIB:
            T = cand
            break
    c1 = 1.0 / (1.0 - ADAM_B1 ** ADAM_STEP)
    c2 = 1.0 / (1.0 - ADAM_B2 ** ADAM_STEP)

    def body(g_ref, w_ref, m_ref, v_ref, d_ref, nm_ref, nv_ref):
        gg = g_ref[...]
        nm = ADAM_B1 * m_ref[...] + (1.0 - ADAM_B1) * gg
        nv = ADAM_B2 * v_ref[...] + (1.0 - ADAM_B2) * (gg * gg)
        nm_ref[...] = nm
        nv_ref[...] = nv
        d_ref[...] = -ADAM_LR * ((nm * c1) / (jnp.sqrt(nv * c2) + ADAM_EPS) + ADAM_WD * w_ref[...])

    blk = pl.BlockSpec((T, C), lambda i: (i, 0))
    sd = jax.ShapeDtypeStruct((R, C), f32)
    return pl.pallas_call(
        body, grid=(R // T,), name=name, out_shape=(sd, sd, sd), in_specs=[blk] * 4, out_specs=(blk, blk, blk),
        compiler_params=_cp(("parallel",), 48),
    )(g, w, m, v)


def _pad_rows(a, rows):
    return jnp.pad(a, ((0, rows - a.shape[0]), (0, 0)))


def _local_step(x, target, wb, sp):
    buckets = jnp.asarray(_bucket_ids())
    bias = _bias_build(sp["rel_bias_table"], buckets)
    saved = []
    xl = x
    for l in range(DEPTH):
        vec = lambda name: sp[name][l][None, :]
        a_in, c_in, *qkv = _inproj_fwd(xl, wb["w_in"][l], vec("b_in"))
        conv_w = _pad_rows(sp["conv_dw_w"][l], 32)
        conv_out, hc = _conv_fwd(a_in, conv_w, vec("conv_dw_b"), vec("conv_ln_g"), vec("conv_ln_b"))
        attn_out, lse = _attn_fwd(qkv, bias)
        bs_t = sp["gmlp_b_s"][l].T
        gm_out = _gmlp_fwd(c_in, vec("gmlp_ln_g"), vec("gmlp_ln_b"), sp["gmlp_w_s"][l], bs_t)
        cat, z1, x1, x1b = _outproj_ln_fwd(conv_out, attn_out, gm_out, wb["w_out"][l], vec("b_out"), xl,
                                           vec("ln1_g"), vec("ln1_b"))
        fconv_w = _pad_rows(sp["ffn_conv_w"][l], 8)
        hfb, fhc, act = _ffn_up_gate_fwd(x1b, wb["ffn_w_up"][l], vec("ffn_b_up"), fconv_w, vec("ffn_conv_b"))
        z2, x2 = _ffn_down_ln_fwd(act, wb["ffn_w_down"][l], vec("ffn_b_down"), x1, vec("ln2_g"), vec("ln2_b"))
        saved.append(dict(x=xl, a_in=a_in, qkv=qkv, c_in=c_in, hc=hc, attn_out=attn_out, lse=lse, cat=cat, z1=z1,
                          x1b=x1b, hfb=hfb, fhc=fhc, act=act, z2=z2, conv_w=conv_w, fconv_w=fconv_w, bs_t=bs_t))
        xl = x2

    grads = {}
    per_layer = {k: [None] * DEPTH for k in (
        "w_in", "b_in", "conv_dw_w", "conv_dw_b", "conv_ln_g", "conv_ln_b", "gmlp_ln_g", "gmlp_ln_b", "gmlp_w_s",
        "gmlp_b_s", "w_out", "b_out", "ln1_g", "ln1_b", "ffn_w_up", "ffn_b_up", "ffn_conv_w", "ffn_conv_b",
        "ffn_w_down", "ffn_b_down", "ln2_g", "ln2_b")}
    dbias_all = []
    l = DEPTH - 1
    vec = lambda name: sp[name][l][None, :]
    dz2, dz2b, loss_part, dg2, db2 = _loss_ln_bwd(xl, target, saved[l]["z2"], vec("ln2_g"))
    loss = jnp.sum(loss_part)
    grad_x = None
    for l in reversed(range(DEPTH)):
        sv = saved[l]
        vec = lambda name: sp[name][l][None, :]
        per_layer["ln2_g"][l] = dg2[0]
        per_layer["ln2_b"][l] = db2[0]
        dw_down, db_down = _wgrad(sv["act"], dz2b, 512, "ffn_down_wgrad")
        per_layer["ffn_w_down"][l] = dw_down
        per_layer["ffn_b_down"][l] = db_down[0]
        dhf, dfcw, dfcb = _ffn_down_gate_bwd(dz2b, wb["ffn_w_down"][l], sv["hfb"], sv["fhc"], sv["fconv_w"])
        per_layer["ffn_conv_w"][l] = dfcw[:FFN_CONV_WIDTH]
        per_layer["ffn_conv_b"][l] = dfcb[0]
        dw_up, db_up = _wgrad(sv["x1b"], dhf, 1408, "ffn_up_wgrad")
        per_layer["ffn_w_up"][l] = dw_up
        per_layer["ffn_b_up"][l] = db_up[0]
        dz1, dz1b, dg1, db1 = _dgrad_ln_bwd(dhf, wb["ffn_w_up"][l], dz2, sv["z1"], vec("ln1_g"), "ffn_up_dgrad_ln")
        per_layer["ln1_g"][l] = dg1[0]
        per_layer["ln1_b"][l] = db1[0]
        dw_out, db_out = _wgrad(sv["cat"], dz1b, 512, "outproj_wgrad")
        per_layer["w_out"][l] = dw_out
        per_layer["b_out"][l] = db_out[0]
        dco, dgo, *perm = _outproj_dgrad(dz1b, wb["w_out"][l], sv["attn_out"], sv["lse"])
        d_outs, stats = perm[:len(DILATIONS)], perm[len(DILATIONS):]
        d_c, dglg, dglb, dws, dbs = _gmlp_bwd(sv["c_in"], dgo, vec("gmlp_ln_g"), vec("gmlp_ln_b"), sp["gmlp_w_s"][l],
                                              sv["bs_t"])
        per_layer["gmlp_ln_g"][l] = dglg[0]
        per_layer["gmlp_ln_b"][l] = dglb[0]
        per_layer["gmlp_w_s"][l] = dws
        per_layer["gmlp_b_s"][l] = dbs[:, :GMLP_GROUPS].T
        dqkvs = []
        for p, d in enumerate(DILATIONS):
            dqkv, dbias = _attn_bwd_pattern(sv["qkv"][p], d_outs[p], stats[p], bias[p], d)
            dqkvs.append(dqkv)
            dbias_all.append(dbias)
        d_a, dcw, dcb, dclg, dclb = _conv_bwd(sv["a_in"], sv["hc"], dco, sv["conv_w"], vec("conv_ln_g"),
                                              vec("conv_ln_b"))
        per_layer["conv_dw_w"][l] = dcw[:CONV_WIDTH]
        per_layer["conv_dw_b"][l] = dcb[0]
        per_layer["conv_ln_g"][l] = dclg[0]
        per_layer["conv_ln_b"][l] = dclb[0]
        dh = _attn_bwd_merge(d_a, dqkvs, d_c)
        dw_in, db_in = _wgrad(sv["x"], dh, 640, "inproj_wgrad")
        per_layer["w_in"][l] = dw_in
        per_layer["b_in"][l] = db_in[0]
        if l > 0:
            pv = saved[l - 1]
            dz2, dz2b, dg2, db2 = _dgrad_ln_bwd(dh, wb["w_in"][l], dz1, pv["z2"], sp["ln2_g"][l - 1][None, :],
                                                "inproj_dgrad_ln")
        else:
            grad_x = _dgrad_ln_bwd(dh, wb["w_in"][l], dz1, None, None, "inproj_dgrad")
    for k, v in per_layer.items():
        grads[k] = v if k in BIG else jnp.stack(v)
    dbias_cat = jnp.stack(dbias_all)
    bk_cat = jnp.concatenate([buckets] * DEPTH, axis=0)
    grads["rel_bias_table"] = _bias_table_grad(dbias_cat, bk_cat)[:, :ATTN_HEADS]
    return loss, grad_x, grads


N_CHIPS = 4
BIG = {"w_in": (D_MODEL, IN_CH, 1), "w_out": (D_MODEL, D_MODEL, 0),
       "ffn_w_up": (D_MODEL, 2 * D_FF, 1), "ffn_w_down": (D_FF, D_MODEL, 0)}
SMALL = ("b_in", "conv_dw_w", "conv_dw_b", "conv_ln_g", "conv_ln_b", "rel_bias_table", "gmlp_ln_g", "gmlp_ln_b",
         "gmlp_w_s", "gmlp_b_s", "b_out", "ln1_g", "ln1_b", "ffn_b_up", "ffn_conv_w", "ffn_conv_b", "ffn_b_down",
         "ln2_g", "ln2_b")
SMALL_SHARDED = ("conv_dw_w", "ffn_conv_w")
WEIGHTS = ("w_in", "b_in", "conv_dw_w", "conv_dw_b", "conv_ln_g", "conv_ln_b", "rel_bias_table", "gmlp_ln_g",
           "gmlp_ln_b", "gmlp_w_s", "gmlp_b_s", "w_out", "b_out", "ln1_g", "ln1_b", "ffn_w_up", "ffn_b_up",
           "ffn_conv_w", "ffn_conv_b", "ffn_w_down", "ffn_b_down", "ln2_g", "ln2_b")
ANY = pl.BlockSpec(memory_space=pl.ANY)


def _position():
    return lax.axis_index("x"), lax.axis_index("y"), lax.axis_index("c")


def _other_chips(x, y):
    return [(1 - x, y), (x, 1 - y), (1 - x, 1 - y)]


def _cast_bf16(a):
    R, C = a.shape
    T = 128

    def body(a_ref, o_ref):
        o_ref[...] = a_ref[...].astype(bf16)

    return pl.pallas_call(
        body, grid=(R // T,), name="cast_bf16", out_shape=jax.ShapeDtypeStruct((R, C), bf16),
        in_specs=[pl.BlockSpec((T, C), lambda i: (i, 0))], out_specs=pl.BlockSpec((T, C), lambda i: (i, 0)),
        compiler_params=_cp(("parallel",), 16),
    )(a)


def _chip_slot(ref, name, p):
    K, N, ax = BIG[name]
    if ax == 1:
        sz = N // N_CHIPS
        return ref.at[:, :, pl.ds(pl.multiple_of(p * sz, 128), sz)]
    sz = K // N_CHIPS
    return ref.at[:, pl.ds(pl.multiple_of(p * sz, 16), sz), :]


def _gather_weights(shards, conv_w, fconv_w):
    names = list(BIG)
    n_t = len(names) + 2

    def body(*refs):
        ins = refs[:n_t]
        outs = refs[n_t:2 * n_t]
        send_sems, recv_sems, local_sems = refs[2 * n_t:]
        x, y, c = _position()
        me = 2 * x + y
        chips = _other_chips(x, y)

        def slot(t, p):
            if t < len(names):
                return _chip_slot(outs[t], names[t], p)
            return outs[t].at[p]

        locs, cps = [], []
        for t in range(n_t):
            loc = pltpu.make_async_copy(ins[t], slot(t, me), local_sems.at[t])
            loc.start()
            locs.append(loc)
            for k, (px, py) in enumerate(chips):
                cp = pltpu.make_async_remote_copy(
                    src_ref=ins[t], dst_ref=slot(t, me), send_sem=send_sems.at[3 * t + k],
                    recv_sem=recv_sems.at[3 * t + k], device_id=(px, py, c), device_id_type=MESH_ID)
                cp.start()
                cps.append(cp)
        for t in range(n_t):
            for k, (px, py) in enumerate(chips):
                pltpu.make_async_remote_copy(
                    src_ref=ins[t], dst_ref=slot(t, 2 * px + py), send_sem=send_sems.at[3 * t + k],
                    recv_sem=recv_sems.at[3 * t + k], device_id=(px, py, c), device_id_type=MESH_ID).wait_recv()
        for cp in cps:
            cp.wait_send()
        for loc in locs:
            loc.wait()

    ins = [shards[n] for n in names] + [conv_w, fconv_w]
    out_shape = [jax.ShapeDtypeStruct((DEPTH, BIG[n][0], BIG[n][1]), bf16) for n in names]
    out_shape += [jax.ShapeDtypeStruct((N_CHIPS,) + conv_w.shape, f32), jax.ShapeDtypeStruct((N_CHIPS,) + fconv_w.shape, f32)]
    outs = pl.pallas_call(
        body, name="gather_weights", out_shape=tuple(out_shape), in_specs=[ANY] * n_t, out_specs=tuple([ANY] * n_t),
        scratch_shapes=[pltpu.SemaphoreType.DMA((3 * n_t,)), pltpu.SemaphoreType.DMA((3 * n_t,)),
                        pltpu.SemaphoreType.DMA((n_t,))],
    )(*ins)
    return dict(zip(names, outs[:len(names)])), outs[-2], outs[-1]


def _half(ref, name, c):
    K, N, ax = BIG[name]
    if ax == 1:
        return ref.at[pl.ds(pl.multiple_of(c * (K // 2), 8), K // 2), :]
    return ref.at[:, pl.ds(pl.multiple_of(c * (N // 2), 128), N // 2)]


def _half_shape(name):
    K, N, ax = BIG[name]
    return (K // 2, N) if ax == 1 else (K, N // 2)


def _shard_of_half(ref, name, q):
    K, N, ax = BIG[name]
    if ax == 1:
        sz = N // N_CHIPS
        return ref.at[:, pl.ds(pl.multiple_of(q * sz, 128), sz)]
    sz = K // N_CHIPS
    return ref.at[pl.ds(pl.multiple_of(q * sz, 8), sz), :]


def _shard_half_shape(name):
    K, N, ax = BIG[name]
    return (K // 2, N // N_CHIPS) if ax == 1 else (K // N_CHIPS, N // 2)


def _shard_shape(name):
    K, N, ax = BIG[name]
    return (K, N // N_CHIPS) if ax == 1 else (K // N_CHIPS, N)


def _place_in_shard(ref, name, l, c):
    K, N, ax = BIG[name]
    if ax == 1:
        return ref.at[l, pl.ds(pl.multiple_of(c * (K // 2), 8), K // 2), :]
    return ref.at[l, :, pl.ds(pl.multiple_of(c * (N // 2), 128), N // 2)]


def _pair_exchange(tensors):
    n_t = len(tensors)

    def body(*refs):
        ins = refs[:n_t]
        outs = refs[n_t:2 * n_t]
        send_sems, recv_sems = refs[2 * n_t:]
        x, y, c = _position()
        cps = []
        for t, (name, _) in enumerate(tensors):
            cp = pltpu.make_async_remote_copy(
                src_ref=_half(ins[t], name, 1 - c), dst_ref=outs[t], send_sem=send_sems.at[t],
                recv_sem=recv_sems.at[t], device_id=(x, y, 1 - c), device_id_type=MESH_ID)
            cp.start()
            cps.append(cp)
        for cp in cps:
            cp.wait()

    return pl.pallas_call(
        body, name="grad_pair_exchange",
        out_shape=tuple(jax.ShapeDtypeStruct(_half_shape(n), f32) for n, _ in tensors),
        in_specs=[ANY] * n_t, out_specs=tuple([ANY] * n_t),
        scratch_shapes=[pltpu.SemaphoreType.DMA((n_t,)), pltpu.SemaphoreType.DMA((n_t,))],
    )(*[g for _, g in tensors])


def _pair_add(g, rcv, name, c_arr):
    K, N, ax = BIG[name]
    hr, hc = _half_shape(name)
    T = 128
    nrt = hr // T

    def body(c_ref, g_ref, r_ref, o_ref):
        o_ref[...] = g_ref[...] + r_ref[...]

    if ax == 1:
        g_spec = pl.BlockSpec((T, hc), lambda i, c: (c[0] * nrt + i, 0))
    else:
        g_spec = pl.BlockSpec((T, hc), lambda i, c: (i, c[0]))
    plain = pl.BlockSpec((T, hc), lambda i, c: (i, 0))
    return pl.pallas_call(
        body, name="grad_pair_add", out_shape=jax.ShapeDtypeStruct((hr, hc), f32),
        grid_spec=pltpu.PrefetchScalarGridSpec(num_scalar_prefetch=1, grid=(nrt,), in_specs=[g_spec, plain],
                                               out_specs=plain),
        compiler_params=_cp(("parallel",), 32),
    )(c_arr, g, rcv)


def _chip_exchange(tensors):
    n_t = len(tensors)

    def body(*refs):
        ins = refs[:n_t]
        outs = refs[n_t:2 * n_t]
        send_sems, recv_sems, local_sems = refs[2 * n_t:]
        x, y, c = _position()
        me = 2 * x + y
        chips = _other_chips(x, y)
        locs, cps = [], []
        for t, (name, _) in enumerate(tensors):
            loc = pltpu.make_async_copy(_shard_of_half(ins[t], name, me), outs[t].at[me], local_sems.at[t])
            loc.start()
            locs.append(loc)
            for k, (px, py) in enumerate(chips):
                cp = pltpu.make_async_remote_copy(
                    src_ref=_shard_of_half(ins[t], name, 2 * px + py), dst_ref=outs[t].at[me],
                    send_sem=send_sems.at[3 * t + k], recv_sem=recv_sems.at[3 * t + k],
                    device_id=(px, py, c), device_id_type=MESH_ID)
                cp.start()
                cps.append(cp)
        for t, (name, _) in enumerate(tensors):
            for k, (px, py) in enumerate(chips):
                pltpu.make_async_remote_copy(
                    src_ref=_shard_of_half(ins[t], name, 2 * px + py), dst_ref=outs[t].at[2 * px + py],
                    send_sem=send_sems.at[3 * t + k], recv_sem=recv_sems.at[3 * t + k],
                    device_id=(px, py, c), device_id_type=MESH_ID).wait_recv()
        for cp in cps:
            cp.wait_send()
        for loc in locs:
            loc.wait()

    return pl.pallas_call(
        body, name="grad_chip_exchange",
        out_shape=tuple(jax.ShapeDtypeStruct((N_CHIPS,) + _shard_half_shape(n), f32) for n, _ in tensors),
        in_specs=[ANY] * n_t, out_specs=tuple([ANY] * n_t),
        scratch_shapes=[pltpu.SemaphoreType.DMA((3 * n_t,)), pltpu.SemaphoreType.DMA((3 * n_t,)),
                        pltpu.SemaphoreType.DMA((n_t,))],
    )(*[g for _, g in tensors])


def _sum_chips(parts):
    _, R, C = parts.shape
    T = 64

    def body(p_ref, o_ref):
        o_ref[...] = ((p_ref[0] + p_ref[1]) + p_ref[2]) + p_ref[3]

    return pl.pallas_call(
        body, grid=(R // T,), name="grad_sum_chips", out_shape=jax.ShapeDtypeStruct((R, C), f32),
        in_specs=[pl.BlockSpec((N_CHIPS, T, C), lambda i: (0, i, 0))], out_specs=pl.BlockSpec((T, C), lambda i: (i, 0)),
        compiler_params=_cp(("parallel",), 32),
    )(parts)


def _pair_gather(tensors):
    n_t = len(tensors)
    names = list(BIG)

    def body(*refs):
        ins = refs[:n_t]
        outs = dict(zip(names, refs[n_t:n_t + len(names)]))
        send_sems, recv_sems, local_sems = refs[n_t + len(names):]
        x, y, c = _position()
        locs, cps = [], []
        for t, (name, l, _) in enumerate(tensors):
            loc = pltpu.make_async_copy(ins[t], _place_in_shard(outs[name], name, l, c), local_sems.at[t])
            loc.start()
            locs.append(loc)
            cp = pltpu.make_async_remote_copy(
                src_ref=ins[t], dst_ref=_place_in_shard(outs[name], name, l, c), send_sem=send_sems.at[t],
                recv_sem=recv_sems.at[t], device_id=(x, y, 1 - c), device_id_type=MESH_ID)
            cp.start()
            cps.append(cp)
        for t, (name, l, _) in enumerate(tensors):
            pltpu.make_async_remote_copy(
                src_ref=ins[t], dst_ref=_place_in_shard(outs[name], name, l, 1 - c), send_sem=send_sems.at[t],
                recv_sem=recv_sems.at[t], device_id=(x, y, 1 - c), device_id_type=MESH_ID).wait_recv()
        for cp in cps:
            cp.wait_send()
        for loc in locs:
            loc.wait()

    outs = pl.pallas_call(
        body, name="grad_pair_gather",
        out_shape=tuple(jax.ShapeDtypeStruct((DEPTH,) + _shard_shape(n), f32) for n in names),
        in_specs=[ANY] * n_t, out_specs=tuple([ANY] * len(names)),
        scratch_shapes=[pltpu.SemaphoreType.DMA((n_t,)), pltpu.SemaphoreType.DMA((n_t,)),
                        pltpu.SemaphoreType.DMA((n_t,))],
    )(*[g for _, _, g in tensors])
    return dict(zip(names, outs))


def _reduce_big_grads(grads):
    c_arr = jnp.reshape(lax.axis_index("c"), (1,)).astype(jnp.int32)
    tensors = [(n, grads[n][l]) for n in BIG for l in range(DEPTH)]
    layers = [l for n in BIG for l in range(DEPTH)]
    received = _pair_exchange(tensors)
    pair = [(n, _pair_add(g, r, n, c_arr)) for (n, g), r in zip(tensors, received)]
    parts = _chip_exchange(pair)
    reduced = [(n, l, _sum_chips(p)) for (n, _), l, p in zip(pair, layers, parts)]
    return _pair_gather(reduced)


def _small_allreduce(buf):
    R = buf.shape[0]
    n_dev = 8

    def body(in_ref, out_ref, slots, send_sems, recv_sems):
        x, y, c = _position()
        me = 4 * x + 2 * y + c
        slots[me] = in_ref[...]
        peers = []
        for k in range(1, n_dev):
            px = 1 - x if k & 4 else x
            py = 1 - y if k & 2 else y
            pc = 1 - c if k & 1 else c
            peers.append((px, py, pc))
        cps = []
        for k, peer in enumerate(peers):
            cp = pltpu.make_async_remote_copy(
                src_ref=in_ref, dst_ref=slots.at[me], send_sem=send_sems.at[k], recv_sem=recv_sems.at[k],
                device_id=peer, device_id_type=MESH_ID)
            cp.start()
            cps.append(cp)
        for k, (px, py, pc) in enumerate(peers):
            pltpu.make_async_remote_copy(
                src_ref=in_ref, dst_ref=slots.at[4 * px + 2 * py + pc], send_sem=send_sems.at[k],
                recv_sem=recv_sems.at[k], device_id=(px, py, pc), device_id_type=MESH_ID).wait_recv()
        for cp in cps:
            cp.wait_send()
        acc = slots[0]
        for dv in range(1, n_dev):
            acc = acc + slots[dv]
        out_ref[...] = acc

    vm = pl.BlockSpec(memory_space=pltpu.VMEM)
    return pl.pallas_call(
        body, name="small_allreduce", out_shape=jax.ShapeDtypeStruct((R, 128), f32), in_specs=[vm], out_specs=vm,
        scratch_shapes=[pltpu.VMEM((n_dev, R, 128), f32), pltpu.SemaphoreType.DMA((n_dev - 1,)),
                        pltpu.SemaphoreType.DMA((n_dev - 1,))],
        compiler_params=pltpu.CompilerParams(vmem_limit_bytes=40 * MIB),
    )(buf)


PACK_UNIT = 1024


def _pack(arrs):
    parts = []
    for a in arrs:
        flat = a.reshape(-1)
        n = -(-flat.shape[0] // PACK_UNIT) * PACK_UNIT
        parts.append(jnp.pad(flat, (0, n - flat.shape[0])))
    return jnp.concatenate(parts).reshape(-1, 128)


def _unpack(buf, shapes):
    flat = buf.reshape(-1)
    out, off = [], 0
    for shp in shapes:
        n = int(np.prod(shp))
        out.append(flat[off:off + n].reshape(shp))
        off += -(-n // PACK_UNIT) * PACK_UNIT
    return out


def _adamw_rows(g, w, m, v, name):
    shp = g.shape
    C = shp[-1]
    outs = _adamw(g.reshape(-1, C), w.reshape(-1, C), m.reshape(-1, C), v.reshape(-1, C), name)
    return [o.reshape(shp) for o in outs]


def kernel(x, w_in, b_in, conv_dw_w, conv_dw_b, conv_ln_g, conv_ln_b, rel_bias_table, gmlp_ln_g, gmlp_ln_b, gmlp_w_s, gmlp_b_s, w_out, b_out, ln1_g, ln1_b, ffn_w_up, ffn_b_up, ffn_conv_w, ffn_conv_b, ffn_w_down, ffn_b_down, ln2_g, ln2_b, loss_target, m_w_in, m_b_in, m_conv_dw_w, m_conv_dw_b, m_conv_ln_g, m_conv_ln_b, m_rel_bias_table, m_gmlp_ln_g, m_gmlp_ln_b, m_gmlp_w_s, m_gmlp_b_s, m_w_out, m_b_out, m_ln1_g, m_ln1_b, m_ffn_w_up, m_ffn_b_up, m_ffn_conv_w, m_ffn_conv_b, m_ffn_w_down, m_ffn_b_down, m_ln2_g, m_ln2_b, v_w_in, v_b_in, v_conv_dw_w, v_conv_dw_b, v_conv_ln_g, v_conv_ln_b, v_rel_bias_table, v_gmlp_ln_g, v_gmlp_ln_b, v_gmlp_w_s, v_gmlp_b_s, v_w_out, v_b_out, v_ln1_g, v_ln1_b, v_ffn_w_up, v_ffn_b_up, v_ffn_conv_w, v_ffn_conv_b, v_ffn_w_down, v_ffn_b_down, v_ln2_g, v_ln2_b):
    w = dict(w_in=w_in, b_in=b_in, conv_dw_w=conv_dw_w, conv_dw_b=conv_dw_b, conv_ln_g=conv_ln_g, conv_ln_b=conv_ln_b,
             rel_bias_table=rel_bias_table, gmlp_ln_g=gmlp_ln_g, gmlp_ln_b=gmlp_ln_b, gmlp_w_s=gmlp_w_s,
             gmlp_b_s=gmlp_b_s, w_out=w_out, b_out=b_out, ln1_g=ln1_g, ln1_b=ln1_b, ffn_w_up=ffn_w_up,
             ffn_b_up=ffn_b_up, ffn_conv_w=ffn_conv_w, ffn_conv_b=ffn_conv_b, ffn_w_down=ffn_w_down,
             ffn_b_down=ffn_b_down, ln2_g=ln2_g, ln2_b=ln2_b)
    m = dict(w_in=m_w_in, b_in=m_b_in, conv_dw_w=m_conv_dw_w, conv_dw_b=m_conv_dw_b, conv_ln_g=m_conv_ln_g,
             conv_ln_b=m_conv_ln_b, rel_bias_table=m_rel_bias_table, gmlp_ln_g=m_gmlp_ln_g, gmlp_ln_b=m_gmlp_ln_b,
             gmlp_w_s=m_gmlp_w_s, gmlp_b_s=m_gmlp_b_s, w_out=m_w_out, b_out=m_b_out, ln1_g=m_ln1_g, ln1_b=m_ln1_b,
             ffn_w_up=m_ffn_w_up, ffn_b_up=m_ffn_b_up, ffn_conv_w=m_ffn_conv_w, ffn_conv_b=m_ffn_conv_b,
             ffn_w_down=m_ffn_w_down, ffn_b_down=m_ffn_b_down, ln2_g=m_ln2_g, ln2_b=m_ln2_b)
    v = dict(w_in=v_w_in, b_in=v_b_in, conv_dw_w=v_conv_dw_w, conv_dw_b=v_conv_dw_b, conv_ln_g=v_conv_ln_g,
             conv_ln_b=v_conv_ln_b, rel_bias_table=v_rel_bias_table, gmlp_ln_g=v_gmlp_ln_g, gmlp_ln_b=v_gmlp_ln_b,
             gmlp_w_s=v_gmlp_w_s, gmlp_b_s=v_gmlp_b_s, w_out=v_w_out, b_out=v_b_out, ln1_g=v_ln1_g, ln1_b=v_ln1_b,
             ffn_w_up=v_ffn_w_up, ffn_b_up=v_ffn_b_up, ffn_conv_w=v_ffn_conv_w, ffn_conv_b=v_ffn_conv_b,
             ffn_w_down=v_ffn_w_down, ffn_b_down=v_ffn_b_down, ln2_g=v_ln2_g, ln2_b=v_ln2_b)

    shards = {n: _cast_bf16(w[n].reshape(-1, w[n].shape[-1])).reshape(w[n].shape) for n in BIG}
    wb, conv_stack, fconv_stack = _gather_weights(shards, conv_dw_w, ffn_conv_w)
    sp = {n: w[n] for n in SMALL}
    sp["conv_dw_w"] = jnp.moveaxis(conv_stack, 0, 2).reshape(DEPTH, CONV_WIDTH, CONV_CH)
    sp["ffn_conv_w"] = jnp.moveaxis(fconv_stack, 0, 2).reshape(DEPTH, FFN_CONV_WIDTH, 2 * D_FF)

    loss_local, grad_x, grads = _local_step(x[0], loss_target[0], wb, sp)
    loss = lax.psum(loss_local, ("x", "y", "c"))

    big = _reduce_big_grads(grads)
    small_shapes = [grads[n].shape for n in SMALL]
    small = dict(zip(SMALL, _unpack(_small_allreduce(_pack([grads[n] for n in SMALL])), small_shapes)))
    chip = 2 * lax.axis_index("x") + lax.axis_index("y")
    for n in SMALL_SHARDED:
        width = w[n].shape[-1]
        small[n] = lax.dynamic_slice_in_dim(small[n], chip * width, width, axis=2)

    g_out, d_out, m_out, v_out = {}, {}, {}, {}
    for n in BIG:
        g_out[n] = big[n]
        d_out[n], m_out[n], v_out[n] = _adamw_rows(big[n], w[n], m[n], v[n], "adamw_" + n)
    shapes = [small[n].shape for n in SMALL]
    packed = [_pack([src[n] for n in SMALL]) for src in (small, w, m, v)]
    upd = _adamw(*packed, "adamw_small")
    for dst, buf in zip((d_out, m_out, v_out), upd):
        dst.update(zip(SMALL, _unpack(buf, shapes)))
    g_out.update(small)

    return (loss, grad_x[None], *[g_out[n] for n in WEIGHTS], *[d_out[n] for n in WEIGHTS],
            *[m_out[n] for n in WEIGHTS], *[v_out[n] for n in WEIGHTS])
```

```python
import functools
import math

import numpy as np
import jax
import jax.numpy as jnp
from jax import lax
from jax.experimental import pallas as pl
from jax.experimental.pallas import tpu as pltpu

f32 = jnp.float32
bf16 = jnp.bfloat16

D_MODEL = 1024
DEPTH = 2
HEAD_DIM = 64
CONV_CH = 256
CONV_WIDTH = 31
ATTN_HEADS = 8
ATTN_CH = ATTN_HEADS * HEAD_DIM
DILATIONS = (1, 4, 16)
ATTN_BLOCK = 128
N_BUCKETS = 32
MAX_DISTANCE = 2048
GMLP_CH = 256
GMLP_GROUPS = 4
GMLP_GROUP_DIM = GMLP_CH // GMLP_GROUPS
CHUNK = 128
IN_CH = 2 * CONV_CH + 3 * ATTN_CH + 2 * GMLP_CH
D_FF = 2816
FFN_CONV_WIDTH = 3
LN_EPS = 1e-5
ALPHA = (2.0 * DEPTH) ** 0.25
ADAM_LR = 0.001
ADAM_B1 = 0.9
ADAM_B2 = 0.999
ADAM_EPS = 1e-08
ADAM_WD = 0.01
ADAM_STEP = 10

CONV_HALO = 32
FFN_HALO = 8
NEG = -1e30
MIB = 2 ** 20
NT_DIMS = (((1,), (1,)), ((), ()))
TN_DIMS = (((0,), (0,)), ((), ()))
MESH_ID = pl.DeviceIdType.MESH


def _cp(sem, vmem_mib):
    return pltpu.CompilerParams(dimension_semantics=sem, vmem_limit_bytes=vmem_mib * MIB)


def _resident(shape):
    nd = len(shape)
    return pl.BlockSpec(shape, lambda *_: (0,) * nd, pipeline_mode=pl.Buffered(1))


def _acc(shape):
    nd = len(shape)
    return pl.BlockSpec(shape, lambda *_: (0,) * nd)


def _sig(x):
    return 1.0 / (1.0 + jnp.exp(-x))


def _ln_stats(z):
    mu = jnp.mean(z, axis=-1, keepdims=True)
    zc = z - mu
    var = jnp.mean(zc * zc, axis=-1, keepdims=True)
    rstd = lax.rsqrt(var + LN_EPS)
    return zc * rstd, rstd


def _ln_bwd(dy, xhat, rstd, g):
    dxh = dy * g
    m1 = jnp.mean(dxh, axis=-1, keepdims=True)
    m2 = jnp.mean(dxh * xhat, axis=-1, keepdims=True)
    return rstd * (dxh - m1 - xhat * m2)


def _colsum(x):
    return jnp.sum(x, axis=0, keepdims=True)


def _t5_bucket_np(dist):
    max_exact = N_BUCKETS // 2
    dd = np.maximum(dist, 1).astype(np.float64)
    large = max_exact + (np.log(dd / max_exact) / math.log(MAX_DISTANCE / max_exact)
                         * (N_BUCKETS - max_exact)).astype(np.int32)
    large = np.minimum(large, N_BUCKETS - 1)
    return np.where(dist < max_exact, dist, large).astype(np.int32)


def _bucket_ids():
    qi = np.arange(ATTN_BLOCK)[:, None]
    kj = np.arange(2 * ATTN_BLOCK)[None, :]
    dist = np.clip(qi + ATTN_BLOCK - kj, 0, None)
    return np.stack([_t5_bucket_np(dist * d) for d in DILATIONS]).astype(np.int32)


LANES = 128
QKV_CH = 3 * ATTN_CH
PERM_TILE = 512


def _slabs(n, rows):
    return [pltpu.VMEM((rows, LANES), f32)] * n


def _rows_of(slab, r, n, d):
    return slab[...] if d == 1 else slab[pl.ds(r, n, stride=d), :]


def _set_rows_of(slab, r, n, d, val):
    if d == 1:
        slab[...] = val
    else:
        slab[pl.ds(r, n, stride=d), :] = val


def _perm_spec(d, ch):
    return pl.BlockSpec((d, PERM_TILE // d, ch), lambda i: (0, i, 0))


def _perm_shape(S, d, ch, dtype):
    return jax.ShapeDtypeStruct((d, S // d, ch), dtype)


def _inproj_fwd(x, w, b):
    S = x.shape[0]
    T = PERM_TILE
    nsl = QKV_CH // LANES

    def body(x_ref, w_ref, b_ref, a_ref, c_ref, *rest):
        q_refs = rest[:len(DILATIONS)]
        slabs = rest[len(DILATIONS):]
        h = jnp.dot(x_ref[...].astype(bf16), w_ref[...], preferred_element_type=f32) + b_ref[...]
        a_ref[...] = h[:, :2 * CONV_CH]
        q0 = 2 * CONV_CH
        c_ref[...] = h[:, q0 + QKV_CH:]
        for j in range(nsl):
            piece = h[:, q0 + LANES * j:q0 + LANES * (j + 1)]
            if LANES * j < ATTN_CH:
                piece = piece * (HEAD_DIM ** -0.5)
            slabs[j][...] = piece
        for d, q_ref in zip(DILATIONS, q_refs):
            for r in range(d):
                for j in range(nsl):
                    q_ref[r, :, LANES * j:LANES * (j + 1)] = _rows_of(slabs[j], r, T // d, d).astype(bf16)

    row = lambda c: pl.BlockSpec((T, c), lambda i: (i, 0))
    return pl.pallas_call(
        body, grid=(S // T,), name="inproj_fwd",
        out_shape=(jax.ShapeDtypeStruct((S, 2 * CONV_CH), f32), jax.ShapeDtypeStruct((S, 2 * GMLP_CH), f32))
        + tuple(_perm_shape(S, d, QKV_CH, bf16) for d in DILATIONS),
        in_specs=[row(D_MODEL), _resident((D_MODEL, IN_CH)), _resident((1, IN_CH))],
        out_specs=(row(2 * CONV_CH), row(2 * GMLP_CH)) + tuple(_perm_spec(d, QKV_CH) for d in DILATIONS),
        scratch_shapes=_slabs(nsl, T),
        compiler_params=_cp(("parallel",), 48),
    )(x, w, b)


def _conv_fwd(a_in, dw_w, dw_b, ln_g, ln_b):
    S = a_in.shape[0]
    T = 512
    hb = T // CONV_HALO

    def body(a_ref, halo_ref, w_ref, b_ref, g_ref, be_ref, out_ref, hc_ref, buf):
        i = pl.program_id(0)
        am = a_ref[...]
        ah = halo_ref[...]
        hgh = ah[:, :CONV_CH] * _sig(ah[:, CONV_CH:])
        buf[0:CONV_HALO, :] = jnp.where(i > 0, hgh, 0.0)
        buf[CONV_HALO:, :] = am[:, :CONV_CH] * _sig(am[:, CONV_CH:])
        acc = jnp.zeros((T, CONV_CH), f32) + b_ref[...]
        for k in range(CONV_WIDTH):
            acc = acc + w_ref[k:k + 1, :] * buf[pl.ds(CONV_HALO - (CONV_WIDTH - 1) + k, T), :]
        hc_ref[...] = acc
        xhat, _ = _ln_stats(acc)
        y = xhat * g_ref[...] + be_ref[...]
        out_ref[...] = (y * _sig(y)).astype(bf16)

    return pl.pallas_call(
        body, grid=(S // T,), name="conv_fwd",
        out_shape=(jax.ShapeDtypeStruct((S, CONV_CH), bf16), jax.ShapeDtypeStruct((S, CONV_CH), f32)),
        in_specs=[pl.BlockSpec((T, 2 * CONV_CH), lambda i: (i, 0)),
                  pl.BlockSpec((CONV_HALO, 2 * CONV_CH), lambda i: (jnp.maximum(i * hb - 1, 0), 0)),
                  _acc((32, CONV_CH)), _acc((1, CONV_CH)), _acc((1, CONV_CH)), _acc((1, CONV_CH))],
        out_specs=(pl.BlockSpec((T, CONV_CH), lambda i: (i, 0)), pl.BlockSpec((T, CONV_CH), lambda i: (i, 0))),
        scratch_shapes=[pltpu.VMEM((T + CONV_HALO, CONV_CH), f32)],
        compiler_params=_cp(("parallel",), 32),
    )(a_in, a_in, dw_w, dw_b, ln_g, ln_b)


def _bias_build(table, buckets):
    def body(t_ref, bk_ref, o_ref):
        h = pl.program_id(1)
        ids = bk_ref[0]
        acc = jnp.zeros((ATTN_BLOCK, 2 * ATTN_BLOCK), f32)
        for b in range(N_BUCKETS):
            acc = jnp.where(ids == b, t_ref[b, h], acc)
        o_ref[0, 0] = acc

    return pl.pallas_call(
        body, grid=(len(DILATIONS), ATTN_HEADS), name="bias_build",
        out_shape=jax.ShapeDtypeStruct((len(DILATIONS), ATTN_HEADS, ATTN_BLOCK, 2 * ATTN_BLOCK), f32),
        in_specs=[pl.BlockSpec(memory_space=pltpu.SMEM),
                  pl.BlockSpec((1, ATTN_BLOCK, 2 * ATTN_BLOCK), lambda p, h: (p, 0, 0))],
        out_specs=pl.BlockSpec((1, 1, ATTN_BLOCK, 2 * ATTN_BLOCK), lambda p, h: (p, h, 0, 0)),
        compiler_params=_cp(("arbitrary", "arbitrary"), 16),
    )(table, buckets)


def _head_tile(tile, h, col):
    lane_head = lax.broadcasted_iota(jnp.int32, tile.shape, 1) // 16
    return jnp.where(lane_head == h, col, tile)


HEAD_PAIRS = ATTN_HEADS // 2
UNITS_PER_BLOCK = ATTN_HEADS


def _attn_tile(L):
    return min(512, L)


def _band_mask(first_block, n):
    B = ATTN_BLOCK
    row = lax.broadcasted_iota(jnp.int32, (B, 2 * B), 0)
    col = lax.broadcasted_iota(jnp.int32, (B, 2 * B), 1)
    valid = (col >= row) & (col <= row + B)
    if first_block:
        valid = valid & ((col >= B) | (n > 0))
    return valid


def _head_lanes(a):
    lane = lax.broadcasted_iota(jnp.int32, (ATTN_BLOCK, LANES), 1)
    return (lane < HEAD_DIM) if a == 0 else (lane >= HEAD_DIM)


def _pair_keys(cur_ref, halo_ref, part, b, j):
    B = ATTN_BLOCK
    c0 = part * ATTN_CH + LANES * j
    own = cur_ref[B * b:B * (b + 1), c0:c0 + LANES]
    prev = halo_ref[:, LANES * j:LANES * (j + 1)] if b == 0 else cur_ref[B * (b - 1):B * b, c0:c0 + LANES]
    return jnp.concatenate([prev, own], axis=0)


def _attn_fwd_pattern(qkv, bias, d):
    _, L, _ = qkv.shape
    B = ATTN_BLOCK
    QB = _attn_tile(L)
    nsb = QB // B
    U = nsb * UNITS_PER_BLOCK

    def body(cur_ref, hk_ref, hv_ref, b_ref, o_ref, lse_ref, lg, pb):
        n = pl.program_id(1)
        for b in range(nsb):
            valid = _band_mask(b == 0, n)
            for j in range(HEAD_PAIRS):
                q2 = cur_ref[B * b:B * (b + 1), LANES * j:LANES * (j + 1)]
                k2 = _pair_keys(cur_ref, hk_ref, 1, b, j)
                for a in range(2):
                    u = (b * HEAD_PAIRS + j) * 2 + a
                    qm = jnp.where(_head_lanes(a), q2, jnp.zeros_like(q2))
                    logits = lax.dot_general(qm, k2, NT_DIMS, preferred_element_type=f32) + b_ref[2 * j + a]
                    lg[B * u:B * (u + 1), :] = jnp.where(valid, logits, NEG)
        m = jnp.max(lg[...], axis=1, keepdims=True)
        p = jnp.exp(lg[...] - m)
        s = jnp.sum(p, axis=1, keepdims=True)
        pb[...] = p.astype(bf16)
        lse = m + jnp.log(s)
        inv = 1.0 / s
        for b in range(nsb):
            tile = jnp.zeros((B, B), f32)
            for j in range(HEAD_PAIRS):
                v2 = _pair_keys(cur_ref, hv_ref, 2, b, j)
                outs = []
                for a in range(2):
                    u = (b * HEAD_PAIRS + j) * 2 + a
                    rows = slice(B * u, B * (u + 1))
                    outs.append(jnp.dot(pb[rows, :], v2, preferred_element_type=f32) * inv[rows])
                    tile = _head_tile(tile, 2 * j + a, lse[rows])
                o_ref[B * b:B * (b + 1), LANES * j:LANES * (j + 1)] = jnp.where(_head_lanes(0), outs[0], outs[1])
            lse_ref[B * b:B * (b + 1), :] = tile

    halo = lambda part: pl.BlockSpec((None, B, ATTN_CH), lambda r, n: (r, jnp.maximum(n * nsb - 1, 0), part))
    tile_spec = lambda c: pl.BlockSpec((None, QB, c), lambda r, n: (r, n, 0))
    return pl.pallas_call(
        body, grid=(d, L // QB), name=f"attn_fwd_d{d}",
        out_shape=(jax.ShapeDtypeStruct((d, L, ATTN_CH), f32), jax.ShapeDtypeStruct((d, L, B), f32)),
        in_specs=[tile_spec(QKV_CH), halo(1), halo(2), _resident((ATTN_HEADS, B, 2 * B))],
        out_specs=(tile_spec(ATTN_CH), tile_spec(B)),
        scratch_shapes=[pltpu.VMEM((U * B, 2 * B), f32), pltpu.VMEM((U * B, 2 * B), bf16)],
        compiler_params=_cp(("parallel", "parallel"), 40),
    )(qkv, qkv, qkv, bias)


def _attn_merge(parts):
    S = parts[0][0].shape[0] * parts[0][0].shape[1]
    T = PERM_TILE
    nsl = ATTN_CH // LANES
    n_p = len(DILATIONS)

    def body(*refs):
        ins = refs[:2 * n_p]
        out_ref, lse_ref = refs[2 * n_p:2 * n_p + 2]
        slabs = refs[2 * n_p + 2:]
        lses = []
        for p, d in enumerate(DILATIONS):
            o_ref, l_ref = ins[2 * p], ins[2 * p + 1]
            osl = slabs[p * (nsl + 1):p * (nsl + 1) + nsl]
            lsl = slabs[p * (nsl + 1) + nsl]
            for r in range(d):
                for j in range(nsl):
                    _set_rows_of(osl[j], r, T // d, d, o_ref[r, :, LANES * j:LANES * (j + 1)])
                _set_rows_of(lsl, r, T // d, d, l_ref[r])
            lses.append(lsl[...])
        big = functools.reduce(jnp.maximum, lses)
        ws = [jnp.exp(l - big) for l in lses]
        tot = functools.reduce(lambda a_, b_: a_ + b_, ws)
        lse_ref[...] = big + jnp.log(tot)
        ws = [w / tot for w in ws]
        for j in range(nsl):
            acc = jnp.zeros((T, LANES), f32)
            for p in range(n_p):
                wa = ws[p][:, 32 * j:32 * j + 1]
                wb = ws[p][:, 32 * j + 16:32 * j + 17]
                lane = lax.broadcasted_iota(jnp.int32, (T, LANES), 1)
                acc = acc + jnp.where(lane < HEAD_DIM, wa, wb) * slabs[p * (nsl + 1) + j][...]
            out_ref[:, LANES * j:LANES * (j + 1)] = acc.astype(bf16)

    in_specs, args = [], []
    for (o, l), d in zip(parts, DILATIONS):
        in_specs += [_perm_spec(d, ATTN_CH), _perm_spec(d, ATTN_BLOCK)]
        args += [o, l]
    row = lambda c: pl.BlockSpec((T, c), lambda i: (i, 0))
    return pl.pallas_call(
        body, grid=(S // T,), name="attn_merge",
        out_shape=(jax.ShapeDtypeStruct((S, ATTN_CH), bf16), jax.ShapeDtypeStruct((S, ATTN_BLOCK), f32)),
        in_specs=in_specs, out_specs=(row(ATTN_CH), row(ATTN_BLOCK)),
        scratch_shapes=_slabs(n_p * (nsl + 1), T),
        compiler_params=_cp(("parallel",), 40),
    )(*args)


def _attn_fwd(qkvs, bias):
    parts = [_attn_fwd_pattern(q, bias[p], d) for p, (q, d) in enumerate(zip(qkvs, DILATIONS))]
    return _attn_merge(parts)


def _tril_bf16(w):
    row = lax.broadcasted_iota(jnp.int32, (CHUNK, CHUNK), 0)
    col = lax.broadcasted_iota(jnp.int32, (CHUNK, CHUNK), 1)
    return jnp.where(col <= row, w, 0.0).astype(bf16)


def _gmlp_fwd(c_in, ln_g, ln_b, w_s, b_s_t):
    S = c_in.shape[0]
    T = 512

    def body(c_ref, g_ref, be_ref, w_ref, bs_ref, out_ref, mix):
        c = c_ref[...]
        xhat, _ = _ln_stats(c[:, GMLP_CH:])
        vb = (xhat * g_ref[...] + be_ref[...]).astype(bf16)
        for g in range(GMLP_GROUPS):
            wt = _tril_bf16(w_ref[g])
            cs = slice(GMLP_GROUP_DIM * g, GMLP_GROUP_DIM * (g + 1))
            for ci in range(T // CHUNK):
                rs = slice(CHUNK * ci, CHUNK * (ci + 1))
                mix[rs, cs] = jnp.dot(wt, vb[rs, cs], preferred_element_type=f32) + bs_ref[:, g:g + 1]
        out_ref[...] = (c[:, :GMLP_CH] * mix[...]).astype(bf16)

    return pl.pallas_call(
        body, grid=(S // T,), name="gmlp_fwd",
        out_shape=jax.ShapeDtypeStruct((S, GMLP_CH), bf16),
        in_specs=[pl.BlockSpec((T, 2 * GMLP_CH), lambda i: (i, 0)), _acc((1, GMLP_CH)), _acc((1, GMLP_CH)),
                  _acc((GMLP_GROUPS, CHUNK, CHUNK)), _acc((CHUNK, GMLP_GROUPS))],
        out_specs=pl.BlockSpec((T, GMLP_CH), lambda i: (i, 0)),
        scratch_shapes=[pltpu.VMEM((T, GMLP_CH), f32)],
        compiler_params=_cp(("parallel",), 32),
    )(c_in, ln_g, ln_b, w_s, b_s_t)


def _outproj_ln_fwd(conv_out, attn_out, gm_out, w, b, x, ln_g, ln_b):
    S = x.shape[0]
    T = 512

    def body(co_ref, ao_ref, go_ref, w_ref, b_ref, x_ref, g_ref, be_ref, cat_ref, z_ref, y_ref, yb_ref):
        cat = jnp.concatenate([co_ref[...], ao_ref[...], go_ref[...]], axis=1)
        cat_ref[...] = cat
        z = jnp.dot(cat, w_ref[...], preferred_element_type=f32) + b_ref[...] + ALPHA * x_ref[...]
        z_ref[...] = z
        xhat, _ = _ln_stats(z)
        y = xhat * g_ref[...] + be_ref[...]
        y_ref[...] = y
        yb_ref[...] = y.astype(bf16)

    row = lambda c: pl.BlockSpec((T, c), lambda i: (i, 0))
    return pl.pallas_call(
        body, grid=(S // T,), name="outproj_ln_fwd",
        out_shape=(jax.ShapeDtypeStruct((S, D_MODEL), bf16), jax.ShapeDtypeStruct((S, D_MODEL), f32),
                   jax.ShapeDtypeStruct((S, D_MODEL), f32), jax.ShapeDtypeStruct((S, D_MODEL), bf16)),
        in_specs=[row(CONV_CH), row(ATTN_CH), row(GMLP_CH), _resident((D_MODEL, D_MODEL)), _acc((1, D_MODEL)),
                  row(D_MODEL), _acc((1, D_MODEL)), _acc((1, D_MODEL))],
        out_specs=(row(D_MODEL), row(D_MODEL), row(D_MODEL), row(D_MODEL)),
        compiler_params=_cp(("parallel",), 40),
    )(conv_out, attn_out, gm_out, w, b, x, ln_g, ln_b)


GATE_ROWS = 32
GATE_COLS = 128
GATE_MM_COLS = 256
SUBLANES = 8


def _gate_cols(c0):
    return slice(c0, c0 + GATE_COLS), slice(D_FF + c0, D_FF + c0 + GATE_COLS)


def _bcast_rows(ref, k, cs):
    return jnp.broadcast_to(ref[k:k + 1, cs], (GATE_ROWS, GATE_COLS))


def _fold_rows(z):
    acc = z[0:SUBLANES]
    for r in range(SUBLANES, GATE_ROWS, SUBLANES):
        acc = acc + z[r:r + SUBLANES]
    return acc


def _ffn_up_gate_fwd(x1b, w, b, conv_w, conv_b):
    S = x1b.shape[0]
    T = 256
    H = FFN_HALO
    K = FFN_CONV_WIDTH

    def body(x_ref, w_ref, b_ref, cw_ref, cb_ref, hfb_ref, hc_ref, act_ref, hbuf, carry):
        @pl.when(pl.program_id(0) == 0)
        def _():
            carry[...] = jnp.zeros_like(carry)
        x = x_ref[...]
        for m0 in range(0, D_FF, GATE_MM_COLS):
            for cm in (slice(m0, m0 + GATE_MM_COLS), slice(D_FF + m0, D_FF + m0 + GATE_MM_COLS)):
                h = jnp.dot(x, w_ref[:, cm], preferred_element_type=f32) + b_ref[:, cm]
                hbuf[:, cm] = h
                hfb_ref[:, cm] = h.astype(bf16)
            for c0 in range(m0, m0 + GATE_MM_COLS, GATE_COLS):
                cols = _gate_cols(c0)
                wts = [[_bcast_rows(cw_ref, k, cs) for k in range(K)] + [_bcast_rows(cb_ref, 0, cs)] for cs in cols]

                def step(rg, tails, cols=cols, wts=wts):
                    rows = pl.ds(pl.multiple_of(rg * GATE_ROWS, GATE_ROWS), GATE_ROWS)
                    hc, new_tails = [], []
                    for cs, wt, tail in zip(cols, wts, tails):
                        h = hbuf[rows, cs]
                        ext = jnp.concatenate([tail, h], axis=0)
                        acc = wt[K] + wt[K - 1] * h
                        for back in range(1, K):
                            acc = acc + wt[K - 1 - back] * pltpu.roll(ext, back, 0)[H:]
                        hc_ref[rows, cs] = acc
                        hc.append(acc)
                        new_tails.append(h[GATE_ROWS - H:])
                    act_ref[rows, cols[0]] = (hc[0] * _sig(hc[0]) * hc[1]).astype(bf16)
                    return tuple(new_tails)

                tails = lax.fori_loop(0, T // GATE_ROWS, step, tuple(carry[:, cs] for cs in cols), unroll=True)
                for cs, tail in zip(cols, tails):
                    carry[:, cs] = tail

    row = lambda c: pl.BlockSpec((T, c), lambda i: (i, 0))
    return pl.pallas_call(
        body, grid=(S // T,), name="ffn_up_gate_fwd",
        out_shape=(jax.ShapeDtypeStruct((S, 2 * D_FF), bf16), jax.ShapeDtypeStruct((S, 2 * D_FF), f32),
                   jax.ShapeDtypeStruct((S, D_FF), bf16)),
        in_specs=[row(D_MODEL), _resident((D_MODEL, 2 * D_FF)), _acc((1, 2 * D_FF)), _acc((8, 2 * D_FF)),
                  _acc((1, 2 * D_FF))],
        out_specs=(row(2 * D_FF), row(2 * D_FF), row(D_FF)),
        scratch_shapes=[pltpu.VMEM((T, 2 * D_FF), f32), pltpu.VMEM((H, 2 * D_FF), f32)],
        compiler_params=_cp(("arbitrary",), 56),
    )(x1b, w, b, conv_w, conv_b)


def _ffn_down_ln_fwd(act, w, b, x1, ln_g, ln_b):
    S = act.shape[0]
    T = 512

    def body(a_ref, w_ref, b_ref, x_ref, g_ref, be_ref, z_ref, y_ref):
        z = jnp.dot(a_ref[...], w_ref[...], preferred_element_type=f32) + b_ref[...] + ALPHA * x_ref[...]
        z_ref[...] = z
        xhat, _ = _ln_stats(z)
        y_ref[...] = xhat * g_ref[...] + be_ref[...]

    row = lambda c: pl.BlockSpec((T, c), lambda i: (i, 0))
    return pl.pallas_call(
        body, grid=(S // T,), name="ffn_down_ln_fwd",
        out_shape=(jax.ShapeDtypeStruct((S, D_MODEL), f32), jax.ShapeDtypeStruct((S, D_MODEL), f32)),
        in_specs=[row(D_FF), _resident((D_FF, D_MODEL)), _acc((1, D_MODEL)), row(D_MODEL), _acc((1, D_MODEL)),
                  _acc((1, D_MODEL))],
        out_specs=(row(D_MODEL), row(D_MODEL)),
        compiler_params=_cp(("parallel",), 40),
    )(act, w, b, x1, ln_g, ln_b)


def _loss_ln_bwd(y, target, z, ln_g):
    S = y.shape[0]
    T = 512

    def body(y_ref, t_ref, z_ref, g_ref, dz_ref, dzb_ref, loss_ref, dg_ref, db_ref):
        @pl.when(pl.program_id(0) == 0)
        def _():
            loss_ref[...] = jnp.zeros_like(loss_ref)
            dg_ref[...] = jnp.zeros_like(dg_ref)
            db_ref[...] = jnp.zeros_like(db_ref)
        err = y_ref[...] - t_ref[...]
        loss_ref[...] += _colsum(err * err) * (0.5 / D_MODEL)
        dy = err * (1.0 / D_MODEL)
        xhat, rstd = _ln_stats(z_ref[...])
        dz = _ln_bwd(dy, xhat, rstd, g_ref[...])
        dz_ref[...] = dz
        dzb_ref[...] = dz.astype(bf16)
        dg_ref[...] += _colsum(dy * xhat)
        db_ref[...] += _colsum(dy)

    row = pl.BlockSpec((T, D_MODEL), lambda i: (i, 0))
    vec = jax.ShapeDtypeStruct((1, D_MODEL), f32)
    return pl.pallas_call(
        body, grid=(S // T,), name="loss_ln_bwd",
        out_shape=(jax.ShapeDtypeStruct((S, D_MODEL), f32), jax.ShapeDtypeStruct((S, D_MODEL), bf16), vec, vec, vec),
        in_specs=[row, row, row, _acc((1, D_MODEL))],
        out_specs=(row, row, _acc((1, D_MODEL)), _acc((1, D_MODEL)), _acc((1, D_MODEL))),
        compiler_params=_cp(("arbitrary",), 40),
    )(y, target, z, ln_g)


def _dgrad_ln_bwd(g, w, dz_res, z, ln_g, name):
    S, K = g.shape
    T = 256
    with_ln = z is not None

    def body(*refs):
        if with_ln:
            g_ref, w_ref, r_ref, z_ref, lg_ref, dz_ref, dzb_ref, dg_ref, db_ref = refs
        else:
            g_ref, w_ref, r_ref, dx_ref = refs
        dx = lax.dot_general(g_ref[...], w_ref[...], NT_DIMS, preferred_element_type=f32) + ALPHA * r_ref[...]
        if not with_ln:
            dx_ref[...] = dx
            return

        @pl.when(pl.program_id(0) == 0)
        def _():
            dg_ref[...] = jnp.zeros_like(dg_ref)
            db_ref[...] = jnp.zeros_like(db_ref)
        xhat, rstd = _ln_stats(z_ref[...])
        dz = _ln_bwd(dx, xhat, rstd, lg_ref[...])
        dz_ref[...] = dz
        dzb_ref[...] = dz.astype(bf16)
        dg_ref[...] += _colsum(dx * xhat)
        db_ref[...] += _colsum(dx)

    row = pl.BlockSpec((T, D_MODEL), lambda i: (i, 0))
    vec = jax.ShapeDtypeStruct((1, D_MODEL), f32)
    in_specs = [pl.BlockSpec((T, K), lambda i: (i, 0)), _resident((D_MODEL, K)), row]
    args = [g, w, dz_res]
    if with_ln:
        in_specs += [row, _acc((1, D_MODEL))]
        args += [z, ln_g]
        out_shape = (jax.ShapeDtypeStruct((S, D_MODEL), f32), jax.ShapeDtypeStruct((S, D_MODEL), bf16), vec, vec)
        out_specs = (row, row, _acc((1, D_MODEL)), _acc((1, D_MODEL)))
    else:
        out_shape = jax.ShapeDtypeStruct((S, D_MODEL), f32)
        out_specs = row
    return pl.pallas_call(
        body, grid=(S // T,), name=name, out_shape=out_shape, in_specs=in_specs, out_specs=out_specs,
        compiler_params=_cp(("arbitrary",), 48),
    )(*args)


def _ffn_down_gate_bwd(dzb, w_down, hfb, hc, conv_w):
    S = hc.shape[0]
    T = 256
    H = FFN_HALO
    nt = S // T
    K = FFN_CONV_WIDTH

    def body(dz_ref, w_ref, h_ref, hc_ref, cw_ref, dh_ref, dw_ref, dcb_ref, da_buf, carry):
        @pl.when(pl.program_id(0) == 0)
        def _():
            dw_ref[...] = jnp.zeros_like(dw_ref)
            dcb_ref[...] = jnp.zeros_like(dcb_ref)
            carry[...] = jnp.zeros_like(carry)
        da_buf[...] = lax.dot_general(dz_ref[...], w_ref[...], NT_DIMS, preferred_element_type=f32)
        ngroups = T // GATE_ROWS
        for c0 in range(0, D_FF, GATE_COLS):
            cols = _gate_cols(c0)
            wts = [[_bcast_rows(cw_ref, k, cs) for k in range(K)] for cs in cols]

            def step(it, state, cols=cols, wts=wts):
                heads, accs = state
                rows = pl.ds(pl.multiple_of((ngroups - 1 - it) * GATE_ROWS, GATE_ROWS), GATE_ROWS)
                g = hc_ref[rows, cols[0]]
                v = hc_ref[rows, cols[1]]
                da = da_buf[rows, cols[0]]
                sg = _sig(g)
                dms = (da * v * (sg * (1.0 + g * (1.0 - sg))), da * (g * sg))
                new_heads, new_accs = [], []
                for cs, wt, dm, head, acc in zip(cols, wts, dms, heads, accs):
                    h0 = h_ref[rows, cs].astype(f32)
                    ext = jnp.concatenate([dm, head], axis=0)
                    dh = wt[K - 1] * dm
                    acc_k = [None] * K + [acc[K] + _fold_rows(dm)]
                    acc_k[K - 1] = acc[K - 1] + _fold_rows(dm * h0)
                    for ahead in range(1, K):
                        dk = pltpu.roll(ext, GATE_ROWS + H - ahead, 0)[:GATE_ROWS]
                        dh = dh + wt[K - 1 - ahead] * dk
                        acc_k[K - 1 - ahead] = acc[K - 1 - ahead] + _fold_rows(dk * h0)
                    dh_ref[rows, cs] = dh.astype(bf16)
                    new_heads.append(dm[:H])
                    new_accs.append(tuple(acc_k))
                return tuple(new_heads), tuple(new_accs)

            zero = jnp.zeros((SUBLANES, GATE_COLS), f32)
            init = (tuple(carry[:, cs] for cs in cols), tuple(tuple(zero for _ in range(K + 1)) for _ in cols))
            heads, accs = lax.fori_loop(0, ngroups, step, init, unroll=True)
            for cs, head, acc in zip(cols, heads, accs):
                carry[:, cs] = head
                dcb_ref[:, cs] += _colsum(acc[K])
                for k in range(K):
                    dw_ref[k:k + 1, cs] += _colsum(acc[k])

    tile = lambda c: pl.BlockSpec((T, c), lambda i: (nt - 1 - i, 0))
    return pl.pallas_call(
        body, grid=(nt,), name="ffn_down_gate_bwd",
        out_shape=(jax.ShapeDtypeStruct((S, 2 * D_FF), bf16), jax.ShapeDtypeStruct((8, 2 * D_FF), f32),
                   jax.ShapeDtypeStruct((1, 2 * D_FF), f32)),
        in_specs=[tile(D_MODEL), _resident((D_FF, D_MODEL)), tile(2 * D_FF), tile(2 * D_FF), _acc((8, 2 * D_FF))],
        out_specs=(tile(2 * D_FF), _acc((8, 2 * D_FF)), _acc((1, 2 * D_FF))),
        scratch_shapes=[pltpu.VMEM((T, D_FF), f32), pltpu.VMEM((H, 2 * D_FF), f32)],
        compiler_params=_cp(("arbitrary",), 48),
    )(dzb, w_down, hfb, hc, conv_w)


def _wgrad(a, g, tn, name):
    S, K = a.shape
    N = g.shape[1]
    T = 1024 if S % 1024 == 0 else S

    def body(a_ref, g_ref, dw_ref, db_ref):
        @pl.when(pl.program_id(1) == 0)
        def _():
            dw_ref[...] = jnp.zeros_like(dw_ref)
            db_ref[...] = jnp.zeros_like(db_ref)
        gt = g_ref[...]
        dw_ref[...] += lax.dot_general(a_ref[...].astype(bf16), gt, TN_DIMS, preferred_element_type=f32)
        db_ref[...] += _colsum(gt.astype(f32))

    return pl.pallas_call(
        body, grid=(N // tn, S // T), name=name,
        out_shape=(jax.ShapeDtypeStruct((K, N), f32), jax.ShapeDtypeStruct((1, N), f32)),
        in_specs=[pl.BlockSpec((T, K), lambda j, i: (i, 0)), pl.BlockSpec((T, tn), lambda j, i: (i, j))],
        out_specs=(pl.BlockSpec((K, tn), lambda j, i: (0, j)), pl.BlockSpec((1, tn), lambda j, i: (0, j))),
        compiler_params=_cp(("parallel", "arbitrary"), 48),
    )(a, g)


def _outproj_dgrad(dzb, w, attn_out, lse):
    S = dzb.shape[0]
    T = PERM_TILE
    nsl = ATTN_CH // LANES
    n_p = len(DILATIONS)

    def body(g_ref, w_ref, ao_ref, lse_ref, dco_ref, dgo_ref, *rest):
        do_refs = rest[:n_p]
        st_refs = rest[n_p:2 * n_p]
        slabs = rest[2 * n_p:]
        dcat = lax.dot_general(g_ref[...], w_ref[...], NT_DIMS, preferred_element_type=f32)
        dco_ref[...] = dcat[:, :CONV_CH]
        dgo_ref[...] = dcat[:, CONV_CH + ATTN_CH:]
        lane = lax.broadcasted_iota(jnp.int32, (T, LANES), 1)
        st = lse_ref[...]
        for j in range(nsl):
            dO = dcat[:, CONV_CH + LANES * j:CONV_CH + LANES * (j + 1)]
            prod = dO * ao_ref[:, LANES * j:LANES * (j + 1)].astype(f32)
            for a in range(2):
                in_head = (lane < HEAD_DIM) if a == 0 else (lane >= HEAD_DIM)
                delta = jnp.sum(jnp.where(in_head, prod, 0.0), axis=1, keepdims=True)
                st = jnp.where((lane // 16 == 2 * j + a) & (lane % 16 >= 8), delta, st)
            slabs[j][...] = dO
        slabs[nsl][...] = st
        for d, do_ref, st_ref in zip(DILATIONS, do_refs, st_refs):
            for r in range(d):
                for j in range(nsl):
                    do_ref[r, :, LANES * j:LANES * (j + 1)] = _rows_of(slabs[j], r, T // d, d).astype(bf16)
                st_ref[r] = _rows_of(slabs[nsl], r, T // d, d)

    row = lambda c: pl.BlockSpec((T, c), lambda i: (i, 0))
    return pl.pallas_call(
        body, grid=(S // T,), name="outproj_dgrad",
        out_shape=(jax.ShapeDtypeStruct((S, CONV_CH), f32), jax.ShapeDtypeStruct((S, GMLP_CH), f32))
        + tuple(_perm_shape(S, d, ATTN_CH, bf16) for d in DILATIONS)
        + tuple(_perm_shape(S, d, ATTN_BLOCK, f32) for d in DILATIONS),
        in_specs=[row(D_MODEL), _resident((D_MODEL, D_MODEL)), row(ATTN_CH), row(ATTN_BLOCK)],
        out_specs=(row(CONV_CH), row(GMLP_CH)) + tuple(_perm_spec(d, ATTN_CH) for d in DILATIONS)
        + tuple(_perm_spec(d, ATTN_BLOCK) for d in DILATIONS),
        scratch_shapes=_slabs(nsl + 1, T),
        compiler_params=_cp(("parallel",), 40),
    )(dzb, w, attn_out, lse)


def _gmlp_bwd(c_in, dgm, ln_g, ln_b, w_s, b_s_t):
    S = c_in.shape[0]
    T = 512
    nsteps = S // T

    def body(c_ref, dg_ref, g_ref, be_ref, w_ref, bs_ref, dc_ref, dlg_ref, dlb_ref, dw_ref, dbs_ref,
             du_buf, dv_buf, dm_acc):
        i = pl.program_id(0)

        @pl.when(i == 0)
        def _():
            dlg_ref[...] = jnp.zeros_like(dlg_ref)
            dlb_ref[...] = jnp.zeros_like(dlb_ref)
            dw_ref[...] = jnp.zeros_like(dw_ref)
            dm_acc[...] = jnp.zeros_like(dm_acc)
        c = c_ref[...]
        u = c[:, :GMLP_CH]
        xhat, rstd = _ln_stats(c[:, GMLP_CH:])
        vb = (xhat * g_ref[...] + be_ref[...]).astype(bf16)
        dgm_t = dg_ref[...]
        dm_all = dgm_t * u
        for g in range(GMLP_GROUPS):
            wt = _tril_bf16(w_ref[g])
            cs = slice(GMLP_GROUP_DIM * g, GMLP_GROUP_DIM * (g + 1))
            dw_g = jnp.zeros((CHUNK, CHUNK), f32)
            for ci in range(T // CHUNK):
                rs = slice(CHUNK * ci, CHUNK * (ci + 1))
                v_c = vb[rs, cs]
                mixed = jnp.dot(wt, v_c, preferred_element_type=f32) + bs_ref[:, g:g + 1]
                dm = dm_all[rs, cs]
                dmb = dm.astype(bf16)
                du_buf[rs, cs] = dgm_t[rs, cs] * mixed
                dv_buf[rs, cs] = lax.dot_general(wt, dmb, TN_DIMS, preferred_element_type=f32)
                dw_g = dw_g + lax.dot_general(dmb, v_c, NT_DIMS, preferred_element_type=f32)
                dm_acc[:, cs] += dm
            dw_ref[g] += dw_g
        dv = dv_buf[...]
        dvr = _ln_bwd(dv, xhat, rstd, g_ref[...])
        dlg_ref[...] += _colsum(dv * xhat)
        dlb_ref[...] += _colsum(dv)
        dc_ref[:, :GMLP_CH] = du_buf[...].astype(bf16)
        dc_ref[:, GMLP_CH:] = dvr.astype(bf16)

        @pl.when(i == nsteps - 1)
        def _():
            row = lax.broadcasted_iota(jnp.int32, (CHUNK, CHUNK), 0)
            col = lax.broadcasted_iota(jnp.int32, (CHUNK, CHUNK), 1)
            tile = jnp.zeros((CHUNK, CHUNK), f32)
            for g in range(GMLP_GROUPS):
                dw_ref[g] = jnp.where(col <= row, dw_ref[g], 0.0)
                gsum = jnp.sum(dm_acc[:, GMLP_GROUP_DIM * g:GMLP_GROUP_DIM * (g + 1)], axis=1, keepdims=True)
                tile = jnp.where(col == g, gsum, tile)
            dbs_ref[...] = tile

    vec = jax.ShapeDtypeStruct((1, GMLP_CH), f32)
    return pl.pallas_call(
        body, grid=(nsteps,), name="gmlp_bwd",
        out_shape=(jax.ShapeDtypeStruct((S, 2 * GMLP_CH), bf16), vec, vec,
                   jax.ShapeDtypeStruct((GMLP_GROUPS, CHUNK, CHUNK), f32), jax.ShapeDtypeStruct((CHUNK, CHUNK), f32)),
        in_specs=[pl.BlockSpec((T, 2 * GMLP_CH), lambda i: (i, 0)), pl.BlockSpec((T, GMLP_CH), lambda i: (i, 0)),
                  _acc((1, GMLP_CH)), _acc((1, GMLP_CH)), _acc((GMLP_GROUPS, CHUNK, CHUNK)), _acc((CHUNK, GMLP_GROUPS))],
        out_specs=(pl.BlockSpec((T, 2 * GMLP_CH), lambda i: (i, 0)), _acc((1, GMLP_CH)), _acc((1, GMLP_CH)),
                   _acc((GMLP_GROUPS, CHUNK, CHUNK)), _acc((CHUNK, CHUNK))),
        scratch_shapes=[pltpu.VMEM((T, GMLP_CH), f32), pltpu.VMEM((T, GMLP_CH), f32), pltpu.VMEM((CHUNK, GMLP_CH), f32)],
        compiler_params=_cp(("arbitrary",), 32),
    )(c_in, dgm, ln_g, ln_b, w_s, b_s_t)


def _attn_bwd_pattern(qkv, d_out, stats, bias, d):
    _, L, _ = qkv.shape
    B = ATTN_BLOCK
    QB = _attn_tile(L)
    nsb = QB // B
    nt = L // QB
    U = nsb * UNITS_PER_BLOCK
    KV = 2 * ATTN_CH

    def body(cur_ref, hk_ref, hv_ref, do_ref, st_ref, b_ref, dqkv_ref, dbias_ref, lg, dp, pb, dsb, dkv, carry):
        r = pl.program_id(0)
        i = pl.program_id(1)
        n = nt - 1 - i

        @pl.when((r == 0) & (i == 0))
        def _():
            dbias_ref[...] = jnp.zeros_like(dbias_ref)

        @pl.when(i == 0)
        def _():
            carry[...] = jnp.zeros_like(carry)

        def operands(b, j, a):
            rows = slice(B * b, B * (b + 1))
            q2 = cur_ref[rows, LANES * j:LANES * (j + 1)]
            do2 = do_ref[rows, LANES * j:LANES * (j + 1)]
            keep = _head_lanes(a)
            return jnp.where(keep, q2, jnp.zeros_like(q2)), jnp.where(keep, do2, jnp.zeros_like(do2))

        for b in range(nsb):
            valid = _band_mask(b == 0, n)
            for j in range(HEAD_PAIRS):
                k2 = _pair_keys(cur_ref, hk_ref, 1, b, j)
                v2 = _pair_keys(cur_ref, hv_ref, 2, b, j)
                for a in range(2):
                    u = (b * HEAD_PAIRS + j) * 2 + a
                    qm, dom = operands(b, j, a)
                    logits = lax.dot_general(qm, k2, NT_DIMS, preferred_element_type=f32) + b_ref[2 * j + a]
                    lg[B * u:B * (u + 1), :] = jnp.where(valid, logits, NEG)
                    dp[B * u:B * (u + 1), :] = lax.dot_general(dom, v2, NT_DIMS, preferred_element_type=f32)
        for b in range(nsb):
            for j in range(HEAD_PAIRS):
                for a in range(2):
                    u = (b * HEAD_PAIRS + j) * 2 + a
                    rows = slice(B * u, B * (u + 1))
                    lane0 = 32 * j + 16 * a
                    lse = st_ref[B * b:B * (b + 1), lane0:lane0 + 1]
                    delta = st_ref[B * b:B * (b + 1), lane0 + 8:lane0 + 9]
                    p = jnp.exp(lg[rows, :] - lse)
                    ds = p * (dp[rows, :] - delta)
                    pb[rows, :] = p.astype(bf16)
                    dsb[rows, :] = ds.astype(bf16)
                    dbias_ref[2 * j + a] += ds
        dkv[...] = jnp.zeros_like(dkv)
        for b in range(nsb):
            for j in range(HEAD_PAIRS):
                k2 = _pair_keys(cur_ref, hk_ref, 1, b, j)
                dq, dk2, dv2 = [], None, None
                for a in range(2):
                    u = (b * HEAD_PAIRS + j) * 2 + a
                    rows = slice(B * u, B * (u + 1))
                    qm, dom = operands(b, j, a)
                    ds_u = dsb[rows, :]
                    dq.append(jnp.dot(ds_u, k2, preferred_element_type=f32))
                    dk_u = lax.dot_general(ds_u, qm, TN_DIMS, preferred_element_type=f32)
                    dv_u = lax.dot_general(pb[rows, :], dom, TN_DIMS, preferred_element_type=f32)
                    dk2 = dk_u if dk2 is None else dk2 + dk_u
                    dv2 = dv_u if dv2 is None else dv2 + dv_u
                dq2 = jnp.where(_head_lanes(0), dq[0], dq[1]) * (HEAD_DIM ** -0.5)
                dqkv_ref[B * b:B * (b + 1), LANES * j:LANES * (j + 1)] = dq2.astype(bf16)
                dkv[B * b:B * (b + 2), LANES * j:LANES * (j + 1)] += dk2
                dkv[B * b:B * (b + 2), ATTN_CH + LANES * j:ATTN_CH + LANES * (j + 1)] += dv2
        dkv[QB:, :] += carry[...]
        dqkv_ref[:, ATTN_CH:] = dkv[B:, :].astype(bf16)
        carry[...] = dkv[0:B, :]

    halo = lambda part: pl.BlockSpec((None, B, ATTN_CH),
                                     lambda r, i: (r, jnp.maximum((nt - 1 - i) * nsb - 1, 0), part))
    tile_spec = lambda c: pl.BlockSpec((None, QB, c), lambda r, i: (r, nt - 1 - i, 0))
    return pl.pallas_call(
        body, grid=(d, nt), name=f"attn_bwd_d{d}",
        out_shape=(jax.ShapeDtypeStruct((d, L, QKV_CH), bf16), jax.ShapeDtypeStruct((ATTN_HEADS, B, 2 * B), f32)),
        in_specs=[tile_spec(QKV_CH), halo(1), halo(2), tile_spec(ATTN_CH), tile_spec(B),
                  _resident((ATTN_HEADS, B, 2 * B))],
        out_specs=(tile_spec(QKV_CH), _acc((ATTN_HEADS, B, 2 * B))),
        scratch_shapes=[pltpu.VMEM((U * B, 2 * B), f32), pltpu.VMEM((U * B, 2 * B), f32),
                        pltpu.VMEM((U * B, 2 * B), bf16), pltpu.VMEM((U * B, 2 * B), bf16),
                        pltpu.VMEM((B + QB, KV), f32), pltpu.VMEM((B, KV), f32)],
        compiler_params=_cp(("arbitrary", "arbitrary"), 48),
    )(qkv, qkv, qkv, d_out, stats, bias)


def _attn_bwd_merge(d_a, dqkvs, d_c):
    S = d_a.shape[0]
    T = PERM_TILE
    nsl = QKV_CH // LANES
    n_p = len(DILATIONS)

    def body(da_ref, *rest):
        g_refs = rest[:n_p]
        dc_ref, dh_ref = rest[n_p:n_p + 2]
        slabs = rest[n_p + 2:]
        q0 = 2 * CONV_CH
        dh_ref[:, :q0] = da_ref[...]
        dh_ref[:, q0 + QKV_CH:] = dc_ref[...]
        for p, (d, g_ref) in enumerate(zip(DILATIONS, g_refs)):
            for r in range(d):
                for j in range(nsl):
                    _set_rows_of(slabs[p * nsl + j], r, T // d, d, g_ref[r, :, LANES * j:LANES * (j + 1)].astype(f32))
        for j in range(nsl):
            acc = slabs[j][...]
            for p in range(1, n_p):
                acc = acc + slabs[p * nsl + j][...]
            dh_ref[:, q0 + LANES * j:q0 + LANES * (j + 1)] = acc.astype(bf16)

    row = lambda c: pl.BlockSpec((T, c), lambda i: (i, 0))
    return pl.pallas_call(
        body, grid=(S // T,), name="attn_bwd_merge", out_shape=jax.ShapeDtypeStruct((S, IN_CH), bf16),
        in_specs=[row(2 * CONV_CH)] + [_perm_spec(d, QKV_CH) for d in DILATIONS] + [row(2 * GMLP_CH)],
        out_specs=row(IN_CH), scratch_shapes=_slabs(n_p * nsl, T),
        compiler_params=_cp(("parallel",), 48),
    )(d_a, *dqkvs, d_c)


def _bias_table_grad(dbias, buckets):
    n = dbias.shape[0]

    def body(db_ref, bk_ref, o_ref):
        p = pl.program_id(0)
        h = pl.program_id(1)

        @pl.when((p == 0) & (h == 0))
        def _():
            o_ref[...] = jnp.zeros_like(o_ref)
        ids = bk_ref[0]
        db = db_ref[0, 0]
        row = lax.broadcasted_iota(jnp.int32, (N_BUCKETS, 128), 0)
        lane = lax.broadcasted_iota(jnp.int32, (N_BUCKETS, 128), 1)
        upd = jnp.zeros((N_BUCKETS, 128), f32)
        for b in range(N_BUCKETS):
            s = jnp.sum(jnp.sum(jnp.where(ids == b, db, 0.0), axis=1, keepdims=True), axis=0, keepdims=True)
            upd = jnp.where((row == b) & (lane == h), s, upd)
        o_ref[...] += upd

    return pl.pallas_call(
        body, grid=(n, ATTN_HEADS), name="bias_table_grad",
        out_shape=jax.ShapeDtypeStruct((N_BUCKETS, 128), f32),
        in_specs=[pl.BlockSpec((1, 1, ATTN_BLOCK, 2 * ATTN_BLOCK), lambda p, h: (p, h, 0, 0)),
                  pl.BlockSpec((1, ATTN_BLOCK, 2 * ATTN_BLOCK), lambda p, h: (p, 0, 0))],
        out_specs=_acc((N_BUCKETS, 128)),
        compiler_params=_cp(("arbitrary", "arbitrary"), 16),
    )(dbias, buckets)


def _conv_bwd(a_in, hc, dco, dw_w, ln_g, ln_b):
    S = a_in.shape[0]
    T = 512
    hb = T // CONV_HALO
    nsteps = S // T
    R = T + CONV_HALO
    K = CONV_WIDTH

    def body(ap_ref, a_ref, hc_ref, hcn_ref, d_ref, dn_ref, w_ref, g_ref, be_ref,
             da_ref, dw_ref, dcb_ref, dlg_ref, dlb_ref, hg_buf, ext, dbuf):
        i = pl.program_id(0)

        @pl.when(i == 0)
        def _():
            dw_ref[...] = jnp.zeros_like(dw_ref)
            dcb_ref[...] = jnp.zeros_like(dcb_ref)
            dlg_ref[...] = jnp.zeros_like(dlg_ref)
            dlb_ref[...] = jnp.zeros_like(dlb_ref)
        am = a_ref[...]
        ah = ap_ref[...]
        a = am[:, :CONV_CH]
        sg = _sig(am[:, CONV_CH:])
        hg_buf[0:CONV_HALO, :] = jnp.where(i > 0, ah[:, :CONV_CH] * _sig(ah[:, CONV_CH:]), 0.0)
        hg_buf[CONV_HALO:, :] = a * sg
        ext[0:T, :] = hc_ref[...]
        ext[T:, :] = hcn_ref[...]
        xhat, rstd = _ln_stats(ext[...])
        hl = xhat * g_ref[...] + be_ref[...]
        ext[0:T, :] = d_ref[...]
        ext[T:, :] = dn_ref[...]
        sl_ = _sig(hl)
        dhl = ext[...] * (sl_ * (1.0 + hl * (1.0 - sl_)))
        dhc = _ln_bwd(dhl, xhat, rstd, g_ref[...])
        rowi = lax.broadcasted_iota(jnp.int32, (R, CONV_CH), 0)
        dbuf[...] = jnp.where((rowi < T) | (i < nsteps - 1), dhc, 0.0)
        dlg_ref[...] += _colsum(dhl[:T] * xhat[:T])
        dlb_ref[...] += _colsum(dhl[:T])
        dm = dbuf[pl.ds(0, T), :]
        dcb_ref[...] += _colsum(dm)
        dhg = jnp.zeros((T, CONV_CH), f32)
        for k in range(K):
            dw_ref[k:k + 1, :] += _colsum(dm * hg_buf[pl.ds(CONV_HALO - (K - 1) + k, T), :])
            dhg = dhg + w_ref[k:k + 1, :] * dbuf[pl.ds(K - 1 - k, T), :]
        da_ref[:, :CONV_CH] = (dhg * sg).astype(bf16)
        da_ref[:, CONV_CH:] = (dhg * a * sg * (1.0 - sg)).astype(bf16)

    vec = jax.ShapeDtypeStruct((1, CONV_CH), f32)
    nxt = lambda i: (jnp.minimum((i + 1) * hb, nsteps * hb - 1), 0)
    return pl.pallas_call(
        body, grid=(nsteps,), name="conv_bwd",
        out_shape=(jax.ShapeDtypeStruct((S, 2 * CONV_CH), bf16), jax.ShapeDtypeStruct((32, CONV_CH), f32), vec, vec, vec),
        in_specs=[pl.BlockSpec((CONV_HALO, 2 * CONV_CH), lambda i: (jnp.maximum(i * hb - 1, 0), 0)),
                  pl.BlockSpec((T, 2 * CONV_CH), lambda i: (i, 0)),
                  pl.BlockSpec((T, CONV_CH), lambda i: (i, 0)), pl.BlockSpec((CONV_HALO, CONV_CH), nxt),
                  pl.BlockSpec((T, CONV_CH), lambda i: (i, 0)), pl.BlockSpec((CONV_HALO, CONV_CH), nxt),
                  _acc((32, CONV_CH)), _acc((1, CONV_CH)), _acc((1, CONV_CH))],
        out_specs=(pl.BlockSpec((T, 2 * CONV_CH), lambda i: (i, 0)), _acc((32, CONV_CH)), _acc((1, CONV_CH)),
                   _acc((1, CONV_CH)), _acc((1, CONV_CH))),
        scratch_shapes=[pltpu.VMEM((T + CONV_HALO, CONV_CH), f32), pltpu.VMEM((R, CONV_CH), f32),
                        pltpu.VMEM((R, CONV_CH), f32)],
        compiler_params=_cp(("arbitrary",), 32),
    )(a_in, a_in, hc, hc, dco, dco, dw_w, ln_g, ln_b)


def _adamw(g, w, m, v, name):
    R, C = g.shape
    T = R
    for cand in (512, 256, 128, 64, 32, 16, 8):
        if R % cand == 0 and cand * C * 4 <= MIB:
            T = cand
            break
    c1 = 1.0 / (1.0 - ADAM_B1 ** ADAM_STEP)
    c2 = 1.0 / (1.0 - ADAM_B2 ** ADAM_STEP)

    def body(g_ref, w_ref, m_ref, v_ref, d_ref, nm_ref, nv_ref):
        gg = g_ref[...]
        nm = ADAM_B1 * m_ref[...] + (1.0 - ADAM_B1) * gg
        nv = ADAM_B2 * v_ref[...] + (1.0 - ADAM_B2) * (gg * gg)
        nm_ref[...] = nm
        nv_ref[...] = nv
        d_ref[...] = -ADAM_LR * ((nm * c1) / (jnp.sqrt(nv * c2) + ADAM_EPS) + ADAM_WD * w_ref[...])

    blk = pl.BlockSpec((T, C), lambda i: (i, 0))
    sd = jax.ShapeDtypeStruct((R, C), f32)
    return pl.pallas_call(
        body, grid=(R // T,), name=name, out_shape=(sd, sd, sd), in_specs=[blk] * 4, out_specs=(blk, blk, blk),
        compiler_params=_cp(("parallel",), 48),
    )(g, w, m, v)


def _pad_rows(a, rows):
    return jnp.pad(a, ((0, rows - a.shape[0]), (0, 0)))


def _local_step(x, target, wb, sp):
    buckets = jnp.asarray(_bucket_ids())
    bias = _bias_build(sp["rel_bias_table"], buckets)
    saved = []
    xl = x
    for l in range(DEPTH):
        vec = lambda name: sp[name][l][None, :]
        a_in, c_in, *qkv = _inproj_fwd(xl, wb["w_in"][l], vec("b_in"))
        conv_w = _pad_rows(sp["conv_dw_w"][l], 32)
        conv_out, hc = _conv_fwd(a_in, conv_w, vec("conv_dw_b"), vec("conv_ln_g"), vec("conv_ln_b"))
        attn_out, lse = _attn_fwd(qkv, bias)
        bs_t = sp["gmlp_b_s"][l].T
        gm_out = _gmlp_fwd(c_in, vec("gmlp_ln_g"), vec("gmlp_ln_b"), sp["gmlp_w_s"][l], bs_t)
        cat, z1, x1, x1b = _outproj_ln_fwd(conv_out, attn_out, gm_out, wb["w_out"][l], vec("b_out"), xl,
                                           vec("ln1_g"), vec("ln1_b"))
        fconv_w = _pad_rows(sp["ffn_conv_w"][l], 8)
        hfb, fhc, act = _ffn_up_gate_fwd(x1b, wb["ffn_w_up"][l], vec("ffn_b_up"), fconv_w, vec("ffn_conv_b"))
        z2, x2 = _ffn_down_ln_fwd(act, wb["ffn_w_down"][l], vec("ffn_b_down"), x1, vec("ln2_g"), vec("ln2_b"))
        saved.append(dict(x=xl, a_in=a_in, qkv=qkv, c_in=c_in, hc=hc, attn_out=attn_out, lse=lse, cat=cat, z1=z1,
                          x1b=x1b, hfb=hfb, fhc=fhc, act=act, z2=z2, conv_w=conv_w, fconv_w=fconv_w, bs_t=bs_t))
        xl = x2

    grads = {}
    per_layer = {k: [None] * DEPTH for k in (
        "w_in", "b_in", "conv_dw_w", "conv_dw_b", "conv_ln_g", "conv_ln_b", "gmlp_ln_g", "gmlp_ln_b", "gmlp_w_s",
        "gmlp_b_s", "w_out", "b_out", "ln1_g", "ln1_b", "ffn_w_up", "ffn_b_up", "ffn_conv_w", "ffn_conv_b",
        "ffn_w_down", "ffn_b_down", "ln2_g", "ln2_b")}
    dbias_all = []
    l = DEPTH - 1
    vec = lambda name: sp[name][l][None, :]
    dz2, dz2b, loss_part, dg2, db2 = _loss_ln_bwd(xl, target, saved[l]["z2"], vec("ln2_g"))
    loss = jnp.sum(loss_part)
    grad_x = None
    for l in reversed(range(DEPTH)):
        sv = saved[l]
        vec = lambda name: sp[name][l][None, :]
        per_layer["ln2_g"][l] = dg2[0]
        per_layer["ln2_b"][l] = db2[0]
        dw_down, db_down = _wgrad(sv["act"], dz2b, 512, "ffn_down_wgrad")
        per_layer["ffn_w_down"][l] = dw_down
        per_layer["ffn_b_down"][l] = db_down[0]
        dhf, dfcw, dfcb = _ffn_down_gate_bwd(dz2b, wb["ffn_w_down"][l], sv["hfb"], sv["fhc"], sv["fconv_w"])
        per_layer["ffn_conv_w"][l] = dfcw[:FFN_CONV_WIDTH]
        per_layer["ffn_conv_b"][l] = dfcb[0]
        dw_up, db_up = _wgrad(sv["x1b"], dhf, 1408, "ffn_up_wgrad")
        per_layer["ffn_w_up"][l] = dw_up
        per_layer["ffn_b_up"][l] = db_up[0]
        dz1, dz1b, dg1, db1 = _dgrad_ln_bwd(dhf, wb["ffn_w_up"][l], dz2, sv["z1"], vec("ln1_g"), "ffn_up_dgrad_ln")
        per_layer["ln1_g"][l] = dg1[0]
        per_layer["ln1_b"][l] = db1[0]
        dw_out, db_out = _wgrad(sv["cat"], dz1b, 512, "outproj_wgrad")
        per_layer["w_out"][l] = dw_out
        per_layer["b_out"][l] = db_out[0]
        dco, dgo, *perm = _outproj_dgrad(dz1b, wb["w_out"][l], sv["attn_out"], sv["lse"])
        d_outs, stats = perm[:len(DILATIONS)], perm[len(DILATIONS):]
        d_c, dglg, dglb, dws, dbs = _gmlp_bwd(sv["c_in"], dgo, vec("gmlp_ln_g"), vec("gmlp_ln_b"), sp["gmlp_w_s"][l],
                                              sv["bs_t"])
        per_layer["gmlp_ln_g"][l] = dglg[0]
        per_layer["gmlp_ln_b"][l] = dglb[0]
        per_layer["gmlp_w_s"][l] = dws
        per_layer["gmlp_b_s"][l] = dbs[:, :GMLP_GROUPS].T
        dqkvs = []
        for p, d in enumerate(DILATIONS):
            dqkv, dbias = _attn_bwd_pattern(sv["qkv"][p], d_outs[p], stats[p], bias[p], d)
            dqkvs.append(dqkv)
            dbias_all.append(dbias)
        d_a, dcw, dcb, dclg, dclb = _conv_bwd(sv["a_in"], sv["hc"], dco, sv["conv_w"], vec("conv_ln_g"),
                                              vec("conv_ln_b"))
        per_layer["conv_dw_w"][l] = dcw[:CONV_WIDTH]
        per_layer["conv_dw_b"][l] = dcb[0]
        per_layer["conv_ln_g"][l] = dclg[0]
        per_layer["conv_ln_b"][l] = dclb[0]
        dh = _attn_bwd_merge(d_a, dqkvs, d_c)
        dw_in, db_in = _wgrad(sv["x"], dh, 640, "inproj_wgrad")
        per_layer["w_in"][l] = dw_in
        per_layer["b_in"][l] = db_in[0]
        if l > 0:
            pv = saved[l - 1]
            dz2, dz2b, dg2, db2 = _dgrad_ln_bwd(dh, wb["w_in"][l], dz1, pv["z2"], sp["ln2_g"][l - 1][None, :],
                                                "inproj_dgrad_ln")
        else:
            grad_x = _dgrad_ln_bwd(dh, wb["w_in"][l], dz1, None, None, "inproj_dgrad")
    for k, v in per_layer.items():
        grads[k] = v if k in BIG else jnp.stack(v)
    dbias_cat = jnp.stack(dbias_all)
    bk_cat = jnp.concatenate([buckets] * DEPTH, axis=0)
    grads["rel_bias_table"] = _bias_table_grad(dbias_cat, bk_cat)[:, :ATTN_HEADS]
    return loss, grad_x, grads


N_CHIPS = 4
BIG = {"w_in": (D_MODEL, IN_CH, 1), "w_out": (D_MODEL, D_MODEL, 0),
       "ffn_w_up": (D_MODEL, 2 * D_FF, 1), "ffn_w_down": (D_FF, D_MODEL, 0)}
SMALL = ("b_in", "conv_dw_w", "conv_dw_b", "conv_ln_g", "conv_ln_b", "rel_bias_table", "gmlp_ln_g", "gmlp_ln_b",
         "gmlp_w_s", "gmlp_b_s", "b_out", "ln1_g", "ln1_b", "ffn_b_up", "ffn_conv_w", "ffn_conv_b", "ffn_b_down",
         "ln2_g", "ln2_b")
SMALL_SHARDED = ("conv_dw_w", "ffn_conv_w")
WEIGHTS = ("w_in", "b_in", "conv_dw_w", "conv_dw_b", "conv_ln_g", "conv_ln_b", "rel_bias_table", "gmlp_ln_g",
           "gmlp_ln_b", "gmlp_w_s", "gmlp_b_s", "w_out", "b_out", "ln1_g", "ln1_b", "ffn_w_up", "ffn_b_up",
           "ffn_conv_w", "ffn_conv_b", "ffn_w_down", "ffn_b_down", "ln2_g", "ln2_b")
ANY = pl.BlockSpec(memory_space=pl.ANY)


def _position():
    return lax.axis_index("x"), lax.axis_index("y"), lax.axis_index("c")


def _other_chips(x, y):
    return [(1 - x, y), (x, 1 - y), (1 - x, 1 - y)]


def _cast_bf16(a):
    R, C = a.shape
    T = 128

    def body(a_ref, o_ref):
        o_ref[...] = a_ref[...].astype(bf16)

    return pl.pallas_call(
        body, grid=(R // T,), name="cast_bf16", out_shape=jax.ShapeDtypeStruct((R, C), bf16),
        in_specs=[pl.BlockSpec((T, C), lambda i: (i, 0))], out_specs=pl.BlockSpec((T, C), lambda i: (i, 0)),
        compiler_params=_cp(("parallel",), 16),
    )(a)


def _chip_slot(ref, name, p):
    K, N, ax = BIG[name]
    if ax == 1:
        sz = N // N_CHIPS
        return ref.at[:, :, pl.ds(pl.multiple_of(p * sz, 128), sz)]
    sz = K // N_CHIPS
    return ref.at[:, pl.ds(pl.multiple_of(p * sz, 16), sz), :]


def _gather_weights(shards, conv_w, fconv_w):
    names = list(BIG)
    n_t = len(names) + 2

    def body(*refs):
        ins = refs[:n_t]
        outs = refs[n_t:2 * n_t]
        send_sems, recv_sems, local_sems = refs[2 * n_t:]
        x, y, c = _position()
        me = 2 * x + y
        chips = _other_chips(x, y)

        def slot(t, p):
            if t < len(names):
                return _chip_slot(outs[t], names[t], p)
            return outs[t].at[p]

        locs, cps = [], []
        for t in range(n_t):
            loc = pltpu.make_async_copy(ins[t], slot(t, me), local_sems.at[t])
            loc.start()
            locs.append(loc)
            for k, (px, py) in enumerate(chips):
                cp = pltpu.make_async_remote_copy(
                    src_ref=ins[t], dst_ref=slot(t, me), send_sem=send_sems.at[3 * t + k],
                    recv_sem=recv_sems.at[3 * t + k], device_id=(px, py, c), device_id_type=MESH_ID)
                cp.start()
                cps.append(cp)
        for t in range(n_t):
            for k, (px, py) in enumerate(chips):
                pltpu.make_async_remote_copy(
                    src_ref=ins[t], dst_ref=slot(t, 2 * px + py), send_sem=send_sems.at[3 * t + k],
                    recv_sem=recv_sems.at[3 * t + k], device_id=(px, py, c), device_id_type=MESH_ID).wait_recv()
        for cp in cps:
            cp.wait_send()
        for loc in locs:
            loc.wait()

    ins = [shards[n] for n in names] + [conv_w, fconv_w]
    out_shape = [jax.ShapeDtypeStruct((DEPTH, BIG[n][0], BIG[n][1]), bf16) for n in names]
    out_shape += [jax.ShapeDtypeStruct((N_CHIPS,) + conv_w.shape, f32), jax.ShapeDtypeStruct((N_CHIPS,) + fconv_w.shape, f32)]
    outs = pl.pallas_call(
        body, name="gather_weights", out_shape=tuple(out_shape), in_specs=[ANY] * n_t, out_specs=tuple([ANY] * n_t),
        scratch_shapes=[pltpu.SemaphoreType.DMA((3 * n_t,)), pltpu.SemaphoreType.DMA((3 * n_t,)),
                        pltpu.SemaphoreType.DMA((n_t,))],
    )(*ins)
    return dict(zip(names, outs[:len(names)])), outs[-2], outs[-1]


def _half(ref, name, c):
    K, N, ax = BIG[name]
    if ax == 1:
        return ref.at[pl.ds(pl.multiple_of(c * (K // 2), 8), K // 2), :]
    return ref.at[:, pl.ds(pl.multiple_of(c * (N // 2), 128), N // 2)]


def _half_shape(name):
    K, N, ax = BIG[name]
    return (K // 2, N) if ax == 1 else (K, N // 2)


def _shard_of_half(ref, name, q):
    K, N, ax = BIG[name]
    if ax == 1:
        sz = N // N_CHIPS
        return ref.at[:, pl.ds(pl.multiple_of(q * sz, 128), sz)]
    sz = K // N_CHIPS
    return ref.at[pl.ds(pl.multiple_of(q * sz, 8), sz), :]


def _shard_half_shape(name):
    K, N, ax = BIG[name]
    return (K // 2, N // N_CHIPS) if ax == 1 else (K // N_CHIPS, N // 2)


def _shard_shape(name):
    K, N, ax = BIG[name]
    return (K, N // N_CHIPS) if ax == 1 else (K // N_CHIPS, N)


def _place_in_shard(ref, name, l, c):
    K, N, ax = BIG[name]
    if ax == 1:
        return ref.at[l, pl.ds(pl.multiple_of(c * (K // 2), 8), K // 2), :]
    return ref.at[l, :, pl.ds(pl.multiple_of(c * (N // 2), 128), N // 2)]


def _pair_exchange(tensors):
    n_t = len(tensors)

    def body(*refs):
        ins = refs[:n_t]
        outs = refs[n_t:2 * n_t]
        send_sems, recv_sems = refs[2 * n_t:]
        x, y, c = _position()
        cps = []
        for t, (name, _) in enumerate(tensors):
            cp = pltpu.make_async_remote_copy(
                src_ref=_half(ins[t], name, 1 - c), dst_ref=outs[t], send_sem=send_sems.at[t],
                recv_sem=recv_sems.at[t], device_id=(x, y, 1 - c), device_id_type=MESH_ID)
            cp.start()
            cps.append(cp)
        for cp in cps:
            cp.wait()

    return pl.pallas_call(
        body, name="grad_pair_exchange",
        out_shape=tuple(jax.ShapeDtypeStruct(_half_shape(n), f32) for n, _ in tensors),
        in_specs=[ANY] * n_t, out_specs=tuple([ANY] * n_t),
        scratch_shapes=[pltpu.SemaphoreType.DMA((n_t,)), pltpu.SemaphoreType.DMA((n_t,))],
    )(*[g for _, g in tensors])


def _pair_add(g, rcv, name, c_arr):
    K, N, ax = BIG[name]
    hr, hc = _half_shape(name)
    T = 128
    nrt = hr // T

    def body(c_ref, g_ref, r_ref, o_ref):
        o_ref[...] = g_ref[...] + r_ref[...]

    if ax == 1:
        g_spec = pl.BlockSpec((T, hc), lambda i, c: (c[0] * nrt + i, 0))
    else:
        g_spec = pl.BlockSpec((T, hc), lambda i, c: (i, c[0]))
    plain = pl.BlockSpec((T, hc), lambda i, c: (i, 0))
    return pl.pallas_call(
        body, name="grad_pair_add", out_shape=jax.ShapeDtypeStruct((hr, hc), f32),
        grid_spec=pltpu.PrefetchScalarGridSpec(num_scalar_prefetch=1, grid=(nrt,), in_specs=[g_spec, plain],
                                               out_specs=plain),
        compiler_params=_cp(("parallel",), 32),
    )(c_arr, g, rcv)


def _chip_exchange(tensors):
    n_t = len(tensors)

    def body(*refs):
        ins = refs[:n_t]
        outs = refs[n_t:2 * n_t]
        send_sems, recv_sems, local_sems = refs[2 * n_t:]
        x, y, c = _position()
        me = 2 * x + y
        chips = _other_chips(x, y)
        locs, cps = [], []
        for t, (name, _) in enumerate(tensors):
            loc = pltpu.make_async_copy(_shard_of_half(ins[t], name, me), outs[t].at[me], local_sems.at[t])
            loc.start()
            locs.append(loc)
            for k, (px, py) in enumerate(chips):
                cp = pltpu.make_async_remote_copy(
                    src_ref=_shard_of_half(ins[t], name, 2 * px + py), dst_ref=outs[t].at[me],
                    send_sem=send_sems.at[3 * t + k], recv_sem=recv_sems.at[3 * t + k],
                    device_id=(px, py, c), device_id_type=MESH_ID)
                cp.start()
                cps.append(cp)
        for t, (name, _) in enumerate(tensors):
            for k, (px, py) in enumerate(chips):
                pltpu.make_async_remote_copy(
                    src_ref=_shard_of_half(ins[t], name, 2 * px + py), dst_ref=outs[t].at[2 * px + py],
                    send_sem=send_sems.at[3 * t + k], recv_sem=recv_sems.at[3 * t + k],
                    device_id=(px, py, c), device_id_type=MESH_ID).wait_recv()
        for cp in cps:
            cp.wait_send()
        for loc in locs:
            loc.wait()

    return pl.pallas_call(
        body, name="grad_chip_exchange",
        out_shape=tuple(jax.ShapeDtypeStruct((N_CHIPS,) + _shard_half_shape(n), f32) for n, _ in tensors),
        in_specs=[ANY] * n_t, out_specs=tuple([ANY] * n_t),
        scratch_shapes=[pltpu.SemaphoreType.DMA((3 * n_t,)), pltpu.SemaphoreType.DMA((3 * n_t,)),
                        pltpu.SemaphoreType.DMA((n_t,))],
    )(*[g for _, g in tensors])


def _sum_chips(parts):
    _, R, C = parts.shape
    T = 64

    def body(p_ref, o_ref):
        o_ref[...] = ((p_ref[0] + p_ref[1]) + p_ref[2]) + p_ref[3]

    return pl.pallas_call(
        body, grid=(R // T,), name="grad_sum_chips", out_shape=jax.ShapeDtypeStruct((R, C), f32),
        in_specs=[pl.BlockSpec((N_CHIPS, T, C), lambda i: (0, i, 0))], out_specs=pl.BlockSpec((T, C), lambda i: (i, 0)),
        compiler_params=_cp(("parallel",), 32),
    )(parts)


def _pair_gather(tensors):
    n_t = len(tensors)
    names = list(BIG)

    def body(*refs):
        ins = refs[:n_t]
        outs = dict(zip(names, refs[n_t:n_t + len(names)]))
        send_sems, recv_sems, local_sems = refs[n_t + len(names):]
        x, y, c = _position()
        locs, cps = [], []
        for t, (name, l, _) in enumerate(tensors):
            loc = pltpu.make_async_copy(ins[t], _place_in_shard(outs[name], name, l, c), local_sems.at[t])
            loc.start()
            locs.append(loc)
            cp = pltpu.make_async_remote_copy(
                src_ref=ins[t], dst_ref=_place_in_shard(outs[name], name, l, c), send_sem=send_sems.at[t],
                recv_sem=recv_sems.at[t], device_id=(x, y, 1 - c), device_id_type=MESH_ID)
            cp.start()
            cps.append(cp)
        for t, (name, l, _) in enumerate(tensors):
            pltpu.make_async_remote_copy(
                src_ref=ins[t], dst_ref=_place_in_shard(outs[name], name, l, 1 - c), send_sem=send_sems.at[t],
                recv_sem=recv_sems.at[t], device_id=(x, y, 1 - c), device_id_type=MESH_ID).wait_recv()
        for cp in cps:
            cp.wait_send()
        for loc in locs:
            loc.wait()

    outs = pl.pallas_call(
        body, name="grad_pair_gather",
        out_shape=tuple(jax.ShapeDtypeStruct((DEPTH,) + _shard_shape(n), f32) for n in names),
        in_specs=[ANY] * n_t, out_specs=tuple([ANY] * len(names)),
        scratch_shapes=[pltpu.SemaphoreType.DMA((n_t,)), pltpu.SemaphoreType.DMA((n_t,)),
                        pltpu.SemaphoreType.DMA((n_t,))],
    )(*[g for _, _, g in tensors])
    return dict(zip(names, outs))


def _reduce_big_grads(grads):
    c_arr = jnp.reshape(lax.axis_index("c"), (1,)).astype(jnp.int32)
    tensors = [(n, grads[n][l]) for n in BIG for l in range(DEPTH)]
    layers = [l for n in BIG for l in range(DEPTH)]
    received = _pair_exchange(tensors)
    pair = [(n, _pair_add(g, r, n, c_arr)) for (n, g), r in zip(tensors, received)]
    parts = _chip_exchange(pair)
    reduced = [(n, l, _sum_chips(p)) for (n, _), l, p in zip(pair, layers, parts)]
    return _pair_gather(reduced)


def _small_allreduce(buf):
    R = buf.shape[0]
    n_dev = 8

    def body(in_ref, out_ref, slots, send_sems, recv_sems):
        x, y, c = _position()
        me = 4 * x + 2 * y + c
        slots[me] = in_ref[...]
        peers = []
        for k in range(1, n_dev):
            px = 1 - x if k & 4 else x
            py = 1 - y if k & 2 else y
            pc = 1 - c if k & 1 else c
            peers.append((px, py, pc))
        cps = []
        for k, peer in enumerate(peers):
            cp = pltpu.make_async_remote_copy(
                src_ref=in_ref, dst_ref=slots.at[me], send_sem=send_sems.at[k], recv_sem=recv_sems.at[k],
                device_id=peer, device_id_type=MESH_ID)
            cp.start()
            cps.append(cp)
        for k, (px, py, pc) in enumerate(peers):
            pltpu.make_async_remote_copy(
                src_ref=in_ref, dst_ref=slots.at[4 * px + 2 * py + pc], send_sem=send_sems.at[k],
                recv_sem=recv_sems.at[k], device_id=(px, py, pc), device_id_type=MESH_ID).wait_recv()
        for cp in cps:
            cp.wait_send()
        acc = slots[0]
        for dv in range(1, n_dev):
            acc = acc + slots[dv]
        out_ref[...] = acc

    vm = pl.BlockSpec(memory_space=pltpu.VMEM)
    return pl.pallas_call(
        body, name="small_allreduce", out_shape=jax.ShapeDtypeStruct((R, 128), f32), in_specs=[vm], out_specs=vm,
        scratch_shapes=[pltpu.VMEM((n_dev, R, 128), f32), pltpu.SemaphoreType.DMA((n_dev - 1,)),
                        pltpu.SemaphoreType.DMA((n_dev - 1,))],
        compiler_params=pltpu.CompilerParams(vmem_limit_bytes=40 * MIB),
    )(buf)


PACK_UNIT = 1024


def _pack(arrs):
    parts = []
    for a in arrs:
        flat = a.reshape(-1)
        n = -(-flat.shape[0] // PACK_UNIT) * PACK_UNIT
        parts.append(jnp.pad(flat, (0, n - flat.shape[0])))
    return jnp.concatenate(parts).reshape(-1, 128)


def _unpack(buf, shapes):
    flat = buf.reshape(-1)
    out, off = [], 0
    for shp in shapes:
        n = int(np.prod(shp))
        out.append(flat[off:off + n].reshape(shp))
        off += -(-n // PACK_UNIT) * PACK_UNIT
    return out


def _adamw_rows(g, w, m, v, name):
    shp = g.shape
    C = shp[-1]
    outs = _adamw(g.reshape(-1, C), w.reshape(-1, C), m.reshape(-1, C), v.reshape(-1, C), name)
    return [o.reshape(shp) for o in outs]


def kernel(x, w_in, b_in, conv_dw_w, conv_dw_b, conv_ln_g, conv_ln_b, rel_bias_table, gmlp_ln_g, gmlp_ln_b, gmlp_w_s, gmlp_b_s, w_out, b_out, ln1_g, ln1_b, ffn_w_up, ffn_b_up, ffn_conv_w, ffn_conv_b, ffn_w_down, ffn_b_down, ln2_g, ln2_b, loss_target, m_w_in, m_b_in, m_conv_dw_w, m_conv_dw_b, m_conv_ln_g, m_conv_ln_b, m_rel_bias_table, m_gmlp_ln_g, m_gmlp_ln_b, m_gmlp_w_s, m_gmlp_b_s, m_w_out, m_b_out, m_ln1_g, m_ln1_b, m_ffn_w_up, m_ffn_b_up, m_ffn_conv_w, m_ffn_conv_b, m_ffn_w_down, m_ffn_b_down, m_ln2_g, m_ln2_b, v_w_in, v_b_in, v_conv_dw_w, v_conv_dw_b, v_conv_ln_g, v_conv_ln_b, v_rel_bias_table, v_gmlp_ln_g, v_gmlp_ln_b, v_gmlp_w_s, v_gmlp_b_s, v_w_out, v_b_out, v_ln1_g, v_ln1_b, v_ffn_w_up, v_ffn_b_up, v_ffn_conv_w, v_ffn_conv_b, v_ffn_w_down, v_ffn_b_down, v_ln2_g, v_ln2_b):
    w = dict(w_in=w_in, b_in=b_in, conv_dw_w=conv_dw_w, conv_dw_b=conv_dw_b, conv_ln_g=conv_ln_g, conv_ln_b=conv_ln_b,
             rel_bias_table=rel_bias_table, gmlp_ln_g=gmlp_ln_g, gmlp_ln_b=gmlp_ln_b, gmlp_w_s=gmlp_w_s,
             gmlp_b_s=gmlp_b_s, w_out=w_out, b_out=b_out, ln1_g=ln1_g, ln1_b=ln1_b, ffn_w_up=ffn_w_up,
             ffn_b_up=ffn_b_up, ffn_conv_w=ffn_conv_w, ffn_conv_b=ffn_conv_b, ffn_w_down=ffn_w_down,
             ffn_b_down=ffn_b_down, ln2_g=ln2_g, ln2_b=ln2_b)
    m = dict(w_in=m_w_in, b_in=m_b_in, conv_dw_w=m_conv_dw_w, conv_dw_b=m_conv_dw_b, conv_ln_g=m_conv_ln_g,
             conv_ln_b=m_conv_ln_b, rel_bias_table=m_rel_bias_table, gmlp_ln_g=m_gmlp_ln_g, gmlp_ln_b=m_gmlp_ln_b,
             gmlp_w_s=m_gmlp_w_s, gmlp_b_s=m_gmlp_b_s, w_out=m_w_out, b_out=m_b_out, ln1_g=m_ln1_g, ln1_b=m_ln1_b,
             ffn_w_up=m_ffn_w_up, ffn_b_up=m_ffn_b_up, ffn_conv_w=m_ffn_conv_w, ffn_conv_b=m_ffn_conv_b,
             ffn_w_down=m_ffn_w_down, ffn_b_down=m_ffn_b_down, ln2_g=m_ln2_g, ln2_b=m_ln2_b)
    v = dict(w_in=v_w_in, b_in=v_b_in, conv_dw_w=v_conv_dw_w, conv_dw_b=v_conv_dw_b, conv_ln_g=v_conv_ln_g,
             conv_ln_b=v_conv_ln_b, rel_bias_table=v_rel_bias_table, gmlp_ln_g=v_gmlp_ln_g, gmlp_ln_b=v_gmlp_ln_b,
             gmlp_w_s=v_gmlp_w_s, gmlp_b_s=v_gmlp_b_s, w_out=v_w_out, b_out=v_b_out, ln1_g=v_ln1_g, ln1_b=v_ln1_b,
             ffn_w_up=v_ffn_w_up, ffn_b_up=v_ffn_b_up, ffn_conv_w=v_ffn_conv_w, ffn_conv_b=v_ffn_conv_b,
             ffn_w_down=v_ffn_w_down, ffn_b_down=v_ffn_b_down, ln2_g=v_ln2_g, ln2_b=v_ln2_b)

    shards = {n: _cast_bf16(w[n].reshape(-1, w[n].shape[-1])).reshape(w[n].shape) for n in BIG}
    wb, conv_stack, fconv_stack = _gather_weights(shards, conv_dw_w, ffn_conv_w)
    sp = {n: w[n] for n in SMALL}
    sp["conv_dw_w"] = jnp.moveaxis(conv_stack, 0, 2).reshape(DEPTH, CONV_WIDTH, CONV_CH)
    sp["ffn_conv_w"] = jnp.moveaxis(fconv_stack, 0, 2).reshape(DEPTH, FFN_CONV_WIDTH, 2 * D_FF)

    loss_local, grad_x, grads = _local_step(x[0], loss_target[0], wb, sp)
    loss = lax.psum(loss_local, ("x", "y", "c"))

    big = _reduce_big_grads(grads)
    small_shapes = [grads[n].shape for n in SMALL]
    small = dict(zip(SMALL, _unpack(_small_allreduce(_pack([grads[n] for n in SMALL])), small_shapes)))
    chip = 2 * lax.axis_index("x") + lax.axis_index("y")
    for n in SMALL_SHARDED:
        width = w[n].shape[-1]
        small[n] = lax.dynamic_slice_in_dim(small[n], chip * width, width, axis=2)

    g_out, d_out, m_out, v_out = {}, {}, {}, {}
    for n in BIG:
        g_out[n] = big[n]
        d_out[n], m_out[n], v_out[n] = _adamw_rows(big[n], w[n], m[n], v[n], "adamw_" + n)
    shapes = [small[n].shape for n in SMALL]
    packed = [_pack([src[n] for n in SMALL]) for src in (small, w, m, v)]
    upd = _adamw(*packed, "adamw_small")
    for dst, buf in zip((d_out, m_out, v_out), upd):
        dst.update(zip(SMALL, _unpack(buf, shapes)))
    g_out.update(small)

    return (loss, grad_x[None], *[g_out[n] for n in WEIGHTS], *[d_out[n] for n in WEIGHTS],
            *[m_out[n] for n in WEIGHTS], *[v_out[n] for n in WEIGHTS])
```

```python
import functools
import math

import numpy as np
import jax
import jax.numpy as jnp
from jax import lax
from jax.experimental import pallas as pl
from jax.experimental.pallas import tpu as pltpu

f32 = jnp.float32
bf16 = jnp.bfloat16

D_MODEL = 1024
DEPTH = 2
HEAD_DIM = 64
CONV_CH = 256
CONV_WIDTH = 31
ATTN_HEADS = 8
ATTN_CH = ATTN_HEADS * HEAD_DIM
DILATIONS = (1, 4, 16)
ATTN_BLOCK = 128
N_BUCKETS = 32
MAX_DISTANCE = 2048
GMLP_CH = 256
GMLP_GROUPS = 4
GMLP_GROUP_DIM = GMLP_CH // GMLP_GROUPS
CHUNK = 128
IN_CH = 2 * CONV_CH + 3 * ATTN_CH + 2 * GMLP_CH
D_FF = 2816
FFN_CONV_WIDTH = 3
LN_EPS = 1e-5
ALPHA = (2.0 * DEPTH) ** 0.25
ADAM_LR = 0.001
ADAM_B1 = 0.9
ADAM_B2 = 0.999
ADAM_EPS = 1e-08
ADAM_WD = 0.01
ADAM_STEP = 10

CONV_HALO = 32
FFN_HALO = 8
NEG = -1e30
MIB = 2 ** 20
NT_DIMS = (((1,), (1,)), ((), ()))
TN_DIMS = (((0,), (0,)), ((), ()))
MESH_ID = pl.DeviceIdType.MESH


def _cp(sem, vmem_mib):
    return pltpu.CompilerParams(dimension_semantics=sem, vmem_limit_bytes=vmem_mib * MIB)


def _resident(shape):
    nd = len(shape)
    return pl.BlockSpec(shape, lambda *_: (0,) * nd, pipeline_mode=pl.Buffered(1))


def _acc(shape):
    nd = len(shape)
    return pl.BlockSpec(shape, lambda *_: (0,) * nd)


def _sig(x):
    return 1.0 / (1.0 + jnp.exp(-x))


def _ln_stats(z):
    mu = jnp.mean(z, axis=-1, keepdims=True)
    zc = z - mu
    var = jnp.mean(zc * zc, axis=-1, keepdims=True)
    rstd = lax.rsqrt(var + LN_EPS)
    return zc * rstd, rstd


def _ln_bwd(dy, xhat, rstd, g):
    dxh = dy * g
    m1 = jnp.mean(dxh, axis=-1, keepdims=True)
    m2 = jnp.mean(dxh * xhat, axis=-1, keepdims=True)
    return rstd * (dxh - m1 - xhat * m2)


def _colsum(x):
    return jnp.sum(x, axis=0, keepdims=True)


def _t5_bucket_np(dist):
    max_exact = N_BUCKETS // 2
    dd = np.maximum(dist, 1).astype(np.float64)
    large = max_exact + (np.log(dd / max_exact) / math.log(MAX_DISTANCE / max_exact)
                         * (N_BUCKETS - max_exact)).astype(np.int32)
    large = np.minimum(large, N_BUCKETS - 1)
    return np.where(dist < max_exact, dist, large).astype(np.int32)


def _bucket_ids():
    qi = np.arange(ATTN_BLOCK)[:, None]
    kj = np.arange(2 * ATTN_BLOCK)[None, :]
    dist = np.clip(qi + ATTN_BLOCK - kj, 0, None)
    return np.stack([_t5_bucket_np(dist * d) for d in DILATIONS]).astype(np.int32)


LANES = 128
QKV_CH = 3 * ATTN_CH
PERM_TILE = 512


def _slabs(n, rows):
    return [pltpu.VMEM((rows, LANES), f32)] * n


def _rows_of(slab, r, n, d):
    return slab[...] if d == 1 else slab[pl.ds(r, n, stride=d), :]


def _set_rows_of(slab, r, n, d, val):
    if d == 1:
        slab[...] = val
    else:
        slab[pl.ds(r, n, stride=d), :] = val


def _perm_spec(d, ch):
    return pl.BlockSpec((d, PERM_TILE // d, ch), lambda i: (0, i, 0))


def _perm_shape(S, d, ch, dtype):
    return jax.ShapeDtypeStruct((d, S // d, ch), dtype)


def _inproj_fwd(x, w, b):
    S = x.shape[0]
    T = PERM_TILE
    nsl = QKV_CH // LANES

    def body(x_ref, w_ref, b_ref, a_ref, c_ref, *rest):
        q_refs = rest[:len(DILATIONS)]
        slabs = rest[len(DILATIONS):]
        h = jnp.dot(x_ref[...].astype(bf16), w_ref[...], preferred_element_type=f32) + b_ref[...]
        a_ref[...] = h[:, :2 * CONV_CH]
        q0 = 2 * CONV_CH
        c_ref[...] = h[:, q0 + QKV_CH:]
        for j in range(nsl):
            piece = h[:, q0 + LANES * j:q0 + LANES * (j + 1)]
            if LANES * j < ATTN_CH:
                piece = piece * (HEAD_DIM ** -0.5)
            slabs[j][...] = piece
        for d, q_ref in zip(DILATIONS, q_refs):
            for r in range(d):
                for j in range(nsl):
                    q_ref[r, :, LANES * j:LANES * (j + 1)] = _rows_of(slabs[j], r, T // d, d).astype(bf16)

    row = lambda c: pl.BlockSpec((T, c), lambda i: (i, 0))
    return pl.pallas_call(
        body, grid=(S // T,), name="inproj_fwd",
        out_shape=(jax.ShapeDtypeStruct((S, 2 * CONV_CH), f32), jax.ShapeDtypeStruct((S, 2 * GMLP_CH), f32))
        + tuple(_perm_shape(S, d, QKV_CH, bf16) for d in DILATIONS),
        in_specs=[row(D_MODEL), _resident((D_MODEL, IN_CH)), _resident((1, IN_CH))],
        out_specs=(row(2 * CONV_CH), row(2 * GMLP_CH)) + tuple(_perm_spec(d, QKV_CH) for d in DILATIONS),
        scratch_shapes=_slabs(nsl, T),
        compiler_params=_cp(("parallel",), 48),
    )(x, w, b)


def _conv_fwd(a_in, dw_w, dw_b, ln_g, ln_b):
    S = a_in.shape[0]
    T = 512
    hb = T // CONV_HALO

    def body(a_ref, halo_ref, w_ref, b_ref, g_ref, be_ref, out_ref, hc_ref, buf):
        i = pl.program_id(0)
        am = a_ref[...]
        ah = halo_ref[...]
        hgh = ah[:, :CONV_CH] * _sig(ah[:, CONV_CH:])
        buf[0:CONV_HALO, :] = jnp.where(i > 0, hgh, 0.0)
        buf[CONV_HALO:, :] = am[:, :CONV_CH] * _sig(am[:, CONV_CH:])
        acc = jnp.zeros((T, CONV_CH), f32) + b_ref[...]
        for k in range(CONV_WIDTH):
            acc = acc + w_ref[k:k + 1, :] * buf[pl.ds(CONV_HALO - (CONV_WIDTH - 1) + k, T), :]
        hc_ref[...] = acc
        xhat, _ = _ln_stats(acc)
        y = xhat * g_ref[...] + be_ref[...]
        out_ref[...] = (y * _sig(y)).astype(bf16)

    return pl.pallas_call(
        body, grid=(S // T,), name="conv_fwd",
        out_shape=(jax.ShapeDtypeStruct((S, CONV_CH), bf16), jax.ShapeDtypeStruct((S, CONV_CH), f32)),
        in_specs=[pl.BlockSpec((T, 2 * CONV_CH), lambda i: (i, 0)),
                  pl.BlockSpec((CONV_HALO, 2 * CONV_CH), lambda i: (jnp.maximum(i * hb - 1, 0), 0)),
                  _acc((32, CONV_CH)), _acc((1, CONV_CH)), _acc((1, CONV_CH)), _acc((1, CONV_CH))],
        out_specs=(pl.BlockSpec((T, CONV_CH), lambda i: (i, 0)), pl.BlockSpec((T, CONV_CH), lambda i: (i, 0))),
        scratch_shapes=[pltpu.VMEM((T + CONV_HALO, CONV_CH), f32)],
        compiler_params=_cp(("parallel",), 32),
    )(a_in, a_in, dw_w, dw_b, ln_g, ln_b)


def _bias_build(table, buckets):
    def body(t_ref, bk_ref, o_ref):
        h = pl.program_id(1)
        ids = bk_ref[0]
        acc = jnp.zeros((ATTN_BLOCK, 2 * ATTN_BLOCK), f32)
        for b in range(N_BUCKETS):
            acc = jnp.where(ids == b, t_ref[b, h], acc)
        o_ref[0, 0] = acc

    return pl.pallas_call(
        body, grid=(len(DILATIONS), ATTN_HEADS), name="bias_build",
        out_shape=jax.ShapeDtypeStruct((len(DILATIONS), ATTN_HEADS, ATTN_BLOCK, 2 * ATTN_BLOCK), f32),
        in_specs=[pl.BlockSpec(memory_space=pltpu.SMEM),
                  pl.BlockSpec((1, ATTN_BLOCK, 2 * ATTN_BLOCK), lambda p, h: (p, 0, 0))],
        out_specs=pl.BlockSpec((1, 1, ATTN_BLOCK, 2 * ATTN_BLOCK), lambda p, h: (p, h, 0, 0)),
        compiler_params=_cp(("arbitrary", "arbitrary"), 16),
    )(table, buckets)


def _head_tile(tile, h, col):
    lane_head = lax.broadcasted_iota(jnp.int32, tile.shape, 1) // 16
    return jnp.where(lane_head == h, col, tile)


HEAD_PAIRS = ATTN_HEADS // 2
UNITS_PER_BLOCK = ATTN_HEADS


def _attn_tile(L):
    return min(512, L)


def _band_mask(first_block, n):
    B = ATTN_BLOCK
    row = lax.broadcasted_iota(jnp.int32, (B, 2 * B), 0)
    col = lax.broadcasted_iota(jnp.int32, (B, 2 * B), 1)
    valid = (col >= row) & (col <= row + B)
    if first_block:
        valid = valid & ((col >= B) | (n > 0))
    return valid


def _head_lanes(a):
    lane = lax.broadcasted_iota(jnp.int32, (ATTN_BLOCK, LANES), 1)
    return (lane < HEAD_DIM) if a == 0 else (lane >= HEAD_DIM)


def _pair_keys(cur_ref, halo_ref, part, b, j):
    B = ATTN_BLOCK
    c0 = part * ATTN_CH + LANES * j
    own = cur_ref[B * b:B * (b + 1), c0:c0 + LANES]
    prev = halo_ref[:, LANES * j:LANES * (j + 1)] if b == 0 else cur_ref[B * (b - 1):B * b, c0:c0 + LANES]
    return jnp.concatenate([prev, own], axis=0)


def _attn_fwd_pattern(qkv, bias, d):
    _, L, _ = qkv.shape
    B = ATTN_BLOCK
    QB = _attn_tile(L)
    nsb = QB // B
    U = nsb * UNITS_PER_BLOCK

    def body(cur_ref, hk_ref, hv_ref, b_ref, o_ref, lse_ref, lg, pb):
        n = pl.program_id(1)
        for b in range(nsb):
            valid = _band_mask(b == 0, n)
            for j in range(HEAD_PAIRS):
                q2 = cur_ref[B * b:B * (b + 1), LANES * j:LANES * (j + 1)]
                k2 = _pair_keys(cur_ref, hk_ref, 1, b, j)
                for a in range(2):
                    u = (b * HEAD_PAIRS + j) * 2 + a
                    qm = jnp.where(_head_lanes(a), q2, jnp.zeros_like(q2))
                    logits = lax.dot_general(qm, k2, NT_DIMS, preferred_element_type=f32) + b_ref[2 * j + a]
                    lg[B * u:B * (u + 1), :] = jnp.where(valid, logits, NEG)
        m = jnp.max(lg[...], axis=1, keepdims=True)
        p = jnp.exp(lg[...] - m)
        s = jnp.sum(p, axis=1, keepdims=True)
        pb[...] = p.astype(bf16)
        lse = m + jnp.log(s)
        inv = 1.0 / s
        for b in range(nsb):
            tile = jnp.zeros((B, B), f32)
            for j in range(HEAD_PAIRS):
                v2 = _pair_keys(cur_ref, hv_ref, 2, b, j)
                outs = []
                for a in range(2):
                    u = (b * HEAD_PAIRS + j) * 2 + a
                    rows = slice(B * u, B * (u + 1))
                    outs.append(jnp.dot(pb[rows, :], v2, preferred_element_type=f32) * inv[rows])
                    tile = _head_tile(tile, 2 * j + a, lse[rows])
                o_ref[B * b:B * (b + 1), LANES * j:LANES * (j + 1)] = jnp.where(_head_lanes(0), outs[0], outs[1])
            lse_ref[B * b:B * (b + 1), :] = tile

    halo = lambda part: pl.BlockSpec((None, B, ATTN_CH), lambda r, n: (r, jnp.maximum(n * nsb - 1, 0), part))
    tile_spec = lambda c: pl.BlockSpec((None, QB, c), lambda r, n: (r, n, 0))
    return pl.pallas_call(
        body, grid=(d, L // QB), name=f"attn_fwd_d{d}",
        out_shape=(jax.ShapeDtypeStruct((d, L, ATTN_CH), f32), jax.ShapeDtypeStruct((d, L, B), f32)),
        in_specs=[tile_spec(QKV_CH), halo(1), halo(2), _resident((ATTN_HEADS, B, 2 * B))],
        out_specs=(tile_spec(ATTN_CH), tile_spec(B)),
        scratch_shapes=[pltpu.VMEM((U * B, 2 * B), f32), pltpu.VMEM((U * B, 2 * B), bf16)],
        compiler_params=_cp(("parallel", "parallel"), 40),
    )(qkv, qkv, qkv, bias)


def _attn_merge(parts):
    S = parts[0][0].shape[0] * parts[0][0].shape[1]
    T = PERM_TILE
    nsl = ATTN_CH // LANES
    n_p = len(DILATIONS)

    def body(*refs):
        ins = refs[:2 * n_p]
        out_ref, lse_ref = refs[2 * n_p:2 * n_p + 2]
        slabs = refs[2 * n_p + 2:]
        lses = []
        for p, d in enumerate(DILATIONS):
            o_ref, l_ref = ins[2 * p], ins[2 * p + 1]
            osl = slabs[p * (nsl + 1):p * (nsl + 1) + nsl]
            lsl = slabs[p * (nsl + 1) + nsl]
            for r in range(d):
                for j in range(nsl):
                    _set_rows_of(osl[j], r, T // d, d, o_ref[r, :, LANES * j:LANES * (j + 1)])
                _set_rows_of(lsl, r, T // d, d, l_ref[r])
            lses.append(lsl[...])
        big = functools.reduce(jnp.maximum, lses)
        ws = [jnp.exp(l - big) for l in lses]
        tot = functools.reduce(lambda a_, b_: a_ + b_, ws)
        lse_ref[...] = big + jnp.log(tot)
        ws = [w / tot for w in ws]
        for j in range(nsl):
            acc = jnp.zeros((T, LANES), f32)
            for p in range(n_p):
                wa = ws[p][:, 32 * j:32 * j + 1]
                wb = ws[p][:, 32 * j + 16:32 * j + 17]
                lane = lax.broadcasted_iota(jnp.int32, (T, LANES), 1)
                acc = acc + jnp.where(lane < HEAD_DIM, wa, wb) * slabs[p * (nsl + 1) + j][...]
            out_ref[:, LANES * j:LANES * (j + 1)] = acc.astype(bf16)

    in_specs, args = [], []
    for (o, l), d in zip(parts, DILATIONS):
        in_specs += [_perm_spec(d, ATTN_CH), _perm_spec(d, ATTN_BLOCK)]
        args += [o, l]
    row = lambda c: pl.BlockSpec((T, c), lambda i: (i, 0))
    return pl.pallas_call(
        body, grid=(S // T,), name="attn_merge",
        out_shape=(jax.ShapeDtypeStruct((S, ATTN_CH), bf16), jax.ShapeDtypeStruct((S, ATTN_BLOCK), f32)),
        in_specs=in_specs, out_specs=(row(ATTN_CH), row(ATTN_BLOCK)),
        scratch_shapes=_slabs(n_p * (nsl + 1), T),
        compiler_params=_cp(("parallel",), 40),
    )(*args)


def _attn_fwd(qkvs, bias):
    parts = [_attn_fwd_pattern(q, bias[p], d) for p, (q, d) in enumerate(zip(qkvs, DILATIONS))]
    return _attn_merge(parts)


def _tril_bf16(w):
    row = lax.broadcasted_iota(jnp.int32, (CHUNK, CHUNK), 0)
    col = lax.broadcasted_iota(jnp.int32, (CHUNK, CHUNK), 1)
    return jnp.where(col <= row, w, 0.0).astype(bf16)


def _gmlp_fwd(c_in, ln_g, ln_b, w_s, b_s_t):
    S = c_in.shape[0]
    T = 512

    def body(c_ref, g_ref, be_ref, w_ref, bs_ref, out_ref, mix):
        c = c_ref[...]
        xhat, _ = _ln_stats(c[:, GMLP_CH:])
        vb = (xhat * g_ref[...] + be_ref[...]).astype(bf16)
        for g in range(GMLP_GROUPS):
            wt = _tril_bf16(w_ref[g])
            cs = slice(GMLP_GROUP_DIM * g, GMLP_GROUP_DIM * (g + 1))
            for ci in range(T // CHUNK):
                rs = slice(CHUNK * ci, CHUNK * (ci + 1))
                mix[rs, cs] = jnp.dot(wt, vb[rs, cs], preferred_element_type=f32) + bs_ref[:, g:g + 1]
        out_ref[...] = (c[:, :GMLP_CH] * mix[...]).astype(bf16)

    return pl.pallas_call(
        body, grid=(S // T,), name="gmlp_fwd",
        out_shape=jax.ShapeDtypeStruct((S, GMLP_CH), bf16),
        in_specs=[pl.BlockSpec((T, 2 * GMLP_CH), lambda i: (i, 0)), _acc((1, GMLP_CH)), _acc((1, GMLP_CH)),
                  _acc((GMLP_GROUPS, CHUNK, CHUNK)), _acc((CHUNK, GMLP_GROUPS))],
        out_specs=pl.BlockSpec((T, GMLP_CH), lambda i: (i, 0)),
        scratch_shapes=[pltpu.VMEM((T, GMLP_CH), f32)],
        compiler_params=_cp(("parallel",), 32),
    )(c_in, ln_g, ln_b, w_s, b_s_t)


def _outproj_ln_fwd(conv_out, attn_out, gm_out, w, b, x, ln_g, ln_b):
    S = x.shape[0]
    T = 512

    def body(co_ref, ao_ref, go_ref, w_ref, b_ref, x_ref, g_ref, be_ref, cat_ref, z_ref, y_ref, yb_ref):
        cat = jnp.concatenate([co_ref[...], ao_ref[...], go_ref[...]], axis=1)
        cat_ref[...] = cat
        z = jnp.dot(cat, w_ref[...], preferred_element_type=f32) + b_ref[...] + ALPHA * x_ref[...]
        z_ref[...] = z
        xhat, _ = _ln_stats(z)
        y = xhat * g_ref[...] + be_ref[...]
        y_ref[...] = y
        yb_ref[...] = y.astype(bf16)

    row = lambda c: pl.BlockSpec((T, c), lambda i: (i, 0))
    return pl.pallas_call(
        body, grid=(S // T,), name="outproj_ln_fwd",
        out_shape=(jax.ShapeDtypeStruct((S, D_MODEL), bf16), jax.ShapeDtypeStruct((S, D_MODEL), f32),
                   jax.ShapeDtypeStruct((S, D_MODEL), f32), jax.ShapeDtypeStruct((S, D_MODEL), bf16)),
        in_specs=[row(CONV_CH), row(ATTN_CH), row(GMLP_CH), _resident((D_MODEL, D_MODEL)), _acc((1, D_MODEL)),
                  row(D_MODEL), _acc((1, D_MODEL)), _acc((1, D_MODEL))],
        out_specs=(row(D_MODEL), row(D_MODEL), row(D_MODEL), row(D_MODEL)),
        compiler_params=_cp(("parallel",), 40),
    )(conv_out, attn_out, gm_out, w, b, x, ln_g, ln_b)


GATE_ROWS = 32
GATE_COLS = 128
GATE_MM_COLS = 256
SUBLANES = 8


def _gate_cols(c0):
    return slice(c0, c0 + GATE_COLS), slice(D_FF + c0, D_FF + c0 + GATE_COLS)


def _bcast_rows(ref, k, cs):
    return jnp.broadcast_to(ref[k:k + 1, cs], (GATE_ROWS, GATE_COLS))


def _fold_rows(z):
    acc = z[0:SUBLANES]
    for r in range(SUBLANES, GATE_ROWS, SUBLANES):
        acc = acc + z[r:r + SUBLANES]
    return acc


def _ffn_up_gate_fwd(x1b, w, b, conv_w, conv_b):
    S = x1b.shape[0]
    T = 256
    H = FFN_HALO
    K = FFN_CONV_WIDTH

    def body(x_ref, w_ref, b_ref, cw_ref, cb_ref, hfb_ref, hc_ref, act_ref, hbuf, carry):
        @pl.when(pl.program_id(0) == 0)
        def _():
            carry[...] = jnp.zeros_like(carry)
        x = x_ref[...]
        for m0 in range(0, D_FF, GATE_MM_COLS):
            for cm in (slice(m0, m0 + GATE_MM_COLS), slice(D_FF + m0, D_FF + m0 + GATE_MM_COLS)):
                h = jnp.dot(x, w_ref[:, cm], preferred_element_type=f32) + b_ref[:, cm]
                hbuf[:, cm] = h
                hfb_ref[:, cm] = h.astype(bf16)
            for c0 in range(m0, m0 + GATE_MM_COLS, GATE_COLS):
                cols = _gate_cols(c0)
                wts = [[_bcast_rows(cw_ref, k, cs) for k in range(K)] + [_bcast_rows(cb_ref, 0, cs)] for cs in cols]

                def step(rg, tails, cols=cols, wts=wts):
                    rows = pl.ds(pl.multiple_of(rg * GATE_ROWS, GATE_ROWS), GATE_ROWS)
                    hc, new_tails = [], []
                    for cs, wt, tail in zip(cols, wts, tails):
                        h = hbuf[rows, cs]
                        ext = jnp.concatenate([tail, h], axis=0)
                        acc = wt[K] + wt[K - 1] * h
                        for back in range(1, K):
                            acc = acc + wt[K - 1 - back] * pltpu.roll(ext, back, 0)[H:]
                        hc_ref[rows, cs] = acc
                        hc.append(acc)
                        new_tails.append(h[GATE_ROWS - H:])
                    act_ref[rows, cols[0]] = (hc[0] * _sig(hc[0]) * hc[1]).astype(bf16)
                    return tuple(new_tails)

                tails = lax.fori_loop(0, T // GATE_ROWS, step, tuple(carry[:, cs] for cs in cols), unroll=True)
                for cs, tail in zip(cols, tails):
                    carry[:, cs] = tail

    row = lambda c: pl.BlockSpec((T, c), lambda i: (i, 0))
    return pl.pallas_call(
        body, grid=(S // T,), name="ffn_up_gate_fwd",
        out_shape=(jax.ShapeDtypeStruct((S, 2 * D_FF), bf16), jax.ShapeDtypeStruct((S, 2 * D_FF), f32),
                   jax.ShapeDtypeStruct((S, D_FF), bf16)),
        in_specs=[row(D_MODEL), _resident((D_MODEL, 2 * D_FF)), _acc((1, 2 * D_FF)), _acc((8, 2 * D_FF)),
                  _acc((1, 2 * D_FF))],
        out_specs=(row(2 * D_FF), row(2 * D_FF), row(D_FF)),
        scratch_shapes=[pltpu.VMEM((T, 2 * D_FF), f32), pltpu.VMEM((H, 2 * D_FF), f32)],
        compiler_params=_cp(("arbitrary",), 56),
    )(x1b, w, b, conv_w, conv_b)


def _ffn_down_ln_fwd(act, w, b, x1, ln_g, ln_b):
    S = act.shape[0]
    T = 512

    def body(a_ref, w_ref, b_ref, x_ref, g_ref, be_ref, z_ref, y_ref):
        z = jnp.dot(a_ref[...], w_ref[...], preferred_element_type=f32) + b_ref[...] + ALPHA * x_ref[...]
        z_ref[...] = z
        xhat, _ = _ln_stats(z)
        y_ref[...] = xhat * g_ref[...] + be_ref[...]

    row = lambda c: pl.BlockSpec((T, c), lambda i: (i, 0))
    return pl.pallas_call(
        body, grid=(S // T,), name="ffn_down_ln_fwd",
        out_shape=(jax.ShapeDtypeStruct((S, D_MODEL), f32), jax.ShapeDtypeStruct((S, D_MODEL), f32)),
        in_specs=[row(D_FF), _resident((D_FF, D_MODEL)), _acc((1, D_MODEL)), row(D_MODEL), _acc((1, D_MODEL)),
                  _acc((1, D_MODEL))],
        out_specs=(row(D_MODEL), row(D_MODEL)),
        compiler_params=_cp(("parallel",), 40),
    )(act, w, b, x1, ln_g, ln_b)


def _loss_ln_bwd(y, target, z, ln_g):
    S = y.shape[0]
    T = 512

    def body(y_ref, t_ref, z_ref, g_ref, dz_ref, dzb_ref, loss_ref, dg_ref, db_ref):
        @pl.when(pl.program_id(0) == 0)
        def _():
            loss_ref[...] = jnp.zeros_like(loss_ref)
            dg_ref[...] = jnp.zeros_like(dg_ref)
            db_ref[...] = jnp.zeros_like(db_ref)
        err = y_ref[...] - t_ref[...]
        loss_ref[...] += _colsum(err * err) * (0.5 / D_MODEL)
        dy = err * (1.0 / D_MODEL)
        xhat, rstd = _ln_stats(z_ref[...])
        dz = _ln_bwd(dy, xhat, rstd, g_ref[...])
        dz_ref[...] = dz
        dzb_ref[...] = dz.astype(bf16)
        dg_ref[...] += _colsum(dy * xhat)
        db_ref[...] += _colsum(dy)

    row = pl.BlockSpec((T, D_MODEL), lambda i: (i, 0))
    vec = jax.ShapeDtypeStruct((1, D_MODEL), f32)
    return pl.pallas_call(
        body, grid=(S // T,), name="loss_ln_bwd",
        out_shape=(jax.ShapeDtypeStruct((S, D_MODEL), f32), jax.ShapeDtypeStruct((S, D_MODEL), bf16), vec, vec, vec),
        in_specs=[row, row, row, _acc((1, D_MODEL))],
        out_specs=(row, row, _acc((1, D_MODEL)), _acc((1, D_MODEL)), _acc((1, D_MODEL))),
        compiler_params=_cp(("arbitrary",), 40),
    )(y, target, z, ln_g)


def _dgrad_ln_bwd(g, w, dz_res, z, ln_g, name):
    S, K = g.shape
    T = 256
    with_ln = z is not None

    def body(*refs):
        if with_ln:
            g_ref, w_ref, r_ref, z_ref, lg_ref, dz_ref, dzb_ref, dg_ref, db_ref = refs
        else:
            g_ref, w_ref, r_ref, dx_ref = refs
        dx = lax.dot_general(g_ref[...], w_ref[...], NT_DIMS, preferred_element_type=f32) + ALPHA * r_ref[...]
        if not with_ln:
            dx_ref[...] = dx
            return

        @pl.when(pl.program_id(0) == 0)
        def _():
            dg_ref[...] = jnp.zeros_like(dg_ref)
            db_ref[...] = jnp.zeros_like(db_ref)
        xhat, rstd = _ln_stats(z_ref[...])
        dz = _ln_bwd(dx, xhat, rstd, lg_ref[...])
        dz_ref[...] = dz
        dzb_ref[...] = dz.astype(bf16)
        dg_ref[...] += _colsum(dx * xhat)
        db_ref[...] += _colsum(dx)

    row = pl.BlockSpec((T, D_MODEL), lambda i: (i, 0))
    vec = jax.ShapeDtypeStruct((1, D_MODEL), f32)
    in_specs = [pl.BlockSpec((T, K), lambda i: (i, 0)), _resident((D_MODEL, K)), row]
    args = [g, w, dz_res]
    if with_ln:
        in_specs += [row, _acc((1, D_MODEL))]
        args += [z, ln_g]
        out_shape = (jax.ShapeDtypeStruct((S, D_MODEL), f32), jax.ShapeDtypeStruct((S, D_MODEL), bf16), vec, vec)
        out_specs = (row, row, _acc((1, D_MODEL)), _acc((1, D_MODEL)))
    else:
        out_shape = jax.ShapeDtypeStruct((S, D_MODEL), f32)
        out_specs = row
    return pl.pallas_call(
        body, grid=(S // T,), name=name, out_shape=out_shape, in_specs=in_specs, out_specs=out_specs,
        compiler_params=_cp(("arbitrary",), 48),
    )(*args)


def _ffn_down_gate_bwd(dzb, w_down, hfb, hc, conv_w):
    S = hc.shape[0]
    T = 256
    H = FFN_HALO
    nt = S // T
    K = FFN_CONV_WIDTH

    def body(dz_ref, w_ref, h_ref, hc_ref, cw_ref, dh_ref, dw_ref, dcb_ref, da_buf, carry):
        @pl.when(pl.program_id(0) == 0)
        def _():
            dw_ref[...] = jnp.zeros_like(dw_ref)
            dcb_ref[...] = jnp.zeros_like(dcb_ref)
            carry[...] = jnp.zeros_like(carry)
        da_buf[...] = lax.dot_general(dz_ref[...], w_ref[...], NT_DIMS, preferred_element_type=f32)
        ngroups = T // GATE_ROWS
        for c0 in range(0, D_FF, GATE_COLS):
            cols = _gate_cols(c0)
            wts = [[_bcast_rows(cw_ref, k, cs) for k in range(K)] for cs in cols]

            def step(it, state, cols=cols, wts=wts):
                heads, accs = state
                rows = pl.ds(pl.multiple_of((ngroups - 1 - it) * GATE_ROWS, GATE_ROWS), GATE_ROWS)
                g = hc_ref[rows, cols[0]]
                v = hc_ref[rows, cols[1]]
                da = da_buf[rows, cols[0]]
                sg = _sig(g)
                dms = (da * v * (sg * (1.0 + g * (1.0 - sg))), da * (g * sg))
                new_heads, new_accs = [], []
                for cs, wt, dm, head, acc in zip(cols, wts, dms, heads, accs):
                    h0 = h_ref[rows, cs].astype(f32)
                    ext = jnp.concatenate([dm, head], axis=0)
                    dh = wt[K - 1] * dm
                    acc_k = [None] * K + [acc[K] + _fold_rows(dm)]
                    acc_k[K - 1] = acc[K - 1] + _fold_rows(dm * h0)
                    for ahead in range(1, K):
                        dk = pltpu.roll(ext, GATE_ROWS + H - ahead, 0)[:GATE_ROWS]
                        dh = dh + wt[K - 1 - ahead] * dk
                        acc_k[K - 1 - ahead] = acc[K - 1 - ahead] + _fold_rows(dk * h0)
                    dh_ref[rows, cs] = dh.astype(bf16)
                    new_heads.append(dm[:H])
                    new_accs.append(tuple(acc_k))
                return tuple(new_heads), tuple(new_accs)

            zero = jnp.zeros((SUBLANES, GATE_COLS), f32)
            init = (tuple(carry[:, cs] for cs in cols), tuple(tuple(zero for _ in range(K + 1)) for _ in cols))
            heads, accs = lax.fori_loop(0, ngroups, step, init, unroll=True)
            for cs, head, acc in zip(cols, heads, accs):
                carry[:, cs] = head
                dcb_ref[:, cs] += _colsum(acc[K])
                for k in range(K):
                    dw_ref[k:k + 1, cs] += _colsum(acc[k])

    tile = lambda c: pl.BlockSpec((T, c), lambda i: (nt - 1 - i, 0))
    return pl.pallas_call(
        body, grid=(nt,), name="ffn_down_gate_bwd",
        out_shape=(jax.ShapeDtypeStruct((S, 2 * D_FF), bf16), jax.ShapeDtypeStruct((8, 2 * D_FF), f32),
                   jax.ShapeDtypeStruct((1, 2 * D_FF), f32)),
        in_specs=[tile(D_MODEL), _resident((D_FF, D_MODEL)), tile(2 * D_FF), tile(2 * D_FF), _acc((8, 2 * D_FF))],
        out_specs=(tile(2 * D_FF), _acc((8, 2 * D_FF)), _acc((1, 2 * D_FF))),
        scratch_shapes=[pltpu.VMEM((T, D_FF), f32), pltpu.VMEM((H, 2 * D_FF), f32)],
        compiler_params=_cp(("arbitrary",), 48),
    )(dzb, w_down, hfb, hc, conv_w)


def _wgrad(a, g, tn, name):
    S, K = a.shape
    N = g.shape[1]
    T = 1024 if S % 1024 == 0 else S

    def body(a_ref, g_ref, dw_ref, db_ref):
        @pl.when(pl.program_id(1) == 0)
        def _():
            dw_ref[...] = jnp.zeros_like(dw_ref)
            db_ref[...] = jnp.zeros_like(db_ref)
        gt = g_ref[...]
        dw_ref[...] += lax.dot_general(a_ref[...].astype(bf16), gt, TN_DIMS, preferred_element_type=f32)
        db_ref[...] += _colsum(gt.astype(f32))

    return pl.pallas_call(
        body, grid=(N // tn, S // T), name=name,
        out_shape=(jax.ShapeDtypeStruct((K, N), f32), jax.ShapeDtypeStruct((1, N), f32)),
        in_specs=[pl.BlockSpec((T, K), lambda j, i: (i, 0)), pl.BlockSpec((T, tn), lambda j, i: (i, j))],
        out_specs=(pl.BlockSpec((K, tn), lambda j, i: (0, j)), pl.BlockSpec((1, tn), lambda j, i: (0, j))),
        compiler_params=_cp(("parallel", "arbitrary"), 48),
    )(a, g)


def _outproj_dgrad(dzb, w, attn_out, lse):
    S = dzb.shape[0]
    T = PERM_TILE
    nsl = ATTN_CH // LANES
    n_p = len(DILATIONS)

    def body(g_ref, w_ref, ao_ref, lse_ref, dco_ref, dgo_ref, *rest):
        do_refs = rest[:n_p]
        st_refs = rest[n_p:2 * n_p]
        slabs = rest[2 * n_p:]
        dcat = lax.dot_general(g_ref[...], w_ref[...], NT_DIMS, preferred_element_type=f32)
        dco_ref[...] = dcat[:, :CONV_CH]
        dgo_ref[...] = dcat[:, CONV_CH + ATTN_CH:]
        lane = lax.broadcasted_iota(jnp.int32, (T, LANES), 1)
        st = lse_ref[...]
        for j in range(nsl):
            dO = dcat[:, CONV_CH + LANES * j:CONV_CH + LANES * (j + 1)]
            prod = dO * ao_ref[:, LANES * j:LANES * (j + 1)].astype(f32)
            for a in range(2):
                in_head = (lane < HEAD_DIM) if a == 0 else (lane >= HEAD_DIM)
                delta = jnp.sum(jnp.where(in_head, prod, 0.0), axis=1, keepdims=True)
                st = jnp.where((lane // 16 == 2 * j + a) & (lane % 16 >= 8), delta, st)
            slabs[j][...] = dO
        slabs[nsl][...] = st
        for d, do_ref, st_ref in zip(DILATIONS, do_refs, st_refs):
            for r in range(d):
                for j in range(nsl):
                    do_ref[r, :, LANES * j:LANES * (j + 1)] = _rows_of(slabs[j], r, T // d, d).astype(bf16)
                st_ref[r] = _rows_of(slabs[nsl], r, T // d, d)

    row = lambda c: pl.BlockSpec((T, c), lambda i: (i, 0))
    return pl.pallas_call(
        body, grid=(S // T,), name="outproj_dgrad",
        out_shape=(jax.ShapeDtypeStruct((S, CONV_CH), f32), jax.ShapeDtypeStruct((S, GMLP_CH), f32))
        + tuple(_perm_shape(S, d, ATTN_CH, bf16) for d in DILATIONS)
        + tuple(_perm_shape(S, d, ATTN_BLOCK, f32) for d in DILATIONS),
        in_specs=[row(D_MODEL), _resident((D_MODEL, D_MODEL)), row(ATTN_CH), row(ATTN_BLOCK)],
        out_specs=(row(CONV_CH), row(GMLP_CH)) + tuple(_perm_spec(d, ATTN_CH) for d in DILATIONS)
        + tuple(_perm_spec(d, ATTN_BLOCK) for d in DILATIONS),
        scratch_shapes=_slabs(nsl + 1, T),
        compiler_params=_cp(("parallel",), 40),
    )(dzb, w, attn_out, lse)


def _gmlp_bwd(c_in, dgm, ln_g, ln_b, w_s, b_s_t):
    S = c_in.shape[0]
    T = 512
    nsteps = S // T

    def body(c_ref, dg_ref, g_ref, be_ref, w_ref, bs_ref, dc_ref, dlg_ref, dlb_ref, dw_ref, dbs_ref,
             du_buf, dv_buf, dm_acc):
        i = pl.program_id(0)

        @pl.when(i == 0)
        def _():
            dlg_ref[...] = jnp.zeros_like(dlg_ref)
            dlb_ref[...] = jnp.zeros_like(dlb_ref)
            dw_ref[...] = jnp.zeros_like(dw_ref)
            dm_acc[...] = jnp.zeros_like(dm_acc)
        c = c_ref[...]
        u = c[:, :GMLP_CH]
        xhat, rstd = _ln_stats(c[:, GMLP_CH:])
        vb = (xhat * g_ref[...] + be_ref[...]).astype(bf16)
        dgm_t = dg_ref[...]
        dm_all = dgm_t * u
        for g in range(GMLP_GROUPS):
            wt = _tril_bf16(w_ref[g])
            cs = slice(GMLP_GROUP_DIM * g, GMLP_GROUP_DIM * (g + 1))
            dw_g = jnp.zeros((CHUNK, CHUNK), f32)
            for ci in range(T // CHUNK):
                rs = slice(CHUNK * ci, CHUNK * (ci + 1))
                v_c = vb[rs, cs]
                mixed = jnp.dot(wt, v_c, preferred_element_type=f32) + bs_ref[:, g:g + 1]
                dm = dm_all[rs, cs]
                dmb = dm.astype(bf16)
                du_buf[rs, cs] = dgm_t[rs, cs] * mixed
                dv_buf[rs, cs] = lax.dot_general(wt, dmb, TN_DIMS, preferred_element_type=f32)
                dw_g = dw_g + lax.dot_general(dmb, v_c, NT_DIMS, preferred_element_type=f32)
                dm_acc[:, cs] += dm
            dw_ref[g] += dw_g
        dv = dv_buf[...]
        dvr = _ln_bwd(dv, xhat, rstd, g_ref[...])
        dlg_ref[...] += _colsum(dv * xhat)
        dlb_ref[...] += _colsum(dv)
        dc_ref[:, :GMLP_CH] = du_buf[...].astype(bf16)
        dc_ref[:, GMLP_CH:] = dvr.astype(bf16)

        @pl.when(i == nsteps - 1)
        def _():
            row = lax.broadcasted_iota(jnp.int32, (CHUNK, CHUNK), 0)
            col = lax.broadcasted_iota(jnp.int32, (CHUNK, CHUNK), 1)
            tile = jnp.zeros((CHUNK, CHUNK), f32)
            for g in range(GMLP_GROUPS):
                dw_ref[g] = jnp.where(col <= row, dw_ref[g], 0.0)
                gsum = jnp.sum(dm_acc[:, GMLP_GROUP_DIM * g:GMLP_GROUP_DIM * (g + 1)], axis=1, keepdims=True)
                tile = jnp.where(col == g, gsum, tile)
            dbs_ref[...] = tile

    vec = jax.ShapeDtypeStruct((1, GMLP_CH), f32)
    return pl.pallas_call(
        body, grid=(nsteps,), name="gmlp_bwd",
        out_shape=(jax.ShapeDtypeStruct((S, 2 * GMLP_CH), bf16), vec, vec,
                   jax.ShapeDtypeStruct((GMLP_GROUPS, CHUNK, CHUNK), f32), jax.ShapeDtypeStruct((CHUNK, CHUNK), f32)),
        in_specs=[pl.BlockSpec((T, 2 * GMLP_CH), lambda i: (i, 0)), pl.BlockSpec((T, GMLP_CH), lambda i: (i, 0)),
                  _acc((1, GMLP_CH)), _acc((1, GMLP_CH)), _acc((GMLP_GROUPS, CHUNK, CHUNK)), _acc((CHUNK, GMLP_GROUPS))],
        out_specs=(pl.BlockSpec((T, 2 * GMLP_CH), lambda i: (i, 0)), _acc((1, GMLP_CH)), _acc((1, GMLP_CH)),
                   _acc((GMLP_GROUPS, CHUNK, CHUNK)), _acc((CHUNK, CHUNK))),
        scratch_shapes=[pltpu.VMEM((T, GMLP_CH), f32), pltpu.VMEM((T, GMLP_CH), f32), pltpu.VMEM((CHUNK, GMLP_CH), f32)],
        compiler_params=_cp(("arbitrary",), 32),
    )(c_in, dgm, ln_g, ln_b, w_s, b_s_t)


def _attn_bwd_pattern(qkv, d_out, stats, bias, d):
    _, L, _ = qkv.shape
    B = ATTN_BLOCK
    QB = _attn_tile(L)
    nsb = QB // B
    nt = L // QB
    U = nsb * UNITS_PER_BLOCK
    KV = 2 * ATTN_CH

    def body(cur_ref, hk_ref, hv_ref, do_ref, st_ref, b_ref, dqkv_ref, dbias_ref, lg, dp, pb, dsb, dkv, carry):
        r = pl.program_id(0)
        i = pl.program_id(1)
        n = nt - 1 - i

        @pl.when((r == 0) & (i == 0))
        def _():
            dbias_ref[...] = jnp.zeros_like(dbias_ref)

        @pl.when(i == 0)
        def _():
            carry[...] = jnp.zeros_like(carry)

        def operands(b, j, a):
            rows = slice(B * b, B * (b + 1))
            q2 = cur_ref[rows, LANES * j:LANES * (j + 1)]
            do2 = do_ref[rows, LANES * j:LANES * (j + 1)]
            keep = _head_lanes(a)
            return jnp.where(keep, q2, jnp.zeros_like(q2)), jnp.where(keep, do2, jnp.zeros_like(do2))

        for b in range(nsb):
            valid = _band_mask(b == 0, n)
            for j in range(HEAD_PAIRS):
                k2 = _pair_keys(cur_ref, hk_ref, 1, b, j)
                v2 = _pair_keys(cur_ref, hv_ref, 2, b, j)
                for a in range(2):
                    u = (b * HEAD_PAIRS + j) * 2 + a
                    qm, dom = operands(b, j, a)
                    logits = lax.dot_general(qm, k2, NT_DIMS, preferred_element_type=f32) + b_ref[2 * j + a]
                    lg[B * u:B * (u + 1), :] = jnp.where(valid, logits, NEG)
                    dp[B * u:B * (u + 1), :] = lax.dot_general(dom, v2, NT_DIMS, preferred_element_type=f32)
        for b in range(nsb):
            for j in range(HEAD_PAIRS):
                for a in range(2):
                    u = (b * HEAD_PAIRS + j) * 2 + a
                    rows = slice(B * u, B * (u + 1))
                    lane0 = 32 * j + 16 * a
                    lse = st_ref[B * b:B * (b + 1), lane0:lane0 + 1]
                    delta = st_ref[B * b:B * (b + 1), lane0 + 8:lane0 + 9]
                    p = jnp.exp(lg[rows, :] - lse)
                    ds = p * (dp[rows, :] - delta)
                    pb[rows, :] = p.astype(bf16)
                    dsb[rows, :] = ds.astype(bf16)
                    dbias_ref[2 * j + a] += ds
        dkv[...] = jnp.zeros_like(dkv)
        for b in range(nsb):
            for j in range(HEAD_PAIRS):
                k2 = _pair_keys(cur_ref, hk_ref, 1, b, j)
                dq, dk2, dv2 = [], None, None
                for a in range(2):
                    u = (b * HEAD_PAIRS + j) * 2 + a
                    rows = slice(B * u, B * (u + 1))
                    qm, dom = operands(b, j, a)
                    ds_u = dsb[rows, :]
                    dq.append(jnp.dot(ds_u, k2, preferred_element_type=f32))
                    dk_u = lax.dot_general(ds_u, qm, TN_DIMS, preferred_element_type=f32)
                    dv_u = lax.dot_general(pb[rows, :], dom, TN_DIMS, preferred_element_type=f32)
                    dk2 = dk_u if dk2 is None else dk2 + dk_u
                    dv2 = dv_u if dv2 is None else dv2 + dv_u
                dq2 = jnp.where(_head_lanes(0), dq[0], dq[1]) * (HEAD_DIM ** -0.5)
                dqkv_ref[B * b:B * (b + 1), LANES * j:LANES * (j + 1)] = dq2.astype(bf16)
                dkv[B * b:B * (b + 2), LANES * j:LANES * (j + 1)] += dk2
                dkv[B * b:B * (b + 2), ATTN_CH + LANES * j:ATTN_CH + LANES * (j + 1)] += dv2
        dkv[QB:, :] += carry[...]
        dqkv_ref[:, ATTN_CH:] = dkv[B:, :].astype(bf16)
        carry[...] = dkv[0:B, :]

    halo = lambda part: pl.BlockSpec((None, B, ATTN_CH),
                                     lambda r, i: (r, jnp.maximum((nt - 1 - i) * nsb - 1, 0), part))
    tile_spec = lambda c: pl.BlockSpec((None, QB, c), lambda r, i: (r, nt - 1 - i, 0))
    return pl.pallas_call(
        body, grid=(d, nt), name=f"attn_bwd_d{d}",
        out_shape=(jax.ShapeDtypeStruct((d, L, QKV_CH), bf16), jax.ShapeDtypeStruct((ATTN_HEADS, B, 2 * B), f32)),
        in_specs=[tile_spec(QKV_CH), halo(1), halo(2), tile_spec(ATTN_CH), tile_spec(B),
                  _resident((ATTN_HEADS, B, 2 * B))],
        out_specs=(tile_spec(QKV_CH), _acc((ATTN_HEADS, B, 2 * B))),
        scratch_shapes=[pltpu.VMEM((U * B, 2 * B), f32), pltpu.VMEM((U * B, 2 * B), f32),
                        pltpu.VMEM((U * B, 2 * B), bf16), pltpu.VMEM((U * B, 2 * B), bf16),
                        pltpu.VMEM((B + QB, KV), f32), pltpu.VMEM((B, KV), f32)],
        compiler_params=_cp(("arbitrary", "arbitrary"), 48),
    )(qkv, qkv, qkv, d_out, stats, bias)


def _attn_bwd_merge(d_a, dqkvs, d_c):
    S = d_a.shape[0]
    T = PERM_TILE
    nsl = QKV_CH // LANES
    n_p = len(DILATIONS)

    def body(da_ref, *rest):
        g_refs = rest[:n_p]
        dc_ref, dh_ref = rest[n_p:n_p + 2]
        slabs = rest[n_p + 2:]
        q0 = 2 * CONV_CH
        dh_ref[:, :q0] = da_ref[...]
        dh_ref[:, q0 + QKV_CH:] = dc_ref[...]
        for p, (d, g_ref) in enumerate(zip(DILATIONS, g_refs)):
            for r in range(d):
                for j in range(nsl):
                    _set_rows_of(slabs[p * nsl + j], r, T // d, d, g_ref[r, :, LANES * j:LANES * (j + 1)].astype(f32))
        for j in range(nsl):
            acc = slabs[j][...]
            for p in range(1, n_p):
                acc = acc + slabs[p * nsl + j][...]
            dh_ref[:, q0 + LANES * j:q0 + LANES * (j + 1)] = acc.astype(bf16)

    row = lambda c: pl.BlockSpec((T, c), lambda i: (i, 0))
    return pl.pallas_call(
        body, grid=(S // T,), name="attn_bwd_merge", out_shape=jax.ShapeDtypeStruct((S, IN_CH), bf16),
        in_specs=[row(2 * CONV_CH)] + [_perm_spec(d, QKV_CH) for d in DILATIONS] + [row(2 * GMLP_CH)],
        out_specs=row(IN_CH), scratch_shapes=_slabs(n_p * nsl, T),
        compiler_params=_cp(("parallel",), 48),
    )(d_a, *dqkvs, d_c)


def _bias_table_grad(dbias, buckets):
    n = dbias.shape[0]

    def body(db_ref, bk_ref, o_ref):
        p = pl.program_id(0)
        h = pl.program_id(1)

        @pl.when((p == 0) & (h == 0))
        def _():
            o_ref[...] = jnp.zeros_like(o_ref)
        ids = bk_ref[0]
        db = db_ref[0, 0]
        row = lax.broadcasted_iota(jnp.int32, (N_BUCKETS, 128), 0)
        lane = lax.broadcasted_iota(jnp.int32, (N_BUCKETS, 128), 1)
        upd = jnp.zeros((N_BUCKETS, 128), f32)
        for b in range(N_BUCKETS):
            s = jnp.sum(jnp.sum(jnp.where(ids == b, db, 0.0), axis=1, keepdims=True), axis=0, keepdims=True)
            upd = jnp.where((row == b) & (lane == h), s, upd)
        o_ref[...] += upd

    return pl.pallas_call(
        body, grid=(n, ATTN_HEADS), name="bias_table_grad",
        out_shape=jax.ShapeDtypeStruct((N_BUCKETS, 128), f32),
        in_specs=[pl.BlockSpec((1, 1, ATTN_BLOCK, 2 * ATTN_BLOCK), lambda p, h: (p, h, 0, 0)),
                  pl.BlockSpec((1, ATTN_BLOCK, 2 * ATTN_BLOCK), lambda p, h: (p, 0, 0))],
        out_specs=_acc((N_BUCKETS, 128)),
        compiler_params=_cp(("arbitrary", "arbitrary"), 16),
    )(dbias, buckets)


def _conv_bwd(a_in, hc, dco, dw_w, ln_g, ln_b):
    S = a_in.shape[0]
    T = 512
    hb = T // CONV_HALO
    nsteps = S // T
    R = T + CONV_HALO
    K = CONV_WIDTH

    def body(ap_ref, a_ref, hc_ref, hcn_ref, d_ref, dn_ref, w_ref, g_ref, be_ref,
             da_ref, dw_ref, dcb_ref, dlg_ref, dlb_ref, hg_buf, ext, dbuf):
        i = pl.program_id(0)

        @pl.when(i == 0)
        def _():
            dw_ref[...] = jnp.zeros_like(dw_ref)
            dcb_ref[...] = jnp.zeros_like(dcb_ref)
            dlg_ref[...] = jnp.zeros_like(dlg_ref)
            dlb_ref[...] = jnp.zeros_like(dlb_ref)
        am = a_ref[...]
        ah = ap_ref[...]
        a = am[:, :CONV_CH]
        sg = _sig(am[:, CONV_CH:])
        hg_buf[0:CONV_HALO, :] = jnp.where(i > 0, ah[:, :CONV_CH] * _sig(ah[:, CONV_CH:]), 0.0)
        hg_buf[CONV_HALO:, :] = a * sg
        ext[0:T, :] = hc_ref[...]
        ext[T:, :] = hcn_ref[...]
        xhat, rstd = _ln_stats(ext[...])
        hl = xhat * g_ref[...] + be_ref[...]
        ext[0:T, :] = d_ref[...]
        ext[T:, :] = dn_ref[...]
        sl_ = _sig(hl)
        dhl = ext[...] * (sl_ * (1.0 + hl * (1.0 - sl_)))
        dhc = _ln_bwd(dhl, xhat, rstd, g_ref[...])
        rowi = lax.broadcasted_iota(jnp.int32, (R, CONV_CH), 0)
        dbuf[...] = jnp.where((rowi < T) | (i < nsteps - 1), dhc, 0.0)
        dlg_ref[...] += _colsum(dhl[:T] * xhat[:T])
        dlb_ref[...] += _colsum(dhl[:T])
        dm = dbuf[pl.ds(0, T), :]
        dcb_ref[...] += _colsum(dm)
        dhg = jnp.zeros((T, CONV_CH), f32)
        for k in range(K):
            dw_ref[k:k + 1, :] += _colsum(dm * hg_buf[pl.ds(CONV_HALO - (K - 1) + k, T), :])
            dhg = dhg + w_ref[k:k + 1, :] * dbuf[pl.ds(K - 1 - k, T), :]
        da_ref[:, :CONV_CH] = (dhg * sg).astype(bf16)
        da_ref[:, CONV_CH:] = (dhg * a * sg * (1.0 - sg)).astype(bf16)

    vec = jax.ShapeDtypeStruct((1, CONV_CH), f32)
    nxt = lambda i: (jnp.minimum((i + 1) * hb, nsteps * hb - 1), 0)
    return pl.pallas_call(
        body, grid=(nsteps,), name="conv_bwd",
        out_shape=(jax.ShapeDtypeStruct((S, 2 * CONV_CH), bf16), jax.ShapeDtypeStruct((32, CONV_CH), f32), vec, vec, vec),
        in_specs=[pl.BlockSpec((CONV_HALO, 2 * CONV_CH), lambda i: (jnp.maximum(i * hb - 1, 0), 0)),
                  pl.BlockSpec((T, 2 * CONV_CH), lambda i: (i, 0)),
                  pl.BlockSpec((T, CONV_CH), lambda i: (i, 0)), pl.BlockSpec((CONV_HALO, CONV_CH), nxt),
                  pl.BlockSpec((T, CONV_CH), lambda i: (i, 0)), pl.BlockSpec((CONV_HALO, CONV_CH), nxt),
                  _acc((32, CONV_CH)), _acc((1, CONV_CH)), _acc((1, CONV_CH))],
        out_specs=(pl.BlockSpec((T, 2 * CONV_CH), lambda i: (i, 0)), _acc((32, CONV_CH)), _acc((1, CONV_CH)),
                   _acc((1, CONV_CH)), _acc((1, CONV_CH))),
        scratch_shapes=[pltpu.VMEM((T + CONV_HALO, CONV_CH), f32), pltpu.VMEM((R, CONV_CH), f32),
                        pltpu.VMEM((R, CONV_CH), f32)],
        compiler_params=_cp(("arbitrary",), 32),
    )(a_in, a_in, hc, hc, dco, dco, dw_w, ln_g, ln_b)


def _adamw(g, w, m, v, name):
    R, C = g.shape
    T = R
    for cand in (512, 256, 128, 64, 32, 16, 8):
        if R % cand == 0 and cand * C * 4 <= MIB:
            T = cand
            break
    c1 = 1.0 / (1.0 - ADAM_B1 ** ADAM_STEP)
    c2 = 1.0 / (1.0 - ADAM_B2 ** ADAM_STEP)

    def body(g_ref, w_ref, m_ref, v_ref, d_ref, nm_ref, nv_ref):
        gg = g_ref[...]
        nm = ADAM_B1 * m_ref[...] + (1.0 - ADAM_B1) * gg
        nv = ADAM_B2 * v_ref[...] + (1.0 - ADAM_B2) * (gg * gg)
        nm_ref[...] = nm
        nv_ref[...] = nv
        d_ref[...] = -ADAM_LR * ((nm * c1) / (jnp.sqrt(nv * c2) + ADAM_EPS) + ADAM_WD * w_ref[...])

    blk = pl.BlockSpec((T, C), lambda i: (i, 0))
    sd = jax.ShapeDtypeStruct((R, C), f32)
    return pl.pallas_call(
        body, grid=(R // T,), name=name, out_shape=(sd, sd, sd), in_specs=[blk] * 4, out_specs=(blk, blk, blk),
        compiler_params=_cp(("parallel",), 48),
    )(g, w, m, v)


def _pad_rows(a, rows):
    return jnp.pad(a, ((0, rows - a.shape[0]), (0, 0)))


def _local_step(x, target, wb, sp):
    buckets = jnp.asarray(_bucket_ids())
    bias = _bias_build(sp["rel_bias_table"], buckets)
    saved = []
    xl = x
    for l in range(DEPTH):
        vec = lambda name: sp[name][l][None, :]
        a_in, c_in, *qkv = _inproj_fwd(xl, wb["w_in"][l], vec("b_in"))
        conv_w = _pad_rows(sp["conv_dw_w"][l], 32)
        conv_out, hc = _conv_fwd(a_in, conv_w, vec("conv_dw_b"), vec("conv_ln_g"), vec("conv_ln_b"))
        attn_out, lse = _attn_fwd(qkv, bias)
        bs_t = sp["gmlp_b_s"][l].T
        gm_out = _gmlp_fwd(c_in, vec("gmlp_ln_g"), vec("gmlp_ln_b"), sp["gmlp_w_s"][l], bs_t)
        cat, z1, x1, x1b = _outproj_ln_fwd(conv_out, attn_out, gm_out, wb["w_out"][l], vec("b_out"), xl,
                                           vec("ln1_g"), vec("ln1_b"))
        fconv_w = _pad_rows(sp["ffn_conv_w"][l], 8)
        hfb, fhc, act = _ffn_up_gate_fwd(x1b, wb["ffn_w_up"][l], vec("ffn_b_up"), fconv_w, vec("ffn_conv_b"))
        z2, x2 = _ffn_down_ln_fwd(act, wb["ffn_w_down"][l], vec("ffn_b_down"), x1, vec("ln2_g"), vec("ln2_b"))
        saved.append(dict(x=xl, a_in=a_in, qkv=qkv, c_in=c_in, hc=hc, attn_out=attn_out, lse=lse, cat=cat, z1=z1,
                          x1b=x1b, hfb=hfb, fhc=fhc, act=act, z2=z2, conv_w=conv_w, fconv_w=fconv_w, bs_t=bs_t))
        xl = x2

    grads = {}
    per_layer = {k: [None] * DEPTH for k in (
        "w_in", "b_in", "conv_dw_w", "conv_dw_b", "conv_ln_g", "conv_ln_b", "gmlp_ln_g", "gmlp_ln_b", "gmlp_w_s",
        "gmlp_b_s", "w_out", "b_out", "ln1_g", "ln1_b", "ffn_w_up", "ffn_b_up", "ffn_conv_w", "ffn_conv_b",
        "ffn_w_down", "ffn_b_down", "ln2_g", "ln2_b")}
    dbias_all = []
    l = DEPTH - 1
    vec = lambda name: sp[name][l][None, :]
    dz2, dz2b, loss_part, dg2, db2 = _loss_ln_bwd(xl, target, saved[l]["z2"], vec("ln2_g"))
    loss = jnp.sum(loss_part)
    grad_x = None
    for l in reversed(range(DEPTH)):
        sv = saved[l]
        vec = lambda name: sp[name][l][None, :]
        per_layer["ln2_g"][l] = dg2[0]
        per_layer["ln2_b"][l] = db2[0]
        dw_down, db_down = _wgrad(sv["act"], dz2b, 512, "ffn_down_wgrad")
        per_layer["ffn_w_down"][l] = dw_down
        per_layer["ffn_b_down"][l] = db_down[0]
        dhf, dfcw, dfcb = _ffn_down_gate_bwd(dz2b, wb["ffn_w_down"][l], sv["hfb"], sv["fhc"], sv["fconv_w"])
        per_layer["ffn_conv_w"][l] = dfcw[:FFN_CONV_WIDTH]
        per_layer["ffn_conv_b"][l] = dfcb[0]
        dw_up, db_up = _wgrad(sv["x1b"], dhf, 1408, "ffn_up_wgrad")
        per_layer["ffn_w_up"][l] = dw_up
        per_layer["ffn_b_up"][l] = db_up[0]
        dz1, dz1b, dg1, db1 = _dgrad_ln_bwd(dhf, wb["ffn_w_up"][l], dz2, sv["z1"], vec("ln1_g"), "ffn_up_dgrad_ln")
        per_layer["ln1_g"][l] = dg1[0]
        per_layer["ln1_b"][l] = db1[0]
        dw_out, db_out = _wgrad(sv["cat"], dz1b, 512, "outproj_wgrad")
        per_layer["w_out"][l] = dw_out
        per_layer["b_out"][l] = db_out[0]
        dco, dgo, *perm = _outproj_dgrad(dz1b, wb["w_out"][l], sv["attn_out"], sv["lse"])
        d_outs, stats = perm[:len(DILATIONS)], perm[len(DILATIONS):]
        d_c, dglg, dglb, dws, dbs = _gmlp_bwd(sv["c_in"], dgo, vec("gmlp_ln_g"), vec("gmlp_ln_b"), sp["gmlp_w_s"][l],
                                              sv["bs_t"])
        per_layer["gmlp_ln_g"][l] = dglg[0]
        per_layer["gmlp_ln_b"][l] = dglb[0]
        per_layer["gmlp_w_s"][l] = dws
        per_layer["gmlp_b_s"][l] = dbs[:, :GMLP_GROUPS].T
        dqkvs = []
        for p, d in enumerate(DILATIONS):
            dqkv, dbias = _attn_bwd_pattern(sv["qkv"][p], d_outs[p], stats[p], bias[p], d)
            dqkvs.append(dqkv)
            dbias_all.append(dbias)
        d_a, dcw, dcb, dclg, dclb = _conv_bwd(sv["a_in"], sv["hc"], dco, sv["conv_w"], vec("conv_ln_g"),
                                              vec("conv_ln_b"))
        per_layer["conv_dw_w"][l] = dcw[:CONV_WIDTH]
        per_layer["conv_dw_b"][l] = dcb[0]
        per_layer["conv_ln_g"][l] = dclg[0]
        per_layer["conv_ln_b"][l] = dclb[0]
        dh = _attn_bwd_merge(d_a, dqkvs, d_c)
        dw_in, db_in = _wgrad(sv["x"], dh, 640, "inproj_wgrad")
        per_layer["w_in"][l] = dw_in
        per_layer["b_in"][l] = db_in[0]
        if l > 0:
            pv = saved[l - 1]
            dz2, dz2b, dg2, db2 = _dgrad_ln_bwd(dh, wb["w_in"][l], dz1, pv["z2"], sp["ln2_g"][l - 1][None, :],
                                                "inproj_dgrad_ln")
        else:
            grad_x = _dgrad_ln_bwd(dh, wb["w_in"][l], dz1, None, None, "inproj_dgrad")
    for k, v in per_layer.items():
        grads[k] = v if k in BIG else jnp.stack(v)
    dbias_cat = jnp.stack(dbias_all)
    bk_cat = jnp.concatenate([buckets] * DEPTH, axis=0)
    grads["rel_bias_table"] = _bias_table_grad(dbias_cat, bk_cat)[:, :ATTN_HEADS]
    return loss, grad_x, grads


N_CHIPS = 4
BIG = {"w_in": (D_MODEL, IN_CH, 1), "w_out": (D_MODEL, D_MODEL, 0),
       "ffn_w_up": (D_MODEL, 2 * D_FF, 1), "ffn_w_down": (D_FF, D_MODEL, 0)}
SMALL = ("b_in", "conv_dw_w", "conv_dw_b", "conv_ln_g", "conv_ln_b", "rel_bias_table", "gmlp_ln_g", "gmlp_ln_b",
         "gmlp_w_s", "gmlp_b_s", "b_out", "ln1_g", "ln1_b", "ffn_b_up", "ffn_conv_w", "ffn_conv_b", "ffn_b_down",
         "ln2_g", "ln2_b")
SMALL_SHARDED = ("conv_dw_w", "ffn_conv_w")
WEIGHTS = ("w_in", "b_in", "conv_dw_w", "conv_dw_b", "conv_ln_g", "conv_ln_b", "rel_bias_table", "gmlp_ln_g",
           "gmlp_ln_b", "gmlp_w_s", "gmlp_b_s", "w_out", "b_out", "ln1_g", "ln1_b", "ffn_w_up", "ffn_b_up",
           "ffn_conv_w", "ffn_conv_b", "ffn_w_down", "ffn_b_down", "ln2_g", "ln2_b")
ANY = pl.BlockSpec(memory_space=pl.ANY)


def _position():
    return lax.axis_index("x"), lax.axis_index("y"), lax.axis_index("c")


def _other_chips(x, y):
    return [(1 - x, y), (x, 1 - y), (1 - x, 1 - y)]


def _cast_bf16(a):
    R, C = a.shape
    T = 128

    def body(a_ref, o_ref):
        o_ref[...] = a_ref[...].astype(bf16)

    return pl.pallas_call(
        body, grid=(R // T,), name="cast_bf16", out_shape=jax.ShapeDtypeStruct((R, C), bf16),
        in_specs=[pl.BlockSpec((T, C), lambda i: (i, 0))], out_specs=pl.BlockSpec((T, C), lambda i: (i, 0)),
        compiler_params=_cp(("parallel",), 16),
    )(a)


def _chip_slot(ref, name, l, p):
    K, N, ax = BIG[name]
    if ax == 1:
        sz = N // N_CHIPS
        return ref.at[l, :, pl.ds(pl.multiple_of(p * sz, 128), sz)]
    sz = K // N_CHIPS
    return ref.at[l, pl.ds(pl.multiple_of(p * sz, 16), sz), :]


def _gather_weights(shards, conv_w, fconv_w):
    names = list(BIG)
    n_big = len(names)
    n_t = n_big + 2
    n_chip = 3 * n_t
    n_pass = 3 * n_big

    def body(*refs):
        ins = refs[:n_t]
        outs = refs[n_t:2 * n_t]
        send_sems, recv_sems, pass_send, pass_recv, local_sems = refs[2 * n_t:]
        x, y, c = _position()
        me = 2 * x + y
        chips = _other_chips(x, y)

        def src(t):
            return ins[t].at[c] if t < n_big else ins[t]

        def slot(t, l, p):
            return _chip_slot(outs[t], names[t], l, p) if t < n_big else outs[t].at[p]

        locs, cps = [], []
        for t in range(n_t):
            for l in (range(DEPTH) if t < n_big else (0,)):
                loc = pltpu.make_async_copy(ins[t].at[l] if t < n_big else ins[t], slot(t, l, me),
                                            local_sems.at[DEPTH * t + l])
                loc.start()
                locs.append(loc)
            for k, (px, py) in enumerate(chips):
                cp = pltpu.make_async_remote_copy(
                    src_ref=src(t), dst_ref=slot(t, c, me), send_sem=send_sems.at[3 * t + k],
                    recv_sem=recv_sems.at[3 * t + k], device_id=(px, py, c), device_id_type=MESH_ID)
                cp.start()
                cps.append(cp)
        for t in range(n_t):
            for k, (px, py) in enumerate(chips):
                landed = slot(t, c, 2 * px + py)
                pltpu.make_async_remote_copy(
                    src_ref=src(t), dst_ref=landed, send_sem=send_sems.at[3 * t + k],
                    recv_sem=recv_sems.at[3 * t + k], device_id=(px, py, c), device_id_type=MESH_ID).wait_recv()
                if t < n_big:
                    cp = pltpu.make_async_remote_copy(
                        src_ref=landed, dst_ref=landed, send_sem=pass_send.at[3 * t + k],
                        recv_sem=pass_recv.at[3 * t + k], device_id=(x, y, 1 - c), device_id_type=MESH_ID)
                    cp.start()
                    cps.append(cp)
        for t in range(n_big):
            for k, (px, py) in enumerate(chips):
                from_sibling = slot(t, 1 - c, 2 * px + py)
                pltpu.make_async_remote_copy(
                    src_ref=from_sibling, dst_ref=from_sibling, send_sem=pass_send.at[3 * t + k],
                    recv_sem=pass_recv.at[3 * t + k], device_id=(x, y, 1 - c), device_id_type=MESH_ID).wait_recv()
        for cp in cps:
            cp.wait_send()
        for loc in locs:
            loc.wait()

    ins = [shards[n] for n in names] + [conv_w, fconv_w]
    out_shape = [jax.ShapeDtypeStruct((DEPTH, BIG[n][0], BIG[n][1]), bf16) for n in names]
    out_shape += [jax.ShapeDtypeStruct((N_CHIPS,) + conv_w.shape, f32), jax.ShapeDtypeStruct((N_CHIPS,) + fconv_w.shape, f32)]
    outs = pl.pallas_call(
        body, name="gather_weights", out_shape=tuple(out_shape), in_specs=[ANY] * n_t, out_specs=tuple([ANY] * n_t),
        scratch_shapes=[pltpu.SemaphoreType.DMA((n_chip,)), pltpu.SemaphoreType.DMA((n_chip,)),
                        pltpu.SemaphoreType.DMA((n_pass,)), pltpu.SemaphoreType.DMA((n_pass,)),
                        pltpu.SemaphoreType.DMA((DEPTH * n_t,))],
    )(*ins)
    return dict(zip(names, outs[:n_big])), outs[-2], outs[-1]


def _half(ref, name, c):
    K, N, ax = BIG[name]
    if ax == 1:
        return ref.at[pl.ds(pl.multiple_of(c * (K // 2), 8), K // 2), :]
    return ref.at[:, pl.ds(pl.multiple_of(c * (N // 2), 128), N // 2)]


def _half_shape(name):
    K, N, ax = BIG[name]
    return (K // 2, N) if ax == 1 else (K, N // 2)


def _shard_of_half(ref, name, q):
    K, N, ax = BIG[name]
    if ax == 1:
        sz = N // N_CHIPS
        return ref.at[:, pl.ds(pl.multiple_of(q * sz, 128), sz)]
    sz = K // N_CHIPS
    return ref.at[pl.ds(pl.multiple_of(q * sz, 16), sz), :]


def _shard_half_shape(name):
    K, N, ax = BIG[name]
    return (K // 2, N // N_CHIPS) if ax == 1 else (K // N_CHIPS, N // 2)


def _shard_shape(name):
    K, N, ax = BIG[name]
    return (K, N // N_CHIPS) if ax == 1 else (K // N_CHIPS, N)


def _pair_exchange(tensors):
    n_t = len(tensors)

    def body(*refs):
        ins = refs[:n_t]
        outs = refs[n_t:2 * n_t]
        send_sems, recv_sems = refs[2 * n_t:]
        x, y, c = _position()
        cps = []
        for t, (name, _) in enumerate(tensors):
            cp = pltpu.make_async_remote_copy(
                src_ref=_half(ins[t], name, 1 - c), dst_ref=outs[t], send_sem=send_sems.at[t],
                recv_sem=recv_sems.at[t], device_id=(x, y, 1 - c), device_id_type=MESH_ID)
            cp.start()
            cps.append(cp)
        for cp in cps:
            cp.wait()

    return pl.pallas_call(
        body, name="grad_pair_exchange",
        out_shape=tuple(jax.ShapeDtypeStruct(_half_shape(n), f32) for n, _ in tensors),
        in_specs=[ANY] * n_t, out_specs=tuple([ANY] * n_t),
        scratch_shapes=[pltpu.SemaphoreType.DMA((n_t,)), pltpu.SemaphoreType.DMA((n_t,))],
    )(*[g for _, g in tensors])


def _pair_add(g, rcv, name, c_arr):
    K, N, ax = BIG[name]
    hr, hc = _half_shape(name)
    T = 128
    nrt = hr // T

    def body(c_ref, g_ref, r_ref, o_ref):
        o_ref[...] = (g_ref[...] + r_ref[...]).astype(bf16)

    if ax == 1:
        g_spec = pl.BlockSpec((T, hc), lambda i, c: (c[0] * nrt + i, 0))
    else:
        g_spec = pl.BlockSpec((T, hc), lambda i, c: (i, c[0]))
    plain = pl.BlockSpec((T, hc), lambda i, c: (i, 0))
    return pl.pallas_call(
        body, name="grad_pair_add", out_shape=jax.ShapeDtypeStruct((hr, hc), bf16),
        grid_spec=pltpu.PrefetchScalarGridSpec(num_scalar_prefetch=1, grid=(nrt,), in_specs=[g_spec, plain],
                                               out_specs=plain),
        compiler_params=_cp(("parallel",), 32),
    )(c_arr, g, rcv)


def _chip_exchange(tensors):
    n_t = len(tensors)

    def body(*refs):
        ins = refs[:n_t]
        outs = refs[n_t:2 * n_t]
        send_sems, recv_sems, local_sems = refs[2 * n_t:]
        x, y, c = _position()
        me = 2 * x + y
        chips = _other_chips(x, y)
        locs, cps = [], []
        for t, (name, _) in enumerate(tensors):
            loc = pltpu.make_async_copy(_shard_of_half(ins[t], name, me), outs[t].at[me], local_sems.at[t])
            loc.start()
            locs.append(loc)
            for k, (px, py) in enumerate(chips):
                cp = pltpu.make_async_remote_copy(
                    src_ref=_shard_of_half(ins[t], name, 2 * px + py), dst_ref=outs[t].at[me],
                    send_sem=send_sems.at[3 * t + k], recv_sem=recv_sems.at[3 * t + k],
                    device_id=(px, py, c), device_id_type=MESH_ID)
                cp.start()
                cps.append(cp)
        for t, (name, _) in enumerate(tensors):
            for k, (px, py) in enumerate(chips):
                pltpu.make_async_remote_copy(
                    src_ref=_shard_of_half(ins[t], name, 2 * px + py), dst_ref=outs[t].at[2 * px + py],
                    send_sem=send_sems.at[3 * t + k], recv_sem=recv_sems.at[3 * t + k],
                    device_id=(px, py, c), device_id_type=MESH_ID).wait_recv()
        for cp in cps:
            cp.wait_send()
        for loc in locs:
            loc.wait()

    return pl.pallas_call(
        body, name="grad_chip_exchange",
        out_shape=tuple(jax.ShapeDtypeStruct((N_CHIPS,) + _shard_half_shape(n), g.dtype) for n, g in tensors),
        in_specs=[ANY] * n_t, out_specs=tuple([ANY] * n_t),
        scratch_shapes=[pltpu.SemaphoreType.DMA((3 * n_t,)), pltpu.SemaphoreType.DMA((3 * n_t,)),
                        pltpu.SemaphoreType.DMA((n_t,))],
    )(*[g for _, g in tensors])


def _sum_chips(parts):
    _, R, C = parts.shape
    T = 64

    def body(p_ref, o_ref):
        o_ref[...] = ((p_ref[0].astype(f32) + p_ref[1].astype(f32)) + p_ref[2].astype(f32)) + p_ref[3].astype(f32)

    return pl.pallas_call(
        body, grid=(R // T,), name="grad_sum_chips", out_shape=jax.ShapeDtypeStruct((R, C), f32),
        in_specs=[pl.BlockSpec((N_CHIPS, T, C), lambda i: (0, i, 0))], out_specs=pl.BlockSpec((T, C), lambda i: (i, 0)),
        compiler_params=_cp(("parallel",), 32),
    )(parts)


def _pair_swap(halves):
    n_t = len(halves)

    def body(*refs):
        ins = refs[:n_t]
        outs = refs[n_t:2 * n_t]
        send_sems, recv_sems = refs[2 * n_t:]
        x, y, c = _position()
        cps = []
        for t in range(n_t):
            cp = pltpu.make_async_remote_copy(
                src_ref=ins[t], dst_ref=outs[t], send_sem=send_sems.at[t], recv_sem=recv_sems.at[t],
                device_id=(x, y, 1 - c), device_id_type=MESH_ID)
            cp.start()
            cps.append(cp)
        for cp in cps:
            cp.wait()

    return pl.pallas_call(
        body, name="grad_pair_swap", out_shape=tuple(jax.ShapeDtypeStruct(h.shape, h.dtype) for h in halves),
        in_specs=[ANY] * n_t, out_specs=tuple([ANY] * n_t),
        scratch_shapes=[pltpu.SemaphoreType.DMA((n_t,)), pltpu.SemaphoreType.DMA((n_t,))],
    )(*halves)


def _adamw_halves(own, other, w, m, v, name, l, c_arr, prev):
    K, N, ax = BIG[name]
    R, C = _shard_shape(name)
    hr, hc = _shard_half_shape(name)
    T = 64
    nrt = hr // T
    c1 = 1.0 / (1.0 - ADAM_B1 ** ADAM_STEP)
    c2 = 1.0 / (1.0 - ADAM_B2 ** ADAM_STEP)

    def body(c_ref, own_ref, oth_ref, w_ref, m_ref, v_ref, *rest):
        g_ref, d_ref, nm_ref, nv_ref = rest[-4:]
        gg = jnp.where(pl.program_id(0) == c_ref[0], own_ref[...], oth_ref[...])
        nm = ADAM_B1 * m_ref[...] + (1.0 - ADAM_B1) * gg
        nv = ADAM_B2 * v_ref[...] + (1.0 - ADAM_B2) * (gg * gg)
        g_ref[...] = gg
        nm_ref[...] = nm
        nv_ref[...] = nv
        d_ref[...] = -ADAM_LR * ((nm * c1) / (jnp.sqrt(nv * c2) + ADAM_EPS) + ADAM_WD * w_ref[...])

    half = pl.BlockSpec((T, hc), lambda h, i, c: (i, 0))
    if ax == 1:
        full = pl.BlockSpec((None, T, hc), lambda h, i, c: (l, h * nrt + i, 0))
    else:
        full = pl.BlockSpec((None, T, hc), lambda h, i, c: (l, i, h))
    sd = jax.ShapeDtypeStruct((DEPTH, R, C), f32)
    args = [c_arr, own, other, w, m, v]
    in_specs = [half, half, full, full, full]
    aliases = {}
    if prev is not None:
        args += list(prev)
        in_specs += [ANY] * 4
        aliases = {6 + k: k for k in range(4)}
    return pl.pallas_call(
        body, name="adamw_" + name, out_shape=(sd, sd, sd, sd),
        grid_spec=pltpu.PrefetchScalarGridSpec(num_scalar_prefetch=1, grid=(2, nrt), in_specs=in_specs,
                                               out_specs=(full, full, full, full)),
        input_output_aliases=aliases,
        compiler_params=_cp(("arbitrary", "arbitrary"), 32),
    )(*args)


def _reduce_big_grads(grads):
    c_arr = jnp.reshape(lax.axis_index("c"), (1,)).astype(jnp.int32)
    tensors = [(n, grads[n][l]) for n in BIG for l in range(DEPTH)]
    received = _pair_exchange(tensors)
    pair = [(n, _pair_add(g, r, n, c_arr)) for (n, g), r in zip(tensors, received)]
    parts = _chip_exchange(pair)
    own = [_sum_chips(p) for p in parts]
    other = _pair_swap(own)
    keys = [(n, l) for n in BIG for l in range(DEPTH)]
    return dict(zip(keys, zip(own, other))), c_arr


def _small_allreduce(buf):
    R = buf.shape[0]
    n_dev = 8

    def body(in_ref, out_ref, slots, send_sems, recv_sems):
        x, y, c = _position()
        me = 4 * x + 2 * y + c
        slots[me] = in_ref[...]
        peers = []
        for k in range(1, n_dev):
            px = 1 - x if k & 4 else x
            py = 1 - y if k & 2 else y
            pc = 1 - c if k & 1 else c
            peers.append((px, py, pc))
        cps = []
        for k, peer in enumerate(peers):
            cp = pltpu.make_async_remote_copy(
                src_ref=in_ref, dst_ref=slots.at[me], send_sem=send_sems.at[k], recv_sem=recv_sems.at[k],
                device_id=peer, device_id_type=MESH_ID)
            cp.start()
            cps.append(cp)
        for k, (px, py, pc) in enumerate(peers):
            pltpu.make_async_remote_copy(
                src_ref=in_ref, dst_ref=slots.at[4 * px + 2 * py + pc], send_sem=send_sems.at[k],
                recv_sem=recv_sems.at[k], device_id=(px, py, pc), device_id_type=MESH_ID).wait_recv()
        for cp in cps:
            cp.wait_send()
        acc = slots[0]
        for dv in range(1, n_dev):
            acc = acc + slots[dv]
        out_ref[...] = acc

    vm = pl.BlockSpec(memory_space=pltpu.VMEM)
    return pl.pallas_call(
        body, name="small_allreduce", out_shape=jax.ShapeDtypeStruct((R, 128), f32), in_specs=[vm], out_specs=vm,
        scratch_shapes=[pltpu.VMEM((n_dev, R, 128), f32), pltpu.SemaphoreType.DMA((n_dev - 1,)),
                        pltpu.SemaphoreType.DMA((n_dev - 1,))],
        compiler_params=pltpu.CompilerParams(vmem_limit_bytes=40 * MIB),
    )(buf)


PACK_UNIT = 1024


def _pack(arrs):
    parts = []
    for a in arrs:
        flat = a.reshape(-1)
        n = -(-flat.shape[0] // PACK_UNIT) * PACK_UNIT
        parts.append(jnp.pad(flat, (0, n - flat.shape[0])))
    return jnp.concatenate(parts).reshape(-1, 128)


def _unpack(buf, shapes):
    flat = buf.reshape(-1)
    out, off = [], 0
    for shp in shapes:
        n = int(np.prod(shp))
        out.append(flat[off:off + n].reshape(shp))
        off += -(-n // PACK_UNIT) * PACK_UNIT
    return out


def kernel(x, w_in, b_in, conv_dw_w, conv_dw_b, conv_ln_g, conv_ln_b, rel_bias_table, gmlp_ln_g, gmlp_ln_b, gmlp_w_s, gmlp_b_s, w_out, b_out, ln1_g, ln1_b, ffn_w_up, ffn_b_up, ffn_conv_w, ffn_conv_b, ffn_w_down, ffn_b_down, ln2_g, ln2_b, loss_target, m_w_in, m_b_in, m_conv_dw_w, m_conv_dw_b, m_conv_ln_g, m_conv_ln_b, m_rel_bias_table, m_gmlp_ln_g, m_gmlp_ln_b, m_gmlp_w_s, m_gmlp_b_s, m_w_out, m_b_out, m_ln1_g, m_ln1_b, m_ffn_w_up, m_ffn_b_up, m_ffn_conv_w, m_ffn_conv_b, m_ffn_w_down, m_ffn_b_down, m_ln2_g, m_ln2_b, v_w_in, v_b_in, v_conv_dw_w, v_conv_dw_b, v_conv_ln_g, v_conv_ln_b, v_rel_bias_table, v_gmlp_ln_g, v_gmlp_ln_b, v_gmlp_w_s, v_gmlp_b_s, v_w_out, v_b_out, v_ln1_g, v_ln1_b, v_ffn_w_up, v_ffn_b_up, v_ffn_conv_w, v_ffn_conv_b, v_ffn_w_down, v_ffn_b_down, v_ln2_g, v_ln2_b):
    w = dict(w_in=w_in, b_in=b_in, conv_dw_w=conv_dw_w, conv_dw_b=conv_dw_b, conv_ln_g=conv_ln_g, conv_ln_b=conv_ln_b,
             rel_bias_table=rel_bias_table, gmlp_ln_g=gmlp_ln_g, gmlp_ln_b=gmlp_ln_b, gmlp_w_s=gmlp_w_s,
             gmlp_b_s=gmlp_b_s, w_out=w_out, b_out=b_out, ln1_g=ln1_g, ln1_b=ln1_b, ffn_w_up=ffn_w_up,
             ffn_b_up=ffn_b_up, ffn_conv_w=ffn_conv_w, ffn_conv_b=ffn_conv_b, ffn_w_down=ffn_w_down,
             ffn_b_down=ffn_b_down, ln2_g=ln2_g, ln2_b=ln2_b)
    m = dict(w_in=m_w_in, b_in=m_b_in, conv_dw_w=m_conv_dw_w, conv_dw_b=m_conv_dw_b, conv_ln_g=m_conv_ln_g,
             conv_ln_b=m_conv_ln_b, rel_bias_table=m_rel_bias_table, gmlp_ln_g=m_gmlp_ln_g, gmlp_ln_b=m_gmlp_ln_b,
             gmlp_w_s=m_gmlp_w_s, gmlp_b_s=m_gmlp_b_s, w_out=m_w_out, b_out=m_b_out, ln1_g=m_ln1_g, ln1_b=m_ln1_b,
             ffn_w_up=m_ffn_w_up, ffn_b_up=m_ffn_b_up, ffn_conv_w=m_ffn_conv_w, ffn_conv_b=m_ffn_conv_b,
             ffn_w_down=m_ffn_w_down, ffn_b_down=m_ffn_b_down, ln2_g=m_ln2_g, ln2_b=m_ln2_b)
    v = dict(w_in=v_w_in, b_in=v_b_in, conv_dw_w=v_conv_dw_w, conv_dw_b=v_conv_dw_b, conv_ln_g=v_conv_ln_g,
             conv_ln_b=v_conv_ln_b, rel_bias_table=v_rel_bias_table, gmlp_ln_g=v_gmlp_ln_g, gmlp_ln_b=v_gmlp_ln_b,
             gmlp_w_s=v_gmlp_w_s, gmlp_b_s=v_gmlp_b_s, w_out=v_w_out, b_out=v_b_out, ln1_g=v_ln1_g, ln1_b=v_ln1_b,
             ffn_w_up=v_ffn_w_up, ffn_b_up=v_ffn_b_up, ffn_conv_w=v_ffn_conv_w, ffn_conv_b=v_ffn_conv_b,
             ffn_w_down=v_ffn_w_down, ffn_b_down=v_ffn_b_down, ln2_g=v_ln2_g, ln2_b=v_ln2_b)

    shards = {n: _cast_bf16(w[n].reshape(-1, w[n].shape[-1])).reshape(w[n].shape) for n in BIG}
    wb, conv_stack, fconv_stack = _gather_weights(shards, conv_dw_w, ffn_conv_w)
    sp = {n: w[n] for n in SMALL}
    sp["conv_dw_w"] = jnp.moveaxis(conv_stack, 0, 2).reshape(DEPTH, CONV_WIDTH, CONV_CH)
    sp["ffn_conv_w"] = jnp.moveaxis(fconv_stack, 0, 2).reshape(DEPTH, FFN_CONV_WIDTH, 2 * D_FF)

    loss_local, grad_x, grads = _local_step(x[0], loss_target[0], wb, sp)
    loss = lax.psum(loss_local, ("x", "y", "c"))

    big, c_arr = _reduce_big_grads(grads)
    small_shapes = [grads[n].shape for n in SMALL]
    small = dict(zip(SMALL, _unpack(_small_allreduce(_pack([grads[n] for n in SMALL])), small_shapes)))
    chip = 2 * lax.axis_index("x") + lax.axis_index("y")
    for n in SMALL_SHARDED:
        width = w[n].shape[-1]
        small[n] = lax.dynamic_slice_in_dim(small[n], chip * width, width, axis=2)

    g_out, d_out, m_out, v_out = {}, {}, {}, {}
    for n in BIG:
        outs = None
        for l in range(DEPTH):
            own, other = big[(n, l)]
            outs = _adamw_halves(own, other, w[n], m[n], v[n], n, l, c_arr, outs)
        g_out[n], d_out[n], m_out[n], v_out[n] = outs
    shapes = [small[n].shape for n in SMALL]
    packed = [_pack([src[n] for n in SMALL]) for src in (small, w, m, v)]
    upd = _adamw(*packed, "adamw_small")
    for dst, buf in zip((d_out, m_out, v_out), upd):
        dst.update(zip(SMALL, _unpack(buf, shapes)))
    g_out.update(small)

    return (loss, grad_x[None], *[g_out[n] for n in WEIGHTS], *[d_out[n] for n in WEIGHTS],
            *[m_out[n] for n in WEIGHTS], *[v_out[n] for n in WEIGHTS])
```

```python
import functools
import math

import numpy as np
import jax
import jax.numpy as jnp
from jax import lax
from jax.experimental import pallas as pl
from jax.experimental.pallas import tpu as pltpu

f32 = jnp.float32
bf16 = jnp.bfloat16

D_MODEL = 1024
DEPTH = 2
HEAD_DIM = 64
CONV_CH = 256
CONV_WIDTH = 31
ATTN_HEADS = 8
ATTN_CH = ATTN_HEADS * HEAD_DIM
DILATIONS = (1, 4, 16)
ATTN_BLOCK = 128
N_BUCKETS = 32
MAX_DISTANCE = 2048
GMLP_CH = 256
GMLP_GROUPS = 4
GMLP_GROUP_DIM = GMLP_CH // GMLP_GROUPS
CHUNK = 128
IN_CH = 2 * CONV_CH + 3 * ATTN_CH + 2 * GMLP_CH
D_FF = 2816
FFN_CONV_WIDTH = 3
LN_EPS = 1e-5
ALPHA = (2.0 * DEPTH) ** 0.25
ADAM_LR = 0.001
ADAM_B1 = 0.9
ADAM_B2 = 0.999
ADAM_EPS = 1e-08
ADAM_WD = 0.01
ADAM_STEP = 10

CONV_HALO = 32
FFN_HALO = 8
NEG = -1e30
MIB = 2 ** 20
NT_DIMS = (((1,), (1,)), ((), ()))
TN_DIMS = (((0,), (0,)), ((), ()))
MESH_ID = pl.DeviceIdType.MESH


def _cp(sem, vmem_mib):
    return pltpu.CompilerParams(dimension_semantics=sem, vmem_limit_bytes=vmem_mib * MIB)


def _resident(shape):
    nd = len(shape)
    return pl.BlockSpec(shape, lambda *_: (0,) * nd, pipeline_mode=pl.Buffered(1))


def _acc(shape):
    nd = len(shape)
    return pl.BlockSpec(shape, lambda *_: (0,) * nd)


def _sig(x):
    return 1.0 / (1.0 + jnp.exp(-x))


def _ln_stats(z):
    mu = jnp.mean(z, axis=-1, keepdims=True)
    zc = z - mu
    var = jnp.mean(zc * zc, axis=-1, keepdims=True)
    rstd = lax.rsqrt(var + LN_EPS)
    return zc * rstd, rstd


def _ln_bwd(dy, xhat, rstd, g):
    dxh = dy * g
    m1 = jnp.mean(dxh, axis=-1, keepdims=True)
    m2 = jnp.mean(dxh * xhat, axis=-1, keepdims=True)
    return rstd * (dxh - m1 - xhat * m2)


def _colsum(x):
    return jnp.sum(x, axis=0, keepdims=True)


def _t5_bucket_np(dist):
    max_exact = N_BUCKETS // 2
    dd = np.maximum(dist, 1).astype(np.float64)
    large = max_exact + (np.log(dd / max_exact) / math.log(MAX_DISTANCE / max_exact)
                         * (N_BUCKETS - max_exact)).astype(np.int32)
    large = np.minimum(large, N_BUCKETS - 1)
    return np.where(dist < max_exact, dist, large).astype(np.int32)


def _bucket_ids():
    qi = np.arange(ATTN_BLOCK)[:, None]
    kj = np.arange(2 * ATTN_BLOCK)[None, :]
    dist = np.clip(qi + ATTN_BLOCK - kj, 0, None)
    return np.stack([_t5_bucket_np(dist * d) for d in DILATIONS]).astype(np.int32)


LANES = 128
QKV_CH = 3 * ATTN_CH
PERM_TILE = 512


def _slabs(n, rows):
    return [pltpu.VMEM((rows, LANES), f32)] * n


def _rows_of(slab, r, n, d):
    return slab[...] if d == 1 else slab[pl.ds(r, n, stride=d), :]


def _set_rows_of(slab, r, n, d, val):
    if d == 1:
        slab[...] = val
    else:
        slab[pl.ds(r, n, stride=d), :] = val


def _perm_spec(d, ch):
    return pl.BlockSpec((d, PERM_TILE // d, ch), lambda i: (0, i, 0))


def _perm_shape(S, d, ch, dtype):
    return jax.ShapeDtypeStruct((d, S // d, ch), dtype)


def _inproj_fwd(x, w, b):
    S = x.shape[0]
    T = PERM_TILE
    nsl = QKV_CH // LANES

    def body(x_ref, w_ref, b_ref, a_ref, c_ref, *rest):
        q_refs = rest[:len(DILATIONS)]
        slabs = rest[len(DILATIONS):]
        h = jnp.dot(x_ref[...].astype(bf16), w_ref[...], preferred_element_type=f32) + b_ref[...]
        a_ref[...] = h[:, :2 * CONV_CH]
        q0 = 2 * CONV_CH
        c_ref[...] = h[:, q0 + QKV_CH:]
        for j in range(nsl):
            piece = h[:, q0 + LANES * j:q0 + LANES * (j + 1)]
            if LANES * j < ATTN_CH:
                piece = piece * (HEAD_DIM ** -0.5)
            slabs[j][...] = piece
        for d, q_ref in zip(DILATIONS, q_refs):
            for r in range(d):
                for j in range(nsl):
                    q_ref[r, :, LANES * j:LANES * (j + 1)] = _rows_of(slabs[j], r, T // d, d).astype(bf16)

    row = lambda c: pl.BlockSpec((T, c), lambda i: (i, 0))
    return pl.pallas_call(
        body, grid=(S // T,), name="inproj_fwd",
        out_shape=(jax.ShapeDtypeStruct((S, 2 * CONV_CH), f32), jax.ShapeDtypeStruct((S, 2 * GMLP_CH), f32))
        + tuple(_perm_shape(S, d, QKV_CH, bf16) for d in DILATIONS),
        in_specs=[row(D_MODEL), _resident((D_MODEL, IN_CH)), _resident((1, IN_CH))],
        out_specs=(row(2 * CONV_CH), row(2 * GMLP_CH)) + tuple(_perm_spec(d, QKV_CH) for d in DILATIONS),
        scratch_shapes=_slabs(nsl, T),
        compiler_params=_cp(("parallel",), 48),
    )(x, w, b)


CONV_GROUP = 64


def _window_rolls(starts):
    groups = {}
    for s in starts:
        groups.setdefault((-s) % SUBLANES, []).append(s)
    return dict(sorted(groups.items()))


def _conv_fwd(a_in, dw_w, dw_b, ln_g, ln_b):
    S = a_in.shape[0]
    T = 512
    hb = T // CONV_HALO

    def body(a_ref, halo_ref, w_ref, b_ref, g_ref, be_ref, out_ref, hc_ref, buf):
        i = pl.program_id(0)
        am = a_ref[...]
        ah = halo_ref[...]
        hgh = ah[:, :CONV_CH] * _sig(ah[:, CONV_CH:])
        buf[0:CONV_HALO, :] = jnp.where(i > 0, hgh, 0.0)
        buf[CONV_HALO:, :] = am[:, :CONV_CH] * _sig(am[:, CONV_CH:])
        starts = _window_rolls(range(CONV_HALO - (CONV_WIDTH - 1), CONV_HALO + 1))
        slabs = [slice(LANES * j, LANES * (j + 1)) for j in range(CONV_CH // LANES)]

        def step(g, _):
            r0 = pl.multiple_of(g * CONV_GROUP, CONV_GROUP)
            rows = pl.ds(r0, CONV_GROUP)
            for cs in slabs:
                ext = buf[pl.ds(r0, CONV_GROUP + CONV_HALO), cs]
                acc = jnp.broadcast_to(b_ref[:, cs], (CONV_GROUP, LANES))
                for b, ss in starts.items():
                    rolled = ext if b == 0 else pltpu.roll(ext, b, 0)
                    for s in ss:
                        k = s - (CONV_HALO - (CONV_WIDTH - 1))
                        acc = acc + w_ref[k:k + 1, cs] * rolled[s + b:s + b + CONV_GROUP]
                hc_ref[rows, cs] = acc
            return 0

        lax.fori_loop(0, T // CONV_GROUP, step, 0)
        xhat, _ = _ln_stats(hc_ref[...])
        y = xhat * g_ref[...] + be_ref[...]
        out_ref[...] = (y * _sig(y)).astype(bf16)

    return pl.pallas_call(
        body, grid=(S // T,), name="conv_fwd",
        out_shape=(jax.ShapeDtypeStruct((S, CONV_CH), bf16), jax.ShapeDtypeStruct((S, CONV_CH), f32)),
        in_specs=[pl.BlockSpec((T, 2 * CONV_CH), lambda i: (i, 0)),
                  pl.BlockSpec((CONV_HALO, 2 * CONV_CH), lambda i: (jnp.maximum(i * hb - 1, 0), 0)),
                  _acc((32, CONV_CH)), _acc((1, CONV_CH)), _acc((1, CONV_CH)), _acc((1, CONV_CH))],
        out_specs=(pl.BlockSpec((T, CONV_CH), lambda i: (i, 0)), pl.BlockSpec((T, CONV_CH), lambda i: (i, 0))),
        scratch_shapes=[pltpu.VMEM((T + CONV_HALO, CONV_CH), f32)],
        compiler_params=_cp(("parallel",), 32),
    )(a_in, a_in, dw_w, dw_b, ln_g, ln_b)


def _bias_build(table, buckets):
    def body(t_ref, bk_ref, o_ref):
        h = pl.program_id(1)
        ids = bk_ref[0]
        acc = jnp.zeros((ATTN_BLOCK, 2 * ATTN_BLOCK), f32)
        for b in range(N_BUCKETS):
            acc = jnp.where(ids == b, t_ref[b, h], acc)
        o_ref[0, 0] = acc

    return pl.pallas_call(
        body, grid=(len(DILATIONS), ATTN_HEADS), name="bias_build",
        out_shape=jax.ShapeDtypeStruct((len(DILATIONS), ATTN_HEADS, ATTN_BLOCK, 2 * ATTN_BLOCK), f32),
        in_specs=[pl.BlockSpec(memory_space=pltpu.SMEM),
                  pl.BlockSpec((1, ATTN_BLOCK, 2 * ATTN_BLOCK), lambda p, h: (p, 0, 0))],
        out_specs=pl.BlockSpec((1, 1, ATTN_BLOCK, 2 * ATTN_BLOCK), lambda p, h: (p, h, 0, 0)),
        compiler_params=_cp(("arbitrary", "arbitrary"), 16),
    )(table, buckets)


def _head_tile(tile, h, col):
    lane_head = lax.broadcasted_iota(jnp.int32, tile.shape, 1) // 16
    return jnp.where(lane_head == h, col, tile)


HEAD_PAIRS = ATTN_HEADS // 2
UNITS_PER_BLOCK = ATTN_HEADS


def _attn_tile(L):
    return min(512, L)


def _band_mask(first_block, n):
    B = ATTN_BLOCK
    row = lax.broadcasted_iota(jnp.int32, (B, 2 * B), 0)
    col = lax.broadcasted_iota(jnp.int32, (B, 2 * B), 1)
    valid = (col >= row) & (col <= row + B)
    if first_block:
        valid = valid & ((col >= B) | (n > 0))
    return valid


def _head_lanes(a):
    lane = lax.broadcasted_iota(jnp.int32, (ATTN_BLOCK, LANES), 1)
    return (lane < HEAD_DIM) if a == 0 else (lane >= HEAD_DIM)


def _pair_keys(cur_ref, halo_ref, part, b, j):
    B = ATTN_BLOCK
    c0 = part * ATTN_CH + LANES * j
    own = cur_ref[B * b:B * (b + 1), c0:c0 + LANES]
    prev = halo_ref[:, LANES * j:LANES * (j + 1)] if b == 0 else cur_ref[B * (b - 1):B * b, c0:c0 + LANES]
    return jnp.concatenate([prev, own], axis=0)


def _attn_fwd_pattern(qkv, bias, d):
    _, L, _ = qkv.shape
    B = ATTN_BLOCK
    QB = _attn_tile(L)
    nsb = QB // B
    U = nsb * UNITS_PER_BLOCK

    def body(cur_ref, hk_ref, hv_ref, b_ref, o_ref, lse_ref, lg, pb):
        n = pl.program_id(1)
        for b in range(nsb):
            valid = _band_mask(b == 0, n)
            for j in range(HEAD_PAIRS):
                q2 = cur_ref[B * b:B * (b + 1), LANES * j:LANES * (j + 1)]
                k2 = _pair_keys(cur_ref, hk_ref, 1, b, j)
                for a in range(2):
                    u = (b * HEAD_PAIRS + j) * 2 + a
                    qm = jnp.where(_head_lanes(a), q2, jnp.zeros_like(q2))
                    logits = lax.dot_general(qm, k2, NT_DIMS, preferred_element_type=f32) + b_ref[2 * j + a]
                    lg[B * u:B * (u + 1), :] = jnp.where(valid, logits, NEG)
        m = jnp.max(lg[...], axis=1, keepdims=True)
        p = jnp.exp(lg[...] - m)
        s = jnp.sum(p, axis=1, keepdims=True)
        pb[...] = p.astype(bf16)
        lse = m + jnp.log(s)
        inv = 1.0 / s
        for b in range(nsb):
            tile = jnp.zeros((B, B), f32)
            for j in range(HEAD_PAIRS):
                v2 = _pair_keys(cur_ref, hv_ref, 2, b, j)
                outs = []
                for a in range(2):
                    u = (b * HEAD_PAIRS + j) * 2 + a
                    rows = slice(B * u, B * (u + 1))
                    outs.append(jnp.dot(pb[rows, :], v2, preferred_element_type=f32) * inv[rows])
                    tile = _head_tile(tile, 2 * j + a, lse[rows])
                o_ref[B * b:B * (b + 1), LANES * j:LANES * (j + 1)] = jnp.where(_head_lanes(0), outs[0], outs[1])
            lse_ref[B * b:B * (b + 1), :] = tile

    halo = lambda part: pl.BlockSpec((None, B, ATTN_CH), lambda r, n: (r, jnp.maximum(n * nsb - 1, 0), part))
    tile_spec = lambda c: pl.BlockSpec((None, QB, c), lambda r, n: (r, n, 0))
    return pl.pallas_call(
        body, grid=(d, L // QB), name=f"attn_fwd_d{d}",
        out_shape=(jax.ShapeDtypeStruct((d, L, ATTN_CH), f32), jax.ShapeDtypeStruct((d, L, B), f32)),
        in_specs=[tile_spec(QKV_CH), halo(1), halo(2), _resident((ATTN_HEADS, B, 2 * B))],
        out_specs=(tile_spec(ATTN_CH), tile_spec(B)),
        scratch_shapes=[pltpu.VMEM((U * B, 2 * B), f32), pltpu.VMEM((U * B, 2 * B), bf16)],
        compiler_params=_cp(("parallel", "parallel"), 40),
    )(qkv, qkv, qkv, bias)


def _attn_merge(parts):
    S = parts[0][0].shape[0] * parts[0][0].shape[1]
    T = PERM_TILE
    nsl = ATTN_CH // LANES
    n_p = len(DILATIONS)

    def body(*refs):
        ins = refs[:2 * n_p]
        out_ref, lse_ref = refs[2 * n_p:2 * n_p + 2]
        slabs = refs[2 * n_p + 2:]
        lses = []
        for p, d in enumerate(DILATIONS):
            o_ref, l_ref = ins[2 * p], ins[2 * p + 1]
            osl = slabs[p * (nsl + 1):p * (nsl + 1) + nsl]
            lsl = slabs[p * (nsl + 1) + nsl]
            for r in range(d):
                for j in range(nsl):
                    _set_rows_of(osl[j], r, T // d, d, o_ref[r, :, LANES * j:LANES * (j + 1)])
                _set_rows_of(lsl, r, T // d, d, l_ref[r])
            lses.append(lsl[...])
        big = functools.reduce(jnp.maximum, lses)
        ws = [jnp.exp(l - big) for l in lses]
        tot = functools.reduce(lambda a_, b_: a_ + b_, ws)
        lse_ref[...] = big + jnp.log(tot)
        ws = [w / tot for w in ws]
        for j in range(nsl):
            acc = jnp.zeros((T, LANES), f32)
            for p in range(n_p):
                wa = ws[p][:, 32 * j:32 * j + 1]
                wb = ws[p][:, 32 * j + 16:32 * j + 17]
                lane = lax.broadcasted_iota(jnp.int32, (T, LANES), 1)
                acc = acc + jnp.where(lane < HEAD_DIM, wa, wb) * slabs[p * (nsl + 1) + j][...]
            out_ref[:, LANES * j:LANES * (j + 1)] = acc.astype(bf16)

    in_specs, args = [], []
    for (o, l), d in zip(parts, DILATIONS):
        in_specs += [_perm_spec(d, ATTN_CH), _perm_spec(d, ATTN_BLOCK)]
        args += [o, l]
    row = lambda c: pl.BlockSpec((T, c), lambda i: (i, 0))
    return pl.pallas_call(
        body, grid=(S // T,), name="attn_merge",
        out_shape=(jax.ShapeDtypeStruct((S, ATTN_CH), bf16), jax.ShapeDtypeStruct((S, ATTN_BLOCK), f32)),
        in_specs=in_specs, out_specs=(row(ATTN_CH), row(ATTN_BLOCK)),
        scratch_shapes=_slabs(n_p * (nsl + 1), T),
        compiler_params=_cp(("parallel",), 40),
    )(*args)


def _attn_fwd(qkvs, bias):
    parts = [_attn_fwd_pattern(q, bias[p], d) for p, (q, d) in enumerate(zip(qkvs, DILATIONS))]
    return _attn_merge(parts)


def _tril_bf16(w):
    row = lax.broadcasted_iota(jnp.int32, (CHUNK, CHUNK), 0)
    col = lax.broadcasted_iota(jnp.int32, (CHUNK, CHUNK), 1)
    return jnp.where(col <= row, w, 0.0).astype(bf16)


def _gmlp_fwd(c_in, ln_g, ln_b, w_s, b_s_t):
    S = c_in.shape[0]
    T = 512

    def body(c_ref, g_ref, be_ref, w_ref, bs_ref, out_ref, mix):
        c = c_ref[...]
        xhat, _ = _ln_stats(c[:, GMLP_CH:])
        vb = (xhat * g_ref[...] + be_ref[...]).astype(bf16)
        for g in range(GMLP_GROUPS):
            wt = _tril_bf16(w_ref[g])
            cs = slice(GMLP_GROUP_DIM * g, GMLP_GROUP_DIM * (g + 1))
            for ci in range(T // CHUNK):
                rs = slice(CHUNK * ci, CHUNK * (ci + 1))
                mix[rs, cs] = jnp.dot(wt, vb[rs, cs], preferred_element_type=f32) + bs_ref[:, g:g + 1]
        out_ref[...] = (c[:, :GMLP_CH] * mix[...]).astype(bf16)

    return pl.pallas_call(
        body, grid=(S // T,), name="gmlp_fwd",
        out_shape=jax.ShapeDtypeStruct((S, GMLP_CH), bf16),
        in_specs=[pl.BlockSpec((T, 2 * GMLP_CH), lambda i: (i, 0)), _acc((1, GMLP_CH)), _acc((1, GMLP_CH)),
                  _acc((GMLP_GROUPS, CHUNK, CHUNK)), _acc((CHUNK, GMLP_GROUPS))],
        out_specs=pl.BlockSpec((T, GMLP_CH), lambda i: (i, 0)),
        scratch_shapes=[pltpu.VMEM((T, GMLP_CH), f32)],
        compiler_params=_cp(("parallel",), 32),
    )(c_in, ln_g, ln_b, w_s, b_s_t)


def _outproj_ln_fwd(conv_out, attn_out, gm_out, w, b, x, ln_g, ln_b):
    S = x.shape[0]
    T = 512

    def body(co_ref, ao_ref, go_ref, w_ref, b_ref, x_ref, g_ref, be_ref, cat_ref, z_ref, y_ref, yb_ref):
        cat = jnp.concatenate([co_ref[...], ao_ref[...], go_ref[...]], axis=1)
        cat_ref[...] = cat
        z = jnp.dot(cat, w_ref[...], preferred_element_type=f32) + b_ref[...] + ALPHA * x_ref[...]
        z_ref[...] = z
        xhat, _ = _ln_stats(z)
        y = xhat * g_ref[...] + be_ref[...]
        y_ref[...] = y
        yb_ref[...] = y.astype(bf16)

    row = lambda c: pl.BlockSpec((T, c), lambda i: (i, 0))
    return pl.pallas_call(
        body, grid=(S // T,), name="outproj_ln_fwd",
        out_shape=(jax.ShapeDtypeStruct((S, D_MODEL), bf16), jax.ShapeDtypeStruct((S, D_MODEL), f32),
                   jax.ShapeDtypeStruct((S, D_MODEL), f32), jax.ShapeDtypeStruct((S, D_MODEL), bf16)),
        in_specs=[row(CONV_CH), row(ATTN_CH), row(GMLP_CH), _resident((D_MODEL, D_MODEL)), _acc((1, D_MODEL)),
                  row(D_MODEL), _acc((1, D_MODEL)), _acc((1, D_MODEL))],
        out_specs=(row(D_MODEL), row(D_MODEL), row(D_MODEL), row(D_MODEL)),
        compiler_params=_cp(("parallel",), 40),
    )(conv_out, attn_out, gm_out, w, b, x, ln_g, ln_b)


GATE_ROWS = 32
GATE_COLS = 128
GATE_MM_COLS = 256
SUBLANES = 8


def _gate_cols(c0):
    return slice(c0, c0 + GATE_COLS), slice(D_FF + c0, D_FF + c0 + GATE_COLS)


def _bcast_rows(ref, k, cs):
    return jnp.broadcast_to(ref[k:k + 1, cs], (GATE_ROWS, GATE_COLS))


def _fold_rows(z):
    acc = z[0:SUBLANES]
    for r in range(SUBLANES, GATE_ROWS, SUBLANES):
        acc = acc + z[r:r + SUBLANES]
    return acc


def _ffn_up_gate_fwd(x1b, w, b, conv_w, conv_b):
    S = x1b.shape[0]
    T = 256
    H = FFN_HALO
    K = FFN_CONV_WIDTH

    def body(x_ref, w_ref, b_ref, cw_ref, cb_ref, hfb_ref, hc_ref, act_ref, hbuf, carry):
        @pl.when(pl.program_id(0) == 0)
        def _():
            carry[...] = jnp.zeros_like(carry)
        x = x_ref[...]
        for m0 in range(0, D_FF, GATE_MM_COLS):
            for cm in (slice(m0, m0 + GATE_MM_COLS), slice(D_FF + m0, D_FF + m0 + GATE_MM_COLS)):
                h = jnp.dot(x, w_ref[:, cm], preferred_element_type=f32) + b_ref[:, cm]
                hbuf[:, cm] = h
                hfb_ref[:, cm] = h.astype(bf16)
            for c0 in range(m0, m0 + GATE_MM_COLS, GATE_COLS):
                cols = _gate_cols(c0)
                wts = [[_bcast_rows(cw_ref, k, cs) for k in range(K)] + [_bcast_rows(cb_ref, 0, cs)] for cs in cols]

                def step(rg, tails, cols=cols, wts=wts):
                    rows = pl.ds(pl.multiple_of(rg * GATE_ROWS, GATE_ROWS), GATE_ROWS)
                    hc, new_tails = [], []
                    for cs, wt, tail in zip(cols, wts, tails):
                        h = hbuf[rows, cs]
                        ext = jnp.concatenate([tail, h], axis=0)
                        acc = wt[K] + wt[K - 1] * h
                        for back in range(1, K):
                            acc = acc + wt[K - 1 - back] * pltpu.roll(ext, back, 0)[H:]
                        hc_ref[rows, cs] = acc
                        hc.append(acc)
                        new_tails.append(h[GATE_ROWS - H:])
                    act_ref[rows, cols[0]] = (hc[0] * _sig(hc[0]) * hc[1]).astype(bf16)
                    return tuple(new_tails)

                tails = lax.fori_loop(0, T // GATE_ROWS, step, tuple(carry[:, cs] for cs in cols), unroll=True)
                for cs, tail in zip(cols, tails):
                    carry[:, cs] = tail

    row = lambda c: pl.BlockSpec((T, c), lambda i: (i, 0))
    return pl.pallas_call(
        body, grid=(S // T,), name="ffn_up_gate_fwd",
        out_shape=(jax.ShapeDtypeStruct((S, 2 * D_FF), bf16), jax.ShapeDtypeStruct((S, 2 * D_FF), f32),
                   jax.ShapeDtypeStruct((S, D_FF), bf16)),
        in_specs=[row(D_MODEL), _resident((D_MODEL, 2 * D_FF)), _acc((1, 2 * D_FF)), _acc((8, 2 * D_FF)),
                  _acc((1, 2 * D_FF))],
        out_specs=(row(2 * D_FF), row(2 * D_FF), row(D_FF)),
        scratch_shapes=[pltpu.VMEM((T, 2 * D_FF), f32), pltpu.VMEM((H, 2 * D_FF), f32)],
        compiler_params=_cp(("arbitrary",), 56),
    )(x1b, w, b, conv_w, conv_b)


def _ffn_down_ln_fwd(act, w, b, x1, ln_g, ln_b):
    S = act.shape[0]
    T = 512

    def body(a_ref, w_ref, b_ref, x_ref, g_ref, be_ref, z_ref, y_ref):
        z = jnp.dot(a_ref[...], w_ref[...], preferred_element_type=f32) + b_ref[...] + ALPHA * x_ref[...]
        z_ref[...] = z
        xhat, _ = _ln_stats(z)
        y_ref[...] = xhat * g_ref[...] + be_ref[...]

    row = lambda c: pl.BlockSpec((T, c), lambda i: (i, 0))
    return pl.pallas_call(
        body, grid=(S // T,), name="ffn_down_ln_fwd",
        out_shape=(jax.ShapeDtypeStruct((S, D_MODEL), f32), jax.ShapeDtypeStruct((S, D_MODEL), f32)),
        in_specs=[row(D_FF), _resident((D_FF, D_MODEL)), _acc((1, D_MODEL)), row(D_MODEL), _acc((1, D_MODEL)),
                  _acc((1, D_MODEL))],
        out_specs=(row(D_MODEL), row(D_MODEL)),
        compiler_params=_cp(("parallel",), 40),
    )(act, w, b, x1, ln_g, ln_b)


def _loss_ln_bwd(y, target, z, ln_g):
    S = y.shape[0]
    T = 512

    def body(y_ref, t_ref, z_ref, g_ref, dz_ref, dzb_ref, loss_ref, dg_ref, db_ref):
        @pl.when(pl.program_id(0) == 0)
        def _():
            loss_ref[...] = jnp.zeros_like(loss_ref)
            dg_ref[...] = jnp.zeros_like(dg_ref)
            db_ref[...] = jnp.zeros_like(db_ref)
        err = y_ref[...] - t_ref[...]
        loss_ref[...] += _colsum(err * err) * (0.5 / D_MODEL)
        dy = err * (1.0 / D_MODEL)
        xhat, rstd = _ln_stats(z_ref[...])
        dz = _ln_bwd(dy, xhat, rstd, g_ref[...])
        dz_ref[...] = dz
        dzb_ref[...] = dz.astype(bf16)
        dg_ref[...] += _colsum(dy * xhat)
        db_ref[...] += _colsum(dy)

    row = pl.BlockSpec((T, D_MODEL), lambda i: (i, 0))
    vec = jax.ShapeDtypeStruct((1, D_MODEL), f32)
    return pl.pallas_call(
        body, grid=(S // T,), name="loss_ln_bwd",
        out_shape=(jax.ShapeDtypeStruct((S, D_MODEL), f32), jax.ShapeDtypeStruct((S, D_MODEL), bf16), vec, vec, vec),
        in_specs=[row, row, row, _acc((1, D_MODEL))],
        out_specs=(row, row, _acc((1, D_MODEL)), _acc((1, D_MODEL)), _acc((1, D_MODEL))),
        compiler_params=_cp(("arbitrary",), 40),
    )(y, target, z, ln_g)


def _dgrad_ln_bwd(g, w, dz_res, z, ln_g, name):
    S, K = g.shape
    T = 256
    with_ln = z is not None

    def body(*refs):
        if with_ln:
            g_ref, w_ref, r_ref, z_ref, lg_ref, dz_ref, dzb_ref, dg_ref, db_ref = refs
        else:
            g_ref, w_ref, r_ref, dx_ref = refs
        dx = lax.dot_general(g_ref[...], w_ref[...], NT_DIMS, preferred_element_type=f32) + ALPHA * r_ref[...]
        if not with_ln:
            dx_ref[...] = dx
            return

        @pl.when(pl.program_id(0) == 0)
        def _():
            dg_ref[...] = jnp.zeros_like(dg_ref)
            db_ref[...] = jnp.zeros_like(db_ref)
        xhat, rstd = _ln_stats(z_ref[...])
        dz = _ln_bwd(dx, xhat, rstd, lg_ref[...])
        dz_ref[...] = dz
        dzb_ref[...] = dz.astype(bf16)
        dg_ref[...] += _colsum(dx * xhat)
        db_ref[...] += _colsum(dx)

    row = pl.BlockSpec((T, D_MODEL), lambda i: (i, 0))
    vec = jax.ShapeDtypeStruct((1, D_MODEL), f32)
    in_specs = [pl.BlockSpec((T, K), lambda i: (i, 0)), _resident((D_MODEL, K)), row]
    args = [g, w, dz_res]
    if with_ln:
        in_specs += [row, _acc((1, D_MODEL))]
        args += [z, ln_g]
        out_shape = (jax.ShapeDtypeStruct((S, D_MODEL), f32), jax.ShapeDtypeStruct((S, D_MODEL), bf16), vec, vec)
        out_specs = (row, row, _acc((1, D_MODEL)), _acc((1, D_MODEL)))
    else:
        out_shape = jax.ShapeDtypeStruct((S, D_MODEL), f32)
        out_specs = row
    return pl.pallas_call(
        body, grid=(S // T,), name=name, out_shape=out_shape, in_specs=in_specs, out_specs=out_specs,
        compiler_params=_cp(("arbitrary",), 48),
    )(*args)


def _ffn_down_gate_bwd(dzb, w_down, hfb, hc, conv_w):
    S = hc.shape[0]
    T = 256
    H = FFN_HALO
    nt = S // T
    K = FFN_CONV_WIDTH

    def body(dz_ref, w_ref, h_ref, hc_ref, cw_ref, dh_ref, dw_ref, dcb_ref, da_buf, carry):
        @pl.when(pl.program_id(0) == 0)
        def _():
            dw_ref[...] = jnp.zeros_like(dw_ref)
            dcb_ref[...] = jnp.zeros_like(dcb_ref)
            carry[...] = jnp.zeros_like(carry)
        da_buf[...] = lax.dot_general(dz_ref[...], w_ref[...], NT_DIMS, preferred_element_type=f32)
        ngroups = T // GATE_ROWS
        for c0 in range(0, D_FF, GATE_COLS):
            cols = _gate_cols(c0)
            wts = [[_bcast_rows(cw_ref, k, cs) for k in range(K)] for cs in cols]

            def step(it, state, cols=cols, wts=wts):
                heads, accs = state
                rows = pl.ds(pl.multiple_of((ngroups - 1 - it) * GATE_ROWS, GATE_ROWS), GATE_ROWS)
                g = hc_ref[rows, cols[0]]
                v = hc_ref[rows, cols[1]]
                da = da_buf[rows, cols[0]]
                sg = _sig(g)
                dms = (da * v * (sg * (1.0 + g * (1.0 - sg))), da * (g * sg))
                new_heads, new_accs = [], []
                for cs, wt, dm, head, acc in zip(cols, wts, dms, heads, accs):
                    h0 = h_ref[rows, cs].astype(f32)
                    ext = jnp.concatenate([dm, head], axis=0)
                    dh = wt[K - 1] * dm
                    acc_k = [None] * K + [acc[K] + _fold_rows(dm)]
                    acc_k[K - 1] = acc[K - 1] + _fold_rows(dm * h0)
                    for ahead in range(1, K):
                        dk = pltpu.roll(ext, GATE_ROWS + H - ahead, 0)[:GATE_ROWS]
                        dh = dh + wt[K - 1 - ahead] * dk
                        acc_k[K - 1 - ahead] = acc[K - 1 - ahead] + _fold_rows(dk * h0)
                    dh_ref[rows, cs] = dh.astype(bf16)
                    new_heads.append(dm[:H])
                    new_accs.append(tuple(acc_k))
                return tuple(new_heads), tuple(new_accs)

            zero = jnp.zeros((SUBLANES, GATE_COLS), f32)
            init = (tuple(carry[:, cs] for cs in cols), tuple(tuple(zero for _ in range(K + 1)) for _ in cols))
            heads, accs = lax.fori_loop(0, ngroups, step, init, unroll=True)
            for cs, head, acc in zip(cols, heads, accs):
                carry[:, cs] = head
                dcb_ref[:, cs] += _colsum(acc[K])
                for k in range(K):
                    dw_ref[k:k + 1, cs] += _colsum(acc[k])

    tile = lambda c: pl.BlockSpec((T, c), lambda i: (nt - 1 - i, 0))
    return pl.pallas_call(
        body, grid=(nt,), name="ffn_down_gate_bwd",
        out_shape=(jax.ShapeDtypeStruct((S, 2 * D_FF), bf16), jax.ShapeDtypeStruct((8, 2 * D_FF), f32),
                   jax.ShapeDtypeStruct((1, 2 * D_FF), f32)),
        in_specs=[tile(D_MODEL), _resident((D_FF, D_MODEL)), tile(2 * D_FF), tile(2 * D_FF), _acc((8, 2 * D_FF))],
        out_specs=(tile(2 * D_FF), _acc((8, 2 * D_FF)), _acc((1, 2 * D_FF))),
        scratch_shapes=[pltpu.VMEM((T, D_FF), f32), pltpu.VMEM((H, 2 * D_FF), f32)],
        compiler_params=_cp(("arbitrary",), 48),
    )(dzb, w_down, hfb, hc, conv_w)


def _wgrad(a, g, tn, name):
    S, K = a.shape
    N = g.shape[1]
    T = 1024 if S % 1024 == 0 else S

    def body(a_ref, g_ref, dw_ref, db_ref):
        @pl.when(pl.program_id(1) == 0)
        def _():
            dw_ref[...] = jnp.zeros_like(dw_ref)
            db_ref[...] = jnp.zeros_like(db_ref)
        gt = g_ref[...]
        dw_ref[...] += lax.dot_general(a_ref[...].astype(bf16), gt, TN_DIMS, preferred_element_type=f32)
        db_ref[...] += _colsum(gt.astype(f32))

    return pl.pallas_call(
        body, grid=(N // tn, S // T), name=name,
        out_shape=(jax.ShapeDtypeStruct((K, N), f32), jax.ShapeDtypeStruct((1, N), f32)),
        in_specs=[pl.BlockSpec((T, K), lambda j, i: (i, 0)), pl.BlockSpec((T, tn), lambda j, i: (i, j))],
        out_specs=(pl.BlockSpec((K, tn), lambda j, i: (0, j)), pl.BlockSpec((1, tn), lambda j, i: (0, j))),
        compiler_params=_cp(("parallel", "arbitrary"), 48),
    )(a, g)


def _outproj_dgrad(dzb, w, attn_out, lse):
    S = dzb.shape[0]
    T = PERM_TILE
    nsl = ATTN_CH // LANES
    n_p = len(DILATIONS)

    def body(g_ref, w_ref, ao_ref, lse_ref, dco_ref, dgo_ref, *rest):
        do_refs = rest[:n_p]
        st_refs = rest[n_p:2 * n_p]
        slabs = rest[2 * n_p:]
        dcat = lax.dot_general(g_ref[...], w_ref[...], NT_DIMS, preferred_element_type=f32)
        dco_ref[...] = dcat[:, :CONV_CH]
        dgo_ref[...] = dcat[:, CONV_CH + ATTN_CH:]
        lane = lax.broadcasted_iota(jnp.int32, (T, LANES), 1)
        st = lse_ref[...]
        for j in range(nsl):
            dO = dcat[:, CONV_CH + LANES * j:CONV_CH + LANES * (j + 1)]
            prod = dO * ao_ref[:, LANES * j:LANES * (j + 1)].astype(f32)
            for a in range(2):
                in_head = (lane < HEAD_DIM) if a == 0 else (lane >= HEAD_DIM)
                delta = jnp.sum(jnp.where(in_head, prod, 0.0), axis=1, keepdims=True)
                st = jnp.where((lane // 16 == 2 * j + a) & (lane % 16 >= 8), delta, st)
            slabs[j][...] = dO
        slabs[nsl][...] = st
        for d, do_ref, st_ref in zip(DILATIONS, do_refs, st_refs):
            for r in range(d):
                for j in range(nsl):
                    do_ref[r, :, LANES * j:LANES * (j + 1)] = _rows_of(slabs[j], r, T // d, d).astype(bf16)
                st_ref[r] = _rows_of(slabs[nsl], r, T // d, d)

    row = lambda c: pl.BlockSpec((T, c), lambda i: (i, 0))
    return pl.pallas_call(
        body, grid=(S // T,), name="outproj_dgrad",
        out_shape=(jax.ShapeDtypeStruct((S, CONV_CH), f32), jax.ShapeDtypeStruct((S, GMLP_CH), f32))
        + tuple(_perm_shape(S, d, ATTN_CH, bf16) for d in DILATIONS)
        + tuple(_perm_shape(S, d, ATTN_BLOCK, f32) for d in DILATIONS),
        in_specs=[row(D_MODEL), _resident((D_MODEL, D_MODEL)), row(ATTN_CH), row(ATTN_BLOCK)],
        out_specs=(row(CONV_CH), row(GMLP_CH)) + tuple(_perm_spec(d, ATTN_CH) for d in DILATIONS)
        + tuple(_perm_spec(d, ATTN_BLOCK) for d in DILATIONS),
        scratch_shapes=_slabs(nsl + 1, T),
        compiler_params=_cp(("parallel",), 40),
    )(dzb, w, attn_out, lse)


def _gmlp_bwd(c_in, dgm, ln_g, ln_b, w_s, b_s_t):
    S = c_in.shape[0]
    T = 512
    nsteps = S // T

    def body(c_ref, dg_ref, g_ref, be_ref, w_ref, bs_ref, dc_ref, dlg_ref, dlb_ref, dw_ref, dbs_ref,
             du_buf, dv_buf, dm_acc):
        i = pl.program_id(0)

        @pl.when(i == 0)
        def _():
            dlg_ref[...] = jnp.zeros_like(dlg_ref)
            dlb_ref[...] = jnp.zeros_like(dlb_ref)
            dw_ref[...] = jnp.zeros_like(dw_ref)
            dm_acc[...] = jnp.zeros_like(dm_acc)
        c = c_ref[...]
        u = c[:, :GMLP_CH]
        xhat, rstd = _ln_stats(c[:, GMLP_CH:])
        vb = (xhat * g_ref[...] + be_ref[...]).astype(bf16)
        dgm_t = dg_ref[...]
        dm_all = dgm_t * u
        for g in range(GMLP_GROUPS):
            wt = _tril_bf16(w_ref[g])
            cs = slice(GMLP_GROUP_DIM * g, GMLP_GROUP_DIM * (g + 1))
            dw_g = jnp.zeros((CHUNK, CHUNK), f32)
            for ci in range(T // CHUNK):
                rs = slice(CHUNK * ci, CHUNK * (ci + 1))
                v_c = vb[rs, cs]
                mixed = jnp.dot(wt, v_c, preferred_element_type=f32) + bs_ref[:, g:g + 1]
                dm = dm_all[rs, cs]
                dmb = dm.astype(bf16)
                du_buf[rs, cs] = dgm_t[rs, cs] * mixed
                dv_buf[rs, cs] = lax.dot_general(wt, dmb, TN_DIMS, preferred_element_type=f32)
                dw_g = dw_g + lax.dot_general(dmb, v_c, NT_DIMS, preferred_element_type=f32)
                dm_acc[:, cs] += dm
            dw_ref[g] += dw_g
        dv = dv_buf[...]
        dvr = _ln_bwd(dv, xhat, rstd, g_ref[...])
        dlg_ref[...] += _colsum(dv * xhat)
        dlb_ref[...] += _colsum(dv)
        dc_ref[:, :GMLP_CH] = du_buf[...].astype(bf16)
        dc_ref[:, GMLP_CH:] = dvr.astype(bf16)

        @pl.when(i == nsteps - 1)
        def _():
            row = lax.broadcasted_iota(jnp.int32, (CHUNK, CHUNK), 0)
            col = lax.broadcasted_iota(jnp.int32, (CHUNK, CHUNK), 1)
            tile = jnp.zeros((CHUNK, CHUNK), f32)
            for g in range(GMLP_GROUPS):
                dw_ref[g] = jnp.where(col <= row, dw_ref[g], 0.0)
                gsum = jnp.sum(dm_acc[:, GMLP_GROUP_DIM * g:GMLP_GROUP_DIM * (g + 1)], axis=1, keepdims=True)
                tile = jnp.where(col == g, gsum, tile)
            dbs_ref[...] = tile

    vec = jax.ShapeDtypeStruct((1, GMLP_CH), f32)
    return pl.pallas_call(
        body, grid=(nsteps,), name="gmlp_bwd",
        out_shape=(jax.ShapeDtypeStruct((S, 2 * GMLP_CH), bf16), vec, vec,
                   jax.ShapeDtypeStruct((GMLP_GROUPS, CHUNK, CHUNK), f32), jax.ShapeDtypeStruct((CHUNK, CHUNK), f32)),
        in_specs=[pl.BlockSpec((T, 2 * GMLP_CH), lambda i: (i, 0)), pl.BlockSpec((T, GMLP_CH), lambda i: (i, 0)),
                  _acc((1, GMLP_CH)), _acc((1, GMLP_CH)), _acc((GMLP_GROUPS, CHUNK, CHUNK)), _acc((CHUNK, GMLP_GROUPS))],
        out_specs=(pl.BlockSpec((T, 2 * GMLP_CH), lambda i: (i, 0)), _acc((1, GMLP_CH)), _acc((1, GMLP_CH)),
                   _acc((GMLP_GROUPS, CHUNK, CHUNK)), _acc((CHUNK, CHUNK))),
        scratch_shapes=[pltpu.VMEM((T, GMLP_CH), f32), pltpu.VMEM((T, GMLP_CH), f32), pltpu.VMEM((CHUNK, GMLP_CH), f32)],
        compiler_params=_cp(("arbitrary",), 32),
    )(c_in, dgm, ln_g, ln_b, w_s, b_s_t)


def _attn_bwd_pattern(qkv, d_out, stats, bias, d):
    _, L, _ = qkv.shape
    B = ATTN_BLOCK
    QB = _attn_tile(L)
    nsb = QB // B
    nt = L // QB
    U = nsb * UNITS_PER_BLOCK
    KV = 2 * ATTN_CH

    def body(cur_ref, hk_ref, hv_ref, do_ref, st_ref, b_ref, dqkv_ref, dbias_ref, lg, dp, pb, dsb, dkv, carry):
        r = pl.program_id(0)
        i = pl.program_id(1)
        n = nt - 1 - i

        @pl.when((r == 0) & (i == 0))
        def _():
            dbias_ref[...] = jnp.zeros_like(dbias_ref)

        @pl.when(i == 0)
        def _():
            carry[...] = jnp.zeros_like(carry)

        def operands(b, j, a):
            rows = slice(B * b, B * (b + 1))
            q2 = cur_ref[rows, LANES * j:LANES * (j + 1)]
            do2 = do_ref[rows, LANES * j:LANES * (j + 1)]
            keep = _head_lanes(a)
            return jnp.where(keep, q2, jnp.zeros_like(q2)), jnp.where(keep, do2, jnp.zeros_like(do2))

        for b in range(nsb):
            valid = _band_mask(b == 0, n)
            for j in range(HEAD_PAIRS):
                k2 = _pair_keys(cur_ref, hk_ref, 1, b, j)
                v2 = _pair_keys(cur_ref, hv_ref, 2, b, j)
                for a in range(2):
                    u = (b * HEAD_PAIRS + j) * 2 + a
                    qm, dom = operands(b, j, a)
                    logits = lax.dot_general(qm, k2, NT_DIMS, preferred_element_type=f32) + b_ref[2 * j + a]
                    lg[B * u:B * (u + 1), :] = jnp.where(valid, logits, NEG)
                    dp[B * u:B * (u + 1), :] = lax.dot_general(dom, v2, NT_DIMS, preferred_element_type=f32)
        for b in range(nsb):
            for j in range(HEAD_PAIRS):
                for a in range(2):
                    u = (b * HEAD_PAIRS + j) * 2 + a
                    rows = slice(B * u, B * (u + 1))
                    lane0 = 32 * j + 16 * a
                    lse = st_ref[B * b:B * (b + 1), lane0:lane0 + 1]
                    delta = st_ref[B * b:B * (b + 1), lane0 + 8:lane0 + 9]
                    p = jnp.exp(lg[rows, :] - lse)
                    ds = p * (dp[rows, :] - delta)
                    pb[rows, :] = p.astype(bf16)
                    dsb[rows, :] = ds.astype(bf16)
                    dbias_ref[2 * j + a] += ds
        dkv[...] = jnp.zeros_like(dkv)
        for b in range(nsb):
            for j in range(HEAD_PAIRS):
                k2 = _pair_keys(cur_ref, hk_ref, 1, b, j)
                dq, dk2, dv2 = [], None, None
                for a in range(2):
                    u = (b * HEAD_PAIRS + j) * 2 + a
                    rows = slice(B * u, B * (u + 1))
                    qm, dom = operands(b, j, a)
                    ds_u = dsb[rows, :]
                    dq.append(jnp.dot(ds_u, k2, preferred_element_type=f32))
                    dk_u = lax.dot_general(ds_u, qm, TN_DIMS, preferred_element_type=f32)
                    dv_u = lax.dot_general(pb[rows, :], dom, TN_DIMS, preferred_element_type=f32)
                    dk2 = dk_u if dk2 is None else dk2 + dk_u
                    dv2 = dv_u if dv2 is None else dv2 + dv_u
                dq2 = jnp.where(_head_lanes(0), dq[0], dq[1]) * (HEAD_DIM ** -0.5)
                dqkv_ref[B * b:B * (b + 1), LANES * j:LANES * (j + 1)] = dq2.astype(bf16)
                dkv[B * b:B * (b + 2), LANES * j:LANES * (j + 1)] += dk2
                dkv[B * b:B * (b + 2), ATTN_CH + LANES * j:ATTN_CH + LANES * (j + 1)] += dv2
        dkv[QB:, :] += carry[...]
        dqkv_ref[:, ATTN_CH:] = dkv[B:, :].astype(bf16)
        carry[...] = dkv[0:B, :]

    halo = lambda part: pl.BlockSpec((None, B, ATTN_CH),
                                     lambda r, i: (r, jnp.maximum((nt - 1 - i) * nsb - 1, 0), part))
    tile_spec = lambda c: pl.BlockSpec((None, QB, c), lambda r, i: (r, nt - 1 - i, 0))
    return pl.pallas_call(
        body, grid=(d, nt), name=f"attn_bwd_d{d}",
        out_shape=(jax.ShapeDtypeStruct((d, L, QKV_CH), bf16), jax.ShapeDtypeStruct((ATTN_HEADS, B, 2 * B), f32)),
        in_specs=[tile_spec(QKV_CH), halo(1), halo(2), tile_spec(ATTN_CH), tile_spec(B),
                  _resident((ATTN_HEADS, B, 2 * B))],
        out_specs=(tile_spec(QKV_CH), _acc((ATTN_HEADS, B, 2 * B))),
        scratch_shapes=[pltpu.VMEM((U * B, 2 * B), f32), pltpu.VMEM((U * B, 2 * B), f32),
                        pltpu.VMEM((U * B, 2 * B), bf16), pltpu.VMEM((U * B, 2 * B), bf16),
                        pltpu.VMEM((B + QB, KV), f32), pltpu.VMEM((B, KV), f32)],
        compiler_params=_cp(("arbitrary", "arbitrary"), 48),
    )(qkv, qkv, qkv, d_out, stats, bias)


def _attn_bwd_merge(d_a, dqkvs, d_c):
    S = d_a.shape[0]
    T = PERM_TILE
    nsl = QKV_CH // LANES
    n_p = len(DILATIONS)

    def body(da_ref, *rest):
        g_refs = rest[:n_p]
        dc_ref, dh_ref = rest[n_p:n_p + 2]
        slabs = rest[n_p + 2:]
        q0 = 2 * CONV_CH
        dh_ref[:, :q0] = da_ref[...]
        dh_ref[:, q0 + QKV_CH:] = dc_ref[...]
        for p, (d, g_ref) in enumerate(zip(DILATIONS, g_refs)):
            for r in range(d):
                for j in range(nsl):
                    _set_rows_of(slabs[p * nsl + j], r, T // d, d, g_ref[r, :, LANES * j:LANES * (j + 1)].astype(f32))
        for j in range(nsl):
            acc = slabs[j][...]
            for p in range(1, n_p):
                acc = acc + slabs[p * nsl + j][...]
            dh_ref[:, q0 + LANES * j:q0 + LANES * (j + 1)] = acc.astype(bf16)

    row = lambda c: pl.BlockSpec((T, c), lambda i: (i, 0))
    return pl.pallas_call(
        body, grid=(S // T,), name="attn_bwd_merge", out_shape=jax.ShapeDtypeStruct((S, IN_CH), bf16),
        in_specs=[row(2 * CONV_CH)] + [_perm_spec(d, QKV_CH) for d in DILATIONS] + [row(2 * GMLP_CH)],
        out_specs=row(IN_CH), scratch_shapes=_slabs(n_p * nsl, T),
        compiler_params=_cp(("parallel",), 48),
    )(d_a, *dqkvs, d_c)


def _bias_table_grad(dbias, buckets):
    n = dbias.shape[0]

    def body(db_ref, bk_ref, o_ref):
        p = pl.program_id(0)
        h = pl.program_id(1)

        @pl.when((p == 0) & (h == 0))
        def _():
            o_ref[...] = jnp.zeros_like(o_ref)
        ids = bk_ref[0]
        db = db_ref[0, 0]
        row = lax.broadcasted_iota(jnp.int32, (N_BUCKETS, 128), 0)
        lane = lax.broadcasted_iota(jnp.int32, (N_BUCKETS, 128), 1)
        upd = jnp.zeros((N_BUCKETS, 128), f32)
        for b in range(N_BUCKETS):
            s = jnp.sum(jnp.sum(jnp.where(ids == b, db, 0.0), axis=1, keepdims=True), axis=0, keepdims=True)
            upd = jnp.where((row == b) & (lane == h), s, upd)
        o_ref[...] += upd

    return pl.pallas_call(
        body, grid=(n, ATTN_HEADS), name="bias_table_grad",
        out_shape=jax.ShapeDtypeStruct((N_BUCKETS, 128), f32),
        in_specs=[pl.BlockSpec((1, 1, ATTN_BLOCK, 2 * ATTN_BLOCK), lambda p, h: (p, h, 0, 0)),
                  pl.BlockSpec((1, ATTN_BLOCK, 2 * ATTN_BLOCK), lambda p, h: (p, 0, 0))],
        out_specs=_acc((N_BUCKETS, 128)),
        compiler_params=_cp(("arbitrary", "arbitrary"), 16),
    )(dbias, buckets)


def _conv_bwd(a_in, hc, dco, dw_w, ln_g, ln_b):
    S = a_in.shape[0]
    T = 512
    hb = T // CONV_HALO
    nsteps = S // T
    R = T + CONV_HALO
    K = CONV_WIDTH

    def body(a_ref, hc_ref, hcn_ref, d_ref, dn_ref, w_ref, g_ref, be_ref,
             da_ref, dw_ref, dcb_ref, dlg_ref, dlb_ref, ext, dbuf, wacc):
        i = pl.program_id(0)

        @pl.when(i == 0)
        def _():
            wacc[...] = jnp.zeros_like(wacc)
            dcb_ref[...] = jnp.zeros_like(dcb_ref)
            dlg_ref[...] = jnp.zeros_like(dlg_ref)
            dlb_ref[...] = jnp.zeros_like(dlb_ref)
        ext[0:T, :] = hc_ref[...]
        ext[T:, :] = hcn_ref[...]
        xhat, rstd = _ln_stats(ext[...])
        hl = xhat * g_ref[...] + be_ref[...]
        ext[0:T, :] = d_ref[...]
        ext[T:, :] = dn_ref[...]
        sl_ = _sig(hl)
        dhl = ext[...] * (sl_ * (1.0 + hl * (1.0 - sl_)))
        dhc = _ln_bwd(dhl, xhat, rstd, g_ref[...])
        rowi = lax.broadcasted_iota(jnp.int32, (R, CONV_CH), 0)
        dbuf[...] = jnp.where((rowi < T) | (i < nsteps - 1), dhc, 0.0)
        dlg_ref[...] += _colsum(dhl[:T] * xhat[:T])
        dlb_ref[...] += _colsum(dhl[:T])
        dcb_ref[...] += _colsum(dbuf[pl.ds(0, T), :])
        starts = _window_rolls(range(K))
        slabs = [slice(LANES * j, LANES * (j + 1)) for j in range(CONV_CH // LANES)]

        def step(g, _):
            r0 = pl.multiple_of(g * CONV_GROUP, CONV_GROUP)
            rows = pl.ds(r0, CONV_GROUP)
            for j, cs in enumerate(slabs):
                gate_cs = slice(CONV_CH + LANES * j, CONV_CH + LANES * (j + 1))
                win = dbuf[pl.ds(r0, CONV_GROUP + CONV_HALO), cs]
                a = a_ref[rows, cs]
                sg = _sig(a_ref[rows, gate_cs])
                hg = a * sg
                dhg = jnp.zeros((CONV_GROUP, LANES), f32)
                for b, ss in starts.items():
                    rolled = win if b == 0 else pltpu.roll(win, b, 0)
                    for s in ss:
                        k = K - 1 - s
                        dk = rolled[s + b:s + b + CONV_GROUP]
                        dhg = dhg + w_ref[k:k + 1, cs] * dk
                        prod = dk * hg
                        fold = prod[0:SUBLANES]
                        for r in range(SUBLANES, CONV_GROUP, SUBLANES):
                            fold = fold + prod[r:r + SUBLANES]
                        wacc[SUBLANES * k:SUBLANES * (k + 1), cs] += fold
                da_ref[rows, cs] = (dhg * sg).astype(bf16)
                da_ref[rows, gate_cs] = (dhg * hg * (1.0 - sg)).astype(bf16)
            return 0

        lax.fori_loop(0, T // CONV_GROUP, step, 0)

        @pl.when(i == nsteps - 1)
        def _():
            for k in range(K):
                dw_ref[k:k + 1, :] = _colsum(wacc[SUBLANES * k:SUBLANES * (k + 1), :])
            dw_ref[K:, :] = jnp.zeros((32 - K, CONV_CH), f32)

    vec = jax.ShapeDtypeStruct((1, CONV_CH), f32)
    nxt = lambda i: (jnp.minimum((i + 1) * hb, nsteps * hb - 1), 0)
    return pl.pallas_call(
        body, grid=(nsteps,), name="conv_bwd",
        out_shape=(jax.ShapeDtypeStruct((S, 2 * CONV_CH), bf16), jax.ShapeDtypeStruct((32, CONV_CH), f32), vec, vec, vec),
        in_specs=[pl.BlockSpec((T, 2 * CONV_CH), lambda i: (i, 0)),
                  pl.BlockSpec((T, CONV_CH), lambda i: (i, 0)), pl.BlockSpec((CONV_HALO, CONV_CH), nxt),
                  pl.BlockSpec((T, CONV_CH), lambda i: (i, 0)), pl.BlockSpec((CONV_HALO, CONV_CH), nxt),
                  _acc((32, CONV_CH)), _acc((1, CONV_CH)), _acc((1, CONV_CH))],
        out_specs=(pl.BlockSpec((T, 2 * CONV_CH), lambda i: (i, 0)), _acc((32, CONV_CH)), _acc((1, CONV_CH)),
                   _acc((1, CONV_CH)), _acc((1, CONV_CH))),
        scratch_shapes=[pltpu.VMEM((R, CONV_CH), f32), pltpu.VMEM((R, CONV_CH), f32),
                        pltpu.VMEM((SUBLANES * 32, CONV_CH), f32)],
        compiler_params=_cp(("arbitrary",), 32),
    )(a_in, hc, hc, dco, dco, dw_w, ln_g, ln_b)


def _adamw(g, w, m, v, name):
    R, C = g.shape
    T = R
    for cand in (512, 256, 128, 64, 32, 16, 8):
        if R % cand == 0 and cand * C * 4 <= MIB:
            T = cand
            break
    c1 = 1.0 / (1.0 - ADAM_B1 ** ADAM_STEP)
    c2 = 1.0 / (1.0 - ADAM_B2 ** ADAM_STEP)

    def body(g_ref, w_ref, m_ref, v_ref, d_ref, nm_ref, nv_ref):
        gg = g_ref[...]
        nm = ADAM_B1 * m_ref[...] + (1.0 - ADAM_B1) * gg
        nv = ADAM_B2 * v_ref[...] + (1.0 - ADAM_B2) * (gg * gg)
        nm_ref[...] = nm
        nv_ref[...] = nv
        d_ref[...] = -ADAM_LR * ((nm * c1) / (jnp.sqrt(nv * c2) + ADAM_EPS) + ADAM_WD * w_ref[...])

    blk = pl.BlockSpec((T, C), lambda i: (i, 0))
    sd = jax.ShapeDtypeStruct((R, C), f32)
    return pl.pallas_call(
        body, grid=(R // T,), name=name, out_shape=(sd, sd, sd), in_specs=[blk] * 4, out_specs=(blk, blk, blk),
        compiler_params=_cp(("parallel",), 48),
    )(g, w, m, v)


def _pad_rows(a, rows):
    return jnp.pad(a, ((0, rows - a.shape[0]), (0, 0)))


def _local_step(x, target, wb, sp):
    buckets = jnp.asarray(_bucket_ids())
    bias = _bias_build(sp["rel_bias_table"], buckets)
    saved = []
    xl = x
    for l in range(DEPTH):
        vec = lambda name: sp[name][l][None, :]
        a_in, c_in, *qkv = _inproj_fwd(xl, wb["w_in"][l], vec("b_in"))
        conv_w = _pad_rows(sp["conv_dw_w"][l], 32)
        conv_out, hc = _conv_fwd(a_in, conv_w, vec("conv_dw_b"), vec("conv_ln_g"), vec("conv_ln_b"))
        attn_out, lse = _attn_fwd(qkv, bias)
        bs_t = sp["gmlp_b_s"][l].T
        gm_out = _gmlp_fwd(c_in, vec("gmlp_ln_g"), vec("gmlp_ln_b"), sp["gmlp_w_s"][l], bs_t)
        cat, z1, x1, x1b = _outproj_ln_fwd(conv_out, attn_out, gm_out, wb["w_out"][l], vec("b_out"), xl,
                                           vec("ln1_g"), vec("ln1_b"))
        fconv_w = _pad_rows(sp["ffn_conv_w"][l], 8)
        hfb, fhc, act = _ffn_up_gate_fwd(x1b, wb["ffn_w_up"][l], vec("ffn_b_up"), fconv_w, vec("ffn_conv_b"))
        z2, x2 = _ffn_down_ln_fwd(act, wb["ffn_w_down"][l], vec("ffn_b_down"), x1, vec("ln2_g"), vec("ln2_b"))
        saved.append(dict(x=xl, a_in=a_in, qkv=qkv, c_in=c_in, hc=hc, attn_out=attn_out, lse=lse, cat=cat, z1=z1,
                          x1b=x1b, hfb=hfb, fhc=fhc, act=act, z2=z2, conv_w=conv_w, fconv_w=fconv_w, bs_t=bs_t))
        xl = x2

    grads = {}
    per_layer = {k: [None] * DEPTH for k in (
        "w_in", "b_in", "conv_dw_w", "conv_dw_b", "conv_ln_g", "conv_ln_b", "gmlp_ln_g", "gmlp_ln_b", "gmlp_w_s",
        "gmlp_b_s", "w_out", "b_out", "ln1_g", "ln1_b", "ffn_w_up", "ffn_b_up", "ffn_conv_w", "ffn_conv_b",
        "ffn_w_down", "ffn_b_down", "ln2_g", "ln2_b")}
    dbias_all = []
    l = DEPTH - 1
    vec = lambda name: sp[name][l][None, :]
    dz2, dz2b, loss_part, dg2, db2 = _loss_ln_bwd(xl, target, saved[l]["z2"], vec("ln2_g"))
    loss = jnp.sum(loss_part)
    grad_x = None
    for l in reversed(range(DEPTH)):
        sv = saved[l]
        vec = lambda name: sp[name][l][None, :]
        per_layer["ln2_g"][l] = dg2[0]
        per_layer["ln2_b"][l] = db2[0]
        dw_down, db_down = _wgrad(sv["act"], dz2b, 512, "ffn_down_wgrad")
        per_layer["ffn_w_down"][l] = dw_down
        per_layer["ffn_b_down"][l] = db_down[0]
        dhf, dfcw, dfcb = _ffn_down_gate_bwd(dz2b, wb["ffn_w_down"][l], sv["hfb"], sv["fhc"], sv["fconv_w"])
        per_layer["ffn_conv_w"][l] = dfcw[:FFN_CONV_WIDTH]
        per_layer["ffn_conv_b"][l] = dfcb[0]
        dw_up, db_up = _wgrad(sv["x1b"], dhf, 1408, "ffn_up_wgrad")
        per_layer["ffn_w_up"][l] = dw_up
        per_layer["ffn_b_up"][l] = db_up[0]
        dz1, dz1b, dg1, db1 = _dgrad_ln_bwd(dhf, wb["ffn_w_up"][l], dz2, sv["z1"], vec("ln1_g"), "ffn_up_dgrad_ln")
        per_layer["ln1_g"][l] = dg1[0]
        per_layer["ln1_b"][l] = db1[0]
        dw_out, db_out = _wgrad(sv["cat"], dz1b, 512, "outproj_wgrad")
        per_layer["w_out"][l] = dw_out
        per_layer["b_out"][l] = db_out[0]
        dco, dgo, *perm = _outproj_dgrad(dz1b, wb["w_out"][l], sv["attn_out"], sv["lse"])
        d_outs, stats = perm[:len(DILATIONS)], perm[len(DILATIONS):]
        d_c, dglg, dglb, dws, dbs = _gmlp_bwd(sv["c_in"], dgo, vec("gmlp_ln_g"), vec("gmlp_ln_b"), sp["gmlp_w_s"][l],
                                              sv["bs_t"])
        per_layer["gmlp_ln_g"][l] = dglg[0]
        per_layer["gmlp_ln_b"][l] = dglb[0]
        per_layer["gmlp_w_s"][l] = dws
        per_layer["gmlp_b_s"][l] = dbs[:, :GMLP_GROUPS].T
        dqkvs = []
        for p, d in enumerate(DILATIONS):
            dqkv, dbias = _attn_bwd_pattern(sv["qkv"][p], d_outs[p], stats[p], bias[p], d)
            dqkvs.append(dqkv)
            dbias_all.append(dbias)
        d_a, dcw, dcb, dclg, dclb = _conv_bwd(sv["a_in"], sv["hc"], dco, sv["conv_w"], vec("conv_ln_g"),
                                              vec("conv_ln_b"))
        per_layer["conv_dw_w"][l] = dcw[:CONV_WIDTH]
        per_layer["conv_dw_b"][l] = dcb[0]
        per_layer["conv_ln_g"][l] = dclg[0]
        per_layer["conv_ln_b"][l] = dclb[0]
        dh = _attn_bwd_merge(d_a, dqkvs, d_c)
        dw_in, db_in = _wgrad(sv["x"], dh, 640, "inproj_wgrad")
        per_layer["w_in"][l] = dw_in
        per_layer["b_in"][l] = db_in[0]
        if l > 0:
            pv = saved[l - 1]
            dz2, dz2b, dg2, db2 = _dgrad_ln_bwd(dh, wb["w_in"][l], dz1, pv["z2"], sp["ln2_g"][l - 1][None, :],
                                                "inproj_dgrad_ln")
        else:
            grad_x = _dgrad_ln_bwd(dh, wb["w_in"][l], dz1, None, None, "inproj_dgrad")
    for k, v in per_layer.items():
        grads[k] = v if k in BIG else jnp.stack(v)
    dbias_cat = jnp.stack(dbias_all)
    bk_cat = jnp.concatenate([buckets] * DEPTH, axis=0)
    grads["rel_bias_table"] = _bias_table_grad(dbias_cat, bk_cat)[:, :ATTN_HEADS]
    return loss, grad_x, grads


N_CHIPS = 4
BIG = {"w_in": (D_MODEL, IN_CH, 1), "w_out": (D_MODEL, D_MODEL, 0),
       "ffn_w_up": (D_MODEL, 2 * D_FF, 1), "ffn_w_down": (D_FF, D_MODEL, 0)}
SMALL = ("b_in", "conv_dw_w", "conv_dw_b", "conv_ln_g", "conv_ln_b", "rel_bias_table", "gmlp_ln_g", "gmlp_ln_b",
         "gmlp_w_s", "gmlp_b_s", "b_out", "ln1_g", "ln1_b", "ffn_b_up", "ffn_conv_w", "ffn_conv_b", "ffn_b_down",
         "ln2_g", "ln2_b")
SMALL_SHARDED = ("conv_dw_w", "ffn_conv_w")
WEIGHTS = ("w_in", "b_in", "conv_dw_w", "conv_dw_b", "conv_ln_g", "conv_ln_b", "rel_bias_table", "gmlp_ln_g",
           "gmlp_ln_b", "gmlp_w_s", "gmlp_b_s", "w_out", "b_out", "ln1_g", "ln1_b", "ffn_w_up", "ffn_b_up",
           "ffn_conv_w", "ffn_conv_b", "ffn_w_down", "ffn_b_down", "ln2_g", "ln2_b")
ANY = pl.BlockSpec(memory_space=pl.ANY)


def _position():
    return lax.axis_index("x"), lax.axis_index("y"), lax.axis_index("c")


def _other_chips(x, y):
    return [(1 - x, y), (x, 1 - y), (1 - x, 1 - y)]


def _cast_bf16(a):
    R, C = a.shape
    T = 128

    def body(a_ref, o_ref):
        o_ref[...] = a_ref[...].astype(bf16)

    return pl.pallas_call(
        body, grid=(R // T,), name="cast_bf16", out_shape=jax.ShapeDtypeStruct((R, C), bf16),
        in_specs=[pl.BlockSpec((T, C), lambda i: (i, 0))], out_specs=pl.BlockSpec((T, C), lambda i: (i, 0)),
        compiler_params=_cp(("parallel",), 16),
    )(a)


def _chip_slot(ref, name, l, p):
    K, N, ax = BIG[name]
    if ax == 1:
        sz = N // N_CHIPS
        return ref.at[l, :, pl.ds(pl.multiple_of(p * sz, 128), sz)]
    sz = K // N_CHIPS
    return ref.at[l, pl.ds(pl.multiple_of(p * sz, 16), sz), :]


def _gather_weights(shards, conv_w, fconv_w):
    names = list(BIG)
    n_big = len(names)
    n_t = n_big + 2
    n_chip = 3 * n_t
    n_pass = 3 * n_big

    def body(*refs):
        ins = refs[:n_t]
        outs = refs[n_t:2 * n_t]
        send_sems, recv_sems, pass_send, pass_recv, local_sems = refs[2 * n_t:]
        x, y, c = _position()
        me = 2 * x + y
        chips = _other_chips(x, y)

        def src(t):
            return ins[t].at[c] if t < n_big else ins[t]

        def slot(t, l, p):
            return _chip_slot(outs[t], names[t], l, p) if t < n_big else outs[t].at[p]

        locs, cps = [], []
        for t in range(n_t):
            for l in (range(DEPTH) if t < n_big else (0,)):
                loc = pltpu.make_async_copy(ins[t].at[l] if t < n_big else ins[t], slot(t, l, me),
                                            local_sems.at[DEPTH * t + l])
                loc.start()
                locs.append(loc)
            for k, (px, py) in enumerate(chips):
                cp = pltpu.make_async_remote_copy(
                    src_ref=src(t), dst_ref=slot(t, c, me), send_sem=send_sems.at[3 * t + k],
                    recv_sem=recv_sems.at[3 * t + k], device_id=(px, py, c), device_id_type=MESH_ID)
                cp.start()
                cps.append(cp)
        for t in range(n_t):
            for k, (px, py) in enumerate(chips):
                landed = slot(t, c, 2 * px + py)
                pltpu.make_async_remote_copy(
                    src_ref=src(t), dst_ref=landed, send_sem=send_sems.at[3 * t + k],
                    recv_sem=recv_sems.at[3 * t + k], device_id=(px, py, c), device_id_type=MESH_ID).wait_recv()
                if t < n_big:
                    cp = pltpu.make_async_remote_copy(
                        src_ref=landed, dst_ref=landed, send_sem=pass_send.at[3 * t + k],
                        recv_sem=pass_recv.at[3 * t + k], device_id=(x, y, 1 - c), device_id_type=MESH_ID)
                    cp.start()
                    cps.append(cp)
        for t in range(n_big):
            for k, (px, py) in enumerate(chips):
                from_sibling = slot(t, 1 - c, 2 * px + py)
                pltpu.make_async_remote_copy(
                    src_ref=from_sibling, dst_ref=from_sibling, send_sem=pass_send.at[3 * t + k],
                    recv_sem=pass_recv.at[3 * t + k], device_id=(x, y, 1 - c), device_id_type=MESH_ID).wait_recv()
        for cp in cps:
            cp.wait_send()
        for loc in locs:
            loc.wait()

    ins = [shards[n] for n in names] + [conv_w, fconv_w]
    out_shape = [jax.ShapeDtypeStruct((DEPTH, BIG[n][0], BIG[n][1]), bf16) for n in names]
    out_shape += [jax.ShapeDtypeStruct((N_CHIPS,) + conv_w.shape, f32), jax.ShapeDtypeStruct((N_CHIPS,) + fconv_w.shape, f32)]
    outs = pl.pallas_call(
        body, name="gather_weights", out_shape=tuple(out_shape), in_specs=[ANY] * n_t, out_specs=tuple([ANY] * n_t),
        scratch_shapes=[pltpu.SemaphoreType.DMA((n_chip,)), pltpu.SemaphoreType.DMA((n_chip,)),
                        pltpu.SemaphoreType.DMA((n_pass,)), pltpu.SemaphoreType.DMA((n_pass,)),
                        pltpu.SemaphoreType.DMA((DEPTH * n_t,))],
    )(*ins)
    return dict(zip(names, outs[:n_big])), outs[-2], outs[-1]


def _half(ref, name, c):
    K, N, ax = BIG[name]
    if ax == 1:
        return ref.at[pl.ds(pl.multiple_of(c * (K // 2), 8), K // 2), :]
    return ref.at[:, pl.ds(pl.multiple_of(c * (N // 2), 128), N // 2)]


def _half_shape(name):
    K, N, ax = BIG[name]
    return (K // 2, N) if ax == 1 else (K, N // 2)


def _shard_of_half(ref, name, q):
    K, N, ax = BIG[name]
    if ax == 1:
        sz = N // N_CHIPS
        return ref.at[:, pl.ds(pl.multiple_of(q * sz, 128), sz)]
    sz = K // N_CHIPS
    return ref.at[pl.ds(pl.multiple_of(q * sz, 16), sz), :]


def _shard_half_shape(name):
    K, N, ax = BIG[name]
    return (K // 2, N // N_CHIPS) if ax == 1 else (K // N_CHIPS, N // 2)


def _shard_shape(name):
    K, N, ax = BIG[name]
    return (K, N // N_CHIPS) if ax == 1 else (K // N_CHIPS, N)


def _pair_exchange(tensors):
    n_t = len(tensors)

    def body(*refs):
        ins = refs[:n_t]
        outs = refs[n_t:2 * n_t]
        send_sems, recv_sems = refs[2 * n_t:]
        x, y, c = _position()
        cps = []
        for t, (name, _) in enumerate(tensors):
            cp = pltpu.make_async_remote_copy(
                src_ref=_half(ins[t], name, 1 - c), dst_ref=outs[t], send_sem=send_sems.at[t],
                recv_sem=recv_sems.at[t], device_id=(x, y, 1 - c), device_id_type=MESH_ID)
            cp.start()
            cps.append(cp)
        for cp in cps:
            cp.wait()

    return pl.pallas_call(
        body, name="grad_pair_exchange",
        out_shape=tuple(jax.ShapeDtypeStruct(_half_shape(n), f32) for n, _ in tensors),
        in_specs=[ANY] * n_t, out_specs=tuple([ANY] * n_t),
        scratch_shapes=[pltpu.SemaphoreType.DMA((n_t,)), pltpu.SemaphoreType.DMA((n_t,))],
    )(*[g for _, g in tensors])


def _pair_add(g, rcv, name, c_arr):
    K, N, ax = BIG[name]
    hr, hc = _half_shape(name)
    T = 128
    nrt = hr // T

    def body(c_ref, g_ref, r_ref, o_ref):
        o_ref[...] = (g_ref[...] + r_ref[...]).astype(bf16)

    if ax == 1:
        g_spec = pl.BlockSpec((T, hc), lambda i, c: (c[0] * nrt + i, 0))
    else:
        g_spec = pl.BlockSpec((T, hc), lambda i, c: (i, c[0]))
    plain = pl.BlockSpec((T, hc), lambda i, c: (i, 0))
    return pl.pallas_call(
        body, name="grad_pair_add", out_shape=jax.ShapeDtypeStruct((hr, hc), bf16),
        grid_spec=pltpu.PrefetchScalarGridSpec(num_scalar_prefetch=1, grid=(nrt,), in_specs=[g_spec, plain],
                                               out_specs=plain),
        compiler_params=_cp(("parallel",), 32),
    )(c_arr, g, rcv)


def _chip_exchange(tensors):
    n_t = len(tensors)

    def body(*refs):
        ins = refs[:n_t]
        outs = refs[n_t:2 * n_t]
        send_sems, recv_sems, local_sems = refs[2 * n_t:]
        x, y, c = _position()
        me = 2 * x + y
        chips = _other_chips(x, y)
        locs, cps = [], []
        for t, (name, _) in enumerate(tensors):
            loc = pltpu.make_async_copy(_shard_of_half(ins[t], name, me), outs[t].at[me], local_sems.at[t])
            loc.start()
            locs.append(loc)
            for k, (px, py) in enumerate(chips):
                cp = pltpu.make_async_remote_copy(
                    src_ref=_shard_of_half(ins[t], name, 2 * px + py), dst_ref=outs[t].at[me],
                    send_sem=send_sems.at[3 * t + k], recv_sem=recv_sems.at[3 * t + k],
                    device_id=(px, py, c), device_id_type=MESH_ID)
                cp.start()
                cps.append(cp)
        for t, (name, _) in enumerate(tensors):
            for k, (px, py) in enumerate(chips):
                pltpu.make_async_remote_copy(
                    src_ref=_shard_of_half(ins[t], name, 2 * px + py), dst_ref=outs[t].at[2 * px + py],
                    send_sem=send_sems.at[3 * t + k], recv_sem=recv_sems.at[3 * t + k],
                    device_id=(px, py, c), device_id_type=MESH_ID).wait_recv()
        for cp in cps:
            cp.wait_send()
        for loc in locs:
            loc.wait()

    return pl.pallas_call(
        body, name="grad_chip_exchange",
        out_shape=tuple(jax.ShapeDtypeStruct((N_CHIPS,) + _shard_half_shape(n), g.dtype) for n, g in tensors),
        in_specs=[ANY] * n_t, out_specs=tuple([ANY] * n_t),
        scratch_shapes=[pltpu.SemaphoreType.DMA((3 * n_t,)), pltpu.SemaphoreType.DMA((3 * n_t,)),
                        pltpu.SemaphoreType.DMA((n_t,))],
    )(*[g for _, g in tensors])


def _sum_chips(parts):
    _, R, C = parts.shape
    T = 64

    def body(p_ref, o_ref):
        o_ref[...] = ((p_ref[0].astype(f32) + p_ref[1].astype(f32)) + p_ref[2].astype(f32)) + p_ref[3].astype(f32)

    return pl.pallas_call(
        body, grid=(R // T,), name="grad_sum_chips", out_shape=jax.ShapeDtypeStruct((R, C), f32),
        in_specs=[pl.BlockSpec((N_CHIPS, T, C), lambda i: (0, i, 0))], out_specs=pl.BlockSpec((T, C), lambda i: (i, 0)),
        compiler_params=_cp(("parallel",), 32),
    )(parts)


def _pair_swap(halves):
    n_t = len(halves)

    def body(*refs):
        ins = refs[:n_t]
        outs = refs[n_t:2 * n_t]
        send_sems, recv_sems = refs[2 * n_t:]
        x, y, c = _position()
        cps = []
        for t in range(n_t):
            cp = pltpu.make_async_remote_copy(
                src_ref=ins[t], dst_ref=outs[t], send_sem=send_sems.at[t], recv_sem=recv_sems.at[t],
                device_id=(x, y, 1 - c), device_id_type=MESH_ID)
            cp.start()
            cps.append(cp)
        for cp in cps:
            cp.wait()

    return pl.pallas_call(
        body, name="grad_pair_swap", out_shape=tuple(jax.ShapeDtypeStruct(h.shape, h.dtype) for h in halves),
        in_specs=[ANY] * n_t, out_specs=tuple([ANY] * n_t),
        scratch_shapes=[pltpu.SemaphoreType.DMA((n_t,)), pltpu.SemaphoreType.DMA((n_t,))],
    )(*halves)


def _adamw_halves(own, other, w, m, v, name, l, c_arr, prev):
    K, N, ax = BIG[name]
    R, C = _shard_shape(name)
    hr, hc = _shard_half_shape(name)
    T = 64
    nrt = hr // T
    c1 = 1.0 / (1.0 - ADAM_B1 ** ADAM_STEP)
    c2 = 1.0 / (1.0 - ADAM_B2 ** ADAM_STEP)

    def body(c_ref, own_ref, oth_ref, w_ref, m_ref, v_ref, *rest):
        g_ref, d_ref, nm_ref, nv_ref = rest[-4:]
        gg = jnp.where(pl.program_id(0) == c_ref[0], own_ref[...], oth_ref[...])
        nm = ADAM_B1 * m_ref[...] + (1.0 - ADAM_B1) * gg
        nv = ADAM_B2 * v_ref[...] + (1.0 - ADAM_B2) * (gg * gg)
        g_ref[...] = gg
        nm_ref[...] = nm
        nv_ref[...] = nv
        d_ref[...] = -ADAM_LR * ((nm * c1) / (jnp.sqrt(nv * c2) + ADAM_EPS) + ADAM_WD * w_ref[...])

    half = pl.BlockSpec((T, hc), lambda h, i, c: (i, 0))
    if ax == 1:
        full = pl.BlockSpec((None, T, hc), lambda h, i, c: (l, h * nrt + i, 0))
    else:
        full = pl.BlockSpec((None, T, hc), lambda h, i, c: (l, i, h))
    sd = jax.ShapeDtypeStruct((DEPTH, R, C), f32)
    args = [c_arr, own, other, w, m, v]
    in_specs = [half, half, full, full, full]
    aliases = {}
    if prev is not None:
        args += list(prev)
        in_specs += [ANY] * 4
        aliases = {6 + k: k for k in range(4)}
    return pl.pallas_call(
        body, name="adamw_" + name, out_shape=(sd, sd, sd, sd),
        grid_spec=pltpu.PrefetchScalarGridSpec(num_scalar_prefetch=1, grid=(2, nrt), in_specs=in_specs,
                                               out_specs=(full, full, full, full)),
        input_output_aliases=aliases,
        compiler_params=_cp(("arbitrary", "arbitrary"), 32),
    )(*args)


def _reduce_big_grads(grads):
    c_arr = jnp.reshape(lax.axis_index("c"), (1,)).astype(jnp.int32)
    tensors = [(n, grads[n][l]) for n in BIG for l in range(DEPTH)]
    received = _pair_exchange(tensors)
    pair = [(n, _pair_add(g, r, n, c_arr)) for (n, g), r in zip(tensors, received)]
    parts = _chip_exchange(pair)
    own = [_sum_chips(p) for p in parts]
    other = _pair_swap(own)
    keys = [(n, l) for n in BIG for l in range(DEPTH)]
    return dict(zip(keys, zip(own, other))), c_arr


def _small_allreduce(buf):
    R = buf.shape[0]
    n_dev = 8

    def body(in_ref, out_ref, slots, send_sems, recv_sems):
        x, y, c = _position()
        me = 4 * x + 2 * y + c
        slots[me] = in_ref[...]
        peers = []
        for k in range(1, n_dev):
            px = 1 - x if k & 4 else x
            py = 1 - y if k & 2 else y
            pc = 1 - c if k & 1 else c
            peers.append((px, py, pc))
        cps = []
        for k, peer in enumerate(peers):
            cp = pltpu.make_async_remote_copy(
                src_ref=in_ref, dst_ref=slots.at[me], send_sem=send_sems.at[k], recv_sem=recv_sems.at[k],
                device_id=peer, device_id_type=MESH_ID)
            cp.start()
            cps.append(cp)
        for k, (px, py, pc) in enumerate(peers):
            pltpu.make_async_remote_copy(
                src_ref=in_ref, dst_ref=slots.at[4 * px + 2 * py + pc], send_sem=send_sems.at[k],
                recv_sem=recv_sems.at[k], device_id=(px, py, pc), device_id_type=MESH_ID).wait_recv()
        for cp in cps:
            cp.wait_send()
        acc = slots[0]
        for dv in range(1, n_dev):
            acc = acc + slots[dv]
        out_ref[...] = acc

    vm = pl.BlockSpec(memory_space=pltpu.VMEM)
    return pl.pallas_call(
        body, name="small_allreduce", out_shape=jax.ShapeDtypeStruct((R, 128), f32), in_specs=[vm], out_specs=vm,
        scratch_shapes=[pltpu.VMEM((n_dev, R, 128), f32), pltpu.SemaphoreType.DMA((n_dev - 1,)),
                        pltpu.SemaphoreType.DMA((n_dev - 1,))],
        compiler_params=pltpu.CompilerParams(vmem_limit_bytes=40 * MIB),
    )(buf)


PACK_UNIT = 1024


def _pack(arrs):
    parts = []
    for a in arrs:
        flat = a.reshape(-1)
        n = -(-flat.shape[0] // PACK_UNIT) * PACK_UNIT
        parts.append(jnp.pad(flat, (0, n - flat.shape[0])))
    return jnp.concatenate(parts).reshape(-1, 128)


def _unpack(buf, shapes):
    flat = buf.reshape(-1)
    out, off = [], 0
    for shp in shapes:
        n = int(np.prod(shp))
        out.append(flat[off:off + n].reshape(shp))
        off += -(-n // PACK_UNIT) * PACK_UNIT
    return out


def kernel(x, w_in, b_in, conv_dw_w, conv_dw_b, conv_ln_g, conv_ln_b, rel_bias_table, gmlp_ln_g, gmlp_ln_b, gmlp_w_s, gmlp_b_s, w_out, b_out, ln1_g, ln1_b, ffn_w_up, ffn_b_up, ffn_conv_w, ffn_conv_b, ffn_w_down, ffn_b_down, ln2_g, ln2_b, loss_target, m_w_in, m_b_in, m_conv_dw_w, m_conv_dw_b, m_conv_ln_g, m_conv_ln_b, m_rel_bias_table, m_gmlp_ln_g, m_gmlp_ln_b, m_gmlp_w_s, m_gmlp_b_s, m_w_out, m_b_out, m_ln1_g, m_ln1_b, m_ffn_w_up, m_ffn_b_up, m_ffn_conv_w, m_ffn_conv_b, m_ffn_w_down, m_ffn_b_down, m_ln2_g, m_ln2_b, v_w_in, v_b_in, v_conv_dw_w, v_conv_dw_b, v_conv_ln_g, v_conv_ln_b, v_rel_bias_table, v_gmlp_ln_g, v_gmlp_ln_b, v_gmlp_w_s, v_gmlp_b_s, v_w_out, v_b_out, v_ln1_g, v_ln1_b, v_ffn_w_up, v_ffn_b_up, v_ffn_conv_w, v_ffn_conv_b, v_ffn_w_down, v_ffn_b_down, v_ln2_g, v_ln2_b):
    w = dict(w_in=w_in, b_in=b_in, conv_dw_w=conv_dw_w, conv_dw_b=conv_dw_b, conv_ln_g=conv_ln_g, conv_ln_b=conv_ln_b,
             rel_bias_table=rel_bias_table, gmlp_ln_g=gmlp_ln_g, gmlp_ln_b=gmlp_ln_b, gmlp_w_s=gmlp_w_s,
             gmlp_b_s=gmlp_b_s, w_out=w_out, b_out=b_out, ln1_g=ln1_g, ln1_b=ln1_b, ffn_w_up=ffn_w_up,
             ffn_b_up=ffn_b_up, ffn_conv_w=ffn_conv_w, ffn_conv_b=ffn_conv_b, ffn_w_down=ffn_w_down,
             ffn_b_down=ffn_b_down, ln2_g=ln2_g, ln2_b=ln2_b)
    m = dict(w_in=m_w_in, b_in=m_b_in, conv_dw_w=m_conv_dw_w, conv_dw_b=m_conv_dw_b, conv_ln_g=m_conv_ln_g,
             conv_ln_b=m_conv_ln_b, rel_bias_table=m_rel_bias_table, gmlp_ln_g=m_gmlp_ln_g, gmlp_ln_b=m_gmlp_ln_b,
             gmlp_w_s=m_gmlp_w_s, gmlp_b_s=m_gmlp_b_s, w_out=m_w_out, b_out=m_b_out, ln1_g=m_ln1_g, ln1_b=m_ln1_b,
             ffn_w_up=m_ffn_w_up, ffn_b_up=m_ffn_b_up, ffn_conv_w=m_ffn_conv_w, ffn_conv_b=m_ffn_conv_b,
             ffn_w_down=m_ffn_w_down, ffn_b_down=m_ffn_b_down, ln2_g=m_ln2_g, ln2_b=m_ln2_b)
    v = dict(w_in=v_w_in, b_in=v_b_in, conv_dw_w=v_conv_dw_w, conv_dw_b=v_conv_dw_b, conv_ln_g=v_conv_ln_g,
             conv_ln_b=v_conv_ln_b, rel_bias_table=v_rel_bias_table, gmlp_ln_g=v_gmlp_ln_g, gmlp_ln_b=v_gmlp_ln_b,
             gmlp_w_s=v_gmlp_w_s, gmlp_b_s=v_gmlp_b_s, w_out=v_w_out, b_out=v_b_out, ln1_g=v_ln1_g, ln1_b=v_ln1_b,
             ffn_w_up=v_ffn_w_up, ffn_b_up=v_ffn_b_up, ffn_conv_w=v_ffn_conv_w, ffn_conv_b=v_ffn_conv_b,
             ffn_w_down=v_ffn_w_down, ffn_b_down=v_ffn_b_down, ln2_g=v_ln2_g, ln2_b=v_ln2_b)

    shards = {n: _cast_bf16(w[n].reshape(-1, w[n].shape[-1])).reshape(w[n].shape) for n in BIG}
    wb, conv_stack, fconv_stack = _gather_weights(shards, conv_dw_w, ffn_conv_w)
    sp = {n: w[n] for n in SMALL}
    sp["conv_dw_w"] = jnp.moveaxis(conv_stack, 0, 2).reshape(DEPTH, CONV_WIDTH, CONV_CH)
    sp["ffn_conv_w"] = jnp.moveaxis(fconv_stack, 0, 2).reshape(DEPTH, FFN_CONV_WIDTH, 2 * D_FF)

    loss_local, grad_x, grads = _local_step(x[0], loss_target[0], wb, sp)
    loss = lax.psum(loss_local, ("x", "y", "c"))

    big, c_arr = _reduce_big_grads(grads)
    small_shapes = [grads[n].shape for n in SMALL]
    small = dict(zip(SMALL, _unpack(_small_allreduce(_pack([grads[n] for n in SMALL])), small_shapes)))
    chip = 2 * lax.axis_index("x") + lax.axis_index("y")
    for n in SMALL_SHARDED:
        width = w[n].shape[-1]
        small[n] = lax.dynamic_slice_in_dim(small[n], chip * width, width, axis=2)

    g_out, d_out, m_out, v_out = {}, {}, {}, {}
    for n in BIG:
        outs = None
        for l in range(DEPTH):
            own, other = big[(n, l)]
            outs = _adamw_halves(own, other, w[n], m[n], v[n], n, l, c_arr, outs)
        g_out[n], d_out[n], m_out[n], v_out[n] = outs
    shapes = [small[n].shape for n in SMALL]
    packed = [_pack([src[n] for n in SMALL]) for src in (small, w, m, v)]
    upd = _adamw(*packed, "adamw_small")
    for dst, buf in zip((d_out, m_out, v_out), upd):
        dst.update(zip(SMALL, _unpack(buf, shapes)))
    g_out.update(small)

    return (loss, grad_x[None], *[g_out[n] for n in WEIGHTS], *[d_out[n] for n in WEIGHTS],
            *[m_out[n] for n in WEIGHTS], *[v_out[n] for n in WEIGHTS])
```

```python
import functools
import math

import numpy as np
import jax
import jax.numpy as jnp
from jax import lax
from jax.experimental import pallas as pl
from jax.experimental.pallas import tpu as pltpu

f32 = jnp.float32
bf16 = jnp.bfloat16

D_MODEL = 1024
DEPTH = 2
HEAD_DIM = 64
CONV_CH = 256
CONV_WIDTH = 31
ATTN_HEADS = 8
ATTN_CH = ATTN_HEADS * HEAD_DIM
DILATIONS = (1, 4, 16)
ATTN_BLOCK = 128
N_BUCKETS = 32
MAX_DISTANCE = 2048
GMLP_CH = 256
GMLP_GROUPS = 4
GMLP_GROUP_DIM = GMLP_CH // GMLP_GROUPS
CHUNK = 128
IN_CH = 2 * CONV_CH + 3 * ATTN_CH + 2 * GMLP_CH
D_FF = 2816
FFN_CONV_WIDTH = 3
LN_EPS = 1e-5
ALPHA = (2.0 * DEPTH) ** 0.25
ADAM_LR = 0.001
ADAM_B1 = 0.9
ADAM_B2 = 0.999
ADAM_EPS = 1e-08
ADAM_WD = 0.01
ADAM_STEP = 10

CONV_HALO = 32
FFN_HALO = 8
NEG = -1e30
MIB = 2 ** 20
NT_DIMS = (((1,), (1,)), ((), ()))
TN_DIMS = (((0,), (0,)), ((), ()))
MESH_ID = pl.DeviceIdType.MESH


def _cp(sem, vmem_mib):
    return pltpu.CompilerParams(dimension_semantics=sem, vmem_limit_bytes=vmem_mib * MIB)


def _resident(shape):
    nd = len(shape)
    return pl.BlockSpec(shape, lambda *_: (0,) * nd, pipeline_mode=pl.Buffered(1))


def _acc(shape):
    nd = len(shape)
    return pl.BlockSpec(shape, lambda *_: (0,) * nd)


def _sig(x):
    return 1.0 / (1.0 + jnp.exp(-x))


def _ln_stats(z):
    mu = jnp.mean(z, axis=-1, keepdims=True)
    zc = z - mu
    var = jnp.mean(zc * zc, axis=-1, keepdims=True)
    rstd = lax.rsqrt(var + LN_EPS)
    return zc * rstd, rstd


def _ln_bwd(dy, xhat, rstd, g):
    dxh = dy * g
    m1 = jnp.mean(dxh, axis=-1, keepdims=True)
    m2 = jnp.mean(dxh * xhat, axis=-1, keepdims=True)
    return rstd * (dxh - m1 - xhat * m2)


def _colsum(x):
    return jnp.sum(x, axis=0, keepdims=True)


def _t5_bucket_np(dist):
    max_exact = N_BUCKETS // 2
    dd = np.maximum(dist, 1).astype(np.float64)
    large = max_exact + (np.log(dd / max_exact) / math.log(MAX_DISTANCE / max_exact)
                         * (N_BUCKETS - max_exact)).astype(np.int32)
    large = np.minimum(large, N_BUCKETS - 1)
    return np.where(dist < max_exact, dist, large).astype(np.int32)


def _bucket_ids():
    qi = np.arange(ATTN_BLOCK)[:, None]
    kj = np.arange(2 * ATTN_BLOCK)[None, :]
    dist = np.clip(qi + ATTN_BLOCK - kj, 0, None)
    return np.stack([_t5_bucket_np(dist * d) for d in DILATIONS]).astype(np.int32)


LANES = 128
QKV_CH = 3 * ATTN_CH
PERM_TILE = 512


def _slabs(n, rows):
    return [pltpu.VMEM((rows, LANES), f32)] * n


def _rows_of(slab, r, n, d):
    return slab[...] if d == 1 else slab[pl.ds(r, n, stride=d), :]


def _set_rows_of(slab, r, n, d, val):
    if d == 1:
        slab[...] = val
    else:
        slab[pl.ds(r, n, stride=d), :] = val


def _perm_spec(d, ch):
    return pl.BlockSpec((d, PERM_TILE // d, ch), lambda i: (0, i, 0))


def _perm_shape(S, d, ch, dtype):
    return jax.ShapeDtypeStruct((d, S // d, ch), dtype)


def _inproj_fwd(x, w, b):
    S = x.shape[0]
    T = PERM_TILE
    nsl = QKV_CH // LANES

    def body(x_ref, w_ref, b_ref, a_ref, c_ref, *rest):
        q_refs = rest[:len(DILATIONS)]
        slabs = rest[len(DILATIONS):]
        h = jnp.dot(x_ref[...].astype(bf16), w_ref[...], preferred_element_type=f32) + b_ref[...]
        a_ref[...] = h[:, :2 * CONV_CH]
        q0 = 2 * CONV_CH
        c_ref[...] = h[:, q0 + QKV_CH:]
        for j in range(nsl):
            piece = h[:, q0 + LANES * j:q0 + LANES * (j + 1)]
            if LANES * j < ATTN_CH:
                piece = piece * (HEAD_DIM ** -0.5)
            slabs[j][...] = piece
        for d, q_ref in zip(DILATIONS, q_refs):
            for r in range(d):
                for j in range(nsl):
                    q_ref[r, :, LANES * j:LANES * (j + 1)] = _rows_of(slabs[j], r, T // d, d).astype(bf16)

    row = lambda c: pl.BlockSpec((T, c), lambda i: (i, 0))
    return pl.pallas_call(
        body, grid=(S // T,), name="inproj_fwd",
        out_shape=(jax.ShapeDtypeStruct((S, 2 * CONV_CH), f32), jax.ShapeDtypeStruct((S, 2 * GMLP_CH), f32))
        + tuple(_perm_shape(S, d, QKV_CH, bf16) for d in DILATIONS),
        in_specs=[row(D_MODEL), _resident((D_MODEL, IN_CH)), _resident((1, IN_CH))],
        out_specs=(row(2 * CONV_CH), row(2 * GMLP_CH)) + tuple(_perm_spec(d, QKV_CH) for d in DILATIONS),
        scratch_shapes=_slabs(nsl, T),
        compiler_params=_cp(("parallel",), 48),
    )(x, w, b)


CONV_GROUP = 64


def _window_rolls(starts):
    groups = {}
    for s in starts:
        groups.setdefault((-s) % SUBLANES, []).append(s)
    return dict(sorted(groups.items()))


def _conv_fwd(a_in, dw_w, dw_b, ln_g, ln_b):
    S = a_in.shape[0]
    T = 512
    hb = T // CONV_HALO

    def body(a_ref, halo_ref, w_ref, b_ref, g_ref, be_ref, out_ref, hc_ref, buf):
        i = pl.program_id(0)
        am = a_ref[...]
        ah = halo_ref[...]
        hgh = ah[:, :CONV_CH] * _sig(ah[:, CONV_CH:])
        buf[0:CONV_HALO, :] = jnp.where(i > 0, hgh, 0.0)
        buf[CONV_HALO:, :] = am[:, :CONV_CH] * _sig(am[:, CONV_CH:])
        starts = _window_rolls(range(CONV_HALO - (CONV_WIDTH - 1), CONV_HALO + 1))
        slabs = [slice(LANES * j, LANES * (j + 1)) for j in range(CONV_CH // LANES)]

        def step(g, _):
            r0 = pl.multiple_of(g * CONV_GROUP, CONV_GROUP)
            rows = pl.ds(r0, CONV_GROUP)
            for cs in slabs:
                ext = buf[pl.ds(r0, CONV_GROUP + CONV_HALO), cs]
                acc = jnp.broadcast_to(b_ref[:, cs], (CONV_GROUP, LANES))
                for b, ss in starts.items():
                    rolled = ext if b == 0 else pltpu.roll(ext, b, 0)
                    for s in ss:
                        k = s - (CONV_HALO - (CONV_WIDTH - 1))
                        acc = acc + w_ref[k:k + 1, cs] * rolled[s + b:s + b + CONV_GROUP]
                hc_ref[rows, cs] = acc
            return 0

        lax.fori_loop(0, T // CONV_GROUP, step, 0)
        xhat, _ = _ln_stats(hc_ref[...])
        y = xhat * g_ref[...] + be_ref[...]
        out_ref[...] = (y * _sig(y)).astype(bf16)

    return pl.pallas_call(
        body, grid=(S // T,), name="conv_fwd",
        out_shape=(jax.ShapeDtypeStruct((S, CONV_CH), bf16), jax.ShapeDtypeStruct((S, CONV_CH), f32)),
        in_specs=[pl.BlockSpec((T, 2 * CONV_CH), lambda i: (i, 0)),
                  pl.BlockSpec((CONV_HALO, 2 * CONV_CH), lambda i: (jnp.maximum(i * hb - 1, 0), 0)),
                  _acc((32, CONV_CH)), _acc((1, CONV_CH)), _acc((1, CONV_CH)), _acc((1, CONV_CH))],
        out_specs=(pl.BlockSpec((T, CONV_CH), lambda i: (i, 0)), pl.BlockSpec((T, CONV_CH), lambda i: (i, 0))),
        scratch_shapes=[pltpu.VMEM((T + CONV_HALO, CONV_CH), f32)],
        compiler_params=_cp(("parallel",), 32),
    )(a_in, a_in, dw_w, dw_b, ln_g, ln_b)


def _bias_build(table, buckets):
    def body(t_ref, bk_ref, o_ref):
        h = pl.program_id(1)
        ids = bk_ref[0]
        acc = jnp.zeros((ATTN_BLOCK, 2 * ATTN_BLOCK), f32)
        for b in range(N_BUCKETS):
            acc = jnp.where(ids == b, t_ref[b, h], acc)
        o_ref[0, 0] = acc

    return pl.pallas_call(
        body, grid=(len(DILATIONS), ATTN_HEADS), name="bias_build",
        out_shape=jax.ShapeDtypeStruct((len(DILATIONS), ATTN_HEADS, ATTN_BLOCK, 2 * ATTN_BLOCK), f32),
        in_specs=[pl.BlockSpec(memory_space=pltpu.SMEM),
                  pl.BlockSpec((1, ATTN_BLOCK, 2 * ATTN_BLOCK), lambda p, h: (p, 0, 0))],
        out_specs=pl.BlockSpec((1, 1, ATTN_BLOCK, 2 * ATTN_BLOCK), lambda p, h: (p, h, 0, 0)),
        compiler_params=_cp(("arbitrary", "arbitrary"), 16),
    )(table, buckets)


def _head_tile(tile, h, col):
    lane_head = lax.broadcasted_iota(jnp.int32, tile.shape, 1) // 16
    return jnp.where(lane_head == h, col, tile)


HEAD_PAIRS = ATTN_HEADS // 2
UNITS_PER_BLOCK = ATTN_HEADS


def _attn_tile(L):
    return min(512, L)


def _band_mask(first_block, n):
    B = ATTN_BLOCK
    row = lax.broadcasted_iota(jnp.int32, (B, 2 * B), 0)
    col = lax.broadcasted_iota(jnp.int32, (B, 2 * B), 1)
    valid = (col >= row) & (col <= row + B)
    if first_block:
        valid = valid & ((col >= B) | (n > 0))
    return valid


def _head_lanes(a):
    lane = lax.broadcasted_iota(jnp.int32, (ATTN_BLOCK, LANES), 1)
    return (lane < HEAD_DIM) if a == 0 else (lane >= HEAD_DIM)


def _pair_keys(cur_ref, halo_ref, part, b, j):
    B = ATTN_BLOCK
    c0 = part * ATTN_CH + LANES * j
    own = cur_ref[B * b:B * (b + 1), c0:c0 + LANES]
    prev = halo_ref[:, LANES * j:LANES * (j + 1)] if b == 0 else cur_ref[B * (b - 1):B * b, c0:c0 + LANES]
    return jnp.concatenate([prev, own], axis=0)


def _attn_fwd_pattern(qkv, bias, d):
    _, L, _ = qkv.shape
    B = ATTN_BLOCK
    QB = _attn_tile(L)
    nsb = QB // B
    U = nsb * UNITS_PER_BLOCK

    def body(cur_ref, hk_ref, hv_ref, b_ref, o_ref, lse_ref, lg, pb):
        n = pl.program_id(1)
        for b in range(nsb):
            valid = _band_mask(b == 0, n)
            for j in range(HEAD_PAIRS):
                q2 = cur_ref[B * b:B * (b + 1), LANES * j:LANES * (j + 1)]
                k2 = _pair_keys(cur_ref, hk_ref, 1, b, j)
                for a in range(2):
                    u = (b * HEAD_PAIRS + j) * 2 + a
                    qm = jnp.where(_head_lanes(a), q2, jnp.zeros_like(q2))
                    logits = lax.dot_general(qm, k2, NT_DIMS, preferred_element_type=f32) + b_ref[2 * j + a]
                    lg[B * u:B * (u + 1), :] = jnp.where(valid, logits, NEG)
        m = jnp.max(lg[...], axis=1, keepdims=True)
        p = jnp.exp(lg[...] - m)
        s = jnp.sum(p, axis=1, keepdims=True)
        pb[...] = p.astype(bf16)
        lse = m + jnp.log(s)
        inv = 1.0 / s
        for b in range(nsb):
            tile = jnp.zeros((B, B), f32)
            for j in range(HEAD_PAIRS):
                v2 = _pair_keys(cur_ref, hv_ref, 2, b, j)
                outs = []
                for a in range(2):
                    u = (b * HEAD_PAIRS + j) * 2 + a
                    rows = slice(B * u, B * (u + 1))
                    outs.append(jnp.dot(pb[rows, :], v2, preferred_element_type=f32) * inv[rows])
                    tile = _head_tile(tile, 2 * j + a, lse[rows])
                o_ref[B * b:B * (b + 1), LANES * j:LANES * (j + 1)] = jnp.where(_head_lanes(0), outs[0], outs[1])
            lse_ref[B * b:B * (b + 1), :] = tile

    halo = lambda part: pl.BlockSpec((None, B, ATTN_CH), lambda r, n: (r, jnp.maximum(n * nsb - 1, 0), part))
    tile_spec = lambda c: pl.BlockSpec((None, QB, c), lambda r, n: (r, n, 0))
    return pl.pallas_call(
        body, grid=(d, L // QB), name=f"attn_fwd_d{d}",
        out_shape=(jax.ShapeDtypeStruct((d, L, ATTN_CH), f32), jax.ShapeDtypeStruct((d, L, B), f32)),
        in_specs=[tile_spec(QKV_CH), halo(1), halo(2), _resident((ATTN_HEADS, B, 2 * B))],
        out_specs=(tile_spec(ATTN_CH), tile_spec(B)),
        scratch_shapes=[pltpu.VMEM((U * B, 2 * B), f32), pltpu.VMEM((U * B, 2 * B), bf16)],
        compiler_params=_cp(("parallel", "parallel"), 40),
    )(qkv, qkv, qkv, bias)


def _attn_merge(parts):
    S = parts[0][0].shape[0] * parts[0][0].shape[1]
    T = PERM_TILE
    nsl = ATTN_CH // LANES
    n_p = len(DILATIONS)

    def body(*refs):
        ins = refs[:2 * n_p]
        out_ref, lse_ref = refs[2 * n_p:2 * n_p + 2]
        slabs = refs[2 * n_p + 2:]
        lses = []
        for p, d in enumerate(DILATIONS):
            o_ref, l_ref = ins[2 * p], ins[2 * p + 1]
            osl = slabs[p * (nsl + 1):p * (nsl + 1) + nsl]
            lsl = slabs[p * (nsl + 1) + nsl]
            for r in range(d):
                for j in range(nsl):
                    _set_rows_of(osl[j], r, T // d, d, o_ref[r, :, LANES * j:LANES * (j + 1)])
                _set_rows_of(lsl, r, T // d, d, l_ref[r])
            lses.append(lsl[...])
        big = functools.reduce(jnp.maximum, lses)
        ws = [jnp.exp(l - big) for l in lses]
        tot = functools.reduce(lambda a_, b_: a_ + b_, ws)
        lse_ref[...] = big + jnp.log(tot)
        ws = [w / tot for w in ws]
        for j in range(nsl):
            acc = jnp.zeros((T, LANES), f32)
            for p in range(n_p):
                wa = ws[p][:, 32 * j:32 * j + 1]
                wb = ws[p][:, 32 * j + 16:32 * j + 17]
                lane = lax.broadcasted_iota(jnp.int32, (T, LANES), 1)
                acc = acc + jnp.where(lane < HEAD_DIM, wa, wb) * slabs[p * (nsl + 1) + j][...]
            out_ref[:, LANES * j:LANES * (j + 1)] = acc.astype(bf16)

    in_specs, args = [], []
    for (o, l), d in zip(parts, DILATIONS):
        in_specs += [_perm_spec(d, ATTN_CH), _perm_spec(d, ATTN_BLOCK)]
        args += [o, l]
    row = lambda c: pl.BlockSpec((T, c), lambda i: (i, 0))
    return pl.pallas_call(
        body, grid=(S // T,), name="attn_merge",
        out_shape=(jax.ShapeDtypeStruct((S, ATTN_CH), bf16), jax.ShapeDtypeStruct((S, ATTN_BLOCK), f32)),
        in_specs=in_specs, out_specs=(row(ATTN_CH), row(ATTN_BLOCK)),
        scratch_shapes=_slabs(n_p * (nsl + 1), T),
        compiler_params=_cp(("parallel",), 40),
    )(*args)


def _attn_fwd(qkvs, bias):
    parts = [_attn_fwd_pattern(q, bias[p], d) for p, (q, d) in enumerate(zip(qkvs, DILATIONS))]
    return _attn_merge(parts)


def _tril_bf16(w):
    row = lax.broadcasted_iota(jnp.int32, (CHUNK, CHUNK), 0)
    col = lax.broadcasted_iota(jnp.int32, (CHUNK, CHUNK), 1)
    return jnp.where(col <= row, w, 0.0).astype(bf16)


def _gmlp_fwd(c_in, ln_g, ln_b, w_s, b_s_t):
    S = c_in.shape[0]
    T = 512

    def body(c_ref, g_ref, be_ref, w_ref, bs_ref, out_ref, mix):
        c = c_ref[...]
        xhat, _ = _ln_stats(c[:, GMLP_CH:])
        vb = (xhat * g_ref[...] + be_ref[...]).astype(bf16)
        for g in range(GMLP_GROUPS):
            wt = _tril_bf16(w_ref[g])
            cs = slice(GMLP_GROUP_DIM * g, GMLP_GROUP_DIM * (g + 1))
            for ci in range(T // CHUNK):
                rs = slice(CHUNK * ci, CHUNK * (ci + 1))
                mix[rs, cs] = jnp.dot(wt, vb[rs, cs], preferred_element_type=f32) + bs_ref[:, g:g + 1]
        out_ref[...] = (c[:, :GMLP_CH] * mix[...]).astype(bf16)

    return pl.pallas_call(
        body, grid=(S // T,), name="gmlp_fwd",
        out_shape=jax.ShapeDtypeStruct((S, GMLP_CH), bf16),
        in_specs=[pl.BlockSpec((T, 2 * GMLP_CH), lambda i: (i, 0)), _acc((1, GMLP_CH)), _acc((1, GMLP_CH)),
                  _acc((GMLP_GROUPS, CHUNK, CHUNK)), _acc((CHUNK, GMLP_GROUPS))],
        out_specs=pl.BlockSpec((T, GMLP_CH), lambda i: (i, 0)),
        scratch_shapes=[pltpu.VMEM((T, GMLP_CH), f32)],
        compiler_params=_cp(("parallel",), 32),
    )(c_in, ln_g, ln_b, w_s, b_s_t)


def _outproj_ln_fwd(conv_out, attn_out, gm_out, w, b, x, ln_g, ln_b):
    S = x.shape[0]
    T = 512

    def body(co_ref, ao_ref, go_ref, w_ref, b_ref, x_ref, g_ref, be_ref, cat_ref, z_ref, y_ref, yb_ref):
        cat = jnp.concatenate([co_ref[...], ao_ref[...], go_ref[...]], axis=1)
        cat_ref[...] = cat
        z = jnp.dot(cat, w_ref[...], preferred_element_type=f32) + b_ref[...] + ALPHA * x_ref[...]
        z_ref[...] = z
        xhat, _ = _ln_stats(z)
        y = xhat * g_ref[...] + be_ref[...]
        y_ref[...] = y
        yb_ref[...] = y.astype(bf16)

    row = lambda c: pl.BlockSpec((T, c), lambda i: (i, 0))
    return pl.pallas_call(
        body, grid=(S // T,), name="outproj_ln_fwd",
        out_shape=(jax.ShapeDtypeStruct((S, D_MODEL), bf16), jax.ShapeDtypeStruct((S, D_MODEL), f32),
                   jax.ShapeDtypeStruct((S, D_MODEL), f32), jax.ShapeDtypeStruct((S, D_MODEL), bf16)),
        in_specs=[row(CONV_CH), row(ATTN_CH), row(GMLP_CH), _resident((D_MODEL, D_MODEL)), _acc((1, D_MODEL)),
                  row(D_MODEL), _acc((1, D_MODEL)), _acc((1, D_MODEL))],
        out_specs=(row(D_MODEL), row(D_MODEL), row(D_MODEL), row(D_MODEL)),
        compiler_params=_cp(("parallel",), 40),
    )(conv_out, attn_out, gm_out, w, b, x, ln_g, ln_b)


GATE_ROWS = 32
GATE_COLS = 128
GATE_MM_COLS = 256
SUBLANES = 8


def _gate_cols(c0):
    return slice(c0, c0 + GATE_COLS), slice(D_FF + c0, D_FF + c0 + GATE_COLS)


def _bcast_rows(ref, k, cs):
    return jnp.broadcast_to(ref[k:k + 1, cs], (GATE_ROWS, GATE_COLS))


def _fold_rows(z):
    acc = z[0:SUBLANES]
    for r in range(SUBLANES, GATE_ROWS, SUBLANES):
        acc = acc + z[r:r + SUBLANES]
    return acc


def _ffn_up_gate_fwd(x1b, w, b, conv_w, conv_b):
    S = x1b.shape[0]
    T = 256
    H = FFN_HALO
    K = FFN_CONV_WIDTH

    def body(x_ref, w_ref, b_ref, cw_ref, cb_ref, hfb_ref, hc_ref, act_ref, hbuf, carry):
        @pl.when(pl.program_id(0) == 0)
        def _():
            carry[...] = jnp.zeros_like(carry)
        x = x_ref[...]
        for m0 in range(0, D_FF, GATE_MM_COLS):
            for cm in (slice(m0, m0 + GATE_MM_COLS), slice(D_FF + m0, D_FF + m0 + GATE_MM_COLS)):
                h = jnp.dot(x, w_ref[:, cm], preferred_element_type=f32) + b_ref[:, cm]
                hbuf[:, cm] = h
                hfb_ref[:, cm] = h.astype(bf16)
            for c0 in range(m0, m0 + GATE_MM_COLS, GATE_COLS):
                cols = _gate_cols(c0)
                wts = [[_bcast_rows(cw_ref, k, cs) for k in range(K)] + [_bcast_rows(cb_ref, 0, cs)] for cs in cols]

                def step(rg, tails, cols=cols, wts=wts):
                    rows = pl.ds(pl.multiple_of(rg * GATE_ROWS, GATE_ROWS), GATE_ROWS)
                    hc, new_tails = [], []
                    for cs, wt, tail in zip(cols, wts, tails):
                        h = hbuf[rows, cs]
                        ext = jnp.concatenate([tail, h], axis=0)
                        acc = wt[K] + wt[K - 1] * h
                        for back in range(1, K):
                            acc = acc + wt[K - 1 - back] * pltpu.roll(ext, back, 0)[H:]
                        hc_ref[rows, cs] = acc
                        hc.append(acc)
                        new_tails.append(h[GATE_ROWS - H:])
                    act_ref[rows, cols[0]] = (hc[0] * _sig(hc[0]) * hc[1]).astype(bf16)
                    return tuple(new_tails)

                tails = lax.fori_loop(0, T // GATE_ROWS, step, tuple(carry[:, cs] for cs in cols), unroll=True)
                for cs, tail in zip(cols, tails):
                    carry[:, cs] = tail

    row = lambda c: pl.BlockSpec((T, c), lambda i: (i, 0))
    return pl.pallas_call(
        body, grid=(S // T,), name="ffn_up_gate_fwd",
        out_shape=(jax.ShapeDtypeStruct((S, 2 * D_FF), bf16), jax.ShapeDtypeStruct((S, 2 * D_FF), f32),
                   jax.ShapeDtypeStruct((S, D_FF), bf16)),
        in_specs=[row(D_MODEL), _resident((D_MODEL, 2 * D_FF)), _acc((1, 2 * D_FF)), _acc((8, 2 * D_FF)),
                  _acc((1, 2 * D_FF))],
        out_specs=(row(2 * D_FF), row(2 * D_FF), row(D_FF)),
        scratch_shapes=[pltpu.VMEM((T, 2 * D_FF), f32), pltpu.VMEM((H, 2 * D_FF), f32)],
        compiler_params=_cp(("arbitrary",), 56),
    )(x1b, w, b, conv_w, conv_b)


def _ffn_down_ln_fwd(act, w, b, x1, ln_g, ln_b):
    S = act.shape[0]
    T = 512

    def body(a_ref, w_ref, b_ref, x_ref, g_ref, be_ref, z_ref, y_ref):
        z = jnp.dot(a_ref[...], w_ref[...], preferred_element_type=f32) + b_ref[...] + ALPHA * x_ref[...]
        z_ref[...] = z
        xhat, _ = _ln_stats(z)
        y_ref[...] = xhat * g_ref[...] + be_ref[...]

    row = lambda c: pl.BlockSpec((T, c), lambda i: (i, 0))
    return pl.pallas_call(
        body, grid=(S // T,), name="ffn_down_ln_fwd",
        out_shape=(jax.ShapeDtypeStruct((S, D_MODEL), f32), jax.ShapeDtypeStruct((S, D_MODEL), f32)),
        in_specs=[row(D_FF), _resident((D_FF, D_MODEL)), _acc((1, D_MODEL)), row(D_MODEL), _acc((1, D_MODEL)),
                  _acc((1, D_MODEL))],
        out_specs=(row(D_MODEL), row(D_MODEL)),
        compiler_params=_cp(("parallel",), 40),
    )(act, w, b, x1, ln_g, ln_b)


def _loss_ln_bwd(y, target, z, ln_g):
    S = y.shape[0]
    T = 512

    def body(y_ref, t_ref, z_ref, g_ref, dz_ref, dzb_ref, loss_ref, dg_ref, db_ref):
        @pl.when(pl.program_id(0) == 0)
        def _():
            loss_ref[...] = jnp.zeros_like(loss_ref)
            dg_ref[...] = jnp.zeros_like(dg_ref)
            db_ref[...] = jnp.zeros_like(db_ref)
        err = y_ref[...] - t_ref[...]
        loss_ref[...] += _colsum(err * err) * (0.5 / D_MODEL)
        dy = err * (1.0 / D_MODEL)
        xhat, rstd = _ln_stats(z_ref[...])
        dz = _ln_bwd(dy, xhat, rstd, g_ref[...])
        dz_ref[...] = dz
        dzb_ref[...] = dz.astype(bf16)
        dg_ref[...] += _colsum(dy * xhat)
        db_ref[...] += _colsum(dy)

    row = pl.BlockSpec((T, D_MODEL), lambda i: (i, 0))
    vec = jax.ShapeDtypeStruct((1, D_MODEL), f32)
    return pl.pallas_call(
        body, grid=(S // T,), name="loss_ln_bwd",
        out_shape=(jax.ShapeDtypeStruct((S, D_MODEL), f32), jax.ShapeDtypeStruct((S, D_MODEL), bf16), vec, vec, vec),
        in_specs=[row, row, row, _acc((1, D_MODEL))],
        out_specs=(row, row, _acc((1, D_MODEL)), _acc((1, D_MODEL)), _acc((1, D_MODEL))),
        compiler_params=_cp(("arbitrary",), 40),
    )(y, target, z, ln_g)


def _dgrad_ln_bwd(g, w, dz_res, z, ln_g, name):
    S, K = g.shape
    T = 256
    with_ln = z is not None

    def body(*refs):
        if with_ln:
            g_ref, w_ref, r_ref, z_ref, lg_ref, dz_ref, dzb_ref, dg_ref, db_ref = refs
        else:
            g_ref, w_ref, r_ref, dx_ref = refs
        dx = lax.dot_general(g_ref[...], w_ref[...], NT_DIMS, preferred_element_type=f32) + ALPHA * r_ref[...]
        if not with_ln:
            dx_ref[...] = dx
            return

        @pl.when(pl.program_id(0) == 0)
        def _():
            dg_ref[...] = jnp.zeros_like(dg_ref)
            db_ref[...] = jnp.zeros_like(db_ref)
        xhat, rstd = _ln_stats(z_ref[...])
        dz = _ln_bwd(dx, xhat, rstd, lg_ref[...])
        dz_ref[...] = dz
        dzb_ref[...] = dz.astype(bf16)
        dg_ref[...] += _colsum(dx * xhat)
        db_ref[...] += _colsum(dx)

    row = pl.BlockSpec((T, D_MODEL), lambda i: (i, 0))
    vec = jax.ShapeDtypeStruct((1, D_MODEL), f32)
    in_specs = [pl.BlockSpec((T, K), lambda i: (i, 0)), _resident((D_MODEL, K)), row]
    args = [g, w, dz_res]
    if with_ln:
        in_specs += [row, _acc((1, D_MODEL))]
        args += [z, ln_g]
        out_shape = (jax.ShapeDtypeStruct((S, D_MODEL), f32), jax.ShapeDtypeStruct((S, D_MODEL), bf16), vec, vec)
        out_specs = (row, row, _acc((1, D_MODEL)), _acc((1, D_MODEL)))
    else:
        out_shape = jax.ShapeDtypeStruct((S, D_MODEL), f32)
        out_specs = row
    return pl.pallas_call(
        body, grid=(S // T,), name=name, out_shape=out_shape, in_specs=in_specs, out_specs=out_specs,
        compiler_params=_cp(("arbitrary",), 48),
    )(*args)


def _ffn_down_gate_bwd(dzb, w_down, hfb, hc, conv_w):
    S = hc.shape[0]
    T = 256
    H = FFN_HALO
    nt = S // T
    K = FFN_CONV_WIDTH

    def body(dz_ref, w_ref, h_ref, hc_ref, cw_ref, dh_ref, dw_ref, dcb_ref, da_buf, carry):
        @pl.when(pl.program_id(0) == 0)
        def _():
            dw_ref[...] = jnp.zeros_like(dw_ref)
            dcb_ref[...] = jnp.zeros_like(dcb_ref)
            carry[...] = jnp.zeros_like(carry)
        da_buf[...] = lax.dot_general(dz_ref[...], w_ref[...], NT_DIMS, preferred_element_type=f32)
        ngroups = T // GATE_ROWS
        for c0 in range(0, D_FF, GATE_COLS):
            cols = _gate_cols(c0)
            wts = [[_bcast_rows(cw_ref, k, cs) for k in range(K)] for cs in cols]

            def step(it, state, cols=cols, wts=wts):
                heads, accs = state
                rows = pl.ds(pl.multiple_of((ngroups - 1 - it) * GATE_ROWS, GATE_ROWS), GATE_ROWS)
                g = hc_ref[rows, cols[0]]
                v = hc_ref[rows, cols[1]]
                da = da_buf[rows, cols[0]]
                sg = _sig(g)
                dms = (da * v * (sg * (1.0 + g * (1.0 - sg))), da * (g * sg))
                new_heads, new_accs = [], []
                for cs, wt, dm, head, acc in zip(cols, wts, dms, heads, accs):
                    h0 = h_ref[rows, cs].astype(f32)
                    ext = jnp.concatenate([dm, head], axis=0)
                    dh = wt[K - 1] * dm
                    acc_k = [None] * K + [acc[K] + _fold_rows(dm)]
                    acc_k[K - 1] = acc[K - 1] + _fold_rows(dm * h0)
                    for ahead in range(1, K):
                        dk = pltpu.roll(ext, GATE_ROWS + H - ahead, 0)[:GATE_ROWS]
                        dh = dh + wt[K - 1 - ahead] * dk
                        acc_k[K - 1 - ahead] = acc[K - 1 - ahead] + _fold_rows(dk * h0)
                    dh_ref[rows, cs] = dh.astype(bf16)
                    new_heads.append(dm[:H])
                    new_accs.append(tuple(acc_k))
                return tuple(new_heads), tuple(new_accs)

            zero = jnp.zeros((SUBLANES, GATE_COLS), f32)
            init = (tuple(carry[:, cs] for cs in cols), tuple(tuple(zero for _ in range(K + 1)) for _ in cols))
            heads, accs = lax.fori_loop(0, ngroups, step, init, unroll=True)
            for cs, head, acc in zip(cols, heads, accs):
                carry[:, cs] = head
                dcb_ref[:, cs] += _colsum(acc[K])
                for k in range(K):
                    dw_ref[k:k + 1, cs] += _colsum(acc[k])

    tile = lambda c: pl.BlockSpec((T, c), lambda i: (nt - 1 - i, 0))
    return pl.pallas_call(
        body, grid=(nt,), name="ffn_down_gate_bwd",
        out_shape=(jax.ShapeDtypeStruct((S, 2 * D_FF), bf16), jax.ShapeDtypeStruct((8, 2 * D_FF), f32),
                   jax.ShapeDtypeStruct((1, 2 * D_FF), f32)),
        in_specs=[tile(D_MODEL), _resident((D_FF, D_MODEL)), tile(2 * D_FF), tile(2 * D_FF), _acc((8, 2 * D_FF))],
        out_specs=(tile(2 * D_FF), _acc((8, 2 * D_FF)), _acc((1, 2 * D_FF))),
        scratch_shapes=[pltpu.VMEM((T, D_FF), f32), pltpu.VMEM((H, 2 * D_FF), f32)],
        compiler_params=_cp(("arbitrary",), 48),
    )(dzb, w_down, hfb, hc, conv_w)


def _wgrad(a, g, tn, name):
    S, K = a.shape
    N = g.shape[1]
    T = 1024 if S % 1024 == 0 else S

    def body(a_ref, g_ref, dw_ref, db_ref):
        @pl.when(pl.program_id(1) == 0)
        def _():
            dw_ref[...] = jnp.zeros_like(dw_ref)
            db_ref[...] = jnp.zeros_like(db_ref)
        gt = g_ref[...]
        dw_ref[...] += lax.dot_general(a_ref[...].astype(bf16), gt, TN_DIMS, preferred_element_type=f32)
        db_ref[...] += _colsum(gt.astype(f32))

    return pl.pallas_call(
        body, grid=(N // tn, S // T), name=name,
        out_shape=(jax.ShapeDtypeStruct((K, N), f32), jax.ShapeDtypeStruct((1, N), f32)),
        in_specs=[pl.BlockSpec((T, K), lambda j, i: (i, 0)), pl.BlockSpec((T, tn), lambda j, i: (i, j))],
        out_specs=(pl.BlockSpec((K, tn), lambda j, i: (0, j)), pl.BlockSpec((1, tn), lambda j, i: (0, j))),
        compiler_params=_cp(("parallel", "arbitrary"), 48),
    )(a, g)


def _outproj_dgrad(dzb, w, attn_out, lse):
    S = dzb.shape[0]
    T = PERM_TILE
    nsl = ATTN_CH // LANES
    n_p = len(DILATIONS)

    def body(g_ref, w_ref, ao_ref, lse_ref, dco_ref, dgo_ref, *rest):
        do_refs = rest[:n_p]
        st_refs = rest[n_p:2 * n_p]
        slabs = rest[2 * n_p:]
        dcat = lax.dot_general(g_ref[...], w_ref[...], NT_DIMS, preferred_element_type=f32)
        dco_ref[...] = dcat[:, :CONV_CH]
        dgo_ref[...] = dcat[:, CONV_CH + ATTN_CH:]
        lane = lax.broadcasted_iota(jnp.int32, (T, LANES), 1)
        st = lse_ref[...]
        for j in range(nsl):
            dO = dcat[:, CONV_CH + LANES * j:CONV_CH + LANES * (j + 1)]
            prod = dO * ao_ref[:, LANES * j:LANES * (j + 1)].astype(f32)
            for a in range(2):
                in_head = (lane < HEAD_DIM) if a == 0 else (lane >= HEAD_DIM)
                delta = jnp.sum(jnp.where(in_head, prod, 0.0), axis=1, keepdims=True)
                st = jnp.where((lane // 16 == 2 * j + a) & (lane % 16 >= 8), delta, st)
            slabs[j][...] = dO
        slabs[nsl][...] = st
        for d, do_ref, st_ref in zip(DILATIONS, do_refs, st_refs):
            for r in range(d):
                for j in range(nsl):
                    do_ref[r, :, LANES * j:LANES * (j + 1)] = _rows_of(slabs[j], r, T // d, d).astype(bf16)
                st_ref[r] = _rows_of(slabs[nsl], r, T // d, d)

    row = lambda c: pl.BlockSpec((T, c), lambda i: (i, 0))
    return pl.pallas_call(
        body, grid=(S // T,), name="outproj_dgrad",
        out_shape=(jax.ShapeDtypeStruct((S, CONV_CH), f32), jax.ShapeDtypeStruct((S, GMLP_CH), f32))
        + tuple(_perm_shape(S, d, ATTN_CH, bf16) for d in DILATIONS)
        + tuple(_perm_shape(S, d, ATTN_BLOCK, f32) for d in DILATIONS),
        in_specs=[row(D_MODEL), _resident((D_MODEL, D_MODEL)), row(ATTN_CH), row(ATTN_BLOCK)],
        out_specs=(row(CONV_CH), row(GMLP_CH)) + tuple(_perm_spec(d, ATTN_CH) for d in DILATIONS)
        + tuple(_perm_spec(d, ATTN_BLOCK) for d in DILATIONS),
        scratch_shapes=_slabs(nsl + 1, T),
        compiler_params=_cp(("parallel",), 40),
    )(dzb, w, attn_out, lse)


def _gmlp_bwd(c_in, dgm, ln_g, ln_b, w_s, b_s_t):
    S = c_in.shape[0]
    T = 512
    nsteps = S // T

    def body(c_ref, dg_ref, g_ref, be_ref, w_ref, bs_ref, dc_ref, dlg_ref, dlb_ref, dw_ref, dbs_ref,
             du_buf, dv_buf, dm_acc):
        i = pl.program_id(0)

        @pl.when(i == 0)
        def _():
            dlg_ref[...] = jnp.zeros_like(dlg_ref)
            dlb_ref[...] = jnp.zeros_like(dlb_ref)
            dw_ref[...] = jnp.zeros_like(dw_ref)
            dm_acc[...] = jnp.zeros_like(dm_acc)
        c = c_ref[...]
        u = c[:, :GMLP_CH]
        xhat, rstd = _ln_stats(c[:, GMLP_CH:])
        vb = (xhat * g_ref[...] + be_ref[...]).astype(bf16)
        dgm_t = dg_ref[...]
        dm_all = dgm_t * u
        for g in range(GMLP_GROUPS):
            wt = _tril_bf16(w_ref[g])
            cs = slice(GMLP_GROUP_DIM * g, GMLP_GROUP_DIM * (g + 1))
            dw_g = jnp.zeros((CHUNK, CHUNK), f32)
            for ci in range(T // CHUNK):
                rs = slice(CHUNK * ci, CHUNK * (ci + 1))
                v_c = vb[rs, cs]
                mixed = jnp.dot(wt, v_c, preferred_element_type=f32) + bs_ref[:, g:g + 1]
                dm = dm_all[rs, cs]
                dmb = dm.astype(bf16)
                du_buf[rs, cs] = dgm_t[rs, cs] * mixed
                dv_buf[rs, cs] = lax.dot_general(wt, dmb, TN_DIMS, preferred_element_type=f32)
                dw_g = dw_g + lax.dot_general(dmb, v_c, NT_DIMS, preferred_element_type=f32)
                dm_acc[:, cs] += dm
            dw_ref[g] += dw_g
        dv = dv_buf[...]
        dvr = _ln_bwd(dv, xhat, rstd, g_ref[...])
        dlg_ref[...] += _colsum(dv * xhat)
        dlb_ref[...] += _colsum(dv)
        dc_ref[:, :GMLP_CH] = du_buf[...].astype(bf16)
        dc_ref[:, GMLP_CH:] = dvr.astype(bf16)

        @pl.when(i == nsteps - 1)
        def _():
            row = lax.broadcasted_iota(jnp.int32, (CHUNK, CHUNK), 0)
            col = lax.broadcasted_iota(jnp.int32, (CHUNK, CHUNK), 1)
            tile = jnp.zeros((CHUNK, CHUNK), f32)
            for g in range(GMLP_GROUPS):
                dw_ref[g] = jnp.where(col <= row, dw_ref[g], 0.0)
                gsum = jnp.sum(dm_acc[:, GMLP_GROUP_DIM * g:GMLP_GROUP_DIM * (g + 1)], axis=1, keepdims=True)
                tile = jnp.where(col == g, gsum, tile)
            dbs_ref[...] = tile

    vec = jax.ShapeDtypeStruct((1, GMLP_CH), f32)
    return pl.pallas_call(
        body, grid=(nsteps,), name="gmlp_bwd",
        out_shape=(jax.ShapeDtypeStruct((S, 2 * GMLP_CH), bf16), vec, vec,
                   jax.ShapeDtypeStruct((GMLP_GROUPS, CHUNK, CHUNK), f32), jax.ShapeDtypeStruct((CHUNK, CHUNK), f32)),
        in_specs=[pl.BlockSpec((T, 2 * GMLP_CH), lambda i: (i, 0)), pl.BlockSpec((T, GMLP_CH), lambda i: (i, 0)),
                  _acc((1, GMLP_CH)), _acc((1, GMLP_CH)), _acc((GMLP_GROUPS, CHUNK, CHUNK)), _acc((CHUNK, GMLP_GROUPS))],
        out_specs=(pl.BlockSpec((T, 2 * GMLP_CH), lambda i: (i, 0)), _acc((1, GMLP_CH)), _acc((1, GMLP_CH)),
                   _acc((GMLP_GROUPS, CHUNK, CHUNK)), _acc((CHUNK, CHUNK))),
        scratch_shapes=[pltpu.VMEM((T, GMLP_CH), f32), pltpu.VMEM((T, GMLP_CH), f32), pltpu.VMEM((CHUNK, GMLP_CH), f32)],
        compiler_params=_cp(("arbitrary",), 32),
    )(c_in, dgm, ln_g, ln_b, w_s, b_s_t)


def _attn_bwd_pattern(qkv, d_out, stats, bias, d):
    _, L, _ = qkv.shape
    B = ATTN_BLOCK
    QB = _attn_tile(L)
    nsb = QB // B
    nt = L // QB
    U = nsb * UNITS_PER_BLOCK
    KV = 2 * ATTN_CH

    def body(cur_ref, hk_ref, hv_ref, do_ref, st_ref, b_ref, dqkv_ref, dbias_ref, lg, dp, pb, dsb, dkv, carry):
        r = pl.program_id(0)
        i = pl.program_id(1)
        n = nt - 1 - i

        @pl.when((r == 0) & (i == 0))
        def _():
            dbias_ref[...] = jnp.zeros_like(dbias_ref)

        @pl.when(i == 0)
        def _():
            carry[...] = jnp.zeros_like(carry)

        def operands(b, j, a):
            rows = slice(B * b, B * (b + 1))
            q2 = cur_ref[rows, LANES * j:LANES * (j + 1)]
            do2 = do_ref[rows, LANES * j:LANES * (j + 1)]
            keep = _head_lanes(a)
            return jnp.where(keep, q2, jnp.zeros_like(q2)), jnp.where(keep, do2, jnp.zeros_like(do2))

        for b in range(nsb):
            valid = _band_mask(b == 0, n)
            for j in range(HEAD_PAIRS):
                k2 = _pair_keys(cur_ref, hk_ref, 1, b, j)
                v2 = _pair_keys(cur_ref, hv_ref, 2, b, j)
                for a in range(2):
                    u = (b * HEAD_PAIRS + j) * 2 + a
                    qm, dom = operands(b, j, a)
                    logits = lax.dot_general(qm, k2, NT_DIMS, preferred_element_type=f32) + b_ref[2 * j + a]
                    lg[B * u:B * (u + 1), :] = jnp.where(valid, logits, NEG)
                    dp[B * u:B * (u + 1), :] = lax.dot_general(dom, v2, NT_DIMS, preferred_element_type=f32)
        for b in range(nsb):
            for j in range(HEAD_PAIRS):
                for a in range(2):
                    u = (b * HEAD_PAIRS + j) * 2 + a
                    rows = slice(B * u, B * (u + 1))
                    lane0 = 32 * j + 16 * a
                    lse = st_ref[B * b:B * (b + 1), lane0:lane0 + 1]
                    delta = st_ref[B * b:B * (b + 1), lane0 + 8:lane0 + 9]
                    p = jnp.exp(lg[rows, :] - lse)
                    ds = p * (dp[rows, :] - delta)
                    pb[rows, :] = p.astype(bf16)
                    dsb[rows, :] = ds.astype(bf16)
                    dbias_ref[2 * j + a] += ds
        dkv[...] = jnp.zeros_like(dkv)
        for b in range(nsb):
            for j in range(HEAD_PAIRS):
                k2 = _pair_keys(cur_ref, hk_ref, 1, b, j)
                dq, dk2, dv2 = [], None, None
                for a in range(2):
                    u = (b * HEAD_PAIRS + j) * 2 + a
                    rows = slice(B * u, B * (u + 1))
                    qm, dom = operands(b, j, a)
                    ds_u = dsb[rows, :]
                    dq.append(jnp.dot(ds_u, k2, preferred_element_type=f32))
                    dk_u = lax.dot_general(ds_u, qm, TN_DIMS, preferred_element_type=f32)
                    dv_u = lax.dot_general(pb[rows, :], dom, TN_DIMS, preferred_element_type=f32)
                    dk2 = dk_u if dk2 is None else dk2 + dk_u
                    dv2 = dv_u if dv2 is None else dv2 + dv_u
                dq2 = jnp.where(_head_lanes(0), dq[0], dq[1]) * (HEAD_DIM ** -0.5)
                dqkv_ref[B * b:B * (b + 1), LANES * j:LANES * (j + 1)] = dq2.astype(bf16)
                dkv[B * b:B * (b + 2), LANES * j:LANES * (j + 1)] += dk2
                dkv[B * b:B * (b + 2), ATTN_CH + LANES * j:ATTN_CH + LANES * (j + 1)] += dv2
        dkv[QB:, :] += carry[...]
        dqkv_ref[:, ATTN_CH:] = dkv[B:, :].astype(bf16)
        carry[...] = dkv[0:B, :]

    halo = lambda part: pl.BlockSpec((None, B, ATTN_CH),
                                     lambda r, i: (r, jnp.maximum((nt - 1 - i) * nsb - 1, 0), part))
    tile_spec = lambda c: pl.BlockSpec((None, QB, c), lambda r, i: (r, nt - 1 - i, 0))
    return pl.pallas_call(
        body, grid=(d, nt), name=f"attn_bwd_d{d}",
        out_shape=(jax.ShapeDtypeStruct((d, L, QKV_CH), bf16), jax.ShapeDtypeStruct((ATTN_HEADS, B, 2 * B), f32)),
        in_specs=[tile_spec(QKV_CH), halo(1), halo(2), tile_spec(ATTN_CH), tile_spec(B),
                  _resident((ATTN_HEADS, B, 2 * B))],
        out_specs=(tile_spec(QKV_CH), _acc((ATTN_HEADS, B, 2 * B))),
        scratch_shapes=[pltpu.VMEM((U * B, 2 * B), f32), pltpu.VMEM((U * B, 2 * B), f32),
                        pltpu.VMEM((U * B, 2 * B), bf16), pltpu.VMEM((U * B, 2 * B), bf16),
                        pltpu.VMEM((B + QB, KV), f32), pltpu.VMEM((B, KV), f32)],
        compiler_params=_cp(("arbitrary", "arbitrary"), 48),
    )(qkv, qkv, qkv, d_out, stats, bias)


def _attn_bwd_merge(d_a, dqkvs, d_c):
    S = d_a.shape[0]
    T = PERM_TILE
    nsl = QKV_CH // LANES
    n_p = len(DILATIONS)

    def body(da_ref, *rest):
        g_refs = rest[:n_p]
        dc_ref, dh_ref = rest[n_p:n_p + 2]
        slabs = rest[n_p + 2:]
        q0 = 2 * CONV_CH
        dh_ref[:, :q0] = da_ref[...]
        dh_ref[:, q0 + QKV_CH:] = dc_ref[...]
        for p, (d, g_ref) in enumerate(zip(DILATIONS, g_refs)):
            for r in range(d):
                for j in range(nsl):
                    _set_rows_of(slabs[p * nsl + j], r, T // d, d, g_ref[r, :, LANES * j:LANES * (j + 1)].astype(f32))
        for j in range(nsl):
            acc = slabs[j][...]
            for p in range(1, n_p):
                acc = acc + slabs[p * nsl + j][...]
            dh_ref[:, q0 + LANES * j:q0 + LANES * (j + 1)] = acc.astype(bf16)

    row = lambda c: pl.BlockSpec((T, c), lambda i: (i, 0))
    return pl.pallas_call(
        body, grid=(S // T,), name="attn_bwd_merge", out_shape=jax.ShapeDtypeStruct((S, IN_CH), bf16),
        in_specs=[row(2 * CONV_CH)] + [_perm_spec(d, QKV_CH) for d in DILATIONS] + [row(2 * GMLP_CH)],
        out_specs=row(IN_CH), scratch_shapes=_slabs(n_p * nsl, T),
        compiler_params=_cp(("parallel",), 48),
    )(d_a, *dqkvs, d_c)


def _bias_table_grad(dbias, buckets):
    n = dbias.shape[0]

    def body(db_ref, bk_ref, o_ref):
        p = pl.program_id(0)
        h = pl.program_id(1)

        @pl.when((p == 0) & (h == 0))
        def _():
            o_ref[...] = jnp.zeros_like(o_ref)
        ids = bk_ref[0]
        db = db_ref[0, 0]
        row = lax.broadcasted_iota(jnp.int32, (N_BUCKETS, 128), 0)
        lane = lax.broadcasted_iota(jnp.int32, (N_BUCKETS, 128), 1)
        upd = jnp.zeros((N_BUCKETS, 128), f32)
        for b in range(N_BUCKETS):
            s = jnp.sum(jnp.sum(jnp.where(ids == b, db, 0.0), axis=1, keepdims=True), axis=0, keepdims=True)
            upd = jnp.where((row == b) & (lane == h), s, upd)
        o_ref[...] += upd

    return pl.pallas_call(
        body, grid=(n, ATTN_HEADS), name="bias_table_grad",
        out_shape=jax.ShapeDtypeStruct((N_BUCKETS, 128), f32),
        in_specs=[pl.BlockSpec((1, 1, ATTN_BLOCK, 2 * ATTN_BLOCK), lambda p, h: (p, h, 0, 0)),
                  pl.BlockSpec((1, ATTN_BLOCK, 2 * ATTN_BLOCK), lambda p, h: (p, 0, 0))],
        out_specs=_acc((N_BUCKETS, 128)),
        compiler_params=_cp(("arbitrary", "arbitrary"), 16),
    )(dbias, buckets)


def _conv_bwd(a_in, hc, dco, dw_w, ln_g, ln_b):
    S = a_in.shape[0]
    T = 512
    hb = T // CONV_HALO
    nsteps = S // T
    R = T + CONV_HALO
    K = CONV_WIDTH

    def body(a_ref, hc_ref, hcn_ref, d_ref, dn_ref, w_ref, g_ref, be_ref,
             da_ref, dw_ref, dcb_ref, dlg_ref, dlb_ref, ext, dbuf, wacc):
        i = pl.program_id(0)

        @pl.when(i == 0)
        def _():
            wacc[...] = jnp.zeros_like(wacc)
            dcb_ref[...] = jnp.zeros_like(dcb_ref)
            dlg_ref[...] = jnp.zeros_like(dlg_ref)
            dlb_ref[...] = jnp.zeros_like(dlb_ref)
        ext[0:T, :] = hc_ref[...]
        ext[T:, :] = hcn_ref[...]
        xhat, rstd = _ln_stats(ext[...])
        hl = xhat * g_ref[...] + be_ref[...]
        ext[0:T, :] = d_ref[...]
        ext[T:, :] = dn_ref[...]
        sl_ = _sig(hl)
        dhl = ext[...] * (sl_ * (1.0 + hl * (1.0 - sl_)))
        dhc = _ln_bwd(dhl, xhat, rstd, g_ref[...])
        rowi = lax.broadcasted_iota(jnp.int32, (R, CONV_CH), 0)
        dbuf[...] = jnp.where((rowi < T) | (i < nsteps - 1), dhc, 0.0)
        dlg_ref[...] += _colsum(dhl[:T] * xhat[:T])
        dlb_ref[...] += _colsum(dhl[:T])
        dcb_ref[...] += _colsum(dbuf[pl.ds(0, T), :])
        starts = _window_rolls(range(K))
        slabs = [slice(LANES * j, LANES * (j + 1)) for j in range(CONV_CH // LANES)]

        def step(g, _):
            r0 = pl.multiple_of(g * CONV_GROUP, CONV_GROUP)
            rows = pl.ds(r0, CONV_GROUP)
            for j, cs in enumerate(slabs):
                gate_cs = slice(CONV_CH + LANES * j, CONV_CH + LANES * (j + 1))
                win = dbuf[pl.ds(r0, CONV_GROUP + CONV_HALO), cs]
                a = a_ref[rows, cs]
                sg = _sig(a_ref[rows, gate_cs])
                hg = a * sg
                dhg = jnp.zeros((CONV_GROUP, LANES), f32)
                for b, ss in starts.items():
                    rolled = win if b == 0 else pltpu.roll(win, b, 0)
                    for s in ss:
                        k = K - 1 - s
                        dk = rolled[s + b:s + b + CONV_GROUP]
                        dhg = dhg + w_ref[k:k + 1, cs] * dk
                        prod = dk * hg
                        fold = prod[0:SUBLANES]
                        for r in range(SUBLANES, CONV_GROUP, SUBLANES):
                            fold = fold + prod[r:r + SUBLANES]
                        wacc[SUBLANES * k:SUBLANES * (k + 1), cs] += fold
                da_ref[rows, cs] = (dhg * sg).astype(bf16)
                da_ref[rows, gate_cs] = (dhg * hg * (1.0 - sg)).astype(bf16)
            return 0

        lax.fori_loop(0, T // CONV_GROUP, step, 0)

        @pl.when(i == nsteps - 1)
        def _():
            for k in range(K):
                dw_ref[k:k + 1, :] = _colsum(wacc[SUBLANES * k:SUBLANES * (k + 1), :])
            dw_ref[K:, :] = jnp.zeros((32 - K, CONV_CH), f32)

    vec = jax.ShapeDtypeStruct((1, CONV_CH), f32)
    nxt = lambda i: (jnp.minimum((i + 1) * hb, nsteps * hb - 1), 0)
    return pl.pallas_call(
        body, grid=(nsteps,), name="conv_bwd",
        out_shape=(jax.ShapeDtypeStruct((S, 2 * CONV_CH), bf16), jax.ShapeDtypeStruct((32, CONV_CH), f32), vec, vec, vec),
        in_specs=[pl.BlockSpec((T, 2 * CONV_CH), lambda i: (i, 0)),
                  pl.BlockSpec((T, CONV_CH), lambda i: (i, 0)), pl.BlockSpec((CONV_HALO, CONV_CH), nxt),
                  pl.BlockSpec((T, CONV_CH), lambda i: (i, 0)), pl.BlockSpec((CONV_HALO, CONV_CH), nxt),
                  _acc((32, CONV_CH)), _acc((1, CONV_CH)), _acc((1, CONV_CH))],
        out_specs=(pl.BlockSpec((T, 2 * CONV_CH), lambda i: (i, 0)), _acc((32, CONV_CH)), _acc((1, CONV_CH)),
                   _acc((1, CONV_CH)), _acc((1, CONV_CH))),
        scratch_shapes=[pltpu.VMEM((R, CONV_CH), f32), pltpu.VMEM((R, CONV_CH), f32),
                        pltpu.VMEM((SUBLANES * 32, CONV_CH), f32)],
        compiler_params=_cp(("arbitrary",), 32),
    )(a_in, hc, hc, dco, dco, dw_w, ln_g, ln_b)


def _adamw(g, w, m, v, name):
    R, C = g.shape
    T = R
    for cand in (512, 256, 128, 64, 32, 16, 8):
        if R % cand == 0 and cand * C * 4 <= MIB:
            T = cand
            break
    c1 = 1.0 / (1.0 - ADAM_B1 ** ADAM_STEP)
    c2 = 1.0 / (1.0 - ADAM_B2 ** ADAM_STEP)

    def body(g_ref, w_ref, m_ref, v_ref, d_ref, nm_ref, nv_ref):
        gg = g_ref[...]
        nm = ADAM_B1 * m_ref[...] + (1.0 - ADAM_B1) * gg
        nv = ADAM_B2 * v_ref[...] + (1.0 - ADAM_B2) * (gg * gg)
        nm_ref[...] = nm
        nv_ref[...] = nv
        d_ref[...] = -ADAM_LR * ((nm * c1) / (jnp.sqrt(nv * c2) + ADAM_EPS) + ADAM_WD * w_ref[...])

    blk = pl.BlockSpec((T, C), lambda i: (i, 0))
    sd = jax.ShapeDtypeStruct((R, C), f32)
    return pl.pallas_call(
        body, grid=(R // T,), name=name, out_shape=(sd, sd, sd), in_specs=[blk] * 4, out_specs=(blk, blk, blk),
        compiler_params=_cp(("parallel",), 48),
    )(g, w, m, v)


def _pad_rows(a, rows):
    return jnp.pad(a, ((0, rows - a.shape[0]), (0, 0)))


def _local_step(x, target, wb, late_weights, sp):
    buckets = jnp.asarray(_bucket_ids())
    bias = _bias_build(sp["rel_bias_table"], buckets)
    wb = dict(wb)
    saved = []
    xl = x
    for l in range(DEPTH):
        vec = lambda name: sp[name][l][None, :]
        a_in, c_in, *qkv = _inproj_fwd(xl, wb["w_in"][l], vec("b_in"))
        conv_w = _pad_rows(sp["conv_dw_w"][l], 32)
        conv_out, hc = _conv_fwd(a_in, conv_w, vec("conv_dw_b"), vec("conv_ln_g"), vec("conv_ln_b"))
        attn_out, lse = _attn_fwd(qkv, bias)
        bs_t = sp["gmlp_b_s"][l].T
        gm_out = _gmlp_fwd(c_in, vec("gmlp_ln_g"), vec("gmlp_ln_b"), sp["gmlp_w_s"][l], bs_t)
        if l == 0:
            wb.update(late_weights(gm_out))
        cat, z1, x1, x1b = _outproj_ln_fwd(conv_out, attn_out, gm_out, wb["w_out"][l], vec("b_out"), xl,
                                           vec("ln1_g"), vec("ln1_b"))
        fconv_w = _pad_rows(sp["ffn_conv_w"][l], 8)
        hfb, fhc, act = _ffn_up_gate_fwd(x1b, wb["ffn_w_up"][l], vec("ffn_b_up"), fconv_w, vec("ffn_conv_b"))
        z2, x2 = _ffn_down_ln_fwd(act, wb["ffn_w_down"][l], vec("ffn_b_down"), x1, vec("ln2_g"), vec("ln2_b"))
        saved.append(dict(x=xl, a_in=a_in, qkv=qkv, c_in=c_in, hc=hc, attn_out=attn_out, lse=lse, cat=cat, z1=z1,
                          x1b=x1b, hfb=hfb, fhc=fhc, act=act, z2=z2, conv_w=conv_w, fconv_w=fconv_w, bs_t=bs_t))
        xl = x2

    grads = {}
    per_layer = {k: [None] * DEPTH for k in (
        "w_in", "b_in", "conv_dw_w", "conv_dw_b", "conv_ln_g", "conv_ln_b", "gmlp_ln_g", "gmlp_ln_b", "gmlp_w_s",
        "gmlp_b_s", "w_out", "b_out", "ln1_g", "ln1_b", "ffn_w_up", "ffn_b_up", "ffn_conv_w", "ffn_conv_b",
        "ffn_w_down", "ffn_b_down", "ln2_g", "ln2_b")}
    dbias_all = []
    l = DEPTH - 1
    vec = lambda name: sp[name][l][None, :]
    dz2, dz2b, loss_part, dg2, db2 = _loss_ln_bwd(xl, target, saved[l]["z2"], vec("ln2_g"))
    loss = jnp.sum(loss_part)
    grad_x = None
    for l in reversed(range(DEPTH)):
        sv = saved[l]
        vec = lambda name: sp[name][l][None, :]
        per_layer["ln2_g"][l] = dg2[0]
        per_layer["ln2_b"][l] = db2[0]
        dw_down, db_down = _wgrad(sv["act"], dz2b, 512, "ffn_down_wgrad")
        per_layer["ffn_w_down"][l] = dw_down
        per_layer["ffn_b_down"][l] = db_down[0]
        dhf, dfcw, dfcb = _ffn_down_gate_bwd(dz2b, wb["ffn_w_down"][l], sv["hfb"], sv["fhc"], sv["fconv_w"])
        per_layer["ffn_conv_w"][l] = dfcw[:FFN_CONV_WIDTH]
        per_layer["ffn_conv_b"][l] = dfcb[0]
        dw_up, db_up = _wgrad(sv["x1b"], dhf, 1408, "ffn_up_wgrad")
        per_layer["ffn_w_up"][l] = dw_up
        per_layer["ffn_b_up"][l] = db_up[0]
        dz1, dz1b, dg1, db1 = _dgrad_ln_bwd(dhf, wb["ffn_w_up"][l], dz2, sv["z1"], vec("ln1_g"), "ffn_up_dgrad_ln")
        per_layer["ln1_g"][l] = dg1[0]
        per_layer["ln1_b"][l] = db1[0]
        dw_out, db_out = _wgrad(sv["cat"], dz1b, 512, "outproj_wgrad")
        per_layer["w_out"][l] = dw_out
        per_layer["b_out"][l] = db_out[0]
        dco, dgo, *perm = _outproj_dgrad(dz1b, wb["w_out"][l], sv["attn_out"], sv["lse"])
        d_outs, stats = perm[:len(DILATIONS)], perm[len(DILATIONS):]
        d_c, dglg, dglb, dws, dbs = _gmlp_bwd(sv["c_in"], dgo, vec("gmlp_ln_g"), vec("gmlp_ln_b"), sp["gmlp_w_s"][l],
                                              sv["bs_t"])
        per_layer["gmlp_ln_g"][l] = dglg[0]
        per_layer["gmlp_ln_b"][l] = dglb[0]
        per_layer["gmlp_w_s"][l] = dws
        per_layer["gmlp_b_s"][l] = dbs[:, :GMLP_GROUPS].T
        dqkvs = []
        for p, d in enumerate(DILATIONS):
            dqkv, dbias = _attn_bwd_pattern(sv["qkv"][p], d_outs[p], stats[p], bias[p], d)
            dqkvs.append(dqkv)
            dbias_all.append(dbias)
        d_a, dcw, dcb, dclg, dclb = _conv_bwd(sv["a_in"], sv["hc"], dco, sv["conv_w"], vec("conv_ln_g"),
                                              vec("conv_ln_b"))
        per_layer["conv_dw_w"][l] = dcw[:CONV_WIDTH]
        per_layer["conv_dw_b"][l] = dcb[0]
        per_layer["conv_ln_g"][l] = dclg[0]
        per_layer["conv_ln_b"][l] = dclb[0]
        dh = _attn_bwd_merge(d_a, dqkvs, d_c)
        dw_in, db_in = _wgrad(sv["x"], dh, 640, "inproj_wgrad")
        per_layer["w_in"][l] = dw_in
        per_layer["b_in"][l] = db_in[0]
        if l > 0:
            pv = saved[l - 1]
            dz2, dz2b, dg2, db2 = _dgrad_ln_bwd(dh, wb["w_in"][l], dz1, pv["z2"], sp["ln2_g"][l - 1][None, :],
                                                "inproj_dgrad_ln")
        else:
            grad_x = _dgrad_ln_bwd(dh, wb["w_in"][l], dz1, None, None, "inproj_dgrad")
    for k, v in per_layer.items():
        grads[k] = v if k in BIG else jnp.stack(v)
    dbias_cat = jnp.stack(dbias_all)
    bk_cat = jnp.concatenate([buckets] * DEPTH, axis=0)
    grads["rel_bias_table"] = _bias_table_grad(dbias_cat, bk_cat)[:, :ATTN_HEADS]
    return loss, grad_x, grads


N_CHIPS = 4
BIG = {"w_in": (D_MODEL, IN_CH, 1), "w_out": (D_MODEL, D_MODEL, 0),
       "ffn_w_up": (D_MODEL, 2 * D_FF, 1), "ffn_w_down": (D_FF, D_MODEL, 0)}
SMALL = ("b_in", "conv_dw_w", "conv_dw_b", "conv_ln_g", "conv_ln_b", "rel_bias_table", "gmlp_ln_g", "gmlp_ln_b",
         "gmlp_w_s", "gmlp_b_s", "b_out", "ln1_g", "ln1_b", "ffn_b_up", "ffn_conv_w", "ffn_conv_b", "ffn_b_down",
         "ln2_g", "ln2_b")
SMALL_SHARDED = ("conv_dw_w", "ffn_conv_w")
WEIGHTS = ("w_in", "b_in", "conv_dw_w", "conv_dw_b", "conv_ln_g", "conv_ln_b", "rel_bias_table", "gmlp_ln_g",
           "gmlp_ln_b", "gmlp_w_s", "gmlp_b_s", "w_out", "b_out", "ln1_g", "ln1_b", "ffn_w_up", "ffn_b_up",
           "ffn_conv_w", "ffn_conv_b", "ffn_w_down", "ffn_b_down", "ln2_g", "ln2_b")
ANY = pl.BlockSpec(memory_space=pl.ANY)


def _position():
    return lax.axis_index("x"), lax.axis_index("y"), lax.axis_index("c")


def _other_chips(x, y):
    return [(1 - x, y), (x, 1 - y), (1 - x, 1 - y)]


def _cast_bf16(a):
    R, C = a.shape
    T = 128

    def body(a_ref, o_ref):
        o_ref[...] = a_ref[...].astype(bf16)

    return pl.pallas_call(
        body, grid=(R // T,), name="cast_bf16", out_shape=jax.ShapeDtypeStruct((R, C), bf16),
        in_specs=[pl.BlockSpec((T, C), lambda i: (i, 0))], out_specs=pl.BlockSpec((T, C), lambda i: (i, 0)),
        compiler_params=_cp(("parallel",), 16),
    )(a)


def _chip_slot(ref, name, l, p):
    K, N, ax = BIG[name]
    if ax == 1:
        sz = N // N_CHIPS
        return ref.at[l, :, pl.ds(pl.multiple_of(p * sz, 128), sz)]
    sz = K // N_CHIPS
    return ref.at[l, pl.ds(pl.multiple_of(p * sz, 16), sz), :]


def _gather_weights(shards, conv_w, fconv_w):
    names = list(shards)
    n_big = len(names)
    n_t = n_big + 2
    n_chip = 3 * n_t
    n_pass = 3 * n_big

    def body(*refs):
        ins = refs[:n_t]
        outs = refs[n_t:2 * n_t]
        send_sems, recv_sems, pass_send, pass_recv, local_sems = refs[2 * n_t:]
        x, y, c = _position()
        me = 2 * x + y
        chips = _other_chips(x, y)

        def src(t):
            return ins[t].at[c] if t < n_big else ins[t]

        def slot(t, l, p):
            return _chip_slot(outs[t], names[t], l, p) if t < n_big else outs[t].at[p]

        locs, cps = [], []
        for t in range(n_t):
            for l in (range(DEPTH) if t < n_big else (0,)):
                loc = pltpu.make_async_copy(ins[t].at[l] if t < n_big else ins[t], slot(t, l, me),
                                            local_sems.at[DEPTH * t + l])
                loc.start()
                locs.append(loc)
            for k, (px, py) in enumerate(chips):
                cp = pltpu.make_async_remote_copy(
                    src_ref=src(t), dst_ref=slot(t, c, me), send_sem=send_sems.at[3 * t + k],
                    recv_sem=recv_sems.at[3 * t + k], device_id=(px, py, c), device_id_type=MESH_ID)
                cp.start()
                cps.append(cp)
        for t in range(n_t):
            for k, (px, py) in enumerate(chips):
                landed = slot(t, c, 2 * px + py)
                pltpu.make_async_remote_copy(
                    src_ref=src(t), dst_ref=landed, send_sem=send_sems.at[3 * t + k],
                    recv_sem=recv_sems.at[3 * t + k], device_id=(px, py, c), device_id_type=MESH_ID).wait_recv()
                if t < n_big:
                    cp = pltpu.make_async_remote_copy(
                        src_ref=landed, dst_ref=landed, send_sem=pass_send.at[3 * t + k],
                        recv_sem=pass_recv.at[3 * t + k], device_id=(x, y, 1 - c), device_id_type=MESH_ID)
                    cp.start()
                    cps.append(cp)
        for t in range(n_big):
            for k, (px, py) in enumerate(chips):
                from_sibling = slot(t, 1 - c, 2 * px + py)
                pltpu.make_async_remote_copy(
                    src_ref=from_sibling, dst_ref=from_sibling, send_sem=pass_send.at[3 * t + k],
                    recv_sem=pass_recv.at[3 * t + k], device_id=(x, y, 1 - c), device_id_type=MESH_ID).wait_recv()
        for cp in cps:
            cp.wait_send()
        for loc in locs:
            loc.wait()

    ins = [shards[n] for n in names] + [conv_w, fconv_w]
    out_shape = [jax.ShapeDtypeStruct((DEPTH, BIG[n][0], BIG[n][1]), bf16) for n in names]
    out_shape += [jax.ShapeDtypeStruct((N_CHIPS,) + conv_w.shape, f32), jax.ShapeDtypeStruct((N_CHIPS,) + fconv_w.shape, f32)]
    outs = pl.pallas_call(
        body, name="gather_weights", out_shape=tuple(out_shape), in_specs=[ANY] * n_t, out_specs=tuple([ANY] * n_t),
        scratch_shapes=[pltpu.SemaphoreType.DMA((n_chip,)), pltpu.SemaphoreType.DMA((n_chip,)),
                        pltpu.SemaphoreType.DMA((n_pass,)), pltpu.SemaphoreType.DMA((n_pass,)),
                        pltpu.SemaphoreType.DMA((DEPTH * n_t,))],
    )(*ins)
    return dict(zip(names, outs[:n_big])), outs[-2], outs[-1]


LATE_WEIGHTS = ("w_out", "ffn_w_up", "ffn_w_down")
HBM = pl.BlockSpec(memory_space=pltpu.HBM)
SEM = pl.BlockSpec(memory_space=pltpu.SEMAPHORE)


def _cast_into_full(shard, name, chip_arr):
    K, N, ax = BIG[name]
    k, n = _shard_shape(name)
    T = 64
    nrt = k // T

    def body(p_ref, a_ref, o_ref):
        o_ref[...] = a_ref[...].astype(bf16)

    if ax == 1:
        out_spec = pl.BlockSpec((None, T, n), lambda l, i, p: (l, i, p[0]))
    else:
        out_spec = pl.BlockSpec((None, T, n), lambda l, i, p: (l, p[0] * nrt + i, 0))
    return pl.pallas_call(
        body, name="cast_into_full", out_shape=jax.ShapeDtypeStruct((DEPTH, K, N), bf16),
        grid_spec=pltpu.PrefetchScalarGridSpec(
            num_scalar_prefetch=1, grid=(DEPTH, nrt),
            in_specs=[pl.BlockSpec((None, T, n), lambda l, i, p: (l, i, 0))], out_specs=out_spec),
        compiler_params=_cp(("parallel", "parallel"), 16),
    )(chip_arr, shard)


def _late_copies(refs, send_sems, recv_sems):
    x, y, c = _position()
    me = 2 * x + y
    idx = 0
    for ref, name in zip(refs, LATE_WEIGHTS):
        for l in range(DEPTH):
            for px, py in _other_chips(x, y):
                def copy(p, ref=ref, name=name, l=l, px=px, py=py, idx=idx):
                    part = _chip_slot(ref, name, l, p)
                    return pltpu.make_async_remote_copy(
                        src_ref=part, dst_ref=part, send_sem=send_sems.at[idx], recv_sem=recv_sems.at[idx],
                        device_id=(px, py, c), device_id_type=MESH_ID)
                yield copy(me), copy(2 * px + py)
                idx += 1


N_LATE_COPIES = 3 * DEPTH * len(LATE_WEIGHTS)


def _gather_start(fulls, after):
    n = len(fulls)

    def body(*refs):
        ins = refs[:n]
        send_sems, recv_sems = refs[n + 1:n + 3]
        token = refs[-1]
        for sent, _ in _late_copies(ins, send_sems, recv_sems):
            sent.start()
        token[...] = jnp.zeros_like(token)

    outs = pl.pallas_call(
        body, name="gather_start",
        out_shape=(pltpu.SemaphoreType.DMA((N_LATE_COPIES,)), pltpu.SemaphoreType.DMA((N_LATE_COPIES,)))
        + tuple(pltpu.HBM(f.shape, f.dtype) for f in fulls) + (jax.ShapeDtypeStruct((SUBLANES, LANES), f32),),
        in_specs=(HBM,) * n + (ANY,),
        out_specs=(SEM, SEM) + (HBM,) * n + (pl.BlockSpec(memory_space=pltpu.VMEM),),
        input_output_aliases={t: 2 + t for t in range(n)},
        compiler_params=pltpu.CompilerParams(has_side_effects=pltpu.SideEffectType.DATAFLOW_SIDE_EFFECTING),
    )(*[pltpu.with_memory_space_constraint(f, pltpu.HBM) for f in fulls], after)
    return outs[0], outs[1], outs[2:2 + n], outs[-1]


def _gather_wait(send_sems, recv_sems, fulls, after):
    n = len(fulls)

    def body(*refs):
        ins = refs[:n]
        send_ref, recv_ref = refs[n:n + 2]
        for sent, landed in _late_copies(ins, send_ref, recv_ref):
            sent.wait_send()
            landed.wait_recv()

    return pl.pallas_call(
        body, name="gather_wait", out_shape=tuple(pltpu.HBM(f.shape, f.dtype) for f in fulls),
        in_specs=(HBM,) * n + (SEM, SEM, ANY), out_specs=(HBM,) * n,
        input_output_aliases={t: t for t in range(n)},
        compiler_params=pltpu.CompilerParams(has_side_effects=pltpu.SideEffectType.DATAFLOW_SIDE_EFFECTING),
    )(*fulls, send_sems, recv_sems, after)


def _half(ref, name, c):
    K, N, ax = BIG[name]
    if ax == 1:
        return ref.at[pl.ds(pl.multiple_of(c * (K // 2), 8), K // 2), :]
    return ref.at[:, pl.ds(pl.multiple_of(c * (N // 2), 128), N // 2)]


def _half_shape(name):
    K, N, ax = BIG[name]
    return (K // 2, N) if ax == 1 else (K, N // 2)


def _shard_of_half(ref, name, q):
    K, N, ax = BIG[name]
    if ax == 1:
        sz = N // N_CHIPS
        return ref.at[:, pl.ds(pl.multiple_of(q * sz, 128), sz)]
    sz = K // N_CHIPS
    return ref.at[pl.ds(pl.multiple_of(q * sz, 16), sz), :]


def _shard_half_shape(name):
    K, N, ax = BIG[name]
    return (K // 2, N // N_CHIPS) if ax == 1 else (K // N_CHIPS, N // 2)


def _shard_shape(name):
    K, N, ax = BIG[name]
    return (K, N // N_CHIPS) if ax == 1 else (K // N_CHIPS, N)


def _pair_exchange(tensors):
    n_t = len(tensors)

    def body(*refs):
        ins = refs[:n_t]
        outs = refs[n_t:2 * n_t]
        send_sems, recv_sems = refs[2 * n_t:]
        x, y, c = _position()
        cps = []
        for t, (name, _) in enumerate(tensors):
            cp = pltpu.make_async_remote_copy(
                src_ref=_half(ins[t], name, 1 - c), dst_ref=outs[t], send_sem=send_sems.at[t],
                recv_sem=recv_sems.at[t], device_id=(x, y, 1 - c), device_id_type=MESH_ID)
            cp.start()
            cps.append(cp)
        for cp in cps:
            cp.wait()

    return pl.pallas_call(
        body, name="grad_pair_exchange",
        out_shape=tuple(jax.ShapeDtypeStruct(_half_shape(n), f32) for n, _ in tensors),
        in_specs=[ANY] * n_t, out_specs=tuple([ANY] * n_t),
        scratch_shapes=[pltpu.SemaphoreType.DMA((n_t,)), pltpu.SemaphoreType.DMA((n_t,))],
    )(*[g for _, g in tensors])


def _pair_add(g, rcv, name, c_arr):
    K, N, ax = BIG[name]
    hr, hc = _half_shape(name)
    T = 128
    nrt = hr // T

    def body(c_ref, g_ref, r_ref, o_ref):
        o_ref[...] = (g_ref[...] + r_ref[...]).astype(bf16)

    if ax == 1:
        g_spec = pl.BlockSpec((T, hc), lambda i, c: (c[0] * nrt + i, 0))
    else:
        g_spec = pl.BlockSpec((T, hc), lambda i, c: (i, c[0]))
    plain = pl.BlockSpec((T, hc), lambda i, c: (i, 0))
    return pl.pallas_call(
        body, name="grad_pair_add", out_shape=jax.ShapeDtypeStruct((hr, hc), bf16),
        grid_spec=pltpu.PrefetchScalarGridSpec(num_scalar_prefetch=1, grid=(nrt,), in_specs=[g_spec, plain],
                                               out_specs=plain),
        compiler_params=_cp(("parallel",), 32),
    )(c_arr, g, rcv)


def _chip_exchange(tensors):
    n_t = len(tensors)

    def body(*refs):
        ins = refs[:n_t]
        outs = refs[n_t:2 * n_t]
        send_sems, recv_sems, local_sems = refs[2 * n_t:]
        x, y, c = _position()
        me = 2 * x + y
        chips = _other_chips(x, y)
        locs, cps = [], []
        for t, (name, _) in enumerate(tensors):
            loc = pltpu.make_async_copy(_shard_of_half(ins[t], name, me), outs[t].at[me], local_sems.at[t])
            loc.start()
            locs.append(loc)
            for k, (px, py) in enumerate(chips):
                cp = pltpu.make_async_remote_copy(
                    src_ref=_shard_of_half(ins[t], name, 2 * px + py), dst_ref=outs[t].at[me],
                    send_sem=send_sems.at[3 * t + k], recv_sem=recv_sems.at[3 * t + k],
                    device_id=(px, py, c), device_id_type=MESH_ID)
                cp.start()
                cps.append(cp)
        for t, (name, _) in enumerate(tensors):
            for k, (px, py) in enumerate(chips):
                pltpu.make_async_remote_copy(
                    src_ref=_shard_of_half(ins[t], name, 2 * px + py), dst_ref=outs[t].at[2 * px + py],
                    send_sem=send_sems.at[3 * t + k], recv_sem=recv_sems.at[3 * t + k],
                    device_id=(px, py, c), device_id_type=MESH_ID).wait_recv()
        for cp in cps:
            cp.wait_send()
        for loc in locs:
            loc.wait()

    return pl.pallas_call(
        body, name="grad_chip_exchange",
        out_shape=tuple(jax.ShapeDtypeStruct((N_CHIPS,) + _shard_half_shape(n), g.dtype) for n, g in tensors),
        in_specs=[ANY] * n_t, out_specs=tuple([ANY] * n_t),
        scratch_shapes=[pltpu.SemaphoreType.DMA((3 * n_t,)), pltpu.SemaphoreType.DMA((3 * n_t,)),
                        pltpu.SemaphoreType.DMA((n_t,))],
    )(*[g for _, g in tensors])


def _sum_chips(parts):
    _, R, C = parts.shape
    T = 64

    def body(p_ref, o_ref):
        o_ref[...] = ((p_ref[0].astype(f32) + p_ref[1].astype(f32)) + p_ref[2].astype(f32)) + p_ref[3].astype(f32)

    return pl.pallas_call(
        body, grid=(R // T,), name="grad_sum_chips", out_shape=jax.ShapeDtypeStruct((R, C), f32),
        in_specs=[pl.BlockSpec((N_CHIPS, T, C), lambda i: (0, i, 0))], out_specs=pl.BlockSpec((T, C), lambda i: (i, 0)),
        compiler_params=_cp(("parallel",), 32),
    )(parts)


def _pair_swap(halves):
    n_t = len(halves)

    def body(*refs):
        ins = refs[:n_t]
        outs = refs[n_t:2 * n_t]
        send_sems, recv_sems = refs[2 * n_t:]
        x, y, c = _position()
        cps = []
        for t in range(n_t):
            cp = pltpu.make_async_remote_copy(
                src_ref=ins[t], dst_ref=outs[t], send_sem=send_sems.at[t], recv_sem=recv_sems.at[t],
                device_id=(x, y, 1 - c), device_id_type=MESH_ID)
            cp.start()
            cps.append(cp)
        for cp in cps:
            cp.wait()

    return pl.pallas_call(
        body, name="grad_pair_swap", out_shape=tuple(jax.ShapeDtypeStruct(h.shape, h.dtype) for h in halves),
        in_specs=[ANY] * n_t, out_specs=tuple([ANY] * n_t),
        scratch_shapes=[pltpu.SemaphoreType.DMA((n_t,)), pltpu.SemaphoreType.DMA((n_t,))],
    )(*halves)


def _adamw_halves(own, other, w, m, v, name, l, c_arr, prev):
    K, N, ax = BIG[name]
    R, C = _shard_shape(name)
    hr, hc = _shard_half_shape(name)
    T = 64
    nrt = hr // T
    c1 = 1.0 / (1.0 - ADAM_B1 ** ADAM_STEP)
    c2 = 1.0 / (1.0 - ADAM_B2 ** ADAM_STEP)

    def body(c_ref, own_ref, oth_ref, w_ref, m_ref, v_ref, *rest):
        g_ref, d_ref, nm_ref, nv_ref = rest[-4:]
        gg = jnp.where(pl.program_id(0) == c_ref[0], own_ref[...], oth_ref[...])
        nm = ADAM_B1 * m_ref[...] + (1.0 - ADAM_B1) * gg
        nv = ADAM_B2 * v_ref[...] + (1.0 - ADAM_B2) * (gg * gg)
        g_ref[...] = gg
        nm_ref[...] = nm
        nv_ref[...] = nv
        d_ref[...] = -ADAM_LR * ((nm * c1) / (jnp.sqrt(nv * c2) + ADAM_EPS) + ADAM_WD * w_ref[...])

    half = pl.BlockSpec((T, hc), lambda h, i, c: (i, 0))
    if ax == 1:
        full = pl.BlockSpec((None, T, hc), lambda h, i, c: (l, h * nrt + i, 0))
    else:
        full = pl.BlockSpec((None, T, hc), lambda h, i, c: (l, i, h))
    sd = jax.ShapeDtypeStruct((DEPTH, R, C), f32)
    args = [c_arr, own, other, w, m, v]
    in_specs = [half, half, full, full, full]
    aliases = {}
    if prev is not None:
        args += list(prev)
        in_specs += [ANY] * 4
        aliases = {6 + k: k for k in range(4)}
    return pl.pallas_call(
        body, name="adamw_" + name, out_shape=(sd, sd, sd, sd),
        grid_spec=pltpu.PrefetchScalarGridSpec(num_scalar_prefetch=1, grid=(2, nrt), in_specs=in_specs,
                                               out_specs=(full, full, full, full)),
        input_output_aliases=aliases,
        compiler_params=_cp(("arbitrary", "arbitrary"), 32),
    )(*args)


def _reduce_big_grads(grads):
    c_arr = jnp.reshape(lax.axis_index("c"), (1,)).astype(jnp.int32)
    tensors = [(n, grads[n][l]) for n in BIG for l in range(DEPTH)]
    received = _pair_exchange(tensors)
    pair = [(n, _pair_add(g, r, n, c_arr)) for (n, g), r in zip(tensors, received)]
    parts = _chip_exchange(pair)
    own = [_sum_chips(p) for p in parts]
    other = _pair_swap(own)
    keys = [(n, l) for n in BIG for l in range(DEPTH)]
    return dict(zip(keys, zip(own, other))), c_arr


def _small_allreduce(buf):
    R = buf.shape[0]
    n_dev = 8

    def body(in_ref, out_ref, slots, send_sems, recv_sems):
        x, y, c = _position()
        me = 4 * x + 2 * y + c
        slots[me] = in_ref[...]
        peers = []
        for k in range(1, n_dev):
            px = 1 - x if k & 4 else x
            py = 1 - y if k & 2 else y
            pc = 1 - c if k & 1 else c
            peers.append((px, py, pc))
        cps = []
        for k, peer in enumerate(peers):
            cp = pltpu.make_async_remote_copy(
                src_ref=in_ref, dst_ref=slots.at[me], send_sem=send_sems.at[k], recv_sem=recv_sems.at[k],
                device_id=peer, device_id_type=MESH_ID)
            cp.start()
            cps.append(cp)
        for k, (px, py, pc) in enumerate(peers):
            pltpu.make_async_remote_copy(
                src_ref=in_ref, dst_ref=slots.at[4 * px + 2 * py + pc], send_sem=send_sems.at[k],
                recv_sem=recv_sems.at[k], device_id=(px, py, pc), device_id_type=MESH_ID).wait_recv()
        for cp in cps:
            cp.wait_send()
        acc = slots[0]
        for dv in range(1, n_dev):
            acc = acc + slots[dv]
        out_ref[...] = acc

    vm = pl.BlockSpec(memory_space=pltpu.VMEM)
    return pl.pallas_call(
        body, name="small_allreduce", out_shape=jax.ShapeDtypeStruct((R, 128), f32), in_specs=[vm], out_specs=vm,
        scratch_shapes=[pltpu.VMEM((n_dev, R, 128), f32), pltpu.SemaphoreType.DMA((n_dev - 1,)),
                        pltpu.SemaphoreType.DMA((n_dev - 1,))],
        compiler_params=pltpu.CompilerParams(vmem_limit_bytes=40 * MIB),
    )(buf)


PACK_UNIT = 1024


def _pack(arrs):
    parts = []
    for a in arrs:
        flat = a.reshape(-1)
        n = -(-flat.shape[0] // PACK_UNIT) * PACK_UNIT
        parts.append(jnp.pad(flat, (0, n - flat.shape[0])))
    return jnp.concatenate(parts).reshape(-1, 128)


def _unpack(buf, shapes):
    flat = buf.reshape(-1)
    out, off = [], 0
    for shp in shapes:
        n = int(np.prod(shp))
        out.append(flat[off:off + n].reshape(shp))
        off += -(-n // PACK_UNIT) * PACK_UNIT
    return out


def kernel(x, w_in, b_in, conv_dw_w, conv_dw_b, conv_ln_g, conv_ln_b, rel_bias_table, gmlp_ln_g, gmlp_ln_b, gmlp_w_s, gmlp_b_s, w_out, b_out, ln1_g, ln1_b, ffn_w_up, ffn_b_up, ffn_conv_w, ffn_conv_b, ffn_w_down, ffn_b_down, ln2_g, ln2_b, loss_target, m_w_in, m_b_in, m_conv_dw_w, m_conv_dw_b, m_conv_ln_g, m_conv_ln_b, m_rel_bias_table, m_gmlp_ln_g, m_gmlp_ln_b, m_gmlp_w_s, m_gmlp_b_s, m_w_out, m_b_out, m_ln1_g, m_ln1_b, m_ffn_w_up, m_ffn_b_up, m_ffn_conv_w, m_ffn_conv_b, m_ffn_w_down, m_ffn_b_down, m_ln2_g, m_ln2_b, v_w_in, v_b_in, v_conv_dw_w, v_conv_dw_b, v_conv_ln_g, v_conv_ln_b, v_rel_bias_table, v_gmlp_ln_g, v_gmlp_ln_b, v_gmlp_w_s, v_gmlp_b_s, v_w_out, v_b_out, v_ln1_g, v_ln1_b, v_ffn_w_up, v_ffn_b_up, v_ffn_conv_w, v_ffn_conv_b, v_ffn_w_down, v_ffn_b_down, v_ln2_g, v_ln2_b):
    w = dict(w_in=w_in, b_in=b_in, conv_dw_w=conv_dw_w, conv_dw_b=conv_dw_b, conv_ln_g=conv_ln_g, conv_ln_b=conv_ln_b,
             rel_bias_table=rel_bias_table, gmlp_ln_g=gmlp_ln_g, gmlp_ln_b=gmlp_ln_b, gmlp_w_s=gmlp_w_s,
             gmlp_b_s=gmlp_b_s, w_out=w_out, b_out=b_out, ln1_g=ln1_g, ln1_b=ln1_b, ffn_w_up=ffn_w_up,
             ffn_b_up=ffn_b_up, ffn_conv_w=ffn_conv_w, ffn_conv_b=ffn_conv_b, ffn_w_down=ffn_w_down,
             ffn_b_down=ffn_b_down, ln2_g=ln2_g, ln2_b=ln2_b)
    m = dict(w_in=m_w_in, b_in=m_b_in, conv_dw_w=m_conv_dw_w, conv_dw_b=m_conv_dw_b, conv_ln_g=m_conv_ln_g,
             conv_ln_b=m_conv_ln_b, rel_bias_table=m_rel_bias_table, gmlp_ln_g=m_gmlp_ln_g, gmlp_ln_b=m_gmlp_ln_b,
             gmlp_w_s=m_gmlp_w_s, gmlp_b_s=m_gmlp_b_s, w_out=m_w_out, b_out=m_b_out, ln1_g=m_ln1_g, ln1_b=m_ln1_b,
             ffn_w_up=m_ffn_w_up, ffn_b_up=m_ffn_b_up, ffn_conv_w=m_ffn_conv_w, ffn_conv_b=m_ffn_conv_b,
             ffn_w_down=m_ffn_w_down, ffn_b_down=m_ffn_b_down, ln2_g=m_ln2_g, ln2_b=m_ln2_b)
    v = dict(w_in=v_w_in, b_in=v_b_in, conv_dw_w=v_conv_dw_w, conv_dw_b=v_conv_dw_b, conv_ln_g=v_conv_ln_g,
             conv_ln_b=v_conv_ln_b, rel_bias_table=v_rel_bias_table, gmlp_ln_g=v_gmlp_ln_g, gmlp_ln_b=v_gmlp_ln_b,
             gmlp_w_s=v_gmlp_w_s, gmlp_b_s=v_gmlp_b_s, w_out=v_w_out, b_out=v_b_out, ln1_g=v_ln1_g, ln1_b=v_ln1_b,
             ffn_w_up=v_ffn_w_up, ffn_b_up=v_ffn_b_up, ffn_conv_w=v_ffn_conv_w, ffn_conv_b=v_ffn_conv_b,
             ffn_w_down=v_ffn_w_down, ffn_b_down=v_ffn_b_down, ln2_g=v_ln2_g, ln2_b=v_ln2_b)

    chip_arr = jnp.reshape(2 * lax.axis_index("x") + lax.axis_index("y"), (1,)).astype(jnp.int32)
    shards = {"w_in": _cast_bf16(w_in.reshape(-1, w_in.shape[-1])).reshape(w_in.shape)}
    wb, conv_stack, fconv_stack = _gather_weights(shards, conv_dw_w, ffn_conv_w)
    send_sems, recv_sems, in_flight, token = _gather_start(
        [_cast_into_full(w[n], n, chip_arr) for n in LATE_WEIGHTS], conv_stack)
    sp = {n: w[n] for n in SMALL}
    sp["conv_dw_w"] = jnp.moveaxis(conv_stack, 0, 2).reshape(DEPTH, CONV_WIDTH, CONV_CH)
    sp["ffn_conv_w"] = jnp.moveaxis(fconv_stack, 0, 2).reshape(DEPTH, FFN_CONV_WIDTH, 2 * D_FF)
    sp["b_in"] = sp["b_in"] + token[0, 0]

    def late_weights(after):
        return dict(zip(LATE_WEIGHTS, _gather_wait(send_sems, recv_sems, in_flight, after)))

    loss_local, grad_x, grads = _local_step(x[0], loss_target[0], wb, late_weights, sp)
    loss = lax.psum(loss_local, ("x", "y", "c"))

    big, c_arr = _reduce_big_grads(grads)
    small_shapes = [grads[n].shape for n in SMALL]
    small = dict(zip(SMALL, _unpack(_small_allreduce(_pack([grads[n] for n in SMALL])), small_shapes)))
    chip = 2 * lax.axis_index("x") + lax.axis_index("y")
    for n in SMALL_SHARDED:
        width = w[n].shape[-1]
        small[n] = lax.dynamic_slice_in_dim(small[n], chip * width, width, axis=2)

    g_out, d_out, m_out, v_out = {}, {}, {}, {}
    for n in BIG:
        outs = None
        for l in range(DEPTH):
            own, other = big[(n, l)]
            outs = _adamw_halves(own, other, w[n], m[n], v[n], n, l, c_arr, outs)
        g_out[n], d_out[n], m_out[n], v_out[n] = outs
    shapes = [small[n].shape for n in SMALL]
    packed = [_pack([src[n] for n in SMALL]) for src in (small, w, m, v)]
    upd = _adamw(*packed, "adamw_small")
    for dst, buf in zip((d_out, m_out, v_out), upd):
        dst.update(zip(SMALL, _unpack(buf, shapes)))
    g_out.update(small)

    return (loss, grad_x[None], *[g_out[n] for n in WEIGHTS], *[d_out[n] for n in WEIGHTS],
            *[m_out[n] for n in WEIGHTS], *[v_out[n] for n in WEIGHTS])
```

```python
import functools
import math

import numpy as np
import jax
import jax.numpy as jnp
from jax import lax
from jax.experimental import pallas as pl
from jax.experimental.pallas import tpu as pltpu

f32 = jnp.float32
bf16 = jnp.bfloat16

D_MODEL = 1024
DEPTH = 2
HEAD_DIM = 64
CONV_CH = 256
CONV_WIDTH = 31
ATTN_HEADS = 8
ATTN_CH = ATTN_HEADS * HEAD_DIM
DILATIONS = (1, 4, 16)
ATTN_BLOCK = 128
N_BUCKETS = 32
MAX_DISTANCE = 2048
GMLP_CH = 256
GMLP_GROUPS = 4
GMLP_GROUP_DIM = GMLP_CH // GMLP_GROUPS
CHUNK = 128
IN_CH = 2 * CONV_CH + 3 * ATTN_CH + 2 * GMLP_CH
D_FF = 2816
FFN_CONV_WIDTH = 3
LN_EPS = 1e-5
ALPHA = (2.0 * DEPTH) ** 0.25
ADAM_LR = 0.001
ADAM_B1 = 0.9
ADAM_B2 = 0.999
ADAM_EPS = 1e-08
ADAM_WD = 0.01
ADAM_STEP = 10

CONV_HALO = 32
FFN_HALO = 8
NEG = -1e30
MIB = 2 ** 20
NT_DIMS = (((1,), (1,)), ((), ()))
TN_DIMS = (((0,), (0,)), ((), ()))
MESH_ID = pl.DeviceIdType.MESH


def _cp(sem, vmem_mib):
    return pltpu.CompilerParams(dimension_semantics=sem, vmem_limit_bytes=vmem_mib * MIB)


def _resident(shape):
    nd = len(shape)
    return pl.BlockSpec(shape, lambda *_: (0,) * nd, pipeline_mode=pl.Buffered(1))


def _acc(shape):
    nd = len(shape)
    return pl.BlockSpec(shape, lambda *_: (0,) * nd)


def _sig(x):
    return 1.0 / (1.0 + jnp.exp(-x))


def _ln_stats(z):
    mu = jnp.mean(z, axis=-1, keepdims=True)
    zc = z - mu
    var = jnp.mean(zc * zc, axis=-1, keepdims=True)
    rstd = lax.rsqrt(var + LN_EPS)
    return zc * rstd, rstd


def _ln_bwd(dy, xhat, rstd, g):
    dxh = dy * g
    m1 = jnp.mean(dxh, axis=-1, keepdims=True)
    m2 = jnp.mean(dxh * xhat, axis=-1, keepdims=True)
    return rstd * (dxh - m1 - xhat * m2)


def _colsum(x):
    return jnp.sum(x, axis=0, keepdims=True)


def _t5_bucket_np(dist):
    max_exact = N_BUCKETS // 2
    dd = np.maximum(dist, 1).astype(np.float64)
    large = max_exact + (np.log(dd / max_exact) / math.log(MAX_DISTANCE / max_exact)
                         * (N_BUCKETS - max_exact)).astype(np.int32)
    large = np.minimum(large, N_BUCKETS - 1)
    return np.where(dist < max_exact, dist, large).astype(np.int32)


def _bucket_ids():
    qi = np.arange(ATTN_BLOCK)[:, None]
    kj = np.arange(2 * ATTN_BLOCK)[None, :]
    dist = np.clip(qi + ATTN_BLOCK - kj, 0, None)
    return np.stack([_t5_bucket_np(dist * d) for d in DILATIONS]).astype(np.int32)


LANES = 128
QKV_CH = 3 * ATTN_CH
PERM_TILE = 512


def _slabs(n, rows):
    return [pltpu.VMEM((rows, LANES), f32)] * n


def _rows_of(slab, r, n, d):
    return slab[...] if d == 1 else slab[pl.ds(r, n, stride=d), :]


def _set_rows_of(slab, r, n, d, val):
    if d == 1:
        slab[...] = val
    else:
        slab[pl.ds(r, n, stride=d), :] = val


def _perm_spec(d, ch):
    return pl.BlockSpec((d, PERM_TILE // d, ch), lambda i: (0, i, 0))


def _perm_shape(S, d, ch, dtype):
    return jax.ShapeDtypeStruct((d, S // d, ch), dtype)


def _inproj_fwd(x, w, b):
    S = x.shape[0]
    T = PERM_TILE
    nsl = QKV_CH // LANES

    def body(x_ref, w_ref, b_ref, a_ref, c_ref, *rest):
        q_refs = rest[:len(DILATIONS)]
        slabs = rest[len(DILATIONS):]
        h = jnp.dot(x_ref[...].astype(bf16), w_ref[...], preferred_element_type=f32) + b_ref[...]
        a_ref[...] = h[:, :2 * CONV_CH]
        q0 = 2 * CONV_CH
        c_ref[...] = h[:, q0 + QKV_CH:]
        for j in range(nsl):
            piece = h[:, q0 + LANES * j:q0 + LANES * (j + 1)]
            if LANES * j < ATTN_CH:
                piece = piece * (HEAD_DIM ** -0.5)
            slabs[j][...] = piece
        for d, q_ref in zip(DILATIONS, q_refs):
            for r in range(d):
                for j in range(nsl):
                    q_ref[r, :, LANES * j:LANES * (j + 1)] = _rows_of(slabs[j], r, T // d, d).astype(bf16)

    row = lambda c: pl.BlockSpec((T, c), lambda i: (i, 0))
    return pl.pallas_call(
        body, grid=(S // T,), name="inproj_fwd",
        out_shape=(jax.ShapeDtypeStruct((S, 2 * CONV_CH), f32), jax.ShapeDtypeStruct((S, 2 * GMLP_CH), f32))
        + tuple(_perm_shape(S, d, QKV_CH, bf16) for d in DILATIONS),
        in_specs=[row(D_MODEL), _resident((D_MODEL, IN_CH)), _resident((1, IN_CH))],
        out_specs=(row(2 * CONV_CH), row(2 * GMLP_CH)) + tuple(_perm_spec(d, QKV_CH) for d in DILATIONS),
        scratch_shapes=_slabs(nsl, T),
        compiler_params=_cp(("parallel",), 48),
    )(x, w, b)


CONV_GROUP = 64


def _window_rolls(starts):
    groups = {}
    for s in starts:
        groups.setdefault((-s) % SUBLANES, []).append(s)
    return dict(sorted(groups.items()))


def _conv_fwd(a_in, dw_w, dw_b, ln_g, ln_b):
    S = a_in.shape[0]
    T = 512
    hb = T // CONV_HALO

    def body(a_ref, halo_ref, w_ref, b_ref, g_ref, be_ref, out_ref, hc_ref, buf):
        i = pl.program_id(0)
        am = a_ref[...]
        ah = halo_ref[...]
        hgh = ah[:, :CONV_CH] * _sig(ah[:, CONV_CH:])
        buf[0:CONV_HALO, :] = jnp.where(i > 0, hgh, 0.0)
        buf[CONV_HALO:, :] = am[:, :CONV_CH] * _sig(am[:, CONV_CH:])
        starts = _window_rolls(range(CONV_HALO - (CONV_WIDTH - 1), CONV_HALO + 1))
        slabs = [slice(LANES * j, LANES * (j + 1)) for j in range(CONV_CH // LANES)]

        def step(g, _):
            r0 = pl.multiple_of(g * CONV_GROUP, CONV_GROUP)
            rows = pl.ds(r0, CONV_GROUP)
            for cs in slabs:
                ext = buf[pl.ds(r0, CONV_GROUP + CONV_HALO), cs]
                acc = jnp.broadcast_to(b_ref[:, cs], (CONV_GROUP, LANES))
                for b, ss in starts.items():
                    rolled = ext if b == 0 else pltpu.roll(ext, b, 0)
                    for s in ss:
                        k = s - (CONV_HALO - (CONV_WIDTH - 1))
                        acc = acc + w_ref[k:k + 1, cs] * rolled[s + b:s + b + CONV_GROUP]
                hc_ref[rows, cs] = acc
            return 0

        lax.fori_loop(0, T // CONV_GROUP, step, 0)
        xhat, _ = _ln_stats(hc_ref[...])
        y = xhat * g_ref[...] + be_ref[...]
        out_ref[...] = (y * _sig(y)).astype(bf16)

    return pl.pallas_call(
        body, grid=(S // T,), name="conv_fwd",
        out_shape=(jax.ShapeDtypeStruct((S, CONV_CH), bf16), jax.ShapeDtypeStruct((S, CONV_CH), f32)),
        in_specs=[pl.BlockSpec((T, 2 * CONV_CH), lambda i: (i, 0)),
                  pl.BlockSpec((CONV_HALO, 2 * CONV_CH), lambda i: (jnp.maximum(i * hb - 1, 0), 0)),
                  _acc((32, CONV_CH)), _acc((1, CONV_CH)), _acc((1, CONV_CH)), _acc((1, CONV_CH))],
        out_specs=(pl.BlockSpec((T, CONV_CH), lambda i: (i, 0)), pl.BlockSpec((T, CONV_CH), lambda i: (i, 0))),
        scratch_shapes=[pltpu.VMEM((T + CONV_HALO, CONV_CH), f32)],
        compiler_params=_cp(("parallel",), 32),
    )(a_in, a_in, dw_w, dw_b, ln_g, ln_b)


def _bias_build(table, buckets):
    def body(t_ref, bk_ref, o_ref):
        h = pl.program_id(1)
        ids = bk_ref[0]
        acc = jnp.zeros((ATTN_BLOCK, 2 * ATTN_BLOCK), f32)
        for b in range(N_BUCKETS):
            acc = jnp.where(ids == b, t_ref[b, h], acc)
        o_ref[0, 0] = acc

    return pl.pallas_call(
        body, grid=(len(DILATIONS), ATTN_HEADS), name="bias_build",
        out_shape=jax.ShapeDtypeStruct((len(DILATIONS), ATTN_HEADS, ATTN_BLOCK, 2 * ATTN_BLOCK), f32),
        in_specs=[pl.BlockSpec(memory_space=pltpu.SMEM),
                  pl.BlockSpec((1, ATTN_BLOCK, 2 * ATTN_BLOCK), lambda p, h: (p, 0, 0))],
        out_specs=pl.BlockSpec((1, 1, ATTN_BLOCK, 2 * ATTN_BLOCK), lambda p, h: (p, h, 0, 0)),
        compiler_params=_cp(("arbitrary", "arbitrary"), 16),
    )(table, buckets)


def _head_tile(tile, h, col):
    lane_head = lax.broadcasted_iota(jnp.int32, tile.shape, 1) // 16
    return jnp.where(lane_head == h, col, tile)


HEAD_PAIRS = ATTN_HEADS // 2
UNITS_PER_BLOCK = ATTN_HEADS


def _attn_tile(L):
    return min(512, L)


def _band_mask(first_block, n):
    B = ATTN_BLOCK
    row = lax.broadcasted_iota(jnp.int32, (B, 2 * B), 0)
    col = lax.broadcasted_iota(jnp.int32, (B, 2 * B), 1)
    valid = (col >= row) & (col <= row + B)
    if first_block:
        valid = valid & ((col >= B) | (n > 0))
    return valid


def _head_lanes(a):
    lane = lax.broadcasted_iota(jnp.int32, (ATTN_BLOCK, LANES), 1)
    return (lane < HEAD_DIM) if a == 0 else (lane >= HEAD_DIM)


def _pair_keys(cur_ref, halo_ref, part, b, j):
    B = ATTN_BLOCK
    c0 = part * ATTN_CH + LANES * j
    own = cur_ref[B * b:B * (b + 1), c0:c0 + LANES]
    prev = halo_ref[:, LANES * j:LANES * (j + 1)] if b == 0 else cur_ref[B * (b - 1):B * b, c0:c0 + LANES]
    return jnp.concatenate([prev, own], axis=0)


def _attn_fwd_pattern(qkv, bias, d):
    _, L, _ = qkv.shape
    B = ATTN_BLOCK
    QB = _attn_tile(L)
    nsb = QB // B
    U = nsb * UNITS_PER_BLOCK

    def body(cur_ref, hk_ref, hv_ref, b_ref, o_ref, lse_ref, lg, pb):
        n = pl.program_id(1)
        for b in range(nsb):
            valid = _band_mask(b == 0, n)
            for j in range(HEAD_PAIRS):
                q2 = cur_ref[B * b:B * (b + 1), LANES * j:LANES * (j + 1)]
                k2 = _pair_keys(cur_ref, hk_ref, 1, b, j)
                for a in range(2):
                    u = (b * HEAD_PAIRS + j) * 2 + a
                    qm = jnp.where(_head_lanes(a), q2, jnp.zeros_like(q2))
                    logits = lax.dot_general(qm, k2, NT_DIMS, preferred_element_type=f32) + b_ref[2 * j + a]
                    lg[B * u:B * (u + 1), :] = jnp.where(valid, logits, NEG)
        m = jnp.max(lg[...], axis=1, keepdims=True)
        p = jnp.exp(lg[...] - m)
        s = jnp.sum(p, axis=1, keepdims=True)
        pb[...] = p.astype(bf16)
        lse = m + jnp.log(s)
        inv = 1.0 / s
        for b in range(nsb):
            tile = jnp.zeros((B, B), f32)
            for j in range(HEAD_PAIRS):
                v2 = _pair_keys(cur_ref, hv_ref, 2, b, j)
                outs = []
                for a in range(2):
                    u = (b * HEAD_PAIRS + j) * 2 + a
                    rows = slice(B * u, B * (u + 1))
                    outs.append(jnp.dot(pb[rows, :], v2, preferred_element_type=f32) * inv[rows])
                    tile = _head_tile(tile, 2 * j + a, lse[rows])
                o_ref[B * b:B * (b + 1), LANES * j:LANES * (j + 1)] = jnp.where(_head_lanes(0), outs[0], outs[1])
            lse_ref[B * b:B * (b + 1), :] = tile

    halo = lambda part: pl.BlockSpec((None, B, ATTN_CH), lambda r, n: (r, jnp.maximum(n * nsb - 1, 0), part))
    tile_spec = lambda c: pl.BlockSpec((None, QB, c), lambda r, n: (r, n, 0))
    return pl.pallas_call(
        body, grid=(d, L // QB), name=f"attn_fwd_d{d}",
        out_shape=(jax.ShapeDtypeStruct((d, L, ATTN_CH), f32), jax.ShapeDtypeStruct((d, L, B), f32)),
        in_specs=[tile_spec(QKV_CH), halo(1), halo(2), _resident((ATTN_HEADS, B, 2 * B))],
        out_specs=(tile_spec(ATTN_CH), tile_spec(B)),
        scratch_shapes=[pltpu.VMEM((U * B, 2 * B), f32), pltpu.VMEM((U * B, 2 * B), bf16)],
        compiler_params=_cp(("parallel", "parallel"), 40),
    )(qkv, qkv, qkv, bias)


def _attn_merge(parts):
    S = parts[0][0].shape[0] * parts[0][0].shape[1]
    T = PERM_TILE
    nsl = ATTN_CH // LANES
    n_p = len(DILATIONS)

    def body(*refs):
        ins = refs[:2 * n_p]
        out_ref, lse_ref = refs[2 * n_p:2 * n_p + 2]
        slabs = refs[2 * n_p + 2:]
        lses = []
        for p, d in enumerate(DILATIONS):
            o_ref, l_ref = ins[2 * p], ins[2 * p + 1]
            osl = slabs[p * (nsl + 1):p * (nsl + 1) + nsl]
            lsl = slabs[p * (nsl + 1) + nsl]
            for r in range(d):
                for j in range(nsl):
                    _set_rows_of(osl[j], r, T // d, d, o_ref[r, :, LANES * j:LANES * (j + 1)])
                _set_rows_of(lsl, r, T // d, d, l_ref[r])
            lses.append(lsl[...])
        big = functools.reduce(jnp.maximum, lses)
        ws = [jnp.exp(l - big) for l in lses]
        tot = functools.reduce(lambda a_, b_: a_ + b_, ws)
        lse_ref[...] = big + jnp.log(tot)
        ws = [w / tot for w in ws]
        for j in range(nsl):
            acc = jnp.zeros((T, LANES), f32)
            for p in range(n_p):
                wa = ws[p][:, 32 * j:32 * j + 1]
                wb = ws[p][:, 32 * j + 16:32 * j + 17]
                lane = lax.broadcasted_iota(jnp.int32, (T, LANES), 1)
                acc = acc + jnp.where(lane < HEAD_DIM, wa, wb) * slabs[p * (nsl + 1) + j][...]
            out_ref[:, LANES * j:LANES * (j + 1)] = acc.astype(bf16)

    in_specs, args = [], []
    for (o, l), d in zip(parts, DILATIONS):
        in_specs += [_perm_spec(d, ATTN_CH), _perm_spec(d, ATTN_BLOCK)]
        args += [o, l]
    row = lambda c: pl.BlockSpec((T, c), lambda i: (i, 0))
    return pl.pallas_call(
        body, grid=(S // T,), name="attn_merge",
        out_shape=(jax.ShapeDtypeStruct((S, ATTN_CH), bf16), jax.ShapeDtypeStruct((S, ATTN_BLOCK), f32)),
        in_specs=in_specs, out_specs=(row(ATTN_CH), row(ATTN_BLOCK)),
        scratch_shapes=_slabs(n_p * (nsl + 1), T),
        compiler_params=_cp(("parallel",), 40),
    )(*args)


def _attn_fwd(qkvs, bias):
    parts = [_attn_fwd_pattern(q, bias[p], d) for p, (q, d) in enumerate(zip(qkvs, DILATIONS))]
    return _attn_merge(parts)


def _tril_bf16(w):
    row = lax.broadcasted_iota(jnp.int32, (CHUNK, CHUNK), 0)
    col = lax.broadcasted_iota(jnp.int32, (CHUNK, CHUNK), 1)
    return jnp.where(col <= row, w, 0.0).astype(bf16)


def _gmlp_fwd(c_in, ln_g, ln_b, w_s, b_s_t):
    S = c_in.shape[0]
    T = 512

    def body(c_ref, g_ref, be_ref, w_ref, bs_ref, out_ref, mix):
        c = c_ref[...]
        xhat, _ = _ln_stats(c[:, GMLP_CH:])
        vb = (xhat * g_ref[...] + be_ref[...]).astype(bf16)
        for g in range(GMLP_GROUPS):
            wt = _tril_bf16(w_ref[g])
            cs = slice(GMLP_GROUP_DIM * g, GMLP_GROUP_DIM * (g + 1))
            for ci in range(T // CHUNK):
                rs = slice(CHUNK * ci, CHUNK * (ci + 1))
                mix[rs, cs] = jnp.dot(wt, vb[rs, cs], preferred_element_type=f32) + bs_ref[:, g:g + 1]
        out_ref[...] = (c[:, :GMLP_CH] * mix[...]).astype(bf16)

    return pl.pallas_call(
        body, grid=(S // T,), name="gmlp_fwd",
        out_shape=jax.ShapeDtypeStruct((S, GMLP_CH), bf16),
        in_specs=[pl.BlockSpec((T, 2 * GMLP_CH), lambda i: (i, 0)), _acc((1, GMLP_CH)), _acc((1, GMLP_CH)),
                  _acc((GMLP_GROUPS, CHUNK, CHUNK)), _acc((CHUNK, GMLP_GROUPS))],
        out_specs=pl.BlockSpec((T, GMLP_CH), lambda i: (i, 0)),
        scratch_shapes=[pltpu.VMEM((T, GMLP_CH), f32)],
        compiler_params=_cp(("parallel",), 32),
    )(c_in, ln_g, ln_b, w_s, b_s_t)


def _outproj_ln_fwd(conv_out, attn_out, gm_out, w, b, x, ln_g, ln_b):
    S = x.shape[0]
    T = 512

    def body(co_ref, ao_ref, go_ref, w_ref, b_ref, x_ref, g_ref, be_ref, cat_ref, z_ref, y_ref, yb_ref):
        cat = jnp.concatenate([co_ref[...], ao_ref[...], go_ref[...]], axis=1)
        cat_ref[...] = cat
        z = jnp.dot(cat, w_ref[...], preferred_element_type=f32) + b_ref[...] + ALPHA * x_ref[...]
        z_ref[...] = z
        xhat, _ = _ln_stats(z)
        y = xhat * g_ref[...] + be_ref[...]
        y_ref[...] = y
        yb_ref[...] = y.astype(bf16)

    row = lambda c: pl.BlockSpec((T, c), lambda i: (i, 0))
    return pl.pallas_call(
        body, grid=(S // T,), name="outproj_ln_fwd",
        out_shape=(jax.ShapeDtypeStruct((S, D_MODEL), bf16), jax.ShapeDtypeStruct((S, D_MODEL), f32),
                   jax.ShapeDtypeStruct((S, D_MODEL), f32), jax.ShapeDtypeStruct((S, D_MODEL), bf16)),
        in_specs=[row(CONV_CH), row(ATTN_CH), row(GMLP_CH), _resident((D_MODEL, D_MODEL)), _acc((1, D_MODEL)),
                  row(D_MODEL), _acc((1, D_MODEL)), _acc((1, D_MODEL))],
        out_specs=(row(D_MODEL), row(D_MODEL), row(D_MODEL), row(D_MODEL)),
        compiler_params=_cp(("parallel",), 40),
    )(conv_out, attn_out, gm_out, w, b, x, ln_g, ln_b)


GATE_ROWS = 32
GATE_COLS = 128
GATE_MM_COLS = 256
SUBLANES = 8


def _gate_cols(c0):
    return slice(c0, c0 + GATE_COLS), slice(D_FF + c0, D_FF + c0 + GATE_COLS)


def _bcast_rows(ref, k, cs):
    return jnp.broadcast_to(ref[k:k + 1, cs], (GATE_ROWS, GATE_COLS))


def _fold_rows(z):
    acc = z[0:SUBLANES]
    for r in range(SUBLANES, GATE_ROWS, SUBLANES):
        acc = acc + z[r:r + SUBLANES]
    return acc


def _ffn_up_gate_fwd(x1b, w, b, conv_w, conv_b):
    S = x1b.shape[0]
    T = 256
    H = FFN_HALO
    K = FFN_CONV_WIDTH

    def body(x_ref, w_ref, b_ref, cw_ref, cb_ref, hfb_ref, hc_ref, act_ref, hbuf, carry):
        @pl.when(pl.program_id(0) == 0)
        def _():
            carry[...] = jnp.zeros_like(carry)
        x = x_ref[...]
        for m0 in range(0, D_FF, GATE_MM_COLS):
            for cm in (slice(m0, m0 + GATE_MM_COLS), slice(D_FF + m0, D_FF + m0 + GATE_MM_COLS)):
                h = jnp.dot(x, w_ref[:, cm], preferred_element_type=f32) + b_ref[:, cm]
                hbuf[:, cm] = h
                hfb_ref[:, cm] = h.astype(bf16)
            for c0 in range(m0, m0 + GATE_MM_COLS, GATE_COLS):
                cols = _gate_cols(c0)
                wts = [[_bcast_rows(cw_ref, k, cs) for k in range(K)] + [_bcast_rows(cb_ref, 0, cs)] for cs in cols]

                def step(rg, tails, cols=cols, wts=wts):
                    rows = pl.ds(pl.multiple_of(rg * GATE_ROWS, GATE_ROWS), GATE_ROWS)
                    hc, new_tails = [], []
                    for cs, wt, tail in zip(cols, wts, tails):
                        h = hbuf[rows, cs]
                        ext = jnp.concatenate([tail, h], axis=0)
                        acc = wt[K] + wt[K - 1] * h
                        for back in range(1, K):
                            acc = acc + wt[K - 1 - back] * pltpu.roll(ext, back, 0)[H:]
                        hc_ref[rows, cs] = acc
                        hc.append(acc)
                        new_tails.append(h[GATE_ROWS - H:])
                    act_ref[rows, cols[0]] = (hc[0] * _sig(hc[0]) * hc[1]).astype(bf16)
                    return tuple(new_tails)

                tails = lax.fori_loop(0, T // GATE_ROWS, step, tuple(carry[:, cs] for cs in cols), unroll=True)
                for cs, tail in zip(cols, tails):
                    carry[:, cs] = tail

    row = lambda c: pl.BlockSpec((T, c), lambda i: (i, 0))
    return pl.pallas_call(
        body, grid=(S // T,), name="ffn_up_gate_fwd",
        out_shape=(jax.ShapeDtypeStruct((S, 2 * D_FF), bf16), jax.ShapeDtypeStruct((S, 2 * D_FF), f32),
                   jax.ShapeDtypeStruct((S, D_FF), bf16)),
        in_specs=[row(D_MODEL), _resident((D_MODEL, 2 * D_FF)), _acc((1, 2 * D_FF)), _acc((8, 2 * D_FF)),
                  _acc((1, 2 * D_FF))],
        out_specs=(row(2 * D_FF), row(2 * D_FF), row(D_FF)),
        scratch_shapes=[pltpu.VMEM((T, 2 * D_FF), f32), pltpu.VMEM((H, 2 * D_FF), f32)],
        compiler_params=_cp(("arbitrary",), 56),
    )(x1b, w, b, conv_w, conv_b)


def _ffn_down_ln_fwd(act, w, b, x1, ln_g, ln_b):
    S = act.shape[0]
    T = 512

    def body(a_ref, w_ref, b_ref, x_ref, g_ref, be_ref, z_ref, y_ref):
        z = jnp.dot(a_ref[...], w_ref[...], preferred_element_type=f32) + b_ref[...] + ALPHA * x_ref[...]
        z_ref[...] = z
        xhat, _ = _ln_stats(z)
        y_ref[...] = xhat * g_ref[...] + be_ref[...]

    row = lambda c: pl.BlockSpec((T, c), lambda i: (i, 0))
    return pl.pallas_call(
        body, grid=(S // T,), name="ffn_down_ln_fwd",
        out_shape=(jax.ShapeDtypeStruct((S, D_MODEL), f32), jax.ShapeDtypeStruct((S, D_MODEL), f32)),
        in_specs=[row(D_FF), _resident((D_FF, D_MODEL)), _acc((1, D_MODEL)), row(D_MODEL), _acc((1, D_MODEL)),
                  _acc((1, D_MODEL))],
        out_specs=(row(D_MODEL), row(D_MODEL)),
        compiler_params=_cp(("parallel",), 40),
    )(act, w, b, x1, ln_g, ln_b)


def _loss_ln_bwd(y, target, z, ln_g):
    S = y.shape[0]
    T = 512

    def body(y_ref, t_ref, z_ref, g_ref, dz_ref, dzb_ref, loss_ref, dg_ref, db_ref):
        @pl.when(pl.program_id(0) == 0)
        def _():
            loss_ref[...] = jnp.zeros_like(loss_ref)
            dg_ref[...] = jnp.zeros_like(dg_ref)
            db_ref[...] = jnp.zeros_like(db_ref)
        err = y_ref[...] - t_ref[...]
        loss_ref[...] += _colsum(err * err) * (0.5 / D_MODEL)
        dy = err * (1.0 / D_MODEL)
        xhat, rstd = _ln_stats(z_ref[...])
        dz = _ln_bwd(dy, xhat, rstd, g_ref[...])
        dz_ref[...] = dz
        dzb_ref[...] = dz.astype(bf16)
        dg_ref[...] += _colsum(dy * xhat)
        db_ref[...] += _colsum(dy)

    row = pl.BlockSpec((T, D_MODEL), lambda i: (i, 0))
    vec = jax.ShapeDtypeStruct((1, D_MODEL), f32)
    return pl.pallas_call(
        body, grid=(S // T,), name="loss_ln_bwd",
        out_shape=(jax.ShapeDtypeStruct((S, D_MODEL), f32), jax.ShapeDtypeStruct((S, D_MODEL), bf16), vec, vec, vec),
        in_specs=[row, row, row, _acc((1, D_MODEL))],
        out_specs=(row, row, _acc((1, D_MODEL)), _acc((1, D_MODEL)), _acc((1, D_MODEL))),
        compiler_params=_cp(("arbitrary",), 40),
    )(y, target, z, ln_g)


def _dgrad_ln_bwd(g, w, dz_res, z, ln_g, name):
    S, K = g.shape
    T = 256
    with_ln = z is not None

    def body(*refs):
        if with_ln:
            g_ref, w_ref, r_ref, z_ref, lg_ref, dz_ref, dzb_ref, dg_ref, db_ref = refs
        else:
            g_ref, w_ref, r_ref, dx_ref = refs
        dx = lax.dot_general(g_ref[...], w_ref[...], NT_DIMS, preferred_element_type=f32) + ALPHA * r_ref[...]
        if not with_ln:
            dx_ref[...] = dx
            return

        @pl.when(pl.program_id(0) == 0)
        def _():
            dg_ref[...] = jnp.zeros_like(dg_ref)
            db_ref[...] = jnp.zeros_like(db_ref)
        xhat, rstd = _ln_stats(z_ref[...])
        dz = _ln_bwd(dx, xhat, rstd, lg_ref[...])
        dz_ref[...] = dz
        dzb_ref[...] = dz.astype(bf16)
        dg_ref[...] += _colsum(dx * xhat)
        db_ref[...] += _colsum(dx)

    row = pl.BlockSpec((T, D_MODEL), lambda i: (i, 0))
    vec = jax.ShapeDtypeStruct((1, D_MODEL), f32)
    in_specs = [pl.BlockSpec((T, K), lambda i: (i, 0)), _resident((D_MODEL, K)), row]
    args = [g, w, dz_res]
    if with_ln:
        in_specs += [row, _acc((1, D_MODEL))]
        args += [z, ln_g]
        out_shape = (jax.ShapeDtypeStruct((S, D_MODEL), f32), jax.ShapeDtypeStruct((S, D_MODEL), bf16), vec, vec)
        out_specs = (row, row, _acc((1, D_MODEL)), _acc((1, D_MODEL)))
    else:
        out_shape = jax.ShapeDtypeStruct((S, D_MODEL), f32)
        out_specs = row
    return pl.pallas_call(
        body, grid=(S // T,), name=name, out_shape=out_shape, in_specs=in_specs, out_specs=out_specs,
        compiler_params=_cp(("arbitrary",), 48),
    )(*args)


def _ffn_down_gate_bwd(dzb, w_down, hfb, hc, conv_w):
    S = hc.shape[0]
    T = 256
    H = FFN_HALO
    nt = S // T
    K = FFN_CONV_WIDTH

    def body(dz_ref, w_ref, h_ref, hc_ref, cw_ref, dh_ref, dw_ref, dcb_ref, da_buf, carry):
        @pl.when(pl.program_id(0) == 0)
        def _():
            dw_ref[...] = jnp.zeros_like(dw_ref)
            dcb_ref[...] = jnp.zeros_like(dcb_ref)
            carry[...] = jnp.zeros_like(carry)
        da_buf[...] = lax.dot_general(dz_ref[...], w_ref[...], NT_DIMS, preferred_element_type=f32)
        ngroups = T // GATE_ROWS
        for c0 in range(0, D_FF, GATE_COLS):
            cols = _gate_cols(c0)
            wts = [[_bcast_rows(cw_ref, k, cs) for k in range(K)] for cs in cols]

            def step(it, state, cols=cols, wts=wts):
                heads, accs = state
                rows = pl.ds(pl.multiple_of((ngroups - 1 - it) * GATE_ROWS, GATE_ROWS), GATE_ROWS)
                g = hc_ref[rows, cols[0]]
                v = hc_ref[rows, cols[1]]
                da = da_buf[rows, cols[0]]
                sg = _sig(g)
                dms = (da * v * (sg * (1.0 + g * (1.0 - sg))), da * (g * sg))
                new_heads, new_accs = [], []
                for cs, wt, dm, head, acc in zip(cols, wts, dms, heads, accs):
                    h0 = h_ref[rows, cs].astype(f32)
                    ext = jnp.concatenate([dm, head], axis=0)
                    dh = wt[K - 1] * dm
                    acc_k = [None] * K + [acc[K] + _fold_rows(dm)]
                    acc_k[K - 1] = acc[K - 1] + _fold_rows(dm * h0)
                    for ahead in range(1, K):
                        dk = pltpu.roll(ext, GATE_ROWS + H - ahead, 0)[:GATE_ROWS]
                        dh = dh + wt[K - 1 - ahead] * dk
                        acc_k[K - 1 - ahead] = acc[K - 1 - ahead] + _fold_rows(dk * h0)
                    dh_ref[rows, cs] = dh.astype(bf16)
                    new_heads.append(dm[:H])
                    new_accs.append(tuple(acc_k))
                return tuple(new_heads), tuple(new_accs)

            zero = jnp.zeros((SUBLANES, GATE_COLS), f32)
            init = (tuple(carry[:, cs] for cs in cols), tuple(tuple(zero for _ in range(K + 1)) for _ in cols))
            heads, accs = lax.fori_loop(0, ngroups, step, init, unroll=True)
            for cs, head, acc in zip(cols, heads, accs):
                carry[:, cs] = head
                dcb_ref[:, cs] += _colsum(acc[K])
                for k in range(K):
                    dw_ref[k:k + 1, cs] += _colsum(acc[k])

    tile = lambda c: pl.BlockSpec((T, c), lambda i: (nt - 1 - i, 0))
    return pl.pallas_call(
        body, grid=(nt,), name="ffn_down_gate_bwd",
        out_shape=(jax.ShapeDtypeStruct((S, 2 * D_FF), bf16), jax.ShapeDtypeStruct((8, 2 * D_FF), f32),
                   jax.ShapeDtypeStruct((1, 2 * D_FF), f32)),
        in_specs=[tile(D_MODEL), _resident((D_FF, D_MODEL)), tile(2 * D_FF), tile(2 * D_FF), _acc((8, 2 * D_FF))],
        out_specs=(tile(2 * D_FF), _acc((8, 2 * D_FF)), _acc((1, 2 * D_FF))),
        scratch_shapes=[pltpu.VMEM((T, D_FF), f32), pltpu.VMEM((H, 2 * D_FF), f32)],
        compiler_params=_cp(("arbitrary",), 48),
    )(dzb, w_down, hfb, hc, conv_w)


def _wgrad(a, g, tn, name):
    S, K = a.shape
    N = g.shape[1]
    T = 1024 if S % 1024 == 0 else S

    def body(a_ref, g_ref, dw_ref, db_ref):
        @pl.when(pl.program_id(1) == 0)
        def _():
            dw_ref[...] = jnp.zeros_like(dw_ref)
            db_ref[...] = jnp.zeros_like(db_ref)
        gt = g_ref[...]
        dw_ref[...] += lax.dot_general(a_ref[...].astype(bf16), gt, TN_DIMS, preferred_element_type=f32)
        db_ref[...] += _colsum(gt.astype(f32))

    return pl.pallas_call(
        body, grid=(N // tn, S // T), name=name,
        out_shape=(jax.ShapeDtypeStruct((K, N), f32), jax.ShapeDtypeStruct((1, N), f32)),
        in_specs=[pl.BlockSpec((T, K), lambda j, i: (i, 0)), pl.BlockSpec((T, tn), lambda j, i: (i, j))],
        out_specs=(pl.BlockSpec((K, tn), lambda j, i: (0, j)), pl.BlockSpec((1, tn), lambda j, i: (0, j))),
        compiler_params=_cp(("parallel", "arbitrary"), 48),
    )(a, g)


def _outproj_dgrad(dzb, w, attn_out, lse):
    S = dzb.shape[0]
    T = PERM_TILE
    nsl = ATTN_CH // LANES
    n_p = len(DILATIONS)

    def body(g_ref, w_ref, ao_ref, lse_ref, dco_ref, dgo_ref, *rest):
        do_refs = rest[:n_p]
        st_refs = rest[n_p:2 * n_p]
        slabs = rest[2 * n_p:]
        dcat = lax.dot_general(g_ref[...], w_ref[...], NT_DIMS, preferred_element_type=f32)
        dco_ref[...] = dcat[:, :CONV_CH]
        dgo_ref[...] = dcat[:, CONV_CH + ATTN_CH:]
        lane = lax.broadcasted_iota(jnp.int32, (T, LANES), 1)
        st = lse_ref[...]
        for j in range(nsl):
            dO = dcat[:, CONV_CH + LANES * j:CONV_CH + LANES * (j + 1)]
            prod = dO * ao_ref[:, LANES * j:LANES * (j + 1)].astype(f32)
            for a in range(2):
                in_head = (lane < HEAD_DIM) if a == 0 else (lane >= HEAD_DIM)
                delta = jnp.sum(jnp.where(in_head, prod, 0.0), axis=1, keepdims=True)
                st = jnp.where((lane // 16 == 2 * j + a) & (lane % 16 >= 8), delta, st)
            slabs[j][...] = dO
        slabs[nsl][...] = st
        for d, do_ref, st_ref in zip(DILATIONS, do_refs, st_refs):
            for r in range(d):
                for j in range(nsl):
                    do_ref[r, :, LANES * j:LANES * (j + 1)] = _rows_of(slabs[j], r, T // d, d).astype(bf16)
                st_ref[r] = _rows_of(slabs[nsl], r, T // d, d)

    row = lambda c: pl.BlockSpec((T, c), lambda i: (i, 0))
    return pl.pallas_call(
        body, grid=(S // T,), name="outproj_dgrad",
        out_shape=(jax.ShapeDtypeStruct((S, CONV_CH), f32), jax.ShapeDtypeStruct((S, GMLP_CH), f32))
        + tuple(_perm_shape(S, d, ATTN_CH, bf16) for d in DILATIONS)
        + tuple(_perm_shape(S, d, ATTN_BLOCK, f32) for d in DILATIONS),
        in_specs=[row(D_MODEL), _resident((D_MODEL, D_MODEL)), row(ATTN_CH), row(ATTN_BLOCK)],
        out_specs=(row(CONV_CH), row(GMLP_CH)) + tuple(_perm_spec(d, ATTN_CH) for d in DILATIONS)
        + tuple(_perm_spec(d, ATTN_BLOCK) for d in DILATIONS),
        scratch_shapes=_slabs(nsl + 1, T),
        compiler_params=_cp(("parallel",), 40),
    )(dzb, w, attn_out, lse)


def _gmlp_bwd(c_in, dgm, ln_g, ln_b, w_s, b_s_t):
    S = c_in.shape[0]
    T = 512
    nsteps = S // T

    def body(c_ref, dg_ref, g_ref, be_ref, w_ref, bs_ref, dc_ref, dlg_ref, dlb_ref, dw_ref, dbs_ref,
             du_buf, dv_buf, dm_acc):
        i = pl.program_id(0)

        @pl.when(i == 0)
        def _():
            dlg_ref[...] = jnp.zeros_like(dlg_ref)
            dlb_ref[...] = jnp.zeros_like(dlb_ref)
            dw_ref[...] = jnp.zeros_like(dw_ref)
            dm_acc[...] = jnp.zeros_like(dm_acc)
        c = c_ref[...]
        u = c[:, :GMLP_CH]
        xhat, rstd = _ln_stats(c[:, GMLP_CH:])
        vb = (xhat * g_ref[...] + be_ref[...]).astype(bf16)
        dgm_t = dg_ref[...]
        dm_all = dgm_t * u
        for g in range(GMLP_GROUPS):
            wt = _tril_bf16(w_ref[g])
            cs = slice(GMLP_GROUP_DIM * g, GMLP_GROUP_DIM * (g + 1))
            dw_g = jnp.zeros((CHUNK, CHUNK), f32)
            for ci in range(T // CHUNK):
                rs = slice(CHUNK * ci, CHUNK * (ci + 1))
                v_c = vb[rs, cs]
                mixed = jnp.dot(wt, v_c, preferred_element_type=f32) + bs_ref[:, g:g + 1]
                dm = dm_all[rs, cs]
                dmb = dm.astype(bf16)
                du_buf[rs, cs] = dgm_t[rs, cs] * mixed
                dv_buf[rs, cs] = lax.dot_general(wt, dmb, TN_DIMS, preferred_element_type=f32)
                dw_g = dw_g + lax.dot_general(dmb, v_c, NT_DIMS, preferred_element_type=f32)
                dm_acc[:, cs] += dm
            dw_ref[g] += dw_g
        dv = dv_buf[...]
        dvr = _ln_bwd(dv, xhat, rstd, g_ref[...])
        dlg_ref[...] += _colsum(dv * xhat)
        dlb_ref[...] += _colsum(dv)
        dc_ref[:, :GMLP_CH] = du_buf[...].astype(bf16)
        dc_ref[:, GMLP_CH:] = dvr.astype(bf16)

        @pl.when(i == nsteps - 1)
        def _():
            row = lax.broadcasted_iota(jnp.int32, (CHUNK, CHUNK), 0)
            col = lax.broadcasted_iota(jnp.int32, (CHUNK, CHUNK), 1)
            tile = jnp.zeros((CHUNK, CHUNK), f32)
            for g in range(GMLP_GROUPS):
                dw_ref[g] = jnp.where(col <= row, dw_ref[g], 0.0)
                gsum = jnp.sum(dm_acc[:, GMLP_GROUP_DIM * g:GMLP_GROUP_DIM * (g + 1)], axis=1, keepdims=True)
                tile = jnp.where(col == g, gsum, tile)
            dbs_ref[...] = tile

    vec = jax.ShapeDtypeStruct((1, GMLP_CH), f32)
    return pl.pallas_call(
        body, grid=(nsteps,), name="gmlp_bwd",
        out_shape=(jax.ShapeDtypeStruct((S, 2 * GMLP_CH), bf16), vec, vec,
                   jax.ShapeDtypeStruct((GMLP_GROUPS, CHUNK, CHUNK), f32), jax.ShapeDtypeStruct((CHUNK, CHUNK), f32)),
        in_specs=[pl.BlockSpec((T, 2 * GMLP_CH), lambda i: (i, 0)), pl.BlockSpec((T, GMLP_CH), lambda i: (i, 0)),
                  _acc((1, GMLP_CH)), _acc((1, GMLP_CH)), _acc((GMLP_GROUPS, CHUNK, CHUNK)), _acc((CHUNK, GMLP_GROUPS))],
        out_specs=(pl.BlockSpec((T, 2 * GMLP_CH), lambda i: (i, 0)), _acc((1, GMLP_CH)), _acc((1, GMLP_CH)),
                   _acc((GMLP_GROUPS, CHUNK, CHUNK)), _acc((CHUNK, CHUNK))),
        scratch_shapes=[pltpu.VMEM((T, GMLP_CH), f32), pltpu.VMEM((T, GMLP_CH), f32), pltpu.VMEM((CHUNK, GMLP_CH), f32)],
        compiler_params=_cp(("arbitrary",), 32),
    )(c_in, dgm, ln_g, ln_b, w_s, b_s_t)


def _attn_bwd_pattern(qkv, d_out, stats, bias, d):
    _, L, _ = qkv.shape
    B = ATTN_BLOCK
    QB = _attn_tile(L)
    nsb = QB // B
    nt = L // QB
    U = nsb * UNITS_PER_BLOCK
    KV = 2 * ATTN_CH

    def body(cur_ref, hk_ref, hv_ref, do_ref, st_ref, b_ref, dqkv_ref, dbias_ref, lg, dp, pb, dsb, dkv, carry):
        r = pl.program_id(0)
        i = pl.program_id(1)
        n = nt - 1 - i

        @pl.when((r == 0) & (i == 0))
        def _():
            dbias_ref[...] = jnp.zeros_like(dbias_ref)

        @pl.when(i == 0)
        def _():
            carry[...] = jnp.zeros_like(carry)

        def operands(b, j, a):
            rows = slice(B * b, B * (b + 1))
            q2 = cur_ref[rows, LANES * j:LANES * (j + 1)]
            do2 = do_ref[rows, LANES * j:LANES * (j + 1)]
            keep = _head_lanes(a)
            return jnp.where(keep, q2, jnp.zeros_like(q2)), jnp.where(keep, do2, jnp.zeros_like(do2))

        for b in range(nsb):
            valid = _band_mask(b == 0, n)
            for j in range(HEAD_PAIRS):
                k2 = _pair_keys(cur_ref, hk_ref, 1, b, j)
                v2 = _pair_keys(cur_ref, hv_ref, 2, b, j)
                for a in range(2):
                    u = (b * HEAD_PAIRS + j) * 2 + a
                    qm, dom = operands(b, j, a)
                    logits = lax.dot_general(qm, k2, NT_DIMS, preferred_element_type=f32) + b_ref[2 * j + a]
                    lg[B * u:B * (u + 1), :] = jnp.where(valid, logits, NEG)
                    dp[B * u:B * (u + 1), :] = lax.dot_general(dom, v2, NT_DIMS, preferred_element_type=f32)
        for b in range(nsb):
            for j in range(HEAD_PAIRS):
                for a in range(2):
                    u = (b * HEAD_PAIRS + j) * 2 + a
                    rows = slice(B * u, B * (u + 1))
                    lane0 = 32 * j + 16 * a
                    lse = st_ref[B * b:B * (b + 1), lane0:lane0 + 1]
                    delta = st_ref[B * b:B * (b + 1), lane0 + 8:lane0 + 9]
                    p = jnp.exp(lg[rows, :] - lse)
                    ds = p * (dp[rows, :] - delta)
                    pb[rows, :] = p.astype(bf16)
                    dsb[rows, :] = ds.astype(bf16)
                    dbias_ref[2 * j + a] += ds
        dkv[...] = jnp.zeros_like(dkv)
        for b in range(nsb):
            for j in range(HEAD_PAIRS):
                k2 = _pair_keys(cur_ref, hk_ref, 1, b, j)
                dq, dk2, dv2 = [], None, None
                for a in range(2):
                    u = (b * HEAD_PAIRS + j) * 2 + a
                    rows = slice(B * u, B * (u + 1))
                    qm, dom = operands(b, j, a)
                    ds_u = dsb[rows, :]
                    dq.append(jnp.dot(ds_u, k2, preferred_element_type=f32))
                    dk_u = lax.dot_general(ds_u, qm, TN_DIMS, preferred_element_type=f32)
                    dv_u = lax.dot_general(pb[rows, :], dom, TN_DIMS, preferred_element_type=f32)
                    dk2 = dk_u if dk2 is None else dk2 + dk_u
                    dv2 = dv_u if dv2 is None else dv2 + dv_u
                dq2 = jnp.where(_head_lanes(0), dq[0], dq[1]) * (HEAD_DIM ** -0.5)
                dqkv_ref[B * b:B * (b + 1), LANES * j:LANES * (j + 1)] = dq2.astype(bf16)
                dkv[B * b:B * (b + 2), LANES * j:LANES * (j + 1)] += dk2
                dkv[B * b:B * (b + 2), ATTN_CH + LANES * j:ATTN_CH + LANES * (j + 1)] += dv2
        dkv[QB:, :] += carry[...]
        dqkv_ref[:, ATTN_CH:] = dkv[B:, :].astype(bf16)
        carry[...] = dkv[0:B, :]

    halo = lambda part: pl.BlockSpec((None, B, ATTN_CH),
                                     lambda r, i: (r, jnp.maximum((nt - 1 - i) * nsb - 1, 0), part))
    tile_spec = lambda c: pl.BlockSpec((None, QB, c), lambda r, i: (r, nt - 1 - i, 0))
    return pl.pallas_call(
        body, grid=(d, nt), name=f"attn_bwd_d{d}",
        out_shape=(jax.ShapeDtypeStruct((d, L, QKV_CH), bf16), jax.ShapeDtypeStruct((ATTN_HEADS, B, 2 * B), f32)),
        in_specs=[tile_spec(QKV_CH), halo(1), halo(2), tile_spec(ATTN_CH), tile_spec(B),
                  _resident((ATTN_HEADS, B, 2 * B))],
        out_specs=(tile_spec(QKV_CH), _acc((ATTN_HEADS, B, 2 * B))),
        scratch_shapes=[pltpu.VMEM((U * B, 2 * B), f32), pltpu.VMEM((U * B, 2 * B), f32),
                        pltpu.VMEM((U * B, 2 * B), bf16), pltpu.VMEM((U * B, 2 * B), bf16),
                        pltpu.VMEM((B + QB, KV), f32), pltpu.VMEM((B, KV), f32)],
        compiler_params=_cp(("arbitrary", "arbitrary"), 48),
    )(qkv, qkv, qkv, d_out, stats, bias)


def _attn_bwd_merge(d_a, dqkvs, d_c):
    S = d_a.shape[0]
    T = PERM_TILE
    nsl = QKV_CH // LANES
    n_p = len(DILATIONS)

    def body(da_ref, *rest):
        g_refs = rest[:n_p]
        dc_ref, dh_ref = rest[n_p:n_p + 2]
        slabs = rest[n_p + 2:]
        q0 = 2 * CONV_CH
        dh_ref[:, :q0] = da_ref[...]
        dh_ref[:, q0 + QKV_CH:] = dc_ref[...]
        for p, (d, g_ref) in enumerate(zip(DILATIONS, g_refs)):
            for r in range(d):
                for j in range(nsl):
                    _set_rows_of(slabs[p * nsl + j], r, T // d, d, g_ref[r, :, LANES * j:LANES * (j + 1)].astype(f32))
        for j in range(nsl):
            acc = slabs[j][...]
            for p in range(1, n_p):
                acc = acc + slabs[p * nsl + j][...]
            dh_ref[:, q0 + LANES * j:q0 + LANES * (j + 1)] = acc.astype(bf16)

    row = lambda c: pl.BlockSpec((T, c), lambda i: (i, 0))
    return pl.pallas_call(
        body, grid=(S // T,), name="attn_bwd_merge", out_shape=jax.ShapeDtypeStruct((S, IN_CH), bf16),
        in_specs=[row(2 * CONV_CH)] + [_perm_spec(d, QKV_CH) for d in DILATIONS] + [row(2 * GMLP_CH)],
        out_specs=row(IN_CH), scratch_shapes=_slabs(n_p * nsl, T),
        compiler_params=_cp(("parallel",), 48),
    )(d_a, *dqkvs, d_c)


def _bias_table_grad(dbias, buckets):
    n = dbias.shape[0]

    def body(db_ref, bk_ref, o_ref):
        p = pl.program_id(0)
        h = pl.program_id(1)

        @pl.when((p == 0) & (h == 0))
        def _():
            o_ref[...] = jnp.zeros_like(o_ref)
        ids = bk_ref[0]
        db = db_ref[0, 0]
        row = lax.broadcasted_iota(jnp.int32, (N_BUCKETS, 128), 0)
        lane = lax.broadcasted_iota(jnp.int32, (N_BUCKETS, 128), 1)
        upd = jnp.zeros((N_BUCKETS, 128), f32)
        for b in range(N_BUCKETS):
            s = jnp.sum(jnp.sum(jnp.where(ids == b, db, 0.0), axis=1, keepdims=True), axis=0, keepdims=True)
            upd = jnp.where((row == b) & (lane == h), s, upd)
        o_ref[...] += upd

    return pl.pallas_call(
        body, grid=(n, ATTN_HEADS), name="bias_table_grad",
        out_shape=jax.ShapeDtypeStruct((N_BUCKETS, 128), f32),
        in_specs=[pl.BlockSpec((1, 1, ATTN_BLOCK, 2 * ATTN_BLOCK), lambda p, h: (p, h, 0, 0)),
                  pl.BlockSpec((1, ATTN_BLOCK, 2 * ATTN_BLOCK), lambda p, h: (p, 0, 0))],
        out_specs=_acc((N_BUCKETS, 128)),
        compiler_params=_cp(("arbitrary", "arbitrary"), 16),
    )(dbias, buckets)


def _conv_bwd(a_in, hc, dco, dw_w, ln_g, ln_b):
    S = a_in.shape[0]
    T = 512
    hb = T // CONV_HALO
    nsteps = S // T
    R = T + CONV_HALO
    K = CONV_WIDTH

    def body(a_ref, hc_ref, hcn_ref, d_ref, dn_ref, w_ref, g_ref, be_ref,
             da_ref, dw_ref, dcb_ref, dlg_ref, dlb_ref, ext, dbuf, wacc):
        i = pl.program_id(0)

        @pl.when(i == 0)
        def _():
            wacc[...] = jnp.zeros_like(wacc)
            dcb_ref[...] = jnp.zeros_like(dcb_ref)
            dlg_ref[...] = jnp.zeros_like(dlg_ref)
            dlb_ref[...] = jnp.zeros_like(dlb_ref)
        ext[0:T, :] = hc_ref[...]
        ext[T:, :] = hcn_ref[...]
        xhat, rstd = _ln_stats(ext[...])
        hl = xhat * g_ref[...] + be_ref[...]
        ext[0:T, :] = d_ref[...]
        ext[T:, :] = dn_ref[...]
        sl_ = _sig(hl)
        dhl = ext[...] * (sl_ * (1.0 + hl * (1.0 - sl_)))
        dhc = _ln_bwd(dhl, xhat, rstd, g_ref[...])
        rowi = lax.broadcasted_iota(jnp.int32, (R, CONV_CH), 0)
        dbuf[...] = jnp.where((rowi < T) | (i < nsteps - 1), dhc, 0.0)
        dlg_ref[...] += _colsum(dhl[:T] * xhat[:T])
        dlb_ref[...] += _colsum(dhl[:T])
        dcb_ref[...] += _colsum(dbuf[pl.ds(0, T), :])
        starts = _window_rolls(range(K))
        slabs = [slice(LANES * j, LANES * (j + 1)) for j in range(CONV_CH // LANES)]

        def step(g, _):
            r0 = pl.multiple_of(g * CONV_GROUP, CONV_GROUP)
            rows = pl.ds(r0, CONV_GROUP)
            for j, cs in enumerate(slabs):
                gate_cs = slice(CONV_CH + LANES * j, CONV_CH + LANES * (j + 1))
                win = dbuf[pl.ds(r0, CONV_GROUP + CONV_HALO), cs]
                a = a_ref[rows, cs]
                sg = _sig(a_ref[rows, gate_cs])
                hg = a * sg
                dhg = jnp.zeros((CONV_GROUP, LANES), f32)
                for b, ss in starts.items():
                    rolled = win if b == 0 else pltpu.roll(win, b, 0)
                    for s in ss:
                        k = K - 1 - s
                        dk = rolled[s + b:s + b + CONV_GROUP]
                        dhg = dhg + w_ref[k:k + 1, cs] * dk
                        prod = dk * hg
                        fold = prod[0:SUBLANES]
                        for r in range(SUBLANES, CONV_GROUP, SUBLANES):
                            fold = fold + prod[r:r + SUBLANES]
                        wacc[SUBLANES * k:SUBLANES * (k + 1), cs] += fold
                da_ref[rows, cs] = (dhg * sg).astype(bf16)
                da_ref[rows, gate_cs] = (dhg * hg * (1.0 - sg)).astype(bf16)
            return 0

        lax.fori_loop(0, T // CONV_GROUP, step, 0)

        @pl.when(i == nsteps - 1)
        def _():
            for k in range(K):
                dw_ref[k:k + 1, :] = _colsum(wacc[SUBLANES * k:SUBLANES * (k + 1), :])
            dw_ref[K:, :] = jnp.zeros((32 - K, CONV_CH), f32)

    vec = jax.ShapeDtypeStruct((1, CONV_CH), f32)
    nxt = lambda i: (jnp.minimum((i + 1) * hb, nsteps * hb - 1), 0)
    return pl.pallas_call(
        body, grid=(nsteps,), name="conv_bwd",
        out_shape=(jax.ShapeDtypeStruct((S, 2 * CONV_CH), bf16), jax.ShapeDtypeStruct((32, CONV_CH), f32), vec, vec, vec),
        in_specs=[pl.BlockSpec((T, 2 * CONV_CH), lambda i: (i, 0)),
                  pl.BlockSpec((T, CONV_CH), lambda i: (i, 0)), pl.BlockSpec((CONV_HALO, CONV_CH), nxt),
                  pl.BlockSpec((T, CONV_CH), lambda i: (i, 0)), pl.BlockSpec((CONV_HALO, CONV_CH), nxt),
                  _acc((32, CONV_CH)), _acc((1, CONV_CH)), _acc((1, CONV_CH))],
        out_specs=(pl.BlockSpec((T, 2 * CONV_CH), lambda i: (i, 0)), _acc((32, CONV_CH)), _acc((1, CONV_CH)),
                   _acc((1, CONV_CH)), _acc((1, CONV_CH))),
        scratch_shapes=[pltpu.VMEM((R, CONV_CH), f32), pltpu.VMEM((R, CONV_CH), f32),
                        pltpu.VMEM((SUBLANES * 32, CONV_CH), f32)],
        compiler_params=_cp(("arbitrary",), 32),
    )(a_in, hc, hc, dco, dco, dw_w, ln_g, ln_b)


def _adamw(g, w, m, v, name):
    R, C = g.shape
    T = R
    for cand in (512, 256, 128, 64, 32, 16, 8):
        if R % cand == 0 and cand * C * 4 <= MIB:
            T = cand
            break
    c1 = 1.0 / (1.0 - ADAM_B1 ** ADAM_STEP)
    c2 = 1.0 / (1.0 - ADAM_B2 ** ADAM_STEP)

    def body(g_ref, w_ref, m_ref, v_ref, d_ref, nm_ref, nv_ref):
        gg = g_ref[...]
        nm = ADAM_B1 * m_ref[...] + (1.0 - ADAM_B1) * gg
        nv = ADAM_B2 * v_ref[...] + (1.0 - ADAM_B2) * (gg * gg)
        nm_ref[...] = nm
        nv_ref[...] = nv
        d_ref[...] = -ADAM_LR * ((nm * c1) / (jnp.sqrt(nv * c2) + ADAM_EPS) + ADAM_WD * w_ref[...])

    blk = pl.BlockSpec((T, C), lambda i: (i, 0))
    sd = jax.ShapeDtypeStruct((R, C), f32)
    return pl.pallas_call(
        body, grid=(R // T,), name=name, out_shape=(sd, sd, sd), in_specs=[blk] * 4, out_specs=(blk, blk, blk),
        compiler_params=_cp(("parallel",), 48),
    )(g, w, m, v)


def _pad_rows(a, rows):
    return jnp.pad(a, ((0, rows - a.shape[0]), (0, 0)))


def _local_step(x, target, wb, late_weights, sp, sink):
    buckets = jnp.asarray(_bucket_ids())
    bias = _bias_build(sp["rel_bias_table"], buckets)
    wb = dict(wb)
    saved = []
    xl = x
    for l in range(DEPTH):
        vec = lambda name: sp[name][l][None, :]
        a_in, c_in, *qkv = _inproj_fwd(xl, wb["w_in"][l], vec("b_in"))
        conv_w = _pad_rows(sp["conv_dw_w"][l], 32)
        conv_out, hc = _conv_fwd(a_in, conv_w, vec("conv_dw_b"), vec("conv_ln_g"), vec("conv_ln_b"))
        attn_out, lse = _attn_fwd(qkv, bias)
        bs_t = sp["gmlp_b_s"][l].T
        gm_out = _gmlp_fwd(c_in, vec("gmlp_ln_g"), vec("gmlp_ln_b"), sp["gmlp_w_s"][l], bs_t)
        if l == 0:
            wb.update(late_weights(gm_out))
        cat, z1, x1, x1b = _outproj_ln_fwd(conv_out, attn_out, gm_out, wb["w_out"][l], vec("b_out"), xl,
                                           vec("ln1_g"), vec("ln1_b"))
        fconv_w = _pad_rows(sp["ffn_conv_w"][l], 8)
        hfb, fhc, act = _ffn_up_gate_fwd(x1b, wb["ffn_w_up"][l], vec("ffn_b_up"), fconv_w, vec("ffn_conv_b"))
        z2, x2 = _ffn_down_ln_fwd(act, wb["ffn_w_down"][l], vec("ffn_b_down"), x1, vec("ln2_g"), vec("ln2_b"))
        saved.append(dict(x=xl, a_in=a_in, qkv=qkv, c_in=c_in, hc=hc, attn_out=attn_out, lse=lse, cat=cat, z1=z1,
                          x1b=x1b, hfb=hfb, fhc=fhc, act=act, z2=z2, conv_w=conv_w, fconv_w=fconv_w, bs_t=bs_t))
        xl = x2

    grads = {}
    per_layer = {k: [None] * DEPTH for k in (
        "b_in", "conv_dw_w", "conv_dw_b", "conv_ln_g", "conv_ln_b", "gmlp_ln_g", "gmlp_ln_b", "gmlp_w_s",
        "gmlp_b_s", "b_out", "ln1_g", "ln1_b", "ffn_b_up", "ffn_conv_w", "ffn_conv_b", "ffn_b_down", "ln2_g", "ln2_b")}
    dbias_all = []
    l = DEPTH - 1
    vec = lambda name: sp[name][l][None, :]
    dz2, dz2b, loss_part, dg2, db2 = _loss_ln_bwd(xl, target, saved[l]["z2"], vec("ln2_g"))
    loss = jnp.sum(loss_part)
    grad_x = None
    tok = jnp.zeros((), f32)
    for l in reversed(range(DEPTH)):
        sv = saved[l]
        vec = lambda name: sp[name][l][None, :] + tok
        per_layer["ln2_g"][l] = dg2[0]
        per_layer["ln2_b"][l] = db2[0]
        dw_down, db_down = _wgrad(sv["act"], dz2b, 512, "ffn_down_wgrad")
        tok = sink.put("ffn_w_down", l, dw_down, tok)
        per_layer["ffn_b_down"][l] = db_down[0]
        dhf, dfcw, dfcb = _ffn_down_gate_bwd(dz2b, wb["ffn_w_down"][l], sv["hfb"], sv["fhc"], sv["fconv_w"])
        per_layer["ffn_conv_w"][l] = dfcw[:FFN_CONV_WIDTH]
        per_layer["ffn_conv_b"][l] = dfcb[0]
        dw_up, db_up = _wgrad(sv["x1b"], dhf, 1408, "ffn_up_wgrad")
        tok = sink.put("ffn_w_up", l, dw_up, tok)
        per_layer["ffn_b_up"][l] = db_up[0]
        dz1, dz1b, dg1, db1 = _dgrad_ln_bwd(dhf, wb["ffn_w_up"][l], dz2, sv["z1"], vec("ln1_g"), "ffn_up_dgrad_ln")
        per_layer["ln1_g"][l] = dg1[0]
        per_layer["ln1_b"][l] = db1[0]
        dw_out, db_out = _wgrad(sv["cat"], dz1b, 512, "outproj_wgrad")
        tok = sink.put("w_out", l, dw_out, tok)
        per_layer["b_out"][l] = db_out[0]
        dco, dgo, *perm = _outproj_dgrad(dz1b, wb["w_out"][l], sv["attn_out"], sv["lse"])
        d_outs, stats = perm[:len(DILATIONS)], perm[len(DILATIONS):]
        d_c, dglg, dglb, dws, dbs = _gmlp_bwd(sv["c_in"], dgo, vec("gmlp_ln_g"), vec("gmlp_ln_b"), sp["gmlp_w_s"][l],
                                              sv["bs_t"])
        per_layer["gmlp_ln_g"][l] = dglg[0]
        per_layer["gmlp_ln_b"][l] = dglb[0]
        per_layer["gmlp_w_s"][l] = dws
        per_layer["gmlp_b_s"][l] = dbs[:, :GMLP_GROUPS].T
        dqkvs = []
        for p, d in enumerate(DILATIONS):
            dqkv, dbias = _attn_bwd_pattern(sv["qkv"][p], d_outs[p], stats[p], bias[p], d)
            dqkvs.append(dqkv)
            dbias_all.append(dbias)
        d_a, dcw, dcb, dclg, dclb = _conv_bwd(sv["a_in"], sv["hc"], dco, sv["conv_w"], vec("conv_ln_g"),
                                              vec("conv_ln_b"))
        per_layer["conv_dw_w"][l] = dcw[:CONV_WIDTH]
        per_layer["conv_dw_b"][l] = dcb[0]
        per_layer["conv_ln_g"][l] = dclg[0]
        per_layer["conv_ln_b"][l] = dclb[0]
        dh = _attn_bwd_merge(d_a, dqkvs, d_c)
        dw_in, db_in = _wgrad(sv["x"], dh, 640, "inproj_wgrad")
        tok = sink.put("w_in", l, dw_in, tok)
        per_layer["b_in"][l] = db_in[0]
        if l > 0:
            pv = saved[l - 1]
            dz2, dz2b, dg2, db2 = _dgrad_ln_bwd(dh, wb["w_in"][l], dz1, pv["z2"], sp["ln2_g"][l - 1][None, :] + tok,
                                                "inproj_dgrad_ln")
        else:
            grad_x = _dgrad_ln_bwd(dh, wb["w_in"][l], dz1, None, None, "inproj_dgrad")
    for k, v in per_layer.items():
        grads[k] = jnp.stack(v)
    dbias_cat = jnp.stack(dbias_all)
    bk_cat = jnp.concatenate([buckets] * DEPTH, axis=0)
    grads["rel_bias_table"] = _bias_table_grad(dbias_cat, bk_cat)[:, :ATTN_HEADS]
    return loss, grad_x, grads, sink.finish(grad_x)


N_CHIPS = 4
BIG = {"w_in": (D_MODEL, IN_CH, 1), "w_out": (D_MODEL, D_MODEL, 0),
       "ffn_w_up": (D_MODEL, 2 * D_FF, 1), "ffn_w_down": (D_FF, D_MODEL, 0)}
SMALL = ("b_in", "conv_dw_w", "conv_dw_b", "conv_ln_g", "conv_ln_b", "rel_bias_table", "gmlp_ln_g", "gmlp_ln_b",
         "gmlp_w_s", "gmlp_b_s", "b_out", "ln1_g", "ln1_b", "ffn_b_up", "ffn_conv_w", "ffn_conv_b", "ffn_b_down",
         "ln2_g", "ln2_b")
SMALL_SHARDED = ("conv_dw_w", "ffn_conv_w")
WEIGHTS = ("w_in", "b_in", "conv_dw_w", "conv_dw_b", "conv_ln_g", "conv_ln_b", "rel_bias_table", "gmlp_ln_g",
           "gmlp_ln_b", "gmlp_w_s", "gmlp_b_s", "w_out", "b_out", "ln1_g", "ln1_b", "ffn_w_up", "ffn_b_up",
           "ffn_conv_w", "ffn_conv_b", "ffn_w_down", "ffn_b_down", "ln2_g", "ln2_b")
ANY = pl.BlockSpec(memory_space=pl.ANY)


def _position():
    return lax.axis_index("x"), lax.axis_index("y"), lax.axis_index("c")


def _other_chips(x, y):
    return [(1 - x, y), (x, 1 - y), (1 - x, 1 - y)]


def _cast_bf16(a):
    R, C = a.shape
    T = 128

    def body(a_ref, o_ref):
        o_ref[...] = a_ref[...].astype(bf16)

    return pl.pallas_call(
        body, grid=(R // T,), name="cast_bf16", out_shape=jax.ShapeDtypeStruct((R, C), bf16),
        in_specs=[pl.BlockSpec((T, C), lambda i: (i, 0))], out_specs=pl.BlockSpec((T, C), lambda i: (i, 0)),
        compiler_params=_cp(("parallel",), 16),
    )(a)


def _chip_slot(ref, name, l, p):
    K, N, ax = BIG[name]
    if ax == 1:
        sz = N // N_CHIPS
        return ref.at[l, :, pl.ds(pl.multiple_of(p * sz, 128), sz)]
    sz = K // N_CHIPS
    return ref.at[l, pl.ds(pl.multiple_of(p * sz, 16), sz), :]


def _gather_weights(shards, conv_w, fconv_w):
    names = list(shards)
    n_big = len(names)
    n_t = n_big + 2
    n_chip = 3 * n_t
    n_pass = 3 * n_big

    def body(*refs):
        ins = refs[:n_t]
        outs = refs[n_t:2 * n_t]
        send_sems, recv_sems, pass_send, pass_recv, local_sems = refs[2 * n_t:]
        x, y, c = _position()
        me = 2 * x + y
        chips = _other_chips(x, y)

        def src(t):
            return ins[t].at[c] if t < n_big else ins[t]

        def slot(t, l, p):
            return _chip_slot(outs[t], names[t], l, p) if t < n_big else outs[t].at[p]

        locs, cps = [], []
        for t in range(n_t):
            for l in (range(DEPTH) if t < n_big else (0,)):
                loc = pltpu.make_async_copy(ins[t].at[l] if t < n_big else ins[t], slot(t, l, me),
                                            local_sems.at[DEPTH * t + l])
                loc.start()
                locs.append(loc)
            for k, (px, py) in enumerate(chips):
                cp = pltpu.make_async_remote_copy(
                    src_ref=src(t), dst_ref=slot(t, c, me), send_sem=send_sems.at[3 * t + k],
                    recv_sem=recv_sems.at[3 * t + k], device_id=(px, py, c), device_id_type=MESH_ID)
                cp.start()
                cps.append(cp)
        for t in range(n_t):
            for k, (px, py) in enumerate(chips):
                landed = slot(t, c, 2 * px + py)
                pltpu.make_async_remote_copy(
                    src_ref=src(t), dst_ref=landed, send_sem=send_sems.at[3 * t + k],
                    recv_sem=recv_sems.at[3 * t + k], device_id=(px, py, c), device_id_type=MESH_ID).wait_recv()
                if t < n_big:
                    cp = pltpu.make_async_remote_copy(
                        src_ref=landed, dst_ref=landed, send_sem=pass_send.at[3 * t + k],
                        recv_sem=pass_recv.at[3 * t + k], device_id=(x, y, 1 - c), device_id_type=MESH_ID)
                    cp.start()
                    cps.append(cp)
        for t in range(n_big):
            for k, (px, py) in enumerate(chips):
                from_sibling = slot(t, 1 - c, 2 * px + py)
                pltpu.make_async_remote_copy(
                    src_ref=from_sibling, dst_ref=from_sibling, send_sem=pass_send.at[3 * t + k],
                    recv_sem=pass_recv.at[3 * t + k], device_id=(x, y, 1 - c), device_id_type=MESH_ID).wait_recv()
        for cp in cps:
            cp.wait_send()
        for loc in locs:
            loc.wait()

    ins = [shards[n] for n in names] + [conv_w, fconv_w]
    out_shape = [jax.ShapeDtypeStruct((DEPTH, BIG[n][0], BIG[n][1]), bf16) for n in names]
    out_shape += [jax.ShapeDtypeStruct((N_CHIPS,) + conv_w.shape, f32), jax.ShapeDtypeStruct((N_CHIPS,) + fconv_w.shape, f32)]
    outs = pl.pallas_call(
        body, name="gather_weights", out_shape=tuple(out_shape), in_specs=[ANY] * n_t, out_specs=tuple([ANY] * n_t),
        scratch_shapes=[pltpu.SemaphoreType.DMA((n_chip,)), pltpu.SemaphoreType.DMA((n_chip,)),
                        pltpu.SemaphoreType.DMA((n_pass,)), pltpu.SemaphoreType.DMA((n_pass,)),
                        pltpu.SemaphoreType.DMA((DEPTH * n_t,))],
    )(*ins)
    return dict(zip(names, outs[:n_big])), outs[-2], outs[-1]


LATE_WEIGHTS = ("w_out", "ffn_w_up", "ffn_w_down")
HBM = pl.BlockSpec(memory_space=pltpu.HBM)
SEM = pl.BlockSpec(memory_space=pltpu.SEMAPHORE)


def _cast_into_full(shard, name, chip_arr):
    K, N, ax = BIG[name]
    k, n = _shard_shape(name)
    T = 64
    nrt = k // T

    def body(p_ref, a_ref, o_ref):
        o_ref[...] = a_ref[...].astype(bf16)

    if ax == 1:
        out_spec = pl.BlockSpec((None, T, n), lambda l, i, p: (l, i, p[0]))
    else:
        out_spec = pl.BlockSpec((None, T, n), lambda l, i, p: (l, p[0] * nrt + i, 0))
    return pl.pallas_call(
        body, name="cast_into_full", out_shape=jax.ShapeDtypeStruct((DEPTH, K, N), bf16),
        grid_spec=pltpu.PrefetchScalarGridSpec(
            num_scalar_prefetch=1, grid=(DEPTH, nrt),
            in_specs=[pl.BlockSpec((None, T, n), lambda l, i, p: (l, i, 0))], out_specs=out_spec),
        compiler_params=_cp(("parallel", "parallel"), 16),
    )(chip_arr, shard)


def _late_copies(refs, send_sems, recv_sems):
    x, y, c = _position()
    me = 2 * x + y
    idx = 0
    for ref, name in zip(refs, LATE_WEIGHTS):
        for l in range(DEPTH):
            for px, py in _other_chips(x, y):
                def copy(p, ref=ref, name=name, l=l, px=px, py=py, idx=idx):
                    part = _chip_slot(ref, name, l, p)
                    return pltpu.make_async_remote_copy(
                        src_ref=part, dst_ref=part, send_sem=send_sems.at[idx], recv_sem=recv_sems.at[idx],
                        device_id=(px, py, c), device_id_type=MESH_ID)
                yield copy(me), copy(2 * px + py)
                idx += 1


N_LATE_COPIES = 3 * DEPTH * len(LATE_WEIGHTS)


def _gather_start(fulls, after):
    n = len(fulls)

    def body(*refs):
        ins = refs[:n]
        send_sems, recv_sems = refs[n + 1:n + 3]
        token = refs[-1]
        for sent, _ in _late_copies(ins, send_sems, recv_sems):
            sent.start()
        token[...] = jnp.zeros_like(token)

    outs = pl.pallas_call(
        body, name="gather_start",
        out_shape=(pltpu.SemaphoreType.DMA((N_LATE_COPIES,)), pltpu.SemaphoreType.DMA((N_LATE_COPIES,)))
        + tuple(pltpu.HBM(f.shape, f.dtype) for f in fulls) + (jax.ShapeDtypeStruct((SUBLANES, LANES), f32),),
        in_specs=(HBM,) * n + (ANY,),
        out_specs=(SEM, SEM) + (HBM,) * n + (pl.BlockSpec(memory_space=pltpu.VMEM),),
        input_output_aliases={t: 2 + t for t in range(n)},
        compiler_params=pltpu.CompilerParams(has_side_effects=pltpu.SideEffectType.DATAFLOW_SIDE_EFFECTING),
    )(*[pltpu.with_memory_space_constraint(f, pltpu.HBM) for f in fulls], after)
    return outs[0], outs[1], outs[2:2 + n], outs[-1]


def _gather_wait(send_sems, recv_sems, fulls, after):
    n = len(fulls)

    def body(*refs):
        ins = refs[:n]
        send_ref, recv_ref = refs[n:n + 2]
        for sent, landed in _late_copies(ins, send_ref, recv_ref):
            sent.wait_send()
            landed.wait_recv()

    return pl.pallas_call(
        body, name="gather_wait", out_shape=tuple(pltpu.HBM(f.shape, f.dtype) for f in fulls),
        in_specs=(HBM,) * n + (SEM, SEM, ANY), out_specs=(HBM,) * n,
        input_output_aliases={t: t for t in range(n)},
        compiler_params=pltpu.CompilerParams(has_side_effects=pltpu.SideEffectType.DATAFLOW_SIDE_EFFECTING),
    )(*fulls, send_sems, recv_sems, after)


def _half(ref, name, c):
    K, N, ax = BIG[name]
    if ax == 1:
        return ref.at[pl.ds(pl.multiple_of(c * (K // 2), 8), K // 2), :]
    return ref.at[:, pl.ds(pl.multiple_of(c * (N // 2), 128), N // 2)]


def _half_shape(name):
    K, N, ax = BIG[name]
    return (K // 2, N) if ax == 1 else (K, N // 2)


def _shard_of_half(ref, name, q):
    K, N, ax = BIG[name]
    if ax == 1:
        sz = N // N_CHIPS
        return ref.at[:, pl.ds(pl.multiple_of(q * sz, 128), sz)]
    sz = K // N_CHIPS
    return ref.at[pl.ds(pl.multiple_of(q * sz, 16), sz), :]


def _shard_half_shape(name):
    K, N, ax = BIG[name]
    return (K // 2, N // N_CHIPS) if ax == 1 else (K // N_CHIPS, N // 2)


def _shard_shape(name):
    K, N, ax = BIG[name]
    return (K, N // N_CHIPS) if ax == 1 else (K // N_CHIPS, N)


def _pair_exchange(tensors):
    n_t = len(tensors)

    def body(*refs):
        ins = refs[:n_t]
        outs = refs[n_t:2 * n_t]
        send_sems, recv_sems = refs[2 * n_t:]
        x, y, c = _position()
        cps = []
        for t, (name, _) in enumerate(tensors):
            cp = pltpu.make_async_remote_copy(
                src_ref=_half(ins[t], name, 1 - c), dst_ref=outs[t], send_sem=send_sems.at[t],
                recv_sem=recv_sems.at[t], device_id=(x, y, 1 - c), device_id_type=MESH_ID)
            cp.start()
            cps.append(cp)
        for cp in cps:
            cp.wait()

    return pl.pallas_call(
        body, name="grad_pair_exchange",
        out_shape=tuple(jax.ShapeDtypeStruct(_half_shape(n), f32) for n, _ in tensors),
        in_specs=[ANY] * n_t, out_specs=tuple([ANY] * n_t),
        scratch_shapes=[pltpu.SemaphoreType.DMA((n_t,)), pltpu.SemaphoreType.DMA((n_t,))],
    )(*[g for _, g in tensors])


def _pair_add(g, rcv, name, c_arr):
    K, N, ax = BIG[name]
    hr, hc = _half_shape(name)
    T = 128
    nrt = hr // T

    def body(c_ref, g_ref, r_ref, o_ref):
        o_ref[...] = (g_ref[...] + r_ref[...]).astype(bf16)

    if ax == 1:
        g_spec = pl.BlockSpec((T, hc), lambda i, c: (c[0] * nrt + i, 0))
    else:
        g_spec = pl.BlockSpec((T, hc), lambda i, c: (i, c[0]))
    plain = pl.BlockSpec((T, hc), lambda i, c: (i, 0))
    return pl.pallas_call(
        body, name="grad_pair_add", out_shape=jax.ShapeDtypeStruct((hr, hc), bf16),
        grid_spec=pltpu.PrefetchScalarGridSpec(num_scalar_prefetch=1, grid=(nrt,), in_specs=[g_spec, plain],
                                               out_specs=plain),
        compiler_params=_cp(("parallel",), 32),
    )(c_arr, g, rcv)


def _chip_copies(names, srcs, lands, send_sems, recv_sems):
    x, y, c = _position()
    me = 2 * x + y
    idx = 0
    for name, src, land in zip(names, srcs, lands):
        for px, py in _other_chips(x, y):
            def copy(q, row, name=name, src=src, land=land, px=px, py=py, idx=idx):
                return pltpu.make_async_remote_copy(
                    src_ref=_shard_of_half(src, name, q), dst_ref=land.at[row], send_sem=send_sems.at[idx],
                    recv_sem=recv_sems.at[idx], device_id=(px, py, c), device_id_type=MESH_ID)
            yield copy(2 * px + py, me), copy(me, 2 * px + py)
            idx += 1


def _chip_exchange_start(tag, tensors):
    names = [n for n, _ in tensors]
    n = len(tensors)
    lands = [lax.empty((N_CHIPS,) + _shard_half_shape(nm), g.dtype) for nm, g in tensors]

    def body(*refs):
        send_sems, recv_sems = refs[2 * n:2 * n + 2]
        for sent, _ in _chip_copies(names, refs[:n], refs[n:2 * n], send_sems, recv_sems):
            sent.start()
        refs[-1][...] = jnp.zeros_like(refs[-1])

    args = [g for _, g in tensors] + lands
    outs = pl.pallas_call(
        body, name="grad_chip_start_" + tag,
        out_shape=(pltpu.SemaphoreType.DMA((3 * n,)), pltpu.SemaphoreType.DMA((3 * n,)))
        + tuple(pltpu.HBM(a.shape, a.dtype) for a in args) + (jax.ShapeDtypeStruct((SUBLANES, LANES), f32),),
        in_specs=(HBM,) * (2 * n), out_specs=(SEM, SEM) + (HBM,) * (2 * n) + (pl.BlockSpec(memory_space=pltpu.VMEM),),
        input_output_aliases={t: 2 + t for t in range(2 * n)},
        compiler_params=pltpu.CompilerParams(has_side_effects=pltpu.SideEffectType.DATAFLOW_SIDE_EFFECTING),
    )(*[pltpu.with_memory_space_constraint(a, pltpu.HBM) for a in args])
    return (tag, names, outs[0], outs[1], outs[2:2 + 2 * n]), outs[-1]


def _chip_exchange_wait(state, after):
    tag, names, send_sems, recv_sems, bufs = state
    n = len(names)

    def body(*refs):
        for sent, landed in _chip_copies(names, refs[:n], refs[n:2 * n], refs[2 * n], refs[2 * n + 1]):
            sent.wait_send()
            landed.wait_recv()

    outs = pl.pallas_call(
        body, name="grad_chip_wait_" + tag, out_shape=tuple(pltpu.HBM(a.shape, a.dtype) for a in bufs),
        in_specs=(HBM,) * (2 * n) + (SEM, SEM, ANY), out_specs=(HBM,) * (2 * n),
        input_output_aliases={t: t for t in range(2 * n)},
        compiler_params=pltpu.CompilerParams(has_side_effects=pltpu.SideEffectType.DATAFLOW_SIDE_EFFECTING),
    )(*bufs, send_sems, recv_sems, after)
    return list(zip(names, outs[:n], outs[n:]))


def _sum_chips(name, half, land, chip_arr):
    K, N, ax = BIG[name]
    R, C = _shard_half_shape(name)
    T = 64
    nrt = R // T

    def body(p_ref, own_ref, land_ref, o_ref):
        parts = [jnp.where(p_ref[0] == q, own_ref[...], land_ref[q]).astype(f32) for q in range(N_CHIPS)]
        o_ref[...] = ((parts[0] + parts[1]) + parts[2]) + parts[3]

    if ax == 1:
        own_spec = pl.BlockSpec((T, C), lambda i, p: (i, p[0]))
    else:
        own_spec = pl.BlockSpec((T, C), lambda i, p: (p[0] * nrt + i, 0))
    return pl.pallas_call(
        body, name="grad_sum_chips", out_shape=jax.ShapeDtypeStruct((R, C), f32),
        grid_spec=pltpu.PrefetchScalarGridSpec(
            num_scalar_prefetch=1, grid=(nrt,),
            in_specs=[own_spec, pl.BlockSpec((N_CHIPS, T, C), lambda i, p: (0, i, 0))],
            out_specs=pl.BlockSpec((T, C), lambda i, p: (i, 0))),
        compiler_params=_cp(("parallel",), 32),
    )(chip_arr, half, land)


def _pair_swap(halves):
    n_t = len(halves)

    def body(*refs):
        ins = refs[:n_t]
        outs = refs[n_t:2 * n_t]
        send_sems, recv_sems = refs[2 * n_t:]
        x, y, c = _position()
        cps = []
        for t in range(n_t):
            cp = pltpu.make_async_remote_copy(
                src_ref=ins[t], dst_ref=outs[t], send_sem=send_sems.at[t], recv_sem=recv_sems.at[t],
                device_id=(x, y, 1 - c), device_id_type=MESH_ID)
            cp.start()
            cps.append(cp)
        for cp in cps:
            cp.wait()

    return pl.pallas_call(
        body, name="grad_pair_swap", out_shape=tuple(jax.ShapeDtypeStruct(h.shape, h.dtype) for h in halves),
        in_specs=[ANY] * n_t, out_specs=tuple([ANY] * n_t),
        scratch_shapes=[pltpu.SemaphoreType.DMA((n_t,)), pltpu.SemaphoreType.DMA((n_t,))],
    )(*halves)


def _adamw_halves(own, other, w, m, v, name, l, c_arr, prev):
    K, N, ax = BIG[name]
    R, C = _shard_shape(name)
    hr, hc = _shard_half_shape(name)
    T = 64
    nrt = hr // T
    c1 = 1.0 / (1.0 - ADAM_B1 ** ADAM_STEP)
    c2 = 1.0 / (1.0 - ADAM_B2 ** ADAM_STEP)

    def body(c_ref, own_ref, oth_ref, w_ref, m_ref, v_ref, *rest):
        g_ref, d_ref, nm_ref, nv_ref = rest[-4:]
        gg = jnp.where(pl.program_id(0) == c_ref[0], own_ref[...], oth_ref[...])
        nm = ADAM_B1 * m_ref[...] + (1.0 - ADAM_B1) * gg
        nv = ADAM_B2 * v_ref[...] + (1.0 - ADAM_B2) * (gg * gg)
        g_ref[...] = gg
        nm_ref[...] = nm
        nv_ref[...] = nv
        d_ref[...] = -ADAM_LR * ((nm * c1) / (jnp.sqrt(nv * c2) + ADAM_EPS) + ADAM_WD * w_ref[...])

    half = pl.BlockSpec((T, hc), lambda h, i, c: (i, 0))
    if ax == 1:
        full = pl.BlockSpec((None, T, hc), lambda h, i, c: (l, h * nrt + i, 0))
    else:
        full = pl.BlockSpec((None, T, hc), lambda h, i, c: (l, i, h))
    sd = jax.ShapeDtypeStruct((DEPTH, R, C), f32)
    args = [c_arr, own, other, w, m, v]
    in_specs = [half, half, full, full, full]
    aliases = {}
    if prev is not None:
        args += list(prev)
        in_specs += [ANY] * 4
        aliases = {6 + k: k for k in range(4)}
    return pl.pallas_call(
        body, name="adamw_" + name, out_shape=(sd, sd, sd, sd),
        grid_spec=pltpu.PrefetchScalarGridSpec(num_scalar_prefetch=1, grid=(2, nrt), in_specs=in_specs,
                                               out_specs=(full, full, full, full)),
        input_output_aliases=aliases,
        compiler_params=_cp(("arbitrary", "arbitrary"), 32),
    )(*args)


class _GradExchange:
    GROUPS = (("l1", tuple((n, DEPTH - 1) for n in BIG)),
              ("l0_ffn", (("ffn_w_down", 0), ("ffn_w_up", 0))),
              ("l0_mix", (("w_out", 0), ("w_in", 0))))

    def __init__(self):
        self.c_arr = jnp.reshape(lax.axis_index("c"), (1,)).astype(jnp.int32)
        self.chip_arr = jnp.reshape(2 * lax.axis_index("x") + lax.axis_index("y"), (1,)).astype(jnp.int32)
        self.grads = {}
        self.started = {}

    def put(self, name, layer, g, tok):
        self.grads[(name, layer)] = g
        for tag, keys in self.GROUPS:
            if tag in self.started or not all(k in self.grads for k in keys):
                continue
            tensors = [(n, self.grads[(n, l)]) for n, l in keys]
            received = _pair_exchange(tensors)
            pair = [(n, _pair_add(t, r, n, self.c_arr)) for (n, t), r in zip(tensors, received)]
            self.started[tag], token = _chip_exchange_start(tag, pair)
            tok = tok + token[0, 0]
        return tok

    def finish(self, after):
        keys, own = [], []
        for tag, group in self.GROUPS:
            landed = _chip_exchange_wait(self.started[tag], after)
            own += [_sum_chips(n, half, land, self.chip_arr) for n, half, land in landed]
            keys += list(group)
        other = _pair_swap(own)
        return dict(zip(keys, zip(own, other)))


def _small_allreduce(buf):
    R = buf.shape[0]
    n_dev = 8

    def body(in_ref, out_ref, slots, send_sems, recv_sems):
        x, y, c = _position()
        me = 4 * x + 2 * y + c
        slots[me] = in_ref[...]
        peers = []
        for k in range(1, n_dev):
            px = 1 - x if k & 4 else x
            py = 1 - y if k & 2 else y
            pc = 1 - c if k & 1 else c
            peers.append((px, py, pc))
        cps = []
        for k, peer in enumerate(peers):
            cp = pltpu.make_async_remote_copy(
                src_ref=in_ref, dst_ref=slots.at[me], send_sem=send_sems.at[k], recv_sem=recv_sems.at[k],
                device_id=peer, device_id_type=MESH_ID)
            cp.start()
            cps.append(cp)
        for k, (px, py, pc) in enumerate(peers):
            pltpu.make_async_remote_copy(
                src_ref=in_ref, dst_ref=slots.at[4 * px + 2 * py + pc], send_sem=send_sems.at[k],
                recv_sem=recv_sems.at[k], device_id=(px, py, pc), device_id_type=MESH_ID).wait_recv()
        for cp in cps:
            cp.wait_send()
        acc = slots[0]
        for dv in range(1, n_dev):
            acc = acc + slots[dv]
        out_ref[...] = acc

    vm = pl.BlockSpec(memory_space=pltpu.VMEM)
    return pl.pallas_call(
        body, name="small_allreduce", out_shape=jax.ShapeDtypeStruct((R, 128), f32), in_specs=[vm], out_specs=vm,
        scratch_shapes=[pltpu.VMEM((n_dev, R, 128), f32), pltpu.SemaphoreType.DMA((n_dev - 1,)),
                        pltpu.SemaphoreType.DMA((n_dev - 1,))],
        compiler_params=pltpu.CompilerParams(vmem_limit_bytes=40 * MIB),
    )(buf)


PACK_UNIT = 1024


def _pack(arrs):
    parts = []
    for a in arrs:
        flat = a.reshape(-1)
        n = -(-flat.shape[0] // PACK_UNIT) * PACK_UNIT
        parts.append(jnp.pad(flat, (0, n - flat.shape[0])))
    return jnp.concatenate(parts).reshape(-1, 128)


def _unpack(buf, shapes):
    flat = buf.reshape(-1)
    out, off = [], 0
    for shp in shapes:
        n = int(np.prod(shp))
        out.append(flat[off:off + n].reshape(shp))
        off += -(-n // PACK_UNIT) * PACK_UNIT
    return out


def kernel(x, w_in, b_in, conv_dw_w, conv_dw_b, conv_ln_g, conv_ln_b, rel_bias_table, gmlp_ln_g, gmlp_ln_b, gmlp_w_s, gmlp_b_s, w_out, b_out, ln1_g, ln1_b, ffn_w_up, ffn_b_up, ffn_conv_w, ffn_conv_b, ffn_w_down, ffn_b_down, ln2_g, ln2_b, loss_target, m_w_in, m_b_in, m_conv_dw_w, m_conv_dw_b, m_conv_ln_g, m_conv_ln_b, m_rel_bias_table, m_gmlp_ln_g, m_gmlp_ln_b, m_gmlp_w_s, m_gmlp_b_s, m_w_out, m_b_out, m_ln1_g, m_ln1_b, m_ffn_w_up, m_ffn_b_up, m_ffn_conv_w, m_ffn_conv_b, m_ffn_w_down, m_ffn_b_down, m_ln2_g, m_ln2_b, v_w_in, v_b_in, v_conv_dw_w, v_conv_dw_b, v_conv_ln_g, v_conv_ln_b, v_rel_bias_table, v_gmlp_ln_g, v_gmlp_ln_b, v_gmlp_w_s, v_gmlp_b_s, v_w_out, v_b_out, v_ln1_g, v_ln1_b, v_ffn_w_up, v_ffn_b_up, v_ffn_conv_w, v_ffn_conv_b, v_ffn_w_down, v_ffn_b_down, v_ln2_g, v_ln2_b):
    w = dict(w_in=w_in, b_in=b_in, conv_dw_w=conv_dw_w, conv_dw_b=conv_dw_b, conv_ln_g=conv_ln_g, conv_ln_b=conv_ln_b,
             rel_bias_table=rel_bias_table, gmlp_ln_g=gmlp_ln_g, gmlp_ln_b=gmlp_ln_b, gmlp_w_s=gmlp_w_s,
             gmlp_b_s=gmlp_b_s, w_out=w_out, b_out=b_out, ln1_g=ln1_g, ln1_b=ln1_b, ffn_w_up=ffn_w_up,
             ffn_b_up=ffn_b_up, ffn_conv_w=ffn_conv_w, ffn_conv_b=ffn_conv_b, ffn_w_down=ffn_w_down,
             ffn_b_down=ffn_b_down, ln2_g=ln2_g, ln2_b=ln2_b)
    m = dict(w_in=m_w_in, b_in=m_b_in, conv_dw_w=m_conv_dw_w, conv_dw_b=m_conv_dw_b, conv_ln_g=m_conv_ln_g,
             conv_ln_b=m_conv_ln_b, rel_bias_table=m_rel_bias_table, gmlp_ln_g=m_gmlp_ln_g, gmlp_ln_b=m_gmlp_ln_b,
             gmlp_w_s=m_gmlp_w_s, gmlp_b_s=m_gmlp_b_s, w_out=m_w_out, b_out=m_b_out, ln1_g=m_ln1_g, ln1_b=m_ln1_b,
             ffn_w_up=m_ffn_w_up, ffn_b_up=m_ffn_b_up, ffn_conv_w=m_ffn_conv_w, ffn_conv_b=m_ffn_conv_b,
             ffn_w_down=m_ffn_w_down, ffn_b_down=m_ffn_b_down, ln2_g=m_ln2_g, ln2_b=m_ln2_b)
    v = dict(w_in=v_w_in, b_in=v_b_in, conv_dw_w=v_conv_dw_w, conv_dw_b=v_conv_dw_b, conv_ln_g=v_conv_ln_g,
             conv_ln_b=v_conv_ln_b, rel_bias_table=v_rel_bias_table, gmlp_ln_g=v_gmlp_ln_g, gmlp_ln_b=v_gmlp_ln_b,
             gmlp_w_s=v_gmlp_w_s, gmlp_b_s=v_gmlp_b_s, w_out=v_w_out, b_out=v_b_out, ln1_g=v_ln1_g, ln1_b=v_ln1_b,
             ffn_w_up=v_ffn_w_up, ffn_b_up=v_ffn_b_up, ffn_conv_w=v_ffn_conv_w, ffn_conv_b=v_ffn_conv_b,
             ffn_w_down=v_ffn_w_down, ffn_b_down=v_ffn_b_down, ln2_g=v_ln2_g, ln2_b=v_ln2_b)

    chip_arr = jnp.reshape(2 * lax.axis_index("x") + lax.axis_index("y"), (1,)).astype(jnp.int32)
    shards = {"w_in": _cast_bf16(w_in.reshape(-1, w_in.shape[-1])).reshape(w_in.shape)}
    wb, conv_stack, fconv_stack = _gather_weights(shards, conv_dw_w, ffn_conv_w)
    send_sems, recv_sems, in_flight, token = _gather_start(
        [_cast_into_full(w[n], n, chip_arr) for n in LATE_WEIGHTS], conv_stack)
    sp = {n: w[n] for n in SMALL}
    sp["conv_dw_w"] = jnp.moveaxis(conv_stack, 0, 2).reshape(DEPTH, CONV_WIDTH, CONV_CH)
    sp["ffn_conv_w"] = jnp.moveaxis(fconv_stack, 0, 2).reshape(DEPTH, FFN_CONV_WIDTH, 2 * D_FF)
    sp["b_in"] = sp["b_in"] + token[0, 0]

    def late_weights(after):
        return dict(zip(LATE_WEIGHTS, _gather_wait(send_sems, recv_sems, in_flight, after)))

    sink = _GradExchange()
    loss_local, grad_x, grads, big = _local_step(x[0], loss_target[0], wb, late_weights, sp, sink)
    loss = lax.psum(loss_local, ("x", "y", "c"))

    small_shapes = [grads[n].shape for n in SMALL]
    small = dict(zip(SMALL, _unpack(_small_allreduce(_pack([grads[n] for n in SMALL])), small_shapes)))
    chip = 2 * lax.axis_index("x") + lax.axis_index("y")
    for n in SMALL_SHARDED:
        width = w[n].shape[-1]
        small[n] = lax.dynamic_slice_in_dim(small[n], chip * width, width, axis=2)

    g_out, d_out, m_out, v_out = {}, {}, {}, {}
    for n in BIG:
        outs = None
        for l in range(DEPTH):
            own, other = big[(n, l)]
            outs = _adamw_halves(own, other, w[n], m[n], v[n], n, l, sink.c_arr, outs)
        g_out[n], d_out[n], m_out[n], v_out[n] = outs
    shapes = [small[n].shape for n in SMALL]
    packed = [_pack([src[n] for n in SMALL]) for src in (small, w, m, v)]
    upd = _adamw(*packed, "adamw_small")
    for dst, buf in zip((d_out, m_out, v_out), upd):
        dst.update(zip(SMALL, _unpack(buf, shapes)))
    g_out.update(small)

    return (loss, grad_x[None], *[g_out[n] for n in WEIGHTS], *[d_out[n] for n in WEIGHTS],
            *[m_out[n] for n in WEIGHTS], *[v_out[n] for n in WEIGHTS])
```

```python
import functools
import math

import numpy as np
import jax
import jax.numpy as jnp
from jax import lax
from jax.experimental import pallas as pl
from jax.experimental.pallas import tpu as pltpu

f32 = jnp.float32
bf16 = jnp.bfloat16

D_MODEL = 1024
DEPTH = 2
HEAD_DIM = 64
CONV_CH = 256
CONV_WIDTH = 31
ATTN_HEADS = 8
ATTN_CH = ATTN_HEADS * HEAD_DIM
DILATIONS = (1, 4, 16)
ATTN_BLOCK = 128
N_BUCKETS = 32
MAX_DISTANCE = 2048
GMLP_CH = 256
GMLP_GROUPS = 4
GMLP_GROUP_DIM = GMLP_CH // GMLP_GROUPS
CHUNK = 128
IN_CH = 2 * CONV_CH + 3 * ATTN_CH + 2 * GMLP_CH
D_FF = 2816
FFN_CONV_WIDTH = 3
LN_EPS = 1e-5
ALPHA = (2.0 * DEPTH) ** 0.25
ADAM_LR = 0.001
ADAM_B1 = 0.9
ADAM_B2 = 0.999
ADAM_EPS = 1e-08
ADAM_WD = 0.01
ADAM_STEP = 10

CONV_HALO = 32
FFN_HALO = 8
NEG = -1e30
MIB = 2 ** 20
NT_DIMS = (((1,), (1,)), ((), ()))
TN_DIMS = (((0,), (0,)), ((), ()))
MESH_ID = pl.DeviceIdType.MESH


def _cp(sem, vmem_mib):
    return pltpu.CompilerParams(dimension_semantics=sem, vmem_limit_bytes=vmem_mib * MIB)


def _resident(shape):
    nd = len(shape)
    return pl.BlockSpec(shape, lambda *_: (0,) * nd, pipeline_mode=pl.Buffered(1))


def _acc(shape):
    nd = len(shape)
    return pl.BlockSpec(shape, lambda *_: (0,) * nd)


def _sig(x):
    return 1.0 / (1.0 + jnp.exp(-x))


def _ln_stats(z):
    mu = jnp.mean(z, axis=-1, keepdims=True)
    zc = z - mu
    var = jnp.mean(zc * zc, axis=-1, keepdims=True)
    rstd = lax.rsqrt(var + LN_EPS)
    return zc * rstd, rstd


def _ln_bwd(dy, xhat, rstd, g):
    dxh = dy * g
    m1 = jnp.mean(dxh, axis=-1, keepdims=True)
    m2 = jnp.mean(dxh * xhat, axis=-1, keepdims=True)
    return rstd * (dxh - m1 - xhat * m2)


def _colsum(x):
    return jnp.sum(x, axis=0, keepdims=True)


def _t5_bucket_np(dist):
    max_exact = N_BUCKETS // 2
    dd = np.maximum(dist, 1).astype(np.float64)
    large = max_exact + (np.log(dd / max_exact) / math.log(MAX_DISTANCE / max_exact)
                         * (N_BUCKETS - max_exact)).astype(np.int32)
    large = np.minimum(large, N_BUCKETS - 1)
    return np.where(dist < max_exact, dist, large).astype(np.int32)


def _bucket_ids():
    qi = np.arange(ATTN_BLOCK)[:, None]
    kj = np.arange(2 * ATTN_BLOCK)[None, :]
    dist = np.clip(qi + ATTN_BLOCK - kj, 0, None)
    return np.stack([_t5_bucket_np(dist * d) for d in DILATIONS]).astype(np.int32)


LANES = 128
QKV_CH = 3 * ATTN_CH
PERM_TILE = 512


def _slabs(n, rows):
    return [pltpu.VMEM((rows, LANES), f32)] * n


def _rows_of(slab, r, n, d):
    return slab[...] if d == 1 else slab[pl.ds(r, n, stride=d), :]


def _set_rows_of(slab, r, n, d, val):
    if d == 1:
        slab[...] = val
    else:
        slab[pl.ds(r, n, stride=d), :] = val


def _perm_spec(d, ch):
    return pl.BlockSpec((d, PERM_TILE // d, ch), lambda i: (0, i, 0))


def _perm_shape(S, d, ch, dtype):
    return jax.ShapeDtypeStruct((d, S // d, ch), dtype)


def _inproj_fwd(x, w, b):
    S = x.shape[0]
    T = PERM_TILE
    nsl = QKV_CH // LANES

    def body(x_ref, w_ref, b_ref, a_ref, c_ref, *rest):
        q_refs = rest[:len(DILATIONS)]
        slabs = rest[len(DILATIONS):]
        h = jnp.dot(x_ref[...].astype(bf16), w_ref[...], preferred_element_type=f32) + b_ref[...]
        a_ref[...] = h[:, :2 * CONV_CH]
        q0 = 2 * CONV_CH
        c_ref[...] = h[:, q0 + QKV_CH:]
        for j in range(nsl):
            piece = h[:, q0 + LANES * j:q0 + LANES * (j + 1)]
            if LANES * j < ATTN_CH:
                piece = piece * (HEAD_DIM ** -0.5)
            slabs[j][...] = piece
        for d, q_ref in zip(DILATIONS, q_refs):
            for r in range(d):
                for j in range(nsl):
                    q_ref[r, :, LANES * j:LANES * (j + 1)] = _rows_of(slabs[j], r, T // d, d).astype(bf16)

    row = lambda c: pl.BlockSpec((T, c), lambda i: (i, 0))
    return pl.pallas_call(
        body, grid=(S // T,), name="inproj_fwd",
        out_shape=(jax.ShapeDtypeStruct((S, 2 * CONV_CH), f32), jax.ShapeDtypeStruct((S, 2 * GMLP_CH), f32))
        + tuple(_perm_shape(S, d, QKV_CH, bf16) for d in DILATIONS),
        in_specs=[row(D_MODEL), _resident((D_MODEL, IN_CH)), _resident((1, IN_CH))],
        out_specs=(row(2 * CONV_CH), row(2 * GMLP_CH)) + tuple(_perm_spec(d, QKV_CH) for d in DILATIONS),
        scratch_shapes=_slabs(nsl, T),
        compiler_params=_cp(("parallel",), 48),
    )(x, w, b)


CONV_GROUP = 64


def _window_rolls(starts):
    groups = {}
    for s in starts:
        groups.setdefault((-s) % SUBLANES, []).append(s)
    return dict(sorted(groups.items()))


def _conv_fwd(a_in, dw_w, dw_b, ln_g, ln_b):
    S = a_in.shape[0]
    T = 512
    hb = T // CONV_HALO

    def body(a_ref, halo_ref, w_ref, b_ref, g_ref, be_ref, out_ref, hc_ref, buf):
        i = pl.program_id(0)
        am = a_ref[...]
        ah = halo_ref[...]
        hgh = ah[:, :CONV_CH] * _sig(ah[:, CONV_CH:])
        buf[0:CONV_HALO, :] = jnp.where(i > 0, hgh, 0.0)
        buf[CONV_HALO:, :] = am[:, :CONV_CH] * _sig(am[:, CONV_CH:])
        starts = _window_rolls(range(CONV_HALO - (CONV_WIDTH - 1), CONV_HALO + 1))
        slabs = [slice(LANES * j, LANES * (j + 1)) for j in range(CONV_CH // LANES)]

        def step(g, _):
            r0 = pl.multiple_of(g * CONV_GROUP, CONV_GROUP)
            rows = pl.ds(r0, CONV_GROUP)
            for cs in slabs:
                ext = buf[pl.ds(r0, CONV_GROUP + CONV_HALO), cs]
                acc = jnp.broadcast_to(b_ref[:, cs], (CONV_GROUP, LANES))
                for b, ss in starts.items():
                    rolled = ext if b == 0 else pltpu.roll(ext, b, 0)
                    for s in ss:
                        k = s - (CONV_HALO - (CONV_WIDTH - 1))
                        acc = acc + w_ref[k:k + 1, cs] * rolled[s + b:s + b + CONV_GROUP]
                hc_ref[rows, cs] = acc
            return 0

        lax.fori_loop(0, T // CONV_GROUP, step, 0)
        xhat, _ = _ln_stats(hc_ref[...])
        y = xhat * g_ref[...] + be_ref[...]
        out_ref[...] = (y * _sig(y)).astype(bf16)

    return pl.pallas_call(
        body, grid=(S // T,), name="conv_fwd",
        out_shape=(jax.ShapeDtypeStruct((S, CONV_CH), bf16), jax.ShapeDtypeStruct((S, CONV_CH), f32)),
        in_specs=[pl.BlockSpec((T, 2 * CONV_CH), lambda i: (i, 0)),
                  pl.BlockSpec((CONV_HALO, 2 * CONV_CH), lambda i: (jnp.maximum(i * hb - 1, 0), 0)),
                  _acc((32, CONV_CH)), _acc((1, CONV_CH)), _acc((1, CONV_CH)), _acc((1, CONV_CH))],
        out_specs=(pl.BlockSpec((T, CONV_CH), lambda i: (i, 0)), pl.BlockSpec((T, CONV_CH), lambda i: (i, 0))),
        scratch_shapes=[pltpu.VMEM((T + CONV_HALO, CONV_CH), f32)],
        compiler_params=_cp(("parallel",), 32),
    )(a_in, a_in, dw_w, dw_b, ln_g, ln_b)


def _bias_build(table, buckets):
    def body(t_ref, bk_ref, o_ref):
        h = pl.program_id(1)
        ids = bk_ref[0]
        acc = jnp.zeros((ATTN_BLOCK, 2 * ATTN_BLOCK), f32)
        for b in range(N_BUCKETS):
            acc = jnp.where(ids == b, t_ref[b, h], acc)
        o_ref[0, 0] = acc

    return pl.pallas_call(
        body, grid=(len(DILATIONS), ATTN_HEADS), name="bias_build",
        out_shape=jax.ShapeDtypeStruct((len(DILATIONS), ATTN_HEADS, ATTN_BLOCK, 2 * ATTN_BLOCK), f32),
        in_specs=[pl.BlockSpec(memory_space=pltpu.SMEM),
                  pl.BlockSpec((1, ATTN_BLOCK, 2 * ATTN_BLOCK), lambda p, h: (p, 0, 0))],
        out_specs=pl.BlockSpec((1, 1, ATTN_BLOCK, 2 * ATTN_BLOCK), lambda p, h: (p, h, 0, 0)),
        compiler_params=_cp(("arbitrary", "arbitrary"), 16),
    )(table, buckets)


def _head_tile(tile, h, col):
    lane_head = lax.broadcasted_iota(jnp.int32, tile.shape, 1) // 16
    return jnp.where(lane_head == h, col, tile)


HEAD_PAIRS = ATTN_HEADS // 2
UNITS_PER_BLOCK = ATTN_HEADS


def _attn_tile(L):
    return min(512, L)


def _band_mask(first_block, n):
    B = ATTN_BLOCK
    row = lax.broadcasted_iota(jnp.int32, (B, 2 * B), 0)
    col = lax.broadcasted_iota(jnp.int32, (B, 2 * B), 1)
    valid = (col >= row) & (col <= row + B)
    if first_block:
        valid = valid & ((col >= B) | (n > 0))
    return valid


def _head_lanes(a):
    lane = lax.broadcasted_iota(jnp.int32, (ATTN_BLOCK, LANES), 1)
    return (lane < HEAD_DIM) if a == 0 else (lane >= HEAD_DIM)


def _pair_keys(cur_ref, halo_ref, part, b, j):
    B = ATTN_BLOCK
    c0 = part * ATTN_CH + LANES * j
    own = cur_ref[B * b:B * (b + 1), c0:c0 + LANES]
    prev = halo_ref[:, LANES * j:LANES * (j + 1)] if b == 0 else cur_ref[B * (b - 1):B * b, c0:c0 + LANES]
    return jnp.concatenate([prev, own], axis=0)


def _attn_fwd_pattern(qkv, bias, d):
    _, L, _ = qkv.shape
    B = ATTN_BLOCK
    QB = _attn_tile(L)
    nsb = QB // B
    U = nsb * UNITS_PER_BLOCK

    def body(cur_ref, hk_ref, hv_ref, b_ref, o_ref, lse_ref, lg, pb):
        n = pl.program_id(1)
        for b in range(nsb):
            valid = _band_mask(b == 0, n)
            for j in range(HEAD_PAIRS):
                q2 = cur_ref[B * b:B * (b + 1), LANES * j:LANES * (j + 1)]
                k2 = _pair_keys(cur_ref, hk_ref, 1, b, j)
                for a in range(2):
                    u = (b * HEAD_PAIRS + j) * 2 + a
                    qm = jnp.where(_head_lanes(a), q2, jnp.zeros_like(q2))
                    logits = lax.dot_general(qm, k2, NT_DIMS, preferred_element_type=f32) + b_ref[2 * j + a]
                    lg[B * u:B * (u + 1), :] = jnp.where(valid, logits, NEG)
        m = jnp.max(lg[...], axis=1, keepdims=True)
        p = jnp.exp(lg[...] - m)
        s = jnp.sum(p, axis=1, keepdims=True)
        pb[...] = p.astype(bf16)
        lse = m + jnp.log(s)
        inv = 1.0 / s
        for b in range(nsb):
            tile = jnp.zeros((B, B), f32)
            for j in range(HEAD_PAIRS):
                v2 = _pair_keys(cur_ref, hv_ref, 2, b, j)
                outs = []
                for a in range(2):
                    u = (b * HEAD_PAIRS + j) * 2 + a
                    rows = slice(B * u, B * (u + 1))
                    outs.append(jnp.dot(pb[rows, :], v2, preferred_element_type=f32) * inv[rows])
                    tile = _head_tile(tile, 2 * j + a, lse[rows])
                o_ref[B * b:B * (b + 1), LANES * j:LANES * (j + 1)] = jnp.where(_head_lanes(0), outs[0], outs[1])
            lse_ref[B * b:B * (b + 1), :] = tile

    halo = lambda part: pl.BlockSpec((None, B, ATTN_CH), lambda r, n: (r, jnp.maximum(n * nsb - 1, 0), part))
    tile_spec = lambda c: pl.BlockSpec((None, QB, c), lambda r, n: (r, n, 0))
    return pl.pallas_call(
        body, grid=(d, L // QB), name=f"attn_fwd_d{d}",
        out_shape=(jax.ShapeDtypeStruct((d, L, ATTN_CH), f32), jax.ShapeDtypeStruct((d, L, B), f32)),
        in_specs=[tile_spec(QKV_CH), halo(1), halo(2), _resident((ATTN_HEADS, B, 2 * B))],
        out_specs=(tile_spec(ATTN_CH), tile_spec(B)),
        scratch_shapes=[pltpu.VMEM((U * B, 2 * B), f32), pltpu.VMEM((U * B, 2 * B), bf16)],
        compiler_params=_cp(("parallel", "parallel"), 40),
    )(qkv, qkv, qkv, bias)


def _attn_merge(parts):
    S = parts[0][0].shape[0] * parts[0][0].shape[1]
    T = PERM_TILE
    nsl = ATTN_CH // LANES
    n_p = len(DILATIONS)

    def body(*refs):
        ins = refs[:2 * n_p]
        out_ref, lse_ref = refs[2 * n_p:2 * n_p + 2]
        slabs = refs[2 * n_p + 2:]
        lses = []
        for p, d in enumerate(DILATIONS):
            o_ref, l_ref = ins[2 * p], ins[2 * p + 1]
            osl = slabs[p * (nsl + 1):p * (nsl + 1) + nsl]
            lsl = slabs[p * (nsl + 1) + nsl]
            for r in range(d):
                for j in range(nsl):
                    _set_rows_of(osl[j], r, T // d, d, o_ref[r, :, LANES * j:LANES * (j + 1)])
                _set_rows_of(lsl, r, T // d, d, l_ref[r])
            lses.append(lsl[...])
        big = functools.reduce(jnp.maximum, lses)
        ws = [jnp.exp(l - big) for l in lses]
        tot = functools.reduce(lambda a_, b_: a_ + b_, ws)
        lse_ref[...] = big + jnp.log(tot)
        ws = [w / tot for w in ws]
        for j in range(nsl):
            acc = jnp.zeros((T, LANES), f32)
            for p in range(n_p):
                wa = ws[p][:, 32 * j:32 * j + 1]
                wb = ws[p][:, 32 * j + 16:32 * j + 17]
                lane = lax.broadcasted_iota(jnp.int32, (T, LANES), 1)
                acc = acc + jnp.where(lane < HEAD_DIM, wa, wb) * slabs[p * (nsl + 1) + j][...]
            out_ref[:, LANES * j:LANES * (j + 1)] = acc.astype(bf16)

    in_specs, args = [], []
    for (o, l), d in zip(parts, DILATIONS):
        in_specs += [_perm_spec(d, ATTN_CH), _perm_spec(d, ATTN_BLOCK)]
        args += [o, l]
    row = lambda c: pl.BlockSpec((T, c), lambda i: (i, 0))
    return pl.pallas_call(
        body, grid=(S // T,), name="attn_merge",
        out_shape=(jax.ShapeDtypeStruct((S, ATTN_CH), bf16), jax.ShapeDtypeStruct((S, ATTN_BLOCK), f32)),
        in_specs=in_specs, out_specs=(row(ATTN_CH), row(ATTN_BLOCK)),
        scratch_shapes=_slabs(n_p * (nsl + 1), T),
        compiler_params=_cp(("parallel",), 40),
    )(*args)


def _attn_fwd(qkvs, bias):
    parts = [_attn_fwd_pattern(q, bias[p], d) for p, (q, d) in enumerate(zip(qkvs, DILATIONS))]
    return _attn_merge(parts)


def _tril_bf16(w):
    row = lax.broadcasted_iota(jnp.int32, (CHUNK, CHUNK), 0)
    col = lax.broadcasted_iota(jnp.int32, (CHUNK, CHUNK), 1)
    return jnp.where(col <= row, w, 0.0).astype(bf16)


def _gmlp_fwd(c_in, ln_g, ln_b, w_s, b_s_t):
    S = c_in.shape[0]
    T = 512

    def body(c_ref, g_ref, be_ref, w_ref, bs_ref, out_ref, mix):
        c = c_ref[...]
        xhat, _ = _ln_stats(c[:, GMLP_CH:])
        vb = (xhat * g_ref[...] + be_ref[...]).astype(bf16)
        for g in range(GMLP_GROUPS):
            wt = _tril_bf16(w_ref[g])
            cs = slice(GMLP_GROUP_DIM * g, GMLP_GROUP_DIM * (g + 1))
            for ci in range(T // CHUNK):
                rs = slice(CHUNK * ci, CHUNK * (ci + 1))
                mix[rs, cs] = jnp.dot(wt, vb[rs, cs], preferred_element_type=f32) + bs_ref[:, g:g + 1]
        out_ref[...] = (c[:, :GMLP_CH] * mix[...]).astype(bf16)

    return pl.pallas_call(
        body, grid=(S // T,), name="gmlp_fwd",
        out_shape=jax.ShapeDtypeStruct((S, GMLP_CH), bf16),
        in_specs=[pl.BlockSpec((T, 2 * GMLP_CH), lambda i: (i, 0)), _acc((1, GMLP_CH)), _acc((1, GMLP_CH)),
                  _acc((GMLP_GROUPS, CHUNK, CHUNK)), _acc((CHUNK, GMLP_GROUPS))],
        out_specs=pl.BlockSpec((T, GMLP_CH), lambda i: (i, 0)),
        scratch_shapes=[pltpu.VMEM((T, GMLP_CH), f32)],
        compiler_params=_cp(("parallel",), 32),
    )(c_in, ln_g, ln_b, w_s, b_s_t)


def _outproj_ln_fwd(conv_out, attn_out, gm_out, w, b, x, ln_g, ln_b):
    S = x.shape[0]
    T = 512

    def body(co_ref, ao_ref, go_ref, w_ref, b_ref, x_ref, g_ref, be_ref, cat_ref, z_ref, y_ref, yb_ref):
        cat = jnp.concatenate([co_ref[...], ao_ref[...], go_ref[...]], axis=1)
        cat_ref[...] = cat
        z = jnp.dot(cat, w_ref[...], preferred_element_type=f32) + b_ref[...] + ALPHA * x_ref[...]
        z_ref[...] = z
        xhat, _ = _ln_stats(z)
        y = xhat * g_ref[...] + be_ref[...]
        y_ref[...] = y
        yb_ref[...] = y.astype(bf16)

    row = lambda c: pl.BlockSpec((T, c), lambda i: (i, 0))
    return pl.pallas_call(
        body, grid=(S // T,), name="outproj_ln_fwd",
        out_shape=(jax.ShapeDtypeStruct((S, D_MODEL), bf16), jax.ShapeDtypeStruct((S, D_MODEL), f32),
                   jax.ShapeDtypeStruct((S, D_MODEL), f32), jax.ShapeDtypeStruct((S, D_MODEL), bf16)),
        in_specs=[row(CONV_CH), row(ATTN_CH), row(GMLP_CH), _resident((D_MODEL, D_MODEL)), _acc((1, D_MODEL)),
                  row(D_MODEL), _acc((1, D_MODEL)), _acc((1, D_MODEL))],
        out_specs=(row(D_MODEL), row(D_MODEL), row(D_MODEL), row(D_MODEL)),
        compiler_params=_cp(("parallel",), 40),
    )(conv_out, attn_out, gm_out, w, b, x, ln_g, ln_b)


GATE_ROWS = 32
GATE_COLS = 128
GATE_MM_COLS = 256
SUBLANES = 8


def _gate_cols(c0):
    return slice(c0, c0 + GATE_COLS), slice(D_FF + c0, D_FF + c0 + GATE_COLS)


def _bcast_rows(ref, k, cs):
    return jnp.broadcast_to(ref[k:k + 1, cs], (GATE_ROWS, GATE_COLS))


def _fold_rows(z):
    acc = z[0:SUBLANES]
    for r in range(SUBLANES, GATE_ROWS, SUBLANES):
        acc = acc + z[r:r + SUBLANES]
    return acc


def _ffn_up_gate_fwd(x1b, w, b, conv_w, conv_b):
    S = x1b.shape[0]
    T = 256
    H = FFN_HALO
    K = FFN_CONV_WIDTH

    def body(x_ref, w_ref, b_ref, cw_ref, cb_ref, hfb_ref, hc_ref, act_ref, hbuf, carry):
        @pl.when(pl.program_id(0) == 0)
        def _():
            carry[...] = jnp.zeros_like(carry)
        x = x_ref[...]
        for m0 in range(0, D_FF, GATE_MM_COLS):
            for cm in (slice(m0, m0 + GATE_MM_COLS), slice(D_FF + m0, D_FF + m0 + GATE_MM_COLS)):
                h = jnp.dot(x, w_ref[:, cm], preferred_element_type=f32) + b_ref[:, cm]
                hbuf[:, cm] = h
                hfb_ref[:, cm] = h.astype(bf16)
            for c0 in range(m0, m0 + GATE_MM_COLS, GATE_COLS):
                cols = _gate_cols(c0)
                wts = [[_bcast_rows(cw_ref, k, cs) for k in range(K)] + [_bcast_rows(cb_ref, 0, cs)] for cs in cols]

                def step(rg, tails, cols=cols, wts=wts):
                    rows = pl.ds(pl.multiple_of(rg * GATE_ROWS, GATE_ROWS), GATE_ROWS)
                    hc, new_tails = [], []
                    for cs, wt, tail in zip(cols, wts, tails):
                        h = hbuf[rows, cs]
                        ext = jnp.concatenate([tail, h], axis=0)
                        acc = wt[K] + wt[K - 1] * h
                        for back in range(1, K):
                            acc = acc + wt[K - 1 - back] * pltpu.roll(ext, back, 0)[H:]
                        hc_ref[rows, cs] = acc
                        hc.append(acc)
                        new_tails.append(h[GATE_ROWS - H:])
                    act_ref[rows, cols[0]] = (hc[0] * _sig(hc[0]) * hc[1]).astype(bf16)
                    return tuple(new_tails)

                tails = lax.fori_loop(0, T // GATE_ROWS, step, tuple(carry[:, cs] for cs in cols), unroll=True)
                for cs, tail in zip(cols, tails):
                    carry[:, cs] = tail

    row = lambda c: pl.BlockSpec((T, c), lambda i: (i, 0))
    return pl.pallas_call(
        body, grid=(S // T,), name="ffn_up_gate_fwd",
        out_shape=(jax.ShapeDtypeStruct((S, 2 * D_FF), bf16), jax.ShapeDtypeStruct((S, 2 * D_FF), f32),
                   jax.ShapeDtypeStruct((S, D_FF), bf16)),
        in_specs=[row(D_MODEL), _resident((D_MODEL, 2 * D_FF)), _acc((1, 2 * D_FF)), _acc((8, 2 * D_FF)),
                  _acc((1, 2 * D_FF))],
        out_specs=(row(2 * D_FF), row(2 * D_FF), row(D_FF)),
        scratch_shapes=[pltpu.VMEM((T, 2 * D_FF), f32), pltpu.VMEM((H, 2 * D_FF), f32)],
        compiler_params=_cp(("arbitrary",), 56),
    )(x1b, w, b, conv_w, conv_b)


def _ffn_down_ln_fwd(act, w, b, x1, ln_g, ln_b):
    S = act.shape[0]
    T = 512

    def body(a_ref, w_ref, b_ref, x_ref, g_ref, be_ref, z_ref, y_ref):
        z = jnp.dot(a_ref[...], w_ref[...], preferred_element_type=f32) + b_ref[...] + ALPHA * x_ref[...]
        z_ref[...] = z
        xhat, _ = _ln_stats(z)
        y_ref[...] = xhat * g_ref[...] + be_ref[...]

    row = lambda c: pl.BlockSpec((T, c), lambda i: (i, 0))
    return pl.pallas_call(
        body, grid=(S // T,), name="ffn_down_ln_fwd",
        out_shape=(jax.ShapeDtypeStruct((S, D_MODEL), f32), jax.ShapeDtypeStruct((S, D_MODEL), f32)),
        in_specs=[row(D_FF), _resident((D_FF, D_MODEL)), _acc((1, D_MODEL)), row(D_MODEL), _acc((1, D_MODEL)),
                  _acc((1, D_MODEL))],
        out_specs=(row(D_MODEL), row(D_MODEL)),
        compiler_params=_cp(("parallel",), 40),
    )(act, w, b, x1, ln_g, ln_b)


def _ffn_down_ln_loss(act, w, b, x1, ln_g, ln_b, target):
    S = act.shape[0]
    T = 512

    def body(a_ref, w_ref, b_ref, x_ref, g_ref, be_ref, t_ref, dz_ref, dzb_ref, loss_ref, dg_ref, db_ref):
        @pl.when(pl.program_id(0) == 0)
        def _():
            loss_ref[...] = jnp.zeros_like(loss_ref)
            dg_ref[...] = jnp.zeros_like(dg_ref)
            db_ref[...] = jnp.zeros_like(db_ref)
        z = jnp.dot(a_ref[...], w_ref[...], preferred_element_type=f32) + b_ref[...] + ALPHA * x_ref[...]
        xhat, rstd = _ln_stats(z)
        err = xhat * g_ref[...] + be_ref[...] - t_ref[...]
        loss_ref[...] += _colsum(err * err) * (0.5 / D_MODEL)
        dy = err * (1.0 / D_MODEL)
        dz = _ln_bwd(dy, xhat, rstd, g_ref[...])
        dz_ref[...] = dz
        dzb_ref[...] = dz.astype(bf16)
        dg_ref[...] += _colsum(dy * xhat)
        db_ref[...] += _colsum(dy)

    row = lambda c: pl.BlockSpec((T, c), lambda i: (i, 0))
    vec = jax.ShapeDtypeStruct((1, D_MODEL), f32)
    return pl.pallas_call(
        body, grid=(S // T,), name="ffn_down_ln_loss",
        out_shape=(jax.ShapeDtypeStruct((S, D_MODEL), f32), jax.ShapeDtypeStruct((S, D_MODEL), bf16), vec, vec, vec),
        in_specs=[row(D_FF), _resident((D_FF, D_MODEL)), _acc((1, D_MODEL)), row(D_MODEL), _acc((1, D_MODEL)),
                  _acc((1, D_MODEL)), row(D_MODEL)],
        out_specs=(row(D_MODEL), row(D_MODEL), _acc((1, D_MODEL)), _acc((1, D_MODEL)), _acc((1, D_MODEL))),
        compiler_params=_cp(("arbitrary",), 40),
    )(act, w, b, x1, ln_g, ln_b, target)


def _dgrad_ln_bwd(g, w, dz_res, z, ln_g, name):
    S, K = g.shape
    T = 256
    with_ln = z is not None

    def body(*refs):
        if with_ln:
            g_ref, w_ref, r_ref, z_ref, lg_ref, dz_ref, dzb_ref, dg_ref, db_ref = refs
        else:
            g_ref, w_ref, r_ref, dx_ref = refs
        dx = lax.dot_general(g_ref[...], w_ref[...], NT_DIMS, preferred_element_type=f32) + ALPHA * r_ref[...]
        if not with_ln:
            dx_ref[...] = dx
            return

        @pl.when(pl.program_id(0) == 0)
        def _():
            dg_ref[...] = jnp.zeros_like(dg_ref)
            db_ref[...] = jnp.zeros_like(db_ref)
        xhat, rstd = _ln_stats(z_ref[...])
        dz = _ln_bwd(dx, xhat, rstd, lg_ref[...])
        dz_ref[...] = dz
        dzb_ref[...] = dz.astype(bf16)
        dg_ref[...] += _colsum(dx * xhat)
        db_ref[...] += _colsum(dx)

    row = pl.BlockSpec((T, D_MODEL), lambda i: (i, 0))
    vec = jax.ShapeDtypeStruct((1, D_MODEL), f32)
    in_specs = [pl.BlockSpec((T, K), lambda i: (i, 0)), _resident((D_MODEL, K)), row]
    args = [g, w, dz_res]
    if with_ln:
        in_specs += [row, _acc((1, D_MODEL))]
        args += [z, ln_g]
        out_shape = (jax.ShapeDtypeStruct((S, D_MODEL), f32), jax.ShapeDtypeStruct((S, D_MODEL), bf16), vec, vec)
        out_specs = (row, row, _acc((1, D_MODEL)), _acc((1, D_MODEL)))
    else:
        out_shape = jax.ShapeDtypeStruct((S, D_MODEL), f32)
        out_specs = row
    return pl.pallas_call(
        body, grid=(S // T,), name=name, out_shape=out_shape, in_specs=in_specs, out_specs=out_specs,
        compiler_params=_cp(("arbitrary",), 48),
    )(*args)


def _ffn_down_gate_bwd(dzb, w_down, hfb, hc, conv_w):
    S = hc.shape[0]
    T = 256
    H = FFN_HALO
    nt = S // T
    K = FFN_CONV_WIDTH

    def body(dz_ref, w_ref, h_ref, hc_ref, cw_ref, dh_ref, dw_ref, dcb_ref, da_buf, carry):
        @pl.when(pl.program_id(0) == 0)
        def _():
            dw_ref[...] = jnp.zeros_like(dw_ref)
            dcb_ref[...] = jnp.zeros_like(dcb_ref)
            carry[...] = jnp.zeros_like(carry)
        da_buf[...] = lax.dot_general(dz_ref[...], w_ref[...], NT_DIMS, preferred_element_type=f32)
        ngroups = T // GATE_ROWS
        for c0 in range(0, D_FF, GATE_COLS):
            cols = _gate_cols(c0)
            wts = [[_bcast_rows(cw_ref, k, cs) for k in range(K)] for cs in cols]

            def step(it, state, cols=cols, wts=wts):
                heads, accs = state
                rows = pl.ds(pl.multiple_of((ngroups - 1 - it) * GATE_ROWS, GATE_ROWS), GATE_ROWS)
                g = hc_ref[rows, cols[0]]
                v = hc_ref[rows, cols[1]]
                da = da_buf[rows, cols[0]]
                sg = _sig(g)
                dms = (da * v * (sg * (1.0 + g * (1.0 - sg))), da * (g * sg))
                new_heads, new_accs = [], []
                for cs, wt, dm, head, acc in zip(cols, wts, dms, heads, accs):
                    h0 = h_ref[rows, cs].astype(f32)
                    ext = jnp.concatenate([dm, head], axis=0)
                    dh = wt[K - 1] * dm
                    acc_k = [None] * K + [acc[K] + _fold_rows(dm)]
                    acc_k[K - 1] = acc[K - 1] + _fold_rows(dm * h0)
                    for ahead in range(1, K):
                        dk = pltpu.roll(ext, GATE_ROWS + H - ahead, 0)[:GATE_ROWS]
                        dh = dh + wt[K - 1 - ahead] * dk
                        acc_k[K - 1 - ahead] = acc[K - 1 - ahead] + _fold_rows(dk * h0)
                    dh_ref[rows, cs] = dh.astype(bf16)
                    new_heads.append(dm[:H])
                    new_accs.append(tuple(acc_k))
                return tuple(new_heads), tuple(new_accs)

            zero = jnp.zeros((SUBLANES, GATE_COLS), f32)
            init = (tuple(carry[:, cs] for cs in cols), tuple(tuple(zero for _ in range(K + 1)) for _ in cols))
            heads, accs = lax.fori_loop(0, ngroups, step, init, unroll=True)
            for cs, head, acc in zip(cols, heads, accs):
                carry[:, cs] = head
                dcb_ref[:, cs] += _colsum(acc[K])
                for k in range(K):
                    dw_ref[k:k + 1, cs] += _colsum(acc[k])

    tile = lambda c: pl.BlockSpec((T, c), lambda i: (nt - 1 - i, 0))
    return pl.pallas_call(
        body, grid=(nt,), name="ffn_down_gate_bwd",
        out_shape=(jax.ShapeDtypeStruct((S, 2 * D_FF), bf16), jax.ShapeDtypeStruct((8, 2 * D_FF), f32),
                   jax.ShapeDtypeStruct((1, 2 * D_FF), f32)),
        in_specs=[tile(D_MODEL), _resident((D_FF, D_MODEL)), tile(2 * D_FF), tile(2 * D_FF), _acc((8, 2 * D_FF))],
        out_specs=(tile(2 * D_FF), _acc((8, 2 * D_FF)), _acc((1, 2 * D_FF))),
        scratch_shapes=[pltpu.VMEM((T, D_FF), f32), pltpu.VMEM((H, 2 * D_FF), f32)],
        compiler_params=_cp(("arbitrary",), 48),
    )(dzb, w_down, hfb, hc, conv_w)


def _wgrad(a, g, tn, name):
    S, K = a.shape
    N = g.shape[1]
    T = 1024 if S % 1024 == 0 else S

    def body(a_ref, g_ref, dw_ref, db_ref):
        @pl.when(pl.program_id(1) == 0)
        def _():
            dw_ref[...] = jnp.zeros_like(dw_ref)
            db_ref[...] = jnp.zeros_like(db_ref)
        gt = g_ref[...]
        dw_ref[...] += lax.dot_general(a_ref[...].astype(bf16), gt, TN_DIMS, preferred_element_type=f32)
        db_ref[...] += _colsum(gt.astype(f32))

    return pl.pallas_call(
        body, grid=(N // tn, S // T), name=name,
        out_shape=(jax.ShapeDtypeStruct((K, N), f32), jax.ShapeDtypeStruct((1, N), f32)),
        in_specs=[pl.BlockSpec((T, K), lambda j, i: (i, 0)), pl.BlockSpec((T, tn), lambda j, i: (i, j))],
        out_specs=(pl.BlockSpec((K, tn), lambda j, i: (0, j)), pl.BlockSpec((1, tn), lambda j, i: (0, j))),
        compiler_params=_cp(("parallel", "arbitrary"), 48),
    )(a, g)


def _outproj_dgrad(dzb, w, attn_out, lse):
    S = dzb.shape[0]
    T = PERM_TILE
    nsl = ATTN_CH // LANES
    n_p = len(DILATIONS)

    def body(g_ref, w_ref, ao_ref, lse_ref, dco_ref, dgo_ref, *rest):
        do_refs = rest[:n_p]
        st_refs = rest[n_p:2 * n_p]
        slabs = rest[2 * n_p:]
        dcat = lax.dot_general(g_ref[...], w_ref[...], NT_DIMS, preferred_element_type=f32)
        dco_ref[...] = dcat[:, :CONV_CH]
        dgo_ref[...] = dcat[:, CONV_CH + ATTN_CH:]
        lane = lax.broadcasted_iota(jnp.int32, (T, LANES), 1)
        st = lse_ref[...]
        for j in range(nsl):
            dO = dcat[:, CONV_CH + LANES * j:CONV_CH + LANES * (j + 1)]
            prod = dO * ao_ref[:, LANES * j:LANES * (j + 1)].astype(f32)
            for a in range(2):
                in_head = (lane < HEAD_DIM) if a == 0 else (lane >= HEAD_DIM)
                delta = jnp.sum(jnp.where(in_head, prod, 0.0), axis=1, keepdims=True)
                st = jnp.where((lane // 16 == 2 * j + a) & (lane % 16 >= 8), delta, st)
            slabs[j][...] = dO
        slabs[nsl][...] = st
        for d, do_ref, st_ref in zip(DILATIONS, do_refs, st_refs):
            for r in range(d):
                for j in range(nsl):
                    do_ref[r, :, LANES * j:LANES * (j + 1)] = _rows_of(slabs[j], r, T // d, d).astype(bf16)
                st_ref[r] = _rows_of(slabs[nsl], r, T // d, d)

    row = lambda c: pl.BlockSpec((T, c), lambda i: (i, 0))
    return pl.pallas_call(
        body, grid=(S // T,), name="outproj_dgrad",
        out_shape=(jax.ShapeDtypeStruct((S, CONV_CH), f32), jax.ShapeDtypeStruct((S, GMLP_CH), f32))
        + tuple(_perm_shape(S, d, ATTN_CH, bf16) for d in DILATIONS)
        + tuple(_perm_shape(S, d, ATTN_BLOCK, f32) for d in DILATIONS),
        in_specs=[row(D_MODEL), _resident((D_MODEL, D_MODEL)), row(ATTN_CH), row(ATTN_BLOCK)],
        out_specs=(row(CONV_CH), row(GMLP_CH)) + tuple(_perm_spec(d, ATTN_CH) for d in DILATIONS)
        + tuple(_perm_spec(d, ATTN_BLOCK) for d in DILATIONS),
        scratch_shapes=_slabs(nsl + 1, T),
        compiler_params=_cp(("parallel",), 40),
    )(dzb, w, attn_out, lse)


def _gmlp_bwd(c_in, dgm, ln_g, ln_b, w_s, b_s_t):
    S = c_in.shape[0]
    T = 512
    nsteps = S // T

    def body(c_ref, dg_ref, g_ref, be_ref, w_ref, bs_ref, dc_ref, dlg_ref, dlb_ref, dw_ref, dbs_ref,
             du_buf, dv_buf, dm_acc):
        i = pl.program_id(0)

        @pl.when(i == 0)
        def _():
            dlg_ref[...] = jnp.zeros_like(dlg_ref)
            dlb_ref[...] = jnp.zeros_like(dlb_ref)
            dw_ref[...] = jnp.zeros_like(dw_ref)
            dm_acc[...] = jnp.zeros_like(dm_acc)
        c = c_ref[...]
        u = c[:, :GMLP_CH]
        xhat, rstd = _ln_stats(c[:, GMLP_CH:])
        vb = (xhat * g_ref[...] + be_ref[...]).astype(bf16)
        dgm_t = dg_ref[...]
        dm_all = dgm_t * u
        for g in range(GMLP_GROUPS):
            wt = _tril_bf16(w_ref[g])
            cs = slice(GMLP_GROUP_DIM * g, GMLP_GROUP_DIM * (g + 1))
            dw_g = jnp.zeros((CHUNK, CHUNK), f32)
            for ci in range(T // CHUNK):
                rs = slice(CHUNK * ci, CHUNK * (ci + 1))
                v_c = vb[rs, cs]
                mixed = jnp.dot(wt, v_c, preferred_element_type=f32) + bs_ref[:, g:g + 1]
                dm = dm_all[rs, cs]
                dmb = dm.astype(bf16)
                du_buf[rs, cs] = dgm_t[rs, cs] * mixed
                dv_buf[rs, cs] = lax.dot_general(wt, dmb, TN_DIMS, preferred_element_type=f32)
                dw_g = dw_g + lax.dot_general(dmb, v_c, NT_DIMS, preferred_element_type=f32)
                dm_acc[:, cs] += dm
            dw_ref[g] += dw_g
        dv = dv_buf[...]
        dvr = _ln_bwd(dv, xhat, rstd, g_ref[...])
        dlg_ref[...] += _colsum(dv * xhat)
        dlb_ref[...] += _colsum(dv)
        dc_ref[:, :GMLP_CH] = du_buf[...].astype(bf16)
        dc_ref[:, GMLP_CH:] = dvr.astype(bf16)

        @pl.when(i == nsteps - 1)
        def _():
            row = lax.broadcasted_iota(jnp.int32, (CHUNK, CHUNK), 0)
            col = lax.broadcasted_iota(jnp.int32, (CHUNK, CHUNK), 1)
            tile = jnp.zeros((CHUNK, CHUNK), f32)
            for g in range(GMLP_GROUPS):
                dw_ref[g] = jnp.where(col <= row, dw_ref[g], 0.0)
                gsum = jnp.sum(dm_acc[:, GMLP_GROUP_DIM * g:GMLP_GROUP_DIM * (g + 1)], axis=1, keepdims=True)
                tile = jnp.where(col == g, gsum, tile)
            dbs_ref[...] = tile

    vec = jax.ShapeDtypeStruct((1, GMLP_CH), f32)
    return pl.pallas_call(
        body, grid=(nsteps,), name="gmlp_bwd",
        out_shape=(jax.ShapeDtypeStruct((S, 2 * GMLP_CH), bf16), vec, vec,
                   jax.ShapeDtypeStruct((GMLP_GROUPS, CHUNK, CHUNK), f32), jax.ShapeDtypeStruct((CHUNK, CHUNK), f32)),
        in_specs=[pl.BlockSpec((T, 2 * GMLP_CH), lambda i: (i, 0)), pl.BlockSpec((T, GMLP_CH), lambda i: (i, 0)),
                  _acc((1, GMLP_CH)), _acc((1, GMLP_CH)), _acc((GMLP_GROUPS, CHUNK, CHUNK)), _acc((CHUNK, GMLP_GROUPS))],
        out_specs=(pl.BlockSpec((T, 2 * GMLP_CH), lambda i: (i, 0)), _acc((1, GMLP_CH)), _acc((1, GMLP_CH)),
                   _acc((GMLP_GROUPS, CHUNK, CHUNK)), _acc((CHUNK, CHUNK))),
        scratch_shapes=[pltpu.VMEM((T, GMLP_CH), f32), pltpu.VMEM((T, GMLP_CH), f32), pltpu.VMEM((CHUNK, GMLP_CH), f32)],
        compiler_params=_cp(("arbitrary",), 32),
    )(c_in, dgm, ln_g, ln_b, w_s, b_s_t)


def _attn_bwd_pattern(qkv, d_out, stats, bias, d):
    _, L, _ = qkv.shape
    B = ATTN_BLOCK
    QB = _attn_tile(L)
    nsb = QB // B
    nt = L // QB
    U = nsb * UNITS_PER_BLOCK
    KV = 2 * ATTN_CH

    def body(cur_ref, hk_ref, hv_ref, do_ref, st_ref, b_ref, dqkv_ref, dbias_ref, lg, dp, pb, dsb, dkv, carry):
        r = pl.program_id(0)
        i = pl.program_id(1)
        n = nt - 1 - i

        @pl.when((r == 0) & (i == 0))
        def _():
            dbias_ref[...] = jnp.zeros_like(dbias_ref)

        @pl.when(i == 0)
        def _():
            carry[...] = jnp.zeros_like(carry)

        def operands(b, j, a):
            rows = slice(B * b, B * (b + 1))
            q2 = cur_ref[rows, LANES * j:LANES * (j + 1)]
            do2 = do_ref[rows, LANES * j:LANES * (j + 1)]
            keep = _head_lanes(a)
            return jnp.where(keep, q2, jnp.zeros_like(q2)), jnp.where(keep, do2, jnp.zeros_like(do2))

        for b in range(nsb):
            valid = _band_mask(b == 0, n)
            for j in range(HEAD_PAIRS):
                k2 = _pair_keys(cur_ref, hk_ref, 1, b, j)
                v2 = _pair_keys(cur_ref, hv_ref, 2, b, j)
                for a in range(2):
                    u = (b * HEAD_PAIRS + j) * 2 + a
                    qm, dom = operands(b, j, a)
                    logits = lax.dot_general(qm, k2, NT_DIMS, preferred_element_type=f32) + b_ref[2 * j + a]
                    lg[B * u:B * (u + 1), :] = jnp.where(valid, logits, NEG)
                    dp[B * u:B * (u + 1), :] = lax.dot_general(dom, v2, NT_DIMS, preferred_element_type=f32)
        for b in range(nsb):
            for j in range(HEAD_PAIRS):
                for a in range(2):
                    u = (b * HEAD_PAIRS + j) * 2 + a
                    rows = slice(B * u, B * (u + 1))
                    lane0 = 32 * j + 16 * a
                    lse = st_ref[B * b:B * (b + 1), lane0:lane0 + 1]
                    delta = st_ref[B * b:B * (b + 1), lane0 + 8:lane0 + 9]
                    p = jnp.exp(lg[rows, :] - lse)
                    ds = p * (dp[rows, :] - delta)
                    pb[rows, :] = p.astype(bf16)
                    dsb[rows, :] = ds.astype(bf16)
                    dbias_ref[2 * j + a] += ds
        dkv[...] = jnp.zeros_like(dkv)
        for b in range(nsb):
            for j in range(HEAD_PAIRS):
                k2 = _pair_keys(cur_ref, hk_ref, 1, b, j)
                dq, dk2, dv2 = [], None, None
                for a in range(2):
                    u = (b * HEAD_PAIRS + j) * 2 + a
                    rows = slice(B * u, B * (u + 1))
                    qm, dom = operands(b, j, a)
                    ds_u = dsb[rows, :]
                    dq.append(jnp.dot(ds_u, k2, preferred_element_type=f32))
                    dk_u = lax.dot_general(ds_u, qm, TN_DIMS, preferred_element_type=f32)
                    dv_u = lax.dot_general(pb[rows, :], dom, TN_DIMS, preferred_element_type=f32)
                    dk2 = dk_u if dk2 is None else dk2 + dk_u
                    dv2 = dv_u if dv2 is None else dv2 + dv_u
                dq2 = jnp.where(_head_lanes(0), dq[0], dq[1]) * (HEAD_DIM ** -0.5)
                dqkv_ref[B * b:B * (b + 1), LANES * j:LANES * (j + 1)] = dq2.astype(bf16)
                dkv[B * b:B * (b + 2), LANES * j:LANES * (j + 1)] += dk2
                dkv[B * b:B * (b + 2), ATTN_CH + LANES * j:ATTN_CH + LANES * (j + 1)] += dv2
        dkv[QB:, :] += carry[...]
        dqkv_ref[:, ATTN_CH:] = dkv[B:, :].astype(bf16)
        carry[...] = dkv[0:B, :]

    halo = lambda part: pl.BlockSpec((None, B, ATTN_CH),
                                     lambda r, i: (r, jnp.maximum((nt - 1 - i) * nsb - 1, 0), part))
    tile_spec = lambda c: pl.BlockSpec((None, QB, c), lambda r, i: (r, nt - 1 - i, 0))
    return pl.pallas_call(
        body, grid=(d, nt), name=f"attn_bwd_d{d}",
        out_shape=(jax.ShapeDtypeStruct((d, L, QKV_CH), bf16), jax.ShapeDtypeStruct((ATTN_HEADS, B, 2 * B), f32)),
        in_specs=[tile_spec(QKV_CH), halo(1), halo(2), tile_spec(ATTN_CH), tile_spec(B),
                  _resident((ATTN_HEADS, B, 2 * B))],
        out_specs=(tile_spec(QKV_CH), _acc((ATTN_HEADS, B, 2 * B))),
        scratch_shapes=[pltpu.VMEM((U * B, 2 * B), f32), pltpu.VMEM((U * B, 2 * B), f32),
                        pltpu.VMEM((U * B, 2 * B), bf16), pltpu.VMEM((U * B, 2 * B), bf16),
                        pltpu.VMEM((B + QB, KV), f32), pltpu.VMEM((B, KV), f32)],
        compiler_params=_cp(("arbitrary", "arbitrary"), 48),
    )(qkv, qkv, qkv, d_out, stats, bias)


def _attn_bwd_merge(d_a, dqkvs, d_c):
    S = d_a.shape[0]
    T = PERM_TILE
    nsl = QKV_CH // LANES
    n_p = len(DILATIONS)

    def body(da_ref, *rest):
        g_refs = rest[:n_p]
        dc_ref, dh_ref = rest[n_p:n_p + 2]
        slabs = rest[n_p + 2:]
        q0 = 2 * CONV_CH
        dh_ref[:, :q0] = da_ref[...]
        dh_ref[:, q0 + QKV_CH:] = dc_ref[...]
        for p, (d, g_ref) in enumerate(zip(DILATIONS, g_refs)):
            for r in range(d):
                for j in range(nsl):
                    _set_rows_of(slabs[p * nsl + j], r, T // d, d, g_ref[r, :, LANES * j:LANES * (j + 1)].astype(f32))
        for j in range(nsl):
            acc = slabs[j][...]
            for p in range(1, n_p):
                acc = acc + slabs[p * nsl + j][...]
            dh_ref[:, q0 + LANES * j:q0 + LANES * (j + 1)] = acc.astype(bf16)

    row = lambda c: pl.BlockSpec((T, c), lambda i: (i, 0))
    return pl.pallas_call(
        body, grid=(S // T,), name="attn_bwd_merge", out_shape=jax.ShapeDtypeStruct((S, IN_CH), bf16),
        in_specs=[row(2 * CONV_CH)] + [_perm_spec(d, QKV_CH) for d in DILATIONS] + [row(2 * GMLP_CH)],
        out_specs=row(IN_CH), scratch_shapes=_slabs(n_p * nsl, T),
        compiler_params=_cp(("parallel",), 48),
    )(d_a, *dqkvs, d_c)


def _bias_table_grad(dbias, buckets):
    n = dbias.shape[0]

    def body(db_ref, bk_ref, o_ref):
        p = pl.program_id(0)
        h = pl.program_id(1)

        @pl.when((p == 0) & (h == 0))
        def _():
            o_ref[...] = jnp.zeros_like(o_ref)
        ids = bk_ref[0]
        db = db_ref[0, 0]
        row = lax.broadcasted_iota(jnp.int32, (N_BUCKETS, 128), 0)
        lane = lax.broadcasted_iota(jnp.int32, (N_BUCKETS, 128), 1)
        upd = jnp.zeros((N_BUCKETS, 128), f32)
        for b in range(N_BUCKETS):
            s = jnp.sum(jnp.sum(jnp.where(ids == b, db, 0.0), axis=1, keepdims=True), axis=0, keepdims=True)
            upd = jnp.where((row == b) & (lane == h), s, upd)
        o_ref[...] += upd

    return pl.pallas_call(
        body, grid=(n, ATTN_HEADS), name="bias_table_grad",
        out_shape=jax.ShapeDtypeStruct((N_BUCKETS, 128), f32),
        in_specs=[pl.BlockSpec((1, 1, ATTN_BLOCK, 2 * ATTN_BLOCK), lambda p, h: (p, h, 0, 0)),
                  pl.BlockSpec((1, ATTN_BLOCK, 2 * ATTN_BLOCK), lambda p, h: (p, 0, 0))],
        out_specs=_acc((N_BUCKETS, 128)),
        compiler_params=_cp(("arbitrary", "arbitrary"), 16),
    )(dbias, buckets)


def _conv_bwd(a_in, hc, dco, dw_w, ln_g, ln_b):
    S = a_in.shape[0]
    T = 512
    hb = T // CONV_HALO
    nsteps = S // T
    R = T + CONV_HALO
    K = CONV_WIDTH

    def body(a_ref, hc_ref, hcn_ref, d_ref, dn_ref, w_ref, g_ref, be_ref,
             da_ref, dw_ref, dcb_ref, dlg_ref, dlb_ref, ext, dbuf, wacc):
        i = pl.program_id(0)

        @pl.when(i == 0)
        def _():
            wacc[...] = jnp.zeros_like(wacc)
            dcb_ref[...] = jnp.zeros_like(dcb_ref)
            dlg_ref[...] = jnp.zeros_like(dlg_ref)
            dlb_ref[...] = jnp.zeros_like(dlb_ref)
        ext[0:T, :] = hc_ref[...]
        ext[T:, :] = hcn_ref[...]
        xhat, rstd = _ln_stats(ext[...])
        hl = xhat * g_ref[...] + be_ref[...]
        ext[0:T, :] = d_ref[...]
        ext[T:, :] = dn_ref[...]
        sl_ = _sig(hl)
        dhl = ext[...] * (sl_ * (1.0 + hl * (1.0 - sl_)))
        dhc = _ln_bwd(dhl, xhat, rstd, g_ref[...])
        rowi = lax.broadcasted_iota(jnp.int32, (R, CONV_CH), 0)
        dbuf[...] = jnp.where((rowi < T) | (i < nsteps - 1), dhc, 0.0)
        dlg_ref[...] += _colsum(dhl[:T] * xhat[:T])
        dlb_ref[...] += _colsum(dhl[:T])
        dcb_ref[...] += _colsum(dbuf[pl.ds(0, T), :])
        starts = _window_rolls(range(K))
        slabs = [slice(LANES * j, LANES * (j + 1)) for j in range(CONV_CH // LANES)]

        def step(g, _):
            r0 = pl.multiple_of(g * CONV_GROUP, CONV_GROUP)
            rows = pl.ds(r0, CONV_GROUP)
            for j, cs in enumerate(slabs):
                gate_cs = slice(CONV_CH + LANES * j, CONV_CH + LANES * (j + 1))
                win = dbuf[pl.ds(r0, CONV_GROUP + CONV_HALO), cs]
                a = a_ref[rows, cs]
                sg = _sig(a_ref[rows, gate_cs])
                hg = a * sg
                dhg = jnp.zeros((CONV_GROUP, LANES), f32)
                for b, ss in starts.items():
                    rolled = win if b == 0 else pltpu.roll(win, b, 0)
                    for s in ss:
                        k = K - 1 - s
                        dk = rolled[s + b:s + b + CONV_GROUP]
                        dhg = dhg + w_ref[k:k + 1, cs] * dk
                        prod = dk * hg
                        fold = prod[0:SUBLANES]
                        for r in range(SUBLANES, CONV_GROUP, SUBLANES):
                            fold = fold + prod[r:r + SUBLANES]
                        wacc[SUBLANES * k:SUBLANES * (k + 1), cs] += fold
                da_ref[rows, cs] = (dhg * sg).astype(bf16)
                da_ref[rows, gate_cs] = (dhg * hg * (1.0 - sg)).astype(bf16)
            return 0

        lax.fori_loop(0, T // CONV_GROUP, step, 0)

        @pl.when(i == nsteps - 1)
        def _():
            for k in range(K):
                dw_ref[k:k + 1, :] = _colsum(wacc[SUBLANES * k:SUBLANES * (k + 1), :])
            dw_ref[K:, :] = jnp.zeros((32 - K, CONV_CH), f32)

    vec = jax.ShapeDtypeStruct((1, CONV_CH), f32)
    nxt = lambda i: (jnp.minimum((i + 1) * hb, nsteps * hb - 1), 0)
    return pl.pallas_call(
        body, grid=(nsteps,), name="conv_bwd",
        out_shape=(jax.ShapeDtypeStruct((S, 2 * CONV_CH), bf16), jax.ShapeDtypeStruct((32, CONV_CH), f32), vec, vec, vec),
        in_specs=[pl.BlockSpec((T, 2 * CONV_CH), lambda i: (i, 0)),
                  pl.BlockSpec((T, CONV_CH), lambda i: (i, 0)), pl.BlockSpec((CONV_HALO, CONV_CH), nxt),
                  pl.BlockSpec((T, CONV_CH), lambda i: (i, 0)), pl.BlockSpec((CONV_HALO, CONV_CH), nxt),
                  _acc((32, CONV_CH)), _acc((1, CONV_CH)), _acc((1, CONV_CH))],
        out_specs=(pl.BlockSpec((T, 2 * CONV_CH), lambda i: (i, 0)), _acc((32, CONV_CH)), _acc((1, CONV_CH)),
                   _acc((1, CONV_CH)), _acc((1, CONV_CH))),
        scratch_shapes=[pltpu.VMEM((R, CONV_CH), f32), pltpu.VMEM((R, CONV_CH), f32),
                        pltpu.VMEM((SUBLANES * 32, CONV_CH), f32)],
        compiler_params=_cp(("arbitrary",), 32),
    )(a_in, hc, hc, dco, dco, dw_w, ln_g, ln_b)


def _adamw(g, w, m, v, name):
    R, C = g.shape
    T = R
    for cand in (512, 256, 128, 64, 32, 16, 8):
        if R % cand == 0 and cand * C * 4 <= MIB:
            T = cand
            break
    c1 = 1.0 / (1.0 - ADAM_B1 ** ADAM_STEP)
    c2 = 1.0 / (1.0 - ADAM_B2 ** ADAM_STEP)

    def body(g_ref, w_ref, m_ref, v_ref, d_ref, nm_ref, nv_ref):
        gg = g_ref[...]
        nm = ADAM_B1 * m_ref[...] + (1.0 - ADAM_B1) * gg
        nv = ADAM_B2 * v_ref[...] + (1.0 - ADAM_B2) * (gg * gg)
        nm_ref[...] = nm
        nv_ref[...] = nv
        d_ref[...] = -ADAM_LR * ((nm * c1) / (jnp.sqrt(nv * c2) + ADAM_EPS) + ADAM_WD * w_ref[...])

    blk = pl.BlockSpec((T, C), lambda i: (i, 0))
    sd = jax.ShapeDtypeStruct((R, C), f32)
    return pl.pallas_call(
        body, grid=(R // T,), name=name, out_shape=(sd, sd, sd), in_specs=[blk] * 4, out_specs=(blk, blk, blk),
        compiler_params=_cp(("parallel",), 48),
    )(g, w, m, v)


def _pad_rows(a, rows):
    return jnp.pad(a, ((0, rows - a.shape[0]), (0, 0)))


def _local_step(x, target, wb, late_weights, sp, sink):
    buckets = jnp.asarray(_bucket_ids())
    bias = _bias_build(sp["rel_bias_table"], buckets)
    wb = dict(wb)
    saved = []
    xl = x
    for l in range(DEPTH):
        vec = lambda name: sp[name][l][None, :]
        a_in, c_in, *qkv = _inproj_fwd(xl, wb["w_in"][l], vec("b_in"))
        conv_w = _pad_rows(sp["conv_dw_w"][l], 32)
        conv_out, hc = _conv_fwd(a_in, conv_w, vec("conv_dw_b"), vec("conv_ln_g"), vec("conv_ln_b"))
        attn_out, lse = _attn_fwd(qkv, bias)
        bs_t = sp["gmlp_b_s"][l].T
        gm_out = _gmlp_fwd(c_in, vec("gmlp_ln_g"), vec("gmlp_ln_b"), sp["gmlp_w_s"][l], bs_t)
        if l == 0:
            wb.update(late_weights(gm_out))
        cat, z1, x1, x1b = _outproj_ln_fwd(conv_out, attn_out, gm_out, wb["w_out"][l], vec("b_out"), xl,
                                           vec("ln1_g"), vec("ln1_b"))
        fconv_w = _pad_rows(sp["ffn_conv_w"][l], 8)
        hfb, fhc, act = _ffn_up_gate_fwd(x1b, wb["ffn_w_up"][l], vec("ffn_b_up"), fconv_w, vec("ffn_conv_b"))
        down = (act, wb["ffn_w_down"][l], vec("ffn_b_down"), x1, vec("ln2_g"), vec("ln2_b"))
        z2, x2 = _ffn_down_ln_fwd(*down) if l < DEPTH - 1 else (None, None)
        saved.append(dict(x=xl, a_in=a_in, qkv=qkv, c_in=c_in, hc=hc, attn_out=attn_out, lse=lse, cat=cat, z1=z1,
                          x1b=x1b, hfb=hfb, fhc=fhc, act=act, z2=z2, conv_w=conv_w, fconv_w=fconv_w, bs_t=bs_t))
        xl = x2

    grads = {}
    per_layer = {k: [None] * DEPTH for k in (
        "b_in", "conv_dw_w", "conv_dw_b", "conv_ln_g", "conv_ln_b", "gmlp_ln_g", "gmlp_ln_b", "gmlp_w_s",
        "gmlp_b_s", "b_out", "ln1_g", "ln1_b", "ffn_b_up", "ffn_conv_w", "ffn_conv_b", "ffn_b_down", "ln2_g", "ln2_b")}
    dbias_all = []
    dz2, dz2b, loss_part, dg2, db2 = _ffn_down_ln_loss(*down, target)
    loss = jnp.sum(loss_part)
    grad_x = None
    tok = jnp.zeros((), f32)
    for l in reversed(range(DEPTH)):
        sv = saved[l]
        vec = lambda name: sp[name][l][None, :] + tok
        per_layer["ln2_g"][l] = dg2[0]
        per_layer["ln2_b"][l] = db2[0]
        dw_down, db_down = _wgrad(sv["act"], dz2b, 512, "ffn_down_wgrad")
        tok = sink.put("ffn_w_down", l, dw_down, tok)
        per_layer["ffn_b_down"][l] = db_down[0]
        dhf, dfcw, dfcb = _ffn_down_gate_bwd(dz2b, wb["ffn_w_down"][l], sv["hfb"], sv["fhc"], sv["fconv_w"])
        per_layer["ffn_conv_w"][l] = dfcw[:FFN_CONV_WIDTH]
        per_layer["ffn_conv_b"][l] = dfcb[0]
        dw_up, db_up = _wgrad(sv["x1b"], dhf, 1408, "ffn_up_wgrad")
        tok = sink.put("ffn_w_up", l, dw_up, tok)
        per_layer["ffn_b_up"][l] = db_up[0]
        dz1, dz1b, dg1, db1 = _dgrad_ln_bwd(dhf, wb["ffn_w_up"][l], dz2, sv["z1"], vec("ln1_g"), "ffn_up_dgrad_ln")
        per_layer["ln1_g"][l] = dg1[0]
        per_layer["ln1_b"][l] = db1[0]
        dw_out, db_out = _wgrad(sv["cat"], dz1b, 512, "outproj_wgrad")
        tok = sink.put("w_out", l, dw_out, tok)
        per_layer["b_out"][l] = db_out[0]
        dco, dgo, *perm = _outproj_dgrad(dz1b, wb["w_out"][l], sv["attn_out"], sv["lse"])
        d_outs, stats = perm[:len(DILATIONS)], perm[len(DILATIONS):]
        d_c, dglg, dglb, dws, dbs = _gmlp_bwd(sv["c_in"], dgo, vec("gmlp_ln_g"), vec("gmlp_ln_b"), sp["gmlp_w_s"][l],
                                              sv["bs_t"])
        per_layer["gmlp_ln_g"][l] = dglg[0]
        per_layer["gmlp_ln_b"][l] = dglb[0]
        per_layer["gmlp_w_s"][l] = dws
        per_layer["gmlp_b_s"][l] = dbs[:, :GMLP_GROUPS].T
        dqkvs = []
        for p, d in enumerate(DILATIONS):
            dqkv, dbias = _attn_bwd_pattern(sv["qkv"][p], d_outs[p], stats[p], bias[p], d)
            dqkvs.append(dqkv)
            dbias_all.append(dbias)
        d_a, dcw, dcb, dclg, dclb = _conv_bwd(sv["a_in"], sv["hc"], dco, sv["conv_w"], vec("conv_ln_g"),
                                              vec("conv_ln_b"))
        per_layer["conv_dw_w"][l] = dcw[:CONV_WIDTH]
        per_layer["conv_dw_b"][l] = dcb[0]
        per_layer["conv_ln_g"][l] = dclg[0]
        per_layer["conv_ln_b"][l] = dclb[0]
        dh = _attn_bwd_merge(d_a, dqkvs, d_c)
        dw_in, db_in = _wgrad(sv["x"], dh, 640, "inproj_wgrad")
        tok = sink.put("w_in", l, dw_in, tok)
        per_layer["b_in"][l] = db_in[0]
        if l > 0:
            pv = saved[l - 1]
            dz2, dz2b, dg2, db2 = _dgrad_ln_bwd(dh, wb["w_in"][l], dz1, pv["z2"], sp["ln2_g"][l - 1][None, :] + tok,
                                                "inproj_dgrad_ln")
        else:
            grad_x = _dgrad_ln_bwd(dh, wb["w_in"][l], dz1, None, None, "inproj_dgrad")
    for k, v in per_layer.items():
        grads[k] = jnp.stack(v)
    dbias_cat = jnp.stack(dbias_all)
    bk_cat = jnp.concatenate([buckets] * DEPTH, axis=0)
    grads["rel_bias_table"] = _bias_table_grad(dbias_cat, bk_cat)[:, :ATTN_HEADS]
    return loss, grad_x, grads, sink.finish(grad_x)


N_CHIPS = 4
BIG = {"w_in": (D_MODEL, IN_CH, 1), "w_out": (D_MODEL, D_MODEL, 0),
       "ffn_w_up": (D_MODEL, 2 * D_FF, 1), "ffn_w_down": (D_FF, D_MODEL, 0)}
SMALL = ("b_in", "conv_dw_w", "conv_dw_b", "conv_ln_g", "conv_ln_b", "rel_bias_table", "gmlp_ln_g", "gmlp_ln_b",
         "gmlp_w_s", "gmlp_b_s", "b_out", "ln1_g", "ln1_b", "ffn_b_up", "ffn_conv_w", "ffn_conv_b", "ffn_b_down",
         "ln2_g", "ln2_b")
SMALL_SHARDED = ("conv_dw_w", "ffn_conv_w")
WEIGHTS = ("w_in", "b_in", "conv_dw_w", "conv_dw_b", "conv_ln_g", "conv_ln_b", "rel_bias_table", "gmlp_ln_g",
           "gmlp_ln_b", "gmlp_w_s", "gmlp_b_s", "w_out", "b_out", "ln1_g", "ln1_b", "ffn_w_up", "ffn_b_up",
           "ffn_conv_w", "ffn_conv_b", "ffn_w_down", "ffn_b_down", "ln2_g", "ln2_b")
ANY = pl.BlockSpec(memory_space=pl.ANY)


def _position():
    return lax.axis_index("x"), lax.axis_index("y"), lax.axis_index("c")


def _other_chips(x, y):
    return [(1 - x, y), (x, 1 - y), (1 - x, 1 - y)]


def _cast_bf16(a):
    R, C = a.shape
    T = 128

    def body(a_ref, o_ref):
        o_ref[...] = a_ref[...].astype(bf16)

    return pl.pallas_call(
        body, grid=(R // T,), name="cast_bf16", out_shape=jax.ShapeDtypeStruct((R, C), bf16),
        in_specs=[pl.BlockSpec((T, C), lambda i: (i, 0))], out_specs=pl.BlockSpec((T, C), lambda i: (i, 0)),
        compiler_params=_cp(("parallel",), 16),
    )(a)


def _chip_slot(ref, name, l, p):
    K, N, ax = BIG[name]
    if ax == 1:
        sz = N // N_CHIPS
        return ref.at[l, :, pl.ds(pl.multiple_of(p * sz, 128), sz)]
    sz = K // N_CHIPS
    return ref.at[l, pl.ds(pl.multiple_of(p * sz, 16), sz), :]


def _gather_weights(shards, conv_w, fconv_w):
    names = list(shards)
    n_big = len(names)
    n_t = n_big + 2
    n_chip = 3 * n_t
    n_pass = 3 * n_big

    def body(*refs):
        ins = refs[:n_t]
        outs = refs[n_t:2 * n_t]
        send_sems, recv_sems, pass_send, pass_recv, local_sems = refs[2 * n_t:]
        x, y, c = _position()
        me = 2 * x + y
        chips = _other_chips(x, y)

        def src(t):
            return ins[t].at[c] if t < n_big else ins[t]

        def slot(t, l, p):
            return _chip_slot(outs[t], names[t], l, p) if t < n_big else outs[t].at[p]

        locs, cps = [], []
        for t in range(n_t):
            for l in (range(DEPTH) if t < n_big else (0,)):
                loc = pltpu.make_async_copy(ins[t].at[l] if t < n_big else ins[t], slot(t, l, me),
                                            local_sems.at[DEPTH * t + l])
                loc.start()
                locs.append(loc)
            for k, (px, py) in enumerate(chips):
                cp = pltpu.make_async_remote_copy(
                    src_ref=src(t), dst_ref=slot(t, c, me), send_sem=send_sems.at[3 * t + k],
                    recv_sem=recv_sems.at[3 * t + k], device_id=(px, py, c), device_id_type=MESH_ID)
                cp.start()
                cps.append(cp)
        for t in range(n_t):
            for k, (px, py) in enumerate(chips):
                landed = slot(t, c, 2 * px + py)
                pltpu.make_async_remote_copy(
                    src_ref=src(t), dst_ref=landed, send_sem=send_sems.at[3 * t + k],
                    recv_sem=recv_sems.at[3 * t + k], device_id=(px, py, c), device_id_type=MESH_ID).wait_recv()
                if t < n_big:
                    cp = pltpu.make_async_remote_copy(
                        src_ref=landed, dst_ref=landed, send_sem=pass_send.at[3 * t + k],
                        recv_sem=pass_recv.at[3 * t + k], device_id=(x, y, 1 - c), device_id_type=MESH_ID)
                    cp.start()
                    cps.append(cp)
        for t in range(n_big):
            for k, (px, py) in enumerate(chips):
                from_sibling = slot(t, 1 - c, 2 * px + py)
                pltpu.make_async_remote_copy(
                    src_ref=from_sibling, dst_ref=from_sibling, send_sem=pass_send.at[3 * t + k],
                    recv_sem=pass_recv.at[3 * t + k], device_id=(x, y, 1 - c), device_id_type=MESH_ID).wait_recv()
        for cp in cps:
            cp.wait_send()
        for loc in locs:
            loc.wait()

    ins = [shards[n] for n in names] + [conv_w, fconv_w]
    out_shape = [jax.ShapeDtypeStruct((DEPTH, BIG[n][0], BIG[n][1]), bf16) for n in names]
    out_shape += [jax.ShapeDtypeStruct((N_CHIPS,) + conv_w.shape, f32), jax.ShapeDtypeStruct((N_CHIPS,) + fconv_w.shape, f32)]
    outs = pl.pallas_call(
        body, name="gather_weights", out_shape=tuple(out_shape), in_specs=[ANY] * n_t, out_specs=tuple([ANY] * n_t),
        scratch_shapes=[pltpu.SemaphoreType.DMA((n_chip,)), pltpu.SemaphoreType.DMA((n_chip,)),
                        pltpu.SemaphoreType.DMA((n_pass,)), pltpu.SemaphoreType.DMA((n_pass,)),
                        pltpu.SemaphoreType.DMA((DEPTH * n_t,))],
    )(*ins)
    return dict(zip(names, outs[:n_big])), outs[-2], outs[-1]


LATE_WEIGHTS = ("w_out", "ffn_w_up", "ffn_w_down")
HBM = pl.BlockSpec(memory_space=pltpu.HBM)
SEM = pl.BlockSpec(memory_space=pltpu.SEMAPHORE)


def _cast_into_full(shard, name, chip_arr):
    K, N, ax = BIG[name]
    k, n = _shard_shape(name)
    T = 64
    nrt = k // T

    def body(p_ref, a_ref, o_ref):
        o_ref[...] = a_ref[...].astype(bf16)

    if ax == 1:
        out_spec = pl.BlockSpec((None, T, n), lambda l, i, p: (l, i, p[0]))
    else:
        out_spec = pl.BlockSpec((None, T, n), lambda l, i, p: (l, p[0] * nrt + i, 0))
    return pl.pallas_call(
        body, name="cast_into_full", out_shape=jax.ShapeDtypeStruct((DEPTH, K, N), bf16),
        grid_spec=pltpu.PrefetchScalarGridSpec(
            num_scalar_prefetch=1, grid=(DEPTH, nrt),
            in_specs=[pl.BlockSpec((None, T, n), lambda l, i, p: (l, i, 0))], out_specs=out_spec),
        compiler_params=_cp(("parallel", "parallel"), 16),
    )(chip_arr, shard)


def _late_copies(refs, send_sems, recv_sems):
    x, y, c = _position()
    me = 2 * x + y
    idx = 0
    for ref, name in zip(refs, LATE_WEIGHTS):
        for l in range(DEPTH):
            for px, py in _other_chips(x, y):
                def copy(p, ref=ref, name=name, l=l, px=px, py=py, idx=idx):
                    part = _chip_slot(ref, name, l, p)
                    return pltpu.make_async_remote_copy(
                        src_ref=part, dst_ref=part, send_sem=send_sems.at[idx], recv_sem=recv_sems.at[idx],
                        device_id=(px, py, c), device_id_type=MESH_ID)
                yield copy(me), copy(2 * px + py)
                idx += 1


N_LATE_COPIES = 3 * DEPTH * len(LATE_WEIGHTS)


def _gather_start(fulls, after):
    n = len(fulls)

    def body(*refs):
        ins = refs[:n]
        send_sems, recv_sems = refs[n + 1:n + 3]
        token = refs[-1]
        for sent, _ in _late_copies(ins, send_sems, recv_sems):
            sent.start()
        token[...] = jnp.zeros_like(token)

    outs = pl.pallas_call(
        body, name="gather_start",
        out_shape=(pltpu.SemaphoreType.DMA((N_LATE_COPIES,)), pltpu.SemaphoreType.DMA((N_LATE_COPIES,)))
        + tuple(pltpu.HBM(f.shape, f.dtype) for f in fulls) + (jax.ShapeDtypeStruct((SUBLANES, LANES), f32),),
        in_specs=(HBM,) * n + (ANY,),
        out_specs=(SEM, SEM) + (HBM,) * n + (pl.BlockSpec(memory_space=pltpu.VMEM),),
        input_output_aliases={t: 2 + t for t in range(n)},
        compiler_params=pltpu.CompilerParams(has_side_effects=pltpu.SideEffectType.DATAFLOW_SIDE_EFFECTING),
    )(*[pltpu.with_memory_space_constraint(f, pltpu.HBM) for f in fulls], after)
    return outs[0], outs[1], outs[2:2 + n], outs[-1]


def _gather_wait(send_sems, recv_sems, fulls, after):
    n = len(fulls)

    def body(*refs):
        ins = refs[:n]
        send_ref, recv_ref = refs[n:n + 2]
        for sent, landed in _late_copies(ins, send_ref, recv_ref):
            sent.wait_send()
            landed.wait_recv()

    return pl.pallas_call(
        body, name="gather_wait", out_shape=tuple(pltpu.HBM(f.shape, f.dtype) for f in fulls),
        in_specs=(HBM,) * n + (SEM, SEM, ANY), out_specs=(HBM,) * n,
        input_output_aliases={t: t for t in range(n)},
        compiler_params=pltpu.CompilerParams(has_side_effects=pltpu.SideEffectType.DATAFLOW_SIDE_EFFECTING),
    )(*fulls, send_sems, recv_sems, after)


def _half(ref, name, c):
    K, N, ax = BIG[name]
    if ax == 1:
        return ref.at[pl.ds(pl.multiple_of(c * (K // 2), 8), K // 2), :]
    return ref.at[:, pl.ds(pl.multiple_of(c * (N // 2), 128), N // 2)]


def _half_shape(name):
    K, N, ax = BIG[name]
    return (K // 2, N) if ax == 1 else (K, N // 2)


def _shard_of_half(ref, name, q):
    K, N, ax = BIG[name]
    if ax == 1:
        sz = N // N_CHIPS
        return ref.at[:, pl.ds(pl.multiple_of(q * sz, 128), sz)]
    sz = K // N_CHIPS
    return ref.at[pl.ds(pl.multiple_of(q * sz, 16), sz), :]


def _shard_half_shape(name):
    K, N, ax = BIG[name]
    return (K // 2, N // N_CHIPS) if ax == 1 else (K // N_CHIPS, N // 2)


def _shard_shape(name):
    K, N, ax = BIG[name]
    return (K, N // N_CHIPS) if ax == 1 else (K // N_CHIPS, N)


def _pair_copies(names, srcs, lands, send_sems, recv_sems):
    x, y, c = _position()
    for idx, (name, src, land) in enumerate(zip(names, srcs, lands)):
        yield pltpu.make_async_remote_copy(
            src_ref=_half(src, name, 1 - c), dst_ref=land, send_sem=send_sems.at[idx], recv_sem=recv_sems.at[idx],
            device_id=(x, y, 1 - c), device_id_type=MESH_ID)


def _pair_exchange_start(tag, tensors):
    names = [n for n, _ in tensors]
    n = len(tensors)
    lands = [lax.empty(_half_shape(nm), f32) for nm in names]

    def body(*refs):
        for cp in _pair_copies(names, refs[:n], refs[n:2 * n], refs[2 * n], refs[2 * n + 1]):
            cp.start()
        refs[-1][...] = jnp.zeros_like(refs[-1])

    args = [g for _, g in tensors] + lands
    outs = pl.pallas_call(
        body, name="grad_pair_start_" + tag,
        out_shape=(pltpu.SemaphoreType.DMA((n,)), pltpu.SemaphoreType.DMA((n,)))
        + tuple(pltpu.HBM(a.shape, a.dtype) for a in args) + (jax.ShapeDtypeStruct((SUBLANES, LANES), f32),),
        in_specs=(HBM,) * (2 * n), out_specs=(SEM, SEM) + (HBM,) * (2 * n) + (pl.BlockSpec(memory_space=pltpu.VMEM),),
        input_output_aliases={t: 2 + t for t in range(2 * n)},
        compiler_params=pltpu.CompilerParams(has_side_effects=pltpu.SideEffectType.DATAFLOW_SIDE_EFFECTING),
    )(*[pltpu.with_memory_space_constraint(a, pltpu.HBM) for a in args])
    return (tag, names, outs[0], outs[1], outs[2:2 + 2 * n]), outs[-1]


def _pair_exchange_wait(state, after):
    tag, names, send_sems, recv_sems, bufs = state
    n = len(names)

    def body(*refs):
        for cp in _pair_copies(names, refs[:n], refs[n:2 * n], refs[2 * n], refs[2 * n + 1]):
            cp.wait_send()
            cp.wait_recv()

    outs = pl.pallas_call(
        body, name="grad_pair_wait_" + tag, out_shape=tuple(pltpu.HBM(a.shape, a.dtype) for a in bufs),
        in_specs=(HBM,) * (2 * n) + (SEM, SEM, ANY), out_specs=(HBM,) * (2 * n),
        input_output_aliases={t: t for t in range(2 * n)},
        compiler_params=pltpu.CompilerParams(has_side_effects=pltpu.SideEffectType.DATAFLOW_SIDE_EFFECTING),
    )(*bufs, send_sems, recv_sems, after)
    return list(zip(names, outs[:n], outs[n:]))


def _pair_add(g, rcv, name, c_arr):
    K, N, ax = BIG[name]
    hr, hc = _half_shape(name)
    T = 128
    nrt = hr // T

    def body(c_ref, g_ref, r_ref, o_ref):
        o_ref[...] = (g_ref[...] + r_ref[...]).astype(bf16)

    if ax == 1:
        g_spec = pl.BlockSpec((T, hc), lambda i, c: (c[0] * nrt + i, 0))
    else:
        g_spec = pl.BlockSpec((T, hc), lambda i, c: (i, c[0]))
    plain = pl.BlockSpec((T, hc), lambda i, c: (i, 0))
    return pl.pallas_call(
        body, name="grad_pair_add", out_shape=jax.ShapeDtypeStruct((hr, hc), bf16),
        grid_spec=pltpu.PrefetchScalarGridSpec(num_scalar_prefetch=1, grid=(nrt,), in_specs=[g_spec, plain],
                                               out_specs=plain),
        compiler_params=_cp(("parallel",), 32),
    )(c_arr, g, rcv)


def _chip_copies(names, srcs, lands, send_sems, recv_sems):
    x, y, c = _position()
    me = 2 * x + y
    idx = 0
    for name, src, land in zip(names, srcs, lands):
        for px, py in _other_chips(x, y):
            def copy(q, row, name=name, src=src, land=land, px=px, py=py, idx=idx):
                return pltpu.make_async_remote_copy(
                    src_ref=_shard_of_half(src, name, q), dst_ref=land.at[row], send_sem=send_sems.at[idx],
                    recv_sem=recv_sems.at[idx], device_id=(px, py, c), device_id_type=MESH_ID)
            yield copy(2 * px + py, me), copy(me, 2 * px + py)
            idx += 1


def _chip_exchange_start(tag, tensors):
    names = [n for n, _ in tensors]
    n = len(tensors)
    lands = [lax.empty((N_CHIPS,) + _shard_half_shape(nm), g.dtype) for nm, g in tensors]

    def body(*refs):
        send_sems, recv_sems = refs[2 * n:2 * n + 2]
        for sent, _ in _chip_copies(names, refs[:n], refs[n:2 * n], send_sems, recv_sems):
            sent.start()
        refs[-1][...] = jnp.zeros_like(refs[-1])

    args = [g for _, g in tensors] + lands
    outs = pl.pallas_call(
        body, name="grad_chip_start_" + tag,
        out_shape=(pltpu.SemaphoreType.DMA((3 * n,)), pltpu.SemaphoreType.DMA((3 * n,)))
        + tuple(pltpu.HBM(a.shape, a.dtype) for a in args) + (jax.ShapeDtypeStruct((SUBLANES, LANES), f32),),
        in_specs=(HBM,) * (2 * n), out_specs=(SEM, SEM) + (HBM,) * (2 * n) + (pl.BlockSpec(memory_space=pltpu.VMEM),),
        input_output_aliases={t: 2 + t for t in range(2 * n)},
        compiler_params=pltpu.CompilerParams(has_side_effects=pltpu.SideEffectType.DATAFLOW_SIDE_EFFECTING),
    )(*[pltpu.with_memory_space_constraint(a, pltpu.HBM) for a in args])
    return (tag, names, outs[0], outs[1], outs[2:2 + 2 * n]), outs[-1]


def _chip_exchange_wait(state, after):
    tag, names, send_sems, recv_sems, bufs = state
    n = len(names)

    def body(*refs):
        for sent, landed in _chip_copies(names, refs[:n], refs[n:2 * n], refs[2 * n], refs[2 * n + 1]):
            sent.wait_send()
            landed.wait_recv()

    outs = pl.pallas_call(
        body, name="grad_chip_wait_" + tag, out_shape=tuple(pltpu.HBM(a.shape, a.dtype) for a in bufs),
        in_specs=(HBM,) * (2 * n) + (SEM, SEM, ANY), out_specs=(HBM,) * (2 * n),
        input_output_aliases={t: t for t in range(2 * n)},
        compiler_params=pltpu.CompilerParams(has_side_effects=pltpu.SideEffectType.DATAFLOW_SIDE_EFFECTING),
    )(*bufs, send_sems, recv_sems, after)
    return list(zip(names, outs[:n], outs[n:]))


def _sum_chips(name, half, land, chip_arr):
    K, N, ax = BIG[name]
    R, C = _shard_half_shape(name)
    T = 64
    nrt = R // T

    def body(p_ref, own_ref, land_ref, o_ref):
        parts = [jnp.where(p_ref[0] == q, own_ref[...], land_ref[q]).astype(f32) for q in range(N_CHIPS)]
        o_ref[...] = ((parts[0] + parts[1]) + parts[2]) + parts[3]

    if ax == 1:
        own_spec = pl.BlockSpec((T, C), lambda i, p: (i, p[0]))
    else:
        own_spec = pl.BlockSpec((T, C), lambda i, p: (p[0] * nrt + i, 0))
    return pl.pallas_call(
        body, name="grad_sum_chips", out_shape=jax.ShapeDtypeStruct((R, C), f32),
        grid_spec=pltpu.PrefetchScalarGridSpec(
            num_scalar_prefetch=1, grid=(nrt,),
            in_specs=[own_spec, pl.BlockSpec((N_CHIPS, T, C), lambda i, p: (0, i, 0))],
            out_specs=pl.BlockSpec((T, C), lambda i, p: (i, 0))),
        compiler_params=_cp(("parallel",), 32),
    )(chip_arr, half, land)


def _pair_swap(halves):
    n_t = len(halves)

    def body(*refs):
        ins = refs[:n_t]
        outs = refs[n_t:2 * n_t]
        send_sems, recv_sems = refs[2 * n_t:]
        x, y, c = _position()
        cps = []
        for t in range(n_t):
            cp = pltpu.make_async_remote_copy(
                src_ref=ins[t], dst_ref=outs[t], send_sem=send_sems.at[t], recv_sem=recv_sems.at[t],
                device_id=(x, y, 1 - c), device_id_type=MESH_ID)
            cp.start()
            cps.append(cp)
        for cp in cps:
            cp.wait()

    return pl.pallas_call(
        body, name="grad_pair_swap", out_shape=tuple(jax.ShapeDtypeStruct(h.shape, h.dtype) for h in halves),
        in_specs=[ANY] * n_t, out_specs=tuple([ANY] * n_t),
        scratch_shapes=[pltpu.SemaphoreType.DMA((n_t,)), pltpu.SemaphoreType.DMA((n_t,))],
    )(*halves)


def _adamw_halves(own, other, w, m, v, name, l, c_arr, prev):
    K, N, ax = BIG[name]
    R, C = _shard_shape(name)
    hr, hc = _shard_half_shape(name)
    T = 64
    nrt = hr // T
    c1 = 1.0 / (1.0 - ADAM_B1 ** ADAM_STEP)
    c2 = 1.0 / (1.0 - ADAM_B2 ** ADAM_STEP)

    def body(c_ref, own_ref, oth_ref, w_ref, m_ref, v_ref, *rest):
        g_ref, d_ref, nm_ref, nv_ref = rest[-4:]
        gg = jnp.where(pl.program_id(0) == c_ref[0], own_ref[...], oth_ref[...])
        nm = ADAM_B1 * m_ref[...] + (1.0 - ADAM_B1) * gg
        nv = ADAM_B2 * v_ref[...] + (1.0 - ADAM_B2) * (gg * gg)
        g_ref[...] = gg
        nm_ref[...] = nm
        nv_ref[...] = nv
        d_ref[...] = -ADAM_LR * ((nm * c1) / (jnp.sqrt(nv * c2) + ADAM_EPS) + ADAM_WD * w_ref[...])

    half = pl.BlockSpec((T, hc), lambda h, i, c: (i, 0))
    if ax == 1:
        full = pl.BlockSpec((None, T, hc), lambda h, i, c: (l, h * nrt + i, 0))
    else:
        full = pl.BlockSpec((None, T, hc), lambda h, i, c: (l, i, h))
    sd = jax.ShapeDtypeStruct((DEPTH, R, C), f32)
    args = [c_arr, own, other, w, m, v]
    in_specs = [half, half, full, full, full]
    aliases = {}
    if prev is not None:
        args += list(prev)
        in_specs += [ANY] * 4
        aliases = {6 + k: k for k in range(4)}
    return pl.pallas_call(
        body, name="adamw_" + name, out_shape=(sd, sd, sd, sd),
        grid_spec=pltpu.PrefetchScalarGridSpec(num_scalar_prefetch=1, grid=(2, nrt), in_specs=in_specs,
                                               out_specs=(full, full, full, full)),
        input_output_aliases=aliases,
        compiler_params=_cp(("arbitrary", "arbitrary"), 32),
    )(*args)


class _GradExchange:
    GROUPS = (("l1", tuple((n, DEPTH - 1) for n in BIG)),
              ("l0_ffn", (("ffn_w_down", 0), ("ffn_w_up", 0))),
              ("l0_mix", (("w_out", 0), ("w_in", 0))))

    def __init__(self):
        self.c_arr = jnp.reshape(lax.axis_index("c"), (1,)).astype(jnp.int32)
        self.chip_arr = jnp.reshape(2 * lax.axis_index("x") + lax.axis_index("y"), (1,)).astype(jnp.int32)
        self.grads = {}
        self.pair_started = {}
        self.chip_started = {}

    def _advance(self, after, tok):
        for tag, _ in self.GROUPS:
            if tag not in self.pair_started or tag in self.chip_started:
                continue
            arrived = _pair_exchange_wait(self.pair_started[tag], after)
            pair = [(n, _pair_add(g, r, n, self.c_arr)) for n, g, r in arrived]
            self.chip_started[tag], token = _chip_exchange_start(tag, pair)
            tok = tok + token[0, 0]
        return tok

    def put(self, name, layer, g, tok):
        self.grads[(name, layer)] = g
        tok = self._advance(g, tok)
        for tag, keys in self.GROUPS:
            if tag in self.pair_started or not all(k in self.grads for k in keys):
                continue
            self.pair_started[tag], token = _pair_exchange_start(tag, [(n, self.grads[(n, l)]) for n, l in keys])
            tok = tok + token[0, 0]
        return tok

    def finish(self, after):
        self._advance(after, jnp.zeros((), f32))
        keys, own = [], []
        for tag, group in self.GROUPS:
            landed = _chip_exchange_wait(self.chip_started[tag], after)
            own += [_sum_chips(n, half, land, self.chip_arr) for n, half, land in landed]
            keys += list(group)
        other = _pair_swap(own)
        return dict(zip(keys, zip(own, other)))


def _small_allreduce(buf):
    R = buf.shape[0]
    n_dev = 8

    def body(in_ref, out_ref, slots, send_sems, recv_sems):
        x, y, c = _position()
        me = 4 * x + 2 * y + c
        slots[me] = in_ref[...]
        peers = []
        for k in range(1, n_dev):
            px = 1 - x if k & 4 else x
            py = 1 - y if k & 2 else y
            pc = 1 - c if k & 1 else c
            peers.append((px, py, pc))
        cps = []
        for k, peer in enumerate(peers):
            cp = pltpu.make_async_remote_copy(
                src_ref=in_ref, dst_ref=slots.at[me], send_sem=send_sems.at[k], recv_sem=recv_sems.at[k],
                device_id=peer, device_id_type=MESH_ID)
            cp.start()
            cps.append(cp)
        for k, (px, py, pc) in enumerate(peers):
            pltpu.make_async_remote_copy(
                src_ref=in_ref, dst_ref=slots.at[4 * px + 2 * py + pc], send_sem=send_sems.at[k],
                recv_sem=recv_sems.at[k], device_id=(px, py, pc), device_id_type=MESH_ID).wait_recv()
        for cp in cps:
            cp.wait_send()
        acc = slots[0]
        for dv in range(1, n_dev):
            acc = acc + slots[dv]
        out_ref[...] = acc

    vm = pl.BlockSpec(memory_space=pltpu.VMEM)
    return pl.pallas_call(
        body, name="small_allreduce", out_shape=jax.ShapeDtypeStruct((R, 128), f32), in_specs=[vm], out_specs=vm,
        scratch_shapes=[pltpu.VMEM((n_dev, R, 128), f32), pltpu.SemaphoreType.DMA((n_dev - 1,)),
                        pltpu.SemaphoreType.DMA((n_dev - 1,))],
        compiler_params=pltpu.CompilerParams(vmem_limit_bytes=40 * MIB),
    )(buf)


PACK_UNIT = 1024


def _pack(arrs):
    parts = []
    for a in arrs:
        flat = a.reshape(-1)
        n = -(-flat.shape[0] // PACK_UNIT) * PACK_UNIT
        parts.append(jnp.pad(flat, (0, n - flat.shape[0])))
    return jnp.concatenate(parts).reshape(-1, 128)


def _unpack(buf, shapes):
    flat = buf.reshape(-1)
    out, off = [], 0
    for shp in shapes:
        n = int(np.prod(shp))
        out.append(flat[off:off + n].reshape(shp))
        off += -(-n // PACK_UNIT) * PACK_UNIT
    return out


def kernel(x, w_in, b_in, conv_dw_w, conv_dw_b, conv_ln_g, conv_ln_b, rel_bias_table, gmlp_ln_g, gmlp_ln_b, gmlp_w_s, gmlp_b_s, w_out, b_out, ln1_g, ln1_b, ffn_w_up, ffn_b_up, ffn_conv_w, ffn_conv_b, ffn_w_down, ffn_b_down, ln2_g, ln2_b, loss_target, m_w_in, m_b_in, m_conv_dw_w, m_conv_dw_b, m_conv_ln_g, m_conv_ln_b, m_rel_bias_table, m_gmlp_ln_g, m_gmlp_ln_b, m_gmlp_w_s, m_gmlp_b_s, m_w_out, m_b_out, m_ln1_g, m_ln1_b, m_ffn_w_up, m_ffn_b_up, m_ffn_conv_w, m_ffn_conv_b, m_ffn_w_down, m_ffn_b_down, m_ln2_g, m_ln2_b, v_w_in, v_b_in, v_conv_dw_w, v_conv_dw_b, v_conv_ln_g, v_conv_ln_b, v_rel_bias_table, v_gmlp_ln_g, v_gmlp_ln_b, v_gmlp_w_s, v_gmlp_b_s, v_w_out, v_b_out, v_ln1_g, v_ln1_b, v_ffn_w_up, v_ffn_b_up, v_ffn_conv_w, v_ffn_conv_b, v_ffn_w_down, v_ffn_b_down, v_ln2_g, v_ln2_b):
    w = dict(w_in=w_in, b_in=b_in, conv_dw_w=conv_dw_w, conv_dw_b=conv_dw_b, conv_ln_g=conv_ln_g, conv_ln_b=conv_ln_b,
             rel_bias_table=rel_bias_table, gmlp_ln_g=gmlp_ln_g, gmlp_ln_b=gmlp_ln_b, gmlp_w_s=gmlp_w_s,
             gmlp_b_s=gmlp_b_s, w_out=w_out, b_out=b_out, ln1_g=ln1_g, ln1_b=ln1_b, ffn_w_up=ffn_w_up,
             ffn_b_up=ffn_b_up, ffn_conv_w=ffn_conv_w, ffn_conv_b=ffn_conv_b, ffn_w_down=ffn_w_down,
             ffn_b_down=ffn_b_down, ln2_g=ln2_g, ln2_b=ln2_b)
    m = dict(w_in=m_w_in, b_in=m_b_in, conv_dw_w=m_conv_dw_w, conv_dw_b=m_conv_dw_b, conv_ln_g=m_conv_ln_g,
             conv_ln_b=m_conv_ln_b, rel_bias_table=m_rel_bias_table, gmlp_ln_g=m_gmlp_ln_g, gmlp_ln_b=m_gmlp_ln_b,
             gmlp_w_s=m_gmlp_w_s, gmlp_b_s=m_gmlp_b_s, w_out=m_w_out, b_out=m_b_out, ln1_g=m_ln1_g, ln1_b=m_ln1_b,
             ffn_w_up=m_ffn_w_up, ffn_b_up=m_ffn_b_up, ffn_conv_w=m_ffn_conv_w, ffn_conv_b=m_ffn_conv_b,
             ffn_w_down=m_ffn_w_down, ffn_b_down=m_ffn_b_down, ln2_g=m_ln2_g, ln2_b=m_ln2_b)
    v = dict(w_in=v_w_in, b_in=v_b_in, conv_dw_w=v_conv_dw_w, conv_dw_b=v_conv_dw_b, conv_ln_g=v_conv_ln_g,
             conv_ln_b=v_conv_ln_b, rel_bias_table=v_rel_bias_table, gmlp_ln_g=v_gmlp_ln_g, gmlp_ln_b=v_gmlp_ln_b,
             gmlp_w_s=v_gmlp_w_s, gmlp_b_s=v_gmlp_b_s, w_out=v_w_out, b_out=v_b_out, ln1_g=v_ln1_g, ln1_b=v_ln1_b,
             ffn_w_up=v_ffn_w_up, ffn_b_up=v_ffn_b_up, ffn_conv_w=v_ffn_conv_w, ffn_conv_b=v_ffn_conv_b,
             ffn_w_down=v_ffn_w_down, ffn_b_down=v_ffn_b_down, ln2_g=v_ln2_g, ln2_b=v_ln2_b)

    chip_arr = jnp.reshape(2 * lax.axis_index("x") + lax.axis_index("y"), (1,)).astype(jnp.int32)
    shards = {"w_in": _cast_bf16(w_in.reshape(-1, w_in.shape[-1])).reshape(w_in.shape)}
    wb, conv_stack, fconv_stack = _gather_weights(shards, conv_dw_w, ffn_conv_w)
    send_sems, recv_sems, in_flight, token = _gather_start(
        [_cast_into_full(w[n], n, chip_arr) for n in LATE_WEIGHTS], conv_stack)
    sp = {n: w[n] for n in SMALL}
    sp["conv_dw_w"] = jnp.moveaxis(conv_stack, 0, 2).reshape(DEPTH, CONV_WIDTH, CONV_CH)
    sp["ffn_conv_w"] = jnp.moveaxis(fconv_stack, 0, 2).reshape(DEPTH, FFN_CONV_WIDTH, 2 * D_FF)
    sp["b_in"] = sp["b_in"] + token[0, 0]

    def late_weights(after):
        return dict(zip(LATE_WEIGHTS, _gather_wait(send_sems, recv_sems, in_flight, after)))

    sink = _GradExchange()
    loss_local, grad_x, grads, big = _local_step(x[0], loss_target[0], wb, late_weights, sp, sink)

    small_shapes = [(1,)] + [grads[n].shape for n in SMALL]
    summed = _unpack(_small_allreduce(_pack([loss_local.reshape(1)] + [grads[n] for n in SMALL])), small_shapes)
    loss = summed[0].reshape(())
    small = dict(zip(SMALL, summed[1:]))
    chip = 2 * lax.axis_index("x") + lax.axis_index("y")
    for n in SMALL_SHARDED:
        width = w[n].shape[-1]
        small[n] = lax.dynamic_slice_in_dim(small[n], chip * width, width, axis=2)

    g_out, d_out, m_out, v_out = {}, {}, {}, {}
    for n in BIG:
        outs = None
        for l in range(DEPTH):
            own, other = big[(n, l)]
            outs = _adamw_halves(own, other, w[n], m[n], v[n], n, l, sink.c_arr, outs)
        g_out[n], d_out[n], m_out[n], v_out[n] = outs
    shapes = [small[n].shape for n in SMALL]
    packed = [_pack([src[n] for n in SMALL]) for src in (small, w, m, v)]
    upd = _adamw(*packed, "adamw_small")
    for dst, buf in zip((d_out, m_out, v_out), upd):
        dst.update(zip(SMALL, _unpack(buf, shapes)))
    g_out.update(small)

    return (loss, grad_x[None], *[g_out[n] for n in WEIGHTS], *[d_out[n] for n in WEIGHTS],
            *[m_out[n] for n in WEIGHTS], *[v_out[n] for n in WEIGHTS])
```

```python
import functools
import math

import numpy as np
import jax
import jax.numpy as jnp
from jax import lax
from jax.experimental import pallas as pl
from jax.experimental.pallas import tpu as pltpu

f32 = jnp.float32
bf16 = jnp.bfloat16

D_MODEL = 1024
DEPTH = 2
HEAD_DIM = 64
CONV_CH = 256
CONV_WIDTH = 31
ATTN_HEADS = 8
ATTN_CH = ATTN_HEADS * HEAD_DIM
DILATIONS = (1, 4, 16)
ATTN_BLOCK = 128
N_BUCKETS = 32
MAX_DISTANCE = 2048
GMLP_CH = 256
GMLP_GROUPS = 4
GMLP_GROUP_DIM = GMLP_CH // GMLP_GROUPS
CHUNK = 128
IN_CH = 2 * CONV_CH + 3 * ATTN_CH + 2 * GMLP_CH
D_FF = 2816
FFN_CONV_WIDTH = 3
LN_EPS = 1e-5
ALPHA = (2.0 * DEPTH) ** 0.25
ADAM_LR = 0.001
ADAM_B1 = 0.9
ADAM_B2 = 0.999
ADAM_EPS = 1e-08
ADAM_WD = 0.01
ADAM_STEP = 10

CONV_HALO = 32
FFN_HALO = 8
NEG = -1e30
MIB = 2 ** 20
NT_DIMS = (((1,), (1,)), ((), ()))
TN_DIMS = (((0,), (0,)), ((), ()))
MESH_ID = pl.DeviceIdType.MESH


def _cp(sem, vmem_mib):
    return pltpu.CompilerParams(dimension_semantics=sem, vmem_limit_bytes=vmem_mib * MIB)


def _resident(shape):
    nd = len(shape)
    return pl.BlockSpec(shape, lambda *_: (0,) * nd, pipeline_mode=pl.Buffered(1))


def _acc(shape):
    nd = len(shape)
    return pl.BlockSpec(shape, lambda *_: (0,) * nd)


def _sig(x):
    return 1.0 / (1.0 + jnp.exp(-x))


def _ln_stats(z):
    mu = jnp.mean(z, axis=-1, keepdims=True)
    zc = z - mu
    var = jnp.mean(zc * zc, axis=-1, keepdims=True)
    rstd = lax.rsqrt(var + LN_EPS)
    return zc * rstd, rstd


def _ln_bwd(dy, xhat, rstd, g):
    dxh = dy * g
    m1 = jnp.mean(dxh, axis=-1, keepdims=True)
    m2 = jnp.mean(dxh * xhat, axis=-1, keepdims=True)
    return rstd * (dxh - m1 - xhat * m2)


def _colsum(x):
    return jnp.sum(x, axis=0, keepdims=True)


def _t5_bucket_np(dist):
    max_exact = N_BUCKETS // 2
    dd = np.maximum(dist, 1).astype(np.float64)
    large = max_exact + (np.log(dd / max_exact) / math.log(MAX_DISTANCE / max_exact)
                         * (N_BUCKETS - max_exact)).astype(np.int32)
    large = np.minimum(large, N_BUCKETS - 1)
    return np.where(dist < max_exact, dist, large).astype(np.int32)


def _bucket_ids():
    qi = np.arange(ATTN_BLOCK)[:, None]
    kj = np.arange(2 * ATTN_BLOCK)[None, :]
    dist = np.clip(qi + ATTN_BLOCK - kj, 0, None)
    return np.stack([_t5_bucket_np(dist * d) for d in DILATIONS]).astype(np.int32)


LANES = 128
QKV_CH = 3 * ATTN_CH
PERM_TILE = 512


def _slabs(n, rows):
    return [pltpu.VMEM((rows, LANES), f32)] * n


def _rows_of(slab, r, n, d):
    return slab[...] if d == 1 else slab[pl.ds(r, n, stride=d), :]


def _set_rows_of(slab, r, n, d, val):
    if d == 1:
        slab[...] = val
    else:
        slab[pl.ds(r, n, stride=d), :] = val


def _perm_spec(d, ch):
    return pl.BlockSpec((d, PERM_TILE // d, ch), lambda i: (0, i, 0))


def _perm_shape(S, d, ch, dtype):
    return jax.ShapeDtypeStruct((d, S // d, ch), dtype)


def _inproj_fwd(x, w, b):
    S = x.shape[0]
    T = PERM_TILE
    nsl = QKV_CH // LANES

    def body(x_ref, w_ref, b_ref, a_ref, c_ref, *rest):
        q_refs = rest[:len(DILATIONS)]
        slabs = rest[len(DILATIONS):]
        h = jnp.dot(x_ref[...].astype(bf16), w_ref[...], preferred_element_type=f32) + b_ref[...]
        a_ref[...] = h[:, :2 * CONV_CH]
        q0 = 2 * CONV_CH
        c_ref[...] = h[:, q0 + QKV_CH:]
        for j in range(nsl):
            piece = h[:, q0 + LANES * j:q0 + LANES * (j + 1)]
            if LANES * j < ATTN_CH:
                piece = piece * (HEAD_DIM ** -0.5)
            slabs[j][...] = piece
        for d, q_ref in zip(DILATIONS, q_refs):
            for r in range(d):
                for j in range(nsl):
                    q_ref[r, :, LANES * j:LANES * (j + 1)] = _rows_of(slabs[j], r, T // d, d).astype(bf16)

    row = lambda c: pl.BlockSpec((T, c), lambda i: (i, 0))
    return pl.pallas_call(
        body, grid=(S // T,), name="inproj_fwd",
        out_shape=(jax.ShapeDtypeStruct((S, 2 * CONV_CH), f32), jax.ShapeDtypeStruct((S, 2 * GMLP_CH), f32))
        + tuple(_perm_shape(S, d, QKV_CH, bf16) for d in DILATIONS),
        in_specs=[row(D_MODEL), _resident((D_MODEL, IN_CH)), _resident((1, IN_CH))],
        out_specs=(row(2 * CONV_CH), row(2 * GMLP_CH)) + tuple(_perm_spec(d, QKV_CH) for d in DILATIONS),
        scratch_shapes=_slabs(nsl, T),
        compiler_params=_cp(("parallel",), 48),
    )(x, w, b)


CONV_GROUP = 64


def _window_rolls(starts):
    groups = {}
    for s in starts:
        groups.setdefault((-s) % SUBLANES, []).append(s)
    return dict(sorted(groups.items()))


def _conv_fwd(a_in, dw_w, dw_b, ln_g, ln_b):
    S = a_in.shape[0]
    T = 512
    hb = T // CONV_HALO

    def body(a_ref, halo_ref, w_ref, b_ref, g_ref, be_ref, out_ref, hc_ref, buf):
        i = pl.program_id(0)
        am = a_ref[...]
        ah = halo_ref[...]
        hgh = ah[:, :CONV_CH] * _sig(ah[:, CONV_CH:])
        buf[0:CONV_HALO, :] = jnp.where(i > 0, hgh, 0.0)
        buf[CONV_HALO:, :] = am[:, :CONV_CH] * _sig(am[:, CONV_CH:])
        starts = _window_rolls(range(CONV_HALO - (CONV_WIDTH - 1), CONV_HALO + 1))
        slabs = [slice(LANES * j, LANES * (j + 1)) for j in range(CONV_CH // LANES)]

        def step(g, _):
            r0 = pl.multiple_of(g * CONV_GROUP, CONV_GROUP)
            rows = pl.ds(r0, CONV_GROUP)
            for cs in slabs:
                ext = buf[pl.ds(r0, CONV_GROUP + CONV_HALO), cs]
                acc = jnp.broadcast_to(b_ref[:, cs], (CONV_GROUP, LANES))
                for b, ss in starts.items():
                    rolled = ext if b == 0 else pltpu.roll(ext, b, 0)
                    for s in ss:
                        k = s - (CONV_HALO - (CONV_WIDTH - 1))
                        acc = acc + w_ref[k:k + 1, cs] * rolled[s + b:s + b + CONV_GROUP]
                hc_ref[rows, cs] = acc
            return 0

        lax.fori_loop(0, T // CONV_GROUP, step, 0)
        xhat, _ = _ln_stats(hc_ref[...])
        y = xhat * g_ref[...] + be_ref[...]
        out_ref[...] = (y * _sig(y)).astype(bf16)

    return pl.pallas_call(
        body, grid=(S // T,), name="conv_fwd",
        out_shape=(jax.ShapeDtypeStruct((S, CONV_CH), bf16), jax.ShapeDtypeStruct((S, CONV_CH), f32)),
        in_specs=[pl.BlockSpec((T, 2 * CONV_CH), lambda i: (i, 0)),
                  pl.BlockSpec((CONV_HALO, 2 * CONV_CH), lambda i: (jnp.maximum(i * hb - 1, 0), 0)),
                  _acc((32, CONV_CH)), _acc((1, CONV_CH)), _acc((1, CONV_CH)), _acc((1, CONV_CH))],
        out_specs=(pl.BlockSpec((T, CONV_CH), lambda i: (i, 0)), pl.BlockSpec((T, CONV_CH), lambda i: (i, 0))),
        scratch_shapes=[pltpu.VMEM((T + CONV_HALO, CONV_CH), f32)],
        compiler_params=_cp(("parallel",), 32),
    )(a_in, a_in, dw_w, dw_b, ln_g, ln_b)


def _bias_build(table, buckets):
    def body(t_ref, bk_ref, o_ref):
        h = pl.program_id(1)
        ids = bk_ref[0]
        acc = jnp.zeros((ATTN_BLOCK, 2 * ATTN_BLOCK), f32)
        for b in range(N_BUCKETS):
            acc = jnp.where(ids == b, t_ref[b, h], acc)
        o_ref[0, 0] = acc

    return pl.pallas_call(
        body, grid=(len(DILATIONS), ATTN_HEADS), name="bias_build",
        out_shape=jax.ShapeDtypeStruct((len(DILATIONS), ATTN_HEADS, ATTN_BLOCK, 2 * ATTN_BLOCK), f32),
        in_specs=[pl.BlockSpec(memory_space=pltpu.SMEM),
                  pl.BlockSpec((1, ATTN_BLOCK, 2 * ATTN_BLOCK), lambda p, h: (p, 0, 0))],
        out_specs=pl.BlockSpec((1, 1, ATTN_BLOCK, 2 * ATTN_BLOCK), lambda p, h: (p, h, 0, 0)),
        compiler_params=_cp(("arbitrary", "arbitrary"), 16),
    )(table, buckets)


def _head_tile(tile, h, col):
    lane_head = lax.broadcasted_iota(jnp.int32, tile.shape, 1) // 16
    return jnp.where(lane_head == h, col, tile)


HEAD_PAIRS = ATTN_HEADS // 2
UNITS_PER_BLOCK = ATTN_HEADS


def _attn_tile(L):
    return min(512, L)


def _band_mask(first_block, n):
    B = ATTN_BLOCK
    row = lax.broadcasted_iota(jnp.int32, (B, 2 * B), 0)
    col = lax.broadcasted_iota(jnp.int32, (B, 2 * B), 1)
    valid = (col >= row) & (col <= row + B)
    if first_block:
        valid = valid & ((col >= B) | (n > 0))
    return valid


def _head_lanes(a):
    lane = lax.broadcasted_iota(jnp.int32, (ATTN_BLOCK, LANES), 1)
    return (lane < HEAD_DIM) if a == 0 else (lane >= HEAD_DIM)


def _pair_keys(cur_ref, halo_ref, part, b, j):
    B = ATTN_BLOCK
    c0 = part * ATTN_CH + LANES * j
    own = cur_ref[B * b:B * (b + 1), c0:c0 + LANES]
    prev = halo_ref[:, LANES * j:LANES * (j + 1)] if b == 0 else cur_ref[B * (b - 1):B * b, c0:c0 + LANES]
    return jnp.concatenate([prev, own], axis=0)


def _attn_fwd_pattern(qkv, bias, d):
    _, L, _ = qkv.shape
    B = ATTN_BLOCK
    QB = _attn_tile(L)
    nsb = QB // B
    U = nsb * UNITS_PER_BLOCK

    def body(cur_ref, hk_ref, hv_ref, b_ref, o_ref, lse_ref, lg, pb):
        n = pl.program_id(1)
        for b in range(nsb):
            valid = _band_mask(b == 0, n)
            for j in range(HEAD_PAIRS):
                q2 = cur_ref[B * b:B * (b + 1), LANES * j:LANES * (j + 1)]
                k2 = _pair_keys(cur_ref, hk_ref, 1, b, j)
                for a in range(2):
                    u = (b * HEAD_PAIRS + j) * 2 + a
                    qm = jnp.where(_head_lanes(a), q2, jnp.zeros_like(q2))
                    logits = lax.dot_general(qm, k2, NT_DIMS, preferred_element_type=f32) + b_ref[2 * j + a]
                    lg[B * u:B * (u + 1), :] = jnp.where(valid, logits, NEG)
        m = jnp.max(lg[...], axis=1, keepdims=True)
        p = jnp.exp(lg[...] - m)
        s = jnp.sum(p, axis=1, keepdims=True)
        pb[...] = p.astype(bf16)
        lse = m + jnp.log(s)
        inv = 1.0 / s
        for b in range(nsb):
            tile = jnp.zeros((B, B), f32)
            for j in range(HEAD_PAIRS):
                v2 = _pair_keys(cur_ref, hv_ref, 2, b, j)
                outs = []
                for a in range(2):
                    u = (b * HEAD_PAIRS + j) * 2 + a
                    rows = slice(B * u, B * (u + 1))
                    outs.append(jnp.dot(pb[rows, :], v2, preferred_element_type=f32) * inv[rows])
                    tile = _head_tile(tile, 2 * j + a, lse[rows])
                o_ref[B * b:B * (b + 1), LANES * j:LANES * (j + 1)] = jnp.where(_head_lanes(0), outs[0], outs[1])
            lse_ref[B * b:B * (b + 1), :] = tile

    halo = lambda part: pl.BlockSpec((None, B, ATTN_CH), lambda r, n: (r, jnp.maximum(n * nsb - 1, 0), part))
    tile_spec = lambda c: pl.BlockSpec((None, QB, c), lambda r, n: (r, n, 0))
    return pl.pallas_call(
        body, grid=(d, L // QB), name=f"attn_fwd_d{d}",
        out_shape=(jax.ShapeDtypeStruct((d, L, ATTN_CH), f32), jax.ShapeDtypeStruct((d, L, B), f32)),
        in_specs=[tile_spec(QKV_CH), halo(1), halo(2), _resident((ATTN_HEADS, B, 2 * B))],
        out_specs=(tile_spec(ATTN_CH), tile_spec(B)),
        scratch_shapes=[pltpu.VMEM((U * B, 2 * B), f32), pltpu.VMEM((U * B, 2 * B), bf16)],
        compiler_params=_cp(("parallel", "parallel"), 40),
    )(qkv, qkv, qkv, bias)


def _attn_merge(parts):
    S = parts[0][0].shape[0] * parts[0][0].shape[1]
    T = PERM_TILE
    nsl = ATTN_CH // LANES
    n_p = len(DILATIONS)

    def body(*refs):
        ins = refs[:2 * n_p]
        out_ref, lse_ref = refs[2 * n_p:2 * n_p + 2]
        slabs = refs[2 * n_p + 2:]
        lses = []
        for p, d in enumerate(DILATIONS):
            o_ref, l_ref = ins[2 * p], ins[2 * p + 1]
            osl = slabs[p * (nsl + 1):p * (nsl + 1) + nsl]
            lsl = slabs[p * (nsl + 1) + nsl]
            for r in range(d):
                for j in range(nsl):
                    _set_rows_of(osl[j], r, T // d, d, o_ref[r, :, LANES * j:LANES * (j + 1)])
                _set_rows_of(lsl, r, T // d, d, l_ref[r])
            lses.append(lsl[...])
        big = functools.reduce(jnp.maximum, lses)
        ws = [jnp.exp(l - big) for l in lses]
        tot = functools.reduce(lambda a_, b_: a_ + b_, ws)
        lse_ref[...] = big + jnp.log(tot)
        ws = [w / tot for w in ws]
        for j in range(nsl):
            acc = jnp.zeros((T, LANES), f32)
            for p in range(n_p):
                wa = ws[p][:, 32 * j:32 * j + 1]
                wb = ws[p][:, 32 * j + 16:32 * j + 17]
                lane = lax.broadcasted_iota(jnp.int32, (T, LANES), 1)
                acc = acc + jnp.where(lane < HEAD_DIM, wa, wb) * slabs[p * (nsl + 1) + j][...]
            out_ref[:, LANES * j:LANES * (j + 1)] = acc.astype(bf16)

    in_specs, args = [], []
    for (o, l), d in zip(parts, DILATIONS):
        in_specs += [_perm_spec(d, ATTN_CH), _perm_spec(d, ATTN_BLOCK)]
        args += [o, l]
    row = lambda c: pl.BlockSpec((T, c), lambda i: (i, 0))
    return pl.pallas_call(
        body, grid=(S // T,), name="attn_merge",
        out_shape=(jax.ShapeDtypeStruct((S, ATTN_CH), bf16), jax.ShapeDtypeStruct((S, ATTN_BLOCK), f32)),
        in_specs=in_specs, out_specs=(row(ATTN_CH), row(ATTN_BLOCK)),
        scratch_shapes=_slabs(n_p * (nsl + 1), T),
        compiler_params=_cp(("parallel",), 40),
    )(*args)


def _attn_fwd(qkvs, bias):
    parts = [_attn_fwd_pattern(q, bias[p], d) for p, (q, d) in enumerate(zip(qkvs, DILATIONS))]
    return _attn_merge(parts)


def _tril_bf16(w):
    row = lax.broadcasted_iota(jnp.int32, (CHUNK, CHUNK), 0)
    col = lax.broadcasted_iota(jnp.int32, (CHUNK, CHUNK), 1)
    return jnp.where(col <= row, w, 0.0).astype(bf16)


def _gmlp_fwd(c_in, ln_g, ln_b, w_s, b_s_t):
    S = c_in.shape[0]
    T = 512

    def body(c_ref, g_ref, be_ref, w_ref, bs_ref, out_ref, mix):
        c = c_ref[...]
        xhat, _ = _ln_stats(c[:, GMLP_CH:])
        vb = (xhat * g_ref[...] + be_ref[...]).astype(bf16)
        for g in range(GMLP_GROUPS):
            wt = _tril_bf16(w_ref[g])
            cs = slice(GMLP_GROUP_DIM * g, GMLP_GROUP_DIM * (g + 1))
            for ci in range(T // CHUNK):
                rs = slice(CHUNK * ci, CHUNK * (ci + 1))
                mix[rs, cs] = jnp.dot(wt, vb[rs, cs], preferred_element_type=f32) + bs_ref[:, g:g + 1]
        out_ref[...] = (c[:, :GMLP_CH] * mix[...]).astype(bf16)

    return pl.pallas_call(
        body, grid=(S // T,), name="gmlp_fwd",
        out_shape=jax.ShapeDtypeStruct((S, GMLP_CH), bf16),
        in_specs=[pl.BlockSpec((T, 2 * GMLP_CH), lambda i: (i, 0)), _acc((1, GMLP_CH)), _acc((1, GMLP_CH)),
                  _acc((GMLP_GROUPS, CHUNK, CHUNK)), _acc((CHUNK, GMLP_GROUPS))],
        out_specs=pl.BlockSpec((T, GMLP_CH), lambda i: (i, 0)),
        scratch_shapes=[pltpu.VMEM((T, GMLP_CH), f32)],
        compiler_params=_cp(("parallel",), 32),
    )(c_in, ln_g, ln_b, w_s, b_s_t)


def _outproj_ln_fwd(conv_out, attn_out, gm_out, w, b, x, ln_g, ln_b):
    S = x.shape[0]
    T = 512

    def body(co_ref, ao_ref, go_ref, w_ref, b_ref, x_ref, g_ref, be_ref, cat_ref, z_ref, yb_ref):
        cat = jnp.concatenate([co_ref[...], ao_ref[...], go_ref[...]], axis=1)
        cat_ref[...] = cat
        z = jnp.dot(cat, w_ref[...], preferred_element_type=f32) + b_ref[...] + ALPHA * x_ref[...]
        z_ref[...] = z
        xhat, _ = _ln_stats(z)
        yb_ref[...] = (xhat * g_ref[...] + be_ref[...]).astype(bf16)

    row = lambda c: pl.BlockSpec((T, c), lambda i: (i, 0))
    return pl.pallas_call(
        body, grid=(S // T,), name="outproj_ln_fwd",
        out_shape=(jax.ShapeDtypeStruct((S, D_MODEL), bf16), jax.ShapeDtypeStruct((S, D_MODEL), f32),
                   jax.ShapeDtypeStruct((S, D_MODEL), bf16)),
        in_specs=[row(CONV_CH), row(ATTN_CH), row(GMLP_CH), _resident((D_MODEL, D_MODEL)), _acc((1, D_MODEL)),
                  row(D_MODEL), _acc((1, D_MODEL)), _acc((1, D_MODEL))],
        out_specs=(row(D_MODEL), row(D_MODEL), row(D_MODEL)),
        compiler_params=_cp(("parallel",), 40),
    )(conv_out, attn_out, gm_out, w, b, x, ln_g, ln_b)


GATE_ROWS = 32
GATE_COLS = 128
GATE_MM_COLS = 256
SUBLANES = 8


def _gate_cols(c0):
    return slice(c0, c0 + GATE_COLS), slice(D_FF + c0, D_FF + c0 + GATE_COLS)


def _bcast_rows(ref, k, cs):
    return jnp.broadcast_to(ref[k:k + 1, cs], (GATE_ROWS, GATE_COLS))


def _fold_rows(z):
    acc = z[0:SUBLANES]
    for r in range(SUBLANES, GATE_ROWS, SUBLANES):
        acc = acc + z[r:r + SUBLANES]
    return acc


def _ffn_up_gate_fwd(x1b, w, b, conv_w, conv_b):
    S = x1b.shape[0]
    T = 256
    H = FFN_HALO
    K = FFN_CONV_WIDTH

    def body(x_ref, w_ref, b_ref, cw_ref, cb_ref, hfb_ref, hc_ref, act_ref, hbuf, carry):
        @pl.when(pl.program_id(0) == 0)
        def _():
            carry[...] = jnp.zeros_like(carry)
        x = x_ref[...]
        for m0 in range(0, D_FF, GATE_MM_COLS):
            for cm in (slice(m0, m0 + GATE_MM_COLS), slice(D_FF + m0, D_FF + m0 + GATE_MM_COLS)):
                h = jnp.dot(x, w_ref[:, cm], preferred_element_type=f32) + b_ref[:, cm]
                hbuf[:, cm] = h
                hfb_ref[:, cm] = h.astype(bf16)
            for c0 in range(m0, m0 + GATE_MM_COLS, GATE_COLS):
                cols = _gate_cols(c0)
                wts = [[_bcast_rows(cw_ref, k, cs) for k in range(K)] + [_bcast_rows(cb_ref, 0, cs)] for cs in cols]

                def step(rg, tails, cols=cols, wts=wts):
                    rows = pl.ds(pl.multiple_of(rg * GATE_ROWS, GATE_ROWS), GATE_ROWS)
                    hc, new_tails = [], []
                    for cs, wt, tail in zip(cols, wts, tails):
                        h = hbuf[rows, cs]
                        ext = jnp.concatenate([tail, h], axis=0)
                        acc = wt[K] + wt[K - 1] * h
                        for back in range(1, K):
                            acc = acc + wt[K - 1 - back] * pltpu.roll(ext, back, 0)[H:]
                        hc_ref[rows, cs] = acc
                        hc.append(acc)
                        new_tails.append(h[GATE_ROWS - H:])
                    act_ref[rows, cols[0]] = (hc[0] * _sig(hc[0]) * hc[1]).astype(bf16)
                    return tuple(new_tails)

                tails = lax.fori_loop(0, T // GATE_ROWS, step, tuple(carry[:, cs] for cs in cols), unroll=True)
                for cs, tail in zip(cols, tails):
                    carry[:, cs] = tail

    row = lambda c: pl.BlockSpec((T, c), lambda i: (i, 0))
    return pl.pallas_call(
        body, grid=(S // T,), name="ffn_up_gate_fwd",
        out_shape=(jax.ShapeDtypeStruct((S, 2 * D_FF), bf16), jax.ShapeDtypeStruct((S, 2 * D_FF), f32),
                   jax.ShapeDtypeStruct((S, D_FF), bf16)),
        in_specs=[row(D_MODEL), _resident((D_MODEL, 2 * D_FF)), _acc((1, 2 * D_FF)), _acc((8, 2 * D_FF)),
                  _acc((1, 2 * D_FF))],
        out_specs=(row(2 * D_FF), row(2 * D_FF), row(D_FF)),
        scratch_shapes=[pltpu.VMEM((T, 2 * D_FF), f32), pltpu.VMEM((H, 2 * D_FF), f32)],
        compiler_params=_cp(("arbitrary",), 56),
    )(x1b, w, b, conv_w, conv_b)


def _ffn_down_ln_fwd(act, w, b, z1, ln1_g, ln1_b, ln_g, ln_b):
    S = act.shape[0]
    T = 512

    def body(a_ref, w_ref, b_ref, z1_ref, g1_ref, be1_ref, g_ref, be_ref, z_ref, y_ref):
        x1 = _ln_stats(z1_ref[...])[0] * g1_ref[...] + be1_ref[...]
        z = jnp.dot(a_ref[...], w_ref[...], preferred_element_type=f32) + b_ref[...] + ALPHA * x1
        z_ref[...] = z
        xhat, _ = _ln_stats(z)
        y_ref[...] = xhat * g_ref[...] + be_ref[...]

    row = lambda c: pl.BlockSpec((T, c), lambda i: (i, 0))
    return pl.pallas_call(
        body, grid=(S // T,), name="ffn_down_ln_fwd",
        out_shape=(jax.ShapeDtypeStruct((S, D_MODEL), f32), jax.ShapeDtypeStruct((S, D_MODEL), f32)),
        in_specs=[row(D_FF), _resident((D_FF, D_MODEL)), _acc((1, D_MODEL)), row(D_MODEL)] + [_acc((1, D_MODEL))] * 4,
        out_specs=(row(D_MODEL), row(D_MODEL)),
        compiler_params=_cp(("parallel",), 40),
    )(act, w, b, z1, ln1_g, ln1_b, ln_g, ln_b)


def _ffn_down_ln_loss(act, w, b, z1, ln1_g, ln1_b, ln_g, ln_b, target):
    S = act.shape[0]
    T = 512

    def body(a_ref, w_ref, b_ref, z1_ref, g1_ref, be1_ref, g_ref, be_ref, t_ref, dz_ref, dzb_ref, loss_ref, dg_ref,
             db_ref):
        @pl.when(pl.program_id(0) == 0)
        def _():
            loss_ref[...] = jnp.zeros_like(loss_ref)
            dg_ref[...] = jnp.zeros_like(dg_ref)
            db_ref[...] = jnp.zeros_like(db_ref)
        x1 = _ln_stats(z1_ref[...])[0] * g1_ref[...] + be1_ref[...]
        z = jnp.dot(a_ref[...], w_ref[...], preferred_element_type=f32) + b_ref[...] + ALPHA * x1
        xhat, rstd = _ln_stats(z)
        err = xhat * g_ref[...] + be_ref[...] - t_ref[...]
        loss_ref[...] += _colsum(err * err) * (0.5 / D_MODEL)
        dy = err * (1.0 / D_MODEL)
        dz = _ln_bwd(dy, xhat, rstd, g_ref[...])
        dz_ref[...] = dz
        dzb_ref[...] = dz.astype(bf16)
        dg_ref[...] += _colsum(dy * xhat)
        db_ref[...] += _colsum(dy)

    row = lambda c: pl.BlockSpec((T, c), lambda i: (i, 0))
    vec = jax.ShapeDtypeStruct((1, D_MODEL), f32)
    return pl.pallas_call(
        body, grid=(S // T,), name="ffn_down_ln_loss",
        out_shape=(jax.ShapeDtypeStruct((S, D_MODEL), f32), jax.ShapeDtypeStruct((S, D_MODEL), bf16), vec, vec, vec),
        in_specs=[row(D_FF), _resident((D_FF, D_MODEL)), _acc((1, D_MODEL)), row(D_MODEL)] + [_acc((1, D_MODEL))] * 4
        + [row(D_MODEL)],
        out_specs=(row(D_MODEL), row(D_MODEL), _acc((1, D_MODEL)), _acc((1, D_MODEL)), _acc((1, D_MODEL))),
        compiler_params=_cp(("arbitrary",), 40),
    )(act, w, b, z1, ln1_g, ln1_b, ln_g, ln_b, target)


def _dgrad_ln_bwd(g, w, dz_res, z, ln_g, name):
    S, K = g.shape
    T = 256
    with_ln = z is not None

    def body(*refs):
        if with_ln:
            g_ref, w_ref, r_ref, z_ref, lg_ref, dz_ref, dzb_ref, dg_ref, db_ref = refs
        else:
            g_ref, w_ref, r_ref, dx_ref = refs
        dx = lax.dot_general(g_ref[...], w_ref[...], NT_DIMS, preferred_element_type=f32) + ALPHA * r_ref[...]
        if not with_ln:
            dx_ref[...] = dx
            return

        @pl.when(pl.program_id(0) == 0)
        def _():
            dg_ref[...] = jnp.zeros_like(dg_ref)
            db_ref[...] = jnp.zeros_like(db_ref)
        xhat, rstd = _ln_stats(z_ref[...])
        dz = _ln_bwd(dx, xhat, rstd, lg_ref[...])
        dz_ref[...] = dz
        dzb_ref[...] = dz.astype(bf16)
        dg_ref[...] += _colsum(dx * xhat)
        db_ref[...] += _colsum(dx)

    row = pl.BlockSpec((T, D_MODEL), lambda i: (i, 0))
    vec = jax.ShapeDtypeStruct((1, D_MODEL), f32)
    in_specs = [pl.BlockSpec((T, K), lambda i: (i, 0)), _resident((D_MODEL, K)), row]
    args = [g, w, dz_res]
    if with_ln:
        in_specs += [row, _acc((1, D_MODEL))]
        args += [z, ln_g]
        out_shape = (jax.ShapeDtypeStruct((S, D_MODEL), f32), jax.ShapeDtypeStruct((S, D_MODEL), bf16), vec, vec)
        out_specs = (row, row, _acc((1, D_MODEL)), _acc((1, D_MODEL)))
    else:
        out_shape = jax.ShapeDtypeStruct((S, D_MODEL), f32)
        out_specs = row
    return pl.pallas_call(
        body, grid=(S // T,), name=name, out_shape=out_shape, in_specs=in_specs, out_specs=out_specs,
        compiler_params=_cp(("arbitrary",), 48),
    )(*args)


def _ffn_down_gate_bwd(dzb, w_down, hfb, hc, conv_w):
    S = hc.shape[0]
    T = 256
    H = FFN_HALO
    nt = S // T
    K = FFN_CONV_WIDTH

    def body(dz_ref, w_ref, h_ref, hc_ref, cw_ref, dh_ref, dw_ref, dcb_ref, da_buf, carry):
        @pl.when(pl.program_id(0) == 0)
        def _():
            dw_ref[...] = jnp.zeros_like(dw_ref)
            dcb_ref[...] = jnp.zeros_like(dcb_ref)
            carry[...] = jnp.zeros_like(carry)
        da_buf[...] = lax.dot_general(dz_ref[...], w_ref[...], NT_DIMS, preferred_element_type=f32)
        ngroups = T // GATE_ROWS
        for c0 in range(0, D_FF, GATE_COLS):
            cols = _gate_cols(c0)
            wts = [[_bcast_rows(cw_ref, k, cs) for k in range(K)] for cs in cols]

            def step(it, state, cols=cols, wts=wts):
                heads, accs = state
                rows = pl.ds(pl.multiple_of((ngroups - 1 - it) * GATE_ROWS, GATE_ROWS), GATE_ROWS)
                g = hc_ref[rows, cols[0]]
                v = hc_ref[rows, cols[1]]
                da = da_buf[rows, cols[0]]
                sg = _sig(g)
                dms = (da * v * (sg * (1.0 + g * (1.0 - sg))), da * (g * sg))
                new_heads, new_accs = [], []
                for cs, wt, dm, head, acc in zip(cols, wts, dms, heads, accs):
                    h0 = h_ref[rows, cs].astype(f32)
                    ext = jnp.concatenate([dm, head], axis=0)
                    dh = wt[K - 1] * dm
                    acc_k = [None] * K + [acc[K] + _fold_rows(dm)]
                    acc_k[K - 1] = acc[K - 1] + _fold_rows(dm * h0)
                    for ahead in range(1, K):
                        dk = pltpu.roll(ext, GATE_ROWS + H - ahead, 0)[:GATE_ROWS]
                        dh = dh + wt[K - 1 - ahead] * dk
                        acc_k[K - 1 - ahead] = acc[K - 1 - ahead] + _fold_rows(dk * h0)
                    dh_ref[rows, cs] = dh.astype(bf16)
                    new_heads.append(dm[:H])
                    new_accs.append(tuple(acc_k))
                return tuple(new_heads), tuple(new_accs)

            zero = jnp.zeros((SUBLANES, GATE_COLS), f32)
            init = (tuple(carry[:, cs] for cs in cols), tuple(tuple(zero for _ in range(K + 1)) for _ in cols))
            heads, accs = lax.fori_loop(0, ngroups, step, init, unroll=True)
            for cs, head, acc in zip(cols, heads, accs):
                carry[:, cs] = head
                dcb_ref[:, cs] += _colsum(acc[K])
                for k in range(K):
                    dw_ref[k:k + 1, cs] += _colsum(acc[k])

    tile = lambda c: pl.BlockSpec((T, c), lambda i: (nt - 1 - i, 0))
    return pl.pallas_call(
        body, grid=(nt,), name="ffn_down_gate_bwd",
        out_shape=(jax.ShapeDtypeStruct((S, 2 * D_FF), bf16), jax.ShapeDtypeStruct((8, 2 * D_FF), f32),
                   jax.ShapeDtypeStruct((1, 2 * D_FF), f32)),
        in_specs=[tile(D_MODEL), _resident((D_FF, D_MODEL)), tile(2 * D_FF), tile(2 * D_FF), _acc((8, 2 * D_FF))],
        out_specs=(tile(2 * D_FF), _acc((8, 2 * D_FF)), _acc((1, 2 * D_FF))),
        scratch_shapes=[pltpu.VMEM((T, D_FF), f32), pltpu.VMEM((H, 2 * D_FF), f32)],
        compiler_params=_cp(("arbitrary",), 48),
    )(dzb, w_down, hfb, hc, conv_w)


def _wgrad(a, g, tn, name):
    S, K = a.shape
    N = g.shape[1]
    T = 1024 if S % 1024 == 0 else S

    def body(a_ref, g_ref, dw_ref, db_ref):
        @pl.when(pl.program_id(1) == 0)
        def _():
            dw_ref[...] = jnp.zeros_like(dw_ref)
            db_ref[...] = jnp.zeros_like(db_ref)
        gt = g_ref[...]
        dw_ref[...] += lax.dot_general(a_ref[...].astype(bf16), gt, TN_DIMS, preferred_element_type=f32)
        db_ref[...] += _colsum(gt.astype(f32))

    return pl.pallas_call(
        body, grid=(N // tn, S // T), name=name,
        out_shape=(jax.ShapeDtypeStruct((K, N), f32), jax.ShapeDtypeStruct((1, N), f32)),
        in_specs=[pl.BlockSpec((T, K), lambda j, i: (i, 0)), pl.BlockSpec((T, tn), lambda j, i: (i, j))],
        out_specs=(pl.BlockSpec((K, tn), lambda j, i: (0, j)), pl.BlockSpec((1, tn), lambda j, i: (0, j))),
        compiler_params=_cp(("parallel", "arbitrary"), 48),
    )(a, g)


def _outproj_dgrad(dzb, w, attn_out, lse):
    S = dzb.shape[0]
    T = PERM_TILE
    nsl = ATTN_CH // LANES
    n_p = len(DILATIONS)

    def body(g_ref, w_ref, ao_ref, lse_ref, dco_ref, dgo_ref, *rest):
        do_refs = rest[:n_p]
        st_refs = rest[n_p:2 * n_p]
        slabs = rest[2 * n_p:]
        dcat = lax.dot_general(g_ref[...], w_ref[...], NT_DIMS, preferred_element_type=f32)
        dco_ref[...] = dcat[:, :CONV_CH]
        dgo_ref[...] = dcat[:, CONV_CH + ATTN_CH:]
        lane = lax.broadcasted_iota(jnp.int32, (T, LANES), 1)
        st = lse_ref[...]
        for j in range(nsl):
            dO = dcat[:, CONV_CH + LANES * j:CONV_CH + LANES * (j + 1)]
            prod = dO * ao_ref[:, LANES * j:LANES * (j + 1)].astype(f32)
            for a in range(2):
                in_head = (lane < HEAD_DIM) if a == 0 else (lane >= HEAD_DIM)
                delta = jnp.sum(jnp.where(in_head, prod, 0.0), axis=1, keepdims=True)
                st = jnp.where((lane // 16 == 2 * j + a) & (lane % 16 >= 8), delta, st)
            slabs[j][...] = dO
        slabs[nsl][...] = st
        for d, do_ref, st_ref in zip(DILATIONS, do_refs, st_refs):
            for r in range(d):
                for j in range(nsl):
                    do_ref[r, :, LANES * j:LANES * (j + 1)] = _rows_of(slabs[j], r, T // d, d).astype(bf16)
                st_ref[r] = _rows_of(slabs[nsl], r, T // d, d)

    row = lambda c: pl.BlockSpec((T, c), lambda i: (i, 0))
    return pl.pallas_call(
        body, grid=(S // T,), name="outproj_dgrad",
        out_shape=(jax.ShapeDtypeStruct((S, CONV_CH), f32), jax.ShapeDtypeStruct((S, GMLP_CH), f32))
        + tuple(_perm_shape(S, d, ATTN_CH, bf16) for d in DILATIONS)
        + tuple(_perm_shape(S, d, ATTN_BLOCK, f32) for d in DILATIONS),
        in_specs=[row(D_MODEL), _resident((D_MODEL, D_MODEL)), row(ATTN_CH), row(ATTN_BLOCK)],
        out_specs=(row(CONV_CH), row(GMLP_CH)) + tuple(_perm_spec(d, ATTN_CH) for d in DILATIONS)
        + tuple(_perm_spec(d, ATTN_BLOCK) for d in DILATIONS),
        scratch_shapes=_slabs(nsl + 1, T),
        compiler_params=_cp(("parallel",), 40),
    )(dzb, w, attn_out, lse)


def _gmlp_bwd(c_in, dgm, ln_g, ln_b, w_s, b_s_t):
    S = c_in.shape[0]
    T = 512
    nsteps = S // T

    def body(c_ref, dg_ref, g_ref, be_ref, w_ref, bs_ref, dc_ref, dlg_ref, dlb_ref, dw_ref, dbs_ref,
             du_buf, dv_buf, dm_acc):
        i = pl.program_id(0)

        @pl.when(i == 0)
        def _():
            dlg_ref[...] = jnp.zeros_like(dlg_ref)
            dlb_ref[...] = jnp.zeros_like(dlb_ref)
            dw_ref[...] = jnp.zeros_like(dw_ref)
            dm_acc[...] = jnp.zeros_like(dm_acc)
        c = c_ref[...]
        u = c[:, :GMLP_CH]
        xhat, rstd = _ln_stats(c[:, GMLP_CH:])
        vb = (xhat * g_ref[...] + be_ref[...]).astype(bf16)
        dgm_t = dg_ref[...]
        dm_all = dgm_t * u
        for g in range(GMLP_GROUPS):
            wt = _tril_bf16(w_ref[g])
            cs = slice(GMLP_GROUP_DIM * g, GMLP_GROUP_DIM * (g + 1))
            dw_g = jnp.zeros((CHUNK, CHUNK), f32)
            for ci in range(T // CHUNK):
                rs = slice(CHUNK * ci, CHUNK * (ci + 1))
                v_c = vb[rs, cs]
                mixed = jnp.dot(wt, v_c, preferred_element_type=f32) + bs_ref[:, g:g + 1]
                dm = dm_all[rs, cs]
                dmb = dm.astype(bf16)
                du_buf[rs, cs] = dgm_t[rs, cs] * mixed
                dv_buf[rs, cs] = lax.dot_general(wt, dmb, TN_DIMS, preferred_element_type=f32)
                dw_g = dw_g + lax.dot_general(dmb, v_c, NT_DIMS, preferred_element_type=f32)
                dm_acc[:, cs] += dm
            dw_ref[g] += dw_g
        dv = dv_buf[...]
        dvr = _ln_bwd(dv, xhat, rstd, g_ref[...])
        dlg_ref[...] += _colsum(dv * xhat)
        dlb_ref[...] += _colsum(dv)
        dc_ref[:, :GMLP_CH] = du_buf[...].astype(bf16)
        dc_ref[:, GMLP_CH:] = dvr.astype(bf16)

        @pl.when(i == nsteps - 1)
        def _():
            row = lax.broadcasted_iota(jnp.int32, (CHUNK, CHUNK), 0)
            col = lax.broadcasted_iota(jnp.int32, (CHUNK, CHUNK), 1)
            tile = jnp.zeros((CHUNK, CHUNK), f32)
            for g in range(GMLP_GROUPS):
                dw_ref[g] = jnp.where(col <= row, dw_ref[g], 0.0)
                gsum = jnp.sum(dm_acc[:, GMLP_GROUP_DIM * g:GMLP_GROUP_DIM * (g + 1)], axis=1, keepdims=True)
                tile = jnp.where(col == g, gsum, tile)
            dbs_ref[...] = tile

    vec = jax.ShapeDtypeStruct((1, GMLP_CH), f32)
    return pl.pallas_call(
        body, grid=(nsteps,), name="gmlp_bwd",
        out_shape=(jax.ShapeDtypeStruct((S, 2 * GMLP_CH), bf16), vec, vec,
                   jax.ShapeDtypeStruct((GMLP_GROUPS, CHUNK, CHUNK), f32), jax.ShapeDtypeStruct((CHUNK, CHUNK), f32)),
        in_specs=[pl.BlockSpec((T, 2 * GMLP_CH), lambda i: (i, 0)), pl.BlockSpec((T, GMLP_CH), lambda i: (i, 0)),
                  _acc((1, GMLP_CH)), _acc((1, GMLP_CH)), _acc((GMLP_GROUPS, CHUNK, CHUNK)), _acc((CHUNK, GMLP_GROUPS))],
        out_specs=(pl.BlockSpec((T, 2 * GMLP_CH), lambda i: (i, 0)), _acc((1, GMLP_CH)), _acc((1, GMLP_CH)),
                   _acc((GMLP_GROUPS, CHUNK, CHUNK)), _acc((CHUNK, CHUNK))),
        scratch_shapes=[pltpu.VMEM((T, GMLP_CH), f32), pltpu.VMEM((T, GMLP_CH), f32), pltpu.VMEM((CHUNK, GMLP_CH), f32)],
        compiler_params=_cp(("arbitrary",), 32),
    )(c_in, dgm, ln_g, ln_b, w_s, b_s_t)


def _attn_bwd_pattern(qkv, d_out, stats, bias, d):
    _, L, _ = qkv.shape
    B = ATTN_BLOCK
    QB = _attn_tile(L)
    nsb = QB // B
    nt = L // QB
    U = nsb * UNITS_PER_BLOCK
    KV = 2 * ATTN_CH

    def body(cur_ref, hk_ref, hv_ref, do_ref, st_ref, b_ref, dqkv_ref, dbias_ref, lg, dp, pb, dsb, dkv, carry):
        r = pl.program_id(0)
        i = pl.program_id(1)
        n = nt - 1 - i

        @pl.when((r == 0) & (i == 0))
        def _():
            dbias_ref[...] = jnp.zeros_like(dbias_ref)

        @pl.when(i == 0)
        def _():
            carry[...] = jnp.zeros_like(carry)

        def operands(b, j, a):
            rows = slice(B * b, B * (b + 1))
            q2 = cur_ref[rows, LANES * j:LANES * (j + 1)]
            do2 = do_ref[rows, LANES * j:LANES * (j + 1)]
            keep = _head_lanes(a)
            return jnp.where(keep, q2, jnp.zeros_like(q2)), jnp.where(keep, do2, jnp.zeros_like(do2))

        for b in range(nsb):
            valid = _band_mask(b == 0, n)
            for j in range(HEAD_PAIRS):
                k2 = _pair_keys(cur_ref, hk_ref, 1, b, j)
                v2 = _pair_keys(cur_ref, hv_ref, 2, b, j)
                for a in range(2):
                    u = (b * HEAD_PAIRS + j) * 2 + a
                    qm, dom = operands(b, j, a)
                    logits = lax.dot_general(qm, k2, NT_DIMS, preferred_element_type=f32) + b_ref[2 * j + a]
                    lg[B * u:B * (u + 1), :] = jnp.where(valid, logits, NEG)
                    dp[B * u:B * (u + 1), :] = lax.dot_general(dom, v2, NT_DIMS, preferred_element_type=f32)
        for b in range(nsb):
            for j in range(HEAD_PAIRS):
                for a in range(2):
                    u = (b * HEAD_PAIRS + j) * 2 + a
                    rows = slice(B * u, B * (u + 1))
                    lane0 = 32 * j + 16 * a
                    lse = st_ref[B * b:B * (b + 1), lane0:lane0 + 1]
                    delta = st_ref[B * b:B * (b + 1), lane0 + 8:lane0 + 9]
                    p = jnp.exp(lg[rows, :] - lse)
                    ds = p * (dp[rows, :] - delta)
                    pb[rows, :] = p.astype(bf16)
                    dsb[rows, :] = ds.astype(bf16)
                    dbias_ref[2 * j + a] += ds
        dkv[...] = jnp.zeros_like(dkv)
        for b in range(nsb):
            for j in range(HEAD_PAIRS):
                k2 = _pair_keys(cur_ref, hk_ref, 1, b, j)
                dq, dk2, dv2 = [], None, None
                for a in range(2):
                    u = (b * HEAD_PAIRS + j) * 2 + a
                    rows = slice(B * u, B * (u + 1))
                    qm, dom = operands(b, j, a)
                    ds_u = dsb[rows, :]
                    dq.append(jnp.dot(ds_u, k2, preferred_element_type=f32))
                    dk_u = lax.dot_general(ds_u, qm, TN_DIMS, preferred_element_type=f32)
                    dv_u = lax.dot_general(pb[rows, :], dom, TN_DIMS, preferred_element_type=f32)
                    dk2 = dk_u if dk2 is None else dk2 + dk_u
                    dv2 = dv_u if dv2 is None else dv2 + dv_u
                dq2 = jnp.where(_head_lanes(0), dq[0], dq[1]) * (HEAD_DIM ** -0.5)
                dqkv_ref[B * b:B * (b + 1), LANES * j:LANES * (j + 1)] = dq2.astype(bf16)
                dkv[B * b:B * (b + 2), LANES * j:LANES * (j + 1)] += dk2
                dkv[B * b:B * (b + 2), ATTN_CH + LANES * j:ATTN_CH + LANES * (j + 1)] += dv2
        dkv[QB:, :] += carry[...]
        dqkv_ref[:, ATTN_CH:] = dkv[B:, :].astype(bf16)
        carry[...] = dkv[0:B, :]

    halo = lambda part: pl.BlockSpec((None, B, ATTN_CH),
                                     lambda r, i: (r, jnp.maximum((nt - 1 - i) * nsb - 1, 0), part))
    tile_spec = lambda c: pl.BlockSpec((None, QB, c), lambda r, i: (r, nt - 1 - i, 0))
    return pl.pallas_call(
        body, grid=(d, nt), name=f"attn_bwd_d{d}",
        out_shape=(jax.ShapeDtypeStruct((d, L, QKV_CH), bf16), jax.ShapeDtypeStruct((ATTN_HEADS, B, 2 * B), f32)),
        in_specs=[tile_spec(QKV_CH), halo(1), halo(2), tile_spec(ATTN_CH), tile_spec(B),
                  _resident((ATTN_HEADS, B, 2 * B))],
        out_specs=(tile_spec(QKV_CH), _acc((ATTN_HEADS, B, 2 * B))),
        scratch_shapes=[pltpu.VMEM((U * B, 2 * B), f32), pltpu.VMEM((U * B, 2 * B), f32),
                        pltpu.VMEM((U * B, 2 * B), bf16), pltpu.VMEM((U * B, 2 * B), bf16),
                        pltpu.VMEM((B + QB, KV), f32), pltpu.VMEM((B, KV), f32)],
        compiler_params=_cp(("arbitrary", "arbitrary"), 48),
    )(qkv, qkv, qkv, d_out, stats, bias)


def _attn_bwd_merge(d_a, dqkvs, d_c):
    S = d_a.shape[0]
    T = PERM_TILE
    nsl = QKV_CH // LANES
    n_p = len(DILATIONS)

    def body(da_ref, *rest):
        g_refs = rest[:n_p]
        dc_ref, dh_ref = rest[n_p:n_p + 2]
        slabs = rest[n_p + 2:]
        q0 = 2 * CONV_CH
        dh_ref[:, :q0] = da_ref[...]
        dh_ref[:, q0 + QKV_CH:] = dc_ref[...]
        for p, (d, g_ref) in enumerate(zip(DILATIONS, g_refs)):
            for r in range(d):
                for j in range(nsl):
                    _set_rows_of(slabs[p * nsl + j], r, T // d, d, g_ref[r, :, LANES * j:LANES * (j + 1)].astype(f32))
        for j in range(nsl):
            acc = slabs[j][...]
            for p in range(1, n_p):
                acc = acc + slabs[p * nsl + j][...]
            dh_ref[:, q0 + LANES * j:q0 + LANES * (j + 1)] = acc.astype(bf16)

    row = lambda c: pl.BlockSpec((T, c), lambda i: (i, 0))
    return pl.pallas_call(
        body, grid=(S // T,), name="attn_bwd_merge", out_shape=jax.ShapeDtypeStruct((S, IN_CH), bf16),
        in_specs=[row(2 * CONV_CH)] + [_perm_spec(d, QKV_CH) for d in DILATIONS] + [row(2 * GMLP_CH)],
        out_specs=row(IN_CH), scratch_shapes=_slabs(n_p * nsl, T),
        compiler_params=_cp(("parallel",), 48),
    )(d_a, *dqkvs, d_c)


def _bias_table_grad(dbias, buckets):
    n = dbias.shape[0]

    def body(db_ref, bk_ref, o_ref):
        p = pl.program_id(0)
        h = pl.program_id(1)

        @pl.when((p == 0) & (h == 0))
        def _():
            o_ref[...] = jnp.zeros_like(o_ref)
        ids = bk_ref[0]
        db = db_ref[0, 0]
        row = lax.broadcasted_iota(jnp.int32, (N_BUCKETS, 128), 0)
        lane = lax.broadcasted_iota(jnp.int32, (N_BUCKETS, 128), 1)
        upd = jnp.zeros((N_BUCKETS, 128), f32)
        for b in range(N_BUCKETS):
            s = jnp.sum(jnp.sum(jnp.where(ids == b, db, 0.0), axis=1, keepdims=True), axis=0, keepdims=True)
            upd = jnp.where((row == b) & (lane == h), s, upd)
        o_ref[...] += upd

    return pl.pallas_call(
        body, grid=(n, ATTN_HEADS), name="bias_table_grad",
        out_shape=jax.ShapeDtypeStruct((N_BUCKETS, 128), f32),
        in_specs=[pl.BlockSpec((1, 1, ATTN_BLOCK, 2 * ATTN_BLOCK), lambda p, h: (p, h, 0, 0)),
                  pl.BlockSpec((1, ATTN_BLOCK, 2 * ATTN_BLOCK), lambda p, h: (p, 0, 0))],
        out_specs=_acc((N_BUCKETS, 128)),
        compiler_params=_cp(("arbitrary", "arbitrary"), 16),
    )(dbias, buckets)


def _conv_bwd(a_in, hc, dco, dw_w, ln_g, ln_b):
    S = a_in.shape[0]
    T = 512
    hb = T // CONV_HALO
    nsteps = S // T
    R = T + CONV_HALO
    K = CONV_WIDTH

    def body(a_ref, hc_ref, hcn_ref, d_ref, dn_ref, w_ref, g_ref, be_ref,
             da_ref, dw_ref, dcb_ref, dlg_ref, dlb_ref, ext, dbuf, wacc):
        i = pl.program_id(0)

        @pl.when(i == 0)
        def _():
            wacc[...] = jnp.zeros_like(wacc)
            dcb_ref[...] = jnp.zeros_like(dcb_ref)
            dlg_ref[...] = jnp.zeros_like(dlg_ref)
            dlb_ref[...] = jnp.zeros_like(dlb_ref)
        ext[0:T, :] = hc_ref[...]
        ext[T:, :] = hcn_ref[...]
        xhat, rstd = _ln_stats(ext[...])
        hl = xhat * g_ref[...] + be_ref[...]
        ext[0:T, :] = d_ref[...]
        ext[T:, :] = dn_ref[...]
        sl_ = _sig(hl)
        dhl = ext[...] * (sl_ * (1.0 + hl * (1.0 - sl_)))
        dhc = _ln_bwd(dhl, xhat, rstd, g_ref[...])
        rowi = lax.broadcasted_iota(jnp.int32, (R, CONV_CH), 0)
        dbuf[...] = jnp.where((rowi < T) | (i < nsteps - 1), dhc, 0.0)
        dlg_ref[...] += _colsum(dhl[:T] * xhat[:T])
        dlb_ref[...] += _colsum(dhl[:T])
        dcb_ref[...] += _colsum(dbuf[pl.ds(0, T), :])
        starts = _window_rolls(range(K))
        slabs = [slice(LANES * j, LANES * (j + 1)) for j in range(CONV_CH // LANES)]

        def step(g, _):
            r0 = pl.multiple_of(g * CONV_GROUP, CONV_GROUP)
            rows = pl.ds(r0, CONV_GROUP)
            for j, cs in enumerate(slabs):
                gate_cs = slice(CONV_CH + LANES * j, CONV_CH + LANES * (j + 1))
                win = dbuf[pl.ds(r0, CONV_GROUP + CONV_HALO), cs]
                a = a_ref[rows, cs]
                sg = _sig(a_ref[rows, gate_cs])
                hg = a * sg
                dhg = jnp.zeros((CONV_GROUP, LANES), f32)
                for b, ss in starts.items():
                    rolled = win if b == 0 else pltpu.roll(win, b, 0)
                    for s in ss:
                        k = K - 1 - s
                        dk = rolled[s + b:s + b + CONV_GROUP]
                        dhg = dhg + w_ref[k:k + 1, cs] * dk
                        prod = dk * hg
                        fold = prod[0:SUBLANES]
                        for r in range(SUBLANES, CONV_GROUP, SUBLANES):
                            fold = fold + prod[r:r + SUBLANES]
                        wacc[SUBLANES * k:SUBLANES * (k + 1), cs] += fold
                da_ref[rows, cs] = (dhg * sg).astype(bf16)
                da_ref[rows, gate_cs] = (dhg * hg * (1.0 - sg)).astype(bf16)
            return 0

        lax.fori_loop(0, T // CONV_GROUP, step, 0)

        @pl.when(i == nsteps - 1)
        def _():
            for k in range(K):
                dw_ref[k:k + 1, :] = _colsum(wacc[SUBLANES * k:SUBLANES * (k + 1), :])
            dw_ref[K:, :] = jnp.zeros((32 - K, CONV_CH), f32)

    vec = jax.ShapeDtypeStruct((1, CONV_CH), f32)
    nxt = lambda i: (jnp.minimum((i + 1) * hb, nsteps * hb - 1), 0)
    return pl.pallas_call(
        body, grid=(nsteps,), name="conv_bwd",
        out_shape=(jax.ShapeDtypeStruct((S, 2 * CONV_CH), bf16), jax.ShapeDtypeStruct((32, CONV_CH), f32), vec, vec, vec),
        in_specs=[pl.BlockSpec((T, 2 * CONV_CH), lambda i: (i, 0)),
                  pl.BlockSpec((T, CONV_CH), lambda i: (i, 0)), pl.BlockSpec((CONV_HALO, CONV_CH), nxt),
                  pl.BlockSpec((T, CONV_CH), lambda i: (i, 0)), pl.BlockSpec((CONV_HALO, CONV_CH), nxt),
                  _acc((32, CONV_CH)), _acc((1, CONV_CH)), _acc((1, CONV_CH))],
        out_specs=(pl.BlockSpec((T, 2 * CONV_CH), lambda i: (i, 0)), _acc((32, CONV_CH)), _acc((1, CONV_CH)),
                   _acc((1, CONV_CH)), _acc((1, CONV_CH))),
        scratch_shapes=[pltpu.VMEM((R, CONV_CH), f32), pltpu.VMEM((R, CONV_CH), f32),
                        pltpu.VMEM((SUBLANES * 32, CONV_CH), f32)],
        compiler_params=_cp(("arbitrary",), 32),
    )(a_in, hc, hc, dco, dco, dw_w, ln_g, ln_b)


def _adamw(g, w, m, v, name):
    R, C = g.shape
    T = R
    for cand in (512, 256, 128, 64, 32, 16, 8):
        if R % cand == 0 and cand * C * 4 <= MIB:
            T = cand
            break
    c1 = 1.0 / (1.0 - ADAM_B1 ** ADAM_STEP)
    c2 = 1.0 / (1.0 - ADAM_B2 ** ADAM_STEP)

    def body(g_ref, w_ref, m_ref, v_ref, d_ref, nm_ref, nv_ref):
        gg = g_ref[...]
        nm = ADAM_B1 * m_ref[...] + (1.0 - ADAM_B1) * gg
        nv = ADAM_B2 * v_ref[...] + (1.0 - ADAM_B2) * (gg * gg)
        nm_ref[...] = nm
        nv_ref[...] = nv
        d_ref[...] = -ADAM_LR * ((nm * c1) / (jnp.sqrt(nv * c2) + ADAM_EPS) + ADAM_WD * w_ref[...])

    blk = pl.BlockSpec((T, C), lambda i: (i, 0))
    sd = jax.ShapeDtypeStruct((R, C), f32)
    return pl.pallas_call(
        body, grid=(R // T,), name=name, out_shape=(sd, sd, sd), in_specs=[blk] * 4, out_specs=(blk, blk, blk),
        compiler_params=_cp(("parallel",), 48),
    )(g, w, m, v)


def _pad_rows(a, rows):
    return jnp.pad(a, ((0, rows - a.shape[0]), (0, 0)))


def _local_step(x, target, wb, late_weights, sp, sink):
    buckets = jnp.asarray(_bucket_ids())
    bias = _bias_build(sp["rel_bias_table"], buckets)
    wb = dict(wb)
    saved = []
    xl = x
    for l in range(DEPTH):
        vec = lambda name: sp[name][l][None, :]
        a_in, c_in, *qkv = _inproj_fwd(xl, wb["w_in"][l], vec("b_in"))
        conv_w = _pad_rows(sp["conv_dw_w"][l], 32)
        conv_out, hc = _conv_fwd(a_in, conv_w, vec("conv_dw_b"), vec("conv_ln_g"), vec("conv_ln_b"))
        attn_out, lse = _attn_fwd(qkv, bias)
        bs_t = sp["gmlp_b_s"][l].T
        gm_out = _gmlp_fwd(c_in, vec("gmlp_ln_g"), vec("gmlp_ln_b"), sp["gmlp_w_s"][l], bs_t)
        if l == 0:
            wb.update(late_weights(gm_out))
        cat, z1, x1b = _outproj_ln_fwd(conv_out, attn_out, gm_out, wb["w_out"][l], vec("b_out"), xl,
                                           vec("ln1_g"), vec("ln1_b"))
        fconv_w = _pad_rows(sp["ffn_conv_w"][l], 8)
        hfb, fhc, act = _ffn_up_gate_fwd(x1b, wb["ffn_w_up"][l], vec("ffn_b_up"), fconv_w, vec("ffn_conv_b"))
        down = (act, wb["ffn_w_down"][l], vec("ffn_b_down"), z1, vec("ln1_g"), vec("ln1_b"), vec("ln2_g"),
                vec("ln2_b"))
        z2, x2 = _ffn_down_ln_fwd(*down) if l < DEPTH - 1 else (None, None)
        saved.append(dict(x=xl, a_in=a_in, qkv=qkv, c_in=c_in, hc=hc, attn_out=attn_out, lse=lse, cat=cat, z1=z1,
                          x1b=x1b, hfb=hfb, fhc=fhc, act=act, z2=z2, conv_w=conv_w, fconv_w=fconv_w, bs_t=bs_t))
        xl = x2

    grads = {}
    per_layer = {k: [None] * DEPTH for k in (
        "b_in", "conv_dw_w", "conv_dw_b", "conv_ln_g", "conv_ln_b", "gmlp_ln_g", "gmlp_ln_b", "gmlp_w_s",
        "gmlp_b_s", "b_out", "ln1_g", "ln1_b", "ffn_b_up", "ffn_conv_w", "ffn_conv_b", "ffn_b_down", "ln2_g", "ln2_b")}
    dbias_all = []
    dz2, dz2b, loss_part, dg2, db2 = _ffn_down_ln_loss(*down, target)
    loss = jnp.sum(loss_part)
    grad_x = None
    tok = jnp.zeros((), f32)
    for l in reversed(range(DEPTH)):
        sv = saved[l]
        vec = lambda name: sp[name][l][None, :] + tok
        per_layer["ln2_g"][l] = dg2[0]
        per_layer["ln2_b"][l] = db2[0]
        dw_down, db_down = _wgrad(sv["act"], dz2b, 512, "ffn_down_wgrad")
        tok = sink.put("ffn_w_down", l, dw_down, tok)
        per_layer["ffn_b_down"][l] = db_down[0]
        dhf, dfcw, dfcb = _ffn_down_gate_bwd(dz2b, wb["ffn_w_down"][l], sv["hfb"], sv["fhc"], sv["fconv_w"])
        per_layer["ffn_conv_w"][l] = dfcw[:FFN_CONV_WIDTH]
        per_layer["ffn_conv_b"][l] = dfcb[0]
        dw_up, db_up = _wgrad(sv["x1b"], dhf, 1408, "ffn_up_wgrad")
        tok = sink.put("ffn_w_up", l, dw_up, tok)
        per_layer["ffn_b_up"][l] = db_up[0]
        dz1, dz1b, dg1, db1 = _dgrad_ln_bwd(dhf, wb["ffn_w_up"][l], dz2, sv["z1"], vec("ln1_g"), "ffn_up_dgrad_ln")
        per_layer["ln1_g"][l] = dg1[0]
        per_layer["ln1_b"][l] = db1[0]
        dw_out, db_out = _wgrad(sv["cat"], dz1b, 512, "outproj_wgrad")
        tok = sink.put("w_out", l, dw_out, tok)
        per_layer["b_out"][l] = db_out[0]
        dco, dgo, *perm = _outproj_dgrad(dz1b, wb["w_out"][l], sv["attn_out"], sv["lse"])
        d_outs, stats = perm[:len(DILATIONS)], perm[len(DILATIONS):]
        d_c, dglg, dglb, dws, dbs = _gmlp_bwd(sv["c_in"], dgo, vec("gmlp_ln_g"), vec("gmlp_ln_b"), sp["gmlp_w_s"][l],
                                              sv["bs_t"])
        per_layer["gmlp_ln_g"][l] = dglg[0]
        per_layer["gmlp_ln_b"][l] = dglb[0]
        per_layer["gmlp_w_s"][l] = dws
        per_layer["gmlp_b_s"][l] = dbs[:, :GMLP_GROUPS].T
        dqkvs = []
        for p, d in enumerate(DILATIONS):
            dqkv, dbias = _attn_bwd_pattern(sv["qkv"][p], d_outs[p], stats[p], bias[p], d)
            dqkvs.append(dqkv)
            dbias_all.append(dbias)
        d_a, dcw, dcb, dclg, dclb = _conv_bwd(sv["a_in"], sv["hc"], dco, sv["conv_w"], vec("conv_ln_g"),
                                              vec("conv_ln_b"))
        per_layer["conv_dw_w"][l] = dcw[:CONV_WIDTH]
        per_layer["conv_dw_b"][l] = dcb[0]
        per_layer["conv_ln_g"][l] = dclg[0]
        per_layer["conv_ln_b"][l] = dclb[0]
        dh = _attn_bwd_merge(d_a, dqkvs, d_c)
        dw_in, db_in = _wgrad(sv["x"], dh, 640, "inproj_wgrad")
        tok = sink.put("w_in", l, dw_in, tok)
        per_layer["b_in"][l] = db_in[0]
        if l > 0:
            pv = saved[l - 1]
            dz2, dz2b, dg2, db2 = _dgrad_ln_bwd(dh, wb["w_in"][l], dz1, pv["z2"], sp["ln2_g"][l - 1][None, :] + tok,
                                                "inproj_dgrad_ln")
        else:
            grad_x = _dgrad_ln_bwd(dh, wb["w_in"][l], dz1, None, None, "inproj_dgrad")
    for k, v in per_layer.items():
        grads[k] = jnp.stack(v)
    dbias_cat = jnp.stack(dbias_all)
    bk_cat = jnp.concatenate([buckets] * DEPTH, axis=0)
    grads["rel_bias_table"] = _bias_table_grad(dbias_cat, bk_cat)[:, :ATTN_HEADS]
    return loss, grad_x, grads, sink.finish(grad_x)


N_CHIPS = 4
BIG = {"w_in": (D_MODEL, IN_CH, 1), "w_out": (D_MODEL, D_MODEL, 0),
       "ffn_w_up": (D_MODEL, 2 * D_FF, 1), "ffn_w_down": (D_FF, D_MODEL, 0)}
SMALL = ("b_in", "conv_dw_w", "conv_dw_b", "conv_ln_g", "conv_ln_b", "rel_bias_table", "gmlp_ln_g", "gmlp_ln_b",
         "gmlp_w_s", "gmlp_b_s", "b_out", "ln1_g", "ln1_b", "ffn_b_up", "ffn_conv_w", "ffn_conv_b", "ffn_b_down",
         "ln2_g", "ln2_b")
SMALL_SHARDED = ("conv_dw_w", "ffn_conv_w")
WEIGHTS = ("w_in", "b_in", "conv_dw_w", "conv_dw_b", "conv_ln_g", "conv_ln_b", "rel_bias_table", "gmlp_ln_g",
           "gmlp_ln_b", "gmlp_w_s", "gmlp_b_s", "w_out", "b_out", "ln1_g", "ln1_b", "ffn_w_up", "ffn_b_up",
           "ffn_conv_w", "ffn_conv_b", "ffn_w_down", "ffn_b_down", "ln2_g", "ln2_b")
ANY = pl.BlockSpec(memory_space=pl.ANY)


def _position():
    return lax.axis_index("x"), lax.axis_index("y"), lax.axis_index("c")


def _other_chips(x, y):
    return [(1 - x, y), (x, 1 - y), (1 - x, 1 - y)]


def _cast_bf16(a):
    R, C = a.shape
    T = 128

    def body(a_ref, o_ref):
        o_ref[...] = a_ref[...].astype(bf16)

    return pl.pallas_call(
        body, grid=(R // T,), name="cast_bf16", out_shape=jax.ShapeDtypeStruct((R, C), bf16),
        in_specs=[pl.BlockSpec((T, C), lambda i: (i, 0))], out_specs=pl.BlockSpec((T, C), lambda i: (i, 0)),
        compiler_params=_cp(("parallel",), 16),
    )(a)


def _chip_slot(ref, name, l, p):
    K, N, ax = BIG[name]
    if ax == 1:
        sz = N // N_CHIPS
        return ref.at[l, :, pl.ds(pl.multiple_of(p * sz, 128), sz)]
    sz = K // N_CHIPS
    return ref.at[l, pl.ds(pl.multiple_of(p * sz, 16), sz), :]


def _gather_weights(shards, conv_w, fconv_w):
    names = list(shards)
    n_big = len(names)
    n_t = n_big + 2
    n_chip = 3 * n_t
    n_pass = 3 * n_big

    def body(*refs):
        ins = refs[:n_t]
        outs = refs[n_t:2 * n_t]
        send_sems, recv_sems, pass_send, pass_recv, local_sems = refs[2 * n_t:]
        x, y, c = _position()
        me = 2 * x + y
        chips = _other_chips(x, y)

        def src(t):
            return ins[t].at[c] if t < n_big else ins[t]

        def slot(t, l, p):
            return _chip_slot(outs[t], names[t], l, p) if t < n_big else outs[t].at[p]

        locs, cps = [], []
        for t in range(n_t):
            for l in (range(DEPTH) if t < n_big else (0,)):
                loc = pltpu.make_async_copy(ins[t].at[l] if t < n_big else ins[t], slot(t, l, me),
                                            local_sems.at[DEPTH * t + l])
                loc.start()
                locs.append(loc)
            for k, (px, py) in enumerate(chips):
                cp = pltpu.make_async_remote_copy(
                    src_ref=src(t), dst_ref=slot(t, c, me), send_sem=send_sems.at[3 * t + k],
                    recv_sem=recv_sems.at[3 * t + k], device_id=(px, py, c), device_id_type=MESH_ID)
                cp.start()
                cps.append(cp)
        for t in range(n_t):
            for k, (px, py) in enumerate(chips):
                landed = slot(t, c, 2 * px + py)
                pltpu.make_async_remote_copy(
                    src_ref=src(t), dst_ref=landed, send_sem=send_sems.at[3 * t + k],
                    recv_sem=recv_sems.at[3 * t + k], device_id=(px, py, c), device_id_type=MESH_ID).wait_recv()
                if t < n_big:
                    cp = pltpu.make_async_remote_copy(
                        src_ref=landed, dst_ref=landed, send_sem=pass_send.at[3 * t + k],
                        recv_sem=pass_recv.at[3 * t + k], device_id=(x, y, 1 - c), device_id_type=MESH_ID)
                    cp.start()
                    cps.append(cp)
        for t in range(n_big):
            for k, (px, py) in enumerate(chips):
                from_sibling = slot(t, 1 - c, 2 * px + py)
                pltpu.make_async_remote_copy(
                    src_ref=from_sibling, dst_ref=from_sibling, send_sem=pass_send.at[3 * t + k],
                    recv_sem=pass_recv.at[3 * t + k], device_id=(x, y, 1 - c), device_id_type=MESH_ID).wait_recv()
        for cp in cps:
            cp.wait_send()
        for loc in locs:
            loc.wait()

    ins = [shards[n] for n in names] + [conv_w, fconv_w]
    out_shape = [jax.ShapeDtypeStruct((DEPTH, BIG[n][0], BIG[n][1]), bf16) for n in names]
    out_shape += [jax.ShapeDtypeStruct((N_CHIPS,) + conv_w.shape, f32), jax.ShapeDtypeStruct((N_CHIPS,) + fconv_w.shape, f32)]
    outs = pl.pallas_call(
        body, name="gather_weights", out_shape=tuple(out_shape), in_specs=[ANY] * n_t, out_specs=tuple([ANY] * n_t),
        scratch_shapes=[pltpu.SemaphoreType.DMA((n_chip,)), pltpu.SemaphoreType.DMA((n_chip,)),
                        pltpu.SemaphoreType.DMA((n_pass,)), pltpu.SemaphoreType.DMA((n_pass,)),
                        pltpu.SemaphoreType.DMA((DEPTH * n_t,))],
    )(*ins)
    return dict(zip(names, outs[:n_big])), outs[-2], outs[-1]


LATE_WEIGHTS = ("w_out", "ffn_w_up", "ffn_w_down")
HBM = pl.BlockSpec(memory_space=pltpu.HBM)
SEM = pl.BlockSpec(memory_space=pltpu.SEMAPHORE)


def _cast_into_full(shard, name, chip_arr):
    K, N, ax = BIG[name]
    k, n = _shard_shape(name)
    T = 64
    nrt = k // T

    def body(p_ref, a_ref, o_ref):
        o_ref[...] = a_ref[...].astype(bf16)

    if ax == 1:
        out_spec = pl.BlockSpec((None, T, n), lambda l, i, p: (l, i, p[0]))
    else:
        out_spec = pl.BlockSpec((None, T, n), lambda l, i, p: (l, p[0] * nrt + i, 0))
    return pl.pallas_call(
        body, name="cast_into_full", out_shape=jax.ShapeDtypeStruct((DEPTH, K, N), bf16),
        grid_spec=pltpu.PrefetchScalarGridSpec(
            num_scalar_prefetch=1, grid=(DEPTH, nrt),
            in_specs=[pl.BlockSpec((None, T, n), lambda l, i, p: (l, i, 0))], out_specs=out_spec),
        compiler_params=_cp(("parallel", "parallel"), 16),
    )(chip_arr, shard)


def _late_copies(refs, send_sems, recv_sems):
    x, y, c = _position()
    me = 2 * x + y
    idx = 0
    for ref, name in zip(refs, LATE_WEIGHTS):
        for l in range(DEPTH):
            for px, py in _other_chips(x, y):
                def copy(p, ref=ref, name=name, l=l, px=px, py=py, idx=idx):
                    part = _chip_slot(ref, name, l, p)
                    return pltpu.make_async_remote_copy(
                        src_ref=part, dst_ref=part, send_sem=send_sems.at[idx], recv_sem=recv_sems.at[idx],
                        device_id=(px, py, c), device_id_type=MESH_ID)
                yield copy(me), copy(2 * px + py)
                idx += 1


N_LATE_COPIES = 3 * DEPTH * len(LATE_WEIGHTS)


def _gather_start(fulls, after):
    n = len(fulls)

    def body(*refs):
        ins = refs[:n]
        send_sems, recv_sems = refs[n + 1:n + 3]
        token = refs[-1]
        for sent, _ in _late_copies(ins, send_sems, recv_sems):
            sent.start()
        token[...] = jnp.zeros_like(token)

    outs = pl.pallas_call(
        body, name="gather_start",
        out_shape=(pltpu.SemaphoreType.DMA((N_LATE_COPIES,)), pltpu.SemaphoreType.DMA((N_LATE_COPIES,)))
        + tuple(pltpu.HBM(f.shape, f.dtype) for f in fulls) + (jax.ShapeDtypeStruct((SUBLANES, LANES), f32),),
        in_specs=(HBM,) * n + (ANY,),
        out_specs=(SEM, SEM) + (HBM,) * n + (pl.BlockSpec(memory_space=pltpu.VMEM),),
        input_output_aliases={t: 2 + t for t in range(n)},
        compiler_params=pltpu.CompilerParams(has_side_effects=pltpu.SideEffectType.DATAFLOW_SIDE_EFFECTING),
    )(*[pltpu.with_memory_space_constraint(f, pltpu.HBM) for f in fulls], after)
    return outs[0], outs[1], outs[2:2 + n], outs[-1]


def _gather_wait(send_sems, recv_sems, fulls, after):
    n = len(fulls)

    def body(*refs):
        ins = refs[:n]
        send_ref, recv_ref = refs[n:n + 2]
        for sent, landed in _late_copies(ins, send_ref, recv_ref):
            sent.wait_send()
            landed.wait_recv()

    return pl.pallas_call(
        body, name="gather_wait", out_shape=tuple(pltpu.HBM(f.shape, f.dtype) for f in fulls),
        in_specs=(HBM,) * n + (SEM, SEM, ANY), out_specs=(HBM,) * n,
        input_output_aliases={t: t for t in range(n)},
        compiler_params=pltpu.CompilerParams(has_side_effects=pltpu.SideEffectType.DATAFLOW_SIDE_EFFECTING),
    )(*fulls, send_sems, recv_sems, after)


def _half(ref, name, c):
    K, N, ax = BIG[name]
    if ax == 1:
        return ref.at[pl.ds(pl.multiple_of(c * (K // 2), 8), K // 2), :]
    return ref.at[:, pl.ds(pl.multiple_of(c * (N // 2), 128), N // 2)]


def _half_shape(name):
    K, N, ax = BIG[name]
    return (K // 2, N) if ax == 1 else (K, N // 2)


def _shard_of_half(ref, name, q):
    K, N, ax = BIG[name]
    if ax == 1:
        sz = N // N_CHIPS
        return ref.at[:, pl.ds(pl.multiple_of(q * sz, 128), sz)]
    sz = K // N_CHIPS
    return ref.at[pl.ds(pl.multiple_of(q * sz, 16), sz), :]


def _shard_half_shape(name):
    K, N, ax = BIG[name]
    return (K // 2, N // N_CHIPS) if ax == 1 else (K // N_CHIPS, N // 2)


def _shard_shape(name):
    K, N, ax = BIG[name]
    return (K, N // N_CHIPS) if ax == 1 else (K // N_CHIPS, N)


def _pair_copies(names, srcs, lands, send_sems, recv_sems):
    x, y, c = _position()
    for idx, (name, src, land) in enumerate(zip(names, srcs, lands)):
        yield pltpu.make_async_remote_copy(
            src_ref=_half(src, name, 1 - c), dst_ref=land, send_sem=send_sems.at[idx], recv_sem=recv_sems.at[idx],
            device_id=(x, y, 1 - c), device_id_type=MESH_ID)


def _pair_exchange_start(tag, tensors):
    names = [n for n, _ in tensors]
    n = len(tensors)
    lands = [lax.empty(_half_shape(nm), f32) for nm in names]

    def body(*refs):
        for cp in _pair_copies(names, refs[:n], refs[n:2 * n], refs[2 * n], refs[2 * n + 1]):
            cp.start()
        refs[-1][...] = jnp.zeros_like(refs[-1])

    args = [g for _, g in tensors] + lands
    outs = pl.pallas_call(
        body, name="grad_pair_start_" + tag,
        out_shape=(pltpu.SemaphoreType.DMA((n,)), pltpu.SemaphoreType.DMA((n,)))
        + tuple(pltpu.HBM(a.shape, a.dtype) for a in args) + (jax.ShapeDtypeStruct((SUBLANES, LANES), f32),),
        in_specs=(HBM,) * (2 * n), out_specs=(SEM, SEM) + (HBM,) * (2 * n) + (pl.BlockSpec(memory_space=pltpu.VMEM),),
        input_output_aliases={t: 2 + t for t in range(2 * n)},
        compiler_params=pltpu.CompilerParams(has_side_effects=pltpu.SideEffectType.DATAFLOW_SIDE_EFFECTING),
    )(*[pltpu.with_memory_space_constraint(a, pltpu.HBM) for a in args])
    return (tag, names, outs[0], outs[1], outs[2:2 + 2 * n]), outs[-1]


def _pair_exchange_wait(state, after):
    tag, names, send_sems, recv_sems, bufs = state
    n = len(names)

    def body(*refs):
        for cp in _pair_copies(names, refs[:n], refs[n:2 * n], refs[2 * n], refs[2 * n + 1]):
            cp.wait_send()
            cp.wait_recv()

    outs = pl.pallas_call(
        body, name="grad_pair_wait_" + tag, out_shape=tuple(pltpu.HBM(a.shape, a.dtype) for a in bufs),
        in_specs=(HBM,) * (2 * n) + (SEM, SEM, ANY), out_specs=(HBM,) * (2 * n),
        input_output_aliases={t: t for t in range(2 * n)},
        compiler_params=pltpu.CompilerParams(has_side_effects=pltpu.SideEffectType.DATAFLOW_SIDE_EFFECTING),
    )(*bufs, send_sems, recv_sems, after)
    return list(zip(names, outs[:n], outs[n:]))


def _pair_add(g, rcv, name, c_arr):
    K, N, ax = BIG[name]
    hr, hc = _half_shape(name)
    T = 128
    nrt = hr // T

    def body(c_ref, g_ref, r_ref, o_ref):
        o_ref[...] = (g_ref[...] + r_ref[...]).astype(bf16)

    if ax == 1:
        g_spec = pl.BlockSpec((T, hc), lambda i, c: (c[0] * nrt + i, 0))
    else:
        g_spec = pl.BlockSpec((T, hc), lambda i, c: (i, c[0]))
    plain = pl.BlockSpec((T, hc), lambda i, c: (i, 0))
    return pl.pallas_call(
        body, name="grad_pair_add", out_shape=jax.ShapeDtypeStruct((hr, hc), bf16),
        grid_spec=pltpu.PrefetchScalarGridSpec(num_scalar_prefetch=1, grid=(nrt,), in_specs=[g_spec, plain],
                                               out_specs=plain),
        compiler_params=_cp(("parallel",), 32),
    )(c_arr, g, rcv)


def _chip_copies(names, srcs, lands, send_sems, recv_sems):
    x, y, c = _position()
    me = 2 * x + y
    idx = 0
    for name, src, land in zip(names, srcs, lands):
        for px, py in _other_chips(x, y):
            def copy(q, row, name=name, src=src, land=land, px=px, py=py, idx=idx):
                return pltpu.make_async_remote_copy(
                    src_ref=_shard_of_half(src, name, q), dst_ref=land.at[row], send_sem=send_sems.at[idx],
                    recv_sem=recv_sems.at[idx], device_id=(px, py, c), device_id_type=MESH_ID)
            yield copy(2 * px + py, me), copy(me, 2 * px + py)
            idx += 1


def _chip_exchange_start(tag, tensors):
    names = [n for n, _ in tensors]
    n = len(tensors)
    lands = [lax.empty((N_CHIPS,) + _shard_half_shape(nm), g.dtype) for nm, g in tensors]

    def body(*refs):
        send_sems, recv_sems = refs[2 * n:2 * n + 2]
        for sent, _ in _chip_copies(names, refs[:n], refs[n:2 * n], send_sems, recv_sems):
            sent.start()
        refs[-1][...] = jnp.zeros_like(refs[-1])

    args = [g for _, g in tensors] + lands
    outs = pl.pallas_call(
        body, name="grad_chip_start_" + tag,
        out_shape=(pltpu.SemaphoreType.DMA((3 * n,)), pltpu.SemaphoreType.DMA((3 * n,)))
        + tuple(pltpu.HBM(a.shape, a.dtype) for a in args) + (jax.ShapeDtypeStruct((SUBLANES, LANES), f32),),
        in_specs=(HBM,) * (2 * n), out_specs=(SEM, SEM) + (HBM,) * (2 * n) + (pl.BlockSpec(memory_space=pltpu.VMEM),),
        input_output_aliases={t: 2 + t for t in range(2 * n)},
        compiler_params=pltpu.CompilerParams(has_side_effects=pltpu.SideEffectType.DATAFLOW_SIDE_EFFECTING),
    )(*[pltpu.with_memory_space_constraint(a, pltpu.HBM) for a in args])
    return (tag, names, outs[0], outs[1], outs[2:2 + 2 * n]), outs[-1]


def _chip_exchange_wait(state, after):
    tag, names, send_sems, recv_sems, bufs = state
    n = len(names)

    def body(*refs):
        for sent, landed in _chip_copies(names, refs[:n], refs[n:2 * n], refs[2 * n], refs[2 * n + 1]):
            sent.wait_send()
            landed.wait_recv()

    outs = pl.pallas_call(
        body, name="grad_chip_wait_" + tag, out_shape=tuple(pltpu.HBM(a.shape, a.dtype) for a in bufs),
        in_specs=(HBM,) * (2 * n) + (SEM, SEM, ANY), out_specs=(HBM,) * (2 * n),
        input_output_aliases={t: t for t in range(2 * n)},
        compiler_params=pltpu.CompilerParams(has_side_effects=pltpu.SideEffectType.DATAFLOW_SIDE_EFFECTING),
    )(*bufs, send_sems, recv_sems, after)
    return list(zip(names, outs[:n], outs[n:]))


def _sum_chips(name, half, land, chip_arr):
    K, N, ax = BIG[name]
    R, C = _shard_half_shape(name)
    T = 64
    nrt = R // T

    def body(p_ref, own_ref, land_ref, o_ref):
        parts = [jnp.where(p_ref[0] == q, own_ref[...], land_ref[q]).astype(f32) for q in range(N_CHIPS)]
        o_ref[...] = ((parts[0] + parts[1]) + parts[2]) + parts[3]

    if ax == 1:
        own_spec = pl.BlockSpec((T, C), lambda i, p: (i, p[0]))
    else:
        own_spec = pl.BlockSpec((T, C), lambda i, p: (p[0] * nrt + i, 0))
    return pl.pallas_call(
        body, name="grad_sum_chips", out_shape=jax.ShapeDtypeStruct((R, C), f32),
        grid_spec=pltpu.PrefetchScalarGridSpec(
            num_scalar_prefetch=1, grid=(nrt,),
            in_specs=[own_spec, pl.BlockSpec((N_CHIPS, T, C), lambda i, p: (0, i, 0))],
            out_specs=pl.BlockSpec((T, C), lambda i, p: (i, 0))),
        compiler_params=_cp(("parallel",), 32),
    )(chip_arr, half, land)


def _pair_swap(halves):
    n_t = len(halves)

    def body(*refs):
        ins = refs[:n_t]
        outs = refs[n_t:2 * n_t]
        send_sems, recv_sems = refs[2 * n_t:]
        x, y, c = _position()
        cps = []
        for t in range(n_t):
            cp = pltpu.make_async_remote_copy(
                src_ref=ins[t], dst_ref=outs[t], send_sem=send_sems.at[t], recv_sem=recv_sems.at[t],
                device_id=(x, y, 1 - c), device_id_type=MESH_ID)
            cp.start()
            cps.append(cp)
        for cp in cps:
            cp.wait()

    return pl.pallas_call(
        body, name="grad_pair_swap", out_shape=tuple(jax.ShapeDtypeStruct(h.shape, h.dtype) for h in halves),
        in_specs=[ANY] * n_t, out_specs=tuple([ANY] * n_t),
        scratch_shapes=[pltpu.SemaphoreType.DMA((n_t,)), pltpu.SemaphoreType.DMA((n_t,))],
    )(*halves)


def _adamw_halves(own, other, w, m, v, name, l, c_arr, prev):
    K, N, ax = BIG[name]
    R, C = _shard_shape(name)
    hr, hc = _shard_half_shape(name)
    T = 64
    nrt = hr // T
    c1 = 1.0 / (1.0 - ADAM_B1 ** ADAM_STEP)
    c2 = 1.0 / (1.0 - ADAM_B2 ** ADAM_STEP)

    def body(c_ref, own_ref, oth_ref, w_ref, m_ref, v_ref, *rest):
        g_ref, d_ref, nm_ref, nv_ref = rest[-4:]
        gg = jnp.where(pl.program_id(0) == c_ref[0], own_ref[...], oth_ref[...])
        nm = ADAM_B1 * m_ref[...] + (1.0 - ADAM_B1) * gg
        nv = ADAM_B2 * v_ref[...] + (1.0 - ADAM_B2) * (gg * gg)
        g_ref[...] = gg
        nm_ref[...] = nm
        nv_ref[...] = nv
        d_ref[...] = -ADAM_LR * ((nm * c1) / (jnp.sqrt(nv * c2) + ADAM_EPS) + ADAM_WD * w_ref[...])

    half = pl.BlockSpec((T, hc), lambda h, i, c: (i, 0))
    if ax == 1:
        full = pl.BlockSpec((None, T, hc), lambda h, i, c: (l, h * nrt + i, 0))
    else:
        full = pl.BlockSpec((None, T, hc), lambda h, i, c: (l, i, h))
    sd = jax.ShapeDtypeStruct((DEPTH, R, C), f32)
    args = [c_arr, own, other, w, m, v]
    in_specs = [half, half, full, full, full]
    aliases = {}
    if prev is not None:
        args += list(prev)
        in_specs += [ANY] * 4
        aliases = {6 + k: k for k in range(4)}
    return pl.pallas_call(
        body, name="adamw_" + name, out_shape=(sd, sd, sd, sd),
        grid_spec=pltpu.PrefetchScalarGridSpec(num_scalar_prefetch=1, grid=(2, nrt), in_specs=in_specs,
                                               out_specs=(full, full, full, full)),
        input_output_aliases=aliases,
        compiler_params=_cp(("arbitrary", "arbitrary"), 32),
    )(*args)


class _GradExchange:
    GROUPS = (("l1", tuple((n, DEPTH - 1) for n in BIG)),
              ("l0_ffn", (("ffn_w_down", 0), ("ffn_w_up", 0))),
              ("l0_mix", (("w_out", 0), ("w_in", 0))))

    def __init__(self):
        self.c_arr = jnp.reshape(lax.axis_index("c"), (1,)).astype(jnp.int32)
        self.chip_arr = jnp.reshape(2 * lax.axis_index("x") + lax.axis_index("y"), (1,)).astype(jnp.int32)
        self.grads = {}
        self.pair_started = {}
        self.chip_started = {}

    def _advance(self, after, tok):
        for tag, _ in self.GROUPS:
            if tag not in self.pair_started or tag in self.chip_started:
                continue
            arrived = _pair_exchange_wait(self.pair_started[tag], after)
            pair = [(n, _pair_add(g, r, n, self.c_arr)) for n, g, r in arrived]
            self.chip_started[tag], token = _chip_exchange_start(tag, pair)
            tok = tok + token[0, 0]
        return tok

    def put(self, name, layer, g, tok):
        self.grads[(name, layer)] = g
        tok = self._advance(g, tok)
        for tag, keys in self.GROUPS:
            if tag in self.pair_started or not all(k in self.grads for k in keys):
                continue
            self.pair_started[tag], token = _pair_exchange_start(tag, [(n, self.grads[(n, l)]) for n, l in keys])
            tok = tok + token[0, 0]
        return tok

    def finish(self, after):
        self._advance(after, jnp.zeros((), f32))
        keys, own = [], []
        for tag, group in self.GROUPS:
            landed = _chip_exchange_wait(self.chip_started[tag], after)
            own += [_sum_chips(n, half, land, self.chip_arr) for n, half, land in landed]
            keys += list(group)
        other = _pair_swap(own)
        return dict(zip(keys, zip(own, other)))


def _small_allreduce(buf):
    R = buf.shape[0]

    def body(in_ref, out_ref, sibling, slots, send_sems, recv_sems):
        x, y, c = _position()
        me = 2 * x + y
        swap = pltpu.make_async_remote_copy(
            src_ref=in_ref, dst_ref=sibling, send_sem=send_sems.at[0], recv_sem=recv_sems.at[0],
            device_id=(x, y, 1 - c), device_id_type=MESH_ID)
        swap.start()
        swap.wait()
        slots[me] = in_ref[...] + sibling[...]
        cps = []
        for k, (px, py) in enumerate(_other_chips(x, y)):
            cp = pltpu.make_async_remote_copy(
                src_ref=slots.at[me], dst_ref=slots.at[me], send_sem=send_sems.at[1 + k], recv_sem=recv_sems.at[1 + k],
                device_id=(px, py, c), device_id_type=MESH_ID)
            cp.start()
            cps.append(cp)
        for k, (px, py) in enumerate(_other_chips(x, y)):
            pltpu.make_async_remote_copy(
                src_ref=slots.at[me], dst_ref=slots.at[2 * px + py], send_sem=send_sems.at[1 + k],
                recv_sem=recv_sems.at[1 + k], device_id=(px, py, c), device_id_type=MESH_ID).wait_recv()
        for cp in cps:
            cp.wait_send()
        out_ref[...] = ((slots[0] + slots[1]) + slots[2]) + slots[3]

    vm = pl.BlockSpec(memory_space=pltpu.VMEM)
    return pl.pallas_call(
        body, name="small_allreduce", out_shape=jax.ShapeDtypeStruct((R, 128), f32), in_specs=[vm], out_specs=vm,
        scratch_shapes=[pltpu.VMEM((R, 128), f32), pltpu.VMEM((N_CHIPS, R, 128), f32),
                        pltpu.SemaphoreType.DMA((N_CHIPS,)), pltpu.SemaphoreType.DMA((N_CHIPS,))],
        compiler_params=pltpu.CompilerParams(vmem_limit_bytes=40 * MIB),
    )(buf)


PACK_UNIT = 1024


def _pack(arrs):
    parts = []
    for a in arrs:
        flat = a.reshape(-1)
        n = -(-flat.shape[0] // PACK_UNIT) * PACK_UNIT
        parts.append(jnp.pad(flat, (0, n - flat.shape[0])))
    return jnp.concatenate(parts).reshape(-1, 128)


def _unpack(buf, shapes):
    flat = buf.reshape(-1)
    out, off = [], 0
    for shp in shapes:
        n = int(np.prod(shp))
        out.append(flat[off:off + n].reshape(shp))
        off += -(-n // PACK_UNIT) * PACK_UNIT
    return out


def kernel(x, w_in, b_in, conv_dw_w, conv_dw_b, conv_ln_g, conv_ln_b, rel_bias_table, gmlp_ln_g, gmlp_ln_b, gmlp_w_s, gmlp_b_s, w_out, b_out, ln1_g, ln1_b, ffn_w_up, ffn_b_up, ffn_conv_w, ffn_conv_b, ffn_w_down, ffn_b_down, ln2_g, ln2_b, loss_target, m_w_in, m_b_in, m_conv_dw_w, m_conv_dw_b, m_conv_ln_g, m_conv_ln_b, m_rel_bias_table, m_gmlp_ln_g, m_gmlp_ln_b, m_gmlp_w_s, m_gmlp_b_s, m_w_out, m_b_out, m_ln1_g, m_ln1_b, m_ffn_w_up, m_ffn_b_up, m_ffn_conv_w, m_ffn_conv_b, m_ffn_w_down, m_ffn_b_down, m_ln2_g, m_ln2_b, v_w_in, v_b_in, v_conv_dw_w, v_conv_dw_b, v_conv_ln_g, v_conv_ln_b, v_rel_bias_table, v_gmlp_ln_g, v_gmlp_ln_b, v_gmlp_w_s, v_gmlp_b_s, v_w_out, v_b_out, v_ln1_g, v_ln1_b, v_ffn_w_up, v_ffn_b_up, v_ffn_conv_w, v_ffn_conv_b, v_ffn_w_down, v_ffn_b_down, v_ln2_g, v_ln2_b):
    w = dict(w_in=w_in, b_in=b_in, conv_dw_w=conv_dw_w, conv_dw_b=conv_dw_b, conv_ln_g=conv_ln_g, conv_ln_b=conv_ln_b,
             rel_bias_table=rel_bias_table, gmlp_ln_g=gmlp_ln_g, gmlp_ln_b=gmlp_ln_b, gmlp_w_s=gmlp_w_s,
             gmlp_b_s=gmlp_b_s, w_out=w_out, b_out=b_out, ln1_g=ln1_g, ln1_b=ln1_b, ffn_w_up=ffn_w_up,
             ffn_b_up=ffn_b_up, ffn_conv_w=ffn_conv_w, ffn_conv_b=ffn_conv_b, ffn_w_down=ffn_w_down,
             ffn_b_down=ffn_b_down, ln2_g=ln2_g, ln2_b=ln2_b)
    m = dict(w_in=m_w_in, b_in=m_b_in, conv_dw_w=m_conv_dw_w, conv_dw_b=m_conv_dw_b, conv_ln_g=m_conv_ln_g,
             conv_ln_b=m_conv_ln_b, rel_bias_table=m_rel_bias_table, gmlp_ln_g=m_gmlp_ln_g, gmlp_ln_b=m_gmlp_ln_b,
             gmlp_w_s=m_gmlp_w_s, gmlp_b_s=m_gmlp_b_s, w_out=m_w_out, b_out=m_b_out, ln1_g=m_ln1_g, ln1_b=m_ln1_b,
             ffn_w_up=m_ffn_w_up, ffn_b_up=m_ffn_b_up, ffn_conv_w=m_ffn_conv_w, ffn_conv_b=m_ffn_conv_b,
             ffn_w_down=m_ffn_w_down, ffn_b_down=m_ffn_b_down, ln2_g=m_ln2_g, ln2_b=m_ln2_b)
    v = dict(w_in=v_w_in, b_in=v_b_in, conv_dw_w=v_conv_dw_w, conv_dw_b=v_conv_dw_b, conv_ln_g=v_conv_ln_g,
             conv_ln_b=v_conv_ln_b, rel_bias_table=v_rel_bias_table, gmlp_ln_g=v_gmlp_ln_g, gmlp_ln_b=v_gmlp_ln_b,
             gmlp_w_s=v_gmlp_w_s, gmlp_b_s=v_gmlp_b_s, w_out=v_w_out, b_out=v_b_out, ln1_g=v_ln1_g, ln1_b=v_ln1_b,
             ffn_w_up=v_ffn_w_up, ffn_b_up=v_ffn_b_up, ffn_conv_w=v_ffn_conv_w, ffn_conv_b=v_ffn_conv_b,
             ffn_w_down=v_ffn_w_down, ffn_b_down=v_ffn_b_down, ln2_g=v_ln2_g, ln2_b=v_ln2_b)

    chip_arr = jnp.reshape(2 * lax.axis_index("x") + lax.axis_index("y"), (1,)).astype(jnp.int32)
    shards = {"w_in": _cast_bf16(w_in.reshape(-1, w_in.shape[-1])).reshape(w_in.shape)}
    wb, conv_stack, fconv_stack = _gather_weights(shards, conv_dw_w, ffn_conv_w)
    send_sems, recv_sems, in_flight, token = _gather_start(
        [_cast_into_full(w[n], n, chip_arr) for n in LATE_WEIGHTS], conv_stack)
    sp = {n: w[n] for n in SMALL}
    sp["conv_dw_w"] = jnp.moveaxis(conv_stack, 0, 2).reshape(DEPTH, CONV_WIDTH, CONV_CH)
    sp["ffn_conv_w"] = jnp.moveaxis(fconv_stack, 0, 2).reshape(DEPTH, FFN_CONV_WIDTH, 2 * D_FF)
    sp["b_in"] = sp["b_in"] + token[0, 0]

    def late_weights(after):
        return dict(zip(LATE_WEIGHTS, _gather_wait(send_sems, recv_sems, in_flight, after)))

    sink = _GradExchange()
    loss_local, grad_x, grads, big = _local_step(x[0], loss_target[0], wb, late_weights, sp, sink)

    small_shapes = [(1,)] + [grads[n].shape for n in SMALL]
    summed = _unpack(_small_allreduce(_pack([loss_local.reshape(1)] + [grads[n] for n in SMALL])), small_shapes)
    loss = summed[0].reshape(())
    small = dict(zip(SMALL, summed[1:]))
    chip = 2 * lax.axis_index("x") + lax.axis_index("y")
    for n in SMALL_SHARDED:
        width = w[n].shape[-1]
        small[n] = lax.dynamic_slice_in_dim(small[n], chip * width, width, axis=2)

    g_out, d_out, m_out, v_out = {}, {}, {}, {}
    for n in BIG:
        outs = None
        for l in range(DEPTH):
            own, other = big[(n, l)]
            outs = _adamw_halves(own, other, w[n], m[n], v[n], n, l, sink.c_arr, outs)
        g_out[n], d_out[n], m_out[n], v_out[n] = outs
    shapes = [small[n].shape for n in SMALL]
    packed = [_pack([src[n] for n in SMALL]) for src in (small, w, m, v)]
    upd = _adamw(*packed, "adamw_small")
    for dst, buf in zip((d_out, m_out, v_out), upd):
        dst.update(zip(SMALL, _unpack(buf, shapes)))
    g_out.update(small)

    return (loss, grad_x[None], *[g_out[n] for n in WEIGHTS], *[d_out[n] for n in WEIGHTS],
            *[m_out[n] for n in WEIGHTS], *[v_out[n] for n in WEIGHTS])
```

```python
import functools
import math

import numpy as np
import jax
import jax.numpy as jnp
from jax import lax
from jax.experimental import pallas as pl
from jax.experimental.pallas import tpu as pltpu

f32 = jnp.float32
bf16 = jnp.bfloat16

D_MODEL = 1024
DEPTH = 2
HEAD_DIM = 64
CONV_CH = 256
CONV_WIDTH = 31
ATTN_HEADS = 8
ATTN_CH = ATTN_HEADS * HEAD_DIM
DILATIONS = (1, 4, 16)
ATTN_BLOCK = 128
N_BUCKETS = 32
MAX_DISTANCE = 2048
GMLP_CH = 256
GMLP_GROUPS = 4
GMLP_GROUP_DIM = GMLP_CH // GMLP_GROUPS
CHUNK = 128
IN_CH = 2 * CONV_CH + 3 * ATTN_CH + 2 * GMLP_CH
D_FF = 2816
FFN_CONV_WIDTH = 3
LN_EPS = 1e-5
ALPHA = (2.0 * DEPTH) ** 0.25
ADAM_LR = 0.001
ADAM_B1 = 0.9
ADAM_B2 = 0.999
ADAM_EPS = 1e-08
ADAM_WD = 0.01
ADAM_STEP = 10

CONV_HALO = 32
FFN_HALO = 8
NEG = -1e30
MIB = 2 ** 20
NT_DIMS = (((1,), (1,)), ((), ()))
TN_DIMS = (((0,), (0,)), ((), ()))
MESH_ID = pl.DeviceIdType.MESH


def _cp(sem, vmem_mib):
    return pltpu.CompilerParams(dimension_semantics=sem, vmem_limit_bytes=vmem_mib * MIB)


def _resident(shape):
    nd = len(shape)
    return pl.BlockSpec(shape, lambda *_: (0,) * nd, pipeline_mode=pl.Buffered(1))


def _acc(shape):
    nd = len(shape)
    return pl.BlockSpec(shape, lambda *_: (0,) * nd)


def _sig(x):
    return 1.0 / (1.0 + jnp.exp(-x))


def _ln_stats(z):
    mu = jnp.mean(z, axis=-1, keepdims=True)
    zc = z - mu
    var = jnp.mean(zc * zc, axis=-1, keepdims=True)
    rstd = lax.rsqrt(var + LN_EPS)
    return zc * rstd, rstd


def _ln_bwd(dy, xhat, rstd, g):
    dxh = dy * g
    m1 = jnp.mean(dxh, axis=-1, keepdims=True)
    m2 = jnp.mean(dxh * xhat, axis=-1, keepdims=True)
    return rstd * (dxh - m1 - xhat * m2)


def _colsum(x):
    return jnp.sum(x, axis=0, keepdims=True)


def _t5_bucket_np(dist):
    max_exact = N_BUCKETS // 2
    dd = np.maximum(dist, 1).astype(np.float64)
    large = max_exact + (np.log(dd / max_exact) / math.log(MAX_DISTANCE / max_exact)
                         * (N_BUCKETS - max_exact)).astype(np.int32)
    large = np.minimum(large, N_BUCKETS - 1)
    return np.where(dist < max_exact, dist, large).astype(np.int32)


def _bucket_ids():
    qi = np.arange(ATTN_BLOCK)[:, None]
    kj = np.arange(2 * ATTN_BLOCK)[None, :]
    dist = np.clip(qi + ATTN_BLOCK - kj, 0, None)
    return np.stack([_t5_bucket_np(dist * d) for d in DILATIONS]).astype(np.int32)


LANES = 128
QKV_CH = 3 * ATTN_CH
PERM_TILE = 512


def _slabs(n, rows):
    return [pltpu.VMEM((rows, LANES), f32)] * n


def _rows_of(slab, r, n, d):
    return slab[...] if d == 1 else slab[pl.ds(r, n, stride=d), :]


def _set_rows_of(slab, r, n, d, val):
    if d == 1:
        slab[...] = val
    else:
        slab[pl.ds(r, n, stride=d), :] = val


def _perm_spec(d, ch):
    return pl.BlockSpec((d, PERM_TILE // d, ch), lambda i: (0, i, 0))


def _perm_shape(S, d, ch, dtype):
    return jax.ShapeDtypeStruct((d, S // d, ch), dtype)


def _inproj_fwd(x, w, b):
    S = x.shape[0]
    T = PERM_TILE
    nsl = QKV_CH // LANES

    def body(x_ref, w_ref, b_ref, a_ref, c_ref, *rest):
        q_refs = rest[:len(DILATIONS)]
        slabs = rest[len(DILATIONS):]
        h = jnp.dot(x_ref[...].astype(bf16), w_ref[...], preferred_element_type=f32) + b_ref[...]
        a_ref[...] = h[:, :2 * CONV_CH]
        q0 = 2 * CONV_CH
        c_ref[...] = h[:, q0 + QKV_CH:]
        for j in range(nsl):
            piece = h[:, q0 + LANES * j:q0 + LANES * (j + 1)]
            if LANES * j < ATTN_CH:
                piece = piece * (HEAD_DIM ** -0.5)
            slabs[j][...] = piece
        for d, q_ref in zip(DILATIONS, q_refs):
            for r in range(d):
                for j in range(nsl):
                    q_ref[r, :, LANES * j:LANES * (j + 1)] = _rows_of(slabs[j], r, T // d, d).astype(bf16)

    row = lambda c: pl.BlockSpec((T, c), lambda i: (i, 0))
    return pl.pallas_call(
        body, grid=(S // T,), name="inproj_fwd",
        out_shape=(jax.ShapeDtypeStruct((S, 2 * CONV_CH), f32), jax.ShapeDtypeStruct((S, 2 * GMLP_CH), f32))
        + tuple(_perm_shape(S, d, QKV_CH, bf16) for d in DILATIONS),
        in_specs=[row(D_MODEL), _resident((D_MODEL, IN_CH)), _resident((1, IN_CH))],
        out_specs=(row(2 * CONV_CH), row(2 * GMLP_CH)) + tuple(_perm_spec(d, QKV_CH) for d in DILATIONS),
        scratch_shapes=_slabs(nsl, T),
        compiler_params=_cp(("parallel",), 48),
    )(x, w, b)


CONV_GROUP = 64


def _window_rolls(starts):
    groups = {}
    for s in starts:
        groups.setdefault((-s) % SUBLANES, []).append(s)
    return dict(sorted(groups.items()))


def _conv_fwd(a_in, dw_w, dw_b, ln_g, ln_b):
    S = a_in.shape[0]
    T = 512
    hb = T // CONV_HALO

    def body(a_ref, halo_ref, w_ref, b_ref, g_ref, be_ref, out_ref, hc_ref, buf):
        i = pl.program_id(0)
        am = a_ref[...]
        ah = halo_ref[...]
        hgh = ah[:, :CONV_CH] * _sig(ah[:, CONV_CH:])
        buf[0:CONV_HALO, :] = jnp.where(i > 0, hgh, 0.0)
        buf[CONV_HALO:, :] = am[:, :CONV_CH] * _sig(am[:, CONV_CH:])
        starts = _window_rolls(range(CONV_HALO - (CONV_WIDTH - 1), CONV_HALO + 1))
        slabs = [slice(LANES * j, LANES * (j + 1)) for j in range(CONV_CH // LANES)]

        def step(g, _):
            r0 = pl.multiple_of(g * CONV_GROUP, CONV_GROUP)
            rows = pl.ds(r0, CONV_GROUP)
            for cs in slabs:
                ext = buf[pl.ds(r0, CONV_GROUP + CONV_HALO), cs]
                acc = jnp.broadcast_to(b_ref[:, cs], (CONV_GROUP, LANES))
                for b, ss in starts.items():
                    rolled = ext if b == 0 else pltpu.roll(ext, b, 0)
                    for s in ss:
                        k = s - (CONV_HALO - (CONV_WIDTH - 1))
                        acc = acc + w_ref[k:k + 1, cs] * rolled[s + b:s + b + CONV_GROUP]
                hc_ref[rows, cs] = acc
            return 0

        lax.fori_loop(0, T // CONV_GROUP, step, 0)
        xhat, _ = _ln_stats(hc_ref[...])
        y = xhat * g_ref[...] + be_ref[...]
        out_ref[...] = (y * _sig(y)).astype(bf16)

    return pl.pallas_call(
        body, grid=(S // T,), name="conv_fwd",
        out_shape=(jax.ShapeDtypeStruct((S, CONV_CH), bf16), jax.ShapeDtypeStruct((S, CONV_CH), f32)),
        in_specs=[pl.BlockSpec((T, 2 * CONV_CH), lambda i: (i, 0)),
                  pl.BlockSpec((CONV_HALO, 2 * CONV_CH), lambda i: (jnp.maximum(i * hb - 1, 0), 0)),
                  _acc((32, CONV_CH)), _acc((1, CONV_CH)), _acc((1, CONV_CH)), _acc((1, CONV_CH))],
        out_specs=(pl.BlockSpec((T, CONV_CH), lambda i: (i, 0)), pl.BlockSpec((T, CONV_CH), lambda i: (i, 0))),
        scratch_shapes=[pltpu.VMEM((T + CONV_HALO, CONV_CH), f32)],
        compiler_params=_cp(("parallel",), 32),
    )(a_in, a_in, dw_w, dw_b, ln_g, ln_b)


def _bias_build(table, buckets):
    def body(t_ref, bk_ref, o_ref):
        h = pl.program_id(1)
        ids = bk_ref[0]
        acc = jnp.zeros((ATTN_BLOCK, 2 * ATTN_BLOCK), f32)
        for b in range(N_BUCKETS):
            acc = jnp.where(ids == b, t_ref[b, h], acc)
        o_ref[0, 0] = acc

    return pl.pallas_call(
        body, grid=(len(DILATIONS), ATTN_HEADS), name="bias_build",
        out_shape=jax.ShapeDtypeStruct((len(DILATIONS), ATTN_HEADS, ATTN_BLOCK, 2 * ATTN_BLOCK), f32),
        in_specs=[pl.BlockSpec(memory_space=pltpu.SMEM),
                  pl.BlockSpec((1, ATTN_BLOCK, 2 * ATTN_BLOCK), lambda p, h: (p, 0, 0))],
        out_specs=pl.BlockSpec((1, 1, ATTN_BLOCK, 2 * ATTN_BLOCK), lambda p, h: (p, h, 0, 0)),
        compiler_params=_cp(("arbitrary", "arbitrary"), 16),
    )(table, buckets)


def _head_tile(tile, h, col):
    lane_head = lax.broadcasted_iota(jnp.int32, tile.shape, 1) // 16
    return jnp.where(lane_head == h, col, tile)


HEAD_PAIRS = ATTN_HEADS // 2
UNITS_PER_BLOCK = ATTN_HEADS


def _attn_tile(L):
    return min(512, L)


def _band_mask(first_block, n):
    B = ATTN_BLOCK
    row = lax.broadcasted_iota(jnp.int32, (B, 2 * B), 0)
    col = lax.broadcasted_iota(jnp.int32, (B, 2 * B), 1)
    valid = (col >= row) & (col <= row + B)
    if first_block:
        valid = valid & ((col >= B) | (n > 0))
    return valid


def _head_lanes(a):
    lane = lax.broadcasted_iota(jnp.int32, (ATTN_BLOCK, LANES), 1)
    return (lane < HEAD_DIM) if a == 0 else (lane >= HEAD_DIM)


def _pair_keys(cur_ref, halo_ref, part, b, j):
    B = ATTN_BLOCK
    c0 = part * ATTN_CH + LANES * j
    own = cur_ref[B * b:B * (b + 1), c0:c0 + LANES]
    prev = halo_ref[:, LANES * j:LANES * (j + 1)] if b == 0 else cur_ref[B * (b - 1):B * b, c0:c0 + LANES]
    return jnp.concatenate([prev, own], axis=0)


def _attn_fwd_pattern(qkv, bias, d):
    _, L, _ = qkv.shape
    B = ATTN_BLOCK
    QB = _attn_tile(L)
    nsb = QB // B
    U = nsb * UNITS_PER_BLOCK

    def body(cur_ref, hk_ref, hv_ref, b_ref, o_ref, lse_ref, lg, pb):
        n = pl.program_id(1)
        for b in range(nsb):
            valid = _band_mask(b == 0, n)
            for j in range(HEAD_PAIRS):
                q2 = cur_ref[B * b:B * (b + 1), LANES * j:LANES * (j + 1)]
                k2 = _pair_keys(cur_ref, hk_ref, 1, b, j)
                for a in range(2):
                    u = (b * HEAD_PAIRS + j) * 2 + a
                    qm = jnp.where(_head_lanes(a), q2, jnp.zeros_like(q2))
                    logits = lax.dot_general(qm, k2, NT_DIMS, preferred_element_type=f32) + b_ref[2 * j + a]
                    lg[B * u:B * (u + 1), :] = jnp.where(valid, logits, NEG)
        m = jnp.max(lg[...], axis=1, keepdims=True)
        p = jnp.exp(lg[...] - m)
        s = jnp.sum(p, axis=1, keepdims=True)
        pb[...] = p.astype(bf16)
        lse = m + jnp.log(s)
        inv = 1.0 / s
        for b in range(nsb):
            tile = jnp.zeros((B, B), f32)
            for j in range(HEAD_PAIRS):
                v2 = _pair_keys(cur_ref, hv_ref, 2, b, j)
                outs = []
                for a in range(2):
                    u = (b * HEAD_PAIRS + j) * 2 + a
                    rows = slice(B * u, B * (u + 1))
                    outs.append(jnp.dot(pb[rows, :], v2, preferred_element_type=f32) * inv[rows])
                    tile = _head_tile(tile, 2 * j + a, lse[rows])
                o_ref[B * b:B * (b + 1), LANES * j:LANES * (j + 1)] = jnp.where(_head_lanes(0), outs[0], outs[1])
            lse_ref[B * b:B * (b + 1), :] = tile

    halo = lambda part: pl.BlockSpec((None, B, ATTN_CH), lambda r, n: (r, jnp.maximum(n * nsb - 1, 0), part))
    tile_spec = lambda c: pl.BlockSpec((None, QB, c), lambda r, n: (r, n, 0))
    return pl.pallas_call(
        body, grid=(d, L // QB), name=f"attn_fwd_d{d}",
        out_shape=(jax.ShapeDtypeStruct((d, L, ATTN_CH), f32), jax.ShapeDtypeStruct((d, L, B), f32)),
        in_specs=[tile_spec(QKV_CH), halo(1), halo(2), _resident((ATTN_HEADS, B, 2 * B))],
        out_specs=(tile_spec(ATTN_CH), tile_spec(B)),
        scratch_shapes=[pltpu.VMEM((U * B, 2 * B), f32), pltpu.VMEM((U * B, 2 * B), bf16)],
        compiler_params=_cp(("parallel", "parallel"), 40),
    )(qkv, qkv, qkv, bias)


def _attn_merge(parts):
    S = parts[0][0].shape[0] * parts[0][0].shape[1]
    T = PERM_TILE
    nsl = ATTN_CH // LANES
    n_p = len(DILATIONS)

    def body(*refs):
        ins = refs[:2 * n_p]
        out_ref, lse_ref = refs[2 * n_p:2 * n_p + 2]
        slabs = refs[2 * n_p + 2:]
        lses = []
        for p, d in enumerate(DILATIONS):
            o_ref, l_ref = ins[2 * p], ins[2 * p + 1]
            osl = slabs[p * (nsl + 1):p * (nsl + 1) + nsl]
            lsl = slabs[p * (nsl + 1) + nsl]
            for r in range(d):
                for j in range(nsl):
                    _set_rows_of(osl[j], r, T // d, d, o_ref[r, :, LANES * j:LANES * (j + 1)])
                _set_rows_of(lsl, r, T // d, d, l_ref[r])
            lses.append(lsl[...])
        big = functools.reduce(jnp.maximum, lses)
        ws = [jnp.exp(l - big) for l in lses]
        tot = functools.reduce(lambda a_, b_: a_ + b_, ws)
        lse_ref[...] = big + jnp.log(tot)
        ws = [w / tot for w in ws]
        for j in range(nsl):
            acc = jnp.zeros((T, LANES), f32)
            for p in range(n_p):
                wa = ws[p][:, 32 * j:32 * j + 1]
                wb = ws[p][:, 32 * j + 16:32 * j + 17]
                lane = lax.broadcasted_iota(jnp.int32, (T, LANES), 1)
                acc = acc + jnp.where(lane < HEAD_DIM, wa, wb) * slabs[p * (nsl + 1) + j][...]
            out_ref[:, LANES * j:LANES * (j + 1)] = acc.astype(bf16)

    in_specs, args = [], []
    for (o, l), d in zip(parts, DILATIONS):
        in_specs += [_perm_spec(d, ATTN_CH), _perm_spec(d, ATTN_BLOCK)]
        args += [o, l]
    row = lambda c: pl.BlockSpec((T, c), lambda i: (i, 0))
    return pl.pallas_call(
        body, grid=(S // T,), name="attn_merge",
        out_shape=(jax.ShapeDtypeStruct((S, ATTN_CH), bf16), jax.ShapeDtypeStruct((S, ATTN_BLOCK), f32)),
        in_specs=in_specs, out_specs=(row(ATTN_CH), row(ATTN_BLOCK)),
        scratch_shapes=_slabs(n_p * (nsl + 1), T),
        compiler_params=_cp(("parallel",), 40),
    )(*args)


def _attn_fwd(qkvs, bias):
    parts = [_attn_fwd_pattern(q, bias[p], d) for p, (q, d) in enumerate(zip(qkvs, DILATIONS))]
    return _attn_merge(parts)


def _tril_bf16(w):
    row = lax.broadcasted_iota(jnp.int32, (CHUNK, CHUNK), 0)
    col = lax.broadcasted_iota(jnp.int32, (CHUNK, CHUNK), 1)
    return jnp.where(col <= row, w, 0.0).astype(bf16)


def _gmlp_fwd(c_in, ln_g, ln_b, w_s, b_s_t):
    S = c_in.shape[0]
    T = 512

    def body(c_ref, g_ref, be_ref, w_ref, bs_ref, out_ref, mix):
        c = c_ref[...]
        xhat, _ = _ln_stats(c[:, GMLP_CH:])
        vb = (xhat * g_ref[...] + be_ref[...]).astype(bf16)
        for g in range(GMLP_GROUPS):
            wt = _tril_bf16(w_ref[g])
            cs = slice(GMLP_GROUP_DIM * g, GMLP_GROUP_DIM * (g + 1))
            for ci in range(T // CHUNK):
                rs = slice(CHUNK * ci, CHUNK * (ci + 1))
                mix[rs, cs] = jnp.dot(wt, vb[rs, cs], preferred_element_type=f32) + bs_ref[:, g:g + 1]
        out_ref[...] = (c[:, :GMLP_CH] * mix[...]).astype(bf16)

    return pl.pallas_call(
        body, grid=(S // T,), name="gmlp_fwd",
        out_shape=jax.ShapeDtypeStruct((S, GMLP_CH), bf16),
        in_specs=[pl.BlockSpec((T, 2 * GMLP_CH), lambda i: (i, 0)), _acc((1, GMLP_CH)), _acc((1, GMLP_CH)),
                  _acc((GMLP_GROUPS, CHUNK, CHUNK)), _acc((CHUNK, GMLP_GROUPS))],
        out_specs=pl.BlockSpec((T, GMLP_CH), lambda i: (i, 0)),
        scratch_shapes=[pltpu.VMEM((T, GMLP_CH), f32)],
        compiler_params=_cp(("parallel",), 32),
    )(c_in, ln_g, ln_b, w_s, b_s_t)


def _outproj_ln_fwd(conv_out, attn_out, gm_out, w, b, x, ln_g, ln_b):
    S = x.shape[0]
    T = 512

    def body(co_ref, ao_ref, go_ref, w_ref, b_ref, x_ref, g_ref, be_ref, cat_ref, z_ref, yb_ref):
        cat = jnp.concatenate([co_ref[...], ao_ref[...], go_ref[...]], axis=1)
        cat_ref[...] = cat
        z = jnp.dot(cat, w_ref[...], preferred_element_type=f32) + b_ref[...] + ALPHA * x_ref[...]
        z_ref[...] = z
        xhat, _ = _ln_stats(z)
        yb_ref[...] = (xhat * g_ref[...] + be_ref[...]).astype(bf16)

    row = lambda c: pl.BlockSpec((T, c), lambda i: (i, 0))
    return pl.pallas_call(
        body, grid=(S // T,), name="outproj_ln_fwd",
        out_shape=(jax.ShapeDtypeStruct((S, D_MODEL), bf16), jax.ShapeDtypeStruct((S, D_MODEL), f32),
                   jax.ShapeDtypeStruct((S, D_MODEL), bf16)),
        in_specs=[row(CONV_CH), row(ATTN_CH), row(GMLP_CH), _resident((D_MODEL, D_MODEL)), _acc((1, D_MODEL)),
                  row(D_MODEL), _acc((1, D_MODEL)), _acc((1, D_MODEL))],
        out_specs=(row(D_MODEL), row(D_MODEL), row(D_MODEL)),
        compiler_params=_cp(("parallel",), 40),
    )(conv_out, attn_out, gm_out, w, b, x, ln_g, ln_b)


GATE_ROWS = 32
GATE_COLS = 128
GATE_MM_COLS = 256
SUBLANES = 8


def _gate_cols(c0):
    return slice(c0, c0 + GATE_COLS), slice(D_FF + c0, D_FF + c0 + GATE_COLS)


def _bcast_rows(ref, k, cs):
    return jnp.broadcast_to(ref[k:k + 1, cs], (GATE_ROWS, GATE_COLS))


def _fold_rows(z):
    acc = z[0:SUBLANES]
    for r in range(SUBLANES, GATE_ROWS, SUBLANES):
        acc = acc + z[r:r + SUBLANES]
    return acc


def _ffn_up_gate_fwd(x1b, w, b, conv_w, conv_b):
    S = x1b.shape[0]
    T = 256
    H = FFN_HALO
    K = FFN_CONV_WIDTH

    def body(x_ref, w_ref, b_ref, cw_ref, cb_ref, hfb_ref, hc_ref, act_ref, hbuf, carry):
        @pl.when(pl.program_id(0) == 0)
        def _():
            carry[...] = jnp.zeros_like(carry)
        x = x_ref[...]
        for m0 in range(0, D_FF, GATE_MM_COLS):
            for cm in (slice(m0, m0 + GATE_MM_COLS), slice(D_FF + m0, D_FF + m0 + GATE_MM_COLS)):
                h = jnp.dot(x, w_ref[:, cm], preferred_element_type=f32) + b_ref[:, cm]
                hbuf[:, cm] = h
                hfb_ref[:, cm] = h.astype(bf16)
            for c0 in range(m0, m0 + GATE_MM_COLS, GATE_COLS):
                cols = _gate_cols(c0)
                wts = [[_bcast_rows(cw_ref, k, cs) for k in range(K)] + [_bcast_rows(cb_ref, 0, cs)] for cs in cols]

                def step(rg, tails, cols=cols, wts=wts):
                    rows = pl.ds(pl.multiple_of(rg * GATE_ROWS, GATE_ROWS), GATE_ROWS)
                    hc, new_tails = [], []
                    for cs, wt, tail in zip(cols, wts, tails):
                        h = hbuf[rows, cs]
                        ext = jnp.concatenate([tail, h], axis=0)
                        acc = wt[K] + wt[K - 1] * h
                        for back in range(1, K):
                            acc = acc + wt[K - 1 - back] * pltpu.roll(ext, back, 0)[H:]
                        hc_ref[rows, cs] = acc
                        hc.append(acc)
                        new_tails.append(h[GATE_ROWS - H:])
                    act_ref[rows, cols[0]] = (hc[0] * _sig(hc[0]) * hc[1]).astype(bf16)
                    return tuple(new_tails)

                tails = lax.fori_loop(0, T // GATE_ROWS, step, tuple(carry[:, cs] for cs in cols), unroll=True)
                for cs, tail in zip(cols, tails):
                    carry[:, cs] = tail

    row = lambda c: pl.BlockSpec((T, c), lambda i: (i, 0))
    return pl.pallas_call(
        body, grid=(S // T,), name="ffn_up_gate_fwd",
        out_shape=(jax.ShapeDtypeStruct((S, 2 * D_FF), bf16), jax.ShapeDtypeStruct((S, 2 * D_FF), f32),
                   jax.ShapeDtypeStruct((S, D_FF), bf16)),
        in_specs=[row(D_MODEL), _resident((D_MODEL, 2 * D_FF)), _acc((1, 2 * D_FF)), _acc((8, 2 * D_FF)),
                  _acc((1, 2 * D_FF))],
        out_specs=(row(2 * D_FF), row(2 * D_FF), row(D_FF)),
        scratch_shapes=[pltpu.VMEM((T, 2 * D_FF), f32), pltpu.VMEM((H, 2 * D_FF), f32)],
        compiler_params=_cp(("arbitrary",), 56),
    )(x1b, w, b, conv_w, conv_b)


def _ffn_down_ln_fwd(act, w, b, z1, ln1_g, ln1_b, ln_g, ln_b):
    S = act.shape[0]
    T = 512

    def body(a_ref, w_ref, b_ref, z1_ref, g1_ref, be1_ref, g_ref, be_ref, z_ref, y_ref):
        x1 = _ln_stats(z1_ref[...])[0] * g1_ref[...] + be1_ref[...]
        z = jnp.dot(a_ref[...], w_ref[...], preferred_element_type=f32) + b_ref[...] + ALPHA * x1
        z_ref[...] = z
        xhat, _ = _ln_stats(z)
        y_ref[...] = xhat * g_ref[...] + be_ref[...]

    row = lambda c: pl.BlockSpec((T, c), lambda i: (i, 0))
    return pl.pallas_call(
        body, grid=(S // T,), name="ffn_down_ln_fwd",
        out_shape=(jax.ShapeDtypeStruct((S, D_MODEL), f32), jax.ShapeDtypeStruct((S, D_MODEL), f32)),
        in_specs=[row(D_FF), _resident((D_FF, D_MODEL)), _acc((1, D_MODEL)), row(D_MODEL)] + [_acc((1, D_MODEL))] * 4,
        out_specs=(row(D_MODEL), row(D_MODEL)),
        compiler_params=_cp(("parallel",), 40),
    )(act, w, b, z1, ln1_g, ln1_b, ln_g, ln_b)


def _ffn_down_ln_loss(act, w, b, z1, ln1_g, ln1_b, ln_g, ln_b, target):
    S = act.shape[0]
    T = 512

    def body(a_ref, w_ref, b_ref, z1_ref, g1_ref, be1_ref, g_ref, be_ref, t_ref, dz_ref, dzb_ref, loss_ref, dg_ref,
             db_ref):
        @pl.when(pl.program_id(0) == 0)
        def _():
            loss_ref[...] = jnp.zeros_like(loss_ref)
            dg_ref[...] = jnp.zeros_like(dg_ref)
            db_ref[...] = jnp.zeros_like(db_ref)
        x1 = _ln_stats(z1_ref[...])[0] * g1_ref[...] + be1_ref[...]
        z = jnp.dot(a_ref[...], w_ref[...], preferred_element_type=f32) + b_ref[...] + ALPHA * x1
        xhat, rstd = _ln_stats(z)
        err = xhat * g_ref[...] + be_ref[...] - t_ref[...]
        loss_ref[...] += _colsum(err * err) * (0.5 / D_MODEL)
        dy = err * (1.0 / D_MODEL)
        dz = _ln_bwd(dy, xhat, rstd, g_ref[...])
        dz_ref[...] = dz
        dzb_ref[...] = dz.astype(bf16)
        dg_ref[...] += _colsum(dy * xhat)
        db_ref[...] += _colsum(dy)

    row = lambda c: pl.BlockSpec((T, c), lambda i: (i, 0))
    vec = jax.ShapeDtypeStruct((1, D_MODEL), f32)
    return pl.pallas_call(
        body, grid=(S // T,), name="ffn_down_ln_loss",
        out_shape=(jax.ShapeDtypeStruct((S, D_MODEL), f32), jax.ShapeDtypeStruct((S, D_MODEL), bf16), vec, vec, vec),
        in_specs=[row(D_FF), _resident((D_FF, D_MODEL)), _acc((1, D_MODEL)), row(D_MODEL)] + [_acc((1, D_MODEL))] * 4
        + [row(D_MODEL)],
        out_specs=(row(D_MODEL), row(D_MODEL), _acc((1, D_MODEL)), _acc((1, D_MODEL)), _acc((1, D_MODEL))),
        compiler_params=_cp(("arbitrary",), 40),
    )(act, w, b, z1, ln1_g, ln1_b, ln_g, ln_b, target)


def _dgrad_ln_bwd(g, w, dz_res, z, ln_g, name):
    S, K = g.shape
    T = 256
    with_ln = z is not None

    def body(*refs):
        if with_ln:
            g_ref, w_ref, r_ref, z_ref, lg_ref, dz_ref, dzb_ref, dg_ref, db_ref = refs
        else:
            g_ref, w_ref, r_ref, dx_ref = refs
        dx = lax.dot_general(g_ref[...], w_ref[...], NT_DIMS, preferred_element_type=f32) + ALPHA * r_ref[...]
        if not with_ln:
            dx_ref[...] = dx
            return

        @pl.when(pl.program_id(0) == 0)
        def _():
            dg_ref[...] = jnp.zeros_like(dg_ref)
            db_ref[...] = jnp.zeros_like(db_ref)
        xhat, rstd = _ln_stats(z_ref[...])
        dz = _ln_bwd(dx, xhat, rstd, lg_ref[...])
        dz_ref[...] = dz
        dzb_ref[...] = dz.astype(bf16)
        dg_ref[...] += _colsum(dx * xhat)
        db_ref[...] += _colsum(dx)

    row = pl.BlockSpec((T, D_MODEL), lambda i: (i, 0))
    vec = jax.ShapeDtypeStruct((1, D_MODEL), f32)
    in_specs = [pl.BlockSpec((T, K), lambda i: (i, 0)), _resident((D_MODEL, K)), row]
    args = [g, w, dz_res]
    if with_ln:
        in_specs += [row, _acc((1, D_MODEL))]
        args += [z, ln_g]
        out_shape = (jax.ShapeDtypeStruct((S, D_MODEL), f32), jax.ShapeDtypeStruct((S, D_MODEL), bf16), vec, vec)
        out_specs = (row, row, _acc((1, D_MODEL)), _acc((1, D_MODEL)))
    else:
        out_shape = jax.ShapeDtypeStruct((S, D_MODEL), f32)
        out_specs = row
    return pl.pallas_call(
        body, grid=(S // T,), name=name, out_shape=out_shape, in_specs=in_specs, out_specs=out_specs,
        compiler_params=_cp(("arbitrary",), 48),
    )(*args)


def _ffn_down_gate_bwd(dzb, w_down, hfb, hc, conv_w):
    S = hc.shape[0]
    T = 256
    H = FFN_HALO
    nt = S // T
    K = FFN_CONV_WIDTH

    def body(dz_ref, w_ref, h_ref, hc_ref, cw_ref, dh_ref, dw_ref, dcb_ref, da_buf, carry):
        @pl.when(pl.program_id(0) == 0)
        def _():
            dw_ref[...] = jnp.zeros_like(dw_ref)
            dcb_ref[...] = jnp.zeros_like(dcb_ref)
            carry[...] = jnp.zeros_like(carry)
        da_buf[...] = lax.dot_general(dz_ref[...], w_ref[...], NT_DIMS, preferred_element_type=f32)
        ngroups = T // GATE_ROWS
        for c0 in range(0, D_FF, GATE_COLS):
            cols = _gate_cols(c0)
            wts = [[_bcast_rows(cw_ref, k, cs) for k in range(K)] for cs in cols]

            def step(it, state, cols=cols, wts=wts):
                heads, accs = state
                rows = pl.ds(pl.multiple_of((ngroups - 1 - it) * GATE_ROWS, GATE_ROWS), GATE_ROWS)
                g = hc_ref[rows, cols[0]]
                v = hc_ref[rows, cols[1]]
                da = da_buf[rows, cols[0]]
                sg = _sig(g)
                dms = (da * v * (sg * (1.0 + g * (1.0 - sg))), da * (g * sg))
                new_heads, new_accs = [], []
                for cs, wt, dm, head, acc in zip(cols, wts, dms, heads, accs):
                    h0 = h_ref[rows, cs].astype(f32)
                    ext = jnp.concatenate([dm, head], axis=0)
                    dh = wt[K - 1] * dm
                    acc_k = [None] * K + [acc[K] + _fold_rows(dm)]
                    acc_k[K - 1] = acc[K - 1] + _fold_rows(dm * h0)
                    for ahead in range(1, K):
                        dk = pltpu.roll(ext, GATE_ROWS + H - ahead, 0)[:GATE_ROWS]
                        dh = dh + wt[K - 1 - ahead] * dk
                        acc_k[K - 1 - ahead] = acc[K - 1 - ahead] + _fold_rows(dk * h0)
                    dh_ref[rows, cs] = dh.astype(bf16)
                    new_heads.append(dm[:H])
                    new_accs.append(tuple(acc_k))
                return tuple(new_heads), tuple(new_accs)

            zero = jnp.zeros((SUBLANES, GATE_COLS), f32)
            init = (tuple(carry[:, cs] for cs in cols), tuple(tuple(zero for _ in range(K + 1)) for _ in cols))
            heads, accs = lax.fori_loop(0, ngroups, step, init, unroll=True)
            for cs, head, acc in zip(cols, heads, accs):
                carry[:, cs] = head
                dcb_ref[:, cs] += _colsum(acc[K])
                for k in range(K):
                    dw_ref[k:k + 1, cs] += _colsum(acc[k])

    tile = lambda c: pl.BlockSpec((T, c), lambda i: (nt - 1 - i, 0))
    return pl.pallas_call(
        body, grid=(nt,), name="ffn_down_gate_bwd",
        out_shape=(jax.ShapeDtypeStruct((S, 2 * D_FF), bf16), jax.ShapeDtypeStruct((8, 2 * D_FF), f32),
                   jax.ShapeDtypeStruct((1, 2 * D_FF), f32)),
        in_specs=[tile(D_MODEL), _resident((D_FF, D_MODEL)), tile(2 * D_FF), tile(2 * D_FF), _acc((8, 2 * D_FF))],
        out_specs=(tile(2 * D_FF), _acc((8, 2 * D_FF)), _acc((1, 2 * D_FF))),
        scratch_shapes=[pltpu.VMEM((T, D_FF), f32), pltpu.VMEM((H, 2 * D_FF), f32)],
        compiler_params=_cp(("arbitrary",), 48),
    )(dzb, w_down, hfb, hc, conv_w)


def _wgrad(a, g, tn, name, rows=1024):
    S, K = a.shape
    N = g.shape[1]
    T = rows if S % rows == 0 else S

    def body(a_ref, g_ref, dw_ref, db_ref):
        @pl.when(pl.program_id(1) == 0)
        def _():
            dw_ref[...] = jnp.zeros_like(dw_ref)
            db_ref[...] = jnp.zeros_like(db_ref)
        gt = g_ref[...]
        dw_ref[...] += lax.dot_general(a_ref[...].astype(bf16), gt, TN_DIMS, preferred_element_type=f32)
        db_ref[...] += _colsum(gt.astype(f32))

    return pl.pallas_call(
        body, grid=(N // tn, S // T), name=name,
        out_shape=(jax.ShapeDtypeStruct((K, N), f32), jax.ShapeDtypeStruct((1, N), f32)),
        in_specs=[pl.BlockSpec((T, K), lambda j, i: (i, 0)), pl.BlockSpec((T, tn), lambda j, i: (i, j))],
        out_specs=(pl.BlockSpec((K, tn), lambda j, i: (0, j)), pl.BlockSpec((1, tn), lambda j, i: (0, j))),
        compiler_params=_cp(("parallel", "arbitrary"), 56),
    )(a, g)


def _outproj_dgrad(dzb, w, attn_out, lse):
    S = dzb.shape[0]
    T = PERM_TILE
    nsl = ATTN_CH // LANES
    n_p = len(DILATIONS)

    def body(g_ref, w_ref, ao_ref, lse_ref, dco_ref, dgo_ref, *rest):
        do_refs = rest[:n_p]
        st_refs = rest[n_p:2 * n_p]
        slabs = rest[2 * n_p:]
        dcat = lax.dot_general(g_ref[...], w_ref[...], NT_DIMS, preferred_element_type=f32)
        dco_ref[...] = dcat[:, :CONV_CH]
        dgo_ref[...] = dcat[:, CONV_CH + ATTN_CH:]
        lane = lax.broadcasted_iota(jnp.int32, (T, LANES), 1)
        st = lse_ref[...]
        for j in range(nsl):
            dO = dcat[:, CONV_CH + LANES * j:CONV_CH + LANES * (j + 1)]
            prod = dO * ao_ref[:, LANES * j:LANES * (j + 1)].astype(f32)
            for a in range(2):
                in_head = (lane < HEAD_DIM) if a == 0 else (lane >= HEAD_DIM)
                delta = jnp.sum(jnp.where(in_head, prod, 0.0), axis=1, keepdims=True)
                st = jnp.where((lane // 16 == 2 * j + a) & (lane % 16 >= 8), delta, st)
            slabs[j][...] = dO
        slabs[nsl][...] = st
        for d, do_ref, st_ref in zip(DILATIONS, do_refs, st_refs):
            for r in range(d):
                for j in range(nsl):
                    do_ref[r, :, LANES * j:LANES * (j + 1)] = _rows_of(slabs[j], r, T // d, d).astype(bf16)
                st_ref[r] = _rows_of(slabs[nsl], r, T // d, d)

    row = lambda c: pl.BlockSpec((T, c), lambda i: (i, 0))
    return pl.pallas_call(
        body, grid=(S // T,), name="outproj_dgrad",
        out_shape=(jax.ShapeDtypeStruct((S, CONV_CH), f32), jax.ShapeDtypeStruct((S, GMLP_CH), f32))
        + tuple(_perm_shape(S, d, ATTN_CH, bf16) for d in DILATIONS)
        + tuple(_perm_shape(S, d, ATTN_BLOCK, f32) for d in DILATIONS),
        in_specs=[row(D_MODEL), _resident((D_MODEL, D_MODEL)), row(ATTN_CH), row(ATTN_BLOCK)],
        out_specs=(row(CONV_CH), row(GMLP_CH)) + tuple(_perm_spec(d, ATTN_CH) for d in DILATIONS)
        + tuple(_perm_spec(d, ATTN_BLOCK) for d in DILATIONS),
        scratch_shapes=_slabs(nsl + 1, T),
        compiler_params=_cp(("parallel",), 40),
    )(dzb, w, attn_out, lse)


def _gmlp_bwd(c_in, dgm, ln_g, ln_b, w_s, b_s_t):
    S = c_in.shape[0]
    T = 512
    nsteps = S // T

    def body(c_ref, dg_ref, g_ref, be_ref, w_ref, bs_ref, dc_ref, dlg_ref, dlb_ref, dw_ref, dbs_ref,
             du_buf, dv_buf, dm_acc):
        i = pl.program_id(0)

        @pl.when(i == 0)
        def _():
            dlg_ref[...] = jnp.zeros_like(dlg_ref)
            dlb_ref[...] = jnp.zeros_like(dlb_ref)
            dw_ref[...] = jnp.zeros_like(dw_ref)
            dm_acc[...] = jnp.zeros_like(dm_acc)
        c = c_ref[...]
        u = c[:, :GMLP_CH]
        xhat, rstd = _ln_stats(c[:, GMLP_CH:])
        vb = (xhat * g_ref[...] + be_ref[...]).astype(bf16)
        dgm_t = dg_ref[...]
        dm_all = dgm_t * u
        for g in range(GMLP_GROUPS):
            wt = _tril_bf16(w_ref[g])
            cs = slice(GMLP_GROUP_DIM * g, GMLP_GROUP_DIM * (g + 1))
            dw_g = jnp.zeros((CHUNK, CHUNK), f32)
            for ci in range(T // CHUNK):
                rs = slice(CHUNK * ci, CHUNK * (ci + 1))
                v_c = vb[rs, cs]
                mixed = jnp.dot(wt, v_c, preferred_element_type=f32) + bs_ref[:, g:g + 1]
                dm = dm_all[rs, cs]
                dmb = dm.astype(bf16)
                du_buf[rs, cs] = dgm_t[rs, cs] * mixed
                dv_buf[rs, cs] = lax.dot_general(wt, dmb, TN_DIMS, preferred_element_type=f32)
                dw_g = dw_g + lax.dot_general(dmb, v_c, NT_DIMS, preferred_element_type=f32)
                dm_acc[:, cs] += dm
            dw_ref[g] += dw_g
        dv = dv_buf[...]
        dvr = _ln_bwd(dv, xhat, rstd, g_ref[...])
        dlg_ref[...] += _colsum(dv * xhat)
        dlb_ref[...] += _colsum(dv)
        dc_ref[:, :GMLP_CH] = du_buf[...].astype(bf16)
        dc_ref[:, GMLP_CH:] = dvr.astype(bf16)

        @pl.when(i == nsteps - 1)
        def _():
            row = lax.broadcasted_iota(jnp.int32, (CHUNK, CHUNK), 0)
            col = lax.broadcasted_iota(jnp.int32, (CHUNK, CHUNK), 1)
            tile = jnp.zeros((CHUNK, CHUNK), f32)
            for g in range(GMLP_GROUPS):
                dw_ref[g] = jnp.where(col <= row, dw_ref[g], 0.0)
                gsum = jnp.sum(dm_acc[:, GMLP_GROUP_DIM * g:GMLP_GROUP_DIM * (g + 1)], axis=1, keepdims=True)
                tile = jnp.where(col == g, gsum, tile)
            dbs_ref[...] = tile

    vec = jax.ShapeDtypeStruct((1, GMLP_CH), f32)
    return pl.pallas_call(
        body, grid=(nsteps,), name="gmlp_bwd",
        out_shape=(jax.ShapeDtypeStruct((S, 2 * GMLP_CH), bf16), vec, vec,
                   jax.ShapeDtypeStruct((GMLP_GROUPS, CHUNK, CHUNK), f32), jax.ShapeDtypeStruct((CHUNK, CHUNK), f32)),
        in_specs=[pl.BlockSpec((T, 2 * GMLP_CH), lambda i: (i, 0)), pl.BlockSpec((T, GMLP_CH), lambda i: (i, 0)),
                  _acc((1, GMLP_CH)), _acc((1, GMLP_CH)), _acc((GMLP_GROUPS, CHUNK, CHUNK)), _acc((CHUNK, GMLP_GROUPS))],
        out_specs=(pl.BlockSpec((T, 2 * GMLP_CH), lambda i: (i, 0)), _acc((1, GMLP_CH)), _acc((1, GMLP_CH)),
                   _acc((GMLP_GROUPS, CHUNK, CHUNK)), _acc((CHUNK, CHUNK))),
        scratch_shapes=[pltpu.VMEM((T, GMLP_CH), f32), pltpu.VMEM((T, GMLP_CH), f32), pltpu.VMEM((CHUNK, GMLP_CH), f32)],
        compiler_params=_cp(("arbitrary",), 32),
    )(c_in, dgm, ln_g, ln_b, w_s, b_s_t)


def _attn_bwd_pattern(qkv, d_out, stats, bias, d):
    _, L, _ = qkv.shape
    B = ATTN_BLOCK
    QB = _attn_tile(L)
    nsb = QB // B
    nt = L // QB
    U = nsb * UNITS_PER_BLOCK
    KV = 2 * ATTN_CH

    def body(cur_ref, hk_ref, hv_ref, do_ref, st_ref, b_ref, dqkv_ref, dbias_ref, lg, dp, pb, dsb, dkv, carry):
        r = pl.program_id(0)
        i = pl.program_id(1)
        n = nt - 1 - i

        @pl.when((r == 0) & (i == 0))
        def _():
            dbias_ref[...] = jnp.zeros_like(dbias_ref)

        @pl.when(i == 0)
        def _():
            carry[...] = jnp.zeros_like(carry)

        def operands(b, j, a):
            rows = slice(B * b, B * (b + 1))
            q2 = cur_ref[rows, LANES * j:LANES * (j + 1)]
            do2 = do_ref[rows, LANES * j:LANES * (j + 1)]
            keep = _head_lanes(a)
            return jnp.where(keep, q2, jnp.zeros_like(q2)), jnp.where(keep, do2, jnp.zeros_like(do2))

        for b in range(nsb):
            valid = _band_mask(b == 0, n)
            for j in range(HEAD_PAIRS):
                k2 = _pair_keys(cur_ref, hk_ref, 1, b, j)
                v2 = _pair_keys(cur_ref, hv_ref, 2, b, j)
                for a in range(2):
                    u = (b * HEAD_PAIRS + j) * 2 + a
                    qm, dom = operands(b, j, a)
                    logits = lax.dot_general(qm, k2, NT_DIMS, preferred_element_type=f32) + b_ref[2 * j + a]
                    lg[B * u:B * (u + 1), :] = jnp.where(valid, logits, NEG)
                    dp[B * u:B * (u + 1), :] = lax.dot_general(dom, v2, NT_DIMS, preferred_element_type=f32)
        for b in range(nsb):
            for j in range(HEAD_PAIRS):
                for a in range(2):
                    u = (b * HEAD_PAIRS + j) * 2 + a
                    rows = slice(B * u, B * (u + 1))
                    lane0 = 32 * j + 16 * a
                    lse = st_ref[B * b:B * (b + 1), lane0:lane0 + 1]
                    delta = st_ref[B * b:B * (b + 1), lane0 + 8:lane0 + 9]
                    p = jnp.exp(lg[rows, :] - lse)
                    ds = p * (dp[rows, :] - delta)
                    pb[rows, :] = p.astype(bf16)
                    dsb[rows, :] = ds.astype(bf16)
                    dbias_ref[2 * j + a] += ds
        dkv[...] = jnp.zeros_like(dkv)
        for b in range(nsb):
            for j in range(HEAD_PAIRS):
                k2 = _pair_keys(cur_ref, hk_ref, 1, b, j)
                dq, dk2, dv2 = [], None, None
                for a in range(2):
                    u = (b * HEAD_PAIRS + j) * 2 + a
                    rows = slice(B * u, B * (u + 1))
                    qm, dom = operands(b, j, a)
                    ds_u = dsb[rows, :]
                    dq.append(jnp.dot(ds_u, k2, preferred_element_type=f32))
                    dk_u = lax.dot_general(ds_u, qm, TN_DIMS, preferred_element_type=f32)
                    dv_u = lax.dot_general(pb[rows, :], dom, TN_DIMS, preferred_element_type=f32)
                    dk2 = dk_u if dk2 is None else dk2 + dk_u
                    dv2 = dv_u if dv2 is None else dv2 + dv_u
                dq2 = jnp.where(_head_lanes(0), dq[0], dq[1]) * (HEAD_DIM ** -0.5)
                dqkv_ref[B * b:B * (b + 1), LANES * j:LANES * (j + 1)] = dq2.astype(bf16)
                dkv[B * b:B * (b + 2), LANES * j:LANES * (j + 1)] += dk2
                dkv[B * b:B * (b + 2), ATTN_CH + LANES * j:ATTN_CH + LANES * (j + 1)] += dv2
        dkv[QB:, :] += carry[...]
        dqkv_ref[:, ATTN_CH:] = dkv[B:, :].astype(bf16)
        carry[...] = dkv[0:B, :]

    halo = lambda part: pl.BlockSpec((None, B, ATTN_CH),
                                     lambda r, i: (r, jnp.maximum((nt - 1 - i) * nsb - 1, 0), part))
    tile_spec = lambda c: pl.BlockSpec((None, QB, c), lambda r, i: (r, nt - 1 - i, 0))
    return pl.pallas_call(
        body, grid=(d, nt), name=f"attn_bwd_d{d}",
        out_shape=(jax.ShapeDtypeStruct((d, L, QKV_CH), bf16), jax.ShapeDtypeStruct((ATTN_HEADS, B, 2 * B), f32)),
        in_specs=[tile_spec(QKV_CH), halo(1), halo(2), tile_spec(ATTN_CH), tile_spec(B),
                  _resident((ATTN_HEADS, B, 2 * B))],
        out_specs=(tile_spec(QKV_CH), _acc((ATTN_HEADS, B, 2 * B))),
        scratch_shapes=[pltpu.VMEM((U * B, 2 * B), f32), pltpu.VMEM((U * B, 2 * B), f32),
                        pltpu.VMEM((U * B, 2 * B), bf16), pltpu.VMEM((U * B, 2 * B), bf16),
                        pltpu.VMEM((B + QB, KV), f32), pltpu.VMEM((B, KV), f32)],
        compiler_params=_cp(("arbitrary", "arbitrary"), 48),
    )(qkv, qkv, qkv, d_out, stats, bias)


def _attn_bwd_merge(d_a, dqkvs, d_c):
    S = d_a.shape[0]
    T = PERM_TILE
    nsl = QKV_CH // LANES
    n_p = len(DILATIONS)

    def body(da_ref, *rest):
        g_refs = rest[:n_p]
        dc_ref, dh_ref = rest[n_p:n_p + 2]
        slabs = rest[n_p + 2:]
        q0 = 2 * CONV_CH
        dh_ref[:, :q0] = da_ref[...]
        dh_ref[:, q0 + QKV_CH:] = dc_ref[...]
        for p, (d, g_ref) in enumerate(zip(DILATIONS, g_refs)):
            for r in range(d):
                for j in range(nsl):
                    _set_rows_of(slabs[p * nsl + j], r, T // d, d, g_ref[r, :, LANES * j:LANES * (j + 1)].astype(f32))
        for j in range(nsl):
            acc = slabs[j][...]
            for p in range(1, n_p):
                acc = acc + slabs[p * nsl + j][...]
            dh_ref[:, q0 + LANES * j:q0 + LANES * (j + 1)] = acc.astype(bf16)

    row = lambda c: pl.BlockSpec((T, c), lambda i: (i, 0))
    return pl.pallas_call(
        body, grid=(S // T,), name="attn_bwd_merge", out_shape=jax.ShapeDtypeStruct((S, IN_CH), bf16),
        in_specs=[row(2 * CONV_CH)] + [_perm_spec(d, QKV_CH) for d in DILATIONS] + [row(2 * GMLP_CH)],
        out_specs=row(IN_CH), scratch_shapes=_slabs(n_p * nsl, T),
        compiler_params=_cp(("parallel",), 48),
    )(d_a, *dqkvs, d_c)


def _bias_table_grad(dbias, buckets):
    n = dbias.shape[0]

    def body(db_ref, bk_ref, o_ref):
        p = pl.program_id(0)
        h = pl.program_id(1)

        @pl.when((p == 0) & (h == 0))
        def _():
            o_ref[...] = jnp.zeros_like(o_ref)
        ids = bk_ref[0]
        db = db_ref[0, 0]
        row = lax.broadcasted_iota(jnp.int32, (N_BUCKETS, 128), 0)
        lane = lax.broadcasted_iota(jnp.int32, (N_BUCKETS, 128), 1)
        upd = jnp.zeros((N_BUCKETS, 128), f32)
        for b in range(N_BUCKETS):
            s = jnp.sum(jnp.sum(jnp.where(ids == b, db, 0.0), axis=1, keepdims=True), axis=0, keepdims=True)
            upd = jnp.where((row == b) & (lane == h), s, upd)
        o_ref[...] += upd

    return pl.pallas_call(
        body, grid=(n, ATTN_HEADS), name="bias_table_grad",
        out_shape=jax.ShapeDtypeStruct((N_BUCKETS, 128), f32),
        in_specs=[pl.BlockSpec((1, 1, ATTN_BLOCK, 2 * ATTN_BLOCK), lambda p, h: (p, h, 0, 0)),
                  pl.BlockSpec((1, ATTN_BLOCK, 2 * ATTN_BLOCK), lambda p, h: (p, 0, 0))],
        out_specs=_acc((N_BUCKETS, 128)),
        compiler_params=_cp(("arbitrary", "arbitrary"), 16),
    )(dbias, buckets)


def _conv_bwd(a_in, hc, dco, dw_w, ln_g, ln_b):
    S = a_in.shape[0]
    T = 512
    hb = T // CONV_HALO
    nsteps = S // T
    R = T + CONV_HALO
    K = CONV_WIDTH

    def body(a_ref, hc_ref, hcn_ref, d_ref, dn_ref, w_ref, g_ref, be_ref,
             da_ref, dw_ref, dcb_ref, dlg_ref, dlb_ref, ext, dbuf, wacc):
        i = pl.program_id(0)

        @pl.when(i == 0)
        def _():
            wacc[...] = jnp.zeros_like(wacc)
            dcb_ref[...] = jnp.zeros_like(dcb_ref)
            dlg_ref[...] = jnp.zeros_like(dlg_ref)
            dlb_ref[...] = jnp.zeros_like(dlb_ref)
        ext[0:T, :] = hc_ref[...]
        ext[T:, :] = hcn_ref[...]
        xhat, rstd = _ln_stats(ext[...])
        hl = xhat * g_ref[...] + be_ref[...]
        ext[0:T, :] = d_ref[...]
        ext[T:, :] = dn_ref[...]
        sl_ = _sig(hl)
        dhl = ext[...] * (sl_ * (1.0 + hl * (1.0 - sl_)))
        dhc = _ln_bwd(dhl, xhat, rstd, g_ref[...])
        rowi = lax.broadcasted_iota(jnp.int32, (R, CONV_CH), 0)
        dbuf[...] = jnp.where((rowi < T) | (i < nsteps - 1), dhc, 0.0)
        dlg_ref[...] += _colsum(dhl[:T] * xhat[:T])
        dlb_ref[...] += _colsum(dhl[:T])
        dcb_ref[...] += _colsum(dbuf[pl.ds(0, T), :])
        starts = _window_rolls(range(K))
        slabs = [slice(LANES * j, LANES * (j + 1)) for j in range(CONV_CH // LANES)]

        def step(g, _):
            r0 = pl.multiple_of(g * CONV_GROUP, CONV_GROUP)
            rows = pl.ds(r0, CONV_GROUP)
            for j, cs in enumerate(slabs):
                gate_cs = slice(CONV_CH + LANES * j, CONV_CH + LANES * (j + 1))
                win = dbuf[pl.ds(r0, CONV_GROUP + CONV_HALO), cs]
                a = a_ref[rows, cs]
                sg = _sig(a_ref[rows, gate_cs])
                hg = a * sg
                dhg = jnp.zeros((CONV_GROUP, LANES), f32)
                for b, ss in starts.items():
                    rolled = win if b == 0 else pltpu.roll(win, b, 0)
                    for s in ss:
                        k = K - 1 - s
                        dk = rolled[s + b:s + b + CONV_GROUP]
                        dhg = dhg + w_ref[k:k + 1, cs] * dk
                        prod = dk * hg
                        fold = prod[0:SUBLANES]
                        for r in range(SUBLANES, CONV_GROUP, SUBLANES):
                            fold = fold + prod[r:r + SUBLANES]
                        wacc[SUBLANES * k:SUBLANES * (k + 1), cs] += fold
                da_ref[rows, cs] = (dhg * sg).astype(bf16)
                da_ref[rows, gate_cs] = (dhg * hg * (1.0 - sg)).astype(bf16)
            return 0

        lax.fori_loop(0, T // CONV_GROUP, step, 0)

        @pl.when(i == nsteps - 1)
        def _():
            for k in range(K):
                dw_ref[k:k + 1, :] = _colsum(wacc[SUBLANES * k:SUBLANES * (k + 1), :])
            dw_ref[K:, :] = jnp.zeros((32 - K, CONV_CH), f32)

    vec = jax.ShapeDtypeStruct((1, CONV_CH), f32)
    nxt = lambda i: (jnp.minimum((i + 1) * hb, nsteps * hb - 1), 0)
    return pl.pallas_call(
        body, grid=(nsteps,), name="conv_bwd",
        out_shape=(jax.ShapeDtypeStruct((S, 2 * CONV_CH), bf16), jax.ShapeDtypeStruct((32, CONV_CH), f32), vec, vec, vec),
        in_specs=[pl.BlockSpec((T, 2 * CONV_CH), lambda i: (i, 0)),
                  pl.BlockSpec((T, CONV_CH), lambda i: (i, 0)), pl.BlockSpec((CONV_HALO, CONV_CH), nxt),
                  pl.BlockSpec((T, CONV_CH), lambda i: (i, 0)), pl.BlockSpec((CONV_HALO, CONV_CH), nxt),
                  _acc((32, CONV_CH)), _acc((1, CONV_CH)), _acc((1, CONV_CH))],
        out_specs=(pl.BlockSpec((T, 2 * CONV_CH), lambda i: (i, 0)), _acc((32, CONV_CH)), _acc((1, CONV_CH)),
                   _acc((1, CONV_CH)), _acc((1, CONV_CH))),
        scratch_shapes=[pltpu.VMEM((R, CONV_CH), f32), pltpu.VMEM((R, CONV_CH), f32),
                        pltpu.VMEM((SUBLANES * 32, CONV_CH), f32)],
        compiler_params=_cp(("arbitrary",), 32),
    )(a_in, hc, hc, dco, dco, dw_w, ln_g, ln_b)


def _adamw(g, w, m, v, name):
    R, C = g.shape
    T = R
    for cand in (512, 256, 128, 64, 32, 16, 8):
        if R % cand == 0 and cand * C * 4 <= MIB:
            T = cand
            break
    c1 = 1.0 / (1.0 - ADAM_B1 ** ADAM_STEP)
    c2 = 1.0 / (1.0 - ADAM_B2 ** ADAM_STEP)

    def body(g_ref, w_ref, m_ref, v_ref, d_ref, nm_ref, nv_ref):
        gg = g_ref[...]
        nm = ADAM_B1 * m_ref[...] + (1.0 - ADAM_B1) * gg
        nv = ADAM_B2 * v_ref[...] + (1.0 - ADAM_B2) * (gg * gg)
        nm_ref[...] = nm
        nv_ref[...] = nv
        d_ref[...] = -ADAM_LR * ((nm * c1) / (jnp.sqrt(nv * c2) + ADAM_EPS) + ADAM_WD * w_ref[...])

    blk = pl.BlockSpec((T, C), lambda i: (i, 0))
    sd = jax.ShapeDtypeStruct((R, C), f32)
    return pl.pallas_call(
        body, grid=(R // T,), name=name, out_shape=(sd, sd, sd), in_specs=[blk] * 4, out_specs=(blk, blk, blk),
        compiler_params=_cp(("parallel",), 48),
    )(g, w, m, v)


def _pad_rows(a, rows):
    return jnp.pad(a, ((0, rows - a.shape[0]), (0, 0)))


def _local_step(x, target, wb, late_weights, sp, sink):
    buckets = jnp.asarray(_bucket_ids())
    bias = _bias_build(sp["rel_bias_table"], buckets)
    wb = dict(wb)
    saved = []
    xl = x
    for l in range(DEPTH):
        vec = lambda name: sp[name][l][None, :]
        a_in, c_in, *qkv = _inproj_fwd(xl, wb["w_in"][l], vec("b_in"))
        conv_w = _pad_rows(sp["conv_dw_w"][l], 32)
        conv_out, hc = _conv_fwd(a_in, conv_w, vec("conv_dw_b"), vec("conv_ln_g"), vec("conv_ln_b"))
        attn_out, lse = _attn_fwd(qkv, bias)
        bs_t = sp["gmlp_b_s"][l].T
        gm_out = _gmlp_fwd(c_in, vec("gmlp_ln_g"), vec("gmlp_ln_b"), sp["gmlp_w_s"][l], bs_t)
        if l == 0:
            wb.update(late_weights(gm_out))
        cat, z1, x1b = _outproj_ln_fwd(conv_out, attn_out, gm_out, wb["w_out"][l], vec("b_out"), xl,
                                           vec("ln1_g"), vec("ln1_b"))
        fconv_w = _pad_rows(sp["ffn_conv_w"][l], 8)
        hfb, fhc, act = _ffn_up_gate_fwd(x1b, wb["ffn_w_up"][l], vec("ffn_b_up"), fconv_w, vec("ffn_conv_b"))
        down = (act, wb["ffn_w_down"][l], vec("ffn_b_down"), z1, vec("ln1_g"), vec("ln1_b"), vec("ln2_g"),
                vec("ln2_b"))
        z2, x2 = _ffn_down_ln_fwd(*down) if l < DEPTH - 1 else (None, None)
        saved.append(dict(x=xl, a_in=a_in, qkv=qkv, c_in=c_in, hc=hc, attn_out=attn_out, lse=lse, cat=cat, z1=z1,
                          x1b=x1b, hfb=hfb, fhc=fhc, act=act, z2=z2, conv_w=conv_w, fconv_w=fconv_w, bs_t=bs_t))
        xl = x2

    grads = {}
    per_layer = {k: [None] * DEPTH for k in (
        "b_in", "conv_dw_w", "conv_dw_b", "conv_ln_g", "conv_ln_b", "gmlp_ln_g", "gmlp_ln_b", "gmlp_w_s",
        "gmlp_b_s", "b_out", "ln1_g", "ln1_b", "ffn_b_up", "ffn_conv_w", "ffn_conv_b", "ffn_b_down", "ln2_g", "ln2_b")}
    dbias_all = []
    dz2, dz2b, loss_part, dg2, db2 = _ffn_down_ln_loss(*down, target)
    loss = jnp.sum(loss_part)
    grad_x = None
    tok = jnp.zeros((), f32)
    for l in reversed(range(DEPTH)):
        sv = saved[l]
        vec = lambda name: sp[name][l][None, :] + tok
        per_layer["ln2_g"][l] = dg2[0]
        per_layer["ln2_b"][l] = db2[0]
        dw_down, db_down = _wgrad(sv["act"], dz2b, 512, "ffn_down_wgrad")
        tok = sink.put("ffn_w_down", l, dw_down, tok)
        per_layer["ffn_b_down"][l] = db_down[0]
        dhf, dfcw, dfcb = _ffn_down_gate_bwd(dz2b, wb["ffn_w_down"][l], sv["hfb"], sv["fhc"], sv["fconv_w"])
        per_layer["ffn_conv_w"][l] = dfcw[:FFN_CONV_WIDTH]
        per_layer["ffn_conv_b"][l] = dfcb[0]
        dw_up, db_up = _wgrad(sv["x1b"], dhf, 1408, "ffn_up_wgrad")
        tok = sink.put("ffn_w_up", l, dw_up, tok)
        per_layer["ffn_b_up"][l] = db_up[0]
        dz1, dz1b, dg1, db1 = _dgrad_ln_bwd(dhf, wb["ffn_w_up"][l], dz2, sv["z1"], vec("ln1_g"), "ffn_up_dgrad_ln")
        per_layer["ln1_g"][l] = dg1[0]
        per_layer["ln1_b"][l] = db1[0]
        dw_out, db_out = _wgrad(sv["cat"], dz1b, D_MODEL, "outproj_wgrad")
        tok = sink.put("w_out", l, dw_out, tok)
        per_layer["b_out"][l] = db_out[0]
        dco, dgo, *perm = _outproj_dgrad(dz1b, wb["w_out"][l], sv["attn_out"], sv["lse"])
        d_outs, stats = perm[:len(DILATIONS)], perm[len(DILATIONS):]
        d_c, dglg, dglb, dws, dbs = _gmlp_bwd(sv["c_in"], dgo, vec("gmlp_ln_g"), vec("gmlp_ln_b"), sp["gmlp_w_s"][l],
                                              sv["bs_t"])
        per_layer["gmlp_ln_g"][l] = dglg[0]
        per_layer["gmlp_ln_b"][l] = dglb[0]
        per_layer["gmlp_w_s"][l] = dws
        per_layer["gmlp_b_s"][l] = dbs[:, :GMLP_GROUPS].T
        dqkvs = []
        for p, d in enumerate(DILATIONS):
            dqkv, dbias = _attn_bwd_pattern(sv["qkv"][p], d_outs[p], stats[p], bias[p], d)
            dqkvs.append(dqkv)
            dbias_all.append(dbias)
        d_a, dcw, dcb, dclg, dclb = _conv_bwd(sv["a_in"], sv["hc"], dco, sv["conv_w"], vec("conv_ln_g"),
                                              vec("conv_ln_b"))
        per_layer["conv_dw_w"][l] = dcw[:CONV_WIDTH]
        per_layer["conv_dw_b"][l] = dcb[0]
        per_layer["conv_ln_g"][l] = dclg[0]
        per_layer["conv_ln_b"][l] = dclb[0]
        dh = _attn_bwd_merge(d_a, dqkvs, d_c)
        dw_in, db_in = _wgrad(sv["x"], dh, IN_CH, "inproj_wgrad")
        tok = sink.put("w_in", l, dw_in, tok)
        per_layer["b_in"][l] = db_in[0]
        if l > 0:
            pv = saved[l - 1]
            dz2, dz2b, dg2, db2 = _dgrad_ln_bwd(dh, wb["w_in"][l], dz1, pv["z2"], sp["ln2_g"][l - 1][None, :] + tok,
                                                "inproj_dgrad_ln")
        else:
            grad_x = _dgrad_ln_bwd(dh, wb["w_in"][l], dz1, None, None, "inproj_dgrad")
    for k, v in per_layer.items():
        grads[k] = jnp.stack(v)
    dbias_cat = jnp.stack(dbias_all)
    bk_cat = jnp.concatenate([buckets] * DEPTH, axis=0)
    grads["rel_bias_table"] = _bias_table_grad(dbias_cat, bk_cat)[:, :ATTN_HEADS]
    return loss, grad_x, grads, sink.finish(grad_x)


N_CHIPS = 4
BIG = {"w_in": (D_MODEL, IN_CH, 1), "w_out": (D_MODEL, D_MODEL, 0),
       "ffn_w_up": (D_MODEL, 2 * D_FF, 1), "ffn_w_down": (D_FF, D_MODEL, 0)}
SMALL = ("b_in", "conv_dw_w", "conv_dw_b", "conv_ln_g", "conv_ln_b", "rel_bias_table", "gmlp_ln_g", "gmlp_ln_b",
         "gmlp_w_s", "gmlp_b_s", "b_out", "ln1_g", "ln1_b", "ffn_b_up", "ffn_conv_w", "ffn_conv_b", "ffn_b_down",
         "ln2_g", "ln2_b")
SMALL_SHARDED = ("conv_dw_w", "ffn_conv_w")
WEIGHTS = ("w_in", "b_in", "conv_dw_w", "conv_dw_b", "conv_ln_g", "conv_ln_b", "rel_bias_table", "gmlp_ln_g",
           "gmlp_ln_b", "gmlp_w_s", "gmlp_b_s", "w_out", "b_out", "ln1_g", "ln1_b", "ffn_w_up", "ffn_b_up",
           "ffn_conv_w", "ffn_conv_b", "ffn_w_down", "ffn_b_down", "ln2_g", "ln2_b")
ANY = pl.BlockSpec(memory_space=pl.ANY)


def _position():
    return lax.axis_index("x"), lax.axis_index("y"), lax.axis_index("c")


def _other_chips(x, y):
    return [(1 - x, y), (x, 1 - y), (1 - x, 1 - y)]


def _cast_bf16(a):
    R, C = a.shape
    T = 128

    def body(a_ref, o_ref):
        o_ref[...] = a_ref[...].astype(bf16)

    return pl.pallas_call(
        body, grid=(R // T,), name="cast_bf16", out_shape=jax.ShapeDtypeStruct((R, C), bf16),
        in_specs=[pl.BlockSpec((T, C), lambda i: (i, 0))], out_specs=pl.BlockSpec((T, C), lambda i: (i, 0)),
        compiler_params=_cp(("parallel",), 16),
    )(a)


def _chip_slot(ref, name, l, p):
    K, N, ax = BIG[name]
    if ax == 1:
        sz = N // N_CHIPS
        return ref.at[l, :, pl.ds(pl.multiple_of(p * sz, 128), sz)]
    sz = K // N_CHIPS
    return ref.at[l, pl.ds(pl.multiple_of(p * sz, 16), sz), :]


def _gather_weights(shards, conv_w, fconv_w):
    names = list(shards)
    n_big = len(names)
    n_t = n_big + 2
    n_chip = 3 * n_t
    n_pass = 3 * n_big

    def body(*refs):
        ins = refs[:n_t]
        outs = refs[n_t:2 * n_t]
        send_sems, recv_sems, pass_send, pass_recv, local_sems = refs[2 * n_t:]
        x, y, c = _position()
        me = 2 * x + y
        chips = _other_chips(x, y)

        def src(t):
            return ins[t].at[c] if t < n_big else ins[t]

        def slot(t, l, p):
            return _chip_slot(outs[t], names[t], l, p) if t < n_big else outs[t].at[p]

        locs, cps = [], []
        for t in range(n_t):
            for l in (range(DEPTH) if t < n_big else (0,)):
                loc = pltpu.make_async_copy(ins[t].at[l] if t < n_big else ins[t], slot(t, l, me),
                                            local_sems.at[DEPTH * t + l])
                loc.start()
                locs.append(loc)
            for k, (px, py) in enumerate(chips):
                cp = pltpu.make_async_remote_copy(
                    src_ref=src(t), dst_ref=slot(t, c, me), send_sem=send_sems.at[3 * t + k],
                    recv_sem=recv_sems.at[3 * t + k], device_id=(px, py, c), device_id_type=MESH_ID)
                cp.start()
                cps.append(cp)
        for t in range(n_t):
            for k, (px, py) in enumerate(chips):
                landed = slot(t, c, 2 * px + py)
                pltpu.make_async_remote_copy(
                    src_ref=src(t), dst_ref=landed, send_sem=send_sems.at[3 * t + k],
                    recv_sem=recv_sems.at[3 * t + k], device_id=(px, py, c), device_id_type=MESH_ID).wait_recv()
                if t < n_big:
                    cp = pltpu.make_async_remote_copy(
                        src_ref=landed, dst_ref=landed, send_sem=pass_send.at[3 * t + k],
                        recv_sem=pass_recv.at[3 * t + k], device_id=(x, y, 1 - c), device_id_type=MESH_ID)
                    cp.start()
                    cps.append(cp)
        for t in range(n_big):
            for k, (px, py) in enumerate(chips):
                from_sibling = slot(t, 1 - c, 2 * px + py)
                pltpu.make_async_remote_copy(
                    src_ref=from_sibling, dst_ref=from_sibling, send_sem=pass_send.at[3 * t + k],
                    recv_sem=pass_recv.at[3 * t + k], device_id=(x, y, 1 - c), device_id_type=MESH_ID).wait_recv()
        for cp in cps:
            cp.wait_send()
        for loc in locs:
            loc.wait()

    ins = [shards[n] for n in names] + [conv_w, fconv_w]
    out_shape = [jax.ShapeDtypeStruct((DEPTH, BIG[n][0], BIG[n][1]), bf16) for n in names]
    out_shape += [jax.ShapeDtypeStruct((N_CHIPS,) + conv_w.shape, f32), jax.ShapeDtypeStruct((N_CHIPS,) + fconv_w.shape, f32)]
    outs = pl.pallas_call(
        body, name="gather_weights", out_shape=tuple(out_shape), in_specs=[ANY] * n_t, out_specs=tuple([ANY] * n_t),
        scratch_shapes=[pltpu.SemaphoreType.DMA((n_chip,)), pltpu.SemaphoreType.DMA((n_chip,)),
                        pltpu.SemaphoreType.DMA((n_pass,)), pltpu.SemaphoreType.DMA((n_pass,)),
                        pltpu.SemaphoreType.DMA((DEPTH * n_t,))],
    )(*ins)
    return dict(zip(names, outs[:n_big])), outs[-2], outs[-1]


LATE_WEIGHTS = ("w_out", "ffn_w_up", "ffn_w_down")
HBM = pl.BlockSpec(memory_space=pltpu.HBM)
SEM = pl.BlockSpec(memory_space=pltpu.SEMAPHORE)


def _cast_into_full(shard, name, chip_arr):
    K, N, ax = BIG[name]
    k, n = _shard_shape(name)
    T = 64
    nrt = k // T

    def body(p_ref, a_ref, o_ref):
        o_ref[...] = a_ref[...].astype(bf16)

    if ax == 1:
        out_spec = pl.BlockSpec((None, T, n), lambda l, i, p: (l, i, p[0]))
    else:
        out_spec = pl.BlockSpec((None, T, n), lambda l, i, p: (l, p[0] * nrt + i, 0))
    return pl.pallas_call(
        body, name="cast_into_full", out_shape=jax.ShapeDtypeStruct((DEPTH, K, N), bf16),
        grid_spec=pltpu.PrefetchScalarGridSpec(
            num_scalar_prefetch=1, grid=(DEPTH, nrt),
            in_specs=[pl.BlockSpec((None, T, n), lambda l, i, p: (l, i, 0))], out_specs=out_spec),
        compiler_params=_cp(("parallel", "parallel"), 16),
    )(chip_arr, shard)


def _late_copies(refs, send_sems, recv_sems):
    x, y, c = _position()
    me = 2 * x + y
    idx = 0
    for ref, name in zip(refs, LATE_WEIGHTS):
        for l in range(DEPTH):
            for px, py in _other_chips(x, y):
                def copy(p, ref=ref, name=name, l=l, px=px, py=py, idx=idx):
                    part = _chip_slot(ref, name, l, p)
                    return pltpu.make_async_remote_copy(
                        src_ref=part, dst_ref=part, send_sem=send_sems.at[idx], recv_sem=recv_sems.at[idx],
                        device_id=(px, py, c), device_id_type=MESH_ID)
                yield copy(me), copy(2 * px + py)
                idx += 1


N_LATE_COPIES = 3 * DEPTH * len(LATE_WEIGHTS)


def _gather_start(fulls, after):
    n = len(fulls)

    def body(*refs):
        ins = refs[:n]
        send_sems, recv_sems = refs[n + 1:n + 3]
        token = refs[-1]
        for sent, _ in _late_copies(ins, send_sems, recv_sems):
            sent.start()
        token[...] = jnp.zeros_like(token)

    outs = pl.pallas_call(
        body, name="gather_start",
        out_shape=(pltpu.SemaphoreType.DMA((N_LATE_COPIES,)), pltpu.SemaphoreType.DMA((N_LATE_COPIES,)))
        + tuple(pltpu.HBM(f.shape, f.dtype) for f in fulls) + (jax.ShapeDtypeStruct((SUBLANES, LANES), f32),),
        in_specs=(HBM,) * n + (ANY,),
        out_specs=(SEM, SEM) + (HBM,) * n + (pl.BlockSpec(memory_space=pltpu.VMEM),),
        input_output_aliases={t: 2 + t for t in range(n)},
        compiler_params=pltpu.CompilerParams(has_side_effects=pltpu.SideEffectType.DATAFLOW_SIDE_EFFECTING),
    )(*[pltpu.with_memory_space_constraint(f, pltpu.HBM) for f in fulls], after)
    return outs[0], outs[1], outs[2:2 + n], outs[-1]


def _gather_wait(send_sems, recv_sems, fulls, after):
    n = len(fulls)

    def body(*refs):
        ins = refs[:n]
        send_ref, recv_ref = refs[n:n + 2]
        for sent, landed in _late_copies(ins, send_ref, recv_ref):
            sent.wait_send()
            landed.wait_recv()

    return pl.pallas_call(
        body, name="gather_wait", out_shape=tuple(pltpu.HBM(f.shape, f.dtype) for f in fulls),
        in_specs=(HBM,) * n + (SEM, SEM, ANY), out_specs=(HBM,) * n,
        input_output_aliases={t: t for t in range(n)},
        compiler_params=pltpu.CompilerParams(has_side_effects=pltpu.SideEffectType.DATAFLOW_SIDE_EFFECTING),
    )(*fulls, send_sems, recv_sems, after)


def _half(ref, name, c):
    K, N, ax = BIG[name]
    if ax == 1:
        return ref.at[pl.ds(pl.multiple_of(c * (K // 2), 8), K // 2), :]
    return ref.at[:, pl.ds(pl.multiple_of(c * (N // 2), 128), N // 2)]


def _half_shape(name):
    K, N, ax = BIG[name]
    return (K // 2, N) if ax == 1 else (K, N // 2)


def _shard_of_half(ref, name, q):
    K, N, ax = BIG[name]
    if ax == 1:
        sz = N // N_CHIPS
        return ref.at[:, pl.ds(pl.multiple_of(q * sz, 128), sz)]
    sz = K // N_CHIPS
    return ref.at[pl.ds(pl.multiple_of(q * sz, 16), sz), :]


def _shard_half_shape(name):
    K, N, ax = BIG[name]
    return (K // 2, N // N_CHIPS) if ax == 1 else (K // N_CHIPS, N // 2)


def _shard_shape(name):
    K, N, ax = BIG[name]
    return (K, N // N_CHIPS) if ax == 1 else (K // N_CHIPS, N)


def _pair_copies(names, srcs, lands, send_sems, recv_sems):
    x, y, c = _position()
    for idx, (name, src, land) in enumerate(zip(names, srcs, lands)):
        yield pltpu.make_async_remote_copy(
            src_ref=_half(src, name, 1 - c), dst_ref=land, send_sem=send_sems.at[idx], recv_sem=recv_sems.at[idx],
            device_id=(x, y, 1 - c), device_id_type=MESH_ID)


def _pair_exchange_start(tag, tensors):
    names = [n for n, _ in tensors]
    n = len(tensors)
    lands = [lax.empty(_half_shape(nm), f32) for nm in names]

    def body(*refs):
        for cp in _pair_copies(names, refs[:n], refs[n:2 * n], refs[2 * n], refs[2 * n + 1]):
            cp.start()
        refs[-1][...] = jnp.zeros_like(refs[-1])

    args = [g for _, g in tensors] + lands
    outs = pl.pallas_call(
        body, name="grad_pair_start_" + tag,
        out_shape=(pltpu.SemaphoreType.DMA((n,)), pltpu.SemaphoreType.DMA((n,)))
        + tuple(pltpu.HBM(a.shape, a.dtype) for a in args) + (jax.ShapeDtypeStruct((SUBLANES, LANES), f32),),
        in_specs=(HBM,) * (2 * n), out_specs=(SEM, SEM) + (HBM,) * (2 * n) + (pl.BlockSpec(memory_space=pltpu.VMEM),),
        input_output_aliases={t: 2 + t for t in range(2 * n)},
        compiler_params=pltpu.CompilerParams(has_side_effects=pltpu.SideEffectType.DATAFLOW_SIDE_EFFECTING),
    )(*[pltpu.with_memory_space_constraint(a, pltpu.HBM) for a in args])
    return (tag, names, outs[0], outs[1], outs[2:2 + 2 * n]), outs[-1]


def _pair_exchange_wait(state, after):
    tag, names, send_sems, recv_sems, bufs = state
    n = len(names)

    def body(*refs):
        for cp in _pair_copies(names, refs[:n], refs[n:2 * n], refs[2 * n], refs[2 * n + 1]):
            cp.wait_send()
            cp.wait_recv()

    outs = pl.pallas_call(
        body, name="grad_pair_wait_" + tag, out_shape=tuple(pltpu.HBM(a.shape, a.dtype) for a in bufs),
        in_specs=(HBM,) * (2 * n) + (SEM, SEM, ANY), out_specs=(HBM,) * (2 * n),
        input_output_aliases={t: t for t in range(2 * n)},
        compiler_params=pltpu.CompilerParams(has_side_effects=pltpu.SideEffectType.DATAFLOW_SIDE_EFFECTING),
    )(*bufs, send_sems, recv_sems, after)
    return list(zip(names, outs[:n], outs[n:]))


def _pair_add(g, rcv, name, c_arr):
    K, N, ax = BIG[name]
    hr, hc = _half_shape(name)
    T = 128
    nrt = hr // T

    def body(c_ref, g_ref, r_ref, o_ref):
        o_ref[...] = (g_ref[...] + r_ref[...]).astype(bf16)

    if ax == 1:
        g_spec = pl.BlockSpec((T, hc), lambda i, c: (c[0] * nrt + i, 0))
    else:
        g_spec = pl.BlockSpec((T, hc), lambda i, c: (i, c[0]))
    plain = pl.BlockSpec((T, hc), lambda i, c: (i, 0))
    return pl.pallas_call(
        body, name="grad_pair_add", out_shape=jax.ShapeDtypeStruct((hr, hc), bf16),
        grid_spec=pltpu.PrefetchScalarGridSpec(num_scalar_prefetch=1, grid=(nrt,), in_specs=[g_spec, plain],
                                               out_specs=plain),
        compiler_params=_cp(("parallel",), 32),
    )(c_arr, g, rcv)


def _chip_copies(names, srcs, lands, send_sems, recv_sems):
    x, y, c = _position()
    me = 2 * x + y
    idx = 0
    for name, src, land in zip(names, srcs, lands):
        for px, py in _other_chips(x, y):
            def copy(q, row, name=name, src=src, land=land, px=px, py=py, idx=idx):
                return pltpu.make_async_remote_copy(
                    src_ref=_shard_of_half(src, name, q), dst_ref=land.at[row], send_sem=send_sems.at[idx],
                    recv_sem=recv_sems.at[idx], device_id=(px, py, c), device_id_type=MESH_ID)
            yield copy(2 * px + py, me), copy(me, 2 * px + py)
            idx += 1


def _chip_exchange_start(tag, tensors):
    names = [n for n, _ in tensors]
    n = len(tensors)
    lands = [lax.empty((N_CHIPS,) + _shard_half_shape(nm), g.dtype) for nm, g in tensors]

    def body(*refs):
        send_sems, recv_sems = refs[2 * n:2 * n + 2]
        for sent, _ in _chip_copies(names, refs[:n], refs[n:2 * n], send_sems, recv_sems):
            sent.start()
        refs[-1][...] = jnp.zeros_like(refs[-1])

    args = [g for _, g in tensors] + lands
    outs = pl.pallas_call(
        body, name="grad_chip_start_" + tag,
        out_shape=(pltpu.SemaphoreType.DMA((3 * n,)), pltpu.SemaphoreType.DMA((3 * n,)))
        + tuple(pltpu.HBM(a.shape, a.dtype) for a in args) + (jax.ShapeDtypeStruct((SUBLANES, LANES), f32),),
        in_specs=(HBM,) * (2 * n), out_specs=(SEM, SEM) + (HBM,) * (2 * n) + (pl.BlockSpec(memory_space=pltpu.VMEM),),
        input_output_aliases={t: 2 + t for t in range(2 * n)},
        compiler_params=pltpu.CompilerParams(has_side_effects=pltpu.SideEffectType.DATAFLOW_SIDE_EFFECTING),
    )(*[pltpu.with_memory_space_constraint(a, pltpu.HBM) for a in args])
    return (tag, names, outs[0], outs[1], outs[2:2 + 2 * n]), outs[-1]


def _chip_exchange_wait(state, after):
    tag, names, send_sems, recv_sems, bufs = state
    n = len(names)

    def body(*refs):
        for sent, landed in _chip_copies(names, refs[:n], refs[n:2 * n], refs[2 * n], refs[2 * n + 1]):
            sent.wait_send()
            landed.wait_recv()

    outs = pl.pallas_call(
        body, name="grad_chip_wait_" + tag, out_shape=tuple(pltpu.HBM(a.shape, a.dtype) for a in bufs),
        in_specs=(HBM,) * (2 * n) + (SEM, SEM, ANY), out_specs=(HBM,) * (2 * n),
        input_output_aliases={t: t for t in range(2 * n)},
        compiler_params=pltpu.CompilerParams(has_side_effects=pltpu.SideEffectType.DATAFLOW_SIDE_EFFECTING),
    )(*bufs, send_sems, recv_sems, after)
    return list(zip(names, outs[:n], outs[n:]))


def _sum_chips(name, half, land, chip_arr):
    K, N, ax = BIG[name]
    R, C = _shard_half_shape(name)
    T = 64
    nrt = R // T

    def body(p_ref, own_ref, land_ref, o_ref):
        parts = [jnp.where(p_ref[0] == q, own_ref[...], land_ref[q]).astype(f32) for q in range(N_CHIPS)]
        o_ref[...] = ((parts[0] + parts[1]) + parts[2]) + parts[3]

    if ax == 1:
        own_spec = pl.BlockSpec((T, C), lambda i, p: (i, p[0]))
    else:
        own_spec = pl.BlockSpec((T, C), lambda i, p: (p[0] * nrt + i, 0))
    return pl.pallas_call(
        body, name="grad_sum_chips", out_shape=jax.ShapeDtypeStruct((R, C), f32),
        grid_spec=pltpu.PrefetchScalarGridSpec(
            num_scalar_prefetch=1, grid=(nrt,),
            in_specs=[own_spec, pl.BlockSpec((N_CHIPS, T, C), lambda i, p: (0, i, 0))],
            out_specs=pl.BlockSpec((T, C), lambda i, p: (i, 0))),
        compiler_params=_cp(("parallel",), 32),
    )(chip_arr, half, land)


def _pair_swap(halves):
    n_t = len(halves)

    def body(*refs):
        ins = refs[:n_t]
        outs = refs[n_t:2 * n_t]
        send_sems, recv_sems = refs[2 * n_t:]
        x, y, c = _position()
        cps = []
        for t in range(n_t):
            cp = pltpu.make_async_remote_copy(
                src_ref=ins[t], dst_ref=outs[t], send_sem=send_sems.at[t], recv_sem=recv_sems.at[t],
                device_id=(x, y, 1 - c), device_id_type=MESH_ID)
            cp.start()
            cps.append(cp)
        for cp in cps:
            cp.wait()

    return pl.pallas_call(
        body, name="grad_pair_swap", out_shape=tuple(jax.ShapeDtypeStruct(h.shape, h.dtype) for h in halves),
        in_specs=[ANY] * n_t, out_specs=tuple([ANY] * n_t),
        scratch_shapes=[pltpu.SemaphoreType.DMA((n_t,)), pltpu.SemaphoreType.DMA((n_t,))],
    )(*halves)


def _adamw_halves(own, other, w, m, v, name, l, c_arr, prev):
    K, N, ax = BIG[name]
    R, C = _shard_shape(name)
    hr, hc = _shard_half_shape(name)
    T = 64
    nrt = hr // T
    c1 = 1.0 / (1.0 - ADAM_B1 ** ADAM_STEP)
    c2 = 1.0 / (1.0 - ADAM_B2 ** ADAM_STEP)

    def body(c_ref, own_ref, oth_ref, w_ref, m_ref, v_ref, *rest):
        g_ref, d_ref, nm_ref, nv_ref = rest[-4:]
        gg = jnp.where(pl.program_id(0) == c_ref[0], own_ref[...], oth_ref[...])
        nm = ADAM_B1 * m_ref[...] + (1.0 - ADAM_B1) * gg
        nv = ADAM_B2 * v_ref[...] + (1.0 - ADAM_B2) * (gg * gg)
        g_ref[...] = gg
        nm_ref[...] = nm
        nv_ref[...] = nv
        d_ref[...] = -ADAM_LR * ((nm * c1) / (jnp.sqrt(nv * c2) + ADAM_EPS) + ADAM_WD * w_ref[...])

    half = pl.BlockSpec((T, hc), lambda h, i, c: (i, 0))
    if ax == 1:
        full = pl.BlockSpec((None, T, hc), lambda h, i, c: (l, h * nrt + i, 0))
    else:
        full = pl.BlockSpec((None, T, hc), lambda h, i, c: (l, i, h))
    sd = jax.ShapeDtypeStruct((DEPTH, R, C), f32)
    args = [c_arr, own, other, w, m, v]
    in_specs = [half, half, full, full, full]
    aliases = {}
    if prev is not None:
        args += list(prev)
        in_specs += [ANY] * 4
        aliases = {6 + k: k for k in range(4)}
    return pl.pallas_call(
        body, name="adamw_" + name, out_shape=(sd, sd, sd, sd),
        grid_spec=pltpu.PrefetchScalarGridSpec(num_scalar_prefetch=1, grid=(2, nrt), in_specs=in_specs,
                                               out_specs=(full, full, full, full)),
        input_output_aliases=aliases,
        compiler_params=_cp(("arbitrary", "arbitrary"), 32),
    )(*args)


class _GradExchange:
    GROUPS = (("l1", tuple((n, DEPTH - 1) for n in BIG)),
              ("l0_ffn", (("ffn_w_down", 0), ("ffn_w_up", 0))),
              ("l0_mix", (("w_out", 0), ("w_in", 0))))

    def __init__(self):
        self.c_arr = jnp.reshape(lax.axis_index("c"), (1,)).astype(jnp.int32)
        self.chip_arr = jnp.reshape(2 * lax.axis_index("x") + lax.axis_index("y"), (1,)).astype(jnp.int32)
        self.grads = {}
        self.pair_started = {}
        self.chip_started = {}

    def _advance(self, after, tok):
        for tag, _ in self.GROUPS:
            if tag not in self.pair_started or tag in self.chip_started:
                continue
            arrived = _pair_exchange_wait(self.pair_started[tag], after)
            pair = [(n, _pair_add(g, r, n, self.c_arr)) for n, g, r in arrived]
            self.chip_started[tag], token = _chip_exchange_start(tag, pair)
            tok = tok + token[0, 0]
        return tok

    def put(self, name, layer, g, tok):
        self.grads[(name, layer)] = g
        tok = self._advance(g, tok)
        for tag, keys in self.GROUPS:
            if tag in self.pair_started or not all(k in self.grads for k in keys):
                continue
            self.pair_started[tag], token = _pair_exchange_start(tag, [(n, self.grads[(n, l)]) for n, l in keys])
            tok = tok + token[0, 0]
        return tok

    def finish(self, after):
        self._advance(after, jnp.zeros((), f32))
        keys, own = [], []
        for tag, group in self.GROUPS:
            landed = _chip_exchange_wait(self.chip_started[tag], after)
            own += [_sum_chips(n, half, land, self.chip_arr) for n, half, land in landed]
            keys += list(group)
        other = _pair_swap(own)
        return dict(zip(keys, zip(own, other)))


def _small_allreduce(buf):
    R = buf.shape[0]

    def body(in_ref, out_ref, sibling, slots, send_sems, recv_sems):
        x, y, c = _position()
        me = 2 * x + y
        swap = pltpu.make_async_remote_copy(
            src_ref=in_ref, dst_ref=sibling, send_sem=send_sems.at[0], recv_sem=recv_sems.at[0],
            device_id=(x, y, 1 - c), device_id_type=MESH_ID)
        swap.start()
        swap.wait()
        slots[me] = in_ref[...] + sibling[...]
        cps = []
        for k, (px, py) in enumerate(_other_chips(x, y)):
            cp = pltpu.make_async_remote_copy(
                src_ref=slots.at[me], dst_ref=slots.at[me], send_sem=send_sems.at[1 + k], recv_sem=recv_sems.at[1 + k],
                device_id=(px, py, c), device_id_type=MESH_ID)
            cp.start()
            cps.append(cp)
        for k, (px, py) in enumerate(_other_chips(x, y)):
            pltpu.make_async_remote_copy(
                src_ref=slots.at[me], dst_ref=slots.at[2 * px + py], send_sem=send_sems.at[1 + k],
                recv_sem=recv_sems.at[1 + k], device_id=(px, py, c), device_id_type=MESH_ID).wait_recv()
        for cp in cps:
            cp.wait_send()
        out_ref[...] = ((slots[0] + slots[1]) + slots[2]) + slots[3]

    vm = pl.BlockSpec(memory_space=pltpu.VMEM)
    return pl.pallas_call(
        body, name="small_allreduce", out_shape=jax.ShapeDtypeStruct((R, 128), f32), in_specs=[vm], out_specs=vm,
        scratch_shapes=[pltpu.VMEM((R, 128), f32), pltpu.VMEM((N_CHIPS, R, 128), f32),
                        pltpu.SemaphoreType.DMA((N_CHIPS,)), pltpu.SemaphoreType.DMA((N_CHIPS,))],
        compiler_params=pltpu.CompilerParams(vmem_limit_bytes=40 * MIB),
    )(buf)


PACK_UNIT = 1024


def _pack(arrs):
    parts = []
    for a in arrs:
        flat = a.reshape(-1)
        n = -(-flat.shape[0] // PACK_UNIT) * PACK_UNIT
        parts.append(jnp.pad(flat, (0, n - flat.shape[0])))
    return jnp.concatenate(parts).reshape(-1, 128)


def _unpack(buf, shapes):
    flat = buf.reshape(-1)
    out, off = [], 0
    for shp in shapes:
        n = int(np.prod(shp))
        out.append(flat[off:off + n].reshape(shp))
        off += -(-n // PACK_UNIT) * PACK_UNIT
    return out


def kernel(x, w_in, b_in, conv_dw_w, conv_dw_b, conv_ln_g, conv_ln_b, rel_bias_table, gmlp_ln_g, gmlp_ln_b, gmlp_w_s, gmlp_b_s, w_out, b_out, ln1_g, ln1_b, ffn_w_up, ffn_b_up, ffn_conv_w, ffn_conv_b, ffn_w_down, ffn_b_down, ln2_g, ln2_b, loss_target, m_w_in, m_b_in, m_conv_dw_w, m_conv_dw_b, m_conv_ln_g, m_conv_ln_b, m_rel_bias_table, m_gmlp_ln_g, m_gmlp_ln_b, m_gmlp_w_s, m_gmlp_b_s, m_w_out, m_b_out, m_ln1_g, m_ln1_b, m_ffn_w_up, m_ffn_b_up, m_ffn_conv_w, m_ffn_conv_b, m_ffn_w_down, m_ffn_b_down, m_ln2_g, m_ln2_b, v_w_in, v_b_in, v_conv_dw_w, v_conv_dw_b, v_conv_ln_g, v_conv_ln_b, v_rel_bias_table, v_gmlp_ln_g, v_gmlp_ln_b, v_gmlp_w_s, v_gmlp_b_s, v_w_out, v_b_out, v_ln1_g, v_ln1_b, v_ffn_w_up, v_ffn_b_up, v_ffn_conv_w, v_ffn_conv_b, v_ffn_w_down, v_ffn_b_down, v_ln2_g, v_ln2_b):
    w = dict(w_in=w_in, b_in=b_in, conv_dw_w=conv_dw_w, conv_dw_b=conv_dw_b, conv_ln_g=conv_ln_g, conv_ln_b=conv_ln_b,
             rel_bias_table=rel_bias_table, gmlp_ln_g=gmlp_ln_g, gmlp_ln_b=gmlp_ln_b, gmlp_w_s=gmlp_w_s,
             gmlp_b_s=gmlp_b_s, w_out=w_out, b_out=b_out, ln1_g=ln1_g, ln1_b=ln1_b, ffn_w_up=ffn_w_up,
             ffn_b_up=ffn_b_up, ffn_conv_w=ffn_conv_w, ffn_conv_b=ffn_conv_b, ffn_w_down=ffn_w_down,
             ffn_b_down=ffn_b_down, ln2_g=ln2_g, ln2_b=ln2_b)
    m = dict(w_in=m_w_in, b_in=m_b_in, conv_dw_w=m_conv_dw_w, conv_dw_b=m_conv_dw_b, conv_ln_g=m_conv_ln_g,
             conv_ln_b=m_conv_ln_b, rel_bias_table=m_rel_bias_table, gmlp_ln_g=m_gmlp_ln_g, gmlp_ln_b=m_gmlp_ln_b,
             gmlp_w_s=m_gmlp_w_s, gmlp_b_s=m_gmlp_b_s, w_out=m_w_out, b_out=m_b_out, ln1_g=m_ln1_g, ln1_b=m_ln1_b,
             ffn_w_up=m_ffn_w_up, ffn_b_up=m_ffn_b_up, ffn_conv_w=m_ffn_conv_w, ffn_conv_b=m_ffn_conv_b,
             ffn_w_down=m_ffn_w_down, ffn_b_down=m_ffn_b_down, ln2_g=m_ln2_g, ln2_b=m_ln2_b)
    v = dict(w_in=v_w_in, b_in=v_b_in, conv_dw_w=v_conv_dw_w, conv_dw_b=v_conv_dw_b, conv_ln_g=v_conv_ln_g,
             conv_ln_b=v_conv_ln_b, rel_bias_table=v_rel_bias_table, gmlp_ln_g=v_gmlp_ln_g, gmlp_ln_b=v_gmlp_ln_b,
             gmlp_w_s=v_gmlp_w_s, gmlp_b_s=v_gmlp_b_s, w_out=v_w_out, b_out=v_b_out, ln1_g=v_ln1_g, ln1_b=v_ln1_b,
             ffn_w_up=v_ffn_w_up, ffn_b_up=v_ffn_b_up, ffn_conv_w=v_ffn_conv_w, ffn_conv_b=v_ffn_conv_b,
             ffn_w_down=v_ffn_w_down, ffn_b_down=v_ffn_b_down, ln2_g=v_ln2_g, ln2_b=v_ln2_b)

    chip_arr = jnp.reshape(2 * lax.axis_index("x") + lax.axis_index("y"), (1,)).astype(jnp.int32)
    shards = {"w_in": _cast_bf16(w_in.reshape(-1, w_in.shape[-1])).reshape(w_in.shape)}
    wb, conv_stack, fconv_stack = _gather_weights(shards, conv_dw_w, ffn_conv_w)
    send_sems, recv_sems, in_flight, token = _gather_start(
        [_cast_into_full(w[n], n, chip_arr) for n in LATE_WEIGHTS], conv_stack)
    sp = {n: w[n] for n in SMALL}
    sp["conv_dw_w"] = jnp.moveaxis(conv_stack, 0, 2).reshape(DEPTH, CONV_WIDTH, CONV_CH)
    sp["ffn_conv_w"] = jnp.moveaxis(fconv_stack, 0, 2).reshape(DEPTH, FFN_CONV_WIDTH, 2 * D_FF)
    sp["b_in"] = sp["b_in"] + token[0, 0]

    def late_weights(after):
        return dict(zip(LATE_WEIGHTS, _gather_wait(send_sems, recv_sems, in_flight, after)))

    sink = _GradExchange()
    loss_local, grad_x, grads, big = _local_step(x[0], loss_target[0], wb, late_weights, sp, sink)

    small_shapes = [(1,)] + [grads[n].shape for n in SMALL]
    summed = _unpack(_small_allreduce(_pack([loss_local.reshape(1)] + [grads[n] for n in SMALL])), small_shapes)
    loss = summed[0].reshape(())
    small = dict(zip(SMALL, summed[1:]))
    chip = 2 * lax.axis_index("x") + lax.axis_index("y")
    for n in SMALL_SHARDED:
        width = w[n].shape[-1]
        small[n] = lax.dynamic_slice_in_dim(small[n], chip * width, width, axis=2)

    g_out, d_out, m_out, v_out = {}, {}, {}, {}
    for n in BIG:
        outs = None
        for l in range(DEPTH):
            own, other = big[(n, l)]
            outs = _adamw_halves(own, other, w[n], m[n], v[n], n, l, sink.c_arr, outs)
        g_out[n], d_out[n], m_out[n], v_out[n] = outs
    shapes = [small[n].shape for n in SMALL]
    packed = [_pack([src[n] for n in SMALL]) for src in (small, w, m, v)]
    upd = _adamw(*packed, "adamw_small")
    for dst, buf in zip((d_out, m_out, v_out), upd):
        dst.update(zip(SMALL, _unpack(buf, shapes)))
    g_out.update(small)

    return (loss, grad_x[None], *[g_out[n] for n in WEIGHTS], *[d_out[n] for n in WEIGHTS],
            *[m_out[n] for n in WEIGHTS], *[v_out[n] for n in WEIGHTS])
```

```python
import functools
import math

import numpy as np
import jax
import jax.numpy as jnp
from jax import lax
from jax.experimental import pallas as pl
from jax.experimental.pallas import tpu as pltpu

f32 = jnp.float32
bf16 = jnp.bfloat16

D_MODEL = 1024
DEPTH = 2
HEAD_DIM = 64
CONV_CH = 256
CONV_WIDTH = 31
ATTN_HEADS = 8
ATTN_CH = ATTN_HEADS * HEAD_DIM
DILATIONS = (1, 4, 16)
ATTN_BLOCK = 128
N_BUCKETS = 32
MAX_DISTANCE = 2048
GMLP_CH = 256
GMLP_GROUPS = 4
GMLP_GROUP_DIM = GMLP_CH // GMLP_GROUPS
CHUNK = 128
IN_CH = 2 * CONV_CH + 3 * ATTN_CH + 2 * GMLP_CH
D_FF = 2816
FFN_CONV_WIDTH = 3
LN_EPS = 1e-5
ALPHA = (2.0 * DEPTH) ** 0.25
ADAM_LR = 0.001
ADAM_B1 = 0.9
ADAM_B2 = 0.999
ADAM_EPS = 1e-08
ADAM_WD = 0.01
ADAM_STEP = 10

CONV_HALO = 32
FFN_HALO = 8
NEG = -1e30
MIB = 2 ** 20
NT_DIMS = (((1,), (1,)), ((), ()))
TN_DIMS = (((0,), (0,)), ((), ()))
MESH_ID = pl.DeviceIdType.MESH


def _cp(sem, vmem_mib):
    return pltpu.CompilerParams(dimension_semantics=sem, vmem_limit_bytes=vmem_mib * MIB)


def _resident(shape):
    nd = len(shape)
    return pl.BlockSpec(shape, lambda *_: (0,) * nd, pipeline_mode=pl.Buffered(1))


def _acc(shape):
    nd = len(shape)
    return pl.BlockSpec(shape, lambda *_: (0,) * nd)


def _sig(x):
    return 1.0 / (1.0 + jnp.exp(-x))


def _ln_stats(z):
    mu = jnp.mean(z, axis=-1, keepdims=True)
    zc = z - mu
    var = jnp.mean(zc * zc, axis=-1, keepdims=True)
    rstd = lax.rsqrt(var + LN_EPS)
    return zc * rstd, rstd


def _ln_bwd(dy, xhat, rstd, g):
    dxh = dy * g
    m1 = jnp.mean(dxh, axis=-1, keepdims=True)
    m2 = jnp.mean(dxh * xhat, axis=-1, keepdims=True)
    return rstd * (dxh - m1 - xhat * m2)


def _colsum(x):
    return jnp.sum(x, axis=0, keepdims=True)


def _t5_bucket_np(dist):
    max_exact = N_BUCKETS // 2
    dd = np.maximum(dist, 1).astype(np.float64)
    large = max_exact + (np.log(dd / max_exact) / math.log(MAX_DISTANCE / max_exact)
                         * (N_BUCKETS - max_exact)).astype(np.int32)
    large = np.minimum(large, N_BUCKETS - 1)
    return np.where(dist < max_exact, dist, large).astype(np.int32)


def _bucket_ids():
    qi = np.arange(ATTN_BLOCK)[:, None]
    kj = np.arange(2 * ATTN_BLOCK)[None, :]
    dist = np.clip(qi + ATTN_BLOCK - kj, 0, None)
    return np.stack([_t5_bucket_np(dist * d) for d in DILATIONS]).astype(np.int32)


LANES = 128
QKV_CH = 3 * ATTN_CH
PERM_TILE = 512


def _slabs(n, rows):
    return [pltpu.VMEM((rows, LANES), f32)] * n


def _rows_of(slab, r, n, d):
    return slab[...] if d == 1 else slab[pl.ds(r, n, stride=d), :]


def _set_rows_of(slab, r, n, d, val):
    if d == 1:
        slab[...] = val
    else:
        slab[pl.ds(r, n, stride=d), :] = val


def _perm_spec(d, ch):
    return pl.BlockSpec((d, PERM_TILE // d, ch), lambda i: (0, i, 0))


def _perm_shape(S, d, ch, dtype):
    return jax.ShapeDtypeStruct((d, S // d, ch), dtype)


def _inproj_fwd(x, w, b):
    S = x.shape[0]
    T = PERM_TILE
    nsl = QKV_CH // LANES

    def body(x_ref, w_ref, b_ref, a_ref, c_ref, *rest):
        q_refs = rest[:len(DILATIONS)]
        slabs = rest[len(DILATIONS):]
        h = jnp.dot(x_ref[...].astype(bf16), w_ref[...], preferred_element_type=f32) + b_ref[...]
        a_ref[...] = h[:, :2 * CONV_CH]
        q0 = 2 * CONV_CH
        c_ref[...] = h[:, q0 + QKV_CH:]
        for j in range(nsl):
            piece = h[:, q0 + LANES * j:q0 + LANES * (j + 1)]
            if LANES * j < ATTN_CH:
                piece = piece * (HEAD_DIM ** -0.5)
            slabs[j][...] = piece
        for d, q_ref in zip(DILATIONS, q_refs):
            for r in range(d):
                for j in range(nsl):
                    q_ref[r, :, LANES * j:LANES * (j + 1)] = _rows_of(slabs[j], r, T // d, d).astype(bf16)

    row = lambda c: pl.BlockSpec((T, c), lambda i: (i, 0))
    return pl.pallas_call(
        body, grid=(S // T,), name="inproj_fwd",
        out_shape=(jax.ShapeDtypeStruct((S, 2 * CONV_CH), f32), jax.ShapeDtypeStruct((S, 2 * GMLP_CH), f32))
        + tuple(_perm_shape(S, d, QKV_CH, bf16) for d in DILATIONS),
        in_specs=[row(D_MODEL), _resident((D_MODEL, IN_CH)), _resident((1, IN_CH))],
        out_specs=(row(2 * CONV_CH), row(2 * GMLP_CH)) + tuple(_perm_spec(d, QKV_CH) for d in DILATIONS),
        scratch_shapes=_slabs(nsl, T),
        compiler_params=_cp(("parallel",), 48),
    )(x, w, b)


CONV_GROUP = 64


def _window_rolls(starts):
    groups = {}
    for s in starts:
        groups.setdefault((-s) % SUBLANES, []).append(s)
    return dict(sorted(groups.items()))


def _conv_fwd(a_in, dw_w, dw_b, ln_g, ln_b):
    S = a_in.shape[0]
    T = 512
    hb = T // CONV_HALO

    def body(a_ref, halo_ref, w_ref, b_ref, g_ref, be_ref, out_ref, hc_ref, buf):
        i = pl.program_id(0)
        am = a_ref[...]
        ah = halo_ref[...]
        hgh = ah[:, :CONV_CH] * _sig(ah[:, CONV_CH:])
        buf[0:CONV_HALO, :] = jnp.where(i > 0, hgh, 0.0)
        buf[CONV_HALO:, :] = am[:, :CONV_CH] * _sig(am[:, CONV_CH:])
        starts = _window_rolls(range(CONV_HALO - (CONV_WIDTH - 1), CONV_HALO + 1))
        slabs = [slice(LANES * j, LANES * (j + 1)) for j in range(CONV_CH // LANES)]

        def step(g, _):
            r0 = pl.multiple_of(g * CONV_GROUP, CONV_GROUP)
            rows = pl.ds(r0, CONV_GROUP)
            for cs in slabs:
                ext = buf[pl.ds(r0, CONV_GROUP + CONV_HALO), cs]
                acc = jnp.broadcast_to(b_ref[:, cs], (CONV_GROUP, LANES))
                for b, ss in starts.items():
                    rolled = ext if b == 0 else pltpu.roll(ext, b, 0)
                    for s in ss:
                        k = s - (CONV_HALO - (CONV_WIDTH - 1))
                        acc = acc + w_ref[k:k + 1, cs] * rolled[s + b:s + b + CONV_GROUP]
                hc_ref[rows, cs] = acc
            return 0

        lax.fori_loop(0, T // CONV_GROUP, step, 0)
        xhat, _ = _ln_stats(hc_ref[...])
        y = xhat * g_ref[...] + be_ref[...]
        out_ref[...] = (y * _sig(y)).astype(bf16)

    return pl.pallas_call(
        body, grid=(S // T,), name="conv_fwd",
        out_shape=(jax.ShapeDtypeStruct((S, CONV_CH), bf16), jax.ShapeDtypeStruct((S, CONV_CH), f32)),
        in_specs=[pl.BlockSpec((T, 2 * CONV_CH), lambda i: (i, 0)),
                  pl.BlockSpec((CONV_HALO, 2 * CONV_CH), lambda i: (jnp.maximum(i * hb - 1, 0), 0)),
                  _acc((32, CONV_CH)), _acc((1, CONV_CH)), _acc((1, CONV_CH)), _acc((1, CONV_CH))],
        out_specs=(pl.BlockSpec((T, CONV_CH), lambda i: (i, 0)), pl.BlockSpec((T, CONV_CH), lambda i: (i, 0))),
        scratch_shapes=[pltpu.VMEM((T + CONV_HALO, CONV_CH), f32)],
        compiler_params=_cp(("parallel",), 32),
    )(a_in, a_in, dw_w, dw_b, ln_g, ln_b)


def _bias_build(table, buckets):
    def body(t_ref, bk_ref, o_ref):
        h = pl.program_id(1)
        ids = bk_ref[0]
        acc = jnp.zeros((ATTN_BLOCK, 2 * ATTN_BLOCK), f32)
        for b in range(N_BUCKETS):
            acc = jnp.where(ids == b, t_ref[b, h], acc)
        o_ref[0, 0] = acc

    return pl.pallas_call(
        body, grid=(len(DILATIONS), ATTN_HEADS), name="bias_build",
        out_shape=jax.ShapeDtypeStruct((len(DILATIONS), ATTN_HEADS, ATTN_BLOCK, 2 * ATTN_BLOCK), f32),
        in_specs=[pl.BlockSpec(memory_space=pltpu.SMEM),
                  pl.BlockSpec((1, ATTN_BLOCK, 2 * ATTN_BLOCK), lambda p, h: (p, 0, 0))],
        out_specs=pl.BlockSpec((1, 1, ATTN_BLOCK, 2 * ATTN_BLOCK), lambda p, h: (p, h, 0, 0)),
        compiler_params=_cp(("arbitrary", "arbitrary"), 16),
    )(table, buckets)


def _head_tile(tile, h, col):
    lane_head = lax.broadcasted_iota(jnp.int32, tile.shape, 1) // 16
    return jnp.where(lane_head == h, col, tile)


HEAD_PAIRS = ATTN_HEADS // 2
UNITS_PER_BLOCK = ATTN_HEADS


def _attn_tile(L):
    return min(512, L)


def _band_mask(first_block, n):
    B = ATTN_BLOCK
    row = lax.broadcasted_iota(jnp.int32, (B, 2 * B), 0)
    col = lax.broadcasted_iota(jnp.int32, (B, 2 * B), 1)
    valid = (col >= row) & (col <= row + B)
    if first_block:
        valid = valid & ((col >= B) | (n > 0))
    return valid


def _head_lanes(a):
    lane = lax.broadcasted_iota(jnp.int32, (ATTN_BLOCK, LANES), 1)
    return (lane < HEAD_DIM) if a == 0 else (lane >= HEAD_DIM)


def _pair_keys(cur_ref, halo_ref, part, b, j):
    B = ATTN_BLOCK
    c0 = part * ATTN_CH + LANES * j
    own = cur_ref[B * b:B * (b + 1), c0:c0 + LANES]
    prev = halo_ref[:, LANES * j:LANES * (j + 1)] if b == 0 else cur_ref[B * (b - 1):B * b, c0:c0 + LANES]
    return jnp.concatenate([prev, own], axis=0)


def _attn_fwd_pattern(qkv, bias, d):
    _, L, _ = qkv.shape
    B = ATTN_BLOCK
    QB = _attn_tile(L)
    nsb = QB // B
    U = nsb * UNITS_PER_BLOCK

    def body(cur_ref, hk_ref, hv_ref, b_ref, o_ref, lse_ref, lg, pb):
        n = pl.program_id(1)
        for b in range(nsb):
            valid = _band_mask(b == 0, n)
            for j in range(HEAD_PAIRS):
                q2 = cur_ref[B * b:B * (b + 1), LANES * j:LANES * (j + 1)]
                k2 = _pair_keys(cur_ref, hk_ref, 1, b, j)
                for a in range(2):
                    u = (b * HEAD_PAIRS + j) * 2 + a
                    qm = jnp.where(_head_lanes(a), q2, jnp.zeros_like(q2))
                    logits = lax.dot_general(qm, k2, NT_DIMS, preferred_element_type=f32) + b_ref[2 * j + a]
                    lg[B * u:B * (u + 1), :] = jnp.where(valid, logits, NEG)
        m = jnp.max(lg[...], axis=1, keepdims=True)
        p = jnp.exp(lg[...] - m)
        s = jnp.sum(p, axis=1, keepdims=True)
        pb[...] = p.astype(bf16)
        lse = m + jnp.log(s)
        inv = 1.0 / s
        for b in range(nsb):
            tile = jnp.zeros((B, B), f32)
            for j in range(HEAD_PAIRS):
                v2 = _pair_keys(cur_ref, hv_ref, 2, b, j)
                outs = []
                for a in range(2):
                    u = (b * HEAD_PAIRS + j) * 2 + a
                    rows = slice(B * u, B * (u + 1))
                    outs.append(jnp.dot(pb[rows, :], v2, preferred_element_type=f32) * inv[rows])
                    tile = _head_tile(tile, 2 * j + a, lse[rows])
                o_ref[B * b:B * (b + 1), LANES * j:LANES * (j + 1)] = jnp.where(_head_lanes(0), outs[0], outs[1])
            lse_ref[B * b:B * (b + 1), :] = tile

    halo = lambda part: pl.BlockSpec((None, B, ATTN_CH), lambda r, n: (r, jnp.maximum(n * nsb - 1, 0), part))
    tile_spec = lambda c: pl.BlockSpec((None, QB, c), lambda r, n: (r, n, 0))
    return pl.pallas_call(
        body, grid=(d, L // QB), name=f"attn_fwd_d{d}",
        out_shape=(jax.ShapeDtypeStruct((d, L, ATTN_CH), f32), jax.ShapeDtypeStruct((d, L, B), f32)),
        in_specs=[tile_spec(QKV_CH), halo(1), halo(2), _resident((ATTN_HEADS, B, 2 * B))],
        out_specs=(tile_spec(ATTN_CH), tile_spec(B)),
        scratch_shapes=[pltpu.VMEM((U * B, 2 * B), f32), pltpu.VMEM((U * B, 2 * B), bf16)],
        compiler_params=_cp(("parallel", "parallel"), 40),
    )(qkv, qkv, qkv, bias)


def _attn_merge(parts):
    S = parts[0][0].shape[0] * parts[0][0].shape[1]
    T = PERM_TILE
    nsl = ATTN_CH // LANES
    n_p = len(DILATIONS)

    def body(*refs):
        ins = refs[:2 * n_p]
        out_ref, lse_ref = refs[2 * n_p:2 * n_p + 2]
        slabs = refs[2 * n_p + 2:]
        lses = []
        for p, d in enumerate(DILATIONS):
            o_ref, l_ref = ins[2 * p], ins[2 * p + 1]
            osl = slabs[p * (nsl + 1):p * (nsl + 1) + nsl]
            lsl = slabs[p * (nsl + 1) + nsl]
            for r in range(d):
                for j in range(nsl):
                    _set_rows_of(osl[j], r, T // d, d, o_ref[r, :, LANES * j:LANES * (j + 1)])
                _set_rows_of(lsl, r, T // d, d, l_ref[r])
            lses.append(lsl[...])
        big = functools.reduce(jnp.maximum, lses)
        ws = [jnp.exp(l - big) for l in lses]
        tot = functools.reduce(lambda a_, b_: a_ + b_, ws)
        lse_ref[...] = big + jnp.log(tot)
        ws = [w / tot for w in ws]
        for j in range(nsl):
            acc = jnp.zeros((T, LANES), f32)
            for p in range(n_p):
                wa = ws[p][:, 32 * j:32 * j + 1]
                wb = ws[p][:, 32 * j + 16:32 * j + 17]
                lane = lax.broadcasted_iota(jnp.int32, (T, LANES), 1)
                acc = acc + jnp.where(lane < HEAD_DIM, wa, wb) * slabs[p * (nsl + 1) + j][...]
            out_ref[:, LANES * j:LANES * (j + 1)] = acc.astype(bf16)

    in_specs, args = [], []
    for (o, l), d in zip(parts, DILATIONS):
        in_specs += [_perm_spec(d, ATTN_CH), _perm_spec(d, ATTN_BLOCK)]
        args += [o, l]
    row = lambda c: pl.BlockSpec((T, c), lambda i: (i, 0))
    return pl.pallas_call(
        body, grid=(S // T,), name="attn_merge",
        out_shape=(jax.ShapeDtypeStruct((S, ATTN_CH), bf16), jax.ShapeDtypeStruct((S, ATTN_BLOCK), f32)),
        in_specs=in_specs, out_specs=(row(ATTN_CH), row(ATTN_BLOCK)),
        scratch_shapes=_slabs(n_p * (nsl + 1), T),
        compiler_params=_cp(("parallel",), 40),
    )(*args)


def _attn_fwd(qkvs, bias):
    parts = [_attn_fwd_pattern(q, bias[p], d) for p, (q, d) in enumerate(zip(qkvs, DILATIONS))]
    return _attn_merge(parts)


def _tril_bf16(w):
    row = lax.broadcasted_iota(jnp.int32, (CHUNK, CHUNK), 0)
    col = lax.broadcasted_iota(jnp.int32, (CHUNK, CHUNK), 1)
    return jnp.where(col <= row, w, 0.0).astype(bf16)


def _gmlp_fwd(c_in, ln_g, ln_b, w_s, b_s_t):
    S = c_in.shape[0]
    T = 512

    def body(c_ref, g_ref, be_ref, w_ref, bs_ref, out_ref, mix):
        c = c_ref[...]
        xhat, _ = _ln_stats(c[:, GMLP_CH:])
        vb = (xhat * g_ref[...] + be_ref[...]).astype(bf16)
        for g in range(GMLP_GROUPS):
            wt = _tril_bf16(w_ref[g])
            cs = slice(GMLP_GROUP_DIM * g, GMLP_GROUP_DIM * (g + 1))
            for ci in range(T // CHUNK):
                rs = slice(CHUNK * ci, CHUNK * (ci + 1))
                mix[rs, cs] = jnp.dot(wt, vb[rs, cs], preferred_element_type=f32) + bs_ref[:, g:g + 1]
        out_ref[...] = (c[:, :GMLP_CH] * mix[...]).astype(bf16)

    return pl.pallas_call(
        body, grid=(S // T,), name="gmlp_fwd",
        out_shape=jax.ShapeDtypeStruct((S, GMLP_CH), bf16),
        in_specs=[pl.BlockSpec((T, 2 * GMLP_CH), lambda i: (i, 0)), _acc((1, GMLP_CH)), _acc((1, GMLP_CH)),
                  _acc((GMLP_GROUPS, CHUNK, CHUNK)), _acc((CHUNK, GMLP_GROUPS))],
        out_specs=pl.BlockSpec((T, GMLP_CH), lambda i: (i, 0)),
        scratch_shapes=[pltpu.VMEM((T, GMLP_CH), f32)],
        compiler_params=_cp(("parallel",), 32),
    )(c_in, ln_g, ln_b, w_s, b_s_t)


def _outproj_ln_fwd(conv_out, attn_out, gm_out, w, b, x, ln_g, ln_b):
    S = x.shape[0]
    T = 512

    def body(co_ref, ao_ref, go_ref, w_ref, b_ref, x_ref, g_ref, be_ref, cat_ref, z_ref, yb_ref):
        cat = jnp.concatenate([co_ref[...], ao_ref[...], go_ref[...]], axis=1)
        cat_ref[...] = cat
        z = jnp.dot(cat, w_ref[...], preferred_element_type=f32) + b_ref[...] + ALPHA * x_ref[...]
        z_ref[...] = z
        xhat, _ = _ln_stats(z)
        yb_ref[...] = (xhat * g_ref[...] + be_ref[...]).astype(bf16)

    row = lambda c: pl.BlockSpec((T, c), lambda i: (i, 0))
    return pl.pallas_call(
        body, grid=(S // T,), name="outproj_ln_fwd",
        out_shape=(jax.ShapeDtypeStruct((S, D_MODEL), bf16), jax.ShapeDtypeStruct((S, D_MODEL), f32),
                   jax.ShapeDtypeStruct((S, D_MODEL), bf16)),
        in_specs=[row(CONV_CH), row(ATTN_CH), row(GMLP_CH), _resident((D_MODEL, D_MODEL)), _acc((1, D_MODEL)),
                  row(D_MODEL), _acc((1, D_MODEL)), _acc((1, D_MODEL))],
        out_specs=(row(D_MODEL), row(D_MODEL), row(D_MODEL)),
        compiler_params=_cp(("parallel",), 40),
    )(conv_out, attn_out, gm_out, w, b, x, ln_g, ln_b)


GATE_ROWS = 32
GATE_COLS = 128
GATE_MM_COLS = 256
SUBLANES = 8


def _gate_cols(c0):
    return slice(c0, c0 + GATE_COLS), slice(D_FF + c0, D_FF + c0 + GATE_COLS)


def _bcast_rows(ref, k, cs):
    return jnp.broadcast_to(ref[k:k + 1, cs], (GATE_ROWS, GATE_COLS))


def _fold_rows(z):
    acc = z[0:SUBLANES]
    for r in range(SUBLANES, GATE_ROWS, SUBLANES):
        acc = acc + z[r:r + SUBLANES]
    return acc


def _ffn_up_gate_fwd(x1b, w, b, conv_w, conv_b):
    S = x1b.shape[0]
    T = 256
    H = FFN_HALO
    K = FFN_CONV_WIDTH

    def body(x_ref, w_ref, b_ref, cw_ref, cb_ref, hfb_ref, hc_ref, act_ref, hbuf, carry):
        @pl.when(pl.program_id(0) == 0)
        def _():
            carry[...] = jnp.zeros_like(carry)
        x = x_ref[...]
        for m0 in range(0, D_FF, GATE_MM_COLS):
            for cm in (slice(m0, m0 + GATE_MM_COLS), slice(D_FF + m0, D_FF + m0 + GATE_MM_COLS)):
                h = jnp.dot(x, w_ref[:, cm], preferred_element_type=f32) + b_ref[:, cm]
                hbuf[:, cm] = h
                hfb_ref[:, cm] = h.astype(bf16)
            for c0 in range(m0, m0 + GATE_MM_COLS, GATE_COLS):
                cols = _gate_cols(c0)
                wts = [[_bcast_rows(cw_ref, k, cs) for k in range(K)] + [_bcast_rows(cb_ref, 0, cs)] for cs in cols]

                def step(rg, tails, cols=cols, wts=wts):
                    rows = pl.ds(pl.multiple_of(rg * GATE_ROWS, GATE_ROWS), GATE_ROWS)
                    hc, new_tails = [], []
                    for cs, wt, tail in zip(cols, wts, tails):
                        h = hbuf[rows, cs]
                        ext = jnp.concatenate([tail, h], axis=0)
                        acc = wt[K] + wt[K - 1] * h
                        for back in range(1, K):
                            acc = acc + wt[K - 1 - back] * pltpu.roll(ext, back, 0)[H:]
                        hc_ref[rows, cs] = acc
                        hc.append(acc)
                        new_tails.append(h[GATE_ROWS - H:])
                    act_ref[rows, cols[0]] = (hc[0] * _sig(hc[0]) * hc[1]).astype(bf16)
                    return tuple(new_tails)

                tails = lax.fori_loop(0, T // GATE_ROWS, step, tuple(carry[:, cs] for cs in cols), unroll=True)
                for cs, tail in zip(cols, tails):
                    carry[:, cs] = tail

    row = lambda c: pl.BlockSpec((T, c), lambda i: (i, 0))
    return pl.pallas_call(
        body, grid=(S // T,), name="ffn_up_gate_fwd",
        out_shape=(jax.ShapeDtypeStruct((S, 2 * D_FF), bf16), jax.ShapeDtypeStruct((S, 2 * D_FF), f32),
                   jax.ShapeDtypeStruct((S, D_FF), bf16)),
        in_specs=[row(D_MODEL), _resident((D_MODEL, 2 * D_FF)), _acc((1, 2 * D_FF)), _acc((8, 2 * D_FF)),
                  _acc((1, 2 * D_FF))],
        out_specs=(row(2 * D_FF), row(2 * D_FF), row(D_FF)),
        scratch_shapes=[pltpu.VMEM((T, 2 * D_FF), f32), pltpu.VMEM((H, 2 * D_FF), f32)],
        compiler_params=_cp(("arbitrary",), 56),
    )(x1b, w, b, conv_w, conv_b)


def _ffn_down_ln_fwd(act, w, b, z1, ln1_g, ln1_b, ln_g, ln_b):
    S = act.shape[0]
    T = 512

    def body(a_ref, w_ref, b_ref, z1_ref, g1_ref, be1_ref, g_ref, be_ref, z_ref, y_ref):
        x1 = _ln_stats(z1_ref[...])[0] * g1_ref[...] + be1_ref[...]
        z = jnp.dot(a_ref[...], w_ref[...], preferred_element_type=f32) + b_ref[...] + ALPHA * x1
        z_ref[...] = z
        xhat, _ = _ln_stats(z)
        y_ref[...] = xhat * g_ref[...] + be_ref[...]

    row = lambda c: pl.BlockSpec((T, c), lambda i: (i, 0))
    return pl.pallas_call(
        body, grid=(S // T,), name="ffn_down_ln_fwd",
        out_shape=(jax.ShapeDtypeStruct((S, D_MODEL), f32), jax.ShapeDtypeStruct((S, D_MODEL), f32)),
        in_specs=[row(D_FF), _resident((D_FF, D_MODEL)), _acc((1, D_MODEL)), row(D_MODEL)] + [_acc((1, D_MODEL))] * 4,
        out_specs=(row(D_MODEL), row(D_MODEL)),
        compiler_params=_cp(("parallel",), 40),
    )(act, w, b, z1, ln1_g, ln1_b, ln_g, ln_b)


def _ffn_down_ln_loss(act, w, b, z1, ln1_g, ln1_b, ln_g, ln_b, target):
    S = act.shape[0]
    T = 512

    def body(a_ref, w_ref, b_ref, z1_ref, g1_ref, be1_ref, g_ref, be_ref, t_ref, dz_ref, dzb_ref, loss_ref, dg_ref,
             db_ref):
        @pl.when(pl.program_id(0) == 0)
        def _():
            loss_ref[...] = jnp.zeros_like(loss_ref)
            dg_ref[...] = jnp.zeros_like(dg_ref)
            db_ref[...] = jnp.zeros_like(db_ref)
        x1 = _ln_stats(z1_ref[...])[0] * g1_ref[...] + be1_ref[...]
        z = jnp.dot(a_ref[...], w_ref[...], preferred_element_type=f32) + b_ref[...] + ALPHA * x1
        xhat, rstd = _ln_stats(z)
        err = xhat * g_ref[...] + be_ref[...] - t_ref[...]
        loss_ref[...] += _colsum(err * err) * (0.5 / D_MODEL)
        dy = err * (1.0 / D_MODEL)
        dz = _ln_bwd(dy, xhat, rstd, g_ref[...])
        dz_ref[...] = dz
        dzb_ref[...] = dz.astype(bf16)
        dg_ref[...] += _colsum(dy * xhat)
        db_ref[...] += _colsum(dy)

    row = lambda c: pl.BlockSpec((T, c), lambda i: (i, 0))
    vec = jax.ShapeDtypeStruct((1, D_MODEL), f32)
    return pl.pallas_call(
        body, grid=(S // T,), name="ffn_down_ln_loss",
        out_shape=(jax.ShapeDtypeStruct((S, D_MODEL), f32), jax.ShapeDtypeStruct((S, D_MODEL), bf16), vec, vec, vec),
        in_specs=[row(D_FF), _resident((D_FF, D_MODEL)), _acc((1, D_MODEL)), row(D_MODEL)] + [_acc((1, D_MODEL))] * 4
        + [row(D_MODEL)],
        out_specs=(row(D_MODEL), row(D_MODEL), _acc((1, D_MODEL)), _acc((1, D_MODEL)), _acc((1, D_MODEL))),
        compiler_params=_cp(("arbitrary",), 40),
    )(act, w, b, z1, ln1_g, ln1_b, ln_g, ln_b, target)


def _dgrad_ln_bwd(g, w, dz_res, z, ln_g, name):
    S, K = g.shape
    SUB = 256
    T = 2 * SUB if S % (2 * SUB) == 0 else SUB
    with_ln = z is not None

    def body(*refs):
        if with_ln:
            g_ref, w_ref, r_ref, z_ref, lg_ref, dz_ref, dzb_ref, dg_ref, db_ref = refs
        else:
            g_ref, w_ref, r_ref, dx_ref = refs
        subs = [slice(s0, s0 + SUB) for s0 in range(0, T, SUB)]
        dxs = [lax.dot_general(g_ref[rs, :], w_ref[...], NT_DIMS, preferred_element_type=f32) + ALPHA * r_ref[rs, :]
               for rs in subs]
        if not with_ln:
            for rs, dx in zip(subs, dxs):
                dx_ref[rs, :] = dx
            return

        @pl.when(pl.program_id(0) == 0)
        def _():
            dg_ref[...] = jnp.zeros_like(dg_ref)
            db_ref[...] = jnp.zeros_like(db_ref)
        for rs, dx in zip(subs, dxs):
            xhat, rstd = _ln_stats(z_ref[rs, :])
            dz = _ln_bwd(dx, xhat, rstd, lg_ref[...])
            dz_ref[rs, :] = dz
            dzb_ref[rs, :] = dz.astype(bf16)
            dg_ref[...] += _colsum(dx * xhat)
            db_ref[...] += _colsum(dx)

    row = pl.BlockSpec((T, D_MODEL), lambda i: (i, 0))
    vec = jax.ShapeDtypeStruct((1, D_MODEL), f32)
    in_specs = [pl.BlockSpec((T, K), lambda i: (i, 0)), _resident((D_MODEL, K)), row]
    args = [g, w, dz_res]
    if with_ln:
        in_specs += [row, _acc((1, D_MODEL))]
        args += [z, ln_g]
        out_shape = (jax.ShapeDtypeStruct((S, D_MODEL), f32), jax.ShapeDtypeStruct((S, D_MODEL), bf16), vec, vec)
        out_specs = (row, row, _acc((1, D_MODEL)), _acc((1, D_MODEL)))
    else:
        out_shape = jax.ShapeDtypeStruct((S, D_MODEL), f32)
        out_specs = row
    return pl.pallas_call(
        body, grid=(S // T,), name=name, out_shape=out_shape, in_specs=in_specs, out_specs=out_specs,
        compiler_params=_cp(("arbitrary",), 48),
    )(*args)


def _ffn_down_gate_bwd(dzb, w_down, hfb, hc, conv_w):
    S = hc.shape[0]
    T = 256
    H = FFN_HALO
    nt = S // T
    K = FFN_CONV_WIDTH

    def body(dz_ref, w_ref, h_ref, hc_ref, cw_ref, dh_ref, dw_ref, dcb_ref, da_buf, carry):
        @pl.when(pl.program_id(0) == 0)
        def _():
            dw_ref[...] = jnp.zeros_like(dw_ref)
            dcb_ref[...] = jnp.zeros_like(dcb_ref)
            carry[...] = jnp.zeros_like(carry)
        da_buf[...] = lax.dot_general(dz_ref[...], w_ref[...], NT_DIMS, preferred_element_type=f32)
        ngroups = T // GATE_ROWS
        for c0 in range(0, D_FF, GATE_COLS):
            cols = _gate_cols(c0)
            wts = [[_bcast_rows(cw_ref, k, cs) for k in range(K)] for cs in cols]

            def step(it, state, cols=cols, wts=wts):
                heads, accs = state
                rows = pl.ds(pl.multiple_of((ngroups - 1 - it) * GATE_ROWS, GATE_ROWS), GATE_ROWS)
                g = hc_ref[rows, cols[0]]
                v = hc_ref[rows, cols[1]]
                da = da_buf[rows, cols[0]]
                sg = _sig(g)
                dms = (da * v * (sg * (1.0 + g * (1.0 - sg))), da * (g * sg))
                new_heads, new_accs = [], []
                for cs, wt, dm, head, acc in zip(cols, wts, dms, heads, accs):
                    h0 = h_ref[rows, cs].astype(f32)
                    ext = jnp.concatenate([dm, head], axis=0)
                    dh = wt[K - 1] * dm
                    acc_k = [None] * K + [acc[K] + _fold_rows(dm)]
                    acc_k[K - 1] = acc[K - 1] + _fold_rows(dm * h0)
                    for ahead in range(1, K):
                        dk = pltpu.roll(ext, GATE_ROWS + H - ahead, 0)[:GATE_ROWS]
                        dh = dh + wt[K - 1 - ahead] * dk
                        acc_k[K - 1 - ahead] = acc[K - 1 - ahead] + _fold_rows(dk * h0)
                    dh_ref[rows, cs] = dh.astype(bf16)
                    new_heads.append(dm[:H])
                    new_accs.append(tuple(acc_k))
                return tuple(new_heads), tuple(new_accs)

            zero = jnp.zeros((SUBLANES, GATE_COLS), f32)
            init = (tuple(carry[:, cs] for cs in cols), tuple(tuple(zero for _ in range(K + 1)) for _ in cols))
            heads, accs = lax.fori_loop(0, ngroups, step, init, unroll=True)
            for cs, head, acc in zip(cols, heads, accs):
                carry[:, cs] = head
                dcb_ref[:, cs] += _colsum(acc[K])
                for k in range(K):
                    dw_ref[k:k + 1, cs] += _colsum(acc[k])

    tile = lambda c: pl.BlockSpec((T, c), lambda i: (nt - 1 - i, 0))
    return pl.pallas_call(
        body, grid=(nt,), name="ffn_down_gate_bwd",
        out_shape=(jax.ShapeDtypeStruct((S, 2 * D_FF), bf16), jax.ShapeDtypeStruct((8, 2 * D_FF), f32),
                   jax.ShapeDtypeStruct((1, 2 * D_FF), f32)),
        in_specs=[tile(D_MODEL), _resident((D_FF, D_MODEL)), tile(2 * D_FF), tile(2 * D_FF), _acc((8, 2 * D_FF))],
        out_specs=(tile(2 * D_FF), _acc((8, 2 * D_FF)), _acc((1, 2 * D_FF))),
        scratch_shapes=[pltpu.VMEM((T, D_FF), f32), pltpu.VMEM((H, 2 * D_FF), f32)],
        compiler_params=_cp(("arbitrary",), 48),
    )(dzb, w_down, hfb, hc, conv_w)


def _wgrad(a, g, tn, name, rows=1024):
    S, K = a.shape
    N = g.shape[1]
    T = rows if S % rows == 0 else S

    def body(a_ref, g_ref, dw_ref, db_ref):
        @pl.when(pl.program_id(1) == 0)
        def _():
            dw_ref[...] = jnp.zeros_like(dw_ref)
            db_ref[...] = jnp.zeros_like(db_ref)
        gt = g_ref[...]
        dw_ref[...] += lax.dot_general(a_ref[...].astype(bf16), gt, TN_DIMS, preferred_element_type=f32)
        db_ref[...] += _colsum(gt.astype(f32))

    return pl.pallas_call(
        body, grid=(N // tn, S // T), name=name,
        out_shape=(jax.ShapeDtypeStruct((K, N), f32), jax.ShapeDtypeStruct((1, N), f32)),
        in_specs=[pl.BlockSpec((T, K), lambda j, i: (i, 0)), pl.BlockSpec((T, tn), lambda j, i: (i, j))],
        out_specs=(pl.BlockSpec((K, tn), lambda j, i: (0, j)), pl.BlockSpec((1, tn), lambda j, i: (0, j))),
        compiler_params=_cp(("parallel", "arbitrary"), 56),
    )(a, g)


def _outproj_dgrad(dzb, w, attn_out, lse):
    S = dzb.shape[0]
    T = PERM_TILE
    nsl = ATTN_CH // LANES
    n_p = len(DILATIONS)

    def body(g_ref, w_ref, ao_ref, lse_ref, dco_ref, dgo_ref, *rest):
        do_refs = rest[:n_p]
        st_refs = rest[n_p:2 * n_p]
        slabs = rest[2 * n_p:]
        dcat = lax.dot_general(g_ref[...], w_ref[...], NT_DIMS, preferred_element_type=f32)
        dco_ref[...] = dcat[:, :CONV_CH]
        dgo_ref[...] = dcat[:, CONV_CH + ATTN_CH:]
        lane = lax.broadcasted_iota(jnp.int32, (T, LANES), 1)
        st = lse_ref[...]
        for j in range(nsl):
            dO = dcat[:, CONV_CH + LANES * j:CONV_CH + LANES * (j + 1)]
            prod = dO * ao_ref[:, LANES * j:LANES * (j + 1)].astype(f32)
            for a in range(2):
                in_head = (lane < HEAD_DIM) if a == 0 else (lane >= HEAD_DIM)
                delta = jnp.sum(jnp.where(in_head, prod, 0.0), axis=1, keepdims=True)
                st = jnp.where((lane // 16 == 2 * j + a) & (lane % 16 >= 8), delta, st)
            slabs[j][...] = dO
        slabs[nsl][...] = st
        for d, do_ref, st_ref in zip(DILATIONS, do_refs, st_refs):
            for r in range(d):
                for j in range(nsl):
                    do_ref[r, :, LANES * j:LANES * (j + 1)] = _rows_of(slabs[j], r, T // d, d).astype(bf16)
                st_ref[r] = _rows_of(slabs[nsl], r, T // d, d)

    row = lambda c: pl.BlockSpec((T, c), lambda i: (i, 0))
    return pl.pallas_call(
        body, grid=(S // T,), name="outproj_dgrad",
        out_shape=(jax.ShapeDtypeStruct((S, CONV_CH), f32), jax.ShapeDtypeStruct((S, GMLP_CH), f32))
        + tuple(_perm_shape(S, d, ATTN_CH, bf16) for d in DILATIONS)
        + tuple(_perm_shape(S, d, ATTN_BLOCK, f32) for d in DILATIONS),
        in_specs=[row(D_MODEL), _resident((D_MODEL, D_MODEL)), row(ATTN_CH), row(ATTN_BLOCK)],
        out_specs=(row(CONV_CH), row(GMLP_CH)) + tuple(_perm_spec(d, ATTN_CH) for d in DILATIONS)
        + tuple(_perm_spec(d, ATTN_BLOCK) for d in DILATIONS),
        scratch_shapes=_slabs(nsl + 1, T),
        compiler_params=_cp(("parallel",), 40),
    )(dzb, w, attn_out, lse)


def _gmlp_bwd(c_in, dgm, ln_g, ln_b, w_s, b_s_t):
    S = c_in.shape[0]
    T = 512
    nsteps = S // T

    def body(c_ref, dg_ref, g_ref, be_ref, w_ref, bs_ref, dc_ref, dlg_ref, dlb_ref, dw_ref, dbs_ref,
             du_buf, dv_buf, dm_acc):
        i = pl.program_id(0)

        @pl.when(i == 0)
        def _():
            dlg_ref[...] = jnp.zeros_like(dlg_ref)
            dlb_ref[...] = jnp.zeros_like(dlb_ref)
            dw_ref[...] = jnp.zeros_like(dw_ref)
            dm_acc[...] = jnp.zeros_like(dm_acc)
        c = c_ref[...]
        u = c[:, :GMLP_CH]
        xhat, rstd = _ln_stats(c[:, GMLP_CH:])
        vb = (xhat * g_ref[...] + be_ref[...]).astype(bf16)
        dgm_t = dg_ref[...]
        dm_all = dgm_t * u
        for g in range(GMLP_GROUPS):
            wt = _tril_bf16(w_ref[g])
            cs = slice(GMLP_GROUP_DIM * g, GMLP_GROUP_DIM * (g + 1))
            dw_g = jnp.zeros((CHUNK, CHUNK), f32)
            for ci in range(T // CHUNK):
                rs = slice(CHUNK * ci, CHUNK * (ci + 1))
                v_c = vb[rs, cs]
                mixed = jnp.dot(wt, v_c, preferred_element_type=f32) + bs_ref[:, g:g + 1]
                dm = dm_all[rs, cs]
                dmb = dm.astype(bf16)
                du_buf[rs, cs] = dgm_t[rs, cs] * mixed
                dv_buf[rs, cs] = lax.dot_general(wt, dmb, TN_DIMS, preferred_element_type=f32)
                dw_g = dw_g + lax.dot_general(dmb, v_c, NT_DIMS, preferred_element_type=f32)
                dm_acc[:, cs] += dm
            dw_ref[g] += dw_g
        dv = dv_buf[...]
        dvr = _ln_bwd(dv, xhat, rstd, g_ref[...])
        dlg_ref[...] += _colsum(dv * xhat)
        dlb_ref[...] += _colsum(dv)
        dc_ref[:, :GMLP_CH] = du_buf[...].astype(bf16)
        dc_ref[:, GMLP_CH:] = dvr.astype(bf16)

        @pl.when(i == nsteps - 1)
        def _():
            row = lax.broadcasted_iota(jnp.int32, (CHUNK, CHUNK), 0)
            col = lax.broadcasted_iota(jnp.int32, (CHUNK, CHUNK), 1)
            tile = jnp.zeros((CHUNK, CHUNK), f32)
            for g in range(GMLP_GROUPS):
                dw_ref[g] = jnp.where(col <= row, dw_ref[g], 0.0)
                gsum = jnp.sum(dm_acc[:, GMLP_GROUP_DIM * g:GMLP_GROUP_DIM * (g + 1)], axis=1, keepdims=True)
                tile = jnp.where(col == g, gsum, tile)
            dbs_ref[...] = tile

    vec = jax.ShapeDtypeStruct((1, GMLP_CH), f32)
    return pl.pallas_call(
        body, grid=(nsteps,), name="gmlp_bwd",
        out_shape=(jax.ShapeDtypeStruct((S, 2 * GMLP_CH), bf16), vec, vec,
                   jax.ShapeDtypeStruct((GMLP_GROUPS, CHUNK, CHUNK), f32), jax.ShapeDtypeStruct((CHUNK, CHUNK), f32)),
        in_specs=[pl.BlockSpec((T, 2 * GMLP_CH), lambda i: (i, 0)), pl.BlockSpec((T, GMLP_CH), lambda i: (i, 0)),
                  _acc((1, GMLP_CH)), _acc((1, GMLP_CH)), _acc((GMLP_GROUPS, CHUNK, CHUNK)), _acc((CHUNK, GMLP_GROUPS))],
        out_specs=(pl.BlockSpec((T, 2 * GMLP_CH), lambda i: (i, 0)), _acc((1, GMLP_CH)), _acc((1, GMLP_CH)),
                   _acc((GMLP_GROUPS, CHUNK, CHUNK)), _acc((CHUNK, CHUNK))),
        scratch_shapes=[pltpu.VMEM((T, GMLP_CH), f32), pltpu.VMEM((T, GMLP_CH), f32), pltpu.VMEM((CHUNK, GMLP_CH), f32)],
        compiler_params=_cp(("arbitrary",), 32),
    )(c_in, dgm, ln_g, ln_b, w_s, b_s_t)


def _attn_bwd_pattern(qkv, d_out, stats, bias, d):
    _, L, _ = qkv.shape
    B = ATTN_BLOCK
    QB = _attn_tile(L)
    nsb = QB // B
    nt = L // QB
    U = nsb * UNITS_PER_BLOCK
    KV = 2 * ATTN_CH

    def body(cur_ref, hk_ref, hv_ref, do_ref, st_ref, b_ref, dqkv_ref, dbias_ref, lg, dp, pb, dsb, dkv, carry):
        r = pl.program_id(0)
        i = pl.program_id(1)
        n = nt - 1 - i

        @pl.when((r == 0) & (i == 0))
        def _():
            dbias_ref[...] = jnp.zeros_like(dbias_ref)

        @pl.when(i == 0)
        def _():
            carry[...] = jnp.zeros_like(carry)

        def operands(b, j, a):
            rows = slice(B * b, B * (b + 1))
            q2 = cur_ref[rows, LANES * j:LANES * (j + 1)]
            do2 = do_ref[rows, LANES * j:LANES * (j + 1)]
            keep = _head_lanes(a)
            return jnp.where(keep, q2, jnp.zeros_like(q2)), jnp.where(keep, do2, jnp.zeros_like(do2))

        for b in range(nsb):
            valid = _band_mask(b == 0, n)
            for j in range(HEAD_PAIRS):
                k2 = _pair_keys(cur_ref, hk_ref, 1, b, j)
                v2 = _pair_keys(cur_ref, hv_ref, 2, b, j)
                for a in range(2):
                    u = (b * HEAD_PAIRS + j) * 2 + a
                    qm, dom = operands(b, j, a)
                    logits = lax.dot_general(qm, k2, NT_DIMS, preferred_element_type=f32) + b_ref[2 * j + a]
                    lg[B * u:B * (u + 1), :] = jnp.where(valid, logits, NEG)
                    dp[B * u:B * (u + 1), :] = lax.dot_general(dom, v2, NT_DIMS, preferred_element_type=f32)
        for b in range(nsb):
            for j in range(HEAD_PAIRS):
                for a in range(2):
                    u = (b * HEAD_PAIRS + j) * 2 + a
                    rows = slice(B * u, B * (u + 1))
                    lane0 = 32 * j + 16 * a
                    lse = st_ref[B * b:B * (b + 1), lane0:lane0 + 1]
                    delta = st_ref[B * b:B * (b + 1), lane0 + 8:lane0 + 9]
                    p = jnp.exp(lg[rows, :] - lse)
                    ds = p * (dp[rows, :] - delta)
                    pb[rows, :] = p.astype(bf16)
                    dsb[rows, :] = ds.astype(bf16)
                    dbias_ref[2 * j + a] += ds
        dkv[...] = jnp.zeros_like(dkv)
        for b in range(nsb):
            for j in range(HEAD_PAIRS):
                k2 = _pair_keys(cur_ref, hk_ref, 1, b, j)
                dq, dk2, dv2 = [], None, None
                for a in range(2):
                    u = (b * HEAD_PAIRS + j) * 2 + a
                    rows = slice(B * u, B * (u + 1))
                    qm, dom = operands(b, j, a)
                    ds_u = dsb[rows, :]
                    dq.append(jnp.dot(ds_u, k2, preferred_element_type=f32))
                    dk_u = lax.dot_general(ds_u, qm, TN_DIMS, preferred_element_type=f32)
                    dv_u = lax.dot_general(pb[rows, :], dom, TN_DIMS, preferred_element_type=f32)
                    dk2 = dk_u if dk2 is None else dk2 + dk_u
                    dv2 = dv_u if dv2 is None else dv2 + dv_u
                dq2 = jnp.where(_head_lanes(0), dq[0], dq[1]) * (HEAD_DIM ** -0.5)
                dqkv_ref[B * b:B * (b + 1), LANES * j:LANES * (j + 1)] = dq2.astype(bf16)
                dkv[B * b:B * (b + 2), LANES * j:LANES * (j + 1)] += dk2
                dkv[B * b:B * (b + 2), ATTN_CH + LANES * j:ATTN_CH + LANES * (j + 1)] += dv2
        dkv[QB:, :] += carry[...]
        dqkv_ref[:, ATTN_CH:] = dkv[B:, :].astype(bf16)
        carry[...] = dkv[0:B, :]

    halo = lambda part: pl.BlockSpec((None, B, ATTN_CH),
                                     lambda r, i: (r, jnp.maximum((nt - 1 - i) * nsb - 1, 0), part))
    tile_spec = lambda c: pl.BlockSpec((None, QB, c), lambda r, i: (r, nt - 1 - i, 0))
    return pl.pallas_call(
        body, grid=(d, nt), name=f"attn_bwd_d{d}",
        out_shape=(jax.ShapeDtypeStruct((d, L, QKV_CH), bf16), jax.ShapeDtypeStruct((ATTN_HEADS, B, 2 * B), f32)),
        in_specs=[tile_spec(QKV_CH), halo(1), halo(2), tile_spec(ATTN_CH), tile_spec(B),
                  _resident((ATTN_HEADS, B, 2 * B))],
        out_specs=(tile_spec(QKV_CH), _acc((ATTN_HEADS, B, 2 * B))),
        scratch_shapes=[pltpu.VMEM((U * B, 2 * B), f32), pltpu.VMEM((U * B, 2 * B), f32),
                        pltpu.VMEM((U * B, 2 * B), bf16), pltpu.VMEM((U * B, 2 * B), bf16),
                        pltpu.VMEM((B + QB, KV), f32), pltpu.VMEM((B, KV), f32)],
        compiler_params=_cp(("arbitrary", "arbitrary"), 48),
    )(qkv, qkv, qkv, d_out, stats, bias)


def _attn_bwd_merge(d_a, dqkvs, d_c):
    S = d_a.shape[0]
    T = PERM_TILE
    nsl = QKV_CH // LANES
    n_p = len(DILATIONS)

    def body(da_ref, *rest):
        g_refs = rest[:n_p]
        dc_ref, dh_ref = rest[n_p:n_p + 2]
        slabs = rest[n_p + 2:]
        q0 = 2 * CONV_CH
        dh_ref[:, :q0] = da_ref[...]
        dh_ref[:, q0 + QKV_CH:] = dc_ref[...]
        for p, (d, g_ref) in enumerate(zip(DILATIONS, g_refs)):
            for r in range(d):
                for j in range(nsl):
                    _set_rows_of(slabs[p * nsl + j], r, T // d, d, g_ref[r, :, LANES * j:LANES * (j + 1)].astype(f32))
        for j in range(nsl):
            acc = slabs[j][...]
            for p in range(1, n_p):
                acc = acc + slabs[p * nsl + j][...]
            dh_ref[:, q0 + LANES * j:q0 + LANES * (j + 1)] = acc.astype(bf16)

    row = lambda c: pl.BlockSpec((T, c), lambda i: (i, 0))
    return pl.pallas_call(
        body, grid=(S // T,), name="attn_bwd_merge", out_shape=jax.ShapeDtypeStruct((S, IN_CH), bf16),
        in_specs=[row(2 * CONV_CH)] + [_perm_spec(d, QKV_CH) for d in DILATIONS] + [row(2 * GMLP_CH)],
        out_specs=row(IN_CH), scratch_shapes=_slabs(n_p * nsl, T),
        compiler_params=_cp(("parallel",), 48),
    )(d_a, *dqkvs, d_c)


def _bias_table_grad(dbias, buckets):
    n = dbias.shape[0]

    def body(db_ref, bk_ref, o_ref):
        p = pl.program_id(0)
        h = pl.program_id(1)

        @pl.when((p == 0) & (h == 0))
        def _():
            o_ref[...] = jnp.zeros_like(o_ref)
        ids = bk_ref[0]
        db = db_ref[0, 0]
        row = lax.broadcasted_iota(jnp.int32, (N_BUCKETS, 128), 0)
        lane = lax.broadcasted_iota(jnp.int32, (N_BUCKETS, 128), 1)
        upd = jnp.zeros((N_BUCKETS, 128), f32)
        for b in range(N_BUCKETS):
            s = jnp.sum(jnp.sum(jnp.where(ids == b, db, 0.0), axis=1, keepdims=True), axis=0, keepdims=True)
            upd = jnp.where((row == b) & (lane == h), s, upd)
        o_ref[...] += upd

    return pl.pallas_call(
        body, grid=(n, ATTN_HEADS), name="bias_table_grad",
        out_shape=jax.ShapeDtypeStruct((N_BUCKETS, 128), f32),
        in_specs=[pl.BlockSpec((1, 1, ATTN_BLOCK, 2 * ATTN_BLOCK), lambda p, h: (p, h, 0, 0)),
                  pl.BlockSpec((1, ATTN_BLOCK, 2 * ATTN_BLOCK), lambda p, h: (p, 0, 0))],
        out_specs=_acc((N_BUCKETS, 128)),
        compiler_params=_cp(("arbitrary", "arbitrary"), 16),
    )(dbias, buckets)


def _conv_bwd(a_in, hc, dco, dw_w, ln_g, ln_b):
    S = a_in.shape[0]
    T = 512
    hb = T // CONV_HALO
    nsteps = S // T
    R = T + CONV_HALO
    K = CONV_WIDTH

    def body(a_ref, hc_ref, hcn_ref, d_ref, dn_ref, w_ref, g_ref, be_ref,
             da_ref, dw_ref, dcb_ref, dlg_ref, dlb_ref, ext, dbuf, wacc):
        i = pl.program_id(0)

        @pl.when(i == 0)
        def _():
            wacc[...] = jnp.zeros_like(wacc)
            dcb_ref[...] = jnp.zeros_like(dcb_ref)
            dlg_ref[...] = jnp.zeros_like(dlg_ref)
            dlb_ref[...] = jnp.zeros_like(dlb_ref)
        ext[0:T, :] = hc_ref[...]
        ext[T:, :] = hcn_ref[...]
        xhat, rstd = _ln_stats(ext[...])
        hl = xhat * g_ref[...] + be_ref[...]
        ext[0:T, :] = d_ref[...]
        ext[T:, :] = dn_ref[...]
        sl_ = _sig(hl)
        dhl = ext[...] * (sl_ * (1.0 + hl * (1.0 - sl_)))
        dhc = _ln_bwd(dhl, xhat, rstd, g_ref[...])
        rowi = lax.broadcasted_iota(jnp.int32, (R, CONV_CH), 0)
        dbuf[...] = jnp.where((rowi < T) | (i < nsteps - 1), dhc, 0.0)
        dlg_ref[...] += _colsum(dhl[:T] * xhat[:T])
        dlb_ref[...] += _colsum(dhl[:T])
        dcb_ref[...] += _colsum(dbuf[pl.ds(0, T), :])
        starts = _window_rolls(range(K))
        slabs = [slice(LANES * j, LANES * (j + 1)) for j in range(CONV_CH // LANES)]

        def step(g, _):
            r0 = pl.multiple_of(g * CONV_GROUP, CONV_GROUP)
            rows = pl.ds(r0, CONV_GROUP)
            for j, cs in enumerate(slabs):
                gate_cs = slice(CONV_CH + LANES * j, CONV_CH + LANES * (j + 1))
                win = dbuf[pl.ds(r0, CONV_GROUP + CONV_HALO), cs]
                a = a_ref[rows, cs]
                sg = _sig(a_ref[rows, gate_cs])
                hg = a * sg
                dhg = jnp.zeros((CONV_GROUP, LANES), f32)
                for b, ss in starts.items():
                    rolled = win if b == 0 else pltpu.roll(win, b, 0)
                    for s in ss:
                        k = K - 1 - s
                        dk = rolled[s + b:s + b + CONV_GROUP]
                        dhg = dhg + w_ref[k:k + 1, cs] * dk
                        prod = dk * hg
                        fold = prod[0:SUBLANES]
                        for r in range(SUBLANES, CONV_GROUP, SUBLANES):
                            fold = fold + prod[r:r + SUBLANES]
                        wacc[SUBLANES * k:SUBLANES * (k + 1), cs] += fold
                da_ref[rows, cs] = (dhg * sg).astype(bf16)
                da_ref[rows, gate_cs] = (dhg * hg * (1.0 - sg)).astype(bf16)
            return 0

        lax.fori_loop(0, T // CONV_GROUP, step, 0)

        @pl.when(i == nsteps - 1)
        def _():
            for k in range(K):
                dw_ref[k:k + 1, :] = _colsum(wacc[SUBLANES * k:SUBLANES * (k + 1), :])
            dw_ref[K:, :] = jnp.zeros((32 - K, CONV_CH), f32)

    vec = jax.ShapeDtypeStruct((1, CONV_CH), f32)
    nxt = lambda i: (jnp.minimum((i + 1) * hb, nsteps * hb - 1), 0)
    return pl.pallas_call(
        body, grid=(nsteps,), name="conv_bwd",
        out_shape=(jax.ShapeDtypeStruct((S, 2 * CONV_CH), bf16), jax.ShapeDtypeStruct((32, CONV_CH), f32), vec, vec, vec),
        in_specs=[pl.BlockSpec((T, 2 * CONV_CH), lambda i: (i, 0)),
                  pl.BlockSpec((T, CONV_CH), lambda i: (i, 0)), pl.BlockSpec((CONV_HALO, CONV_CH), nxt),
                  pl.BlockSpec((T, CONV_CH), lambda i: (i, 0)), pl.BlockSpec((CONV_HALO, CONV_CH), nxt),
                  _acc((32, CONV_CH)), _acc((1, CONV_CH)), _acc((1, CONV_CH))],
        out_specs=(pl.BlockSpec((T, 2 * CONV_CH), lambda i: (i, 0)), _acc((32, CONV_CH)), _acc((1, CONV_CH)),
                   _acc((1, CONV_CH)), _acc((1, CONV_CH))),
        scratch_shapes=[pltpu.VMEM((R, CONV_CH), f32), pltpu.VMEM((R, CONV_CH), f32),
                        pltpu.VMEM((SUBLANES * 32, CONV_CH), f32)],
        compiler_params=_cp(("arbitrary",), 32),
    )(a_in, hc, hc, dco, dco, dw_w, ln_g, ln_b)


def _adamw(g, w, m, v, name):
    R, C = g.shape
    T = R
    for cand in (512, 256, 128, 64, 32, 16, 8):
        if R % cand == 0 and cand * C * 4 <= MIB:
            T = cand
            break
    c1 = 1.0 / (1.0 - ADAM_B1 ** ADAM_STEP)
    c2 = 1.0 / (1.0 - ADAM_B2 ** ADAM_STEP)

    def body(g_ref, w_ref, m_ref, v_ref, d_ref, nm_ref, nv_ref):
        gg = g_ref[...]
        nm = ADAM_B1 * m_ref[...] + (1.0 - ADAM_B1) * gg
        nv = ADAM_B2 * v_ref[...] + (1.0 - ADAM_B2) * (gg * gg)
        nm_ref[...] = nm
        nv_ref[...] = nv
        d_ref[...] = -ADAM_LR * ((nm * c1) / (jnp.sqrt(nv * c2) + ADAM_EPS) + ADAM_WD * w_ref[...])

    blk = pl.BlockSpec((T, C), lambda i: (i, 0))
    sd = jax.ShapeDtypeStruct((R, C), f32)
    return pl.pallas_call(
        body, grid=(R // T,), name=name, out_shape=(sd, sd, sd), in_specs=[blk] * 4, out_specs=(blk, blk, blk),
        compiler_params=_cp(("parallel",), 48),
    )(g, w, m, v)


def _pad_rows(a, rows):
    return jnp.pad(a, ((0, rows - a.shape[0]), (0, 0)))


def _local_step(x, target, wb, late_weights, sp, sink):
    buckets = jnp.asarray(_bucket_ids())
    bias = _bias_build(sp["rel_bias_table"], buckets)
    wb = dict(wb)
    saved = []
    xl = x
    for l in range(DEPTH):
        vec = lambda name: sp[name][l][None, :]
        a_in, c_in, *qkv = _inproj_fwd(xl, wb["w_in"][l], vec("b_in"))
        conv_w = _pad_rows(sp["conv_dw_w"][l], 32)
        conv_out, hc = _conv_fwd(a_in, conv_w, vec("conv_dw_b"), vec("conv_ln_g"), vec("conv_ln_b"))
        attn_out, lse = _attn_fwd(qkv, bias)
        bs_t = sp["gmlp_b_s"][l].T
        gm_out = _gmlp_fwd(c_in, vec("gmlp_ln_g"), vec("gmlp_ln_b"), sp["gmlp_w_s"][l], bs_t)
        if l == 0:
            wb.update(late_weights(gm_out))
        cat, z1, x1b = _outproj_ln_fwd(conv_out, attn_out, gm_out, wb["w_out"][l], vec("b_out"), xl,
                                           vec("ln1_g"), vec("ln1_b"))
        fconv_w = _pad_rows(sp["ffn_conv_w"][l], 8)
        hfb, fhc, act = _ffn_up_gate_fwd(x1b, wb["ffn_w_up"][l], vec("ffn_b_up"), fconv_w, vec("ffn_conv_b"))
        down = (act, wb["ffn_w_down"][l], vec("ffn_b_down"), z1, vec("ln1_g"), vec("ln1_b"), vec("ln2_g"),
                vec("ln2_b"))
        z2, x2 = _ffn_down_ln_fwd(*down) if l < DEPTH - 1 else (None, None)
        saved.append(dict(x=xl, a_in=a_in, qkv=qkv, c_in=c_in, hc=hc, attn_out=attn_out, lse=lse, cat=cat, z1=z1,
                          x1b=x1b, hfb=hfb, fhc=fhc, act=act, z2=z2, conv_w=conv_w, fconv_w=fconv_w, bs_t=bs_t))
        xl = x2

    grads = {}
    per_layer = {k: [None] * DEPTH for k in (
        "b_in", "conv_dw_w", "conv_dw_b", "conv_ln_g", "conv_ln_b", "gmlp_ln_g", "gmlp_ln_b", "gmlp_w_s",
        "gmlp_b_s", "b_out", "ln1_g", "ln1_b", "ffn_b_up", "ffn_conv_w", "ffn_conv_b", "ffn_b_down", "ln2_g", "ln2_b")}
    dbias_all = []
    dz2, dz2b, loss_part, dg2, db2 = _ffn_down_ln_loss(*down, target)
    loss = jnp.sum(loss_part)
    grad_x = None
    tok = jnp.zeros((), f32)
    for l in reversed(range(DEPTH)):
        sv = saved[l]
        vec = lambda name: sp[name][l][None, :] + tok
        per_layer["ln2_g"][l] = dg2[0]
        per_layer["ln2_b"][l] = db2[0]
        dw_down, db_down = _wgrad(sv["act"], dz2b, 512, "ffn_down_wgrad")
        tok = sink.put("ffn_w_down", l, dw_down, tok)
        per_layer["ffn_b_down"][l] = db_down[0]
        dhf, dfcw, dfcb = _ffn_down_gate_bwd(dz2b, wb["ffn_w_down"][l], sv["hfb"], sv["fhc"], sv["fconv_w"])
        per_layer["ffn_conv_w"][l] = dfcw[:FFN_CONV_WIDTH]
        per_layer["ffn_conv_b"][l] = dfcb[0]
        dw_up, db_up = _wgrad(sv["x1b"], dhf, 1408, "ffn_up_wgrad")
        tok = sink.put("ffn_w_up", l, dw_up, tok)
        per_layer["ffn_b_up"][l] = db_up[0]
        dz1, dz1b, dg1, db1 = _dgrad_ln_bwd(dhf, wb["ffn_w_up"][l], dz2, sv["z1"], vec("ln1_g"), "ffn_up_dgrad_ln")
        per_layer["ln1_g"][l] = dg1[0]
        per_layer["ln1_b"][l] = db1[0]
        dw_out, db_out = _wgrad(sv["cat"], dz1b, D_MODEL, "outproj_wgrad")
        tok = sink.put("w_out", l, dw_out, tok)
        per_layer["b_out"][l] = db_out[0]
        dco, dgo, *perm = _outproj_dgrad(dz1b, wb["w_out"][l], sv["attn_out"], sv["lse"])
        d_outs, stats = perm[:len(DILATIONS)], perm[len(DILATIONS):]
        d_c, dglg, dglb, dws, dbs = _gmlp_bwd(sv["c_in"], dgo, vec("gmlp_ln_g"), vec("gmlp_ln_b"), sp["gmlp_w_s"][l],
                                              sv["bs_t"])
        per_layer["gmlp_ln_g"][l] = dglg[0]
        per_layer["gmlp_ln_b"][l] = dglb[0]
        per_layer["gmlp_w_s"][l] = dws
        per_layer["gmlp_b_s"][l] = dbs[:, :GMLP_GROUPS].T
        dqkvs = []
        for p, d in enumerate(DILATIONS):
            dqkv, dbias = _attn_bwd_pattern(sv["qkv"][p], d_outs[p], stats[p], bias[p], d)
            dqkvs.append(dqkv)
            dbias_all.append(dbias)
        d_a, dcw, dcb, dclg, dclb = _conv_bwd(sv["a_in"], sv["hc"], dco, sv["conv_w"], vec("conv_ln_g"),
                                              vec("conv_ln_b"))
        per_layer["conv_dw_w"][l] = dcw[:CONV_WIDTH]
        per_layer["conv_dw_b"][l] = dcb[0]
        per_layer["conv_ln_g"][l] = dclg[0]
        per_layer["conv_ln_b"][l] = dclb[0]
        dh = _attn_bwd_merge(d_a, dqkvs, d_c)
        dw_in, db_in = _wgrad(sv["x"], dh, IN_CH, "inproj_wgrad")
        tok = sink.put("w_in", l, dw_in, tok)
        per_layer["b_in"][l] = db_in[0]
        if l > 0:
            pv = saved[l - 1]
            dz2, dz2b, dg2, db2 = _dgrad_ln_bwd(dh, wb["w_in"][l], dz1, pv["z2"], sp["ln2_g"][l - 1][None, :] + tok,
                                                "inproj_dgrad_ln")
        else:
            grad_x = _dgrad_ln_bwd(dh, wb["w_in"][l], dz1, None, None, "inproj_dgrad")
    for k, v in per_layer.items():
        grads[k] = jnp.stack(v)
    dbias_cat = jnp.stack(dbias_all)
    bk_cat = jnp.concatenate([buckets] * DEPTH, axis=0)
    grads["rel_bias_table"] = _bias_table_grad(dbias_cat, bk_cat)[:, :ATTN_HEADS]
    return loss, grad_x, grads, sink.finish(grad_x)


N_CHIPS = 4
BIG = {"w_in": (D_MODEL, IN_CH, 1), "w_out": (D_MODEL, D_MODEL, 0),
       "ffn_w_up": (D_MODEL, 2 * D_FF, 1), "ffn_w_down": (D_FF, D_MODEL, 0)}
SMALL = ("b_in", "conv_dw_w", "conv_dw_b", "conv_ln_g", "conv_ln_b", "rel_bias_table", "gmlp_ln_g", "gmlp_ln_b",
         "gmlp_w_s", "gmlp_b_s", "b_out", "ln1_g", "ln1_b", "ffn_b_up", "ffn_conv_w", "ffn_conv_b", "ffn_b_down",
         "ln2_g", "ln2_b")
SMALL_SHARDED = ("conv_dw_w", "ffn_conv_w")
WEIGHTS = ("w_in", "b_in", "conv_dw_w", "conv_dw_b", "conv_ln_g", "conv_ln_b", "rel_bias_table", "gmlp_ln_g",
           "gmlp_ln_b", "gmlp_w_s", "gmlp_b_s", "w_out", "b_out", "ln1_g", "ln1_b", "ffn_w_up", "ffn_b_up",
           "ffn_conv_w", "ffn_conv_b", "ffn_w_down", "ffn_b_down", "ln2_g", "ln2_b")
ANY = pl.BlockSpec(memory_space=pl.ANY)


def _position():
    return lax.axis_index("x"), lax.axis_index("y"), lax.axis_index("c")


def _other_chips(x, y):
    return [(1 - x, y), (x, 1 - y), (1 - x, 1 - y)]


def _cast_bf16(a):
    R, C = a.shape
    T = 128

    def body(a_ref, o_ref):
        o_ref[...] = a_ref[...].astype(bf16)

    return pl.pallas_call(
        body, grid=(R // T,), name="cast_bf16", out_shape=jax.ShapeDtypeStruct((R, C), bf16),
        in_specs=[pl.BlockSpec((T, C), lambda i: (i, 0))], out_specs=pl.BlockSpec((T, C), lambda i: (i, 0)),
        compiler_params=_cp(("parallel",), 16),
    )(a)


def _chip_slot(ref, name, l, p):
    K, N, ax = BIG[name]
    if ax == 1:
        sz = N // N_CHIPS
        return ref.at[l, :, pl.ds(pl.multiple_of(p * sz, 128), sz)]
    sz = K // N_CHIPS
    return ref.at[l, pl.ds(pl.multiple_of(p * sz, 16), sz), :]


def _gather_weights(shards, conv_w, fconv_w):
    names = list(shards)
    n_big = len(names)
    n_t = n_big + 2
    n_chip = 3 * n_t
    n_pass = 3 * n_big

    def body(*refs):
        ins = refs[:n_t]
        outs = refs[n_t:2 * n_t]
        send_sems, recv_sems, pass_send, pass_recv, local_sems = refs[2 * n_t:]
        x, y, c = _position()
        me = 2 * x + y
        chips = _other_chips(x, y)

        def src(t):
            return ins[t].at[c] if t < n_big else ins[t]

        def slot(t, l, p):
            return _chip_slot(outs[t], names[t], l, p) if t < n_big else outs[t].at[p]

        locs, cps = [], []
        for t in range(n_t):
            for l in (range(DEPTH) if t < n_big else (0,)):
                loc = pltpu.make_async_copy(ins[t].at[l] if t < n_big else ins[t], slot(t, l, me),
                                            local_sems.at[DEPTH * t + l])
                loc.start()
                locs.append(loc)
            for k, (px, py) in enumerate(chips):
                cp = pltpu.make_async_remote_copy(
                    src_ref=src(t), dst_ref=slot(t, c, me), send_sem=send_sems.at[3 * t + k],
                    recv_sem=recv_sems.at[3 * t + k], device_id=(px, py, c), device_id_type=MESH_ID)
                cp.start()
                cps.append(cp)
        for t in range(n_t):
            for k, (px, py) in enumerate(chips):
                landed = slot(t, c, 2 * px + py)
                pltpu.make_async_remote_copy(
                    src_ref=src(t), dst_ref=landed, send_sem=send_sems.at[3 * t + k],
                    recv_sem=recv_sems.at[3 * t + k], device_id=(px, py, c), device_id_type=MESH_ID).wait_recv()
                if t < n_big:
                    cp = pltpu.make_async_remote_copy(
                        src_ref=landed, dst_ref=landed, send_sem=pass_send.at[3 * t + k],
                        recv_sem=pass_recv.at[3 * t + k], device_id=(x, y, 1 - c), device_id_type=MESH_ID)
                    cp.start()
                    cps.append(cp)
        for t in range(n_big):
            for k, (px, py) in enumerate(chips):
                from_sibling = slot(t, 1 - c, 2 * px + py)
                pltpu.make_async_remote_copy(
                    src_ref=from_sibling, dst_ref=from_sibling, send_sem=pass_send.at[3 * t + k],
                    recv_sem=pass_recv.at[3 * t + k], device_id=(x, y, 1 - c), device_id_type=MESH_ID).wait_recv()
        for cp in cps:
            cp.wait_send()
        for loc in locs:
            loc.wait()

    ins = [shards[n] for n in names] + [conv_w, fconv_w]
    out_shape = [jax.ShapeDtypeStruct((DEPTH, BIG[n][0], BIG[n][1]), bf16) for n in names]
    out_shape += [jax.ShapeDtypeStruct((N_CHIPS,) + conv_w.shape, f32), jax.ShapeDtypeStruct((N_CHIPS,) + fconv_w.shape, f32)]
    outs = pl.pallas_call(
        body, name="gather_weights", out_shape=tuple(out_shape), in_specs=[ANY] * n_t, out_specs=tuple([ANY] * n_t),
        scratch_shapes=[pltpu.SemaphoreType.DMA((n_chip,)), pltpu.SemaphoreType.DMA((n_chip,)),
                        pltpu.SemaphoreType.DMA((n_pass,)), pltpu.SemaphoreType.DMA((n_pass,)),
                        pltpu.SemaphoreType.DMA((DEPTH * n_t,))],
    )(*ins)
    return dict(zip(names, outs[:n_big])), outs[-2], outs[-1]


LATE_WEIGHTS = ("w_out", "ffn_w_up", "ffn_w_down")
HBM = pl.BlockSpec(memory_space=pltpu.HBM)
SEM = pl.BlockSpec(memory_space=pltpu.SEMAPHORE)


def _cast_into_full(shard, name, chip_arr):
    K, N, ax = BIG[name]
    k, n = _shard_shape(name)
    T = 64
    nrt = k // T

    def body(p_ref, a_ref, o_ref):
        o_ref[...] = a_ref[...].astype(bf16)

    if ax == 1:
        out_spec = pl.BlockSpec((None, T, n), lambda l, i, p: (l, i, p[0]))
    else:
        out_spec = pl.BlockSpec((None, T, n), lambda l, i, p: (l, p[0] * nrt + i, 0))
    return pl.pallas_call(
        body, name="cast_into_full", out_shape=jax.ShapeDtypeStruct((DEPTH, K, N), bf16),
        grid_spec=pltpu.PrefetchScalarGridSpec(
            num_scalar_prefetch=1, grid=(DEPTH, nrt),
            in_specs=[pl.BlockSpec((None, T, n), lambda l, i, p: (l, i, 0))], out_specs=out_spec),
        compiler_params=_cp(("parallel", "parallel"), 16),
    )(chip_arr, shard)


def _late_copies(refs, send_sems, recv_sems):
    x, y, c = _position()
    me = 2 * x + y
    idx = 0
    for ref, name in zip(refs, LATE_WEIGHTS):
        for l in range(DEPTH):
            for px, py in _other_chips(x, y):
                def copy(p, ref=ref, name=name, l=l, px=px, py=py, idx=idx):
                    part = _chip_slot(ref, name, l, p)
                    return pltpu.make_async_remote_copy(
                        src_ref=part, dst_ref=part, send_sem=send_sems.at[idx], recv_sem=recv_sems.at[idx],
                        device_id=(px, py, c), device_id_type=MESH_ID)
                yield copy(me), copy(2 * px + py)
                idx += 1


N_LATE_COPIES = 3 * DEPTH * len(LATE_WEIGHTS)


def _gather_start(fulls, after):
    n = len(fulls)

    def body(*refs):
        ins = refs[:n]
        send_sems, recv_sems = refs[n + 1:n + 3]
        token = refs[-1]
        for sent, _ in _late_copies(ins, send_sems, recv_sems):
            sent.start()
        token[...] = jnp.zeros_like(token)

    outs = pl.pallas_call(
        body, name="gather_start",
        out_shape=(pltpu.SemaphoreType.DMA((N_LATE_COPIES,)), pltpu.SemaphoreType.DMA((N_LATE_COPIES,)))
        + tuple(pltpu.HBM(f.shape, f.dtype) for f in fulls) + (jax.ShapeDtypeStruct((SUBLANES, LANES), f32),),
        in_specs=(HBM,) * n + (ANY,),
        out_specs=(SEM, SEM) + (HBM,) * n + (pl.BlockSpec(memory_space=pltpu.VMEM),),
        input_output_aliases={t: 2 + t for t in range(n)},
        compiler_params=pltpu.CompilerParams(has_side_effects=pltpu.SideEffectType.DATAFLOW_SIDE_EFFECTING),
    )(*[pltpu.with_memory_space_constraint(f, pltpu.HBM) for f in fulls], after)
    return outs[0], outs[1], outs[2:2 + n], outs[-1]


def _gather_wait(send_sems, recv_sems, fulls, after):
    n = len(fulls)

    def body(*refs):
        ins = refs[:n]
        send_ref, recv_ref = refs[n:n + 2]
        for sent, landed in _late_copies(ins, send_ref, recv_ref):
            sent.wait_send()
            landed.wait_recv()

    return pl.pallas_call(
        body, name="gather_wait", out_shape=tuple(pltpu.HBM(f.shape, f.dtype) for f in fulls),
        in_specs=(HBM,) * n + (SEM, SEM, ANY), out_specs=(HBM,) * n,
        input_output_aliases={t: t for t in range(n)},
        compiler_params=pltpu.CompilerParams(has_side_effects=pltpu.SideEffectType.DATAFLOW_SIDE_EFFECTING),
    )(*fulls, send_sems, recv_sems, after)


def _half(ref, name, c):
    K, N, ax = BIG[name]
    if ax == 1:
        return ref.at[pl.ds(pl.multiple_of(c * (K // 2), 8), K // 2), :]
    return ref.at[:, pl.ds(pl.multiple_of(c * (N // 2), 128), N // 2)]


def _half_shape(name):
    K, N, ax = BIG[name]
    return (K // 2, N) if ax == 1 else (K, N // 2)


def _shard_of_half(ref, name, q):
    K, N, ax = BIG[name]
    if ax == 1:
        sz = N // N_CHIPS
        return ref.at[:, pl.ds(pl.multiple_of(q * sz, 128), sz)]
    sz = K // N_CHIPS
    return ref.at[pl.ds(pl.multiple_of(q * sz, 16), sz), :]


def _shard_half_shape(name):
    K, N, ax = BIG[name]
    return (K // 2, N // N_CHIPS) if ax == 1 else (K // N_CHIPS, N // 2)


def _shard_shape(name):
    K, N, ax = BIG[name]
    return (K, N // N_CHIPS) if ax == 1 else (K // N_CHIPS, N)


def _pair_copies(names, srcs, lands, send_sems, recv_sems):
    x, y, c = _position()
    for idx, (name, src, land) in enumerate(zip(names, srcs, lands)):
        yield pltpu.make_async_remote_copy(
            src_ref=_half(src, name, 1 - c), dst_ref=land, send_sem=send_sems.at[idx], recv_sem=recv_sems.at[idx],
            device_id=(x, y, 1 - c), device_id_type=MESH_ID)


def _pair_exchange_start(tag, tensors):
    names = [n for n, _ in tensors]
    n = len(tensors)
    lands = [lax.empty(_half_shape(nm), f32) for nm in names]

    def body(*refs):
        for cp in _pair_copies(names, refs[:n], refs[n:2 * n], refs[2 * n], refs[2 * n + 1]):
            cp.start()
        refs[-1][...] = jnp.zeros_like(refs[-1])

    args = [g for _, g in tensors] + lands
    outs = pl.pallas_call(
        body, name="grad_pair_start_" + tag,
        out_shape=(pltpu.SemaphoreType.DMA((n,)), pltpu.SemaphoreType.DMA((n,)))
        + tuple(pltpu.HBM(a.shape, a.dtype) for a in args) + (jax.ShapeDtypeStruct((SUBLANES, LANES), f32),),
        in_specs=(HBM,) * (2 * n), out_specs=(SEM, SEM) + (HBM,) * (2 * n) + (pl.BlockSpec(memory_space=pltpu.VMEM),),
        input_output_aliases={t: 2 + t for t in range(2 * n)},
        compiler_params=pltpu.CompilerParams(has_side_effects=pltpu.SideEffectType.DATAFLOW_SIDE_EFFECTING),
    )(*[pltpu.with_memory_space_constraint(a, pltpu.HBM) for a in args])
    return (tag, names, outs[0], outs[1], outs[2:2 + 2 * n]), outs[-1]


def _pair_exchange_wait(state, after):
    tag, names, send_sems, recv_sems, bufs = state
    n = len(names)

    def body(*refs):
        for cp in _pair_copies(names, refs[:n], refs[n:2 * n], refs[2 * n], refs[2 * n + 1]):
            cp.wait_send()
            cp.wait_recv()

    outs = pl.pallas_call(
        body, name="grad_pair_wait_" + tag, out_shape=tuple(pltpu.HBM(a.shape, a.dtype) for a in bufs),
        in_specs=(HBM,) * (2 * n) + (SEM, SEM, ANY), out_specs=(HBM,) * (2 * n),
        input_output_aliases={t: t for t in range(2 * n)},
        compiler_params=pltpu.CompilerParams(has_side_effects=pltpu.SideEffectType.DATAFLOW_SIDE_EFFECTING),
    )(*bufs, send_sems, recv_sems, after)
    return list(zip(names, outs[:n], outs[n:]))


def _pair_add(g, rcv, name, c_arr):
    K, N, ax = BIG[name]
    hr, hc = _half_shape(name)
    T = 128
    nrt = hr // T

    def body(c_ref, g_ref, r_ref, o_ref):
        o_ref[...] = (g_ref[...] + r_ref[...]).astype(bf16)

    if ax == 1:
        g_spec = pl.BlockSpec((T, hc), lambda i, c: (c[0] * nrt + i, 0))
    else:
        g_spec = pl.BlockSpec((T, hc), lambda i, c: (i, c[0]))
    plain = pl.BlockSpec((T, hc), lambda i, c: (i, 0))
    return pl.pallas_call(
        body, name="grad_pair_add", out_shape=jax.ShapeDtypeStruct((hr, hc), bf16),
        grid_spec=pltpu.PrefetchScalarGridSpec(num_scalar_prefetch=1, grid=(nrt,), in_specs=[g_spec, plain],
                                               out_specs=plain),
        compiler_params=_cp(("parallel",), 32),
    )(c_arr, g, rcv)


def _chip_copies(names, srcs, lands, send_sems, recv_sems):
    x, y, c = _position()
    me = 2 * x + y
    idx = 0
    for name, src, land in zip(names, srcs, lands):
        for px, py in _other_chips(x, y):
            def copy(q, row, name=name, src=src, land=land, px=px, py=py, idx=idx):
                return pltpu.make_async_remote_copy(
                    src_ref=_shard_of_half(src, name, q), dst_ref=land.at[row], send_sem=send_sems.at[idx],
                    recv_sem=recv_sems.at[idx], device_id=(px, py, c), device_id_type=MESH_ID)
            yield copy(2 * px + py, me), copy(me, 2 * px + py)
            idx += 1


def _chip_exchange_start(tag, tensors):
    names = [n for n, _ in tensors]
    n = len(tensors)
    lands = [lax.empty((N_CHIPS,) + _shard_half_shape(nm), g.dtype) for nm, g in tensors]

    def body(*refs):
        send_sems, recv_sems = refs[2 * n:2 * n + 2]
        for sent, _ in _chip_copies(names, refs[:n], refs[n:2 * n], send_sems, recv_sems):
            sent.start()
        refs[-1][...] = jnp.zeros_like(refs[-1])

    args = [g for _, g in tensors] + lands
    outs = pl.pallas_call(
        body, name="grad_chip_start_" + tag,
        out_shape=(pltpu.SemaphoreType.DMA((3 * n,)), pltpu.SemaphoreType.DMA((3 * n,)))
        + tuple(pltpu.HBM(a.shape, a.dtype) for a in args) + (jax.ShapeDtypeStruct((SUBLANES, LANES), f32),),
        in_specs=(HBM,) * (2 * n), out_specs=(SEM, SEM) + (HBM,) * (2 * n) + (pl.BlockSpec(memory_space=pltpu.VMEM),),
        input_output_aliases={t: 2 + t for t in range(2 * n)},
        compiler_params=pltpu.CompilerParams(has_side_effects=pltpu.SideEffectType.DATAFLOW_SIDE_EFFECTING),
    )(*[pltpu.with_memory_space_constraint(a, pltpu.HBM) for a in args])
    return (tag, names, outs[0], outs[1], outs[2:2 + 2 * n]), outs[-1]


def _chip_exchange_wait(state, after):
    tag, names, send_sems, recv_sems, bufs = state
    n = len(names)

    def body(*refs):
        for sent, landed in _chip_copies(names, refs[:n], refs[n:2 * n], refs[2 * n], refs[2 * n + 1]):
            sent.wait_send()
            landed.wait_recv()

    outs = pl.pallas_call(
        body, name="grad_chip_wait_" + tag, out_shape=tuple(pltpu.HBM(a.shape, a.dtype) for a in bufs),
        in_specs=(HBM,) * (2 * n) + (SEM, SEM, ANY), out_specs=(HBM,) * (2 * n),
        input_output_aliases={t: t for t in range(2 * n)},
        compiler_params=pltpu.CompilerParams(has_side_effects=pltpu.SideEffectType.DATAFLOW_SIDE_EFFECTING),
    )(*bufs, send_sems, recv_sems, after)
    return list(zip(names, outs[:n], outs[n:]))


def _sum_chips(name, half, land, chip_arr):
    K, N, ax = BIG[name]
    R, C = _shard_half_shape(name)
    T = 64
    nrt = R // T

    def body(p_ref, own_ref, land_ref, o_ref):
        parts = [jnp.where(p_ref[0] == q, own_ref[...], land_ref[q]).astype(f32) for q in range(N_CHIPS)]
        o_ref[...] = ((parts[0] + parts[1]) + parts[2]) + parts[3]

    if ax == 1:
        own_spec = pl.BlockSpec((T, C), lambda i, p: (i, p[0]))
    else:
        own_spec = pl.BlockSpec((T, C), lambda i, p: (p[0] * nrt + i, 0))
    return pl.pallas_call(
        body, name="grad_sum_chips", out_shape=jax.ShapeDtypeStruct((R, C), f32),
        grid_spec=pltpu.PrefetchScalarGridSpec(
            num_scalar_prefetch=1, grid=(nrt,),
            in_specs=[own_spec, pl.BlockSpec((N_CHIPS, T, C), lambda i, p: (0, i, 0))],
            out_specs=pl.BlockSpec((T, C), lambda i, p: (i, 0))),
        compiler_params=_cp(("parallel",), 32),
    )(chip_arr, half, land)


def _pair_swap(halves):
    n_t = len(halves)

    def body(*refs):
        ins = refs[:n_t]
        outs = refs[n_t:2 * n_t]
        send_sems, recv_sems = refs[2 * n_t:]
        x, y, c = _position()
        cps = []
        for t in range(n_t):
            cp = pltpu.make_async_remote_copy(
                src_ref=ins[t], dst_ref=outs[t], send_sem=send_sems.at[t], recv_sem=recv_sems.at[t],
                device_id=(x, y, 1 - c), device_id_type=MESH_ID)
            cp.start()
            cps.append(cp)
        for cp in cps:
            cp.wait()

    return pl.pallas_call(
        body, name="grad_pair_swap", out_shape=tuple(jax.ShapeDtypeStruct(h.shape, h.dtype) for h in halves),
        in_specs=[ANY] * n_t, out_specs=tuple([ANY] * n_t),
        scratch_shapes=[pltpu.SemaphoreType.DMA((n_t,)), pltpu.SemaphoreType.DMA((n_t,))],
    )(*halves)


def _adamw_halves(own, other, w, m, v, name, l, c_arr, prev):
    K, N, ax = BIG[name]
    R, C = _shard_shape(name)
    hr, hc = _shard_half_shape(name)
    T = 64
    nrt = hr // T
    c1 = 1.0 / (1.0 - ADAM_B1 ** ADAM_STEP)
    c2 = 1.0 / (1.0 - ADAM_B2 ** ADAM_STEP)

    def body(c_ref, own_ref, oth_ref, w_ref, m_ref, v_ref, *rest):
        g_ref, d_ref, nm_ref, nv_ref = rest[-4:]
        gg = jnp.where(pl.program_id(0) == c_ref[0], own_ref[...], oth_ref[...])
        nm = ADAM_B1 * m_ref[...] + (1.0 - ADAM_B1) * gg
        nv = ADAM_B2 * v_ref[...] + (1.0 - ADAM_B2) * (gg * gg)
        g_ref[...] = gg
        nm_ref[...] = nm
        nv_ref[...] = nv
        d_ref[...] = -ADAM_LR * ((nm * c1) / (jnp.sqrt(nv * c2) + ADAM_EPS) + ADAM_WD * w_ref[...])

    half = pl.BlockSpec((T, hc), lambda h, i, c: (i, 0))
    if ax == 1:
        full = pl.BlockSpec((None, T, hc), lambda h, i, c: (l, h * nrt + i, 0))
    else:
        full = pl.BlockSpec((None, T, hc), lambda h, i, c: (l, i, h))
    sd = jax.ShapeDtypeStruct((DEPTH, R, C), f32)
    args = [c_arr, own, other, w, m, v]
    in_specs = [half, half, full, full, full]
    aliases = {}
    if prev is not None:
        args += list(prev)
        in_specs += [ANY] * 4
        aliases = {6 + k: k for k in range(4)}
    return pl.pallas_call(
        body, name="adamw_" + name, out_shape=(sd, sd, sd, sd),
        grid_spec=pltpu.PrefetchScalarGridSpec(num_scalar_prefetch=1, grid=(2, nrt), in_specs=in_specs,
                                               out_specs=(full, full, full, full)),
        input_output_aliases=aliases,
        compiler_params=_cp(("arbitrary", "arbitrary"), 32),
    )(*args)


class _GradExchange:
    GROUPS = (("l1", tuple((n, DEPTH - 1) for n in BIG)),
              ("l0_ffn", (("ffn_w_down", 0), ("ffn_w_up", 0))),
              ("l0_mix", (("w_out", 0), ("w_in", 0))))

    def __init__(self):
        self.c_arr = jnp.reshape(lax.axis_index("c"), (1,)).astype(jnp.int32)
        self.chip_arr = jnp.reshape(2 * lax.axis_index("x") + lax.axis_index("y"), (1,)).astype(jnp.int32)
        self.grads = {}
        self.pair_started = {}
        self.chip_started = {}

    def _advance(self, after, tok):
        for tag, _ in self.GROUPS:
            if tag not in self.pair_started or tag in self.chip_started:
                continue
            arrived = _pair_exchange_wait(self.pair_started[tag], after)
            pair = [(n, _pair_add(g, r, n, self.c_arr)) for n, g, r in arrived]
            self.chip_started[tag], token = _chip_exchange_start(tag, pair)
            tok = tok + token[0, 0]
        return tok

    def put(self, name, layer, g, tok):
        self.grads[(name, layer)] = g
        tok = self._advance(g, tok)
        for tag, keys in self.GROUPS:
            if tag in self.pair_started or not all(k in self.grads for k in keys):
                continue
            self.pair_started[tag], token = _pair_exchange_start(tag, [(n, self.grads[(n, l)]) for n, l in keys])
            tok = tok + token[0, 0]
        return tok

    def finish(self, after):
        self._advance(after, jnp.zeros((), f32))
        keys, own = [], []
        for tag, group in self.GROUPS:
            landed = _chip_exchange_wait(self.chip_started[tag], after)
            own += [_sum_chips(n, half, land, self.chip_arr) for n, half, land in landed]
            keys += list(group)
        other = _pair_swap(own)
        return dict(zip(keys, zip(own, other)))


def _small_allreduce(buf):
    R = buf.shape[0]

    def body(in_ref, out_ref, sibling, slots, send_sems, recv_sems):
        x, y, c = _position()
        me = 2 * x + y
        swap = pltpu.make_async_remote_copy(
            src_ref=in_ref, dst_ref=sibling, send_sem=send_sems.at[0], recv_sem=recv_sems.at[0],
            device_id=(x, y, 1 - c), device_id_type=MESH_ID)
        swap.start()
        swap.wait()
        slots[me] = in_ref[...] + sibling[...]
        cps = []
        for k, (px, py) in enumerate(_other_chips(x, y)):
            cp = pltpu.make_async_remote_copy(
                src_ref=slots.at[me], dst_ref=slots.at[me], send_sem=send_sems.at[1 + k], recv_sem=recv_sems.at[1 + k],
                device_id=(px, py, c), device_id_type=MESH_ID)
            cp.start()
            cps.append(cp)
        for k, (px, py) in enumerate(_other_chips(x, y)):
            pltpu.make_async_remote_copy(
                src_ref=slots.at[me], dst_ref=slots.at[2 * px + py], send_sem=send_sems.at[1 + k],
                recv_sem=recv_sems.at[1 + k], device_id=(px, py, c), device_id_type=MESH_ID).wait_recv()
        for cp in cps:
            cp.wait_send()
        out_ref[...] = ((slots[0] + slots[1]) + slots[2]) + slots[3]

    vm = pl.BlockSpec(memory_space=pltpu.VMEM)
    return pl.pallas_call(
        body, name="small_allreduce", out_shape=jax.ShapeDtypeStruct((R, 128), f32), in_specs=[vm], out_specs=vm,
        scratch_shapes=[pltpu.VMEM((R, 128), f32), pltpu.VMEM((N_CHIPS, R, 128), f32),
                        pltpu.SemaphoreType.DMA((N_CHIPS,)), pltpu.SemaphoreType.DMA((N_CHIPS,))],
        compiler_params=pltpu.CompilerParams(vmem_limit_bytes=40 * MIB),
    )(buf)


PACK_UNIT = 1024


def _pack(arrs):
    parts = []
    for a in arrs:
        flat = a.reshape(-1)
        n = -(-flat.shape[0] // PACK_UNIT) * PACK_UNIT
        parts.append(jnp.pad(flat, (0, n - flat.shape[0])))
    return jnp.concatenate(parts).reshape(-1, 128)


def _unpack(buf, shapes):
    flat = buf.reshape(-1)
    out, off = [], 0
    for shp in shapes:
        n = int(np.prod(shp))
        out.append(flat[off:off + n].reshape(shp))
        off += -(-n // PACK_UNIT) * PACK_UNIT
    return out


def kernel(x, w_in, b_in, conv_dw_w, conv_dw_b, conv_ln_g, conv_ln_b, rel_bias_table, gmlp_ln_g, gmlp_ln_b, gmlp_w_s, gmlp_b_s, w_out, b_out, ln1_g, ln1_b, ffn_w_up, ffn_b_up, ffn_conv_w, ffn_conv_b, ffn_w_down, ffn_b_down, ln2_g, ln2_b, loss_target, m_w_in, m_b_in, m_conv_dw_w, m_conv_dw_b, m_conv_ln_g, m_conv_ln_b, m_rel_bias_table, m_gmlp_ln_g, m_gmlp_ln_b, m_gmlp_w_s, m_gmlp_b_s, m_w_out, m_b_out, m_ln1_g, m_ln1_b, m_ffn_w_up, m_ffn_b_up, m_ffn_conv_w, m_ffn_conv_b, m_ffn_w_down, m_ffn_b_down, m_ln2_g, m_ln2_b, v_w_in, v_b_in, v_conv_dw_w, v_conv_dw_b, v_conv_ln_g, v_conv_ln_b, v_rel_bias_table, v_gmlp_ln_g, v_gmlp_ln_b, v_gmlp_w_s, v_gmlp_b_s, v_w_out, v_b_out, v_ln1_g, v_ln1_b, v_ffn_w_up, v_ffn_b_up, v_ffn_conv_w, v_ffn_conv_b, v_ffn_w_down, v_ffn_b_down, v_ln2_g, v_ln2_b):
    w = dict(w_in=w_in, b_in=b_in, conv_dw_w=conv_dw_w, conv_dw_b=conv_dw_b, conv_ln_g=conv_ln_g, conv_ln_b=conv_ln_b,
             rel_bias_table=rel_bias_table, gmlp_ln_g=gmlp_ln_g, gmlp_ln_b=gmlp_ln_b, gmlp_w_s=gmlp_w_s,
             gmlp_b_s=gmlp_b_s, w_out=w_out, b_out=b_out, ln1_g=ln1_g, ln1_b=ln1_b, ffn_w_up=ffn_w_up,
             ffn_b_up=ffn_b_up, ffn_conv_w=ffn_conv_w, ffn_conv_b=ffn_conv_b, ffn_w_down=ffn_w_down,
             ffn_b_down=ffn_b_down, ln2_g=ln2_g, ln2_b=ln2_b)
    m = dict(w_in=m_w_in, b_in=m_b_in, conv_dw_w=m_conv_dw_w, conv_dw_b=m_conv_dw_b, conv_ln_g=m_conv_ln_g,
             conv_ln_b=m_conv_ln_b, rel_bias_table=m_rel_bias_table, gmlp_ln_g=m_gmlp_ln_g, gmlp_ln_b=m_gmlp_ln_b,
             gmlp_w_s=m_gmlp_w_s, gmlp_b_s=m_gmlp_b_s, w_out=m_w_out, b_out=m_b_out, ln1_g=m_ln1_g, ln1_b=m_ln1_b,
             ffn_w_up=m_ffn_w_up, ffn_b_up=m_ffn_b_up, ffn_conv_w=m_ffn_conv_w, ffn_conv_b=m_ffn_conv_b,
             ffn_w_down=m_ffn_w_down, ffn_b_down=m_ffn_b_down, ln2_g=m_ln2_g, ln2_b=m_ln2_b)
    v = dict(w_in=v_w_in, b_in=v_b_in, conv_dw_w=v_conv_dw_w, conv_dw_b=v_conv_dw_b, conv_ln_g=v_conv_ln_g,
             conv_ln_b=v_conv_ln_b, rel_bias_table=v_rel_bias_table, gmlp_ln_g=v_gmlp_ln_g, gmlp_ln_b=v_gmlp_ln_b,
             gmlp_w_s=v_gmlp_w_s, gmlp_b_s=v_gmlp_b_s, w_out=v_w_out, b_out=v_b_out, ln1_g=v_ln1_g, ln1_b=v_ln1_b,
             ffn_w_up=v_ffn_w_up, ffn_b_up=v_ffn_b_up, ffn_conv_w=v_ffn_conv_w, ffn_conv_b=v_ffn_conv_b,
             ffn_w_down=v_ffn_w_down, ffn_b_down=v_ffn_b_down, ln2_g=v_ln2_g, ln2_b=v_ln2_b)

    chip_arr = jnp.reshape(2 * lax.axis_index("x") + lax.axis_index("y"), (1,)).astype(jnp.int32)
    shards = {"w_in": _cast_bf16(w_in.reshape(-1, w_in.shape[-1])).reshape(w_in.shape)}
    wb, conv_stack, fconv_stack = _gather_weights(shards, conv_dw_w, ffn_conv_w)
    send_sems, recv_sems, in_flight, token = _gather_start(
        [_cast_into_full(w[n], n, chip_arr) for n in LATE_WEIGHTS], conv_stack)
    sp = {n: w[n] for n in SMALL}
    sp["conv_dw_w"] = jnp.moveaxis(conv_stack, 0, 2).reshape(DEPTH, CONV_WIDTH, CONV_CH)
    sp["ffn_conv_w"] = jnp.moveaxis(fconv_stack, 0, 2).reshape(DEPTH, FFN_CONV_WIDTH, 2 * D_FF)
    sp["b_in"] = sp["b_in"] + token[0, 0]

    def late_weights(after):
        return dict(zip(LATE_WEIGHTS, _gather_wait(send_sems, recv_sems, in_flight, after)))

    sink = _GradExchange()
    loss_local, grad_x, grads, big = _local_step(x[0], loss_target[0], wb, late_weights, sp, sink)

    small_shapes = [(1,)] + [grads[n].shape for n in SMALL]
    summed = _unpack(_small_allreduce(_pack([loss_local.reshape(1)] + [grads[n] for n in SMALL])), small_shapes)
    loss = summed[0].reshape(())
    small = dict(zip(SMALL, summed[1:]))
    chip = 2 * lax.axis_index("x") + lax.axis_index("y")
    for n in SMALL_SHARDED:
        width = w[n].shape[-1]
        small[n] = lax.dynamic_slice_in_dim(small[n], chip * width, width, axis=2)

    g_out, d_out, m_out, v_out = {}, {}, {}, {}
    for n in BIG:
        outs = None
        for l in range(DEPTH):
            own, other = big[(n, l)]
            outs = _adamw_halves(own, other, w[n], m[n], v[n], n, l, sink.c_arr, outs)
        g_out[n], d_out[n], m_out[n], v_out[n] = outs
    shapes = [small[n].shape for n in SMALL]
    packed = [_pack([src[n] for n in SMALL]) for src in (small, w, m, v)]
    upd = _adamw(*packed, "adamw_small")
    for dst, buf in zip((d_out, m_out, v_out), upd):
        dst.update(zip(SMALL, _unpack(buf, shapes)))
    g_out.update(small)

    return (loss, grad_x[None], *[g_out[n] for n in WEIGHTS], *[d_out[n] for n in WEIGHTS],
            *[m_out[n] for n in WEIGHTS], *[v_out[n] for n in WEIGHTS])
```

```python
import functools
import math

import numpy as np
import jax
import jax.numpy as jnp
from jax import lax
from jax.experimental import pallas as pl
from jax.experimental.pallas import tpu as pltpu

f32 = jnp.float32
bf16 = jnp.bfloat16

D_MODEL = 1024
DEPTH = 2
HEAD_DIM = 64
CONV_CH = 256
CONV_WIDTH = 31
ATTN_HEADS = 8
ATTN_CH = ATTN_HEADS * HEAD_DIM
DILATIONS = (1, 4, 16)
ATTN_BLOCK = 128
N_BUCKETS = 32
MAX_DISTANCE = 2048
GMLP_CH = 256
GMLP_GROUPS = 4
GMLP_GROUP_DIM = GMLP_CH // GMLP_GROUPS
CHUNK = 128
IN_CH = 2 * CONV_CH + 3 * ATTN_CH + 2 * GMLP_CH
D_FF = 2816
FFN_CONV_WIDTH = 3
LN_EPS = 1e-5
ALPHA = (2.0 * DEPTH) ** 0.25
ADAM_LR = 0.001
ADAM_B1 = 0.9
ADAM_B2 = 0.999
ADAM_EPS = 1e-08
ADAM_WD = 0.01
ADAM_STEP = 10

CONV_HALO = 32
FFN_HALO = 8
NEG = -1e30
MIB = 2 ** 20
NT_DIMS = (((1,), (1,)), ((), ()))
TN_DIMS = (((0,), (0,)), ((), ()))
MESH_ID = pl.DeviceIdType.MESH


def _cp(sem, vmem_mib):
    return pltpu.CompilerParams(dimension_semantics=sem, vmem_limit_bytes=vmem_mib * MIB)


def _resident(shape):
    nd = len(shape)
    return pl.BlockSpec(shape, lambda *_: (0,) * nd, pipeline_mode=pl.Buffered(1))


def _acc(shape):
    nd = len(shape)
    return pl.BlockSpec(shape, lambda *_: (0,) * nd)


def _sig(x):
    return 1.0 / (1.0 + jnp.exp(-x))


def _ln_stats(z):
    mu = jnp.mean(z, axis=-1, keepdims=True)
    zc = z - mu
    var = jnp.mean(zc * zc, axis=-1, keepdims=True)
    rstd = lax.rsqrt(var + LN_EPS)
    return zc * rstd, rstd


def _ln_bwd(dy, xhat, rstd, g):
    dxh = dy * g
    m1 = jnp.mean(dxh, axis=-1, keepdims=True)
    m2 = jnp.mean(dxh * xhat, axis=-1, keepdims=True)
    return rstd * (dxh - m1 - xhat * m2)


def _colsum(x):
    return jnp.sum(x, axis=0, keepdims=True)


def _t5_bucket_np(dist):
    max_exact = N_BUCKETS // 2
    dd = np.maximum(dist, 1).astype(np.float64)
    large = max_exact + (np.log(dd / max_exact) / math.log(MAX_DISTANCE / max_exact)
                         * (N_BUCKETS - max_exact)).astype(np.int32)
    large = np.minimum(large, N_BUCKETS - 1)
    return np.where(dist < max_exact, dist, large).astype(np.int32)


def _bucket_ids():
    qi = np.arange(ATTN_BLOCK)[:, None]
    kj = np.arange(2 * ATTN_BLOCK)[None, :]
    dist = np.clip(qi + ATTN_BLOCK - kj, 0, None)
    return np.stack([_t5_bucket_np(dist * d) for d in DILATIONS]).astype(np.int32)


LANES = 128
QKV_CH = 3 * ATTN_CH
PERM_TILE = 512


def _slabs(n, rows):
    return [pltpu.VMEM((rows, LANES), f32)] * n


def _rows_of(slab, r, n, d):
    return slab[...] if d == 1 else slab[pl.ds(r, n, stride=d), :]


def _set_rows_of(slab, r, n, d, val):
    if d == 1:
        slab[...] = val
    else:
        slab[pl.ds(r, n, stride=d), :] = val


def _perm_spec(d, ch):
    return pl.BlockSpec((d, PERM_TILE // d, ch), lambda i: (0, i, 0))


def _perm_shape(S, d, ch, dtype):
    return jax.ShapeDtypeStruct((d, S // d, ch), dtype)


def _inproj_fwd(x, w, b):
    S = x.shape[0]
    T = PERM_TILE
    nsl = QKV_CH // LANES

    def body(x_ref, w_ref, b_ref, a_ref, c_ref, *rest):
        q_refs = rest[:len(DILATIONS)]
        slabs = rest[len(DILATIONS):]
        h = jnp.dot(x_ref[...].astype(bf16), w_ref[...], preferred_element_type=f32) + b_ref[...]
        a_ref[...] = h[:, :2 * CONV_CH]
        q0 = 2 * CONV_CH
        c_ref[...] = h[:, q0 + QKV_CH:]
        for j in range(nsl):
            piece = h[:, q0 + LANES * j:q0 + LANES * (j + 1)]
            if LANES * j < ATTN_CH:
                piece = piece * (HEAD_DIM ** -0.5)
            slabs[j][...] = piece
        for d, q_ref in zip(DILATIONS, q_refs):
            for r in range(d):
                for j in range(nsl):
                    q_ref[r, :, LANES * j:LANES * (j + 1)] = _rows_of(slabs[j], r, T // d, d).astype(bf16)

    row = lambda c: pl.BlockSpec((T, c), lambda i: (i, 0))
    return pl.pallas_call(
        body, grid=(S // T,), name="inproj_fwd",
        out_shape=(jax.ShapeDtypeStruct((S, 2 * CONV_CH), f32), jax.ShapeDtypeStruct((S, 2 * GMLP_CH), f32))
        + tuple(_perm_shape(S, d, QKV_CH, bf16) for d in DILATIONS),
        in_specs=[row(D_MODEL), _resident((D_MODEL, IN_CH)), _resident((1, IN_CH))],
        out_specs=(row(2 * CONV_CH), row(2 * GMLP_CH)) + tuple(_perm_spec(d, QKV_CH) for d in DILATIONS),
        scratch_shapes=_slabs(nsl, T),
        compiler_params=_cp(("parallel",), 48),
    )(x, w, b)


CONV_GROUP = 64


def _window_rolls(starts):
    groups = {}
    for s in starts:
        groups.setdefault((-s) % SUBLANES, []).append(s)
    return dict(sorted(groups.items()))


def _conv_fwd(a_in, dw_w, dw_b, ln_g, ln_b):
    S = a_in.shape[0]
    T = 512
    hb = T // CONV_HALO

    def body(a_ref, halo_ref, w_ref, b_ref, g_ref, be_ref, out_ref, hc_ref, buf):
        i = pl.program_id(0)
        am = a_ref[...]
        ah = halo_ref[...]
        hgh = ah[:, :CONV_CH] * _sig(ah[:, CONV_CH:])
        buf[0:CONV_HALO, :] = jnp.where(i > 0, hgh, 0.0)
        buf[CONV_HALO:, :] = am[:, :CONV_CH] * _sig(am[:, CONV_CH:])
        starts = _window_rolls(range(CONV_HALO - (CONV_WIDTH - 1), CONV_HALO + 1))
        slabs = [slice(LANES * j, LANES * (j + 1)) for j in range(CONV_CH // LANES)]

        def step(g, _):
            r0 = pl.multiple_of(g * CONV_GROUP, CONV_GROUP)
            rows = pl.ds(r0, CONV_GROUP)
            for cs in slabs:
                ext = buf[pl.ds(r0, CONV_GROUP + CONV_HALO), cs]
                acc = jnp.broadcast_to(b_ref[:, cs], (CONV_GROUP, LANES))
                for b, ss in starts.items():
                    rolled = ext if b == 0 else pltpu.roll(ext, b, 0)
                    for s in ss:
                        k = s - (CONV_HALO - (CONV_WIDTH - 1))
                        acc = acc + w_ref[k:k + 1, cs] * rolled[s + b:s + b + CONV_GROUP]
                hc_ref[rows, cs] = acc
            return 0

        lax.fori_loop(0, T // CONV_GROUP, step, 0)
        xhat, _ = _ln_stats(hc_ref[...])
        y = xhat * g_ref[...] + be_ref[...]
        out_ref[...] = (y * _sig(y)).astype(bf16)

    return pl.pallas_call(
        body, grid=(S // T,), name="conv_fwd",
        out_shape=(jax.ShapeDtypeStruct((S, CONV_CH), bf16), jax.ShapeDtypeStruct((S, CONV_CH), f32)),
        in_specs=[pl.BlockSpec((T, 2 * CONV_CH), lambda i: (i, 0)),
                  pl.BlockSpec((CONV_HALO, 2 * CONV_CH), lambda i: (jnp.maximum(i * hb - 1, 0), 0)),
                  _acc((32, CONV_CH)), _acc((1, CONV_CH)), _acc((1, CONV_CH)), _acc((1, CONV_CH))],
        out_specs=(pl.BlockSpec((T, CONV_CH), lambda i: (i, 0)), pl.BlockSpec((T, CONV_CH), lambda i: (i, 0))),
        scratch_shapes=[pltpu.VMEM((T + CONV_HALO, CONV_CH), f32)],
        compiler_params=_cp(("parallel",), 32),
    )(a_in, a_in, dw_w, dw_b, ln_g, ln_b)


def _bias_build(table, buckets):
    def body(t_ref, bk_ref, o_ref):
        h = pl.program_id(1)
        ids = bk_ref[0]
        acc = jnp.zeros((ATTN_BLOCK, 2 * ATTN_BLOCK), f32)
        for b in range(N_BUCKETS):
            acc = jnp.where(ids == b, t_ref[b, h], acc)
        o_ref[0, 0] = acc

    return pl.pallas_call(
        body, grid=(len(DILATIONS), ATTN_HEADS), name="bias_build",
        out_shape=jax.ShapeDtypeStruct((len(DILATIONS), ATTN_HEADS, ATTN_BLOCK, 2 * ATTN_BLOCK), f32),
        in_specs=[pl.BlockSpec(memory_space=pltpu.SMEM),
                  pl.BlockSpec((1, ATTN_BLOCK, 2 * ATTN_BLOCK), lambda p, h: (p, 0, 0))],
        out_specs=pl.BlockSpec((1, 1, ATTN_BLOCK, 2 * ATTN_BLOCK), lambda p, h: (p, h, 0, 0)),
        compiler_params=_cp(("arbitrary", "arbitrary"), 16),
    )(table, buckets)


def _head_tile(tile, h, col):
    lane_head = lax.broadcasted_iota(jnp.int32, tile.shape, 1) // 16
    return jnp.where(lane_head == h, col, tile)


HEAD_PAIRS = ATTN_HEADS // 2
UNITS_PER_BLOCK = ATTN_HEADS


def _attn_tile(L):
    return min(512, L)


def _band_mask(first_block, n):
    B = ATTN_BLOCK
    row = lax.broadcasted_iota(jnp.int32, (B, 2 * B), 0)
    col = lax.broadcasted_iota(jnp.int32, (B, 2 * B), 1)
    valid = (col >= row) & (col <= row + B)
    if first_block:
        valid = valid & ((col >= B) | (n > 0))
    return valid


def _head_lanes(a):
    lane = lax.broadcasted_iota(jnp.int32, (ATTN_BLOCK, LANES), 1)
    return (lane < HEAD_DIM) if a == 0 else (lane >= HEAD_DIM)


def _pair_keys(cur_ref, halo_ref, part, b, j):
    B = ATTN_BLOCK
    c0 = part * ATTN_CH + LANES * j
    own = cur_ref[B * b:B * (b + 1), c0:c0 + LANES]
    prev = halo_ref[:, LANES * j:LANES * (j + 1)] if b == 0 else cur_ref[B * (b - 1):B * b, c0:c0 + LANES]
    return jnp.concatenate([prev, own], axis=0)


def _attn_fwd_pattern(qkv, bias, d):
    _, L, _ = qkv.shape
    B = ATTN_BLOCK
    QB = _attn_tile(L)
    nsb = QB // B
    U = nsb * UNITS_PER_BLOCK

    def body(cur_ref, hk_ref, hv_ref, b_ref, o_ref, lse_ref, lg, pb):
        n = pl.program_id(1)
        for b in range(nsb):
            valid = _band_mask(b == 0, n)
            for j in range(HEAD_PAIRS):
                q2 = cur_ref[B * b:B * (b + 1), LANES * j:LANES * (j + 1)]
                k2 = _pair_keys(cur_ref, hk_ref, 1, b, j)
                for a in range(2):
                    u = (b * HEAD_PAIRS + j) * 2 + a
                    qm = jnp.where(_head_lanes(a), q2, jnp.zeros_like(q2))
                    logits = lax.dot_general(qm, k2, NT_DIMS, preferred_element_type=f32) + b_ref[2 * j + a]
                    lg[B * u:B * (u + 1), :] = jnp.where(valid, logits, NEG)
        m = jnp.max(lg[...], axis=1, keepdims=True)
        p = jnp.exp(lg[...] - m)
        s = jnp.sum(p, axis=1, keepdims=True)
        pb[...] = p.astype(bf16)
        lse = m + jnp.log(s)
        inv = 1.0 / s
        for b in range(nsb):
            tile = jnp.zeros((B, B), f32)
            for j in range(HEAD_PAIRS):
                v2 = _pair_keys(cur_ref, hv_ref, 2, b, j)
                outs = []
                for a in range(2):
                    u = (b * HEAD_PAIRS + j) * 2 + a
                    rows = slice(B * u, B * (u + 1))
                    outs.append(jnp.dot(pb[rows, :], v2, preferred_element_type=f32) * inv[rows])
                    tile = _head_tile(tile, 2 * j + a, lse[rows])
                o_ref[B * b:B * (b + 1), LANES * j:LANES * (j + 1)] = jnp.where(_head_lanes(0), outs[0], outs[1])
            lse_ref[B * b:B * (b + 1), :] = tile

    halo = lambda part: pl.BlockSpec((None, B, ATTN_CH), lambda r, n: (r, jnp.maximum(n * nsb - 1, 0), part))
    tile_spec = lambda c: pl.BlockSpec((None, QB, c), lambda r, n: (r, n, 0))
    return pl.pallas_call(
        body, grid=(d, L // QB), name=f"attn_fwd_d{d}",
        out_shape=(jax.ShapeDtypeStruct((d, L, ATTN_CH), f32), jax.ShapeDtypeStruct((d, L, B), f32)),
        in_specs=[tile_spec(QKV_CH), halo(1), halo(2), _resident((ATTN_HEADS, B, 2 * B))],
        out_specs=(tile_spec(ATTN_CH), tile_spec(B)),
        scratch_shapes=[pltpu.VMEM((U * B, 2 * B), f32), pltpu.VMEM((U * B, 2 * B), bf16)],
        compiler_params=_cp(("parallel", "parallel"), 40),
    )(qkv, qkv, qkv, bias)


def _attn_merge(parts):
    S = parts[0][0].shape[0] * parts[0][0].shape[1]
    T = PERM_TILE
    nsl = ATTN_CH // LANES
    n_p = len(DILATIONS)

    def body(*refs):
        ins = refs[:2 * n_p]
        out_ref, lse_ref = refs[2 * n_p:2 * n_p + 2]
        slabs = refs[2 * n_p + 2:]
        lses = []
        for p, d in enumerate(DILATIONS):
            o_ref, l_ref = ins[2 * p], ins[2 * p + 1]
            osl = slabs[p * (nsl + 1):p * (nsl + 1) + nsl]
            lsl = slabs[p * (nsl + 1) + nsl]
            for r in range(d):
                for j in range(nsl):
                    _set_rows_of(osl[j], r, T // d, d, o_ref[r, :, LANES * j:LANES * (j + 1)])
                _set_rows_of(lsl, r, T // d, d, l_ref[r])
            lses.append(lsl[...])
        big = functools.reduce(jnp.maximum, lses)
        ws = [jnp.exp(l - big) for l in lses]
        tot = functools.reduce(lambda a_, b_: a_ + b_, ws)
        lse_ref[...] = big + jnp.log(tot)
        ws = [w / tot for w in ws]
        for j in range(nsl):
            acc = jnp.zeros((T, LANES), f32)
            for p in range(n_p):
                wa = ws[p][:, 32 * j:32 * j + 1]
                wb = ws[p][:, 32 * j + 16:32 * j + 17]
                lane = lax.broadcasted_iota(jnp.int32, (T, LANES), 1)
                acc = acc + jnp.where(lane < HEAD_DIM, wa, wb) * slabs[p * (nsl + 1) + j][...]
            out_ref[:, LANES * j:LANES * (j + 1)] = acc.astype(bf16)

    in_specs, args = [], []
    for (o, l), d in zip(parts, DILATIONS):
        in_specs += [_perm_spec(d, ATTN_CH), _perm_spec(d, ATTN_BLOCK)]
        args += [o, l]
    row = lambda c: pl.BlockSpec((T, c), lambda i: (i, 0))
    return pl.pallas_call(
        body, grid=(S // T,), name="attn_merge",
        out_shape=(jax.ShapeDtypeStruct((S, ATTN_CH), bf16), jax.ShapeDtypeStruct((S, ATTN_BLOCK), f32)),
        in_specs=in_specs, out_specs=(row(ATTN_CH), row(ATTN_BLOCK)),
        scratch_shapes=_slabs(n_p * (nsl + 1), T),
        compiler_params=_cp(("parallel",), 40),
    )(*args)


def _attn_fwd(qkvs, bias):
    parts = [_attn_fwd_pattern(q, bias[p], d) for p, (q, d) in enumerate(zip(qkvs, DILATIONS))]
    return _attn_merge(parts)


def _tril_bf16(w):
    row = lax.broadcasted_iota(jnp.int32, (CHUNK, CHUNK), 0)
    col = lax.broadcasted_iota(jnp.int32, (CHUNK, CHUNK), 1)
    return jnp.where(col <= row, w, 0.0).astype(bf16)


def _gmlp_fwd(c_in, ln_g, ln_b, w_s, b_s_t):
    S = c_in.shape[0]
    T = 512

    def body(c_ref, g_ref, be_ref, w_ref, bs_ref, out_ref, mix):
        c = c_ref[...]
        xhat, _ = _ln_stats(c[:, GMLP_CH:])
        vb = (xhat * g_ref[...] + be_ref[...]).astype(bf16)
        for g in range(GMLP_GROUPS):
            wt = _tril_bf16(w_ref[g])
            cs = slice(GMLP_GROUP_DIM * g, GMLP_GROUP_DIM * (g + 1))
            for ci in range(T // CHUNK):
                rs = slice(CHUNK * ci, CHUNK * (ci + 1))
                mix[rs, cs] = jnp.dot(wt, vb[rs, cs], preferred_element_type=f32) + bs_ref[:, g:g + 1]
        out_ref[...] = (c[:, :GMLP_CH] * mix[...]).astype(bf16)

    return pl.pallas_call(
        body, grid=(S // T,), name="gmlp_fwd",
        out_shape=jax.ShapeDtypeStruct((S, GMLP_CH), bf16),
        in_specs=[pl.BlockSpec((T, 2 * GMLP_CH), lambda i: (i, 0)), _acc((1, GMLP_CH)), _acc((1, GMLP_CH)),
                  _acc((GMLP_GROUPS, CHUNK, CHUNK)), _acc((CHUNK, GMLP_GROUPS))],
        out_specs=pl.BlockSpec((T, GMLP_CH), lambda i: (i, 0)),
        scratch_shapes=[pltpu.VMEM((T, GMLP_CH), f32)],
        compiler_params=_cp(("parallel",), 32),
    )(c_in, ln_g, ln_b, w_s, b_s_t)


def _outproj_ln_fwd(conv_out, attn_out, gm_out, w, b, x, ln_g, ln_b):
    S = x.shape[0]
    T = 512

    def body(co_ref, ao_ref, go_ref, w_ref, b_ref, x_ref, g_ref, be_ref, cat_ref, z_ref, yb_ref):
        cat = jnp.concatenate([co_ref[...], ao_ref[...], go_ref[...]], axis=1)
        cat_ref[...] = cat
        z = jnp.dot(cat, w_ref[...], preferred_element_type=f32) + b_ref[...] + ALPHA * x_ref[...]
        z_ref[...] = z
        xhat, _ = _ln_stats(z)
        yb_ref[...] = (xhat * g_ref[...] + be_ref[...]).astype(bf16)

    row = lambda c: pl.BlockSpec((T, c), lambda i: (i, 0))
    return pl.pallas_call(
        body, grid=(S // T,), name="outproj_ln_fwd",
        out_shape=(jax.ShapeDtypeStruct((S, D_MODEL), bf16), jax.ShapeDtypeStruct((S, D_MODEL), f32),
                   jax.ShapeDtypeStruct((S, D_MODEL), bf16)),
        in_specs=[row(CONV_CH), row(ATTN_CH), row(GMLP_CH), _resident((D_MODEL, D_MODEL)), _acc((1, D_MODEL)),
                  row(D_MODEL), _acc((1, D_MODEL)), _acc((1, D_MODEL))],
        out_specs=(row(D_MODEL), row(D_MODEL), row(D_MODEL)),
        compiler_params=_cp(("parallel",), 40),
    )(conv_out, attn_out, gm_out, w, b, x, ln_g, ln_b)


GATE_ROWS = 32
GATE_COLS = 128
GATE_MM_COLS = 256
SUBLANES = 8


def _gate_cols(c0):
    return slice(c0, c0 + GATE_COLS), slice(D_FF + c0, D_FF + c0 + GATE_COLS)


def _bcast_rows(ref, k, cs):
    return jnp.broadcast_to(ref[k:k + 1, cs], (GATE_ROWS, GATE_COLS))


def _fold_rows(z):
    acc = z[0:SUBLANES]
    for r in range(SUBLANES, GATE_ROWS, SUBLANES):
        acc = acc + z[r:r + SUBLANES]
    return acc


def _ffn_up_gate_fwd(x1b, w, b, conv_w, conv_b):
    S = x1b.shape[0]
    T = 256
    H = FFN_HALO
    K = FFN_CONV_WIDTH

    def body(x_ref, w_ref, b_ref, cw_ref, cb_ref, hfb_ref, hc_ref, act_ref, hbuf, carry):
        @pl.when(pl.program_id(0) == 0)
        def _():
            carry[...] = jnp.zeros_like(carry)
        x = x_ref[...]
        for m0 in range(0, D_FF, GATE_MM_COLS):
            for cm in (slice(m0, m0 + GATE_MM_COLS), slice(D_FF + m0, D_FF + m0 + GATE_MM_COLS)):
                h = jnp.dot(x, w_ref[:, cm], preferred_element_type=f32) + b_ref[:, cm]
                hbuf[:, cm] = h
                hfb_ref[:, cm] = h.astype(bf16)
            for c0 in range(m0, m0 + GATE_MM_COLS, GATE_COLS):
                cols = _gate_cols(c0)
                wts = [[_bcast_rows(cw_ref, k, cs) for k in range(K)] + [_bcast_rows(cb_ref, 0, cs)] for cs in cols]

                def step(rg, tails, cols=cols, wts=wts):
                    rows = pl.ds(pl.multiple_of(rg * GATE_ROWS, GATE_ROWS), GATE_ROWS)
                    hc, new_tails = [], []
                    for cs, wt, tail in zip(cols, wts, tails):
                        h = hbuf[rows, cs]
                        ext = jnp.concatenate([tail, h], axis=0)
                        acc = wt[K] + wt[K - 1] * h
                        for back in range(1, K):
                            acc = acc + wt[K - 1 - back] * pltpu.roll(ext, back, 0)[H:]
                        hc_ref[rows, cs] = acc
                        hc.append(acc)
                        new_tails.append(h[GATE_ROWS - H:])
                    act_ref[rows, cols[0]] = (hc[0] * _sig(hc[0]) * hc[1]).astype(bf16)
                    return tuple(new_tails)

                tails = lax.fori_loop(0, T // GATE_ROWS, step, tuple(carry[:, cs] for cs in cols), unroll=True)
                for cs, tail in zip(cols, tails):
                    carry[:, cs] = tail

    row = lambda c: pl.BlockSpec((T, c), lambda i: (i, 0))
    return pl.pallas_call(
        body, grid=(S // T,), name="ffn_up_gate_fwd",
        out_shape=(jax.ShapeDtypeStruct((S, 2 * D_FF), bf16), jax.ShapeDtypeStruct((S, 2 * D_FF), f32),
                   jax.ShapeDtypeStruct((S, D_FF), bf16)),
        in_specs=[row(D_MODEL), _resident((D_MODEL, 2 * D_FF)), _acc((1, 2 * D_FF)), _acc((8, 2 * D_FF)),
                  _acc((1, 2 * D_FF))],
        out_specs=(row(2 * D_FF), row(2 * D_FF), row(D_FF)),
        scratch_shapes=[pltpu.VMEM((T, 2 * D_FF), f32), pltpu.VMEM((H, 2 * D_FF), f32)],
        compiler_params=_cp(("arbitrary",), 56),
    )(x1b, w, b, conv_w, conv_b)


def _ffn_down_ln_fwd(act, w, b, z1, ln1_g, ln1_b, ln_g, ln_b):
    S = act.shape[0]
    T = 512

    def body(a_ref, w_ref, b_ref, z1_ref, g1_ref, be1_ref, g_ref, be_ref, z_ref, y_ref):
        subs = [slice(s0, s0 + T // 2) for s0 in (0, T // 2)]
        zs = [jnp.dot(a_ref[rs, :], w_ref[...], preferred_element_type=f32) + b_ref[...]
              + ALPHA * (_ln_stats(z1_ref[rs, :])[0] * g1_ref[...] + be1_ref[...]) for rs in subs]
        for rs, z in zip(subs, zs):
            z_ref[rs, :] = z
            xhat, _ = _ln_stats(z)
            y_ref[rs, :] = xhat * g_ref[...] + be_ref[...]

    row = lambda c: pl.BlockSpec((T, c), lambda i: (i, 0))
    return pl.pallas_call(
        body, grid=(S // T,), name="ffn_down_ln_fwd",
        out_shape=(jax.ShapeDtypeStruct((S, D_MODEL), f32), jax.ShapeDtypeStruct((S, D_MODEL), f32)),
        in_specs=[row(D_FF), _resident((D_FF, D_MODEL)), _acc((1, D_MODEL)), row(D_MODEL)] + [_acc((1, D_MODEL))] * 4,
        out_specs=(row(D_MODEL), row(D_MODEL)),
        compiler_params=_cp(("parallel",), 40),
    )(act, w, b, z1, ln1_g, ln1_b, ln_g, ln_b)


def _ffn_down_ln_loss(act, w, b, z1, ln1_g, ln1_b, ln_g, ln_b, target):
    S = act.shape[0]
    T = 512

    def body(a_ref, w_ref, b_ref, z1_ref, g1_ref, be1_ref, g_ref, be_ref, t_ref, dz_ref, dzb_ref, loss_ref, dg_ref,
             db_ref):
        @pl.when(pl.program_id(0) == 0)
        def _():
            loss_ref[...] = jnp.zeros_like(loss_ref)
            dg_ref[...] = jnp.zeros_like(dg_ref)
            db_ref[...] = jnp.zeros_like(db_ref)
        subs = [slice(s0, s0 + T // 2) for s0 in (0, T // 2)]
        zs = [jnp.dot(a_ref[rs, :], w_ref[...], preferred_element_type=f32) + b_ref[...]
              + ALPHA * (_ln_stats(z1_ref[rs, :])[0] * g1_ref[...] + be1_ref[...]) for rs in subs]
        for rs, z in zip(subs, zs):
            xhat, rstd = _ln_stats(z)
            err = xhat * g_ref[...] + be_ref[...] - t_ref[rs, :]
            loss_ref[...] += _colsum(err * err) * (0.5 / D_MODEL)
            dy = err * (1.0 / D_MODEL)
            dz = _ln_bwd(dy, xhat, rstd, g_ref[...])
            dz_ref[rs, :] = dz
            dzb_ref[rs, :] = dz.astype(bf16)
            dg_ref[...] += _colsum(dy * xhat)
            db_ref[...] += _colsum(dy)

    row = lambda c: pl.BlockSpec((T, c), lambda i: (i, 0))
    vec = jax.ShapeDtypeStruct((1, D_MODEL), f32)
    return pl.pallas_call(
        body, grid=(S // T,), name="ffn_down_ln_loss",
        out_shape=(jax.ShapeDtypeStruct((S, D_MODEL), f32), jax.ShapeDtypeStruct((S, D_MODEL), bf16), vec, vec, vec),
        in_specs=[row(D_FF), _resident((D_FF, D_MODEL)), _acc((1, D_MODEL)), row(D_MODEL)] + [_acc((1, D_MODEL))] * 4
        + [row(D_MODEL)],
        out_specs=(row(D_MODEL), row(D_MODEL), _acc((1, D_MODEL)), _acc((1, D_MODEL)), _acc((1, D_MODEL))),
        compiler_params=_cp(("arbitrary",), 40),
    )(act, w, b, z1, ln1_g, ln1_b, ln_g, ln_b, target)


def _dgrad_ln_bwd(g, w, dz_res, z, ln_g, name):
    S, K = g.shape
    SUB = 256
    T = 2 * SUB if S % (2 * SUB) == 0 else SUB
    with_ln = z is not None

    def body(*refs):
        if with_ln:
            g_ref, w_ref, r_ref, z_ref, lg_ref, dz_ref, dzb_ref, dg_ref, db_ref = refs
        else:
            g_ref, w_ref, r_ref, dx_ref = refs
        subs = [slice(s0, s0 + SUB) for s0 in range(0, T, SUB)]
        dxs = [lax.dot_general(g_ref[rs, :], w_ref[...], NT_DIMS, preferred_element_type=f32) + ALPHA * r_ref[rs, :]
               for rs in subs]
        if not with_ln:
            for rs, dx in zip(subs, dxs):
                dx_ref[rs, :] = dx
            return

        @pl.when(pl.program_id(0) == 0)
        def _():
            dg_ref[...] = jnp.zeros_like(dg_ref)
            db_ref[...] = jnp.zeros_like(db_ref)
        for rs, dx in zip(subs, dxs):
            xhat, rstd = _ln_stats(z_ref[rs, :])
            dz = _ln_bwd(dx, xhat, rstd, lg_ref[...])
            dz_ref[rs, :] = dz
            dzb_ref[rs, :] = dz.astype(bf16)
            dg_ref[...] += _colsum(dx * xhat)
            db_ref[...] += _colsum(dx)

    row = pl.BlockSpec((T, D_MODEL), lambda i: (i, 0))
    vec = jax.ShapeDtypeStruct((1, D_MODEL), f32)
    in_specs = [pl.BlockSpec((T, K), lambda i: (i, 0)), _resident((D_MODEL, K)), row]
    args = [g, w, dz_res]
    if with_ln:
        in_specs += [row, _acc((1, D_MODEL))]
        args += [z, ln_g]
        out_shape = (jax.ShapeDtypeStruct((S, D_MODEL), f32), jax.ShapeDtypeStruct((S, D_MODEL), bf16), vec, vec)
        out_specs = (row, row, _acc((1, D_MODEL)), _acc((1, D_MODEL)))
    else:
        out_shape = jax.ShapeDtypeStruct((S, D_MODEL), f32)
        out_specs = row
    return pl.pallas_call(
        body, grid=(S // T,), name=name, out_shape=out_shape, in_specs=in_specs, out_specs=out_specs,
        compiler_params=_cp(("arbitrary",), 48),
    )(*args)


def _ffn_down_gate_bwd(dzb, w_down, hfb, hc, conv_w):
    S = hc.shape[0]
    T = 256
    H = FFN_HALO
    nt = S // T
    K = FFN_CONV_WIDTH

    def body(dz_ref, w_ref, h_ref, hc_ref, cw_ref, dh_ref, dw_ref, dcb_ref, da_buf, carry):
        @pl.when(pl.program_id(0) == 0)
        def _():
            dw_ref[...] = jnp.zeros_like(dw_ref)
            dcb_ref[...] = jnp.zeros_like(dcb_ref)
            carry[...] = jnp.zeros_like(carry)
        da_buf[...] = lax.dot_general(dz_ref[...], w_ref[...], NT_DIMS, preferred_element_type=f32)
        ngroups = T // GATE_ROWS
        for c0 in range(0, D_FF, GATE_COLS):
            cols = _gate_cols(c0)
            wts = [[_bcast_rows(cw_ref, k, cs) for k in range(K)] for cs in cols]

            def step(it, state, cols=cols, wts=wts):
                heads, accs = state
                rows = pl.ds(pl.multiple_of((ngroups - 1 - it) * GATE_ROWS, GATE_ROWS), GATE_ROWS)
                g = hc_ref[rows, cols[0]]
                v = hc_ref[rows, cols[1]]
                da = da_buf[rows, cols[0]]
                sg = _sig(g)
                dms = (da * v * (sg * (1.0 + g * (1.0 - sg))), da * (g * sg))
                new_heads, new_accs = [], []
                for cs, wt, dm, head, acc in zip(cols, wts, dms, heads, accs):
                    h0 = h_ref[rows, cs].astype(f32)
                    ext = jnp.concatenate([dm, head], axis=0)
                    dh = wt[K - 1] * dm
                    acc_k = [None] * K + [acc[K] + _fold_rows(dm)]
                    acc_k[K - 1] = acc[K - 1] + _fold_rows(dm * h0)
                    for ahead in range(1, K):
                        dk = pltpu.roll(ext, GATE_ROWS + H - ahead, 0)[:GATE_ROWS]
                        dh = dh + wt[K - 1 - ahead] * dk
                        acc_k[K - 1 - ahead] = acc[K - 1 - ahead] + _fold_rows(dk * h0)
                    dh_ref[rows, cs] = dh.astype(bf16)
                    new_heads.append(dm[:H])
                    new_accs.append(tuple(acc_k))
                return tuple(new_heads), tuple(new_accs)

            zero = jnp.zeros((SUBLANES, GATE_COLS), f32)
            init = (tuple(carry[:, cs] for cs in cols), tuple(tuple(zero for _ in range(K + 1)) for _ in cols))
            heads, accs = lax.fori_loop(0, ngroups, step, init, unroll=True)
            for cs, head, acc in zip(cols, heads, accs):
                carry[:, cs] = head
                dcb_ref[:, cs] += _colsum(acc[K])
                for k in range(K):
                    dw_ref[k:k + 1, cs] += _colsum(acc[k])

    tile = lambda c: pl.BlockSpec((T, c), lambda i: (nt - 1 - i, 0))
    return pl.pallas_call(
        body, grid=(nt,), name="ffn_down_gate_bwd",
        out_shape=(jax.ShapeDtypeStruct((S, 2 * D_FF), bf16), jax.ShapeDtypeStruct((8, 2 * D_FF), f32),
                   jax.ShapeDtypeStruct((1, 2 * D_FF), f32)),
        in_specs=[tile(D_MODEL), _resident((D_FF, D_MODEL)), tile(2 * D_FF), tile(2 * D_FF), _acc((8, 2 * D_FF))],
        out_specs=(tile(2 * D_FF), _acc((8, 2 * D_FF)), _acc((1, 2 * D_FF))),
        scratch_shapes=[pltpu.VMEM((T, D_FF), f32), pltpu.VMEM((H, 2 * D_FF), f32)],
        compiler_params=_cp(("arbitrary",), 48),
    )(dzb, w_down, hfb, hc, conv_w)


def _wgrad(a, g, tn, name, rows=1024):
    S, K = a.shape
    N = g.shape[1]
    T = rows if S % rows == 0 else S

    def body(a_ref, g_ref, dw_ref, db_ref):
        @pl.when(pl.program_id(1) == 0)
        def _():
            dw_ref[...] = jnp.zeros_like(dw_ref)
            db_ref[...] = jnp.zeros_like(db_ref)
        gt = g_ref[...]
        dw_ref[...] += lax.dot_general(a_ref[...].astype(bf16), gt, TN_DIMS, preferred_element_type=f32)
        db_ref[...] += _colsum(gt.astype(f32))

    return pl.pallas_call(
        body, grid=(N // tn, S // T), name=name,
        out_shape=(jax.ShapeDtypeStruct((K, N), f32), jax.ShapeDtypeStruct((1, N), f32)),
        in_specs=[pl.BlockSpec((T, K), lambda j, i: (i, 0)), pl.BlockSpec((T, tn), lambda j, i: (i, j))],
        out_specs=(pl.BlockSpec((K, tn), lambda j, i: (0, j)), pl.BlockSpec((1, tn), lambda j, i: (0, j))),
        compiler_params=_cp(("parallel", "arbitrary"), 56),
    )(a, g)


def _outproj_dgrad(dzb, w, attn_out, lse):
    S = dzb.shape[0]
    T = PERM_TILE
    nsl = ATTN_CH // LANES
    n_p = len(DILATIONS)

    def body(g_ref, w_ref, ao_ref, lse_ref, dco_ref, dgo_ref, *rest):
        do_refs = rest[:n_p]
        st_refs = rest[n_p:2 * n_p]
        slabs = rest[2 * n_p:]
        dcat = lax.dot_general(g_ref[...], w_ref[...], NT_DIMS, preferred_element_type=f32)
        dco_ref[...] = dcat[:, :CONV_CH]
        dgo_ref[...] = dcat[:, CONV_CH + ATTN_CH:]
        lane = lax.broadcasted_iota(jnp.int32, (T, LANES), 1)
        st = lse_ref[...]
        for j in range(nsl):
            dO = dcat[:, CONV_CH + LANES * j:CONV_CH + LANES * (j + 1)]
            prod = dO * ao_ref[:, LANES * j:LANES * (j + 1)].astype(f32)
            for a in range(2):
                in_head = (lane < HEAD_DIM) if a == 0 else (lane >= HEAD_DIM)
                delta = jnp.sum(jnp.where(in_head, prod, 0.0), axis=1, keepdims=True)
                st = jnp.where((lane // 16 == 2 * j + a) & (lane % 16 >= 8), delta, st)
            slabs[j][...] = dO
        slabs[nsl][...] = st
        for d, do_ref, st_ref in zip(DILATIONS, do_refs, st_refs):
            for r in range(d):
                for j in range(nsl):
                    do_ref[r, :, LANES * j:LANES * (j + 1)] = _rows_of(slabs[j], r, T // d, d).astype(bf16)
                st_ref[r] = _rows_of(slabs[nsl], r, T // d, d)

    row = lambda c: pl.BlockSpec((T, c), lambda i: (i, 0))
    return pl.pallas_call(
        body, grid=(S // T,), name="outproj_dgrad",
        out_shape=(jax.ShapeDtypeStruct((S, CONV_CH), f32), jax.ShapeDtypeStruct((S, GMLP_CH), f32))
        + tuple(_perm_shape(S, d, ATTN_CH, bf16) for d in DILATIONS)
        + tuple(_perm_shape(S, d, ATTN_BLOCK, f32) for d in DILATIONS),
        in_specs=[row(D_MODEL), _resident((D_MODEL, D_MODEL)), row(ATTN_CH), row(ATTN_BLOCK)],
        out_specs=(row(CONV_CH), row(GMLP_CH)) + tuple(_perm_spec(d, ATTN_CH) for d in DILATIONS)
        + tuple(_perm_spec(d, ATTN_BLOCK) for d in DILATIONS),
        scratch_shapes=_slabs(nsl + 1, T),
        compiler_params=_cp(("parallel",), 40),
    )(dzb, w, attn_out, lse)


def _gmlp_bwd(c_in, dgm, ln_g, ln_b, w_s, b_s_t):
    S = c_in.shape[0]
    T = 512
    nsteps = S // T

    def body(c_ref, dg_ref, g_ref, be_ref, w_ref, bs_ref, dc_ref, dlg_ref, dlb_ref, dw_ref, dbs_ref,
             du_buf, dv_buf, dm_acc):
        i = pl.program_id(0)

        @pl.when(i == 0)
        def _():
            dlg_ref[...] = jnp.zeros_like(dlg_ref)
            dlb_ref[...] = jnp.zeros_like(dlb_ref)
            dw_ref[...] = jnp.zeros_like(dw_ref)
            dm_acc[...] = jnp.zeros_like(dm_acc)
        c = c_ref[...]
        u = c[:, :GMLP_CH]
        xhat, rstd = _ln_stats(c[:, GMLP_CH:])
        vb = (xhat * g_ref[...] + be_ref[...]).astype(bf16)
        dgm_t = dg_ref[...]
        dm_all = dgm_t * u
        for g in range(GMLP_GROUPS):
            wt = _tril_bf16(w_ref[g])
            cs = slice(GMLP_GROUP_DIM * g, GMLP_GROUP_DIM * (g + 1))
            dw_g = jnp.zeros((CHUNK, CHUNK), f32)
            for ci in range(T // CHUNK):
                rs = slice(CHUNK * ci, CHUNK * (ci + 1))
                v_c = vb[rs, cs]
                mixed = jnp.dot(wt, v_c, preferred_element_type=f32) + bs_ref[:, g:g + 1]
                dm = dm_all[rs, cs]
                dmb = dm.astype(bf16)
                du_buf[rs, cs] = dgm_t[rs, cs] * mixed
                dv_buf[rs, cs] = lax.dot_general(wt, dmb, TN_DIMS, preferred_element_type=f32)
                dw_g = dw_g + lax.dot_general(dmb, v_c, NT_DIMS, preferred_element_type=f32)
                dm_acc[:, cs] += dm
            dw_ref[g] += dw_g
        dv = dv_buf[...]
        dvr = _ln_bwd(dv, xhat, rstd, g_ref[...])
        dlg_ref[...] += _colsum(dv * xhat)
        dlb_ref[...] += _colsum(dv)
        dc_ref[:, :GMLP_CH] = du_buf[...].astype(bf16)
        dc_ref[:, GMLP_CH:] = dvr.astype(bf16)

        @pl.when(i == nsteps - 1)
        def _():
            row = lax.broadcasted_iota(jnp.int32, (CHUNK, CHUNK), 0)
            col = lax.broadcasted_iota(jnp.int32, (CHUNK, CHUNK), 1)
            tile = jnp.zeros((CHUNK, CHUNK), f32)
            for g in range(GMLP_GROUPS):
                dw_ref[g] = jnp.where(col <= row, dw_ref[g], 0.0)
                gsum = jnp.sum(dm_acc[:, GMLP_GROUP_DIM * g:GMLP_GROUP_DIM * (g + 1)], axis=1, keepdims=True)
                tile = jnp.where(col == g, gsum, tile)
            dbs_ref[...] = tile

    vec = jax.ShapeDtypeStruct((1, GMLP_CH), f32)
    return pl.pallas_call(
        body, grid=(nsteps,), name="gmlp_bwd",
        out_shape=(jax.ShapeDtypeStruct((S, 2 * GMLP_CH), bf16), vec, vec,
                   jax.ShapeDtypeStruct((GMLP_GROUPS, CHUNK, CHUNK), f32), jax.ShapeDtypeStruct((CHUNK, CHUNK), f32)),
        in_specs=[pl.BlockSpec((T, 2 * GMLP_CH), lambda i: (i, 0)), pl.BlockSpec((T, GMLP_CH), lambda i: (i, 0)),
                  _acc((1, GMLP_CH)), _acc((1, GMLP_CH)), _acc((GMLP_GROUPS, CHUNK, CHUNK)), _acc((CHUNK, GMLP_GROUPS))],
        out_specs=(pl.BlockSpec((T, 2 * GMLP_CH), lambda i: (i, 0)), _acc((1, GMLP_CH)), _acc((1, GMLP_CH)),
                   _acc((GMLP_GROUPS, CHUNK, CHUNK)), _acc((CHUNK, CHUNK))),
        scratch_shapes=[pltpu.VMEM((T, GMLP_CH), f32), pltpu.VMEM((T, GMLP_CH), f32), pltpu.VMEM((CHUNK, GMLP_CH), f32)],
        compiler_params=_cp(("arbitrary",), 32),
    )(c_in, dgm, ln_g, ln_b, w_s, b_s_t)


def _attn_bwd_pattern(qkv, d_out, stats, bias, d):
    _, L, _ = qkv.shape
    B = ATTN_BLOCK
    QB = _attn_tile(L)
    nsb = QB // B
    nt = L // QB
    U = nsb * UNITS_PER_BLOCK
    KV = 2 * ATTN_CH

    def body(cur_ref, hk_ref, hv_ref, do_ref, st_ref, b_ref, dqkv_ref, dbias_ref, lg, dp, pb, dsb, dkv, carry):
        r = pl.program_id(0)
        i = pl.program_id(1)
        n = nt - 1 - i

        @pl.when((r == 0) & (i == 0))
        def _():
            dbias_ref[...] = jnp.zeros_like(dbias_ref)

        @pl.when(i == 0)
        def _():
            carry[...] = jnp.zeros_like(carry)

        def operands(b, j, a):
            rows = slice(B * b, B * (b + 1))
            q2 = cur_ref[rows, LANES * j:LANES * (j + 1)]
            do2 = do_ref[rows, LANES * j:LANES * (j + 1)]
            keep = _head_lanes(a)
            return jnp.where(keep, q2, jnp.zeros_like(q2)), jnp.where(keep, do2, jnp.zeros_like(do2))

        for b in range(nsb):
            valid = _band_mask(b == 0, n)
            for j in range(HEAD_PAIRS):
                k2 = _pair_keys(cur_ref, hk_ref, 1, b, j)
                v2 = _pair_keys(cur_ref, hv_ref, 2, b, j)
                for a in range(2):
                    u = (b * HEAD_PAIRS + j) * 2 + a
                    qm, dom = operands(b, j, a)
                    logits = lax.dot_general(qm, k2, NT_DIMS, preferred_element_type=f32) + b_ref[2 * j + a]
                    lg[B * u:B * (u + 1), :] = jnp.where(valid, logits, NEG)
                    dp[B * u:B * (u + 1), :] = lax.dot_general(dom, v2, NT_DIMS, preferred_element_type=f32)
        for b in range(nsb):
            for j in range(HEAD_PAIRS):
                for a in range(2):
                    u = (b * HEAD_PAIRS + j) * 2 + a
                    rows = slice(B * u, B * (u + 1))
                    lane0 = 32 * j + 16 * a
                    lse = st_ref[B * b:B * (b + 1), lane0:lane0 + 1]
                    delta = st_ref[B * b:B * (b + 1), lane0 + 8:lane0 + 9]
                    p = jnp.exp(lg[rows, :] - lse)
                    ds = p * (dp[rows, :] - delta)
                    pb[rows, :] = p.astype(bf16)
                    dsb[rows, :] = ds.astype(bf16)
                    dbias_ref[2 * j + a] += ds
        dkv[...] = jnp.zeros_like(dkv)
        for b in range(nsb):
            for j in range(HEAD_PAIRS):
                k2 = _pair_keys(cur_ref, hk_ref, 1, b, j)
                dq, dk2, dv2 = [], None, None
                for a in range(2):
                    u = (b * HEAD_PAIRS + j) * 2 + a
                    rows = slice(B * u, B * (u + 1))
                    qm, dom = operands(b, j, a)
                    ds_u = dsb[rows, :]
                    dq.append(jnp.dot(ds_u, k2, preferred_element_type=f32))
                    dk_u = lax.dot_general(ds_u, qm, TN_DIMS, preferred_element_type=f32)
                    dv_u = lax.dot_general(pb[rows, :], dom, TN_DIMS, preferred_element_type=f32)
                    dk2 = dk_u if dk2 is None else dk2 + dk_u
                    dv2 = dv_u if dv2 is None else dv2 + dv_u
                dq2 = jnp.where(_head_lanes(0), dq[0], dq[1]) * (HEAD_DIM ** -0.5)
                dqkv_ref[B * b:B * (b + 1), LANES * j:LANES * (j + 1)] = dq2.astype(bf16)
                dkv[B * b:B * (b + 2), LANES * j:LANES * (j + 1)] += dk2
                dkv[B * b:B * (b + 2), ATTN_CH + LANES * j:ATTN_CH + LANES * (j + 1)] += dv2
        dkv[QB:, :] += carry[...]
        dqkv_ref[:, ATTN_CH:] = dkv[B:, :].astype(bf16)
        carry[...] = dkv[0:B, :]

    halo = lambda part: pl.BlockSpec((None, B, ATTN_CH),
                                     lambda r, i: (r, jnp.maximum((nt - 1 - i) * nsb - 1, 0), part))
    tile_spec = lambda c: pl.BlockSpec((None, QB, c), lambda r, i: (r, nt - 1 - i, 0))
    return pl.pallas_call(
        body, grid=(d, nt), name=f"attn_bwd_d{d}",
        out_shape=(jax.ShapeDtypeStruct((d, L, QKV_CH), bf16), jax.ShapeDtypeStruct((ATTN_HEADS, B, 2 * B), f32)),
        in_specs=[tile_spec(QKV_CH), halo(1), halo(2), tile_spec(ATTN_CH), tile_spec(B),
                  _resident((ATTN_HEADS, B, 2 * B))],
        out_specs=(tile_spec(QKV_CH), _acc((ATTN_HEADS, B, 2 * B))),
        scratch_shapes=[pltpu.VMEM((U * B, 2 * B), f32), pltpu.VMEM((U * B, 2 * B), f32),
                        pltpu.VMEM((U * B, 2 * B), bf16), pltpu.VMEM((U * B, 2 * B), bf16),
                        pltpu.VMEM((B + QB, KV), f32), pltpu.VMEM((B, KV), f32)],
        compiler_params=_cp(("arbitrary", "arbitrary"), 48),
    )(qkv, qkv, qkv, d_out, stats, bias)


def _attn_bwd_merge(d_a, dqkvs, d_c):
    S = d_a.shape[0]
    T = PERM_TILE
    nsl = QKV_CH // LANES
    n_p = len(DILATIONS)

    def body(da_ref, *rest):
        g_refs = rest[:n_p]
        dc_ref, dh_ref = rest[n_p:n_p + 2]
        slabs = rest[n_p + 2:]
        q0 = 2 * CONV_CH
        dh_ref[:, :q0] = da_ref[...]
        dh_ref[:, q0 + QKV_CH:] = dc_ref[...]
        for p, (d, g_ref) in enumerate(zip(DILATIONS, g_refs)):
            for r in range(d):
                for j in range(nsl):
                    _set_rows_of(slabs[p * nsl + j], r, T // d, d, g_ref[r, :, LANES * j:LANES * (j + 1)].astype(f32))
        for j in range(nsl):
            acc = slabs[j][...]
            for p in range(1, n_p):
                acc = acc + slabs[p * nsl + j][...]
            dh_ref[:, q0 + LANES * j:q0 + LANES * (j + 1)] = acc.astype(bf16)

    row = lambda c: pl.BlockSpec((T, c), lambda i: (i, 0))
    return pl.pallas_call(
        body, grid=(S // T,), name="attn_bwd_merge", out_shape=jax.ShapeDtypeStruct((S, IN_CH), bf16),
        in_specs=[row(2 * CONV_CH)] + [_perm_spec(d, QKV_CH) for d in DILATIONS] + [row(2 * GMLP_CH)],
        out_specs=row(IN_CH), scratch_shapes=_slabs(n_p * nsl, T),
        compiler_params=_cp(("parallel",), 48),
    )(d_a, *dqkvs, d_c)


def _bias_table_grad(dbias, buckets):
    n = dbias.shape[0]

    def body(db_ref, bk_ref, o_ref):
        p = pl.program_id(0)
        h = pl.program_id(1)

        @pl.when((p == 0) & (h == 0))
        def _():
            o_ref[...] = jnp.zeros_like(o_ref)
        ids = bk_ref[0]
        db = db_ref[0, 0]
        row = lax.broadcasted_iota(jnp.int32, (N_BUCKETS, 128), 0)
        lane = lax.broadcasted_iota(jnp.int32, (N_BUCKETS, 128), 1)
        upd = jnp.zeros((N_BUCKETS, 128), f32)
        for b in range(N_BUCKETS):
            s = jnp.sum(jnp.sum(jnp.where(ids == b, db, 0.0), axis=1, keepdims=True), axis=0, keepdims=True)
            upd = jnp.where((row == b) & (lane == h), s, upd)
        o_ref[...] += upd

    return pl.pallas_call(
        body, grid=(n, ATTN_HEADS), name="bias_table_grad",
        out_shape=jax.ShapeDtypeStruct((N_BUCKETS, 128), f32),
        in_specs=[pl.BlockSpec((1, 1, ATTN_BLOCK, 2 * ATTN_BLOCK), lambda p, h: (p, h, 0, 0)),
                  pl.BlockSpec((1, ATTN_BLOCK, 2 * ATTN_BLOCK), lambda p, h: (p, 0, 0))],
        out_specs=_acc((N_BUCKETS, 128)),
        compiler_params=_cp(("arbitrary", "arbitrary"), 16),
    )(dbias, buckets)


def _conv_bwd(a_in, hc, dco, dw_w, ln_g, ln_b):
    S = a_in.shape[0]
    T = 512
    hb = T // CONV_HALO
    nsteps = S // T
    R = T + CONV_HALO
    K = CONV_WIDTH

    def body(a_ref, hc_ref, hcn_ref, d_ref, dn_ref, w_ref, g_ref, be_ref,
             da_ref, dw_ref, dcb_ref, dlg_ref, dlb_ref, ext, dbuf, wacc):
        i = pl.program_id(0)

        @pl.when(i == 0)
        def _():
            wacc[...] = jnp.zeros_like(wacc)
            dcb_ref[...] = jnp.zeros_like(dcb_ref)
            dlg_ref[...] = jnp.zeros_like(dlg_ref)
            dlb_ref[...] = jnp.zeros_like(dlb_ref)
        ext[0:T, :] = hc_ref[...]
        ext[T:, :] = hcn_ref[...]
        xhat, rstd = _ln_stats(ext[...])
        hl = xhat * g_ref[...] + be_ref[...]
        ext[0:T, :] = d_ref[...]
        ext[T:, :] = dn_ref[...]
        sl_ = _sig(hl)
        dhl = ext[...] * (sl_ * (1.0 + hl * (1.0 - sl_)))
        dhc = _ln_bwd(dhl, xhat, rstd, g_ref[...])
        rowi = lax.broadcasted_iota(jnp.int32, (R, CONV_CH), 0)
        dbuf[...] = jnp.where((rowi < T) | (i < nsteps - 1), dhc, 0.0)
        dlg_ref[...] += _colsum(dhl[:T] * xhat[:T])
        dlb_ref[...] += _colsum(dhl[:T])
        dcb_ref[...] += _colsum(dbuf[pl.ds(0, T), :])
        starts = _window_rolls(range(K))
        slabs = [slice(LANES * j, LANES * (j + 1)) for j in range(CONV_CH // LANES)]

        def step(g, _):
            r0 = pl.multiple_of(g * CONV_GROUP, CONV_GROUP)
            rows = pl.ds(r0, CONV_GROUP)
            for j, cs in enumerate(slabs):
                gate_cs = slice(CONV_CH + LANES * j, CONV_CH + LANES * (j + 1))
                win = dbuf[pl.ds(r0, CONV_GROUP + CONV_HALO), cs]
                a = a_ref[rows, cs]
                sg = _sig(a_ref[rows, gate_cs])
                hg = a * sg
                dhg = jnp.zeros((CONV_GROUP, LANES), f32)
                for b, ss in starts.items():
                    rolled = win if b == 0 else pltpu.roll(win, b, 0)
                    for s in ss:
                        k = K - 1 - s
                        dk = rolled[s + b:s + b + CONV_GROUP]
                        dhg = dhg + w_ref[k:k + 1, cs] * dk
                        prod = dk * hg
                        fold = prod[0:SUBLANES]
                        for r in range(SUBLANES, CONV_GROUP, SUBLANES):
                            fold = fold + prod[r:r + SUBLANES]
                        wacc[SUBLANES * k:SUBLANES * (k + 1), cs] += fold
                da_ref[rows, cs] = (dhg * sg).astype(bf16)
                da_ref[rows, gate_cs] = (dhg * hg * (1.0 - sg)).astype(bf16)
            return 0

        lax.fori_loop(0, T // CONV_GROUP, step, 0)

        @pl.when(i == nsteps - 1)
        def _():
            for k in range(K):
                dw_ref[k:k + 1, :] = _colsum(wacc[SUBLANES * k:SUBLANES * (k + 1), :])
            dw_ref[K:, :] = jnp.zeros((32 - K, CONV_CH), f32)

    vec = jax.ShapeDtypeStruct((1, CONV_CH), f32)
    nxt = lambda i: (jnp.minimum((i + 1) * hb, nsteps * hb - 1), 0)
    return pl.pallas_call(
        body, grid=(nsteps,), name="conv_bwd",
        out_shape=(jax.ShapeDtypeStruct((S, 2 * CONV_CH), bf16), jax.ShapeDtypeStruct((32, CONV_CH), f32), vec, vec, vec),
        in_specs=[pl.BlockSpec((T, 2 * CONV_CH), lambda i: (i, 0)),
                  pl.BlockSpec((T, CONV_CH), lambda i: (i, 0)), pl.BlockSpec((CONV_HALO, CONV_CH), nxt),
                  pl.BlockSpec((T, CONV_CH), lambda i: (i, 0)), pl.BlockSpec((CONV_HALO, CONV_CH), nxt),
                  _acc((32, CONV_CH)), _acc((1, CONV_CH)), _acc((1, CONV_CH))],
        out_specs=(pl.BlockSpec((T, 2 * CONV_CH), lambda i: (i, 0)), _acc((32, CONV_CH)), _acc((1, CONV_CH)),
                   _acc((1, CONV_CH)), _acc((1, CONV_CH))),
        scratch_shapes=[pltpu.VMEM((R, CONV_CH), f32), pltpu.VMEM((R, CONV_CH), f32),
                        pltpu.VMEM((SUBLANES * 32, CONV_CH), f32)],
        compiler_params=_cp(("arbitrary",), 32),
    )(a_in, hc, hc, dco, dco, dw_w, ln_g, ln_b)


def _adamw(g, w, m, v, name):
    R, C = g.shape
    T = R
    for cand in (512, 256, 128, 64, 32, 16, 8):
        if R % cand == 0 and cand * C * 4 <= MIB:
            T = cand
            break
    c1 = 1.0 / (1.0 - ADAM_B1 ** ADAM_STEP)
    c2 = 1.0 / (1.0 - ADAM_B2 ** ADAM_STEP)

    def body(g_ref, w_ref, m_ref, v_ref, d_ref, nm_ref, nv_ref):
        gg = g_ref[...]
        nm = ADAM_B1 * m_ref[...] + (1.0 - ADAM_B1) * gg
        nv = ADAM_B2 * v_ref[...] + (1.0 - ADAM_B2) * (gg * gg)
        nm_ref[...] = nm
        nv_ref[...] = nv
        d_ref[...] = -ADAM_LR * ((nm * c1) / (jnp.sqrt(nv * c2) + ADAM_EPS) + ADAM_WD * w_ref[...])

    blk = pl.BlockSpec((T, C), lambda i: (i, 0))
    sd = jax.ShapeDtypeStruct((R, C), f32)
    return pl.pallas_call(
        body, grid=(R // T,), name=name, out_shape=(sd, sd, sd), in_specs=[blk] * 4, out_specs=(blk, blk, blk),
        compiler_params=_cp(("parallel",), 48),
    )(g, w, m, v)


def _pad_rows(a, rows):
    return jnp.pad(a, ((0, rows - a.shape[0]), (0, 0)))


def _local_step(x, target, wb, late_weights, sp, sink):
    buckets = jnp.asarray(_bucket_ids())
    bias = _bias_build(sp["rel_bias_table"], buckets)
    wb = dict(wb)
    saved = []
    xl = x
    for l in range(DEPTH):
        vec = lambda name: sp[name][l][None, :]
        a_in, c_in, *qkv = _inproj_fwd(xl, wb["w_in"][l], vec("b_in"))
        conv_w = _pad_rows(sp["conv_dw_w"][l], 32)
        conv_out, hc = _conv_fwd(a_in, conv_w, vec("conv_dw_b"), vec("conv_ln_g"), vec("conv_ln_b"))
        attn_out, lse = _attn_fwd(qkv, bias)
        bs_t = sp["gmlp_b_s"][l].T
        gm_out = _gmlp_fwd(c_in, vec("gmlp_ln_g"), vec("gmlp_ln_b"), sp["gmlp_w_s"][l], bs_t)
        if l == 0:
            wb.update(late_weights(gm_out))
        cat, z1, x1b = _outproj_ln_fwd(conv_out, attn_out, gm_out, wb["w_out"][l], vec("b_out"), xl,
                                           vec("ln1_g"), vec("ln1_b"))
        fconv_w = _pad_rows(sp["ffn_conv_w"][l], 8)
        hfb, fhc, act = _ffn_up_gate_fwd(x1b, wb["ffn_w_up"][l], vec("ffn_b_up"), fconv_w, vec("ffn_conv_b"))
        down = (act, wb["ffn_w_down"][l], vec("ffn_b_down"), z1, vec("ln1_g"), vec("ln1_b"), vec("ln2_g"),
                vec("ln2_b"))
        z2, x2 = _ffn_down_ln_fwd(*down) if l < DEPTH - 1 else (None, None)
        saved.append(dict(x=xl, a_in=a_in, qkv=qkv, c_in=c_in, hc=hc, attn_out=attn_out, lse=lse, cat=cat, z1=z1,
                          x1b=x1b, hfb=hfb, fhc=fhc, act=act, z2=z2, conv_w=conv_w, fconv_w=fconv_w, bs_t=bs_t))
        xl = x2

    grads = {}
    per_layer = {k: [None] * DEPTH for k in (
        "b_in", "conv_dw_w", "conv_dw_b", "conv_ln_g", "conv_ln_b", "gmlp_ln_g", "gmlp_ln_b", "gmlp_w_s",
        "gmlp_b_s", "b_out", "ln1_g", "ln1_b", "ffn_b_up", "ffn_conv_w", "ffn_conv_b", "ffn_b_down", "ln2_g", "ln2_b")}
    dbias_all = []
    dz2, dz2b, loss_part, dg2, db2 = _ffn_down_ln_loss(*down, target)
    loss = jnp.sum(loss_part)
    grad_x = None
    tok = jnp.zeros((), f32)
    for l in reversed(range(DEPTH)):
        sv = saved[l]
        vec = lambda name: sp[name][l][None, :] + tok
        per_layer["ln2_g"][l] = dg2[0]
        per_layer["ln2_b"][l] = db2[0]
        dw_down, db_down = _wgrad(sv["act"], dz2b, 512, "ffn_down_wgrad")
        tok = sink.put("ffn_w_down", l, dw_down, tok)
        per_layer["ffn_b_down"][l] = db_down[0]
        dhf, dfcw, dfcb = _ffn_down_gate_bwd(dz2b, wb["ffn_w_down"][l], sv["hfb"], sv["fhc"], sv["fconv_w"])
        per_layer["ffn_conv_w"][l] = dfcw[:FFN_CONV_WIDTH]
        per_layer["ffn_conv_b"][l] = dfcb[0]
        dw_up, db_up = _wgrad(sv["x1b"], dhf, 1408, "ffn_up_wgrad")
        tok = sink.put("ffn_w_up", l, dw_up, tok)
        per_layer["ffn_b_up"][l] = db_up[0]
        dz1, dz1b, dg1, db1 = _dgrad_ln_bwd(dhf, wb["ffn_w_up"][l], dz2, sv["z1"], vec("ln1_g"), "ffn_up_dgrad_ln")
        per_layer["ln1_g"][l] = dg1[0]
        per_layer["ln1_b"][l] = db1[0]
        dw_out, db_out = _wgrad(sv["cat"], dz1b, D_MODEL, "outproj_wgrad")
        tok = sink.put("w_out", l, dw_out, tok)
        per_layer["b_out"][l] = db_out[0]
        dco, dgo, *perm = _outproj_dgrad(dz1b, wb["w_out"][l], sv["attn_out"], sv["lse"])
        d_outs, stats = perm[:len(DILATIONS)], perm[len(DILATIONS):]
        d_c, dglg, dglb, dws, dbs = _gmlp_bwd(sv["c_in"], dgo, vec("gmlp_ln_g"), vec("gmlp_ln_b"), sp["gmlp_w_s"][l],
                                              sv["bs_t"])
        per_layer["gmlp_ln_g"][l] = dglg[0]
        per_layer["gmlp_ln_b"][l] = dglb[0]
        per_layer["gmlp_w_s"][l] = dws
        per_layer["gmlp_b_s"][l] = dbs[:, :GMLP_GROUPS].T
        dqkvs = []
        for p, d in enumerate(DILATIONS):
            dqkv, dbias = _attn_bwd_pattern(sv["qkv"][p], d_outs[p], stats[p], bias[p], d)
            dqkvs.append(dqkv)
            dbias_all.append(dbias)
        d_a, dcw, dcb, dclg, dclb = _conv_bwd(sv["a_in"], sv["hc"], dco, sv["conv_w"], vec("conv_ln_g"),
                                              vec("conv_ln_b"))
        per_layer["conv_dw_w"][l] = dcw[:CONV_WIDTH]
        per_layer["conv_dw_b"][l] = dcb[0]
        per_layer["conv_ln_g"][l] = dclg[0]
        per_layer["conv_ln_b"][l] = dclb[0]
        dh = _attn_bwd_merge(d_a, dqkvs, d_c)
        dw_in, db_in = _wgrad(sv["x"], dh, IN_CH, "inproj_wgrad")
        tok = sink.put("w_in", l, dw_in, tok)
        per_layer["b_in"][l] = db_in[0]
        if l > 0:
            pv = saved[l - 1]
            dz2, dz2b, dg2, db2 = _dgrad_ln_bwd(dh, wb["w_in"][l], dz1, pv["z2"], sp["ln2_g"][l - 1][None, :] + tok,
                                                "inproj_dgrad_ln")
        else:
            grad_x = _dgrad_ln_bwd(dh, wb["w_in"][l], dz1, None, None, "inproj_dgrad")
    for k, v in per_layer.items():
        grads[k] = jnp.stack(v)
    dbias_cat = jnp.stack(dbias_all)
    bk_cat = jnp.concatenate([buckets] * DEPTH, axis=0)
    grads["rel_bias_table"] = _bias_table_grad(dbias_cat, bk_cat)[:, :ATTN_HEADS]
    return loss, grad_x, grads, sink.finish(grad_x)


N_CHIPS = 4
BIG = {"w_in": (D_MODEL, IN_CH, 1), "w_out": (D_MODEL, D_MODEL, 0),
       "ffn_w_up": (D_MODEL, 2 * D_FF, 1), "ffn_w_down": (D_FF, D_MODEL, 0)}
SMALL = ("b_in", "conv_dw_w", "conv_dw_b", "conv_ln_g", "conv_ln_b", "rel_bias_table", "gmlp_ln_g", "gmlp_ln_b",
         "gmlp_w_s", "gmlp_b_s", "b_out", "ln1_g", "ln1_b", "ffn_b_up", "ffn_conv_w", "ffn_conv_b", "ffn_b_down",
         "ln2_g", "ln2_b")
SMALL_SHARDED = ("conv_dw_w", "ffn_conv_w")
WEIGHTS = ("w_in", "b_in", "conv_dw_w", "conv_dw_b", "conv_ln_g", "conv_ln_b", "rel_bias_table", "gmlp_ln_g",
           "gmlp_ln_b", "gmlp_w_s", "gmlp_b_s", "w_out", "b_out", "ln1_g", "ln1_b", "ffn_w_up", "ffn_b_up",
           "ffn_conv_w", "ffn_conv_b", "ffn_w_down", "ffn_b_down", "ln2_g", "ln2_b")
ANY = pl.BlockSpec(memory_space=pl.ANY)


def _position():
    return lax.axis_index("x"), lax.axis_index("y"), lax.axis_index("c")


def _other_chips(x, y):
    return [(1 - x, y), (x, 1 - y), (1 - x, 1 - y)]


def _cast_bf16(a):
    R, C = a.shape
    T = 128

    def body(a_ref, o_ref):
        o_ref[...] = a_ref[...].astype(bf16)

    return pl.pallas_call(
        body, grid=(R // T,), name="cast_bf16", out_shape=jax.ShapeDtypeStruct((R, C), bf16),
        in_specs=[pl.BlockSpec((T, C), lambda i: (i, 0))], out_specs=pl.BlockSpec((T, C), lambda i: (i, 0)),
        compiler_params=_cp(("parallel",), 16),
    )(a)


def _chip_slot(ref, name, l, p):
    K, N, ax = BIG[name]
    if ax == 1:
        sz = N // N_CHIPS
        return ref.at[l, :, pl.ds(pl.multiple_of(p * sz, 128), sz)]
    sz = K // N_CHIPS
    return ref.at[l, pl.ds(pl.multiple_of(p * sz, 16), sz), :]


def _gather_weights(shards, conv_w, fconv_w):
    names = list(shards)
    n_big = len(names)
    n_t = n_big + 2
    n_chip = 3 * n_t
    n_pass = 3 * n_big

    def body(*refs):
        ins = refs[:n_t]
        outs = refs[n_t:2 * n_t]
        send_sems, recv_sems, pass_send, pass_recv, local_sems = refs[2 * n_t:]
        x, y, c = _position()
        me = 2 * x + y
        chips = _other_chips(x, y)

        def src(t):
            return ins[t].at[c] if t < n_big else ins[t]

        def slot(t, l, p):
            return _chip_slot(outs[t], names[t], l, p) if t < n_big else outs[t].at[p]

        locs, cps = [], []
        for t in range(n_t):
            for l in (range(DEPTH) if t < n_big else (0,)):
                loc = pltpu.make_async_copy(ins[t].at[l] if t < n_big else ins[t], slot(t, l, me),
                                            local_sems.at[DEPTH * t + l])
                loc.start()
                locs.append(loc)
            for k, (px, py) in enumerate(chips):
                cp = pltpu.make_async_remote_copy(
                    src_ref=src(t), dst_ref=slot(t, c, me), send_sem=send_sems.at[3 * t + k],
                    recv_sem=recv_sems.at[3 * t + k], device_id=(px, py, c), device_id_type=MESH_ID)
                cp.start()
                cps.append(cp)
        for t in range(n_t):
            for k, (px, py) in enumerate(chips):
                landed = slot(t, c, 2 * px + py)
                pltpu.make_async_remote_copy(
                    src_ref=src(t), dst_ref=landed, send_sem=send_sems.at[3 * t + k],
                    recv_sem=recv_sems.at[3 * t + k], device_id=(px, py, c), device_id_type=MESH_ID).wait_recv()
                if t < n_big:
                    cp = pltpu.make_async_remote_copy(
                        src_ref=landed, dst_ref=landed, send_sem=pass_send.at[3 * t + k],
                        recv_sem=pass_recv.at[3 * t + k], device_id=(x, y, 1 - c), device_id_type=MESH_ID)
                    cp.start()
                    cps.append(cp)
        for t in range(n_big):
            for k, (px, py) in enumerate(chips):
                from_sibling = slot(t, 1 - c, 2 * px + py)
                pltpu.make_async_remote_copy(
                    src_ref=from_sibling, dst_ref=from_sibling, send_sem=pass_send.at[3 * t + k],
                    recv_sem=pass_recv.at[3 * t + k], device_id=(x, y, 1 - c), device_id_type=MESH_ID).wait_recv()
        for cp in cps:
            cp.wait_send()
        for loc in locs:
            loc.wait()

    ins = [shards[n] for n in names] + [conv_w, fconv_w]
    out_shape = [jax.ShapeDtypeStruct((DEPTH, BIG[n][0], BIG[n][1]), bf16) for n in names]
    out_shape += [jax.ShapeDtypeStruct((N_CHIPS,) + conv_w.shape, f32), jax.ShapeDtypeStruct((N_CHIPS,) + fconv_w.shape, f32)]
    outs = pl.pallas_call(
        body, name="gather_weights", out_shape=tuple(out_shape), in_specs=[ANY] * n_t, out_specs=tuple([ANY] * n_t),
        scratch_shapes=[pltpu.SemaphoreType.DMA((n_chip,)), pltpu.SemaphoreType.DMA((n_chip,)),
                        pltpu.SemaphoreType.DMA((n_pass,)), pltpu.SemaphoreType.DMA((n_pass,)),
                        pltpu.SemaphoreType.DMA((DEPTH * n_t,))],
    )(*ins)
    return dict(zip(names, outs[:n_big])), outs[-2], outs[-1]


LATE_WEIGHTS = ("w_out", "ffn_w_up", "ffn_w_down")
HBM = pl.BlockSpec(memory_space=pltpu.HBM)
SEM = pl.BlockSpec(memory_space=pltpu.SEMAPHORE)


def _cast_into_full(shard, name, chip_arr):
    K, N, ax = BIG[name]
    k, n = _shard_shape(name)
    T = 64
    nrt = k // T

    def body(p_ref, a_ref, o_ref):
        o_ref[...] = a_ref[...].astype(bf16)

    if ax == 1:
        out_spec = pl.BlockSpec((None, T, n), lambda l, i, p: (l, i, p[0]))
    else:
        out_spec = pl.BlockSpec((None, T, n), lambda l, i, p: (l, p[0] * nrt + i, 0))
    return pl.pallas_call(
        body, name="cast_into_full", out_shape=jax.ShapeDtypeStruct((DEPTH, K, N), bf16),
        grid_spec=pltpu.PrefetchScalarGridSpec(
            num_scalar_prefetch=1, grid=(DEPTH, nrt),
            in_specs=[pl.BlockSpec((None, T, n), lambda l, i, p: (l, i, 0))], out_specs=out_spec),
        compiler_params=_cp(("parallel", "parallel"), 16),
    )(chip_arr, shard)


def _late_copies(refs, send_sems, recv_sems):
    x, y, c = _position()
    me = 2 * x + y
    idx = 0
    for ref, name in zip(refs, LATE_WEIGHTS):
        for l in range(DEPTH):
            for px, py in _other_chips(x, y):
                def copy(p, ref=ref, name=name, l=l, px=px, py=py, idx=idx):
                    part = _chip_slot(ref, name, l, p)
                    return pltpu.make_async_remote_copy(
                        src_ref=part, dst_ref=part, send_sem=send_sems.at[idx], recv_sem=recv_sems.at[idx],
                        device_id=(px, py, c), device_id_type=MESH_ID)
                yield copy(me), copy(2 * px + py)
                idx += 1


N_LATE_COPIES = 3 * DEPTH * len(LATE_WEIGHTS)


def _gather_start(fulls, after):
    n = len(fulls)

    def body(*refs):
        ins = refs[:n]
        send_sems, recv_sems = refs[n + 1:n + 3]
        token = refs[-1]
        for sent, _ in _late_copies(ins, send_sems, recv_sems):
            sent.start()
        token[...] = jnp.zeros_like(token)

    outs = pl.pallas_call(
        body, name="gather_start",
        out_shape=(pltpu.SemaphoreType.DMA((N_LATE_COPIES,)), pltpu.SemaphoreType.DMA((N_LATE_COPIES,)))
        + tuple(pltpu.HBM(f.shape, f.dtype) for f in fulls) + (jax.ShapeDtypeStruct((SUBLANES, LANES), f32),),
        in_specs=(HBM,) * n + (ANY,),
        out_specs=(SEM, SEM) + (HBM,) * n + (pl.BlockSpec(memory_space=pltpu.VMEM),),
        input_output_aliases={t: 2 + t for t in range(n)},
        compiler_params=pltpu.CompilerParams(has_side_effects=pltpu.SideEffectType.DATAFLOW_SIDE_EFFECTING),
    )(*[pltpu.with_memory_space_constraint(f, pltpu.HBM) for f in fulls], after)
    return outs[0], outs[1], outs[2:2 + n], outs[-1]


def _gather_wait(send_sems, recv_sems, fulls, after):
    n = len(fulls)

    def body(*refs):
        ins = refs[:n]
        send_ref, recv_ref = refs[n:n + 2]
        for sent, landed in _late_copies(ins, send_ref, recv_ref):
            sent.wait_send()
            landed.wait_recv()

    return pl.pallas_call(
        body, name="gather_wait", out_shape=tuple(pltpu.HBM(f.shape, f.dtype) for f in fulls),
        in_specs=(HBM,) * n + (SEM, SEM, ANY), out_specs=(HBM,) * n,
        input_output_aliases={t: t for t in range(n)},
        compiler_params=pltpu.CompilerParams(has_side_effects=pltpu.SideEffectType.DATAFLOW_SIDE_EFFECTING),
    )(*fulls, send_sems, recv_sems, after)


def _half(ref, name, c):
    K, N, ax = BIG[name]
    if ax == 1:
        return ref.at[pl.ds(pl.multiple_of(c * (K // 2), 8), K // 2), :]
    return ref.at[:, pl.ds(pl.multiple_of(c * (N // 2), 128), N // 2)]


def _half_shape(name):
    K, N, ax = BIG[name]
    return (K // 2, N) if ax == 1 else (K, N // 2)


def _shard_of_half(ref, name, q):
    K, N, ax = BIG[name]
    if ax == 1:
        sz = N // N_CHIPS
        return ref.at[:, pl.ds(pl.multiple_of(q * sz, 128), sz)]
    sz = K // N_CHIPS
    return ref.at[pl.ds(pl.multiple_of(q * sz, 16), sz), :]


def _shard_half_shape(name):
    K, N, ax = BIG[name]
    return (K // 2, N // N_CHIPS) if ax == 1 else (K // N_CHIPS, N // 2)


def _shard_shape(name):
    K, N, ax = BIG[name]
    return (K, N // N_CHIPS) if ax == 1 else (K // N_CHIPS, N)


def _pair_copies(names, srcs, lands, send_sems, recv_sems):
    x, y, c = _position()
    for idx, (name, src, land) in enumerate(zip(names, srcs, lands)):
        yield pltpu.make_async_remote_copy(
            src_ref=_half(src, name, 1 - c), dst_ref=land, send_sem=send_sems.at[idx], recv_sem=recv_sems.at[idx],
            device_id=(x, y, 1 - c), device_id_type=MESH_ID)


def _pair_exchange_start(tag, tensors):
    names = [n for n, _ in tensors]
    n = len(tensors)
    lands = [lax.empty(_half_shape(nm), f32) for nm in names]

    def body(*refs):
        for cp in _pair_copies(names, refs[:n], refs[n:2 * n], refs[2 * n], refs[2 * n + 1]):
            cp.start()
        refs[-1][...] = jnp.zeros_like(refs[-1])

    args = [g for _, g in tensors] + lands
    outs = pl.pallas_call(
        body, name="grad_pair_start_" + tag,
        out_shape=(pltpu.SemaphoreType.DMA((n,)), pltpu.SemaphoreType.DMA((n,)))
        + tuple(pltpu.HBM(a.shape, a.dtype) for a in args) + (jax.ShapeDtypeStruct((SUBLANES, LANES), f32),),
        in_specs=(HBM,) * (2 * n), out_specs=(SEM, SEM) + (HBM,) * (2 * n) + (pl.BlockSpec(memory_space=pltpu.VMEM),),
        input_output_aliases={t: 2 + t for t in range(2 * n)},
        compiler_params=pltpu.CompilerParams(has_side_effects=pltpu.SideEffectType.DATAFLOW_SIDE_EFFECTING),
    )(*[pltpu.with_memory_space_constraint(a, pltpu.HBM) for a in args])
    return (tag, names, outs[0], outs[1], outs[2:2 + 2 * n]), outs[-1]


def _pair_exchange_wait(state, after):
    tag, names, send_sems, recv_sems, bufs = state
    n = len(names)

    def body(*refs):
        for cp in _pair_copies(names, refs[:n], refs[n:2 * n], refs[2 * n], refs[2 * n + 1]):
            cp.wait_send()
            cp.wait_recv()

    outs = pl.pallas_call(
        body, name="grad_pair_wait_" + tag, out_shape=tuple(pltpu.HBM(a.shape, a.dtype) for a in bufs),
        in_specs=(HBM,) * (2 * n) + (SEM, SEM, ANY), out_specs=(HBM,) * (2 * n),
        input_output_aliases={t: t for t in range(2 * n)},
        compiler_params=pltpu.CompilerParams(has_side_effects=pltpu.SideEffectType.DATAFLOW_SIDE_EFFECTING),
    )(*bufs, send_sems, recv_sems, after)
    return list(zip(names, outs[:n], outs[n:]))


def _pair_add(g, rcv, name, c_arr):
    K, N, ax = BIG[name]
    hr, hc = _half_shape(name)
    T = 128
    nrt = hr // T

    def body(c_ref, g_ref, r_ref, o_ref):
        o_ref[...] = (g_ref[...] + r_ref[...]).astype(bf16)

    if ax == 1:
        g_spec = pl.BlockSpec((T, hc), lambda i, c: (c[0] * nrt + i, 0))
    else:
        g_spec = pl.BlockSpec((T, hc), lambda i, c: (i, c[0]))
    plain = pl.BlockSpec((T, hc), lambda i, c: (i, 0))
    return pl.pallas_call(
        body, name="grad_pair_add", out_shape=jax.ShapeDtypeStruct((hr, hc), bf16),
        grid_spec=pltpu.PrefetchScalarGridSpec(num_scalar_prefetch=1, grid=(nrt,), in_specs=[g_spec, plain],
                                               out_specs=plain),
        compiler_params=_cp(("parallel",), 32),
    )(c_arr, g, rcv)


def _chip_copies(names, srcs, lands, send_sems, recv_sems):
    x, y, c = _position()
    me = 2 * x + y
    idx = 0
    for name, src, land in zip(names, srcs, lands):
        for px, py in _other_chips(x, y):
            def copy(q, row, name=name, src=src, land=land, px=px, py=py, idx=idx):
                return pltpu.make_async_remote_copy(
                    src_ref=_shard_of_half(src, name, q), dst_ref=land.at[row], send_sem=send_sems.at[idx],
                    recv_sem=recv_sems.at[idx], device_id=(px, py, c), device_id_type=MESH_ID)
            yield copy(2 * px + py, me), copy(me, 2 * px + py)
            idx += 1


def _chip_exchange_start(tag, tensors):
    names = [n for n, _ in tensors]
    n = len(tensors)
    lands = [lax.empty((N_CHIPS,) + _shard_half_shape(nm), g.dtype) for nm, g in tensors]

    def body(*refs):
        send_sems, recv_sems = refs[2 * n:2 * n + 2]
        for sent, _ in _chip_copies(names, refs[:n], refs[n:2 * n], send_sems, recv_sems):
            sent.start()
        refs[-1][...] = jnp.zeros_like(refs[-1])

    args = [g for _, g in tensors] + lands
    outs = pl.pallas_call(
        body, name="grad_chip_start_" + tag,
        out_shape=(pltpu.SemaphoreType.DMA((3 * n,)), pltpu.SemaphoreType.DMA((3 * n,)))
        + tuple(pltpu.HBM(a.shape, a.dtype) for a in args) + (jax.ShapeDtypeStruct((SUBLANES, LANES), f32),),
        in_specs=(HBM,) * (2 * n), out_specs=(SEM, SEM) + (HBM,) * (2 * n) + (pl.BlockSpec(memory_space=pltpu.VMEM),),
        input_output_aliases={t: 2 + t for t in range(2 * n)},
        compiler_params=pltpu.CompilerParams(has_side_effects=pltpu.SideEffectType.DATAFLOW_SIDE_EFFECTING),
    )(*[pltpu.with_memory_space_constraint(a, pltpu.HBM) for a in args])
    return (tag, names, outs[0], outs[1], outs[2:2 + 2 * n]), outs[-1]


def _chip_exchange_wait(state, after):
    tag, names, send_sems, recv_sems, bufs = state
    n = len(names)

    def body(*refs):
        for sent, landed in _chip_copies(names, refs[:n], refs[n:2 * n], refs[2 * n], refs[2 * n + 1]):
            sent.wait_send()
            landed.wait_recv()

    outs = pl.pallas_call(
        body, name="grad_chip_wait_" + tag, out_shape=tuple(pltpu.HBM(a.shape, a.dtype) for a in bufs),
        in_specs=(HBM,) * (2 * n) + (SEM, SEM, ANY), out_specs=(HBM,) * (2 * n),
        input_output_aliases={t: t for t in range(2 * n)},
        compiler_params=pltpu.CompilerParams(has_side_effects=pltpu.SideEffectType.DATAFLOW_SIDE_EFFECTING),
    )(*bufs, send_sems, recv_sems, after)
    return list(zip(names, outs[:n], outs[n:]))


def _sum_chips(name, half, land, chip_arr):
    K, N, ax = BIG[name]
    R, C = _shard_half_shape(name)
    T = 64
    nrt = R // T

    def body(p_ref, own_ref, land_ref, o_ref):
        parts = [jnp.where(p_ref[0] == q, own_ref[...], land_ref[q]).astype(f32) for q in range(N_CHIPS)]
        o_ref[...] = ((parts[0] + parts[1]) + parts[2]) + parts[3]

    if ax == 1:
        own_spec = pl.BlockSpec((T, C), lambda i, p: (i, p[0]))
    else:
        own_spec = pl.BlockSpec((T, C), lambda i, p: (p[0] * nrt + i, 0))
    return pl.pallas_call(
        body, name="grad_sum_chips", out_shape=jax.ShapeDtypeStruct((R, C), f32),
        grid_spec=pltpu.PrefetchScalarGridSpec(
            num_scalar_prefetch=1, grid=(nrt,),
            in_specs=[own_spec, pl.BlockSpec((N_CHIPS, T, C), lambda i, p: (0, i, 0))],
            out_specs=pl.BlockSpec((T, C), lambda i, p: (i, 0))),
        compiler_params=_cp(("parallel",), 32),
    )(chip_arr, half, land)


def _pair_swap(halves):
    n_t = len(halves)

    def body(*refs):
        ins = refs[:n_t]
        outs = refs[n_t:2 * n_t]
        send_sems, recv_sems = refs[2 * n_t:]
        x, y, c = _position()
        cps = []
        for t in range(n_t):
            cp = pltpu.make_async_remote_copy(
                src_ref=ins[t], dst_ref=outs[t], send_sem=send_sems.at[t], recv_sem=recv_sems.at[t],
                device_id=(x, y, 1 - c), device_id_type=MESH_ID)
            cp.start()
            cps.append(cp)
        for cp in cps:
            cp.wait()

    return pl.pallas_call(
        body, name="grad_pair_swap", out_shape=tuple(jax.ShapeDtypeStruct(h.shape, h.dtype) for h in halves),
        in_specs=[ANY] * n_t, out_specs=tuple([ANY] * n_t),
        scratch_shapes=[pltpu.SemaphoreType.DMA((n_t,)), pltpu.SemaphoreType.DMA((n_t,))],
    )(*halves)


def _adamw_halves(own, other, w, m, v, name, l, c_arr, prev):
    K, N, ax = BIG[name]
    R, C = _shard_shape(name)
    hr, hc = _shard_half_shape(name)
    T = 64
    nrt = hr // T
    c1 = 1.0 / (1.0 - ADAM_B1 ** ADAM_STEP)
    c2 = 1.0 / (1.0 - ADAM_B2 ** ADAM_STEP)

    def body(c_ref, own_ref, oth_ref, w_ref, m_ref, v_ref, *rest):
        g_ref, d_ref, nm_ref, nv_ref = rest[-4:]
        gg = jnp.where(pl.program_id(0) == c_ref[0], own_ref[...], oth_ref[...])
        nm = ADAM_B1 * m_ref[...] + (1.0 - ADAM_B1) * gg
        nv = ADAM_B2 * v_ref[...] + (1.0 - ADAM_B2) * (gg * gg)
        g_ref[...] = gg
        nm_ref[...] = nm
        nv_ref[...] = nv
        d_ref[...] = -ADAM_LR * ((nm * c1) / (jnp.sqrt(nv * c2) + ADAM_EPS) + ADAM_WD * w_ref[...])

    half = pl.BlockSpec((T, hc), lambda h, i, c: (i, 0))
    if ax == 1:
        full = pl.BlockSpec((None, T, hc), lambda h, i, c: (l, h * nrt + i, 0))
    else:
        full = pl.BlockSpec((None, T, hc), lambda h, i, c: (l, i, h))
    sd = jax.ShapeDtypeStruct((DEPTH, R, C), f32)
    args = [c_arr, own, other, w, m, v]
    in_specs = [half, half, full, full, full]
    aliases = {}
    if prev is not None:
        args += list(prev)
        in_specs += [ANY] * 4
        aliases = {6 + k: k for k in range(4)}
    return pl.pallas_call(
        body, name="adamw_" + name, out_shape=(sd, sd, sd, sd),
        grid_spec=pltpu.PrefetchScalarGridSpec(num_scalar_prefetch=1, grid=(2, nrt), in_specs=in_specs,
                                               out_specs=(full, full, full, full)),
        input_output_aliases=aliases,
        compiler_params=_cp(("arbitrary", "arbitrary"), 32),
    )(*args)


class _GradExchange:
    GROUPS = (("l1", tuple((n, DEPTH - 1) for n in BIG)),
              ("l0_ffn", (("ffn_w_down", 0), ("ffn_w_up", 0))),
              ("l0_out", (("w_out", 0),)),
              ("l0_in", (("w_in", 0),)))

    def __init__(self):
        self.c_arr = jnp.reshape(lax.axis_index("c"), (1,)).astype(jnp.int32)
        self.chip_arr = jnp.reshape(2 * lax.axis_index("x") + lax.axis_index("y"), (1,)).astype(jnp.int32)
        self.grads = {}
        self.pair_started = {}
        self.chip_started = {}

    def _advance(self, after, tok):
        for tag, _ in self.GROUPS:
            if tag not in self.pair_started or tag in self.chip_started:
                continue
            arrived = _pair_exchange_wait(self.pair_started[tag], after)
            pair = [(n, _pair_add(g, r, n, self.c_arr)) for n, g, r in arrived]
            self.chip_started[tag], token = _chip_exchange_start(tag, pair)
            tok = tok + token[0, 0]
        return tok

    def put(self, name, layer, g, tok):
        self.grads[(name, layer)] = g
        tok = self._advance(g, tok)
        for tag, keys in self.GROUPS:
            if tag in self.pair_started or not all(k in self.grads for k in keys):
                continue
            self.pair_started[tag], token = _pair_exchange_start(tag, [(n, self.grads[(n, l)]) for n, l in keys])
            tok = tok + token[0, 0]
        return tok

    def finish(self, after):
        self._advance(after, jnp.zeros((), f32))
        keys, own = [], []
        for tag, group in self.GROUPS:
            landed = _chip_exchange_wait(self.chip_started[tag], after)
            own += [_sum_chips(n, half, land, self.chip_arr) for n, half, land in landed]
            keys += list(group)
        other = _pair_swap(own)
        return dict(zip(keys, zip(own, other)))


def _small_allreduce(buf):
    R = buf.shape[0]

    def body(in_ref, out_ref, sibling, slots, send_sems, recv_sems):
        x, y, c = _position()
        me = 2 * x + y
        swap = pltpu.make_async_remote_copy(
            src_ref=in_ref, dst_ref=sibling, send_sem=send_sems.at[0], recv_sem=recv_sems.at[0],
            device_id=(x, y, 1 - c), device_id_type=MESH_ID)
        swap.start()
        swap.wait()
        slots[me] = in_ref[...] + sibling[...]
        cps = []
        for k, (px, py) in enumerate(_other_chips(x, y)):
            cp = pltpu.make_async_remote_copy(
                src_ref=slots.at[me], dst_ref=slots.at[me], send_sem=send_sems.at[1 + k], recv_sem=recv_sems.at[1 + k],
                device_id=(px, py, c), device_id_type=MESH_ID)
            cp.start()
            cps.append(cp)
        for k, (px, py) in enumerate(_other_chips(x, y)):
            pltpu.make_async_remote_copy(
                src_ref=slots.at[me], dst_ref=slots.at[2 * px + py], send_sem=send_sems.at[1 + k],
                recv_sem=recv_sems.at[1 + k], device_id=(px, py, c), device_id_type=MESH_ID).wait_recv()
        for cp in cps:
            cp.wait_send()
        out_ref[...] = ((slots[0] + slots[1]) + slots[2]) + slots[3]

    vm = pl.BlockSpec(memory_space=pltpu.VMEM)
    return pl.pallas_call(
        body, name="small_allreduce", out_shape=jax.ShapeDtypeStruct((R, 128), f32), in_specs=[vm], out_specs=vm,
        scratch_shapes=[pltpu.VMEM((R, 128), f32), pltpu.VMEM((N_CHIPS, R, 128), f32),
                        pltpu.SemaphoreType.DMA((N_CHIPS,)), pltpu.SemaphoreType.DMA((N_CHIPS,))],
        compiler_params=pltpu.CompilerParams(vmem_limit_bytes=40 * MIB),
    )(buf)


PACK_UNIT = 1024


def _pack(arrs):
    parts = []
    for a in arrs:
        flat = a.reshape(-1)
        n = -(-flat.shape[0] // PACK_UNIT) * PACK_UNIT
        parts.append(jnp.pad(flat, (0, n - flat.shape[0])))
    return jnp.concatenate(parts).reshape(-1, 128)


def _unpack(buf, shapes):
    flat = buf.reshape(-1)
    out, off = [], 0
    for shp in shapes:
        n = int(np.prod(shp))
        out.append(flat[off:off + n].reshape(shp))
        off += -(-n // PACK_UNIT) * PACK_UNIT
    return out


def kernel(x, w_in, b_in, conv_dw_w, conv_dw_b, conv_ln_g, conv_ln_b, rel_bias_table, gmlp_ln_g, gmlp_ln_b, gmlp_w_s, gmlp_b_s, w_out, b_out, ln1_g, ln1_b, ffn_w_up, ffn_b_up, ffn_conv_w, ffn_conv_b, ffn_w_down, ffn_b_down, ln2_g, ln2_b, loss_target, m_w_in, m_b_in, m_conv_dw_w, m_conv_dw_b, m_conv_ln_g, m_conv_ln_b, m_rel_bias_table, m_gmlp_ln_g, m_gmlp_ln_b, m_gmlp_w_s, m_gmlp_b_s, m_w_out, m_b_out, m_ln1_g, m_ln1_b, m_ffn_w_up, m_ffn_b_up, m_ffn_conv_w, m_ffn_conv_b, m_ffn_w_down, m_ffn_b_down, m_ln2_g, m_ln2_b, v_w_in, v_b_in, v_conv_dw_w, v_conv_dw_b, v_conv_ln_g, v_conv_ln_b, v_rel_bias_table, v_gmlp_ln_g, v_gmlp_ln_b, v_gmlp_w_s, v_gmlp_b_s, v_w_out, v_b_out, v_ln1_g, v_ln1_b, v_ffn_w_up, v_ffn_b_up, v_ffn_conv_w, v_ffn_conv_b, v_ffn_w_down, v_ffn_b_down, v_ln2_g, v_ln2_b):
    w = dict(w_in=w_in, b_in=b_in, conv_dw_w=conv_dw_w, conv_dw_b=conv_dw_b, conv_ln_g=conv_ln_g, conv_ln_b=conv_ln_b,
             rel_bias_table=rel_bias_table, gmlp_ln_g=gmlp_ln_g, gmlp_ln_b=gmlp_ln_b, gmlp_w_s=gmlp_w_s,
             gmlp_b_s=gmlp_b_s, w_out=w_out, b_out=b_out, ln1_g=ln1_g, ln1_b=ln1_b, ffn_w_up=ffn_w_up,
             ffn_b_up=ffn_b_up, ffn_conv_w=ffn_conv_w, ffn_conv_b=ffn_conv_b, ffn_w_down=ffn_w_down,
             ffn_b_down=ffn_b_down, ln2_g=ln2_g, ln2_b=ln2_b)
    m = dict(w_in=m_w_in, b_in=m_b_in, conv_dw_w=m_conv_dw_w, conv_dw_b=m_conv_dw_b, conv_ln_g=m_conv_ln_g,
             conv_ln_b=m_conv_ln_b, rel_bias_table=m_rel_bias_table, gmlp_ln_g=m_gmlp_ln_g, gmlp_ln_b=m_gmlp_ln_b,
             gmlp_w_s=m_gmlp_w_s, gmlp_b_s=m_gmlp_b_s, w_out=m_w_out, b_out=m_b_out, ln1_g=m_ln1_g, ln1_b=m_ln1_b,
             ffn_w_up=m_ffn_w_up, ffn_b_up=m_ffn_b_up, ffn_conv_w=m_ffn_conv_w, ffn_conv_b=m_ffn_conv_b,
             ffn_w_down=m_ffn_w_down, ffn_b_down=m_ffn_b_down, ln2_g=m_ln2_g, ln2_b=m_ln2_b)
    v = dict(w_in=v_w_in, b_in=v_b_in, conv_dw_w=v_conv_dw_w, conv_dw_b=v_conv_dw_b, conv_ln_g=v_conv_ln_g,
             conv_ln_b=v_conv_ln_b, rel_bias_table=v_rel_bias_table, gmlp_ln_g=v_gmlp_ln_g, gmlp_ln_b=v_gmlp_ln_b,
             gmlp_w_s=v_gmlp_w_s, gmlp_b_s=v_gmlp_b_s, w_out=v_w_out, b_out=v_b_out, ln1_g=v_ln1_g, ln1_b=v_ln1_b,
             ffn_w_up=v_ffn_w_up, ffn_b_up=v_ffn_b_up, ffn_conv_w=v_ffn_conv_w, ffn_conv_b=v_ffn_conv_b,
             ffn_w_down=v_ffn_w_down, ffn_b_down=v_ffn_b_down, ln2_g=v_ln2_g, ln2_b=v_ln2_b)

    chip_arr = jnp.reshape(2 * lax.axis_index("x") + lax.axis_index("y"), (1,)).astype(jnp.int32)
    shards = {"w_in": _cast_bf16(w_in.reshape(-1, w_in.shape[-1])).reshape(w_in.shape)}
    wb, conv_stack, fconv_stack = _gather_weights(shards, conv_dw_w, ffn_conv_w)
    send_sems, recv_sems, in_flight, token = _gather_start(
        [_cast_into_full(w[n], n, chip_arr) for n in LATE_WEIGHTS], conv_stack)
    sp = {n: w[n] for n in SMALL}
    sp["conv_dw_w"] = jnp.moveaxis(conv_stack, 0, 2).reshape(DEPTH, CONV_WIDTH, CONV_CH)
    sp["ffn_conv_w"] = jnp.moveaxis(fconv_stack, 0, 2).reshape(DEPTH, FFN_CONV_WIDTH, 2 * D_FF)
    sp["b_in"] = sp["b_in"] + token[0, 0]

    def late_weights(after):
        return dict(zip(LATE_WEIGHTS, _gather_wait(send_sems, recv_sems, in_flight, after)))

    sink = _GradExchange()
    loss_local, grad_x, grads, big = _local_step(x[0], loss_target[0], wb, late_weights, sp, sink)

    small_shapes = [(1,)] + [grads[n].shape for n in SMALL]
    summed = _unpack(_small_allreduce(_pack([loss_local.reshape(1)] + [grads[n] for n in SMALL])), small_shapes)
    loss = summed[0].reshape(())
    small = dict(zip(SMALL, summed[1:]))
    chip = 2 * lax.axis_index("x") + lax.axis_index("y")
    for n in SMALL_SHARDED:
        width = w[n].shape[-1]
        small[n] = lax.dynamic_slice_in_dim(small[n], chip * width, width, axis=2)

    g_out, d_out, m_out, v_out = {}, {}, {}, {}
    for n in BIG:
        outs = None
        for l in range(DEPTH):
            own, other = big[(n, l)]
            outs = _adamw_halves(own, other, w[n], m[n], v[n], n, l, sink.c_arr, outs)
        g_out[n], d_out[n], m_out[n], v_out[n] = outs
    shapes = [small[n].shape for n in SMALL]
    packed = [_pack([src[n] for n in SMALL]) for src in (small, w, m, v)]
    upd = _adamw(*packed, "adamw_small")
    for dst, buf in zip((d_out, m_out, v_out), upd):
        dst.update(zip(SMALL, _unpack(buf, shapes)))
    g_out.update(small)

    return (loss, grad_x[None], *[g_out[n] for n in WEIGHTS], *[d_out[n] for n in WEIGHTS],
            *[m_out[n] for n in WEIGHTS], *[v_out[n] for n in WEIGHTS])
```

```python
import functools
import math

import numpy as np
import jax
import jax.numpy as jnp
from jax import lax
from jax.experimental import pallas as pl
from jax.experimental.pallas import tpu as pltpu

f32 = jnp.float32
bf16 = jnp.bfloat16

D_MODEL = 1024
DEPTH = 2
HEAD_DIM = 64
CONV_CH = 256
CONV_WIDTH = 31
ATTN_HEADS = 8
ATTN_CH = ATTN_HEADS * HEAD_DIM
DILATIONS = (1, 4, 16)
ATTN_BLOCK = 128
N_BUCKETS = 32
MAX_DISTANCE = 2048
GMLP_CH = 256
GMLP_GROUPS = 4
GMLP_GROUP_DIM = GMLP_CH // GMLP_GROUPS
CHUNK = 128
IN_CH = 2 * CONV_CH + 3 * ATTN_CH + 2 * GMLP_CH
D_FF = 2816
FFN_CONV_WIDTH = 3
LN_EPS = 1e-5
ALPHA = (2.0 * DEPTH) ** 0.25
ADAM_LR = 0.001
ADAM_B1 = 0.9
ADAM_B2 = 0.999
ADAM_EPS = 1e-08
ADAM_WD = 0.01
ADAM_STEP = 10

CONV_HALO = 32
FFN_HALO = 8
NEG = -1e30
MIB = 2 ** 20
NT_DIMS = (((1,), (1,)), ((), ()))
TN_DIMS = (((0,), (0,)), ((), ()))
MESH_ID = pl.DeviceIdType.MESH


def _cp(sem, vmem_mib):
    return pltpu.CompilerParams(dimension_semantics=sem, vmem_limit_bytes=vmem_mib * MIB)


def _resident(shape):
    nd = len(shape)
    return pl.BlockSpec(shape, lambda *_: (0,) * nd, pipeline_mode=pl.Buffered(1))


def _acc(shape):
    nd = len(shape)
    return pl.BlockSpec(shape, lambda *_: (0,) * nd)


def _sig(x):
    return 1.0 / (1.0 + jnp.exp(-x))


def _ln_stats(z):
    mu = jnp.mean(z, axis=-1, keepdims=True)
    zc = z - mu
    var = jnp.mean(zc * zc, axis=-1, keepdims=True)
    rstd = lax.rsqrt(var + LN_EPS)
    return zc * rstd, rstd


def _ln_bwd(dy, xhat, rstd, g):
    dxh = dy * g
    m1 = jnp.mean(dxh, axis=-1, keepdims=True)
    m2 = jnp.mean(dxh * xhat, axis=-1, keepdims=True)
    return rstd * (dxh - m1 - xhat * m2)


def _colsum(x):
    return jnp.sum(x, axis=0, keepdims=True)


def _t5_bucket_np(dist):
    max_exact = N_BUCKETS // 2
    dd = np.maximum(dist, 1).astype(np.float64)
    large = max_exact + (np.log(dd / max_exact) / math.log(MAX_DISTANCE / max_exact)
                         * (N_BUCKETS - max_exact)).astype(np.int32)
    large = np.minimum(large, N_BUCKETS - 1)
    return np.where(dist < max_exact, dist, large).astype(np.int32)


def _bucket_ids():
    qi = np.arange(ATTN_BLOCK)[:, None]
    kj = np.arange(2 * ATTN_BLOCK)[None, :]
    dist = np.clip(qi + ATTN_BLOCK - kj, 0, None)
    return np.stack([_t5_bucket_np(dist * d) for d in DILATIONS]).astype(np.int32)


LANES = 128
QKV_CH = 3 * ATTN_CH
PERM_TILE = 512


def _slabs(n, rows):
    return [pltpu.VMEM((rows, LANES), f32)] * n


def _rows_of(slab, r, n, d):
    return slab[...] if d == 1 else slab[pl.ds(r, n, stride=d), :]


def _set_rows_of(slab, r, n, d, val):
    if d == 1:
        slab[...] = val
    else:
        slab[pl.ds(r, n, stride=d), :] = val


def _perm_spec(d, ch):
    return pl.BlockSpec((d, PERM_TILE // d, ch), lambda i: (0, i, 0))


def _perm_shape(S, d, ch, dtype):
    return jax.ShapeDtypeStruct((d, S // d, ch), dtype)


def _inproj_fwd(x, w, b):
    S = x.shape[0]
    T = PERM_TILE
    nsl = QKV_CH // LANES

    def body(x_ref, w_ref, b_ref, a_ref, c_ref, *rest):
        q_refs = rest[:len(DILATIONS)]
        slabs = rest[len(DILATIONS):]
        h = jnp.dot(x_ref[...].astype(bf16), w_ref[...], preferred_element_type=f32) + b_ref[...]
        a_ref[...] = h[:, :2 * CONV_CH]
        q0 = 2 * CONV_CH
        c_ref[...] = h[:, q0 + QKV_CH:]
        for j in range(nsl):
            piece = h[:, q0 + LANES * j:q0 + LANES * (j + 1)]
            if LANES * j < ATTN_CH:
                piece = piece * (HEAD_DIM ** -0.5)
            slabs[j][...] = piece
        for d, q_ref in zip(DILATIONS, q_refs):
            for r in range(d):
                for j in range(nsl):
                    q_ref[r, :, LANES * j:LANES * (j + 1)] = _rows_of(slabs[j], r, T // d, d).astype(bf16)

    row = lambda c: pl.BlockSpec((T, c), lambda i: (i, 0))
    return pl.pallas_call(
        body, grid=(S // T,), name="inproj_fwd",
        out_shape=(jax.ShapeDtypeStruct((S, 2 * CONV_CH), f32), jax.ShapeDtypeStruct((S, 2 * GMLP_CH), f32))
        + tuple(_perm_shape(S, d, QKV_CH, bf16) for d in DILATIONS),
        in_specs=[row(D_MODEL), _resident((D_MODEL, IN_CH)), _resident((1, IN_CH))],
        out_specs=(row(2 * CONV_CH), row(2 * GMLP_CH)) + tuple(_perm_spec(d, QKV_CH) for d in DILATIONS),
        scratch_shapes=_slabs(nsl, T),
        compiler_params=_cp(("parallel",), 48),
    )(x, w, b)


CONV_GROUP = 64


def _window_rolls(starts):
    groups = {}
    for s in starts:
        groups.setdefault((-s) % SUBLANES, []).append(s)
    return dict(sorted(groups.items()))


def _conv_fwd(a_in, dw_w, dw_b, ln_g, ln_b):
    S = a_in.shape[0]
    T = 512
    hb = T // CONV_HALO

    def body(a_ref, halo_ref, w_ref, b_ref, g_ref, be_ref, out_ref, hc_ref, buf):
        i = pl.program_id(0)
        am = a_ref[...]
        ah = halo_ref[...]
        hgh = ah[:, :CONV_CH] * _sig(ah[:, CONV_CH:])
        buf[0:CONV_HALO, :] = jnp.where(i > 0, hgh, 0.0)
        buf[CONV_HALO:, :] = am[:, :CONV_CH] * _sig(am[:, CONV_CH:])
        starts = _window_rolls(range(CONV_HALO - (CONV_WIDTH - 1), CONV_HALO + 1))
        slabs = [slice(LANES * j, LANES * (j + 1)) for j in range(CONV_CH // LANES)]

        def step(g, _):
            r0 = pl.multiple_of(g * CONV_GROUP, CONV_GROUP)
            rows = pl.ds(r0, CONV_GROUP)
            for cs in slabs:
                ext = buf[pl.ds(r0, CONV_GROUP + CONV_HALO), cs]
                acc = jnp.broadcast_to(b_ref[:, cs], (CONV_GROUP, LANES))
                for b, ss in starts.items():
                    rolled = ext if b == 0 else pltpu.roll(ext, b, 0)
                    for s in ss:
                        k = s - (CONV_HALO - (CONV_WIDTH - 1))
                        acc = acc + w_ref[k:k + 1, cs] * rolled[s + b:s + b + CONV_GROUP]
                hc_ref[rows, cs] = acc
            return 0

        lax.fori_loop(0, T // CONV_GROUP, step, 0)
        xhat, _ = _ln_stats(hc_ref[...])
        y = xhat * g_ref[...] + be_ref[...]
        out_ref[...] = (y * _sig(y)).astype(bf16)

    return pl.pallas_call(
        body, grid=(S // T,), name="conv_fwd",
        out_shape=(jax.ShapeDtypeStruct((S, CONV_CH), bf16), jax.ShapeDtypeStruct((S, CONV_CH), f32)),
        in_specs=[pl.BlockSpec((T, 2 * CONV_CH), lambda i: (i, 0)),
                  pl.BlockSpec((CONV_HALO, 2 * CONV_CH), lambda i: (jnp.maximum(i * hb - 1, 0), 0)),
                  _acc((32, CONV_CH)), _acc((1, CONV_CH)), _acc((1, CONV_CH)), _acc((1, CONV_CH))],
        out_specs=(pl.BlockSpec((T, CONV_CH), lambda i: (i, 0)), pl.BlockSpec((T, CONV_CH), lambda i: (i, 0))),
        scratch_shapes=[pltpu.VMEM((T + CONV_HALO, CONV_CH), f32)],
        compiler_params=_cp(("parallel",), 32),
    )(a_in, a_in, dw_w, dw_b, ln_g, ln_b)


def _bias_build(table, buckets):
    def body(t_ref, bk_ref, o_ref):
        h = pl.program_id(1)
        ids = bk_ref[0]
        acc = jnp.zeros((ATTN_BLOCK, 2 * ATTN_BLOCK), f32)
        for b in range(N_BUCKETS):
            acc = jnp.where(ids == b, t_ref[b, h], acc)
        o_ref[0, 0] = acc

    return pl.pallas_call(
        body, grid=(len(DILATIONS), ATTN_HEADS), name="bias_build",
        out_shape=jax.ShapeDtypeStruct((len(DILATIONS), ATTN_HEADS, ATTN_BLOCK, 2 * ATTN_BLOCK), f32),
        in_specs=[pl.BlockSpec(memory_space=pltpu.SMEM),
                  pl.BlockSpec((1, ATTN_BLOCK, 2 * ATTN_BLOCK), lambda p, h: (p, 0, 0))],
        out_specs=pl.BlockSpec((1, 1, ATTN_BLOCK, 2 * ATTN_BLOCK), lambda p, h: (p, h, 0, 0)),
        compiler_params=_cp(("arbitrary", "arbitrary"), 16),
    )(table, buckets)


def _head_tile(tile, h, col):
    lane_head = lax.broadcasted_iota(jnp.int32, tile.shape, 1) // 16
    return jnp.where(lane_head == h, col, tile)


HEAD_PAIRS = ATTN_HEADS // 2
UNITS_PER_BLOCK = ATTN_HEADS


def _attn_tile(L):
    return min(512, L)


def _band_mask(first_block, n):
    B = ATTN_BLOCK
    row = lax.broadcasted_iota(jnp.int32, (B, 2 * B), 0)
    col = lax.broadcasted_iota(jnp.int32, (B, 2 * B), 1)
    valid = (col >= row) & (col <= row + B)
    if first_block:
        valid = valid & ((col >= B) | (n > 0))
    return valid


def _head_lanes(a):
    lane = lax.broadcasted_iota(jnp.int32, (ATTN_BLOCK, LANES), 1)
    return (lane < HEAD_DIM) if a == 0 else (lane >= HEAD_DIM)


def _pair_keys(cur_ref, halo_ref, part, b, j):
    B = ATTN_BLOCK
    c0 = part * ATTN_CH + LANES * j
    own = cur_ref[B * b:B * (b + 1), c0:c0 + LANES]
    prev = halo_ref[:, LANES * j:LANES * (j + 1)] if b == 0 else cur_ref[B * (b - 1):B * b, c0:c0 + LANES]
    return jnp.concatenate([prev, own], axis=0)


def _attn_fwd_pattern(qkv, bias, d):
    _, L, _ = qkv.shape
    B = ATTN_BLOCK
    QB = _attn_tile(L)
    nsb = QB // B
    U = nsb * UNITS_PER_BLOCK

    def body(cur_ref, hk_ref, hv_ref, b_ref, o_ref, lse_ref, lg, pb):
        n = pl.program_id(1)
        for b in range(nsb):
            valid = _band_mask(b == 0, n)
            for j in range(HEAD_PAIRS):
                q2 = cur_ref[B * b:B * (b + 1), LANES * j:LANES * (j + 1)]
                k2 = _pair_keys(cur_ref, hk_ref, 1, b, j)
                for a in range(2):
                    u = (b * HEAD_PAIRS + j) * 2 + a
                    qm = jnp.where(_head_lanes(a), q2, jnp.zeros_like(q2))
                    logits = lax.dot_general(qm, k2, NT_DIMS, preferred_element_type=f32) + b_ref[2 * j + a]
                    lg[B * u:B * (u + 1), :] = jnp.where(valid, logits, NEG)
        m = jnp.max(lg[...], axis=1, keepdims=True)
        p = jnp.exp(lg[...] - m)
        s = jnp.sum(p, axis=1, keepdims=True)
        pb[...] = p.astype(bf16)
        lse = m + jnp.log(s)
        inv = 1.0 / s
        for b in range(nsb):
            tile = jnp.zeros((B, B), f32)
            for j in range(HEAD_PAIRS):
                v2 = _pair_keys(cur_ref, hv_ref, 2, b, j)
                outs = []
                for a in range(2):
                    u = (b * HEAD_PAIRS + j) * 2 + a
                    rows = slice(B * u, B * (u + 1))
                    outs.append(jnp.dot(pb[rows, :], v2, preferred_element_type=f32) * inv[rows])
                    tile = _head_tile(tile, 2 * j + a, lse[rows])
                o_ref[B * b:B * (b + 1), LANES * j:LANES * (j + 1)] = jnp.where(_head_lanes(0), outs[0], outs[1])
            lse_ref[B * b:B * (b + 1), :] = tile

    halo = lambda part: pl.BlockSpec((None, B, ATTN_CH), lambda r, n: (r, jnp.maximum(n * nsb - 1, 0), part))
    tile_spec = lambda c: pl.BlockSpec((None, QB, c), lambda r, n: (r, n, 0))
    return pl.pallas_call(
        body, grid=(d, L // QB), name=f"attn_fwd_d{d}",
        out_shape=(jax.ShapeDtypeStruct((d, L, ATTN_CH), f32), jax.ShapeDtypeStruct((d, L, B), f32)),
        in_specs=[tile_spec(QKV_CH), halo(1), halo(2), _resident((ATTN_HEADS, B, 2 * B))],
        out_specs=(tile_spec(ATTN_CH), tile_spec(B)),
        scratch_shapes=[pltpu.VMEM((U * B, 2 * B), f32), pltpu.VMEM((U * B, 2 * B), bf16)],
        compiler_params=_cp(("parallel", "parallel"), 40),
    )(qkv, qkv, qkv, bias)


def _attn_merge(parts):
    S = parts[0][0].shape[0] * parts[0][0].shape[1]
    T = PERM_TILE
    nsl = ATTN_CH // LANES
    n_p = len(DILATIONS)

    def body(*refs):
        ins = refs[:2 * n_p]
        out_ref, lse_ref = refs[2 * n_p:2 * n_p + 2]
        slabs = refs[2 * n_p + 2:]
        lses = []
        for p, d in enumerate(DILATIONS):
            o_ref, l_ref = ins[2 * p], ins[2 * p + 1]
            osl = slabs[p * (nsl + 1):p * (nsl + 1) + nsl]
            lsl = slabs[p * (nsl + 1) + nsl]
            for r in range(d):
                for j in range(nsl):
                    _set_rows_of(osl[j], r, T // d, d, o_ref[r, :, LANES * j:LANES * (j + 1)])
                _set_rows_of(lsl, r, T // d, d, l_ref[r])
            lses.append(lsl[...])
        big = functools.reduce(jnp.maximum, lses)
        ws = [jnp.exp(l - big) for l in lses]
        tot = functools.reduce(lambda a_, b_: a_ + b_, ws)
        lse_ref[...] = big + jnp.log(tot)
        ws = [w / tot for w in ws]
        for j in range(nsl):
            acc = jnp.zeros((T, LANES), f32)
            for p in range(n_p):
                wa = ws[p][:, 32 * j:32 * j + 1]
                wb = ws[p][:, 32 * j + 16:32 * j + 17]
                lane = lax.broadcasted_iota(jnp.int32, (T, LANES), 1)
                acc = acc + jnp.where(lane < HEAD_DIM, wa, wb) * slabs[p * (nsl + 1) + j][...]
            out_ref[:, LANES * j:LANES * (j + 1)] = acc.astype(bf16)

    in_specs, args = [], []
    for (o, l), d in zip(parts, DILATIONS):
        in_specs += [_perm_spec(d, ATTN_CH), _perm_spec(d, ATTN_BLOCK)]
        args += [o, l]
    row = lambda c: pl.BlockSpec((T, c), lambda i: (i, 0))
    return pl.pallas_call(
        body, grid=(S // T,), name="attn_merge",
        out_shape=(jax.ShapeDtypeStruct((S, ATTN_CH), bf16), jax.ShapeDtypeStruct((S, ATTN_BLOCK), f32)),
        in_specs=in_specs, out_specs=(row(ATTN_CH), row(ATTN_BLOCK)),
        scratch_shapes=_slabs(n_p * (nsl + 1), T),
        compiler_params=_cp(("parallel",), 40),
    )(*args)


def _attn_fwd(qkvs, bias):
    parts = [_attn_fwd_pattern(q, bias[p], d) for p, (q, d) in enumerate(zip(qkvs, DILATIONS))]
    return _attn_merge(parts)


def _tril_bf16(w):
    row = lax.broadcasted_iota(jnp.int32, (CHUNK, CHUNK), 0)
    col = lax.broadcasted_iota(jnp.int32, (CHUNK, CHUNK), 1)
    return jnp.where(col <= row, w, 0.0).astype(bf16)


def _gmlp_fwd(c_in, ln_g, ln_b, w_s, b_s_t):
    S = c_in.shape[0]
    T = 512

    def body(c_ref, g_ref, be_ref, w_ref, bs_ref, out_ref, mix):
        c = c_ref[...]
        xhat, _ = _ln_stats(c[:, GMLP_CH:])
        vb = (xhat * g_ref[...] + be_ref[...]).astype(bf16)
        for g in range(GMLP_GROUPS):
            wt = _tril_bf16(w_ref[g])
            cs = slice(GMLP_GROUP_DIM * g, GMLP_GROUP_DIM * (g + 1))
            for ci in range(T // CHUNK):
                rs = slice(CHUNK * ci, CHUNK * (ci + 1))
                mix[rs, cs] = jnp.dot(wt, vb[rs, cs], preferred_element_type=f32) + bs_ref[:, g:g + 1]
        out_ref[...] = (c[:, :GMLP_CH] * mix[...]).astype(bf16)

    return pl.pallas_call(
        body, grid=(S // T,), name="gmlp_fwd",
        out_shape=jax.ShapeDtypeStruct((S, GMLP_CH), bf16),
        in_specs=[pl.BlockSpec((T, 2 * GMLP_CH), lambda i: (i, 0)), _acc((1, GMLP_CH)), _acc((1, GMLP_CH)),
                  _acc((GMLP_GROUPS, CHUNK, CHUNK)), _acc((CHUNK, GMLP_GROUPS))],
        out_specs=pl.BlockSpec((T, GMLP_CH), lambda i: (i, 0)),
        scratch_shapes=[pltpu.VMEM((T, GMLP_CH), f32)],
        compiler_params=_cp(("parallel",), 32),
    )(c_in, ln_g, ln_b, w_s, b_s_t)


def _outproj_ln_fwd(conv_out, attn_out, gm_out, w, b, x, ln_g, ln_b):
    S = x.shape[0]
    T = 512

    def body(co_ref, ao_ref, go_ref, w_ref, b_ref, x_ref, g_ref, be_ref, cat_ref, z_ref, yb_ref):
        cat = jnp.concatenate([co_ref[...], ao_ref[...], go_ref[...]], axis=1)
        cat_ref[...] = cat
        z = jnp.dot(cat, w_ref[...], preferred_element_type=f32) + b_ref[...] + ALPHA * x_ref[...]
        z_ref[...] = z
        xhat, _ = _ln_stats(z)
        yb_ref[...] = (xhat * g_ref[...] + be_ref[...]).astype(bf16)

    row = lambda c: pl.BlockSpec((T, c), lambda i: (i, 0))
    return pl.pallas_call(
        body, grid=(S // T,), name="outproj_ln_fwd",
        out_shape=(jax.ShapeDtypeStruct((S, D_MODEL), bf16), jax.ShapeDtypeStruct((S, D_MODEL), f32),
                   jax.ShapeDtypeStruct((S, D_MODEL), bf16)),
        in_specs=[row(CONV_CH), row(ATTN_CH), row(GMLP_CH), _resident((D_MODEL, D_MODEL)), _acc((1, D_MODEL)),
                  row(D_MODEL), _acc((1, D_MODEL)), _acc((1, D_MODEL))],
        out_specs=(row(D_MODEL), row(D_MODEL), row(D_MODEL)),
        compiler_params=_cp(("parallel",), 40),
    )(conv_out, attn_out, gm_out, w, b, x, ln_g, ln_b)


GATE_ROWS = 32
GATE_COLS = 128
GATE_MM_COLS = 256
SUBLANES = 8


def _gate_cols(c0):
    return slice(c0, c0 + GATE_COLS), slice(D_FF + c0, D_FF + c0 + GATE_COLS)


def _bcast_rows(ref, k, cs):
    return jnp.broadcast_to(ref[k:k + 1, cs], (GATE_ROWS, GATE_COLS))


def _fold_rows(z):
    acc = z[0:SUBLANES]
    for r in range(SUBLANES, GATE_ROWS, SUBLANES):
        acc = acc + z[r:r + SUBLANES]
    return acc


def _ffn_up_gate_fwd(x1b, w, b, conv_w, conv_b):
    S = x1b.shape[0]
    T = 256
    H = FFN_HALO
    K = FFN_CONV_WIDTH

    def body(x_ref, w_ref, b_ref, cw_ref, cb_ref, hfb_ref, hc_ref, act_ref, hbuf, carry):
        @pl.when(pl.program_id(0) == 0)
        def _():
            carry[...] = jnp.zeros_like(carry)
        x = x_ref[...]
        for m0 in range(0, D_FF, GATE_MM_COLS):
            for cm in (slice(m0, m0 + GATE_MM_COLS), slice(D_FF + m0, D_FF + m0 + GATE_MM_COLS)):
                h = jnp.dot(x, w_ref[:, cm], preferred_element_type=f32) + b_ref[:, cm]
                hbuf[:, cm] = h
                hfb_ref[:, cm] = h.astype(bf16)
            for c0 in range(m0, m0 + GATE_MM_COLS, GATE_COLS):
                cols = _gate_cols(c0)
                wts = [[_bcast_rows(cw_ref, k, cs) for k in range(K)] + [_bcast_rows(cb_ref, 0, cs)] for cs in cols]

                def step(rg, tails, cols=cols, wts=wts):
                    rows = pl.ds(pl.multiple_of(rg * GATE_ROWS, GATE_ROWS), GATE_ROWS)
                    hc, new_tails = [], []
                    for cs, wt, tail in zip(cols, wts, tails):
                        h = hbuf[rows, cs]
                        ext = jnp.concatenate([tail, h], axis=0)
                        acc = wt[K] + wt[K - 1] * h
                        for back in range(1, K):
                            acc = acc + wt[K - 1 - back] * pltpu.roll(ext, back, 0)[H:]
                        hc_ref[rows, cs] = acc
                        hc.append(acc)
                        new_tails.append(h[GATE_ROWS - H:])
                    act_ref[rows, cols[0]] = (hc[0] * _sig(hc[0]) * hc[1]).astype(bf16)
                    return tuple(new_tails)

                tails = lax.fori_loop(0, T // GATE_ROWS, step, tuple(carry[:, cs] for cs in cols), unroll=True)
                for cs, tail in zip(cols, tails):
                    carry[:, cs] = tail

    row = lambda c: pl.BlockSpec((T, c), lambda i: (i, 0))
    return pl.pallas_call(
        body, grid=(S // T,), name="ffn_up_gate_fwd",
        out_shape=(jax.ShapeDtypeStruct((S, 2 * D_FF), bf16), jax.ShapeDtypeStruct((S, 2 * D_FF), f32),
                   jax.ShapeDtypeStruct((S, D_FF), bf16)),
        in_specs=[row(D_MODEL), _resident((D_MODEL, 2 * D_FF)), _acc((1, 2 * D_FF)), _acc((8, 2 * D_FF)),
                  _acc((1, 2 * D_FF))],
        out_specs=(row(2 * D_FF), row(2 * D_FF), row(D_FF)),
        scratch_shapes=[pltpu.VMEM((T, 2 * D_FF), f32), pltpu.VMEM((H, 2 * D_FF), f32)],
        compiler_params=_cp(("arbitrary",), 56),
    )(x1b, w, b, conv_w, conv_b)


def _ffn_down_ln_fwd(act, w, b, z1, ln1_g, ln1_b, ln_g, ln_b):
    S = act.shape[0]
    T = 512

    def body(a_ref, w_ref, b_ref, z1_ref, g1_ref, be1_ref, g_ref, be_ref, z_ref, y_ref):
        subs = [slice(s0, s0 + T // 2) for s0 in (0, T // 2)]
        zs = [jnp.dot(a_ref[rs, :], w_ref[...], preferred_element_type=f32) + b_ref[...]
              + ALPHA * (_ln_stats(z1_ref[rs, :])[0] * g1_ref[...] + be1_ref[...]) for rs in subs]
        for rs, z in zip(subs, zs):
            z_ref[rs, :] = z
            xhat, _ = _ln_stats(z)
            y_ref[rs, :] = xhat * g_ref[...] + be_ref[...]

    row = lambda c: pl.BlockSpec((T, c), lambda i: (i, 0))
    return pl.pallas_call(
        body, grid=(S // T,), name="ffn_down_ln_fwd",
        out_shape=(jax.ShapeDtypeStruct((S, D_MODEL), f32), jax.ShapeDtypeStruct((S, D_MODEL), f32)),
        in_specs=[row(D_FF), _resident((D_FF, D_MODEL)), _acc((1, D_MODEL)), row(D_MODEL)] + [_acc((1, D_MODEL))] * 4,
        out_specs=(row(D_MODEL), row(D_MODEL)),
        compiler_params=_cp(("parallel",), 40),
    )(act, w, b, z1, ln1_g, ln1_b, ln_g, ln_b)


def _ffn_down_ln_loss(act, w, b, z1, ln1_g, ln1_b, ln_g, ln_b, target):
    S = act.shape[0]
    T = 512

    def body(a_ref, w_ref, b_ref, z1_ref, g1_ref, be1_ref, g_ref, be_ref, t_ref, dz_ref, dzb_ref, loss_ref, dg_ref,
             db_ref):
        @pl.when(pl.program_id(0) == 0)
        def _():
            loss_ref[...] = jnp.zeros_like(loss_ref)
            dg_ref[...] = jnp.zeros_like(dg_ref)
            db_ref[...] = jnp.zeros_like(db_ref)
        subs = [slice(s0, s0 + T // 2) for s0 in (0, T // 2)]
        zs = [jnp.dot(a_ref[rs, :], w_ref[...], preferred_element_type=f32) + b_ref[...]
              + ALPHA * (_ln_stats(z1_ref[rs, :])[0] * g1_ref[...] + be1_ref[...]) for rs in subs]
        for rs, z in zip(subs, zs):
            xhat, rstd = _ln_stats(z)
            err = xhat * g_ref[...] + be_ref[...] - t_ref[rs, :]
            loss_ref[...] += _colsum(err * err) * (0.5 / D_MODEL)
            dy = err * (1.0 / D_MODEL)
            dz = _ln_bwd(dy, xhat, rstd, g_ref[...])
            dz_ref[rs, :] = dz
            dzb_ref[rs, :] = dz.astype(bf16)
            dg_ref[...] += _colsum(dy * xhat)
            db_ref[...] += _colsum(dy)

    row = lambda c: pl.BlockSpec((T, c), lambda i: (i, 0))
    vec = jax.ShapeDtypeStruct((1, D_MODEL), f32)
    return pl.pallas_call(
        body, grid=(S // T,), name="ffn_down_ln_loss",
        out_shape=(jax.ShapeDtypeStruct((S, D_MODEL), f32), jax.ShapeDtypeStruct((S, D_MODEL), bf16), vec, vec, vec),
        in_specs=[row(D_FF), _resident((D_FF, D_MODEL)), _acc((1, D_MODEL)), row(D_MODEL)] + [_acc((1, D_MODEL))] * 4
        + [row(D_MODEL)],
        out_specs=(row(D_MODEL), row(D_MODEL), _acc((1, D_MODEL)), _acc((1, D_MODEL)), _acc((1, D_MODEL))),
        compiler_params=_cp(("arbitrary",), 40),
    )(act, w, b, z1, ln1_g, ln1_b, ln_g, ln_b, target)


def _dgrad_ln_bwd(g, w, dz_res, z, ln_g, name):
    S, K = g.shape
    SUB = 256
    T = 2 * SUB if S % (2 * SUB) == 0 else SUB
    with_ln = z is not None

    def body(*refs):
        if with_ln:
            g_ref, w_ref, r_ref, z_ref, lg_ref, dz_ref, dzb_ref, dg_ref, db_ref = refs
        else:
            g_ref, w_ref, r_ref, dx_ref = refs
        subs = [slice(s0, s0 + SUB) for s0 in range(0, T, SUB)]
        dxs = [lax.dot_general(g_ref[rs, :], w_ref[...], NT_DIMS, preferred_element_type=f32) + ALPHA * r_ref[rs, :]
               for rs in subs]
        if not with_ln:
            for rs, dx in zip(subs, dxs):
                dx_ref[rs, :] = dx
            return

        @pl.when(pl.program_id(0) == 0)
        def _():
            dg_ref[...] = jnp.zeros_like(dg_ref)
            db_ref[...] = jnp.zeros_like(db_ref)
        for rs, dx in zip(subs, dxs):
            xhat, rstd = _ln_stats(z_ref[rs, :])
            dz = _ln_bwd(dx, xhat, rstd, lg_ref[...])
            dz_ref[rs, :] = dz
            dzb_ref[rs, :] = dz.astype(bf16)
            dg_ref[...] += _colsum(dx * xhat)
            db_ref[...] += _colsum(dx)

    row = pl.BlockSpec((T, D_MODEL), lambda i: (i, 0))
    vec = jax.ShapeDtypeStruct((1, D_MODEL), f32)
    in_specs = [pl.BlockSpec((T, K), lambda i: (i, 0)), _resident((D_MODEL, K)), row]
    args = [g, w, dz_res]
    if with_ln:
        in_specs += [row, _acc((1, D_MODEL))]
        args += [z, ln_g]
        out_shape = (jax.ShapeDtypeStruct((S, D_MODEL), f32), jax.ShapeDtypeStruct((S, D_MODEL), bf16), vec, vec)
        out_specs = (row, row, _acc((1, D_MODEL)), _acc((1, D_MODEL)))
    else:
        out_shape = jax.ShapeDtypeStruct((S, D_MODEL), f32)
        out_specs = row
    return pl.pallas_call(
        body, grid=(S // T,), name=name, out_shape=out_shape, in_specs=in_specs, out_specs=out_specs,
        compiler_params=_cp(("arbitrary",), 48),
    )(*args)


def _ffn_down_gate_bwd(dzb, w_down, hfb, hc, conv_w):
    S = hc.shape[0]
    T = 256
    H = FFN_HALO
    nt = S // T
    K = FFN_CONV_WIDTH

    def body(dz_ref, w_ref, h_ref, hc_ref, cw_ref, dh_ref, dw_ref, dcb_ref, da_buf, carry):
        @pl.when(pl.program_id(0) == 0)
        def _():
            dw_ref[...] = jnp.zeros_like(dw_ref)
            dcb_ref[...] = jnp.zeros_like(dcb_ref)
            carry[...] = jnp.zeros_like(carry)
        da_buf[...] = lax.dot_general(dz_ref[...], w_ref[...], NT_DIMS, preferred_element_type=f32)
        ngroups = T // GATE_ROWS
        for c0 in range(0, D_FF, GATE_COLS):
            cols = _gate_cols(c0)
            wts = [[_bcast_rows(cw_ref, k, cs) for k in range(K)] for cs in cols]

            def step(it, state, cols=cols, wts=wts):
                heads, accs = state
                rows = pl.ds(pl.multiple_of((ngroups - 1 - it) * GATE_ROWS, GATE_ROWS), GATE_ROWS)
                g = hc_ref[rows, cols[0]]
                v = hc_ref[rows, cols[1]]
                da = da_buf[rows, cols[0]]
                sg = _sig(g)
                dms = (da * v * (sg * (1.0 + g * (1.0 - sg))), da * (g * sg))
                new_heads, new_accs = [], []
                for cs, wt, dm, head, acc in zip(cols, wts, dms, heads, accs):
                    h0 = h_ref[rows, cs].astype(f32)
                    ext = jnp.concatenate([dm, head], axis=0)
                    dh = wt[K - 1] * dm
                    acc_k = [None] * K + [acc[K] + _fold_rows(dm)]
                    acc_k[K - 1] = acc[K - 1] + _fold_rows(dm * h0)
                    for ahead in range(1, K):
                        dk = pltpu.roll(ext, GATE_ROWS + H - ahead, 0)[:GATE_ROWS]
                        dh = dh + wt[K - 1 - ahead] * dk
                        acc_k[K - 1 - ahead] = acc[K - 1 - ahead] + _fold_rows(dk * h0)
                    dh_ref[rows, cs] = dh.astype(bf16)
                    new_heads.append(dm[:H])
                    new_accs.append(tuple(acc_k))
                return tuple(new_heads), tuple(new_accs)

            zero = jnp.zeros((SUBLANES, GATE_COLS), f32)
            init = (tuple(carry[:, cs] for cs in cols), tuple(tuple(zero for _ in range(K + 1)) for _ in cols))
            heads, accs = lax.fori_loop(0, ngroups, step, init, unroll=True)
            for cs, head, acc in zip(cols, heads, accs):
                carry[:, cs] = head
                dcb_ref[:, cs] += _colsum(acc[K])
                for k in range(K):
                    dw_ref[k:k + 1, cs] += _colsum(acc[k])

    tile = lambda c: pl.BlockSpec((T, c), lambda i: (nt - 1 - i, 0))
    return pl.pallas_call(
        body, grid=(nt,), name="ffn_down_gate_bwd",
        out_shape=(jax.ShapeDtypeStruct((S, 2 * D_FF), bf16), jax.ShapeDtypeStruct((8, 2 * D_FF), f32),
                   jax.ShapeDtypeStruct((1, 2 * D_FF), f32)),
        in_specs=[tile(D_MODEL), _resident((D_FF, D_MODEL)), tile(2 * D_FF), tile(2 * D_FF), _acc((8, 2 * D_FF))],
        out_specs=(tile(2 * D_FF), _acc((8, 2 * D_FF)), _acc((1, 2 * D_FF))),
        scratch_shapes=[pltpu.VMEM((T, D_FF), f32), pltpu.VMEM((H, 2 * D_FF), f32)],
        compiler_params=_cp(("arbitrary",), 48),
    )(dzb, w_down, hfb, hc, conv_w)


def _wgrad(a, g, tn, name, rows=1024):
    S, K = a.shape
    N = g.shape[1]
    T = rows if S % rows == 0 else S

    def body(a_ref, g_ref, dw_ref, db_ref):
        @pl.when(pl.program_id(1) == 0)
        def _():
            dw_ref[...] = jnp.zeros_like(dw_ref)
            db_ref[...] = jnp.zeros_like(db_ref)
        gt = g_ref[...]
        dw_ref[...] += lax.dot_general(a_ref[...].astype(bf16), gt, TN_DIMS, preferred_element_type=f32)
        db_ref[...] += _colsum(gt.astype(f32))

    return pl.pallas_call(
        body, grid=(N // tn, S // T), name=name,
        out_shape=(jax.ShapeDtypeStruct((K, N), f32), jax.ShapeDtypeStruct((1, N), f32)),
        in_specs=[pl.BlockSpec((T, K), lambda j, i: (i, 0)), pl.BlockSpec((T, tn), lambda j, i: (i, j))],
        out_specs=(pl.BlockSpec((K, tn), lambda j, i: (0, j)), pl.BlockSpec((1, tn), lambda j, i: (0, j))),
        compiler_params=_cp(("parallel", "arbitrary"), 56),
    )(a, g)


def _outproj_dgrad(dzb, w, attn_out, lse):
    S = dzb.shape[0]
    T = PERM_TILE
    nsl = ATTN_CH // LANES
    n_p = len(DILATIONS)

    def body(g_ref, w_ref, ao_ref, lse_ref, dco_ref, dgo_ref, *rest):
        do_refs = rest[:n_p]
        st_refs = rest[n_p:2 * n_p]
        slabs = rest[2 * n_p:]
        dcat = lax.dot_general(g_ref[...], w_ref[...], NT_DIMS, preferred_element_type=f32)
        dco_ref[...] = dcat[:, :CONV_CH]
        dgo_ref[...] = dcat[:, CONV_CH + ATTN_CH:]
        lane = lax.broadcasted_iota(jnp.int32, (T, LANES), 1)
        st = lse_ref[...]
        for j in range(nsl):
            dO = dcat[:, CONV_CH + LANES * j:CONV_CH + LANES * (j + 1)]
            prod = dO * ao_ref[:, LANES * j:LANES * (j + 1)].astype(f32)
            for a in range(2):
                in_head = (lane < HEAD_DIM) if a == 0 else (lane >= HEAD_DIM)
                delta = jnp.sum(jnp.where(in_head, prod, 0.0), axis=1, keepdims=True)
                st = jnp.where((lane // 16 == 2 * j + a) & (lane % 16 >= 8), delta, st)
            slabs[j][...] = dO
        slabs[nsl][...] = st
        for d, do_ref, st_ref in zip(DILATIONS, do_refs, st_refs):
            for r in range(d):
                for j in range(nsl):
                    do_ref[r, :, LANES * j:LANES * (j + 1)] = _rows_of(slabs[j], r, T // d, d).astype(bf16)
                st_ref[r] = _rows_of(slabs[nsl], r, T // d, d)

    row = lambda c: pl.BlockSpec((T, c), lambda i: (i, 0))
    return pl.pallas_call(
        body, grid=(S // T,), name="outproj_dgrad",
        out_shape=(jax.ShapeDtypeStruct((S, CONV_CH), f32), jax.ShapeDtypeStruct((S, GMLP_CH), f32))
        + tuple(_perm_shape(S, d, ATTN_CH, bf16) for d in DILATIONS)
        + tuple(_perm_shape(S, d, ATTN_BLOCK, f32) for d in DILATIONS),
        in_specs=[row(D_MODEL), _resident((D_MODEL, D_MODEL)), row(ATTN_CH), row(ATTN_BLOCK)],
        out_specs=(row(CONV_CH), row(GMLP_CH)) + tuple(_perm_spec(d, ATTN_CH) for d in DILATIONS)
        + tuple(_perm_spec(d, ATTN_BLOCK) for d in DILATIONS),
        scratch_shapes=_slabs(nsl + 1, T),
        compiler_params=_cp(("parallel",), 40),
    )(dzb, w, attn_out, lse)


def _gmlp_bwd(c_in, dgm, ln_g, ln_b, w_s, b_s_t):
    S = c_in.shape[0]
    T = 512
    nsteps = S // T

    def body(c_ref, dg_ref, g_ref, be_ref, w_ref, bs_ref, dc_ref, dlg_ref, dlb_ref, dw_ref, dbs_ref,
             du_buf, dv_buf, dm_acc):
        i = pl.program_id(0)

        @pl.when(i == 0)
        def _():
            dlg_ref[...] = jnp.zeros_like(dlg_ref)
            dlb_ref[...] = jnp.zeros_like(dlb_ref)
            dw_ref[...] = jnp.zeros_like(dw_ref)
            dm_acc[...] = jnp.zeros_like(dm_acc)
        c = c_ref[...]
        u = c[:, :GMLP_CH]
        xhat, rstd = _ln_stats(c[:, GMLP_CH:])
        vb = (xhat * g_ref[...] + be_ref[...]).astype(bf16)
        dgm_t = dg_ref[...]
        dm_all = dgm_t * u
        for g in range(GMLP_GROUPS):
            wt = _tril_bf16(w_ref[g])
            cs = slice(GMLP_GROUP_DIM * g, GMLP_GROUP_DIM * (g + 1))
            dw_g = jnp.zeros((CHUNK, CHUNK), f32)
            for ci in range(T // CHUNK):
                rs = slice(CHUNK * ci, CHUNK * (ci + 1))
                v_c = vb[rs, cs]
                mixed = jnp.dot(wt, v_c, preferred_element_type=f32) + bs_ref[:, g:g + 1]
                dm = dm_all[rs, cs]
                dmb = dm.astype(bf16)
                du_buf[rs, cs] = dgm_t[rs, cs] * mixed
                dv_buf[rs, cs] = lax.dot_general(wt, dmb, TN_DIMS, preferred_element_type=f32)
                dw_g = dw_g + lax.dot_general(dmb, v_c, NT_DIMS, preferred_element_type=f32)
                dm_acc[:, cs] += dm
            dw_ref[g] += dw_g
        dv = dv_buf[...]
        dvr = _ln_bwd(dv, xhat, rstd, g_ref[...])
        dlg_ref[...] += _colsum(dv * xhat)
        dlb_ref[...] += _colsum(dv)
        dc_ref[:, :GMLP_CH] = du_buf[...].astype(bf16)
        dc_ref[:, GMLP_CH:] = dvr.astype(bf16)

        @pl.when(i == nsteps - 1)
        def _():
            row = lax.broadcasted_iota(jnp.int32, (CHUNK, CHUNK), 0)
            col = lax.broadcasted_iota(jnp.int32, (CHUNK, CHUNK), 1)
            tile = jnp.zeros((CHUNK, CHUNK), f32)
            for g in range(GMLP_GROUPS):
                dw_ref[g] = jnp.where(col <= row, dw_ref[g], 0.0)
                gsum = jnp.sum(dm_acc[:, GMLP_GROUP_DIM * g:GMLP_GROUP_DIM * (g + 1)], axis=1, keepdims=True)
                tile = jnp.where(col == g, gsum, tile)
            dbs_ref[...] = tile

    vec = jax.ShapeDtypeStruct((1, GMLP_CH), f32)
    return pl.pallas_call(
        body, grid=(nsteps,), name="gmlp_bwd",
        out_shape=(jax.ShapeDtypeStruct((S, 2 * GMLP_CH), bf16), vec, vec,
                   jax.ShapeDtypeStruct((GMLP_GROUPS, CHUNK, CHUNK), f32), jax.ShapeDtypeStruct((CHUNK, CHUNK), f32)),
        in_specs=[pl.BlockSpec((T, 2 * GMLP_CH), lambda i: (i, 0)), pl.BlockSpec((T, GMLP_CH), lambda i: (i, 0)),
                  _acc((1, GMLP_CH)), _acc((1, GMLP_CH)), _acc((GMLP_GROUPS, CHUNK, CHUNK)), _acc((CHUNK, GMLP_GROUPS))],
        out_specs=(pl.BlockSpec((T, 2 * GMLP_CH), lambda i: (i, 0)), _acc((1, GMLP_CH)), _acc((1, GMLP_CH)),
                   _acc((GMLP_GROUPS, CHUNK, CHUNK)), _acc((CHUNK, CHUNK))),
        scratch_shapes=[pltpu.VMEM((T, GMLP_CH), f32), pltpu.VMEM((T, GMLP_CH), f32), pltpu.VMEM((CHUNK, GMLP_CH), f32)],
        compiler_params=_cp(("arbitrary",), 32),
    )(c_in, dgm, ln_g, ln_b, w_s, b_s_t)


def _attn_bwd_pattern(qkv, d_out, stats, bias, d):
    _, L, _ = qkv.shape
    B = ATTN_BLOCK
    QB = _attn_tile(L)
    nsb = QB // B
    nt = L // QB
    U = nsb * UNITS_PER_BLOCK
    KV = 2 * ATTN_CH

    def body(cur_ref, hk_ref, hv_ref, do_ref, st_ref, b_ref, dqkv_ref, dbias_ref, lg, dp, pb, dsb, dkv, carry):
        r = pl.program_id(0)
        i = pl.program_id(1)
        n = nt - 1 - i

        @pl.when((r == 0) & (i == 0))
        def _():
            dbias_ref[...] = jnp.zeros_like(dbias_ref)

        @pl.when(i == 0)
        def _():
            carry[...] = jnp.zeros_like(carry)

        def operands(b, j, a):
            rows = slice(B * b, B * (b + 1))
            q2 = cur_ref[rows, LANES * j:LANES * (j + 1)]
            do2 = do_ref[rows, LANES * j:LANES * (j + 1)]
            keep = _head_lanes(a)
            return jnp.where(keep, q2, jnp.zeros_like(q2)), jnp.where(keep, do2, jnp.zeros_like(do2))

        for b in range(nsb):
            valid = _band_mask(b == 0, n)
            for j in range(HEAD_PAIRS):
                k2 = _pair_keys(cur_ref, hk_ref, 1, b, j)
                v2 = _pair_keys(cur_ref, hv_ref, 2, b, j)
                for a in range(2):
                    u = (b * HEAD_PAIRS + j) * 2 + a
                    qm, dom = operands(b, j, a)
                    logits = lax.dot_general(qm, k2, NT_DIMS, preferred_element_type=f32) + b_ref[2 * j + a]
                    lg[B * u:B * (u + 1), :] = jnp.where(valid, logits, NEG)
                    dp[B * u:B * (u + 1), :] = lax.dot_general(dom, v2, NT_DIMS, preferred_element_type=f32)
        for b in range(nsb):
            for j in range(HEAD_PAIRS):
                for a in range(2):
                    u = (b * HEAD_PAIRS + j) * 2 + a
                    rows = slice(B * u, B * (u + 1))
                    lane0 = 32 * j + 16 * a
                    lse = st_ref[B * b:B * (b + 1), lane0:lane0 + 1]
                    delta = st_ref[B * b:B * (b + 1), lane0 + 8:lane0 + 9]
                    p = jnp.exp(lg[rows, :] - lse)
                    ds = p * (dp[rows, :] - delta)
                    pb[rows, :] = p.astype(bf16)
                    dsb[rows, :] = ds.astype(bf16)
                    dbias_ref[2 * j + a] += ds
        dkv[...] = jnp.zeros_like(dkv)
        for b in range(nsb):
            for j in range(HEAD_PAIRS):
                k2 = _pair_keys(cur_ref, hk_ref, 1, b, j)
                dq, dk2, dv2 = [], None, None
                for a in range(2):
                    u = (b * HEAD_PAIRS + j) * 2 + a
                    rows = slice(B * u, B * (u + 1))
                    qm, dom = operands(b, j, a)
                    ds_u = dsb[rows, :]
                    dq.append(jnp.dot(ds_u, k2, preferred_element_type=f32))
                    dk_u = lax.dot_general(ds_u, qm, TN_DIMS, preferred_element_type=f32)
                    dv_u = lax.dot_general(pb[rows, :], dom, TN_DIMS, preferred_element_type=f32)
                    dk2 = dk_u if dk2 is None else dk2 + dk_u
                    dv2 = dv_u if dv2 is None else dv2 + dv_u
                dq2 = jnp.where(_head_lanes(0), dq[0], dq[1]) * (HEAD_DIM ** -0.5)
                dqkv_ref[B * b:B * (b + 1), LANES * j:LANES * (j + 1)] = dq2.astype(bf16)
                dkv[B * b:B * (b + 2), LANES * j:LANES * (j + 1)] += dk2
                dkv[B * b:B * (b + 2), ATTN_CH + LANES * j:ATTN_CH + LANES * (j + 1)] += dv2
        dkv[QB:, :] += carry[...]
        dqkv_ref[:, ATTN_CH:] = dkv[B:, :].astype(bf16)
        carry[...] = dkv[0:B, :]

    halo = lambda part: pl.BlockSpec((None, B, ATTN_CH),
                                     lambda r, i: (r, jnp.maximum((nt - 1 - i) * nsb - 1, 0), part))
    tile_spec = lambda c: pl.BlockSpec((None, QB, c), lambda r, i: (r, nt - 1 - i, 0))
    return pl.pallas_call(
        body, grid=(d, nt), name=f"attn_bwd_d{d}",
        out_shape=(jax.ShapeDtypeStruct((d, L, QKV_CH), bf16), jax.ShapeDtypeStruct((ATTN_HEADS, B, 2 * B), f32)),
        in_specs=[tile_spec(QKV_CH), halo(1), halo(2), tile_spec(ATTN_CH), tile_spec(B),
                  _resident((ATTN_HEADS, B, 2 * B))],
        out_specs=(tile_spec(QKV_CH), _acc((ATTN_HEADS, B, 2 * B))),
        scratch_shapes=[pltpu.VMEM((U * B, 2 * B), f32), pltpu.VMEM((U * B, 2 * B), f32),
                        pltpu.VMEM((U * B, 2 * B), bf16), pltpu.VMEM((U * B, 2 * B), bf16),
                        pltpu.VMEM((B + QB, KV), f32), pltpu.VMEM((B, KV), f32)],
        compiler_params=_cp(("arbitrary", "arbitrary"), 48),
    )(qkv, qkv, qkv, d_out, stats, bias)


def _attn_bwd_merge(d_a, dqkvs, d_c):
    S = d_a.shape[0]
    T = PERM_TILE
    nsl = QKV_CH // LANES
    n_p = len(DILATIONS)

    def body(da_ref, *rest):
        g_refs = rest[:n_p]
        dc_ref, dh_ref = rest[n_p:n_p + 2]
        slabs = rest[n_p + 2:]
        q0 = 2 * CONV_CH
        dh_ref[:, :q0] = da_ref[...]
        dh_ref[:, q0 + QKV_CH:] = dc_ref[...]
        for p, (d, g_ref) in enumerate(zip(DILATIONS, g_refs)):
            for r in range(d):
                for j in range(nsl):
                    _set_rows_of(slabs[p * nsl + j], r, T // d, d, g_ref[r, :, LANES * j:LANES * (j + 1)].astype(f32))
        for j in range(nsl):
            acc = slabs[j][...]
            for p in range(1, n_p):
                acc = acc + slabs[p * nsl + j][...]
            dh_ref[:, q0 + LANES * j:q0 + LANES * (j + 1)] = acc.astype(bf16)

    row = lambda c: pl.BlockSpec((T, c), lambda i: (i, 0))
    return pl.pallas_call(
        body, grid=(S // T,), name="attn_bwd_merge", out_shape=jax.ShapeDtypeStruct((S, IN_CH), bf16),
        in_specs=[row(2 * CONV_CH)] + [_perm_spec(d, QKV_CH) for d in DILATIONS] + [row(2 * GMLP_CH)],
        out_specs=row(IN_CH), scratch_shapes=_slabs(n_p * nsl, T),
        compiler_params=_cp(("parallel",), 48),
    )(d_a, *dqkvs, d_c)


def _bias_table_grad(dbias, buckets):
    n = dbias.shape[0]

    def body(db_ref, bk_ref, o_ref):
        p = pl.program_id(0)
        h = pl.program_id(1)

        @pl.when((p == 0) & (h == 0))
        def _():
            o_ref[...] = jnp.zeros_like(o_ref)
        ids = bk_ref[0]
        db = db_ref[0, 0]
        row = lax.broadcasted_iota(jnp.int32, (N_BUCKETS, 128), 0)
        lane = lax.broadcasted_iota(jnp.int32, (N_BUCKETS, 128), 1)
        upd = jnp.zeros((N_BUCKETS, 128), f32)
        for b in range(N_BUCKETS):
            s = jnp.sum(jnp.sum(jnp.where(ids == b, db, 0.0), axis=1, keepdims=True), axis=0, keepdims=True)
            upd = jnp.where((row == b) & (lane == h), s, upd)
        o_ref[...] += upd

    return pl.pallas_call(
        body, grid=(n, ATTN_HEADS), name="bias_table_grad",
        out_shape=jax.ShapeDtypeStruct((N_BUCKETS, 128), f32),
        in_specs=[pl.BlockSpec((1, 1, ATTN_BLOCK, 2 * ATTN_BLOCK), lambda p, h: (p, h, 0, 0)),
                  pl.BlockSpec((1, ATTN_BLOCK, 2 * ATTN_BLOCK), lambda p, h: (p, 0, 0))],
        out_specs=_acc((N_BUCKETS, 128)),
        compiler_params=_cp(("arbitrary", "arbitrary"), 16),
    )(dbias, buckets)


def _conv_bwd(a_in, hc, dco, dw_w, ln_g, ln_b):
    S = a_in.shape[0]
    T = 512
    hb = T // CONV_HALO
    nsteps = S // T
    R = T + CONV_HALO
    K = CONV_WIDTH

    def body(a_ref, hc_ref, hcn_ref, d_ref, dn_ref, w_ref, g_ref, be_ref,
             da_ref, dw_ref, dcb_ref, dlg_ref, dlb_ref, ext, dbuf, wacc):
        i = pl.program_id(0)

        @pl.when(i == 0)
        def _():
            wacc[...] = jnp.zeros_like(wacc)
            dcb_ref[...] = jnp.zeros_like(dcb_ref)
            dlg_ref[...] = jnp.zeros_like(dlg_ref)
            dlb_ref[...] = jnp.zeros_like(dlb_ref)
        ext[0:T, :] = hc_ref[...]
        ext[T:, :] = hcn_ref[...]
        xhat, rstd = _ln_stats(ext[...])
        hl = xhat * g_ref[...] + be_ref[...]
        ext[0:T, :] = d_ref[...]
        ext[T:, :] = dn_ref[...]
        sl_ = _sig(hl)
        dhl = ext[...] * (sl_ * (1.0 + hl * (1.0 - sl_)))
        dhc = _ln_bwd(dhl, xhat, rstd, g_ref[...])
        rowi = lax.broadcasted_iota(jnp.int32, (R, CONV_CH), 0)
        dbuf[...] = jnp.where((rowi < T) | (i < nsteps - 1), dhc, 0.0)
        dlg_ref[...] += _colsum(dhl[:T] * xhat[:T])
        dlb_ref[...] += _colsum(dhl[:T])
        dcb_ref[...] += _colsum(dbuf[pl.ds(0, T), :])
        starts = _window_rolls(range(K))
        slabs = [slice(LANES * j, LANES * (j + 1)) for j in range(CONV_CH // LANES)]

        def step(g, _):
            r0 = pl.multiple_of(g * CONV_GROUP, CONV_GROUP)
            rows = pl.ds(r0, CONV_GROUP)
            for j, cs in enumerate(slabs):
                gate_cs = slice(CONV_CH + LANES * j, CONV_CH + LANES * (j + 1))
                win = dbuf[pl.ds(r0, CONV_GROUP + CONV_HALO), cs]
                a = a_ref[rows, cs]
                sg = _sig(a_ref[rows, gate_cs])
                hg = a * sg
                dhg = jnp.zeros((CONV_GROUP, LANES), f32)
                for b, ss in starts.items():
                    rolled = win if b == 0 else pltpu.roll(win, b, 0)
                    for s in ss:
                        k = K - 1 - s
                        dk = rolled[s + b:s + b + CONV_GROUP]
                        dhg = dhg + w_ref[k:k + 1, cs] * dk
                        prod = dk * hg
                        fold = prod[0:SUBLANES]
                        for r in range(SUBLANES, CONV_GROUP, SUBLANES):
                            fold = fold + prod[r:r + SUBLANES]
                        wacc[SUBLANES * k:SUBLANES * (k + 1), cs] += fold
                da_ref[rows, cs] = (dhg * sg).astype(bf16)
                da_ref[rows, gate_cs] = (dhg * hg * (1.0 - sg)).astype(bf16)
            return 0

        lax.fori_loop(0, T // CONV_GROUP, step, 0)

        @pl.when(i == nsteps - 1)
        def _():
            for k in range(K):
                dw_ref[k:k + 1, :] = _colsum(wacc[SUBLANES * k:SUBLANES * (k + 1), :])
            dw_ref[K:, :] = jnp.zeros((32 - K, CONV_CH), f32)

    vec = jax.ShapeDtypeStruct((1, CONV_CH), f32)
    nxt = lambda i: (jnp.minimum((i + 1) * hb, nsteps * hb - 1), 0)
    return pl.pallas_call(
        body, grid=(nsteps,), name="conv_bwd",
        out_shape=(jax.ShapeDtypeStruct((S, 2 * CONV_CH), bf16), jax.ShapeDtypeStruct((32, CONV_CH), f32), vec, vec, vec),
        in_specs=[pl.BlockSpec((T, 2 * CONV_CH), lambda i: (i, 0)),
                  pl.BlockSpec((T, CONV_CH), lambda i: (i, 0)), pl.BlockSpec((CONV_HALO, CONV_CH), nxt),
                  pl.BlockSpec((T, CONV_CH), lambda i: (i, 0)), pl.BlockSpec((CONV_HALO, CONV_CH), nxt),
                  _acc((32, CONV_CH)), _acc((1, CONV_CH)), _acc((1, CONV_CH))],
        out_specs=(pl.BlockSpec((T, 2 * CONV_CH), lambda i: (i, 0)), _acc((32, CONV_CH)), _acc((1, CONV_CH)),
                   _acc((1, CONV_CH)), _acc((1, CONV_CH))),
        scratch_shapes=[pltpu.VMEM((R, CONV_CH), f32), pltpu.VMEM((R, CONV_CH), f32),
                        pltpu.VMEM((SUBLANES * 32, CONV_CH), f32)],
        compiler_params=_cp(("arbitrary",), 32),
    )(a_in, hc, hc, dco, dco, dw_w, ln_g, ln_b)


def _adamw(g, w, m, v, name):
    R, C = g.shape
    T = R
    for cand in (512, 256, 128, 64, 32, 16, 8):
        if R % cand == 0 and cand * C * 4 <= MIB:
            T = cand
            break
    c1 = 1.0 / (1.0 - ADAM_B1 ** ADAM_STEP)
    c2 = 1.0 / (1.0 - ADAM_B2 ** ADAM_STEP)

    def body(g_ref, w_ref, m_ref, v_ref, d_ref, nm_ref, nv_ref):
        gg = g_ref[...]
        nm = ADAM_B1 * m_ref[...] + (1.0 - ADAM_B1) * gg
        nv = ADAM_B2 * v_ref[...] + (1.0 - ADAM_B2) * (gg * gg)
        nm_ref[...] = nm
        nv_ref[...] = nv
        d_ref[...] = -ADAM_LR * ((nm * c1) / (jnp.sqrt(nv * c2) + ADAM_EPS) + ADAM_WD * w_ref[...])

    blk = pl.BlockSpec((T, C), lambda i: (i, 0))
    sd = jax.ShapeDtypeStruct((R, C), f32)
    return pl.pallas_call(
        body, grid=(R // T,), name=name, out_shape=(sd, sd, sd), in_specs=[blk] * 4, out_specs=(blk, blk, blk),
        compiler_params=_cp(("parallel",), 48),
    )(g, w, m, v)


def _pad_rows(a, rows):
    return jnp.pad(a, ((0, rows - a.shape[0]), (0, 0)))


def _local_step(x, target, wb, late_weights, sp, sink):
    buckets = jnp.asarray(_bucket_ids())
    bias = _bias_build(sp["rel_bias_table"], buckets)
    wb = dict(wb)
    saved = []
    xl = x
    for l in range(DEPTH):
        vec = lambda name: sp[name][l][None, :]
        a_in, c_in, *qkv = _inproj_fwd(xl, wb["w_in"][l], vec("b_in"))
        conv_w = _pad_rows(sp["conv_dw_w"][l], 32)
        conv_out, hc = _conv_fwd(a_in, conv_w, vec("conv_dw_b"), vec("conv_ln_g"), vec("conv_ln_b"))
        attn_out, lse = _attn_fwd(qkv, bias)
        bs_t = sp["gmlp_b_s"][l].T
        gm_out = _gmlp_fwd(c_in, vec("gmlp_ln_g"), vec("gmlp_ln_b"), sp["gmlp_w_s"][l], bs_t)
        if l == 0:
            wb.update(late_weights(gm_out))
        cat, z1, x1b = _outproj_ln_fwd(conv_out, attn_out, gm_out, wb["w_out"][l], vec("b_out"), xl,
                                           vec("ln1_g"), vec("ln1_b"))
        fconv_w = _pad_rows(sp["ffn_conv_w"][l], 8)
        hfb, fhc, act = _ffn_up_gate_fwd(x1b, wb["ffn_w_up"][l], vec("ffn_b_up"), fconv_w, vec("ffn_conv_b"))
        down = (act, wb["ffn_w_down"][l], vec("ffn_b_down"), z1, vec("ln1_g"), vec("ln1_b"), vec("ln2_g"),
                vec("ln2_b"))
        z2, x2 = _ffn_down_ln_fwd(*down) if l < DEPTH - 1 else (None, None)
        saved.append(dict(x=xl, a_in=a_in, qkv=qkv, c_in=c_in, hc=hc, attn_out=attn_out, lse=lse, cat=cat, z1=z1,
                          x1b=x1b, hfb=hfb, fhc=fhc, act=act, z2=z2, conv_w=conv_w, fconv_w=fconv_w, bs_t=bs_t))
        xl = x2

    grads = {}
    per_layer = {k: [None] * DEPTH for k in (
        "b_in", "conv_dw_w", "conv_dw_b", "conv_ln_g", "conv_ln_b", "gmlp_ln_g", "gmlp_ln_b", "gmlp_w_s",
        "gmlp_b_s", "b_out", "ln1_g", "ln1_b", "ffn_b_up", "ffn_conv_w", "ffn_conv_b", "ffn_b_down", "ln2_g", "ln2_b")}
    dbias_all = []
    dz2, dz2b, loss_part, dg2, db2 = _ffn_down_ln_loss(*down, target)
    loss = jnp.sum(loss_part)
    grad_x = None
    tok = jnp.zeros((), f32)
    for l in reversed(range(DEPTH)):
        sv = saved[l]
        vec = lambda name: sp[name][l][None, :] + tok
        per_layer["ln2_g"][l] = dg2[0]
        per_layer["ln2_b"][l] = db2[0]
        dw_down, db_down = _wgrad(sv["act"], dz2b, 512, "ffn_down_wgrad")
        tok = sink.put("ffn_w_down", l, dw_down, tok)
        per_layer["ffn_b_down"][l] = db_down[0]
        dhf, dfcw, dfcb = _ffn_down_gate_bwd(dz2b, wb["ffn_w_down"][l], sv["hfb"], sv["fhc"], sv["fconv_w"])
        per_layer["ffn_conv_w"][l] = dfcw[:FFN_CONV_WIDTH]
        per_layer["ffn_conv_b"][l] = dfcb[0]
        dw_up, db_up = _wgrad(sv["x1b"], dhf, 1408, "ffn_up_wgrad")
        tok = sink.put("ffn_w_up", l, dw_up, tok)
        per_layer["ffn_b_up"][l] = db_up[0]
        dz1, dz1b, dg1, db1 = _dgrad_ln_bwd(dhf, wb["ffn_w_up"][l], dz2, sv["z1"], vec("ln1_g"), "ffn_up_dgrad_ln")
        per_layer["ln1_g"][l] = dg1[0]
        per_layer["ln1_b"][l] = db1[0]
        dw_out, db_out = _wgrad(sv["cat"], dz1b, D_MODEL, "outproj_wgrad")
        tok = sink.put("w_out", l, dw_out, tok)
        per_layer["b_out"][l] = db_out[0]
        dco, dgo, *perm = _outproj_dgrad(dz1b, wb["w_out"][l], sv["attn_out"], sv["lse"])
        d_outs, stats = perm[:len(DILATIONS)], perm[len(DILATIONS):]
        d_c, dglg, dglb, dws, dbs = _gmlp_bwd(sv["c_in"], dgo, vec("gmlp_ln_g"), vec("gmlp_ln_b"), sp["gmlp_w_s"][l],
                                              sv["bs_t"])
        per_layer["gmlp_ln_g"][l] = dglg[0]
        per_layer["gmlp_ln_b"][l] = dglb[0]
        per_layer["gmlp_w_s"][l] = dws
        per_layer["gmlp_b_s"][l] = dbs[:, :GMLP_GROUPS].T
        dqkvs = []
        for p, d in enumerate(DILATIONS):
            dqkv, dbias = _attn_bwd_pattern(sv["qkv"][p], d_outs[p], stats[p], bias[p], d)
            dqkvs.append(dqkv)
            dbias_all.append(dbias)
        d_a, dcw, dcb, dclg, dclb = _conv_bwd(sv["a_in"], sv["hc"], dco, sv["conv_w"], vec("conv_ln_g"),
                                              vec("conv_ln_b"))
        per_layer["conv_dw_w"][l] = dcw[:CONV_WIDTH]
        per_layer["conv_dw_b"][l] = dcb[0]
        per_layer["conv_ln_g"][l] = dclg[0]
        per_layer["conv_ln_b"][l] = dclb[0]
        dh = _attn_bwd_merge(d_a, dqkvs, d_c)
        dw_in, db_in = _wgrad(sv["x"], dh, IN_CH, "inproj_wgrad")
        tok = sink.put("w_in", l, dw_in, tok)
        per_layer["b_in"][l] = db_in[0]
        if l > 0:
            pv = saved[l - 1]
            dz2, dz2b, dg2, db2 = _dgrad_ln_bwd(dh, wb["w_in"][l], dz1, pv["z2"], sp["ln2_g"][l - 1][None, :] + tok,
                                                "inproj_dgrad_ln")
        else:
            grad_x = _dgrad_ln_bwd(dh, wb["w_in"][l], dz1, None, None, "inproj_dgrad")
    for k, v in per_layer.items():
        grads[k] = jnp.stack(v)
    dbias_cat = jnp.stack(dbias_all)
    bk_cat = jnp.concatenate([buckets] * DEPTH, axis=0)
    grads["rel_bias_table"] = _bias_table_grad(dbias_cat, bk_cat)[:, :ATTN_HEADS]
    return loss, grad_x, grads, sink.finish(grad_x)


N_CHIPS = 4
BIG = {"w_in": (D_MODEL, IN_CH, 1), "w_out": (D_MODEL, D_MODEL, 0),
       "ffn_w_up": (D_MODEL, 2 * D_FF, 1), "ffn_w_down": (D_FF, D_MODEL, 0)}
SMALL = ("b_in", "conv_dw_w", "conv_dw_b", "conv_ln_g", "conv_ln_b", "rel_bias_table", "gmlp_ln_g", "gmlp_ln_b",
         "gmlp_w_s", "gmlp_b_s", "b_out", "ln1_g", "ln1_b", "ffn_b_up", "ffn_conv_w", "ffn_conv_b", "ffn_b_down",
         "ln2_g", "ln2_b")
SMALL_SHARDED = ("conv_dw_w", "ffn_conv_w")
WEIGHTS = ("w_in", "b_in", "conv_dw_w", "conv_dw_b", "conv_ln_g", "conv_ln_b", "rel_bias_table", "gmlp_ln_g",
           "gmlp_ln_b", "gmlp_w_s", "gmlp_b_s", "w_out", "b_out", "ln1_g", "ln1_b", "ffn_w_up", "ffn_b_up",
           "ffn_conv_w", "ffn_conv_b", "ffn_w_down", "ffn_b_down", "ln2_g", "ln2_b")
ANY = pl.BlockSpec(memory_space=pl.ANY)


def _position():
    return lax.axis_index("x"), lax.axis_index("y"), lax.axis_index("c")


def _other_chips(x, y):
    return [(1 - x, y), (x, 1 - y), (1 - x, 1 - y)]


def _cast_bf16(a):
    R, C = a.shape
    T = 128

    def body(a_ref, o_ref):
        o_ref[...] = a_ref[...].astype(bf16)

    return pl.pallas_call(
        body, grid=(R // T,), name="cast_bf16", out_shape=jax.ShapeDtypeStruct((R, C), bf16),
        in_specs=[pl.BlockSpec((T, C), lambda i: (i, 0))], out_specs=pl.BlockSpec((T, C), lambda i: (i, 0)),
        compiler_params=_cp(("parallel",), 16),
    )(a)


def _chip_slot(ref, name, l, p):
    K, N, ax = BIG[name]
    if ax == 1:
        sz = N // N_CHIPS
        return ref.at[l, :, pl.ds(pl.multiple_of(p * sz, 128), sz)]
    sz = K // N_CHIPS
    return ref.at[l, pl.ds(pl.multiple_of(p * sz, 16), sz), :]


def _gather_weights(shards, conv_w, fconv_w):
    names = list(shards)
    n_big = len(names)
    n_t = n_big + 2
    n_chip = 3 * n_t
    n_pass = 3 * n_big

    def body(*refs):
        ins = refs[:n_t]
        outs = refs[n_t:2 * n_t]
        send_sems, recv_sems, pass_send, pass_recv, local_sems = refs[2 * n_t:]
        x, y, c = _position()
        me = 2 * x + y
        chips = _other_chips(x, y)

        def src(t):
            return ins[t].at[c] if t < n_big else ins[t]

        def slot(t, l, p):
            return _chip_slot(outs[t], names[t], l, p) if t < n_big else outs[t].at[p]

        locs, cps = [], []
        for t in range(n_t):
            for l in (range(DEPTH) if t < n_big else (0,)):
                loc = pltpu.make_async_copy(ins[t].at[l] if t < n_big else ins[t], slot(t, l, me),
                                            local_sems.at[DEPTH * t + l])
                loc.start()
                locs.append(loc)
            for k, (px, py) in enumerate(chips):
                cp = pltpu.make_async_remote_copy(
                    src_ref=src(t), dst_ref=slot(t, c, me), send_sem=send_sems.at[3 * t + k],
                    recv_sem=recv_sems.at[3 * t + k], device_id=(px, py, c), device_id_type=MESH_ID)
                cp.start()
                cps.append(cp)
        for t in range(n_t):
            for k, (px, py) in enumerate(chips):
                landed = slot(t, c, 2 * px + py)
                pltpu.make_async_remote_copy(
                    src_ref=src(t), dst_ref=landed, send_sem=send_sems.at[3 * t + k],
                    recv_sem=recv_sems.at[3 * t + k], device_id=(px, py, c), device_id_type=MESH_ID).wait_recv()
                if t < n_big:
                    cp = pltpu.make_async_remote_copy(
                        src_ref=landed, dst_ref=landed, send_sem=pass_send.at[3 * t + k],
                        recv_sem=pass_recv.at[3 * t + k], device_id=(x, y, 1 - c), device_id_type=MESH_ID)
                    cp.start()
                    cps.append(cp)
        for t in range(n_big):
            for k, (px, py) in enumerate(chips):
                from_sibling = slot(t, 1 - c, 2 * px + py)
                pltpu.make_async_remote_copy(
                    src_ref=from_sibling, dst_ref=from_sibling, send_sem=pass_send.at[3 * t + k],
                    recv_sem=pass_recv.at[3 * t + k], device_id=(x, y, 1 - c), device_id_type=MESH_ID).wait_recv()
        for cp in cps:
            cp.wait_send()
        for loc in locs:
            loc.wait()

    ins = [shards[n] for n in names] + [conv_w, fconv_w]
    out_shape = [jax.ShapeDtypeStruct((DEPTH, BIG[n][0], BIG[n][1]), bf16) for n in names]
    out_shape += [jax.ShapeDtypeStruct((N_CHIPS,) + conv_w.shape, f32), jax.ShapeDtypeStruct((N_CHIPS,) + fconv_w.shape, f32)]
    outs = pl.pallas_call(
        body, name="gather_weights", out_shape=tuple(out_shape), in_specs=[ANY] * n_t, out_specs=tuple([ANY] * n_t),
        scratch_shapes=[pltpu.SemaphoreType.DMA((n_chip,)), pltpu.SemaphoreType.DMA((n_chip,)),
                        pltpu.SemaphoreType.DMA((n_pass,)), pltpu.SemaphoreType.DMA((n_pass,)),
                        pltpu.SemaphoreType.DMA((DEPTH * n_t,))],
    )(*ins)
    return dict(zip(names, outs[:n_big])), outs[-2], outs[-1]


LATE_WEIGHTS = ("w_out", "ffn_w_up", "ffn_w_down")
HBM = pl.BlockSpec(memory_space=pltpu.HBM)
SEM = pl.BlockSpec(memory_space=pltpu.SEMAPHORE)


def _cast_into_full(shard, name, chip_arr):
    K, N, ax = BIG[name]
    k, n = _shard_shape(name)
    T = 64
    nrt = k // T

    def body(p_ref, a_ref, o_ref):
        o_ref[...] = a_ref[...].astype(bf16)

    if ax == 1:
        out_spec = pl.BlockSpec((None, T, n), lambda l, i, p: (l, i, p[0]))
    else:
        out_spec = pl.BlockSpec((None, T, n), lambda l, i, p: (l, p[0] * nrt + i, 0))
    return pl.pallas_call(
        body, name="cast_into_full", out_shape=jax.ShapeDtypeStruct((DEPTH, K, N), bf16),
        grid_spec=pltpu.PrefetchScalarGridSpec(
            num_scalar_prefetch=1, grid=(DEPTH, nrt),
            in_specs=[pl.BlockSpec((None, T, n), lambda l, i, p: (l, i, 0))], out_specs=out_spec),
        compiler_params=_cp(("parallel", "parallel"), 16),
    )(chip_arr, shard)


def _late_copies(refs, send_sems, recv_sems):
    x, y, c = _position()
    me = 2 * x + y
    idx = 0
    for ref, name in zip(refs, LATE_WEIGHTS):
        for l in range(DEPTH):
            for px, py in _other_chips(x, y):
                def copy(p, ref=ref, name=name, l=l, px=px, py=py, idx=idx):
                    part = _chip_slot(ref, name, l, p)
                    return pltpu.make_async_remote_copy(
                        src_ref=part, dst_ref=part, send_sem=send_sems.at[idx], recv_sem=recv_sems.at[idx],
                        device_id=(px, py, c), device_id_type=MESH_ID)
                yield copy(me), copy(2 * px + py)
                idx += 1


N_LATE_COPIES = 3 * DEPTH * len(LATE_WEIGHTS)


def _gather_start(fulls, after):
    n = len(fulls)

    def body(*refs):
        ins = refs[:n]
        send_sems, recv_sems = refs[n + 1:n + 3]
        token = refs[-1]
        for sent, _ in _late_copies(ins, send_sems, recv_sems):
            sent.start()
        token[...] = jnp.zeros_like(token)

    outs = pl.pallas_call(
        body, name="gather_start",
        out_shape=(pltpu.SemaphoreType.DMA((N_LATE_COPIES,)), pltpu.SemaphoreType.DMA((N_LATE_COPIES,)))
        + tuple(pltpu.HBM(f.shape, f.dtype) for f in fulls) + (jax.ShapeDtypeStruct((SUBLANES, LANES), f32),),
        in_specs=(HBM,) * n + (ANY,),
        out_specs=(SEM, SEM) + (HBM,) * n + (pl.BlockSpec(memory_space=pltpu.VMEM),),
        input_output_aliases={t: 2 + t for t in range(n)},
        compiler_params=pltpu.CompilerParams(has_side_effects=pltpu.SideEffectType.DATAFLOW_SIDE_EFFECTING),
    )(*[pltpu.with_memory_space_constraint(f, pltpu.HBM) for f in fulls], after)
    return outs[0], outs[1], outs[2:2 + n], outs[-1]


def _gather_wait(send_sems, recv_sems, fulls, after):
    n = len(fulls)

    def body(*refs):
        ins = refs[:n]
        send_ref, recv_ref = refs[n:n + 2]
        for sent, landed in _late_copies(ins, send_ref, recv_ref):
            sent.wait_send()
            landed.wait_recv()

    return pl.pallas_call(
        body, name="gather_wait", out_shape=tuple(pltpu.HBM(f.shape, f.dtype) for f in fulls),
        in_specs=(HBM,) * n + (SEM, SEM, ANY), out_specs=(HBM,) * n,
        input_output_aliases={t: t for t in range(n)},
        compiler_params=pltpu.CompilerParams(has_side_effects=pltpu.SideEffectType.DATAFLOW_SIDE_EFFECTING),
    )(*fulls, send_sems, recv_sems, after)


def _half(ref, name, c):
    K, N, ax = BIG[name]
    if ax == 1:
        return ref.at[pl.ds(pl.multiple_of(c * (K // 2), 8), K // 2), :]
    return ref.at[:, pl.ds(pl.multiple_of(c * (N // 2), 128), N // 2)]


def _half_shape(name):
    K, N, ax = BIG[name]
    return (K // 2, N) if ax == 1 else (K, N // 2)


def _shard_of_half(ref, name, q):
    K, N, ax = BIG[name]
    if ax == 1:
        sz = N // N_CHIPS
        return ref.at[:, pl.ds(pl.multiple_of(q * sz, 128), sz)]
    sz = K // N_CHIPS
    return ref.at[pl.ds(pl.multiple_of(q * sz, 16), sz), :]


def _shard_half_shape(name):
    K, N, ax = BIG[name]
    return (K // 2, N // N_CHIPS) if ax == 1 else (K // N_CHIPS, N // 2)


def _shard_shape(name):
    K, N, ax = BIG[name]
    return (K, N // N_CHIPS) if ax == 1 else (K // N_CHIPS, N)


def _pair_copies(names, srcs, lands, send_sems, recv_sems):
    x, y, c = _position()
    for idx, (name, src, land) in enumerate(zip(names, srcs, lands)):
        yield pltpu.make_async_remote_copy(
            src_ref=_half(src, name, 1 - c), dst_ref=land, send_sem=send_sems.at[idx], recv_sem=recv_sems.at[idx],
            device_id=(x, y, 1 - c), device_id_type=MESH_ID)


def _pair_exchange_start(tag, tensors):
    names = [n for n, _ in tensors]
    n = len(tensors)
    lands = [lax.empty(_half_shape(nm), f32) for nm in names]

    def body(*refs):
        for cp in _pair_copies(names, refs[:n], refs[n:2 * n], refs[2 * n], refs[2 * n + 1]):
            cp.start()
        refs[-1][...] = jnp.zeros_like(refs[-1])

    args = [g for _, g in tensors] + lands
    outs = pl.pallas_call(
        body, name="grad_pair_start_" + tag,
        out_shape=(pltpu.SemaphoreType.DMA((n,)), pltpu.SemaphoreType.DMA((n,)))
        + tuple(pltpu.HBM(a.shape, a.dtype) for a in args) + (jax.ShapeDtypeStruct((SUBLANES, LANES), f32),),
        in_specs=(HBM,) * (2 * n), out_specs=(SEM, SEM) + (HBM,) * (2 * n) + (pl.BlockSpec(memory_space=pltpu.VMEM),),
        input_output_aliases={t: 2 + t for t in range(2 * n)},
        compiler_params=pltpu.CompilerParams(has_side_effects=pltpu.SideEffectType.DATAFLOW_SIDE_EFFECTING),
    )(*[pltpu.with_memory_space_constraint(a, pltpu.HBM) for a in args])
    return (tag, names, outs[0], outs[1], outs[2:2 + 2 * n]), outs[-1]


def _pair_exchange_wait(state, after):
    tag, names, send_sems, recv_sems, bufs = state
    n = len(names)

    def body(*refs):
        for cp in _pair_copies(names, refs[:n], refs[n:2 * n], refs[2 * n], refs[2 * n + 1]):
            cp.wait_send()
            cp.wait_recv()

    outs = pl.pallas_call(
        body, name="grad_pair_wait_" + tag, out_shape=tuple(pltpu.HBM(a.shape, a.dtype) for a in bufs),
        in_specs=(HBM,) * (2 * n) + (SEM, SEM, ANY), out_specs=(HBM,) * (2 * n),
        input_output_aliases={t: t for t in range(2 * n)},
        compiler_params=pltpu.CompilerParams(has_side_effects=pltpu.SideEffectType.DATAFLOW_SIDE_EFFECTING),
    )(*bufs, send_sems, recv_sems, after)
    return list(zip(names, outs[:n], outs[n:]))


def _pair_add(g, rcv, name, c_arr):
    K, N, ax = BIG[name]
    hr, hc = _half_shape(name)
    T = 128
    nrt = hr // T

    def body(c_ref, g_ref, r_ref, o_ref):
        o_ref[...] = (g_ref[...] + r_ref[...]).astype(bf16)

    if ax == 1:
        g_spec = pl.BlockSpec((T, hc), lambda i, c: (c[0] * nrt + i, 0))
    else:
        g_spec = pl.BlockSpec((T, hc), lambda i, c: (i, c[0]))
    plain = pl.BlockSpec((T, hc), lambda i, c: (i, 0))
    return pl.pallas_call(
        body, name="grad_pair_add", out_shape=jax.ShapeDtypeStruct((hr, hc), bf16),
        grid_spec=pltpu.PrefetchScalarGridSpec(num_scalar_prefetch=1, grid=(nrt,), in_specs=[g_spec, plain],
                                               out_specs=plain),
        compiler_params=_cp(("parallel",), 32),
    )(c_arr, g, rcv)


def _chip_copies(names, srcs, lands, send_sems, recv_sems):
    x, y, c = _position()
    me = 2 * x + y
    idx = 0
    for name, src, land in zip(names, srcs, lands):
        for px, py in _other_chips(x, y):
            def copy(q, row, name=name, src=src, land=land, px=px, py=py, idx=idx):
                return pltpu.make_async_remote_copy(
                    src_ref=_shard_of_half(src, name, q), dst_ref=land.at[row], send_sem=send_sems.at[idx],
                    recv_sem=recv_sems.at[idx], device_id=(px, py, c), device_id_type=MESH_ID)
            yield copy(2 * px + py, me), copy(me, 2 * px + py)
            idx += 1


def _chip_exchange_start(tag, tensors):
    names = [n for n, _ in tensors]
    n = len(tensors)
    lands = [lax.empty((N_CHIPS,) + _shard_half_shape(nm), g.dtype) for nm, g in tensors]

    def body(*refs):
        send_sems, recv_sems = refs[2 * n:2 * n + 2]
        for sent, _ in _chip_copies(names, refs[:n], refs[n:2 * n], send_sems, recv_sems):
            sent.start()
        refs[-1][...] = jnp.zeros_like(refs[-1])

    args = [g for _, g in tensors] + lands
    outs = pl.pallas_call(
        body, name="grad_chip_start_" + tag,
        out_shape=(pltpu.SemaphoreType.DMA((3 * n,)), pltpu.SemaphoreType.DMA((3 * n,)))
        + tuple(pltpu.HBM(a.shape, a.dtype) for a in args) + (jax.ShapeDtypeStruct((SUBLANES, LANES), f32),),
        in_specs=(HBM,) * (2 * n), out_specs=(SEM, SEM) + (HBM,) * (2 * n) + (pl.BlockSpec(memory_space=pltpu.VMEM),),
        input_output_aliases={t: 2 + t for t in range(2 * n)},
        compiler_params=pltpu.CompilerParams(has_side_effects=pltpu.SideEffectType.DATAFLOW_SIDE_EFFECTING),
    )(*[pltpu.with_memory_space_constraint(a, pltpu.HBM) for a in args])
    return (tag, names, outs[0], outs[1], outs[2:2 + 2 * n]), outs[-1]


def _chip_exchange_wait(state, after):
    tag, names, send_sems, recv_sems, bufs = state
    n = len(names)

    def body(*refs):
        for sent, landed in _chip_copies(names, refs[:n], refs[n:2 * n], refs[2 * n], refs[2 * n + 1]):
            sent.wait_send()
            landed.wait_recv()

    outs = pl.pallas_call(
        body, name="grad_chip_wait_" + tag, out_shape=tuple(pltpu.HBM(a.shape, a.dtype) for a in bufs),
        in_specs=(HBM,) * (2 * n) + (SEM, SEM, ANY), out_specs=(HBM,) * (2 * n),
        input_output_aliases={t: t for t in range(2 * n)},
        compiler_params=pltpu.CompilerParams(has_side_effects=pltpu.SideEffectType.DATAFLOW_SIDE_EFFECTING),
    )(*bufs, send_sems, recv_sems, after)
    return list(zip(names, outs[:n], outs[n:]))


def _sum_chips(name, half, land, chip_arr):
    K, N, ax = BIG[name]
    R, C = _shard_half_shape(name)
    T = 64
    nrt = R // T

    def body(p_ref, own_ref, land_ref, o_ref):
        parts = [jnp.where(p_ref[0] == q, own_ref[...], land_ref[q]).astype(f32) for q in range(N_CHIPS)]
        o_ref[...] = ((parts[0] + parts[1]) + parts[2]) + parts[3]

    if ax == 1:
        own_spec = pl.BlockSpec((T, C), lambda i, p: (i, p[0]))
    else:
        own_spec = pl.BlockSpec((T, C), lambda i, p: (p[0] * nrt + i, 0))
    return pl.pallas_call(
        body, name="grad_sum_chips", out_shape=jax.ShapeDtypeStruct((R, C), f32),
        grid_spec=pltpu.PrefetchScalarGridSpec(
            num_scalar_prefetch=1, grid=(nrt,),
            in_specs=[own_spec, pl.BlockSpec((N_CHIPS, T, C), lambda i, p: (0, i, 0))],
            out_specs=pl.BlockSpec((T, C), lambda i, p: (i, 0))),
        compiler_params=_cp(("parallel",), 32),
    )(chip_arr, half, land)


def _pair_swap(halves):
    n_t = len(halves)

    def body(*refs):
        ins = refs[:n_t]
        outs = refs[n_t:2 * n_t]
        send_sems, recv_sems = refs[2 * n_t:]
        x, y, c = _position()
        cps = []
        for t in range(n_t):
            cp = pltpu.make_async_remote_copy(
                src_ref=ins[t], dst_ref=outs[t], send_sem=send_sems.at[t], recv_sem=recv_sems.at[t],
                device_id=(x, y, 1 - c), device_id_type=MESH_ID)
            cp.start()
            cps.append(cp)
        for cp in cps:
            cp.wait()

    return pl.pallas_call(
        body, name="grad_pair_swap", out_shape=tuple(jax.ShapeDtypeStruct(h.shape, h.dtype) for h in halves),
        in_specs=[ANY] * n_t, out_specs=tuple([ANY] * n_t),
        scratch_shapes=[pltpu.SemaphoreType.DMA((n_t,)), pltpu.SemaphoreType.DMA((n_t,))],
    )(*halves)


def _adamw_halves(own, other, w, m, v, name, l, c_arr, prev):
    K, N, ax = BIG[name]
    R, C = _shard_shape(name)
    hr, hc = _shard_half_shape(name)
    T = 64
    nrt = hr // T
    c1 = 1.0 / (1.0 - ADAM_B1 ** ADAM_STEP)
    c2 = 1.0 / (1.0 - ADAM_B2 ** ADAM_STEP)

    def body(c_ref, own_ref, oth_ref, w_ref, m_ref, v_ref, *rest):
        g_ref, d_ref, nm_ref, nv_ref = rest[-4:]
        gg = jnp.where(pl.program_id(0) == c_ref[0], own_ref[...], oth_ref[...])
        nm = ADAM_B1 * m_ref[...] + (1.0 - ADAM_B1) * gg
        nv = ADAM_B2 * v_ref[...] + (1.0 - ADAM_B2) * (gg * gg)
        g_ref[...] = gg
        nm_ref[...] = nm
        nv_ref[...] = nv
        d_ref[...] = -ADAM_LR * ((nm * c1) / (jnp.sqrt(nv * c2) + ADAM_EPS) + ADAM_WD * w_ref[...])

    half = pl.BlockSpec((T, hc), lambda h, i, c: (i, 0))
    if ax == 1:
        full = pl.BlockSpec((None, T, hc), lambda h, i, c: (l, h * nrt + i, 0))
    else:
        full = pl.BlockSpec((None, T, hc), lambda h, i, c: (l, i, h))
    sd = jax.ShapeDtypeStruct((DEPTH, R, C), f32)
    args = [c_arr, own, other, w, m, v]
    in_specs = [half, half, full, full, full]
    aliases = {}
    if prev is not None:
        args += list(prev)
        in_specs += [ANY] * 4
        aliases = {6 + k: k for k in range(4)}
    return pl.pallas_call(
        body, name="adamw_" + name, out_shape=(sd, sd, sd, sd),
        grid_spec=pltpu.PrefetchScalarGridSpec(num_scalar_prefetch=1, grid=(2, nrt), in_specs=in_specs,
                                               out_specs=(full, full, full, full)),
        input_output_aliases=aliases,
        compiler_params=_cp(("arbitrary", "arbitrary"), 32),
    )(*args)


class _GradExchange:
    GROUPS = (("l1", tuple((n, DEPTH - 1) for n in BIG)),
              ("l0_ffn", (("ffn_w_down", 0), ("ffn_w_up", 0))),
              ("l0_mix", (("w_out", 0), ("w_in", 0))))

    def __init__(self):
        self.c_arr = jnp.reshape(lax.axis_index("c"), (1,)).astype(jnp.int32)
        self.chip_arr = jnp.reshape(2 * lax.axis_index("x") + lax.axis_index("y"), (1,)).astype(jnp.int32)
        self.grads = {}
        self.pair_started = {}
        self.chip_started = {}

    def _advance(self, after, tok):
        for tag, _ in self.GROUPS:
            if tag not in self.pair_started or tag in self.chip_started:
                continue
            arrived = _pair_exchange_wait(self.pair_started[tag], after)
            pair = [(n, _pair_add(g, r, n, self.c_arr)) for n, g, r in arrived]
            self.chip_started[tag], token = _chip_exchange_start(tag, pair)
            tok = tok + token[0, 0]
        return tok

    def put(self, name, layer, g, tok):
        self.grads[(name, layer)] = g
        tok = self._advance(g, tok)
        for tag, keys in self.GROUPS:
            if tag in self.pair_started or not all(k in self.grads for k in keys):
                continue
            self.pair_started[tag], token = _pair_exchange_start(tag, [(n, self.grads[(n, l)]) for n, l in keys])
            tok = tok + token[0, 0]
        return tok

    def finish(self, after):
        self._advance(after, jnp.zeros((), f32))
        keys, own = [], []
        for tag, group in self.GROUPS:
            landed = _chip_exchange_wait(self.chip_started[tag], after)
            own += [_sum_chips(n, half, land, self.chip_arr) for n, half, land in landed]
            keys += list(group)
        other = _pair_swap(own)
        return dict(zip(keys, zip(own, other)))


def _small_allreduce(buf):
    R = buf.shape[0]

    def body(in_ref, out_ref, sibling, slots, send_sems, recv_sems):
        x, y, c = _position()
        me = 2 * x + y
        swap = pltpu.make_async_remote_copy(
            src_ref=in_ref, dst_ref=sibling, send_sem=send_sems.at[0], recv_sem=recv_sems.at[0],
            device_id=(x, y, 1 - c), device_id_type=MESH_ID)
        swap.start()
        swap.wait()
        slots[me] = in_ref[...] + sibling[...]
        cps = []
        for k, (px, py) in enumerate(_other_chips(x, y)):
            cp = pltpu.make_async_remote_copy(
                src_ref=slots.at[me], dst_ref=slots.at[me], send_sem=send_sems.at[1 + k], recv_sem=recv_sems.at[1 + k],
                device_id=(px, py, c), device_id_type=MESH_ID)
            cp.start()
            cps.append(cp)
        for k, (px, py) in enumerate(_other_chips(x, y)):
            pltpu.make_async_remote_copy(
                src_ref=slots.at[me], dst_ref=slots.at[2 * px + py], send_sem=send_sems.at[1 + k],
                recv_sem=recv_sems.at[1 + k], device_id=(px, py, c), device_id_type=MESH_ID).wait_recv()
        for cp in cps:
            cp.wait_send()
        out_ref[...] = ((slots[0] + slots[1]) + slots[2]) + slots[3]

    vm = pl.BlockSpec(memory_space=pltpu.VMEM)
    return pl.pallas_call(
        body, name="small_allreduce", out_shape=jax.ShapeDtypeStruct((R, 128), f32), in_specs=[vm], out_specs=vm,
        scratch_shapes=[pltpu.VMEM((R, 128), f32), pltpu.VMEM((N_CHIPS, R, 128), f32),
                        pltpu.SemaphoreType.DMA((N_CHIPS,)), pltpu.SemaphoreType.DMA((N_CHIPS,))],
        compiler_params=pltpu.CompilerParams(vmem_limit_bytes=40 * MIB),
    )(buf)


PACK_UNIT = 1024


def _pack(arrs):
    parts = []
    for a in arrs:
        flat = a.reshape(-1)
        n = -(-flat.shape[0] // PACK_UNIT) * PACK_UNIT
        parts.append(jnp.pad(flat, (0, n - flat.shape[0])))
    return jnp.concatenate(parts).reshape(-1, 128)


def _unpack(buf, shapes):
    flat = buf.reshape(-1)
    out, off = [], 0
    for shp in shapes:
        n = int(np.prod(shp))
        out.append(flat[off:off + n].reshape(shp))
        off += -(-n // PACK_UNIT) * PACK_UNIT
    return out


def kernel(x, w_in, b_in, conv_dw_w, conv_dw_b, conv_ln_g, conv_ln_b, rel_bias_table, gmlp_ln_g, gmlp_ln_b, gmlp_w_s, gmlp_b_s, w_out, b_out, ln1_g, ln1_b, ffn_w_up, ffn_b_up, ffn_conv_w, ffn_conv_b, ffn_w_down, ffn_b_down, ln2_g, ln2_b, loss_target, m_w_in, m_b_in, m_conv_dw_w, m_conv_dw_b, m_conv_ln_g, m_conv_ln_b, m_rel_bias_table, m_gmlp_ln_g, m_gmlp_ln_b, m_gmlp_w_s, m_gmlp_b_s, m_w_out, m_b_out, m_ln1_g, m_ln1_b, m_ffn_w_up, m_ffn_b_up, m_ffn_conv_w, m_ffn_conv_b, m_ffn_w_down, m_ffn_b_down, m_ln2_g, m_ln2_b, v_w_in, v_b_in, v_conv_dw_w, v_conv_dw_b, v_conv_ln_g, v_conv_ln_b, v_rel_bias_table, v_gmlp_ln_g, v_gmlp_ln_b, v_gmlp_w_s, v_gmlp_b_s, v_w_out, v_b_out, v_ln1_g, v_ln1_b, v_ffn_w_up, v_ffn_b_up, v_ffn_conv_w, v_ffn_conv_b, v_ffn_w_down, v_ffn_b_down, v_ln2_g, v_ln2_b):
    w = dict(w_in=w_in, b_in=b_in, conv_dw_w=conv_dw_w, conv_dw_b=conv_dw_b, conv_ln_g=conv_ln_g, conv_ln_b=conv_ln_b,
             rel_bias_table=rel_bias_table, gmlp_ln_g=gmlp_ln_g, gmlp_ln_b=gmlp_ln_b, gmlp_w_s=gmlp_w_s,
             gmlp_b_s=gmlp_b_s, w_out=w_out, b_out=b_out, ln1_g=ln1_g, ln1_b=ln1_b, ffn_w_up=ffn_w_up,
             ffn_b_up=ffn_b_up, ffn_conv_w=ffn_conv_w, ffn_conv_b=ffn_conv_b, ffn_w_down=ffn_w_down,
             ffn_b_down=ffn_b_down, ln2_g=ln2_g, ln2_b=ln2_b)
    m = dict(w_in=m_w_in, b_in=m_b_in, conv_dw_w=m_conv_dw_w, conv_dw_b=m_conv_dw_b, conv_ln_g=m_conv_ln_g,
             conv_ln_b=m_conv_ln_b, rel_bias_table=m_rel_bias_table, gmlp_ln_g=m_gmlp_ln_g, gmlp_ln_b=m_gmlp_ln_b,
             gmlp_w_s=m_gmlp_w_s, gmlp_b_s=m_gmlp_b_s, w_out=m_w_out, b_out=m_b_out, ln1_g=m_ln1_g, ln1_b=m_ln1_b,
             ffn_w_up=m_ffn_w_up, ffn_b_up=m_ffn_b_up, ffn_conv_w=m_ffn_conv_w, ffn_conv_b=m_ffn_conv_b,
             ffn_w_down=m_ffn_w_down, ffn_b_down=m_ffn_b_down, ln2_g=m_ln2_g, ln2_b=m_ln2_b)
    v = dict(w_in=v_w_in, b_in=v_b_in, conv_dw_w=v_conv_dw_w, conv_dw_b=v_conv_dw_b, conv_ln_g=v_conv_ln_g,
             conv_ln_b=v_conv_ln_b, rel_bias_table=v_rel_bias_table, gmlp_ln_g=v_gmlp_ln_g, gmlp_ln_b=v_gmlp_ln_b,
             gmlp_w_s=v_gmlp_w_s, gmlp_b_s=v_gmlp_b_s, w_out=v_w_out, b_out=v_b_out, ln1_g=v_ln1_g, ln1_b=v_ln1_b,
             ffn_w_up=v_ffn_w_up, ffn_b_up=v_ffn_b_up, ffn_conv_w=v_ffn_conv_w, ffn_conv_b=v_ffn_conv_b,
             ffn_w_down=v_ffn_w_down, ffn_b_down=v_ffn_b_down, ln2_g=v_ln2_g, ln2_b=v_ln2_b)

    chip_arr = jnp.reshape(2 * lax.axis_index("x") + lax.axis_index("y"), (1,)).astype(jnp.int32)
    shards = {"w_in": _cast_bf16(w_in.reshape(-1, w_in.shape[-1])).reshape(w_in.shape)}
    wb, conv_stack, fconv_stack = _gather_weights(shards, conv_dw_w, ffn_conv_w)
    send_sems, recv_sems, in_flight, token = _gather_start(
        [_cast_into_full(w[n], n, chip_arr) for n in LATE_WEIGHTS], conv_stack)
    sp = {n: w[n] for n in SMALL}
    sp["conv_dw_w"] = jnp.moveaxis(conv_stack, 0, 2).reshape(DEPTH, CONV_WIDTH, CONV_CH)
    sp["ffn_conv_w"] = jnp.moveaxis(fconv_stack, 0, 2).reshape(DEPTH, FFN_CONV_WIDTH, 2 * D_FF)
    sp["b_in"] = sp["b_in"] + token[0, 0]

    def late_weights(after):
        return dict(zip(LATE_WEIGHTS, _gather_wait(send_sems, recv_sems, in_flight, after)))

    sink = _GradExchange()
    loss_local, grad_x, grads, big = _local_step(x[0], loss_target[0], wb, late_weights, sp, sink)

    small_shapes = [(1,)] + [grads[n].shape for n in SMALL]
    summed = _unpack(_small_allreduce(_pack([loss_local.reshape(1)] + [grads[n] for n in SMALL])), small_shapes)
    loss = summed[0].reshape(())
    small = dict(zip(SMALL, summed[1:]))
    chip = 2 * lax.axis_index("x") + lax.axis_index("y")
    for n in SMALL_SHARDED:
        width = w[n].shape[-1]
        small[n] = lax.dynamic_slice_in_dim(small[n], chip * width, width, axis=2)

    g_out, d_out, m_out, v_out = {}, {}, {}, {}
    for n in BIG:
        outs = None
        for l in range(DEPTH):
            own, other = big[(n, l)]
            outs = _adamw_halves(own, other, w[n], m[n], v[n], n, l, sink.c_arr, outs)
        g_out[n], d_out[n], m_out[n], v_out[n] = outs
    shapes = [small[n].shape for n in SMALL]
    packed = [_pack([src[n] for n in SMALL]) for src in (small, w, m, v)]
    upd = _adamw(*packed, "adamw_small")
    for dst, buf in zip((d_out, m_out, v_out), upd):
        dst.update(zip(SMALL, _unpack(buf, shapes)))
    g_out.update(small)

    return (loss, grad_x[None], *[g_out[n] for n in WEIGHTS], *[d_out[n] for n in WEIGHTS],
            *[m_out[n] for n in WEIGHTS], *[v_out[n] for n in WEIGHTS])
```

```python
import functools
import math

import numpy as np
import jax
import jax.numpy as jnp
from jax import lax
from jax.experimental import pallas as pl
from jax.experimental.pallas import tpu as pltpu

f32 = jnp.float32
bf16 = jnp.bfloat16

D_MODEL = 1024
DEPTH = 2
HEAD_DIM = 64
CONV_CH = 256
CONV_WIDTH = 31
ATTN_HEADS = 8
ATTN_CH = ATTN_HEADS * HEAD_DIM
DILATIONS = (1, 4, 16)
ATTN_BLOCK = 128
N_BUCKETS = 32
MAX_DISTANCE = 2048
GMLP_CH = 256
GMLP_GROUPS = 4
GMLP_GROUP_DIM = GMLP_CH // GMLP_GROUPS
CHUNK = 128
IN_CH = 2 * CONV_CH + 3 * ATTN_CH + 2 * GMLP_CH
D_FF = 2816
FFN_CONV_WIDTH = 3
LN_EPS = 1e-5
ALPHA = (2.0 * DEPTH) ** 0.25
ADAM_LR = 0.001
ADAM_B1 = 0.9
ADAM_B2 = 0.999
ADAM_EPS = 1e-08
ADAM_WD = 0.01
ADAM_STEP = 10

CONV_HALO = 32
FFN_HALO = 8
NEG = -1e30
MIB = 2 ** 20
NT_DIMS = (((1,), (1,)), ((), ()))
TN_DIMS = (((0,), (0,)), ((), ()))
MESH_ID = pl.DeviceIdType.MESH


def _cp(sem, vmem_mib):
    return pltpu.CompilerParams(dimension_semantics=sem, vmem_limit_bytes=vmem_mib * MIB)


def _resident(shape):
    nd = len(shape)
    return pl.BlockSpec(shape, lambda *_: (0,) * nd, pipeline_mode=pl.Buffered(1))


def _acc(shape):
    nd = len(shape)
    return pl.BlockSpec(shape, lambda *_: (0,) * nd)


def _sig(x):
    return 1.0 / (1.0 + jnp.exp(-x))


def _ln_stats(z):
    mu = jnp.mean(z, axis=-1, keepdims=True)
    zc = z - mu
    var = jnp.mean(zc * zc, axis=-1, keepdims=True)
    rstd = lax.rsqrt(var + LN_EPS)
    return zc * rstd, rstd


def _ln_bwd(dy, xhat, rstd, g):
    dxh = dy * g
    m1 = jnp.mean(dxh, axis=-1, keepdims=True)
    m2 = jnp.mean(dxh * xhat, axis=-1, keepdims=True)
    return rstd * (dxh - m1 - xhat * m2)


def _colsum(x):
    return jnp.sum(x, axis=0, keepdims=True)


def _t5_bucket_np(dist):
    max_exact = N_BUCKETS // 2
    dd = np.maximum(dist, 1).astype(np.float64)
    large = max_exact + (np.log(dd / max_exact) / math.log(MAX_DISTANCE / max_exact)
                         * (N_BUCKETS - max_exact)).astype(np.int32)
    large = np.minimum(large, N_BUCKETS - 1)
    return np.where(dist < max_exact, dist, large).astype(np.int32)


def _bucket_ids():
    qi = np.arange(ATTN_BLOCK)[:, None]
    kj = np.arange(2 * ATTN_BLOCK)[None, :]
    dist = np.clip(qi + ATTN_BLOCK - kj, 0, None)
    return np.stack([_t5_bucket_np(dist * d) for d in DILATIONS]).astype(np.int32)


LANES = 128
QKV_CH = 3 * ATTN_CH
PERM_TILE = 512


def _slabs(n, rows):
    return [pltpu.VMEM((rows, LANES), f32)] * n


def _rows_of(slab, r, n, d):
    return slab[...] if d == 1 else slab[pl.ds(r, n, stride=d), :]


def _set_rows_of(slab, r, n, d, val):
    if d == 1:
        slab[...] = val
    else:
        slab[pl.ds(r, n, stride=d), :] = val


def _perm_spec(d, ch):
    return pl.BlockSpec((d, PERM_TILE // d, ch), lambda i: (0, i, 0))


def _perm_shape(S, d, ch, dtype):
    return jax.ShapeDtypeStruct((d, S // d, ch), dtype)


def _inproj_fwd(x, w, b):
    S = x.shape[0]
    T = PERM_TILE
    nsl = QKV_CH // LANES

    def body(x_ref, w_ref, b_ref, a_ref, c_ref, *rest):
        q_refs = rest[:len(DILATIONS)]
        slabs = rest[len(DILATIONS):]
        h = jnp.dot(x_ref[...].astype(bf16), w_ref[...], preferred_element_type=f32) + b_ref[...]
        a_ref[...] = h[:, :2 * CONV_CH]
        q0 = 2 * CONV_CH
        c_ref[...] = h[:, q0 + QKV_CH:]
        for j in range(nsl):
            piece = h[:, q0 + LANES * j:q0 + LANES * (j + 1)]
            if LANES * j < ATTN_CH:
                piece = piece * (HEAD_DIM ** -0.5)
            slabs[j][...] = piece
        for d, q_ref in zip(DILATIONS, q_refs):
            for r in range(d):
                for j in range(nsl):
                    q_ref[r, :, LANES * j:LANES * (j + 1)] = _rows_of(slabs[j], r, T // d, d).astype(bf16)

    row = lambda c: pl.BlockSpec((T, c), lambda i: (i, 0))
    return pl.pallas_call(
        body, grid=(S // T,), name="inproj_fwd",
        out_shape=(jax.ShapeDtypeStruct((S, 2 * CONV_CH), f32), jax.ShapeDtypeStruct((S, 2 * GMLP_CH), f32))
        + tuple(_perm_shape(S, d, QKV_CH, bf16) for d in DILATIONS),
        in_specs=[row(D_MODEL), _resident((D_MODEL, IN_CH)), _resident((1, IN_CH))],
        out_specs=(row(2 * CONV_CH), row(2 * GMLP_CH)) + tuple(_perm_spec(d, QKV_CH) for d in DILATIONS),
        scratch_shapes=_slabs(nsl, T),
        compiler_params=_cp(("parallel",), 48),
    )(x, w, b)


CONV_GROUP = 64


def _window_rolls(starts):
    groups = {}
    for s in starts:
        groups.setdefault((-s) % SUBLANES, []).append(s)
    return dict(sorted(groups.items()))


def _conv_fwd(a_in, dw_w, dw_b, ln_g, ln_b):
    S = a_in.shape[0]
    T = 512
    hb = T // CONV_HALO

    def body(a_ref, halo_ref, w_ref, b_ref, g_ref, be_ref, out_ref, hc_ref, buf):
        i = pl.program_id(0)
        am = a_ref[...]
        ah = halo_ref[...]
        hgh = ah[:, :CONV_CH] * _sig(ah[:, CONV_CH:])
        buf[0:CONV_HALO, :] = jnp.where(i > 0, hgh, 0.0)
        buf[CONV_HALO:, :] = am[:, :CONV_CH] * _sig(am[:, CONV_CH:])
        starts = _window_rolls(range(CONV_HALO - (CONV_WIDTH - 1), CONV_HALO + 1))
        slabs = [slice(LANES * j, LANES * (j + 1)) for j in range(CONV_CH // LANES)]

        def step(g, _):
            r0 = pl.multiple_of(g * CONV_GROUP, CONV_GROUP)
            rows = pl.ds(r0, CONV_GROUP)
            for cs in slabs:
                ext = buf[pl.ds(r0, CONV_GROUP + CONV_HALO), cs]
                acc = jnp.broadcast_to(b_ref[:, cs], (CONV_GROUP, LANES))
                for b, ss in starts.items():
                    rolled = ext if b == 0 else pltpu.roll(ext, b, 0)
                    for s in ss:
                        k = s - (CONV_HALO - (CONV_WIDTH - 1))
                        acc = acc + w_ref[k:k + 1, cs] * rolled[s + b:s + b + CONV_GROUP]
                hc_ref[rows, cs] = acc
            return 0

        lax.fori_loop(0, T // CONV_GROUP, step, 0)
        xhat, _ = _ln_stats(hc_ref[...])
        y = xhat * g_ref[...] + be_ref[...]
        out_ref[...] = (y * _sig(y)).astype(bf16)

    return pl.pallas_call(
        body, grid=(S // T,), name="conv_fwd",
        out_shape=(jax.ShapeDtypeStruct((S, CONV_CH), bf16), jax.ShapeDtypeStruct((S, CONV_CH), f32)),
        in_specs=[pl.BlockSpec((T, 2 * CONV_CH), lambda i: (i, 0)),
                  pl.BlockSpec((CONV_HALO, 2 * CONV_CH), lambda i: (jnp.maximum(i * hb - 1, 0), 0)),
                  _acc((32, CONV_CH)), _acc((1, CONV_CH)), _acc((1, CONV_CH)), _acc((1, CONV_CH))],
        out_specs=(pl.BlockSpec((T, CONV_CH), lambda i: (i, 0)), pl.BlockSpec((T, CONV_CH), lambda i: (i, 0))),
        scratch_shapes=[pltpu.VMEM((T + CONV_HALO, CONV_CH), f32)],
        compiler_params=_cp(("parallel",), 32),
    )(a_in, a_in, dw_w, dw_b, ln_g, ln_b)


def _bias_build(table, buckets):
    def body(t_ref, bk_ref, o_ref):
        h = pl.program_id(1)
        ids = bk_ref[0]
        acc = jnp.zeros((ATTN_BLOCK, 2 * ATTN_BLOCK), f32)
        for b in range(N_BUCKETS):
            acc = jnp.where(ids == b, t_ref[b, h], acc)
        row = lax.broadcasted_iota(jnp.int32, acc.shape, 0)
        col = lax.broadcasted_iota(jnp.int32, acc.shape, 1)
        o_ref[0, 0] = jnp.where((col >= row) & (col <= row + ATTN_BLOCK), acc, NEG)

    return pl.pallas_call(
        body, grid=(len(DILATIONS), ATTN_HEADS), name="bias_build",
        out_shape=jax.ShapeDtypeStruct((len(DILATIONS), ATTN_HEADS, ATTN_BLOCK, 2 * ATTN_BLOCK), f32),
        in_specs=[pl.BlockSpec(memory_space=pltpu.SMEM),
                  pl.BlockSpec((1, ATTN_BLOCK, 2 * ATTN_BLOCK), lambda p, h: (p, 0, 0))],
        out_specs=pl.BlockSpec((1, 1, ATTN_BLOCK, 2 * ATTN_BLOCK), lambda p, h: (p, h, 0, 0)),
        compiler_params=_cp(("arbitrary", "arbitrary"), 16),
    )(table, buckets)


def _head_tile(tile, h, col):
    lane_head = lax.broadcasted_iota(jnp.int32, tile.shape, 1) // 16
    return jnp.where(lane_head == h, col, tile)


HEAD_PAIRS = ATTN_HEADS // 2
UNITS_PER_BLOCK = ATTN_HEADS


def _attn_tile(L):
    return min(512, L)


def _mask_logits(logits, first_block, n):
    if not first_block:
        return logits
    col = lax.broadcasted_iota(jnp.int32, logits.shape, 1)
    return jnp.where((col >= ATTN_BLOCK) | (n > 0), logits, NEG)


def _head_lanes(a):
    lane = lax.broadcasted_iota(jnp.int32, (ATTN_BLOCK, LANES), 1)
    return (lane < HEAD_DIM) if a == 0 else (lane >= HEAD_DIM)


def _pair_keys(cur_ref, halo_ref, part, b, j):
    B = ATTN_BLOCK
    c0 = part * ATTN_CH + LANES * j
    own = cur_ref[B * b:B * (b + 1), c0:c0 + LANES]
    prev = halo_ref[:, LANES * j:LANES * (j + 1)] if b == 0 else cur_ref[B * (b - 1):B * b, c0:c0 + LANES]
    return jnp.concatenate([prev, own], axis=0)


def _attn_fwd_pattern(qkv, bias, d):
    _, L, _ = qkv.shape
    B = ATTN_BLOCK
    QB = _attn_tile(L)
    nsb = QB // B
    U = nsb * UNITS_PER_BLOCK

    def body(cur_ref, hk_ref, hv_ref, b_ref, o_ref, lse_ref, lg, pb):
        n = pl.program_id(1)
        for b in range(nsb):
            for j in range(HEAD_PAIRS):
                q2 = cur_ref[B * b:B * (b + 1), LANES * j:LANES * (j + 1)]
                k2 = _pair_keys(cur_ref, hk_ref, 1, b, j)
                for a in range(2):
                    u = (b * HEAD_PAIRS + j) * 2 + a
                    qm = jnp.where(_head_lanes(a), q2, jnp.zeros_like(q2))
                    logits = lax.dot_general(qm, k2, NT_DIMS, preferred_element_type=f32) + b_ref[2 * j + a]
                    lg[B * u:B * (u + 1), :] = _mask_logits(logits, b == 0, n)
        m = jnp.max(lg[...], axis=1, keepdims=True)
        p = jnp.exp(lg[...] - m)
        s = jnp.sum(p, axis=1, keepdims=True)
        pb[...] = p.astype(bf16)
        lse = m + jnp.log(s)
        inv = 1.0 / s
        for b in range(nsb):
            tile = jnp.zeros((B, B), f32)
            for j in range(HEAD_PAIRS):
                v2 = _pair_keys(cur_ref, hv_ref, 2, b, j)
                outs = []
                for a in range(2):
                    u = (b * HEAD_PAIRS + j) * 2 + a
                    rows = slice(B * u, B * (u + 1))
                    outs.append(jnp.dot(pb[rows, :], v2, preferred_element_type=f32) * inv[rows])
                    tile = _head_tile(tile, 2 * j + a, lse[rows])
                o_ref[B * b:B * (b + 1), LANES * j:LANES * (j + 1)] = jnp.where(_head_lanes(0), outs[0], outs[1])
            lse_ref[B * b:B * (b + 1), :] = tile

    halo = lambda part: pl.BlockSpec((None, B, ATTN_CH), lambda r, n: (r, jnp.maximum(n * nsb - 1, 0), part))
    tile_spec = lambda c: pl.BlockSpec((None, QB, c), lambda r, n: (r, n, 0))
    return pl.pallas_call(
        body, grid=(d, L // QB), name=f"attn_fwd_d{d}",
        out_shape=(jax.ShapeDtypeStruct((d, L, ATTN_CH), f32), jax.ShapeDtypeStruct((d, L, B), f32)),
        in_specs=[tile_spec(QKV_CH), halo(1), halo(2), _resident((ATTN_HEADS, B, 2 * B))],
        out_specs=(tile_spec(ATTN_CH), tile_spec(B)),
        scratch_shapes=[pltpu.VMEM((U * B, 2 * B), f32), pltpu.VMEM((U * B, 2 * B), bf16)],
        compiler_params=_cp(("parallel", "parallel"), 40),
    )(qkv, qkv, qkv, bias)


def _attn_merge(parts):
    S = parts[0][0].shape[0] * parts[0][0].shape[1]
    T = PERM_TILE
    nsl = ATTN_CH // LANES
    n_p = len(DILATIONS)

    def body(*refs):
        ins = refs[:2 * n_p]
        out_ref, lse_ref = refs[2 * n_p:2 * n_p + 2]
        slabs = refs[2 * n_p + 2:]
        lses = []
        for p, d in enumerate(DILATIONS):
            o_ref, l_ref = ins[2 * p], ins[2 * p + 1]
            osl = slabs[p * (nsl + 1):p * (nsl + 1) + nsl]
            lsl = slabs[p * (nsl + 1) + nsl]
            for r in range(d):
                for j in range(nsl):
                    _set_rows_of(osl[j], r, T // d, d, o_ref[r, :, LANES * j:LANES * (j + 1)])
                _set_rows_of(lsl, r, T // d, d, l_ref[r])
            lses.append(lsl[...])
        big = functools.reduce(jnp.maximum, lses)
        ws = [jnp.exp(l - big) for l in lses]
        tot = functools.reduce(lambda a_, b_: a_ + b_, ws)
        lse_ref[...] = big + jnp.log(tot)
        ws = [w / tot for w in ws]
        for j in range(nsl):
            acc = jnp.zeros((T, LANES), f32)
            for p in range(n_p):
                wa = ws[p][:, 32 * j:32 * j + 1]
                wb = ws[p][:, 32 * j + 16:32 * j + 17]
                lane = lax.broadcasted_iota(jnp.int32, (T, LANES), 1)
                acc = acc + jnp.where(lane < HEAD_DIM, wa, wb) * slabs[p * (nsl + 1) + j][...]
            out_ref[:, LANES * j:LANES * (j + 1)] = acc.astype(bf16)

    in_specs, args = [], []
    for (o, l), d in zip(parts, DILATIONS):
        in_specs += [_perm_spec(d, ATTN_CH), _perm_spec(d, ATTN_BLOCK)]
        args += [o, l]
    row = lambda c: pl.BlockSpec((T, c), lambda i: (i, 0))
    return pl.pallas_call(
        body, grid=(S // T,), name="attn_merge",
        out_shape=(jax.ShapeDtypeStruct((S, ATTN_CH), bf16), jax.ShapeDtypeStruct((S, ATTN_BLOCK), f32)),
        in_specs=in_specs, out_specs=(row(ATTN_CH), row(ATTN_BLOCK)),
        scratch_shapes=_slabs(n_p * (nsl + 1), T),
        compiler_params=_cp(("parallel",), 40),
    )(*args)


def _attn_fwd(qkvs, bias):
    parts = [_attn_fwd_pattern(q, bias[p], d) for p, (q, d) in enumerate(zip(qkvs, DILATIONS))]
    return _attn_merge(parts)


def _tril_bf16(w):
    row = lax.broadcasted_iota(jnp.int32, (CHUNK, CHUNK), 0)
    col = lax.broadcasted_iota(jnp.int32, (CHUNK, CHUNK), 1)
    return jnp.where(col <= row, w, 0.0).astype(bf16)


def _gmlp_fwd(c_in, ln_g, ln_b, w_s, b_s_t):
    S = c_in.shape[0]
    T = 512

    def body(c_ref, g_ref, be_ref, w_ref, bs_ref, out_ref, mix):
        c = c_ref[...]
        xhat, _ = _ln_stats(c[:, GMLP_CH:])
        vb = (xhat * g_ref[...] + be_ref[...]).astype(bf16)
        for g in range(GMLP_GROUPS):
            wt = _tril_bf16(w_ref[g])
            cs = slice(GMLP_GROUP_DIM * g, GMLP_GROUP_DIM * (g + 1))
            for ci in range(T // CHUNK):
                rs = slice(CHUNK * ci, CHUNK * (ci + 1))
                mix[rs, cs] = jnp.dot(wt, vb[rs, cs], preferred_element_type=f32) + bs_ref[:, g:g + 1]
        out_ref[...] = (c[:, :GMLP_CH] * mix[...]).astype(bf16)

    return pl.pallas_call(
        body, grid=(S // T,), name="gmlp_fwd",
        out_shape=jax.ShapeDtypeStruct((S, GMLP_CH), bf16),
        in_specs=[pl.BlockSpec((T, 2 * GMLP_CH), lambda i: (i, 0)), _acc((1, GMLP_CH)), _acc((1, GMLP_CH)),
                  _acc((GMLP_GROUPS, CHUNK, CHUNK)), _acc((CHUNK, GMLP_GROUPS))],
        out_specs=pl.BlockSpec((T, GMLP_CH), lambda i: (i, 0)),
        scratch_shapes=[pltpu.VMEM((T, GMLP_CH), f32)],
        compiler_params=_cp(("parallel",), 32),
    )(c_in, ln_g, ln_b, w_s, b_s_t)


def _outproj_ln_fwd(conv_out, attn_out, gm_out, w, b, x, ln_g, ln_b):
    S = x.shape[0]
    T = 512

    def body(co_ref, ao_ref, go_ref, w_ref, b_ref, x_ref, g_ref, be_ref, cat_ref, z_ref, yb_ref):
        cat = jnp.concatenate([co_ref[...], ao_ref[...], go_ref[...]], axis=1)
        cat_ref[...] = cat
        z = jnp.dot(cat, w_ref[...], preferred_element_type=f32) + b_ref[...] + ALPHA * x_ref[...]
        z_ref[...] = z
        xhat, _ = _ln_stats(z)
        yb_ref[...] = (xhat * g_ref[...] + be_ref[...]).astype(bf16)

    row = lambda c: pl.BlockSpec((T, c), lambda i: (i, 0))
    return pl.pallas_call(
        body, grid=(S // T,), name="outproj_ln_fwd",
        out_shape=(jax.ShapeDtypeStruct((S, D_MODEL), bf16), jax.ShapeDtypeStruct((S, D_MODEL), f32),
                   jax.ShapeDtypeStruct((S, D_MODEL), bf16)),
        in_specs=[row(CONV_CH), row(ATTN_CH), row(GMLP_CH), _resident((D_MODEL, D_MODEL)), _acc((1, D_MODEL)),
                  row(D_MODEL), _acc((1, D_MODEL)), _acc((1, D_MODEL))],
        out_specs=(row(D_MODEL), row(D_MODEL), row(D_MODEL)),
        compiler_params=_cp(("parallel",), 40),
    )(conv_out, attn_out, gm_out, w, b, x, ln_g, ln_b)


GATE_ROWS = 32
GATE_COLS = 128
GATE_MM_COLS = 256
SUBLANES = 8


def _gate_cols(c0):
    return slice(c0, c0 + GATE_COLS), slice(D_FF + c0, D_FF + c0 + GATE_COLS)


def _bcast_rows(ref, k, cs):
    return jnp.broadcast_to(ref[k:k + 1, cs], (GATE_ROWS, GATE_COLS))


def _fold_rows(z):
    acc = z[0:SUBLANES]
    for r in range(SUBLANES, GATE_ROWS, SUBLANES):
        acc = acc + z[r:r + SUBLANES]
    return acc


def _ffn_up_gate_fwd(x1b, w, b, conv_w, conv_b):
    S = x1b.shape[0]
    T = 256
    H = FFN_HALO
    K = FFN_CONV_WIDTH

    def body(x_ref, w_ref, b_ref, cw_ref, cb_ref, hfb_ref, hc_ref, act_ref, hbuf, carry):
        @pl.when(pl.program_id(0) == 0)
        def _():
            carry[...] = jnp.zeros_like(carry)
        x = x_ref[...]
        for m0 in range(0, D_FF, GATE_MM_COLS):
            for cm in (slice(m0, m0 + GATE_MM_COLS), slice(D_FF + m0, D_FF + m0 + GATE_MM_COLS)):
                h = jnp.dot(x, w_ref[:, cm], preferred_element_type=f32) + b_ref[:, cm]
                hbuf[:, cm] = h
                hfb_ref[:, cm] = h.astype(bf16)
            for c0 in range(m0, m0 + GATE_MM_COLS, GATE_COLS):
                cols = _gate_cols(c0)
                wts = [[_bcast_rows(cw_ref, k, cs) for k in range(K)] + [_bcast_rows(cb_ref, 0, cs)] for cs in cols]

                def step(rg, tails, cols=cols, wts=wts):
                    rows = pl.ds(pl.multiple_of(rg * GATE_ROWS, GATE_ROWS), GATE_ROWS)
                    hc, new_tails = [], []
                    for cs, wt, tail in zip(cols, wts, tails):
                        h = hbuf[rows, cs]
                        ext = jnp.concatenate([tail, h], axis=0)
                        acc = wt[K] + wt[K - 1] * h
                        for back in range(1, K):
                            acc = acc + wt[K - 1 - back] * pltpu.roll(ext, back, 0)[H:]
                        hc_ref[rows, cs] = acc
                        hc.append(acc)
                        new_tails.append(h[GATE_ROWS - H:])
                    act_ref[rows, cols[0]] = (hc[0] * _sig(hc[0]) * hc[1]).astype(bf16)
                    return tuple(new_tails)

                tails = lax.fori_loop(0, T // GATE_ROWS, step, tuple(carry[:, cs] for cs in cols), unroll=True)
                for cs, tail in zip(cols, tails):
                    carry[:, cs] = tail

    row = lambda c: pl.BlockSpec((T, c), lambda i: (i, 0))
    return pl.pallas_call(
        body, grid=(S // T,), name="ffn_up_gate_fwd",
        out_shape=(jax.ShapeDtypeStruct((S, 2 * D_FF), bf16), jax.ShapeDtypeStruct((S, 2 * D_FF), f32),
                   jax.ShapeDtypeStruct((S, D_FF), bf16)),
        in_specs=[row(D_MODEL), _resident((D_MODEL, 2 * D_FF)), _acc((1, 2 * D_FF)), _acc((8, 2 * D_FF)),
                  _acc((1, 2 * D_FF))],
        out_specs=(row(2 * D_FF), row(2 * D_FF), row(D_FF)),
        scratch_shapes=[pltpu.VMEM((T, 2 * D_FF), f32), pltpu.VMEM((H, 2 * D_FF), f32)],
        compiler_params=_cp(("arbitrary",), 56),
    )(x1b, w, b, conv_w, conv_b)


def _ffn_down_ln_fwd(act, w, b, z1, ln1_g, ln1_b, ln_g, ln_b):
    S = act.shape[0]
    T = 512

    def body(a_ref, w_ref, b_ref, z1_ref, g1_ref, be1_ref, g_ref, be_ref, z_ref, y_ref):
        subs = [slice(s0, s0 + T // 2) for s0 in (0, T // 2)]
        zs = [jnp.dot(a_ref[rs, :], w_ref[...], preferred_element_type=f32) + b_ref[...]
              + ALPHA * (_ln_stats(z1_ref[rs, :])[0] * g1_ref[...] + be1_ref[...]) for rs in subs]
        for rs, z in zip(subs, zs):
            z_ref[rs, :] = z
            xhat, _ = _ln_stats(z)
            y_ref[rs, :] = xhat * g_ref[...] + be_ref[...]

    row = lambda c: pl.BlockSpec((T, c), lambda i: (i, 0))
    return pl.pallas_call(
        body, grid=(S // T,), name="ffn_down_ln_fwd",
        out_shape=(jax.ShapeDtypeStruct((S, D_MODEL), f32), jax.ShapeDtypeStruct((S, D_MODEL), f32)),
        in_specs=[row(D_FF), _resident((D_FF, D_MODEL)), _acc((1, D_MODEL)), row(D_MODEL)] + [_acc((1, D_MODEL))] * 4,
        out_specs=(row(D_MODEL), row(D_MODEL)),
        compiler_params=_cp(("parallel",), 40),
    )(act, w, b, z1, ln1_g, ln1_b, ln_g, ln_b)


def _ffn_down_ln_loss(act, w, b, z1, ln1_g, ln1_b, ln_g, ln_b, target):
    S = act.shape[0]
    T = 512

    def body(a_ref, w_ref, b_ref, z1_ref, g1_ref, be1_ref, g_ref, be_ref, t_ref, dz_ref, dzb_ref, loss_ref, dg_ref,
             db_ref):
        @pl.when(pl.program_id(0) == 0)
        def _():
            loss_ref[...] = jnp.zeros_like(loss_ref)
            dg_ref[...] = jnp.zeros_like(dg_ref)
            db_ref[...] = jnp.zeros_like(db_ref)
        subs = [slice(s0, s0 + T // 2) for s0 in (0, T // 2)]
        zs = [jnp.dot(a_ref[rs, :], w_ref[...], preferred_element_type=f32) + b_ref[...]
              + ALPHA * (_ln_stats(z1_ref[rs, :])[0] * g1_ref[...] + be1_ref[...]) for rs in subs]
        for rs, z in zip(subs, zs):
            xhat, rstd = _ln_stats(z)
            err = xhat * g_ref[...] + be_ref[...] - t_ref[rs, :]
            loss_ref[...] += _colsum(err * err) * (0.5 / D_MODEL)
            dy = err * (1.0 / D_MODEL)
            dz = _ln_bwd(dy, xhat, rstd, g_ref[...])
            dz_ref[rs, :] = dz
            dzb_ref[rs, :] = dz.astype(bf16)
            dg_ref[...] += _colsum(dy * xhat)
            db_ref[...] += _colsum(dy)

    row = lambda c: pl.BlockSpec((T, c), lambda i: (i, 0))
    vec = jax.ShapeDtypeStruct((1, D_MODEL), f32)
    return pl.pallas_call(
        body, grid=(S // T,), name="ffn_down_ln_loss",
        out_shape=(jax.ShapeDtypeStruct((S, D_MODEL), f32), jax.ShapeDtypeStruct((S, D_MODEL), bf16), vec, vec, vec),
        in_specs=[row(D_FF), _resident((D_FF, D_MODEL)), _acc((1, D_MODEL)), row(D_MODEL)] + [_acc((1, D_MODEL))] * 4
        + [row(D_MODEL)],
        out_specs=(row(D_MODEL), row(D_MODEL), _acc((1, D_MODEL)), _acc((1, D_MODEL)), _acc((1, D_MODEL))),
        compiler_params=_cp(("arbitrary",), 40),
    )(act, w, b, z1, ln1_g, ln1_b, ln_g, ln_b, target)


def _dgrad_ln_bwd(g, w, dz_res, z, ln_g, name):
    S, K = g.shape
    SUB = 256
    T = 2 * SUB if S % (2 * SUB) == 0 else SUB
    with_ln = z is not None

    def body(*refs):
        if with_ln:
            g_ref, w_ref, r_ref, z_ref, lg_ref, dz_ref, dzb_ref, dg_ref, db_ref = refs
        else:
            g_ref, w_ref, r_ref, dx_ref = refs
        subs = [slice(s0, s0 + SUB) for s0 in range(0, T, SUB)]
        dxs = [lax.dot_general(g_ref[rs, :], w_ref[...], NT_DIMS, preferred_element_type=f32) + ALPHA * r_ref[rs, :]
               for rs in subs]
        if not with_ln:
            for rs, dx in zip(subs, dxs):
                dx_ref[rs, :] = dx
            return

        @pl.when(pl.program_id(0) == 0)
        def _():
            dg_ref[...] = jnp.zeros_like(dg_ref)
            db_ref[...] = jnp.zeros_like(db_ref)
        for rs, dx in zip(subs, dxs):
            xhat, rstd = _ln_stats(z_ref[rs, :])
            dz = _ln_bwd(dx, xhat, rstd, lg_ref[...])
            dz_ref[rs, :] = dz
            dzb_ref[rs, :] = dz.astype(bf16)
            dg_ref[...] += _colsum(dx * xhat)
            db_ref[...] += _colsum(dx)

    row = pl.BlockSpec((T, D_MODEL), lambda i: (i, 0))
    vec = jax.ShapeDtypeStruct((1, D_MODEL), f32)
    in_specs = [pl.BlockSpec((T, K), lambda i: (i, 0)), _resident((D_MODEL, K)), row]
    args = [g, w, dz_res]
    if with_ln:
        in_specs += [row, _acc((1, D_MODEL))]
        args += [z, ln_g]
        out_shape = (jax.ShapeDtypeStruct((S, D_MODEL), f32), jax.ShapeDtypeStruct((S, D_MODEL), bf16), vec, vec)
        out_specs = (row, row, _acc((1, D_MODEL)), _acc((1, D_MODEL)))
    else:
        out_shape = jax.ShapeDtypeStruct((S, D_MODEL), f32)
        out_specs = row
    return pl.pallas_call(
        body, grid=(S // T,), name=name, out_shape=out_shape, in_specs=in_specs, out_specs=out_specs,
        compiler_params=_cp(("arbitrary",), 48),
    )(*args)


def _ffn_down_gate_bwd(dzb, w_down, hfb, hc, conv_w):
    S = hc.shape[0]
    T = 256
    H = FFN_HALO
    nt = S // T
    K = FFN_CONV_WIDTH

    def body(dz_ref, w_ref, h_ref, hc_ref, cw_ref, dh_ref, dw_ref, dcb_ref, da_buf, carry):
        @pl.when(pl.program_id(0) == 0)
        def _():
            dw_ref[...] = jnp.zeros_like(dw_ref)
            dcb_ref[...] = jnp.zeros_like(dcb_ref)
            carry[...] = jnp.zeros_like(carry)
        da_buf[...] = lax.dot_general(dz_ref[...], w_ref[...], NT_DIMS, preferred_element_type=f32)
        ngroups = T // GATE_ROWS
        for c0 in range(0, D_FF, GATE_COLS):
            cols = _gate_cols(c0)
            wts = [[_bcast_rows(cw_ref, k, cs) for k in range(K)] for cs in cols]

            def step(it, state, cols=cols, wts=wts):
                heads, accs = state
                rows = pl.ds(pl.multiple_of((ngroups - 1 - it) * GATE_ROWS, GATE_ROWS), GATE_ROWS)
                g = hc_ref[rows, cols[0]]
                v = hc_ref[rows, cols[1]]
                da = da_buf[rows, cols[0]]
                sg = _sig(g)
                dms = (da * v * (sg * (1.0 + g * (1.0 - sg))), da * (g * sg))
                new_heads, new_accs = [], []
                for cs, wt, dm, head, acc in zip(cols, wts, dms, heads, accs):
                    h0 = h_ref[rows, cs].astype(f32)
                    ext = jnp.concatenate([dm, head], axis=0)
                    dh = wt[K - 1] * dm
                    acc_k = [None] * K + [acc[K] + _fold_rows(dm)]
                    acc_k[K - 1] = acc[K - 1] + _fold_rows(dm * h0)
                    for ahead in range(1, K):
                        dk = pltpu.roll(ext, GATE_ROWS + H - ahead, 0)[:GATE_ROWS]
                        dh = dh + wt[K - 1 - ahead] * dk
                        acc_k[K - 1 - ahead] = acc[K - 1 - ahead] + _fold_rows(dk * h0)
                    dh_ref[rows, cs] = dh.astype(bf16)
                    new_heads.append(dm[:H])
                    new_accs.append(tuple(acc_k))
                return tuple(new_heads), tuple(new_accs)

            zero = jnp.zeros((SUBLANES, GATE_COLS), f32)
            init = (tuple(carry[:, cs] for cs in cols), tuple(tuple(zero for _ in range(K + 1)) for _ in cols))
            heads, accs = lax.fori_loop(0, ngroups, step, init, unroll=True)
            for cs, head, acc in zip(cols, heads, accs):
                carry[:, cs] = head
                dcb_ref[:, cs] += _colsum(acc[K])
                for k in range(K):
                    dw_ref[k:k + 1, cs] += _colsum(acc[k])

    tile = lambda c: pl.BlockSpec((T, c), lambda i: (nt - 1 - i, 0))
    return pl.pallas_call(
        body, grid=(nt,), name="ffn_down_gate_bwd",
        out_shape=(jax.ShapeDtypeStruct((S, 2 * D_FF), bf16), jax.ShapeDtypeStruct((8, 2 * D_FF), f32),
                   jax.ShapeDtypeStruct((1, 2 * D_FF), f32)),
        in_specs=[tile(D_MODEL), _resident((D_FF, D_MODEL)), tile(2 * D_FF), tile(2 * D_FF), _acc((8, 2 * D_FF))],
        out_specs=(tile(2 * D_FF), _acc((8, 2 * D_FF)), _acc((1, 2 * D_FF))),
        scratch_shapes=[pltpu.VMEM((T, D_FF), f32), pltpu.VMEM((H, 2 * D_FF), f32)],
        compiler_params=_cp(("arbitrary",), 48),
    )(dzb, w_down, hfb, hc, conv_w)


def _wgrad(a, g, tn, name, rows=1024):
    S, K = a.shape
    N = g.shape[1]
    T = rows if S % rows == 0 else S

    def body(a_ref, g_ref, dw_ref, db_ref):
        @pl.when(pl.program_id(1) == 0)
        def _():
            dw_ref[...] = jnp.zeros_like(dw_ref)
            db_ref[...] = jnp.zeros_like(db_ref)
        gt = g_ref[...]
        dw_ref[...] += lax.dot_general(a_ref[...].astype(bf16), gt, TN_DIMS, preferred_element_type=f32)
        db_ref[...] += _colsum(gt.astype(f32))

    return pl.pallas_call(
        body, grid=(N // tn, S // T), name=name,
        out_shape=(jax.ShapeDtypeStruct((K, N), f32), jax.ShapeDtypeStruct((1, N), f32)),
        in_specs=[pl.BlockSpec((T, K), lambda j, i: (i, 0)), pl.BlockSpec((T, tn), lambda j, i: (i, j))],
        out_specs=(pl.BlockSpec((K, tn), lambda j, i: (0, j)), pl.BlockSpec((1, tn), lambda j, i: (0, j))),
        compiler_params=_cp(("parallel", "arbitrary"), 56),
    )(a, g)


def _outproj_dgrad(dzb, w, attn_out, lse):
    S = dzb.shape[0]
    T = PERM_TILE
    nsl = ATTN_CH // LANES
    n_p = len(DILATIONS)

    def body(g_ref, w_ref, ao_ref, lse_ref, dco_ref, dgo_ref, *rest):
        do_refs = rest[:n_p]
        st_refs = rest[n_p:2 * n_p]
        slabs = rest[2 * n_p:]
        dcat = lax.dot_general(g_ref[...], w_ref[...], NT_DIMS, preferred_element_type=f32)
        dco_ref[...] = dcat[:, :CONV_CH]
        dgo_ref[...] = dcat[:, CONV_CH + ATTN_CH:]
        lane = lax.broadcasted_iota(jnp.int32, (T, LANES), 1)
        st = lse_ref[...]
        for j in range(nsl):
            dO = dcat[:, CONV_CH + LANES * j:CONV_CH + LANES * (j + 1)]
            prod = dO * ao_ref[:, LANES * j:LANES * (j + 1)].astype(f32)
            for a in range(2):
                in_head = (lane < HEAD_DIM) if a == 0 else (lane >= HEAD_DIM)
                delta = jnp.sum(jnp.where(in_head, prod, 0.0), axis=1, keepdims=True)
                st = jnp.where((lane // 16 == 2 * j + a) & (lane % 16 >= 8), delta, st)
            slabs[j][...] = dO
        slabs[nsl][...] = st
        for d, do_ref, st_ref in zip(DILATIONS, do_refs, st_refs):
            for r in range(d):
                for j in range(nsl):
                    do_ref[r, :, LANES * j:LANES * (j + 1)] = _rows_of(slabs[j], r, T // d, d).astype(bf16)
                st_ref[r] = _rows_of(slabs[nsl], r, T // d, d)

    row = lambda c: pl.BlockSpec((T, c), lambda i: (i, 0))
    return pl.pallas_call(
        body, grid=(S // T,), name="outproj_dgrad",
        out_shape=(jax.ShapeDtypeStruct((S, CONV_CH), f32), jax.ShapeDtypeStruct((S, GMLP_CH), f32))
        + tuple(_perm_shape(S, d, ATTN_CH, bf16) for d in DILATIONS)
        + tuple(_perm_shape(S, d, ATTN_BLOCK, f32) for d in DILATIONS),
        in_specs=[row(D_MODEL), _resident((D_MODEL, D_MODEL)), row(ATTN_CH), row(ATTN_BLOCK)],
        out_specs=(row(CONV_CH), row(GMLP_CH)) + tuple(_perm_spec(d, ATTN_CH) for d in DILATIONS)
        + tuple(_perm_spec(d, ATTN_BLOCK) for d in DILATIONS),
        scratch_shapes=_slabs(nsl + 1, T),
        compiler_params=_cp(("parallel",), 40),
    )(dzb, w, attn_out, lse)


def _gmlp_bwd(c_in, dgm, ln_g, ln_b, w_s, b_s_t):
    S = c_in.shape[0]
    T = 512
    nsteps = S // T

    def body(c_ref, dg_ref, g_ref, be_ref, w_ref, bs_ref, dc_ref, dlg_ref, dlb_ref, dw_ref, dbs_ref,
             du_buf, dv_buf, dm_acc):
        i = pl.program_id(0)

        @pl.when(i == 0)
        def _():
            dlg_ref[...] = jnp.zeros_like(dlg_ref)
            dlb_ref[...] = jnp.zeros_like(dlb_ref)
            dw_ref[...] = jnp.zeros_like(dw_ref)
            dm_acc[...] = jnp.zeros_like(dm_acc)
        c = c_ref[...]
        u = c[:, :GMLP_CH]
        xhat, rstd = _ln_stats(c[:, GMLP_CH:])
        vb = (xhat * g_ref[...] + be_ref[...]).astype(bf16)
        dgm_t = dg_ref[...]
        dm_all = dgm_t * u
        for g in range(GMLP_GROUPS):
            wt = _tril_bf16(w_ref[g])
            cs = slice(GMLP_GROUP_DIM * g, GMLP_GROUP_DIM * (g + 1))
            dw_g = jnp.zeros((CHUNK, CHUNK), f32)
            for ci in range(T // CHUNK):
                rs = slice(CHUNK * ci, CHUNK * (ci + 1))
                v_c = vb[rs, cs]
                mixed = jnp.dot(wt, v_c, preferred_element_type=f32) + bs_ref[:, g:g + 1]
                dm = dm_all[rs, cs]
                dmb = dm.astype(bf16)
                du_buf[rs, cs] = dgm_t[rs, cs] * mixed
                dv_buf[rs, cs] = lax.dot_general(wt, dmb, TN_DIMS, preferred_element_type=f32)
                dw_g = dw_g + lax.dot_general(dmb, v_c, NT_DIMS, preferred_element_type=f32)
                dm_acc[:, cs] += dm
            dw_ref[g] += dw_g
        dv = dv_buf[...]
        dvr = _ln_bwd(dv, xhat, rstd, g_ref[...])
        dlg_ref[...] += _colsum(dv * xhat)
        dlb_ref[...] += _colsum(dv)
        dc_ref[:, :GMLP_CH] = du_buf[...].astype(bf16)
        dc_ref[:, GMLP_CH:] = dvr.astype(bf16)

        @pl.when(i == nsteps - 1)
        def _():
            row = lax.broadcasted_iota(jnp.int32, (CHUNK, CHUNK), 0)
            col = lax.broadcasted_iota(jnp.int32, (CHUNK, CHUNK), 1)
            tile = jnp.zeros((CHUNK, CHUNK), f32)
            for g in range(GMLP_GROUPS):
                dw_ref[g] = jnp.where(col <= row, dw_ref[g], 0.0)
                gsum = jnp.sum(dm_acc[:, GMLP_GROUP_DIM * g:GMLP_GROUP_DIM * (g + 1)], axis=1, keepdims=True)
                tile = jnp.where(col == g, gsum, tile)
            dbs_ref[...] = tile

    vec = jax.ShapeDtypeStruct((1, GMLP_CH), f32)
    return pl.pallas_call(
        body, grid=(nsteps,), name="gmlp_bwd",
        out_shape=(jax.ShapeDtypeStruct((S, 2 * GMLP_CH), bf16), vec, vec,
                   jax.ShapeDtypeStruct((GMLP_GROUPS, CHUNK, CHUNK), f32), jax.ShapeDtypeStruct((CHUNK, CHUNK), f32)),
        in_specs=[pl.BlockSpec((T, 2 * GMLP_CH), lambda i: (i, 0)), pl.BlockSpec((T, GMLP_CH), lambda i: (i, 0)),
                  _acc((1, GMLP_CH)), _acc((1, GMLP_CH)), _acc((GMLP_GROUPS, CHUNK, CHUNK)), _acc((CHUNK, GMLP_GROUPS))],
        out_specs=(pl.BlockSpec((T, 2 * GMLP_CH), lambda i: (i, 0)), _acc((1, GMLP_CH)), _acc((1, GMLP_CH)),
                   _acc((GMLP_GROUPS, CHUNK, CHUNK)), _acc((CHUNK, CHUNK))),
        scratch_shapes=[pltpu.VMEM((T, GMLP_CH), f32), pltpu.VMEM((T, GMLP_CH), f32), pltpu.VMEM((CHUNK, GMLP_CH), f32)],
        compiler_params=_cp(("arbitrary",), 32),
    )(c_in, dgm, ln_g, ln_b, w_s, b_s_t)


def _attn_bwd_pattern(qkv, d_out, stats, bias, d):
    _, L, _ = qkv.shape
    B = ATTN_BLOCK
    QB = _attn_tile(L)
    nsb = QB // B
    nt = L // QB
    U = nsb * UNITS_PER_BLOCK
    KV = 2 * ATTN_CH

    def body(cur_ref, hk_ref, hv_ref, do_ref, st_ref, b_ref, dqkv_ref, dbias_ref, lg, dp, pb, dsb, dkv, carry):
        r = pl.program_id(0)
        i = pl.program_id(1)
        n = nt - 1 - i

        @pl.when((r == 0) & (i == 0))
        def _():
            dbias_ref[...] = jnp.zeros_like(dbias_ref)

        @pl.when(i == 0)
        def _():
            carry[...] = jnp.zeros_like(carry)

        def operands(b, j, a):
            rows = slice(B * b, B * (b + 1))
            q2 = cur_ref[rows, LANES * j:LANES * (j + 1)]
            do2 = do_ref[rows, LANES * j:LANES * (j + 1)]
            keep = _head_lanes(a)
            return jnp.where(keep, q2, jnp.zeros_like(q2)), jnp.where(keep, do2, jnp.zeros_like(do2))

        for b in range(nsb):
            for j in range(HEAD_PAIRS):
                k2 = _pair_keys(cur_ref, hk_ref, 1, b, j)
                v2 = _pair_keys(cur_ref, hv_ref, 2, b, j)
                for a in range(2):
                    u = (b * HEAD_PAIRS + j) * 2 + a
                    qm, dom = operands(b, j, a)
                    logits = lax.dot_general(qm, k2, NT_DIMS, preferred_element_type=f32) + b_ref[2 * j + a]
                    lg[B * u:B * (u + 1), :] = _mask_logits(logits, b == 0, n)
                    dp[B * u:B * (u + 1), :] = lax.dot_general(dom, v2, NT_DIMS, preferred_element_type=f32)
        for b in range(nsb):
            for j in range(HEAD_PAIRS):
                for a in range(2):
                    u = (b * HEAD_PAIRS + j) * 2 + a
                    rows = slice(B * u, B * (u + 1))
                    lane0 = 32 * j + 16 * a
                    lse = st_ref[B * b:B * (b + 1), lane0:lane0 + 1]
                    delta = st_ref[B * b:B * (b + 1), lane0 + 8:lane0 + 9]
                    p = jnp.exp(lg[rows, :] - lse)
                    ds = p * (dp[rows, :] - delta)
                    pb[rows, :] = p.astype(bf16)
                    dsb[rows, :] = ds.astype(bf16)
                    dbias_ref[2 * j + a] += ds
        dkv[...] = jnp.zeros_like(dkv)
        for b in range(nsb):
            for j in range(HEAD_PAIRS):
                k2 = _pair_keys(cur_ref, hk_ref, 1, b, j)
                dq, dk2, dv2 = [], None, None
                for a in range(2):
                    u = (b * HEAD_PAIRS + j) * 2 + a
                    rows = slice(B * u, B * (u + 1))
                    qm, dom = operands(b, j, a)
                    ds_u = dsb[rows, :]
                    dq.append(jnp.dot(ds_u, k2, preferred_element_type=f32))
                    dk_u = lax.dot_general(ds_u, qm, TN_DIMS, preferred_element_type=f32)
                    dv_u = lax.dot_general(pb[rows, :], dom, TN_DIMS, preferred_element_type=f32)
                    dk2 = dk_u if dk2 is None else dk2 + dk_u
                    dv2 = dv_u if dv2 is None else dv2 + dv_u
                dq2 = jnp.where(_head_lanes(0), dq[0], dq[1]) * (HEAD_DIM ** -0.5)
                dqkv_ref[B * b:B * (b + 1), LANES * j:LANES * (j + 1)] = dq2.astype(bf16)
                dkv[B * b:B * (b + 2), LANES * j:LANES * (j + 1)] += dk2
                dkv[B * b:B * (b + 2), ATTN_CH + LANES * j:ATTN_CH + LANES * (j + 1)] += dv2
        dkv[QB:, :] += carry[...]
        dqkv_ref[:, ATTN_CH:] = dkv[B:, :].astype(bf16)
        carry[...] = dkv[0:B, :]

    halo = lambda part: pl.BlockSpec((None, B, ATTN_CH),
                                     lambda r, i: (r, jnp.maximum((nt - 1 - i) * nsb - 1, 0), part))
    tile_spec = lambda c: pl.BlockSpec((None, QB, c), lambda r, i: (r, nt - 1 - i, 0))
    return pl.pallas_call(
        body, grid=(d, nt), name=f"attn_bwd_d{d}",
        out_shape=(jax.ShapeDtypeStruct((d, L, QKV_CH), bf16), jax.ShapeDtypeStruct((ATTN_HEADS, B, 2 * B), f32)),
        in_specs=[tile_spec(QKV_CH), halo(1), halo(2), tile_spec(ATTN_CH), tile_spec(B),
                  _resident((ATTN_HEADS, B, 2 * B))],
        out_specs=(tile_spec(QKV_CH), _acc((ATTN_HEADS, B, 2 * B))),
        scratch_shapes=[pltpu.VMEM((U * B, 2 * B), f32), pltpu.VMEM((U * B, 2 * B), f32),
                        pltpu.VMEM((U * B, 2 * B), bf16), pltpu.VMEM((U * B, 2 * B), bf16),
                        pltpu.VMEM((B + QB, KV), f32), pltpu.VMEM((B, KV), f32)],
        compiler_params=_cp(("arbitrary", "arbitrary"), 48),
    )(qkv, qkv, qkv, d_out, stats, bias)


def _attn_bwd_merge(d_a, dqkvs, d_c):
    S = d_a.shape[0]
    T = PERM_TILE
    nsl = QKV_CH // LANES
    n_p = len(DILATIONS)

    def body(da_ref, *rest):
        g_refs = rest[:n_p]
        dc_ref, dh_ref = rest[n_p:n_p + 2]
        slabs = rest[n_p + 2:]
        q0 = 2 * CONV_CH
        dh_ref[:, :q0] = da_ref[...]
        dh_ref[:, q0 + QKV_CH:] = dc_ref[...]
        for p, (d, g_ref) in enumerate(zip(DILATIONS, g_refs)):
            for r in range(d):
                for j in range(nsl):
                    _set_rows_of(slabs[p * nsl + j], r, T // d, d, g_ref[r, :, LANES * j:LANES * (j + 1)].astype(f32))
        for j in range(nsl):
            acc = slabs[j][...]
            for p in range(1, n_p):
                acc = acc + slabs[p * nsl + j][...]
            dh_ref[:, q0 + LANES * j:q0 + LANES * (j + 1)] = acc.astype(bf16)

    row = lambda c: pl.BlockSpec((T, c), lambda i: (i, 0))
    return pl.pallas_call(
        body, grid=(S // T,), name="attn_bwd_merge", out_shape=jax.ShapeDtypeStruct((S, IN_CH), bf16),
        in_specs=[row(2 * CONV_CH)] + [_perm_spec(d, QKV_CH) for d in DILATIONS] + [row(2 * GMLP_CH)],
        out_specs=row(IN_CH), scratch_shapes=_slabs(n_p * nsl, T),
        compiler_params=_cp(("parallel",), 48),
    )(d_a, *dqkvs, d_c)


def _bias_table_grad(dbias, buckets):
    n = dbias.shape[0]

    def body(db_ref, bk_ref, o_ref):
        p = pl.program_id(0)
        h = pl.program_id(1)

        @pl.when((p == 0) & (h == 0))
        def _():
            o_ref[...] = jnp.zeros_like(o_ref)
        ids = bk_ref[0]
        db = db_ref[0, 0]
        row = lax.broadcasted_iota(jnp.int32, (N_BUCKETS, 128), 0)
        lane = lax.broadcasted_iota(jnp.int32, (N_BUCKETS, 128), 1)
        upd = jnp.zeros((N_BUCKETS, 128), f32)
        for b in range(N_BUCKETS):
            s = jnp.sum(jnp.sum(jnp.where(ids == b, db, 0.0), axis=1, keepdims=True), axis=0, keepdims=True)
            upd = jnp.where((row == b) & (lane == h), s, upd)
        o_ref[...] += upd

    return pl.pallas_call(
        body, grid=(n, ATTN_HEADS), name="bias_table_grad",
        out_shape=jax.ShapeDtypeStruct((N_BUCKETS, 128), f32),
        in_specs=[pl.BlockSpec((1, 1, ATTN_BLOCK, 2 * ATTN_BLOCK), lambda p, h: (p, h, 0, 0)),
                  pl.BlockSpec((1, ATTN_BLOCK, 2 * ATTN_BLOCK), lambda p, h: (p, 0, 0))],
        out_specs=_acc((N_BUCKETS, 128)),
        compiler_params=_cp(("arbitrary", "arbitrary"), 16),
    )(dbias, buckets)


def _conv_bwd(a_in, hc, dco, dw_w, ln_g, ln_b):
    S = a_in.shape[0]
    T = 512
    hb = T // CONV_HALO
    nsteps = S // T
    R = T + CONV_HALO
    K = CONV_WIDTH

    def body(a_ref, hc_ref, hcn_ref, d_ref, dn_ref, w_ref, g_ref, be_ref,
             da_ref, dw_ref, dcb_ref, dlg_ref, dlb_ref, ext, dbuf, wacc):
        i = pl.program_id(0)

        @pl.when(i == 0)
        def _():
            wacc[...] = jnp.zeros_like(wacc)
            dcb_ref[...] = jnp.zeros_like(dcb_ref)
            dlg_ref[...] = jnp.zeros_like(dlg_ref)
            dlb_ref[...] = jnp.zeros_like(dlb_ref)
        ext[0:T, :] = hc_ref[...]
        ext[T:, :] = hcn_ref[...]
        xhat, rstd = _ln_stats(ext[...])
        hl = xhat * g_ref[...] + be_ref[...]
        ext[0:T, :] = d_ref[...]
        ext[T:, :] = dn_ref[...]
        sl_ = _sig(hl)
        dhl = ext[...] * (sl_ * (1.0 + hl * (1.0 - sl_)))
        dhc = _ln_bwd(dhl, xhat, rstd, g_ref[...])
        rowi = lax.broadcasted_iota(jnp.int32, (R, CONV_CH), 0)
        dbuf[...] = jnp.where((rowi < T) | (i < nsteps - 1), dhc, 0.0)
        dlg_ref[...] += _colsum(dhl[:T] * xhat[:T])
        dlb_ref[...] += _colsum(dhl[:T])
        dcb_ref[...] += _colsum(dbuf[pl.ds(0, T), :])
        starts = _window_rolls(range(K))
        slabs = [slice(LANES * j, LANES * (j + 1)) for j in range(CONV_CH // LANES)]

        def step(g, _):
            r0 = pl.multiple_of(g * CONV_GROUP, CONV_GROUP)
            rows = pl.ds(r0, CONV_GROUP)
            for j, cs in enumerate(slabs):
                gate_cs = slice(CONV_CH + LANES * j, CONV_CH + LANES * (j + 1))
                win = dbuf[pl.ds(r0, CONV_GROUP + CONV_HALO), cs]
                a = a_ref[rows, cs]
                sg = _sig(a_ref[rows, gate_cs])
                hg = a * sg
                dhg = jnp.zeros((CONV_GROUP, LANES), f32)
                for b, ss in starts.items():
                    rolled = win if b == 0 else pltpu.roll(win, b, 0)
                    for s in ss:
                        k = K - 1 - s
                        dk = rolled[s + b:s + b + CONV_GROUP]
                        dhg = dhg + w_ref[k:k + 1, cs] * dk
                        prod = dk * hg
                        fold = prod[0:SUBLANES]
                        for r in range(SUBLANES, CONV_GROUP, SUBLANES):
                            fold = fold + prod[r:r + SUBLANES]
                        wacc[SUBLANES * k:SUBLANES * (k + 1), cs] += fold
                da_ref[rows, cs] = (dhg * sg).astype(bf16)
                da_ref[rows, gate_cs] = (dhg * hg * (1.0 - sg)).astype(bf16)
            return 0

        lax.fori_loop(0, T // CONV_GROUP, step, 0)

        @pl.when(i == nsteps - 1)
        def _():
            for k in range(K):
                dw_ref[k:k + 1, :] = _colsum(wacc[SUBLANES * k:SUBLANES * (k + 1), :])
            dw_ref[K:, :] = jnp.zeros((32 - K, CONV_CH), f32)

    vec = jax.ShapeDtypeStruct((1, CONV_CH), f32)
    nxt = lambda i: (jnp.minimum((i + 1) * hb, nsteps * hb - 1), 0)
    return pl.pallas_call(
        body, grid=(nsteps,), name="conv_bwd",
        out_shape=(jax.ShapeDtypeStruct((S, 2 * CONV_CH), bf16), jax.ShapeDtypeStruct((32, CONV_CH), f32), vec, vec, vec),
        in_specs=[pl.BlockSpec((T, 2 * CONV_CH), lambda i: (i, 0)),
                  pl.BlockSpec((T, CONV_CH), lambda i: (i, 0)), pl.BlockSpec((CONV_HALO, CONV_CH), nxt),
                  pl.BlockSpec((T, CONV_CH), lambda i: (i, 0)), pl.BlockSpec((CONV_HALO, CONV_CH), nxt),
                  _acc((32, CONV_CH)), _acc((1, CONV_CH)), _acc((1, CONV_CH))],
        out_specs=(pl.BlockSpec((T, 2 * CONV_CH), lambda i: (i, 0)), _acc((32, CONV_CH)), _acc((1, CONV_CH)),
                   _acc((1, CONV_CH)), _acc((1, CONV_CH))),
        scratch_shapes=[pltpu.VMEM((R, CONV_CH), f32), pltpu.VMEM((R, CONV_CH), f32),
                        pltpu.VMEM((SUBLANES * 32, CONV_CH), f32)],
        compiler_params=_cp(("arbitrary",), 32),
    )(a_in, hc, hc, dco, dco, dw_w, ln_g, ln_b)


def _adamw(g, w, m, v, name):
    R, C = g.shape
    T = R
    for cand in (512, 256, 128, 64, 32, 16, 8):
        if R % cand == 0 and cand * C * 4 <= MIB:
            T = cand
            break
    c1 = 1.0 / (1.0 - ADAM_B1 ** ADAM_STEP)
    c2 = 1.0 / (1.0 - ADAM_B2 ** ADAM_STEP)

    def body(g_ref, w_ref, m_ref, v_ref, d_ref, nm_ref, nv_ref):
        gg = g_ref[...]
        nm = ADAM_B1 * m_ref[...] + (1.0 - ADAM_B1) * gg
        nv = ADAM_B2 * v_ref[...] + (1.0 - ADAM_B2) * (gg * gg)
        nm_ref[...] = nm
        nv_ref[...] = nv
        d_ref[...] = -ADAM_LR * ((nm * c1) / (jnp.sqrt(nv * c2) + ADAM_EPS) + ADAM_WD * w_ref[...])

    blk = pl.BlockSpec((T, C), lambda i: (i, 0))
    sd = jax.ShapeDtypeStruct((R, C), f32)
    return pl.pallas_call(
        body, grid=(R // T,), name=name, out_shape=(sd, sd, sd), in_specs=[blk] * 4, out_specs=(blk, blk, blk),
        compiler_params=_cp(("parallel",), 48),
    )(g, w, m, v)


def _pad_rows(a, rows):
    return jnp.pad(a, ((0, rows - a.shape[0]), (0, 0)))


def _local_step(x, target, wb, late_weights, sp, sink):
    buckets = jnp.asarray(_bucket_ids())
    bias = _bias_build(sp["rel_bias_table"], buckets)
    wb = dict(wb)
    saved = []
    xl = x
    for l in range(DEPTH):
        vec = lambda name: sp[name][l][None, :]
        a_in, c_in, *qkv = _inproj_fwd(xl, wb["w_in"][l], vec("b_in"))
        conv_w = _pad_rows(sp["conv_dw_w"][l], 32)
        conv_out, hc = _conv_fwd(a_in, conv_w, vec("conv_dw_b"), vec("conv_ln_g"), vec("conv_ln_b"))
        attn_out, lse = _attn_fwd(qkv, bias)
        bs_t = sp["gmlp_b_s"][l].T
        gm_out = _gmlp_fwd(c_in, vec("gmlp_ln_g"), vec("gmlp_ln_b"), sp["gmlp_w_s"][l], bs_t)
        if l == 0:
            wb.update(late_weights(gm_out))
        cat, z1, x1b = _outproj_ln_fwd(conv_out, attn_out, gm_out, wb["w_out"][l], vec("b_out"), xl,
                                           vec("ln1_g"), vec("ln1_b"))
        fconv_w = _pad_rows(sp["ffn_conv_w"][l], 8)
        hfb, fhc, act = _ffn_up_gate_fwd(x1b, wb["ffn_w_up"][l], vec("ffn_b_up"), fconv_w, vec("ffn_conv_b"))
        down = (act, wb["ffn_w_down"][l], vec("ffn_b_down"), z1, vec("ln1_g"), vec("ln1_b"), vec("ln2_g"),
                vec("ln2_b"))
        z2, x2 = _ffn_down_ln_fwd(*down) if l < DEPTH - 1 else (None, None)
        saved.append(dict(x=xl, a_in=a_in, qkv=qkv, c_in=c_in, hc=hc, attn_out=attn_out, lse=lse, cat=cat, z1=z1,
                          x1b=x1b, hfb=hfb, fhc=fhc, act=act, z2=z2, conv_w=conv_w, fconv_w=fconv_w, bs_t=bs_t))
        xl = x2

    grads = {}
    per_layer = {k: [None] * DEPTH for k in (
        "b_in", "conv_dw_w", "conv_dw_b", "conv_ln_g", "conv_ln_b", "gmlp_ln_g", "gmlp_ln_b", "gmlp_w_s",
        "gmlp_b_s", "b_out", "ln1_g", "ln1_b", "ffn_b_up", "ffn_conv_w", "ffn_conv_b", "ffn_b_down", "ln2_g", "ln2_b")}
    dbias_all = []
    dz2, dz2b, loss_part, dg2, db2 = _ffn_down_ln_loss(*down, target)
    loss = jnp.sum(loss_part)
    grad_x = None
    tok = jnp.zeros((), f32)
    for l in reversed(range(DEPTH)):
        sv = saved[l]
        vec = lambda name: sp[name][l][None, :] + tok
        per_layer["ln2_g"][l] = dg2[0]
        per_layer["ln2_b"][l] = db2[0]
        dw_down, db_down = _wgrad(sv["act"], dz2b, 512, "ffn_down_wgrad")
        tok = sink.put("ffn_w_down", l, dw_down, tok)
        per_layer["ffn_b_down"][l] = db_down[0]
        dhf, dfcw, dfcb = _ffn_down_gate_bwd(dz2b, wb["ffn_w_down"][l], sv["hfb"], sv["fhc"], sv["fconv_w"])
        per_layer["ffn_conv_w"][l] = dfcw[:FFN_CONV_WIDTH]
        per_layer["ffn_conv_b"][l] = dfcb[0]
        dw_up, db_up = _wgrad(sv["x1b"], dhf, 1408, "ffn_up_wgrad")
        tok = sink.put("ffn_w_up", l, dw_up, tok)
        per_layer["ffn_b_up"][l] = db_up[0]
        dz1, dz1b, dg1, db1 = _dgrad_ln_bwd(dhf, wb["ffn_w_up"][l], dz2, sv["z1"], vec("ln1_g"), "ffn_up_dgrad_ln")
        per_layer["ln1_g"][l] = dg1[0]
        per_layer["ln1_b"][l] = db1[0]
        dw_out, db_out = _wgrad(sv["cat"], dz1b, D_MODEL, "outproj_wgrad")
        tok = sink.put("w_out", l, dw_out, tok)
        per_layer["b_out"][l] = db_out[0]
        dco, dgo, *perm = _outproj_dgrad(dz1b, wb["w_out"][l], sv["attn_out"], sv["lse"])
        d_outs, stats = perm[:len(DILATIONS)], perm[len(DILATIONS):]
        d_c, dglg, dglb, dws, dbs = _gmlp_bwd(sv["c_in"], dgo, vec("gmlp_ln_g"), vec("gmlp_ln_b"), sp["gmlp_w_s"][l],
                                              sv["bs_t"])
        per_layer["gmlp_ln_g"][l] = dglg[0]
        per_layer["gmlp_ln_b"][l] = dglb[0]
        per_layer["gmlp_w_s"][l] = dws
        per_layer["gmlp_b_s"][l] = dbs[:, :GMLP_GROUPS].T
        dqkvs = []
        for p, d in enumerate(DILATIONS):
            dqkv, dbias = _attn_bwd_pattern(sv["qkv"][p], d_outs[p], stats[p], bias[p], d)
            dqkvs.append(dqkv)
            dbias_all.append(dbias)
        d_a, dcw, dcb, dclg, dclb = _conv_bwd(sv["a_in"], sv["hc"], dco, sv["conv_w"], vec("conv_ln_g"),
                                              vec("conv_ln_b"))
        per_layer["conv_dw_w"][l] = dcw[:CONV_WIDTH]
        per_layer["conv_dw_b"][l] = dcb[0]
        per_layer["conv_ln_g"][l] = dclg[0]
        per_layer["conv_ln_b"][l] = dclb[0]
        dh = _attn_bwd_merge(d_a, dqkvs, d_c)
        dw_in, db_in = _wgrad(sv["x"], dh, IN_CH, "inproj_wgrad")
        tok = sink.put("w_in", l, dw_in, tok)
        per_layer["b_in"][l] = db_in[0]
        if l > 0:
            pv = saved[l - 1]
            dz2, dz2b, dg2, db2 = _dgrad_ln_bwd(dh, wb["w_in"][l], dz1, pv["z2"], sp["ln2_g"][l - 1][None, :] + tok,
                                                "inproj_dgrad_ln")
        else:
            grad_x = _dgrad_ln_bwd(dh, wb["w_in"][l], dz1, None, None, "inproj_dgrad")
    for k, v in per_layer.items():
        grads[k] = jnp.stack(v)
    dbias_cat = jnp.stack(dbias_all)
    bk_cat = jnp.concatenate([buckets] * DEPTH, axis=0)
    grads["rel_bias_table"] = _bias_table_grad(dbias_cat, bk_cat)[:, :ATTN_HEADS]
    return loss, grad_x, grads, sink.finish(grad_x)


N_CHIPS = 4
BIG = {"w_in": (D_MODEL, IN_CH, 1), "w_out": (D_MODEL, D_MODEL, 0),
       "ffn_w_up": (D_MODEL, 2 * D_FF, 1), "ffn_w_down": (D_FF, D_MODEL, 0)}
SMALL = ("b_in", "conv_dw_w", "conv_dw_b", "conv_ln_g", "conv_ln_b", "rel_bias_table", "gmlp_ln_g", "gmlp_ln_b",
         "gmlp_w_s", "gmlp_b_s", "b_out", "ln1_g", "ln1_b", "ffn_b_up", "ffn_conv_w", "ffn_conv_b", "ffn_b_down",
         "ln2_g", "ln2_b")
SMALL_SHARDED = ("conv_dw_w", "ffn_conv_w")
WEIGHTS = ("w_in", "b_in", "conv_dw_w", "conv_dw_b", "conv_ln_g", "conv_ln_b", "rel_bias_table", "gmlp_ln_g",
           "gmlp_ln_b", "gmlp_w_s", "gmlp_b_s", "w_out", "b_out", "ln1_g", "ln1_b", "ffn_w_up", "ffn_b_up",
           "ffn_conv_w", "ffn_conv_b", "ffn_w_down", "ffn_b_down", "ln2_g", "ln2_b")
ANY = pl.BlockSpec(memory_space=pl.ANY)


def _position():
    return lax.axis_index("x"), lax.axis_index("y"), lax.axis_index("c")


def _other_chips(x, y):
    return [(1 - x, y), (x, 1 - y), (1 - x, 1 - y)]


def _cast_bf16(a):
    R, C = a.shape
    T = 128

    def body(a_ref, o_ref):
        o_ref[...] = a_ref[...].astype(bf16)

    return pl.pallas_call(
        body, grid=(R // T,), name="cast_bf16", out_shape=jax.ShapeDtypeStruct((R, C), bf16),
        in_specs=[pl.BlockSpec((T, C), lambda i: (i, 0))], out_specs=pl.BlockSpec((T, C), lambda i: (i, 0)),
        compiler_params=_cp(("parallel",), 16),
    )(a)


def _chip_slot(ref, name, l, p):
    K, N, ax = BIG[name]
    if ax == 1:
        sz = N // N_CHIPS
        return ref.at[l, :, pl.ds(pl.multiple_of(p * sz, 128), sz)]
    sz = K // N_CHIPS
    return ref.at[l, pl.ds(pl.multiple_of(p * sz, 16), sz), :]


def _gather_weights(shards, conv_w, fconv_w):
    names = list(shards)
    n_big = len(names)
    n_t = n_big + 2
    n_chip = 3 * n_t
    n_pass = 3 * n_big

    def body(*refs):
        ins = refs[:n_t]
        outs = refs[n_t:2 * n_t]
        send_sems, recv_sems, pass_send, pass_recv, local_sems = refs[2 * n_t:]
        x, y, c = _position()
        me = 2 * x + y
        chips = _other_chips(x, y)

        def src(t):
            return ins[t].at[c] if t < n_big else ins[t]

        def slot(t, l, p):
            return _chip_slot(outs[t], names[t], l, p) if t < n_big else outs[t].at[p]

        locs, cps = [], []
        for t in range(n_t):
            for l in (range(DEPTH) if t < n_big else (0,)):
                loc = pltpu.make_async_copy(ins[t].at[l] if t < n_big else ins[t], slot(t, l, me),
                                            local_sems.at[DEPTH * t + l])
                loc.start()
                locs.append(loc)
            for k, (px, py) in enumerate(chips):
                cp = pltpu.make_async_remote_copy(
                    src_ref=src(t), dst_ref=slot(t, c, me), send_sem=send_sems.at[3 * t + k],
                    recv_sem=recv_sems.at[3 * t + k], device_id=(px, py, c), device_id_type=MESH_ID)
                cp.start()
                cps.append(cp)
        for t in range(n_t):
            for k, (px, py) in enumerate(chips):
                landed = slot(t, c, 2 * px + py)
                pltpu.make_async_remote_copy(
                    src_ref=src(t), dst_ref=landed, send_sem=send_sems.at[3 * t + k],
                    recv_sem=recv_sems.at[3 * t + k], device_id=(px, py, c), device_id_type=MESH_ID).wait_recv()
                if t < n_big:
                    cp = pltpu.make_async_remote_copy(
                        src_ref=landed, dst_ref=landed, send_sem=pass_send.at[3 * t + k],
                        recv_sem=pass_recv.at[3 * t + k], device_id=(x, y, 1 - c), device_id_type=MESH_ID)
                    cp.start()
                    cps.append(cp)
        for t in range(n_big):
            for k, (px, py) in enumerate(chips):
                from_sibling = slot(t, 1 - c, 2 * px + py)
                pltpu.make_async_remote_copy(
                    src_ref=from_sibling, dst_ref=from_sibling, send_sem=pass_send.at[3 * t + k],
                    recv_sem=pass_recv.at[3 * t + k], device_id=(x, y, 1 - c), device_id_type=MESH_ID).wait_recv()
        for cp in cps:
            cp.wait_send()
        for loc in locs:
            loc.wait()

    ins = [shards[n] for n in names] + [conv_w, fconv_w]
    out_shape = [jax.ShapeDtypeStruct((DEPTH, BIG[n][0], BIG[n][1]), bf16) for n in names]
    out_shape += [jax.ShapeDtypeStruct((N_CHIPS,) + conv_w.shape, f32), jax.ShapeDtypeStruct((N_CHIPS,) + fconv_w.shape, f32)]
    outs = pl.pallas_call(
        body, name="gather_weights", out_shape=tuple(out_shape), in_specs=[ANY] * n_t, out_specs=tuple([ANY] * n_t),
        scratch_shapes=[pltpu.SemaphoreType.DMA((n_chip,)), pltpu.SemaphoreType.DMA((n_chip,)),
                        pltpu.SemaphoreType.DMA((n_pass,)), pltpu.SemaphoreType.DMA((n_pass,)),
                        pltpu.SemaphoreType.DMA((DEPTH * n_t,))],
    )(*ins)
    return dict(zip(names, outs[:n_big])), outs[-2], outs[-1]


LATE_WEIGHTS = ("w_out", "ffn_w_up", "ffn_w_down")
HBM = pl.BlockSpec(memory_space=pltpu.HBM)
SEM = pl.BlockSpec(memory_space=pltpu.SEMAPHORE)


def _cast_into_full(shard, name, chip_arr):
    K, N, ax = BIG[name]
    k, n = _shard_shape(name)
    T = 64
    nrt = k // T

    def body(p_ref, a_ref, o_ref):
        o_ref[...] = a_ref[...].astype(bf16)

    if ax == 1:
        out_spec = pl.BlockSpec((None, T, n), lambda l, i, p: (l, i, p[0]))
    else:
        out_spec = pl.BlockSpec((None, T, n), lambda l, i, p: (l, p[0] * nrt + i, 0))
    return pl.pallas_call(
        body, name="cast_into_full", out_shape=jax.ShapeDtypeStruct((DEPTH, K, N), bf16),
        grid_spec=pltpu.PrefetchScalarGridSpec(
            num_scalar_prefetch=1, grid=(DEPTH, nrt),
            in_specs=[pl.BlockSpec((None, T, n), lambda l, i, p: (l, i, 0))], out_specs=out_spec),
        compiler_params=_cp(("parallel", "parallel"), 16),
    )(chip_arr, shard)


def _late_copies(refs, send_sems, recv_sems):
    x, y, c = _position()
    me = 2 * x + y
    idx = 0
    for ref, name in zip(refs, LATE_WEIGHTS):
        for l in range(DEPTH):
            for px, py in _other_chips(x, y):
                def copy(p, ref=ref, name=name, l=l, px=px, py=py, idx=idx):
                    part = _chip_slot(ref, name, l, p)
                    return pltpu.make_async_remote_copy(
                        src_ref=part, dst_ref=part, send_sem=send_sems.at[idx], recv_sem=recv_sems.at[idx],
                        device_id=(px, py, c), device_id_type=MESH_ID)
                yield copy(me), copy(2 * px + py)
                idx += 1


N_LATE_COPIES = 3 * DEPTH * len(LATE_WEIGHTS)


def _gather_start(fulls, after):
    n = len(fulls)

    def body(*refs):
        ins = refs[:n]
        send_sems, recv_sems = refs[n + 1:n + 3]
        token = refs[-1]
        for sent, _ in _late_copies(ins, send_sems, recv_sems):
            sent.start()
        token[...] = jnp.zeros_like(token)

    outs = pl.pallas_call(
        body, name="gather_start",
        out_shape=(pltpu.SemaphoreType.DMA((N_LATE_COPIES,)), pltpu.SemaphoreType.DMA((N_LATE_COPIES,)))
        + tuple(pltpu.HBM(f.shape, f.dtype) for f in fulls) + (jax.ShapeDtypeStruct((SUBLANES, LANES), f32),),
        in_specs=(HBM,) * n + (ANY,),
        out_specs=(SEM, SEM) + (HBM,) * n + (pl.BlockSpec(memory_space=pltpu.VMEM),),
        input_output_aliases={t: 2 + t for t in range(n)},
        compiler_params=pltpu.CompilerParams(has_side_effects=pltpu.SideEffectType.DATAFLOW_SIDE_EFFECTING),
    )(*[pltpu.with_memory_space_constraint(f, pltpu.HBM) for f in fulls], after)
    return outs[0], outs[1], outs[2:2 + n], outs[-1]


def _gather_wait(send_sems, recv_sems, fulls, after):
    n = len(fulls)

    def body(*refs):
        ins = refs[:n]
        send_ref, recv_ref = refs[n:n + 2]
        for sent, landed in _late_copies(ins, send_ref, recv_ref):
            sent.wait_send()
            landed.wait_recv()

    return pl.pallas_call(
        body, name="gather_wait", out_shape=tuple(pltpu.HBM(f.shape, f.dtype) for f in fulls),
        in_specs=(HBM,) * n + (SEM, SEM, ANY), out_specs=(HBM,) * n,
        input_output_aliases={t: t for t in range(n)},
        compiler_params=pltpu.CompilerParams(has_side_effects=pltpu.SideEffectType.DATAFLOW_SIDE_EFFECTING),
    )(*fulls, send_sems, recv_sems, after)


def _half(ref, name, c):
    K, N, ax = BIG[name]
    if ax == 1:
        return ref.at[pl.ds(pl.multiple_of(c * (K // 2), 8), K // 2), :]
    return ref.at[:, pl.ds(pl.multiple_of(c * (N // 2), 128), N // 2)]


def _half_shape(name):
    K, N, ax = BIG[name]
    return (K // 2, N) if ax == 1 else (K, N // 2)


def _shard_of_half(ref, name, q):
    K, N, ax = BIG[name]
    if ax == 1:
        sz = N // N_CHIPS
        return ref.at[:, pl.ds(pl.multiple_of(q * sz, 128), sz)]
    sz = K // N_CHIPS
    return ref.at[pl.ds(pl.multiple_of(q * sz, 16), sz), :]


def _shard_half_shape(name):
    K, N, ax = BIG[name]
    return (K // 2, N // N_CHIPS) if ax == 1 else (K // N_CHIPS, N // 2)


def _shard_shape(name):
    K, N, ax = BIG[name]
    return (K, N // N_CHIPS) if ax == 1 else (K // N_CHIPS, N)


def _pair_copies(names, srcs, lands, send_sems, recv_sems):
    x, y, c = _position()
    for idx, (name, src, land) in enumerate(zip(names, srcs, lands)):
        yield pltpu.make_async_remote_copy(
            src_ref=_half(src, name, 1 - c), dst_ref=land, send_sem=send_sems.at[idx], recv_sem=recv_sems.at[idx],
            device_id=(x, y, 1 - c), device_id_type=MESH_ID)


def _pair_exchange_start(tag, tensors):
    names = [n for n, _ in tensors]
    n = len(tensors)
    lands = [lax.empty(_half_shape(nm), f32) for nm in names]

    def body(*refs):
        for cp in _pair_copies(names, refs[:n], refs[n:2 * n], refs[2 * n], refs[2 * n + 1]):
            cp.start()
        refs[-1][...] = jnp.zeros_like(refs[-1])

    args = [g for _, g in tensors] + lands
    outs = pl.pallas_call(
        body, name="grad_pair_start_" + tag,
        out_shape=(pltpu.SemaphoreType.DMA((n,)), pltpu.SemaphoreType.DMA((n,)))
        + tuple(pltpu.HBM(a.shape, a.dtype) for a in args) + (jax.ShapeDtypeStruct((SUBLANES, LANES), f32),),
        in_specs=(HBM,) * (2 * n), out_specs=(SEM, SEM) + (HBM,) * (2 * n) + (pl.BlockSpec(memory_space=pltpu.VMEM),),
        input_output_aliases={t: 2 + t for t in range(2 * n)},
        compiler_params=pltpu.CompilerParams(has_side_effects=pltpu.SideEffectType.DATAFLOW_SIDE_EFFECTING),
    )(*[pltpu.with_memory_space_constraint(a, pltpu.HBM) for a in args])
    return (tag, names, outs[0], outs[1], outs[2:2 + 2 * n]), outs[-1]


def _pair_exchange_wait(state, after):
    tag, names, send_sems, recv_sems, bufs = state
    n = len(names)

    def body(*refs):
        for cp in _pair_copies(names, refs[:n], refs[n:2 * n], refs[2 * n], refs[2 * n + 1]):
            cp.wait_send()
            cp.wait_recv()

    outs = pl.pallas_call(
        body, name="grad_pair_wait_" + tag, out_shape=tuple(pltpu.HBM(a.shape, a.dtype) for a in bufs),
        in_specs=(HBM,) * (2 * n) + (SEM, SEM, ANY), out_specs=(HBM,) * (2 * n),
        input_output_aliases={t: t for t in range(2 * n)},
        compiler_params=pltpu.CompilerParams(has_side_effects=pltpu.SideEffectType.DATAFLOW_SIDE_EFFECTING),
    )(*bufs, send_sems, recv_sems, after)
    return list(zip(names, outs[:n], outs[n:]))


def _pair_add(g, rcv, name, c_arr):
    K, N, ax = BIG[name]
    hr, hc = _half_shape(name)
    T = 128
    nrt = hr // T

    def body(c_ref, g_ref, r_ref, o_ref):
        o_ref[...] = (g_ref[...] + r_ref[...]).astype(bf16)

    if ax == 1:
        g_spec = pl.BlockSpec((T, hc), lambda i, c: (c[0] * nrt + i, 0))
    else:
        g_spec = pl.BlockSpec((T, hc), lambda i, c: (i, c[0]))
    plain = pl.BlockSpec((T, hc), lambda i, c: (i, 0))
    return pl.pallas_call(
        body, name="grad_pair_add", out_shape=jax.ShapeDtypeStruct((hr, hc), bf16),
        grid_spec=pltpu.PrefetchScalarGridSpec(num_scalar_prefetch=1, grid=(nrt,), in_specs=[g_spec, plain],
                                               out_specs=plain),
        compiler_params=_cp(("parallel",), 32),
    )(c_arr, g, rcv)


def _chip_copies(names, srcs, lands, send_sems, recv_sems):
    x, y, c = _position()
    me = 2 * x + y
    idx = 0
    for name, src, land in zip(names, srcs, lands):
        for px, py in _other_chips(x, y):
            def copy(q, row, name=name, src=src, land=land, px=px, py=py, idx=idx):
                return pltpu.make_async_remote_copy(
                    src_ref=_shard_of_half(src, name, q), dst_ref=land.at[row], send_sem=send_sems.at[idx],
                    recv_sem=recv_sems.at[idx], device_id=(px, py, c), device_id_type=MESH_ID)
            yield copy(2 * px + py, me), copy(me, 2 * px + py)
            idx += 1


def _chip_exchange_start(tag, tensors):
    names = [n for n, _ in tensors]
    n = len(tensors)
    lands = [lax.empty((N_CHIPS,) + _shard_half_shape(nm), g.dtype) for nm, g in tensors]

    def body(*refs):
        send_sems, recv_sems = refs[2 * n:2 * n + 2]
        for sent, _ in _chip_copies(names, refs[:n], refs[n:2 * n], send_sems, recv_sems):
            sent.start()
        refs[-1][...] = jnp.zeros_like(refs[-1])

    args = [g for _, g in tensors] + lands
    outs = pl.pallas_call(
        body, name="grad_chip_start_" + tag,
        out_shape=(pltpu.SemaphoreType.DMA((3 * n,)), pltpu.SemaphoreType.DMA((3 * n,)))
        + tuple(pltpu.HBM(a.shape, a.dtype) for a in args) + (jax.ShapeDtypeStruct((SUBLANES, LANES), f32),),
        in_specs=(HBM,) * (2 * n), out_specs=(SEM, SEM) + (HBM,) * (2 * n) + (pl.BlockSpec(memory_space=pltpu.VMEM),),
        input_output_aliases={t: 2 + t for t in range(2 * n)},
        compiler_params=pltpu.CompilerParams(has_side_effects=pltpu.SideEffectType.DATAFLOW_SIDE_EFFECTING),
    )(*[pltpu.with_memory_space_constraint(a, pltpu.HBM) for a in args])
    return (tag, names, outs[0], outs[1], outs[2:2 + 2 * n]), outs[-1]


def _chip_exchange_wait(state, after):
    tag, names, send_sems, recv_sems, bufs = state
    n = len(names)

    def body(*refs):
        for sent, landed in _chip_copies(names, refs[:n], refs[n:2 * n], refs[2 * n], refs[2 * n + 1]):
            sent.wait_send()
            landed.wait_recv()

    outs = pl.pallas_call(
        body, name="grad_chip_wait_" + tag, out_shape=tuple(pltpu.HBM(a.shape, a.dtype) for a in bufs),
        in_specs=(HBM,) * (2 * n) + (SEM, SEM, ANY), out_specs=(HBM,) * (2 * n),
        input_output_aliases={t: t for t in range(2 * n)},
        compiler_params=pltpu.CompilerParams(has_side_effects=pltpu.SideEffectType.DATAFLOW_SIDE_EFFECTING),
    )(*bufs, send_sems, recv_sems, after)
    return list(zip(names, outs[:n], outs[n:]))


def _sum_chips(name, half, land, chip_arr):
    K, N, ax = BIG[name]
    R, C = _shard_half_shape(name)
    T = 64
    nrt = R // T

    def body(p_ref, own_ref, land_ref, o_ref):
        parts = [jnp.where(p_ref[0] == q, own_ref[...], land_ref[q]).astype(f32) for q in range(N_CHIPS)]
        o_ref[...] = ((parts[0] + parts[1]) + parts[2]) + parts[3]

    if ax == 1:
        own_spec = pl.BlockSpec((T, C), lambda i, p: (i, p[0]))
    else:
        own_spec = pl.BlockSpec((T, C), lambda i, p: (p[0] * nrt + i, 0))
    return pl.pallas_call(
        body, name="grad_sum_chips", out_shape=jax.ShapeDtypeStruct((R, C), f32),
        grid_spec=pltpu.PrefetchScalarGridSpec(
            num_scalar_prefetch=1, grid=(nrt,),
            in_specs=[own_spec, pl.BlockSpec((N_CHIPS, T, C), lambda i, p: (0, i, 0))],
            out_specs=pl.BlockSpec((T, C), lambda i, p: (i, 0))),
        compiler_params=_cp(("parallel",), 32),
    )(chip_arr, half, land)


def _pair_swap(halves):
    n_t = len(halves)

    def body(*refs):
        ins = refs[:n_t]
        outs = refs[n_t:2 * n_t]
        send_sems, recv_sems = refs[2 * n_t:]
        x, y, c = _position()
        cps = []
        for t in range(n_t):
            cp = pltpu.make_async_remote_copy(
                src_ref=ins[t], dst_ref=outs[t], send_sem=send_sems.at[t], recv_sem=recv_sems.at[t],
                device_id=(x, y, 1 - c), device_id_type=MESH_ID)
            cp.start()
            cps.append(cp)
        for cp in cps:
            cp.wait()

    return pl.pallas_call(
        body, name="grad_pair_swap", out_shape=tuple(jax.ShapeDtypeStruct(h.shape, h.dtype) for h in halves),
        in_specs=[ANY] * n_t, out_specs=tuple([ANY] * n_t),
        scratch_shapes=[pltpu.SemaphoreType.DMA((n_t,)), pltpu.SemaphoreType.DMA((n_t,))],
    )(*halves)


def _adamw_halves(own, other, w, m, v, name, l, c_arr, prev):
    K, N, ax = BIG[name]
    R, C = _shard_shape(name)
    hr, hc = _shard_half_shape(name)
    T = 64
    nrt = hr // T
    c1 = 1.0 / (1.0 - ADAM_B1 ** ADAM_STEP)
    c2 = 1.0 / (1.0 - ADAM_B2 ** ADAM_STEP)

    def body(c_ref, own_ref, oth_ref, w_ref, m_ref, v_ref, *rest):
        g_ref, d_ref, nm_ref, nv_ref = rest[-4:]
        gg = jnp.where(pl.program_id(0) == c_ref[0], own_ref[...], oth_ref[...])
        nm = ADAM_B1 * m_ref[...] + (1.0 - ADAM_B1) * gg
        nv = ADAM_B2 * v_ref[...] + (1.0 - ADAM_B2) * (gg * gg)
        g_ref[...] = gg
        nm_ref[...] = nm
        nv_ref[...] = nv
        d_ref[...] = -ADAM_LR * ((nm * c1) / (jnp.sqrt(nv * c2) + ADAM_EPS) + ADAM_WD * w_ref[...])

    half = pl.BlockSpec((T, hc), lambda h, i, c: (i, 0))
    if ax == 1:
        full = pl.BlockSpec((None, T, hc), lambda h, i, c: (l, h * nrt + i, 0))
    else:
        full = pl.BlockSpec((None, T, hc), lambda h, i, c: (l, i, h))
    sd = jax.ShapeDtypeStruct((DEPTH, R, C), f32)
    args = [c_arr, own, other, w, m, v]
    in_specs = [half, half, full, full, full]
    aliases = {}
    if prev is not None:
        args += list(prev)
        in_specs += [ANY] * 4
        aliases = {6 + k: k for k in range(4)}
    return pl.pallas_call(
        body, name="adamw_" + name, out_shape=(sd, sd, sd, sd),
        grid_spec=pltpu.PrefetchScalarGridSpec(num_scalar_prefetch=1, grid=(2, nrt), in_specs=in_specs,
                                               out_specs=(full, full, full, full)),
        input_output_aliases=aliases,
        compiler_params=_cp(("arbitrary", "arbitrary"), 32),
    )(*args)


class _GradExchange:
    GROUPS = (("l1", tuple((n, DEPTH - 1) for n in BIG)),
              ("l0_ffn", (("ffn_w_down", 0), ("ffn_w_up", 0))),
              ("l0_mix", (("w_out", 0), ("w_in", 0))))

    def __init__(self):
        self.c_arr = jnp.reshape(lax.axis_index("c"), (1,)).astype(jnp.int32)
        self.chip_arr = jnp.reshape(2 * lax.axis_index("x") + lax.axis_index("y"), (1,)).astype(jnp.int32)
        self.grads = {}
        self.pair_started = {}
        self.chip_started = {}

    def _advance(self, after, tok):
        for tag, _ in self.GROUPS:
            if tag not in self.pair_started or tag in self.chip_started:
                continue
            arrived = _pair_exchange_wait(self.pair_started[tag], after)
            pair = [(n, _pair_add(g, r, n, self.c_arr)) for n, g, r in arrived]
            self.chip_started[tag], token = _chip_exchange_start(tag, pair)
            tok = tok + token[0, 0]
        return tok

    def put(self, name, layer, g, tok):
        self.grads[(name, layer)] = g
        tok = self._advance(g, tok)
        for tag, keys in self.GROUPS:
            if tag in self.pair_started or not all(k in self.grads for k in keys):
                continue
            self.pair_started[tag], token = _pair_exchange_start(tag, [(n, self.grads[(n, l)]) for n, l in keys])
            tok = tok + token[0, 0]
        return tok

    def finish(self, after):
        self._advance(after, jnp.zeros((), f32))
        keys, own = [], []
        for tag, group in self.GROUPS:
            landed = _chip_exchange_wait(self.chip_started[tag], after)
            own += [_sum_chips(n, half, land, self.chip_arr) for n, half, land in landed]
            keys += list(group)
        other = _pair_swap(own)
        return dict(zip(keys, zip(own, other)))


def _small_allreduce(buf):
    R = buf.shape[0]

    def body(in_ref, out_ref, sibling, slots, send_sems, recv_sems):
        x, y, c = _position()
        me = 2 * x + y
        swap = pltpu.make_async_remote_copy(
            src_ref=in_ref, dst_ref=sibling, send_sem=send_sems.at[0], recv_sem=recv_sems.at[0],
            device_id=(x, y, 1 - c), device_id_type=MESH_ID)
        swap.start()
        swap.wait()
        slots[me] = in_ref[...] + sibling[...]
        cps = []
        for k, (px, py) in enumerate(_other_chips(x, y)):
            cp = pltpu.make_async_remote_copy(
                src_ref=slots.at[me], dst_ref=slots.at[me], send_sem=send_sems.at[1 + k], recv_sem=recv_sems.at[1 + k],
                device_id=(px, py, c), device_id_type=MESH_ID)
            cp.start()
            cps.append(cp)
        for k, (px, py) in enumerate(_other_chips(x, y)):
            pltpu.make_async_remote_copy(
                src_ref=slots.at[me], dst_ref=slots.at[2 * px + py], send_sem=send_sems.at[1 + k],
                recv_sem=recv_sems.at[1 + k], device_id=(px, py, c), device_id_type=MESH_ID).wait_recv()
        for cp in cps:
            cp.wait_send()
        out_ref[...] = ((slots[0] + slots[1]) + slots[2]) + slots[3]

    vm = pl.BlockSpec(memory_space=pltpu.VMEM)
    return pl.pallas_call(
        body, name="small_allreduce", out_shape=jax.ShapeDtypeStruct((R, 128), f32), in_specs=[vm], out_specs=vm,
        scratch_shapes=[pltpu.VMEM((R, 128), f32), pltpu.VMEM((N_CHIPS, R, 128), f32),
                        pltpu.SemaphoreType.DMA((N_CHIPS,)), pltpu.SemaphoreType.DMA((N_CHIPS,))],
        compiler_params=pltpu.CompilerParams(vmem_limit_bytes=40 * MIB),
    )(buf)


PACK_UNIT = 1024


def _pack(arrs):
    parts = []
    for a in arrs:
        flat = a.reshape(-1)
        n = -(-flat.shape[0] // PACK_UNIT) * PACK_UNIT
        parts.append(jnp.pad(flat, (0, n - flat.shape[0])))
    return jnp.concatenate(parts).reshape(-1, 128)


def _unpack(buf, shapes):
    flat = buf.reshape(-1)
    out, off = [], 0
    for shp in shapes:
        n = int(np.prod(shp))
        out.append(flat[off:off + n].reshape(shp))
        off += -(-n // PACK_UNIT) * PACK_UNIT
    return out


def kernel(x, w_in, b_in, conv_dw_w, conv_dw_b, conv_ln_g, conv_ln_b, rel_bias_table, gmlp_ln_g, gmlp_ln_b, gmlp_w_s, gmlp_b_s, w_out, b_out, ln1_g, ln1_b, ffn_w_up, ffn_b_up, ffn_conv_w, ffn_conv_b, ffn_w_down, ffn_b_down, ln2_g, ln2_b, loss_target, m_w_in, m_b_in, m_conv_dw_w, m_conv_dw_b, m_conv_ln_g, m_conv_ln_b, m_rel_bias_table, m_gmlp_ln_g, m_gmlp_ln_b, m_gmlp_w_s, m_gmlp_b_s, m_w_out, m_b_out, m_ln1_g, m_ln1_b, m_ffn_w_up, m_ffn_b_up, m_ffn_conv_w, m_ffn_conv_b, m_ffn_w_down, m_ffn_b_down, m_ln2_g, m_ln2_b, v_w_in, v_b_in, v_conv_dw_w, v_conv_dw_b, v_conv_ln_g, v_conv_ln_b, v_rel_bias_table, v_gmlp_ln_g, v_gmlp_ln_b, v_gmlp_w_s, v_gmlp_b_s, v_w_out, v_b_out, v_ln1_g, v_ln1_b, v_ffn_w_up, v_ffn_b_up, v_ffn_conv_w, v_ffn_conv_b, v_ffn_w_down, v_ffn_b_down, v_ln2_g, v_ln2_b):
    w = dict(w_in=w_in, b_in=b_in, conv_dw_w=conv_dw_w, conv_dw_b=conv_dw_b, conv_ln_g=conv_ln_g, conv_ln_b=conv_ln_b,
             rel_bias_table=rel_bias_table, gmlp_ln_g=gmlp_ln_g, gmlp_ln_b=gmlp_ln_b, gmlp_w_s=gmlp_w_s,
             gmlp_b_s=gmlp_b_s, w_out=w_out, b_out=b_out, ln1_g=ln1_g, ln1_b=ln1_b, ffn_w_up=ffn_w_up,
             ffn_b_up=ffn_b_up, ffn_conv_w=ffn_conv_w, ffn_conv_b=ffn_conv_b, ffn_w_down=ffn_w_down,
             ffn_b_down=ffn_b_down, ln2_g=ln2_g, ln2_b=ln2_b)
    m = dict(w_in=m_w_in, b_in=m_b_in, conv_dw_w=m_conv_dw_w, conv_dw_b=m_conv_dw_b, conv_ln_g=m_conv_ln_g,
             conv_ln_b=m_conv_ln_b, rel_bias_table=m_rel_bias_table, gmlp_ln_g=m_gmlp_ln_g, gmlp_ln_b=m_gmlp_ln_b,
             gmlp_w_s=m_gmlp_w_s, gmlp_b_s=m_gmlp_b_s, w_out=m_w_out, b_out=m_b_out, ln1_g=m_ln1_g, ln1_b=m_ln1_b,
             ffn_w_up=m_ffn_w_up, ffn_b_up=m_ffn_b_up, ffn_conv_w=m_ffn_conv_w, ffn_conv_b=m_ffn_conv_b,
             ffn_w_down=m_ffn_w_down, ffn_b_down=m_ffn_b_down, ln2_g=m_ln2_g, ln2_b=m_ln2_b)
    v = dict(w_in=v_w_in, b_in=v_b_in, conv_dw_w=v_conv_dw_w, conv_dw_b=v_conv_dw_b, conv_ln_g=v_conv_ln_g,
             conv_ln_b=v_conv_ln_b, rel_bias_table=v_rel_bias_table, gmlp_ln_g=v_gmlp_ln_g, gmlp_ln_b=v_gmlp_ln_b,
             gmlp_w_s=v_gmlp_w_s, gmlp_b_s=v_gmlp_b_s, w_out=v_w_out, b_out=v_b_out, ln1_g=v_ln1_g, ln1_b=v_ln1_b,
             ffn_w_up=v_ffn_w_up, ffn_b_up=v_ffn_b_up, ffn_conv_w=v_ffn_conv_w, ffn_conv_b=v_ffn_conv_b,
             ffn_w_down=v_ffn_w_down, ffn_b_down=v_ffn_b_down, ln2_g=v_ln2_g, ln2_b=v_ln2_b)

    chip_arr = jnp.reshape(2 * lax.axis_index("x") + lax.axis_index("y"), (1,)).astype(jnp.int32)
    shards = {"w_in": _cast_bf16(w_in.reshape(-1, w_in.shape[-1])).reshape(w_in.shape)}
    wb, conv_stack, fconv_stack = _gather_weights(shards, conv_dw_w, ffn_conv_w)
    send_sems, recv_sems, in_flight, token = _gather_start(
        [_cast_into_full(w[n], n, chip_arr) for n in LATE_WEIGHTS], conv_stack)
    sp = {n: w[n] for n in SMALL}
    sp["conv_dw_w"] = jnp.moveaxis(conv_stack, 0, 2).reshape(DEPTH, CONV_WIDTH, CONV_CH)
    sp["ffn_conv_w"] = jnp.moveaxis(fconv_stack, 0, 2).reshape(DEPTH, FFN_CONV_WIDTH, 2 * D_FF)
    sp["b_in"] = sp["b_in"] + token[0, 0]

    def late_weights(after):
        return dict(zip(LATE_WEIGHTS, _gather_wait(send_sems, recv_sems, in_flight, after)))

    sink = _GradExchange()
    loss_local, grad_x, grads, big = _local_step(x[0], loss_target[0], wb, late_weights, sp, sink)

    small_shapes = [(1,)] + [grads[n].shape for n in SMALL]
    summed = _unpack(_small_allreduce(_pack([loss_local.reshape(1)] + [grads[n] for n in SMALL])), small_shapes)
    loss = summed[0].reshape(())
    small = dict(zip(SMALL, summed[1:]))
    chip = 2 * lax.axis_index("x") + lax.axis_index("y")
    for n in SMALL_SHARDED:
        width = w[n].shape[-1]
        small[n] = lax.dynamic_slice_in_dim(small[n], chip * width, width, axis=2)

    g_out, d_out, m_out, v_out = {}, {}, {}, {}
    for n in BIG:
        outs = None
        for l in range(DEPTH):
            own, other = big[(n, l)]
            outs = _adamw_halves(own, other, w[n], m[n], v[n], n, l, sink.c_arr, outs)
        g_out[n], d_out[n], m_out[n], v_out[n] = outs
    shapes = [small[n].shape for n in SMALL]
    packed = [_pack([src[n] for n in SMALL]) for src in (small, w, m, v)]
    upd = _adamw(*packed, "adamw_small")
    for dst, buf in zip((d_out, m_out, v_out), upd):
        dst.update(zip(SMALL, _unpack(buf, shapes)))
    g_out.update(small)

    return (loss, grad_x[None], *[g_out[n] for n in WEIGHTS], *[d_out[n] for n in WEIGHTS],
            *[m_out[n] for n in WEIGHTS], *[v_out[n] for n in WEIGHTS])
```

```python
import functools
import math

import numpy as np
import jax
import jax.numpy as jnp
from jax import lax
from jax.experimental import pallas as pl
from jax.experimental.pallas import tpu as pltpu

f32 = jnp.float32
bf16 = jnp.bfloat16

D_MODEL = 1024
DEPTH = 2
HEAD_DIM = 64
CONV_CH = 256
CONV_WIDTH = 31
ATTN_HEADS = 8
ATTN_CH = ATTN_HEADS * HEAD_DIM
DILATIONS = (1, 4, 16)
ATTN_BLOCK = 128
N_BUCKETS = 32
MAX_DISTANCE = 2048
GMLP_CH = 256
GMLP_GROUPS = 4
GMLP_GROUP_DIM = GMLP_CH // GMLP_GROUPS
CHUNK = 128
IN_CH = 2 * CONV_CH + 3 * ATTN_CH + 2 * GMLP_CH
D_FF = 2816
FFN_CONV_WIDTH = 3
LN_EPS = 1e-5
ALPHA = (2.0 * DEPTH) ** 0.25
ADAM_LR = 0.001
ADAM_B1 = 0.9
ADAM_B2 = 0.999
ADAM_EPS = 1e-08
ADAM_WD = 0.01
ADAM_STEP = 10

CONV_HALO = 32
FFN_HALO = 8
NEG = -1e30
MIB = 2 ** 20
NT_DIMS = (((1,), (1,)), ((), ()))
TN_DIMS = (((0,), (0,)), ((), ()))
MESH_ID = pl.DeviceIdType.MESH


def _cp(sem, vmem_mib):
    return pltpu.CompilerParams(dimension_semantics=sem, vmem_limit_bytes=vmem_mib * MIB)


def _resident(shape):
    nd = len(shape)
    return pl.BlockSpec(shape, lambda *_: (0,) * nd, pipeline_mode=pl.Buffered(1))


def _acc(shape):
    nd = len(shape)
    return pl.BlockSpec(shape, lambda *_: (0,) * nd)


def _sig(x):
    return 1.0 / (1.0 + jnp.exp(-x))


def _ln_stats(z):
    mu = jnp.mean(z, axis=-1, keepdims=True)
    zc = z - mu
    var = jnp.mean(zc * zc, axis=-1, keepdims=True)
    rstd = lax.rsqrt(var + LN_EPS)
    return zc * rstd, rstd


def _ln_bwd(dy, xhat, rstd, g):
    dxh = dy * g
    m1 = jnp.mean(dxh, axis=-1, keepdims=True)
    m2 = jnp.mean(dxh * xhat, axis=-1, keepdims=True)
    return rstd * (dxh - m1 - xhat * m2)


def _colsum(x):
    return jnp.sum(x, axis=0, keepdims=True)


def _t5_bucket_np(dist):
    max_exact = N_BUCKETS // 2
    dd = np.maximum(dist, 1).astype(np.float64)
    large = max_exact + (np.log(dd / max_exact) / math.log(MAX_DISTANCE / max_exact)
                         * (N_BUCKETS - max_exact)).astype(np.int32)
    large = np.minimum(large, N_BUCKETS - 1)
    return np.where(dist < max_exact, dist, large).astype(np.int32)


def _bucket_ids():
    qi = np.arange(ATTN_BLOCK)[:, None]
    kj = np.arange(2 * ATTN_BLOCK)[None, :]
    dist = np.clip(qi + ATTN_BLOCK - kj, 0, None)
    return np.stack([_t5_bucket_np(dist * d) for d in DILATIONS]).astype(np.int32)


LANES = 128
QKV_CH = 3 * ATTN_CH
PERM_TILE = 512


def _slabs(n, rows):
    return [pltpu.VMEM((rows, LANES), f32)] * n


def _rows_of(slab, r, n, d):
    return slab[...] if d == 1 else slab[pl.ds(r, n, stride=d), :]


def _set_rows_of(slab, r, n, d, val):
    if d == 1:
        slab[...] = val
    else:
        slab[pl.ds(r, n, stride=d), :] = val


def _perm_spec(d, ch):
    return pl.BlockSpec((d, PERM_TILE // d, ch), lambda i: (0, i, 0))


def _perm_shape(S, d, ch, dtype):
    return jax.ShapeDtypeStruct((d, S // d, ch), dtype)


def _inproj_fwd(x, w, b):
    S = x.shape[0]
    T = PERM_TILE
    nsl = QKV_CH // LANES

    def body(x_ref, w_ref, b_ref, a_ref, c_ref, *rest):
        q_refs = rest[:len(DILATIONS)]
        slabs = rest[len(DILATIONS):]
        h = jnp.dot(x_ref[...].astype(bf16), w_ref[...], preferred_element_type=f32) + b_ref[...]
        a_ref[...] = h[:, :2 * CONV_CH]
        q0 = 2 * CONV_CH
        c_ref[...] = h[:, q0 + QKV_CH:]
        for j in range(nsl):
            piece = h[:, q0 + LANES * j:q0 + LANES * (j + 1)]
            if LANES * j < ATTN_CH:
                piece = piece * (HEAD_DIM ** -0.5)
            slabs[j][...] = piece
        for d, q_ref in zip(DILATIONS, q_refs):
            for r in range(d):
                for j in range(nsl):
                    q_ref[r, :, LANES * j:LANES * (j + 1)] = _rows_of(slabs[j], r, T // d, d).astype(bf16)

    row = lambda c: pl.BlockSpec((T, c), lambda i: (i, 0))
    return pl.pallas_call(
        body, grid=(S // T,), name="inproj_fwd",
        out_shape=(jax.ShapeDtypeStruct((S, 2 * CONV_CH), f32), jax.ShapeDtypeStruct((S, 2 * GMLP_CH), f32))
        + tuple(_perm_shape(S, d, QKV_CH, bf16) for d in DILATIONS),
        in_specs=[row(D_MODEL), _resident((D_MODEL, IN_CH)), _resident((1, IN_CH))],
        out_specs=(row(2 * CONV_CH), row(2 * GMLP_CH)) + tuple(_perm_spec(d, QKV_CH) for d in DILATIONS),
        scratch_shapes=_slabs(nsl, T),
        compiler_params=_cp(("parallel",), 48),
    )(x, w, b)


CONV_GROUP = 64


def _window_rolls(starts):
    groups = {}
    for s in starts:
        groups.setdefault((-s) % SUBLANES, []).append(s)
    return dict(sorted(groups.items()))


def _conv_fwd(a_in, dw_w, dw_b, ln_g, ln_b):
    S = a_in.shape[0]
    T = 512
    hb = T // CONV_HALO

    def body(a_ref, halo_ref, w_ref, b_ref, g_ref, be_ref, out_ref, hc_ref, buf):
        i = pl.program_id(0)
        am = a_ref[...]
        ah = halo_ref[...]
        hgh = ah[:, :CONV_CH] * _sig(ah[:, CONV_CH:])
        buf[0:CONV_HALO, :] = jnp.where(i > 0, hgh, 0.0)
        buf[CONV_HALO:, :] = am[:, :CONV_CH] * _sig(am[:, CONV_CH:])
        starts = _window_rolls(range(CONV_HALO - (CONV_WIDTH - 1), CONV_HALO + 1))
        slabs = [slice(LANES * j, LANES * (j + 1)) for j in range(CONV_CH // LANES)]

        def step(g, _):
            r0 = pl.multiple_of(g * CONV_GROUP, CONV_GROUP)
            rows = pl.ds(r0, CONV_GROUP)
            for cs in slabs:
                ext = buf[pl.ds(r0, CONV_GROUP + CONV_HALO), cs]
                acc = jnp.broadcast_to(b_ref[:, cs], (CONV_GROUP, LANES))
                for b, ss in starts.items():
                    rolled = ext if b == 0 else pltpu.roll(ext, b, 0)
                    for s in ss:
                        k = s - (CONV_HALO - (CONV_WIDTH - 1))
                        acc = acc + w_ref[k:k + 1, cs] * rolled[s + b:s + b + CONV_GROUP]
                hc_ref[rows, cs] = acc
            return 0

        lax.fori_loop(0, T // CONV_GROUP, step, 0)
        xhat, _ = _ln_stats(hc_ref[...])
        y = xhat * g_ref[...] + be_ref[...]
        out_ref[...] = (y * _sig(y)).astype(bf16)

    return pl.pallas_call(
        body, grid=(S // T,), name="conv_fwd",
        out_shape=(jax.ShapeDtypeStruct((S, CONV_CH), bf16), jax.ShapeDtypeStruct((S, CONV_CH), f32)),
        in_specs=[pl.BlockSpec((T, 2 * CONV_CH), lambda i: (i, 0)),
                  pl.BlockSpec((CONV_HALO, 2 * CONV_CH), lambda i: (jnp.maximum(i * hb - 1, 0), 0)),
                  _acc((32, CONV_CH)), _acc((1, CONV_CH)), _acc((1, CONV_CH)), _acc((1, CONV_CH))],
        out_specs=(pl.BlockSpec((T, CONV_CH), lambda i: (i, 0)), pl.BlockSpec((T, CONV_CH), lambda i: (i, 0))),
        scratch_shapes=[pltpu.VMEM((T + CONV_HALO, CONV_CH), f32)],
        compiler_params=_cp(("parallel",), 32),
    )(a_in, a_in, dw_w, dw_b, ln_g, ln_b)


def _bias_build(table, buckets):
    def body(t_ref, bk_ref, o_ref):
        h = pl.program_id(1)
        ids = bk_ref[0]
        acc = jnp.zeros((ATTN_BLOCK, 2 * ATTN_BLOCK), f32)
        for b in range(N_BUCKETS):
            acc = jnp.where(ids == b, t_ref[b, h], acc)
        row = lax.broadcasted_iota(jnp.int32, acc.shape, 0)
        col = lax.broadcasted_iota(jnp.int32, acc.shape, 1)
        o_ref[0, 0] = jnp.where((col >= row) & (col <= row + ATTN_BLOCK), acc, NEG)

    return pl.pallas_call(
        body, grid=(len(DILATIONS), ATTN_HEADS), name="bias_build",
        out_shape=jax.ShapeDtypeStruct((len(DILATIONS), ATTN_HEADS, ATTN_BLOCK, 2 * ATTN_BLOCK), f32),
        in_specs=[pl.BlockSpec(memory_space=pltpu.SMEM),
                  pl.BlockSpec((1, ATTN_BLOCK, 2 * ATTN_BLOCK), lambda p, h: (p, 0, 0))],
        out_specs=pl.BlockSpec((1, 1, ATTN_BLOCK, 2 * ATTN_BLOCK), lambda p, h: (p, h, 0, 0)),
        compiler_params=_cp(("arbitrary", "arbitrary"), 16),
    )(table, buckets)


def _head_tile(tile, h, col):
    lane_head = lax.broadcasted_iota(jnp.int32, tile.shape, 1) // 16
    return jnp.where(lane_head == h, col, tile)


HEAD_PAIRS = ATTN_HEADS // 2
UNITS_PER_BLOCK = ATTN_HEADS


def _attn_tile(L):
    return min(512, L)


def _mask_logits(logits, first_block, n):
    if not first_block:
        return logits
    col = lax.broadcasted_iota(jnp.int32, logits.shape, 1)
    return jnp.where((col >= ATTN_BLOCK) | (n > 0), logits, NEG)


def _head_lanes(a):
    lane = lax.broadcasted_iota(jnp.int32, (ATTN_BLOCK, LANES), 1)
    return (lane < HEAD_DIM) if a == 0 else (lane >= HEAD_DIM)


def _pair_keys(cur_ref, halo_ref, part, b, j):
    B = ATTN_BLOCK
    c0 = part * ATTN_CH + LANES * j
    own = cur_ref[B * b:B * (b + 1), c0:c0 + LANES]
    prev = halo_ref[:, LANES * j:LANES * (j + 1)] if b == 0 else cur_ref[B * (b - 1):B * b, c0:c0 + LANES]
    return jnp.concatenate([prev, own], axis=0)


def _attn_fwd_pattern(qkv, bias, d):
    _, L, _ = qkv.shape
    B = ATTN_BLOCK
    QB = _attn_tile(L)
    nsb = QB // B
    U = nsb * UNITS_PER_BLOCK

    def body(cur_ref, hk_ref, hv_ref, b_ref, o_ref, lse_ref, lg, pb):
        n = pl.program_id(1)
        for b in range(nsb):
            for j in range(HEAD_PAIRS):
                q2 = cur_ref[B * b:B * (b + 1), LANES * j:LANES * (j + 1)]
                k2 = _pair_keys(cur_ref, hk_ref, 1, b, j)
                for a in range(2):
                    u = (b * HEAD_PAIRS + j) * 2 + a
                    qm = jnp.where(_head_lanes(a), q2, jnp.zeros_like(q2))
                    logits = lax.dot_general(qm, k2, NT_DIMS, preferred_element_type=f32) + b_ref[2 * j + a]
                    lg[B * u:B * (u + 1), :] = _mask_logits(logits, b == 0, n)
        m = jnp.max(lg[...], axis=1, keepdims=True)
        p = jnp.exp(lg[...] - m)
        s = jnp.sum(p, axis=1, keepdims=True)
        pb[...] = p.astype(bf16)
        lse = m + jnp.log(s)
        inv = 1.0 / s
        for b in range(nsb):
            tile = jnp.zeros((B, B), f32)
            for j in range(HEAD_PAIRS):
                v2 = _pair_keys(cur_ref, hv_ref, 2, b, j)
                outs = []
                for a in range(2):
                    u = (b * HEAD_PAIRS + j) * 2 + a
                    rows = slice(B * u, B * (u + 1))
                    outs.append(jnp.dot(pb[rows, :], v2, preferred_element_type=f32) * inv[rows])
                    tile = _head_tile(tile, 2 * j + a, lse[rows])
                o_ref[B * b:B * (b + 1), LANES * j:LANES * (j + 1)] = jnp.where(_head_lanes(0), outs[0], outs[1])
            lse_ref[B * b:B * (b + 1), :] = tile

    halo = lambda part: pl.BlockSpec((None, B, ATTN_CH), lambda r, n: (r, jnp.maximum(n * nsb - 1, 0), part))
    tile_spec = lambda c: pl.BlockSpec((None, QB, c), lambda r, n: (r, n, 0))
    return pl.pallas_call(
        body, grid=(d, L // QB), name=f"attn_fwd_d{d}",
        out_shape=(jax.ShapeDtypeStruct((d, L, ATTN_CH), f32), jax.ShapeDtypeStruct((d, L, B), f32)),
        in_specs=[tile_spec(QKV_CH), halo(1), halo(2), _resident((ATTN_HEADS, B, 2 * B))],
        out_specs=(tile_spec(ATTN_CH), tile_spec(B)),
        scratch_shapes=[pltpu.VMEM((U * B, 2 * B), f32), pltpu.VMEM((U * B, 2 * B), bf16)],
        compiler_params=_cp(("parallel", "parallel"), 40),
    )(qkv, qkv, qkv, bias)


def _attn_merge(parts):
    S = parts[0][0].shape[0] * parts[0][0].shape[1]
    T = PERM_TILE
    nsl = ATTN_CH // LANES
    n_p = len(DILATIONS)

    def body(*refs):
        ins = refs[:2 * n_p]
        out_ref, lse_ref = refs[2 * n_p:2 * n_p + 2]
        slabs = refs[2 * n_p + 2:]
        lses = []
        for p, d in enumerate(DILATIONS):
            o_ref, l_ref = ins[2 * p], ins[2 * p + 1]
            osl = slabs[p * (nsl + 1):p * (nsl + 1) + nsl]
            lsl = slabs[p * (nsl + 1) + nsl]
            for r in range(d):
                for j in range(nsl):
                    _set_rows_of(osl[j], r, T // d, d, o_ref[r, :, LANES * j:LANES * (j + 1)])
                _set_rows_of(lsl, r, T // d, d, l_ref[r])
            lses.append(lsl[...])
        big = functools.reduce(jnp.maximum, lses)
        ws = [jnp.exp(l - big) for l in lses]
        tot = functools.reduce(lambda a_, b_: a_ + b_, ws)
        lse_ref[...] = big + jnp.log(tot)
        ws = [w / tot for w in ws]
        for j in range(nsl):
            acc = jnp.zeros((T, LANES), f32)
            for p in range(n_p):
                wa = ws[p][:, 32 * j:32 * j + 1]
                wb = ws[p][:, 32 * j + 16:32 * j + 17]
                lane = lax.broadcasted_iota(jnp.int32, (T, LANES), 1)
                acc = acc + jnp.where(lane < HEAD_DIM, wa, wb) * slabs[p * (nsl + 1) + j][...]
            out_ref[:, LANES * j:LANES * (j + 1)] = acc.astype(bf16)

    in_specs, args = [], []
    for (o, l), d in zip(parts, DILATIONS):
        in_specs += [_perm_spec(d, ATTN_CH), _perm_spec(d, ATTN_BLOCK)]
        args += [o, l]
    row = lambda c: pl.BlockSpec((T, c), lambda i: (i, 0))
    return pl.pallas_call(
        body, grid=(S // T,), name="attn_merge",
        out_shape=(jax.ShapeDtypeStruct((S, ATTN_CH), bf16), jax.ShapeDtypeStruct((S, ATTN_BLOCK), f32)),
        in_specs=in_specs, out_specs=(row(ATTN_CH), row(ATTN_BLOCK)),
        scratch_shapes=_slabs(n_p * (nsl + 1), T),
        compiler_params=_cp(("parallel",), 40),
    )(*args)


def _attn_fwd(qkvs, bias):
    parts = [_attn_fwd_pattern(q, bias[p], d) for p, (q, d) in enumerate(zip(qkvs, DILATIONS))]
    return _attn_merge(parts)


def _tril_bf16(w):
    row = lax.broadcasted_iota(jnp.int32, (CHUNK, CHUNK), 0)
    col = lax.broadcasted_iota(jnp.int32, (CHUNK, CHUNK), 1)
    return jnp.where(col <= row, w, 0.0).astype(bf16)


def _gmlp_fwd(c_in, ln_g, ln_b, w_s, b_s_t):
    S = c_in.shape[0]
    T = 512

    def body(c_ref, g_ref, be_ref, w_ref, bs_ref, out_ref, mix):
        c = c_ref[...]
        xhat, _ = _ln_stats(c[:, GMLP_CH:])
        vb = (xhat * g_ref[...] + be_ref[...]).astype(bf16)
        for g in range(GMLP_GROUPS):
            wt = _tril_bf16(w_ref[g])
            cs = slice(GMLP_GROUP_DIM * g, GMLP_GROUP_DIM * (g + 1))
            for ci in range(T // CHUNK):
                rs = slice(CHUNK * ci, CHUNK * (ci + 1))
                mix[rs, cs] = jnp.dot(wt, vb[rs, cs], preferred_element_type=f32) + bs_ref[:, g:g + 1]
        out_ref[...] = (c[:, :GMLP_CH] * mix[...]).astype(bf16)

    return pl.pallas_call(
        body, grid=(S // T,), name="gmlp_fwd",
        out_shape=jax.ShapeDtypeStruct((S, GMLP_CH), bf16),
        in_specs=[pl.BlockSpec((T, 2 * GMLP_CH), lambda i: (i, 0)), _acc((1, GMLP_CH)), _acc((1, GMLP_CH)),
                  _acc((GMLP_GROUPS, CHUNK, CHUNK)), _acc((CHUNK, GMLP_GROUPS))],
        out_specs=pl.BlockSpec((T, GMLP_CH), lambda i: (i, 0)),
        scratch_shapes=[pltpu.VMEM((T, GMLP_CH), f32)],
        compiler_params=_cp(("parallel",), 32),
    )(c_in, ln_g, ln_b, w_s, b_s_t)


def _outproj_ln_fwd(conv_out, attn_out, gm_out, w, b, x, ln_g, ln_b):
    S = x.shape[0]
    T = 512

    def body(co_ref, ao_ref, go_ref, w_ref, b_ref, x_ref, g_ref, be_ref, cat_ref, z_ref, yb_ref):
        cat = jnp.concatenate([co_ref[...], ao_ref[...], go_ref[...]], axis=1)
        cat_ref[...] = cat
        z = jnp.dot(cat, w_ref[...], preferred_element_type=f32) + b_ref[...] + ALPHA * x_ref[...]
        z_ref[...] = z
        xhat, _ = _ln_stats(z)
        yb_ref[...] = (xhat * g_ref[...] + be_ref[...]).astype(bf16)

    row = lambda c: pl.BlockSpec((T, c), lambda i: (i, 0))
    return pl.pallas_call(
        body, grid=(S // T,), name="outproj_ln_fwd",
        out_shape=(jax.ShapeDtypeStruct((S, D_MODEL), bf16), jax.ShapeDtypeStruct((S, D_MODEL), f32),
                   jax.ShapeDtypeStruct((S, D_MODEL), bf16)),
        in_specs=[row(CONV_CH), row(ATTN_CH), row(GMLP_CH), _resident((D_MODEL, D_MODEL)), _acc((1, D_MODEL)),
                  row(D_MODEL), _acc((1, D_MODEL)), _acc((1, D_MODEL))],
        out_specs=(row(D_MODEL), row(D_MODEL), row(D_MODEL)),
        compiler_params=_cp(("parallel",), 40),
    )(conv_out, attn_out, gm_out, w, b, x, ln_g, ln_b)


GATE_ROWS = 32
GATE_COLS = 128
GATE_MM_COLS = 256
SUBLANES = 8


def _gate_cols(c0):
    return slice(c0, c0 + GATE_COLS), slice(D_FF + c0, D_FF + c0 + GATE_COLS)


def _bcast_rows(ref, k, cs):
    return jnp.broadcast_to(ref[k:k + 1, cs], (GATE_ROWS, GATE_COLS))


def _fold_rows(z):
    acc = z[0:SUBLANES]
    for r in range(SUBLANES, GATE_ROWS, SUBLANES):
        acc = acc + z[r:r + SUBLANES]
    return acc


def _ffn_up_gate_fwd(x1b, w, b, conv_w, conv_b):
    S = x1b.shape[0]
    T = 256
    H = FFN_HALO
    K = FFN_CONV_WIDTH

    def body(x_ref, w_ref, b_ref, cw_ref, cb_ref, hfb_ref, hc_ref, act_ref, hbuf, carry):
        @pl.when(pl.program_id(0) == 0)
        def _():
            carry[...] = jnp.zeros_like(carry)
        x = x_ref[...]
        for m0 in range(0, D_FF, GATE_MM_COLS):
            for cm in (slice(m0, m0 + GATE_MM_COLS), slice(D_FF + m0, D_FF + m0 + GATE_MM_COLS)):
                h = jnp.dot(x, w_ref[:, cm], preferred_element_type=f32) + b_ref[:, cm]
                hbuf[:, cm] = h
                hfb_ref[:, cm] = h.astype(bf16)
            for c0 in range(m0, m0 + GATE_MM_COLS, GATE_COLS):
                cols = _gate_cols(c0)
                wts = [[_bcast_rows(cw_ref, k, cs) for k in range(K)] + [_bcast_rows(cb_ref, 0, cs)] for cs in cols]

                def step(rg, tails, cols=cols, wts=wts):
                    rows = pl.ds(pl.multiple_of(rg * GATE_ROWS, GATE_ROWS), GATE_ROWS)
                    hc, new_tails = [], []
                    for cs, wt, tail in zip(cols, wts, tails):
                        h = hbuf[rows, cs]
                        ext = jnp.concatenate([tail, h], axis=0)
                        acc = wt[K] + wt[K - 1] * h
                        for back in range(1, K):
                            acc = acc + wt[K - 1 - back] * pltpu.roll(ext, back, 0)[H:]
                        hc_ref[rows, cs] = acc
                        hc.append(acc)
                        new_tails.append(h[GATE_ROWS - H:])
                    act_ref[rows, cols[0]] = (hc[0] * _sig(hc[0]) * hc[1]).astype(bf16)
                    return tuple(new_tails)

                tails = lax.fori_loop(0, T // GATE_ROWS, step, tuple(carry[:, cs] for cs in cols), unroll=True)
                for cs, tail in zip(cols, tails):
                    carry[:, cs] = tail

    row = lambda c: pl.BlockSpec((T, c), lambda i: (i, 0))
    return pl.pallas_call(
        body, grid=(S // T,), name="ffn_up_gate_fwd",
        out_shape=(jax.ShapeDtypeStruct((S, 2 * D_FF), bf16), jax.ShapeDtypeStruct((S, 2 * D_FF), f32),
                   jax.ShapeDtypeStruct((S, D_FF), bf16)),
        in_specs=[row(D_MODEL), _resident((D_MODEL, 2 * D_FF)), _acc((1, 2 * D_FF)), _acc((8, 2 * D_FF)),
                  _acc((1, 2 * D_FF))],
        out_specs=(row(2 * D_FF), row(2 * D_FF), row(D_FF)),
        scratch_shapes=[pltpu.VMEM((T, 2 * D_FF), f32), pltpu.VMEM((H, 2 * D_FF), f32)],
        compiler_params=_cp(("arbitrary",), 56),
    )(x1b, w, b, conv_w, conv_b)


def _ffn_down_ln_fwd(act, w, b, z1, ln1_g, ln1_b, ln_g, ln_b):
    S = act.shape[0]
    T = 512

    def body(a_ref, w_ref, b_ref, z1_ref, g1_ref, be1_ref, g_ref, be_ref, z_ref, y_ref):
        subs = [slice(s0, s0 + T // 2) for s0 in (0, T // 2)]
        zs = [jnp.dot(a_ref[rs, :], w_ref[...], preferred_element_type=f32) + b_ref[...]
              + ALPHA * (_ln_stats(z1_ref[rs, :])[0] * g1_ref[...] + be1_ref[...]) for rs in subs]
        for rs, z in zip(subs, zs):
            z_ref[rs, :] = z
            xhat, _ = _ln_stats(z)
            y_ref[rs, :] = xhat * g_ref[...] + be_ref[...]

    row = lambda c: pl.BlockSpec((T, c), lambda i: (i, 0))
    return pl.pallas_call(
        body, grid=(S // T,), name="ffn_down_ln_fwd",
        out_shape=(jax.ShapeDtypeStruct((S, D_MODEL), f32), jax.ShapeDtypeStruct((S, D_MODEL), f32)),
        in_specs=[row(D_FF), _resident((D_FF, D_MODEL)), _acc((1, D_MODEL)), row(D_MODEL)] + [_acc((1, D_MODEL))] * 4,
        out_specs=(row(D_MODEL), row(D_MODEL)),
        compiler_params=_cp(("parallel",), 40),
    )(act, w, b, z1, ln1_g, ln1_b, ln_g, ln_b)


def _ffn_down_ln_loss(act, w, b, z1, ln1_g, ln1_b, ln_g, ln_b, target):
    S = act.shape[0]
    T = 512

    def body(a_ref, w_ref, b_ref, z1_ref, g1_ref, be1_ref, g_ref, be_ref, t_ref, dz_ref, dzb_ref, loss_ref, dg_ref,
             db_ref):
        @pl.when(pl.program_id(0) == 0)
        def _():
            loss_ref[...] = jnp.zeros_like(loss_ref)
            dg_ref[...] = jnp.zeros_like(dg_ref)
            db_ref[...] = jnp.zeros_like(db_ref)
        subs = [slice(s0, s0 + T // 2) for s0 in (0, T // 2)]
        zs = [jnp.dot(a_ref[rs, :], w_ref[...], preferred_element_type=f32) + b_ref[...]
              + ALPHA * (_ln_stats(z1_ref[rs, :])[0] * g1_ref[...] + be1_ref[...]) for rs in subs]
        for rs, z in zip(subs, zs):
            xhat, rstd = _ln_stats(z)
            err = xhat * g_ref[...] + be_ref[...] - t_ref[rs, :]
            loss_ref[...] += _colsum(err * err) * (0.5 / D_MODEL)
            dy = err * (1.0 / D_MODEL)
            dz = _ln_bwd(dy, xhat, rstd, g_ref[...])
            dz_ref[rs, :] = dz
            dzb_ref[rs, :] = dz.astype(bf16)
            dg_ref[...] += _colsum(dy * xhat)
            db_ref[...] += _colsum(dy)

    row = lambda c: pl.BlockSpec((T, c), lambda i: (i, 0))
    vec = jax.ShapeDtypeStruct((1, D_MODEL), f32)
    return pl.pallas_call(
        body, grid=(S // T,), name="ffn_down_ln_loss",
        out_shape=(jax.ShapeDtypeStruct((S, D_MODEL), f32), jax.ShapeDtypeStruct((S, D_MODEL), bf16), vec, vec, vec),
        in_specs=[row(D_FF), _resident((D_FF, D_MODEL)), _acc((1, D_MODEL)), row(D_MODEL)] + [_acc((1, D_MODEL))] * 4
        + [row(D_MODEL)],
        out_specs=(row(D_MODEL), row(D_MODEL), _acc((1, D_MODEL)), _acc((1, D_MODEL)), _acc((1, D_MODEL))),
        compiler_params=_cp(("arbitrary",), 40),
    )(act, w, b, z1, ln1_g, ln1_b, ln_g, ln_b, target)


def _dgrad_ln_bwd(g, w, dz_res, z, ln_g, name):
    S, K = g.shape
    SUB = 256
    T = 2 * SUB if S % (2 * SUB) == 0 else SUB
    with_ln = z is not None

    def body(*refs):
        if with_ln:
            g_ref, w_ref, r_ref, z_ref, lg_ref, dz_ref, dzb_ref, dg_ref, db_ref = refs
        else:
            g_ref, w_ref, r_ref, dx_ref = refs
        subs = [slice(s0, s0 + SUB) for s0 in range(0, T, SUB)]
        dxs = [lax.dot_general(g_ref[rs, :], w_ref[...], NT_DIMS, preferred_element_type=f32) + ALPHA * r_ref[rs, :]
               for rs in subs]
        if not with_ln:
            for rs, dx in zip(subs, dxs):
                dx_ref[rs, :] = dx
            return

        @pl.when(pl.program_id(0) == 0)
        def _():
            dg_ref[...] = jnp.zeros_like(dg_ref)
            db_ref[...] = jnp.zeros_like(db_ref)
        for rs, dx in zip(subs, dxs):
            xhat, rstd = _ln_stats(z_ref[rs, :])
            dz = _ln_bwd(dx, xhat, rstd, lg_ref[...])
            dz_ref[rs, :] = dz
            dzb_ref[rs, :] = dz.astype(bf16)
            dg_ref[...] += _colsum(dx * xhat)
            db_ref[...] += _colsum(dx)

    row = pl.BlockSpec((T, D_MODEL), lambda i: (i, 0))
    vec = jax.ShapeDtypeStruct((1, D_MODEL), f32)
    in_specs = [pl.BlockSpec((T, K), lambda i: (i, 0)), _resident((D_MODEL, K)), row]
    args = [g, w, dz_res]
    if with_ln:
        in_specs += [row, _acc((1, D_MODEL))]
        args += [z, ln_g]
        out_shape = (jax.ShapeDtypeStruct((S, D_MODEL), f32), jax.ShapeDtypeStruct((S, D_MODEL), bf16), vec, vec)
        out_specs = (row, row, _acc((1, D_MODEL)), _acc((1, D_MODEL)))
    else:
        out_shape = jax.ShapeDtypeStruct((S, D_MODEL), f32)
        out_specs = row
    return pl.pallas_call(
        body, grid=(S // T,), name=name, out_shape=out_shape, in_specs=in_specs, out_specs=out_specs,
        compiler_params=_cp(("arbitrary",), 48),
    )(*args)


def _ffn_down_gate_bwd(dzb, w_down, hfb, hc, conv_w):
    S = hc.shape[0]
    T = 256
    H = FFN_HALO
    nt = S // T
    K = FFN_CONV_WIDTH

    def body(dz_ref, w_ref, h_ref, hc_ref, cw_ref, dh_ref, dw_ref, dcb_ref, da_buf, carry):
        @pl.when(pl.program_id(0) == 0)
        def _():
            dw_ref[...] = jnp.zeros_like(dw_ref)
            dcb_ref[...] = jnp.zeros_like(dcb_ref)
            carry[...] = jnp.zeros_like(carry)
        da_buf[...] = lax.dot_general(dz_ref[...], w_ref[...], NT_DIMS, preferred_element_type=f32)
        ngroups = T // GATE_ROWS
        for c0 in range(0, D_FF, GATE_COLS):
            cols = _gate_cols(c0)
            wts = [[_bcast_rows(cw_ref, k, cs) for k in range(K)] for cs in cols]

            def step(it, state, cols=cols, wts=wts):
                heads, accs = state
                rows = pl.ds(pl.multiple_of((ngroups - 1 - it) * GATE_ROWS, GATE_ROWS), GATE_ROWS)
                g = hc_ref[rows, cols[0]]
                v = hc_ref[rows, cols[1]]
                da = da_buf[rows, cols[0]]
                sg = _sig(g)
                silu = g * sg
                dms = (da * v * (sg + silu * (1.0 - sg)), da * silu)
                new_heads, new_accs = [], []
                for cs, wt, dm, head, acc in zip(cols, wts, dms, heads, accs):
                    h0 = h_ref[rows, cs].astype(f32)
                    ext = jnp.concatenate([dm, head], axis=0)
                    dh = wt[K - 1] * dm
                    acc_k = [None] * K + [acc[K] + _fold_rows(dm)]
                    acc_k[K - 1] = acc[K - 1] + _fold_rows(dm * h0)
                    for ahead in range(1, K):
                        dk = pltpu.roll(ext, GATE_ROWS + H - ahead, 0)[:GATE_ROWS]
                        dh = dh + wt[K - 1 - ahead] * dk
                        acc_k[K - 1 - ahead] = acc[K - 1 - ahead] + _fold_rows(dk * h0)
                    dh_ref[rows, cs] = dh.astype(bf16)
                    new_heads.append(dm[:H])
                    new_accs.append(tuple(acc_k))
                return tuple(new_heads), tuple(new_accs)

            zero = jnp.zeros((SUBLANES, GATE_COLS), f32)
            init = (tuple(carry[:, cs] for cs in cols), tuple(tuple(zero for _ in range(K + 1)) for _ in cols))
            heads, accs = lax.fori_loop(0, ngroups, step, init, unroll=True)
            for cs, head, acc in zip(cols, heads, accs):
                carry[:, cs] = head
                dcb_ref[:, cs] += _colsum(acc[K])
                for k in range(K):
                    dw_ref[k:k + 1, cs] += _colsum(acc[k])

    tile = lambda c: pl.BlockSpec((T, c), lambda i: (nt - 1 - i, 0))
    return pl.pallas_call(
        body, grid=(nt,), name="ffn_down_gate_bwd",
        out_shape=(jax.ShapeDtypeStruct((S, 2 * D_FF), bf16), jax.ShapeDtypeStruct((8, 2 * D_FF), f32),
                   jax.ShapeDtypeStruct((1, 2 * D_FF), f32)),
        in_specs=[tile(D_MODEL), _resident((D_FF, D_MODEL)), tile(2 * D_FF), tile(2 * D_FF), _acc((8, 2 * D_FF))],
        out_specs=(tile(2 * D_FF), _acc((8, 2 * D_FF)), _acc((1, 2 * D_FF))),
        scratch_shapes=[pltpu.VMEM((T, D_FF), f32), pltpu.VMEM((H, 2 * D_FF), f32)],
        compiler_params=_cp(("arbitrary",), 48),
    )(dzb, w_down, hfb, hc, conv_w)


def _wgrad(a, g, tn, name, rows=1024):
    S, K = a.shape
    N = g.shape[1]
    T = rows if S % rows == 0 else S

    def body(a_ref, g_ref, dw_ref, db_ref):
        @pl.when(pl.program_id(1) == 0)
        def _():
            dw_ref[...] = jnp.zeros_like(dw_ref)
            db_ref[...] = jnp.zeros_like(db_ref)
        gt = g_ref[...]
        dw_ref[...] += lax.dot_general(a_ref[...].astype(bf16), gt, TN_DIMS, preferred_element_type=f32)
        db_ref[...] += _colsum(gt.astype(f32))

    return pl.pallas_call(
        body, grid=(N // tn, S // T), name=name,
        out_shape=(jax.ShapeDtypeStruct((K, N), f32), jax.ShapeDtypeStruct((1, N), f32)),
        in_specs=[pl.BlockSpec((T, K), lambda j, i: (i, 0)), pl.BlockSpec((T, tn), lambda j, i: (i, j))],
        out_specs=(pl.BlockSpec((K, tn), lambda j, i: (0, j)), pl.BlockSpec((1, tn), lambda j, i: (0, j))),
        compiler_params=_cp(("parallel", "arbitrary"), 56),
    )(a, g)


def _outproj_dgrad(dzb, w, attn_out, lse):
    S = dzb.shape[0]
    T = PERM_TILE
    nsl = ATTN_CH // LANES
    n_p = len(DILATIONS)

    def body(g_ref, w_ref, ao_ref, lse_ref, dco_ref, dgo_ref, *rest):
        do_refs = rest[:n_p]
        st_refs = rest[n_p:2 * n_p]
        slabs = rest[2 * n_p:]
        dcat = lax.dot_general(g_ref[...], w_ref[...], NT_DIMS, preferred_element_type=f32)
        dco_ref[...] = dcat[:, :CONV_CH]
        dgo_ref[...] = dcat[:, CONV_CH + ATTN_CH:]
        lane = lax.broadcasted_iota(jnp.int32, (T, LANES), 1)
        st = lse_ref[...]
        for j in range(nsl):
            dO = dcat[:, CONV_CH + LANES * j:CONV_CH + LANES * (j + 1)]
            prod = dO * ao_ref[:, LANES * j:LANES * (j + 1)].astype(f32)
            for a in range(2):
                in_head = (lane < HEAD_DIM) if a == 0 else (lane >= HEAD_DIM)
                delta = jnp.sum(jnp.where(in_head, prod, 0.0), axis=1, keepdims=True)
                st = jnp.where((lane // 16 == 2 * j + a) & (lane % 16 >= 8), delta, st)
            slabs[j][...] = dO
        slabs[nsl][...] = st
        for d, do_ref, st_ref in zip(DILATIONS, do_refs, st_refs):
            for r in range(d):
                for j in range(nsl):
                    do_ref[r, :, LANES * j:LANES * (j + 1)] = _rows_of(slabs[j], r, T // d, d).astype(bf16)
                st_ref[r] = _rows_of(slabs[nsl], r, T // d, d)

    row = lambda c: pl.BlockSpec((T, c), lambda i: (i, 0))
    return pl.pallas_call(
        body, grid=(S // T,), name="outproj_dgrad",
        out_shape=(jax.ShapeDtypeStruct((S, CONV_CH), f32), jax.ShapeDtypeStruct((S, GMLP_CH), f32))
        + tuple(_perm_shape(S, d, ATTN_CH, bf16) for d in DILATIONS)
        + tuple(_perm_shape(S, d, ATTN_BLOCK, f32) for d in DILATIONS),
        in_specs=[row(D_MODEL), _resident((D_MODEL, D_MODEL)), row(ATTN_CH), row(ATTN_BLOCK)],
        out_specs=(row(CONV_CH), row(GMLP_CH)) + tuple(_perm_spec(d, ATTN_CH) for d in DILATIONS)
        + tuple(_perm_spec(d, ATTN_BLOCK) for d in DILATIONS),
        scratch_shapes=_slabs(nsl + 1, T),
        compiler_params=_cp(("parallel",), 40),
    )(dzb, w, attn_out, lse)


def _gmlp_bwd(c_in, dgm, ln_g, ln_b, w_s, b_s_t):
    S = c_in.shape[0]
    T = 512
    nsteps = S // T

    def body(c_ref, dg_ref, g_ref, be_ref, w_ref, bs_ref, dc_ref, dlg_ref, dlb_ref, dw_ref, dbs_ref,
             du_buf, dv_buf, dm_acc):
        i = pl.program_id(0)

        @pl.when(i == 0)
        def _():
            dlg_ref[...] = jnp.zeros_like(dlg_ref)
            dlb_ref[...] = jnp.zeros_like(dlb_ref)
            dw_ref[...] = jnp.zeros_like(dw_ref)
            dm_acc[...] = jnp.zeros_like(dm_acc)
        c = c_ref[...]
        u = c[:, :GMLP_CH]
        xhat, rstd = _ln_stats(c[:, GMLP_CH:])
        vb = (xhat * g_ref[...] + be_ref[...]).astype(bf16)
        dgm_t = dg_ref[...]
        dm_all = dgm_t * u
        for g in range(GMLP_GROUPS):
            wt = _tril_bf16(w_ref[g])
            cs = slice(GMLP_GROUP_DIM * g, GMLP_GROUP_DIM * (g + 1))
            dw_g = jnp.zeros((CHUNK, CHUNK), f32)
            for ci in range(T // CHUNK):
                rs = slice(CHUNK * ci, CHUNK * (ci + 1))
                v_c = vb[rs, cs]
                mixed = jnp.dot(wt, v_c, preferred_element_type=f32) + bs_ref[:, g:g + 1]
                dm = dm_all[rs, cs]
                dmb = dm.astype(bf16)
                du_buf[rs, cs] = dgm_t[rs, cs] * mixed
                dv_buf[rs, cs] = lax.dot_general(wt, dmb, TN_DIMS, preferred_element_type=f32)
                dw_g = dw_g + lax.dot_general(dmb, v_c, NT_DIMS, preferred_element_type=f32)
                dm_acc[:, cs] += dm
            dw_ref[g] += dw_g
        dv = dv_buf[...]
        dvr = _ln_bwd(dv, xhat, rstd, g_ref[...])
        dlg_ref[...] += _colsum(dv * xhat)
        dlb_ref[...] += _colsum(dv)
        dc_ref[:, :GMLP_CH] = du_buf[...].astype(bf16)
        dc_ref[:, GMLP_CH:] = dvr.astype(bf16)

        @pl.when(i == nsteps - 1)
        def _():
            row = lax.broadcasted_iota(jnp.int32, (CHUNK, CHUNK), 0)
            col = lax.broadcasted_iota(jnp.int32, (CHUNK, CHUNK), 1)
            tile = jnp.zeros((CHUNK, CHUNK), f32)
            for g in range(GMLP_GROUPS):
                dw_ref[g] = jnp.where(col <= row, dw_ref[g], 0.0)
                gsum = jnp.sum(dm_acc[:, GMLP_GROUP_DIM * g:GMLP_GROUP_DIM * (g + 1)], axis=1, keepdims=True)
                tile = jnp.where(col == g, gsum, tile)
            dbs_ref[...] = tile

    vec = jax.ShapeDtypeStruct((1, GMLP_CH), f32)
    return pl.pallas_call(
        body, grid=(nsteps,), name="gmlp_bwd",
        out_shape=(jax.ShapeDtypeStruct((S, 2 * GMLP_CH), bf16), vec, vec,
                   jax.ShapeDtypeStruct((GMLP_GROUPS, CHUNK, CHUNK), f32), jax.ShapeDtypeStruct((CHUNK, CHUNK), f32)),
        in_specs=[pl.BlockSpec((T, 2 * GMLP_CH), lambda i: (i, 0)), pl.BlockSpec((T, GMLP_CH), lambda i: (i, 0)),
                  _acc((1, GMLP_CH)), _acc((1, GMLP_CH)), _acc((GMLP_GROUPS, CHUNK, CHUNK)), _acc((CHUNK, GMLP_GROUPS))],
        out_specs=(pl.BlockSpec((T, 2 * GMLP_CH), lambda i: (i, 0)), _acc((1, GMLP_CH)), _acc((1, GMLP_CH)),
                   _acc((GMLP_GROUPS, CHUNK, CHUNK)), _acc((CHUNK, CHUNK))),
        scratch_shapes=[pltpu.VMEM((T, GMLP_CH), f32), pltpu.VMEM((T, GMLP_CH), f32), pltpu.VMEM((CHUNK, GMLP_CH), f32)],
        compiler_params=_cp(("arbitrary",), 32),
    )(c_in, dgm, ln_g, ln_b, w_s, b_s_t)


def _attn_bwd_pattern(qkv, d_out, stats, bias, d):
    _, L, _ = qkv.shape
    B = ATTN_BLOCK
    QB = _attn_tile(L)
    nsb = QB // B
    nt = L // QB
    U = nsb * UNITS_PER_BLOCK
    KV = 2 * ATTN_CH

    def body(cur_ref, hk_ref, hv_ref, do_ref, st_ref, b_ref, dqkv_ref, dbias_ref, lg, dp, pb, dsb, dkv, carry):
        r = pl.program_id(0)
        i = pl.program_id(1)
        n = nt - 1 - i

        @pl.when((r == 0) & (i == 0))
        def _():
            dbias_ref[...] = jnp.zeros_like(dbias_ref)

        @pl.when(i == 0)
        def _():
            carry[...] = jnp.zeros_like(carry)

        def operands(b, j, a):
            rows = slice(B * b, B * (b + 1))
            q2 = cur_ref[rows, LANES * j:LANES * (j + 1)]
            do2 = do_ref[rows, LANES * j:LANES * (j + 1)]
            keep = _head_lanes(a)
            return jnp.where(keep, q2, jnp.zeros_like(q2)), jnp.where(keep, do2, jnp.zeros_like(do2))

        for b in range(nsb):
            for j in range(HEAD_PAIRS):
                k2 = _pair_keys(cur_ref, hk_ref, 1, b, j)
                v2 = _pair_keys(cur_ref, hv_ref, 2, b, j)
                for a in range(2):
                    u = (b * HEAD_PAIRS + j) * 2 + a
                    qm, dom = operands(b, j, a)
                    logits = lax.dot_general(qm, k2, NT_DIMS, preferred_element_type=f32) + b_ref[2 * j + a]
                    lg[B * u:B * (u + 1), :] = _mask_logits(logits, b == 0, n)
                    dp[B * u:B * (u + 1), :] = lax.dot_general(dom, v2, NT_DIMS, preferred_element_type=f32)
        for j in range(HEAD_PAIRS):
            for a in range(2):
                lane0 = 32 * j + 16 * a
                ds_sum = None
                for b in range(nsb):
                    u = (b * HEAD_PAIRS + j) * 2 + a
                    rows = slice(B * u, B * (u + 1))
                    lse = st_ref[B * b:B * (b + 1), lane0:lane0 + 1]
                    delta = st_ref[B * b:B * (b + 1), lane0 + 8:lane0 + 9]
                    p = jnp.exp(lg[rows, :] - lse)
                    ds = p * (dp[rows, :] - delta)
                    pb[rows, :] = p.astype(bf16)
                    dsb[rows, :] = ds.astype(bf16)
                    ds_sum = ds if ds_sum is None else ds_sum + ds
                dbias_ref[2 * j + a] += ds_sum
        dkv[...] = jnp.zeros_like(dkv)
        for b in range(nsb):
            for j in range(HEAD_PAIRS):
                k2 = _pair_keys(cur_ref, hk_ref, 1, b, j)
                dq, dk2, dv2 = [], None, None
                for a in range(2):
                    u = (b * HEAD_PAIRS + j) * 2 + a
                    rows = slice(B * u, B * (u + 1))
                    qm, dom = operands(b, j, a)
                    ds_u = dsb[rows, :]
                    dq.append(jnp.dot(ds_u, k2, preferred_element_type=f32))
                    dk_u = lax.dot_general(ds_u, qm, TN_DIMS, preferred_element_type=f32)
                    dv_u = lax.dot_general(pb[rows, :], dom, TN_DIMS, preferred_element_type=f32)
                    dk2 = dk_u if dk2 is None else dk2 + dk_u
                    dv2 = dv_u if dv2 is None else dv2 + dv_u
                dq2 = jnp.where(_head_lanes(0), dq[0], dq[1]) * (HEAD_DIM ** -0.5)
                dqkv_ref[B * b:B * (b + 1), LANES * j:LANES * (j + 1)] = dq2.astype(bf16)
                dkv[B * b:B * (b + 2), LANES * j:LANES * (j + 1)] += dk2
                dkv[B * b:B * (b + 2), ATTN_CH + LANES * j:ATTN_CH + LANES * (j + 1)] += dv2
        dkv[QB:, :] += carry[...]
        dqkv_ref[:, ATTN_CH:] = dkv[B:, :].astype(bf16)
        carry[...] = dkv[0:B, :]

    halo = lambda part: pl.BlockSpec((None, B, ATTN_CH),
                                     lambda r, i: (r, jnp.maximum((nt - 1 - i) * nsb - 1, 0), part))
    tile_spec = lambda c: pl.BlockSpec((None, QB, c), lambda r, i: (r, nt - 1 - i, 0))
    return pl.pallas_call(
        body, grid=(d, nt), name=f"attn_bwd_d{d}",
        out_shape=(jax.ShapeDtypeStruct((d, L, QKV_CH), bf16), jax.ShapeDtypeStruct((ATTN_HEADS, B, 2 * B), f32)),
        in_specs=[tile_spec(QKV_CH), halo(1), halo(2), tile_spec(ATTN_CH), tile_spec(B),
                  _resident((ATTN_HEADS, B, 2 * B))],
        out_specs=(tile_spec(QKV_CH), _acc((ATTN_HEADS, B, 2 * B))),
        scratch_shapes=[pltpu.VMEM((U * B, 2 * B), f32), pltpu.VMEM((U * B, 2 * B), f32),
                        pltpu.VMEM((U * B, 2 * B), bf16), pltpu.VMEM((U * B, 2 * B), bf16),
                        pltpu.VMEM((B + QB, KV), f32), pltpu.VMEM((B, KV), f32)],
        compiler_params=_cp(("arbitrary", "arbitrary"), 48),
    )(qkv, qkv, qkv, d_out, stats, bias)


def _attn_bwd_merge(d_a, dqkvs, d_c):
    S = d_a.shape[0]
    T = PERM_TILE
    nsl = QKV_CH // LANES
    n_p = len(DILATIONS)

    def body(da_ref, *rest):
        g_refs = rest[:n_p]
        dc_ref, dh_ref = rest[n_p:n_p + 2]
        slabs = rest[n_p + 2:]
        q0 = 2 * CONV_CH
        dh_ref[:, :q0] = da_ref[...]
        dh_ref[:, q0 + QKV_CH:] = dc_ref[...]
        for p, (d, g_ref) in enumerate(zip(DILATIONS, g_refs)):
            for r in range(d):
                for j in range(nsl):
                    _set_rows_of(slabs[p * nsl + j], r, T // d, d, g_ref[r, :, LANES * j:LANES * (j + 1)].astype(f32))
        for j in range(nsl):
            acc = slabs[j][...]
            for p in range(1, n_p):
                acc = acc + slabs[p * nsl + j][...]
            dh_ref[:, q0 + LANES * j:q0 + LANES * (j + 1)] = acc.astype(bf16)

    row = lambda c: pl.BlockSpec((T, c), lambda i: (i, 0))
    return pl.pallas_call(
        body, grid=(S // T,), name="attn_bwd_merge", out_shape=jax.ShapeDtypeStruct((S, IN_CH), bf16),
        in_specs=[row(2 * CONV_CH)] + [_perm_spec(d, QKV_CH) for d in DILATIONS] + [row(2 * GMLP_CH)],
        out_specs=row(IN_CH), scratch_shapes=_slabs(n_p * nsl, T),
        compiler_params=_cp(("parallel",), 48),
    )(d_a, *dqkvs, d_c)


def _bias_table_grad(dbias, buckets):
    n = dbias.shape[0]

    def body(db_ref, bk_ref, o_ref):
        p = pl.program_id(0)
        h = pl.program_id(1)

        @pl.when((p == 0) & (h == 0))
        def _():
            o_ref[...] = jnp.zeros_like(o_ref)
        ids = bk_ref[0]
        db = db_ref[0, 0]
        row = lax.broadcasted_iota(jnp.int32, (N_BUCKETS, 128), 0)
        lane = lax.broadcasted_iota(jnp.int32, (N_BUCKETS, 128), 1)
        upd = jnp.zeros((N_BUCKETS, 128), f32)
        for b in range(N_BUCKETS):
            s = jnp.sum(jnp.sum(jnp.where(ids == b, db, 0.0), axis=1, keepdims=True), axis=0, keepdims=True)
            upd = jnp.where((row == b) & (lane == h), s, upd)
        o_ref[...] += upd

    return pl.pallas_call(
        body, grid=(n, ATTN_HEADS), name="bias_table_grad",
        out_shape=jax.ShapeDtypeStruct((N_BUCKETS, 128), f32),
        in_specs=[pl.BlockSpec((1, 1, ATTN_BLOCK, 2 * ATTN_BLOCK), lambda p, h: (p, h, 0, 0)),
                  pl.BlockSpec((1, ATTN_BLOCK, 2 * ATTN_BLOCK), lambda p, h: (p, 0, 0))],
        out_specs=_acc((N_BUCKETS, 128)),
        compiler_params=_cp(("arbitrary", "arbitrary"), 16),
    )(dbias, buckets)


def _conv_bwd(a_in, hc, dco, dw_w, ln_g, ln_b):
    S = a_in.shape[0]
    T = 512
    hb = T // CONV_HALO
    nsteps = S // T
    R = T + CONV_HALO
    K = CONV_WIDTH

    def body(a_ref, hc_ref, hcn_ref, d_ref, dn_ref, w_ref, g_ref, be_ref,
             da_ref, dw_ref, dcb_ref, dlg_ref, dlb_ref, ext, dbuf, wacc):
        i = pl.program_id(0)

        @pl.when(i == 0)
        def _():
            wacc[...] = jnp.zeros_like(wacc)
            dcb_ref[...] = jnp.zeros_like(dcb_ref)
            dlg_ref[...] = jnp.zeros_like(dlg_ref)
            dlb_ref[...] = jnp.zeros_like(dlb_ref)
        ext[0:T, :] = hc_ref[...]
        ext[T:, :] = hcn_ref[...]
        xhat, rstd = _ln_stats(ext[...])
        hl = xhat * g_ref[...] + be_ref[...]
        ext[0:T, :] = d_ref[...]
        ext[T:, :] = dn_ref[...]
        sl_ = _sig(hl)
        dhl = ext[...] * (sl_ * (1.0 + hl * (1.0 - sl_)))
        dhc = _ln_bwd(dhl, xhat, rstd, g_ref[...])
        rowi = lax.broadcasted_iota(jnp.int32, (R, CONV_CH), 0)
        dbuf[...] = jnp.where((rowi < T) | (i < nsteps - 1), dhc, 0.0)
        dlg_ref[...] += _colsum(dhl[:T] * xhat[:T])
        dlb_ref[...] += _colsum(dhl[:T])
        dcb_ref[...] += _colsum(dbuf[pl.ds(0, T), :])
        starts = _window_rolls(range(K))
        slabs = [slice(LANES * j, LANES * (j + 1)) for j in range(CONV_CH // LANES)]

        def step(g, _):
            r0 = pl.multiple_of(g * CONV_GROUP, CONV_GROUP)
            rows = pl.ds(r0, CONV_GROUP)
            for j, cs in enumerate(slabs):
                gate_cs = slice(CONV_CH + LANES * j, CONV_CH + LANES * (j + 1))
                win = dbuf[pl.ds(r0, CONV_GROUP + CONV_HALO), cs]
                a = a_ref[rows, cs]
                sg = _sig(a_ref[rows, gate_cs])
                hg = a * sg
                dhg = jnp.zeros((CONV_GROUP, LANES), f32)
                for b, ss in starts.items():
                    rolled = win if b == 0 else pltpu.roll(win, b, 0)
                    for s in ss:
                        k = K - 1 - s
                        dk = rolled[s + b:s + b + CONV_GROUP]
                        dhg = dhg + w_ref[k:k + 1, cs] * dk
                        prod = dk * hg
                        fold = prod[0:SUBLANES]
                        for r in range(SUBLANES, CONV_GROUP, SUBLANES):
                            fold = fold + prod[r:r + SUBLANES]
                        wacc[SUBLANES * k:SUBLANES * (k + 1), cs] += fold
                da_ref[rows, cs] = (dhg * sg).astype(bf16)
                da_ref[rows, gate_cs] = (dhg * hg * (1.0 - sg)).astype(bf16)
            return 0

        lax.fori_loop(0, T // CONV_GROUP, step, 0)

        @pl.when(i == nsteps - 1)
        def _():
            for k in range(K):
                dw_ref[k:k + 1, :] = _colsum(wacc[SUBLANES * k:SUBLANES * (k + 1), :])
            dw_ref[K:, :] = jnp.zeros((32 - K, CONV_CH), f32)

    vec = jax.ShapeDtypeStruct((1, CONV_CH), f32)
    nxt = lambda i: (jnp.minimum((i + 1) * hb, nsteps * hb - 1), 0)
    return pl.pallas_call(
        body, grid=(nsteps,), name="conv_bwd",
        out_shape=(jax.ShapeDtypeStruct((S, 2 * CONV_CH), bf16), jax.ShapeDtypeStruct((32, CONV_CH), f32), vec, vec, vec),
        in_specs=[pl.BlockSpec((T, 2 * CONV_CH), lambda i: (i, 0)),
                  pl.BlockSpec((T, CONV_CH), lambda i: (i, 0)), pl.BlockSpec((CONV_HALO, CONV_CH), nxt),
                  pl.BlockSpec((T, CONV_CH), lambda i: (i, 0)), pl.BlockSpec((CONV_HALO, CONV_CH), nxt),
                  _acc((32, CONV_CH)), _acc((1, CONV_CH)), _acc((1, CONV_CH))],
        out_specs=(pl.BlockSpec((T, 2 * CONV_CH), lambda i: (i, 0)), _acc((32, CONV_CH)), _acc((1, CONV_CH)),
                   _acc((1, CONV_CH)), _acc((1, CONV_CH))),
        scratch_shapes=[pltpu.VMEM((R, CONV_CH), f32), pltpu.VMEM((R, CONV_CH), f32),
                        pltpu.VMEM((SUBLANES * 32, CONV_CH), f32)],
        compiler_params=_cp(("arbitrary",), 32),
    )(a_in, hc, hc, dco, dco, dw_w, ln_g, ln_b)


def _adamw(g, w, m, v, name):
    R, C = g.shape
    T = R
    for cand in (512, 256, 128, 64, 32, 16, 8):
        if R % cand == 0 and cand * C * 4 <= MIB:
            T = cand
            break
    c1 = 1.0 / (1.0 - ADAM_B1 ** ADAM_STEP)
    c2 = 1.0 / (1.0 - ADAM_B2 ** ADAM_STEP)

    def body(g_ref, w_ref, m_ref, v_ref, d_ref, nm_ref, nv_ref):
        gg = g_ref[...]
        nm = ADAM_B1 * m_ref[...] + (1.0 - ADAM_B1) * gg
        nv = ADAM_B2 * v_ref[...] + (1.0 - ADAM_B2) * (gg * gg)
        nm_ref[...] = nm
        nv_ref[...] = nv
        d_ref[...] = -ADAM_LR * ((nm * c1) / (jnp.sqrt(nv * c2) + ADAM_EPS) + ADAM_WD * w_ref[...])

    blk = pl.BlockSpec((T, C), lambda i: (i, 0))
    sd = jax.ShapeDtypeStruct((R, C), f32)
    return pl.pallas_call(
        body, grid=(R // T,), name=name, out_shape=(sd, sd, sd), in_specs=[blk] * 4, out_specs=(blk, blk, blk),
        compiler_params=_cp(("parallel",), 48),
    )(g, w, m, v)


def _pad_rows(a, rows):
    return jnp.pad(a, ((0, rows - a.shape[0]), (0, 0)))


def _local_step(x, target, wb, late_weights, sp, sink):
    buckets = jnp.asarray(_bucket_ids())
    bias = _bias_build(sp["rel_bias_table"], buckets)
    wb = dict(wb)
    saved = []
    xl = x
    for l in range(DEPTH):
        vec = lambda name: sp[name][l][None, :]
        a_in, c_in, *qkv = _inproj_fwd(xl, wb["w_in"][l], vec("b_in"))
        conv_w = _pad_rows(sp["conv_dw_w"][l], 32)
        conv_out, hc = _conv_fwd(a_in, conv_w, vec("conv_dw_b"), vec("conv_ln_g"), vec("conv_ln_b"))
        attn_out, lse = _attn_fwd(qkv, bias)
        bs_t = sp["gmlp_b_s"][l].T
        gm_out = _gmlp_fwd(c_in, vec("gmlp_ln_g"), vec("gmlp_ln_b"), sp["gmlp_w_s"][l], bs_t)
        if l == 0:
            wb.update(late_weights(gm_out))
        cat, z1, x1b = _outproj_ln_fwd(conv_out, attn_out, gm_out, wb["w_out"][l], vec("b_out"), xl,
                                           vec("ln1_g"), vec("ln1_b"))
        fconv_w = _pad_rows(sp["ffn_conv_w"][l], 8)
        hfb, fhc, act = _ffn_up_gate_fwd(x1b, wb["ffn_w_up"][l], vec("ffn_b_up"), fconv_w, vec("ffn_conv_b"))
        down = (act, wb["ffn_w_down"][l], vec("ffn_b_down"), z1, vec("ln1_g"), vec("ln1_b"), vec("ln2_g"),
                vec("ln2_b"))
        z2, x2 = _ffn_down_ln_fwd(*down) if l < DEPTH - 1 else (None, None)
        saved.append(dict(x=xl, a_in=a_in, qkv=qkv, c_in=c_in, hc=hc, attn_out=attn_out, lse=lse, cat=cat, z1=z1,
                          x1b=x1b, hfb=hfb, fhc=fhc, act=act, z2=z2, conv_w=conv_w, fconv_w=fconv_w, bs_t=bs_t))
        xl = x2

    grads = {}
    per_layer = {k: [None] * DEPTH for k in (
        "b_in", "conv_dw_w", "conv_dw_b", "conv_ln_g", "conv_ln_b", "gmlp_ln_g", "gmlp_ln_b", "gmlp_w_s",
        "gmlp_b_s", "b_out", "ln1_g", "ln1_b", "ffn_b_up", "ffn_conv_w", "ffn_conv_b", "ffn_b_down", "ln2_g", "ln2_b")}
    dbias_all = []
    dz2, dz2b, loss_part, dg2, db2 = _ffn_down_ln_loss(*down, target)
    loss = jnp.sum(loss_part)
    grad_x = None
    tok = jnp.zeros((), f32)
    for l in reversed(range(DEPTH)):
        sv = saved[l]
        vec = lambda name: sp[name][l][None, :] + tok
        per_layer["ln2_g"][l] = dg2[0]
        per_layer["ln2_b"][l] = db2[0]
        dw_down, db_down = _wgrad(sv["act"], dz2b, 512, "ffn_down_wgrad")
        tok = sink.put("ffn_w_down", l, dw_down, tok)
        per_layer["ffn_b_down"][l] = db_down[0]
        dhf, dfcw, dfcb = _ffn_down_gate_bwd(dz2b, wb["ffn_w_down"][l], sv["hfb"], sv["fhc"], sv["fconv_w"])
        per_layer["ffn_conv_w"][l] = dfcw[:FFN_CONV_WIDTH]
        per_layer["ffn_conv_b"][l] = dfcb[0]
        dw_up, db_up = _wgrad(sv["x1b"], dhf, 1408, "ffn_up_wgrad")
        tok = sink.put("ffn_w_up", l, dw_up, tok)
        per_layer["ffn_b_up"][l] = db_up[0]
        dz1, dz1b, dg1, db1 = _dgrad_ln_bwd(dhf, wb["ffn_w_up"][l], dz2, sv["z1"], vec("ln1_g"), "ffn_up_dgrad_ln")
        per_layer["ln1_g"][l] = dg1[0]
        per_layer["ln1_b"][l] = db1[0]
        dw_out, db_out = _wgrad(sv["cat"], dz1b, D_MODEL, "outproj_wgrad")
        tok = sink.put("w_out", l, dw_out, tok)
        per_layer["b_out"][l] = db_out[0]
        dco, dgo, *perm = _outproj_dgrad(dz1b, wb["w_out"][l], sv["attn_out"], sv["lse"])
        d_outs, stats = perm[:len(DILATIONS)], perm[len(DILATIONS):]
        d_c, dglg, dglb, dws, dbs = _gmlp_bwd(sv["c_in"], dgo, vec("gmlp_ln_g"), vec("gmlp_ln_b"), sp["gmlp_w_s"][l],
                                              sv["bs_t"])
        per_layer["gmlp_ln_g"][l] = dglg[0]
        per_layer["gmlp_ln_b"][l] = dglb[0]
        per_layer["gmlp_w_s"][l] = dws
        per_layer["gmlp_b_s"][l] = dbs[:, :GMLP_GROUPS].T
        dqkvs = []
        for p, d in enumerate(DILATIONS):
            dqkv, dbias = _attn_bwd_pattern(sv["qkv"][p], d_outs[p], stats[p], bias[p], d)
            dqkvs.append(dqkv)
            dbias_all.append(dbias)
        d_a, dcw, dcb, dclg, dclb = _conv_bwd(sv["a_in"], sv["hc"], dco, sv["conv_w"], vec("conv_ln_g"),
                                              vec("conv_ln_b"))
        per_layer["conv_dw_w"][l] = dcw[:CONV_WIDTH]
        per_layer["conv_dw_b"][l] = dcb[0]
        per_layer["conv_ln_g"][l] = dclg[0]
        per_layer["conv_ln_b"][l] = dclb[0]
        dh = _attn_bwd_merge(d_a, dqkvs, d_c)
        dw_in, db_in = _wgrad(sv["x"], dh, IN_CH, "inproj_wgrad")
        tok = sink.put("w_in", l, dw_in, tok)
        per_layer["b_in"][l] = db_in[0]
        if l > 0:
            pv = saved[l - 1]
            dz2, dz2b, dg2, db2 = _dgrad_ln_bwd(dh, wb["w_in"][l], dz1, pv["z2"], sp["ln2_g"][l - 1][None, :] + tok,
                                                "inproj_dgrad_ln")
        else:
            grad_x = _dgrad_ln_bwd(dh, wb["w_in"][l], dz1, None, None, "inproj_dgrad")
    for k, v in per_layer.items():
        grads[k] = jnp.stack(v)
    dbias_cat = jnp.stack(dbias_all)
    bk_cat = jnp.concatenate([buckets] * DEPTH, axis=0)
    grads["rel_bias_table"] = _bias_table_grad(dbias_cat, bk_cat)[:, :ATTN_HEADS]
    return loss, grad_x, grads, sink.finish(grad_x)


N_CHIPS = 4
BIG = {"w_in": (D_MODEL, IN_CH, 1), "w_out": (D_MODEL, D_MODEL, 0),
       "ffn_w_up": (D_MODEL, 2 * D_FF, 1), "ffn_w_down": (D_FF, D_MODEL, 0)}
SMALL = ("b_in", "conv_dw_w", "conv_dw_b", "conv_ln_g", "conv_ln_b", "rel_bias_table", "gmlp_ln_g", "gmlp_ln_b",
         "gmlp_w_s", "gmlp_b_s", "b_out", "ln1_g", "ln1_b", "ffn_b_up", "ffn_conv_w", "ffn_conv_b", "ffn_b_down",
         "ln2_g", "ln2_b")
SMALL_SHARDED = ("conv_dw_w", "ffn_conv_w")
WEIGHTS = ("w_in", "b_in", "conv_dw_w", "conv_dw_b", "conv_ln_g", "conv_ln_b", "rel_bias_table", "gmlp_ln_g",
           "gmlp_ln_b", "gmlp_w_s", "gmlp_b_s", "w_out", "b_out", "ln1_g", "ln1_b", "ffn_w_up", "ffn_b_up",
           "ffn_conv_w", "ffn_conv_b", "ffn_w_down", "ffn_b_down", "ln2_g", "ln2_b")
ANY = pl.BlockSpec(memory_space=pl.ANY)


def _position():
    return lax.axis_index("x"), lax.axis_index("y"), lax.axis_index("c")


def _other_chips(x, y):
    return [(1 - x, y), (x, 1 - y), (1 - x, 1 - y)]


def _cast_bf16(a):
    R, C = a.shape
    T = 128

    def body(a_ref, o_ref):
        o_ref[...] = a_ref[...].astype(bf16)

    return pl.pallas_call(
        body, grid=(R // T,), name="cast_bf16", out_shape=jax.ShapeDtypeStruct((R, C), bf16),
        in_specs=[pl.BlockSpec((T, C), lambda i: (i, 0))], out_specs=pl.BlockSpec((T, C), lambda i: (i, 0)),
        compiler_params=_cp(("parallel",), 16),
    )(a)


def _chip_slot(ref, name, l, p):
    K, N, ax = BIG[name]
    if ax == 1:
        sz = N // N_CHIPS
        return ref.at[l, :, pl.ds(pl.multiple_of(p * sz, 128), sz)]
    sz = K // N_CHIPS
    return ref.at[l, pl.ds(pl.multiple_of(p * sz, 16), sz), :]


def _gather_weights(shards, conv_w, fconv_w):
    names = list(shards)
    n_big = len(names)
    n_t = n_big + 2
    n_chip = 3 * n_t
    n_pass = 3 * n_big

    def body(*refs):
        ins = refs[:n_t]
        outs = refs[n_t:2 * n_t]
        send_sems, recv_sems, pass_send, pass_recv, local_sems = refs[2 * n_t:]
        x, y, c = _position()
        me = 2 * x + y
        chips = _other_chips(x, y)

        def src(t):
            return ins[t].at[c] if t < n_big else ins[t]

        def slot(t, l, p):
            return _chip_slot(outs[t], names[t], l, p) if t < n_big else outs[t].at[p]

        locs, cps = [], []
        for t in range(n_t):
            for l in (range(DEPTH) if t < n_big else (0,)):
                loc = pltpu.make_async_copy(ins[t].at[l] if t < n_big else ins[t], slot(t, l, me),
                                            local_sems.at[DEPTH * t + l])
                loc.start()
                locs.append(loc)
            for k, (px, py) in enumerate(chips):
                cp = pltpu.make_async_remote_copy(
                    src_ref=src(t), dst_ref=slot(t, c, me), send_sem=send_sems.at[3 * t + k],
                    recv_sem=recv_sems.at[3 * t + k], device_id=(px, py, c), device_id_type=MESH_ID)
                cp.start()
                cps.append(cp)
        for t in range(n_t):
            for k, (px, py) in enumerate(chips):
                landed = slot(t, c, 2 * px + py)
                pltpu.make_async_remote_copy(
                    src_ref=src(t), dst_ref=landed, send_sem=send_sems.at[3 * t + k],
                    recv_sem=recv_sems.at[3 * t + k], device_id=(px, py, c), device_id_type=MESH_ID).wait_recv()
                if t < n_big:
                    cp = pltpu.make_async_remote_copy(
                        src_ref=landed, dst_ref=landed, send_sem=pass_send.at[3 * t + k],
                        recv_sem=pass_recv.at[3 * t + k], device_id=(x, y, 1 - c), device_id_type=MESH_ID)
                    cp.start()
                    cps.append(cp)
        for t in range(n_big):
            for k, (px, py) in enumerate(chips):
                from_sibling = slot(t, 1 - c, 2 * px + py)
                pltpu.make_async_remote_copy(
                    src_ref=from_sibling, dst_ref=from_sibling, send_sem=pass_send.at[3 * t + k],
                    recv_sem=pass_recv.at[3 * t + k], device_id=(x, y, 1 - c), device_id_type=MESH_ID).wait_recv()
        for cp in cps:
            cp.wait_send()
        for loc in locs:
            loc.wait()

    ins = [shards[n] for n in names] + [conv_w, fconv_w]
    out_shape = [jax.ShapeDtypeStruct((DEPTH, BIG[n][0], BIG[n][1]), bf16) for n in names]
    out_shape += [jax.ShapeDtypeStruct((N_CHIPS,) + conv_w.shape, f32), jax.ShapeDtypeStruct((N_CHIPS,) + fconv_w.shape, f32)]
    outs = pl.pallas_call(
        body, name="gather_weights", out_shape=tuple(out_shape), in_specs=[ANY] * n_t, out_specs=tuple([ANY] * n_t),
        scratch_shapes=[pltpu.SemaphoreType.DMA((n_chip,)), pltpu.SemaphoreType.DMA((n_chip,)),
                        pltpu.SemaphoreType.DMA((n_pass,)), pltpu.SemaphoreType.DMA((n_pass,)),
                        pltpu.SemaphoreType.DMA((DEPTH * n_t,))],
    )(*ins)
    return dict(zip(names, outs[:n_big])), outs[-2], outs[-1]


LATE_WEIGHTS = ("w_out", "ffn_w_up", "ffn_w_down")
HBM = pl.BlockSpec(memory_space=pltpu.HBM)
SEM = pl.BlockSpec(memory_space=pltpu.SEMAPHORE)


def _cast_into_full(shard, name, chip_arr):
    K, N, ax = BIG[name]
    k, n = _shard_shape(name)
    T = 64
    nrt = k // T

    def body(p_ref, a_ref, o_ref):
        o_ref[...] = a_ref[...].astype(bf16)

    if ax == 1:
        out_spec = pl.BlockSpec((None, T, n), lambda l, i, p: (l, i, p[0]))
    else:
        out_spec = pl.BlockSpec((None, T, n), lambda l, i, p: (l, p[0] * nrt + i, 0))
    return pl.pallas_call(
        body, name="cast_into_full", out_shape=jax.ShapeDtypeStruct((DEPTH, K, N), bf16),
        grid_spec=pltpu.PrefetchScalarGridSpec(
            num_scalar_prefetch=1, grid=(DEPTH, nrt),
            in_specs=[pl.BlockSpec((None, T, n), lambda l, i, p: (l, i, 0))], out_specs=out_spec),
        compiler_params=_cp(("parallel", "parallel"), 16),
    )(chip_arr, shard)


def _late_copies(refs, send_sems, recv_sems):
    x, y, c = _position()
    me = 2 * x + y
    idx = 0
    for ref, name in zip(refs, LATE_WEIGHTS):
        for l in range(DEPTH):
            for px, py in _other_chips(x, y):
                def copy(p, ref=ref, name=name, l=l, px=px, py=py, idx=idx):
                    part = _chip_slot(ref, name, l, p)
                    return pltpu.make_async_remote_copy(
                        src_ref=part, dst_ref=part, send_sem=send_sems.at[idx], recv_sem=recv_sems.at[idx],
                        device_id=(px, py, c), device_id_type=MESH_ID)
                yield copy(me), copy(2 * px + py)
                idx += 1


N_LATE_COPIES = 3 * DEPTH * len(LATE_WEIGHTS)


def _gather_start(fulls, after):
    n = len(fulls)

    def body(*refs):
        ins = refs[:n]
        send_sems, recv_sems = refs[n + 1:n + 3]
        token = refs[-1]
        for sent, _ in _late_copies(ins, send_sems, recv_sems):
            sent.start()
        token[...] = jnp.zeros_like(token)

    outs = pl.pallas_call(
        body, name="gather_start",
        out_shape=(pltpu.SemaphoreType.DMA((N_LATE_COPIES,)), pltpu.SemaphoreType.DMA((N_LATE_COPIES,)))
        + tuple(pltpu.HBM(f.shape, f.dtype) for f in fulls) + (jax.ShapeDtypeStruct((SUBLANES, LANES), f32),),
        in_specs=(HBM,) * n + (ANY,),
        out_specs=(SEM, SEM) + (HBM,) * n + (pl.BlockSpec(memory_space=pltpu.VMEM),),
        input_output_aliases={t: 2 + t for t in range(n)},
        compiler_params=pltpu.CompilerParams(has_side_effects=pltpu.SideEffectType.DATAFLOW_SIDE_EFFECTING),
    )(*[pltpu.with_memory_space_constraint(f, pltpu.HBM) for f in fulls], after)
    return outs[0], outs[1], outs[2:2 + n], outs[-1]


def _gather_wait(send_sems, recv_sems, fulls, after):
    n = len(fulls)

    def body(*refs):
        ins = refs[:n]
        send_ref, recv_ref = refs[n:n + 2]
        for sent, landed in _late_copies(ins, send_ref, recv_ref):
            sent.wait_send()
            landed.wait_recv()

    return pl.pallas_call(
        body, name="gather_wait", out_shape=tuple(pltpu.HBM(f.shape, f.dtype) for f in fulls),
        in_specs=(HBM,) * n + (SEM, SEM, ANY), out_specs=(HBM,) * n,
        input_output_aliases={t: t for t in range(n)},
        compiler_params=pltpu.CompilerParams(has_side_effects=pltpu.SideEffectType.DATAFLOW_SIDE_EFFECTING),
    )(*fulls, send_sems, recv_sems, after)


def _half(ref, name, c):
    K, N, ax = BIG[name]
    if ax == 1:
        return ref.at[pl.ds(pl.multiple_of(c * (K // 2), 8), K // 2), :]
    return ref.at[:, pl.ds(pl.multiple_of(c * (N // 2), 128), N // 2)]


def _half_shape(name):
    K, N, ax = BIG[name]
    return (K // 2, N) if ax == 1 else (K, N // 2)


def _shard_of_half(ref, name, q):
    K, N, ax = BIG[name]
    if ax == 1:
        sz = N // N_CHIPS
        return ref.at[:, pl.ds(pl.multiple_of(q * sz, 128), sz)]
    sz = K // N_CHIPS
    return ref.at[pl.ds(pl.multiple_of(q * sz, 16), sz), :]


def _shard_half_shape(name):
    K, N, ax = BIG[name]
    return (K // 2, N // N_CHIPS) if ax == 1 else (K // N_CHIPS, N // 2)


def _shard_shape(name):
    K, N, ax = BIG[name]
    return (K, N // N_CHIPS) if ax == 1 else (K // N_CHIPS, N)


def _pair_copies(names, srcs, lands, send_sems, recv_sems):
    x, y, c = _position()
    for idx, (name, src, land) in enumerate(zip(names, srcs, lands)):
        yield pltpu.make_async_remote_copy(
            src_ref=_half(src, name, 1 - c), dst_ref=land, send_sem=send_sems.at[idx], recv_sem=recv_sems.at[idx],
            device_id=(x, y, 1 - c), device_id_type=MESH_ID)


def _pair_exchange_start(tag, tensors):
    names = [n for n, _ in tensors]
    n = len(tensors)
    lands = [lax.empty(_half_shape(nm), f32) for nm in names]

    def body(*refs):
        for cp in _pair_copies(names, refs[:n], refs[n:2 * n], refs[2 * n], refs[2 * n + 1]):
            cp.start()
        refs[-1][...] = jnp.zeros_like(refs[-1])

    args = [g for _, g in tensors] + lands
    outs = pl.pallas_call(
        body, name="grad_pair_start_" + tag,
        out_shape=(pltpu.SemaphoreType.DMA((n,)), pltpu.SemaphoreType.DMA((n,)))
        + tuple(pltpu.HBM(a.shape, a.dtype) for a in args) + (jax.ShapeDtypeStruct((SUBLANES, LANES), f32),),
        in_specs=(HBM,) * (2 * n), out_specs=(SEM, SEM) + (HBM,) * (2 * n) + (pl.BlockSpec(memory_space=pltpu.VMEM),),
        input_output_aliases={t: 2 + t for t in range(2 * n)},
        compiler_params=pltpu.CompilerParams(has_side_effects=pltpu.SideEffectType.DATAFLOW_SIDE_EFFECTING),
    )(*[pltpu.with_memory_space_constraint(a, pltpu.HBM) for a in args])
    return (tag, names, outs[0], outs[1], outs[2:2 + 2 * n]), outs[-1]


def _pair_exchange_wait(state, after):
    tag, names, send_sems, recv_sems, bufs = state
    n = len(names)

    def body(*refs):
        for cp in _pair_copies(names, refs[:n], refs[n:2 * n], refs[2 * n], refs[2 * n + 1]):
            cp.wait_send()
            cp.wait_recv()

    outs = pl.pallas_call(
        body, name="grad_pair_wait_" + tag, out_shape=tuple(pltpu.HBM(a.shape, a.dtype) for a in bufs),
        in_specs=(HBM,) * (2 * n) + (SEM, SEM, ANY), out_specs=(HBM,) * (2 * n),
        input_output_aliases={t: t for t in range(2 * n)},
        compiler_params=pltpu.CompilerParams(has_side_effects=pltpu.SideEffectType.DATAFLOW_SIDE_EFFECTING),
    )(*bufs, send_sems, recv_sems, after)
    return list(zip(names, outs[:n], outs[n:]))


def _pair_add(g, rcv, name, c_arr):
    K, N, ax = BIG[name]
    hr, hc = _half_shape(name)
    T = 128
    nrt = hr // T

    def body(c_ref, g_ref, r_ref, o_ref):
        o_ref[...] = (g_ref[...] + r_ref[...]).astype(bf16)

    if ax == 1:
        g_spec = pl.BlockSpec((T, hc), lambda i, c: (c[0] * nrt + i, 0))
    else:
        g_spec = pl.BlockSpec((T, hc), lambda i, c: (i, c[0]))
    plain = pl.BlockSpec((T, hc), lambda i, c: (i, 0))
    return pl.pallas_call(
        body, name="grad_pair_add", out_shape=jax.ShapeDtypeStruct((hr, hc), bf16),
        grid_spec=pltpu.PrefetchScalarGridSpec(num_scalar_prefetch=1, grid=(nrt,), in_specs=[g_spec, plain],
                                               out_specs=plain),
        compiler_params=_cp(("parallel",), 32),
    )(c_arr, g, rcv)


def _chip_copies(names, srcs, lands, send_sems, recv_sems):
    x, y, c = _position()
    me = 2 * x + y
    idx = 0
    for name, src, land in zip(names, srcs, lands):
        for px, py in _other_chips(x, y):
            def copy(q, row, name=name, src=src, land=land, px=px, py=py, idx=idx):
                return pltpu.make_async_remote_copy(
                    src_ref=_shard_of_half(src, name, q), dst_ref=land.at[row], send_sem=send_sems.at[idx],
                    recv_sem=recv_sems.at[idx], device_id=(px, py, c), device_id_type=MESH_ID)
            yield copy(2 * px + py, me), copy(me, 2 * px + py)
            idx += 1


def _chip_exchange_start(tag, tensors):
    names = [n for n, _ in tensors]
    n = len(tensors)
    lands = [lax.empty((N_CHIPS,) + _shard_half_shape(nm), g.dtype) for nm, g in tensors]

    def body(*refs):
        send_sems, recv_sems = refs[2 * n:2 * n + 2]
        for sent, _ in _chip_copies(names, refs[:n], refs[n:2 * n], send_sems, recv_sems):
            sent.start()
        refs[-1][...] = jnp.zeros_like(refs[-1])

    args = [g for _, g in tensors] + lands
    outs = pl.pallas_call(
        body, name="grad_chip_start_" + tag,
        out_shape=(pltpu.SemaphoreType.DMA((3 * n,)), pltpu.SemaphoreType.DMA((3 * n,)))
        + tuple(pltpu.HBM(a.shape, a.dtype) for a in args) + (jax.ShapeDtypeStruct((SUBLANES, LANES), f32),),
        in_specs=(HBM,) * (2 * n), out_specs=(SEM, SEM) + (HBM,) * (2 * n) + (pl.BlockSpec(memory_space=pltpu.VMEM),),
        input_output_aliases={t: 2 + t for t in range(2 * n)},
        compiler_params=pltpu.CompilerParams(has_side_effects=pltpu.SideEffectType.DATAFLOW_SIDE_EFFECTING),
    )(*[pltpu.with_memory_space_constraint(a, pltpu.HBM) for a in args])
    return (tag, names, outs[0], outs[1], outs[2:2 + 2 * n]), outs[-1]


def _chip_exchange_wait(state, after):
    tag, names, send_sems, recv_sems, bufs = state
    n = len(names)

    def body(*refs):
        for sent, landed in _chip_copies(names, refs[:n], refs[n:2 * n], refs[2 * n], refs[2 * n + 1]):
            sent.wait_send()
            landed.wait_recv()

    outs = pl.pallas_call(
        body, name="grad_chip_wait_" + tag, out_shape=tuple(pltpu.HBM(a.shape, a.dtype) for a in bufs),
        in_specs=(HBM,) * (2 * n) + (SEM, SEM, ANY), out_specs=(HBM,) * (2 * n),
        input_output_aliases={t: t for t in range(2 * n)},
        compiler_params=pltpu.CompilerParams(has_side_effects=pltpu.SideEffectType.DATAFLOW_SIDE_EFFECTING),
    )(*bufs, send_sems, recv_sems, after)
    return list(zip(names, outs[:n], outs[n:]))


def _sum_chips(name, half, land, chip_arr):
    K, N, ax = BIG[name]
    R, C = _shard_half_shape(name)
    T = 64
    nrt = R // T

    def body(p_ref, own_ref, land_ref, o_ref):
        parts = [jnp.where(p_ref[0] == q, own_ref[...], land_ref[q]).astype(f32) for q in range(N_CHIPS)]
        o_ref[...] = ((parts[0] + parts[1]) + parts[2]) + parts[3]

    if ax == 1:
        own_spec = pl.BlockSpec((T, C), lambda i, p: (i, p[0]))
    else:
        own_spec = pl.BlockSpec((T, C), lambda i, p: (p[0] * nrt + i, 0))
    return pl.pallas_call(
        body, name="grad_sum_chips", out_shape=jax.ShapeDtypeStruct((R, C), f32),
        grid_spec=pltpu.PrefetchScalarGridSpec(
            num_scalar_prefetch=1, grid=(nrt,),
            in_specs=[own_spec, pl.BlockSpec((N_CHIPS, T, C), lambda i, p: (0, i, 0))],
            out_specs=pl.BlockSpec((T, C), lambda i, p: (i, 0))),
        compiler_params=_cp(("parallel",), 32),
    )(chip_arr, half, land)


def _pair_swap(halves):
    n_t = len(halves)

    def body(*refs):
        ins = refs[:n_t]
        outs = refs[n_t:2 * n_t]
        send_sems, recv_sems = refs[2 * n_t:]
        x, y, c = _position()
        cps = []
        for t in range(n_t):
            cp = pltpu.make_async_remote_copy(
                src_ref=ins[t], dst_ref=outs[t], send_sem=send_sems.at[t], recv_sem=recv_sems.at[t],
                device_id=(x, y, 1 - c), device_id_type=MESH_ID)
            cp.start()
            cps.append(cp)
        for cp in cps:
            cp.wait()

    return pl.pallas_call(
        body, name="grad_pair_swap", out_shape=tuple(jax.ShapeDtypeStruct(h.shape, h.dtype) for h in halves),
        in_specs=[ANY] * n_t, out_specs=tuple([ANY] * n_t),
        scratch_shapes=[pltpu.SemaphoreType.DMA((n_t,)), pltpu.SemaphoreType.DMA((n_t,))],
    )(*halves)


def _adamw_halves(own, other, w, m, v, name, l, c_arr, prev):
    K, N, ax = BIG[name]
    R, C = _shard_shape(name)
    hr, hc = _shard_half_shape(name)
    T = 64
    nrt = hr // T
    c1 = 1.0 / (1.0 - ADAM_B1 ** ADAM_STEP)
    c2 = 1.0 / (1.0 - ADAM_B2 ** ADAM_STEP)

    def body(c_ref, own_ref, oth_ref, w_ref, m_ref, v_ref, *rest):
        g_ref, d_ref, nm_ref, nv_ref = rest[-4:]
        gg = jnp.where(pl.program_id(0) == c_ref[0], own_ref[...], oth_ref[...])
        nm = ADAM_B1 * m_ref[...] + (1.0 - ADAM_B1) * gg
        nv = ADAM_B2 * v_ref[...] + (1.0 - ADAM_B2) * (gg * gg)
        g_ref[...] = gg
        nm_ref[...] = nm
        nv_ref[...] = nv
        d_ref[...] = -ADAM_LR * ((nm * c1) / (jnp.sqrt(nv * c2) + ADAM_EPS) + ADAM_WD * w_ref[...])

    half = pl.BlockSpec((T, hc), lambda h, i, c: (i, 0))
    if ax == 1:
        full = pl.BlockSpec((None, T, hc), lambda h, i, c: (l, h * nrt + i, 0))
    else:
        full = pl.BlockSpec((None, T, hc), lambda h, i, c: (l, i, h))
    sd = jax.ShapeDtypeStruct((DEPTH, R, C), f32)
    args = [c_arr, own, other, w, m, v]
    in_specs = [half, half, full, full, full]
    aliases = {}
    if prev is not None:
        args += list(prev)
        in_specs += [ANY] * 4
        aliases = {6 + k: k for k in range(4)}
    return pl.pallas_call(
        body, name="adamw_" + name, out_shape=(sd, sd, sd, sd),
        grid_spec=pltpu.PrefetchScalarGridSpec(num_scalar_prefetch=1, grid=(2, nrt), in_specs=in_specs,
                                               out_specs=(full, full, full, full)),
        input_output_aliases=aliases,
        compiler_params=_cp(("arbitrary", "arbitrary"), 32),
    )(*args)


class _GradExchange:
    GROUPS = (("l1", tuple((n, DEPTH - 1) for n in BIG)),
              ("l0_ffn", (("ffn_w_down", 0), ("ffn_w_up", 0))),
              ("l0_mix", (("w_out", 0), ("w_in", 0))))

    def __init__(self):
        self.c_arr = jnp.reshape(lax.axis_index("c"), (1,)).astype(jnp.int32)
        self.chip_arr = jnp.reshape(2 * lax.axis_index("x") + lax.axis_index("y"), (1,)).astype(jnp.int32)
        self.grads = {}
        self.pair_started = {}
        self.chip_started = {}

    def _advance(self, after, tok):
        for tag, _ in self.GROUPS:
            if tag not in self.pair_started or tag in self.chip_started:
                continue
            arrived = _pair_exchange_wait(self.pair_started[tag], after)
            pair = [(n, _pair_add(g, r, n, self.c_arr)) for n, g, r in arrived]
            self.chip_started[tag], token = _chip_exchange_start(tag, pair)
            tok = tok + token[0, 0]
        return tok

    def put(self, name, layer, g, tok):
        self.grads[(name, layer)] = g
        tok = self._advance(g, tok)
        for tag, keys in self.GROUPS:
            if tag in self.pair_started or not all(k in self.grads for k in keys):
                continue
            self.pair_started[tag], token = _pair_exchange_start(tag, [(n, self.grads[(n, l)]) for n, l in keys])
            tok = tok + token[0, 0]
        return tok

    def finish(self, after):
        self._advance(after, jnp.zeros((), f32))
        keys, own = [], []
        for tag, group in self.GROUPS:
            landed = _chip_exchange_wait(self.chip_started[tag], after)
            own += [_sum_chips(n, half, land, self.chip_arr) for n, half, land in landed]
            keys += list(group)
        other = _pair_swap(own)
        return dict(zip(keys, zip(own, other)))


def _small_allreduce(buf):
    R = buf.shape[0]

    def body(in_ref, out_ref, sibling, slots, send_sems, recv_sems):
        x, y, c = _position()
        me = 2 * x + y
        swap = pltpu.make_async_remote_copy(
            src_ref=in_ref, dst_ref=sibling, send_sem=send_sems.at[0], recv_sem=recv_sems.at[0],
            device_id=(x, y, 1 - c), device_id_type=MESH_ID)
        swap.start()
        swap.wait()
        slots[me] = in_ref[...] + sibling[...]
        cps = []
        for k, (px, py) in enumerate(_other_chips(x, y)):
            cp = pltpu.make_async_remote_copy(
                src_ref=slots.at[me], dst_ref=slots.at[me], send_sem=send_sems.at[1 + k], recv_sem=recv_sems.at[1 + k],
                device_id=(px, py, c), device_id_type=MESH_ID)
            cp.start()
            cps.append(cp)
        for k, (px, py) in enumerate(_other_chips(x, y)):
            pltpu.make_async_remote_copy(
                src_ref=slots.at[me], dst_ref=slots.at[2 * px + py], send_sem=send_sems.at[1 + k],
                recv_sem=recv_sems.at[1 + k], device_id=(px, py, c), device_id_type=MESH_ID).wait_recv()
        for cp in cps:
            cp.wait_send()
        out_ref[...] = ((slots[0] + slots[1]) + slots[2]) + slots[3]

    vm = pl.BlockSpec(memory_space=pltpu.VMEM)
    return pl.pallas_call(
        body, name="small_allreduce", out_shape=jax.ShapeDtypeStruct((R, 128), f32), in_specs=[vm], out_specs=vm,
        scratch_shapes=[pltpu.VMEM((R, 128), f32), pltpu.VMEM((N_CHIPS, R, 128), f32),
                        pltpu.SemaphoreType.DMA((N_CHIPS,)), pltpu.SemaphoreType.DMA((N_CHIPS,))],
        compiler_params=pltpu.CompilerParams(vmem_limit_bytes=40 * MIB),
    )(buf)


PACK_UNIT = 1024


def _pack(arrs):
    parts = []
    for a in arrs:
        flat = a.reshape(-1)
        n = -(-flat.shape[0] // PACK_UNIT) * PACK_UNIT
        parts.append(jnp.pad(flat, (0, n - flat.shape[0])))
    return jnp.concatenate(parts).reshape(-1, 128)


def _unpack(buf, shapes):
    flat = buf.reshape(-1)
    out, off = [], 0
    for shp in shapes:
        n = int(np.prod(shp))
        out.append(flat[off:off + n].reshape(shp))
        off += -(-n // PACK_UNIT) * PACK_UNIT
    return out


def kernel(x, w_in, b_in, conv_dw_w, conv_dw_b, conv_ln_g, conv_ln_b, rel_bias_table, gmlp_ln_g, gmlp_ln_b, gmlp_w_s, gmlp_b_s, w_out, b_out, ln1_g, ln1_b, ffn_w_up, ffn_b_up, ffn_conv_w, ffn_conv_b, ffn_w_down, ffn_b_down, ln2_g, ln2_b, loss_target, m_w_in, m_b_in, m_conv_dw_w, m_conv_dw_b, m_conv_ln_g, m_conv_ln_b, m_rel_bias_table, m_gmlp_ln_g, m_gmlp_ln_b, m_gmlp_w_s, m_gmlp_b_s, m_w_out, m_b_out, m_ln1_g, m_ln1_b, m_ffn_w_up, m_ffn_b_up, m_ffn_conv_w, m_ffn_conv_b, m_ffn_w_down, m_ffn_b_down, m_ln2_g, m_ln2_b, v_w_in, v_b_in, v_conv_dw_w, v_conv_dw_b, v_conv_ln_g, v_conv_ln_b, v_rel_bias_table, v_gmlp_ln_g, v_gmlp_ln_b, v_gmlp_w_s, v_gmlp_b_s, v_w_out, v_b_out, v_ln1_g, v_ln1_b, v_ffn_w_up, v_ffn_b_up, v_ffn_conv_w, v_ffn_conv_b, v_ffn_w_down, v_ffn_b_down, v_ln2_g, v_ln2_b):
    w = dict(w_in=w_in, b_in=b_in, conv_dw_w=conv_dw_w, conv_dw_b=conv_dw_b, conv_ln_g=conv_ln_g, conv_ln_b=conv_ln_b,
             rel_bias_table=rel_bias_table, gmlp_ln_g=gmlp_ln_g, gmlp_ln_b=gmlp_ln_b, gmlp_w_s=gmlp_w_s,
             gmlp_b_s=gmlp_b_s, w_out=w_out, b_out=b_out, ln1_g=ln1_g, ln1_b=ln1_b, ffn_w_up=ffn_w_up,
             ffn_b_up=ffn_b_up, ffn_conv_w=ffn_conv_w, ffn_conv_b=ffn_conv_b, ffn_w_down=ffn_w_down,
             ffn_b_down=ffn_b_down, ln2_g=ln2_g, ln2_b=ln2_b)
    m = dict(w_in=m_w_in, b_in=m_b_in, conv_dw_w=m_conv_dw_w, conv_dw_b=m_conv_dw_b, conv_ln_g=m_conv_ln_g,
             conv_ln_b=m_conv_ln_b, rel_bias_table=m_rel_bias_table, gmlp_ln_g=m_gmlp_ln_g, gmlp_ln_b=m_gmlp_ln_b,
             gmlp_w_s=m_gmlp_w_s, gmlp_b_s=m_gmlp_b_s, w_out=m_w_out, b_out=m_b_out, ln1_g=m_ln1_g, ln1_b=m_ln1_b,
             ffn_w_up=m_ffn_w_up, ffn_b_up=m_ffn_b_up, ffn_conv_w=m_ffn_conv_w, ffn_conv_b=m_ffn_conv_b,
             ffn_w_down=m_ffn_w_down, ffn_b_down=m_ffn_b_down, ln2_g=m_ln2_g, ln2_b=m_ln2_b)
    v = dict(w_in=v_w_in, b_in=v_b_in, conv_dw_w=v_conv_dw_w, conv_dw_b=v_conv_dw_b, conv_ln_g=v_conv_ln_g,
             conv_ln_b=v_conv_ln_b, rel_bias_table=v_rel_bias_table, gmlp_ln_g=v_gmlp_ln_g, gmlp_ln_b=v_gmlp_ln_b,
             gmlp_w_s=v_gmlp_w_s, gmlp_b_s=v_gmlp_b_s, w_out=v_w_out, b_out=v_b_out, ln1_g=v_ln1_g, ln1_b=v_ln1_b,
             ffn_w_up=v_ffn_w_up, ffn_b_up=v_ffn_b_up, ffn_conv_w=v_ffn_conv_w, ffn_conv_b=v_ffn_conv_b,
             ffn_w_down=v_ffn_w_down, ffn_b_down=v_ffn_b_down, ln2_g=v_ln2_g, ln2_b=v_ln2_b)

    chip_arr = jnp.reshape(2 * lax.axis_index("x") + lax.axis_index("y"), (1,)).astype(jnp.int32)
    shards = {"w_in": _cast_bf16(w_in.reshape(-1, w_in.shape[-1])).reshape(w_in.shape)}
    wb, conv_stack, fconv_stack = _gather_weights(shards, conv_dw_w, ffn_conv_w)
    send_sems, recv_sems, in_flight, token = _gather_start(
        [_cast_into_full(w[n], n, chip_arr) for n in LATE_WEIGHTS], conv_stack)
    sp = {n: w[n] for n in SMALL}
    sp["conv_dw_w"] = jnp.moveaxis(conv_stack, 0, 2).reshape(DEPTH, CONV_WIDTH, CONV_CH)
    sp["ffn_conv_w"] = jnp.moveaxis(fconv_stack, 0, 2).reshape(DEPTH, FFN_CONV_WIDTH, 2 * D_FF)
    sp["b_in"] = sp["b_in"] + token[0, 0]

    def late_weights(after):
        return dict(zip(LATE_WEIGHTS, _gather_wait(send_sems, recv_sems, in_flight, after)))

    sink = _GradExchange()
    loss_local, grad_x, grads, big = _local_step(x[0], loss_target[0], wb, late_weights, sp, sink)

    small_shapes = [(1,)] + [grads[n].shape for n in SMALL]
    summed = _unpack(_small_allreduce(_pack([loss_local.reshape(1)] + [grads[n] for n in SMALL])), small_shapes)
    loss = summed[0].reshape(())
    small = dict(zip(SMALL, summed[1:]))
    chip = 2 * lax.axis_index("x") + lax.axis_index("y")
    for n in SMALL_SHARDED:
        width = w[n].shape[-1]
        small[n] = lax.dynamic_slice_in_dim(small[n], chip * width, width, axis=2)

    g_out, d_out, m_out, v_out = {}, {}, {}, {}
    for n in BIG:
        outs = None
        for l in range(DEPTH):
            own, other = big[(n, l)]
            outs = _adamw_halves(own, other, w[n], m[n], v[n], n, l, sink.c_arr, outs)
        g_out[n], d_out[n], m_out[n], v_out[n] = outs
    shapes = [small[n].shape for n in SMALL]
    packed = [_pack([src[n] for n in SMALL]) for src in (small, w, m, v)]
    upd = _adamw(*packed, "adamw_small")
    for dst, buf in zip((d_out, m_out, v_out), upd):
        dst.update(zip(SMALL, _unpack(buf, shapes)))
    g_out.update(small)

    return (loss, grad_x[None], *[g_out[n] for n in WEIGHTS], *[d_out[n] for n in WEIGHTS],
            *[m_out[n] for n in WEIGHTS], *[v_out[n] for n in WEIGHTS])
```

```python
import functools
import math

import numpy as np
import jax
import jax.numpy as jnp
from jax import lax
from jax.experimental import pallas as pl
from jax.experimental.pallas import tpu as pltpu

f32 = jnp.float32
bf16 = jnp.bfloat16

D_MODEL = 1024
DEPTH = 2
HEAD_DIM = 64
CONV_CH = 256
CONV_WIDTH = 31
ATTN_HEADS = 8
ATTN_CH = ATTN_HEADS * HEAD_DIM
DILATIONS = (1, 4, 16)
ATTN_BLOCK = 128
N_BUCKETS = 32
MAX_DISTANCE = 2048
GMLP_CH = 256
GMLP_GROUPS = 4
GMLP_GROUP_DIM = GMLP_CH // GMLP_GROUPS
CHUNK = 128
IN_CH = 2 * CONV_CH + 3 * ATTN_CH + 2 * GMLP_CH
D_FF = 2816
FFN_CONV_WIDTH = 3
LN_EPS = 1e-5
ALPHA = (2.0 * DEPTH) ** 0.25
ADAM_LR = 0.001
ADAM_B1 = 0.9
ADAM_B2 = 0.999
ADAM_EPS = 1e-08
ADAM_WD = 0.01
ADAM_STEP = 10

CONV_HALO = 32
FFN_HALO = 8
NEG = -1e30
MIB = 2 ** 20
NT_DIMS = (((1,), (1,)), ((), ()))
TN_DIMS = (((0,), (0,)), ((), ()))
MESH_ID = pl.DeviceIdType.MESH


def _cp(sem, vmem_mib):
    return pltpu.CompilerParams(dimension_semantics=sem, vmem_limit_bytes=vmem_mib * MIB)


def _resident(shape):
    nd = len(shape)
    return pl.BlockSpec(shape, lambda *_: (0,) * nd, pipeline_mode=pl.Buffered(1))


def _acc(shape):
    nd = len(shape)
    return pl.BlockSpec(shape, lambda *_: (0,) * nd)


def _sig(x):
    return 1.0 / (1.0 + jnp.exp(-x))


def _ln_stats(z):
    mu = jnp.mean(z, axis=-1, keepdims=True)
    zc = z - mu
    var = jnp.mean(zc * zc, axis=-1, keepdims=True)
    rstd = lax.rsqrt(var + LN_EPS)
    return zc * rstd, rstd


def _ln_bwd(dy, xhat, rstd, g):
    dxh = dy * g
    m1 = jnp.mean(dxh, axis=-1, keepdims=True)
    m2 = jnp.mean(dxh * xhat, axis=-1, keepdims=True)
    return rstd * (dxh - m1 - xhat * m2)


def _colsum(x):
    return jnp.sum(x, axis=0, keepdims=True)


def _t5_bucket_np(dist):
    max_exact = N_BUCKETS // 2
    dd = np.maximum(dist, 1).astype(np.float64)
    large = max_exact + (np.log(dd / max_exact) / math.log(MAX_DISTANCE / max_exact)
                         * (N_BUCKETS - max_exact)).astype(np.int32)
    large = np.minimum(large, N_BUCKETS - 1)
    return np.where(dist < max_exact, dist, large).astype(np.int32)


def _bucket_ids():
    qi = np.arange(ATTN_BLOCK)[:, None]
    kj = np.arange(2 * ATTN_BLOCK)[None, :]
    dist = np.clip(qi + ATTN_BLOCK - kj, 0, None)
    return np.stack([_t5_bucket_np(dist * d) for d in DILATIONS]).astype(np.int32)


LANES = 128
QKV_CH = 3 * ATTN_CH
PERM_TILE = 512


def _slabs(n, rows):
    return [pltpu.VMEM((rows, LANES), f32)] * n


def _rows_of(slab, r, n, d):
    return slab[...] if d == 1 else slab[pl.ds(r, n, stride=d), :]


def _set_rows_of(slab, r, n, d, val):
    if d == 1:
        slab[...] = val
    else:
        slab[pl.ds(r, n, stride=d), :] = val


def _perm_spec(d, ch):
    return pl.BlockSpec((d, PERM_TILE // d, ch), lambda i: (0, i, 0))


def _perm_shape(S, d, ch, dtype):
    return jax.ShapeDtypeStruct((d, S // d, ch), dtype)


def _inproj_fwd(x, w, b):
    S = x.shape[0]
    T = PERM_TILE
    nsl = QKV_CH // LANES

    def body(x_ref, w_ref, b_ref, a_ref, c_ref, *rest):
        q_refs = rest[:len(DILATIONS)]
        slabs = rest[len(DILATIONS):]
        h = jnp.dot(x_ref[...].astype(bf16), w_ref[...], preferred_element_type=f32) + b_ref[...]
        a_ref[...] = h[:, :2 * CONV_CH]
        q0 = 2 * CONV_CH
        c_ref[...] = h[:, q0 + QKV_CH:]
        for j in range(nsl):
            piece = h[:, q0 + LANES * j:q0 + LANES * (j + 1)]
            if LANES * j < ATTN_CH:
                piece = piece * (HEAD_DIM ** -0.5)
            slabs[j][...] = piece
        for d, q_ref in zip(DILATIONS, q_refs):
            for r in range(d):
                for j in range(nsl):
                    q_ref[r, :, LANES * j:LANES * (j + 1)] = _rows_of(slabs[j], r, T // d, d).astype(bf16)

    row = lambda c: pl.BlockSpec((T, c), lambda i: (i, 0))
    return pl.pallas_call(
        body, grid=(S // T,), name="inproj_fwd",
        out_shape=(jax.ShapeDtypeStruct((S, 2 * CONV_CH), f32), jax.ShapeDtypeStruct((S, 2 * GMLP_CH), f32))
        + tuple(_perm_shape(S, d, QKV_CH, bf16) for d in DILATIONS),
        in_specs=[row(D_MODEL), _resident((D_MODEL, IN_CH)), _resident((1, IN_CH))],
        out_specs=(row(2 * CONV_CH), row(2 * GMLP_CH)) + tuple(_perm_spec(d, QKV_CH) for d in DILATIONS),
        scratch_shapes=_slabs(nsl, T),
        compiler_params=_cp(("parallel",), 48),
    )(x, w, b)


CONV_GROUP = 64


def _window_rolls(starts):
    groups = {}
    for s in starts:
        groups.setdefault((-s) % SUBLANES, []).append(s)
    return dict(sorted(groups.items()))


def _conv_fwd(a_in, dw_w, dw_b, ln_g, ln_b):
    S = a_in.shape[0]
    T = 512
    hb = T // CONV_HALO

    def body(a_ref, halo_ref, w_ref, b_ref, g_ref, be_ref, out_ref, hc_ref, buf):
        i = pl.program_id(0)
        am = a_ref[...]
        ah = halo_ref[...]
        hgh = ah[:, :CONV_CH] * _sig(ah[:, CONV_CH:])
        buf[0:CONV_HALO, :] = jnp.where(i > 0, hgh, 0.0)
        buf[CONV_HALO:, :] = am[:, :CONV_CH] * _sig(am[:, CONV_CH:])
        starts = _window_rolls(range(CONV_HALO - (CONV_WIDTH - 1), CONV_HALO + 1))
        slabs = [slice(LANES * j, LANES * (j + 1)) for j in range(CONV_CH // LANES)]

        def step(g, _):
            r0 = pl.multiple_of(g * CONV_GROUP, CONV_GROUP)
            rows = pl.ds(r0, CONV_GROUP)
            for cs in slabs:
                ext = buf[pl.ds(r0, CONV_GROUP + CONV_HALO), cs]
                acc = jnp.broadcast_to(b_ref[:, cs], (CONV_GROUP, LANES))
                for b, ss in starts.items():
                    rolled = ext if b == 0 else pltpu.roll(ext, b, 0)
                    for s in ss:
                        k = s - (CONV_HALO - (CONV_WIDTH - 1))
                        acc = acc + w_ref[k:k + 1, cs] * rolled[s + b:s + b + CONV_GROUP]
                hc_ref[rows, cs] = acc
            return 0

        lax.fori_loop(0, T // CONV_GROUP, step, 0)
        xhat, _ = _ln_stats(hc_ref[...])
        y = xhat * g_ref[...] + be_ref[...]
        out_ref[...] = (y * _sig(y)).astype(bf16)

    return pl.pallas_call(
        body, grid=(S // T,), name="conv_fwd",
        out_shape=(jax.ShapeDtypeStruct((S, CONV_CH), bf16), jax.ShapeDtypeStruct((S, CONV_CH), f32)),
        in_specs=[pl.BlockSpec((T, 2 * CONV_CH), lambda i: (i, 0)),
                  pl.BlockSpec((CONV_HALO, 2 * CONV_CH), lambda i: (jnp.maximum(i * hb - 1, 0), 0)),
                  _acc((32, CONV_CH)), _acc((1, CONV_CH)), _acc((1, CONV_CH)), _acc((1, CONV_CH))],
        out_specs=(pl.BlockSpec((T, CONV_CH), lambda i: (i, 0)), pl.BlockSpec((T, CONV_CH), lambda i: (i, 0))),
        scratch_shapes=[pltpu.VMEM((T + CONV_HALO, CONV_CH), f32)],
        compiler_params=_cp(("parallel",), 32),
    )(a_in, a_in, dw_w, dw_b, ln_g, ln_b)


def _bias_build(table, buckets):
    def body(t_ref, bk_ref, o_ref):
        h = pl.program_id(1)
        ids = bk_ref[0]
        acc = jnp.zeros((ATTN_BLOCK, 2 * ATTN_BLOCK), f32)
        for b in range(N_BUCKETS):
            acc = jnp.where(ids == b, t_ref[b, h], acc)
        row = lax.broadcasted_iota(jnp.int32, acc.shape, 0)
        col = lax.broadcasted_iota(jnp.int32, acc.shape, 1)
        o_ref[0, 0] = jnp.where((col >= row) & (col <= row + ATTN_BLOCK), acc, NEG)

    return pl.pallas_call(
        body, grid=(len(DILATIONS), ATTN_HEADS), name="bias_build",
        out_shape=jax.ShapeDtypeStruct((len(DILATIONS), ATTN_HEADS, ATTN_BLOCK, 2 * ATTN_BLOCK), f32),
        in_specs=[pl.BlockSpec(memory_space=pltpu.SMEM),
                  pl.BlockSpec((1, ATTN_BLOCK, 2 * ATTN_BLOCK), lambda p, h: (p, 0, 0))],
        out_specs=pl.BlockSpec((1, 1, ATTN_BLOCK, 2 * ATTN_BLOCK), lambda p, h: (p, h, 0, 0)),
        compiler_params=_cp(("arbitrary", "arbitrary"), 16),
    )(table, buckets)


def _head_tile(tile, h, col):
    lane_head = lax.broadcasted_iota(jnp.int32, tile.shape, 1) // 16
    return jnp.where(lane_head == h, col, tile)


HEAD_PAIRS = ATTN_HEADS // 2
UNITS_PER_BLOCK = ATTN_HEADS


def _attn_tile(L):
    return min(512, L)


def _mask_logits(logits, first_block, n):
    if not first_block:
        return logits
    col = lax.broadcasted_iota(jnp.int32, logits.shape, 1)
    return jnp.where((col >= ATTN_BLOCK) | (n > 0), logits, NEG)


def _head_lanes(a):
    lane = lax.broadcasted_iota(jnp.int32, (ATTN_BLOCK, LANES), 1)
    return (lane < HEAD_DIM) if a == 0 else (lane >= HEAD_DIM)


def _pair_keys(cur_ref, halo_ref, part, b, j):
    B = ATTN_BLOCK
    c0 = part * ATTN_CH + LANES * j
    own = cur_ref[B * b:B * (b + 1), c0:c0 + LANES]
    prev = halo_ref[:, LANES * j:LANES * (j + 1)] if b == 0 else cur_ref[B * (b - 1):B * b, c0:c0 + LANES]
    return jnp.concatenate([prev, own], axis=0)


def _attn_fwd_pattern(qkv, bias, d):
    _, L, _ = qkv.shape
    B = ATTN_BLOCK
    QB = _attn_tile(L)
    nsb = QB // B
    U = nsb * UNITS_PER_BLOCK

    def body(cur_ref, hk_ref, hv_ref, b_ref, o_ref, lse_ref, lg, pb):
        n = pl.program_id(1)
        for b in range(nsb):
            for j in range(HEAD_PAIRS):
                q2 = cur_ref[B * b:B * (b + 1), LANES * j:LANES * (j + 1)]
                k2 = _pair_keys(cur_ref, hk_ref, 1, b, j)
                for a in range(2):
                    u = (b * HEAD_PAIRS + j) * 2 + a
                    qm = jnp.where(_head_lanes(a), q2, jnp.zeros_like(q2))
                    logits = lax.dot_general(qm, k2, NT_DIMS, preferred_element_type=f32) + b_ref[2 * j + a]
                    lg[B * u:B * (u + 1), :] = _mask_logits(logits, b == 0, n)
        m = jnp.max(lg[...], axis=1, keepdims=True)
        p = jnp.exp(lg[...] - m)
        s = jnp.sum(p, axis=1, keepdims=True)
        pb[...] = p.astype(bf16)
        lse = m + jnp.log(s)
        inv = 1.0 / s
        for b in range(nsb):
            tile = jnp.zeros((B, B), f32)
            for j in range(HEAD_PAIRS):
                v2 = _pair_keys(cur_ref, hv_ref, 2, b, j)
                outs = []
                for a in range(2):
                    u = (b * HEAD_PAIRS + j) * 2 + a
                    rows = slice(B * u, B * (u + 1))
                    outs.append(jnp.dot(pb[rows, :], v2, preferred_element_type=f32) * inv[rows])
                    tile = _head_tile(tile, 2 * j + a, lse[rows])
                o_ref[B * b:B * (b + 1), LANES * j:LANES * (j + 1)] = jnp.where(_head_lanes(0), outs[0], outs[1])
            lse_ref[B * b:B * (b + 1), :] = tile

    halo = lambda part: pl.BlockSpec((None, B, ATTN_CH), lambda r, n: (r, jnp.maximum(n * nsb - 1, 0), part))
    tile_spec = lambda c: pl.BlockSpec((None, QB, c), lambda r, n: (r, n, 0))
    return pl.pallas_call(
        body, grid=(d, L // QB), name=f"attn_fwd_d{d}",
        out_shape=(jax.ShapeDtypeStruct((d, L, ATTN_CH), f32), jax.ShapeDtypeStruct((d, L, B), f32)),
        in_specs=[tile_spec(QKV_CH), halo(1), halo(2), _resident((ATTN_HEADS, B, 2 * B))],
        out_specs=(tile_spec(ATTN_CH), tile_spec(B)),
        scratch_shapes=[pltpu.VMEM((U * B, 2 * B), f32), pltpu.VMEM((U * B, 2 * B), bf16)],
        compiler_params=_cp(("parallel", "parallel"), 40),
    )(qkv, qkv, qkv, bias)


def _attn_merge(parts):
    S = parts[0][0].shape[0] * parts[0][0].shape[1]
    T = PERM_TILE
    nsl = ATTN_CH // LANES
    n_p = len(DILATIONS)

    def body(*refs):
        ins = refs[:2 * n_p]
        out_ref, lse_ref = refs[2 * n_p:2 * n_p + 2]
        slabs = refs[2 * n_p + 2:]
        lses = []
        for p, d in enumerate(DILATIONS):
            o_ref, l_ref = ins[2 * p], ins[2 * p + 1]
            osl = slabs[p * (nsl + 1):p * (nsl + 1) + nsl]
            lsl = slabs[p * (nsl + 1) + nsl]
            for r in range(d):
                for j in range(nsl):
                    _set_rows_of(osl[j], r, T // d, d, o_ref[r, :, LANES * j:LANES * (j + 1)])
                _set_rows_of(lsl, r, T // d, d, l_ref[r])
            lses.append(lsl[...])
        big = functools.reduce(jnp.maximum, lses)
        ws = [jnp.exp(l - big) for l in lses]
        tot = functools.reduce(lambda a_, b_: a_ + b_, ws)
        lse_ref[...] = big + jnp.log(tot)
        ws = [w / tot for w in ws]
        for j in range(nsl):
            acc = jnp.zeros((T, LANES), f32)
            for p in range(n_p):
                wa = ws[p][:, 32 * j:32 * j + 1]
                wb = ws[p][:, 32 * j + 16:32 * j + 17]
                lane = lax.broadcasted_iota(jnp.int32, (T, LANES), 1)
                acc = acc + jnp.where(lane < HEAD_DIM, wa, wb) * slabs[p * (nsl + 1) + j][...]
            out_ref[:, LANES * j:LANES * (j + 1)] = acc.astype(bf16)

    in_specs, args = [], []
    for (o, l), d in zip(parts, DILATIONS):
        in_specs += [_perm_spec(d, ATTN_CH), _perm_spec(d, ATTN_BLOCK)]
        args += [o, l]
    row = lambda c: pl.BlockSpec((T, c), lambda i: (i, 0))
    return pl.pallas_call(
        body, grid=(S // T,), name="attn_merge",
        out_shape=(jax.ShapeDtypeStruct((S, ATTN_CH), bf16), jax.ShapeDtypeStruct((S, ATTN_BLOCK), f32)),
        in_specs=in_specs, out_specs=(row(ATTN_CH), row(ATTN_BLOCK)),
        scratch_shapes=_slabs(n_p * (nsl + 1), T),
        compiler_params=_cp(("parallel",), 40),
    )(*args)


def _attn_fwd(qkvs, bias):
    parts = [_attn_fwd_pattern(q, bias[p], d) for p, (q, d) in enumerate(zip(qkvs, DILATIONS))]
    return _attn_merge(parts)


def _tril_bf16(w):
    row = lax.broadcasted_iota(jnp.int32, (CHUNK, CHUNK), 0)
    col = lax.broadcasted_iota(jnp.int32, (CHUNK, CHUNK), 1)
    return jnp.where(col <= row, w, 0.0).astype(bf16)


def _gmlp_fwd(c_in, ln_g, ln_b, w_s, b_s_t):
    S = c_in.shape[0]
    T = 512

    def body(c_ref, g_ref, be_ref, w_ref, bs_ref, out_ref, mix):
        c = c_ref[...]
        xhat, _ = _ln_stats(c[:, GMLP_CH:])
        vb = (xhat * g_ref[...] + be_ref[...]).astype(bf16)
        for g in range(GMLP_GROUPS):
            wt = _tril_bf16(w_ref[g])
            cs = slice(GMLP_GROUP_DIM * g, GMLP_GROUP_DIM * (g + 1))
            for ci in range(T // CHUNK):
                rs = slice(CHUNK * ci, CHUNK * (ci + 1))
                mix[rs, cs] = jnp.dot(wt, vb[rs, cs], preferred_element_type=f32) + bs_ref[:, g:g + 1]
        out_ref[...] = (c[:, :GMLP_CH] * mix[...]).astype(bf16)

    return pl.pallas_call(
        body, grid=(S // T,), name="gmlp_fwd",
        out_shape=jax.ShapeDtypeStruct((S, GMLP_CH), bf16),
        in_specs=[pl.BlockSpec((T, 2 * GMLP_CH), lambda i: (i, 0)), _acc((1, GMLP_CH)), _acc((1, GMLP_CH)),
                  _acc((GMLP_GROUPS, CHUNK, CHUNK)), _acc((CHUNK, GMLP_GROUPS))],
        out_specs=pl.BlockSpec((T, GMLP_CH), lambda i: (i, 0)),
        scratch_shapes=[pltpu.VMEM((T, GMLP_CH), f32)],
        compiler_params=_cp(("parallel",), 32),
    )(c_in, ln_g, ln_b, w_s, b_s_t)


def _outproj_ln_fwd(conv_out, attn_out, gm_out, w, b, x, ln_g, ln_b):
    S = x.shape[0]
    T = 512

    def body(co_ref, ao_ref, go_ref, w_ref, b_ref, x_ref, g_ref, be_ref, cat_ref, z_ref, yb_ref):
        cat = jnp.concatenate([co_ref[...], ao_ref[...], go_ref[...]], axis=1)
        cat_ref[...] = cat
        z = jnp.dot(cat, w_ref[...], preferred_element_type=f32) + b_ref[...] + ALPHA * x_ref[...]
        z_ref[...] = z
        xhat, _ = _ln_stats(z)
        yb_ref[...] = (xhat * g_ref[...] + be_ref[...]).astype(bf16)

    row = lambda c: pl.BlockSpec((T, c), lambda i: (i, 0))
    return pl.pallas_call(
        body, grid=(S // T,), name="outproj_ln_fwd",
        out_shape=(jax.ShapeDtypeStruct((S, D_MODEL), bf16), jax.ShapeDtypeStruct((S, D_MODEL), f32),
                   jax.ShapeDtypeStruct((S, D_MODEL), bf16)),
        in_specs=[row(CONV_CH), row(ATTN_CH), row(GMLP_CH), _resident((D_MODEL, D_MODEL)), _acc((1, D_MODEL)),
                  row(D_MODEL), _acc((1, D_MODEL)), _acc((1, D_MODEL))],
        out_specs=(row(D_MODEL), row(D_MODEL), row(D_MODEL)),
        compiler_params=_cp(("parallel",), 40),
    )(conv_out, attn_out, gm_out, w, b, x, ln_g, ln_b)


GATE_ROWS = 32
GATE_COLS = 128
GATE_MM_COLS = 1408
SUBLANES = 8


def _gate_cols(c0):
    return slice(c0, c0 + GATE_COLS), slice(D_FF + c0, D_FF + c0 + GATE_COLS)


def _bcast_rows(ref, k, cs):
    return jnp.broadcast_to(ref[k:k + 1, cs], (GATE_ROWS, GATE_COLS))


def _fold_rows(z):
    acc = z[0:SUBLANES]
    for r in range(SUBLANES, GATE_ROWS, SUBLANES):
        acc = acc + z[r:r + SUBLANES]
    return acc


def _ffn_up_gate_fwd(x1b, w, b, conv_w, conv_b):
    S = x1b.shape[0]
    T = 256
    H = FFN_HALO
    K = FFN_CONV_WIDTH

    def body(x_ref, w_ref, b_ref, cw_ref, cb_ref, hfb_ref, hc_ref, act_ref, hbuf, carry):
        @pl.when(pl.program_id(0) == 0)
        def _():
            carry[...] = jnp.zeros_like(carry)
        x = x_ref[...]
        for m0 in range(0, D_FF, GATE_MM_COLS):
            for cm in (slice(m0, m0 + GATE_MM_COLS), slice(D_FF + m0, D_FF + m0 + GATE_MM_COLS)):
                h = jnp.dot(x, w_ref[:, cm], preferred_element_type=f32) + b_ref[:, cm]
                hbuf[:, cm] = h
                hfb_ref[:, cm] = h.astype(bf16)
            for c0 in range(m0, m0 + GATE_MM_COLS, GATE_COLS):
                cols = _gate_cols(c0)
                wts = [[_bcast_rows(cw_ref, k, cs) for k in range(K)] + [_bcast_rows(cb_ref, 0, cs)] for cs in cols]

                def step(rg, tails, cols=cols, wts=wts):
                    rows = pl.ds(pl.multiple_of(rg * GATE_ROWS, GATE_ROWS), GATE_ROWS)
                    hc, new_tails = [], []
                    for cs, wt, tail in zip(cols, wts, tails):
                        h = hbuf[rows, cs]
                        ext = jnp.concatenate([tail, h], axis=0)
                        acc = wt[K] + wt[K - 1] * h
                        for back in range(1, K):
                            acc = acc + wt[K - 1 - back] * pltpu.roll(ext, back, 0)[H:]
                        hc_ref[rows, cs] = acc
                        hc.append(acc)
                        new_tails.append(h[GATE_ROWS - H:])
                    act_ref[rows, cols[0]] = (hc[0] * _sig(hc[0]) * hc[1]).astype(bf16)
                    return tuple(new_tails)

                tails = lax.fori_loop(0, T // GATE_ROWS, step, tuple(carry[:, cs] for cs in cols), unroll=True)
                for cs, tail in zip(cols, tails):
                    carry[:, cs] = tail

    row = lambda c: pl.BlockSpec((T, c), lambda i: (i, 0))
    return pl.pallas_call(
        body, grid=(S // T,), name="ffn_up_gate_fwd",
        out_shape=(jax.ShapeDtypeStruct((S, 2 * D_FF), bf16), jax.ShapeDtypeStruct((S, 2 * D_FF), f32),
                   jax.ShapeDtypeStruct((S, D_FF), bf16)),
        in_specs=[row(D_MODEL), _resident((D_MODEL, 2 * D_FF)), _acc((1, 2 * D_FF)), _acc((8, 2 * D_FF)),
                  _acc((1, 2 * D_FF))],
        out_specs=(row(2 * D_FF), row(2 * D_FF), row(D_FF)),
        scratch_shapes=[pltpu.VMEM((T, 2 * D_FF), f32), pltpu.VMEM((H, 2 * D_FF), f32)],
        compiler_params=_cp(("arbitrary",), 56),
    )(x1b, w, b, conv_w, conv_b)


def _ffn_down_ln_fwd(act, w, b, z1, ln1_g, ln1_b, ln_g, ln_b):
    S = act.shape[0]
    T = 512

    def body(a_ref, w_ref, b_ref, z1_ref, g1_ref, be1_ref, g_ref, be_ref, z_ref, y_ref):
        subs = [slice(s0, s0 + T // 2) for s0 in (0, T // 2)]
        zs = [jnp.dot(a_ref[rs, :], w_ref[...], preferred_element_type=f32) + b_ref[...]
              + ALPHA * (_ln_stats(z1_ref[rs, :])[0] * g1_ref[...] + be1_ref[...]) for rs in subs]
        for rs, z in zip(subs, zs):
            z_ref[rs, :] = z
            xhat, _ = _ln_stats(z)
            y_ref[rs, :] = xhat * g_ref[...] + be_ref[...]

    row = lambda c: pl.BlockSpec((T, c), lambda i: (i, 0))
    return pl.pallas_call(
        body, grid=(S // T,), name="ffn_down_ln_fwd",
        out_shape=(jax.ShapeDtypeStruct((S, D_MODEL), f32), jax.ShapeDtypeStruct((S, D_MODEL), f32)),
        in_specs=[row(D_FF), _resident((D_FF, D_MODEL)), _acc((1, D_MODEL)), row(D_MODEL)] + [_acc((1, D_MODEL))] * 4,
        out_specs=(row(D_MODEL), row(D_MODEL)),
        compiler_params=_cp(("parallel",), 40),
    )(act, w, b, z1, ln1_g, ln1_b, ln_g, ln_b)


def _ffn_down_ln_loss(act, w, b, z1, ln1_g, ln1_b, ln_g, ln_b, target):
    S = act.shape[0]
    T = 512

    def body(a_ref, w_ref, b_ref, z1_ref, g1_ref, be1_ref, g_ref, be_ref, t_ref, dz_ref, dzb_ref, loss_ref, dg_ref,
             db_ref):
        @pl.when(pl.program_id(0) == 0)
        def _():
            loss_ref[...] = jnp.zeros_like(loss_ref)
            dg_ref[...] = jnp.zeros_like(dg_ref)
            db_ref[...] = jnp.zeros_like(db_ref)
        subs = [slice(s0, s0 + T // 2) for s0 in (0, T // 2)]
        zs = [jnp.dot(a_ref[rs, :], w_ref[...], preferred_element_type=f32) + b_ref[...]
              + ALPHA * (_ln_stats(z1_ref[rs, :])[0] * g1_ref[...] + be1_ref[...]) for rs in subs]
        for rs, z in zip(subs, zs):
            xhat, rstd = _ln_stats(z)
            err = xhat * g_ref[...] + be_ref[...] - t_ref[rs, :]
            loss_ref[...] += _colsum(err * err) * (0.5 / D_MODEL)
            dy = err * (1.0 / D_MODEL)
            dz = _ln_bwd(dy, xhat, rstd, g_ref[...])
            dz_ref[rs, :] = dz
            dzb_ref[rs, :] = dz.astype(bf16)
            dg_ref[...] += _colsum(dy * xhat)
            db_ref[...] += _colsum(dy)

    row = lambda c: pl.BlockSpec((T, c), lambda i: (i, 0))
    vec = jax.ShapeDtypeStruct((1, D_MODEL), f32)
    return pl.pallas_call(
        body, grid=(S // T,), name="ffn_down_ln_loss",
        out_shape=(jax.ShapeDtypeStruct((S, D_MODEL), f32), jax.ShapeDtypeStruct((S, D_MODEL), bf16), vec, vec, vec),
        in_specs=[row(D_FF), _resident((D_FF, D_MODEL)), _acc((1, D_MODEL)), row(D_MODEL)] + [_acc((1, D_MODEL))] * 4
        + [row(D_MODEL)],
        out_specs=(row(D_MODEL), row(D_MODEL), _acc((1, D_MODEL)), _acc((1, D_MODEL)), _acc((1, D_MODEL))),
        compiler_params=_cp(("arbitrary",), 40),
    )(act, w, b, z1, ln1_g, ln1_b, ln_g, ln_b, target)


def _dgrad_ln_bwd(g, w, dz_res, z, ln_g, name):
    S, K = g.shape
    SUB = 256
    T = 2 * SUB if S % (2 * SUB) == 0 else SUB
    with_ln = z is not None

    def body(*refs):
        if with_ln:
            g_ref, w_ref, r_ref, z_ref, lg_ref, dz_ref, dzb_ref, dg_ref, db_ref = refs
        else:
            g_ref, w_ref, r_ref, dx_ref = refs
        subs = [slice(s0, s0 + SUB) for s0 in range(0, T, SUB)]
        dxs = [lax.dot_general(g_ref[rs, :], w_ref[...], NT_DIMS, preferred_element_type=f32) + ALPHA * r_ref[rs, :]
               for rs in subs]
        if not with_ln:
            for rs, dx in zip(subs, dxs):
                dx_ref[rs, :] = dx
            return

        @pl.when(pl.program_id(0) == 0)
        def _():
            dg_ref[...] = jnp.zeros_like(dg_ref)
            db_ref[...] = jnp.zeros_like(db_ref)
        for rs, dx in zip(subs, dxs):
            xhat, rstd = _ln_stats(z_ref[rs, :])
            dz = _ln_bwd(dx, xhat, rstd, lg_ref[...])
            dz_ref[rs, :] = dz
            dzb_ref[rs, :] = dz.astype(bf16)
            dg_ref[...] += _colsum(dx * xhat)
            db_ref[...] += _colsum(dx)

    row = pl.BlockSpec((T, D_MODEL), lambda i: (i, 0))
    vec = jax.ShapeDtypeStruct((1, D_MODEL), f32)
    in_specs = [pl.BlockSpec((T, K), lambda i: (i, 0)), _resident((D_MODEL, K)), row]
    args = [g, w, dz_res]
    if with_ln:
        in_specs += [row, _acc((1, D_MODEL))]
        args += [z, ln_g]
        out_shape = (jax.ShapeDtypeStruct((S, D_MODEL), f32), jax.ShapeDtypeStruct((S, D_MODEL), bf16), vec, vec)
        out_specs = (row, row, _acc((1, D_MODEL)), _acc((1, D_MODEL)))
    else:
        out_shape = jax.ShapeDtypeStruct((S, D_MODEL), f32)
        out_specs = row
    return pl.pallas_call(
        body, grid=(S // T,), name=name, out_shape=out_shape, in_specs=in_specs, out_specs=out_specs,
        compiler_params=_cp(("arbitrary",), 48),
    )(*args)


def _ffn_down_gate_bwd(dzb, w_down, hfb, hc, conv_w):
    S = hc.shape[0]
    T = 256
    H = FFN_HALO
    nt = S // T
    K = FFN_CONV_WIDTH

    def body(dz_ref, w_ref, h_ref, hc_ref, cw_ref, dh_ref, dw_ref, dcb_ref, da_buf, carry):
        @pl.when(pl.program_id(0) == 0)
        def _():
            dw_ref[...] = jnp.zeros_like(dw_ref)
            dcb_ref[...] = jnp.zeros_like(dcb_ref)
            carry[...] = jnp.zeros_like(carry)
        da_buf[...] = lax.dot_general(dz_ref[...], w_ref[...], NT_DIMS, preferred_element_type=f32)
        ngroups = T // GATE_ROWS
        for c0 in range(0, D_FF, GATE_COLS):
            cols = _gate_cols(c0)
            wts = [[_bcast_rows(cw_ref, k, cs) for k in range(K)] for cs in cols]

            def step(it, state, cols=cols, wts=wts):
                heads, accs = state
                rows = pl.ds(pl.multiple_of((ngroups - 1 - it) * GATE_ROWS, GATE_ROWS), GATE_ROWS)
                g = hc_ref[rows, cols[0]]
                v = hc_ref[rows, cols[1]]
                da = da_buf[rows, cols[0]]
                sg = _sig(g)
                dms = (da * v * (sg * (1.0 + g * (1.0 - sg))), da * (g * sg))
                new_heads, new_accs = [], []
                for cs, wt, dm, head, acc in zip(cols, wts, dms, heads, accs):
                    h0 = h_ref[rows, cs].astype(f32)
                    ext = jnp.concatenate([dm, head], axis=0)
                    dh = wt[K - 1] * dm
                    acc_k = [None] * K + [acc[K] + _fold_rows(dm)]
                    acc_k[K - 1] = acc[K - 1] + _fold_rows(dm * h0)
                    for ahead in range(1, K):
                        dk = pltpu.roll(ext, GATE_ROWS + H - ahead, 0)[:GATE_ROWS]
                        dh = dh + wt[K - 1 - ahead] * dk
                        acc_k[K - 1 - ahead] = acc[K - 1 - ahead] + _fold_rows(dk * h0)
                    dh_ref[rows, cs] = dh.astype(bf16)
                    new_heads.append(dm[:H])
                    new_accs.append(tuple(acc_k))
                return tuple(new_heads), tuple(new_accs)

            zero = jnp.zeros((SUBLANES, GATE_COLS), f32)
            init = (tuple(carry[:, cs] for cs in cols), tuple(tuple(zero for _ in range(K + 1)) for _ in cols))
            heads, accs = lax.fori_loop(0, ngroups, step, init, unroll=True)
            for cs, head, acc in zip(cols, heads, accs):
                carry[:, cs] = head
                dcb_ref[:, cs] += _colsum(acc[K])
                for k in range(K):
                    dw_ref[k:k + 1, cs] += _colsum(acc[k])

    tile = lambda c: pl.BlockSpec((T, c), lambda i: (nt - 1 - i, 0))
    return pl.pallas_call(
        body, grid=(nt,), name="ffn_down_gate_bwd",
        out_shape=(jax.ShapeDtypeStruct((S, 2 * D_FF), bf16), jax.ShapeDtypeStruct((8, 2 * D_FF), f32),
                   jax.ShapeDtypeStruct((1, 2 * D_FF), f32)),
        in_specs=[tile(D_MODEL), _resident((D_FF, D_MODEL)), tile(2 * D_FF), tile(2 * D_FF), _acc((8, 2 * D_FF))],
        out_specs=(tile(2 * D_FF), _acc((8, 2 * D_FF)), _acc((1, 2 * D_FF))),
        scratch_shapes=[pltpu.VMEM((T, D_FF), f32), pltpu.VMEM((H, 2 * D_FF), f32)],
        compiler_params=_cp(("arbitrary",), 48),
    )(dzb, w_down, hfb, hc, conv_w)


def _wgrad(a, g, tn, name, rows=1024):
    S, K = a.shape
    N = g.shape[1]
    T = rows if S % rows == 0 else S

    def body(a_ref, g_ref, dw_ref, db_ref):
        @pl.when(pl.program_id(1) == 0)
        def _():
            dw_ref[...] = jnp.zeros_like(dw_ref)
            db_ref[...] = jnp.zeros_like(db_ref)
        gt = g_ref[...]
        dw_ref[...] += lax.dot_general(a_ref[...].astype(bf16), gt, TN_DIMS, preferred_element_type=f32)
        db_ref[...] += _colsum(gt.astype(f32))

    return pl.pallas_call(
        body, grid=(N // tn, S // T), name=name,
        out_shape=(jax.ShapeDtypeStruct((K, N), f32), jax.ShapeDtypeStruct((1, N), f32)),
        in_specs=[pl.BlockSpec((T, K), lambda j, i: (i, 0)), pl.BlockSpec((T, tn), lambda j, i: (i, j))],
        out_specs=(pl.BlockSpec((K, tn), lambda j, i: (0, j)), pl.BlockSpec((1, tn), lambda j, i: (0, j))),
        compiler_params=_cp(("parallel", "arbitrary"), 56),
    )(a, g)


def _outproj_dgrad(dzb, w, attn_out, lse):
    S = dzb.shape[0]
    T = PERM_TILE
    nsl = ATTN_CH // LANES
    n_p = len(DILATIONS)

    def body(g_ref, w_ref, ao_ref, lse_ref, dco_ref, dgo_ref, *rest):
        do_refs = rest[:n_p]
        st_refs = rest[n_p:2 * n_p]
        slabs = rest[2 * n_p:]
        dcat = lax.dot_general(g_ref[...], w_ref[...], NT_DIMS, preferred_element_type=f32)
        dco_ref[...] = dcat[:, :CONV_CH]
        dgo_ref[...] = dcat[:, CONV_CH + ATTN_CH:]
        lane = lax.broadcasted_iota(jnp.int32, (T, LANES), 1)
        st = lse_ref[...]
        for j in range(nsl):
            dO = dcat[:, CONV_CH + LANES * j:CONV_CH + LANES * (j + 1)]
            prod = dO * ao_ref[:, LANES * j:LANES * (j + 1)].astype(f32)
            for a in range(2):
                in_head = (lane < HEAD_DIM) if a == 0 else (lane >= HEAD_DIM)
                delta = jnp.sum(jnp.where(in_head, prod, 0.0), axis=1, keepdims=True)
                st = jnp.where((lane // 16 == 2 * j + a) & (lane % 16 >= 8), delta, st)
            slabs[j][...] = dO
        slabs[nsl][...] = st
        for d, do_ref, st_ref in zip(DILATIONS, do_refs, st_refs):
            for r in range(d):
                for j in range(nsl):
                    do_ref[r, :, LANES * j:LANES * (j + 1)] = _rows_of(slabs[j], r, T // d, d).astype(bf16)
                st_ref[r] = _rows_of(slabs[nsl], r, T // d, d)

    row = lambda c: pl.BlockSpec((T, c), lambda i: (i, 0))
    return pl.pallas_call(
        body, grid=(S // T,), name="outproj_dgrad",
        out_shape=(jax.ShapeDtypeStruct((S, CONV_CH), f32), jax.ShapeDtypeStruct((S, GMLP_CH), f32))
        + tuple(_perm_shape(S, d, ATTN_CH, bf16) for d in DILATIONS)
        + tuple(_perm_shape(S, d, ATTN_BLOCK, f32) for d in DILATIONS),
        in_specs=[row(D_MODEL), _resident((D_MODEL, D_MODEL)), row(ATTN_CH), row(ATTN_BLOCK)],
        out_specs=(row(CONV_CH), row(GMLP_CH)) + tuple(_perm_spec(d, ATTN_CH) for d in DILATIONS)
        + tuple(_perm_spec(d, ATTN_BLOCK) for d in DILATIONS),
        scratch_shapes=_slabs(nsl + 1, T),
        compiler_params=_cp(("parallel",), 40),
    )(dzb, w, attn_out, lse)


def _gmlp_bwd(c_in, dgm, ln_g, ln_b, w_s, b_s_t):
    S = c_in.shape[0]
    T = 512
    nsteps = S // T

    def body(c_ref, dg_ref, g_ref, be_ref, w_ref, bs_ref, dc_ref, dlg_ref, dlb_ref, dw_ref, dbs_ref,
             du_buf, dv_buf, dm_acc):
        i = pl.program_id(0)

        @pl.when(i == 0)
        def _():
            dlg_ref[...] = jnp.zeros_like(dlg_ref)
            dlb_ref[...] = jnp.zeros_like(dlb_ref)
            dw_ref[...] = jnp.zeros_like(dw_ref)
            dm_acc[...] = jnp.zeros_like(dm_acc)
        c = c_ref[...]
        u = c[:, :GMLP_CH]
        xhat, rstd = _ln_stats(c[:, GMLP_CH:])
        vb = (xhat * g_ref[...] + be_ref[...]).astype(bf16)
        dgm_t = dg_ref[...]
        dm_all = dgm_t * u
        for g in range(GMLP_GROUPS):
            wt = _tril_bf16(w_ref[g])
            cs = slice(GMLP_GROUP_DIM * g, GMLP_GROUP_DIM * (g + 1))
            dw_g = jnp.zeros((CHUNK, CHUNK), f32)
            for ci in range(T // CHUNK):
                rs = slice(CHUNK * ci, CHUNK * (ci + 1))
                v_c = vb[rs, cs]
                mixed = jnp.dot(wt, v_c, preferred_element_type=f32) + bs_ref[:, g:g + 1]
                dm = dm_all[rs, cs]
                dmb = dm.astype(bf16)
                du_buf[rs, cs] = dgm_t[rs, cs] * mixed
                dv_buf[rs, cs] = lax.dot_general(wt, dmb, TN_DIMS, preferred_element_type=f32)
                dw_g = dw_g + lax.dot_general(dmb, v_c, NT_DIMS, preferred_element_type=f32)
                dm_acc[:, cs] += dm
            dw_ref[g] += dw_g
        dv = dv_buf[...]
        dvr = _ln_bwd(dv, xhat, rstd, g_ref[...])
        dlg_ref[...] += _colsum(dv * xhat)
        dlb_ref[...] += _colsum(dv)
        dc_ref[:, :GMLP_CH] = du_buf[...].astype(bf16)
        dc_ref[:, GMLP_CH:] = dvr.astype(bf16)

        @pl.when(i == nsteps - 1)
        def _():
            row = lax.broadcasted_iota(jnp.int32, (CHUNK, CHUNK), 0)
            col = lax.broadcasted_iota(jnp.int32, (CHUNK, CHUNK), 1)
            tile = jnp.zeros((CHUNK, CHUNK), f32)
            for g in range(GMLP_GROUPS):
                dw_ref[g] = jnp.where(col <= row, dw_ref[g], 0.0)
                gsum = jnp.sum(dm_acc[:, GMLP_GROUP_DIM * g:GMLP_GROUP_DIM * (g + 1)], axis=1, keepdims=True)
                tile = jnp.where(col == g, gsum, tile)
            dbs_ref[...] = tile

    vec = jax.ShapeDtypeStruct((1, GMLP_CH), f32)
    return pl.pallas_call(
        body, grid=(nsteps,), name="gmlp_bwd",
        out_shape=(jax.ShapeDtypeStruct((S, 2 * GMLP_CH), bf16), vec, vec,
                   jax.ShapeDtypeStruct((GMLP_GROUPS, CHUNK, CHUNK), f32), jax.ShapeDtypeStruct((CHUNK, CHUNK), f32)),
        in_specs=[pl.BlockSpec((T, 2 * GMLP_CH), lambda i: (i, 0)), pl.BlockSpec((T, GMLP_CH), lambda i: (i, 0)),
                  _acc((1, GMLP_CH)), _acc((1, GMLP_CH)), _acc((GMLP_GROUPS, CHUNK, CHUNK)), _acc((CHUNK, GMLP_GROUPS))],
        out_specs=(pl.BlockSpec((T, 2 * GMLP_CH), lambda i: (i, 0)), _acc((1, GMLP_CH)), _acc((1, GMLP_CH)),
                   _acc((GMLP_GROUPS, CHUNK, CHUNK)), _acc((CHUNK, CHUNK))),
        scratch_shapes=[pltpu.VMEM((T, GMLP_CH), f32), pltpu.VMEM((T, GMLP_CH), f32), pltpu.VMEM((CHUNK, GMLP_CH), f32)],
        compiler_params=_cp(("arbitrary",), 32),
    )(c_in, dgm, ln_g, ln_b, w_s, b_s_t)


def _attn_bwd_pattern(qkv, d_out, stats, bias, d):
    _, L, _ = qkv.shape
    B = ATTN_BLOCK
    QB = _attn_tile(L)
    nsb = QB // B
    nt = L // QB
    U = nsb * UNITS_PER_BLOCK
    KV = 2 * ATTN_CH

    def body(cur_ref, hk_ref, hv_ref, do_ref, st_ref, b_ref, dqkv_ref, dbias_ref, lg, dp, pb, dsb, dkv, carry):
        r = pl.program_id(0)
        i = pl.program_id(1)
        n = nt - 1 - i

        @pl.when((r == 0) & (i == 0))
        def _():
            dbias_ref[...] = jnp.zeros_like(dbias_ref)

        @pl.when(i == 0)
        def _():
            carry[...] = jnp.zeros_like(carry)

        def operands(b, j, a):
            rows = slice(B * b, B * (b + 1))
            q2 = cur_ref[rows, LANES * j:LANES * (j + 1)]
            do2 = do_ref[rows, LANES * j:LANES * (j + 1)]
            keep = _head_lanes(a)
            return jnp.where(keep, q2, jnp.zeros_like(q2)), jnp.where(keep, do2, jnp.zeros_like(do2))

        for b in range(nsb):
            for j in range(HEAD_PAIRS):
                k2 = _pair_keys(cur_ref, hk_ref, 1, b, j)
                v2 = _pair_keys(cur_ref, hv_ref, 2, b, j)
                for a in range(2):
                    u = (b * HEAD_PAIRS + j) * 2 + a
                    qm, dom = operands(b, j, a)
                    logits = lax.dot_general(qm, k2, NT_DIMS, preferred_element_type=f32) + b_ref[2 * j + a]
                    lg[B * u:B * (u + 1), :] = _mask_logits(logits, b == 0, n)
                    dp[B * u:B * (u + 1), :] = lax.dot_general(dom, v2, NT_DIMS, preferred_element_type=f32)
        for b in range(nsb):
            for j in range(HEAD_PAIRS):
                for a in range(2):
                    u = (b * HEAD_PAIRS + j) * 2 + a
                    rows = slice(B * u, B * (u + 1))
                    lane0 = 32 * j + 16 * a
                    lse = st_ref[B * b:B * (b + 1), lane0:lane0 + 1]
                    delta = st_ref[B * b:B * (b + 1), lane0 + 8:lane0 + 9]
                    p = jnp.exp(lg[rows, :] - lse)
                    ds = p * (dp[rows, :] - delta)
                    pb[rows, :] = p.astype(bf16)
                    dsb[rows, :] = ds.astype(bf16)
                    dbias_ref[2 * j + a] += ds
        dkv[...] = jnp.zeros_like(dkv)
        for b in range(nsb):
            for j in range(HEAD_PAIRS):
                k2 = _pair_keys(cur_ref, hk_ref, 1, b, j)
                dq, dk2, dv2 = [], None, None
                for a in range(2):
                    u = (b * HEAD_PAIRS + j) * 2 + a
                    rows = slice(B * u, B * (u + 1))
                    qm, dom = operands(b, j, a)
                    ds_u = dsb[rows, :]
                    dq.append(jnp.dot(ds_u, k2, preferred_element_type=f32))
                    dk_u = lax.dot_general(ds_u, qm, TN_DIMS, preferred_element_type=f32)
                    dv_u = lax.dot_general(pb[rows, :], dom, TN_DIMS, preferred_element_type=f32)
                    dk2 = dk_u if dk2 is None else dk2 + dk_u
                    dv2 = dv_u if dv2 is None else dv2 + dv_u
                dq2 = jnp.where(_head_lanes(0), dq[0], dq[1]) * (HEAD_DIM ** -0.5)
                dqkv_ref[B * b:B * (b + 1), LANES * j:LANES * (j + 1)] = dq2.astype(bf16)
                dkv[B * b:B * (b + 2), LANES * j:LANES * (j + 1)] += dk2
                dkv[B * b:B * (b + 2), ATTN_CH + LANES * j:ATTN_CH + LANES * (j + 1)] += dv2
        dkv[QB:, :] += carry[...]
        dqkv_ref[:, ATTN_CH:] = dkv[B:, :].astype(bf16)
        carry[...] = dkv[0:B, :]

    halo = lambda part: pl.BlockSpec((None, B, ATTN_CH),
                                     lambda r, i: (r, jnp.maximum((nt - 1 - i) * nsb - 1, 0), part))
    tile_spec = lambda c: pl.BlockSpec((None, QB, c), lambda r, i: (r, nt - 1 - i, 0))
    return pl.pallas_call(
        body, grid=(d, nt), name=f"attn_bwd_d{d}",
        out_shape=(jax.ShapeDtypeStruct((d, L, QKV_CH), bf16), jax.ShapeDtypeStruct((ATTN_HEADS, B, 2 * B), f32)),
        in_specs=[tile_spec(QKV_CH), halo(1), halo(2), tile_spec(ATTN_CH), tile_spec(B),
                  _resident((ATTN_HEADS, B, 2 * B))],
        out_specs=(tile_spec(QKV_CH), _acc((ATTN_HEADS, B, 2 * B))),
        scratch_shapes=[pltpu.VMEM((U * B, 2 * B), f32), pltpu.VMEM((U * B, 2 * B), f32),
                        pltpu.VMEM((U * B, 2 * B), bf16), pltpu.VMEM((U * B, 2 * B), bf16),
                        pltpu.VMEM((B + QB, KV), f32), pltpu.VMEM((B, KV), f32)],
        compiler_params=_cp(("arbitrary", "arbitrary"), 48),
    )(qkv, qkv, qkv, d_out, stats, bias)


def _attn_bwd_merge(d_a, dqkvs, d_c):
    S = d_a.shape[0]
    T = PERM_TILE
    nsl = QKV_CH // LANES
    n_p = len(DILATIONS)

    def body(da_ref, *rest):
        g_refs = rest[:n_p]
        dc_ref, dh_ref = rest[n_p:n_p + 2]
        slabs = rest[n_p + 2:]
        q0 = 2 * CONV_CH
        dh_ref[:, :q0] = da_ref[...]
        dh_ref[:, q0 + QKV_CH:] = dc_ref[...]
        for p, (d, g_ref) in enumerate(zip(DILATIONS, g_refs)):
            for r in range(d):
                for j in range(nsl):
                    _set_rows_of(slabs[p * nsl + j], r, T // d, d, g_ref[r, :, LANES * j:LANES * (j + 1)].astype(f32))
        for j in range(nsl):
            acc = slabs[j][...]
            for p in range(1, n_p):
                acc = acc + slabs[p * nsl + j][...]
            dh_ref[:, q0 + LANES * j:q0 + LANES * (j + 1)] = acc.astype(bf16)

    row = lambda c: pl.BlockSpec((T, c), lambda i: (i, 0))
    return pl.pallas_call(
        body, grid=(S // T,), name="attn_bwd_merge", out_shape=jax.ShapeDtypeStruct((S, IN_CH), bf16),
        in_specs=[row(2 * CONV_CH)] + [_perm_spec(d, QKV_CH) for d in DILATIONS] + [row(2 * GMLP_CH)],
        out_specs=row(IN_CH), scratch_shapes=_slabs(n_p * nsl, T),
        compiler_params=_cp(("parallel",), 48),
    )(d_a, *dqkvs, d_c)


def _bias_table_grad(dbias, buckets):
    n = dbias.shape[0]

    def body(db_ref, bk_ref, o_ref):
        p = pl.program_id(0)
        h = pl.program_id(1)

        @pl.when((p == 0) & (h == 0))
        def _():
            o_ref[...] = jnp.zeros_like(o_ref)
        ids = bk_ref[0]
        db = db_ref[0, 0]
        row = lax.broadcasted_iota(jnp.int32, (N_BUCKETS, 128), 0)
        lane = lax.broadcasted_iota(jnp.int32, (N_BUCKETS, 128), 1)
        upd = jnp.zeros((N_BUCKETS, 128), f32)
        for b in range(N_BUCKETS):
            s = jnp.sum(jnp.sum(jnp.where(ids == b, db, 0.0), axis=1, keepdims=True), axis=0, keepdims=True)
            upd = jnp.where((row == b) & (lane == h), s, upd)
        o_ref[...] += upd

    return pl.pallas_call(
        body, grid=(n, ATTN_HEADS), name="bias_table_grad",
        out_shape=jax.ShapeDtypeStruct((N_BUCKETS, 128), f32),
        in_specs=[pl.BlockSpec((1, 1, ATTN_BLOCK, 2 * ATTN_BLOCK), lambda p, h: (p, h, 0, 0)),
                  pl.BlockSpec((1, ATTN_BLOCK, 2 * ATTN_BLOCK), lambda p, h: (p, 0, 0))],
        out_specs=_acc((N_BUCKETS, 128)),
        compiler_params=_cp(("arbitrary", "arbitrary"), 16),
    )(dbias, buckets)


def _conv_bwd(a_in, hc, dco, dw_w, ln_g, ln_b):
    S = a_in.shape[0]
    T = 512
    hb = T // CONV_HALO
    nsteps = S // T
    R = T + CONV_HALO
    K = CONV_WIDTH

    def body(a_ref, hc_ref, hcn_ref, d_ref, dn_ref, w_ref, g_ref, be_ref,
             da_ref, dw_ref, dcb_ref, dlg_ref, dlb_ref, ext, dbuf, wacc):
        i = pl.program_id(0)

        @pl.when(i == 0)
        def _():
            wacc[...] = jnp.zeros_like(wacc)
            dcb_ref[...] = jnp.zeros_like(dcb_ref)
            dlg_ref[...] = jnp.zeros_like(dlg_ref)
            dlb_ref[...] = jnp.zeros_like(dlb_ref)
        ext[0:T, :] = hc_ref[...]
        ext[T:, :] = hcn_ref[...]
        xhat, rstd = _ln_stats(ext[...])
        hl = xhat * g_ref[...] + be_ref[...]
        ext[0:T, :] = d_ref[...]
        ext[T:, :] = dn_ref[...]
        sl_ = _sig(hl)
        dhl = ext[...] * (sl_ * (1.0 + hl * (1.0 - sl_)))
        dhc = _ln_bwd(dhl, xhat, rstd, g_ref[...])
        rowi = lax.broadcasted_iota(jnp.int32, (R, CONV_CH), 0)
        dbuf[...] = jnp.where((rowi < T) | (i < nsteps - 1), dhc, 0.0)
        dlg_ref[...] += _colsum(dhl[:T] * xhat[:T])
        dlb_ref[...] += _colsum(dhl[:T])
        dcb_ref[...] += _colsum(dbuf[pl.ds(0, T), :])
        starts = _window_rolls(range(K))
        slabs = [slice(LANES * j, LANES * (j + 1)) for j in range(CONV_CH // LANES)]

        def step(g, _):
            r0 = pl.multiple_of(g * CONV_GROUP, CONV_GROUP)
            rows = pl.ds(r0, CONV_GROUP)
            for j, cs in enumerate(slabs):
                gate_cs = slice(CONV_CH + LANES * j, CONV_CH + LANES * (j + 1))
                win = dbuf[pl.ds(r0, CONV_GROUP + CONV_HALO), cs]
                a = a_ref[rows, cs]
                sg = _sig(a_ref[rows, gate_cs])
                hg = a * sg
                dhg = jnp.zeros((CONV_GROUP, LANES), f32)
                for b, ss in starts.items():
                    rolled = win if b == 0 else pltpu.roll(win, b, 0)
                    for s in ss:
                        k = K - 1 - s
                        dk = rolled[s + b:s + b + CONV_GROUP]
                        dhg = dhg + w_ref[k:k + 1, cs] * dk
                        prod = dk * hg
                        fold = prod[0:SUBLANES]
                        for r in range(SUBLANES, CONV_GROUP, SUBLANES):
                            fold = fold + prod[r:r + SUBLANES]
                        wacc[SUBLANES * k:SUBLANES * (k + 1), cs] += fold
                da_ref[rows, cs] = (dhg * sg).astype(bf16)
                da_ref[rows, gate_cs] = (dhg * hg * (1.0 - sg)).astype(bf16)
            return 0

        lax.fori_loop(0, T // CONV_GROUP, step, 0)

        @pl.when(i == nsteps - 1)
        def _():
            for k in range(K):
                dw_ref[k:k + 1, :] = _colsum(wacc[SUBLANES * k:SUBLANES * (k + 1), :])
            dw_ref[K:, :] = jnp.zeros((32 - K, CONV_CH), f32)

    vec = jax.ShapeDtypeStruct((1, CONV_CH), f32)
    nxt = lambda i: (jnp.minimum((i + 1) * hb, nsteps * hb - 1), 0)
    return pl.pallas_call(
        body, grid=(nsteps,), name="conv_bwd",
        out_shape=(jax.ShapeDtypeStruct((S, 2 * CONV_CH), bf16), jax.ShapeDtypeStruct((32, CONV_CH), f32), vec, vec, vec),
        in_specs=[pl.BlockSpec((T, 2 * CONV_CH), lambda i: (i, 0)),
                  pl.BlockSpec((T, CONV_CH), lambda i: (i, 0)), pl.BlockSpec((CONV_HALO, CONV_CH), nxt),
                  pl.BlockSpec((T, CONV_CH), lambda i: (i, 0)), pl.BlockSpec((CONV_HALO, CONV_CH), nxt),
                  _acc((32, CONV_CH)), _acc((1, CONV_CH)), _acc((1, CONV_CH))],
        out_specs=(pl.BlockSpec((T, 2 * CONV_CH), lambda i: (i, 0)), _acc((32, CONV_CH)), _acc((1, CONV_CH)),
                   _acc((1, CONV_CH)), _acc((1, CONV_CH))),
        scratch_shapes=[pltpu.VMEM((R, CONV_CH), f32), pltpu.VMEM((R, CONV_CH), f32),
                        pltpu.VMEM((SUBLANES * 32, CONV_CH), f32)],
        compiler_params=_cp(("arbitrary",), 32),
    )(a_in, hc, hc, dco, dco, dw_w, ln_g, ln_b)


def _adamw(g, w, m, v, name):
    R, C = g.shape
    T = R
    for cand in (512, 256, 128, 64, 32, 16, 8):
        if R % cand == 0 and cand * C * 4 <= MIB:
            T = cand
            break
    c1 = 1.0 / (1.0 - ADAM_B1 ** ADAM_STEP)
    c2 = 1.0 / (1.0 - ADAM_B2 ** ADAM_STEP)

    def body(g_ref, w_ref, m_ref, v_ref, d_ref, nm_ref, nv_ref):
        gg = g_ref[...]
        nm = ADAM_B1 * m_ref[...] + (1.0 - ADAM_B1) * gg
        nv = ADAM_B2 * v_ref[...] + (1.0 - ADAM_B2) * (gg * gg)
        nm_ref[...] = nm
        nv_ref[...] = nv
        d_ref[...] = -ADAM_LR * ((nm * c1) / (jnp.sqrt(nv * c2) + ADAM_EPS) + ADAM_WD * w_ref[...])

    blk = pl.BlockSpec((T, C), lambda i: (i, 0))
    sd = jax.ShapeDtypeStruct((R, C), f32)
    return pl.pallas_call(
        body, grid=(R // T,), name=name, out_shape=(sd, sd, sd), in_specs=[blk] * 4, out_specs=(blk, blk, blk),
        compiler_params=_cp(("parallel",), 48),
    )(g, w, m, v)


def _pad_rows(a, rows):
    return jnp.pad(a, ((0, rows - a.shape[0]), (0, 0)))


def _local_step(x, target, wb, late_weights, sp, sink):
    buckets = jnp.asarray(_bucket_ids())
    bias = _bias_build(sp["rel_bias_table"], buckets)
    wb = dict(wb)
    saved = []
    xl = x
    for l in range(DEPTH):
        vec = lambda name: sp[name][l][None, :]
        a_in, c_in, *qkv = _inproj_fwd(xl, wb["w_in"][l], vec("b_in"))
        conv_w = _pad_rows(sp["conv_dw_w"][l], 32)
        conv_out, hc = _conv_fwd(a_in, conv_w, vec("conv_dw_b"), vec("conv_ln_g"), vec("conv_ln_b"))
        attn_out, lse = _attn_fwd(qkv, bias)
        bs_t = sp["gmlp_b_s"][l].T
        gm_out = _gmlp_fwd(c_in, vec("gmlp_ln_g"), vec("gmlp_ln_b"), sp["gmlp_w_s"][l], bs_t)
        if l == 0:
            wb.update(late_weights(gm_out))
        cat, z1, x1b = _outproj_ln_fwd(conv_out, attn_out, gm_out, wb["w_out"][l], vec("b_out"), xl,
                                           vec("ln1_g"), vec("ln1_b"))
        fconv_w = _pad_rows(sp["ffn_conv_w"][l], 8)
        hfb, fhc, act = _ffn_up_gate_fwd(x1b, wb["ffn_w_up"][l], vec("ffn_b_up"), fconv_w, vec("ffn_conv_b"))
        down = (act, wb["ffn_w_down"][l], vec("ffn_b_down"), z1, vec("ln1_g"), vec("ln1_b"), vec("ln2_g"),
                vec("ln2_b"))
        z2, x2 = _ffn_down_ln_fwd(*down) if l < DEPTH - 1 else (None, None)
        saved.append(dict(x=xl, a_in=a_in, qkv=qkv, c_in=c_in, hc=hc, attn_out=attn_out, lse=lse, cat=cat, z1=z1,
                          x1b=x1b, hfb=hfb, fhc=fhc, act=act, z2=z2, conv_w=conv_w, fconv_w=fconv_w, bs_t=bs_t))
        xl = x2

    grads = {}
    per_layer = {k: [None] * DEPTH for k in (
        "b_in", "conv_dw_w", "conv_dw_b", "conv_ln_g", "conv_ln_b", "gmlp_ln_g", "gmlp_ln_b", "gmlp_w_s",
        "gmlp_b_s", "b_out", "ln1_g", "ln1_b", "ffn_b_up", "ffn_conv_w", "ffn_conv_b", "ffn_b_down", "ln2_g", "ln2_b")}
    dbias_all = []
    dz2, dz2b, loss_part, dg2, db2 = _ffn_down_ln_loss(*down, target)
    loss = jnp.sum(loss_part)
    grad_x = None
    tok = jnp.zeros((), f32)
    for l in reversed(range(DEPTH)):
        sv = saved[l]
        vec = lambda name: sp[name][l][None, :] + tok
        per_layer["ln2_g"][l] = dg2[0]
        per_layer["ln2_b"][l] = db2[0]
        dw_down, db_down = _wgrad(sv["act"], dz2b, 512, "ffn_down_wgrad")
        tok = sink.put("ffn_w_down", l, dw_down, tok)
        per_layer["ffn_b_down"][l] = db_down[0]
        dhf, dfcw, dfcb = _ffn_down_gate_bwd(dz2b, wb["ffn_w_down"][l], sv["hfb"], sv["fhc"], sv["fconv_w"])
        per_layer["ffn_conv_w"][l] = dfcw[:FFN_CONV_WIDTH]
        per_layer["ffn_conv_b"][l] = dfcb[0]
        dw_up, db_up = _wgrad(sv["x1b"], dhf, 1408, "ffn_up_wgrad")
        tok = sink.put("ffn_w_up", l, dw_up, tok)
        per_layer["ffn_b_up"][l] = db_up[0]
        dz1, dz1b, dg1, db1 = _dgrad_ln_bwd(dhf, wb["ffn_w_up"][l], dz2, sv["z1"], vec("ln1_g"), "ffn_up_dgrad_ln")
        per_layer["ln1_g"][l] = dg1[0]
        per_layer["ln1_b"][l] = db1[0]
        dw_out, db_out = _wgrad(sv["cat"], dz1b, D_MODEL, "outproj_wgrad")
        tok = sink.put("w_out", l, dw_out, tok)
        per_layer["b_out"][l] = db_out[0]
        dco, dgo, *perm = _outproj_dgrad(dz1b, wb["w_out"][l], sv["attn_out"], sv["lse"])
        d_outs, stats = perm[:len(DILATIONS)], perm[len(DILATIONS):]
        d_c, dglg, dglb, dws, dbs = _gmlp_bwd(sv["c_in"], dgo, vec("gmlp_ln_g"), vec("gmlp_ln_b"), sp["gmlp_w_s"][l],
                                              sv["bs_t"])
        per_layer["gmlp_ln_g"][l] = dglg[0]
        per_layer["gmlp_ln_b"][l] = dglb[0]
        per_layer["gmlp_w_s"][l] = dws
        per_layer["gmlp_b_s"][l] = dbs[:, :GMLP_GROUPS].T
        dqkvs = []
        for p, d in enumerate(DILATIONS):
            dqkv, dbias = _attn_bwd_pattern(sv["qkv"][p], d_outs[p], stats[p], bias[p], d)
            dqkvs.append(dqkv)
            dbias_all.append(dbias)
        d_a, dcw, dcb, dclg, dclb = _conv_bwd(sv["a_in"], sv["hc"], dco, sv["conv_w"], vec("conv_ln_g"),
                                              vec("conv_ln_b"))
        per_layer["conv_dw_w"][l] = dcw[:CONV_WIDTH]
        per_layer["conv_dw_b"][l] = dcb[0]
        per_layer["conv_ln_g"][l] = dclg[0]
        per_layer["conv_ln_b"][l] = dclb[0]
        dh = _attn_bwd_merge(d_a, dqkvs, d_c)
        dw_in, db_in = _wgrad(sv["x"], dh, IN_CH, "inproj_wgrad")
        tok = sink.put("w_in", l, dw_in, tok)
        per_layer["b_in"][l] = db_in[0]
        if l > 0:
            pv = saved[l - 1]
            dz2, dz2b, dg2, db2 = _dgrad_ln_bwd(dh, wb["w_in"][l], dz1, pv["z2"], sp["ln2_g"][l - 1][None, :] + tok,
                                                "inproj_dgrad_ln")
        else:
            grad_x = _dgrad_ln_bwd(dh, wb["w_in"][l], dz1, None, None, "inproj_dgrad")
    for k, v in per_layer.items():
        grads[k] = jnp.stack(v)
    dbias_cat = jnp.stack(dbias_all)
    bk_cat = jnp.concatenate([buckets] * DEPTH, axis=0)
    grads["rel_bias_table"] = _bias_table_grad(dbias_cat, bk_cat)[:, :ATTN_HEADS]
    return loss, grad_x, grads, sink.finish(grad_x)


N_CHIPS = 4
BIG = {"w_in": (D_MODEL, IN_CH, 1), "w_out": (D_MODEL, D_MODEL, 0),
       "ffn_w_up": (D_MODEL, 2 * D_FF, 1), "ffn_w_down": (D_FF, D_MODEL, 0)}
SMALL = ("b_in", "conv_dw_w", "conv_dw_b", "conv_ln_g", "conv_ln_b", "rel_bias_table", "gmlp_ln_g", "gmlp_ln_b",
         "gmlp_w_s", "gmlp_b_s", "b_out", "ln1_g", "ln1_b", "ffn_b_up", "ffn_conv_w", "ffn_conv_b", "ffn_b_down",
         "ln2_g", "ln2_b")
SMALL_SHARDED = ("conv_dw_w", "ffn_conv_w")
WEIGHTS = ("w_in", "b_in", "conv_dw_w", "conv_dw_b", "conv_ln_g", "conv_ln_b", "rel_bias_table", "gmlp_ln_g",
           "gmlp_ln_b", "gmlp_w_s", "gmlp_b_s", "w_out", "b_out", "ln1_g", "ln1_b", "ffn_w_up", "ffn_b_up",
           "ffn_conv_w", "ffn_conv_b", "ffn_w_down", "ffn_b_down", "ln2_g", "ln2_b")
ANY = pl.BlockSpec(memory_space=pl.ANY)


def _position():
    return lax.axis_index("x"), lax.axis_index("y"), lax.axis_index("c")


def _other_chips(x, y):
    return [(1 - x, y), (x, 1 - y), (1 - x, 1 - y)]


def _cast_bf16(a):
    R, C = a.shape
    T = 128

    def body(a_ref, o_ref):
        o_ref[...] = a_ref[...].astype(bf16)

    return pl.pallas_call(
        body, grid=(R // T,), name="cast_bf16", out_shape=jax.ShapeDtypeStruct((R, C), bf16),
        in_specs=[pl.BlockSpec((T, C), lambda i: (i, 0))], out_specs=pl.BlockSpec((T, C), lambda i: (i, 0)),
        compiler_params=_cp(("parallel",), 16),
    )(a)


def _chip_slot(ref, name, l, p):
    K, N, ax = BIG[name]
    if ax == 1:
        sz = N // N_CHIPS
        return ref.at[l, :, pl.ds(pl.multiple_of(p * sz, 128), sz)]
    sz = K // N_CHIPS
    return ref.at[l, pl.ds(pl.multiple_of(p * sz, 16), sz), :]


def _gather_weights(shards, conv_w, fconv_w):
    names = list(shards)
    n_big = len(names)
    n_t = n_big + 2
    n_chip = 3 * n_t
    n_pass = 3 * n_big

    def body(*refs):
        ins = refs[:n_t]
        outs = refs[n_t:2 * n_t]
        send_sems, recv_sems, pass_send, pass_recv, local_sems = refs[2 * n_t:]
        x, y, c = _position()
        me = 2 * x + y
        chips = _other_chips(x, y)

        def src(t):
            return ins[t].at[c] if t < n_big else ins[t]

        def slot(t, l, p):
            return _chip_slot(outs[t], names[t], l, p) if t < n_big else outs[t].at[p]

        locs, cps = [], []
        for t in range(n_t):
            for l in (range(DEPTH) if t < n_big else (0,)):
                loc = pltpu.make_async_copy(ins[t].at[l] if t < n_big else ins[t], slot(t, l, me),
                                            local_sems.at[DEPTH * t + l])
                loc.start()
                locs.append(loc)
            for k, (px, py) in enumerate(chips):
                cp = pltpu.make_async_remote_copy(
                    src_ref=src(t), dst_ref=slot(t, c, me), send_sem=send_sems.at[3 * t + k],
                    recv_sem=recv_sems.at[3 * t + k], device_id=(px, py, c), device_id_type=MESH_ID)
                cp.start()
                cps.append(cp)
        for t in range(n_t):
            for k, (px, py) in enumerate(chips):
                landed = slot(t, c, 2 * px + py)
                pltpu.make_async_remote_copy(
                    src_ref=src(t), dst_ref=landed, send_sem=send_sems.at[3 * t + k],
                    recv_sem=recv_sems.at[3 * t + k], device_id=(px, py, c), device_id_type=MESH_ID).wait_recv()
                if t < n_big:
                    cp = pltpu.make_async_remote_copy(
                        src_ref=landed, dst_ref=landed, send_sem=pass_send.at[3 * t + k],
                        recv_sem=pass_recv.at[3 * t + k], device_id=(x, y, 1 - c), device_id_type=MESH_ID)
                    cp.start()
                    cps.append(cp)
        for t in range(n_big):
            for k, (px, py) in enumerate(chips):
                from_sibling = slot(t, 1 - c, 2 * px + py)
                pltpu.make_async_remote_copy(
                    src_ref=from_sibling, dst_ref=from_sibling, send_sem=pass_send.at[3 * t + k],
                    recv_sem=pass_recv.at[3 * t + k], device_id=(x, y, 1 - c), device_id_type=MESH_ID).wait_recv()
        for cp in cps:
            cp.wait_send()
        for loc in locs:
            loc.wait()

    ins = [shards[n] for n in names] + [conv_w, fconv_w]
    out_shape = [jax.ShapeDtypeStruct((DEPTH, BIG[n][0], BIG[n][1]), bf16) for n in names]
    out_shape += [jax.ShapeDtypeStruct((N_CHIPS,) + conv_w.shape, f32), jax.ShapeDtypeStruct((N_CHIPS,) + fconv_w.shape, f32)]
    outs = pl.pallas_call(
        body, name="gather_weights", out_shape=tuple(out_shape), in_specs=[ANY] * n_t, out_specs=tuple([ANY] * n_t),
        scratch_shapes=[pltpu.SemaphoreType.DMA((n_chip,)), pltpu.SemaphoreType.DMA((n_chip,)),
                        pltpu.SemaphoreType.DMA((n_pass,)), pltpu.SemaphoreType.DMA((n_pass,)),
                        pltpu.SemaphoreType.DMA((DEPTH * n_t,))],
    )(*ins)
    return dict(zip(names, outs[:n_big])), outs[-2], outs[-1]


LATE_WEIGHTS = ("w_out", "ffn_w_up", "ffn_w_down")
HBM = pl.BlockSpec(memory_space=pltpu.HBM)
SEM = pl.BlockSpec(memory_space=pltpu.SEMAPHORE)


def _cast_into_full(shard, name, chip_arr):
    K, N, ax = BIG[name]
    k, n = _shard_shape(name)
    T = 64
    nrt = k // T

    def body(p_ref, a_ref, o_ref):
        o_ref[...] = a_ref[...].astype(bf16)

    if ax == 1:
        out_spec = pl.BlockSpec((None, T, n), lambda l, i, p: (l, i, p[0]))
    else:
        out_spec = pl.BlockSpec((None, T, n), lambda l, i, p: (l, p[0] * nrt + i, 0))
    return pl.pallas_call(
        body, name="cast_into_full", out_shape=jax.ShapeDtypeStruct((DEPTH, K, N), bf16),
        grid_spec=pltpu.PrefetchScalarGridSpec(
            num_scalar_prefetch=1, grid=(DEPTH, nrt),
            in_specs=[pl.BlockSpec((None, T, n), lambda l, i, p: (l, i, 0))], out_specs=out_spec),
        compiler_params=_cp(("parallel", "parallel"), 16),
    )(chip_arr, shard)


def _late_copies(refs, send_sems, recv_sems):
    x, y, c = _position()
    me = 2 * x + y
    idx = 0
    for ref, name in zip(refs, LATE_WEIGHTS):
        for l in range(DEPTH):
            for px, py in _other_chips(x, y):
                def copy(p, ref=ref, name=name, l=l, px=px, py=py, idx=idx):
                    part = _chip_slot(ref, name, l, p)
                    return pltpu.make_async_remote_copy(
                        src_ref=part, dst_ref=part, send_sem=send_sems.at[idx], recv_sem=recv_sems.at[idx],
                        device_id=(px, py, c), device_id_type=MESH_ID)
                yield copy(me), copy(2 * px + py)
                idx += 1


N_LATE_COPIES = 3 * DEPTH * len(LATE_WEIGHTS)


def _gather_start(fulls, after):
    n = len(fulls)

    def body(*refs):
        ins = refs[:n]
        send_sems, recv_sems = refs[n + 1:n + 3]
        token = refs[-1]
        for sent, _ in _late_copies(ins, send_sems, recv_sems):
            sent.start()
        token[...] = jnp.zeros_like(token)

    outs = pl.pallas_call(
        body, name="gather_start",
        out_shape=(pltpu.SemaphoreType.DMA((N_LATE_COPIES,)), pltpu.SemaphoreType.DMA((N_LATE_COPIES,)))
        + tuple(pltpu.HBM(f.shape, f.dtype) for f in fulls) + (jax.ShapeDtypeStruct((SUBLANES, LANES), f32),),
        in_specs=(HBM,) * n + (ANY,),
        out_specs=(SEM, SEM) + (HBM,) * n + (pl.BlockSpec(memory_space=pltpu.VMEM),),
        input_output_aliases={t: 2 + t for t in range(n)},
        compiler_params=pltpu.CompilerParams(has_side_effects=pltpu.SideEffectType.DATAFLOW_SIDE_EFFECTING),
    )(*[pltpu.with_memory_space_constraint(f, pltpu.HBM) for f in fulls], after)
    return outs[0], outs[1], outs[2:2 + n], outs[-1]


def _gather_wait(send_sems, recv_sems, fulls, after):
    n = len(fulls)

    def body(*refs):
        ins = refs[:n]
        send_ref, recv_ref = refs[n:n + 2]
        for sent, landed in _late_copies(ins, send_ref, recv_ref):
            sent.wait_send()
            landed.wait_recv()

    return pl.pallas_call(
        body, name="gather_wait", out_shape=tuple(pltpu.HBM(f.shape, f.dtype) for f in fulls),
        in_specs=(HBM,) * n + (SEM, SEM, ANY), out_specs=(HBM,) * n,
        input_output_aliases={t: t for t in range(n)},
        compiler_params=pltpu.CompilerParams(has_side_effects=pltpu.SideEffectType.DATAFLOW_SIDE_EFFECTING),
    )(*fulls, send_sems, recv_sems, after)


def _half(ref, name, c):
    K, N, ax = BIG[name]
    if ax == 1:
        return ref.at[pl.ds(pl.multiple_of(c * (K // 2), 8), K // 2), :]
    return ref.at[:, pl.ds(pl.multiple_of(c * (N // 2), 128), N // 2)]


def _half_shape(name):
    K, N, ax = BIG[name]
    return (K // 2, N) if ax == 1 else (K, N // 2)


def _shard_of_half(ref, name, q):
    K, N, ax = BIG[name]
    if ax == 1:
        sz = N // N_CHIPS
        return ref.at[:, pl.ds(pl.multiple_of(q * sz, 128), sz)]
    sz = K // N_CHIPS
    return ref.at[pl.ds(pl.multiple_of(q * sz, 16), sz), :]


def _shard_half_shape(name):
    K, N, ax = BIG[name]
    return (K // 2, N // N_CHIPS) if ax == 1 else (K // N_CHIPS, N // 2)


def _shard_shape(name):
    K, N, ax = BIG[name]
    return (K, N // N_CHIPS) if ax == 1 else (K // N_CHIPS, N)


def _pair_copies(names, srcs, lands, send_sems, recv_sems):
    x, y, c = _position()
    for idx, (name, src, land) in enumerate(zip(names, srcs, lands)):
        yield pltpu.make_async_remote_copy(
            src_ref=_half(src, name, 1 - c), dst_ref=land, send_sem=send_sems.at[idx], recv_sem=recv_sems.at[idx],
            device_id=(x, y, 1 - c), device_id_type=MESH_ID)


def _pair_exchange_start(tag, tensors):
    names = [n for n, _ in tensors]
    n = len(tensors)
    lands = [lax.empty(_half_shape(nm), f32) for nm in names]

    def body(*refs):
        for cp in _pair_copies(names, refs[:n], refs[n:2 * n], refs[2 * n], refs[2 * n + 1]):
            cp.start()
        refs[-1][...] = jnp.zeros_like(refs[-1])

    args = [g for _, g in tensors] + lands
    outs = pl.pallas_call(
        body, name="grad_pair_start_" + tag,
        out_shape=(pltpu.SemaphoreType.DMA((n,)), pltpu.SemaphoreType.DMA((n,)))
        + tuple(pltpu.HBM(a.shape, a.dtype) for a in args) + (jax.ShapeDtypeStruct((SUBLANES, LANES), f32),),
        in_specs=(HBM,) * (2 * n), out_specs=(SEM, SEM) + (HBM,) * (2 * n) + (pl.BlockSpec(memory_space=pltpu.VMEM),),
        input_output_aliases={t: 2 + t for t in range(2 * n)},
        compiler_params=pltpu.CompilerParams(has_side_effects=pltpu.SideEffectType.DATAFLOW_SIDE_EFFECTING),
    )(*[pltpu.with_memory_space_constraint(a, pltpu.HBM) for a in args])
    return (tag, names, outs[0], outs[1], outs[2:2 + 2 * n]), outs[-1]


def _pair_exchange_wait(state, after):
    tag, names, send_sems, recv_sems, bufs = state
    n = len(names)

    def body(*refs):
        for cp in _pair_copies(names, refs[:n], refs[n:2 * n], refs[2 * n], refs[2 * n + 1]):
            cp.wait_send()
            cp.wait_recv()

    outs = pl.pallas_call(
        body, name="grad_pair_wait_" + tag, out_shape=tuple(pltpu.HBM(a.shape, a.dtype) for a in bufs),
        in_specs=(HBM,) * (2 * n) + (SEM, SEM, ANY), out_specs=(HBM,) * (2 * n),
        input_output_aliases={t: t for t in range(2 * n)},
        compiler_params=pltpu.CompilerParams(has_side_effects=pltpu.SideEffectType.DATAFLOW_SIDE_EFFECTING),
    )(*bufs, send_sems, recv_sems, after)
    return list(zip(names, outs[:n], outs[n:]))


def _pair_add(g, rcv, name, c_arr):
    K, N, ax = BIG[name]
    hr, hc = _half_shape(name)
    T = 128
    nrt = hr // T

    def body(c_ref, g_ref, r_ref, o_ref):
        o_ref[...] = (g_ref[...] + r_ref[...]).astype(bf16)

    if ax == 1:
        g_spec = pl.BlockSpec((T, hc), lambda i, c: (c[0] * nrt + i, 0))
    else:
        g_spec = pl.BlockSpec((T, hc), lambda i, c: (i, c[0]))
    plain = pl.BlockSpec((T, hc), lambda i, c: (i, 0))
    return pl.pallas_call(
        body, name="grad_pair_add", out_shape=jax.ShapeDtypeStruct((hr, hc), bf16),
        grid_spec=pltpu.PrefetchScalarGridSpec(num_scalar_prefetch=1, grid=(nrt,), in_specs=[g_spec, plain],
                                               out_specs=plain),
        compiler_params=_cp(("parallel",), 32),
    )(c_arr, g, rcv)


def _chip_copies(names, srcs, lands, send_sems, recv_sems):
    x, y, c = _position()
    me = 2 * x + y
    idx = 0
    for name, src, land in zip(names, srcs, lands):
        for px, py in _other_chips(x, y):
            def copy(q, row, name=name, src=src, land=land, px=px, py=py, idx=idx):
                return pltpu.make_async_remote_copy(
                    src_ref=_shard_of_half(src, name, q), dst_ref=land.at[row], send_sem=send_sems.at[idx],
                    recv_sem=recv_sems.at[idx], device_id=(px, py, c), device_id_type=MESH_ID)
            yield copy(2 * px + py, me), copy(me, 2 * px + py)
            idx += 1


def _chip_exchange_start(tag, tensors):
    names = [n for n, _ in tensors]
    n = len(tensors)
    lands = [lax.empty((N_CHIPS,) + _shard_half_shape(nm), g.dtype) for nm, g in tensors]

    def body(*refs):
        send_sems, recv_sems = refs[2 * n:2 * n + 2]
        for sent, _ in _chip_copies(names, refs[:n], refs[n:2 * n], send_sems, recv_sems):
            sent.start()
        refs[-1][...] = jnp.zeros_like(refs[-1])

    args = [g for _, g in tensors] + lands
    outs = pl.pallas_call(
        body, name="grad_chip_start_" + tag,
        out_shape=(pltpu.SemaphoreType.DMA((3 * n,)), pltpu.SemaphoreType.DMA((3 * n,)))
        + tuple(pltpu.HBM(a.shape, a.dtype) for a in args) + (jax.ShapeDtypeStruct((SUBLANES, LANES), f32),),
        in_specs=(HBM,) * (2 * n), out_specs=(SEM, SEM) + (HBM,) * (2 * n) + (pl.BlockSpec(memory_space=pltpu.VMEM),),
        input_output_aliases={t: 2 + t for t in range(2 * n)},
        compiler_params=pltpu.CompilerParams(has_side_effects=pltpu.SideEffectType.DATAFLOW_SIDE_EFFECTING),
    )(*[pltpu.with_memory_space_constraint(a, pltpu.HBM) for a in args])
    return (tag, names, outs[0], outs[1], outs[2:2 + 2 * n]), outs[-1]


def _chip_exchange_wait(state, after):
    tag, names, send_sems, recv_sems, bufs = state
    n = len(names)

    def body(*refs):
        for sent, landed in _chip_copies(names, refs[:n], refs[n:2 * n], refs[2 * n], refs[2 * n + 1]):
            sent.wait_send()
            landed.wait_recv()

    outs = pl.pallas_call(
        body, name="grad_chip_wait_" + tag, out_shape=tuple(pltpu.HBM(a.shape, a.dtype) for a in bufs),
        in_specs=(HBM,) * (2 * n) + (SEM, SEM, ANY), out_specs=(HBM,) * (2 * n),
        input_output_aliases={t: t for t in range(2 * n)},
        compiler_params=pltpu.CompilerParams(has_side_effects=pltpu.SideEffectType.DATAFLOW_SIDE_EFFECTING),
    )(*bufs, send_sems, recv_sems, after)
    return list(zip(names, outs[:n], outs[n:]))


def _sum_chips(name, half, land, chip_arr):
    K, N, ax = BIG[name]
    R, C = _shard_half_shape(name)
    T = 64
    nrt = R // T

    def body(p_ref, own_ref, land_ref, o_ref):
        parts = [jnp.where(p_ref[0] == q, own_ref[...], land_ref[q]).astype(f32) for q in range(N_CHIPS)]
        o_ref[...] = ((parts[0] + parts[1]) + parts[2]) + parts[3]

    if ax == 1:
        own_spec = pl.BlockSpec((T, C), lambda i, p: (i, p[0]))
    else:
        own_spec = pl.BlockSpec((T, C), lambda i, p: (p[0] * nrt + i, 0))
    return pl.pallas_call(
        body, name="grad_sum_chips", out_shape=jax.ShapeDtypeStruct((R, C), f32),
        grid_spec=pltpu.PrefetchScalarGridSpec(
            num_scalar_prefetch=1, grid=(nrt,),
            in_specs=[own_spec, pl.BlockSpec((N_CHIPS, T, C), lambda i, p: (0, i, 0))],
            out_specs=pl.BlockSpec((T, C), lambda i, p: (i, 0))),
        compiler_params=_cp(("parallel",), 32),
    )(chip_arr, half, land)


def _pair_swap(halves):
    n_t = len(halves)

    def body(*refs):
        ins = refs[:n_t]
        outs = refs[n_t:2 * n_t]
        send_sems, recv_sems = refs[2 * n_t:]
        x, y, c = _position()
        cps = []
        for t in range(n_t):
            cp = pltpu.make_async_remote_copy(
                src_ref=ins[t], dst_ref=outs[t], send_sem=send_sems.at[t], recv_sem=recv_sems.at[t],
                device_id=(x, y, 1 - c), device_id_type=MESH_ID)
            cp.start()
            cps.append(cp)
        for cp in cps:
            cp.wait()

    return pl.pallas_call(
        body, name="grad_pair_swap", out_shape=tuple(jax.ShapeDtypeStruct(h.shape, h.dtype) for h in halves),
        in_specs=[ANY] * n_t, out_specs=tuple([ANY] * n_t),
        scratch_shapes=[pltpu.SemaphoreType.DMA((n_t,)), pltpu.SemaphoreType.DMA((n_t,))],
    )(*halves)


def _adamw_halves(own, other, w, m, v, name, l, c_arr, prev):
    K, N, ax = BIG[name]
    R, C = _shard_shape(name)
    hr, hc = _shard_half_shape(name)
    T = 64
    nrt = hr // T
    c1 = 1.0 / (1.0 - ADAM_B1 ** ADAM_STEP)
    c2 = 1.0 / (1.0 - ADAM_B2 ** ADAM_STEP)

    def body(c_ref, own_ref, oth_ref, w_ref, m_ref, v_ref, *rest):
        g_ref, d_ref, nm_ref, nv_ref = rest[-4:]
        gg = jnp.where(pl.program_id(0) == c_ref[0], own_ref[...], oth_ref[...])
        nm = ADAM_B1 * m_ref[...] + (1.0 - ADAM_B1) * gg
        nv = ADAM_B2 * v_ref[...] + (1.0 - ADAM_B2) * (gg * gg)
        g_ref[...] = gg
        nm_ref[...] = nm
        nv_ref[...] = nv
        d_ref[...] = -ADAM_LR * ((nm * c1) / (jnp.sqrt(nv * c2) + ADAM_EPS) + ADAM_WD * w_ref[...])

    half = pl.BlockSpec((T, hc), lambda h, i, c: (i, 0))
    if ax == 1:
        full = pl.BlockSpec((None, T, hc), lambda h, i, c: (l, h * nrt + i, 0))
    else:
        full = pl.BlockSpec((None, T, hc), lambda h, i, c: (l, i, h))
    sd = jax.ShapeDtypeStruct((DEPTH, R, C), f32)
    args = [c_arr, own, other, w, m, v]
    in_specs = [half, half, full, full, full]
    aliases = {}
    if prev is not None:
        args += list(prev)
        in_specs += [ANY] * 4
        aliases = {6 + k: k for k in range(4)}
    return pl.pallas_call(
        body, name="adamw_" + name, out_shape=(sd, sd, sd, sd),
        grid_spec=pltpu.PrefetchScalarGridSpec(num_scalar_prefetch=1, grid=(2, nrt), in_specs=in_specs,
                                               out_specs=(full, full, full, full)),
        input_output_aliases=aliases,
        compiler_params=_cp(("arbitrary", "arbitrary"), 32),
    )(*args)


class _GradExchange:
    GROUPS = (("l1", tuple((n, DEPTH - 1) for n in BIG)),
              ("l0_ffn", (("ffn_w_down", 0), ("ffn_w_up", 0))),
              ("l0_mix", (("w_out", 0), ("w_in", 0))))

    def __init__(self):
        self.c_arr = jnp.reshape(lax.axis_index("c"), (1,)).astype(jnp.int32)
        self.chip_arr = jnp.reshape(2 * lax.axis_index("x") + lax.axis_index("y"), (1,)).astype(jnp.int32)
        self.grads = {}
        self.pair_started = {}
        self.chip_started = {}

    def _advance(self, after, tok):
        for tag, _ in self.GROUPS:
            if tag not in self.pair_started or tag in self.chip_started:
                continue
            arrived = _pair_exchange_wait(self.pair_started[tag], after)
            pair = [(n, _pair_add(g, r, n, self.c_arr)) for n, g, r in arrived]
            self.chip_started[tag], token = _chip_exchange_start(tag, pair)
            tok = tok + token[0, 0]
        return tok

    def put(self, name, layer, g, tok):
        self.grads[(name, layer)] = g
        tok = self._advance(g, tok)
        for tag, keys in self.GROUPS:
            if tag in self.pair_started or not all(k in self.grads for k in keys):
                continue
            self.pair_started[tag], token = _pair_exchange_start(tag, [(n, self.grads[(n, l)]) for n, l in keys])
            tok = tok + token[0, 0]
        return tok

    def finish(self, after):
        self._advance(after, jnp.zeros((), f32))
        keys, own = [], []
        for tag, group in self.GROUPS:
            landed = _chip_exchange_wait(self.chip_started[tag], after)
            own += [_sum_chips(n, half, land, self.chip_arr) for n, half, land in landed]
            keys += list(group)
        other = _pair_swap(own)
        return dict(zip(keys, zip(own, other)))


def _small_allreduce(buf):
    R = buf.shape[0]

    def body(in_ref, out_ref, sibling, slots, send_sems, recv_sems):
        x, y, c = _position()
        me = 2 * x + y
        swap = pltpu.make_async_remote_copy(
            src_ref=in_ref, dst_ref=sibling, send_sem=send_sems.at[0], recv_sem=recv_sems.at[0],
            device_id=(x, y, 1 - c), device_id_type=MESH_ID)
        swap.start()
        swap.wait()
        slots[me] = in_ref[...] + sibling[...]
        cps = []
        for k, (px, py) in enumerate(_other_chips(x, y)):
            cp = pltpu.make_async_remote_copy(
                src_ref=slots.at[me], dst_ref=slots.at[me], send_sem=send_sems.at[1 + k], recv_sem=recv_sems.at[1 + k],
                device_id=(px, py, c), device_id_type=MESH_ID)
            cp.start()
            cps.append(cp)
        for k, (px, py) in enumerate(_other_chips(x, y)):
            pltpu.make_async_remote_copy(
                src_ref=slots.at[me], dst_ref=slots.at[2 * px + py], send_sem=send_sems.at[1 + k],
                recv_sem=recv_sems.at[1 + k], device_id=(px, py, c), device_id_type=MESH_ID).wait_recv()
        for cp in cps:
            cp.wait_send()
        out_ref[...] = ((slots[0] + slots[1]) + slots[2]) + slots[3]

    vm = pl.BlockSpec(memory_space=pltpu.VMEM)
    return pl.pallas_call(
        body, name="small_allreduce", out_shape=jax.ShapeDtypeStruct((R, 128), f32), in_specs=[vm], out_specs=vm,
        scratch_shapes=[pltpu.VMEM((R, 128), f32), pltpu.VMEM((N_CHIPS, R, 128), f32),
                        pltpu.SemaphoreType.DMA((N_CHIPS,)), pltpu.SemaphoreType.DMA((N_CHIPS,))],
        compiler_params=pltpu.CompilerParams(vmem_limit_bytes=40 * MIB),
    )(buf)


PACK_UNIT = 1024


def _pack(arrs):
    parts = []
    for a in arrs:
        flat = a.reshape(-1)
        n = -(-flat.shape[0] // PACK_UNIT) * PACK_UNIT
        parts.append(jnp.pad(flat, (0, n - flat.shape[0])))
    return jnp.concatenate(parts).reshape(-1, 128)


def _unpack(buf, shapes):
    flat = buf.reshape(-1)
    out, off = [], 0
    for shp in shapes:
        n = int(np.prod(shp))
        out.append(flat[off:off + n].reshape(shp))
        off += -(-n // PACK_UNIT) * PACK_UNIT
    return out


def kernel(x, w_in, b_in, conv_dw_w, conv_dw_b, conv_ln_g, conv_ln_b, rel_bias_table, gmlp_ln_g, gmlp_ln_b, gmlp_w_s, gmlp_b_s, w_out, b_out, ln1_g, ln1_b, ffn_w_up, ffn_b_up, ffn_conv_w, ffn_conv_b, ffn_w_down, ffn_b_down, ln2_g, ln2_b, loss_target, m_w_in, m_b_in, m_conv_dw_w, m_conv_dw_b, m_conv_ln_g, m_conv_ln_b, m_rel_bias_table, m_gmlp_ln_g, m_gmlp_ln_b, m_gmlp_w_s, m_gmlp_b_s, m_w_out, m_b_out, m_ln1_g, m_ln1_b, m_ffn_w_up, m_ffn_b_up, m_ffn_conv_w, m_ffn_conv_b, m_ffn_w_down, m_ffn_b_down, m_ln2_g, m_ln2_b, v_w_in, v_b_in, v_conv_dw_w, v_conv_dw_b, v_conv_ln_g, v_conv_ln_b, v_rel_bias_table, v_gmlp_ln_g, v_gmlp_ln_b, v_gmlp_w_s, v_gmlp_b_s, v_w_out, v_b_out, v_ln1_g, v_ln1_b, v_ffn_w_up, v_ffn_b_up, v_ffn_conv_w, v_ffn_conv_b, v_ffn_w_down, v_ffn_b_down, v_ln2_g, v_ln2_b):
    w = dict(w_in=w_in, b_in=b_in, conv_dw_w=conv_dw_w, conv_dw_b=conv_dw_b, conv_ln_g=conv_ln_g, conv_ln_b=conv_ln_b,
             rel_bias_table=rel_bias_table, gmlp_ln_g=gmlp_ln_g, gmlp_ln_b=gmlp_ln_b, gmlp_w_s=gmlp_w_s,
             gmlp_b_s=gmlp_b_s, w_out=w_out, b_out=b_out, ln1_g=ln1_g, ln1_b=ln1_b, ffn_w_up=ffn_w_up,
             ffn_b_up=ffn_b_up, ffn_conv_w=ffn_conv_w, ffn_conv_b=ffn_conv_b, ffn_w_down=ffn_w_down,
             ffn_b_down=ffn_b_down, ln2_g=ln2_g, ln2_b=ln2_b)
    m = dict(w_in=m_w_in, b_in=m_b_in, conv_dw_w=m_conv_dw_w, conv_dw_b=m_conv_dw_b, conv_ln_g=m_conv_ln_g,
             conv_ln_b=m_conv_ln_b, rel_bias_table=m_rel_bias_table, gmlp_ln_g=m_gmlp_ln_g, gmlp_ln_b=m_gmlp_ln_b,
             gmlp_w_s=m_gmlp_w_s, gmlp_b_s=m_gmlp_b_s, w_out=m_w_out, b_out=m_b_out, ln1_g=m_ln1_g, ln1_b=m_ln1_b,
             ffn_w_up=m_ffn_w_up, ffn_b_up=m_ffn_b_up, ffn_conv_w=m_ffn_conv_w, ffn_conv_b=m_ffn_conv_b,
             ffn_w_down=m_ffn_w_down, ffn_b_down=m_ffn_b_down, ln2_g=m_ln2_g, ln2_b=m_ln2_b)
    v = dict(w_in=v_w_in, b_in=v_b_in, conv_dw_w=v_conv_dw_w, conv_dw_b=v_conv_dw_b, conv_ln_g=v_conv_ln_g,
             conv_ln_b=v_conv_ln_b, rel_bias_table=v_rel_bias_table, gmlp_ln_g=v_gmlp_ln_g, gmlp_ln_b=v_gmlp_ln_b,
             gmlp_w_s=v_gmlp_w_s, gmlp_b_s=v_gmlp_b_s, w_out=v_w_out, b_out=v_b_out, ln1_g=v_ln1_g, ln1_b=v_ln1_b,
             ffn_w_up=v_ffn_w_up, ffn_b_up=v_ffn_b_up, ffn_conv_w=v_ffn_conv_w, ffn_conv_b=v_ffn_conv_b,
             ffn_w_down=v_ffn_w_down, ffn_b_down=v_ffn_b_down, ln2_g=v_ln2_g, ln2_b=v_ln2_b)

    chip_arr = jnp.reshape(2 * lax.axis_index("x") + lax.axis_index("y"), (1,)).astype(jnp.int32)
    shards = {"w_in": _cast_bf16(w_in.reshape(-1, w_in.shape[-1])).reshape(w_in.shape)}
    wb, conv_stack, fconv_stack = _gather_weights(shards, conv_dw_w, ffn_conv_w)
    send_sems, recv_sems, in_flight, token = _gather_start(
        [_cast_into_full(w[n], n, chip_arr) for n in LATE_WEIGHTS], conv_stack)
    sp = {n: w[n] for n in SMALL}
    sp["conv_dw_w"] = jnp.moveaxis(conv_stack, 0, 2).reshape(DEPTH, CONV_WIDTH, CONV_CH)
    sp["ffn_conv_w"] = jnp.moveaxis(fconv_stack, 0, 2).reshape(DEPTH, FFN_CONV_WIDTH, 2 * D_FF)
    sp["b_in"] = sp["b_in"] + token[0, 0]

    def late_weights(after):
        return dict(zip(LATE_WEIGHTS, _gather_wait(send_sems, recv_sems, in_flight, after)))

    sink = _GradExchange()
    loss_local, grad_x, grads, big = _local_step(x[0], loss_target[0], wb, late_weights, sp, sink)

    small_shapes = [(1,)] + [grads[n].shape for n in SMALL]
    summed = _unpack(_small_allreduce(_pack([loss_local.reshape(1)] + [grads[n] for n in SMALL])), small_shapes)
    loss = summed[0].reshape(())
    small = dict(zip(SMALL, summed[1:]))
    chip = 2 * lax.axis_index("x") + lax.axis_index("y")
    for n in SMALL_SHARDED:
        width = w[n].shape[-1]
        small[n] = lax.dynamic_slice_in_dim(small[n], chip * width, width, axis=2)

    g_out, d_out, m_out, v_out = {}, {}, {}, {}
    for n in BIG:
        outs = None
        for l in range(DEPTH):
            own, other = big[(n, l)]
            outs = _adamw_halves(own, other, w[n], m[n], v[n], n, l, sink.c_arr, outs)
        g_out[n], d_out[n], m_out[n], v_out[n] = outs
    shapes = [small[n].shape for n in SMALL]
    packed = [_pack([src[n] for n in SMALL]) for src in (small, w, m, v)]
    upd = _adamw(*packed, "adamw_small")
    for dst, buf in zip((d_out, m_out, v_out), upd):
        dst.update(zip(SMALL, _unpack(buf, shapes)))
    g_out.update(small)

    return (loss, grad_x[None], *[g_out[n] for n in WEIGHTS], *[d_out[n] for n in WEIGHTS],
            *[m_out[n] for n in WEIGHTS], *[v_out[n] for n in WEIGHTS])
```

```python
import functools
import math

import numpy as np
import jax
import jax.numpy as jnp
from jax import lax
from jax.experimental import pallas as pl
from jax.experimental.pallas import tpu as pltpu

f32 = jnp.float32
bf16 = jnp.bfloat16

D_MODEL = 1024
DEPTH = 2
HEAD_DIM = 64
CONV_CH = 256
CONV_WIDTH = 31
ATTN_HEADS = 8
ATTN_CH = ATTN_HEADS * HEAD_DIM
DILATIONS = (1, 4, 16)
ATTN_BLOCK = 128
N_BUCKETS = 32
MAX_DISTANCE = 2048
GMLP_CH = 256
GMLP_GROUPS = 4
GMLP_GROUP_DIM = GMLP_CH // GMLP_GROUPS
CHUNK = 128
IN_CH = 2 * CONV_CH + 3 * ATTN_CH + 2 * GMLP_CH
D_FF = 2816
FFN_CONV_WIDTH = 3
LN_EPS = 1e-5
ALPHA = (2.0 * DEPTH) ** 0.25
ADAM_LR = 0.001
ADAM_B1 = 0.9
ADAM_B2 = 0.999
ADAM_EPS = 1e-08
ADAM_WD = 0.01
ADAM_STEP = 10

CONV_HALO = 32
FFN_HALO = 8
NEG = -1e30
MIB = 2 ** 20
NT_DIMS = (((1,), (1,)), ((), ()))
TN_DIMS = (((0,), (0,)), ((), ()))
MESH_ID = pl.DeviceIdType.MESH


def _cp(sem, vmem_mib):
    return pltpu.CompilerParams(dimension_semantics=sem, vmem_limit_bytes=vmem_mib * MIB)


def _resident(shape):
    nd = len(shape)
    return pl.BlockSpec(shape, lambda *_: (0,) * nd, pipeline_mode=pl.Buffered(1))


def _acc(shape):
    nd = len(shape)
    return pl.BlockSpec(shape, lambda *_: (0,) * nd)


def _sig(x):
    return 1.0 / (1.0 + jnp.exp(-x))


def _ln_stats(z):
    mu = jnp.mean(z, axis=-1, keepdims=True)
    zc = z - mu
    var = jnp.mean(zc * zc, axis=-1, keepdims=True)
    rstd = lax.rsqrt(var + LN_EPS)
    return zc * rstd, rstd


def _ln_bwd(dy, xhat, rstd, g):
    dxh = dy * g
    m1 = jnp.mean(dxh, axis=-1, keepdims=True)
    m2 = jnp.mean(dxh * xhat, axis=-1, keepdims=True)
    return rstd * (dxh - m1 - xhat * m2)


def _colsum(x):
    return jnp.sum(x, axis=0, keepdims=True)


def _t5_bucket_np(dist):
    max_exact = N_BUCKETS // 2
    dd = np.maximum(dist, 1).astype(np.float64)
    large = max_exact + (np.log(dd / max_exact) / math.log(MAX_DISTANCE / max_exact)
                         * (N_BUCKETS - max_exact)).astype(np.int32)
    large = np.minimum(large, N_BUCKETS - 1)
    return np.where(dist < max_exact, dist, large).astype(np.int32)


def _bucket_ids():
    qi = np.arange(ATTN_BLOCK)[:, None]
    kj = np.arange(2 * ATTN_BLOCK)[None, :]
    dist = np.clip(qi + ATTN_BLOCK - kj, 0, None)
    return np.stack([_t5_bucket_np(dist * d) for d in DILATIONS]).astype(np.int32)


LANES = 128
QKV_CH = 3 * ATTN_CH
PERM_TILE = 512


def _slabs(n, rows):
    return [pltpu.VMEM((rows, LANES), f32)] * n


def _rows_of(slab, r, n, d):
    return slab[...] if d == 1 else slab[pl.ds(r, n, stride=d), :]


def _set_rows_of(slab, r, n, d, val):
    if d == 1:
        slab[...] = val
    else:
        slab[pl.ds(r, n, stride=d), :] = val


def _perm_spec(d, ch):
    return pl.BlockSpec((d, PERM_TILE // d, ch), lambda i: (0, i, 0))


def _perm_shape(S, d, ch, dtype):
    return jax.ShapeDtypeStruct((d, S // d, ch), dtype)


def _inproj_fwd(x, w, b):
    S = x.shape[0]
    T = PERM_TILE
    nsl = QKV_CH // LANES

    def body(x_ref, w_ref, b_ref, a_ref, c_ref, *rest):
        q_refs = rest[:len(DILATIONS)]
        slabs = rest[len(DILATIONS):]
        h = jnp.dot(x_ref[...].astype(bf16), w_ref[...], preferred_element_type=f32) + b_ref[...]
        a_ref[...] = h[:, :2 * CONV_CH]
        q0 = 2 * CONV_CH
        c_ref[...] = h[:, q0 + QKV_CH:]
        for j in range(nsl):
            piece = h[:, q0 + LANES * j:q0 + LANES * (j + 1)]
            if LANES * j < ATTN_CH:
                piece = piece * (HEAD_DIM ** -0.5)
            slabs[j][...] = piece
        for d, q_ref in zip(DILATIONS, q_refs):
            for r in range(d):
                for j in range(nsl):
                    q_ref[r, :, LANES * j:LANES * (j + 1)] = _rows_of(slabs[j], r, T // d, d).astype(bf16)

    row = lambda c: pl.BlockSpec((T, c), lambda i: (i, 0))
    return pl.pallas_call(
        body, grid=(S // T,), name="inproj_fwd",
        out_shape=(jax.ShapeDtypeStruct((S, 2 * CONV_CH), f32), jax.ShapeDtypeStruct((S, 2 * GMLP_CH), f32))
        + tuple(_perm_shape(S, d, QKV_CH, bf16) for d in DILATIONS),
        in_specs=[row(D_MODEL), _resident((D_MODEL, IN_CH)), _resident((1, IN_CH))],
        out_specs=(row(2 * CONV_CH), row(2 * GMLP_CH)) + tuple(_perm_spec(d, QKV_CH) for d in DILATIONS),
        scratch_shapes=_slabs(nsl, T),
        compiler_params=_cp(("parallel",), 48),
    )(x, w, b)


CONV_GROUP = 64


def _window_rolls(starts):
    groups = {}
    for s in starts:
        groups.setdefault((-s) % SUBLANES, []).append(s)
    return dict(sorted(groups.items()))


def _conv_fwd(a_in, dw_w, dw_b, ln_g, ln_b):
    S = a_in.shape[0]
    T = 512
    hb = T // CONV_HALO

    def body(a_ref, halo_ref, w_ref, b_ref, g_ref, be_ref, out_ref, hc_ref, buf):
        i = pl.program_id(0)
        am = a_ref[...]
        ah = halo_ref[...]
        hgh = ah[:, :CONV_CH] * _sig(ah[:, CONV_CH:])
        buf[0:CONV_HALO, :] = jnp.where(i > 0, hgh, 0.0)
        buf[CONV_HALO:, :] = am[:, :CONV_CH] * _sig(am[:, CONV_CH:])
        starts = _window_rolls(range(CONV_HALO - (CONV_WIDTH - 1), CONV_HALO + 1))
        slabs = [slice(LANES * j, LANES * (j + 1)) for j in range(CONV_CH // LANES)]

        def step(g, _):
            r0 = pl.multiple_of(g * CONV_GROUP, CONV_GROUP)
            rows = pl.ds(r0, CONV_GROUP)
            for cs in slabs:
                ext = buf[pl.ds(r0, CONV_GROUP + CONV_HALO), cs]
                acc = jnp.broadcast_to(b_ref[:, cs], (CONV_GROUP, LANES))
                for b, ss in starts.items():
                    rolled = ext if b == 0 else pltpu.roll(ext, b, 0)
                    for s in ss:
                        k = s - (CONV_HALO - (CONV_WIDTH - 1))
                        acc = acc + w_ref[k:k + 1, cs] * rolled[s + b:s + b + CONV_GROUP]
                hc_ref[rows, cs] = acc
            return 0

        lax.fori_loop(0, T // CONV_GROUP, step, 0)
        xhat, _ = _ln_stats(hc_ref[...])
        y = xhat * g_ref[...] + be_ref[...]
        out_ref[...] = (y * _sig(y)).astype(bf16)

    return pl.pallas_call(
        body, grid=(S // T,), name="conv_fwd",
        out_shape=(jax.ShapeDtypeStruct((S, CONV_CH), bf16), jax.ShapeDtypeStruct((S, CONV_CH), f32)),
        in_specs=[pl.BlockSpec((T, 2 * CONV_CH), lambda i: (i, 0)),
                  pl.BlockSpec((CONV_HALO, 2 * CONV_CH), lambda i: (jnp.maximum(i * hb - 1, 0), 0)),
                  _acc((32, CONV_CH)), _acc((1, CONV_CH)), _acc((1, CONV_CH)), _acc((1, CONV_CH))],
        out_specs=(pl.BlockSpec((T, CONV_CH), lambda i: (i, 0)), pl.BlockSpec((T, CONV_CH), lambda i: (i, 0))),
        scratch_shapes=[pltpu.VMEM((T + CONV_HALO, CONV_CH), f32)],
        compiler_params=_cp(("parallel",), 32),
    )(a_in, a_in, dw_w, dw_b, ln_g, ln_b)


def _bias_build(table, buckets):
    def body(t_ref, bk_ref, o_ref):
        h = pl.program_id(1)
        ids = bk_ref[0]
        acc = jnp.zeros((ATTN_BLOCK, 2 * ATTN_BLOCK), f32)
        for b in range(N_BUCKETS):
            acc = jnp.where(ids == b, t_ref[b, h], acc)
        row = lax.broadcasted_iota(jnp.int32, acc.shape, 0)
        col = lax.broadcasted_iota(jnp.int32, acc.shape, 1)
        o_ref[0, 0] = jnp.where((col >= row) & (col <= row + ATTN_BLOCK), acc, NEG)

    return pl.pallas_call(
        body, grid=(len(DILATIONS), ATTN_HEADS), name="bias_build",
        out_shape=jax.ShapeDtypeStruct((len(DILATIONS), ATTN_HEADS, ATTN_BLOCK, 2 * ATTN_BLOCK), f32),
        in_specs=[pl.BlockSpec(memory_space=pltpu.SMEM),
                  pl.BlockSpec((1, ATTN_BLOCK, 2 * ATTN_BLOCK), lambda p, h: (p, 0, 0))],
        out_specs=pl.BlockSpec((1, 1, ATTN_BLOCK, 2 * ATTN_BLOCK), lambda p, h: (p, h, 0, 0)),
        compiler_params=_cp(("arbitrary", "arbitrary"), 16),
    )(table, buckets)


def _head_tile(tile, h, col):
    lane_head = lax.broadcasted_iota(jnp.int32, tile.shape, 1) // 16
    return jnp.where(lane_head == h, col, tile)


HEAD_PAIRS = ATTN_HEADS // 2
UNITS_PER_BLOCK = ATTN_HEADS


def _attn_tile(L):
    return min(512, L)


def _mask_logits(logits, first_block, n):
    if not first_block:
        return logits
    col = lax.broadcasted_iota(jnp.int32, logits.shape, 1)
    return jnp.where((col >= ATTN_BLOCK) | (n > 0), logits, NEG)


def _head_lanes(a):
    lane = lax.broadcasted_iota(jnp.int32, (ATTN_BLOCK, LANES), 1)
    return (lane < HEAD_DIM) if a == 0 else (lane >= HEAD_DIM)


def _pair_keys(cur_ref, halo_ref, part, b, j):
    B = ATTN_BLOCK
    c0 = part * ATTN_CH + LANES * j
    own = cur_ref[B * b:B * (b + 1), c0:c0 + LANES]
    prev = halo_ref[:, LANES * j:LANES * (j + 1)] if b == 0 else cur_ref[B * (b - 1):B * b, c0:c0 + LANES]
    return jnp.concatenate([prev, own], axis=0)


def _attn_fwd_pattern(qkv, bias, d):
    _, L, _ = qkv.shape
    B = ATTN_BLOCK
    QB = _attn_tile(L)
    nsb = QB // B
    U = nsb * UNITS_PER_BLOCK

    def body(cur_ref, hk_ref, hv_ref, b_ref, o_ref, lse_ref, lg, pb):
        n = pl.program_id(1)
        for b in range(nsb):
            for j in range(HEAD_PAIRS):
                q2 = cur_ref[B * b:B * (b + 1), LANES * j:LANES * (j + 1)]
                k2 = _pair_keys(cur_ref, hk_ref, 1, b, j)
                for a in range(2):
                    u = (b * HEAD_PAIRS + j) * 2 + a
                    qm = jnp.where(_head_lanes(a), q2, jnp.zeros_like(q2))
                    logits = lax.dot_general(qm, k2, NT_DIMS, preferred_element_type=f32) + b_ref[2 * j + a]
                    lg[B * u:B * (u + 1), :] = _mask_logits(logits, b == 0, n)
        m = jnp.max(lg[...], axis=1, keepdims=True)
        p = jnp.exp(lg[...] - m)
        s = jnp.sum(p, axis=1, keepdims=True)
        pb[...] = p.astype(bf16)
        lse = m + jnp.log(s)
        inv = 1.0 / s
        for b in range(nsb):
            tile = jnp.zeros((B, B), f32)
            for j in range(HEAD_PAIRS):
                v2 = _pair_keys(cur_ref, hv_ref, 2, b, j)
                outs = []
                for a in range(2):
                    u = (b * HEAD_PAIRS + j) * 2 + a
                    rows = slice(B * u, B * (u + 1))
                    outs.append(jnp.dot(pb[rows, :], v2, preferred_element_type=f32) * inv[rows])
                    tile = _head_tile(tile, 2 * j + a, lse[rows])
                o_ref[B * b:B * (b + 1), LANES * j:LANES * (j + 1)] = jnp.where(_head_lanes(0), outs[0], outs[1])
            lse_ref[B * b:B * (b + 1), :] = tile

    halo = lambda part: pl.BlockSpec((None, B, ATTN_CH), lambda r, n: (r, jnp.maximum(n * nsb - 1, 0), part))
    tile_spec = lambda c: pl.BlockSpec((None, QB, c), lambda r, n: (r, n, 0))
    return pl.pallas_call(
        body, grid=(d, L // QB), name=f"attn_fwd_d{d}",
        out_shape=(jax.ShapeDtypeStruct((d, L, ATTN_CH), f32), jax.ShapeDtypeStruct((d, L, B), f32)),
        in_specs=[tile_spec(QKV_CH), halo(1), halo(2), _resident((ATTN_HEADS, B, 2 * B))],
        out_specs=(tile_spec(ATTN_CH), tile_spec(B)),
        scratch_shapes=[pltpu.VMEM((U * B, 2 * B), f32), pltpu.VMEM((U * B, 2 * B), bf16)],
        compiler_params=_cp(("parallel", "parallel"), 40),
    )(qkv, qkv, qkv, bias)


def _attn_merge(parts):
    S = parts[0][0].shape[0] * parts[0][0].shape[1]
    T = PERM_TILE
    nsl = ATTN_CH // LANES
    n_p = len(DILATIONS)

    def body(*refs):
        ins = refs[:2 * n_p]
        out_ref, lse_ref = refs[2 * n_p:2 * n_p + 2]
        slabs = refs[2 * n_p + 2:]
        lses = []
        for p, d in enumerate(DILATIONS):
            o_ref, l_ref = ins[2 * p], ins[2 * p + 1]
            osl = slabs[p * (nsl + 1):p * (nsl + 1) + nsl]
            lsl = slabs[p * (nsl + 1) + nsl]
            for r in range(d):
                for j in range(nsl):
                    _set_rows_of(osl[j], r, T // d, d, o_ref[r, :, LANES * j:LANES * (j + 1)])
                _set_rows_of(lsl, r, T // d, d, l_ref[r])
            lses.append(lsl[...])
        big = functools.reduce(jnp.maximum, lses)
        ws = [jnp.exp(l - big) for l in lses]
        tot = functools.reduce(lambda a_, b_: a_ + b_, ws)
        lse_ref[...] = big + jnp.log(tot)
        ws = [w / tot for w in ws]
        for j in range(nsl):
            acc = jnp.zeros((T, LANES), f32)
            for p in range(n_p):
                wa = ws[p][:, 32 * j:32 * j + 1]
                wb = ws[p][:, 32 * j + 16:32 * j + 17]
                lane = lax.broadcasted_iota(jnp.int32, (T, LANES), 1)
                acc = acc + jnp.where(lane < HEAD_DIM, wa, wb) * slabs[p * (nsl + 1) + j][...]
            out_ref[:, LANES * j:LANES * (j + 1)] = acc.astype(bf16)

    in_specs, args = [], []
    for (o, l), d in zip(parts, DILATIONS):
        in_specs += [_perm_spec(d, ATTN_CH), _perm_spec(d, ATTN_BLOCK)]
        args += [o, l]
    row = lambda c: pl.BlockSpec((T, c), lambda i: (i, 0))
    return pl.pallas_call(
        body, grid=(S // T,), name="attn_merge",
        out_shape=(jax.ShapeDtypeStruct((S, ATTN_CH), bf16), jax.ShapeDtypeStruct((S, ATTN_BLOCK), f32)),
        in_specs=in_specs, out_specs=(row(ATTN_CH), row(ATTN_BLOCK)),
        scratch_shapes=_slabs(n_p * (nsl + 1), T),
        compiler_params=_cp(("parallel",), 40),
    )(*args)


def _attn_fwd(qkvs, bias):
    parts = [_attn_fwd_pattern(q, bias[p], d) for p, (q, d) in enumerate(zip(qkvs, DILATIONS))]
    return _attn_merge(parts)


def _tril_bf16(w):
    row = lax.broadcasted_iota(jnp.int32, (CHUNK, CHUNK), 0)
    col = lax.broadcasted_iota(jnp.int32, (CHUNK, CHUNK), 1)
    return jnp.where(col <= row, w, 0.0).astype(bf16)


def _gmlp_fwd(c_in, ln_g, ln_b, w_s, b_s_t):
    S = c_in.shape[0]
    T = 512

    def body(c_ref, g_ref, be_ref, w_ref, bs_ref, out_ref, mix):
        c = c_ref[...]
        xhat, _ = _ln_stats(c[:, GMLP_CH:])
        vb = (xhat * g_ref[...] + be_ref[...]).astype(bf16)
        for g in range(GMLP_GROUPS):
            wt = _tril_bf16(w_ref[g])
            cs = slice(GMLP_GROUP_DIM * g, GMLP_GROUP_DIM * (g + 1))
            for ci in range(T // CHUNK):
                rs = slice(CHUNK * ci, CHUNK * (ci + 1))
                mix[rs, cs] = jnp.dot(wt, vb[rs, cs], preferred_element_type=f32) + bs_ref[:, g:g + 1]
        out_ref[...] = (c[:, :GMLP_CH] * mix[...]).astype(bf16)

    return pl.pallas_call(
        body, grid=(S // T,), name="gmlp_fwd",
        out_shape=jax.ShapeDtypeStruct((S, GMLP_CH), bf16),
        in_specs=[pl.BlockSpec((T, 2 * GMLP_CH), lambda i: (i, 0)), _acc((1, GMLP_CH)), _acc((1, GMLP_CH)),
                  _acc((GMLP_GROUPS, CHUNK, CHUNK)), _acc((CHUNK, GMLP_GROUPS))],
        out_specs=pl.BlockSpec((T, GMLP_CH), lambda i: (i, 0)),
        scratch_shapes=[pltpu.VMEM((T, GMLP_CH), f32)],
        compiler_params=_cp(("parallel",), 32),
    )(c_in, ln_g, ln_b, w_s, b_s_t)


def _outproj_ln_fwd(conv_out, attn_out, gm_out, w, b, x, ln_g, ln_b):
    S = x.shape[0]
    T = 512

    def body(co_ref, ao_ref, go_ref, w_ref, b_ref, x_ref, g_ref, be_ref, cat_ref, z_ref, yb_ref):
        cat = jnp.concatenate([co_ref[...], ao_ref[...], go_ref[...]], axis=1)
        cat_ref[...] = cat
        z = jnp.dot(cat, w_ref[...], preferred_element_type=f32) + b_ref[...] + ALPHA * x_ref[...]
        z_ref[...] = z
        xhat, _ = _ln_stats(z)
        yb_ref[...] = (xhat * g_ref[...] + be_ref[...]).astype(bf16)

    row = lambda c: pl.BlockSpec((T, c), lambda i: (i, 0))
    return pl.pallas_call(
        body, grid=(S // T,), name="outproj_ln_fwd",
        out_shape=(jax.ShapeDtypeStruct((S, D_MODEL), bf16), jax.ShapeDtypeStruct((S, D_MODEL), f32),
                   jax.ShapeDtypeStruct((S, D_MODEL), bf16)),
        in_specs=[row(CONV_CH), row(ATTN_CH), row(GMLP_CH), _resident((D_MODEL, D_MODEL)), _acc((1, D_MODEL)),
                  row(D_MODEL), _acc((1, D_MODEL)), _acc((1, D_MODEL))],
        out_specs=(row(D_MODEL), row(D_MODEL), row(D_MODEL)),
        compiler_params=_cp(("parallel",), 40),
    )(conv_out, attn_out, gm_out, w, b, x, ln_g, ln_b)


GATE_ROWS = 32
GATE_COLS = 128
GATE_MM_COLS = 2816
SUBLANES = 8


def _gate_cols(c0):
    return slice(c0, c0 + GATE_COLS), slice(D_FF + c0, D_FF + c0 + GATE_COLS)


def _bcast_rows(ref, k, cs):
    return jnp.broadcast_to(ref[k:k + 1, cs], (GATE_ROWS, GATE_COLS))


def _fold_rows(z):
    acc = z[0:SUBLANES]
    for r in range(SUBLANES, GATE_ROWS, SUBLANES):
        acc = acc + z[r:r + SUBLANES]
    return acc


def _ffn_up_gate_fwd(x1b, w, b, conv_w, conv_b):
    S = x1b.shape[0]
    T = 256
    H = FFN_HALO
    K = FFN_CONV_WIDTH

    def body(x_ref, w_ref, b_ref, cw_ref, cb_ref, hfb_ref, hc_ref, act_ref, hbuf, carry):
        @pl.when(pl.program_id(0) == 0)
        def _():
            carry[...] = jnp.zeros_like(carry)
        x = x_ref[...]
        for m0 in range(0, D_FF, GATE_MM_COLS):
            for cm in (slice(m0, m0 + GATE_MM_COLS), slice(D_FF + m0, D_FF + m0 + GATE_MM_COLS)):
                h = jnp.dot(x, w_ref[:, cm], preferred_element_type=f32) + b_ref[:, cm]
                hbuf[:, cm] = h
                hfb_ref[:, cm] = h.astype(bf16)
            for c0 in range(m0, m0 + GATE_MM_COLS, GATE_COLS):
                cols = _gate_cols(c0)
                wts = [[_bcast_rows(cw_ref, k, cs) for k in range(K)] + [_bcast_rows(cb_ref, 0, cs)] for cs in cols]

                def step(rg, tails, cols=cols, wts=wts):
                    rows = pl.ds(pl.multiple_of(rg * GATE_ROWS, GATE_ROWS), GATE_ROWS)
                    hc, new_tails = [], []
                    for cs, wt, tail in zip(cols, wts, tails):
                        h = hbuf[rows, cs]
                        ext = jnp.concatenate([tail, h], axis=0)
                        acc = wt[K] + wt[K - 1] * h
                        for back in range(1, K):
                            acc = acc + wt[K - 1 - back] * pltpu.roll(ext, back, 0)[H:]
                        hc_ref[rows, cs] = acc
                        hc.append(acc)
                        new_tails.append(h[GATE_ROWS - H:])
                    act_ref[rows, cols[0]] = (hc[0] * _sig(hc[0]) * hc[1]).astype(bf16)
                    return tuple(new_tails)

                tails = lax.fori_loop(0, T // GATE_ROWS, step, tuple(carry[:, cs] for cs in cols), unroll=True)
                for cs, tail in zip(cols, tails):
                    carry[:, cs] = tail

    row = lambda c: pl.BlockSpec((T, c), lambda i: (i, 0))
    return pl.pallas_call(
        body, grid=(S // T,), name="ffn_up_gate_fwd",
        out_shape=(jax.ShapeDtypeStruct((S, 2 * D_FF), bf16), jax.ShapeDtypeStruct((S, 2 * D_FF), f32),
                   jax.ShapeDtypeStruct((S, D_FF), bf16)),
        in_specs=[row(D_MODEL), _resident((D_MODEL, 2 * D_FF)), _acc((1, 2 * D_FF)), _acc((8, 2 * D_FF)),
                  _acc((1, 2 * D_FF))],
        out_specs=(row(2 * D_FF), row(2 * D_FF), row(D_FF)),
        scratch_shapes=[pltpu.VMEM((T, 2 * D_FF), f32), pltpu.VMEM((H, 2 * D_FF), f32)],
        compiler_params=_cp(("arbitrary",), 56),
    )(x1b, w, b, conv_w, conv_b)


def _ffn_down_ln_fwd(act, w, b, z1, ln1_g, ln1_b, ln_g, ln_b):
    S = act.shape[0]
    T = 512

    def body(a_ref, w_ref, b_ref, z1_ref, g1_ref, be1_ref, g_ref, be_ref, z_ref, y_ref):
        subs = [slice(s0, s0 + T // 2) for s0 in (0, T // 2)]
        zs = [jnp.dot(a_ref[rs, :], w_ref[...], preferred_element_type=f32) + b_ref[...]
              + ALPHA * (_ln_stats(z1_ref[rs, :])[0] * g1_ref[...] + be1_ref[...]) for rs in subs]
        for rs, z in zip(subs, zs):
            z_ref[rs, :] = z
            xhat, _ = _ln_stats(z)
            y_ref[rs, :] = xhat * g_ref[...] + be_ref[...]

    row = lambda c: pl.BlockSpec((T, c), lambda i: (i, 0))
    return pl.pallas_call(
        body, grid=(S // T,), name="ffn_down_ln_fwd",
        out_shape=(jax.ShapeDtypeStruct((S, D_MODEL), f32), jax.ShapeDtypeStruct((S, D_MODEL), f32)),
        in_specs=[row(D_FF), _resident((D_FF, D_MODEL)), _acc((1, D_MODEL)), row(D_MODEL)] + [_acc((1, D_MODEL))] * 4,
        out_specs=(row(D_MODEL), row(D_MODEL)),
        compiler_params=_cp(("parallel",), 40),
    )(act, w, b, z1, ln1_g, ln1_b, ln_g, ln_b)


def _ffn_down_ln_loss(act, w, b, z1, ln1_g, ln1_b, ln_g, ln_b, target):
    S = act.shape[0]
    T = 512

    def body(a_ref, w_ref, b_ref, z1_ref, g1_ref, be1_ref, g_ref, be_ref, t_ref, dz_ref, dzb_ref, loss_ref, dg_ref,
             db_ref):
        @pl.when(pl.program_id(0) == 0)
        def _():
            loss_ref[...] = jnp.zeros_like(loss_ref)
            dg_ref[...] = jnp.zeros_like(dg_ref)
            db_ref[...] = jnp.zeros_like(db_ref)
        subs = [slice(s0, s0 + T // 2) for s0 in (0, T // 2)]
        zs = [jnp.dot(a_ref[rs, :], w_ref[...], preferred_element_type=f32) + b_ref[...]
              + ALPHA * (_ln_stats(z1_ref[rs, :])[0] * g1_ref[...] + be1_ref[...]) for rs in subs]
        for rs, z in zip(subs, zs):
            xhat, rstd = _ln_stats(z)
            err = xhat * g_ref[...] + be_ref[...] - t_ref[rs, :]
            loss_ref[...] += _colsum(err * err) * (0.5 / D_MODEL)
            dy = err * (1.0 / D_MODEL)
            dz = _ln_bwd(dy, xhat, rstd, g_ref[...])
            dz_ref[rs, :] = dz
            dzb_ref[rs, :] = dz.astype(bf16)
            dg_ref[...] += _colsum(dy * xhat)
            db_ref[...] += _colsum(dy)

    row = lambda c: pl.BlockSpec((T, c), lambda i: (i, 0))
    vec = jax.ShapeDtypeStruct((1, D_MODEL), f32)
    return pl.pallas_call(
        body, grid=(S // T,), name="ffn_down_ln_loss",
        out_shape=(jax.ShapeDtypeStruct((S, D_MODEL), f32), jax.ShapeDtypeStruct((S, D_MODEL), bf16), vec, vec, vec),
        in_specs=[row(D_FF), _resident((D_FF, D_MODEL)), _acc((1, D_MODEL)), row(D_MODEL)] + [_acc((1, D_MODEL))] * 4
        + [row(D_MODEL)],
        out_specs=(row(D_MODEL), row(D_MODEL), _acc((1, D_MODEL)), _acc((1, D_MODEL)), _acc((1, D_MODEL))),
        compiler_params=_cp(("arbitrary",), 40),
    )(act, w, b, z1, ln1_g, ln1_b, ln_g, ln_b, target)


def _dgrad_ln_bwd(g, w, dz_res, z, ln_g, name):
    S, K = g.shape
    SUB = 256
    T = 2 * SUB if S % (2 * SUB) == 0 else SUB
    with_ln = z is not None

    def body(*refs):
        if with_ln:
            g_ref, w_ref, r_ref, z_ref, lg_ref, dz_ref, dzb_ref, dg_ref, db_ref = refs
        else:
            g_ref, w_ref, r_ref, dx_ref = refs
        subs = [slice(s0, s0 + SUB) for s0 in range(0, T, SUB)]
        dxs = [lax.dot_general(g_ref[rs, :], w_ref[...], NT_DIMS, preferred_element_type=f32) + ALPHA * r_ref[rs, :]
               for rs in subs]
        if not with_ln:
            for rs, dx in zip(subs, dxs):
                dx_ref[rs, :] = dx
            return

        @pl.when(pl.program_id(0) == 0)
        def _():
            dg_ref[...] = jnp.zeros_like(dg_ref)
            db_ref[...] = jnp.zeros_like(db_ref)
        for rs, dx in zip(subs, dxs):
            xhat, rstd = _ln_stats(z_ref[rs, :])
            dz = _ln_bwd(dx, xhat, rstd, lg_ref[...])
            dz_ref[rs, :] = dz
            dzb_ref[rs, :] = dz.astype(bf16)
            dg_ref[...] += _colsum(dx * xhat)
            db_ref[...] += _colsum(dx)

    row = pl.BlockSpec((T, D_MODEL), lambda i: (i, 0))
    vec = jax.ShapeDtypeStruct((1, D_MODEL), f32)
    in_specs = [pl.BlockSpec((T, K), lambda i: (i, 0)), _resident((D_MODEL, K)), row]
    args = [g, w, dz_res]
    if with_ln:
        in_specs += [row, _acc((1, D_MODEL))]
        args += [z, ln_g]
        out_shape = (jax.ShapeDtypeStruct((S, D_MODEL), f32), jax.ShapeDtypeStruct((S, D_MODEL), bf16), vec, vec)
        out_specs = (row, row, _acc((1, D_MODEL)), _acc((1, D_MODEL)))
    else:
        out_shape = jax.ShapeDtypeStruct((S, D_MODEL), f32)
        out_specs = row
    return pl.pallas_call(
        body, grid=(S // T,), name=name, out_shape=out_shape, in_specs=in_specs, out_specs=out_specs,
        compiler_params=_cp(("arbitrary",), 48),
    )(*args)


def _ffn_down_gate_bwd(dzb, w_down, hfb, hc, conv_w):
    S = hc.shape[0]
    T = 256
    H = FFN_HALO
    nt = S // T
    K = FFN_CONV_WIDTH

    def body(dz_ref, w_ref, h_ref, hc_ref, cw_ref, dh_ref, dw_ref, dcb_ref, da_buf, carry):
        @pl.when(pl.program_id(0) == 0)
        def _():
            dw_ref[...] = jnp.zeros_like(dw_ref)
            dcb_ref[...] = jnp.zeros_like(dcb_ref)
            carry[...] = jnp.zeros_like(carry)
        da_buf[...] = lax.dot_general(dz_ref[...], w_ref[...], NT_DIMS, preferred_element_type=f32)
        ngroups = T // GATE_ROWS
        for c0 in range(0, D_FF, GATE_COLS):
            cols = _gate_cols(c0)
            wts = [[_bcast_rows(cw_ref, k, cs) for k in range(K)] for cs in cols]

            def step(it, state, cols=cols, wts=wts):
                heads, accs = state
                rows = pl.ds(pl.multiple_of((ngroups - 1 - it) * GATE_ROWS, GATE_ROWS), GATE_ROWS)
                g = hc_ref[rows, cols[0]]
                v = hc_ref[rows, cols[1]]
                da = da_buf[rows, cols[0]]
                sg = _sig(g)
                dms = (da * v * (sg * (1.0 + g * (1.0 - sg))), da * (g * sg))
                new_heads, new_accs = [], []
                for cs, wt, dm, head, acc in zip(cols, wts, dms, heads, accs):
                    h0 = h_ref[rows, cs].astype(f32)
                    ext = jnp.concatenate([dm, head], axis=0)
                    dh = wt[K - 1] * dm
                    acc_k = [None] * K + [acc[K] + _fold_rows(dm)]
                    acc_k[K - 1] = acc[K - 1] + _fold_rows(dm * h0)
                    for ahead in range(1, K):
                        dk = pltpu.roll(ext, GATE_ROWS + H - ahead, 0)[:GATE_ROWS]
                        dh = dh + wt[K - 1 - ahead] * dk
                        acc_k[K - 1 - ahead] = acc[K - 1 - ahead] + _fold_rows(dk * h0)
                    dh_ref[rows, cs] = dh.astype(bf16)
                    new_heads.append(dm[:H])
                    new_accs.append(tuple(acc_k))
                return tuple(new_heads), tuple(new_accs)

            zero = jnp.zeros((SUBLANES, GATE_COLS), f32)
            init = (tuple(carry[:, cs] for cs in cols), tuple(tuple(zero for _ in range(K + 1)) for _ in cols))
            heads, accs = lax.fori_loop(0, ngroups, step, init, unroll=True)
            for cs, head, acc in zip(cols, heads, accs):
                carry[:, cs] = head
                dcb_ref[:, cs] += _colsum(acc[K])
                for k in range(K):
                    dw_ref[k:k + 1, cs] += _colsum(acc[k])

    tile = lambda c: pl.BlockSpec((T, c), lambda i: (nt - 1 - i, 0))
    return pl.pallas_call(
        body, grid=(nt,), name="ffn_down_gate_bwd",
        out_shape=(jax.ShapeDtypeStruct((S, 2 * D_FF), bf16), jax.ShapeDtypeStruct((8, 2 * D_FF), f32),
                   jax.ShapeDtypeStruct((1, 2 * D_FF), f32)),
        in_specs=[tile(D_MODEL), _resident((D_FF, D_MODEL)), tile(2 * D_FF), tile(2 * D_FF), _acc((8, 2 * D_FF))],
        out_specs=(tile(2 * D_FF), _acc((8, 2 * D_FF)), _acc((1, 2 * D_FF))),
        scratch_shapes=[pltpu.VMEM((T, D_FF), f32), pltpu.VMEM((H, 2 * D_FF), f32)],
        compiler_params=_cp(("arbitrary",), 48),
    )(dzb, w_down, hfb, hc, conv_w)


def _wgrad(a, g, tn, name, rows=1024):
    S, K = a.shape
    N = g.shape[1]
    T = rows if S % rows == 0 else S

    def body(a_ref, g_ref, dw_ref, db_ref):
        @pl.when(pl.program_id(1) == 0)
        def _():
            dw_ref[...] = jnp.zeros_like(dw_ref)
            db_ref[...] = jnp.zeros_like(db_ref)
        gt = g_ref[...]
        dw_ref[...] += lax.dot_general(a_ref[...].astype(bf16), gt, TN_DIMS, preferred_element_type=f32)
        db_ref[...] += _colsum(gt.astype(f32))

    return pl.pallas_call(
        body, grid=(N // tn, S // T), name=name,
        out_shape=(jax.ShapeDtypeStruct((K, N), f32), jax.ShapeDtypeStruct((1, N), f32)),
        in_specs=[pl.BlockSpec((T, K), lambda j, i: (i, 0)), pl.BlockSpec((T, tn), lambda j, i: (i, j))],
        out_specs=(pl.BlockSpec((K, tn), lambda j, i: (0, j)), pl.BlockSpec((1, tn), lambda j, i: (0, j))),
        compiler_params=_cp(("parallel", "arbitrary"), 56),
    )(a, g)


def _outproj_dgrad(dzb, w, attn_out, lse):
    S = dzb.shape[0]
    T = PERM_TILE
    nsl = ATTN_CH // LANES
    n_p = len(DILATIONS)

    def body(g_ref, w_ref, ao_ref, lse_ref, dco_ref, dgo_ref, *rest):
        do_refs = rest[:n_p]
        st_refs = rest[n_p:2 * n_p]
        slabs = rest[2 * n_p:]
        dcat = lax.dot_general(g_ref[...], w_ref[...], NT_DIMS, preferred_element_type=f32)
        dco_ref[...] = dcat[:, :CONV_CH]
        dgo_ref[...] = dcat[:, CONV_CH + ATTN_CH:]
        lane = lax.broadcasted_iota(jnp.int32, (T, LANES), 1)
        st = lse_ref[...]
        for j in range(nsl):
            dO = dcat[:, CONV_CH + LANES * j:CONV_CH + LANES * (j + 1)]
            prod = dO * ao_ref[:, LANES * j:LANES * (j + 1)].astype(f32)
            for a in range(2):
                in_head = (lane < HEAD_DIM) if a == 0 else (lane >= HEAD_DIM)
                delta = jnp.sum(jnp.where(in_head, prod, 0.0), axis=1, keepdims=True)
                st = jnp.where((lane // 16 == 2 * j + a) & (lane % 16 >= 8), delta, st)
            slabs[j][...] = dO
        slabs[nsl][...] = st
        for d, do_ref, st_ref in zip(DILATIONS, do_refs, st_refs):
            for r in range(d):
                for j in range(nsl):
                    do_ref[r, :, LANES * j:LANES * (j + 1)] = _rows_of(slabs[j], r, T // d, d).astype(bf16)
                st_ref[r] = _rows_of(slabs[nsl], r, T // d, d)

    row = lambda c: pl.BlockSpec((T, c), lambda i: (i, 0))
    return pl.pallas_call(
        body, grid=(S // T,), name="outproj_dgrad",
        out_shape=(jax.ShapeDtypeStruct((S, CONV_CH), f32), jax.ShapeDtypeStruct((S, GMLP_CH), f32))
        + tuple(_perm_shape(S, d, ATTN_CH, bf16) for d in DILATIONS)
        + tuple(_perm_shape(S, d, ATTN_BLOCK, f32) for d in DILATIONS),
        in_specs=[row(D_MODEL), _resident((D_MODEL, D_MODEL)), row(ATTN_CH), row(ATTN_BLOCK)],
        out_specs=(row(CONV_CH), row(GMLP_CH)) + tuple(_perm_spec(d, ATTN_CH) for d in DILATIONS)
        + tuple(_perm_spec(d, ATTN_BLOCK) for d in DILATIONS),
        scratch_shapes=_slabs(nsl + 1, T),
        compiler_params=_cp(("parallel",), 40),
    )(dzb, w, attn_out, lse)


def _gmlp_bwd(c_in, dgm, ln_g, ln_b, w_s, b_s_t):
    S = c_in.shape[0]
    T = 512
    nsteps = S // T

    def body(c_ref, dg_ref, g_ref, be_ref, w_ref, bs_ref, dc_ref, dlg_ref, dlb_ref, dw_ref, dbs_ref,
             du_buf, dv_buf, dm_acc):
        i = pl.program_id(0)

        @pl.when(i == 0)
        def _():
            dlg_ref[...] = jnp.zeros_like(dlg_ref)
            dlb_ref[...] = jnp.zeros_like(dlb_ref)
            dw_ref[...] = jnp.zeros_like(dw_ref)
            dm_acc[...] = jnp.zeros_like(dm_acc)
        c = c_ref[...]
        u = c[:, :GMLP_CH]
        xhat, rstd = _ln_stats(c[:, GMLP_CH:])
        vb = (xhat * g_ref[...] + be_ref[...]).astype(bf16)
        dgm_t = dg_ref[...]
        dm_all = dgm_t * u
        for g in range(GMLP_GROUPS):
            wt = _tril_bf16(w_ref[g])
            cs = slice(GMLP_GROUP_DIM * g, GMLP_GROUP_DIM * (g + 1))
            dw_g = jnp.zeros((CHUNK, CHUNK), f32)
            for ci in range(T // CHUNK):
                rs = slice(CHUNK * ci, CHUNK * (ci + 1))
                v_c = vb[rs, cs]
                mixed = jnp.dot(wt, v_c, preferred_element_type=f32) + bs_ref[:, g:g + 1]
                dm = dm_all[rs, cs]
                dmb = dm.astype(bf16)
                du_buf[rs, cs] = dgm_t[rs, cs] * mixed
                dv_buf[rs, cs] = lax.dot_general(wt, dmb, TN_DIMS, preferred_element_type=f32)
                dw_g = dw_g + lax.dot_general(dmb, v_c, NT_DIMS, preferred_element_type=f32)
                dm_acc[:, cs] += dm
            dw_ref[g] += dw_g
        dv = dv_buf[...]
        dvr = _ln_bwd(dv, xhat, rstd, g_ref[...])
        dlg_ref[...] += _colsum(dv * xhat)
        dlb_ref[...] += _colsum(dv)
        dc_ref[:, :GMLP_CH] = du_buf[...].astype(bf16)
        dc_ref[:, GMLP_CH:] = dvr.astype(bf16)

        @pl.when(i == nsteps - 1)
        def _():
            row = lax.broadcasted_iota(jnp.int32, (CHUNK, CHUNK), 0)
            col = lax.broadcasted_iota(jnp.int32, (CHUNK, CHUNK), 1)
            tile = jnp.zeros((CHUNK, CHUNK), f32)
            for g in range(GMLP_GROUPS):
                dw_ref[g] = jnp.where(col <= row, dw_ref[g], 0.0)
                gsum = jnp.sum(dm_acc[:, GMLP_GROUP_DIM * g:GMLP_GROUP_DIM * (g + 1)], axis=1, keepdims=True)
                tile = jnp.where(col == g, gsum, tile)
            dbs_ref[...] = tile

    vec = jax.ShapeDtypeStruct((1, GMLP_CH), f32)
    return pl.pallas_call(
        body, grid=(nsteps,), name="gmlp_bwd",
        out_shape=(jax.ShapeDtypeStruct((S, 2 * GMLP_CH), bf16), vec, vec,
                   jax.ShapeDtypeStruct((GMLP_GROUPS, CHUNK, CHUNK), f32), jax.ShapeDtypeStruct((CHUNK, CHUNK), f32)),
        in_specs=[pl.BlockSpec((T, 2 * GMLP_CH), lambda i: (i, 0)), pl.BlockSpec((T, GMLP_CH), lambda i: (i, 0)),
                  _acc((1, GMLP_CH)), _acc((1, GMLP_CH)), _acc((GMLP_GROUPS, CHUNK, CHUNK)), _acc((CHUNK, GMLP_GROUPS))],
        out_specs=(pl.BlockSpec((T, 2 * GMLP_CH), lambda i: (i, 0)), _acc((1, GMLP_CH)), _acc((1, GMLP_CH)),
                   _acc((GMLP_GROUPS, CHUNK, CHUNK)), _acc((CHUNK, CHUNK))),
        scratch_shapes=[pltpu.VMEM((T, GMLP_CH), f32), pltpu.VMEM((T, GMLP_CH), f32), pltpu.VMEM((CHUNK, GMLP_CH), f32)],
        compiler_params=_cp(("arbitrary",), 32),
    )(c_in, dgm, ln_g, ln_b, w_s, b_s_t)


def _attn_bwd_pattern(qkv, d_out, stats, bias, d):
    _, L, _ = qkv.shape
    B = ATTN_BLOCK
    QB = _attn_tile(L)
    nsb = QB // B
    nt = L // QB
    U = nsb * UNITS_PER_BLOCK
    KV = 2 * ATTN_CH

    def body(cur_ref, hk_ref, hv_ref, do_ref, st_ref, b_ref, dqkv_ref, dbias_ref, lg, dp, pb, dsb, dkv, carry):
        r = pl.program_id(0)
        i = pl.program_id(1)
        n = nt - 1 - i

        @pl.when((r == 0) & (i == 0))
        def _():
            dbias_ref[...] = jnp.zeros_like(dbias_ref)

        @pl.when(i == 0)
        def _():
            carry[...] = jnp.zeros_like(carry)

        def operands(b, j, a):
            rows = slice(B * b, B * (b + 1))
            q2 = cur_ref[rows, LANES * j:LANES * (j + 1)]
            do2 = do_ref[rows, LANES * j:LANES * (j + 1)]
            keep = _head_lanes(a)
            return jnp.where(keep, q2, jnp.zeros_like(q2)), jnp.where(keep, do2, jnp.zeros_like(do2))

        for b in range(nsb):
            for j in range(HEAD_PAIRS):
                k2 = _pair_keys(cur_ref, hk_ref, 1, b, j)
                v2 = _pair_keys(cur_ref, hv_ref, 2, b, j)
                for a in range(2):
                    u = (b * HEAD_PAIRS + j) * 2 + a
                    qm, dom = operands(b, j, a)
                    logits = lax.dot_general(qm, k2, NT_DIMS, preferred_element_type=f32) + b_ref[2 * j + a]
                    lg[B * u:B * (u + 1), :] = _mask_logits(logits, b == 0, n)
                    dp[B * u:B * (u + 1), :] = lax.dot_general(dom, v2, NT_DIMS, preferred_element_type=f32)
        for b in range(nsb):
            for j in range(HEAD_PAIRS):
                for a in range(2):
                    u = (b * HEAD_PAIRS + j) * 2 + a
                    rows = slice(B * u, B * (u + 1))
                    lane0 = 32 * j + 16 * a
                    lse = st_ref[B * b:B * (b + 1), lane0:lane0 + 1]
                    delta = st_ref[B * b:B * (b + 1), lane0 + 8:lane0 + 9]
                    p = jnp.exp(lg[rows, :] - lse)
                    ds = p * (dp[rows, :] - delta)
                    pb[rows, :] = p.astype(bf16)
                    dsb[rows, :] = ds.astype(bf16)
                    dbias_ref[2 * j + a] += ds
        dkv[...] = jnp.zeros_like(dkv)
        for b in range(nsb):
            for j in range(HEAD_PAIRS):
                k2 = _pair_keys(cur_ref, hk_ref, 1, b, j)
                dq, dk2, dv2 = [], None, None
                for a in range(2):
                    u = (b * HEAD_PAIRS + j) * 2 + a
                    rows = slice(B * u, B * (u + 1))
                    qm, dom = operands(b, j, a)
                    ds_u = dsb[rows, :]
                    dq.append(jnp.dot(ds_u, k2, preferred_element_type=f32))
                    dk_u = lax.dot_general(ds_u, qm, TN_DIMS, preferred_element_type=f32)
                    dv_u = lax.dot_general(pb[rows, :], dom, TN_DIMS, preferred_element_type=f32)
                    dk2 = dk_u if dk2 is None else dk2 + dk_u
                    dv2 = dv_u if dv2 is None else dv2 + dv_u
                dq2 = jnp.where(_head_lanes(0), dq[0], dq[1]) * (HEAD_DIM ** -0.5)
                dqkv_ref[B * b:B * (b + 1), LANES * j:LANES * (j + 1)] = dq2.astype(bf16)
                dkv[B * b:B * (b + 2), LANES * j:LANES * (j + 1)] += dk2
                dkv[B * b:B * (b + 2), ATTN_CH + LANES * j:ATTN_CH + LANES * (j + 1)] += dv2
        dkv[QB:, :] += carry[...]
        dqkv_ref[:, ATTN_CH:] = dkv[B:, :].astype(bf16)
        carry[...] = dkv[0:B, :]

    halo = lambda part: pl.BlockSpec((None, B, ATTN_CH),
                                     lambda r, i: (r, jnp.maximum((nt - 1 - i) * nsb - 1, 0), part))
    tile_spec = lambda c: pl.BlockSpec((None, QB, c), lambda r, i: (r, nt - 1 - i, 0))
    return pl.pallas_call(
        body, grid=(d, nt), name=f"attn_bwd_d{d}",
        out_shape=(jax.ShapeDtypeStruct((d, L, QKV_CH), bf16), jax.ShapeDtypeStruct((ATTN_HEADS, B, 2 * B), f32)),
        in_specs=[tile_spec(QKV_CH), halo(1), halo(2), tile_spec(ATTN_CH), tile_spec(B),
                  _resident((ATTN_HEADS, B, 2 * B))],
        out_specs=(tile_spec(QKV_CH), _acc((ATTN_HEADS, B, 2 * B))),
        scratch_shapes=[pltpu.VMEM((U * B, 2 * B), f32), pltpu.VMEM((U * B, 2 * B), f32),
                        pltpu.VMEM((U * B, 2 * B), bf16), pltpu.VMEM((U * B, 2 * B), bf16),
                        pltpu.VMEM((B + QB, KV), f32), pltpu.VMEM((B, KV), f32)],
        compiler_params=_cp(("arbitrary", "arbitrary"), 48),
    )(qkv, qkv, qkv, d_out, stats, bias)


def _attn_bwd_merge(d_a, dqkvs, d_c):
    S = d_a.shape[0]
    T = PERM_TILE
    nsl = QKV_CH // LANES
    n_p = len(DILATIONS)

    def body(da_ref, *rest):
        g_refs = rest[:n_p]
        dc_ref, dh_ref = rest[n_p:n_p + 2]
        slabs = rest[n_p + 2:]
        q0 = 2 * CONV_CH
        dh_ref[:, :q0] = da_ref[...]
        dh_ref[:, q0 + QKV_CH:] = dc_ref[...]
        for p, (d, g_ref) in enumerate(zip(DILATIONS, g_refs)):
            for r in range(d):
                for j in range(nsl):
                    _set_rows_of(slabs[p * nsl + j], r, T // d, d, g_ref[r, :, LANES * j:LANES * (j + 1)].astype(f32))
        for j in range(nsl):
            acc = slabs[j][...]
            for p in range(1, n_p):
                acc = acc + slabs[p * nsl + j][...]
            dh_ref[:, q0 + LANES * j:q0 + LANES * (j + 1)] = acc.astype(bf16)

    row = lambda c: pl.BlockSpec((T, c), lambda i: (i, 0))
    return pl.pallas_call(
        body, grid=(S // T,), name="attn_bwd_merge", out_shape=jax.ShapeDtypeStruct((S, IN_CH), bf16),
        in_specs=[row(2 * CONV_CH)] + [_perm_spec(d, QKV_CH) for d in DILATIONS] + [row(2 * GMLP_CH)],
        out_specs=row(IN_CH), scratch_shapes=_slabs(n_p * nsl, T),
        compiler_params=_cp(("parallel",), 48),
    )(d_a, *dqkvs, d_c)


def _bias_table_grad(dbias, buckets):
    n = dbias.shape[0]

    def body(db_ref, bk_ref, o_ref):
        p = pl.program_id(0)
        h = pl.program_id(1)

        @pl.when((p == 0) & (h == 0))
        def _():
            o_ref[...] = jnp.zeros_like(o_ref)
        ids = bk_ref[0]
        db = db_ref[0, 0]
        row = lax.broadcasted_iota(jnp.int32, (N_BUCKETS, 128), 0)
        lane = lax.broadcasted_iota(jnp.int32, (N_BUCKETS, 128), 1)
        upd = jnp.zeros((N_BUCKETS, 128), f32)
        for b in range(N_BUCKETS):
            s = jnp.sum(jnp.sum(jnp.where(ids == b, db, 0.0), axis=1, keepdims=True), axis=0, keepdims=True)
            upd = jnp.where((row == b) & (lane == h), s, upd)
        o_ref[...] += upd

    return pl.pallas_call(
        body, grid=(n, ATTN_HEADS), name="bias_table_grad",
        out_shape=jax.ShapeDtypeStruct((N_BUCKETS, 128), f32),
        in_specs=[pl.BlockSpec((1, 1, ATTN_BLOCK, 2 * ATTN_BLOCK), lambda p, h: (p, h, 0, 0)),
                  pl.BlockSpec((1, ATTN_BLOCK, 2 * ATTN_BLOCK), lambda p, h: (p, 0, 0))],
        out_specs=_acc((N_BUCKETS, 128)),
        compiler_params=_cp(("arbitrary", "arbitrary"), 16),
    )(dbias, buckets)


def _conv_bwd(a_in, hc, dco, dw_w, ln_g, ln_b):
    S = a_in.shape[0]
    T = 512
    hb = T // CONV_HALO
    nsteps = S // T
    R = T + CONV_HALO
    K = CONV_WIDTH

    def body(a_ref, hc_ref, hcn_ref, d_ref, dn_ref, w_ref, g_ref, be_ref,
             da_ref, dw_ref, dcb_ref, dlg_ref, dlb_ref, ext, dbuf, wacc):
        i = pl.program_id(0)

        @pl.when(i == 0)
        def _():
            wacc[...] = jnp.zeros_like(wacc)
            dcb_ref[...] = jnp.zeros_like(dcb_ref)
            dlg_ref[...] = jnp.zeros_like(dlg_ref)
            dlb_ref[...] = jnp.zeros_like(dlb_ref)
        ext[0:T, :] = hc_ref[...]
        ext[T:, :] = hcn_ref[...]
        xhat, rstd = _ln_stats(ext[...])
        hl = xhat * g_ref[...] + be_ref[...]
        ext[0:T, :] = d_ref[...]
        ext[T:, :] = dn_ref[...]
        sl_ = _sig(hl)
        dhl = ext[...] * (sl_ * (1.0 + hl * (1.0 - sl_)))
        dhc = _ln_bwd(dhl, xhat, rstd, g_ref[...])
        rowi = lax.broadcasted_iota(jnp.int32, (R, CONV_CH), 0)
        dbuf[...] = jnp.where((rowi < T) | (i < nsteps - 1), dhc, 0.0)
        dlg_ref[...] += _colsum(dhl[:T] * xhat[:T])
        dlb_ref[...] += _colsum(dhl[:T])
        dcb_ref[...] += _colsum(dbuf[pl.ds(0, T), :])
        starts = _window_rolls(range(K))
        slabs = [slice(LANES * j, LANES * (j + 1)) for j in range(CONV_CH // LANES)]

        def step(g, _):
            r0 = pl.multiple_of(g * CONV_GROUP, CONV_GROUP)
            rows = pl.ds(r0, CONV_GROUP)
            for j, cs in enumerate(slabs):
                gate_cs = slice(CONV_CH + LANES * j, CONV_CH + LANES * (j + 1))
                win = dbuf[pl.ds(r0, CONV_GROUP + CONV_HALO), cs]
                a = a_ref[rows, cs]
                sg = _sig(a_ref[rows, gate_cs])
                hg = a * sg
                dhg = jnp.zeros((CONV_GROUP, LANES), f32)
                for b, ss in starts.items():
                    rolled = win if b == 0 else pltpu.roll(win, b, 0)
                    for s in ss:
                        k = K - 1 - s
                        dk = rolled[s + b:s + b + CONV_GROUP]
                        dhg = dhg + w_ref[k:k + 1, cs] * dk
                        prod = dk * hg
                        fold = prod[0:SUBLANES]
                        for r in range(SUBLANES, CONV_GROUP, SUBLANES):
                            fold = fold + prod[r:r + SUBLANES]
                        wacc[SUBLANES * k:SUBLANES * (k + 1), cs] += fold
                da_ref[rows, cs] = (dhg * sg).astype(bf16)
                da_ref[rows, gate_cs] = (dhg * hg * (1.0 - sg)).astype(bf16)
            return 0

        lax.fori_loop(0, T // CONV_GROUP, step, 0)

        @pl.when(i == nsteps - 1)
        def _():
            for k in range(K):
                dw_ref[k:k + 1, :] = _colsum(wacc[SUBLANES * k:SUBLANES * (k + 1), :])
            dw_ref[K:, :] = jnp.zeros((32 - K, CONV_CH), f32)

    vec = jax.ShapeDtypeStruct((1, CONV_CH), f32)
    nxt = lambda i: (jnp.minimum((i + 1) * hb, nsteps * hb - 1), 0)
    return pl.pallas_call(
        body, grid=(nsteps,), name="conv_bwd",
        out_shape=(jax.ShapeDtypeStruct((S, 2 * CONV_CH), bf16), jax.ShapeDtypeStruct((32, CONV_CH), f32), vec, vec, vec),
        in_specs=[pl.BlockSpec((T, 2 * CONV_CH), lambda i: (i, 0)),
                  pl.BlockSpec((T, CONV_CH), lambda i: (i, 0)), pl.BlockSpec((CONV_HALO, CONV_CH), nxt),
                  pl.BlockSpec((T, CONV_CH), lambda i: (i, 0)), pl.BlockSpec((CONV_HALO, CONV_CH), nxt),
                  _acc((32, CONV_CH)), _acc((1, CONV_CH)), _acc((1, CONV_CH))],
        out_specs=(pl.BlockSpec((T, 2 * CONV_CH), lambda i: (i, 0)), _acc((32, CONV_CH)), _acc((1, CONV_CH)),
                   _acc((1, CONV_CH)), _acc((1, CONV_CH))),
        scratch_shapes=[pltpu.VMEM((R, CONV_CH), f32), pltpu.VMEM((R, CONV_CH), f32),
                        pltpu.VMEM((SUBLANES * 32, CONV_CH), f32)],
        compiler_params=_cp(("arbitrary",), 32),
    )(a_in, hc, hc, dco, dco, dw_w, ln_g, ln_b)


def _adamw(g, w, m, v, name):
    R, C = g.shape
    T = R
    for cand in (512, 256, 128, 64, 32, 16, 8):
        if R % cand == 0 and cand * C * 4 <= MIB:
            T = cand
            break
    c1 = 1.0 / (1.0 - ADAM_B1 ** ADAM_STEP)
    c2 = 1.0 / (1.0 - ADAM_B2 ** ADAM_STEP)

    def body(g_ref, w_ref, m_ref, v_ref, d_ref, nm_ref, nv_ref):
        gg = g_ref[...]
        nm = ADAM_B1 * m_ref[...] + (1.0 - ADAM_B1) * gg
        nv = ADAM_B2 * v_ref[...] + (1.0 - ADAM_B2) * (gg * gg)
        nm_ref[...] = nm
        nv_ref[...] = nv
        d_ref[...] = -ADAM_LR * ((nm * c1) / (jnp.sqrt(nv * c2) + ADAM_EPS) + ADAM_WD * w_ref[...])

    blk = pl.BlockSpec((T, C), lambda i: (i, 0))
    sd = jax.ShapeDtypeStruct((R, C), f32)
    return pl.pallas_call(
        body, grid=(R // T,), name=name, out_shape=(sd, sd, sd), in_specs=[blk] * 4, out_specs=(blk, blk, blk),
        compiler_params=_cp(("parallel",), 48),
    )(g, w, m, v)


def _pad_rows(a, rows):
    return jnp.pad(a, ((0, rows - a.shape[0]), (0, 0)))


def _local_step(x, target, wb, late_weights, sp, sink):
    buckets = jnp.asarray(_bucket_ids())
    bias = _bias_build(sp["rel_bias_table"], buckets)
    wb = dict(wb)
    saved = []
    xl = x
    for l in range(DEPTH):
        vec = lambda name: sp[name][l][None, :]
        a_in, c_in, *qkv = _inproj_fwd(xl, wb["w_in"][l], vec("b_in"))
        conv_w = _pad_rows(sp["conv_dw_w"][l], 32)
        conv_out, hc = _conv_fwd(a_in, conv_w, vec("conv_dw_b"), vec("conv_ln_g"), vec("conv_ln_b"))
        attn_out, lse = _attn_fwd(qkv, bias)
        bs_t = sp["gmlp_b_s"][l].T
        gm_out = _gmlp_fwd(c_in, vec("gmlp_ln_g"), vec("gmlp_ln_b"), sp["gmlp_w_s"][l], bs_t)
        if l == 0:
            wb.update(late_weights(gm_out))
        cat, z1, x1b = _outproj_ln_fwd(conv_out, attn_out, gm_out, wb["w_out"][l], vec("b_out"), xl,
                                           vec("ln1_g"), vec("ln1_b"))
        fconv_w = _pad_rows(sp["ffn_conv_w"][l], 8)
        hfb, fhc, act = _ffn_up_gate_fwd(x1b, wb["ffn_w_up"][l], vec("ffn_b_up"), fconv_w, vec("ffn_conv_b"))
        down = (act, wb["ffn_w_down"][l], vec("ffn_b_down"), z1, vec("ln1_g"), vec("ln1_b"), vec("ln2_g"),
                vec("ln2_b"))
        z2, x2 = _ffn_down_ln_fwd(*down) if l < DEPTH - 1 else (None, None)
        saved.append(dict(x=xl, a_in=a_in, qkv=qkv, c_in=c_in, hc=hc, attn_out=attn_out, lse=lse, cat=cat, z1=z1,
                          x1b=x1b, hfb=hfb, fhc=fhc, act=act, z2=z2, conv_w=conv_w, fconv_w=fconv_w, bs_t=bs_t))
        xl = x2

    grads = {}
    per_layer = {k: [None] * DEPTH for k in (
        "b_in", "conv_dw_w", "conv_dw_b", "conv_ln_g", "conv_ln_b", "gmlp_ln_g", "gmlp_ln_b", "gmlp_w_s",
        "gmlp_b_s", "b_out", "ln1_g", "ln1_b", "ffn_b_up", "ffn_conv_w", "ffn_conv_b", "ffn_b_down", "ln2_g", "ln2_b")}
    dbias_all = []
    dz2, dz2b, loss_part, dg2, db2 = _ffn_down_ln_loss(*down, target)
    loss = jnp.sum(loss_part)
    grad_x = None
    tok = jnp.zeros((), f32)
    for l in reversed(range(DEPTH)):
        sv = saved[l]
        vec = lambda name: sp[name][l][None, :] + tok
        per_layer["ln2_g"][l] = dg2[0]
        per_layer["ln2_b"][l] = db2[0]
        dw_down, db_down = _wgrad(sv["act"], dz2b, 512, "ffn_down_wgrad")
        tok = sink.put("ffn_w_down", l, dw_down, tok)
        per_layer["ffn_b_down"][l] = db_down[0]
        dhf, dfcw, dfcb = _ffn_down_gate_bwd(dz2b, wb["ffn_w_down"][l], sv["hfb"], sv["fhc"], sv["fconv_w"])
        per_layer["ffn_conv_w"][l] = dfcw[:FFN_CONV_WIDTH]
        per_layer["ffn_conv_b"][l] = dfcb[0]
        dw_up, db_up = _wgrad(sv["x1b"], dhf, 1408, "ffn_up_wgrad")
        tok = sink.put("ffn_w_up", l, dw_up, tok)
        per_layer["ffn_b_up"][l] = db_up[0]
        dz1, dz1b, dg1, db1 = _dgrad_ln_bwd(dhf, wb["ffn_w_up"][l], dz2, sv["z1"], vec("ln1_g"), "ffn_up_dgrad_ln")
        per_layer["ln1_g"][l] = dg1[0]
        per_layer["ln1_b"][l] = db1[0]
        dw_out, db_out = _wgrad(sv["cat"], dz1b, D_MODEL, "outproj_wgrad")
        tok = sink.put("w_out", l, dw_out, tok)
        per_layer["b_out"][l] = db_out[0]
        dco, dgo, *perm = _outproj_dgrad(dz1b, wb["w_out"][l], sv["attn_out"], sv["lse"])
        d_outs, stats = perm[:len(DILATIONS)], perm[len(DILATIONS):]
        d_c, dglg, dglb, dws, dbs = _gmlp_bwd(sv["c_in"], dgo, vec("gmlp_ln_g"), vec("gmlp_ln_b"), sp["gmlp_w_s"][l],
                                              sv["bs_t"])
        per_layer["gmlp_ln_g"][l] = dglg[0]
        per_layer["gmlp_ln_b"][l] = dglb[0]
        per_layer["gmlp_w_s"][l] = dws
        per_layer["gmlp_b_s"][l] = dbs[:, :GMLP_GROUPS].T
        dqkvs = []
        for p, d in enumerate(DILATIONS):
            dqkv, dbias = _attn_bwd_pattern(sv["qkv"][p], d_outs[p], stats[p], bias[p], d)
            dqkvs.append(dqkv)
            dbias_all.append(dbias)
        d_a, dcw, dcb, dclg, dclb = _conv_bwd(sv["a_in"], sv["hc"], dco, sv["conv_w"], vec("conv_ln_g"),
                                              vec("conv_ln_b"))
        per_layer["conv_dw_w"][l] = dcw[:CONV_WIDTH]
        per_layer["conv_dw_b"][l] = dcb[0]
        per_layer["conv_ln_g"][l] = dclg[0]
        per_layer["conv_ln_b"][l] = dclb[0]
        dh = _attn_bwd_merge(d_a, dqkvs, d_c)
        dw_in, db_in = _wgrad(sv["x"], dh, IN_CH, "inproj_wgrad")
        tok = sink.put("w_in", l, dw_in, tok)
        per_layer["b_in"][l] = db_in[0]
        if l > 0:
            pv = saved[l - 1]
            dz2, dz2b, dg2, db2 = _dgrad_ln_bwd(dh, wb["w_in"][l], dz1, pv["z2"], sp["ln2_g"][l - 1][None, :] + tok,
                                                "inproj_dgrad_ln")
        else:
            grad_x = _dgrad_ln_bwd(dh, wb["w_in"][l], dz1, None, None, "inproj_dgrad")
    for k, v in per_layer.items():
        grads[k] = jnp.stack(v)
    dbias_cat = jnp.stack(dbias_all)
    bk_cat = jnp.concatenate([buckets] * DEPTH, axis=0)
    grads["rel_bias_table"] = _bias_table_grad(dbias_cat, bk_cat)[:, :ATTN_HEADS]
    return loss, grad_x, grads, sink.finish(grad_x)


N_CHIPS = 4
BIG = {"w_in": (D_MODEL, IN_CH, 1), "w_out": (D_MODEL, D_MODEL, 0),
       "ffn_w_up": (D_MODEL, 2 * D_FF, 1), "ffn_w_down": (D_FF, D_MODEL, 0)}
SMALL = ("b_in", "conv_dw_w", "conv_dw_b", "conv_ln_g", "conv_ln_b", "rel_bias_table", "gmlp_ln_g", "gmlp_ln_b",
         "gmlp_w_s", "gmlp_b_s", "b_out", "ln1_g", "ln1_b", "ffn_b_up", "ffn_conv_w", "ffn_conv_b", "ffn_b_down",
         "ln2_g", "ln2_b")
SMALL_SHARDED = ("conv_dw_w", "ffn_conv_w")
WEIGHTS = ("w_in", "b_in", "conv_dw_w", "conv_dw_b", "conv_ln_g", "conv_ln_b", "rel_bias_table", "gmlp_ln_g",
           "gmlp_ln_b", "gmlp_w_s", "gmlp_b_s", "w_out", "b_out", "ln1_g", "ln1_b", "ffn_w_up", "ffn_b_up",
           "ffn_conv_w", "ffn_conv_b", "ffn_w_down", "ffn_b_down", "ln2_g", "ln2_b")
ANY = pl.BlockSpec(memory_space=pl.ANY)


def _position():
    return lax.axis_index("x"), lax.axis_index("y"), lax.axis_index("c")


def _other_chips(x, y):
    return [(1 - x, y), (x, 1 - y), (1 - x, 1 - y)]


def _cast_bf16(a):
    R, C = a.shape
    T = 128

    def body(a_ref, o_ref):
        o_ref[...] = a_ref[...].astype(bf16)

    return pl.pallas_call(
        body, grid=(R // T,), name="cast_bf16", out_shape=jax.ShapeDtypeStruct((R, C), bf16),
        in_specs=[pl.BlockSpec((T, C), lambda i: (i, 0))], out_specs=pl.BlockSpec((T, C), lambda i: (i, 0)),
        compiler_params=_cp(("parallel",), 16),
    )(a)


def _chip_slot(ref, name, l, p):
    K, N, ax = BIG[name]
    if ax == 1:
        sz = N // N_CHIPS
        return ref.at[l, :, pl.ds(pl.multiple_of(p * sz, 128), sz)]
    sz = K // N_CHIPS
    return ref.at[l, pl.ds(pl.multiple_of(p * sz, 16), sz), :]


def _gather_weights(shards, conv_w, fconv_w):
    names = list(shards)
    n_big = len(names)
    n_t = n_big + 2
    n_chip = 3 * n_t
    n_pass = 3 * n_big

    def body(*refs):
        ins = refs[:n_t]
        outs = refs[n_t:2 * n_t]
        send_sems, recv_sems, pass_send, pass_recv, local_sems = refs[2 * n_t:]
        x, y, c = _position()
        me = 2 * x + y
        chips = _other_chips(x, y)

        def src(t):
            return ins[t].at[c] if t < n_big else ins[t]

        def slot(t, l, p):
            return _chip_slot(outs[t], names[t], l, p) if t < n_big else outs[t].at[p]

        locs, cps = [], []
        for t in range(n_t):
            for l in (range(DEPTH) if t < n_big else (0,)):
                loc = pltpu.make_async_copy(ins[t].at[l] if t < n_big else ins[t], slot(t, l, me),
                                            local_sems.at[DEPTH * t + l])
                loc.start()
                locs.append(loc)
            for k, (px, py) in enumerate(chips):
                cp = pltpu.make_async_remote_copy(
                    src_ref=src(t), dst_ref=slot(t, c, me), send_sem=send_sems.at[3 * t + k],
                    recv_sem=recv_sems.at[3 * t + k], device_id=(px, py, c), device_id_type=MESH_ID)
                cp.start()
                cps.append(cp)
        for t in range(n_t):
            for k, (px, py) in enumerate(chips):
                landed = slot(t, c, 2 * px + py)
                pltpu.make_async_remote_copy(
                    src_ref=src(t), dst_ref=landed, send_sem=send_sems.at[3 * t + k],
                    recv_sem=recv_sems.at[3 * t + k], device_id=(px, py, c), device_id_type=MESH_ID).wait_recv()
                if t < n_big:
                    cp = pltpu.make_async_remote_copy(
                        src_ref=landed, dst_ref=landed, send_sem=pass_send.at[3 * t + k],
                        recv_sem=pass_recv.at[3 * t + k], device_id=(x, y, 1 - c), device_id_type=MESH_ID)
                    cp.start()
                    cps.append(cp)
        for t in range(n_big):
            for k, (px, py) in enumerate(chips):
                from_sibling = slot(t, 1 - c, 2 * px + py)
                pltpu.make_async_remote_copy(
                    src_ref=from_sibling, dst_ref=from_sibling, send_sem=pass_send.at[3 * t + k],
                    recv_sem=pass_recv.at[3 * t + k], device_id=(x, y, 1 - c), device_id_type=MESH_ID).wait_recv()
        for cp in cps:
            cp.wait_send()
        for loc in locs:
            loc.wait()

    ins = [shards[n] for n in names] + [conv_w, fconv_w]
    out_shape = [jax.ShapeDtypeStruct((DEPTH, BIG[n][0], BIG[n][1]), bf16) for n in names]
    out_shape += [jax.ShapeDtypeStruct((N_CHIPS,) + conv_w.shape, f32), jax.ShapeDtypeStruct((N_CHIPS,) + fconv_w.shape, f32)]
    outs = pl.pallas_call(
        body, name="gather_weights", out_shape=tuple(out_shape), in_specs=[ANY] * n_t, out_specs=tuple([ANY] * n_t),
        scratch_shapes=[pltpu.SemaphoreType.DMA((n_chip,)), pltpu.SemaphoreType.DMA((n_chip,)),
                        pltpu.SemaphoreType.DMA((n_pass,)), pltpu.SemaphoreType.DMA((n_pass,)),
                        pltpu.SemaphoreType.DMA((DEPTH * n_t,))],
    )(*ins)
    return dict(zip(names, outs[:n_big])), outs[-2], outs[-1]


LATE_WEIGHTS = ("w_out", "ffn_w_up", "ffn_w_down")
HBM = pl.BlockSpec(memory_space=pltpu.HBM)
SEM = pl.BlockSpec(memory_space=pltpu.SEMAPHORE)


def _cast_into_full(shard, name, chip_arr):
    K, N, ax = BIG[name]
    k, n = _shard_shape(name)
    T = 64
    nrt = k // T

    def body(p_ref, a_ref, o_ref):
        o_ref[...] = a_ref[...].astype(bf16)

    if ax == 1:
        out_spec = pl.BlockSpec((None, T, n), lambda l, i, p: (l, i, p[0]))
    else:
        out_spec = pl.BlockSpec((None, T, n), lambda l, i, p: (l, p[0] * nrt + i, 0))
    return pl.pallas_call(
        body, name="cast_into_full", out_shape=jax.ShapeDtypeStruct((DEPTH, K, N), bf16),
        grid_spec=pltpu.PrefetchScalarGridSpec(
            num_scalar_prefetch=1, grid=(DEPTH, nrt),
            in_specs=[pl.BlockSpec((None, T, n), lambda l, i, p: (l, i, 0))], out_specs=out_spec),
        compiler_params=_cp(("parallel", "parallel"), 16),
    )(chip_arr, shard)


def _late_copies(refs, send_sems, recv_sems):
    x, y, c = _position()
    me = 2 * x + y
    idx = 0
    for ref, name in zip(refs, LATE_WEIGHTS):
        for l in range(DEPTH):
            for px, py in _other_chips(x, y):
                def copy(p, ref=ref, name=name, l=l, px=px, py=py, idx=idx):
                    part = _chip_slot(ref, name, l, p)
                    return pltpu.make_async_remote_copy(
                        src_ref=part, dst_ref=part, send_sem=send_sems.at[idx], recv_sem=recv_sems.at[idx],
                        device_id=(px, py, c), device_id_type=MESH_ID)
                yield copy(me), copy(2 * px + py)
                idx += 1


N_LATE_COPIES = 3 * DEPTH * len(LATE_WEIGHTS)


def _gather_start(fulls, after):
    n = len(fulls)

    def body(*refs):
        ins = refs[:n]
        send_sems, recv_sems = refs[n + 1:n + 3]
        token = refs[-1]
        for sent, _ in _late_copies(ins, send_sems, recv_sems):
            sent.start()
        token[...] = jnp.zeros_like(token)

    outs = pl.pallas_call(
        body, name="gather_start",
        out_shape=(pltpu.SemaphoreType.DMA((N_LATE_COPIES,)), pltpu.SemaphoreType.DMA((N_LATE_COPIES,)))
        + tuple(pltpu.HBM(f.shape, f.dtype) for f in fulls) + (jax.ShapeDtypeStruct((SUBLANES, LANES), f32),),
        in_specs=(HBM,) * n + (ANY,),
        out_specs=(SEM, SEM) + (HBM,) * n + (pl.BlockSpec(memory_space=pltpu.VMEM),),
        input_output_aliases={t: 2 + t for t in range(n)},
        compiler_params=pltpu.CompilerParams(has_side_effects=pltpu.SideEffectType.DATAFLOW_SIDE_EFFECTING),
    )(*[pltpu.with_memory_space_constraint(f, pltpu.HBM) for f in fulls], after)
    return outs[0], outs[1], outs[2:2 + n], outs[-1]


def _gather_wait(send_sems, recv_sems, fulls, after):
    n = len(fulls)

    def body(*refs):
        ins = refs[:n]
        send_ref, recv_ref = refs[n:n + 2]
        for sent, landed in _late_copies(ins, send_ref, recv_ref):
            sent.wait_send()
            landed.wait_recv()

    return pl.pallas_call(
        body, name="gather_wait", out_shape=tuple(pltpu.HBM(f.shape, f.dtype) for f in fulls),
        in_specs=(HBM,) * n + (SEM, SEM, ANY), out_specs=(HBM,) * n,
        input_output_aliases={t: t for t in range(n)},
        compiler_params=pltpu.CompilerParams(has_side_effects=pltpu.SideEffectType.DATAFLOW_SIDE_EFFECTING),
    )(*fulls, send_sems, recv_sems, after)


def _half(ref, name, c):
    K, N, ax = BIG[name]
    if ax == 1:
        return ref.at[pl.ds(pl.multiple_of(c * (K // 2), 8), K // 2), :]
    return ref.at[:, pl.ds(pl.multiple_of(c * (N // 2), 128), N // 2)]


def _half_shape(name):
    K, N, ax = BIG[name]
    return (K // 2, N) if ax == 1 else (K, N // 2)


def _shard_of_half(ref, name, q):
    K, N, ax = BIG[name]
    if ax == 1:
        sz = N // N_CHIPS
        return ref.at[:, pl.ds(pl.multiple_of(q * sz, 128), sz)]
    sz = K // N_CHIPS
    return ref.at[pl.ds(pl.multiple_of(q * sz, 16), sz), :]


def _shard_half_shape(name):
    K, N, ax = BIG[name]
    return (K // 2, N // N_CHIPS) if ax == 1 else (K // N_CHIPS, N // 2)


def _shard_shape(name):
    K, N, ax = BIG[name]
    return (K, N // N_CHIPS) if ax == 1 else (K // N_CHIPS, N)


def _pair_copies(names, srcs, lands, send_sems, recv_sems):
    x, y, c = _position()
    for idx, (name, src, land) in enumerate(zip(names, srcs, lands)):
        yield pltpu.make_async_remote_copy(
            src_ref=_half(src, name, 1 - c), dst_ref=land, send_sem=send_sems.at[idx], recv_sem=recv_sems.at[idx],
            device_id=(x, y, 1 - c), device_id_type=MESH_ID)


def _pair_exchange_start(tag, tensors):
    names = [n for n, _ in tensors]
    n = len(tensors)
    lands = [lax.empty(_half_shape(nm), f32) for nm in names]

    def body(*refs):
        for cp in _pair_copies(names, refs[:n], refs[n:2 * n], refs[2 * n], refs[2 * n + 1]):
            cp.start()
        refs[-1][...] = jnp.zeros_like(refs[-1])

    args = [g for _, g in tensors] + lands
    outs = pl.pallas_call(
        body, name="grad_pair_start_" + tag,
        out_shape=(pltpu.SemaphoreType.DMA((n,)), pltpu.SemaphoreType.DMA((n,)))
        + tuple(pltpu.HBM(a.shape, a.dtype) for a in args) + (jax.ShapeDtypeStruct((SUBLANES, LANES), f32),),
        in_specs=(HBM,) * (2 * n), out_specs=(SEM, SEM) + (HBM,) * (2 * n) + (pl.BlockSpec(memory_space=pltpu.VMEM),),
        input_output_aliases={t: 2 + t for t in range(2 * n)},
        compiler_params=pltpu.CompilerParams(has_side_effects=pltpu.SideEffectType.DATAFLOW_SIDE_EFFECTING),
    )(*[pltpu.with_memory_space_constraint(a, pltpu.HBM) for a in args])
    return (tag, names, outs[0], outs[1], outs[2:2 + 2 * n]), outs[-1]


def _pair_exchange_wait(state, after):
    tag, names, send_sems, recv_sems, bufs = state
    n = len(names)

    def body(*refs):
        for cp in _pair_copies(names, refs[:n], refs[n:2 * n], refs[2 * n], refs[2 * n + 1]):
            cp.wait_send()
            cp.wait_recv()

    outs = pl.pallas_call(
        body, name="grad_pair_wait_" + tag, out_shape=tuple(pltpu.HBM(a.shape, a.dtype) for a in bufs),
        in_specs=(HBM,) * (2 * n) + (SEM, SEM, ANY), out_specs=(HBM,) * (2 * n),
        input_output_aliases={t: t for t in range(2 * n)},
        compiler_params=pltpu.CompilerParams(has_side_effects=pltpu.SideEffectType.DATAFLOW_SIDE_EFFECTING),
    )(*bufs, send_sems, recv_sems, after)
    return list(zip(names, outs[:n], outs[n:]))


def _pair_add(g, rcv, name, c_arr):
    K, N, ax = BIG[name]
    hr, hc = _half_shape(name)
    T = 128
    nrt = hr // T

    def body(c_ref, g_ref, r_ref, o_ref):
        o_ref[...] = (g_ref[...] + r_ref[...]).astype(bf16)

    if ax == 1:
        g_spec = pl.BlockSpec((T, hc), lambda i, c: (c[0] * nrt + i, 0))
    else:
        g_spec = pl.BlockSpec((T, hc), lambda i, c: (i, c[0]))
    plain = pl.BlockSpec((T, hc), lambda i, c: (i, 0))
    return pl.pallas_call(
        body, name="grad_pair_add", out_shape=jax.ShapeDtypeStruct((hr, hc), bf16),
        grid_spec=pltpu.PrefetchScalarGridSpec(num_scalar_prefetch=1, grid=(nrt,), in_specs=[g_spec, plain],
                                               out_specs=plain),
        compiler_params=_cp(("parallel",), 32),
    )(c_arr, g, rcv)


def _chip_copies(names, srcs, lands, send_sems, recv_sems):
    x, y, c = _position()
    me = 2 * x + y
    idx = 0
    for name, src, land in zip(names, srcs, lands):
        for px, py in _other_chips(x, y):
            def copy(q, row, name=name, src=src, land=land, px=px, py=py, idx=idx):
                return pltpu.make_async_remote_copy(
                    src_ref=_shard_of_half(src, name, q), dst_ref=land.at[row], send_sem=send_sems.at[idx],
                    recv_sem=recv_sems.at[idx], device_id=(px, py, c), device_id_type=MESH_ID)
            yield copy(2 * px + py, me), copy(me, 2 * px + py)
            idx += 1


def _chip_exchange_start(tag, tensors):
    names = [n for n, _ in tensors]
    n = len(tensors)
    lands = [lax.empty((N_CHIPS,) + _shard_half_shape(nm), g.dtype) for nm, g in tensors]

    def body(*refs):
        send_sems, recv_sems = refs[2 * n:2 * n + 2]
        for sent, _ in _chip_copies(names, refs[:n], refs[n:2 * n], send_sems, recv_sems):
            sent.start()
        refs[-1][...] = jnp.zeros_like(refs[-1])

    args = [g for _, g in tensors] + lands
    outs = pl.pallas_call(
        body, name="grad_chip_start_" + tag,
        out_shape=(pltpu.SemaphoreType.DMA((3 * n,)), pltpu.SemaphoreType.DMA((3 * n,)))
        + tuple(pltpu.HBM(a.shape, a.dtype) for a in args) + (jax.ShapeDtypeStruct((SUBLANES, LANES), f32),),
        in_specs=(HBM,) * (2 * n), out_specs=(SEM, SEM) + (HBM,) * (2 * n) + (pl.BlockSpec(memory_space=pltpu.VMEM),),
        input_output_aliases={t: 2 + t for t in range(2 * n)},
        compiler_params=pltpu.CompilerParams(has_side_effects=pltpu.SideEffectType.DATAFLOW_SIDE_EFFECTING),
    )(*[pltpu.with_memory_space_constraint(a, pltpu.HBM) for a in args])
    return (tag, names, outs[0], outs[1], outs[2:2 + 2 * n]), outs[-1]


def _chip_exchange_wait(state, after):
    tag, names, send_sems, recv_sems, bufs = state
    n = len(names)

    def body(*refs):
        for sent, landed in _chip_copies(names, refs[:n], refs[n:2 * n], refs[2 * n], refs[2 * n + 1]):
            sent.wait_send()
            landed.wait_recv()

    outs = pl.pallas_call(
        body, name="grad_chip_wait_" + tag, out_shape=tuple(pltpu.HBM(a.shape, a.dtype) for a in bufs),
        in_specs=(HBM,) * (2 * n) + (SEM, SEM, ANY), out_specs=(HBM,) * (2 * n),
        input_output_aliases={t: t for t in range(2 * n)},
        compiler_params=pltpu.CompilerParams(has_side_effects=pltpu.SideEffectType.DATAFLOW_SIDE_EFFECTING),
    )(*bufs, send_sems, recv_sems, after)
    return list(zip(names, outs[:n], outs[n:]))


def _sum_chips(name, half, land, chip_arr):
    K, N, ax = BIG[name]
    R, C = _shard_half_shape(name)
    T = 64
    nrt = R // T

    def body(p_ref, own_ref, land_ref, o_ref):
        parts = [jnp.where(p_ref[0] == q, own_ref[...], land_ref[q]).astype(f32) for q in range(N_CHIPS)]
        o_ref[...] = ((parts[0] + parts[1]) + parts[2]) + parts[3]

    if ax == 1:
        own_spec = pl.BlockSpec((T, C), lambda i, p: (i, p[0]))
    else:
        own_spec = pl.BlockSpec((T, C), lambda i, p: (p[0] * nrt + i, 0))
    return pl.pallas_call(
        body, name="grad_sum_chips", out_shape=jax.ShapeDtypeStruct((R, C), f32),
        grid_spec=pltpu.PrefetchScalarGridSpec(
            num_scalar_prefetch=1, grid=(nrt,),
            in_specs=[own_spec, pl.BlockSpec((N_CHIPS, T, C), lambda i, p: (0, i, 0))],
            out_specs=pl.BlockSpec((T, C), lambda i, p: (i, 0))),
        compiler_params=_cp(("parallel",), 32),
    )(chip_arr, half, land)


def _pair_swap(halves):
    n_t = len(halves)

    def body(*refs):
        ins = refs[:n_t]
        outs = refs[n_t:2 * n_t]
        send_sems, recv_sems = refs[2 * n_t:]
        x, y, c = _position()
        cps = []
        for t in range(n_t):
            cp = pltpu.make_async_remote_copy(
                src_ref=ins[t], dst_ref=outs[t], send_sem=send_sems.at[t], recv_sem=recv_sems.at[t],
                device_id=(x, y, 1 - c), device_id_type=MESH_ID)
            cp.start()
            cps.append(cp)
        for cp in cps:
            cp.wait()

    return pl.pallas_call(
        body, name="grad_pair_swap", out_shape=tuple(jax.ShapeDtypeStruct(h.shape, h.dtype) for h in halves),
        in_specs=[ANY] * n_t, out_specs=tuple([ANY] * n_t),
        scratch_shapes=[pltpu.SemaphoreType.DMA((n_t,)), pltpu.SemaphoreType.DMA((n_t,))],
    )(*halves)


def _adamw_halves(own, other, w, m, v, name, l, c_arr, prev):
    K, N, ax = BIG[name]
    R, C = _shard_shape(name)
    hr, hc = _shard_half_shape(name)
    T = 64
    nrt = hr // T
    c1 = 1.0 / (1.0 - ADAM_B1 ** ADAM_STEP)
    c2 = 1.0 / (1.0 - ADAM_B2 ** ADAM_STEP)

    def body(c_ref, own_ref, oth_ref, w_ref, m_ref, v_ref, *rest):
        g_ref, d_ref, nm_ref, nv_ref = rest[-4:]
        gg = jnp.where(pl.program_id(0) == c_ref[0], own_ref[...], oth_ref[...])
        nm = ADAM_B1 * m_ref[...] + (1.0 - ADAM_B1) * gg
        nv = ADAM_B2 * v_ref[...] + (1.0 - ADAM_B2) * (gg * gg)
        g_ref[...] = gg
        nm_ref[...] = nm
        nv_ref[...] = nv
        d_ref[...] = -ADAM_LR * ((nm * c1) / (jnp.sqrt(nv * c2) + ADAM_EPS) + ADAM_WD * w_ref[...])

    half = pl.BlockSpec((T, hc), lambda h, i, c: (i, 0))
    if ax == 1:
        full = pl.BlockSpec((None, T, hc), lambda h, i, c: (l, h * nrt + i, 0))
    else:
        full = pl.BlockSpec((None, T, hc), lambda h, i, c: (l, i, h))
    sd = jax.ShapeDtypeStruct((DEPTH, R, C), f32)
    args = [c_arr, own, other, w, m, v]
    in_specs = [half, half, full, full, full]
    aliases = {}
    if prev is not None:
        args += list(prev)
        in_specs += [ANY] * 4
        aliases = {6 + k: k for k in range(4)}
    return pl.pallas_call(
        body, name="adamw_" + name, out_shape=(sd, sd, sd, sd),
        grid_spec=pltpu.PrefetchScalarGridSpec(num_scalar_prefetch=1, grid=(2, nrt), in_specs=in_specs,
                                               out_specs=(full, full, full, full)),
        input_output_aliases=aliases,
        compiler_params=_cp(("arbitrary", "arbitrary"), 32),
    )(*args)


class _GradExchange:
    GROUPS = (("l1", tuple((n, DEPTH - 1) for n in BIG)),
              ("l0_ffn", (("ffn_w_down", 0), ("ffn_w_up", 0))),
              ("l0_mix", (("w_out", 0), ("w_in", 0))))

    def __init__(self):
        self.c_arr = jnp.reshape(lax.axis_index("c"), (1,)).astype(jnp.int32)
        self.chip_arr = jnp.reshape(2 * lax.axis_index("x") + lax.axis_index("y"), (1,)).astype(jnp.int32)
        self.grads = {}
        self.pair_started = {}
        self.chip_started = {}

    def _advance(self, after, tok):
        for tag, _ in self.GROUPS:
            if tag not in self.pair_started or tag in self.chip_started:
                continue
            arrived = _pair_exchange_wait(self.pair_started[tag], after)
            pair = [(n, _pair_add(g, r, n, self.c_arr)) for n, g, r in arrived]
            self.chip_started[tag], token = _chip_exchange_start(tag, pair)
            tok = tok + token[0, 0]
        return tok

    def put(self, name, layer, g, tok):
        self.grads[(name, layer)] = g
        tok = self._advance(g, tok)
        for tag, keys in self.GROUPS:
            if tag in self.pair_started or not all(k in self.grads for k in keys):
                continue
            self.pair_started[tag], token = _pair_exchange_start(tag, [(n, self.grads[(n, l)]) for n, l in keys])
            tok = tok + token[0, 0]
        return tok

    def finish(self, after):
        self._advance(after, jnp.zeros((), f32))
        keys, own = [], []
        for tag, group in self.GROUPS:
            landed = _chip_exchange_wait(self.chip_started[tag], after)
            own += [_sum_chips(n, half, land, self.chip_arr) for n, half, land in landed]
            keys += list(group)
        other = _pair_swap(own)
        return dict(zip(keys, zip(own, other)))


def _small_allreduce(buf):
    R = buf.shape[0]

    def body(in_ref, out_ref, sibling, slots, send_sems, recv_sems):
        x, y, c = _position()
        me = 2 * x + y
        swap = pltpu.make_async_remote_copy(
            src_ref=in_ref, dst_ref=sibling, send_sem=send_sems.at[0], recv_sem=recv_sems.at[0],
            device_id=(x, y, 1 - c), device_id_type=MESH_ID)
        swap.start()
        swap.wait()
        slots[me] = in_ref[...] + sibling[...]
        cps = []
        for k, (px, py) in enumerate(_other_chips(x, y)):
            cp = pltpu.make_async_remote_copy(
                src_ref=slots.at[me], dst_ref=slots.at[me], send_sem=send_sems.at[1 + k], recv_sem=recv_sems.at[1 + k],
                device_id=(px, py, c), device_id_type=MESH_ID)
            cp.start()
            cps.append(cp)
        for k, (px, py) in enumerate(_other_chips(x, y)):
            pltpu.make_async_remote_copy(
                src_ref=slots.at[me], dst_ref=slots.at[2 * px + py], send_sem=send_sems.at[1 + k],
                recv_sem=recv_sems.at[1 + k], device_id=(px, py, c), device_id_type=MESH_ID).wait_recv()
        for cp in cps:
            cp.wait_send()
        out_ref[...] = ((slots[0] + slots[1]) + slots[2]) + slots[3]

    vm = pl.BlockSpec(memory_space=pltpu.VMEM)
    return pl.pallas_call(
        body, name="small_allreduce", out_shape=jax.ShapeDtypeStruct((R, 128), f32), in_specs=[vm], out_specs=vm,
        scratch_shapes=[pltpu.VMEM((R, 128), f32), pltpu.VMEM((N_CHIPS, R, 128), f32),
                        pltpu.SemaphoreType.DMA((N_CHIPS,)), pltpu.SemaphoreType.DMA((N_CHIPS,))],
        compiler_params=pltpu.CompilerParams(vmem_limit_bytes=40 * MIB),
    )(buf)


PACK_UNIT = 1024


def _pack(arrs):
    parts = []
    for a in arrs:
        flat = a.reshape(-1)
        n = -(-flat.shape[0] // PACK_UNIT) * PACK_UNIT
        parts.append(jnp.pad(flat, (0, n - flat.shape[0])))
    return jnp.concatenate(parts).reshape(-1, 128)


def _unpack(buf, shapes):
    flat = buf.reshape(-1)
    out, off = [], 0
    for shp in shapes:
        n = int(np.prod(shp))
        out.append(flat[off:off + n].reshape(shp))
        off += -(-n // PACK_UNIT) * PACK_UNIT
    return out


def kernel(x, w_in, b_in, conv_dw_w, conv_dw_b, conv_ln_g, conv_ln_b, rel_bias_table, gmlp_ln_g, gmlp_ln_b, gmlp_w_s, gmlp_b_s, w_out, b_out, ln1_g, ln1_b, ffn_w_up, ffn_b_up, ffn_conv_w, ffn_conv_b, ffn_w_down, ffn_b_down, ln2_g, ln2_b, loss_target, m_w_in, m_b_in, m_conv_dw_w, m_conv_dw_b, m_conv_ln_g, m_conv_ln_b, m_rel_bias_table, m_gmlp_ln_g, m_gmlp_ln_b, m_gmlp_w_s, m_gmlp_b_s, m_w_out, m_b_out, m_ln1_g, m_ln1_b, m_ffn_w_up, m_ffn_b_up, m_ffn_conv_w, m_ffn_conv_b, m_ffn_w_down, m_ffn_b_down, m_ln2_g, m_ln2_b, v_w_in, v_b_in, v_conv_dw_w, v_conv_dw_b, v_conv_ln_g, v_conv_ln_b, v_rel_bias_table, v_gmlp_ln_g, v_gmlp_ln_b, v_gmlp_w_s, v_gmlp_b_s, v_w_out, v_b_out, v_ln1_g, v_ln1_b, v_ffn_w_up, v_ffn_b_up, v_ffn_conv_w, v_ffn_conv_b, v_ffn_w_down, v_ffn_b_down, v_ln2_g, v_ln2_b):
    w = dict(w_in=w_in, b_in=b_in, conv_dw_w=conv_dw_w, conv_dw_b=conv_dw_b, conv_ln_g=conv_ln_g, conv_ln_b=conv_ln_b,
             rel_bias_table=rel_bias_table, gmlp_ln_g=gmlp_ln_g, gmlp_ln_b=gmlp_ln_b, gmlp_w_s=gmlp_w_s,
             gmlp_b_s=gmlp_b_s, w_out=w_out, b_out=b_out, ln1_g=ln1_g, ln1_b=ln1_b, ffn_w_up=ffn_w_up,
             ffn_b_up=ffn_b_up, ffn_conv_w=ffn_conv_w, ffn_conv_b=ffn_conv_b, ffn_w_down=ffn_w_down,
             ffn_b_down=ffn_b_down, ln2_g=ln2_g, ln2_b=ln2_b)
    m = dict(w_in=m_w_in, b_in=m_b_in, conv_dw_w=m_conv_dw_w, conv_dw_b=m_conv_dw_b, conv_ln_g=m_conv_ln_g,
             conv_ln_b=m_conv_ln_b, rel_bias_table=m_rel_bias_table, gmlp_ln_g=m_gmlp_ln_g, gmlp_ln_b=m_gmlp_ln_b,
             gmlp_w_s=m_gmlp_w_s, gmlp_b_s=m_gmlp_b_s, w_out=m_w_out, b_out=m_b_out, ln1_g=m_ln1_g, ln1_b=m_ln1_b,
             ffn_w_up=m_ffn_w_up, ffn_b_up=m_ffn_b_up, ffn_conv_w=m_ffn_conv_w, ffn_conv_b=m_ffn_conv_b,
             ffn_w_down=m_ffn_w_down, ffn_b_down=m_ffn_b_down, ln2_g=m_ln2_g, ln2_b=m_ln2_b)
    v = dict(w_in=v_w_in, b_in=v_b_in, conv_dw_w=v_conv_dw_w, conv_dw_b=v_conv_dw_b, conv_ln_g=v_conv_ln_g,
             conv_ln_b=v_conv_ln_b, rel_bias_table=v_rel_bias_table, gmlp_ln_g=v_gmlp_ln_g, gmlp_ln_b=v_gmlp_ln_b,
             gmlp_w_s=v_gmlp_w_s, gmlp_b_s=v_gmlp_b_s, w_out=v_w_out, b_out=v_b_out, ln1_g=v_ln1_g, ln1_b=v_ln1_b,
             ffn_w_up=v_ffn_w_up, ffn_b_up=v_ffn_b_up, ffn_conv_w=v_ffn_conv_w, ffn_conv_b=v_ffn_conv_b,
             ffn_w_down=v_ffn_w_down, ffn_b_down=v_ffn_b_down, ln2_g=v_ln2_g, ln2_b=v_ln2_b)

    chip_arr = jnp.reshape(2 * lax.axis_index("x") + lax.axis_index("y"), (1,)).astype(jnp.int32)
    shards = {"w_in": _cast_bf16(w_in.reshape(-1, w_in.shape[-1])).reshape(w_in.shape)}
    wb, conv_stack, fconv_stack = _gather_weights(shards, conv_dw_w, ffn_conv_w)
    send_sems, recv_sems, in_flight, token = _gather_start(
        [_cast_into_full(w[n], n, chip_arr) for n in LATE_WEIGHTS], conv_stack)
    sp = {n: w[n] for n in SMALL}
    sp["conv_dw_w"] = jnp.moveaxis(conv_stack, 0, 2).reshape(DEPTH, CONV_WIDTH, CONV_CH)
    sp["ffn_conv_w"] = jnp.moveaxis(fconv_stack, 0, 2).reshape(DEPTH, FFN_CONV_WIDTH, 2 * D_FF)
    sp["b_in"] = sp["b_in"] + token[0, 0]

    def late_weights(after):
        return dict(zip(LATE_WEIGHTS, _gather_wait(send_sems, recv_sems, in_flight, after)))

    sink = _GradExchange()
    loss_local, grad_x, grads, big = _local_step(x[0], loss_target[0], wb, late_weights, sp, sink)

    small_shapes = [(1,)] + [grads[n].shape for n in SMALL]
    summed = _unpack(_small_allreduce(_pack([loss_local.reshape(1)] + [grads[n] for n in SMALL])), small_shapes)
    loss = summed[0].reshape(())
    small = dict(zip(SMALL, summed[1:]))
    chip = 2 * lax.axis_index("x") + lax.axis_index("y")
    for n in SMALL_SHARDED:
        width = w[n].shape[-1]
        small[n] = lax.dynamic_slice_in_dim(small[n], chip * width, width, axis=2)

    g_out, d_out, m_out, v_out = {}, {}, {}, {}
    for n in BIG:
        outs = None
        for l in range(DEPTH):
            own, other = big[(n, l)]
            outs = _adamw_halves(own, other, w[n], m[n], v[n], n, l, sink.c_arr, outs)
        g_out[n], d_out[n], m_out[n], v_out[n] = outs
    shapes = [small[n].shape for n in SMALL]
    packed = [_pack([src[n] for n in SMALL]) for src in (small, w, m, v)]
    upd = _adamw(*packed, "adamw_small")
    for dst, buf in zip((d_out, m_out, v_out), upd):
        dst.update(zip(SMALL, _unpack(buf, shapes)))
    g_out.update(small)

    return (loss, grad_x[None], *[g_out[n] for n in WEIGHTS], *[d_out[n] for n in WEIGHTS],
            *[m_out[n] for n in WEIGHTS], *[v_out[n] for n in WEIGHTS])
```
